```python
import math
import jax, jax.numpy as jnp
from jax import lax
import numpy as np

D_MODEL = 1024
BATCH = 16
SEQ = 2048
DEPTH = 1

N_META = 16
D_MIX = D_MODEL
C_CONV = D_MIX // 2
CONV_GROUPS = 8
CONV_WIDTH = 31
GLA_V = D_MIX - C_CONV
GLA_HEADS = 4
GLA_DV = GLA_V // GLA_HEADS
GLA_DK = GLA_DV // 2
GLA_K = GLA_HEADS * GLA_DK
GATE_RANK = 16
GATE_TAU = 16.0
CHUNK = 64
META_PAD = CHUNK - N_META
D_FF = int(math.ceil(D_MODEL * 8 / 3 / 256) * 256)
RMS_EPS = 1e-6
LN_EPS = 1e-5

IN_SPLITS = [C_CONV, C_CONV, GLA_K, GLA_K, GLA_V, GLA_V, GATE_RANK]
D_IN = sum(IN_SPLITS)

kernel_name = "hymba_conformer_conv_gla_hybrid"


def rms_norm(x, g):
    xf = x.astype(jnp.float32)
    y = xf * lax.rsqrt(jnp.mean(xf * xf, axis=-1, keepdims=True) + RMS_EPS)
    return (y * g.astype(jnp.float32)).astype(x.dtype)


def conv_group(u_val, u_gate, conv_w, conv_b, ln_g, ln_b):
    v = u_val * jax.nn.sigmoid(u_gate)
    y = lax.conv_general_dilated(
        v, conv_w[:, None, :].astype(v.dtype),
        window_strides=(1,), padding=((CONV_WIDTH - 1, 0),),
        dimension_numbers=("NWC", "WIO", "NWC"),
        feature_group_count=C_CONV) + conv_b
    yf = y.astype(jnp.float32)
    mu = jnp.mean(yf, axis=-1, keepdims=True)
    var = jnp.mean(jnp.square(yf - mu), axis=-1, keepdims=True)
    yf = (yf - mu) * lax.rsqrt(var + LN_EPS) * ln_g.astype(jnp.float32) + ln_b.astype(jnp.float32)
    return jax.nn.silu(yf).astype(u_val.dtype)


def _to_chunks(t, d):
    b, lp, _ = t.shape
    return t.reshape(b, lp // CHUNK, CHUNK, GLA_HEADS, d).transpose(0, 1, 3, 2, 4)


def gla_group(q, k, v, g, gate_lr, w_gate2, gate_b, norm_g):
    dt = q.dtype
    bsz, seq_len, _ = q.shape
    f = lambda t: t.astype(jnp.float32)
    log_a = jax.nn.log_sigmoid(f(gate_lr) @ f(w_gate2) + f(gate_b)) / GATE_TAU
    pad = lambda t: jnp.pad(f(t), ((0, 0), (META_PAD, 0), (0, 0)))
    qc = _to_chunks(pad(q), GLA_DK) * (GLA_DK ** -0.5)
    kc = _to_chunks(pad(k), GLA_DK)
    vc = _to_chunks(pad(v), GLA_DV)
    bc = jnp.cumsum(_to_chunks(pad(log_a), GLA_DK), axis=3)
    b_last = bc[:, :, :, -1:, :]

    q_in = qc * jnp.exp(bc)
    k_in = kc * jnp.exp(-bc)
    causal = jnp.tril(jnp.ones((CHUNK, CHUNK), dtype=bool))
    scores = jnp.einsum("bnhid,bnhjd->bnhij", q_in, k_in)
    scores = jnp.where(causal, scores, 0.0)
    o_intra = jnp.einsum("bnhij,bnhje->bnhie", scores, vc)

    kv = jnp.einsum("bnhjd,bnhje->bnhde", kc * jnp.exp(b_last - bc), vc)
    decay = jnp.exp(b_last[:, :, :, 0, :])

    def step(state, inp):
        dec, kv_n = inp
        return dec[..., None] * state + kv_n, state

    s0 = jnp.zeros((bsz, GLA_HEADS, GLA_DK, GLA_DV), jnp.float32)
    _, s_prev = lax.scan(step, s0, (decay.swapaxes(0, 1), kv.swapaxes(0, 1)))
    s_prev = s_prev.swapaxes(0, 1)
    o_inter = jnp.einsum("bnhid,bnhde->bnhie", q_in, s_prev)

    o = (o_intra + o_inter).transpose(0, 1, 3, 2, 4).reshape(bsz, -1, GLA_HEADS, GLA_DV)[:, META_PAD:]
    o = o * lax.rsqrt(jnp.mean(o * o, axis=-1, keepdims=True) + RMS_EPS) * f(norm_g)
    o = o * jax.nn.silu(f(g)).reshape(bsz, seq_len, GLA_HEADS, GLA_DV)
    return o.reshape(bsz, seq_len, GLA_V).astype(dt)


def swiglu(x, w_gate, w_up, w_down):
    return (jax.nn.silu(x @ w_gate) * (x @ w_up)) @ w_down


def _fwd_setup_inputs(seed: int = 0) -> dict:
    key = jax.random.key(seed)
    ks = jax.random.split(key, 20)
    n = lambda k, shape, s: jax.random.normal(k, shape, jnp.float32) * s
    return {
        "x": n(ks[0], (BATCH, SEQ, D_MODEL), 1.0),
        "meta_tokens": n(ks[1], (N_META, D_MODEL), 1.0),
        "norm_mix_g": 1.0 + n(ks[2], (DEPTH, D_MODEL), 0.02),
        "w_in": n(ks[3], (DEPTH, D_MODEL, D_IN), D_MODEL ** -0.5),
        "conv_w": n(ks[4], (DEPTH, CONV_WIDTH, C_CONV), CONV_WIDTH ** -0.5),
        "conv_b": n(ks[5], (DEPTH, C_CONV), 0.02),
        "conv_ln_g": 1.0 + n(ks[6], (DEPTH, C_CONV), 0.02),
        "conv_ln_b": n(ks[7], (DEPTH, C_CONV), 0.02),
        "gla_w_gate2": n(ks[8], (DEPTH, GATE_RANK, GLA_K), GATE_RANK ** -0.5),
        "gla_gate_b": n(ks[9], (DEPTH, GLA_K), 0.1),
        "gla_norm_g": 1.0 + n(ks[10], (DEPTH, GLA_DV), 0.02),
        "w_out": n(ks[11], (DEPTH, D_MIX, D_MODEL), D_MIX ** -0.5),
        "norm_ffn_g": 1.0 + n(ks[12], (DEPTH, D_MODEL), 0.02),
        "w_ffn_gate": n(ks[13], (DEPTH, D_MODEL, D_FF), D_MODEL ** -0.5),
        "w_ffn_up": n(ks[14], (DEPTH, D_MODEL, D_FF), D_MODEL ** -0.5),
        "w_ffn_down": n(ks[15], (DEPTH, D_FF, D_MODEL), D_FF ** -0.5),
        "norm_final_g": 1.0 + n(ks[16], (D_MODEL,), 0.02),
    }


def _fwd_reference(x, meta_tokens, norm_mix_g, w_in, conv_w, conv_b, conv_ln_g, conv_ln_b,
              gla_w_gate2, gla_gate_b, gla_norm_g, w_out, norm_ffn_g, w_ffn_gate,
              w_ffn_up, w_ffn_down, norm_final_g):
    bsz = x.shape[0]
    meta = jnp.broadcast_to(meta_tokens[None].astype(x.dtype), (bsz, N_META, D_MODEL))
    h = jnp.concatenate([meta, x], axis=1)
    split_idx = list(np.cumsum(IN_SPLITS)[:-1])
    for l in range(DEPTH):
        u = rms_norm(h, norm_mix_g[l]) @ w_in[l]
        c_val, c_gate, q, k, v, g, gate_lr = jnp.split(u, split_idx, axis=-1)
        y_conv = conv_group(c_val, c_gate, conv_w[l], conv_b[l], conv_ln_g[l], conv_ln_b[l])
        y_gla = gla_group(q, k, v, g, gate_lr, gla_w_gate2[l], gla_gate_b[l], gla_norm_g[l])
        h = h + jnp.concatenate([y_conv, y_gla], axis=-1) @ w_out[l]
        h = h + swiglu(rms_norm(h, norm_ffn_g[l]), w_ffn_gate[l], w_ffn_up[l], w_ffn_down[l])
    y = rms_norm(h, norm_final_g)
    return y[:, N_META:]


import jax as _jax
import jax.numpy as _jnp

TWIN_FORMAT = 'train_step'
FWD_PARAMS = ['x', 'meta_tokens', 'norm_mix_g', 'w_in', 'conv_w', 'conv_b', 'conv_ln_g', 'conv_ln_b', 'gla_w_gate2', 'gla_gate_b', 'gla_norm_g', 'w_out', 'norm_ffn_g', 'w_ffn_gate', 'w_ffn_up', 'w_ffn_down', 'norm_final_g']
TWIN_WEIGHTS = ['meta_tokens', 'norm_mix_g', 'w_in', 'conv_w', 'conv_b', 'conv_ln_g', 'conv_ln_b', 'gla_w_gate2', 'gla_gate_b', 'gla_norm_g', 'w_out', 'norm_ffn_g', 'w_ffn_gate', 'w_ffn_up', 'w_ffn_down', 'norm_final_g']
TWIN_DIFF_INPUT = 'x'
TWIN_INPUTS = ['x', 'meta_tokens', 'norm_mix_g', 'w_in', 'conv_w', 'conv_b', 'conv_ln_g', 'conv_ln_b', 'gla_w_gate2', 'gla_gate_b', 'gla_norm_g', 'w_out', 'norm_ffn_g', 'w_ffn_gate', 'w_ffn_up', 'w_ffn_down', 'norm_final_g', 'loss_target', 'm_meta_tokens', 'm_norm_mix_g', 'm_w_in', 'm_conv_w', 'm_conv_b', 'm_conv_ln_g', 'm_conv_ln_b', 'm_gla_w_gate2', 'm_gla_gate_b', 'm_gla_norm_g', 'm_w_out', 'm_norm_ffn_g', 'm_w_ffn_gate', 'm_w_ffn_up', 'm_w_ffn_down', 'm_norm_final_g', 'v_meta_tokens', 'v_norm_mix_g', 'v_w_in', 'v_conv_w', 'v_conv_b', 'v_conv_ln_g', 'v_conv_ln_b', 'v_gla_w_gate2', 'v_gla_gate_b', 'v_gla_norm_g', 'v_w_out', 'v_norm_ffn_g', 'v_w_ffn_gate', 'v_w_ffn_up', 'v_w_ffn_down', 'v_norm_final_g']
TWIN_OUTPUTS = ['loss', 'grad_x', 'grad_meta_tokens', 'grad_norm_mix_g', 'grad_w_in', 'grad_conv_w', 'grad_conv_b', 'grad_conv_ln_g', 'grad_conv_ln_b', 'grad_gla_w_gate2', 'grad_gla_gate_b', 'grad_gla_norm_g', 'grad_w_out', 'grad_norm_ffn_g', 'grad_w_ffn_gate', 'grad_w_ffn_up', 'grad_w_ffn_down', 'grad_norm_final_g', 'delta_meta_tokens', 'delta_norm_mix_g', 'delta_w_in', 'delta_conv_w', 'delta_conv_b', 'delta_conv_ln_g', 'delta_conv_ln_b', 'delta_gla_w_gate2', 'delta_gla_gate_b', 'delta_gla_norm_g', 'delta_w_out', 'delta_norm_ffn_g', 'delta_w_ffn_gate', 'delta_w_ffn_up', 'delta_w_ffn_down', 'delta_norm_final_g', 'new_m_meta_tokens', 'new_m_norm_mix_g', 'new_m_w_in', 'new_m_conv_w', 'new_m_conv_b', 'new_m_conv_ln_g', 'new_m_conv_ln_b', 'new_m_gla_w_gate2', 'new_m_gla_gate_b', 'new_m_gla_norm_g', 'new_m_w_out', 'new_m_norm_ffn_g', 'new_m_w_ffn_gate', 'new_m_w_ffn_up', 'new_m_w_ffn_down', 'new_m_norm_final_g', 'new_v_meta_tokens', 'new_v_norm_mix_g', 'new_v_w_in', 'new_v_conv_w', 'new_v_conv_b', 'new_v_conv_ln_g', 'new_v_conv_ln_b', 'new_v_gla_w_gate2', 'new_v_gla_gate_b', 'new_v_gla_norm_g', 'new_v_w_out', 'new_v_norm_ffn_g', 'new_v_w_ffn_gate', 'new_v_w_ffn_up', 'new_v_w_ffn_down', 'new_v_norm_final_g']
TWIN_LEAF_KINDS = {'loss': 'loss', 'grad_x': 'grad_x', 'grad_meta_tokens': 'grad_w', 'grad_norm_mix_g': 'grad_w', 'grad_w_in': 'grad_w', 'grad_conv_w': 'grad_w', 'grad_conv_b': 'grad_w', 'grad_conv_ln_g': 'grad_w', 'grad_conv_ln_b': 'grad_w', 'grad_gla_w_gate2': 'grad_w', 'grad_gla_gate_b': 'grad_w', 'grad_gla_norm_g': 'grad_w', 'grad_w_out': 'grad_w', 'grad_norm_ffn_g': 'grad_w', 'grad_w_ffn_gate': 'grad_w', 'grad_w_ffn_up': 'grad_w', 'grad_w_ffn_down': 'grad_w', 'grad_norm_final_g': 'grad_w', 'delta_meta_tokens': 'delta_w', 'delta_norm_mix_g': 'delta_w', 'delta_w_in': 'delta_w', 'delta_conv_w': 'delta_w', 'delta_conv_b': 'delta_w', 'delta_conv_ln_g': 'delta_w', 'delta_conv_ln_b': 'delta_w', 'delta_gla_w_gate2': 'delta_w', 'delta_gla_gate_b': 'delta_w', 'delta_gla_norm_g': 'delta_w', 'delta_w_out': 'delta_w', 'delta_norm_ffn_g': 'delta_w', 'delta_w_ffn_gate': 'delta_w', 'delta_w_ffn_up': 'delta_w', 'delta_w_ffn_down': 'delta_w', 'delta_norm_final_g': 'delta_w', 'new_m_meta_tokens': 'new_m', 'new_m_norm_mix_g': 'new_m', 'new_m_w_in': 'new_m', 'new_m_conv_w': 'new_m', 'new_m_conv_b': 'new_m', 'new_m_conv_ln_g': 'new_m', 'new_m_conv_ln_b': 'new_m', 'new_m_gla_w_gate2': 'new_m', 'new_m_gla_gate_b': 'new_m', 'new_m_gla_norm_g': 'new_m', 'new_m_w_out': 'new_m', 'new_m_norm_ffn_g': 'new_m', 'new_m_w_ffn_gate': 'new_m', 'new_m_w_ffn_up': 'new_m', 'new_m_w_ffn_down': 'new_m', 'new_m_norm_final_g': 'new_m', 'new_v_meta_tokens': 'new_v', 'new_v_norm_mix_g': 'new_v', 'new_v_w_in': 'new_v', 'new_v_conv_w': 'new_v', 'new_v_conv_b': 'new_v', 'new_v_conv_ln_g': 'new_v', 'new_v_conv_ln_b': 'new_v', 'new_v_gla_w_gate2': 'new_v', 'new_v_gla_gate_b': 'new_v', 'new_v_gla_norm_g': 'new_v', 'new_v_w_out': 'new_v', 'new_v_norm_ffn_g': 'new_v', 'new_v_w_ffn_gate': 'new_v', 'new_v_w_ffn_up': 'new_v', 'new_v_w_ffn_down': 'new_v', 'new_v_norm_final_g': 'new_v'}


def _forward(args):
    return _fwd_reference(*[args[k] for k in FWD_PARAMS])


def _output_shape():
    out = _jax.eval_shape(lambda: _forward(_fwd_setup_inputs(0)))
    return out.shape, out.dtype

N_MICROBATCH = 1
ADAM_LR = 0.001
ADAM_B1 = 0.9
ADAM_B2 = 0.999
ADAM_EPS = 1e-08
ADAM_WD = 0.01
ADAM_STEP = 10
PER_EXAMPLE_BATCH_AXIS = {'x': 0, 'loss_target': 0}
SHARED_INPUTS = []
_WEIGHT_DTYPES = {'meta_tokens': _jnp.float32, 'norm_mix_g': _jnp.float32, 'w_in': _jnp.float32, 'conv_w': _jnp.float32, 'conv_b': _jnp.float32, 'conv_ln_g': _jnp.float32, 'conv_ln_b': _jnp.float32, 'gla_w_gate2': _jnp.float32, 'gla_gate_b': _jnp.float32, 'gla_norm_g': _jnp.float32, 'w_out': _jnp.float32, 'norm_ffn_g': _jnp.float32, 'w_ffn_gate': _jnp.float32, 'w_ffn_up': _jnp.float32, 'w_ffn_down': _jnp.float32, 'norm_final_g': _jnp.float32}
MOMENT_SCALE = {'meta_tokens': 6.553450e-03, 'norm_mix_g': 1.764959e-01, 'w_in': 1.047846e-01, 'conv_w': 1.038458e-01, 'conv_b': 2.072593e-01, 'conv_ln_g': 1.343122e-01, 'conv_ln_b': 1.133050e-01, 'gla_w_gate2': 1.580142e-02, 'gla_gate_b': 6.048570e-02, 'gla_norm_g': 1.951791e-01, 'w_out': 1.008458e-01, 'norm_ffn_g': 1.184377e-01, 'w_ffn_gate': 5.048233e-02, 'w_ffn_up': 4.879075e-02, 'w_ffn_down': 8.077080e-02, 'norm_final_g': 3.203533e+01}


def _to_microbatches(a, axis):
    t = _jnp.moveaxis(a, axis, 0)
    t = t.reshape((N_MICROBATCH, t.shape[0] // N_MICROBATCH) + t.shape[1:])
    return _jnp.moveaxis(t, 1, axis + 1)


def setup_inputs(seed: int = 0) -> dict:
    inp = _fwd_setup_inputs(seed)
    key = _jax.random.fold_in(_jax.random.key(seed), 7919)
    shape, _ = _output_shape()
    out = dict(inp)
    out["loss_target"] = _jax.random.normal(_jax.random.fold_in(key, 0), shape, _jnp.float32)
    for i, name in enumerate(TWIN_WEIGHTS):
        w = inp[name].astype(_jnp.float32)
        if MOMENT_SCALE is None:
            s = _jnp.sqrt(_jnp.mean(_jnp.square(w)) + 1e-30)
        else:
            s = MOMENT_SCALE[name]
        km, kv = _jax.random.split(_jax.random.fold_in(key, i + 1))
        out[name] = w
        out["m_" + name] = s * _jax.random.normal(km, w.shape, _jnp.float32)
        out["v_" + name] = (s * s) * _jax.random.uniform(kv, w.shape, _jnp.float32, 0.5, 1.5)
    if N_MICROBATCH > 1:
        for name, axis in PER_EXAMPLE_BATCH_AXIS.items():
            out[name] = _to_microbatches(out[name], axis)
    return {'x': out['x'], 'meta_tokens': out['meta_tokens'], 'norm_mix_g': out['norm_mix_g'], 'w_in': out['w_in'], 'conv_w': out['conv_w'], 'conv_b': out['conv_b'], 'conv_ln_g': out['conv_ln_g'], 'conv_ln_b': out['conv_ln_b'], 'gla_w_gate2': out['gla_w_gate2'], 'gla_gate_b': out['gla_gate_b'], 'gla_norm_g': out['gla_norm_g'], 'w_out': out['w_out'], 'norm_ffn_g': out['norm_ffn_g'], 'w_ffn_gate': out['w_ffn_gate'], 'w_ffn_up': out['w_ffn_up'], 'w_ffn_down': out['w_ffn_down'], 'norm_final_g': out['norm_final_g'], 'loss_target': out['loss_target'], 'm_meta_tokens': out['m_meta_tokens'], 'm_norm_mix_g': out['m_norm_mix_g'], 'm_w_in': out['m_w_in'], 'm_conv_w': out['m_conv_w'], 'm_conv_b': out['m_conv_b'], 'm_conv_ln_g': out['m_conv_ln_g'], 'm_conv_ln_b': out['m_conv_ln_b'], 'm_gla_w_gate2': out['m_gla_w_gate2'], 'm_gla_gate_b': out['m_gla_gate_b'], 'm_gla_norm_g': out['m_gla_norm_g'], 'm_w_out': out['m_w_out'], 'm_norm_ffn_g': out['m_norm_ffn_g'], 'm_w_ffn_gate': out['m_w_ffn_gate'], 'm_w_ffn_up': out['m_w_ffn_up'], 'm_w_ffn_down': out['m_w_ffn_down'], 'm_norm_final_g': out['m_norm_final_g'], 'v_meta_tokens': out['v_meta_tokens'], 'v_norm_mix_g': out['v_norm_mix_g'], 'v_w_in': out['v_w_in'], 'v_conv_w': out['v_conv_w'], 'v_conv_b': out['v_conv_b'], 'v_conv_ln_g': out['v_conv_ln_g'], 'v_conv_ln_b': out['v_conv_ln_b'], 'v_gla_w_gate2': out['v_gla_w_gate2'], 'v_gla_gate_b': out['v_gla_gate_b'], 'v_gla_norm_g': out['v_gla_norm_g'], 'v_w_out': out['v_w_out'], 'v_norm_ffn_g': out['v_norm_ffn_g'], 'v_w_ffn_gate': out['v_w_ffn_gate'], 'v_w_ffn_up': out['v_w_ffn_up'], 'v_w_ffn_down': out['v_w_ffn_down'], 'v_norm_final_g': out['v_norm_final_g']}


def _loss(weights, diff, rest, loss_target):
    with _jax.named_scope("forward"):
        args = {**rest, TWIN_DIFF_INPUT: diff, **{k: w.astype(_WEIGHT_DTYPES[k]) for k, w in weights.items()}}
        y = _forward(args)
    with _jax.named_scope("loss_head"):
        err = _jnp.square(y.astype(_jnp.float32) - loss_target)
        return 0.5 * _jnp.sum(_jnp.mean(err, axis=-1)) if err.ndim else 0.5 * err


def _adamw(w, g, m, v):
    m = ADAM_B1 * m + (1.0 - ADAM_B1) * g
    v = ADAM_B2 * v + (1.0 - ADAM_B2) * _jnp.square(g)
    m_hat = m / (1.0 - ADAM_B1 ** ADAM_STEP)
    v_hat = v / (1.0 - ADAM_B2 ** ADAM_STEP)
    delta = -ADAM_LR * (m_hat / (_jnp.sqrt(v_hat) + ADAM_EPS) + ADAM_WD * w)
    return delta, m, v


def reference(x, meta_tokens, norm_mix_g, w_in, conv_w, conv_b, conv_ln_g, conv_ln_b, gla_w_gate2, gla_gate_b, gla_norm_g, w_out, norm_ffn_g, w_ffn_gate, w_ffn_up, w_ffn_down, norm_final_g, loss_target, m_meta_tokens, m_norm_mix_g, m_w_in, m_conv_w, m_conv_b, m_conv_ln_g, m_conv_ln_b, m_gla_w_gate2, m_gla_gate_b, m_gla_norm_g, m_w_out, m_norm_ffn_g, m_w_ffn_gate, m_w_ffn_up, m_w_ffn_down, m_norm_final_g, v_meta_tokens, v_norm_mix_g, v_w_in, v_conv_w, v_conv_b, v_conv_ln_g, v_conv_ln_b, v_gla_w_gate2, v_gla_gate_b, v_gla_norm_g, v_w_out, v_norm_ffn_g, v_w_ffn_gate, v_w_ffn_up, v_w_ffn_down, v_norm_final_g):
    given = dict(x=x, meta_tokens=meta_tokens, norm_mix_g=norm_mix_g, w_in=w_in, conv_w=conv_w, conv_b=conv_b, conv_ln_g=conv_ln_g, conv_ln_b=conv_ln_b, gla_w_gate2=gla_w_gate2, gla_gate_b=gla_gate_b, gla_norm_g=gla_norm_g, w_out=w_out, norm_ffn_g=norm_ffn_g, w_ffn_gate=w_ffn_gate, w_ffn_up=w_ffn_up, w_ffn_down=w_ffn_down, norm_final_g=norm_final_g, loss_target=loss_target, m_meta_tokens=m_meta_tokens, m_norm_mix_g=m_norm_mix_g, m_w_in=m_w_in, m_conv_w=m_conv_w, m_conv_b=m_conv_b, m_conv_ln_g=m_conv_ln_g, m_conv_ln_b=m_conv_ln_b, m_gla_w_gate2=m_gla_w_gate2, m_gla_gate_b=m_gla_gate_b, m_gla_norm_g=m_gla_norm_g, m_w_out=m_w_out, m_norm_ffn_g=m_norm_ffn_g, m_w_ffn_gate=m_w_ffn_gate, m_w_ffn_up=m_w_ffn_up, m_w_ffn_down=m_w_ffn_down, m_norm_final_g=m_norm_final_g, v_meta_tokens=v_meta_tokens, v_norm_mix_g=v_norm_mix_g, v_w_in=v_w_in, v_conv_w=v_conv_w, v_conv_b=v_conv_b, v_conv_ln_g=v_conv_ln_g, v_conv_ln_b=v_conv_ln_b, v_gla_w_gate2=v_gla_w_gate2, v_gla_gate_b=v_gla_gate_b, v_gla_norm_g=v_gla_norm_g, v_w_out=v_w_out, v_norm_ffn_g=v_norm_ffn_g, v_w_ffn_gate=v_w_ffn_gate, v_w_ffn_up=v_w_ffn_up, v_w_ffn_down=v_w_ffn_down, v_norm_final_g=v_norm_final_g)
    weights = {n: given[n] for n in TWIN_WEIGHTS}
    shared = {n: given[n] for n in SHARED_INPUTS}
    per_example = {n: given[n] for n in ['x']}
    grad_fn = _jax.value_and_grad(_loss, argnums=(0, 1))

    def one_microbatch(ex, loss_target):
        ex = dict(ex)
        diff = ex.pop(TWIN_DIFF_INPUT)
        return grad_fn(weights, diff, {**shared, **ex}, loss_target)

    if N_MICROBATCH == 1:
        loss, (grad_w, grad_x) = one_microbatch(per_example, given["loss_target"])
    else:
        def body(carry, xs):
            loss_sum, grad_sum = carry
            l_k, (gw_k, gx_k) = one_microbatch(xs[0], xs[1])
            with _jax.named_scope("update"):
                return (loss_sum + l_k, _jax.tree.map(_jnp.add, grad_sum, gw_k)), gx_k

        init = (_jnp.zeros((), _jnp.float32), _jax.tree.map(_jnp.zeros_like, weights))
        (loss, grad_w), grad_x = _jax.lax.scan(body, init, (per_example, given["loss_target"]))
    with _jax.named_scope("update"):
        delta_w, new_m, new_v = {}, {}, {}
        for n in TWIN_WEIGHTS:
            delta_w[n], new_m[n], new_v[n] = _adamw(weights[n], grad_w[n], given["m_" + n], given["v_" + n])
    return (loss, grad_x, *[grad_w[n] for n in TWIN_WEIGHTS], *[delta_w[n] for n in TWIN_WEIGHTS],
            *[new_m[n] for n in TWIN_WEIGHTS], *[new_v[n] for n in TWIN_WEIGHTS])
```

```python
import functools

import jax
import jax.numpy as jnp
from jax import lax
from jax.experimental import pallas as pl
from jax.experimental.pallas import tpu as pltpu

F32 = jnp.float32
BF16 = jnp.bfloat16
MESH = pl.DeviceIdType.MESH

D = 1024
N_META = 16
C_CONV = 512
CONV_W = 31
GLA_K = 256
GLA_V = 512
N_HEADS = 4
DK = 64
DV = 128
RANK = 16
CHUNK = 64
PAD_ROWS = CHUNK - N_META
HEAD_ROWS = CHUNK
D_IN = 2576
D_IN_PAD = 2688
D_GLA_IN = D_IN_PAD - 2 * C_CONV
D_FF = 2816
RMS_EPS = 1e-6
LN_EPS = 1e-5
GATE_TAU = 16.0
N_CHIPS = 4

ADAM_LR = 0.001
ADAM_B1 = 0.9
ADAM_B2 = 0.999
ADAM_EPS = 1e-08
ADAM_WD = 0.01
ADAM_STEP = 10

V7X_VMEM_BYTES = 64 * 1024 * 1024
VMEM_LIMIT = V7X_VMEM_BYTES - 8 * 1024 * 1024

SLAB_ROWS = 3072
HALF_ROWS = SLAB_ROWS // 2
SMALL_ROWS = 8


def _dot(a, b):
    return jnp.dot(a, b, preferred_element_type=F32)


def _dot_nt(a, b):
    return lax.dot_general(a, b, (((1,), (1,)), ((), ())), preferred_element_type=F32)


def _dot_tn(a, b):
    return lax.dot_general(a, b, (((0,), (0,)), ((), ())), preferred_element_type=F32)


def _sigmoid(x):
    return 1.0 / (1.0 + jnp.exp(-x))


def _const_spec(shape):
    return pl.BlockSpec(shape, lambda *_: (0,) * len(shape), pipeline_mode=pl.Buffered(1))


def _acc_spec(shape):
    return pl.BlockSpec(shape, lambda *_: (0,) * len(shape))


def _params(n_axes):
    return pltpu.CompilerParams(dimension_semantics=("arbitrary",) * n_axes, vmem_limit_bytes=VMEM_LIMIT)


def _row_tile(t, want):
    for r in (want, 384, 192, 128, 64):
        if r <= want and t % r == 0:
            return r
    raise ValueError(f"no row tile for {t}")


def _in_proj(h0, g_mix, w_in):
    t = h0.shape[0]
    r = _row_tile(t, 384)

    def body(h_ref, g_ref, w_ref, u_ref, hn_ref):
        h = h_ref[...]
        rstd = lax.rsqrt(jnp.mean(h * h, axis=-1, keepdims=True) + RMS_EPS)
        hn = (h * rstd * g_ref[...]).astype(BF16)
        hn_ref[...] = hn
        u_ref[...] = _dot(hn, w_ref[...])

    return pl.pallas_call(
        body, name="in_proj", grid=(t // r,),
        in_specs=[pl.BlockSpec((r, D), lambda i: (i, 0)), _const_spec((1, D)), _const_spec((D, D_IN_PAD))],
        out_specs=[pl.BlockSpec((r, D_IN_PAD), lambda i: (i, 0)), pl.BlockSpec((r, D), lambda i: (i, 0))],
        out_shape=[jax.ShapeDtypeStruct((t, D_IN_PAD), F32), jax.ShapeDtypeStruct((t, D), BF16)],
        compiler_params=_params(1),
    )(h0, g_mix, w_in)


CONV_TILE = 192
CONV_SUB = 32
CONV_LEAD = CONV_SUB - (CONV_W - 1)


def _conv_fwd(u, conv_w, conv_b, ln_g, ln_b, n_ex, lp):
    r = CONV_TILE
    nt = lp // r
    hb = r // CONV_SUB

    def body(cur_ref, prev_ref, w_ref, b_ref, lg_ref, lb_ref, yc_ref, y_ref, glu):
        i = pl.program_id(1)
        cur = cur_ref[...]
        glu[CONV_SUB:CONV_SUB + r, :] = cur[:, :C_CONV] * _sigmoid(cur[:, C_CONV:])
        pv = prev_ref[...]
        halo = pv[:, :C_CONV] * _sigmoid(pv[:, C_CONV:])
        glu[0:CONV_SUB, :] = jnp.where(i > 0, halo, 0.0)
        w = w_ref[...]
        for j in range(r // CONV_SUB):
            r0 = j * CONV_SUB
            acc = jnp.zeros((CONV_SUB, C_CONV), F32) + b_ref[...]
            for k in range(CONV_W):
                acc = acc + w[k:k + 1, :] * glu[r0 + CONV_LEAD + k:r0 + CONV_LEAD + k + CONV_SUB, :]
            mu = jnp.mean(acc, axis=-1, keepdims=True)
            cen = acc - mu
            var = jnp.mean(cen * cen, axis=-1, keepdims=True)
            out = cen * lax.rsqrt(var + LN_EPS) * lg_ref[...] + lb_ref[...]
            y = out * _sigmoid(out)
            row = i * r + r0 + lax.broadcasted_iota(jnp.int32, (CONV_SUB, 1), 0)
            y = jnp.where(row >= PAD_ROWS, y, 0.0)
            yc_ref[r0:r0 + CONV_SUB, :] = acc
            y_ref[r0:r0 + CONV_SUB, :] = y.astype(BF16)

    t = n_ex * lp
    return pl.pallas_call(
        body, name="conv_fwd", grid=(n_ex, nt),
        in_specs=[pl.BlockSpec((r, 2 * C_CONV), lambda b, i: (b * nt + i, 0)),
                  pl.BlockSpec((CONV_SUB, 2 * C_CONV), lambda b, i: (jnp.maximum((b * nt + i) * hb - 1, 0), 0)),
                  _const_spec((32, C_CONV)), _const_spec((1, C_CONV)), _const_spec((1, C_CONV)), _const_spec((1, C_CONV))],
        out_specs=[pl.BlockSpec((r, C_CONV), lambda b, i: (b * nt + i, 0)),
                   pl.BlockSpec((r, C_CONV), lambda b, i: (b * nt + i, 0))],
        out_shape=[jax.ShapeDtypeStruct((t, C_CONV), F32), jax.ShapeDtypeStruct((t, C_CONV), BF16)],
        scratch_shapes=[pltpu.VMEM((r + CONV_SUB, C_CONV), F32)],
        compiler_params=_params(2),
    )(u, u, conv_w, conv_b, ln_g, ln_b)


def _gla_gates(lr, w2, gb, first_chunk):
    z = _dot(lr.astype(BF16), w2) + gb
    a = (jnp.minimum(z, 0.0) - jnp.log(1.0 + jnp.exp(-jnp.abs(z)))) * (1.0 / GATE_TAU)
    row = lax.broadcasted_iota(jnp.int32, (CHUNK, 1), 0)
    live = jnp.logical_or(jnp.logical_not(first_chunk), row >= PAD_ROWS)
    return z, jnp.where(live, a, 0.0), live


def _tri(lower):
    i = lax.broadcasted_iota(jnp.int32, (CHUNK, CHUNK), 0)
    j = lax.broadcasted_iota(jnp.int32, (CHUNK, CHUNK), 1)
    return (i >= j) if lower else (i <= j)


def _gla_fwd(u, w2, gb, ng, n_ex, lp):
    nc = lp // CHUNK
    t = n_ex * lp

    def body(qk_ref, v_ref, g_ref, lr_ref, w2_ref, gb_ref, ng_ref, y_ref, st_ref, state):
        n = pl.program_id(1)

        @pl.when(n == 0)
        def _():
            state[...] = jnp.zeros_like(state)

        st = state[...]
        st_ref[...] = st
        qk = qk_ref[...]
        q, k = qk[:, :GLA_K], qk[:, GLA_K:]
        _, a, _ = _gla_gates(lr_ref[...], w2_ref[...], gb_ref[...], n == 0)
        causal = _tri(True)
        b = jnp.dot(causal.astype(F32), a, preferred_element_type=F32, precision=lax.Precision.HIGHEST)
        bl = b[CHUNK - 1:CHUNK, :]
        q_in = (q * (DK ** -0.5) * jnp.exp(b)).astype(BF16)
        k_in = (k * jnp.exp(-b)).astype(BF16)
        k_dec = (k * jnp.exp(bl - b)).astype(BF16)
        decay = jnp.exp(bl)
        v = v_ref[...]
        g = g_ref[...]
        st_b = st.astype(BF16)
        ys, new = [], []
        for h in range(N_HEADS):
            ks = slice(h * DK, (h + 1) * DK)
            vs = slice(h * DV, (h + 1) * DV)
            vh = v[:, vs].astype(BF16)
            s = jnp.where(causal, _dot_nt(q_in[:, ks], k_in[:, ks]), 0.0)
            o = _dot(s.astype(BF16), vh) + _dot_nt(q_in[:, ks], st_b[:, ks])
            new.append(decay[:, ks] * st[:, ks] + _dot_tn(vh, k_dec[:, ks]))
            rstd = lax.rsqrt(jnp.mean(o * o, axis=-1, keepdims=True) + RMS_EPS)
            gh = g[:, vs]
            ys.append(o * rstd * ng_ref[...] * (gh * _sigmoid(gh)))
        state[...] = jnp.concatenate(new, axis=1)
        y_ref[...] = jnp.concatenate(ys, axis=1).astype(BF16)

    blk = lambda w, col: pl.BlockSpec((CHUNK, w), lambda b, n: (b * nc + n, col))
    return pl.pallas_call(
        body, name="gla_fwd", grid=(n_ex, nc),
        in_specs=[blk(2 * GLA_K, 2), blk(GLA_V, 3), blk(GLA_V, 4), blk(128, 20),
                  _const_spec((128, GLA_K)), _const_spec((1, GLA_K)), _const_spec((1, DV))],
        out_specs=[pl.BlockSpec((CHUNK, GLA_V), lambda b, n: (b * nc + n, 0)),
                   pl.BlockSpec((DV, GLA_K), lambda b, n: (b * nc + n, 0))],
        out_shape=[jax.ShapeDtypeStruct((t, GLA_V), BF16), jax.ShapeDtypeStruct((n_ex * nc * DV, GLA_K), F32)],
        scratch_shapes=[pltpu.VMEM((DV, GLA_K), F32)],
        compiler_params=_params(2),
    )(u, u, u, u, w2, gb, ng)


FFN_TILE = 192


def _mix_out_ffn_up(h0, y_conv, y_gla, w_out, g_ffn, w_gate, w_up):
    t = h0.shape[0]
    r = _row_tile(t, FFN_TILE)

    def body(h0_ref, yc_ref, yg_ref, wo_ref, g_ref, wg_ref, wu_ref, h1_ref, hn_ref, gate_ref, up_ref, act_ref):
        h1 = h0_ref[...] + _dot(yc_ref[...], wo_ref[0:C_CONV, :]) + _dot(yg_ref[...], wo_ref[C_CONV:D, :])
        h1_ref[...] = h1
        rstd = lax.rsqrt(jnp.mean(h1 * h1, axis=-1, keepdims=True) + RMS_EPS)
        hn = (h1 * rstd * g_ref[...]).astype(BF16)
        hn_ref[...] = hn
        gate = _dot(hn, wg_ref[...])
        up = _dot(hn, wu_ref[...])
        gate_ref[...] = gate
        up_ref[...] = up
        act_ref[...] = (gate * _sigmoid(gate) * up).astype(BF16)

    rows = lambda w: pl.BlockSpec((r, w), lambda i: (i, 0))
    return pl.pallas_call(
        body, name="mix_out_ffn_up", grid=(t // r,),
        in_specs=[rows(D), rows(C_CONV), rows(GLA_V), _const_spec((D, D)), _const_spec((1, D)),
                  _const_spec((D, D_FF)), _const_spec((D, D_FF))],
        out_specs=[rows(D), rows(D), rows(D_FF), rows(D_FF), rows(D_FF)],
        out_shape=[jax.ShapeDtypeStruct((t, D), F32), jax.ShapeDtypeStruct((t, D), BF16),
                   jax.ShapeDtypeStruct((t, D_FF), F32), jax.ShapeDtypeStruct((t, D_FF), F32),
                   jax.ShapeDtypeStruct((t, D_FF), BF16)],
        compiler_params=_params(1),
    )(h0, y_conv, y_gla, w_out, g_ffn, w_gate, w_up)


def _ffn_down_loss(act, w_down, h1, target, g_final, row_mask):
    t = h1.shape[0]
    r = _row_tile(t, 384)

    def body(act_ref, wd_ref, h1_ref, tgt_ref, gf_ref, mask_ref, dh2_ref, loss_ref, dgf_ref):
        @pl.when(pl.program_id(0) == 0)
        def _():
            loss_ref[...] = jnp.zeros_like(loss_ref)
            dgf_ref[...] = jnp.zeros_like(dgf_ref)

        h2 = h1_ref[...] + _dot(act_ref[...], wd_ref[...])
        rstd = lax.rsqrt(jnp.mean(h2 * h2, axis=-1, keepdims=True) + RMS_EPS)
        nrm = h2 * rstd
        gf = gf_ref[...]
        err = (nrm * gf - tgt_ref[...]) * mask_ref[...]
        loss_ref[...] += jnp.sum(err * err) * (0.5 / D)
        dy = err * (1.0 / D)
        dgf_ref[...] += jnp.sum(dy * nrm, axis=0, keepdims=True)
        dn = dy * gf
        dh2_ref[...] = rstd * (dn - nrm * jnp.mean(dn * nrm, axis=-1, keepdims=True))

    rows = lambda w: pl.BlockSpec((r, w), lambda i: (i, 0))
    return pl.pallas_call(
        body, name="ffn_down_loss", grid=(t // r,),
        in_specs=[rows(D_FF), _const_spec((D_FF, D)), rows(D), rows(D), _const_spec((1, D)), rows(1)],
        out_specs=[rows(D), _acc_spec((1, 128)), _acc_spec((1, D))],
        out_shape=[jax.ShapeDtypeStruct((t, D), F32), jax.ShapeDtypeStruct((1, 128), F32),
                   jax.ShapeDtypeStruct((1, D), F32)],
        compiler_params=_params(1),
    )(act, w_down, h1, target, g_final, row_mask)


def _ffn_bwd(dh2, gate, up, h1, w_down, w_gate, w_up, w_out, g_ffn):
    t = h1.shape[0]
    r = _row_tile(t, FFN_TILE)

    def body(dh2_ref, gate_ref, up_ref, h1_ref, wd_ref, wg_ref, wu_ref, wo_ref, g_ref,
             dgate_ref, dup_ref, dh1_ref, dycat_ref, dg_ref):
        @pl.when(pl.program_id(0) == 0)
        def _():
            dg_ref[...] = jnp.zeros_like(dg_ref)

        dh2 = dh2_ref[...]
        dact = _dot_nt(dh2.astype(BF16), wd_ref[...])
        gate = gate_ref[...]
        sg = _sigmoid(gate)
        dgate = (dact * up_ref[...] * (sg * (1.0 + gate * (1.0 - sg)))).astype(BF16)
        dup = (dact * (gate * sg)).astype(BF16)
        dgate_ref[...] = dgate
        dup_ref[...] = dup
        dhn = _dot_nt(dgate, wg_ref[...]) + _dot_nt(dup, wu_ref[...])
        h1 = h1_ref[...]
        rstd = lax.rsqrt(jnp.mean(h1 * h1, axis=-1, keepdims=True) + RMS_EPS)
        nrm = h1 * rstd
        dg_ref[...] += jnp.sum(dhn * nrm, axis=0, keepdims=True)
        dn = dhn * g_ref[...]
        dh1 = dh2 + rstd * (dn - nrm * jnp.mean(dn * nrm, axis=-1, keepdims=True))
        dh1_ref[...] = dh1
        dycat_ref[...] = _dot_nt(dh1.astype(BF16), wo_ref[...])

    rows = lambda w: pl.BlockSpec((r, w), lambda i: (i, 0))
    return pl.pallas_call(
        body, name="ffn_bwd", grid=(t // r,),
        in_specs=[rows(D), rows(D_FF), rows(D_FF), rows(D), _const_spec((D_FF, D)), _const_spec((D, D_FF)),
                  _const_spec((D, D_FF)), _const_spec((D, D)), _const_spec((1, D))],
        out_specs=[rows(D_FF), rows(D_FF), rows(D), rows(D), _acc_spec((1, D))],
        out_shape=[jax.ShapeDtypeStruct((t, D_FF), BF16), jax.ShapeDtypeStruct((t, D_FF), BF16),
                   jax.ShapeDtypeStruct((t, D), F32), jax.ShapeDtypeStruct((t, D), F32),
                   jax.ShapeDtypeStruct((1, D), F32)],
        compiler_params=_params(1),
    )(dh2, gate, up, h1, w_down, w_gate, w_up, w_out, g_ffn)


def _conv_bwd(dycat, yc, u, conv_w, ln_g, ln_b, n_ex, lp):
    r = CONV_TILE
    nt = lp // r
    hb = r // CONV_SUB
    nsub = r // CONV_SUB

    def ln_bwd(dy, yc_rows, live, lg, lb):
        mu = jnp.mean(yc_rows, axis=-1, keepdims=True)
        cen = yc_rows - mu
        rs = lax.rsqrt(jnp.mean(cen * cen, axis=-1, keepdims=True) + LN_EPS)
        yn = cen * rs
        out = yn * lg + lb
        so = _sigmoid(out)
        dout = jnp.where(live, dy * (so * (1.0 + out * (1.0 - so))), 0.0)
        dyn = dout * lg
        dyc = rs * (dyn - jnp.mean(dyn, axis=-1, keepdims=True) - yn * jnp.mean(dyn * yn, axis=-1, keepdims=True))
        return dyc, dout, yn

    def body(dy_ref, dyn_ref, yc_ref, ycn_ref, cur_ref, prev_ref, w_ref, lg_ref, lb_ref,
             du_ref, dw_ref, db_ref, dlg_ref, dlb_ref, glu, dycs, dwacc):
        b = pl.program_id(0)
        i = pl.program_id(1)
        first = jnp.logical_and(b == 0, i == 0)

        @pl.when(first)
        def _():
            dwacc[...] = jnp.zeros_like(dwacc)
            db_ref[...] = jnp.zeros_like(db_ref)
            dlg_ref[...] = jnp.zeros_like(dlg_ref)
            dlb_ref[...] = jnp.zeros_like(dlb_ref)

        lg, lb = lg_ref[...], lb_ref[...]
        cur = cur_ref[...]
        sig = _sigmoid(cur[:, C_CONV:])
        glu[CONV_SUB:CONV_SUB + r, :] = cur[:, :C_CONV] * sig
        pv = prev_ref[...]
        glu[0:CONV_SUB, :] = jnp.where(i > 0, pv[:, :C_CONV] * _sigmoid(pv[:, C_CONV:]), 0.0)

        row = i * r + lax.broadcasted_iota(jnp.int32, (r, 1), 0)
        dyc, dout, yn = ln_bwd(dy_ref[...], yc_ref[...], row >= PAD_ROWS, lg, lb)
        dycs[0:r, :] = dyc
        dycn, _, _ = ln_bwd(dyn_ref[...], ycn_ref[...], i < nt - 1, lg, lb)
        dycs[r:r + CONV_SUB, :] = dycn
        db_ref[...] += jnp.sum(dyc, axis=0, keepdims=True)
        dlg_ref[...] += jnp.sum(dout * yn, axis=0, keepdims=True)
        dlb_ref[...] += jnp.sum(dout, axis=0, keepdims=True)

        w = w_ref[...]
        for j in range(nsub):
            r0 = j * CONV_SUB
            dblk = dycs[r0:r0 + CONV_SUB, :]
            dglu = jnp.zeros((CONV_SUB, C_CONV), F32)
            for k in range(CONV_W):
                dglu = dglu + w[k:k + 1, :] * dycs[r0 + (CONV_W - 1) - k:r0 + (CONV_W - 1) - k + CONV_SUB, :]
                prod = dblk * glu[r0 + CONV_LEAD + k:r0 + CONV_LEAD + k + CONV_SUB, :]
                dwacc[k] += prod.reshape(CONV_SUB // 8, 8, C_CONV).sum(axis=0)
            sg = sig[r0:r0 + CONV_SUB, :]
            cv = cur[r0:r0 + CONV_SUB, :C_CONV]
            du_ref[r0:r0 + CONV_SUB, :C_CONV] = (dglu * sg).astype(BF16)
            du_ref[r0:r0 + CONV_SUB, C_CONV:] = (dglu * cv * sg * (1.0 - sg)).astype(BF16)

        @pl.when(jnp.logical_and(b == n_ex - 1, i == nt - 1))
        def _():
            dw_ref[...] = jnp.sum(dwacc[...], axis=1)

    t = n_ex * lp
    cur_rows = lambda w, col: pl.BlockSpec((r, w), lambda b, i: (b * nt + i, col))
    nxt_rows = lambda w, col: pl.BlockSpec(
        (CONV_SUB, w), lambda b, i: (jnp.minimum((b * nt + i + 1) * hb, n_ex * nt * hb - 1), col))
    return pl.pallas_call(
        body, name="conv_bwd", grid=(n_ex, nt),
        in_specs=[cur_rows(C_CONV, 0), nxt_rows(C_CONV, 0), cur_rows(C_CONV, 0), nxt_rows(C_CONV, 0),
                  cur_rows(2 * C_CONV, 0),
                  pl.BlockSpec((CONV_SUB, 2 * C_CONV), lambda b, i: (jnp.maximum((b * nt + i) * hb - 1, 0), 0)),
                  _const_spec((32, C_CONV)), _const_spec((1, C_CONV)), _const_spec((1, C_CONV))],
        out_specs=[cur_rows(2 * C_CONV, 0), _acc_spec((32, C_CONV)), _acc_spec((1, C_CONV)),
                   _acc_spec((1, C_CONV)), _acc_spec((1, C_CONV))],
        out_shape=[jax.ShapeDtypeStruct((t, 2 * C_CONV), BF16), jax.ShapeDtypeStruct((32, C_CONV), F32),
                   jax.ShapeDtypeStruct((1, C_CONV), F32), jax.ShapeDtypeStruct((1, C_CONV), F32),
                   jax.ShapeDtypeStruct((1, C_CONV), F32)],
        scratch_shapes=[pltpu.VMEM((r + CONV_SUB, C_CONV), F32), pltpu.VMEM((r + CONV_SUB, C_CONV), F32),
                        pltpu.VMEM((32, 8, C_CONV), F32)],
        compiler_params=_params(2),
    )(dycat, dycat, yc, yc, u, u, conv_w, ln_g, ln_b)


def _gla_bwd(dycat, u, states, w2, gb, ng, n_ex, lp):
    nc = lp // CHUNK
    t = n_ex * lp

    def body(dy_ref, qk_ref, v_ref, g_ref, lr_ref, st_ref, w2_ref, gb_ref, ng_ref,
             du_ref, dw2_ref, dgb_ref, dng_ref, dstate):
        bi = pl.program_id(0)
        n = pl.program_id(1)
        chunk = nc - 1 - n

        @pl.when(jnp.logical_and(bi == 0, n == 0))
        def _():
            dw2_ref[...] = jnp.zeros_like(dw2_ref)
            dgb_ref[...] = jnp.zeros_like(dgb_ref)
            dng_ref[...] = jnp.zeros_like(dng_ref)

        @pl.when(n == 0)
        def _():
            dstate[...] = jnp.zeros_like(dstate)

        qk = qk_ref[...]
        q, k = qk[:, :GLA_K], qk[:, GLA_K:]
        lr = lr_ref[...]
        z, a, live = _gla_gates(lr, w2_ref[...], gb_ref[...], chunk == 0)
        causal = _tri(True)
        b = jnp.dot(causal.astype(F32), a, preferred_element_type=F32, precision=lax.Precision.HIGHEST)
        bl = b[CHUNK - 1:CHUNK, :]
        e_pos, e_neg, e_dec = jnp.exp(b), jnp.exp(-b), jnp.exp(bl - b)
        q_f = q * (DK ** -0.5) * e_pos
        k_f = k * e_neg
        kd_f = k * e_dec
        q_in, k_in, k_dec = q_f.astype(BF16), k_f.astype(BF16), kd_f.astype(BF16)
        decay = jnp.exp(bl)
        v = v_ref[...]
        g = g_ref[...]
        dy = dy_ref[...]
        ngv = ng_ref[...]
        st = st_ref[...]
        st_b = st.astype(BF16)
        dst = dstate[...]
        dst_b = dst.astype(BF16)
        dqs, dks, dvs, dgs, dbs, dbls, new_dst = [], [], [], [], [], [], []
        dng = jnp.zeros((1, DV), F32)
        for h in range(N_HEADS):
            ks = slice(h * DK, (h + 1) * DK)
            vs = slice(h * DV, (h + 1) * DV)
            qh, kh, kdh = q_in[:, ks], k_in[:, ks], k_dec[:, ks]
            vh = v[:, vs].astype(BF16)
            s = jnp.where(causal, _dot_nt(qh, kh), 0.0).astype(BF16)
            o = _dot(s, vh) + _dot_nt(qh, st_b[:, ks])
            rstd = lax.rsqrt(jnp.mean(o * o, axis=-1, keepdims=True) + RMS_EPS)
            nrm = o * rstd
            gh = g[:, vs]
            sg = _sigmoid(gh)
            dyh = dy[:, vs]
            dgs.append(dyh * nrm * ngv * (sg * (1.0 + gh * (1.0 - sg))))
            dt = dyh * (gh * sg)
            dng = dng + jnp.sum(dt * nrm, axis=0, keepdims=True)
            dn = dt * ngv
            do = (rstd * (dn - nrm * jnp.mean(dn * nrm, axis=-1, keepdims=True))).astype(BF16)
            da = jnp.where(causal, _dot_nt(do, vh), 0.0).astype(BF16)
            dvs.append(_dot_tn(s, do) + _dot_nt(kdh, dst_b[:, ks]))
            dq_in = _dot(da, kh) + _dot(do, st_b[:, ks])
            dk_in = _dot_tn(da, qh)
            dk_dec = _dot(vh, dst_b[:, ks])
            new_dst.append(_dot_tn(do, qh) + decay[:, ks] * dst[:, ks])
            dbls.append(jnp.sum(dk_dec * kd_f[:, ks], axis=0, keepdims=True)
                        + decay[:, ks] * jnp.sum(dst[:, ks] * st[:, ks], axis=0, keepdims=True))
            dqs.append(dq_in * (DK ** -0.5) * e_pos[:, ks])
            dks.append(dk_in * e_neg[:, ks] + dk_dec * e_dec[:, ks])
            dbs.append(dq_in * q_f[:, ks] - dk_in * k_f[:, ks] - dk_dec * kd_f[:, ks])
        dstate[...] = jnp.concatenate(new_dst, axis=1)
        row = lax.broadcasted_iota(jnp.int32, (CHUNK, 1), 0)
        db = jnp.concatenate(dbs, axis=1) + jnp.where(row == CHUNK - 1, jnp.concatenate(dbls, axis=1), 0.0)
        da_log = jnp.dot(_tri(False).astype(F32), db, preferred_element_type=F32, precision=lax.Precision.HIGHEST)
        dz = jnp.where(live, da_log * (1.0 - _sigmoid(z)) * (1.0 / GATE_TAU), 0.0)
        dz_b = dz.astype(BF16)
        du_ref[:, 0:GLA_K] = jnp.concatenate(dqs, axis=1).astype(BF16)
        du_ref[:, GLA_K:2 * GLA_K] = jnp.concatenate(dks, axis=1).astype(BF16)
        du_ref[:, 2 * GLA_K:2 * GLA_K + GLA_V] = jnp.concatenate(dvs, axis=1).astype(BF16)
        du_ref[:, 2 * GLA_K + GLA_V:2 * GLA_K + 2 * GLA_V] = jnp.concatenate(dgs, axis=1).astype(BF16)
        du_ref[:, 2 * GLA_K + 2 * GLA_V:] = _dot_nt(dz_b, w2_ref[...]).astype(BF16)
        dw2_ref[...] += _dot_tn(lr.astype(BF16), dz_b)
        dgb_ref[...] += jnp.sum(dz, axis=0, keepdims=True)
        dng_ref[...] += dng

    rev = lambda w, col: pl.BlockSpec((CHUNK, w), lambda b, n: (b * nc + nc - 1 - n, col))
    return pl.pallas_call(
        body, name="gla_bwd", grid=(n_ex, nc),
        in_specs=[rev(GLA_V, 1), rev(2 * GLA_K, 2), rev(GLA_V, 3), rev(GLA_V, 4), rev(128, 20),
                  pl.BlockSpec((DV, GLA_K), lambda b, n: (b * nc + nc - 1 - n, 0)),
                  _const_spec((128, GLA_K)), _const_spec((1, GLA_K)), _const_spec((1, DV))],
        out_specs=[rev(D_GLA_IN, 0), _acc_spec((128, GLA_K)), _acc_spec((1, GLA_K)), _acc_spec((1, DV))],
        out_shape=[jax.ShapeDtypeStruct((t, D_GLA_IN), BF16), jax.ShapeDtypeStruct((128, GLA_K), F32),
                   jax.ShapeDtypeStruct((1, GLA_K), F32), jax.ShapeDtypeStruct((1, DV), F32)],
        scratch_shapes=[pltpu.VMEM((DV, GLA_K), F32)],
        compiler_params=_params(2),
    )(dycat, u, u, u, u, states, w2, gb, ng)


def _in_proj_bwd(du_conv, du_gla, w_in_conv, w_in_gla, h0, dh1, g_mix):
    t = h0.shape[0]
    r = _row_tile(t, 384)

    def body(dc_ref, dg_ref, wc_ref, wg_ref, h_ref, dh1_ref, g_ref, dh0_ref, dgm_ref):
        @pl.when(pl.program_id(0) == 0)
        def _():
            dgm_ref[...] = jnp.zeros_like(dgm_ref)

        dhn = _dot_nt(dc_ref[...], wc_ref[...]) + _dot_nt(dg_ref[...], wg_ref[...])
        h = h_ref[...]
        rstd = lax.rsqrt(jnp.mean(h * h, axis=-1, keepdims=True) + RMS_EPS)
        nrm = h * rstd
        dgm_ref[...] += jnp.sum(dhn * nrm, axis=0, keepdims=True)
        dn = dhn * g_ref[...]
        dh0_ref[...] = dh1_ref[...] + rstd * (dn - nrm * jnp.mean(dn * nrm, axis=-1, keepdims=True))

    rows = lambda w: pl.BlockSpec((r, w), lambda i: (i, 0))
    return pl.pallas_call(
        body, name="in_proj_bwd", grid=(t // r,),
        in_specs=[rows(2 * C_CONV), rows(D_GLA_IN), _const_spec((D, 2 * C_CONV)), _const_spec((D, D_GLA_IN)),
                  rows(D), rows(D), _const_spec((1, D))],
        out_specs=[rows(D), _acc_spec((1, D))],
        out_shape=[jax.ShapeDtypeStruct((t, D), F32), jax.ShapeDtypeStruct((1, D), F32)],
        compiler_params=_params(1),
    )(du_conv, du_gla, w_in_conv, w_in_gla, h0, dh1, g_mix)


def _wgrad(x, dy, name):
    t, m = x.shape
    n = dy.shape[1]
    tk = _row_tile(t, 384)
    tm = m if m <= 1024 else m // 2
    tn = n if n <= 1664 else n // 2

    def body(x_ref, dy_ref, o_ref):
        @pl.when(pl.program_id(2) == 0)
        def _():
            o_ref[...] = jnp.zeros_like(o_ref)

        o_ref[...] += _dot_tn(x_ref[...].astype(BF16), dy_ref[...].astype(BF16))

    return pl.pallas_call(
        body, name=name, grid=(m // tm, n // tn, t // tk),
        in_specs=[pl.BlockSpec((tk, tm), lambda i, j, k: (k, i)), pl.BlockSpec((tk, tn), lambda i, j, k: (k, j))],
        out_specs=pl.BlockSpec((tm, tn), lambda i, j, k: (i, j)),
        out_shape=jax.ShapeDtypeStruct((m, n), F32),
        compiler_params=_params(3),
    )(x, dy)


def _mesh_pos():
    return lax.axis_index("x"), lax.axis_index("y"), lax.axis_index("c")


def _other_chips(x, y):
    return [(1 - x, y), (x, 1 - y), (1 - x, 1 - y)]


HBM_SPEC = pl.BlockSpec(memory_space=pltpu.HBM)


def _gather_shards(shards):
    n = len(shards)

    def body(*refs):
        ins, outs = refs[:n], refs[n:2 * n]
        send_sems, recv_sems, local_sems = refs[2 * n:]
        x, y, c = _mesh_pos()
        mine = 2 * x + y
        chips = _other_chips(x, y)
        local = [pltpu.make_async_copy(ins[a], outs[a].at[mine], local_sems.at[a]) for a in range(n)]
        for cp in local:
            cp.start()

        def remote(a, k, block):
            px, py = chips[k]
            return pltpu.make_async_remote_copy(
                src_ref=ins[a], dst_ref=outs[a].at[block], send_sem=send_sems.at[3 * a + k],
                recv_sem=recv_sems.at[3 * a + k], device_id=(px, py, c), device_id_type=MESH)

        sends = [remote(a, k, mine) for a in range(n) for k in range(3)]
        for cp in sends:
            cp.start()
        for a in range(n):
            for k, (px, py) in enumerate(chips):
                remote(a, k, 2 * px + py).wait_recv()
        for cp in sends:
            cp.wait_send()
        for cp in local:
            cp.wait()

    return pl.pallas_call(
        body, name="gather_shards",
        in_specs=[HBM_SPEC] * n, out_specs=[HBM_SPEC] * n,
        out_shape=[jax.ShapeDtypeStruct((N_CHIPS,) + s.shape, s.dtype) for s in shards],
        scratch_shapes=[pltpu.SemaphoreType.DMA((3 * n,)), pltpu.SemaphoreType.DMA((3 * n,)),
                        pltpu.SemaphoreType.DMA((n,))],
        compiler_params=pltpu.CompilerParams(has_side_effects=True),
    )(*shards)


def _send_half_to_sibling(g2):
    def body(g_ref, recv_ref, send_sem, recv_sem):
        x, y, c = _mesh_pos()
        cp = pltpu.make_async_remote_copy(
            src_ref=g_ref.at[1 - c], dst_ref=recv_ref, send_sem=send_sem, recv_sem=recv_sem,
            device_id=(x, y, 1 - c), device_id_type=MESH)
        cp.start()
        cp.wait()

    return pl.pallas_call(
        body, name="rs_to_sibling", in_specs=[HBM_SPEC], out_specs=HBM_SPEC,
        out_shape=jax.ShapeDtypeStruct(g2.shape[1:], g2.dtype),
        scratch_shapes=[pltpu.SemaphoreType.DMA(()), pltpu.SemaphoreType.DMA(())],
        compiler_params=pltpu.CompilerParams(has_side_effects=True),
    )(g2)


def _add_own_half(g2, recv, c):
    rows = N_CHIPS * HALF_ROWS
    tr = 512
    g2f = g2.reshape(2, rows, D)
    recvf = recv.reshape(rows, D)

    def body(c_ref, a_ref, b_ref, o_ref):
        o_ref[...] = a_ref[0] + b_ref[...]

    out = pl.pallas_call(
        body, name="rs_add_halves",
        grid_spec=pltpu.PrefetchScalarGridSpec(
            num_scalar_prefetch=1, grid=(rows // tr,),
            in_specs=[pl.BlockSpec((1, tr, D), lambda i, s: (s[0], i, 0)), pl.BlockSpec((tr, D), lambda i, s: (i, 0))],
            out_specs=pl.BlockSpec((tr, D), lambda i, s: (i, 0))),
        out_shape=jax.ShapeDtypeStruct((rows, D), F32),
        compiler_params=_params(1),
    )(jnp.reshape(c, (1,)).astype(jnp.int32), g2f, recvf)
    return out.reshape(N_CHIPS, HALF_ROWS, D)


def _exchange_chip_sums(p):
    def body(p_ref, out_ref, send_sems, recv_sems, local_sem):
        x, y, c = _mesh_pos()
        mine = 2 * x + y
        chips = _other_chips(x, y)
        local = pltpu.make_async_copy(p_ref.at[mine], out_ref.at[mine], local_sem)
        local.start()

        def remote(k, src_block, dst_block):
            px, py = chips[k]
            return pltpu.make_async_remote_copy(
                src_ref=p_ref.at[src_block], dst_ref=out_ref.at[dst_block], send_sem=send_sems.at[k],
                recv_sem=recv_sems.at[k], device_id=(px, py, c), device_id_type=MESH)

        sends = [remote(k, 2 * px + py, mine) for k, (px, py) in enumerate(chips)]
        for cp in sends:
            cp.start()
        for k, (px, py) in enumerate(chips):
            remote(k, mine, 2 * px + py).wait_recv()
        for cp in sends:
            cp.wait_send()
        local.wait()

    return pl.pallas_call(
        body, name="rs_chip_exchange", in_specs=[HBM_SPEC], out_specs=HBM_SPEC,
        out_shape=jax.ShapeDtypeStruct(p.shape, p.dtype),
        scratch_shapes=[pltpu.SemaphoreType.DMA((3,)), pltpu.SemaphoreType.DMA((3,)), pltpu.SemaphoreType.DMA(())],
        compiler_params=pltpu.CompilerParams(has_side_effects=True),
    )(p)


def _sum_chips(parts):
    tr = 512

    def body(p_ref, o_ref):
        o_ref[...] = ((p_ref[0] + p_ref[1]) + p_ref[2]) + p_ref[3]

    return pl.pallas_call(
        body, name="rs_sum_chips", grid=(HALF_ROWS // tr,),
        in_specs=[pl.BlockSpec((N_CHIPS, tr, D), lambda i: (0, i, 0))],
        out_specs=pl.BlockSpec((tr, D), lambda i: (i, 0)),
        out_shape=jax.ShapeDtypeStruct((HALF_ROWS, D), F32),
        compiler_params=_params(1),
    )(parts)


def _share_with_sibling(half):
    def body(h_ref, out_ref, send_sem, recv_sem, local_sem):
        x, y, c = _mesh_pos()
        local = pltpu.make_async_copy(h_ref, out_ref.at[c], local_sem)
        local.start()
        cp = pltpu.make_async_remote_copy(
            src_ref=h_ref, dst_ref=out_ref.at[c], send_sem=send_sem, recv_sem=recv_sem,
            device_id=(x, y, 1 - c), device_id_type=MESH)
        cp.start()
        pltpu.make_async_remote_copy(
            src_ref=h_ref, dst_ref=out_ref.at[1 - c], send_sem=send_sem, recv_sem=recv_sem,
            device_id=(x, y, 1 - c), device_id_type=MESH).wait_recv()
        cp.wait_send()
        local.wait()

    return pl.pallas_call(
        body, name="rs_share_sibling", in_specs=[HBM_SPEC], out_specs=HBM_SPEC,
        out_shape=jax.ShapeDtypeStruct((2,) + half.shape, half.dtype),
        scratch_shapes=[pltpu.SemaphoreType.DMA(()), pltpu.SemaphoreType.DMA(()), pltpu.SemaphoreType.DMA(())],
        compiler_params=pltpu.CompilerParams(has_side_effects=True),
    )(half)


def _adam_update(g, w, m, v):
    m2 = ADAM_B1 * m + (1.0 - ADAM_B1) * g
    v2 = ADAM_B2 * v + (1.0 - ADAM_B2) * (g * g)
    m_hat = m2 / (1.0 - ADAM_B1 ** ADAM_STEP)
    v_hat = v2 / (1.0 - ADAM_B2 ** ADAM_STEP)
    delta = -ADAM_LR * (m_hat / (jnp.sqrt(v_hat) + ADAM_EPS) + ADAM_WD * w)
    return delta, m2, v2


def _adamw_slab(g, w, m, v):
    rows = g.shape[0]
    tr = 256

    def body(g_ref, w_ref, m_ref, v_ref, d_ref, m2_ref, v2_ref):
        d_ref[...], m2_ref[...], v2_ref[...] = _adam_update(g_ref[...], w_ref[...], m_ref[...], v_ref[...])

    spec = pl.BlockSpec((tr, D), lambda i: (i, 0))
    return pl.pallas_call(
        body, name="adamw_slab", grid=(rows // tr,), in_specs=[spec] * 4, out_specs=[spec] * 3,
        out_shape=[jax.ShapeDtypeStruct((rows, D), F32)] * 3,
        compiler_params=_params(1),
    )(g, w, m, v)


def _allreduce_small_adamw(part, w, m, v):
    def body(p_ref, w_ref, m_ref, v_ref, g_ref, d_ref, m2_ref, v2_ref, slots, send_sems, recv_sems):
        x, y, c = _mesh_pos()
        mine = 4 * x + 2 * y + c
        peers = [(px, py, pc) for px in (x, 1 - x) for py in (y, 1 - y) for pc in (c, 1 - c)][1:]

        def remote(k, slot):
            return pltpu.make_async_remote_copy(
                src_ref=p_ref, dst_ref=slots.at[slot], send_sem=send_sems.at[k], recv_sem=recv_sems.at[k],
                device_id=peers[k], device_id_type=MESH)

        sends = [remote(k, mine) for k in range(7)]
        for cp in sends:
            cp.start()
        slots[mine] = p_ref[...]
        for k, (px, py, pc) in enumerate(peers):
            remote(k, 4 * px + 2 * py + pc).wait_recv()
        for cp in sends:
            cp.wait_send()
        g = slots[0]
        for d in range(1, 8):
            g = g + slots[d]
        g_ref[...] = g
        d_ref[...], m2_ref[...], v2_ref[...] = _adam_update(g, w_ref[...], m_ref[...], v_ref[...])

    vm = pl.BlockSpec(memory_space=pltpu.VMEM)
    shape = jax.ShapeDtypeStruct(part.shape, F32)
    return pl.pallas_call(
        body, name="small_allreduce_adamw", in_specs=[vm] * 4, out_specs=[vm] * 4, out_shape=[shape] * 4,
        scratch_shapes=[pltpu.VMEM((8,) + part.shape, F32), pltpu.SemaphoreType.DMA((7,)),
                        pltpu.SemaphoreType.DMA((7,))],
        compiler_params=pltpu.CompilerParams(has_side_effects=True),
    )(part, w, m, v)


def _rows_of(a):
    flat = a.reshape(-1)
    pad = (-flat.shape[0]) % D
    if pad:
        flat = jnp.concatenate([flat, jnp.zeros((pad,), flat.dtype)])
    return flat.reshape(-1, D)


SLAB_PARTS = (("w_in", (D, D_IN // N_CHIPS)), ("w_out", (D // N_CHIPS, D)), ("w_ffn_gate", (D, D_FF // N_CHIPS)),
              ("w_ffn_up", (D, D_FF // N_CHIPS)), ("w_ffn_down", (D_FF // N_CHIPS, D)),
              ("meta_tokens", (N_META, D // N_CHIPS)), ("conv_w", (CONV_W, C_CONV // N_CHIPS)),
              ("gla_w_gate2", (RANK, GLA_K // N_CHIPS)))


def _pack_slab(parts):
    rows = [_rows_of(parts[name].reshape(shape)) for name, shape in SLAB_PARTS]
    used = sum(r.shape[0] for r in rows)
    rows.append(jnp.zeros((SLAB_ROWS - used, D), F32))
    return jnp.concatenate(rows, axis=0)


def _unpack_slab(slab, lead):
    out, r0 = {}, 0
    for name, shape in SLAB_PARTS:
        size = shape[0] * shape[1]
        nrows = -(-size // D)
        out[name] = slab[r0:r0 + nrows].reshape(-1)[:size].reshape(lead[name] + shape)
        r0 += nrows
    return out


SMALL_PARTS = (("norm_mix_g", 0, 0, D), ("norm_ffn_g", 1, 0, D), ("norm_final_g", 2, 0, D),
               ("conv_b", 3, 0, C_CONV), ("conv_ln_g", 3, C_CONV, C_CONV), ("conv_ln_b", 4, 0, C_CONV),
               ("gla_gate_b", 4, C_CONV, GLA_K), ("gla_norm_g", 4, C_CONV + GLA_K, DV))


def _pack_small(parts):
    slab = jnp.zeros((SMALL_ROWS, D), F32)
    for name, row, col, size in SMALL_PARTS:
        slab = lax.dynamic_update_slice(slab, parts[name].reshape(1, size).astype(F32), (row, col))
    return slab


def _unpack_small(slab, shapes):
    return {name: slab[row, col:col + size].reshape(shapes[name]) for name, row, col, size in SMALL_PARTS}


def _column_block(full, j, width):
    return lax.dynamic_slice_in_dim(full, j * width, width, axis=1)


def _local_step(x, target, w):
    n_ex, seq, _ = x.shape
    lp = HEAD_ROWS + seq
    t = n_ex * lp
    meta = jnp.broadcast_to(w["meta_tokens"][None], (n_ex, N_META, D))
    h0 = jnp.concatenate([jnp.zeros((n_ex, PAD_ROWS, D), F32), meta, x], axis=1).reshape(t, D)
    tgt = jnp.concatenate([jnp.zeros((n_ex, HEAD_ROWS, D), F32), target], axis=1).reshape(t, D)
    row_mask = jnp.concatenate([jnp.zeros((n_ex, HEAD_ROWS, 1), F32), jnp.ones((n_ex, seq, 1), F32)],
                               axis=1).reshape(t, 1)

    u, hn = _in_proj(h0, w["norm_mix_g"], w["w_in"])
    yc, y_conv = _conv_fwd(u, w["conv_w"], w["conv_b"], w["conv_ln_g"], w["conv_ln_b"], n_ex, lp)
    y_gla, states = _gla_fwd(u, w["gla_w_gate2"], w["gla_gate_b"], w["gla_norm_g"], n_ex, lp)
    h1, hn2, gate, up, act = _mix_out_ffn_up(h0, y_conv, y_gla, w["w_out"], w["norm_ffn_g"],
                                             w["w_ffn_gate"], w["w_ffn_up"])
    dh2, loss, d_final_g = _ffn_down_loss(act, w["w_ffn_down"], h1, tgt, w["norm_final_g"], row_mask)

    dgate, dup, dh1, dycat, d_ffn_g = _ffn_bwd(dh2, gate, up, h1, w["w_ffn_down"], w["w_ffn_gate"],
                                                w["w_ffn_up"], w["w_out"], w["norm_ffn_g"])
    du_conv, d_conv_w, d_conv_b, d_ln_g, d_ln_b = _conv_bwd(dycat, yc, u, w["conv_w"], w["conv_ln_g"],
                                                            w["conv_ln_b"], n_ex, lp)
    du_gla, d_w2, d_gate_b, d_norm_g = _gla_bwd(dycat, u, states, w["gla_w_gate2"], w["gla_gate_b"],
                                                w["gla_norm_g"], n_ex, lp)
    dh0, d_mix_g = _in_proj_bwd(du_conv, du_gla, w["w_in"][:, :2 * C_CONV], w["w_in"][:, 2 * C_CONV:],
                                h0, dh1, w["norm_mix_g"])

    d_w_in = jnp.concatenate([_wgrad(hn, du_conv, "wgrad_in_conv"), _wgrad(hn, du_gla, "wgrad_in_gla")],
                             axis=1)[:, :D_IN]
    d_w_out = jnp.concatenate([_wgrad(y_conv, dh1, "wgrad_out_conv"), _wgrad(y_gla, dh1, "wgrad_out_gla")], axis=0)
    dh0 = dh0.reshape(n_ex, lp, D)
    grads = {
        "w_in": d_w_in, "w_out": d_w_out,
        "w_ffn_gate": _wgrad(hn2, dgate, "wgrad_gate"), "w_ffn_up": _wgrad(hn2, dup, "wgrad_up"),
        "w_ffn_down": _wgrad(act, dh2, "wgrad_down"),
        "meta_tokens": jnp.sum(dh0[:, PAD_ROWS:HEAD_ROWS], axis=0),
        "conv_w": d_conv_w[:CONV_W], "gla_w_gate2": d_w2[:RANK],
        "norm_mix_g": d_mix_g, "norm_ffn_g": d_ffn_g, "norm_final_g": d_final_g,
        "conv_b": d_conv_b, "conv_ln_g": d_ln_g, "conv_ln_b": d_ln_b,
        "gla_gate_b": d_gate_b, "gla_norm_g": d_norm_g,
    }
    return loss[0, 0], dh0[:, HEAD_ROWS:], grads


WEIGHT_NAMES = ("meta_tokens", "norm_mix_g", "w_in", "conv_w", "conv_b", "conv_ln_g", "conv_ln_b", "gla_w_gate2",
                "gla_gate_b", "gla_norm_g", "w_out", "norm_ffn_g", "w_ffn_gate", "w_ffn_up", "w_ffn_down",
                "norm_final_g")
MATMUL_WEIGHTS = ("w_in", "w_out", "w_ffn_gate", "w_ffn_up", "w_ffn_down")
ROW_SHARDED = ("w_out", "w_ffn_down")


def _full_weights(ws):
    names = [name for name, _ in SLAB_PARTS]
    shards = [ws[name].reshape(shape).astype(BF16 if name in MATMUL_WEIGHTS else F32) for name, shape in SLAB_PARTS]
    gathered = dict(zip(names, _gather_shards(shards)))
    cols = lambda a: jnp.concatenate([a[j] for j in range(N_CHIPS)], axis=1)
    full = {name: ws[name].reshape(1, -1) for name, _, _, _ in SMALL_PARTS}
    w_in = cols(gathered["w_in"])
    full["w_in"] = jnp.concatenate([w_in, jnp.zeros((D, D_IN_PAD - D_IN), BF16)], axis=1)
    full["w_out"] = gathered["w_out"].reshape(D, D)
    full["w_ffn_gate"] = cols(gathered["w_ffn_gate"])
    full["w_ffn_up"] = cols(gathered["w_ffn_up"])
    full["w_ffn_down"] = gathered["w_ffn_down"].reshape(D_FF, D)
    full["meta_tokens"] = cols(gathered["meta_tokens"])
    conv_w = cols(gathered["conv_w"])
    full["conv_w"] = jnp.concatenate([conv_w, jnp.zeros((32 - CONV_W, C_CONV), F32)], axis=0)
    w2 = cols(gathered["gla_w_gate2"])
    full["gla_w_gate2"] = jnp.concatenate([w2, jnp.zeros((128 - RANK, GLA_K), F32)], axis=0).astype(BF16)
    return full


def _grad_slabs(grads):
    slabs = []
    for j in range(N_CHIPS):
        parts = {}
        for name, shape in SLAB_PARTS:
            g = grads[name]
            if name in ROW_SHARDED:
                parts[name] = g[j * shape[0]:(j + 1) * shape[0]]
            else:
                parts[name] = g[:, j * shape[1]:(j + 1) * shape[1]]
        slabs.append(_pack_slab(parts))
    g = jnp.stack(slabs).reshape(N_CHIPS, 2, HALF_ROWS, D)
    return jnp.transpose(g, (1, 0, 2, 3))


def kernel(x, meta_tokens, norm_mix_g, w_in, conv_w, conv_b, conv_ln_g, conv_ln_b, gla_w_gate2, gla_gate_b, gla_norm_g, w_out, norm_ffn_g, w_ffn_gate, w_ffn_up, w_ffn_down, norm_final_g, loss_target, m_meta_tokens, m_norm_mix_g, m_w_in, m_conv_w, m_conv_b, m_conv_ln_g, m_conv_ln_b, m_gla_w_gate2, m_gla_gate_b, m_gla_norm_g, m_w_out, m_norm_ffn_g, m_w_ffn_gate, m_w_ffn_up, m_w_ffn_down, m_norm_final_g, v_meta_tokens, v_norm_mix_g, v_w_in, v_conv_w, v_conv_b, v_conv_ln_g, v_conv_ln_b, v_gla_w_gate2, v_gla_gate_b, v_gla_norm_g, v_w_out, v_norm_ffn_g, v_w_ffn_gate, v_w_ffn_up, v_w_ffn_down, v_norm_final_g):
    ws = dict(zip(WEIGHT_NAMES, (meta_tokens, norm_mix_g, w_in, conv_w, conv_b, conv_ln_g, conv_ln_b, gla_w_gate2,
                                 gla_gate_b, gla_norm_g, w_out, norm_ffn_g, w_ffn_gate, w_ffn_up, w_ffn_down,
                                 norm_final_g)))
    ms = dict(zip(WEIGHT_NAMES, (m_meta_tokens, m_norm_mix_g, m_w_in, m_conv_w, m_conv_b, m_conv_ln_g, m_conv_ln_b,
                                 m_gla_w_gate2, m_gla_gate_b, m_gla_norm_g, m_w_out, m_norm_ffn_g, m_w_ffn_gate,
                                 m_w_ffn_up, m_w_ffn_down, m_norm_final_g)))
    vs = dict(zip(WEIGHT_NAMES, (v_meta_tokens, v_norm_mix_g, v_w_in, v_conv_w, v_conv_b, v_conv_ln_g, v_conv_ln_b,
                                 v_gla_w_gate2, v_gla_gate_b, v_gla_norm_g, v_w_out, v_norm_ffn_g, v_w_ffn_gate,
                                 v_w_ffn_up, v_w_ffn_down, v_norm_final_g)))
    c = lax.axis_index("c")

    full = _full_weights(ws)
    loss, grad_x, grads = _local_step(x, loss_target, full)
    loss = lax.psum(loss, ("x", "y", "c"))

    g2 = _grad_slabs(grads)
    from_sibling = _send_half_to_sibling(g2)
    chip_sums = _add_own_half(g2, from_sibling, c)
    reduced_half = _sum_chips(_exchange_chip_sums(chip_sums))
    g_slab = _share_with_sibling(reduced_half).reshape(SLAB_ROWS, D)
    d_slab, m_slab, v_slab = _adamw_slab(g_slab, _pack_slab(ws), _pack_slab(ms), _pack_slab(vs))
    lead = {name: ws[name].shape[:ws[name].ndim - 2] for name, _ in SLAB_PARTS}
    out = {"grad": _unpack_slab(g_slab, lead), "delta": _unpack_slab(d_slab, lead),
           "new_m": _unpack_slab(m_slab, lead), "new_v": _unpack_slab(v_slab, lead)}

    small_shapes = {name: ws[name].shape for name, _, _, _ in SMALL_PARTS}
    g_s, d_s, m_s, v_s = _allreduce_small_adamw(_pack_small(grads), _pack_small(ws), _pack_small(ms), _pack_small(vs))
    for kind, slab in (("grad", g_s), ("delta", d_s), ("new_m", m_s), ("new_v", v_s)):
        out[kind].update(_unpack_small(slab, small_shapes))

    return (loss, grad_x, *[out[kind][name] for kind in ("grad", "delta", "new_m", "new_v") for name in WEIGHT_NAMES])
```

```python
import functools

import jax
import jax.numpy as jnp
from jax import lax
from jax.experimental import pallas as pl
from jax.experimental.pallas import tpu as pltpu

F32 = jnp.float32
BF16 = jnp.bfloat16
MESH = pl.DeviceIdType.MESH

D = 1024
N_META = 16
C_CONV = 512
CONV_W = 31
GLA_K = 256
GLA_V = 512
N_HEADS = 4
DK = 64
DV = 128
RANK = 16
CHUNK = 64
PAD_ROWS = CHUNK - N_META
HEAD_ROWS = CHUNK
D_IN = 2576
D_IN_PAD = 2688
D_GLA_IN = D_IN_PAD - 2 * C_CONV
D_FF = 2816
RMS_EPS = 1e-6
LN_EPS = 1e-5
GATE_TAU = 16.0
N_CHIPS = 4

ADAM_LR = 0.001
ADAM_B1 = 0.9
ADAM_B2 = 0.999
ADAM_EPS = 1e-08
ADAM_WD = 0.01
ADAM_STEP = 10

V7X_VMEM_BYTES = 64 * 1024 * 1024
VMEM_LIMIT = V7X_VMEM_BYTES - 8 * 1024 * 1024

SLAB_ROWS = 3072
HALF_ROWS = SLAB_ROWS // 2
SMALL_ROWS = 8


def _dot(a, b):
    return jnp.dot(a, b, preferred_element_type=F32)


def _dot_nt(a, b):
    return lax.dot_general(a, b, (((1,), (1,)), ((), ())), preferred_element_type=F32)


def _dot_tn(a, b):
    return lax.dot_general(a, b, (((0,), (0,)), ((), ())), preferred_element_type=F32)


def _sigmoid(x):
    return 1.0 / (1.0 + jnp.exp(-x))


def _const_spec(shape):
    return pl.BlockSpec(shape, lambda *_: (0,) * len(shape), pipeline_mode=pl.Buffered(1))


def _acc_spec(shape):
    return pl.BlockSpec(shape, lambda *_: (0,) * len(shape))


def _params(n_axes):
    return pltpu.CompilerParams(dimension_semantics=("arbitrary",) * n_axes, vmem_limit_bytes=VMEM_LIMIT)


def _row_tile(t, want):
    for r in (want, 384, 192, 128, 64):
        if r <= want and t % r == 0:
            return r
    raise ValueError(f"no row tile for {t}")


def _in_proj(h0, g_mix, w_in):
    t = h0.shape[0]
    r = _row_tile(t, 384)

    def body(h_ref, g_ref, w_ref, u_ref, hn_ref):
        h = h_ref[...]
        rstd = lax.rsqrt(jnp.mean(h * h, axis=-1, keepdims=True) + RMS_EPS)
        hn = (h * rstd * g_ref[...]).astype(BF16)
        hn_ref[...] = hn
        u_ref[...] = _dot(hn, w_ref[...])

    return pl.pallas_call(
        body, name="in_proj", grid=(t // r,),
        in_specs=[pl.BlockSpec((r, D), lambda i: (i, 0)), _const_spec((1, D)), _const_spec((D, D_IN_PAD))],
        out_specs=[pl.BlockSpec((r, D_IN_PAD), lambda i: (i, 0)), pl.BlockSpec((r, D), lambda i: (i, 0))],
        out_shape=[jax.ShapeDtypeStruct((t, D_IN_PAD), F32), jax.ShapeDtypeStruct((t, D), BF16)],
        compiler_params=_params(1),
    )(h0, g_mix, w_in)


CONV_TILE = 192
CONV_SUB = 32
CONV_LEAD = CONV_SUB - (CONV_W - 1)


def _conv_fwd(u, conv_w, conv_b, ln_g, ln_b, n_ex, lp):
    r = CONV_TILE
    nt = lp // r
    hb = r // CONV_SUB

    def body(cur_ref, prev_ref, w_ref, b_ref, lg_ref, lb_ref, yc_ref, y_ref, glu):
        i = pl.program_id(1)
        cur = cur_ref[...]
        glu[CONV_SUB:CONV_SUB + r, :] = cur[:, :C_CONV] * _sigmoid(cur[:, C_CONV:])
        pv = prev_ref[...]
        halo = pv[:, :C_CONV] * _sigmoid(pv[:, C_CONV:])
        glu[0:CONV_SUB, :] = jnp.where(i > 0, halo, 0.0)
        w = w_ref[...]
        for j in range(r // CONV_SUB):
            r0 = j * CONV_SUB
            acc = jnp.zeros((CONV_SUB, C_CONV), F32) + b_ref[...]
            for k in range(CONV_W):
                acc = acc + w[k:k + 1, :] * glu[r0 + CONV_LEAD + k:r0 + CONV_LEAD + k + CONV_SUB, :]
            mu = jnp.mean(acc, axis=-1, keepdims=True)
            cen = acc - mu
            var = jnp.mean(cen * cen, axis=-1, keepdims=True)
            out = cen * lax.rsqrt(var + LN_EPS) * lg_ref[...] + lb_ref[...]
            y = out * _sigmoid(out)
            row = i * r + r0 + lax.broadcasted_iota(jnp.int32, (CONV_SUB, 1), 0)
            y = jnp.where(row >= PAD_ROWS, y, 0.0)
            yc_ref[r0:r0 + CONV_SUB, :] = acc
            y_ref[r0:r0 + CONV_SUB, :] = y.astype(BF16)

    t = n_ex * lp
    return pl.pallas_call(
        body, name="conv_fwd", grid=(n_ex, nt),
        in_specs=[pl.BlockSpec((r, 2 * C_CONV), lambda b, i: (b * nt + i, 0)),
                  pl.BlockSpec((CONV_SUB, 2 * C_CONV), lambda b, i: (jnp.maximum((b * nt + i) * hb - 1, 0), 0)),
                  _const_spec((32, C_CONV)), _const_spec((1, C_CONV)), _const_spec((1, C_CONV)), _const_spec((1, C_CONV))],
        out_specs=[pl.BlockSpec((r, C_CONV), lambda b, i: (b * nt + i, 0)),
                   pl.BlockSpec((r, C_CONV), lambda b, i: (b * nt + i, 0))],
        out_shape=[jax.ShapeDtypeStruct((t, C_CONV), F32), jax.ShapeDtypeStruct((t, C_CONV), BF16)],
        scratch_shapes=[pltpu.VMEM((r + CONV_SUB, C_CONV), F32)],
        compiler_params=_params(2),
    )(u, u, conv_w, conv_b, ln_g, ln_b)


def _gla_gates(lr, w2, gb, first_chunk):
    z = _dot(lr.astype(BF16), w2) + gb
    a = (jnp.minimum(z, 0.0) - jnp.log(1.0 + jnp.exp(-jnp.abs(z)))) * (1.0 / GATE_TAU)
    row = lax.broadcasted_iota(jnp.int32, (CHUNK, 1), 0)
    live = jnp.logical_or(jnp.logical_not(first_chunk), row >= PAD_ROWS)
    return z, jnp.where(live, a, 0.0), live


def _tri(lower):
    i = lax.broadcasted_iota(jnp.int32, (CHUNK, CHUNK), 0)
    j = lax.broadcasted_iota(jnp.int32, (CHUNK, CHUNK), 1)
    return (i >= j) if lower else (i <= j)


def _gla_fwd(u, w2, gb, ng, n_ex, lp):
    nc = lp // CHUNK
    t = n_ex * lp

    def body(qk_ref, v_ref, g_ref, lr_ref, w2_ref, gb_ref, ng_ref, y_ref, st_ref, state):
        n = pl.program_id(1)

        @pl.when(n == 0)
        def _():
            state[...] = jnp.zeros_like(state)

        st = state[...]
        st_ref[...] = st
        qk = qk_ref[...]
        q, k = qk[:, :GLA_K], qk[:, GLA_K:]
        _, a, _ = _gla_gates(lr_ref[...], w2_ref[...], gb_ref[...], n == 0)
        causal = _tri(True)
        b = jnp.dot(causal.astype(F32), a, preferred_element_type=F32, precision=lax.Precision.HIGHEST)
        bl = b[CHUNK - 1:CHUNK, :]
        q_in = (q * (DK ** -0.5) * jnp.exp(b)).astype(BF16)
        k_in = (k * jnp.exp(-b)).astype(BF16)
        k_dec = (k * jnp.exp(bl - b)).astype(BF16)
        decay = jnp.exp(bl)
        v = v_ref[...]
        g = g_ref[...]
        st_b = st.astype(BF16)
        ys, new = [], []
        for h in range(N_HEADS):
            ks = slice(h * DK, (h + 1) * DK)
            vs = slice(h * DV, (h + 1) * DV)
            vh = v[:, vs].astype(BF16)
            s = jnp.where(causal, _dot_nt(q_in[:, ks], k_in[:, ks]), 0.0)
            o = _dot(s.astype(BF16), vh) + _dot_nt(q_in[:, ks], st_b[:, ks])
            new.append(decay[:, ks] * st[:, ks] + _dot_tn(vh, k_dec[:, ks]))
            rstd = lax.rsqrt(jnp.mean(o * o, axis=-1, keepdims=True) + RMS_EPS)
            gh = g[:, vs]
            ys.append(o * rstd * ng_ref[...] * (gh * _sigmoid(gh)))
        state[...] = jnp.concatenate(new, axis=1)
        y_ref[...] = jnp.concatenate(ys, axis=1).astype(BF16)

    blk = lambda w, col: pl.BlockSpec((CHUNK, w), lambda b, n: (b * nc + n, col))
    return pl.pallas_call(
        body, name="gla_fwd", grid=(n_ex, nc),
        in_specs=[blk(2 * GLA_K, 2), blk(GLA_V, 3), blk(GLA_V, 4), blk(128, 20),
                  _const_spec((128, GLA_K)), _const_spec((1, GLA_K)), _const_spec((1, DV))],
        out_specs=[pl.BlockSpec((CHUNK, GLA_V), lambda b, n: (b * nc + n, 0)),
                   pl.BlockSpec((DV, GLA_K), lambda b, n: (b * nc + n, 0))],
        out_shape=[jax.ShapeDtypeStruct((t, GLA_V), BF16), jax.ShapeDtypeStruct((n_ex * nc * DV, GLA_K), F32)],
        scratch_shapes=[pltpu.VMEM((DV, GLA_K), F32)],
        compiler_params=_params(2),
    )(u, u, u, u, w2, gb, ng)


FFN_TILE = 192


def _mix_out_ffn_up(h0, y_conv, y_gla, w_out, g_ffn, w_gate_t, w_up_t):
    t = h0.shape[0]
    r = _row_tile(t, FFN_TILE)

    def body(h0_ref, yc_ref, yg_ref, wo_ref, g_ref, wg_ref, wu_ref, h1_ref, hn_ref, gate_ref, up_ref, act_ref):
        h1 = h0_ref[...] + _dot(yc_ref[...], wo_ref[0:C_CONV, :]) + _dot(yg_ref[...], wo_ref[C_CONV:D, :])
        h1_ref[...] = h1
        rstd = lax.rsqrt(jnp.mean(h1 * h1, axis=-1, keepdims=True) + RMS_EPS)
        hn = (h1 * rstd * g_ref[...]).astype(BF16)
        hn_ref[...] = hn
        gate = _dot_nt(hn, wg_ref[...])
        up = _dot_nt(hn, wu_ref[...])
        gate_ref[...] = gate
        up_ref[...] = up
        act_ref[...] = (gate * _sigmoid(gate) * up).astype(BF16)

    rows = lambda w: pl.BlockSpec((r, w), lambda i: (i, 0))
    return pl.pallas_call(
        body, name="mix_out_ffn_up", grid=(t // r,),
        in_specs=[rows(D), rows(C_CONV), rows(GLA_V), _const_spec((D, D)), _const_spec((1, D)),
                  _const_spec((D_FF, D)), _const_spec((D_FF, D))],
        out_specs=[rows(D), rows(D), rows(D_FF), rows(D_FF), rows(D_FF)],
        out_shape=[jax.ShapeDtypeStruct((t, D), F32), jax.ShapeDtypeStruct((t, D), BF16),
                   jax.ShapeDtypeStruct((t, D_FF), F32), jax.ShapeDtypeStruct((t, D_FF), F32),
                   jax.ShapeDtypeStruct((t, D_FF), BF16)],
        compiler_params=_params(1),
    )(h0, y_conv, y_gla, w_out, g_ffn, w_gate_t, w_up_t)


def _ffn_down_loss(act, w_down, h1, target, g_final, row_mask):
    t = h1.shape[0]
    r = _row_tile(t, 384)

    def body(act_ref, wd_ref, h1_ref, tgt_ref, gf_ref, mask_ref, dh2_ref, loss_ref, dgf_ref):
        @pl.when(pl.program_id(0) == 0)
        def _():
            loss_ref[...] = jnp.zeros_like(loss_ref)
            dgf_ref[...] = jnp.zeros_like(dgf_ref)

        h2 = h1_ref[...] + _dot(act_ref[...], wd_ref[...])
        rstd = lax.rsqrt(jnp.mean(h2 * h2, axis=-1, keepdims=True) + RMS_EPS)
        nrm = h2 * rstd
        gf = gf_ref[...]
        err = (nrm * gf - tgt_ref[...]) * mask_ref[...]
        loss_ref[...] += jnp.sum(err * err) * (0.5 / D)
        dy = err * (1.0 / D)
        dgf_ref[...] += jnp.sum(dy * nrm, axis=0, keepdims=True)
        dn = dy * gf
        dh2_ref[...] = rstd * (dn - nrm * jnp.mean(dn * nrm, axis=-1, keepdims=True))

    rows = lambda w: pl.BlockSpec((r, w), lambda i: (i, 0))
    return pl.pallas_call(
        body, name="ffn_down_loss", grid=(t // r,),
        in_specs=[rows(D_FF), _const_spec((D_FF, D)), rows(D), rows(D), _const_spec((1, D)), rows(1)],
        out_specs=[rows(D), _acc_spec((1, 128)), _acc_spec((1, D))],
        out_shape=[jax.ShapeDtypeStruct((t, D), F32), jax.ShapeDtypeStruct((1, 128), F32),
                   jax.ShapeDtypeStruct((1, D), F32)],
        compiler_params=_params(1),
    )(act, w_down, h1, target, g_final, row_mask)


def _ffn_bwd(dh2, gate, up, h1, w_down, w_gate_t, w_up_t, w_out, g_ffn):
    t = h1.shape[0]
    r = _row_tile(t, FFN_TILE)

    def body(dh2_ref, gate_ref, up_ref, h1_ref, wd_ref, wg_ref, wu_ref, wo_ref, g_ref,
             dgate_ref, dup_ref, dh1_ref, dycat_ref, dg_ref):
        @pl.when(pl.program_id(0) == 0)
        def _():
            dg_ref[...] = jnp.zeros_like(dg_ref)

        dh2 = dh2_ref[...]
        dact = _dot_nt(dh2.astype(BF16), wd_ref[...])
        gate = gate_ref[...]
        sg = _sigmoid(gate)
        dgate = (dact * up_ref[...] * (sg * (1.0 + gate * (1.0 - sg)))).astype(BF16)
        dup = (dact * (gate * sg)).astype(BF16)
        dgate_ref[...] = dgate
        dup_ref[...] = dup
        dhn = _dot(dgate, wg_ref[...]) + _dot(dup, wu_ref[...])
        h1 = h1_ref[...]
        rstd = lax.rsqrt(jnp.mean(h1 * h1, axis=-1, keepdims=True) + RMS_EPS)
        nrm = h1 * rstd
        dg_ref[...] += jnp.sum(dhn * nrm, axis=0, keepdims=True)
        dn = dhn * g_ref[...]
        dh1 = dh2 + rstd * (dn - nrm * jnp.mean(dn * nrm, axis=-1, keepdims=True))
        dh1_ref[...] = dh1
        dycat_ref[...] = _dot_nt(dh1.astype(BF16), wo_ref[...])

    rows = lambda w: pl.BlockSpec((r, w), lambda i: (i, 0))
    return pl.pallas_call(
        body, name="ffn_bwd", grid=(t // r,),
        in_specs=[rows(D), rows(D_FF), rows(D_FF), rows(D), _const_spec((D_FF, D)), _const_spec((D_FF, D)),
                  _const_spec((D_FF, D)), _const_spec((D, D)), _const_spec((1, D))],
        out_specs=[rows(D_FF), rows(D_FF), rows(D), rows(D), _acc_spec((1, D))],
        out_shape=[jax.ShapeDtypeStruct((t, D_FF), BF16), jax.ShapeDtypeStruct((t, D_FF), BF16),
                   jax.ShapeDtypeStruct((t, D), F32), jax.ShapeDtypeStruct((t, D), F32),
                   jax.ShapeDtypeStruct((1, D), F32)],
        compiler_params=_params(1),
    )(dh2, gate, up, h1, w_down, w_gate_t, w_up_t, w_out, g_ffn)


def _conv_bwd(dycat, yc, u, conv_w, ln_g, ln_b, n_ex, lp):
    r = CONV_TILE
    nt = lp // r
    hb = r // CONV_SUB
    nsub = r // CONV_SUB

    def ln_bwd(dy, yc_rows, live, lg, lb):
        mu = jnp.mean(yc_rows, axis=-1, keepdims=True)
        cen = yc_rows - mu
        rs = lax.rsqrt(jnp.mean(cen * cen, axis=-1, keepdims=True) + LN_EPS)
        yn = cen * rs
        out = yn * lg + lb
        so = _sigmoid(out)
        dout = jnp.where(live, dy * (so * (1.0 + out * (1.0 - so))), 0.0)
        dyn = dout * lg
        dyc = rs * (dyn - jnp.mean(dyn, axis=-1, keepdims=True) - yn * jnp.mean(dyn * yn, axis=-1, keepdims=True))
        return dyc, dout, yn

    def body(dy_ref, dyn_ref, yc_ref, ycn_ref, cur_ref, prev_ref, w_ref, lg_ref, lb_ref,
             du_ref, dw_ref, db_ref, dlg_ref, dlb_ref, glu, dycs, dwacc):
        b = pl.program_id(0)
        i = pl.program_id(1)
        first = jnp.logical_and(b == 0, i == 0)

        @pl.when(first)
        def _():
            dwacc[...] = jnp.zeros_like(dwacc)
            db_ref[...] = jnp.zeros_like(db_ref)
            dlg_ref[...] = jnp.zeros_like(dlg_ref)
            dlb_ref[...] = jnp.zeros_like(dlb_ref)

        lg, lb = lg_ref[...], lb_ref[...]
        cur = cur_ref[...]
        sig = _sigmoid(cur[:, C_CONV:])
        glu[CONV_SUB:CONV_SUB + r, :] = cur[:, :C_CONV] * sig
        pv = prev_ref[...]
        glu[0:CONV_SUB, :] = jnp.where(i > 0, pv[:, :C_CONV] * _sigmoid(pv[:, C_CONV:]), 0.0)

        row = i * r + lax.broadcasted_iota(jnp.int32, (r, 1), 0)
        dyc, dout, yn = ln_bwd(dy_ref[...], yc_ref[...], row >= PAD_ROWS, lg, lb)
        dycs[0:r, :] = dyc
        dycn, _, _ = ln_bwd(dyn_ref[...], ycn_ref[...], i < nt - 1, lg, lb)
        dycs[r:r + CONV_SUB, :] = dycn
        db_ref[...] += jnp.sum(dyc, axis=0, keepdims=True)
        dlg_ref[...] += jnp.sum(dout * yn, axis=0, keepdims=True)
        dlb_ref[...] += jnp.sum(dout, axis=0, keepdims=True)

        w = w_ref[...]
        for j in range(nsub):
            r0 = j * CONV_SUB
            dblk = dycs[r0:r0 + CONV_SUB, :]
            dglu = jnp.zeros((CONV_SUB, C_CONV), F32)
            for k in range(CONV_W):
                dglu = dglu + w[k:k + 1, :] * dycs[r0 + (CONV_W - 1) - k:r0 + (CONV_W - 1) - k + CONV_SUB, :]
                prod = dblk * glu[r0 + CONV_LEAD + k:r0 + CONV_LEAD + k + CONV_SUB, :]
                dwacc[k] += prod.reshape(CONV_SUB // 8, 8, C_CONV).sum(axis=0)
            sg = sig[r0:r0 + CONV_SUB, :]
            cv = cur[r0:r0 + CONV_SUB, :C_CONV]
            du_ref[r0:r0 + CONV_SUB, :C_CONV] = (dglu * sg).astype(BF16)
            du_ref[r0:r0 + CONV_SUB, C_CONV:] = (dglu * cv * sg * (1.0 - sg)).astype(BF16)

        @pl.when(jnp.logical_and(b == n_ex - 1, i == nt - 1))
        def _():
            dw_ref[...] = jnp.sum(dwacc[...], axis=1)

    t = n_ex * lp
    cur_rows = lambda w, col: pl.BlockSpec((r, w), lambda b, i: (b * nt + i, col))
    nxt_rows = lambda w, col: pl.BlockSpec(
        (CONV_SUB, w), lambda b, i: (jnp.minimum((b * nt + i + 1) * hb, n_ex * nt * hb - 1), col))
    return pl.pallas_call(
        body, name="conv_bwd", grid=(n_ex, nt),
        in_specs=[cur_rows(C_CONV, 0), nxt_rows(C_CONV, 0), cur_rows(C_CONV, 0), nxt_rows(C_CONV, 0),
                  cur_rows(2 * C_CONV, 0),
                  pl.BlockSpec((CONV_SUB, 2 * C_CONV), lambda b, i: (jnp.maximum((b * nt + i) * hb - 1, 0), 0)),
                  _const_spec((32, C_CONV)), _const_spec((1, C_CONV)), _const_spec((1, C_CONV))],
        out_specs=[cur_rows(2 * C_CONV, 0), _acc_spec((32, C_CONV)), _acc_spec((1, C_CONV)),
                   _acc_spec((1, C_CONV)), _acc_spec((1, C_CONV))],
        out_shape=[jax.ShapeDtypeStruct((t, 2 * C_CONV), BF16), jax.ShapeDtypeStruct((32, C_CONV), F32),
                   jax.ShapeDtypeStruct((1, C_CONV), F32), jax.ShapeDtypeStruct((1, C_CONV), F32),
                   jax.ShapeDtypeStruct((1, C_CONV), F32)],
        scratch_shapes=[pltpu.VMEM((r + CONV_SUB, C_CONV), F32), pltpu.VMEM((r + CONV_SUB, C_CONV), F32),
                        pltpu.VMEM((32, 8, C_CONV), F32)],
        compiler_params=_params(2),
    )(dycat, dycat, yc, yc, u, u, conv_w, ln_g, ln_b)


def _gla_bwd(dycat, u, states, w2, gb, ng, n_ex, lp):
    nc = lp // CHUNK
    t = n_ex * lp

    def body(dy_ref, qk_ref, v_ref, g_ref, lr_ref, st_ref, w2_ref, gb_ref, ng_ref,
             du_ref, dw2_ref, dgb_ref, dng_ref, dstate):
        bi = pl.program_id(0)
        n = pl.program_id(1)
        chunk = nc - 1 - n

        @pl.when(jnp.logical_and(bi == 0, n == 0))
        def _():
            dw2_ref[...] = jnp.zeros_like(dw2_ref)
            dgb_ref[...] = jnp.zeros_like(dgb_ref)
            dng_ref[...] = jnp.zeros_like(dng_ref)

        @pl.when(n == 0)
        def _():
            dstate[...] = jnp.zeros_like(dstate)

        qk = qk_ref[...]
        q, k = qk[:, :GLA_K], qk[:, GLA_K:]
        lr = lr_ref[...]
        z, a, live = _gla_gates(lr, w2_ref[...], gb_ref[...], chunk == 0)
        causal = _tri(True)
        b = jnp.dot(causal.astype(F32), a, preferred_element_type=F32, precision=lax.Precision.HIGHEST)
        bl = b[CHUNK - 1:CHUNK, :]
        e_pos, e_neg, e_dec = jnp.exp(b), jnp.exp(-b), jnp.exp(bl - b)
        q_f = q * (DK ** -0.5) * e_pos
        k_f = k * e_neg
        kd_f = k * e_dec
        q_in, k_in, k_dec = q_f.astype(BF16), k_f.astype(BF16), kd_f.astype(BF16)
        decay = jnp.exp(bl)
        v = v_ref[...]
        g = g_ref[...]
        dy = dy_ref[...]
        ngv = ng_ref[...]
        st = st_ref[...]
        st_b = st.astype(BF16)
        dst = dstate[...]
        dst_b = dst.astype(BF16)
        dqs, dks, dvs, dgs, dbs, dbls, new_dst = [], [], [], [], [], [], []
        dng = jnp.zeros((1, DV), F32)
        for h in range(N_HEADS):
            ks = slice(h * DK, (h + 1) * DK)
            vs = slice(h * DV, (h + 1) * DV)
            qh, kh, kdh = q_in[:, ks], k_in[:, ks], k_dec[:, ks]
            vh = v[:, vs].astype(BF16)
            s = jnp.where(causal, _dot_nt(qh, kh), 0.0).astype(BF16)
            o = _dot(s, vh) + _dot_nt(qh, st_b[:, ks])
            rstd = lax.rsqrt(jnp.mean(o * o, axis=-1, keepdims=True) + RMS_EPS)
            nrm = o * rstd
            gh = g[:, vs]
            sg = _sigmoid(gh)
            dyh = dy[:, vs]
            dgs.append(dyh * nrm * ngv * (sg * (1.0 + gh * (1.0 - sg))))
            dt = dyh * (gh * sg)
            dng = dng + jnp.sum(dt * nrm, axis=0, keepdims=True)
            dn = dt * ngv
            do = (rstd * (dn - nrm * jnp.mean(dn * nrm, axis=-1, keepdims=True))).astype(BF16)
            da = jnp.where(causal, _dot_nt(do, vh), 0.0).astype(BF16)
            dvs.append(_dot_tn(s, do) + _dot_nt(kdh, dst_b[:, ks]))
            dq_in = _dot(da, kh) + _dot(do, st_b[:, ks])
            dk_in = _dot_tn(da, qh)
            dk_dec = _dot(vh, dst_b[:, ks])
            new_dst.append(_dot_tn(do, qh) + decay[:, ks] * dst[:, ks])
            dbls.append(jnp.sum(dk_dec * kd_f[:, ks], axis=0, keepdims=True)
                        + decay[:, ks] * jnp.sum(dst[:, ks] * st[:, ks], axis=0, keepdims=True))
            dqs.append(dq_in * (DK ** -0.5) * e_pos[:, ks])
            dks.append(dk_in * e_neg[:, ks] + dk_dec * e_dec[:, ks])
            dbs.append(dq_in * q_f[:, ks] - dk_in * k_f[:, ks] - dk_dec * kd_f[:, ks])
        dstate[...] = jnp.concatenate(new_dst, axis=1)
        row = lax.broadcasted_iota(jnp.int32, (CHUNK, 1), 0)
        db = jnp.concatenate(dbs, axis=1) + jnp.where(row == CHUNK - 1, jnp.concatenate(dbls, axis=1), 0.0)
        da_log = jnp.dot(_tri(False).astype(F32), db, preferred_element_type=F32, precision=lax.Precision.HIGHEST)
        dz = jnp.where(live, da_log * (1.0 - _sigmoid(z)) * (1.0 / GATE_TAU), 0.0)
        dz_b = dz.astype(BF16)
        du_ref[:, 0:GLA_K] = jnp.concatenate(dqs, axis=1).astype(BF16)
        du_ref[:, GLA_K:2 * GLA_K] = jnp.concatenate(dks, axis=1).astype(BF16)
        du_ref[:, 2 * GLA_K:2 * GLA_K + GLA_V] = jnp.concatenate(dvs, axis=1).astype(BF16)
        du_ref[:, 2 * GLA_K + GLA_V:2 * GLA_K + 2 * GLA_V] = jnp.concatenate(dgs, axis=1).astype(BF16)
        du_ref[:, 2 * GLA_K + 2 * GLA_V:] = _dot_nt(dz_b, w2_ref[...]).astype(BF16)
        dw2_ref[...] += _dot_tn(lr.astype(BF16), dz_b)
        dgb_ref[...] += jnp.sum(dz, axis=0, keepdims=True)
        dng_ref[...] += dng

    rev = lambda w, col: pl.BlockSpec((CHUNK, w), lambda b, n: (b * nc + nc - 1 - n, col))
    return pl.pallas_call(
        body, name="gla_bwd", grid=(n_ex, nc),
        in_specs=[rev(GLA_V, 1), rev(2 * GLA_K, 2), rev(GLA_V, 3), rev(GLA_V, 4), rev(128, 20),
                  pl.BlockSpec((DV, GLA_K), lambda b, n: (b * nc + nc - 1 - n, 0)),
                  _const_spec((128, GLA_K)), _const_spec((1, GLA_K)), _const_spec((1, DV))],
        out_specs=[rev(D_GLA_IN, 0), _acc_spec((128, GLA_K)), _acc_spec((1, GLA_K)), _acc_spec((1, DV))],
        out_shape=[jax.ShapeDtypeStruct((t, D_GLA_IN), BF16), jax.ShapeDtypeStruct((128, GLA_K), F32),
                   jax.ShapeDtypeStruct((1, GLA_K), F32), jax.ShapeDtypeStruct((1, DV), F32)],
        scratch_shapes=[pltpu.VMEM((DV, GLA_K), F32)],
        compiler_params=_params(2),
    )(dycat, u, u, u, u, states, w2, gb, ng)


def _in_proj_bwd(du_conv, du_gla, w_in_conv, w_in_gla, h0, dh1, g_mix):
    t = h0.shape[0]
    r = _row_tile(t, 384)

    def body(dc_ref, dg_ref, wc_ref, wg_ref, h_ref, dh1_ref, g_ref, dh0_ref, dgm_ref):
        @pl.when(pl.program_id(0) == 0)
        def _():
            dgm_ref[...] = jnp.zeros_like(dgm_ref)

        dhn = _dot_nt(dc_ref[...], wc_ref[...]) + _dot_nt(dg_ref[...], wg_ref[...])
        h = h_ref[...]
        rstd = lax.rsqrt(jnp.mean(h * h, axis=-1, keepdims=True) + RMS_EPS)
        nrm = h * rstd
        dgm_ref[...] += jnp.sum(dhn * nrm, axis=0, keepdims=True)
        dn = dhn * g_ref[...]
        dh0_ref[...] = dh1_ref[...] + rstd * (dn - nrm * jnp.mean(dn * nrm, axis=-1, keepdims=True))

    rows = lambda w: pl.BlockSpec((r, w), lambda i: (i, 0))
    return pl.pallas_call(
        body, name="in_proj_bwd", grid=(t // r,),
        in_specs=[rows(2 * C_CONV), rows(D_GLA_IN), _const_spec((D, 2 * C_CONV)), _const_spec((D, D_GLA_IN)),
                  rows(D), rows(D), _const_spec((1, D))],
        out_specs=[rows(D), _acc_spec((1, D))],
        out_shape=[jax.ShapeDtypeStruct((t, D), F32), jax.ShapeDtypeStruct((1, D), F32)],
        compiler_params=_params(1),
    )(du_conv, du_gla, w_in_conv, w_in_gla, h0, dh1, g_mix)


def _wgrad(x, dy, name):
    t, m = x.shape
    n = dy.shape[1]
    tk = _row_tile(t, 384)
    tm = m if m <= D_GLA_IN else m // 2
    tn = n

    def body(x_ref, dy_ref, o_ref):
        @pl.when(pl.program_id(2) == 0)
        def _():
            o_ref[...] = jnp.zeros_like(o_ref)

        o_ref[...] += _dot_tn(x_ref[...].astype(BF16), dy_ref[...].astype(BF16))

    return pl.pallas_call(
        body, name=name, grid=(m // tm, n // tn, t // tk),
        in_specs=[pl.BlockSpec((tk, tm), lambda i, j, k: (k, i)), pl.BlockSpec((tk, tn), lambda i, j, k: (k, j))],
        out_specs=pl.BlockSpec((tm, tn), lambda i, j, k: (i, j)),
        out_shape=jax.ShapeDtypeStruct((m, n), F32),
        compiler_params=_params(3),
    )(x, dy)


def _mesh_pos():
    return lax.axis_index("x"), lax.axis_index("y"), lax.axis_index("c")


def _other_chips(x, y):
    return [(1 - x, y), (x, 1 - y), (1 - x, 1 - y)]


HBM_SPEC = pl.BlockSpec(memory_space=pltpu.HBM)


def _gather_shards(shards):
    n = len(shards)

    def body(*refs):
        ins, outs = refs[:n], refs[n:2 * n]
        send_sems, recv_sems, local_sems = refs[2 * n:]
        x, y, c = _mesh_pos()
        mine = 2 * x + y
        chips = _other_chips(x, y)
        local = [pltpu.make_async_copy(ins[a], outs[a].at[mine], local_sems.at[a]) for a in range(n)]
        for cp in local:
            cp.start()

        def remote(a, k, block):
            px, py = chips[k]
            return pltpu.make_async_remote_copy(
                src_ref=ins[a], dst_ref=outs[a].at[block], send_sem=send_sems.at[3 * a + k],
                recv_sem=recv_sems.at[3 * a + k], device_id=(px, py, c), device_id_type=MESH)

        sends = [remote(a, k, mine) for a in range(n) for k in range(3)]
        for cp in sends:
            cp.start()
        for a in range(n):
            for k, (px, py) in enumerate(chips):
                remote(a, k, 2 * px + py).wait_recv()
        for cp in sends:
            cp.wait_send()
        for cp in local:
            cp.wait()

    return pl.pallas_call(
        body, name="gather_shards",
        in_specs=[HBM_SPEC] * n, out_specs=[HBM_SPEC] * n,
        out_shape=[jax.ShapeDtypeStruct((N_CHIPS,) + s.shape, s.dtype) for s in shards],
        scratch_shapes=[pltpu.SemaphoreType.DMA((3 * n,)), pltpu.SemaphoreType.DMA((3 * n,)),
                        pltpu.SemaphoreType.DMA((n,))],
        compiler_params=pltpu.CompilerParams(has_side_effects=True),
    )(*shards)


def _send_half_to_sibling(g2):
    def body(g_ref, recv_ref, send_sem, recv_sem):
        x, y, c = _mesh_pos()
        cp = pltpu.make_async_remote_copy(
            src_ref=g_ref.at[1 - c], dst_ref=recv_ref, send_sem=send_sem, recv_sem=recv_sem,
            device_id=(x, y, 1 - c), device_id_type=MESH)
        cp.start()
        cp.wait()

    return pl.pallas_call(
        body, name="rs_to_sibling", in_specs=[HBM_SPEC], out_specs=HBM_SPEC,
        out_shape=jax.ShapeDtypeStruct(g2.shape[1:], g2.dtype),
        scratch_shapes=[pltpu.SemaphoreType.DMA(()), pltpu.SemaphoreType.DMA(())],
        compiler_params=pltpu.CompilerParams(has_side_effects=True),
    )(g2)


def _add_own_half(g2, recv, c):
    rows = N_CHIPS * HALF_ROWS
    tr = 512
    g2f = g2.reshape(2, rows, D)
    recvf = recv.reshape(rows, D)

    def body(c_ref, a_ref, b_ref, o_ref):
        o_ref[...] = a_ref[0] + b_ref[...]

    out = pl.pallas_call(
        body, name="rs_add_halves",
        grid_spec=pltpu.PrefetchScalarGridSpec(
            num_scalar_prefetch=1, grid=(rows // tr,),
            in_specs=[pl.BlockSpec((1, tr, D), lambda i, s: (s[0], i, 0)), pl.BlockSpec((tr, D), lambda i, s: (i, 0))],
            out_specs=pl.BlockSpec((tr, D), lambda i, s: (i, 0))),
        out_shape=jax.ShapeDtypeStruct((rows, D), F32),
        compiler_params=_params(1),
    )(jnp.reshape(c, (1,)).astype(jnp.int32), g2f, recvf)
    return out.reshape(N_CHIPS, HALF_ROWS, D)


def _exchange_chip_sums(p):
    def body(p_ref, out_ref, send_sems, recv_sems, local_sem):
        x, y, c = _mesh_pos()
        mine = 2 * x + y
        chips = _other_chips(x, y)
        local = pltpu.make_async_copy(p_ref.at[mine], out_ref.at[mine], local_sem)
        local.start()

        def remote(k, src_block, dst_block):
            px, py = chips[k]
            return pltpu.make_async_remote_copy(
                src_ref=p_ref.at[src_block], dst_ref=out_ref.at[dst_block], send_sem=send_sems.at[k],
                recv_sem=recv_sems.at[k], device_id=(px, py, c), device_id_type=MESH)

        sends = [remote(k, 2 * px + py, mine) for k, (px, py) in enumerate(chips)]
        for cp in sends:
            cp.start()
        for k, (px, py) in enumerate(chips):
            remote(k, mine, 2 * px + py).wait_recv()
        for cp in sends:
            cp.wait_send()
        local.wait()

    return pl.pallas_call(
        body, name="rs_chip_exchange", in_specs=[HBM_SPEC], out_specs=HBM_SPEC,
        out_shape=jax.ShapeDtypeStruct(p.shape, p.dtype),
        scratch_shapes=[pltpu.SemaphoreType.DMA((3,)), pltpu.SemaphoreType.DMA((3,)), pltpu.SemaphoreType.DMA(())],
        compiler_params=pltpu.CompilerParams(has_side_effects=True),
    )(p)


def _sum_chips(parts):
    tr = 512

    def body(p_ref, o_ref):
        o_ref[...] = ((p_ref[0] + p_ref[1]) + p_ref[2]) + p_ref[3]

    return pl.pallas_call(
        body, name="rs_sum_chips", grid=(HALF_ROWS // tr,),
        in_specs=[pl.BlockSpec((N_CHIPS, tr, D), lambda i: (0, i, 0))],
        out_specs=pl.BlockSpec((tr, D), lambda i: (i, 0)),
        out_shape=jax.ShapeDtypeStruct((HALF_ROWS, D), F32),
        compiler_params=_params(1),
    )(parts)


def _share_with_sibling(half):
    def body(h_ref, out_ref, send_sem, recv_sem, local_sem):
        x, y, c = _mesh_pos()
        local = pltpu.make_async_copy(h_ref, out_ref.at[c], local_sem)
        local.start()
        cp = pltpu.make_async_remote_copy(
            src_ref=h_ref, dst_ref=out_ref.at[c], send_sem=send_sem, recv_sem=recv_sem,
            device_id=(x, y, 1 - c), device_id_type=MESH)
        cp.start()
        pltpu.make_async_remote_copy(
            src_ref=h_ref, dst_ref=out_ref.at[1 - c], send_sem=send_sem, recv_sem=recv_sem,
            device_id=(x, y, 1 - c), device_id_type=MESH).wait_recv()
        cp.wait_send()
        local.wait()

    return pl.pallas_call(
        body, name="rs_share_sibling", in_specs=[HBM_SPEC], out_specs=HBM_SPEC,
        out_shape=jax.ShapeDtypeStruct((2,) + half.shape, half.dtype),
        scratch_shapes=[pltpu.SemaphoreType.DMA(()), pltpu.SemaphoreType.DMA(()), pltpu.SemaphoreType.DMA(())],
        compiler_params=pltpu.CompilerParams(has_side_effects=True),
    )(half)


def _adam_update(g, w, m, v):
    m2 = ADAM_B1 * m + (1.0 - ADAM_B1) * g
    v2 = ADAM_B2 * v + (1.0 - ADAM_B2) * (g * g)
    m_hat = m2 / (1.0 - ADAM_B1 ** ADAM_STEP)
    v_hat = v2 / (1.0 - ADAM_B2 ** ADAM_STEP)
    delta = -ADAM_LR * (m_hat / (jnp.sqrt(v_hat) + ADAM_EPS) + ADAM_WD * w)
    return delta, m2, v2


def _adamw_slab(g, w, m, v):
    rows = g.shape[0]
    tr = 256

    def body(g_ref, w_ref, m_ref, v_ref, d_ref, m2_ref, v2_ref):
        d_ref[...], m2_ref[...], v2_ref[...] = _adam_update(g_ref[...], w_ref[...], m_ref[...], v_ref[...])

    spec = pl.BlockSpec((tr, D), lambda i: (i, 0))
    return pl.pallas_call(
        body, name="adamw_slab", grid=(rows // tr,), in_specs=[spec] * 4, out_specs=[spec] * 3,
        out_shape=[jax.ShapeDtypeStruct((rows, D), F32)] * 3,
        compiler_params=_params(1),
    )(g, w, m, v)


def _allreduce_small_adamw(part, w, m, v):
    def body(p_ref, w_ref, m_ref, v_ref, g_ref, d_ref, m2_ref, v2_ref, slots, send_sems, recv_sems):
        x, y, c = _mesh_pos()
        mine = 4 * x + 2 * y + c
        peers = [(px, py, pc) for px in (x, 1 - x) for py in (y, 1 - y) for pc in (c, 1 - c)][1:]

        def remote(k, slot):
            return pltpu.make_async_remote_copy(
                src_ref=p_ref, dst_ref=slots.at[slot], send_sem=send_sems.at[k], recv_sem=recv_sems.at[k],
                device_id=peers[k], device_id_type=MESH)

        sends = [remote(k, mine) for k in range(7)]
        for cp in sends:
            cp.start()
        slots[mine] = p_ref[...]
        for k, (px, py, pc) in enumerate(peers):
            remote(k, 4 * px + 2 * py + pc).wait_recv()
        for cp in sends:
            cp.wait_send()
        g = slots[0]
        for d in range(1, 8):
            g = g + slots[d]
        g_ref[...] = g
        d_ref[...], m2_ref[...], v2_ref[...] = _adam_update(g, w_ref[...], m_ref[...], v_ref[...])

    vm = pl.BlockSpec(memory_space=pltpu.VMEM)
    shape = jax.ShapeDtypeStruct(part.shape, F32)
    return pl.pallas_call(
        body, name="small_allreduce_adamw", in_specs=[vm] * 4, out_specs=[vm] * 4, out_shape=[shape] * 4,
        scratch_shapes=[pltpu.VMEM((8,) + part.shape, F32), pltpu.SemaphoreType.DMA((7,)),
                        pltpu.SemaphoreType.DMA((7,))],
        compiler_params=pltpu.CompilerParams(has_side_effects=True),
    )(part, w, m, v)


def _half(ref, c, axis):
    n = ref.shape[axis] // 2
    return ref.at[(slice(None),) * axis + (pl.ds(c * n, n),)]


def _remote(src, dst, send_sem, recv_sem, device):
    return pltpu.make_async_remote_copy(src_ref=src, dst_ref=dst, send_sem=send_sem, recv_sem=recv_sem,
                                        device_id=device, device_id_type=MESH)


def _gather_weights(split, axes, whole):
    ns, n = len(split), len(split) + len(whole)

    def body(*refs):
        ins, outs = refs[:n], refs[n:2 * n]
        ici_send, ici_recv, d2d_send, d2d_recv, local_sems = refs[2 * n:]
        x, y, c = _mesh_pos()
        mine = 2 * x + y
        chips = _other_chips(x, y)
        local = [pltpu.make_async_copy(ins[a], outs[a].at[mine], local_sems.at[a]) for a in range(n)]
        for cp in local:
            cp.start()

        def ici(a, k, block):
            px, py = chips[k]
            src, dst = ins[a], outs[a].at[block]
            if a < ns:
                src, dst = _half(src, c, axes[a]), _half(dst, c, axes[a])
            return _remote(src, dst, ici_send.at[3 * a + k], ici_recv.at[3 * a + k], (px, py, c))

        def d2d(a, k, block, half):
            part = _half(outs[a].at[block], half, axes[a])
            return _remote(part, part, d2d_send.at[3 * a + k], d2d_recv.at[3 * a + k], (x, y, 1 - c))

        sends = [ici(a, k, mine) for a in range(n) for k in range(3)]
        for cp in sends:
            cp.start()
        for a in range(n):
            for k, (px, py) in enumerate(chips):
                ici(a, k, 2 * px + py).wait_recv()
                if a < ns:
                    sends.append(d2d(a, k, 2 * px + py, c))
                    sends[-1].start()
        for a in range(ns):
            for k, (px, py) in enumerate(chips):
                d2d(a, k, 2 * px + py, 1 - c).wait_recv()
        for cp in sends:
            cp.wait_send()
        for cp in local:
            cp.wait()

    arrays = list(split) + list(whole)
    return pl.pallas_call(
        body, name="gather_weights", in_specs=[HBM_SPEC] * n, out_specs=[HBM_SPEC] * n,
        out_shape=[jax.ShapeDtypeStruct((N_CHIPS,) + s.shape, s.dtype) for s in arrays],
        scratch_shapes=[pltpu.SemaphoreType.DMA((3 * n,)), pltpu.SemaphoreType.DMA((3 * n,)),
                        pltpu.SemaphoreType.DMA((3 * ns,)), pltpu.SemaphoreType.DMA((3 * ns,)),
                        pltpu.SemaphoreType.DMA((n,))],
        compiler_params=pltpu.CompilerParams(has_side_effects=True),
    )(*arrays)


def _rs_to_sibling(gs):
    n = len(gs)

    def body(*refs):
        ins, outs, send_sems, recv_sems = refs[:n], refs[n:2 * n], refs[2 * n], refs[2 * n + 1]
        x, y, c = _mesh_pos()
        copies = [_remote(_half(ins[a], 1 - c, 2), outs[a], send_sems.at[a], recv_sems.at[a], (x, y, 1 - c))
                  for a in range(n)]
        for cp in copies:
            cp.start()
        for cp in copies:
            cp.wait()

    return pl.pallas_call(
        body, name="rs_to_sibling", in_specs=[HBM_SPEC] * n, out_specs=[HBM_SPEC] * n,
        out_shape=[jax.ShapeDtypeStruct(g.shape[:2] + (g.shape[2] // 2,), g.dtype) for g in gs],
        scratch_shapes=[pltpu.SemaphoreType.DMA((n,)), pltpu.SemaphoreType.DMA((n,))],
        compiler_params=pltpu.CompilerParams(has_side_effects=True),
    )(*gs)


def _rs_add_halves(g, recv, c, name):
    _, rows, w = g.shape
    h = w // 2
    tr = rows // 2 if rows % 16 == 0 and rows > 64 else rows

    def body(c_ref, a_ref, b_ref, o_ref):
        o_ref[...] = (a_ref[...] + b_ref[...]).astype(BF16)

    return pl.pallas_call(
        body, name=name,
        grid_spec=pltpu.PrefetchScalarGridSpec(
            num_scalar_prefetch=1, grid=(N_CHIPS, rows // tr),
            in_specs=[pl.BlockSpec((1, tr, h), lambda j, i, s: (j, i, s[0])),
                      pl.BlockSpec((1, tr, h), lambda j, i, s: (j, i, 0))],
            out_specs=pl.BlockSpec((1, tr, h), lambda j, i, s: (j, i, 0))),
        out_shape=jax.ShapeDtypeStruct((N_CHIPS, rows, h), BF16),
        compiler_params=_params(2),
    )(jnp.reshape(c, (1,)).astype(jnp.int32), g, recv)


def _rs_chip_exchange(ps):
    n = len(ps)

    def body(*refs):
        ins, outs = refs[:n], refs[n:2 * n]
        send_sems, recv_sems, local_sems = refs[2 * n:]
        x, y, c = _mesh_pos()
        mine = 2 * x + y
        chips = _other_chips(x, y)
        local = [pltpu.make_async_copy(ins[a].at[mine], outs[a].at[mine], local_sems.at[a]) for a in range(n)]
        for cp in local:
            cp.start()

        def ici(a, k, src_block, dst_block):
            px, py = chips[k]
            return _remote(ins[a].at[src_block], outs[a].at[dst_block], send_sems.at[3 * a + k],
                           recv_sems.at[3 * a + k], (px, py, c))

        sends = [ici(a, k, 2 * px + py, mine) for a in range(n) for k, (px, py) in enumerate(chips)]
        for cp in sends:
            cp.start()
        for a in range(n):
            for k, (px, py) in enumerate(chips):
                ici(a, k, mine, 2 * px + py).wait_recv()
        for cp in sends:
            cp.wait_send()
        for cp in local:
            cp.wait()

    return pl.pallas_call(
        body, name="rs_chip_exchange", in_specs=[HBM_SPEC] * n, out_specs=[HBM_SPEC] * n,
        out_shape=[jax.ShapeDtypeStruct(p.shape, p.dtype) for p in ps],
        scratch_shapes=[pltpu.SemaphoreType.DMA((3 * n,)), pltpu.SemaphoreType.DMA((3 * n,)),
                        pltpu.SemaphoreType.DMA((n,))],
        compiler_params=pltpu.CompilerParams(has_side_effects=True),
    )(*ps)


def _rs_sum_chips(parts, name):
    _, rows, h = parts.shape
    tr = rows // 2 if rows % 16 == 0 and rows > 64 else rows

    def body(p_ref, o_ref):
        p = p_ref[...].astype(F32)
        o_ref[...] = ((p[0] + p[1]) + p[2]) + p[3]

    return pl.pallas_call(
        body, name=name, grid=(rows // tr,),
        in_specs=[pl.BlockSpec((N_CHIPS, tr, h), lambda i: (0, i, 0))],
        out_specs=pl.BlockSpec((tr, h), lambda i: (i, 0)),
        out_shape=jax.ShapeDtypeStruct((rows, h), F32),
        compiler_params=_params(1),
    )(parts)


def _rs_share(halves):
    n = len(halves)

    def body(*refs):
        ins, outs = refs[:n], refs[n:2 * n]
        send_sems, recv_sems, local_sems = refs[2 * n:]
        x, y, c = _mesh_pos()
        local = [pltpu.make_async_copy(ins[a], _half(outs[a], c, 1), local_sems.at[a]) for a in range(n)]
        for cp in local:
            cp.start()
        sends = [_remote(ins[a], _half(outs[a], c, 1), send_sems.at[a], recv_sems.at[a], (x, y, 1 - c))
                 for a in range(n)]
        for cp in sends:
            cp.start()
        for a in range(n):
            _remote(ins[a], _half(outs[a], 1 - c, 1), send_sems.at[a], recv_sems.at[a], (x, y, 1 - c)).wait_recv()
        for cp in sends:
            cp.wait_send()
        for cp in local:
            cp.wait()

    return pl.pallas_call(
        body, name="rs_share", in_specs=[HBM_SPEC] * n, out_specs=[HBM_SPEC] * n,
        out_shape=[jax.ShapeDtypeStruct((p.shape[0], 2 * p.shape[1]), p.dtype) for p in halves],
        scratch_shapes=[pltpu.SemaphoreType.DMA((n,)), pltpu.SemaphoreType.DMA((n,)),
                        pltpu.SemaphoreType.DMA((n,))],
        compiler_params=pltpu.CompilerParams(has_side_effects=True),
    )(*halves)


def _adamw(g, w, m, v, name):
    rows, cols = g.shape
    tr = 256 if rows % 256 == 0 else (rows // 2 if rows % 16 == 0 and rows > 64 else rows)

    def body(g_ref, w_ref, m_ref, v_ref, d_ref, m2_ref, v2_ref):
        d_ref[...], m2_ref[...], v2_ref[...] = _adam_update(g_ref[...], w_ref[...], m_ref[...], v_ref[...])

    spec = pl.BlockSpec((tr, cols), lambda i: (i, 0))
    return pl.pallas_call(
        body, name=name, grid=(rows // tr,), in_specs=[spec] * 4, out_specs=[spec] * 3,
        out_shape=[jax.ShapeDtypeStruct((rows, cols), F32)] * 3,
        compiler_params=_params(1),
    )(g, w, m, v)


def _rows_of(a):
    flat = a.reshape(-1)
    pad = (-flat.shape[0]) % D
    if pad:
        flat = jnp.concatenate([flat, jnp.zeros((pad,), flat.dtype)])
    return flat.reshape(-1, D)


SLAB_PARTS = (("w_in", (D, D_IN // N_CHIPS)), ("w_out", (D // N_CHIPS, D)), ("w_ffn_gate", (D, D_FF // N_CHIPS)),
              ("w_ffn_up", (D, D_FF // N_CHIPS)), ("w_ffn_down", (D_FF // N_CHIPS, D)),
              ("meta_tokens", (N_META, D // N_CHIPS)), ("conv_w", (CONV_W, C_CONV // N_CHIPS)),
              ("gla_w_gate2", (RANK, GLA_K // N_CHIPS)))


def _pack_slab(parts):
    rows = [_rows_of(parts[name].reshape(shape)) for name, shape in SLAB_PARTS]
    used = sum(r.shape[0] for r in rows)
    rows.append(jnp.zeros((SLAB_ROWS - used, D), F32))
    return jnp.concatenate(rows, axis=0)


def _unpack_slab(slab, lead):
    out, r0 = {}, 0
    for name, shape in SLAB_PARTS:
        size = shape[0] * shape[1]
        nrows = -(-size // D)
        out[name] = slab[r0:r0 + nrows].reshape(-1)[:size].reshape(lead[name] + shape)
        r0 += nrows
    return out


SMALL_PARTS = (("norm_mix_g", 0, 0, D), ("norm_ffn_g", 1, 0, D), ("norm_final_g", 2, 0, D),
               ("conv_b", 3, 0, C_CONV), ("conv_ln_g", 3, C_CONV, C_CONV), ("conv_ln_b", 4, 0, C_CONV),
               ("gla_gate_b", 4, C_CONV, GLA_K), ("gla_norm_g", 4, C_CONV + GLA_K, DV))


def _pack_small(parts):
    slab = jnp.zeros((SMALL_ROWS, D), F32)
    for name, row, col, size in SMALL_PARTS:
        slab = lax.dynamic_update_slice(slab, parts[name].reshape(1, size).astype(F32), (row, col))
    return slab


def _unpack_small(slab, shapes):
    return {name: slab[row, col:col + size].reshape(shapes[name]) for name, row, col, size in SMALL_PARTS}


def _column_block(full, j, width):
    return lax.dynamic_slice_in_dim(full, j * width, width, axis=1)


def _local_step(x, target, w):
    n_ex, seq, _ = x.shape
    lp = HEAD_ROWS + seq
    t = n_ex * lp
    meta = jnp.broadcast_to(w["meta_tokens"][None], (n_ex, N_META, D))
    h0 = jnp.concatenate([jnp.zeros((n_ex, PAD_ROWS, D), F32), meta, x], axis=1).reshape(t, D)
    tgt = jnp.concatenate([jnp.zeros((n_ex, HEAD_ROWS, D), F32), target], axis=1).reshape(t, D)
    row_mask = jnp.concatenate([jnp.zeros((n_ex, HEAD_ROWS, 1), F32), jnp.ones((n_ex, seq, 1), F32)],
                               axis=1).reshape(t, 1)

    u, hn = _in_proj(h0, w["norm_mix_g"], w["w_in"])
    yc, y_conv = _conv_fwd(u, w["conv_w"], w["conv_b"], w["conv_ln_g"], w["conv_ln_b"], n_ex, lp)
    y_gla, states = _gla_fwd(u, w["gla_w_gate2"], w["gla_gate_b"], w["gla_norm_g"], n_ex, lp)
    h1, hn2, gate, up, act = _mix_out_ffn_up(h0, y_conv, y_gla, w["w_out"], w["norm_ffn_g"],
                                             w["w_ffn_gate_t"], w["w_ffn_up_t"])
    dh2, loss, d_final_g = _ffn_down_loss(act, w["w_ffn_down"], h1, tgt, w["norm_final_g"], row_mask)

    dgate, dup, dh1, dycat, d_ffn_g = _ffn_bwd(dh2, gate, up, h1, w["w_ffn_down"], w["w_ffn_gate_t"],
                                                w["w_ffn_up_t"], w["w_out"], w["norm_ffn_g"])
    du_conv, d_conv_w, d_conv_b, d_ln_g, d_ln_b = _conv_bwd(dycat, yc, u, w["conv_w"], w["conv_ln_g"],
                                                            w["conv_ln_b"], n_ex, lp)
    du_gla, d_w2, d_gate_b, d_norm_g = _gla_bwd(dycat, u, states, w["gla_w_gate2"], w["gla_gate_b"],
                                                w["gla_norm_g"], n_ex, lp)
    dh0, d_mix_g = _in_proj_bwd(du_conv, du_gla, w["w_in"][:, :2 * C_CONV], w["w_in"][:, 2 * C_CONV:],
                                h0, dh1, w["norm_mix_g"])

    d_w_in_t = jnp.concatenate([_wgrad(du_conv, hn, "wgrad_in_conv"), _wgrad(du_gla, hn, "wgrad_in_gla")],
                               axis=0)[:D_IN]
    d_w_out = jnp.concatenate([_wgrad(y_conv, dh1, "wgrad_out_conv"), _wgrad(y_gla, dh1, "wgrad_out_gla")], axis=0)
    dh0 = dh0.reshape(n_ex, lp, D)
    grads = {
        "w_in_t": d_w_in_t, "w_out": d_w_out,
        "w_ffn_gate_t": _wgrad(dgate, hn2, "wgrad_gate"), "w_ffn_up_t": _wgrad(dup, hn2, "wgrad_up"),
        "w_ffn_down": _wgrad(act, dh2, "wgrad_down"),
        "meta_tokens": jnp.sum(dh0[:, PAD_ROWS:HEAD_ROWS], axis=0),
        "conv_w": d_conv_w, "gla_w_gate2": d_w2[:RANK],
        "norm_mix_g": d_mix_g, "norm_ffn_g": d_ffn_g, "norm_final_g": d_final_g,
        "conv_b": d_conv_b, "conv_ln_g": d_ln_g, "conv_ln_b": d_ln_b,
        "gla_gate_b": d_gate_b, "gla_norm_g": d_norm_g,
    }
    return loss[0, 0], dh0[:, HEAD_ROWS:], grads


WEIGHT_NAMES = ("meta_tokens", "norm_mix_g", "w_in", "conv_w", "conv_b", "conv_ln_g", "conv_ln_b", "gla_w_gate2",
                "gla_gate_b", "gla_norm_g", "w_out", "norm_ffn_g", "w_ffn_gate", "w_ffn_up", "w_ffn_down",
                "norm_final_g")
MATMUL_WEIGHTS = ("w_in", "w_out", "w_ffn_gate", "w_ffn_up", "w_ffn_down")
ROW_SHARDED = ("w_out", "w_ffn_down")


def _full_weights(ws):
    sh = lambda name: ws[name].reshape(ws[name].shape[-2:])
    split = [sh("w_in").astype(BF16), sh("w_out").astype(BF16), sh("w_ffn_gate").T.astype(BF16),
             sh("w_ffn_up").T.astype(BF16), sh("w_ffn_down").astype(BF16)]
    whole = [sh("meta_tokens"), sh("conv_w"), sh("gla_w_gate2")]
    w_in, w_out, gate_t, up_t, down, meta, conv_w, w2 = _gather_weights(split, [0, 0, 0, 0, 0], whole)
    cols = lambda a: jnp.concatenate([a[j] for j in range(N_CHIPS)], axis=1)
    full = {name: ws[name].reshape(1, -1) for name, _, _, _ in SMALL_PARTS}
    full["w_in"] = jnp.concatenate([cols(w_in), jnp.zeros((D, D_IN_PAD - D_IN), BF16)], axis=1)
    full["w_out"] = w_out.reshape(D, D)
    full["w_ffn_gate_t"] = gate_t.reshape(D_FF, D)
    full["w_ffn_up_t"] = up_t.reshape(D_FF, D)
    full["w_ffn_down"] = down.reshape(D_FF, D)
    full["meta_tokens"] = cols(meta)
    full["conv_w"] = jnp.concatenate([cols(conv_w), jnp.zeros((32 - CONV_W, C_CONV), F32)], axis=0)
    full["gla_w_gate2"] = jnp.concatenate([cols(w2), jnp.zeros((128 - RANK, GLA_K), F32)], axis=0).astype(BF16)
    return full


SMALL_RS_ROWS = 48


def _pack_small_sharded(grads):
    by_chip = lambda g, w: jnp.transpose(g.reshape(g.shape[0], N_CHIPS, w), (1, 0, 2))
    meta = by_chip(grads["meta_tokens"], D // N_CHIPS)
    conv = by_chip(grads["conv_w"], C_CONV // N_CHIPS).reshape(N_CHIPS, 16, 256)
    w2 = by_chip(grads["gla_w_gate2"], GLA_K // N_CHIPS).reshape(N_CHIPS, 4, 256)
    pad = jnp.zeros((N_CHIPS, SMALL_RS_ROWS - 36, 256), F32)
    return jnp.concatenate([meta, conv, w2, pad], axis=1)


def _unpack_small_sharded(g):
    return {"meta_tokens": g[0:16], "conv_w": g[16:32].reshape(32, C_CONV // N_CHIPS)[:CONV_W],
            "gla_w_gate2": g[32:36].reshape(RANK, GLA_K // N_CHIPS)}


def kernel(x, meta_tokens, norm_mix_g, w_in, conv_w, conv_b, conv_ln_g, conv_ln_b, gla_w_gate2, gla_gate_b, gla_norm_g, w_out, norm_ffn_g, w_ffn_gate, w_ffn_up, w_ffn_down, norm_final_g, loss_target, m_meta_tokens, m_norm_mix_g, m_w_in, m_conv_w, m_conv_b, m_conv_ln_g, m_conv_ln_b, m_gla_w_gate2, m_gla_gate_b, m_gla_norm_g, m_w_out, m_norm_ffn_g, m_w_ffn_gate, m_w_ffn_up, m_w_ffn_down, m_norm_final_g, v_meta_tokens, v_norm_mix_g, v_w_in, v_conv_w, v_conv_b, v_conv_ln_g, v_conv_ln_b, v_gla_w_gate2, v_gla_gate_b, v_gla_norm_g, v_w_out, v_norm_ffn_g, v_w_ffn_gate, v_w_ffn_up, v_w_ffn_down, v_norm_final_g):
    ws = dict(zip(WEIGHT_NAMES, (meta_tokens, norm_mix_g, w_in, conv_w, conv_b, conv_ln_g, conv_ln_b, gla_w_gate2,
                                 gla_gate_b, gla_norm_g, w_out, norm_ffn_g, w_ffn_gate, w_ffn_up, w_ffn_down,
                                 norm_final_g)))
    ms = dict(zip(WEIGHT_NAMES, (m_meta_tokens, m_norm_mix_g, m_w_in, m_conv_w, m_conv_b, m_conv_ln_g, m_conv_ln_b,
                                 m_gla_w_gate2, m_gla_gate_b, m_gla_norm_g, m_w_out, m_norm_ffn_g, m_w_ffn_gate,
                                 m_w_ffn_up, m_w_ffn_down, m_norm_final_g)))
    vs = dict(zip(WEIGHT_NAMES, (v_meta_tokens, v_norm_mix_g, v_w_in, v_conv_w, v_conv_b, v_conv_ln_g, v_conv_ln_b,
                                 v_gla_w_gate2, v_gla_gate_b, v_gla_norm_g, v_w_out, v_norm_ffn_g, v_w_ffn_gate,
                                 v_w_ffn_up, v_w_ffn_down, v_norm_final_g)))
    c = lax.axis_index("c")

    full = _full_weights(ws)
    loss, grad_x, grads = _local_step(x, loss_target, full)
    loss = lax.psum(loss, ("x", "y", "c"))

    rs_names = ("w_in", "w_out", "w_ffn_gate", "w_ffn_up", "w_ffn_down", "small")
    by_owner = [grads["w_in_t"].reshape(N_CHIPS, D_IN // N_CHIPS, D), grads["w_out"].reshape(N_CHIPS, D // N_CHIPS, D),
                grads["w_ffn_gate_t"].reshape(N_CHIPS, D_FF // N_CHIPS, D),
                grads["w_ffn_up_t"].reshape(N_CHIPS, D_FF // N_CHIPS, D),
                grads["w_ffn_down"].reshape(N_CHIPS, D_FF // N_CHIPS, D), _pack_small_sharded(grads)]
    from_sibling = _rs_to_sibling(by_owner)
    chip_sums = [_rs_add_halves(g, r, c, "rs_add_" + nm) for g, r, nm in zip(by_owner, from_sibling, rs_names)]
    halves = [_rs_sum_chips(p, "rs_sum_" + nm) for p, nm in zip(_rs_chip_exchange(chip_sums), rs_names)]
    reduced = dict(zip(rs_names, _rs_share(halves)))
    g_sharded = {"w_in": reduced["w_in"].T, "w_out": reduced["w_out"], "w_ffn_gate": reduced["w_ffn_gate"].T,
                 "w_ffn_up": reduced["w_ffn_up"].T, "w_ffn_down": reduced["w_ffn_down"],
                 **_unpack_small_sharded(reduced["small"])}
    out = {"grad": {}, "delta": {}, "new_m": {}, "new_v": {}}
    for name, g in g_sharded.items():
        shape = ws[name].shape
        flat = lambda a: a.reshape(shape[-2:])
        delta, new_m, new_v = _adamw(g, flat(ws[name]), flat(ms[name]), flat(vs[name]), "adamw_" + name)
        for kind, a in (("grad", g), ("delta", delta), ("new_m", new_m), ("new_v", new_v)):
            out[kind][name] = a.reshape(shape)

    small_shapes = {name: ws[name].shape for name, _, _, _ in SMALL_PARTS}
    g_s, d_s, m_s, v_s = _allreduce_small_adamw(_pack_small(grads), _pack_small(ws), _pack_small(ms), _pack_small(vs))
    for kind, slab in (("grad", g_s), ("delta", d_s), ("new_m", m_s), ("new_v", v_s)):
        out[kind].update(_unpack_small(slab, small_shapes))

    return (loss, grad_x, *[out[kind][name] for kind in ("grad", "delta", "new_m", "new_v") for name in WEIGHT_NAMES])
```

```python
import functools
from typing import Any, Callable, NamedTuple, Sequence

import jax
import jax.numpy as jnp
from jax import lax
from jax.experimental import pallas as pl
from jax.experimental.pallas import tpu as pltpu

F32 = jnp.float32
BF16 = jnp.bfloat16
MESH = pl.DeviceIdType.MESH

D = 1024
N_META = 16
C_CONV = 512
CONV_W = 31
GLA_K = 256
GLA_V = 512
N_HEADS = 4
DK = 64
DV = 128
RANK = 16
CHUNK = 64
PAD_ROWS = CHUNK - N_META
HEAD_ROWS = CHUNK
D_IN = 2576
D_IN_PAD = 2688
D_GLA_IN = D_IN_PAD - 2 * C_CONV
D_FF = 2816
RMS_EPS = 1e-6
LN_EPS = 1e-5
GATE_TAU = 16.0
N_CHIPS = 4

ADAM_LR = 0.001
ADAM_B1 = 0.9
ADAM_B2 = 0.999
ADAM_EPS = 1e-08
ADAM_WD = 0.01
ADAM_STEP = 10

V7X_VMEM_BYTES = 64 * 1024 * 1024
VMEM_LIMIT = V7X_VMEM_BYTES - 8 * 1024 * 1024

SLAB_ROWS = 3072
HALF_ROWS = SLAB_ROWS // 2
SMALL_ROWS = 8


def _dot(a, b):
    return jnp.dot(a, b, preferred_element_type=F32)


def _dot_nt(a, b):
    return lax.dot_general(a, b, (((1,), (1,)), ((), ())), preferred_element_type=F32)


def _dot_tn(a, b):
    return lax.dot_general(a, b, (((0,), (0,)), ((), ())), preferred_element_type=F32)


def _sigmoid(x):
    return 1.0 / (1.0 + jnp.exp(-x))


def _const_spec(shape):
    return pl.BlockSpec(shape, lambda *_: (0,) * len(shape), pipeline_mode=pl.Buffered(1))


def _acc_spec(shape):
    return pl.BlockSpec(shape, lambda *_: (0,) * len(shape))


def _params(n_axes):
    return pltpu.CompilerParams(dimension_semantics=("arbitrary",) * n_axes, vmem_limit_bytes=VMEM_LIMIT)


def _row_tile(t, want):
    for r in (want, 384, 192, 128, 64):
        if r <= want and t % r == 0:
            return r
    raise ValueError(f"no row tile for {t}")


class _Plan(NamedTuple):
    arrays: Sequence[Any]
    out_shape: Sequence[Any]
    sems: Sequence[Any]
    make: Callable


def _call(body, *, name, grid, in_specs, out_specs, out_shape, scratch_shapes=(), plan=None):
    n_in, n_out, n_scr = len(in_specs), len(out_specs), len(scratch_shapes)
    if plan is None:
        plan = _Plan([], [], [], lambda ins, outs, sems: (lambda: None, lambda: None))
    nx_in, nx_out = len(plan.arrays), len(plan.out_shape)

    def hosted(*refs):
        ins, xins = refs[:n_in], refs[n_in:n_in + nx_in]
        o0 = n_in + nx_in
        outs, xouts = refs[o0:o0 + n_out], refs[o0 + n_out:o0 + n_out + nx_out]
        s0 = o0 + n_out + nx_out
        scr, sems = refs[s0:s0 + n_scr], refs[s0 + n_scr:]
        ids = [pl.program_id(a) for a in range(len(grid))]
        first = functools.reduce(jnp.logical_and, [i == 0 for i in ids])
        last = functools.reduce(jnp.logical_and, [i == g - 1 for i, g in zip(ids, grid)])
        start, finish = plan.make(xins, xouts, sems)
        pl.when(first)(start)
        body(*ins, *outs, *scr)
        pl.when(last)(finish)

    call = pl.pallas_call(
        hosted, name=name, grid=grid, in_specs=list(in_specs) + [HBM_SPEC] * nx_in,
        out_specs=list(out_specs) + [HBM_SPEC] * nx_out, out_shape=list(out_shape) + list(plan.out_shape),
        scratch_shapes=list(scratch_shapes) + list(plan.sems),
        compiler_params=pltpu.CompilerParams(dimension_semantics=("arbitrary",) * len(grid),
                                             vmem_limit_bytes=VMEM_LIMIT, has_side_effects=nx_in > 0))

    def run(*args):
        res = call(*args, *plan.arrays)
        return res[:n_out], res[n_out:]

    return run


def _in_proj(h0, g_mix, w_in, plan=None):
    t = h0.shape[0]
    r = _row_tile(t, 384)

    def body(h_ref, g_ref, w_ref, u_ref, hn_ref):
        h = h_ref[...]
        rstd = lax.rsqrt(jnp.mean(h * h, axis=-1, keepdims=True) + RMS_EPS)
        hn = (h * rstd * g_ref[...]).astype(BF16)
        hn_ref[...] = hn
        u_ref[...] = _dot(hn, w_ref[...])

    return _call(
        body, name="in_proj", grid=(t // r,),
        in_specs=[pl.BlockSpec((r, D), lambda i: (i, 0)), _const_spec((1, D)), _const_spec((D, D_IN_PAD))],
        out_specs=[pl.BlockSpec((r, D_IN_PAD), lambda i: (i, 0)), pl.BlockSpec((r, D), lambda i: (i, 0))],
        out_shape=[jax.ShapeDtypeStruct((t, D_IN_PAD), F32), jax.ShapeDtypeStruct((t, D), BF16)],
        plan=plan,
    )(h0, g_mix, w_in)


CONV_TILE = 192
CONV_SUB = 32
CONV_LEAD = CONV_SUB - (CONV_W - 1)


def _conv_fwd(u, conv_w, conv_b, ln_g, ln_b, n_ex, lp, plan=None):
    r = CONV_TILE
    nt = lp // r
    hb = r // CONV_SUB

    def body(cur_ref, prev_ref, w_ref, b_ref, lg_ref, lb_ref, yc_ref, y_ref, glu):
        i = pl.program_id(1)
        cur = cur_ref[...]
        glu[CONV_SUB:CONV_SUB + r, :] = cur[:, :C_CONV] * _sigmoid(cur[:, C_CONV:])
        pv = prev_ref[...]
        halo = pv[:, :C_CONV] * _sigmoid(pv[:, C_CONV:])
        glu[0:CONV_SUB, :] = jnp.where(i > 0, halo, 0.0)
        w = w_ref[...]
        for j in range(r // CONV_SUB):
            r0 = j * CONV_SUB
            acc = jnp.zeros((CONV_SUB, C_CONV), F32) + b_ref[...]
            for k in range(CONV_W):
                acc = acc + w[k:k + 1, :] * glu[r0 + CONV_LEAD + k:r0 + CONV_LEAD + k + CONV_SUB, :]
            mu = jnp.mean(acc, axis=-1, keepdims=True)
            cen = acc - mu
            var = jnp.mean(cen * cen, axis=-1, keepdims=True)
            out = cen * lax.rsqrt(var + LN_EPS) * lg_ref[...] + lb_ref[...]
            y = out * _sigmoid(out)
            row = i * r + r0 + lax.broadcasted_iota(jnp.int32, (CONV_SUB, 1), 0)
            y = jnp.where(row >= PAD_ROWS, y, 0.0)
            yc_ref[r0:r0 + CONV_SUB, :] = acc
            y_ref[r0:r0 + CONV_SUB, :] = y.astype(BF16)

    t = n_ex * lp
    return _call(
        body, name="conv_fwd", grid=(n_ex, nt),
        in_specs=[pl.BlockSpec((r, 2 * C_CONV), lambda b, i: (b * nt + i, 0)),
                  pl.BlockSpec((CONV_SUB, 2 * C_CONV), lambda b, i: (jnp.maximum((b * nt + i) * hb - 1, 0), 0)),
                  _const_spec((32, C_CONV)), _const_spec((1, C_CONV)), _const_spec((1, C_CONV)), _const_spec((1, C_CONV))],
        out_specs=[pl.BlockSpec((r, C_CONV), lambda b, i: (b * nt + i, 0)),
                   pl.BlockSpec((r, C_CONV), lambda b, i: (b * nt + i, 0))],
        out_shape=[jax.ShapeDtypeStruct((t, C_CONV), F32), jax.ShapeDtypeStruct((t, C_CONV), BF16)],
        scratch_shapes=[pltpu.VMEM((r + CONV_SUB, C_CONV), F32)],
        plan=plan,
    )(u, u, conv_w, conv_b, ln_g, ln_b)


def _gla_gates(lr, w2, gb, first_chunk):
    z = _dot(lr.astype(BF16), w2) + gb
    a = (jnp.minimum(z, 0.0) - jnp.log(1.0 + jnp.exp(-jnp.abs(z)))) * (1.0 / GATE_TAU)
    row = lax.broadcasted_iota(jnp.int32, (CHUNK, 1), 0)
    live = jnp.logical_or(jnp.logical_not(first_chunk), row >= PAD_ROWS)
    return z, jnp.where(live, a, 0.0), live


def _tri(lower):
    i = lax.broadcasted_iota(jnp.int32, (CHUNK, CHUNK), 0)
    j = lax.broadcasted_iota(jnp.int32, (CHUNK, CHUNK), 1)
    return (i >= j) if lower else (i <= j)


def _gla_fwd(u, w2, gb, ng, n_ex, lp, plan=None):
    nc = lp // CHUNK
    t = n_ex * lp

    def body(qk_ref, v_ref, g_ref, lr_ref, w2_ref, gb_ref, ng_ref, y_ref, st_ref, state):
        n = pl.program_id(1)

        @pl.when(n == 0)
        def _():
            state[...] = jnp.zeros_like(state)

        st = state[...]
        st_ref[...] = st
        qk = qk_ref[...]
        q, k = qk[:, :GLA_K], qk[:, GLA_K:]
        _, a, _ = _gla_gates(lr_ref[...], w2_ref[...], gb_ref[...], n == 0)
        causal = _tri(True)
        b = jnp.dot(causal.astype(F32), a, preferred_element_type=F32, precision=lax.Precision.HIGHEST)
        bl = b[CHUNK - 1:CHUNK, :]
        q_in = (q * (DK ** -0.5) * jnp.exp(b)).astype(BF16)
        k_in = (k * jnp.exp(-b)).astype(BF16)
        k_dec = (k * jnp.exp(bl - b)).astype(BF16)
        decay = jnp.exp(bl)
        v = v_ref[...]
        g = g_ref[...]
        st_b = st.astype(BF16)
        ys, new = [], []
        for h in range(N_HEADS):
            ks = slice(h * DK, (h + 1) * DK)
            vs = slice(h * DV, (h + 1) * DV)
            vh = v[:, vs].astype(BF16)
            s = jnp.where(causal, _dot_nt(q_in[:, ks], k_in[:, ks]), 0.0)
            o = _dot(s.astype(BF16), vh) + _dot_nt(q_in[:, ks], st_b[:, ks])
            new.append(decay[:, ks] * st[:, ks] + _dot_tn(vh, k_dec[:, ks]))
            rstd = lax.rsqrt(jnp.mean(o * o, axis=-1, keepdims=True) + RMS_EPS)
            gh = g[:, vs]
            ys.append(o * rstd * ng_ref[...] * (gh * _sigmoid(gh)))
        state[...] = jnp.concatenate(new, axis=1)
        y_ref[...] = jnp.concatenate(ys, axis=1).astype(BF16)

    blk = lambda w, col: pl.BlockSpec((CHUNK, w), lambda b, n: (b * nc + n, col))
    return _call(
        body, name="gla_fwd", grid=(n_ex, nc),
        in_specs=[blk(2 * GLA_K, 2), blk(GLA_V, 3), blk(GLA_V, 4), blk(128, 20),
                  _const_spec((128, GLA_K)), _const_spec((1, GLA_K)), _const_spec((1, DV))],
        out_specs=[pl.BlockSpec((CHUNK, GLA_V), lambda b, n: (b * nc + n, 0)),
                   pl.BlockSpec((DV, GLA_K), lambda b, n: (b * nc + n, 0))],
        out_shape=[jax.ShapeDtypeStruct((t, GLA_V), BF16), jax.ShapeDtypeStruct((n_ex * nc * DV, GLA_K), F32)],
        scratch_shapes=[pltpu.VMEM((DV, GLA_K), F32)],
        plan=plan,
    )(u, u, u, u, w2, gb, ng)


FFN_TILE = 192


def _mix_out_ffn_up(h0, y_conv, y_gla, w_out, g_ffn, w_gate_t, w_up_t):
    t = h0.shape[0]
    r = _row_tile(t, FFN_TILE)

    def body(h0_ref, yc_ref, yg_ref, wo_ref, g_ref, wg_ref, wu_ref, h1_ref, hn_ref, gate_ref, up_ref, act_ref):
        h1 = h0_ref[...] + _dot(yc_ref[...], wo_ref[0:C_CONV, :]) + _dot(yg_ref[...], wo_ref[C_CONV:D, :])
        h1_ref[...] = h1
        rstd = lax.rsqrt(jnp.mean(h1 * h1, axis=-1, keepdims=True) + RMS_EPS)
        hn = (h1 * rstd * g_ref[...]).astype(BF16)
        hn_ref[...] = hn
        gate = _dot_nt(hn, wg_ref[...])
        up = _dot_nt(hn, wu_ref[...])
        gate_ref[...] = gate
        up_ref[...] = up
        act_ref[...] = (gate * _sigmoid(gate) * up).astype(BF16)

    rows = lambda w: pl.BlockSpec((r, w), lambda i: (i, 0))
    return pl.pallas_call(
        body, name="mix_out_ffn_up", grid=(t // r,),
        in_specs=[rows(D), rows(C_CONV), rows(GLA_V), _const_spec((D, D)), _const_spec((1, D)),
                  _const_spec((D_FF, D)), _const_spec((D_FF, D))],
        out_specs=[rows(D), rows(D), rows(D_FF), rows(D_FF), rows(D_FF)],
        out_shape=[jax.ShapeDtypeStruct((t, D), F32), jax.ShapeDtypeStruct((t, D), BF16),
                   jax.ShapeDtypeStruct((t, D_FF), F32), jax.ShapeDtypeStruct((t, D_FF), F32),
                   jax.ShapeDtypeStruct((t, D_FF), BF16)],
        compiler_params=_params(1),
    )(h0, y_conv, y_gla, w_out, g_ffn, w_gate_t, w_up_t)


def _ffn_down_loss(act, w_down, h1, target, g_final, row_mask):
    t = h1.shape[0]
    r = _row_tile(t, 384)

    def body(act_ref, wd_ref, h1_ref, tgt_ref, gf_ref, mask_ref, dh2_ref, loss_ref, dgf_ref):
        @pl.when(pl.program_id(0) == 0)
        def _():
            loss_ref[...] = jnp.zeros_like(loss_ref)
            dgf_ref[...] = jnp.zeros_like(dgf_ref)

        h2 = h1_ref[...] + _dot(act_ref[...], wd_ref[...])
        rstd = lax.rsqrt(jnp.mean(h2 * h2, axis=-1, keepdims=True) + RMS_EPS)
        nrm = h2 * rstd
        gf = gf_ref[...]
        err = (nrm * gf - tgt_ref[...]) * mask_ref[...]
        loss_ref[...] += jnp.sum(err * err) * (0.5 / D)
        dy = err * (1.0 / D)
        dgf_ref[...] += jnp.sum(dy * nrm, axis=0, keepdims=True)
        dn = dy * gf
        dh2_ref[...] = rstd * (dn - nrm * jnp.mean(dn * nrm, axis=-1, keepdims=True))

    rows = lambda w: pl.BlockSpec((r, w), lambda i: (i, 0))
    return pl.pallas_call(
        body, name="ffn_down_loss", grid=(t // r,),
        in_specs=[rows(D_FF), _const_spec((D_FF, D)), rows(D), rows(D), _const_spec((1, D)), rows(1)],
        out_specs=[rows(D), _acc_spec((1, 128)), _acc_spec((1, D))],
        out_shape=[jax.ShapeDtypeStruct((t, D), F32), jax.ShapeDtypeStruct((1, 128), F32),
                   jax.ShapeDtypeStruct((1, D), F32)],
        compiler_params=_params(1),
    )(act, w_down, h1, target, g_final, row_mask)


def _ffn_bwd(dh2, gate, up, h1, w_down, w_gate_t, w_up_t, w_out, g_ffn):
    t = h1.shape[0]
    r = _row_tile(t, FFN_TILE)

    def body(dh2_ref, gate_ref, up_ref, h1_ref, wd_ref, wg_ref, wu_ref, wo_ref, g_ref,
             dgate_ref, dup_ref, dh1_ref, dycat_ref, dg_ref):
        @pl.when(pl.program_id(0) == 0)
        def _():
            dg_ref[...] = jnp.zeros_like(dg_ref)

        dh2 = dh2_ref[...]
        dact = _dot_nt(dh2.astype(BF16), wd_ref[...])
        gate = gate_ref[...]
        sg = _sigmoid(gate)
        dgate = (dact * up_ref[...] * (sg * (1.0 + gate * (1.0 - sg)))).astype(BF16)
        dup = (dact * (gate * sg)).astype(BF16)
        dgate_ref[...] = dgate
        dup_ref[...] = dup
        dhn = _dot(dgate, wg_ref[...]) + _dot(dup, wu_ref[...])
        h1 = h1_ref[...]
        rstd = lax.rsqrt(jnp.mean(h1 * h1, axis=-1, keepdims=True) + RMS_EPS)
        nrm = h1 * rstd
        dg_ref[...] += jnp.sum(dhn * nrm, axis=0, keepdims=True)
        dn = dhn * g_ref[...]
        dh1 = dh2 + rstd * (dn - nrm * jnp.mean(dn * nrm, axis=-1, keepdims=True))
        dh1_ref[...] = dh1
        dycat_ref[...] = _dot_nt(dh1.astype(BF16), wo_ref[...])

    rows = lambda w: pl.BlockSpec((r, w), lambda i: (i, 0))
    return pl.pallas_call(
        body, name="ffn_bwd", grid=(t // r,),
        in_specs=[rows(D), rows(D_FF), rows(D_FF), rows(D), _const_spec((D_FF, D)), _const_spec((D_FF, D)),
                  _const_spec((D_FF, D)), _const_spec((D, D)), _const_spec((1, D))],
        out_specs=[rows(D_FF), rows(D_FF), rows(D), rows(D), _acc_spec((1, D))],
        out_shape=[jax.ShapeDtypeStruct((t, D_FF), BF16), jax.ShapeDtypeStruct((t, D_FF), BF16),
                   jax.ShapeDtypeStruct((t, D), F32), jax.ShapeDtypeStruct((t, D), F32),
                   jax.ShapeDtypeStruct((1, D), F32)],
        compiler_params=_params(1),
    )(dh2, gate, up, h1, w_down, w_gate_t, w_up_t, w_out, g_ffn)


def _conv_bwd(dycat, yc, u, conv_w, ln_g, ln_b, n_ex, lp, plan=None):
    r = CONV_TILE
    nt = lp // r
    hb = r // CONV_SUB
    nsub = r // CONV_SUB

    def ln_bwd(dy, yc_rows, live, lg, lb):
        mu = jnp.mean(yc_rows, axis=-1, keepdims=True)
        cen = yc_rows - mu
        rs = lax.rsqrt(jnp.mean(cen * cen, axis=-1, keepdims=True) + LN_EPS)
        yn = cen * rs
        out = yn * lg + lb
        so = _sigmoid(out)
        dout = jnp.where(live, dy * (so * (1.0 + out * (1.0 - so))), 0.0)
        dyn = dout * lg
        dyc = rs * (dyn - jnp.mean(dyn, axis=-1, keepdims=True) - yn * jnp.mean(dyn * yn, axis=-1, keepdims=True))
        return dyc, dout, yn

    def body(dy_ref, dyn_ref, yc_ref, ycn_ref, cur_ref, prev_ref, w_ref, lg_ref, lb_ref,
             du_ref, dw_ref, db_ref, dlg_ref, dlb_ref, glu, dycs, dwacc):
        b = pl.program_id(0)
        i = pl.program_id(1)
        first = jnp.logical_and(b == 0, i == 0)

        @pl.when(first)
        def _():
            dwacc[...] = jnp.zeros_like(dwacc)
            db_ref[...] = jnp.zeros_like(db_ref)
            dlg_ref[...] = jnp.zeros_like(dlg_ref)
            dlb_ref[...] = jnp.zeros_like(dlb_ref)

        lg, lb = lg_ref[...], lb_ref[...]
        cur = cur_ref[...]
        sig = _sigmoid(cur[:, C_CONV:])
        glu[CONV_SUB:CONV_SUB + r, :] = cur[:, :C_CONV] * sig
        pv = prev_ref[...]
        glu[0:CONV_SUB, :] = jnp.where(i > 0, pv[:, :C_CONV] * _sigmoid(pv[:, C_CONV:]), 0.0)

        row = i * r + lax.broadcasted_iota(jnp.int32, (r, 1), 0)
        dyc, dout, yn = ln_bwd(dy_ref[...], yc_ref[...], row >= PAD_ROWS, lg, lb)
        dycs[0:r, :] = dyc
        dycn, _, _ = ln_bwd(dyn_ref[...], ycn_ref[...], i < nt - 1, lg, lb)
        dycs[r:r + CONV_SUB, :] = dycn
        db_ref[...] += jnp.sum(dyc, axis=0, keepdims=True)
        dlg_ref[...] += jnp.sum(dout * yn, axis=0, keepdims=True)
        dlb_ref[...] += jnp.sum(dout, axis=0, keepdims=True)

        w = w_ref[...]
        for j in range(nsub):
            r0 = j * CONV_SUB
            dblk = dycs[r0:r0 + CONV_SUB, :]
            dglu = jnp.zeros((CONV_SUB, C_CONV), F32)
            for k in range(CONV_W):
                dglu = dglu + w[k:k + 1, :] * dycs[r0 + (CONV_W - 1) - k:r0 + (CONV_W - 1) - k + CONV_SUB, :]
                prod = dblk * glu[r0 + CONV_LEAD + k:r0 + CONV_LEAD + k + CONV_SUB, :]
                dwacc[k] += prod.reshape(CONV_SUB // 8, 8, C_CONV).sum(axis=0)
            sg = sig[r0:r0 + CONV_SUB, :]
            cv = cur[r0:r0 + CONV_SUB, :C_CONV]
            du_ref[r0:r0 + CONV_SUB, :C_CONV] = (dglu * sg).astype(BF16)
            du_ref[r0:r0 + CONV_SUB, C_CONV:] = (dglu * cv * sg * (1.0 - sg)).astype(BF16)

        @pl.when(jnp.logical_and(b == n_ex - 1, i == nt - 1))
        def _():
            dw_ref[...] = jnp.sum(dwacc[...], axis=1)

    t = n_ex * lp
    cur_rows = lambda w, col: pl.BlockSpec((r, w), lambda b, i: (b * nt + i, col))
    nxt_rows = lambda w, col: pl.BlockSpec(
        (CONV_SUB, w), lambda b, i: (jnp.minimum((b * nt + i + 1) * hb, n_ex * nt * hb - 1), col))
    return _call(
        body, name="conv_bwd", grid=(n_ex, nt),
        in_specs=[cur_rows(C_CONV, 0), nxt_rows(C_CONV, 0), cur_rows(C_CONV, 0), nxt_rows(C_CONV, 0),
                  cur_rows(2 * C_CONV, 0),
                  pl.BlockSpec((CONV_SUB, 2 * C_CONV), lambda b, i: (jnp.maximum((b * nt + i) * hb - 1, 0), 0)),
                  _const_spec((32, C_CONV)), _const_spec((1, C_CONV)), _const_spec((1, C_CONV))],
        out_specs=[cur_rows(2 * C_CONV, 0), _acc_spec((32, C_CONV)), _acc_spec((1, C_CONV)),
                   _acc_spec((1, C_CONV)), _acc_spec((1, C_CONV))],
        out_shape=[jax.ShapeDtypeStruct((t, 2 * C_CONV), BF16), jax.ShapeDtypeStruct((32, C_CONV), F32),
                   jax.ShapeDtypeStruct((1, C_CONV), F32), jax.ShapeDtypeStruct((1, C_CONV), F32),
                   jax.ShapeDtypeStruct((1, C_CONV), F32)],
        scratch_shapes=[pltpu.VMEM((r + CONV_SUB, C_CONV), F32), pltpu.VMEM((r + CONV_SUB, C_CONV), F32),
                        pltpu.VMEM((32, 8, C_CONV), F32)],
        plan=plan,
    )(dycat, dycat, yc, yc, u, u, conv_w, ln_g, ln_b)


def _gla_bwd(dycat, u, states, w2, gb, ng, n_ex, lp, plan=None):
    nc = lp // CHUNK
    t = n_ex * lp

    def body(dy_ref, qk_ref, v_ref, g_ref, lr_ref, st_ref, w2_ref, gb_ref, ng_ref,
             du_ref, dw2_ref, dgb_ref, dng_ref, dstate):
        bi = pl.program_id(0)
        n = pl.program_id(1)
        chunk = nc - 1 - n

        @pl.when(jnp.logical_and(bi == 0, n == 0))
        def _():
            dw2_ref[...] = jnp.zeros_like(dw2_ref)
            dgb_ref[...] = jnp.zeros_like(dgb_ref)
            dng_ref[...] = jnp.zeros_like(dng_ref)

        @pl.when(n == 0)
        def _():
            dstate[...] = jnp.zeros_like(dstate)

        qk = qk_ref[...]
        q, k = qk[:, :GLA_K], qk[:, GLA_K:]
        lr = lr_ref[...]
        z, a, live = _gla_gates(lr, w2_ref[...], gb_ref[...], chunk == 0)
        causal = _tri(True)
        b = jnp.dot(causal.astype(F32), a, preferred_element_type=F32, precision=lax.Precision.HIGHEST)
        bl = b[CHUNK - 1:CHUNK, :]
        e_pos, e_neg, e_dec = jnp.exp(b), jnp.exp(-b), jnp.exp(bl - b)
        q_f = q * (DK ** -0.5) * e_pos
        k_f = k * e_neg
        kd_f = k * e_dec
        q_in, k_in, k_dec = q_f.astype(BF16), k_f.astype(BF16), kd_f.astype(BF16)
        decay = jnp.exp(bl)
        v = v_ref[...]
        g = g_ref[...]
        dy = dy_ref[...]
        ngv = ng_ref[...]
        st = st_ref[...]
        st_b = st.astype(BF16)
        dst = dstate[...]
        dst_b = dst.astype(BF16)
        dqs, dks, dvs, dgs, dbs, dbls, new_dst = [], [], [], [], [], [], []
        dng = jnp.zeros((1, DV), F32)
        for h in range(N_HEADS):
            ks = slice(h * DK, (h + 1) * DK)
            vs = slice(h * DV, (h + 1) * DV)
            qh, kh, kdh = q_in[:, ks], k_in[:, ks], k_dec[:, ks]
            vh = v[:, vs].astype(BF16)
            s = jnp.where(causal, _dot_nt(qh, kh), 0.0).astype(BF16)
            o = _dot(s, vh) + _dot_nt(qh, st_b[:, ks])
            rstd = lax.rsqrt(jnp.mean(o * o, axis=-1, keepdims=True) + RMS_EPS)
            nrm = o * rstd
            gh = g[:, vs]
            sg = _sigmoid(gh)
            dyh = dy[:, vs]
            dgs.append(dyh * nrm * ngv * (sg * (1.0 + gh * (1.0 - sg))))
            dt = dyh * (gh * sg)
            dng = dng + jnp.sum(dt * nrm, axis=0, keepdims=True)
            dn = dt * ngv
            do = (rstd * (dn - nrm * jnp.mean(dn * nrm, axis=-1, keepdims=True))).astype(BF16)
            da = jnp.where(causal, _dot_nt(do, vh), 0.0).astype(BF16)
            dvs.append(_dot_tn(s, do) + _dot_nt(kdh, dst_b[:, ks]))
            dq_in = _dot(da, kh) + _dot(do, st_b[:, ks])
            dk_in = _dot_tn(da, qh)
            dk_dec = _dot(vh, dst_b[:, ks])
            new_dst.append(_dot_tn(do, qh) + decay[:, ks] * dst[:, ks])
            dbls.append(jnp.sum(dk_dec * kd_f[:, ks], axis=0, keepdims=True)
                        + decay[:, ks] * jnp.sum(dst[:, ks] * st[:, ks], axis=0, keepdims=True))
            dqs.append(dq_in * (DK ** -0.5) * e_pos[:, ks])
            dks.append(dk_in * e_neg[:, ks] + dk_dec * e_dec[:, ks])
            dbs.append(dq_in * q_f[:, ks] - dk_in * k_f[:, ks] - dk_dec * kd_f[:, ks])
        dstate[...] = jnp.concatenate(new_dst, axis=1)
        row = lax.broadcasted_iota(jnp.int32, (CHUNK, 1), 0)
        db = jnp.concatenate(dbs, axis=1) + jnp.where(row == CHUNK - 1, jnp.concatenate(dbls, axis=1), 0.0)
        da_log = jnp.dot(_tri(False).astype(F32), db, preferred_element_type=F32, precision=lax.Precision.HIGHEST)
        dz = jnp.where(live, da_log * (1.0 - _sigmoid(z)) * (1.0 / GATE_TAU), 0.0)
        dz_b = dz.astype(BF16)
        du_ref[:, 0:GLA_K] = jnp.concatenate(dqs, axis=1).astype(BF16)
        du_ref[:, GLA_K:2 * GLA_K] = jnp.concatenate(dks, axis=1).astype(BF16)
        du_ref[:, 2 * GLA_K:2 * GLA_K + GLA_V] = jnp.concatenate(dvs, axis=1).astype(BF16)
        du_ref[:, 2 * GLA_K + GLA_V:2 * GLA_K + 2 * GLA_V] = jnp.concatenate(dgs, axis=1).astype(BF16)
        du_ref[:, 2 * GLA_K + 2 * GLA_V:] = _dot_nt(dz_b, w2_ref[...]).astype(BF16)
        dw2_ref[...] += _dot_tn(lr.astype(BF16), dz_b)
        dgb_ref[...] += jnp.sum(dz, axis=0, keepdims=True)
        dng_ref[...] += dng

    rev = lambda w, col: pl.BlockSpec((CHUNK, w), lambda b, n: (b * nc + nc - 1 - n, col))
    return _call(
        body, name="gla_bwd", grid=(n_ex, nc),
        in_specs=[rev(GLA_V, 1), rev(2 * GLA_K, 2), rev(GLA_V, 3), rev(GLA_V, 4), rev(128, 20),
                  pl.BlockSpec((DV, GLA_K), lambda b, n: (b * nc + nc - 1 - n, 0)),
                  _const_spec((128, GLA_K)), _const_spec((1, GLA_K)), _const_spec((1, DV))],
        out_specs=[rev(D_GLA_IN, 0), _acc_spec((128, GLA_K)), _acc_spec((1, GLA_K)), _acc_spec((1, DV))],
        out_shape=[jax.ShapeDtypeStruct((t, D_GLA_IN), BF16), jax.ShapeDtypeStruct((128, GLA_K), F32),
                   jax.ShapeDtypeStruct((1, GLA_K), F32), jax.ShapeDtypeStruct((1, DV), F32)],
        scratch_shapes=[pltpu.VMEM((DV, GLA_K), F32)],
        plan=plan,
    )(dycat, u, u, u, u, states, w2, gb, ng)


def _in_proj_bwd(du_conv, du_gla, w_in_conv, w_in_gla, h0, dh1, g_mix, plan=None):
    t = h0.shape[0]
    r = _row_tile(t, 384)

    def body(dc_ref, dg_ref, wc_ref, wg_ref, h_ref, dh1_ref, g_ref, dh0_ref, dgm_ref):
        @pl.when(pl.program_id(0) == 0)
        def _():
            dgm_ref[...] = jnp.zeros_like(dgm_ref)

        dhn = _dot_nt(dc_ref[...], wc_ref[...]) + _dot_nt(dg_ref[...], wg_ref[...])
        h = h_ref[...]
        rstd = lax.rsqrt(jnp.mean(h * h, axis=-1, keepdims=True) + RMS_EPS)
        nrm = h * rstd
        dgm_ref[...] += jnp.sum(dhn * nrm, axis=0, keepdims=True)
        dn = dhn * g_ref[...]
        dh0_ref[...] = dh1_ref[...] + rstd * (dn - nrm * jnp.mean(dn * nrm, axis=-1, keepdims=True))

    rows = lambda w: pl.BlockSpec((r, w), lambda i: (i, 0))
    return _call(
        body, name="in_proj_bwd", grid=(t // r,),
        in_specs=[rows(2 * C_CONV), rows(D_GLA_IN), _const_spec((D, 2 * C_CONV)), _const_spec((D, D_GLA_IN)),
                  rows(D), rows(D), _const_spec((1, D))],
        out_specs=[rows(D), _acc_spec((1, D))],
        out_shape=[jax.ShapeDtypeStruct((t, D), F32), jax.ShapeDtypeStruct((1, D), F32)],
        plan=plan,
    )(du_conv, du_gla, w_in_conv, w_in_gla, h0, dh1, g_mix)


def _wgrad(x, dy, name):
    t, m = x.shape
    n = dy.shape[1]
    tk = _row_tile(t, 384)
    tm = m if m <= D_GLA_IN else m // 2
    tn = n

    def body(x_ref, dy_ref, o_ref):
        @pl.when(pl.program_id(2) == 0)
        def _():
            o_ref[...] = jnp.zeros_like(o_ref)

        o_ref[...] += _dot_tn(x_ref[...].astype(BF16), dy_ref[...].astype(BF16))

    return pl.pallas_call(
        body, name=name, grid=(m // tm, n // tn, t // tk),
        in_specs=[pl.BlockSpec((tk, tm), lambda i, j, k: (k, i)), pl.BlockSpec((tk, tn), lambda i, j, k: (k, j))],
        out_specs=pl.BlockSpec((tm, tn), lambda i, j, k: (i, j)),
        out_shape=jax.ShapeDtypeStruct((m, n), F32),
        compiler_params=_params(3),
    )(x, dy)


def _mesh_pos():
    return lax.axis_index("x"), lax.axis_index("y"), lax.axis_index("c")


def _other_chips(x, y):
    return [(1 - x, y), (x, 1 - y), (1 - x, 1 - y)]


HBM_SPEC = pl.BlockSpec(memory_space=pltpu.HBM)


def _gather_shards(shards):
    n = len(shards)

    def body(*refs):
        ins, outs = refs[:n], refs[n:2 * n]
        send_sems, recv_sems, local_sems = refs[2 * n:]
        x, y, c = _mesh_pos()
        mine = 2 * x + y
        chips = _other_chips(x, y)
        local = [pltpu.make_async_copy(ins[a], outs[a].at[mine], local_sems.at[a]) for a in range(n)]
        for cp in local:
            cp.start()

        def remote(a, k, block):
            px, py = chips[k]
            return pltpu.make_async_remote_copy(
                src_ref=ins[a], dst_ref=outs[a].at[block], send_sem=send_sems.at[3 * a + k],
                recv_sem=recv_sems.at[3 * a + k], device_id=(px, py, c), device_id_type=MESH)

        sends = [remote(a, k, mine) for a in range(n) for k in range(3)]
        for cp in sends:
            cp.start()
        for a in range(n):
            for k, (px, py) in enumerate(chips):
                remote(a, k, 2 * px + py).wait_recv()
        for cp in sends:
            cp.wait_send()
        for cp in local:
            cp.wait()

    return pl.pallas_call(
        body, name="gather_shards",
        in_specs=[HBM_SPEC] * n, out_specs=[HBM_SPEC] * n,
        out_shape=[jax.ShapeDtypeStruct((N_CHIPS,) + s.shape, s.dtype) for s in shards],
        scratch_shapes=[pltpu.SemaphoreType.DMA((3 * n,)), pltpu.SemaphoreType.DMA((3 * n,)),
                        pltpu.SemaphoreType.DMA((n,))],
        compiler_params=pltpu.CompilerParams(has_side_effects=True),
    )(*shards)


def _send_half_to_sibling(g2):
    def body(g_ref, recv_ref, send_sem, recv_sem):
        x, y, c = _mesh_pos()
        cp = pltpu.make_async_remote_copy(
            src_ref=g_ref.at[1 - c], dst_ref=recv_ref, send_sem=send_sem, recv_sem=recv_sem,
            device_id=(x, y, 1 - c), device_id_type=MESH)
        cp.start()
        cp.wait()

    return pl.pallas_call(
        body, name="rs_to_sibling", in_specs=[HBM_SPEC], out_specs=HBM_SPEC,
        out_shape=jax.ShapeDtypeStruct(g2.shape[1:], g2.dtype),
        scratch_shapes=[pltpu.SemaphoreType.DMA(()), pltpu.SemaphoreType.DMA(())],
        compiler_params=pltpu.CompilerParams(has_side_effects=True),
    )(g2)


def _add_own_half(g2, recv, c):
    rows = N_CHIPS * HALF_ROWS
    tr = 512
    g2f = g2.reshape(2, rows, D)
    recvf = recv.reshape(rows, D)

    def body(c_ref, a_ref, b_ref, o_ref):
        o_ref[...] = a_ref[0] + b_ref[...]

    out = pl.pallas_call(
        body, name="rs_add_halves",
        grid_spec=pltpu.PrefetchScalarGridSpec(
            num_scalar_prefetch=1, grid=(rows // tr,),
            in_specs=[pl.BlockSpec((1, tr, D), lambda i, s: (s[0], i, 0)), pl.BlockSpec((tr, D), lambda i, s: (i, 0))],
            out_specs=pl.BlockSpec((tr, D), lambda i, s: (i, 0))),
        out_shape=jax.ShapeDtypeStruct((rows, D), F32),
        compiler_params=_params(1),
    )(jnp.reshape(c, (1,)).astype(jnp.int32), g2f, recvf)
    return out.reshape(N_CHIPS, HALF_ROWS, D)


def _exchange_chip_sums(p):
    def body(p_ref, out_ref, send_sems, recv_sems, local_sem):
        x, y, c = _mesh_pos()
        mine = 2 * x + y
        chips = _other_chips(x, y)
        local = pltpu.make_async_copy(p_ref.at[mine], out_ref.at[mine], local_sem)
        local.start()

        def remote(k, src_block, dst_block):
            px, py = chips[k]
            return pltpu.make_async_remote_copy(
                src_ref=p_ref.at[src_block], dst_ref=out_ref.at[dst_block], send_sem=send_sems.at[k],
                recv_sem=recv_sems.at[k], device_id=(px, py, c), device_id_type=MESH)

        sends = [remote(k, 2 * px + py, mine) for k, (px, py) in enumerate(chips)]
        for cp in sends:
            cp.start()
        for k, (px, py) in enumerate(chips):
            remote(k, mine, 2 * px + py).wait_recv()
        for cp in sends:
            cp.wait_send()
        local.wait()

    return pl.pallas_call(
        body, name="rs_chip_exchange", in_specs=[HBM_SPEC], out_specs=HBM_SPEC,
        out_shape=jax.ShapeDtypeStruct(p.shape, p.dtype),
        scratch_shapes=[pltpu.SemaphoreType.DMA((3,)), pltpu.SemaphoreType.DMA((3,)), pltpu.SemaphoreType.DMA(())],
        compiler_params=pltpu.CompilerParams(has_side_effects=True),
    )(p)


def _sum_chips(parts):
    tr = 512

    def body(p_ref, o_ref):
        o_ref[...] = ((p_ref[0] + p_ref[1]) + p_ref[2]) + p_ref[3]

    return pl.pallas_call(
        body, name="rs_sum_chips", grid=(HALF_ROWS // tr,),
        in_specs=[pl.BlockSpec((N_CHIPS, tr, D), lambda i: (0, i, 0))],
        out_specs=pl.BlockSpec((tr, D), lambda i: (i, 0)),
        out_shape=jax.ShapeDtypeStruct((HALF_ROWS, D), F32),
        compiler_params=_params(1),
    )(parts)


def _share_with_sibling(half):
    def body(h_ref, out_ref, send_sem, recv_sem, local_sem):
        x, y, c = _mesh_pos()
        local = pltpu.make_async_copy(h_ref, out_ref.at[c], local_sem)
        local.start()
        cp = pltpu.make_async_remote_copy(
            src_ref=h_ref, dst_ref=out_ref.at[c], send_sem=send_sem, recv_sem=recv_sem,
            device_id=(x, y, 1 - c), device_id_type=MESH)
        cp.start()
        pltpu.make_async_remote_copy(
            src_ref=h_ref, dst_ref=out_ref.at[1 - c], send_sem=send_sem, recv_sem=recv_sem,
            device_id=(x, y, 1 - c), device_id_type=MESH).wait_recv()
        cp.wait_send()
        local.wait()

    return pl.pallas_call(
        body, name="rs_share_sibling", in_specs=[HBM_SPEC], out_specs=HBM_SPEC,
        out_shape=jax.ShapeDtypeStruct((2,) + half.shape, half.dtype),
        scratch_shapes=[pltpu.SemaphoreType.DMA(()), pltpu.SemaphoreType.DMA(()), pltpu.SemaphoreType.DMA(())],
        compiler_params=pltpu.CompilerParams(has_side_effects=True),
    )(half)


def _adam_update(g, w, m, v):
    m2 = ADAM_B1 * m + (1.0 - ADAM_B1) * g
    v2 = ADAM_B2 * v + (1.0 - ADAM_B2) * (g * g)
    m_hat = m2 / (1.0 - ADAM_B1 ** ADAM_STEP)
    v_hat = v2 / (1.0 - ADAM_B2 ** ADAM_STEP)
    delta = -ADAM_LR * (m_hat / (jnp.sqrt(v_hat) + ADAM_EPS) + ADAM_WD * w)
    return delta, m2, v2


def _adamw_slab(g, w, m, v):
    rows = g.shape[0]
    tr = 256

    def body(g_ref, w_ref, m_ref, v_ref, d_ref, m2_ref, v2_ref):
        d_ref[...], m2_ref[...], v2_ref[...] = _adam_update(g_ref[...], w_ref[...], m_ref[...], v_ref[...])

    spec = pl.BlockSpec((tr, D), lambda i: (i, 0))
    return pl.pallas_call(
        body, name="adamw_slab", grid=(rows // tr,), in_specs=[spec] * 4, out_specs=[spec] * 3,
        out_shape=[jax.ShapeDtypeStruct((rows, D), F32)] * 3,
        compiler_params=_params(1),
    )(g, w, m, v)


def _allreduce_small_adamw(part, w, m, v):
    def body(p_ref, w_ref, m_ref, v_ref, g_ref, d_ref, m2_ref, v2_ref, slots, send_sems, recv_sems):
        x, y, c = _mesh_pos()
        mine = 4 * x + 2 * y + c
        peers = [(px, py, pc) for px in (x, 1 - x) for py in (y, 1 - y) for pc in (c, 1 - c)][1:]

        def remote(k, slot):
            return pltpu.make_async_remote_copy(
                src_ref=p_ref, dst_ref=slots.at[slot], send_sem=send_sems.at[k], recv_sem=recv_sems.at[k],
                device_id=peers[k], device_id_type=MESH)

        sends = [remote(k, mine) for k in range(7)]
        for cp in sends:
            cp.start()
        slots[mine] = p_ref[...]
        for k, (px, py, pc) in enumerate(peers):
            remote(k, 4 * px + 2 * py + pc).wait_recv()
        for cp in sends:
            cp.wait_send()
        g = slots[0]
        for d in range(1, 8):
            g = g + slots[d]
        g_ref[...] = g
        d_ref[...], m2_ref[...], v2_ref[...] = _adam_update(g, w_ref[...], m_ref[...], v_ref[...])

    vm = pl.BlockSpec(memory_space=pltpu.VMEM)
    shape = jax.ShapeDtypeStruct(part.shape, F32)
    return pl.pallas_call(
        body, name="small_allreduce_adamw", in_specs=[vm] * 4, out_specs=[vm] * 4, out_shape=[shape] * 4,
        scratch_shapes=[pltpu.VMEM((8,) + part.shape, F32), pltpu.SemaphoreType.DMA((7,)),
                        pltpu.SemaphoreType.DMA((7,))],
        compiler_params=pltpu.CompilerParams(has_side_effects=True),
    )(part, w, m, v)


def _half(ref, c, axis):
    n = ref.shape[axis] // 2
    return ref.at[(slice(None),) * axis + (pl.ds(c * n, n),)]


def _remote(src, dst, send_sem, recv_sem, device):
    return pltpu.make_async_remote_copy(src_ref=src, dst_ref=dst, send_sem=send_sem, recv_sem=recv_sem,
                                        device_id=device, device_id_type=MESH)


def _gather_weights(split, axes, whole):
    ns, n = len(split), len(split) + len(whole)

    def body(*refs):
        ins, outs = refs[:n], refs[n:2 * n]
        ici_send, ici_recv, d2d_send, d2d_recv, local_sems = refs[2 * n:]
        x, y, c = _mesh_pos()
        mine = 2 * x + y
        chips = _other_chips(x, y)
        local = [pltpu.make_async_copy(ins[a], outs[a].at[mine], local_sems.at[a]) for a in range(n)]
        for cp in local:
            cp.start()

        def ici(a, k, block):
            px, py = chips[k]
            src, dst = ins[a], outs[a].at[block]
            if a < ns:
                src, dst = _half(src, c, axes[a]), _half(dst, c, axes[a])
            return _remote(src, dst, ici_send.at[3 * a + k], ici_recv.at[3 * a + k], (px, py, c))

        def d2d(a, k, block, half):
            part = _half(outs[a].at[block], half, axes[a])
            return _remote(part, part, d2d_send.at[3 * a + k], d2d_recv.at[3 * a + k], (x, y, 1 - c))

        sends = [ici(a, k, mine) for a in range(n) for k in range(3)]
        for cp in sends:
            cp.start()
        for a in range(n):
            for k, (px, py) in enumerate(chips):
                ici(a, k, 2 * px + py).wait_recv()
                if a < ns:
                    sends.append(d2d(a, k, 2 * px + py, c))
                    sends[-1].start()
        for a in range(ns):
            for k, (px, py) in enumerate(chips):
                d2d(a, k, 2 * px + py, 1 - c).wait_recv()
        for cp in sends:
            cp.wait_send()
        for cp in local:
            cp.wait()

    arrays = list(split) + list(whole)
    return pl.pallas_call(
        body, name="gather_weights", in_specs=[HBM_SPEC] * n, out_specs=[HBM_SPEC] * n,
        out_shape=[jax.ShapeDtypeStruct((N_CHIPS,) + s.shape, s.dtype) for s in arrays],
        scratch_shapes=[pltpu.SemaphoreType.DMA((3 * n,)), pltpu.SemaphoreType.DMA((3 * n,)),
                        pltpu.SemaphoreType.DMA((3 * ns,)), pltpu.SemaphoreType.DMA((3 * ns,)),
                        pltpu.SemaphoreType.DMA((n,))],
        compiler_params=pltpu.CompilerParams(has_side_effects=True),
    )(*arrays)


def _rs_to_sibling(gs):
    n = len(gs)

    def body(*refs):
        ins, outs, send_sems, recv_sems = refs[:n], refs[n:2 * n], refs[2 * n], refs[2 * n + 1]
        x, y, c = _mesh_pos()
        copies = [_remote(_half(ins[a], 1 - c, 2), outs[a], send_sems.at[a], recv_sems.at[a], (x, y, 1 - c))
                  for a in range(n)]
        for cp in copies:
            cp.start()
        for cp in copies:
            cp.wait()

    return pl.pallas_call(
        body, name="rs_to_sibling", in_specs=[HBM_SPEC] * n, out_specs=[HBM_SPEC] * n,
        out_shape=[jax.ShapeDtypeStruct(g.shape[:2] + (g.shape[2] // 2,), g.dtype) for g in gs],
        scratch_shapes=[pltpu.SemaphoreType.DMA((n,)), pltpu.SemaphoreType.DMA((n,))],
        compiler_params=pltpu.CompilerParams(has_side_effects=True),
    )(*gs)


def _rs_add_halves(g, recv, c, name):
    _, rows, w = g.shape
    h = w // 2
    tr = rows // 2 if rows % 16 == 0 and rows > 64 else rows

    def body(c_ref, a_ref, b_ref, o_ref):
        o_ref[...] = (a_ref[...] + b_ref[...]).astype(BF16)

    return pl.pallas_call(
        body, name=name,
        grid_spec=pltpu.PrefetchScalarGridSpec(
            num_scalar_prefetch=1, grid=(N_CHIPS, rows // tr),
            in_specs=[pl.BlockSpec((1, tr, h), lambda j, i, s: (j, i, s[0])),
                      pl.BlockSpec((1, tr, h), lambda j, i, s: (j, i, 0))],
            out_specs=pl.BlockSpec((1, tr, h), lambda j, i, s: (j, i, 0))),
        out_shape=jax.ShapeDtypeStruct((N_CHIPS, rows, h), BF16),
        compiler_params=_params(2),
    )(jnp.reshape(c, (1,)).astype(jnp.int32), g, recv)


def _rs_chip_exchange(ps):
    n = len(ps)

    def body(*refs):
        ins, outs = refs[:n], refs[n:2 * n]
        send_sems, recv_sems, local_sems = refs[2 * n:]
        x, y, c = _mesh_pos()
        mine = 2 * x + y
        chips = _other_chips(x, y)
        local = [pltpu.make_async_copy(ins[a].at[mine], outs[a].at[mine], local_sems.at[a]) for a in range(n)]
        for cp in local:
            cp.start()

        def ici(a, k, src_block, dst_block):
            px, py = chips[k]
            return _remote(ins[a].at[src_block], outs[a].at[dst_block], send_sems.at[3 * a + k],
                           recv_sems.at[3 * a + k], (px, py, c))

        sends = [ici(a, k, 2 * px + py, mine) for a in range(n) for k, (px, py) in enumerate(chips)]
        for cp in sends:
            cp.start()
        for a in range(n):
            for k, (px, py) in enumerate(chips):
                ici(a, k, mine, 2 * px + py).wait_recv()
        for cp in sends:
            cp.wait_send()
        for cp in local:
            cp.wait()

    return pl.pallas_call(
        body, name="rs_chip_exchange", in_specs=[HBM_SPEC] * n, out_specs=[HBM_SPEC] * n,
        out_shape=[jax.ShapeDtypeStruct(p.shape, p.dtype) for p in ps],
        scratch_shapes=[pltpu.SemaphoreType.DMA((3 * n,)), pltpu.SemaphoreType.DMA((3 * n,)),
                        pltpu.SemaphoreType.DMA((n,))],
        compiler_params=pltpu.CompilerParams(has_side_effects=True),
    )(*ps)


def _rs_sum_chips(parts, name):
    _, rows, h = parts.shape
    tr = rows // 2 if rows % 16 == 0 and rows > 64 else rows

    def body(p_ref, o_ref):
        p = p_ref[...].astype(F32)
        o_ref[...] = ((p[0] + p[1]) + p[2]) + p[3]

    return pl.pallas_call(
        body, name=name, grid=(rows // tr,),
        in_specs=[pl.BlockSpec((N_CHIPS, tr, h), lambda i: (0, i, 0))],
        out_specs=pl.BlockSpec((tr, h), lambda i: (i, 0)),
        out_shape=jax.ShapeDtypeStruct((rows, h), F32),
        compiler_params=_params(1),
    )(parts)


def _rs_share(halves):
    n = len(halves)

    def body(*refs):
        ins, outs = refs[:n], refs[n:2 * n]
        send_sems, recv_sems, local_sems = refs[2 * n:]
        x, y, c = _mesh_pos()
        local = [pltpu.make_async_copy(ins[a], _half(outs[a], c, 1), local_sems.at[a]) for a in range(n)]
        for cp in local:
            cp.start()
        sends = [_remote(ins[a], _half(outs[a], c, 1), send_sems.at[a], recv_sems.at[a], (x, y, 1 - c))
                 for a in range(n)]
        for cp in sends:
            cp.start()
        for a in range(n):
            _remote(ins[a], _half(outs[a], 1 - c, 1), send_sems.at[a], recv_sems.at[a], (x, y, 1 - c)).wait_recv()
        for cp in sends:
            cp.wait_send()
        for cp in local:
            cp.wait()

    return pl.pallas_call(
        body, name="rs_share", in_specs=[HBM_SPEC] * n, out_specs=[HBM_SPEC] * n,
        out_shape=[jax.ShapeDtypeStruct((p.shape[0], 2 * p.shape[1]), p.dtype) for p in halves],
        scratch_shapes=[pltpu.SemaphoreType.DMA((n,)), pltpu.SemaphoreType.DMA((n,)),
                        pltpu.SemaphoreType.DMA((n,))],
        compiler_params=pltpu.CompilerParams(has_side_effects=True),
    )(*halves)


def _adamw(g, w, m, v, name):
    rows, cols = g.shape
    tr = 256 if rows % 256 == 0 else (rows // 2 if rows % 16 == 0 and rows > 64 else rows)

    def body(g_ref, w_ref, m_ref, v_ref, d_ref, m2_ref, v2_ref):
        d_ref[...], m2_ref[...], v2_ref[...] = _adam_update(g_ref[...], w_ref[...], m_ref[...], v_ref[...])

    spec = pl.BlockSpec((tr, cols), lambda i: (i, 0))
    return pl.pallas_call(
        body, name=name, grid=(rows // tr,), in_specs=[spec] * 4, out_specs=[spec] * 3,
        out_shape=[jax.ShapeDtypeStruct((rows, cols), F32)] * 3,
        compiler_params=_params(1),
    )(g, w, m, v)


def _rows_of(a):
    flat = a.reshape(-1)
    pad = (-flat.shape[0]) % D
    if pad:
        flat = jnp.concatenate([flat, jnp.zeros((pad,), flat.dtype)])
    return flat.reshape(-1, D)


SLAB_PARTS = (("w_in", (D, D_IN // N_CHIPS)), ("w_out", (D // N_CHIPS, D)), ("w_ffn_gate", (D, D_FF // N_CHIPS)),
              ("w_ffn_up", (D, D_FF // N_CHIPS)), ("w_ffn_down", (D_FF // N_CHIPS, D)),
              ("meta_tokens", (N_META, D // N_CHIPS)), ("conv_w", (CONV_W, C_CONV // N_CHIPS)),
              ("gla_w_gate2", (RANK, GLA_K // N_CHIPS)))


def _pack_slab(parts):
    rows = [_rows_of(parts[name].reshape(shape)) for name, shape in SLAB_PARTS]
    used = sum(r.shape[0] for r in rows)
    rows.append(jnp.zeros((SLAB_ROWS - used, D), F32))
    return jnp.concatenate(rows, axis=0)


def _unpack_slab(slab, lead):
    out, r0 = {}, 0
    for name, shape in SLAB_PARTS:
        size = shape[0] * shape[1]
        nrows = -(-size // D)
        out[name] = slab[r0:r0 + nrows].reshape(-1)[:size].reshape(lead[name] + shape)
        r0 += nrows
    return out


SMALL_PARTS = (("norm_mix_g", 0, 0, D), ("norm_ffn_g", 1, 0, D), ("norm_final_g", 2, 0, D),
               ("conv_b", 3, 0, C_CONV), ("conv_ln_g", 3, C_CONV, C_CONV), ("conv_ln_b", 4, 0, C_CONV),
               ("gla_gate_b", 4, C_CONV, GLA_K), ("gla_norm_g", 4, C_CONV + GLA_K, DV))


def _pack_small(parts):
    slab = jnp.zeros((SMALL_ROWS, D), F32)
    for name, row, col, size in SMALL_PARTS:
        slab = lax.dynamic_update_slice(slab, parts[name].reshape(1, size).astype(F32), (row, col))
    return slab


def _unpack_small(slab, shapes):
    return {name: slab[row, col:col + size].reshape(shapes[name]) for name, row, col, size in SMALL_PARTS}


def _column_block(full, j, width):
    return lax.dynamic_slice_in_dim(full, j * width, width, axis=1)


def _local_step(x, target, w):
    n_ex, seq, _ = x.shape
    lp = HEAD_ROWS + seq
    t = n_ex * lp
    meta = jnp.broadcast_to(w["meta_tokens"][None], (n_ex, N_META, D))
    h0 = jnp.concatenate([jnp.zeros((n_ex, PAD_ROWS, D), F32), meta, x], axis=1).reshape(t, D)
    tgt = jnp.concatenate([jnp.zeros((n_ex, HEAD_ROWS, D), F32), target], axis=1).reshape(t, D)
    row_mask = jnp.concatenate([jnp.zeros((n_ex, HEAD_ROWS, 1), F32), jnp.ones((n_ex, seq, 1), F32)],
                               axis=1).reshape(t, 1)

    u, hn = _in_proj(h0, w["norm_mix_g"], w["w_in"])
    yc, y_conv = _conv_fwd(u, w["conv_w"], w["conv_b"], w["conv_ln_g"], w["conv_ln_b"], n_ex, lp)
    y_gla, states = _gla_fwd(u, w["gla_w_gate2"], w["gla_gate_b"], w["gla_norm_g"], n_ex, lp)
    h1, hn2, gate, up, act = _mix_out_ffn_up(h0, y_conv, y_gla, w["w_out"], w["norm_ffn_g"],
                                             w["w_ffn_gate_t"], w["w_ffn_up_t"])
    dh2, loss, d_final_g = _ffn_down_loss(act, w["w_ffn_down"], h1, tgt, w["norm_final_g"], row_mask)

    dgate, dup, dh1, dycat, d_ffn_g = _ffn_bwd(dh2, gate, up, h1, w["w_ffn_down"], w["w_ffn_gate_t"],
                                                w["w_ffn_up_t"], w["w_out"], w["norm_ffn_g"])
    du_conv, d_conv_w, d_conv_b, d_ln_g, d_ln_b = _conv_bwd(dycat, yc, u, w["conv_w"], w["conv_ln_g"],
                                                            w["conv_ln_b"], n_ex, lp)
    du_gla, d_w2, d_gate_b, d_norm_g = _gla_bwd(dycat, u, states, w["gla_w_gate2"], w["gla_gate_b"],
                                                w["gla_norm_g"], n_ex, lp)
    dh0, d_mix_g = _in_proj_bwd(du_conv, du_gla, w["w_in"][:, :2 * C_CONV], w["w_in"][:, 2 * C_CONV:],
                                h0, dh1, w["norm_mix_g"])

    d_w_in_t = jnp.concatenate([_wgrad(du_conv, hn, "wgrad_in_conv"), _wgrad(du_gla, hn, "wgrad_in_gla")],
                               axis=0)[:D_IN]
    d_w_out = jnp.concatenate([_wgrad(y_conv, dh1, "wgrad_out_conv"), _wgrad(y_gla, dh1, "wgrad_out_gla")], axis=0)
    dh0 = dh0.reshape(n_ex, lp, D)
    grads = {
        "w_in_t": d_w_in_t, "w_out": d_w_out,
        "w_ffn_gate_t": _wgrad(dgate, hn2, "wgrad_gate"), "w_ffn_up_t": _wgrad(dup, hn2, "wgrad_up"),
        "w_ffn_down": _wgrad(act, dh2, "wgrad_down"),
        "meta_tokens": jnp.sum(dh0[:, PAD_ROWS:HEAD_ROWS], axis=0),
        "conv_w": d_conv_w, "gla_w_gate2": d_w2[:RANK],
        "norm_mix_g": d_mix_g, "norm_ffn_g": d_ffn_g, "norm_final_g": d_final_g,
        "conv_b": d_conv_b, "conv_ln_g": d_ln_g, "conv_ln_b": d_ln_b,
        "gla_gate_b": d_gate_b, "gla_norm_g": d_norm_g,
    }
    return loss[0, 0], dh0[:, HEAD_ROWS:], grads


WEIGHT_NAMES = ("meta_tokens", "norm_mix_g", "w_in", "conv_w", "conv_b", "conv_ln_g", "conv_ln_b", "gla_w_gate2",
                "gla_gate_b", "gla_norm_g", "w_out", "norm_ffn_g", "w_ffn_gate", "w_ffn_up", "w_ffn_down",
                "norm_final_g")
MATMUL_WEIGHTS = ("w_in", "w_out", "w_ffn_gate", "w_ffn_up", "w_ffn_down")
ROW_SHARDED = ("w_out", "w_ffn_down")


def _full_weights(ws):
    sh = lambda name: ws[name].reshape(ws[name].shape[-2:])
    split = [sh("w_in").astype(BF16), sh("w_out").astype(BF16), sh("w_ffn_gate").T.astype(BF16),
             sh("w_ffn_up").T.astype(BF16), sh("w_ffn_down").astype(BF16)]
    whole = [sh("meta_tokens"), sh("conv_w"), sh("gla_w_gate2")]
    w_in, w_out, gate_t, up_t, down, meta, conv_w, w2 = _gather_weights(split, [0, 0, 0, 0, 0], whole)
    cols = lambda a: jnp.concatenate([a[j] for j in range(N_CHIPS)], axis=1)
    full = {name: ws[name].reshape(1, -1) for name, _, _, _ in SMALL_PARTS}
    full["w_in"] = jnp.concatenate([cols(w_in), jnp.zeros((D, D_IN_PAD - D_IN), BF16)], axis=1)
    full["w_out"] = w_out.reshape(D, D)
    full["w_ffn_gate_t"] = gate_t.reshape(D_FF, D)
    full["w_ffn_up_t"] = up_t.reshape(D_FF, D)
    full["w_ffn_down"] = down.reshape(D_FF, D)
    full["meta_tokens"] = cols(meta)
    full["conv_w"] = jnp.concatenate([cols(conv_w), jnp.zeros((32 - CONV_W, C_CONV), F32)], axis=0)
    full["gla_w_gate2"] = jnp.concatenate([cols(w2), jnp.zeros((128 - RANK, GLA_K), F32)], axis=0).astype(BF16)
    return full


SMALL_RS_ROWS = 48


def _pack_small_sharded(grads):
    by_chip = lambda g, w: jnp.transpose(g.reshape(g.shape[0], N_CHIPS, w), (1, 0, 2))
    meta = by_chip(grads["meta_tokens"], D // N_CHIPS)
    conv = by_chip(grads["conv_w"], C_CONV // N_CHIPS).reshape(N_CHIPS, 16, 256)
    w2 = by_chip(grads["gla_w_gate2"], GLA_K // N_CHIPS).reshape(N_CHIPS, 4, 256)
    pad = jnp.zeros((N_CHIPS, SMALL_RS_ROWS - 36, 256), F32)
    return jnp.concatenate([meta, conv, w2, pad], axis=1)


def _unpack_small_sharded(g):
    return {"meta_tokens": g[0:16], "conv_w": g[16:32].reshape(32, C_CONV // N_CHIPS)[:CONV_W],
            "gla_w_gate2": g[32:36].reshape(RANK, GLA_K // N_CHIPS)}


def _kernel_without_overlap(x, meta_tokens, norm_mix_g, w_in, conv_w, conv_b, conv_ln_g, conv_ln_b, gla_w_gate2, gla_gate_b, gla_norm_g, w_out, norm_ffn_g, w_ffn_gate, w_ffn_up, w_ffn_down, norm_final_g, loss_target, m_meta_tokens, m_norm_mix_g, m_w_in, m_conv_w, m_conv_b, m_conv_ln_g, m_conv_ln_b, m_gla_w_gate2, m_gla_gate_b, m_gla_norm_g, m_w_out, m_norm_ffn_g, m_w_ffn_gate, m_w_ffn_up, m_w_ffn_down, m_norm_final_g, v_meta_tokens, v_norm_mix_g, v_w_in, v_conv_w, v_conv_b, v_conv_ln_g, v_conv_ln_b, v_gla_w_gate2, v_gla_gate_b, v_gla_norm_g, v_w_out, v_norm_ffn_g, v_w_ffn_gate, v_w_ffn_up, v_w_ffn_down, v_norm_final_g):
    ws = dict(zip(WEIGHT_NAMES, (meta_tokens, norm_mix_g, w_in, conv_w, conv_b, conv_ln_g, conv_ln_b, gla_w_gate2,
                                 gla_gate_b, gla_norm_g, w_out, norm_ffn_g, w_ffn_gate, w_ffn_up, w_ffn_down,
                                 norm_final_g)))
    ms = dict(zip(WEIGHT_NAMES, (m_meta_tokens, m_norm_mix_g, m_w_in, m_conv_w, m_conv_b, m_conv_ln_g, m_conv_ln_b,
                                 m_gla_w_gate2, m_gla_gate_b, m_gla_norm_g, m_w_out, m_norm_ffn_g, m_w_ffn_gate,
                                 m_w_ffn_up, m_w_ffn_down, m_norm_final_g)))
    vs = dict(zip(WEIGHT_NAMES, (v_meta_tokens, v_norm_mix_g, v_w_in, v_conv_w, v_conv_b, v_conv_ln_g, v_conv_ln_b,
                                 v_gla_w_gate2, v_gla_gate_b, v_gla_norm_g, v_w_out, v_norm_ffn_g, v_w_ffn_gate,
                                 v_w_ffn_up, v_w_ffn_down, v_norm_final_g)))
    c = lax.axis_index("c")

    full = _full_weights(ws)
    loss, grad_x, grads = _local_step(x, loss_target, full)
    loss = lax.psum(loss, ("x", "y", "c"))

    rs_names = ("w_in", "w_out", "w_ffn_gate", "w_ffn_up", "w_ffn_down", "small")
    by_owner = [grads["w_in_t"].reshape(N_CHIPS, D_IN // N_CHIPS, D), grads["w_out"].reshape(N_CHIPS, D // N_CHIPS, D),
                grads["w_ffn_gate_t"].reshape(N_CHIPS, D_FF // N_CHIPS, D),
                grads["w_ffn_up_t"].reshape(N_CHIPS, D_FF // N_CHIPS, D),
                grads["w_ffn_down"].reshape(N_CHIPS, D_FF // N_CHIPS, D), _pack_small_sharded(grads)]
    from_sibling = _rs_to_sibling(by_owner)
    chip_sums = [_rs_add_halves(g, r, c, "rs_add_" + nm) for g, r, nm in zip(by_owner, from_sibling, rs_names)]
    halves = [_rs_sum_chips(p, "rs_sum_" + nm) for p, nm in zip(_rs_chip_exchange(chip_sums), rs_names)]
    reduced = dict(zip(rs_names, _rs_share(halves)))
    g_sharded = {"w_in": reduced["w_in"].T, "w_out": reduced["w_out"], "w_ffn_gate": reduced["w_ffn_gate"].T,
                 "w_ffn_up": reduced["w_ffn_up"].T, "w_ffn_down": reduced["w_ffn_down"],
                 **_unpack_small_sharded(reduced["small"])}
    out = {"grad": {}, "delta": {}, "new_m": {}, "new_v": {}}
    for name, g in g_sharded.items():
        shape = ws[name].shape
        flat = lambda a: a.reshape(shape[-2:])
        delta, new_m, new_v = _adamw(g, flat(ws[name]), flat(ms[name]), flat(vs[name]), "adamw_" + name)
        for kind, a in (("grad", g), ("delta", delta), ("new_m", new_m), ("new_v", new_v)):
            out[kind][name] = a.reshape(shape)

    small_shapes = {name: ws[name].shape for name, _, _, _ in SMALL_PARTS}
    g_s, d_s, m_s, v_s = _allreduce_small_adamw(_pack_small(grads), _pack_small(ws), _pack_small(ms), _pack_small(vs))
    for kind, slab in (("grad", g_s), ("delta", d_s), ("new_m", m_s), ("new_v", v_s)):
        out[kind].update(_unpack_small(slab, small_shapes))

    return (loss, grad_x, *[out[kind][name] for kind in ("grad", "delta", "new_m", "new_v") for name in WEIGHT_NAMES])


def _gather_plan(split, whole=()):
    split, whole = list(split), list(whole)
    ns, n = len(split), len(split) + len(whole)

    def make(ins, outs, sems):
        ici_send, ici_recv, d2d_send, d2d_recv, local_sems = sems
        x, y, c = _mesh_pos()
        mine = 2 * x + y
        chips = _other_chips(x, y)
        blocks = [2 * px + py for px, py in chips]

        def local(a):
            return pltpu.make_async_copy(ins[a], outs[a].at[mine], local_sems.at[a])

        def ici(a, k, block):
            px, py = chips[k]
            src, dst = ins[a], outs[a].at[block]
            if a < ns:
                src, dst = _half(src, c, 0), _half(dst, c, 0)
            return _remote(src, dst, ici_send.at[3 * a + k], ici_recv.at[3 * a + k], (px, py, c))

        def d2d(a, k, half):
            part = _half(outs[a].at[blocks[k]], half, 0)
            return _remote(part, part, d2d_send.at[3 * a + k], d2d_recv.at[3 * a + k], (x, y, 1 - c))

        def start():
            for a in range(n):
                local(a).start()
                for k in range(3):
                    ici(a, k, mine).start()

        def finish():
            for a in range(n):
                for k in range(3):
                    ici(a, k, blocks[k]).wait_recv()
                    if a < ns:
                        d2d(a, k, c).start()
            for a in range(ns):
                for k in range(3):
                    d2d(a, k, 1 - c).wait_recv()
            for a in range(n):
                for k in range(3):
                    ici(a, k, mine).wait_send()
                    if a < ns:
                        d2d(a, k, c).wait_send()
                local(a).wait()

        return start, finish

    arrays = split + whole
    return _Plan(arrays, [jax.ShapeDtypeStruct((N_CHIPS,) + s.shape, s.dtype) for s in arrays],
                 [pltpu.SemaphoreType.DMA((3 * n,)), pltpu.SemaphoreType.DMA((3 * n,)),
                  pltpu.SemaphoreType.DMA((3 * ns,)), pltpu.SemaphoreType.DMA((3 * ns,)),
                  pltpu.SemaphoreType.DMA((n,))], make)


def _to_sibling_plan(gs):
    n = len(gs)

    def make(ins, outs, sems):
        send_sems, recv_sems = sems
        x, y, c = _mesh_pos()

        def copy(a):
            return _remote(_half(ins[a], 1 - c, 2), outs[a], send_sems.at[a], recv_sems.at[a], (x, y, 1 - c))

        def start():
            for a in range(n):
                copy(a).start()

        def finish():
            for a in range(n):
                copy(a).wait()

        return start, finish

    return _Plan(list(gs), [jax.ShapeDtypeStruct(g.shape[:2] + (g.shape[2] // 2,), g.dtype) for g in gs],
                 [pltpu.SemaphoreType.DMA((n,)), pltpu.SemaphoreType.DMA((n,))], make)


def _chip_exchange_plan(ps):
    n = len(ps)

    def make(ins, outs, sems):
        send_sems, recv_sems, local_sems = sems
        x, y, c = _mesh_pos()
        mine = 2 * x + y
        chips = _other_chips(x, y)
        blocks = [2 * px + py for px, py in chips]

        def local(a):
            return pltpu.make_async_copy(ins[a].at[mine], outs[a].at[mine], local_sems.at[a])

        def ici(a, k, src_block, dst_block):
            px, py = chips[k]
            return _remote(ins[a].at[src_block], outs[a].at[dst_block], send_sems.at[3 * a + k],
                           recv_sems.at[3 * a + k], (px, py, c))

        def start():
            for a in range(n):
                local(a).start()
                for k in range(3):
                    ici(a, k, blocks[k], mine).start()

        def finish():
            for a in range(n):
                for k in range(3):
                    ici(a, k, mine, blocks[k]).wait_recv()
            for a in range(n):
                for k in range(3):
                    ici(a, k, blocks[k], mine).wait_send()
                local(a).wait()

        return start, finish

    return _Plan(list(ps), [jax.ShapeDtypeStruct(p.shape, p.dtype) for p in ps],
                 [pltpu.SemaphoreType.DMA((3 * n,)), pltpu.SemaphoreType.DMA((3 * n,)),
                  pltpu.SemaphoreType.DMA((n,))], make)


def _share_plan(halves):
    n = len(halves)

    def make(ins, outs, sems):
        send_sems, recv_sems, local_sems = sems
        x, y, c = _mesh_pos()

        def local(a):
            return pltpu.make_async_copy(ins[a], outs[a].at[c], local_sems.at[a])

        def d2d(a, half):
            return _remote(ins[a], outs[a].at[half], send_sems.at[a], recv_sems.at[a], (x, y, 1 - c))

        def start():
            for a in range(n):
                local(a).start()
                d2d(a, c).start()

        def finish():
            for a in range(n):
                d2d(a, 1 - c).wait_recv()
            for a in range(n):
                d2d(a, c).wait_send()
                local(a).wait()

        return start, finish

    return _Plan(list(halves), [jax.ShapeDtypeStruct((2,) + p.shape, p.dtype) for p in halves],
                 [pltpu.SemaphoreType.DMA((n,)), pltpu.SemaphoreType.DMA((n,)), pltpu.SemaphoreType.DMA((n,))], make)


def _exchange(plan, name):
    n_in, n_out = len(plan.arrays), len(plan.out_shape)

    def body(*refs):
        start, finish = plan.make(refs[:n_in], refs[n_in:n_in + n_out], refs[n_in + n_out:])
        start()
        finish()

    return pl.pallas_call(
        body, name=name, in_specs=[HBM_SPEC] * n_in, out_specs=[HBM_SPEC] * n_out, out_shape=list(plan.out_shape),
        scratch_shapes=list(plan.sems), compiler_params=pltpu.CompilerParams(has_side_effects=True),
    )(*plan.arrays)


def _adamw_halves(gh, w, m, v, name):
    _, rows, h = gh.shape
    tr = 256 if rows % 256 == 0 else rows // 2

    def body(g_ref, w_ref, m_ref, v_ref, go_ref, d_ref, m2_ref, v2_ref):
        g = jnp.concatenate([g_ref[0], g_ref[1]], axis=1)
        go_ref[...] = g
        d_ref[...], m2_ref[...], v2_ref[...] = _adam_update(g, w_ref[...], m_ref[...], v_ref[...])

    spec = pl.BlockSpec((tr, 2 * h), lambda i: (i, 0))
    return pl.pallas_call(
        body, name=name, grid=(rows // tr,),
        in_specs=[pl.BlockSpec((2, tr, h), lambda i: (0, i, 0))] + [spec] * 3, out_specs=[spec] * 4,
        out_shape=[jax.ShapeDtypeStruct((rows, 2 * h), F32)] * 4,
        compiler_params=_params(1),
    )(gh, w, m, v)


def _columns(gathered):
    return jnp.concatenate([gathered[j] for j in range(N_CHIPS)], axis=1)


def _join_halves(gh):
    return jnp.concatenate([gh[0], gh[1]], axis=1)


def kernel(x, meta_tokens, norm_mix_g, w_in, conv_w, conv_b, conv_ln_g, conv_ln_b, gla_w_gate2, gla_gate_b, gla_norm_g, w_out, norm_ffn_g, w_ffn_gate, w_ffn_up, w_ffn_down, norm_final_g, loss_target, m_meta_tokens, m_norm_mix_g, m_w_in, m_conv_w, m_conv_b, m_conv_ln_g, m_conv_ln_b, m_gla_w_gate2, m_gla_gate_b, m_gla_norm_g, m_w_out, m_norm_ffn_g, m_w_ffn_gate, m_w_ffn_up, m_w_ffn_down, m_norm_final_g, v_meta_tokens, v_norm_mix_g, v_w_in, v_conv_w, v_conv_b, v_conv_ln_g, v_conv_ln_b, v_gla_w_gate2, v_gla_gate_b, v_gla_norm_g, v_w_out, v_norm_ffn_g, v_w_ffn_gate, v_w_ffn_up, v_w_ffn_down, v_norm_final_g):
    ws = dict(zip(WEIGHT_NAMES, (meta_tokens, norm_mix_g, w_in, conv_w, conv_b, conv_ln_g, conv_ln_b, gla_w_gate2,
                                 gla_gate_b, gla_norm_g, w_out, norm_ffn_g, w_ffn_gate, w_ffn_up, w_ffn_down,
                                 norm_final_g)))
    ms = dict(zip(WEIGHT_NAMES, (m_meta_tokens, m_norm_mix_g, m_w_in, m_conv_w, m_conv_b, m_conv_ln_g, m_conv_ln_b,
                                 m_gla_w_gate2, m_gla_gate_b, m_gla_norm_g, m_w_out, m_norm_ffn_g, m_w_ffn_gate,
                                 m_w_ffn_up, m_w_ffn_down, m_norm_final_g)))
    vs = dict(zip(WEIGHT_NAMES, (v_meta_tokens, v_norm_mix_g, v_w_in, v_conv_w, v_conv_b, v_conv_ln_g, v_conv_ln_b,
                                 v_gla_w_gate2, v_gla_gate_b, v_gla_norm_g, v_w_out, v_norm_ffn_g, v_w_ffn_gate,
                                 v_w_ffn_up, v_w_ffn_down, v_norm_final_g)))
    c = lax.axis_index("c")
    shard = lambda d, name: d[name].reshape(d[name].shape[-2:])
    vec = {name: ws[name].reshape(1, -1) for name, _, _, _ in SMALL_PARTS}
    n_ex, seq, _ = x.shape
    lp = HEAD_ROWS + seq
    t = n_ex * lp

    w_in_g, meta_g, conv_w_g, w2_g = _exchange(
        _gather_plan([shard(ws, "w_in").astype(BF16)],
                     [shard(ws, "meta_tokens"), shard(ws, "conv_w"), shard(ws, "gla_w_gate2")]), "gather_first")
    w_in_full = jnp.concatenate([_columns(w_in_g), jnp.zeros((D, D_IN_PAD - D_IN), BF16)], axis=1)
    conv_w_full = jnp.concatenate([_columns(conv_w_g), jnp.zeros((32 - CONV_W, C_CONV), F32)], axis=0)
    w2_full = jnp.concatenate([_columns(w2_g), jnp.zeros((128 - RANK, GLA_K), F32)], axis=0).astype(BF16)

    meta = jnp.broadcast_to(_columns(meta_g)[None], (n_ex, N_META, D))
    h0 = jnp.concatenate([jnp.zeros((n_ex, PAD_ROWS, D), F32), meta, x], axis=1).reshape(t, D)
    tgt = jnp.concatenate([jnp.zeros((n_ex, HEAD_ROWS, D), F32), loss_target], axis=1).reshape(t, D)
    row_mask = jnp.concatenate([jnp.zeros((n_ex, HEAD_ROWS, 1), F32), jnp.ones((n_ex, seq, 1), F32)],
                               axis=1).reshape(t, 1)

    (u, hn), (w_out_g,) = _in_proj(h0, vec["norm_mix_g"], w_in_full,
                                   plan=_gather_plan([shard(ws, "w_out").astype(BF16)]))
    (yc, y_conv), (gate_g, up_g) = _conv_fwd(
        u, conv_w_full, vec["conv_b"], vec["conv_ln_g"], vec["conv_ln_b"], n_ex, lp,
        plan=_gather_plan([shard(ws, "w_ffn_gate").T.astype(BF16), shard(ws, "w_ffn_up").T.astype(BF16)]))
    (y_gla, states), (down_g,) = _gla_fwd(u, w2_full, vec["gla_gate_b"], vec["gla_norm_g"], n_ex, lp,
                                          plan=_gather_plan([shard(ws, "w_ffn_down").astype(BF16)]))
    w_out_full, w_down_full = w_out_g.reshape(D, D), down_g.reshape(D_FF, D)
    w_gate_t, w_up_t = gate_g.reshape(D_FF, D), up_g.reshape(D_FF, D)

    h1, hn2, gate, up, act = _mix_out_ffn_up(h0, y_conv, y_gla, w_out_full, vec["norm_ffn_g"], w_gate_t, w_up_t)
    dh2, loss, d_final_g = _ffn_down_loss(act, w_down_full, h1, tgt, vec["norm_final_g"], row_mask)
    loss = lax.psum(loss[0, 0], ("x", "y", "c"))
    dgate, dup, dh1, dycat, d_ffn_g = _ffn_bwd(dh2, gate, up, h1, w_down_full, w_gate_t, w_up_t, w_out_full,
                                                vec["norm_ffn_g"])

    early = ("w_out", "w_ffn_gate", "w_ffn_up", "w_ffn_down")
    d_w_out = jnp.concatenate([_wgrad(y_conv, dh1, "wgrad_out_conv"), _wgrad(y_gla, dh1, "wgrad_out_gla")], axis=0)
    by_owner = [d_w_out.reshape(N_CHIPS, D // N_CHIPS, D),
                _wgrad(dgate, hn2, "wgrad_gate").reshape(N_CHIPS, D_FF // N_CHIPS, D),
                _wgrad(dup, hn2, "wgrad_up").reshape(N_CHIPS, D_FF // N_CHIPS, D),
                _wgrad(act, dh2, "wgrad_down").reshape(N_CHIPS, D_FF // N_CHIPS, D)]
    (du_conv, d_conv_w, d_conv_b, d_ln_g, d_ln_b), from_sibling = _conv_bwd(
        dycat, yc, u, conv_w_full, vec["conv_ln_g"], vec["conv_ln_b"], n_ex, lp, plan=_to_sibling_plan(by_owner))
    chip_sums = [_rs_add_halves(g, r, c, "rs_add_" + nm) for g, r, nm in zip(by_owner, from_sibling, early)]
    (du_gla, d_w2, d_gate_b, d_norm_g), exchanged = _gla_bwd(
        dycat, u, states, w2_full, vec["gla_gate_b"], vec["gla_norm_g"], n_ex, lp,
        plan=_chip_exchange_plan(chip_sums))
    halves = [_rs_sum_chips(p, "rs_sum_" + nm) for p, nm in zip(exchanged, early)]
    (dh0, d_mix_g), shared = _in_proj_bwd(du_conv, du_gla, w_in_full[:, :2 * C_CONV], w_in_full[:, 2 * C_CONV:],
                                          h0, dh1, vec["norm_mix_g"], plan=_share_plan(halves))
    dh0 = dh0.reshape(n_ex, lp, D)
    grad_x = dh0[:, HEAD_ROWS:]

    out = {"grad": {}, "delta": {}, "new_m": {}, "new_v": {}}

    def update(name, g=None, halves=None):
        shape = ws[name].shape
        w2d, m2d, v2d = shard(ws, name), shard(ms, name), shard(vs, name)
        if halves is not None:
            res = _adamw_halves(halves, w2d, m2d, v2d, "adamw_" + name)
        else:
            res = [g, *_adamw(g, w2d, m2d, v2d, "adamw_" + name)]
        for kind, a in zip(("grad", "delta", "new_m", "new_v"), res):
            out[kind][name] = a.reshape(shape)

    update("w_out", halves=shared[0])
    update("w_ffn_gate", g=_join_halves(shared[1]).T)
    update("w_ffn_up", g=_join_halves(shared[2]).T)
    update("w_ffn_down", halves=shared[3])

    small = {"norm_mix_g": d_mix_g, "norm_ffn_g": d_ffn_g, "norm_final_g": d_final_g, "conv_b": d_conv_b,
             "conv_ln_g": d_ln_g, "conv_ln_b": d_ln_b, "gla_gate_b": d_gate_b, "gla_norm_g": d_norm_g}
    small_shapes = {name: ws[name].shape for name, _, _, _ in SMALL_PARTS}
    g_s, d_s, m_s, v_s = _allreduce_small_adamw(_pack_small(small), _pack_small(ws), _pack_small(ms), _pack_small(vs))
    for kind, slab in (("grad", g_s), ("delta", d_s), ("new_m", m_s), ("new_v", v_s)):
        out[kind].update(_unpack_small(slab, small_shapes))

    d_w_in_t = jnp.concatenate([_wgrad(du_conv, hn, "wgrad_in_conv"), _wgrad(du_gla, hn, "wgrad_in_gla")],
                               axis=0)[:D_IN]
    small_sharded = {"meta_tokens": jnp.sum(dh0[:, PAD_ROWS:HEAD_ROWS], axis=0), "conv_w": d_conv_w,
                     "gla_w_gate2": d_w2[:RANK]}
    late = ("w_in", "small")
    by_owner = [d_w_in_t.reshape(N_CHIPS, D_IN // N_CHIPS, D), _pack_small_sharded(small_sharded)]
    from_sibling = _exchange(_to_sibling_plan(by_owner), "rs_late_to_sibling")
    chip_sums = [_rs_add_halves(g, r, c, "rs_add_" + nm) for g, r, nm in zip(by_owner, from_sibling, late)]
    exchanged = _exchange(_chip_exchange_plan(chip_sums), "rs_late_chip_exchange")
    halves = [_rs_sum_chips(p, "rs_sum_" + nm) for p, nm in zip(exchanged, late)]
    shared = _exchange(_share_plan(halves), "rs_late_share")
    update("w_in", g=_join_halves(shared[0]).T)
    for name, g in _unpack_small_sharded(_join_halves(shared[1])).items():
        update(name, g=g)

    return (loss, grad_x, *[out[kind][name] for kind in ("grad", "delta", "new_m", "new_v") for name in WEIGHT_NAMES])
```

```python
import functools
from typing import Any, Callable, NamedTuple, Sequence

import jax
import jax.numpy as jnp
from jax import lax
from jax.experimental import pallas as pl
from jax.experimental.pallas import tpu as pltpu

F32 = jnp.float32
BF16 = jnp.bfloat16
MESH = pl.DeviceIdType.MESH

D = 1024
N_META = 16
C_CONV = 512
CONV_W = 31
GLA_K = 256
GLA_V = 512
N_HEADS = 4
DK = 64
DV = 128
RANK = 16
CHUNK = 64
PAD_ROWS = CHUNK - N_META
HEAD_ROWS = CHUNK
D_IN = 2576
D_IN_PAD = 2688
D_GLA_IN = D_IN_PAD - 2 * C_CONV
D_FF = 2816
RMS_EPS = 1e-6
LN_EPS = 1e-5
GATE_TAU = 16.0
N_CHIPS = 4

ADAM_LR = 0.001
ADAM_B1 = 0.9
ADAM_B2 = 0.999
ADAM_EPS = 1e-08
ADAM_WD = 0.01
ADAM_STEP = 10

V7X_VMEM_BYTES = 64 * 1024 * 1024
VMEM_LIMIT = V7X_VMEM_BYTES - 8 * 1024 * 1024

SLAB_ROWS = 3072
HALF_ROWS = SLAB_ROWS // 2
SMALL_ROWS = 8


def _dot(a, b):
    return jnp.dot(a, b, preferred_element_type=F32)


def _dot_nt(a, b):
    return lax.dot_general(a, b, (((1,), (1,)), ((), ())), preferred_element_type=F32)


def _dot_tn(a, b):
    return lax.dot_general(a, b, (((0,), (0,)), ((), ())), preferred_element_type=F32)


def _sigmoid(x):
    return 1.0 / (1.0 + jnp.exp(-x))


def _const_spec(shape):
    return pl.BlockSpec(shape, lambda *_: (0,) * len(shape), pipeline_mode=pl.Buffered(1))


def _acc_spec(shape):
    return pl.BlockSpec(shape, lambda *_: (0,) * len(shape))


def _params(n_axes):
    return pltpu.CompilerParams(dimension_semantics=("arbitrary",) * n_axes, vmem_limit_bytes=VMEM_LIMIT)


def _row_tile(t, want):
    for r in (want, 384, 192, 128, 64):
        if r <= want and t % r == 0:
            return r
    raise ValueError(f"no row tile for {t}")


class _Plan(NamedTuple):
    arrays: Sequence[Any]
    out_shape: Sequence[Any]
    sems: Sequence[Any]
    make: Callable


def _call(body, *, name, grid, in_specs, out_specs, out_shape, scratch_shapes=(), plan=None):
    n_in, n_out, n_scr = len(in_specs), len(out_specs), len(scratch_shapes)
    if plan is None:
        plan = _Plan([], [], [], lambda ins, outs, sems: (lambda: None, lambda: None))
    nx_in, nx_out = len(plan.arrays), len(plan.out_shape)

    def hosted(*refs):
        ins, xins = refs[:n_in], refs[n_in:n_in + nx_in]
        o0 = n_in + nx_in
        outs, xouts = refs[o0:o0 + n_out], refs[o0 + n_out:o0 + n_out + nx_out]
        s0 = o0 + n_out + nx_out
        scr, sems = refs[s0:s0 + n_scr], refs[s0 + n_scr:]
        ids = [pl.program_id(a) for a in range(len(grid))]
        first = functools.reduce(jnp.logical_and, [i == 0 for i in ids])
        last = functools.reduce(jnp.logical_and, [i == g - 1 for i, g in zip(ids, grid)])
        start, finish = plan.make(xins, xouts, sems)
        pl.when(first)(start)
        body(*ins, *outs, *scr)
        pl.when(last)(finish)

    call = pl.pallas_call(
        hosted, name=name, grid=grid, in_specs=list(in_specs) + [HBM_SPEC] * nx_in,
        out_specs=list(out_specs) + [HBM_SPEC] * nx_out, out_shape=list(out_shape) + list(plan.out_shape),
        scratch_shapes=list(scratch_shapes) + list(plan.sems),
        compiler_params=pltpu.CompilerParams(dimension_semantics=("arbitrary",) * len(grid),
                                             vmem_limit_bytes=VMEM_LIMIT, has_side_effects=nx_in > 0))

    def run(*args):
        res = call(*args, *plan.arrays)
        return res[:n_out], res[n_out:]

    return run


def _in_proj(h0, g_mix, w_in, plan=None):
    t = h0.shape[0]
    r = _row_tile(t, 384)

    def body(h_ref, g_ref, w_ref, u_ref, hn_ref):
        h = h_ref[...]
        rstd = lax.rsqrt(jnp.mean(h * h, axis=-1, keepdims=True) + RMS_EPS)
        hn = (h * rstd * g_ref[...]).astype(BF16)
        hn_ref[...] = hn
        u_ref[...] = _dot(hn, w_ref[...])

    return _call(
        body, name="in_proj", grid=(t // r,),
        in_specs=[pl.BlockSpec((r, D), lambda i: (i, 0)), _const_spec((1, D)), _const_spec((D, D_IN_PAD))],
        out_specs=[pl.BlockSpec((r, D_IN_PAD), lambda i: (i, 0)), pl.BlockSpec((r, D), lambda i: (i, 0))],
        out_shape=[jax.ShapeDtypeStruct((t, D_IN_PAD), F32), jax.ShapeDtypeStruct((t, D), BF16)],
        plan=plan,
    )(h0, g_mix, w_in)


CONV_TILE = 192
CONV_SUB = 32
CONV_LEAD = CONV_SUB - (CONV_W - 1)


def _conv_fwd(u, conv_w, conv_b, ln_g, ln_b, n_ex, lp, plan=None):
    r = CONV_TILE
    nt = lp // r
    hb = r // CONV_SUB

    def body(cur_ref, prev_ref, w_ref, b_ref, lg_ref, lb_ref, yc_ref, y_ref, glu):
        i = pl.program_id(1)
        cur = cur_ref[...]
        glu[CONV_SUB:CONV_SUB + r, :] = cur[:, :C_CONV] * _sigmoid(cur[:, C_CONV:])
        pv = prev_ref[...]
        halo = pv[:, :C_CONV] * _sigmoid(pv[:, C_CONV:])
        glu[0:CONV_SUB, :] = jnp.where(i > 0, halo, 0.0)
        w = w_ref[...]
        for j in range(r // CONV_SUB):
            r0 = j * CONV_SUB
            acc = jnp.zeros((CONV_SUB, C_CONV), F32) + b_ref[...]
            for k in range(CONV_W):
                acc = acc + w[k:k + 1, :] * glu[r0 + CONV_LEAD + k:r0 + CONV_LEAD + k + CONV_SUB, :]
            mu = jnp.mean(acc, axis=-1, keepdims=True)
            cen = acc - mu
            var = jnp.mean(cen * cen, axis=-1, keepdims=True)
            out = cen * lax.rsqrt(var + LN_EPS) * lg_ref[...] + lb_ref[...]
            y = out * _sigmoid(out)
            row = i * r + r0 + lax.broadcasted_iota(jnp.int32, (CONV_SUB, 1), 0)
            y = jnp.where(row >= PAD_ROWS, y, 0.0)
            yc_ref[r0:r0 + CONV_SUB, :] = acc
            y_ref[r0:r0 + CONV_SUB, :] = y.astype(BF16)

    t = n_ex * lp
    return _call(
        body, name="conv_fwd", grid=(n_ex, nt),
        in_specs=[pl.BlockSpec((r, 2 * C_CONV), lambda b, i: (b * nt + i, 0)),
                  pl.BlockSpec((CONV_SUB, 2 * C_CONV), lambda b, i: (jnp.maximum((b * nt + i) * hb - 1, 0), 0)),
                  _const_spec((32, C_CONV)), _const_spec((1, C_CONV)), _const_spec((1, C_CONV)), _const_spec((1, C_CONV))],
        out_specs=[pl.BlockSpec((r, C_CONV), lambda b, i: (b * nt + i, 0)),
                   pl.BlockSpec((r, C_CONV), lambda b, i: (b * nt + i, 0))],
        out_shape=[jax.ShapeDtypeStruct((t, C_CONV), F32), jax.ShapeDtypeStruct((t, C_CONV), BF16)],
        scratch_shapes=[pltpu.VMEM((r + CONV_SUB, C_CONV), F32)],
        plan=plan,
    )(u, u, conv_w, conv_b, ln_g, ln_b)


def _gla_gates(lr, w2, gb, first_chunk):
    z = _dot(lr.astype(BF16), w2) + gb
    a = (jnp.minimum(z, 0.0) - jnp.log(1.0 + jnp.exp(-jnp.abs(z)))) * (1.0 / GATE_TAU)
    row = lax.broadcasted_iota(jnp.int32, (CHUNK, 1), 0)
    live = jnp.logical_or(jnp.logical_not(first_chunk), row >= PAD_ROWS)
    return z, jnp.where(live, a, 0.0), live


def _tri(lower):
    i = lax.broadcasted_iota(jnp.int32, (CHUNK, CHUNK), 0)
    j = lax.broadcasted_iota(jnp.int32, (CHUNK, CHUNK), 1)
    return (i >= j) if lower else (i <= j)


def _gla_fwd(u, w2, gb, ng, n_ex, lp, plan=None):
    nc = lp // CHUNK
    t = n_ex * lp

    def body(qk_ref, v_ref, g_ref, lr_ref, w2_ref, gb_ref, ng_ref, y_ref, st_ref, state):
        n = pl.program_id(1)

        @pl.when(n == 0)
        def _():
            state[...] = jnp.zeros_like(state)

        st = state[...]
        st_ref[...] = st
        qk = qk_ref[...]
        q, k = qk[:, :GLA_K], qk[:, GLA_K:]
        _, a, _ = _gla_gates(lr_ref[...], w2_ref[...], gb_ref[...], n == 0)
        causal = _tri(True)
        b = jnp.dot(causal.astype(F32), a, preferred_element_type=F32, precision=lax.Precision.HIGHEST)
        bl = b[CHUNK - 1:CHUNK, :]
        q_in = (q * (DK ** -0.5) * jnp.exp(b)).astype(BF16)
        k_in = (k * jnp.exp(-b)).astype(BF16)
        k_dec = (k * jnp.exp(bl - b)).astype(BF16)
        decay = jnp.exp(bl)
        v = v_ref[...]
        g = g_ref[...]
        st_b = st.astype(BF16)
        ys, new = [], []
        for h in range(N_HEADS):
            ks = slice(h * DK, (h + 1) * DK)
            vs = slice(h * DV, (h + 1) * DV)
            vh = v[:, vs].astype(BF16)
            s = jnp.where(causal, _dot_nt(q_in[:, ks], k_in[:, ks]), 0.0)
            o = _dot(s.astype(BF16), vh) + _dot_nt(q_in[:, ks], st_b[:, ks])
            new.append(decay[:, ks] * st[:, ks] + _dot_tn(vh, k_dec[:, ks]))
            rstd = lax.rsqrt(jnp.mean(o * o, axis=-1, keepdims=True) + RMS_EPS)
            gh = g[:, vs]
            ys.append(o * rstd * ng_ref[...] * (gh * _sigmoid(gh)))
        state[...] = jnp.concatenate(new, axis=1)
        y_ref[...] = jnp.concatenate(ys, axis=1).astype(BF16)

    blk = lambda w, col: pl.BlockSpec((CHUNK, w), lambda b, n: (b * nc + n, col))
    return _call(
        body, name="gla_fwd", grid=(n_ex, nc),
        in_specs=[blk(2 * GLA_K, 2), blk(GLA_V, 3), blk(GLA_V, 4), blk(128, 20),
                  _const_spec((128, GLA_K)), _const_spec((1, GLA_K)), _const_spec((1, DV))],
        out_specs=[pl.BlockSpec((CHUNK, GLA_V), lambda b, n: (b * nc + n, 0)),
                   pl.BlockSpec((DV, GLA_K), lambda b, n: (b * nc + n, 0))],
        out_shape=[jax.ShapeDtypeStruct((t, GLA_V), BF16), jax.ShapeDtypeStruct((n_ex * nc * DV, GLA_K), F32)],
        scratch_shapes=[pltpu.VMEM((DV, GLA_K), F32)],
        plan=plan,
    )(u, u, u, u, w2, gb, ng)


FFN_TILE = 192


def _mix_out_ffn_up(h0, y_conv, y_gla, w_out, g_ffn, w_gate_t, w_up_t):
    t = h0.shape[0]
    r = _row_tile(t, FFN_TILE)

    def body(h0_ref, yc_ref, yg_ref, wo_ref, g_ref, wg_ref, wu_ref, h1_ref, hn_ref, gate_ref, up_ref, act_ref):
        h1 = h0_ref[...] + _dot(yc_ref[...], wo_ref[0:C_CONV, :]) + _dot(yg_ref[...], wo_ref[C_CONV:D, :])
        h1_ref[...] = h1
        rstd = lax.rsqrt(jnp.mean(h1 * h1, axis=-1, keepdims=True) + RMS_EPS)
        hn = (h1 * rstd * g_ref[...]).astype(BF16)
        hn_ref[...] = hn
        gate = _dot_nt(hn, wg_ref[...])
        up = _dot_nt(hn, wu_ref[...])
        gate_ref[...] = gate
        up_ref[...] = up
        act_ref[...] = (gate * _sigmoid(gate) * up).astype(BF16)

    rows = lambda w: pl.BlockSpec((r, w), lambda i: (i, 0))
    return pl.pallas_call(
        body, name="mix_out_ffn_up", grid=(t // r,),
        in_specs=[rows(D), rows(C_CONV), rows(GLA_V), _const_spec((D, D)), _const_spec((1, D)),
                  _const_spec((D_FF, D)), _const_spec((D_FF, D))],
        out_specs=[rows(D), rows(D), rows(D_FF), rows(D_FF), rows(D_FF)],
        out_shape=[jax.ShapeDtypeStruct((t, D), F32), jax.ShapeDtypeStruct((t, D), BF16),
                   jax.ShapeDtypeStruct((t, D_FF), F32), jax.ShapeDtypeStruct((t, D_FF), F32),
                   jax.ShapeDtypeStruct((t, D_FF), BF16)],
        compiler_params=_params(1),
    )(h0, y_conv, y_gla, w_out, g_ffn, w_gate_t, w_up_t)


def _ffn_down_loss(act, w_down, h1, target, g_final, row_mask):
    t = h1.shape[0]
    r = _row_tile(t, 384)

    def body(act_ref, wd_ref, h1_ref, tgt_ref, gf_ref, mask_ref, dh2_ref, loss_ref, dgf_ref):
        @pl.when(pl.program_id(0) == 0)
        def _():
            loss_ref[...] = jnp.zeros_like(loss_ref)
            dgf_ref[...] = jnp.zeros_like(dgf_ref)

        h2 = h1_ref[...] + _dot(act_ref[...], wd_ref[...])
        rstd = lax.rsqrt(jnp.mean(h2 * h2, axis=-1, keepdims=True) + RMS_EPS)
        nrm = h2 * rstd
        gf = gf_ref[...]
        err = (nrm * gf - tgt_ref[...]) * mask_ref[...]
        loss_ref[...] += jnp.sum(err * err) * (0.5 / D)
        dy = err * (1.0 / D)
        dgf_ref[...] += jnp.sum(dy * nrm, axis=0, keepdims=True)
        dn = dy * gf
        dh2_ref[...] = rstd * (dn - nrm * jnp.mean(dn * nrm, axis=-1, keepdims=True))

    rows = lambda w: pl.BlockSpec((r, w), lambda i: (i, 0))
    return pl.pallas_call(
        body, name="ffn_down_loss", grid=(t // r,),
        in_specs=[rows(D_FF), _const_spec((D_FF, D)), rows(D), rows(D), _const_spec((1, D)), rows(1)],
        out_specs=[rows(D), _acc_spec((1, 128)), _acc_spec((1, D))],
        out_shape=[jax.ShapeDtypeStruct((t, D), F32), jax.ShapeDtypeStruct((1, 128), F32),
                   jax.ShapeDtypeStruct((1, D), F32)],
        compiler_params=_params(1),
    )(act, w_down, h1, target, g_final, row_mask)


def _ffn_bwd(dh2, gate, up, h1, w_down, w_gate_t, w_up_t, w_out, g_ffn):
    t = h1.shape[0]
    r = _row_tile(t, FFN_TILE)

    def body(dh2_ref, gate_ref, up_ref, h1_ref, wd_ref, wg_ref, wu_ref, wo_ref, g_ref,
             dgate_ref, dup_ref, dh1_ref, dycat_ref, dg_ref):
        @pl.when(pl.program_id(0) == 0)
        def _():
            dg_ref[...] = jnp.zeros_like(dg_ref)

        dh2 = dh2_ref[...]
        dact = _dot_nt(dh2.astype(BF16), wd_ref[...])
        gate = gate_ref[...]
        sg = _sigmoid(gate)
        dgate = (dact * up_ref[...] * (sg * (1.0 + gate * (1.0 - sg)))).astype(BF16)
        dup = (dact * (gate * sg)).astype(BF16)
        dgate_ref[...] = dgate
        dup_ref[...] = dup
        dhn = _dot(dgate, wg_ref[...]) + _dot(dup, wu_ref[...])
        h1 = h1_ref[...]
        rstd = lax.rsqrt(jnp.mean(h1 * h1, axis=-1, keepdims=True) + RMS_EPS)
        nrm = h1 * rstd
        dg_ref[...] += jnp.sum(dhn * nrm, axis=0, keepdims=True)
        dn = dhn * g_ref[...]
        dh1 = dh2 + rstd * (dn - nrm * jnp.mean(dn * nrm, axis=-1, keepdims=True))
        dh1_ref[...] = dh1
        dycat_ref[...] = _dot_nt(dh1.astype(BF16), wo_ref[...])

    rows = lambda w: pl.BlockSpec((r, w), lambda i: (i, 0))
    return pl.pallas_call(
        body, name="ffn_bwd", grid=(t // r,),
        in_specs=[rows(D), rows(D_FF), rows(D_FF), rows(D), _const_spec((D_FF, D)), _const_spec((D_FF, D)),
                  _const_spec((D_FF, D)), _const_spec((D, D)), _const_spec((1, D))],
        out_specs=[rows(D_FF), rows(D_FF), rows(D), rows(D), _acc_spec((1, D))],
        out_shape=[jax.ShapeDtypeStruct((t, D_FF), BF16), jax.ShapeDtypeStruct((t, D_FF), BF16),
                   jax.ShapeDtypeStruct((t, D), F32), jax.ShapeDtypeStruct((t, D), F32),
                   jax.ShapeDtypeStruct((1, D), F32)],
        compiler_params=_params(1),
    )(dh2, gate, up, h1, w_down, w_gate_t, w_up_t, w_out, g_ffn)


def _conv_bwd(dycat, yc, u, conv_w, ln_g, ln_b, n_ex, lp, plan=None):
    r = CONV_TILE
    nt = lp // r
    hb = r // CONV_SUB
    nsub = r // CONV_SUB

    def ln_bwd(dy, yc_rows, live, lg, lb):
        mu = jnp.mean(yc_rows, axis=-1, keepdims=True)
        cen = yc_rows - mu
        rs = lax.rsqrt(jnp.mean(cen * cen, axis=-1, keepdims=True) + LN_EPS)
        yn = cen * rs
        out = yn * lg + lb
        so = _sigmoid(out)
        dout = jnp.where(live, dy * (so * (1.0 + out * (1.0 - so))), 0.0)
        dyn = dout * lg
        dyc = rs * (dyn - jnp.mean(dyn, axis=-1, keepdims=True) - yn * jnp.mean(dyn * yn, axis=-1, keepdims=True))
        return dyc, dout, yn

    def body(dy_ref, dyn_ref, yc_ref, ycn_ref, cur_ref, prev_ref, w_ref, lg_ref, lb_ref,
             du_ref, dw_ref, db_ref, dlg_ref, dlb_ref, glu, dycs, dwacc):
        b = pl.program_id(0)
        i = pl.program_id(1)
        first = jnp.logical_and(b == 0, i == 0)

        @pl.when(first)
        def _():
            dwacc[...] = jnp.zeros_like(dwacc)
            db_ref[...] = jnp.zeros_like(db_ref)
            dlg_ref[...] = jnp.zeros_like(dlg_ref)
            dlb_ref[...] = jnp.zeros_like(dlb_ref)

        lg, lb = lg_ref[...], lb_ref[...]
        cur = cur_ref[...]
        sig = _sigmoid(cur[:, C_CONV:])
        glu[CONV_SUB:CONV_SUB + r, :] = cur[:, :C_CONV] * sig
        pv = prev_ref[...]
        glu[0:CONV_SUB, :] = jnp.where(i > 0, pv[:, :C_CONV] * _sigmoid(pv[:, C_CONV:]), 0.0)

        row = i * r + lax.broadcasted_iota(jnp.int32, (r, 1), 0)
        dyc, dout, yn = ln_bwd(dy_ref[...], yc_ref[...], row >= PAD_ROWS, lg, lb)
        dycs[0:r, :] = dyc
        dycn, _, _ = ln_bwd(dyn_ref[...], ycn_ref[...], i < nt - 1, lg, lb)
        dycs[r:r + CONV_SUB, :] = dycn
        db_ref[...] += jnp.sum(dyc, axis=0, keepdims=True)
        dlg_ref[...] += jnp.sum(dout * yn, axis=0, keepdims=True)
        dlb_ref[...] += jnp.sum(dout, axis=0, keepdims=True)

        w = w_ref[...]
        for j in range(nsub):
            r0 = j * CONV_SUB
            dblk = dycs[r0:r0 + CONV_SUB, :]
            dglu = jnp.zeros((CONV_SUB, C_CONV), F32)
            for k in range(CONV_W):
                dglu = dglu + w[k:k + 1, :] * dycs[r0 + (CONV_W - 1) - k:r0 + (CONV_W - 1) - k + CONV_SUB, :]
                prod = dblk * glu[r0 + CONV_LEAD + k:r0 + CONV_LEAD + k + CONV_SUB, :]
                dwacc[k] += prod.reshape(CONV_SUB // 8, 8, C_CONV).sum(axis=0)
            sg = sig[r0:r0 + CONV_SUB, :]
            cv = cur[r0:r0 + CONV_SUB, :C_CONV]
            du_ref[r0:r0 + CONV_SUB, :C_CONV] = (dglu * sg).astype(BF16)
            du_ref[r0:r0 + CONV_SUB, C_CONV:] = (dglu * cv * sg * (1.0 - sg)).astype(BF16)

        @pl.when(jnp.logical_and(b == n_ex - 1, i == nt - 1))
        def _():
            dw_ref[...] = jnp.sum(dwacc[...], axis=1)

    t = n_ex * lp
    cur_rows = lambda w, col: pl.BlockSpec((r, w), lambda b, i: (b * nt + i, col))
    nxt_rows = lambda w, col: pl.BlockSpec(
        (CONV_SUB, w), lambda b, i: (jnp.minimum((b * nt + i + 1) * hb, n_ex * nt * hb - 1), col))
    return _call(
        body, name="conv_bwd", grid=(n_ex, nt),
        in_specs=[cur_rows(C_CONV, 0), nxt_rows(C_CONV, 0), cur_rows(C_CONV, 0), nxt_rows(C_CONV, 0),
                  cur_rows(2 * C_CONV, 0),
                  pl.BlockSpec((CONV_SUB, 2 * C_CONV), lambda b, i: (jnp.maximum((b * nt + i) * hb - 1, 0), 0)),
                  _const_spec((32, C_CONV)), _const_spec((1, C_CONV)), _const_spec((1, C_CONV))],
        out_specs=[cur_rows(2 * C_CONV, 0), _acc_spec((32, C_CONV)), _acc_spec((1, C_CONV)),
                   _acc_spec((1, C_CONV)), _acc_spec((1, C_CONV))],
        out_shape=[jax.ShapeDtypeStruct((t, 2 * C_CONV), BF16), jax.ShapeDtypeStruct((32, C_CONV), F32),
                   jax.ShapeDtypeStruct((1, C_CONV), F32), jax.ShapeDtypeStruct((1, C_CONV), F32),
                   jax.ShapeDtypeStruct((1, C_CONV), F32)],
        scratch_shapes=[pltpu.VMEM((r + CONV_SUB, C_CONV), F32), pltpu.VMEM((r + CONV_SUB, C_CONV), F32),
                        pltpu.VMEM((32, 8, C_CONV), F32)],
        plan=plan,
    )(dycat, dycat, yc, yc, u, u, conv_w, ln_g, ln_b)


def _gla_bwd(dycat, u, states, w2, gb, ng, n_ex, lp, plan=None):
    nc = lp // CHUNK
    t = n_ex * lp

    def body(dy_ref, qk_ref, v_ref, g_ref, lr_ref, st_ref, w2_ref, gb_ref, ng_ref,
             du_ref, dw2_ref, dgb_ref, dng_ref, dstate):
        bi = pl.program_id(0)
        n = pl.program_id(1)
        chunk = nc - 1 - n

        @pl.when(jnp.logical_and(bi == 0, n == 0))
        def _():
            dw2_ref[...] = jnp.zeros_like(dw2_ref)
            dgb_ref[...] = jnp.zeros_like(dgb_ref)
            dng_ref[...] = jnp.zeros_like(dng_ref)

        @pl.when(n == 0)
        def _():
            dstate[...] = jnp.zeros_like(dstate)

        qk = qk_ref[...]
        q, k = qk[:, :GLA_K], qk[:, GLA_K:]
        lr = lr_ref[...]
        z, a, live = _gla_gates(lr, w2_ref[...], gb_ref[...], chunk == 0)
        causal = _tri(True)
        b = jnp.dot(causal.astype(F32), a, preferred_element_type=F32, precision=lax.Precision.HIGHEST)
        bl = b[CHUNK - 1:CHUNK, :]
        e_pos, e_neg, e_dec = jnp.exp(b), jnp.exp(-b), jnp.exp(bl - b)
        q_f = q * (DK ** -0.5) * e_pos
        k_f = k * e_neg
        kd_f = k * e_dec
        q_in, k_in, k_dec = q_f.astype(BF16), k_f.astype(BF16), kd_f.astype(BF16)
        decay = jnp.exp(bl)
        v = v_ref[...]
        g = g_ref[...]
        dy = dy_ref[...]
        ngv = ng_ref[...]
        st = st_ref[...]
        st_b = st.astype(BF16)
        dst = dstate[...]
        dst_b = dst.astype(BF16)
        dqs, dks, dvs, dgs, dbs, dbls, new_dst = [], [], [], [], [], [], []
        dng = jnp.zeros((1, DV), F32)
        for h in range(N_HEADS):
            ks = slice(h * DK, (h + 1) * DK)
            vs = slice(h * DV, (h + 1) * DV)
            qh, kh, kdh = q_in[:, ks], k_in[:, ks], k_dec[:, ks]
            vh = v[:, vs].astype(BF16)
            s = jnp.where(causal, _dot_nt(qh, kh), 0.0).astype(BF16)
            o = _dot(s, vh) + _dot_nt(qh, st_b[:, ks])
            rstd = lax.rsqrt(jnp.mean(o * o, axis=-1, keepdims=True) + RMS_EPS)
            nrm = o * rstd
            gh = g[:, vs]
            sg = _sigmoid(gh)
            dyh = dy[:, vs]
            dgs.append(dyh * nrm * ngv * (sg * (1.0 + gh * (1.0 - sg))))
            dt = dyh * (gh * sg)
            dng = dng + jnp.sum(dt * nrm, axis=0, keepdims=True)
            dn = dt * ngv
            do = (rstd * (dn - nrm * jnp.mean(dn * nrm, axis=-1, keepdims=True))).astype(BF16)
            da = jnp.where(causal, _dot_nt(do, vh), 0.0).astype(BF16)
            dvs.append(_dot_tn(s, do) + _dot_nt(kdh, dst_b[:, ks]))
            dq_in = _dot(da, kh) + _dot(do, st_b[:, ks])
            dk_in = _dot_tn(da, qh)
            dk_dec = _dot(vh, dst_b[:, ks])
            new_dst.append(_dot_tn(do, qh) + decay[:, ks] * dst[:, ks])
            dbls.append(jnp.sum(dk_dec * kd_f[:, ks], axis=0, keepdims=True)
                        + decay[:, ks] * jnp.sum(dst[:, ks] * st[:, ks], axis=0, keepdims=True))
            dqs.append(dq_in * (DK ** -0.5) * e_pos[:, ks])
            dks.append(dk_in * e_neg[:, ks] + dk_dec * e_dec[:, ks])
            dbs.append(dq_in * q_f[:, ks] - dk_in * k_f[:, ks] - dk_dec * kd_f[:, ks])
        dstate[...] = jnp.concatenate(new_dst, axis=1)
        row = lax.broadcasted_iota(jnp.int32, (CHUNK, 1), 0)
        db = jnp.concatenate(dbs, axis=1) + jnp.where(row == CHUNK - 1, jnp.concatenate(dbls, axis=1), 0.0)
        da_log = jnp.dot(_tri(False).astype(F32), db, preferred_element_type=F32, precision=lax.Precision.HIGHEST)
        dz = jnp.where(live, da_log * (1.0 - _sigmoid(z)) * (1.0 / GATE_TAU), 0.0)
        dz_b = dz.astype(BF16)
        du_ref[:, 0:GLA_K] = jnp.concatenate(dqs, axis=1).astype(BF16)
        du_ref[:, GLA_K:2 * GLA_K] = jnp.concatenate(dks, axis=1).astype(BF16)
        du_ref[:, 2 * GLA_K:2 * GLA_K + GLA_V] = jnp.concatenate(dvs, axis=1).astype(BF16)
        du_ref[:, 2 * GLA_K + GLA_V:2 * GLA_K + 2 * GLA_V] = jnp.concatenate(dgs, axis=1).astype(BF16)
        du_ref[:, 2 * GLA_K + 2 * GLA_V:] = _dot_nt(dz_b, w2_ref[...]).astype(BF16)
        dw2_ref[...] += _dot_tn(lr.astype(BF16), dz_b)
        dgb_ref[...] += jnp.sum(dz, axis=0, keepdims=True)
        dng_ref[...] += dng

    rev = lambda w, col: pl.BlockSpec((CHUNK, w), lambda b, n: (b * nc + nc - 1 - n, col))
    return _call(
        body, name="gla_bwd", grid=(n_ex, nc),
        in_specs=[rev(GLA_V, 1), rev(2 * GLA_K, 2), rev(GLA_V, 3), rev(GLA_V, 4), rev(128, 20),
                  pl.BlockSpec((DV, GLA_K), lambda b, n: (b * nc + nc - 1 - n, 0)),
                  _const_spec((128, GLA_K)), _const_spec((1, GLA_K)), _const_spec((1, DV))],
        out_specs=[rev(D_GLA_IN, 0), _acc_spec((128, GLA_K)), _acc_spec((1, GLA_K)), _acc_spec((1, DV))],
        out_shape=[jax.ShapeDtypeStruct((t, D_GLA_IN), BF16), jax.ShapeDtypeStruct((128, GLA_K), F32),
                   jax.ShapeDtypeStruct((1, GLA_K), F32), jax.ShapeDtypeStruct((1, DV), F32)],
        scratch_shapes=[pltpu.VMEM((DV, GLA_K), F32)],
        plan=plan,
    )(dycat, u, u, u, u, states, w2, gb, ng)


def _in_proj_bwd(du_conv, du_gla, w_in_conv, w_in_gla, h0, dh1, g_mix, plan=None):
    t = h0.shape[0]
    r = _row_tile(t, 384)

    def body(dc_ref, dg_ref, wc_ref, wg_ref, h_ref, dh1_ref, g_ref, dh0_ref, dgm_ref):
        @pl.when(pl.program_id(0) == 0)
        def _():
            dgm_ref[...] = jnp.zeros_like(dgm_ref)

        dhn = _dot_nt(dc_ref[...], wc_ref[...]) + _dot_nt(dg_ref[...], wg_ref[...])
        h = h_ref[...]
        rstd = lax.rsqrt(jnp.mean(h * h, axis=-1, keepdims=True) + RMS_EPS)
        nrm = h * rstd
        dgm_ref[...] += jnp.sum(dhn * nrm, axis=0, keepdims=True)
        dn = dhn * g_ref[...]
        dh0_ref[...] = dh1_ref[...] + rstd * (dn - nrm * jnp.mean(dn * nrm, axis=-1, keepdims=True))

    rows = lambda w: pl.BlockSpec((r, w), lambda i: (i, 0))
    return _call(
        body, name="in_proj_bwd", grid=(t // r,),
        in_specs=[rows(2 * C_CONV), rows(D_GLA_IN), _const_spec((D, 2 * C_CONV)), _const_spec((D, D_GLA_IN)),
                  rows(D), rows(D), _const_spec((1, D))],
        out_specs=[rows(D), _acc_spec((1, D))],
        out_shape=[jax.ShapeDtypeStruct((t, D), F32), jax.ShapeDtypeStruct((1, D), F32)],
        plan=plan,
    )(du_conv, du_gla, w_in_conv, w_in_gla, h0, dh1, g_mix)


def _wgrad(x, dy, name):
    t, m = x.shape
    n = dy.shape[1]
    tk = _row_tile(t, 384)
    tm = m if m <= D_GLA_IN else m // 2
    tn = n

    def body(x_ref, dy_ref, o_ref):
        @pl.when(pl.program_id(2) == 0)
        def _():
            o_ref[...] = jnp.zeros_like(o_ref)

        o_ref[...] += _dot_tn(x_ref[...].astype(BF16), dy_ref[...].astype(BF16))

    return pl.pallas_call(
        body, name=name, grid=(m // tm, n // tn, t // tk),
        in_specs=[pl.BlockSpec((tk, tm), lambda i, j, k: (k, i)), pl.BlockSpec((tk, tn), lambda i, j, k: (k, j))],
        out_specs=pl.BlockSpec((tm, tn), lambda i, j, k: (i, j)),
        out_shape=jax.ShapeDtypeStruct((m, n), F32),
        compiler_params=_params(3),
    )(x, dy)


def _mesh_pos():
    return lax.axis_index("x"), lax.axis_index("y"), lax.axis_index("c")


def _other_chips(x, y):
    return [(1 - x, y), (x, 1 - y), (1 - x, 1 - y)]


HBM_SPEC = pl.BlockSpec(memory_space=pltpu.HBM)


def _gather_shards(shards):
    n = len(shards)

    def body(*refs):
        ins, outs = refs[:n], refs[n:2 * n]
        send_sems, recv_sems, local_sems = refs[2 * n:]
        x, y, c = _mesh_pos()
        mine = 2 * x + y
        chips = _other_chips(x, y)
        local = [pltpu.make_async_copy(ins[a], outs[a].at[mine], local_sems.at[a]) for a in range(n)]
        for cp in local:
            cp.start()

        def remote(a, k, block):
            px, py = chips[k]
            return pltpu.make_async_remote_copy(
                src_ref=ins[a], dst_ref=outs[a].at[block], send_sem=send_sems.at[3 * a + k],
                recv_sem=recv_sems.at[3 * a + k], device_id=(px, py, c), device_id_type=MESH)

        sends = [remote(a, k, mine) for a in range(n) for k in range(3)]
        for cp in sends:
            cp.start()
        for a in range(n):
            for k, (px, py) in enumerate(chips):
                remote(a, k, 2 * px + py).wait_recv()
        for cp in sends:
            cp.wait_send()
        for cp in local:
            cp.wait()

    return pl.pallas_call(
        body, name="gather_shards",
        in_specs=[HBM_SPEC] * n, out_specs=[HBM_SPEC] * n,
        out_shape=[jax.ShapeDtypeStruct((N_CHIPS,) + s.shape, s.dtype) for s in shards],
        scratch_shapes=[pltpu.SemaphoreType.DMA((3 * n,)), pltpu.SemaphoreType.DMA((3 * n,)),
                        pltpu.SemaphoreType.DMA((n,))],
        compiler_params=pltpu.CompilerParams(has_side_effects=True),
    )(*shards)


def _send_half_to_sibling(g2):
    def body(g_ref, recv_ref, send_sem, recv_sem):
        x, y, c = _mesh_pos()
        cp = pltpu.make_async_remote_copy(
            src_ref=g_ref.at[1 - c], dst_ref=recv_ref, send_sem=send_sem, recv_sem=recv_sem,
            device_id=(x, y, 1 - c), device_id_type=MESH)
        cp.start()
        cp.wait()

    return pl.pallas_call(
        body, name="rs_to_sibling", in_specs=[HBM_SPEC], out_specs=HBM_SPEC,
        out_shape=jax.ShapeDtypeStruct(g2.shape[1:], g2.dtype),
        scratch_shapes=[pltpu.SemaphoreType.DMA(()), pltpu.SemaphoreType.DMA(())],
        compiler_params=pltpu.CompilerParams(has_side_effects=True),
    )(g2)


def _add_own_half(g2, recv, c):
    rows = N_CHIPS * HALF_ROWS
    tr = 512
    g2f = g2.reshape(2, rows, D)
    recvf = recv.reshape(rows, D)

    def body(c_ref, a_ref, b_ref, o_ref):
        o_ref[...] = a_ref[0] + b_ref[...]

    out = pl.pallas_call(
        body, name="rs_add_halves",
        grid_spec=pltpu.PrefetchScalarGridSpec(
            num_scalar_prefetch=1, grid=(rows // tr,),
            in_specs=[pl.BlockSpec((1, tr, D), lambda i, s: (s[0], i, 0)), pl.BlockSpec((tr, D), lambda i, s: (i, 0))],
            out_specs=pl.BlockSpec((tr, D), lambda i, s: (i, 0))),
        out_shape=jax.ShapeDtypeStruct((rows, D), F32),
        compiler_params=_params(1),
    )(jnp.reshape(c, (1,)).astype(jnp.int32), g2f, recvf)
    return out.reshape(N_CHIPS, HALF_ROWS, D)


def _exchange_chip_sums(p):
    def body(p_ref, out_ref, send_sems, recv_sems, local_sem):
        x, y, c = _mesh_pos()
        mine = 2 * x + y
        chips = _other_chips(x, y)
        local = pltpu.make_async_copy(p_ref.at[mine], out_ref.at[mine], local_sem)
        local.start()

        def remote(k, src_block, dst_block):
            px, py = chips[k]
            return pltpu.make_async_remote_copy(
                src_ref=p_ref.at[src_block], dst_ref=out_ref.at[dst_block], send_sem=send_sems.at[k],
                recv_sem=recv_sems.at[k], device_id=(px, py, c), device_id_type=MESH)

        sends = [remote(k, 2 * px + py, mine) for k, (px, py) in enumerate(chips)]
        for cp in sends:
            cp.start()
        for k, (px, py) in enumerate(chips):
            remote(k, mine, 2 * px + py).wait_recv()
        for cp in sends:
            cp.wait_send()
        local.wait()

    return pl.pallas_call(
        body, name="rs_chip_exchange", in_specs=[HBM_SPEC], out_specs=HBM_SPEC,
        out_shape=jax.ShapeDtypeStruct(p.shape, p.dtype),
        scratch_shapes=[pltpu.SemaphoreType.DMA((3,)), pltpu.SemaphoreType.DMA((3,)), pltpu.SemaphoreType.DMA(())],
        compiler_params=pltpu.CompilerParams(has_side_effects=True),
    )(p)


def _sum_chips(parts):
    tr = 512

    def body(p_ref, o_ref):
        o_ref[...] = ((p_ref[0] + p_ref[1]) + p_ref[2]) + p_ref[3]

    return pl.pallas_call(
        body, name="rs_sum_chips", grid=(HALF_ROWS // tr,),
        in_specs=[pl.BlockSpec((N_CHIPS, tr, D), lambda i: (0, i, 0))],
        out_specs=pl.BlockSpec((tr, D), lambda i: (i, 0)),
        out_shape=jax.ShapeDtypeStruct((HALF_ROWS, D), F32),
        compiler_params=_params(1),
    )(parts)


def _share_with_sibling(half):
    def body(h_ref, out_ref, send_sem, recv_sem, local_sem):
        x, y, c = _mesh_pos()
        local = pltpu.make_async_copy(h_ref, out_ref.at[c], local_sem)
        local.start()
        cp = pltpu.make_async_remote_copy(
            src_ref=h_ref, dst_ref=out_ref.at[c], send_sem=send_sem, recv_sem=recv_sem,
            device_id=(x, y, 1 - c), device_id_type=MESH)
        cp.start()
        pltpu.make_async_remote_copy(
            src_ref=h_ref, dst_ref=out_ref.at[1 - c], send_sem=send_sem, recv_sem=recv_sem,
            device_id=(x, y, 1 - c), device_id_type=MESH).wait_recv()
        cp.wait_send()
        local.wait()

    return pl.pallas_call(
        body, name="rs_share_sibling", in_specs=[HBM_SPEC], out_specs=HBM_SPEC,
        out_shape=jax.ShapeDtypeStruct((2,) + half.shape, half.dtype),
        scratch_shapes=[pltpu.SemaphoreType.DMA(()), pltpu.SemaphoreType.DMA(()), pltpu.SemaphoreType.DMA(())],
        compiler_params=pltpu.CompilerParams(has_side_effects=True),
    )(half)


def _adam_update(g, w, m, v):
    m2 = ADAM_B1 * m + (1.0 - ADAM_B1) * g
    v2 = ADAM_B2 * v + (1.0 - ADAM_B2) * (g * g)
    m_hat = m2 / (1.0 - ADAM_B1 ** ADAM_STEP)
    v_hat = v2 / (1.0 - ADAM_B2 ** ADAM_STEP)
    delta = -ADAM_LR * (m_hat / (jnp.sqrt(v_hat) + ADAM_EPS) + ADAM_WD * w)
    return delta, m2, v2


def _adamw_slab(g, w, m, v):
    rows = g.shape[0]
    tr = 256

    def body(g_ref, w_ref, m_ref, v_ref, d_ref, m2_ref, v2_ref):
        d_ref[...], m2_ref[...], v2_ref[...] = _adam_update(g_ref[...], w_ref[...], m_ref[...], v_ref[...])

    spec = pl.BlockSpec((tr, D), lambda i: (i, 0))
    return pl.pallas_call(
        body, name="adamw_slab", grid=(rows // tr,), in_specs=[spec] * 4, out_specs=[spec] * 3,
        out_shape=[jax.ShapeDtypeStruct((rows, D), F32)] * 3,
        compiler_params=_params(1),
    )(g, w, m, v)


def _allreduce_small_adamw(part, w, m, v):
    def body(p_ref, w_ref, m_ref, v_ref, g_ref, d_ref, m2_ref, v2_ref, slots, send_sems, recv_sems):
        x, y, c = _mesh_pos()
        mine = 4 * x + 2 * y + c
        peers = [(px, py, pc) for px in (x, 1 - x) for py in (y, 1 - y) for pc in (c, 1 - c)][1:]

        def remote(k, slot):
            return pltpu.make_async_remote_copy(
                src_ref=p_ref, dst_ref=slots.at[slot], send_sem=send_sems.at[k], recv_sem=recv_sems.at[k],
                device_id=peers[k], device_id_type=MESH)

        sends = [remote(k, mine) for k in range(7)]
        for cp in sends:
            cp.start()
        slots[mine] = p_ref[...]
        for k, (px, py, pc) in enumerate(peers):
            remote(k, 4 * px + 2 * py + pc).wait_recv()
        for cp in sends:
            cp.wait_send()
        g = slots[0]
        for d in range(1, 8):
            g = g + slots[d]
        g_ref[...] = g
        d_ref[...], m2_ref[...], v2_ref[...] = _adam_update(g, w_ref[...], m_ref[...], v_ref[...])

    vm = pl.BlockSpec(memory_space=pltpu.VMEM)
    shape = jax.ShapeDtypeStruct(part.shape, F32)
    return pl.pallas_call(
        body, name="small_allreduce_adamw", in_specs=[vm] * 4, out_specs=[vm] * 4, out_shape=[shape] * 4,
        scratch_shapes=[pltpu.VMEM((8,) + part.shape, F32), pltpu.SemaphoreType.DMA((7,)),
                        pltpu.SemaphoreType.DMA((7,))],
        compiler_params=pltpu.CompilerParams(has_side_effects=True),
    )(part, w, m, v)


def _half(ref, c, axis):
    n = ref.shape[axis] // 2
    return ref.at[(slice(None),) * axis + (pl.ds(c * n, n),)]


def _remote(src, dst, send_sem, recv_sem, device):
    return pltpu.make_async_remote_copy(src_ref=src, dst_ref=dst, send_sem=send_sem, recv_sem=recv_sem,
                                        device_id=device, device_id_type=MESH)


def _gather_weights(split, axes, whole):
    ns, n = len(split), len(split) + len(whole)

    def body(*refs):
        ins, outs = refs[:n], refs[n:2 * n]
        ici_send, ici_recv, d2d_send, d2d_recv, local_sems = refs[2 * n:]
        x, y, c = _mesh_pos()
        mine = 2 * x + y
        chips = _other_chips(x, y)
        local = [pltpu.make_async_copy(ins[a], outs[a].at[mine], local_sems.at[a]) for a in range(n)]
        for cp in local:
            cp.start()

        def ici(a, k, block):
            px, py = chips[k]
            src, dst = ins[a], outs[a].at[block]
            if a < ns:
                src, dst = _half(src, c, axes[a]), _half(dst, c, axes[a])
            return _remote(src, dst, ici_send.at[3 * a + k], ici_recv.at[3 * a + k], (px, py, c))

        def d2d(a, k, block, half):
            part = _half(outs[a].at[block], half, axes[a])
            return _remote(part, part, d2d_send.at[3 * a + k], d2d_recv.at[3 * a + k], (x, y, 1 - c))

        sends = [ici(a, k, mine) for a in range(n) for k in range(3)]
        for cp in sends:
            cp.start()
        for a in range(n):
            for k, (px, py) in enumerate(chips):
                ici(a, k, 2 * px + py).wait_recv()
                if a < ns:
                    sends.append(d2d(a, k, 2 * px + py, c))
                    sends[-1].start()
        for a in range(ns):
            for k, (px, py) in enumerate(chips):
                d2d(a, k, 2 * px + py, 1 - c).wait_recv()
        for cp in sends:
            cp.wait_send()
        for cp in local:
            cp.wait()

    arrays = list(split) + list(whole)
    return pl.pallas_call(
        body, name="gather_weights", in_specs=[HBM_SPEC] * n, out_specs=[HBM_SPEC] * n,
        out_shape=[jax.ShapeDtypeStruct((N_CHIPS,) + s.shape, s.dtype) for s in arrays],
        scratch_shapes=[pltpu.SemaphoreType.DMA((3 * n,)), pltpu.SemaphoreType.DMA((3 * n,)),
                        pltpu.SemaphoreType.DMA((3 * ns,)), pltpu.SemaphoreType.DMA((3 * ns,)),
                        pltpu.SemaphoreType.DMA((n,))],
        compiler_params=pltpu.CompilerParams(has_side_effects=True),
    )(*arrays)


def _rs_to_sibling(gs):
    n = len(gs)

    def body(*refs):
        ins, outs, send_sems, recv_sems = refs[:n], refs[n:2 * n], refs[2 * n], refs[2 * n + 1]
        x, y, c = _mesh_pos()
        copies = [_remote(_half(ins[a], 1 - c, 2), outs[a], send_sems.at[a], recv_sems.at[a], (x, y, 1 - c))
                  for a in range(n)]
        for cp in copies:
            cp.start()
        for cp in copies:
            cp.wait()

    return pl.pallas_call(
        body, name="rs_to_sibling", in_specs=[HBM_SPEC] * n, out_specs=[HBM_SPEC] * n,
        out_shape=[jax.ShapeDtypeStruct(g.shape[:2] + (g.shape[2] // 2,), g.dtype) for g in gs],
        scratch_shapes=[pltpu.SemaphoreType.DMA((n,)), pltpu.SemaphoreType.DMA((n,))],
        compiler_params=pltpu.CompilerParams(has_side_effects=True),
    )(*gs)


def _rs_add_halves(g, recv, c, name):
    _, rows, w = g.shape
    h = w // 2
    tr = rows // 2 if rows % 16 == 0 and rows > 64 else rows

    def body(c_ref, a_ref, b_ref, o_ref):
        o_ref[...] = (a_ref[...] + b_ref[...]).astype(BF16)

    return pl.pallas_call(
        body, name=name,
        grid_spec=pltpu.PrefetchScalarGridSpec(
            num_scalar_prefetch=1, grid=(N_CHIPS, rows // tr),
            in_specs=[pl.BlockSpec((1, tr, h), lambda j, i, s: (j, i, s[0])),
                      pl.BlockSpec((1, tr, h), lambda j, i, s: (j, i, 0))],
            out_specs=pl.BlockSpec((1, tr, h), lambda j, i, s: (j, i, 0))),
        out_shape=jax.ShapeDtypeStruct((N_CHIPS, rows, h), BF16),
        compiler_params=_params(2),
    )(jnp.reshape(c, (1,)).astype(jnp.int32), g, recv)


def _rs_chip_exchange(ps):
    n = len(ps)

    def body(*refs):
        ins, outs = refs[:n], refs[n:2 * n]
        send_sems, recv_sems, local_sems = refs[2 * n:]
        x, y, c = _mesh_pos()
        mine = 2 * x + y
        chips = _other_chips(x, y)
        local = [pltpu.make_async_copy(ins[a].at[mine], outs[a].at[mine], local_sems.at[a]) for a in range(n)]
        for cp in local:
            cp.start()

        def ici(a, k, src_block, dst_block):
            px, py = chips[k]
            return _remote(ins[a].at[src_block], outs[a].at[dst_block], send_sems.at[3 * a + k],
                           recv_sems.at[3 * a + k], (px, py, c))

        sends = [ici(a, k, 2 * px + py, mine) for a in range(n) for k, (px, py) in enumerate(chips)]
        for cp in sends:
            cp.start()
        for a in range(n):
            for k, (px, py) in enumerate(chips):
                ici(a, k, mine, 2 * px + py).wait_recv()
        for cp in sends:
            cp.wait_send()
        for cp in local:
            cp.wait()

    return pl.pallas_call(
        body, name="rs_chip_exchange", in_specs=[HBM_SPEC] * n, out_specs=[HBM_SPEC] * n,
        out_shape=[jax.ShapeDtypeStruct(p.shape, p.dtype) for p in ps],
        scratch_shapes=[pltpu.SemaphoreType.DMA((3 * n,)), pltpu.SemaphoreType.DMA((3 * n,)),
                        pltpu.SemaphoreType.DMA((n,))],
        compiler_params=pltpu.CompilerParams(has_side_effects=True),
    )(*ps)


def _rs_sum_chips(parts, name):
    _, rows, h = parts.shape
    tr = rows // 2 if rows % 16 == 0 and rows > 64 else rows

    def body(p_ref, o_ref):
        p = p_ref[...].astype(F32)
        o_ref[...] = ((p[0] + p[1]) + p[2]) + p[3]

    return pl.pallas_call(
        body, name=name, grid=(rows // tr,),
        in_specs=[pl.BlockSpec((N_CHIPS, tr, h), lambda i: (0, i, 0))],
        out_specs=pl.BlockSpec((tr, h), lambda i: (i, 0)),
        out_shape=jax.ShapeDtypeStruct((rows, h), F32),
        compiler_params=_params(1),
    )(parts)


def _rs_share(halves):
    n = len(halves)

    def body(*refs):
        ins, outs = refs[:n], refs[n:2 * n]
        send_sems, recv_sems, local_sems = refs[2 * n:]
        x, y, c = _mesh_pos()
        local = [pltpu.make_async_copy(ins[a], _half(outs[a], c, 1), local_sems.at[a]) for a in range(n)]
        for cp in local:
            cp.start()
        sends = [_remote(ins[a], _half(outs[a], c, 1), send_sems.at[a], recv_sems.at[a], (x, y, 1 - c))
                 for a in range(n)]
        for cp in sends:
            cp.start()
        for a in range(n):
            _remote(ins[a], _half(outs[a], 1 - c, 1), send_sems.at[a], recv_sems.at[a], (x, y, 1 - c)).wait_recv()
        for cp in sends:
            cp.wait_send()
        for cp in local:
            cp.wait()

    return pl.pallas_call(
        body, name="rs_share", in_specs=[HBM_SPEC] * n, out_specs=[HBM_SPEC] * n,
        out_shape=[jax.ShapeDtypeStruct((p.shape[0], 2 * p.shape[1]), p.dtype) for p in halves],
        scratch_shapes=[pltpu.SemaphoreType.DMA((n,)), pltpu.SemaphoreType.DMA((n,)),
                        pltpu.SemaphoreType.DMA((n,))],
        compiler_params=pltpu.CompilerParams(has_side_effects=True),
    )(*halves)


def _adamw(g, w, m, v, name):
    rows, cols = g.shape
    tr = 256 if rows % 256 == 0 else (rows // 2 if rows % 16 == 0 and rows > 64 else rows)

    def body(g_ref, w_ref, m_ref, v_ref, d_ref, m2_ref, v2_ref):
        d_ref[...], m2_ref[...], v2_ref[...] = _adam_update(g_ref[...], w_ref[...], m_ref[...], v_ref[...])

    spec = pl.BlockSpec((tr, cols), lambda i: (i, 0))
    return pl.pallas_call(
        body, name=name, grid=(rows // tr,), in_specs=[spec] * 4, out_specs=[spec] * 3,
        out_shape=[jax.ShapeDtypeStruct((rows, cols), F32)] * 3,
        compiler_params=_params(1),
    )(g, w, m, v)


def _rows_of(a):
    flat = a.reshape(-1)
    pad = (-flat.shape[0]) % D
    if pad:
        flat = jnp.concatenate([flat, jnp.zeros((pad,), flat.dtype)])
    return flat.reshape(-1, D)


SLAB_PARTS = (("w_in", (D, D_IN // N_CHIPS)), ("w_out", (D // N_CHIPS, D)), ("w_ffn_gate", (D, D_FF // N_CHIPS)),
              ("w_ffn_up", (D, D_FF // N_CHIPS)), ("w_ffn_down", (D_FF // N_CHIPS, D)),
              ("meta_tokens", (N_META, D // N_CHIPS)), ("conv_w", (CONV_W, C_CONV // N_CHIPS)),
              ("gla_w_gate2", (RANK, GLA_K // N_CHIPS)))


def _pack_slab(parts):
    rows = [_rows_of(parts[name].reshape(shape)) for name, shape in SLAB_PARTS]
    used = sum(r.shape[0] for r in rows)
    rows.append(jnp.zeros((SLAB_ROWS - used, D), F32))
    return jnp.concatenate(rows, axis=0)


def _unpack_slab(slab, lead):
    out, r0 = {}, 0
    for name, shape in SLAB_PARTS:
        size = shape[0] * shape[1]
        nrows = -(-size // D)
        out[name] = slab[r0:r0 + nrows].reshape(-1)[:size].reshape(lead[name] + shape)
        r0 += nrows
    return out


SMALL_PARTS = (("norm_mix_g", 0, 0, D), ("norm_ffn_g", 1, 0, D), ("norm_final_g", 2, 0, D),
               ("conv_b", 3, 0, C_CONV), ("conv_ln_g", 3, C_CONV, C_CONV), ("conv_ln_b", 4, 0, C_CONV),
               ("gla_gate_b", 4, C_CONV, GLA_K), ("gla_norm_g", 4, C_CONV + GLA_K, DV))


def _pack_small(parts):
    slab = jnp.zeros((SMALL_ROWS, D), F32)
    for name, row, col, size in SMALL_PARTS:
        slab = lax.dynamic_update_slice(slab, parts[name].reshape(1, size).astype(F32), (row, col))
    return slab


def _unpack_small(slab, shapes):
    return {name: slab[row, col:col + size].reshape(shapes[name]) for name, row, col, size in SMALL_PARTS}


def _column_block(full, j, width):
    return lax.dynamic_slice_in_dim(full, j * width, width, axis=1)


def _local_step(x, target, w):
    n_ex, seq, _ = x.shape
    lp = HEAD_ROWS + seq
    t = n_ex * lp
    meta = jnp.broadcast_to(w["meta_tokens"][None], (n_ex, N_META, D))
    h0 = jnp.concatenate([jnp.zeros((n_ex, PAD_ROWS, D), F32), meta, x], axis=1).reshape(t, D)
    tgt = jnp.concatenate([jnp.zeros((n_ex, HEAD_ROWS, D), F32), target], axis=1).reshape(t, D)
    row_mask = jnp.concatenate([jnp.zeros((n_ex, HEAD_ROWS, 1), F32), jnp.ones((n_ex, seq, 1), F32)],
                               axis=1).reshape(t, 1)

    u, hn = _in_proj(h0, w["norm_mix_g"], w["w_in"])
    yc, y_conv = _conv_fwd(u, w["conv_w"], w["conv_b"], w["conv_ln_g"], w["conv_ln_b"], n_ex, lp)
    y_gla, states = _gla_fwd(u, w["gla_w_gate2"], w["gla_gate_b"], w["gla_norm_g"], n_ex, lp)
    h1, hn2, gate, up, act = _mix_out_ffn_up(h0, y_conv, y_gla, w["w_out"], w["norm_ffn_g"],
                                             w["w_ffn_gate_t"], w["w_ffn_up_t"])
    dh2, loss, d_final_g = _ffn_down_loss(act, w["w_ffn_down"], h1, tgt, w["norm_final_g"], row_mask)

    dgate, dup, dh1, dycat, d_ffn_g = _ffn_bwd(dh2, gate, up, h1, w["w_ffn_down"], w["w_ffn_gate_t"],
                                                w["w_ffn_up_t"], w["w_out"], w["norm_ffn_g"])
    du_conv, d_conv_w, d_conv_b, d_ln_g, d_ln_b = _conv_bwd(dycat, yc, u, w["conv_w"], w["conv_ln_g"],
                                                            w["conv_ln_b"], n_ex, lp)
    du_gla, d_w2, d_gate_b, d_norm_g = _gla_bwd(dycat, u, states, w["gla_w_gate2"], w["gla_gate_b"],
                                                w["gla_norm_g"], n_ex, lp)
    dh0, d_mix_g = _in_proj_bwd(du_conv, du_gla, w["w_in"][:, :2 * C_CONV], w["w_in"][:, 2 * C_CONV:],
                                h0, dh1, w["norm_mix_g"])

    d_w_in_t = jnp.concatenate([_wgrad(du_conv, hn, "wgrad_in_conv"), _wgrad(du_gla, hn, "wgrad_in_gla")],
                               axis=0)[:D_IN]
    d_w_out = jnp.concatenate([_wgrad(y_conv, dh1, "wgrad_out_conv"), _wgrad(y_gla, dh1, "wgrad_out_gla")], axis=0)
    dh0 = dh0.reshape(n_ex, lp, D)
    grads = {
        "w_in_t": d_w_in_t, "w_out": d_w_out,
        "w_ffn_gate_t": _wgrad(dgate, hn2, "wgrad_gate"), "w_ffn_up_t": _wgrad(dup, hn2, "wgrad_up"),
        "w_ffn_down": _wgrad(act, dh2, "wgrad_down"),
        "meta_tokens": jnp.sum(dh0[:, PAD_ROWS:HEAD_ROWS], axis=0),
        "conv_w": d_conv_w, "gla_w_gate2": d_w2[:RANK],
        "norm_mix_g": d_mix_g, "norm_ffn_g": d_ffn_g, "norm_final_g": d_final_g,
        "conv_b": d_conv_b, "conv_ln_g": d_ln_g, "conv_ln_b": d_ln_b,
        "gla_gate_b": d_gate_b, "gla_norm_g": d_norm_g,
    }
    return loss[0, 0], dh0[:, HEAD_ROWS:], grads


WEIGHT_NAMES = ("meta_tokens", "norm_mix_g", "w_in", "conv_w", "conv_b", "conv_ln_g", "conv_ln_b", "gla_w_gate2",
                "gla_gate_b", "gla_norm_g", "w_out", "norm_ffn_g", "w_ffn_gate", "w_ffn_up", "w_ffn_down",
                "norm_final_g")
MATMUL_WEIGHTS = ("w_in", "w_out", "w_ffn_gate", "w_ffn_up", "w_ffn_down")
ROW_SHARDED = ("w_out", "w_ffn_down")


def _full_weights(ws):
    sh = lambda name: ws[name].reshape(ws[name].shape[-2:])
    split = [sh("w_in").astype(BF16), sh("w_out").astype(BF16), sh("w_ffn_gate").T.astype(BF16),
             sh("w_ffn_up").T.astype(BF16), sh("w_ffn_down").astype(BF16)]
    whole = [sh("meta_tokens"), sh("conv_w"), sh("gla_w_gate2")]
    w_in, w_out, gate_t, up_t, down, meta, conv_w, w2 = _gather_weights(split, [0, 0, 0, 0, 0], whole)
    cols = lambda a: jnp.concatenate([a[j] for j in range(N_CHIPS)], axis=1)
    full = {name: ws[name].reshape(1, -1) for name, _, _, _ in SMALL_PARTS}
    full["w_in"] = jnp.concatenate([cols(w_in), jnp.zeros((D, D_IN_PAD - D_IN), BF16)], axis=1)
    full["w_out"] = w_out.reshape(D, D)
    full["w_ffn_gate_t"] = gate_t.reshape(D_FF, D)
    full["w_ffn_up_t"] = up_t.reshape(D_FF, D)
    full["w_ffn_down"] = down.reshape(D_FF, D)
    full["meta_tokens"] = cols(meta)
    full["conv_w"] = jnp.concatenate([cols(conv_w), jnp.zeros((32 - CONV_W, C_CONV), F32)], axis=0)
    full["gla_w_gate2"] = jnp.concatenate([cols(w2), jnp.zeros((128 - RANK, GLA_K), F32)], axis=0).astype(BF16)
    return full


SMALL_RS_ROWS = 48


def _pack_small_sharded(grads):
    by_chip = lambda g, w: jnp.transpose(g.reshape(g.shape[0], N_CHIPS, w), (1, 0, 2))
    meta = by_chip(grads["meta_tokens"], D // N_CHIPS)
    conv = by_chip(grads["conv_w"], C_CONV // N_CHIPS).reshape(N_CHIPS, 16, 256)
    w2 = by_chip(grads["gla_w_gate2"], GLA_K // N_CHIPS).reshape(N_CHIPS, 4, 256)
    pad = jnp.zeros((N_CHIPS, SMALL_RS_ROWS - 36, 256), F32)
    return jnp.concatenate([meta, conv, w2, pad], axis=1)


def _unpack_small_sharded(g):
    return {"meta_tokens": g[0:16], "conv_w": g[16:32].reshape(32, C_CONV // N_CHIPS)[:CONV_W],
            "gla_w_gate2": g[32:36].reshape(RANK, GLA_K // N_CHIPS)}


def _kernel_without_overlap(x, meta_tokens, norm_mix_g, w_in, conv_w, conv_b, conv_ln_g, conv_ln_b, gla_w_gate2, gla_gate_b, gla_norm_g, w_out, norm_ffn_g, w_ffn_gate, w_ffn_up, w_ffn_down, norm_final_g, loss_target, m_meta_tokens, m_norm_mix_g, m_w_in, m_conv_w, m_conv_b, m_conv_ln_g, m_conv_ln_b, m_gla_w_gate2, m_gla_gate_b, m_gla_norm_g, m_w_out, m_norm_ffn_g, m_w_ffn_gate, m_w_ffn_up, m_w_ffn_down, m_norm_final_g, v_meta_tokens, v_norm_mix_g, v_w_in, v_conv_w, v_conv_b, v_conv_ln_g, v_conv_ln_b, v_gla_w_gate2, v_gla_gate_b, v_gla_norm_g, v_w_out, v_norm_ffn_g, v_w_ffn_gate, v_w_ffn_up, v_w_ffn_down, v_norm_final_g):
    ws = dict(zip(WEIGHT_NAMES, (meta_tokens, norm_mix_g, w_in, conv_w, conv_b, conv_ln_g, conv_ln_b, gla_w_gate2,
                                 gla_gate_b, gla_norm_g, w_out, norm_ffn_g, w_ffn_gate, w_ffn_up, w_ffn_down,
                                 norm_final_g)))
    ms = dict(zip(WEIGHT_NAMES, (m_meta_tokens, m_norm_mix_g, m_w_in, m_conv_w, m_conv_b, m_conv_ln_g, m_conv_ln_b,
                                 m_gla_w_gate2, m_gla_gate_b, m_gla_norm_g, m_w_out, m_norm_ffn_g, m_w_ffn_gate,
                                 m_w_ffn_up, m_w_ffn_down, m_norm_final_g)))
    vs = dict(zip(WEIGHT_NAMES, (v_meta_tokens, v_norm_mix_g, v_w_in, v_conv_w, v_conv_b, v_conv_ln_g, v_conv_ln_b,
                                 v_gla_w_gate2, v_gla_gate_b, v_gla_norm_g, v_w_out, v_norm_ffn_g, v_w_ffn_gate,
                                 v_w_ffn_up, v_w_ffn_down, v_norm_final_g)))
    c = lax.axis_index("c")

    full = _full_weights(ws)
    loss, grad_x, grads = _local_step(x, loss_target, full)
    loss = lax.psum(loss, ("x", "y", "c"))

    rs_names = ("w_in", "w_out", "w_ffn_gate", "w_ffn_up", "w_ffn_down", "small")
    by_owner = [grads["w_in_t"].reshape(N_CHIPS, D_IN // N_CHIPS, D), grads["w_out"].reshape(N_CHIPS, D // N_CHIPS, D),
                grads["w_ffn_gate_t"].reshape(N_CHIPS, D_FF // N_CHIPS, D),
                grads["w_ffn_up_t"].reshape(N_CHIPS, D_FF // N_CHIPS, D),
                grads["w_ffn_down"].reshape(N_CHIPS, D_FF // N_CHIPS, D), _pack_small_sharded(grads)]
    from_sibling = _rs_to_sibling(by_owner)
    chip_sums = [_rs_add_halves(g, r, c, "rs_add_" + nm) for g, r, nm in zip(by_owner, from_sibling, rs_names)]
    halves = [_rs_sum_chips(p, "rs_sum_" + nm) for p, nm in zip(_rs_chip_exchange(chip_sums), rs_names)]
    reduced = dict(zip(rs_names, _rs_share(halves)))
    g_sharded = {"w_in": reduced["w_in"].T, "w_out": reduced["w_out"], "w_ffn_gate": reduced["w_ffn_gate"].T,
                 "w_ffn_up": reduced["w_ffn_up"].T, "w_ffn_down": reduced["w_ffn_down"],
                 **_unpack_small_sharded(reduced["small"])}
    out = {"grad": {}, "delta": {}, "new_m": {}, "new_v": {}}
    for name, g in g_sharded.items():
        shape = ws[name].shape
        flat = lambda a: a.reshape(shape[-2:])
        delta, new_m, new_v = _adamw(g, flat(ws[name]), flat(ms[name]), flat(vs[name]), "adamw_" + name)
        for kind, a in (("grad", g), ("delta", delta), ("new_m", new_m), ("new_v", new_v)):
            out[kind][name] = a.reshape(shape)

    small_shapes = {name: ws[name].shape for name, _, _, _ in SMALL_PARTS}
    g_s, d_s, m_s, v_s = _allreduce_small_adamw(_pack_small(grads), _pack_small(ws), _pack_small(ms), _pack_small(vs))
    for kind, slab in (("grad", g_s), ("delta", d_s), ("new_m", m_s), ("new_v", v_s)):
        out[kind].update(_unpack_small(slab, small_shapes))

    return (loss, grad_x, *[out[kind][name] for kind in ("grad", "delta", "new_m", "new_v") for name in WEIGHT_NAMES])


def _gather_plan(split, whole=()):
    split, whole = list(split), list(whole)
    ns, n = len(split), len(split) + len(whole)

    def make(ins, outs, sems):
        ici_send, ici_recv, d2d_send, d2d_recv, local_sems = sems
        x, y, c = _mesh_pos()
        mine = 2 * x + y
        chips = _other_chips(x, y)
        blocks = [2 * px + py for px, py in chips]

        def local(a, p):
            rows = arrays[a].shape[0] // pieces[a]
            part = pl.ds(p * rows, rows)
            return pltpu.make_async_copy(ins[a].at[part], outs[a].at[mine, part], local_sems.at[LOCAL_PIECES * a + p])

        def ici(a, k, block):
            px, py = chips[k]
            src, dst = ins[a], outs[a].at[block]
            if a < ns:
                src, dst = _half(src, c, 0), _half(dst, c, 0)
            return _remote(src, dst, ici_send.at[3 * a + k], ici_recv.at[3 * a + k], (px, py, c))

        def d2d(a, k, half):
            part = _half(outs[a].at[blocks[k]], half, 0)
            return _remote(part, part, d2d_send.at[3 * a + k], d2d_recv.at[3 * a + k], (x, y, 1 - c))

        def start():
            for a in range(n):
                for k in range(3):
                    ici(a, k, mine).start()
                for p in range(pieces[a]):
                    local(a, p).start()

        def finish():
            for a in range(n):
                for k in range(3):
                    ici(a, k, blocks[k]).wait_recv()
                    if a < ns:
                        d2d(a, k, c).start()
            for a in range(ns):
                for k in range(3):
                    d2d(a, k, 1 - c).wait_recv()
            for a in range(n):
                for k in range(3):
                    ici(a, k, mine).wait_send()
                    if a < ns:
                        d2d(a, k, c).wait_send()
                for p in range(pieces[a]):
                    local(a, p).wait()

        return start, finish

    arrays = split + whole
    pieces = [max(p for p in (LOCAL_PIECES, 4, 2, 1) if p == 1 or s.shape[0] % (16 * p) == 0) for s in arrays]
    return _Plan(arrays, [jax.ShapeDtypeStruct((N_CHIPS,) + s.shape, s.dtype) for s in arrays],
                 [pltpu.SemaphoreType.DMA((3 * n,)), pltpu.SemaphoreType.DMA((3 * n,)),
                  pltpu.SemaphoreType.DMA((3 * ns,)), pltpu.SemaphoreType.DMA((3 * ns,)),
                  pltpu.SemaphoreType.DMA((LOCAL_PIECES * n,))], make)


LOCAL_PIECES = 8


def _to_sibling_plan(gs):
    n = len(gs)

    def make(ins, outs, sems):
        send_sems, recv_sems = sems
        x, y, c = _mesh_pos()

        def copy(a):
            return _remote(_half(ins[a], 1 - c, 2), outs[a], send_sems.at[a], recv_sems.at[a], (x, y, 1 - c))

        def start():
            for a in range(n):
                copy(a).start()

        def finish():
            for a in range(n):
                copy(a).wait()

        return start, finish

    return _Plan(list(gs), [jax.ShapeDtypeStruct(g.shape[:2] + (g.shape[2] // 2,), g.dtype) for g in gs],
                 [pltpu.SemaphoreType.DMA((n,)), pltpu.SemaphoreType.DMA((n,))], make)


def _chip_exchange_plan(ps):
    n = len(ps)

    def make(ins, outs, sems):
        send_sems, recv_sems = sems
        x, y, c = _mesh_pos()
        chips = _other_chips(x, y)

        def ici(a, k):
            px, py = chips[k]
            return _remote(ins[a].at[2 * px + py], outs[a].at[k], send_sems.at[3 * a + k],
                           recv_sems.at[3 * a + k], (px, py, c))

        def start():
            for a in range(n):
                for k in range(3):
                    ici(a, k).start()

        def finish():
            for a in range(n):
                for k in range(3):
                    ici(a, k).wait()

        return start, finish

    return _Plan(list(ps), [jax.ShapeDtypeStruct((3,) + p.shape[1:], p.dtype) for p in ps],
                 [pltpu.SemaphoreType.DMA((3 * n,)), pltpu.SemaphoreType.DMA((3 * n,))], make)


def _share_plan(halves):
    n = len(halves)

    def make(ins, outs, sems):
        send_sems, recv_sems = sems
        x, y, c = _mesh_pos()

        def d2d(a):
            return _remote(ins[a], outs[a], send_sems.at[a], recv_sems.at[a], (x, y, 1 - c))

        def start():
            for a in range(n):
                d2d(a).start()

        def finish():
            for a in range(n):
                d2d(a).wait()

        return start, finish

    return _Plan(list(halves), [jax.ShapeDtypeStruct(p.shape, p.dtype) for p in halves],
                 [pltpu.SemaphoreType.DMA((n,)), pltpu.SemaphoreType.DMA((n,))], make)


def _rs_sum(own, others, mine, name):
    _, rows, h = own.shape
    tr = rows // 2 if rows % 16 == 0 and rows > 64 else rows

    def body(mine_ref, own_ref, oth_ref, o_ref):
        p = oth_ref[...].astype(F32)
        o_ref[...] = ((own_ref[0].astype(F32) + p[0]) + p[1]) + p[2]

    return pl.pallas_call(
        body, name=name,
        grid_spec=pltpu.PrefetchScalarGridSpec(
            num_scalar_prefetch=1, grid=(rows // tr,),
            in_specs=[pl.BlockSpec((1, tr, h), lambda i, s: (s[0], i, 0)),
                      pl.BlockSpec((3, tr, h), lambda i, s: (0, i, 0))],
            out_specs=pl.BlockSpec((tr, h), lambda i, s: (i, 0))),
        out_shape=jax.ShapeDtypeStruct((rows, h), F32),
        compiler_params=_params(1),
    )(jnp.reshape(mine, (1,)).astype(jnp.int32), own, others)


def _join(mine, theirs, c):
    return jnp.where(c == 0, jnp.concatenate([mine, theirs], axis=1), jnp.concatenate([theirs, mine], axis=1))


def _exchange(plan, name):
    n_in, n_out = len(plan.arrays), len(plan.out_shape)

    def body(*refs):
        start, finish = plan.make(refs[:n_in], refs[n_in:n_in + n_out], refs[n_in + n_out:])
        start()
        finish()

    return pl.pallas_call(
        body, name=name, in_specs=[HBM_SPEC] * n_in, out_specs=[HBM_SPEC] * n_out, out_shape=list(plan.out_shape),
        scratch_shapes=list(plan.sems), compiler_params=pltpu.CompilerParams(has_side_effects=True),
    )(*plan.arrays)


def _adamw_halves(mine, theirs, c, w, m, v, name):
    rows, h = mine.shape
    tr = 256 if rows % 256 == 0 else rows // 2

    def body(c_ref, a_ref, b_ref, w_ref, m_ref, v_ref, go_ref, d_ref, m2_ref, v2_ref):
        first = c_ref[0] == 0
        a, b = a_ref[...], b_ref[...]
        g = jnp.concatenate([jnp.where(first, a, b), jnp.where(first, b, a)], axis=1)
        go_ref[...] = g
        d_ref[...], m2_ref[...], v2_ref[...] = _adam_update(g, w_ref[...], m_ref[...], v_ref[...])

    half = pl.BlockSpec((tr, h), lambda i, s: (i, 0))
    spec = pl.BlockSpec((tr, 2 * h), lambda i, s: (i, 0))
    return pl.pallas_call(
        body, name=name,
        grid_spec=pltpu.PrefetchScalarGridSpec(num_scalar_prefetch=1, grid=(rows // tr,),
                                               in_specs=[half, half, spec, spec, spec], out_specs=[spec] * 4),
        out_shape=[jax.ShapeDtypeStruct((rows, 2 * h), F32)] * 4,
        compiler_params=_params(1),
    )(jnp.reshape(c, (1,)).astype(jnp.int32), mine, theirs, w, m, v)


def _columns(gathered):
    return jnp.concatenate([gathered[j] for j in range(N_CHIPS)], axis=1)


def kernel(x, meta_tokens, norm_mix_g, w_in, conv_w, conv_b, conv_ln_g, conv_ln_b, gla_w_gate2, gla_gate_b, gla_norm_g, w_out, norm_ffn_g, w_ffn_gate, w_ffn_up, w_ffn_down, norm_final_g, loss_target, m_meta_tokens, m_norm_mix_g, m_w_in, m_conv_w, m_conv_b, m_conv_ln_g, m_conv_ln_b, m_gla_w_gate2, m_gla_gate_b, m_gla_norm_g, m_w_out, m_norm_ffn_g, m_w_ffn_gate, m_w_ffn_up, m_w_ffn_down, m_norm_final_g, v_meta_tokens, v_norm_mix_g, v_w_in, v_conv_w, v_conv_b, v_conv_ln_g, v_conv_ln_b, v_gla_w_gate2, v_gla_gate_b, v_gla_norm_g, v_w_out, v_norm_ffn_g, v_w_ffn_gate, v_w_ffn_up, v_w_ffn_down, v_norm_final_g):
    ws = dict(zip(WEIGHT_NAMES, (meta_tokens, norm_mix_g, w_in, conv_w, conv_b, conv_ln_g, conv_ln_b, gla_w_gate2,
                                 gla_gate_b, gla_norm_g, w_out, norm_ffn_g, w_ffn_gate, w_ffn_up, w_ffn_down,
                                 norm_final_g)))
    ms = dict(zip(WEIGHT_NAMES, (m_meta_tokens, m_norm_mix_g, m_w_in, m_conv_w, m_conv_b, m_conv_ln_g, m_conv_ln_b,
                                 m_gla_w_gate2, m_gla_gate_b, m_gla_norm_g, m_w_out, m_norm_ffn_g, m_w_ffn_gate,
                                 m_w_ffn_up, m_w_ffn_down, m_norm_final_g)))
    vs = dict(zip(WEIGHT_NAMES, (v_meta_tokens, v_norm_mix_g, v_w_in, v_conv_w, v_conv_b, v_conv_ln_g, v_conv_ln_b,
                                 v_gla_w_gate2, v_gla_gate_b, v_gla_norm_g, v_w_out, v_norm_ffn_g, v_w_ffn_gate,
                                 v_w_ffn_up, v_w_ffn_down, v_norm_final_g)))
    c = lax.axis_index("c")
    shard = lambda d, name: d[name].reshape(d[name].shape[-2:])
    vec = {name: ws[name].reshape(1, -1) for name, _, _, _ in SMALL_PARTS}
    n_ex, seq, _ = x.shape
    lp = HEAD_ROWS + seq
    t = n_ex * lp

    w_in_g, meta_g, conv_w_g, w2_g = _exchange(
        _gather_plan([shard(ws, "w_in").astype(BF16)],
                     [shard(ws, "meta_tokens"), shard(ws, "conv_w"), shard(ws, "gla_w_gate2")]), "gather_first")
    w_in_full = jnp.concatenate([_columns(w_in_g), jnp.zeros((D, D_IN_PAD - D_IN), BF16)], axis=1)
    conv_w_full = jnp.concatenate([_columns(conv_w_g), jnp.zeros((32 - CONV_W, C_CONV), F32)], axis=0)
    w2_full = jnp.concatenate([_columns(w2_g), jnp.zeros((128 - RANK, GLA_K), F32)], axis=0).astype(BF16)

    meta = jnp.broadcast_to(_columns(meta_g)[None], (n_ex, N_META, D))
    h0 = jnp.concatenate([jnp.zeros((n_ex, PAD_ROWS, D), F32), meta, x], axis=1).reshape(t, D)
    tgt = jnp.concatenate([jnp.zeros((n_ex, HEAD_ROWS, D), F32), loss_target], axis=1).reshape(t, D)
    row_mask = jnp.concatenate([jnp.zeros((n_ex, HEAD_ROWS, 1), F32), jnp.ones((n_ex, seq, 1), F32)],
                               axis=1).reshape(t, 1)

    (u, hn), (w_out_g,) = _in_proj(h0, vec["norm_mix_g"], w_in_full,
                                   plan=_gather_plan([shard(ws, "w_out").astype(BF16)]))
    (yc, y_conv), (gate_g, up_g) = _conv_fwd(
        u, conv_w_full, vec["conv_b"], vec["conv_ln_g"], vec["conv_ln_b"], n_ex, lp,
        plan=_gather_plan([shard(ws, "w_ffn_gate").T.astype(BF16), shard(ws, "w_ffn_up").T.astype(BF16)]))
    (y_gla, states), (down_g,) = _gla_fwd(u, w2_full, vec["gla_gate_b"], vec["gla_norm_g"], n_ex, lp,
                                          plan=_gather_plan([shard(ws, "w_ffn_down").astype(BF16)]))
    w_out_full, w_down_full = w_out_g.reshape(D, D), down_g.reshape(D_FF, D)
    w_gate_t, w_up_t = gate_g.reshape(D_FF, D), up_g.reshape(D_FF, D)

    h1, hn2, gate, up, act = _mix_out_ffn_up(h0, y_conv, y_gla, w_out_full, vec["norm_ffn_g"], w_gate_t, w_up_t)
    dh2, loss, d_final_g = _ffn_down_loss(act, w_down_full, h1, tgt, vec["norm_final_g"], row_mask)
    loss = lax.psum(loss[0, 0], ("x", "y", "c"))
    dgate, dup, dh1, dycat, d_ffn_g = _ffn_bwd(dh2, gate, up, h1, w_down_full, w_gate_t, w_up_t, w_out_full,
                                                vec["norm_ffn_g"])

    early = ("w_out", "w_ffn_gate", "w_ffn_up", "w_ffn_down")
    d_w_out = jnp.concatenate([_wgrad(y_conv, dh1, "wgrad_out_conv"), _wgrad(y_gla, dh1, "wgrad_out_gla")], axis=0)
    by_owner = [d_w_out.reshape(N_CHIPS, D // N_CHIPS, D),
                _wgrad(dgate, hn2, "wgrad_gate").reshape(N_CHIPS, D_FF // N_CHIPS, D),
                _wgrad(dup, hn2, "wgrad_up").reshape(N_CHIPS, D_FF // N_CHIPS, D),
                _wgrad(act, dh2, "wgrad_down").reshape(N_CHIPS, D_FF // N_CHIPS, D)]
    (du_conv, d_conv_w, d_conv_b, d_ln_g, d_ln_b), from_sibling = _conv_bwd(
        dycat, yc, u, conv_w_full, vec["conv_ln_g"], vec["conv_ln_b"], n_ex, lp, plan=_to_sibling_plan(by_owner))
    chip_sums = [_rs_add_halves(g, r, c, "rs_add_" + nm) for g, r, nm in zip(by_owner, from_sibling, early)]
    (du_gla, d_w2, d_gate_b, d_norm_g), exchanged = _gla_bwd(
        dycat, u, states, w2_full, vec["gla_gate_b"], vec["gla_norm_g"], n_ex, lp,
        plan=_chip_exchange_plan(chip_sums))
    mine = 2 * lax.axis_index("x") + lax.axis_index("y")
    halves = [_rs_sum(own, oth, mine, "rs_sum_" + nm) for own, oth, nm in zip(chip_sums, exchanged, early)]
    (dh0, d_mix_g), shared = _in_proj_bwd(du_conv, du_gla, w_in_full[:, :2 * C_CONV], w_in_full[:, 2 * C_CONV:],
                                          h0, dh1, vec["norm_mix_g"], plan=_share_plan(halves))
    dh0 = dh0.reshape(n_ex, lp, D)
    grad_x = dh0[:, HEAD_ROWS:]

    out = {"grad": {}, "delta": {}, "new_m": {}, "new_v": {}}

    def update(name, g=None, halves=None):
        shape = ws[name].shape
        w2d, m2d, v2d = shard(ws, name), shard(ms, name), shard(vs, name)
        if halves is not None:
            res = _adamw_halves(*halves, c, w2d, m2d, v2d, "adamw_" + name)
        else:
            res = [g, *_adamw(g, w2d, m2d, v2d, "adamw_" + name)]
        for kind, a in zip(("grad", "delta", "new_m", "new_v"), res):
            out[kind][name] = a.reshape(shape)

    update("w_out", halves=(halves[0], shared[0]))
    update("w_ffn_gate", g=_join(halves[1], shared[1], c).T)
    update("w_ffn_up", g=_join(halves[2], shared[2], c).T)
    update("w_ffn_down", halves=(halves[3], shared[3]))

    small = {"norm_mix_g": d_mix_g, "norm_ffn_g": d_ffn_g, "norm_final_g": d_final_g, "conv_b": d_conv_b,
             "conv_ln_g": d_ln_g, "conv_ln_b": d_ln_b, "gla_gate_b": d_gate_b, "gla_norm_g": d_norm_g}
    small_shapes = {name: ws[name].shape for name, _, _, _ in SMALL_PARTS}
    g_s, d_s, m_s, v_s = _allreduce_small_adamw(_pack_small(small), _pack_small(ws), _pack_small(ms), _pack_small(vs))
    for kind, slab in (("grad", g_s), ("delta", d_s), ("new_m", m_s), ("new_v", v_s)):
        out[kind].update(_unpack_small(slab, small_shapes))

    d_w_in_t = jnp.concatenate([_wgrad(du_conv, hn, "wgrad_in_conv"), _wgrad(du_gla, hn, "wgrad_in_gla")],
                               axis=0)[:D_IN]
    small_sharded = {"meta_tokens": jnp.sum(dh0[:, PAD_ROWS:HEAD_ROWS], axis=0), "conv_w": d_conv_w,
                     "gla_w_gate2": d_w2[:RANK]}
    late = ("w_in", "small")
    by_owner = [d_w_in_t.reshape(N_CHIPS, D_IN // N_CHIPS, D), _pack_small_sharded(small_sharded)]
    from_sibling = _exchange(_to_sibling_plan(by_owner), "rs_late_to_sibling")
    chip_sums = [_rs_add_halves(g, r, c, "rs_add_" + nm) for g, r, nm in zip(by_owner, from_sibling, late)]
    exchanged = _exchange(_chip_exchange_plan(chip_sums), "rs_late_chip_exchange")
    halves = [_rs_sum(own, oth, mine, "rs_sum_" + nm) for own, oth, nm in zip(chip_sums, exchanged, late)]
    shared = _exchange(_share_plan(halves), "rs_late_share")
    update("w_in", g=_join(halves[0], shared[0], c).T)
    for name, g in _unpack_small_sharded(_join(halves[1], shared[1], c)).items():
        update(name, g=g)

    return (loss, grad_x, *[out[kind][name] for kind in ("grad", "delta", "new_m", "new_v") for name in WEIGHT_NAMES])
```

```python
import functools
from typing import Any, Callable, NamedTuple, Sequence

import jax
import jax.numpy as jnp
from jax import lax
from jax.experimental import pallas as pl
from jax.experimental.pallas import tpu as pltpu

F32 = jnp.float32
BF16 = jnp.bfloat16
MESH = pl.DeviceIdType.MESH

D = 1024
N_META = 16
C_CONV = 512
CONV_W = 31
GLA_K = 256
GLA_V = 512
N_HEADS = 4
DK = 64
DV = 128
RANK = 16
CHUNK = 64
PAD_ROWS = CHUNK - N_META
HEAD_ROWS = CHUNK
D_IN = 2576
D_IN_PAD = 2688
D_GLA_IN = D_IN_PAD - 2 * C_CONV
D_FF = 2816
RMS_EPS = 1e-6
LN_EPS = 1e-5
GATE_TAU = 16.0
N_CHIPS = 4

ADAM_LR = 0.001
ADAM_B1 = 0.9
ADAM_B2 = 0.999
ADAM_EPS = 1e-08
ADAM_WD = 0.01
ADAM_STEP = 10

V7X_VMEM_BYTES = 64 * 1024 * 1024
VMEM_LIMIT = V7X_VMEM_BYTES - 8 * 1024 * 1024

SLAB_ROWS = 3072
HALF_ROWS = SLAB_ROWS // 2
SMALL_ROWS = 8


def _dot(a, b):
    return jnp.dot(a, b, preferred_element_type=F32)


def _dot_nt(a, b):
    return lax.dot_general(a, b, (((1,), (1,)), ((), ())), preferred_element_type=F32)


def _dot_tn(a, b):
    return lax.dot_general(a, b, (((0,), (0,)), ((), ())), preferred_element_type=F32)


def _sigmoid(x):
    return 1.0 / (1.0 + jnp.exp(-x))


def _const_spec(shape):
    return pl.BlockSpec(shape, lambda *_: (0,) * len(shape), pipeline_mode=pl.Buffered(1))


def _acc_spec(shape):
    return pl.BlockSpec(shape, lambda *_: (0,) * len(shape))


def _params(n_axes):
    return pltpu.CompilerParams(dimension_semantics=("arbitrary",) * n_axes, vmem_limit_bytes=VMEM_LIMIT)


def _row_tile(t, want):
    for r in (want, 384, 192, 128, 64):
        if r <= want and t % r == 0:
            return r
    raise ValueError(f"no row tile for {t}")


class _Plan(NamedTuple):
    arrays: Sequence[Any]
    out_shape: Sequence[Any]
    sems: Sequence[Any]
    make: Callable


def _call(body, *, name, grid, in_specs, out_specs, out_shape, scratch_shapes=(), plan=None):
    n_in, n_out, n_scr = len(in_specs), len(out_specs), len(scratch_shapes)
    if plan is None:
        plan = _Plan([], [], [], lambda ins, outs, sems: (lambda: None, lambda: None))
    nx_in, nx_out = len(plan.arrays), len(plan.out_shape)

    def hosted(*refs):
        ins, xins = refs[:n_in], refs[n_in:n_in + nx_in]
        o0 = n_in + nx_in
        outs, xouts = refs[o0:o0 + n_out], refs[o0 + n_out:o0 + n_out + nx_out]
        s0 = o0 + n_out + nx_out
        scr, sems = refs[s0:s0 + n_scr], refs[s0 + n_scr:]
        ids = [pl.program_id(a) for a in range(len(grid))]
        first = functools.reduce(jnp.logical_and, [i == 0 for i in ids])
        last = functools.reduce(jnp.logical_and, [i == g - 1 for i, g in zip(ids, grid)])
        start, finish = plan.make(xins, xouts, sems)
        pl.when(first)(start)
        body(*ins, *outs, *scr)
        pl.when(last)(finish)

    call = pl.pallas_call(
        hosted, name=name, grid=grid, in_specs=list(in_specs) + [HBM_SPEC] * nx_in,
        out_specs=list(out_specs) + [HBM_SPEC] * nx_out, out_shape=list(out_shape) + list(plan.out_shape),
        scratch_shapes=list(scratch_shapes) + list(plan.sems),
        compiler_params=pltpu.CompilerParams(dimension_semantics=("arbitrary",) * len(grid),
                                             vmem_limit_bytes=VMEM_LIMIT, has_side_effects=nx_in > 0))

    def run(*args):
        res = call(*args, *plan.arrays)
        return res[:n_out], res[n_out:]

    return run


def _in_proj(h0, g_mix, w_in, plan=None):
    t = h0.shape[0]
    r = _row_tile(t, 384)

    def body(h_ref, g_ref, w_ref, u_ref, hn_ref):
        h = h_ref[...]
        rstd = lax.rsqrt(jnp.mean(h * h, axis=-1, keepdims=True) + RMS_EPS)
        hn = (h * rstd * g_ref[...]).astype(BF16)
        hn_ref[...] = hn
        u_ref[...] = _dot(hn, w_ref[...])

    return _call(
        body, name="in_proj", grid=(t // r,),
        in_specs=[pl.BlockSpec((r, D), lambda i: (i, 0)), _const_spec((1, D)), _const_spec((D, D_IN_PAD))],
        out_specs=[pl.BlockSpec((r, D_IN_PAD), lambda i: (i, 0)), pl.BlockSpec((r, D), lambda i: (i, 0))],
        out_shape=[jax.ShapeDtypeStruct((t, D_IN_PAD), F32), jax.ShapeDtypeStruct((t, D), BF16)],
        plan=plan,
    )(h0, g_mix, w_in)


CONV_TILE = 192
CONV_SUB = 32
CONV_LEAD = CONV_SUB - (CONV_W - 1)


def _conv_fwd(u, conv_w, conv_b, ln_g, ln_b, n_ex, lp, plan=None):
    r = CONV_TILE
    nt = lp // r
    hb = r // CONV_SUB

    def body(cur_ref, prev_ref, w_ref, b_ref, lg_ref, lb_ref, yc_ref, y_ref, glu):
        i = pl.program_id(1)
        cur = cur_ref[...]
        glu[CONV_SUB:CONV_SUB + r, :] = cur[:, :C_CONV] * _sigmoid(cur[:, C_CONV:])
        pv = prev_ref[...]
        halo = pv[:, :C_CONV] * _sigmoid(pv[:, C_CONV:])
        glu[0:CONV_SUB, :] = jnp.where(i > 0, halo, 0.0)
        w = w_ref[...]
        for j in range(r // CONV_SUB):
            r0 = j * CONV_SUB
            acc = jnp.zeros((CONV_SUB, C_CONV), F32) + b_ref[...]
            for k in range(CONV_W):
                acc = acc + w[k:k + 1, :] * glu[r0 + CONV_LEAD + k:r0 + CONV_LEAD + k + CONV_SUB, :]
            mu = jnp.mean(acc, axis=-1, keepdims=True)
            cen = acc - mu
            var = jnp.mean(cen * cen, axis=-1, keepdims=True)
            out = cen * lax.rsqrt(var + LN_EPS) * lg_ref[...] + lb_ref[...]
            y = out * _sigmoid(out)
            row = i * r + r0 + lax.broadcasted_iota(jnp.int32, (CONV_SUB, 1), 0)
            y = jnp.where(row >= PAD_ROWS, y, 0.0)
            yc_ref[r0:r0 + CONV_SUB, :] = acc
            y_ref[r0:r0 + CONV_SUB, :] = y.astype(BF16)

    t = n_ex * lp
    return _call(
        body, name="conv_fwd", grid=(n_ex, nt),
        in_specs=[pl.BlockSpec((r, 2 * C_CONV), lambda b, i: (b * nt + i, 0)),
                  pl.BlockSpec((CONV_SUB, 2 * C_CONV), lambda b, i: (jnp.maximum((b * nt + i) * hb - 1, 0), 0)),
                  _const_spec((32, C_CONV)), _const_spec((1, C_CONV)), _const_spec((1, C_CONV)), _const_spec((1, C_CONV))],
        out_specs=[pl.BlockSpec((r, C_CONV), lambda b, i: (b * nt + i, 0)),
                   pl.BlockSpec((r, C_CONV), lambda b, i: (b * nt + i, 0))],
        out_shape=[jax.ShapeDtypeStruct((t, C_CONV), F32), jax.ShapeDtypeStruct((t, C_CONV), BF16)],
        scratch_shapes=[pltpu.VMEM((r + CONV_SUB, C_CONV), F32)],
        plan=plan,
    )(u, u, conv_w, conv_b, ln_g, ln_b)


def _gla_gates(lr, w2, gb, first_chunk):
    z = _dot(lr.astype(BF16), w2) + gb
    a = (jnp.minimum(z, 0.0) - jnp.log(1.0 + jnp.exp(-jnp.abs(z)))) * (1.0 / GATE_TAU)
    row = lax.broadcasted_iota(jnp.int32, (CHUNK, 1), 0)
    live = jnp.logical_or(jnp.logical_not(first_chunk), row >= PAD_ROWS)
    return z, jnp.where(live, a, 0.0), live


def _tri(lower):
    i = lax.broadcasted_iota(jnp.int32, (CHUNK, CHUNK), 0)
    j = lax.broadcasted_iota(jnp.int32, (CHUNK, CHUNK), 1)
    return (i >= j) if lower else (i <= j)


def _gla_fwd(u, w2, gb, ng, n_ex, lp, plan=None):
    nc = lp // CHUNK
    t = n_ex * lp

    def body(qk_ref, v_ref, g_ref, lr_ref, w2_ref, gb_ref, ng_ref, y_ref, st_ref, state):
        n = pl.program_id(1)

        @pl.when(n == 0)
        def _():
            state[...] = jnp.zeros_like(state)

        st = state[...]
        st_ref[...] = st
        qk = qk_ref[...]
        q, k = qk[:, :GLA_K], qk[:, GLA_K:]
        _, a, _ = _gla_gates(lr_ref[...], w2_ref[...], gb_ref[...], n == 0)
        causal = _tri(True)
        b = jnp.dot(causal.astype(F32), a, preferred_element_type=F32, precision=lax.Precision.HIGHEST)
        bl = b[CHUNK - 1:CHUNK, :]
        q_in = (q * (DK ** -0.5) * jnp.exp(b)).astype(BF16)
        k_in = (k * jnp.exp(-b)).astype(BF16)
        k_dec = (k * jnp.exp(bl - b)).astype(BF16)
        decay = jnp.exp(bl)
        v = v_ref[...]
        g = g_ref[...]
        st_b = st.astype(BF16)
        ys, new = [], []
        for h in range(N_HEADS):
            ks = slice(h * DK, (h + 1) * DK)
            vs = slice(h * DV, (h + 1) * DV)
            vh = v[:, vs].astype(BF16)
            s = jnp.where(causal, _dot_nt(q_in[:, ks], k_in[:, ks]), 0.0)
            o = _dot(s.astype(BF16), vh) + _dot_nt(q_in[:, ks], st_b[:, ks])
            new.append(decay[:, ks] * st[:, ks] + _dot_tn(vh, k_dec[:, ks]))
            rstd = lax.rsqrt(jnp.mean(o * o, axis=-1, keepdims=True) + RMS_EPS)
            gh = g[:, vs]
            ys.append(o * rstd * ng_ref[...] * (gh * _sigmoid(gh)))
        state[...] = jnp.concatenate(new, axis=1)
        y_ref[...] = jnp.concatenate(ys, axis=1).astype(BF16)

    blk = lambda w, col: pl.BlockSpec((CHUNK, w), lambda b, n: (b * nc + n, col))
    return _call(
        body, name="gla_fwd", grid=(n_ex, nc),
        in_specs=[blk(2 * GLA_K, 2), blk(GLA_V, 3), blk(GLA_V, 4), blk(128, 20),
                  _const_spec((128, GLA_K)), _const_spec((1, GLA_K)), _const_spec((1, DV))],
        out_specs=[pl.BlockSpec((CHUNK, GLA_V), lambda b, n: (b * nc + n, 0)),
                   pl.BlockSpec((DV, GLA_K), lambda b, n: (b * nc + n, 0))],
        out_shape=[jax.ShapeDtypeStruct((t, GLA_V), BF16), jax.ShapeDtypeStruct((n_ex * nc * DV, GLA_K), F32)],
        scratch_shapes=[pltpu.VMEM((DV, GLA_K), F32)],
        plan=plan,
    )(u, u, u, u, w2, gb, ng)


FFN_TILE = 192


def _mix_out_ffn_up(h0, y_conv, y_gla, w_out, g_ffn, w_gate, w_up):
    t = h0.shape[0]
    r = _row_tile(t, FFN_TILE)

    def body(h0_ref, yc_ref, yg_ref, wo_ref, g_ref, wg_ref, wu_ref, h1_ref, hn_ref, gate_ref, up_ref, act_ref):
        h1 = h0_ref[...] + _dot(yc_ref[...], wo_ref[0:C_CONV, :]) + _dot(yg_ref[...], wo_ref[C_CONV:D, :])
        h1_ref[...] = h1
        rstd = lax.rsqrt(jnp.mean(h1 * h1, axis=-1, keepdims=True) + RMS_EPS)
        hn = (h1 * rstd * g_ref[...]).astype(BF16)
        hn_ref[...] = hn
        gate = _dot(hn, wg_ref[...])
        up = _dot(hn, wu_ref[...])
        gate_ref[...] = gate
        up_ref[...] = up
        act_ref[...] = (gate * _sigmoid(gate) * up).astype(BF16)

    rows = lambda w: pl.BlockSpec((r, w), lambda i: (i, 0))
    return pl.pallas_call(
        body, name="mix_out_ffn_up", grid=(t // r,),
        in_specs=[rows(D), rows(C_CONV), rows(GLA_V), _const_spec((D, D)), _const_spec((1, D)),
                  _const_spec((D, D_FF)), _const_spec((D, D_FF))],
        out_specs=[rows(D), rows(D), rows(D_FF), rows(D_FF), rows(D_FF)],
        out_shape=[jax.ShapeDtypeStruct((t, D), F32), jax.ShapeDtypeStruct((t, D), BF16),
                   jax.ShapeDtypeStruct((t, D_FF), F32), jax.ShapeDtypeStruct((t, D_FF), F32),
                   jax.ShapeDtypeStruct((t, D_FF), BF16)],
        compiler_params=_params(1),
    )(h0, y_conv, y_gla, w_out, g_ffn, w_gate, w_up)


def _ffn_down_loss(act, w_down, h1, target, g_final, row_mask):
    t = h1.shape[0]
    r = _row_tile(t, 384)

    def body(act_ref, wd_ref, h1_ref, tgt_ref, gf_ref, mask_ref, dh2_ref, loss_ref, dgf_ref):
        @pl.when(pl.program_id(0) == 0)
        def _():
            loss_ref[...] = jnp.zeros_like(loss_ref)
            dgf_ref[...] = jnp.zeros_like(dgf_ref)

        h2 = h1_ref[...] + _dot(act_ref[...], wd_ref[...])
        rstd = lax.rsqrt(jnp.mean(h2 * h2, axis=-1, keepdims=True) + RMS_EPS)
        nrm = h2 * rstd
        gf = gf_ref[...]
        err = (nrm * gf - tgt_ref[...]) * mask_ref[...]
        loss_ref[...] += jnp.sum(err * err) * (0.5 / D)
        dy = err * (1.0 / D)
        dgf_ref[...] += jnp.sum(dy * nrm, axis=0, keepdims=True)
        dn = dy * gf
        dh2_ref[...] = rstd * (dn - nrm * jnp.mean(dn * nrm, axis=-1, keepdims=True))

    rows = lambda w: pl.BlockSpec((r, w), lambda i: (i, 0))
    return pl.pallas_call(
        body, name="ffn_down_loss", grid=(t // r,),
        in_specs=[rows(D_FF), _const_spec((D_FF, D)), rows(D), rows(D), _const_spec((1, D)), rows(1)],
        out_specs=[rows(D), _acc_spec((1, 128)), _acc_spec((1, D))],
        out_shape=[jax.ShapeDtypeStruct((t, D), F32), jax.ShapeDtypeStruct((1, 128), F32),
                   jax.ShapeDtypeStruct((1, D), F32)],
        compiler_params=_params(1),
    )(act, w_down, h1, target, g_final, row_mask)


def _ffn_bwd(dh2, gate, up, h1, w_down_t, w_gate_t, w_up_t, w_out_t, g_ffn):
    t = h1.shape[0]
    r = _row_tile(t, FFN_TILE)

    def body(dh2_ref, gate_ref, up_ref, h1_ref, wd_ref, wg_ref, wu_ref, wo_ref, g_ref,
             dgate_ref, dup_ref, dh1_ref, dycat_ref, dg_ref):
        @pl.when(pl.program_id(0) == 0)
        def _():
            dg_ref[...] = jnp.zeros_like(dg_ref)

        dh2 = dh2_ref[...]
        dact = _dot(dh2.astype(BF16), wd_ref[...])
        gate = gate_ref[...]
        sg = _sigmoid(gate)
        dgate = (dact * up_ref[...] * (sg * (1.0 + gate * (1.0 - sg)))).astype(BF16)
        dup = (dact * (gate * sg)).astype(BF16)
        dgate_ref[...] = dgate
        dup_ref[...] = dup
        dhn = _dot(dgate, wg_ref[...]) + _dot(dup, wu_ref[...])
        h1 = h1_ref[...]
        rstd = lax.rsqrt(jnp.mean(h1 * h1, axis=-1, keepdims=True) + RMS_EPS)
        nrm = h1 * rstd
        dg_ref[...] += jnp.sum(dhn * nrm, axis=0, keepdims=True)
        dn = dhn * g_ref[...]
        dh1 = dh2 + rstd * (dn - nrm * jnp.mean(dn * nrm, axis=-1, keepdims=True))
        dh1_ref[...] = dh1
        dycat_ref[...] = _dot(dh1.astype(BF16), wo_ref[...])

    rows = lambda w: pl.BlockSpec((r, w), lambda i: (i, 0))
    return pl.pallas_call(
        body, name="ffn_bwd", grid=(t // r,),
        in_specs=[rows(D), rows(D_FF), rows(D_FF), rows(D), _const_spec((D, D_FF)), _const_spec((D_FF, D)),
                  _const_spec((D_FF, D)), _const_spec((D, D)), _const_spec((1, D))],
        out_specs=[rows(D_FF), rows(D_FF), rows(D), rows(D), _acc_spec((1, D))],
        out_shape=[jax.ShapeDtypeStruct((t, D_FF), BF16), jax.ShapeDtypeStruct((t, D_FF), BF16),
                   jax.ShapeDtypeStruct((t, D), F32), jax.ShapeDtypeStruct((t, D), F32),
                   jax.ShapeDtypeStruct((1, D), F32)],
        compiler_params=_params(1),
    )(dh2, gate, up, h1, w_down_t, w_gate_t, w_up_t, w_out_t, g_ffn)


def _conv_bwd(dycat, yc, u, conv_w, ln_g, ln_b, n_ex, lp, plan=None):
    r = CONV_TILE
    nt = lp // r
    hb = r // CONV_SUB
    nsub = r // CONV_SUB

    def ln_bwd(dy, yc_rows, live, lg, lb):
        mu = jnp.mean(yc_rows, axis=-1, keepdims=True)
        cen = yc_rows - mu
        rs = lax.rsqrt(jnp.mean(cen * cen, axis=-1, keepdims=True) + LN_EPS)
        yn = cen * rs
        out = yn * lg + lb
        so = _sigmoid(out)
        dout = jnp.where(live, dy * (so * (1.0 + out * (1.0 - so))), 0.0)
        dyn = dout * lg
        dyc = rs * (dyn - jnp.mean(dyn, axis=-1, keepdims=True) - yn * jnp.mean(dyn * yn, axis=-1, keepdims=True))
        return dyc, dout, yn

    def body(dy_ref, dyn_ref, yc_ref, ycn_ref, cur_ref, prev_ref, w_ref, lg_ref, lb_ref,
             du_ref, dw_ref, db_ref, dlg_ref, dlb_ref, glu, dycs, dwacc):
        b = pl.program_id(0)
        i = pl.program_id(1)
        first = jnp.logical_and(b == 0, i == 0)

        @pl.when(first)
        def _():
            dwacc[...] = jnp.zeros_like(dwacc)
            db_ref[...] = jnp.zeros_like(db_ref)
            dlg_ref[...] = jnp.zeros_like(dlg_ref)
            dlb_ref[...] = jnp.zeros_like(dlb_ref)

        lg, lb = lg_ref[...], lb_ref[...]
        cur = cur_ref[...]
        sig = _sigmoid(cur[:, C_CONV:])
        glu[CONV_SUB:CONV_SUB + r, :] = cur[:, :C_CONV] * sig
        pv = prev_ref[...]
        glu[0:CONV_SUB, :] = jnp.where(i > 0, pv[:, :C_CONV] * _sigmoid(pv[:, C_CONV:]), 0.0)

        row = i * r + lax.broadcasted_iota(jnp.int32, (r, 1), 0)
        dyc, dout, yn = ln_bwd(dy_ref[...], yc_ref[...], row >= PAD_ROWS, lg, lb)
        dycs[0:r, :] = dyc
        dycn, _, _ = ln_bwd(dyn_ref[...], ycn_ref[...], i < nt - 1, lg, lb)
        dycs[r:r + CONV_SUB, :] = dycn
        db_ref[...] += jnp.sum(dyc, axis=0, keepdims=True)
        dlg_ref[...] += jnp.sum(dout * yn, axis=0, keepdims=True)
        dlb_ref[...] += jnp.sum(dout, axis=0, keepdims=True)

        w = w_ref[...]
        for j in range(nsub):
            r0 = j * CONV_SUB
            dblk = dycs[r0:r0 + CONV_SUB, :]
            dglu = jnp.zeros((CONV_SUB, C_CONV), F32)
            for k in range(CONV_W):
                dglu = dglu + w[k:k + 1, :] * dycs[r0 + (CONV_W - 1) - k:r0 + (CONV_W - 1) - k + CONV_SUB, :]
                prod = dblk * glu[r0 + CONV_LEAD + k:r0 + CONV_LEAD + k + CONV_SUB, :]
                dwacc[k] += prod.reshape(CONV_SUB // 8, 8, C_CONV).sum(axis=0)
            sg = sig[r0:r0 + CONV_SUB, :]
            cv = cur[r0:r0 + CONV_SUB, :C_CONV]
            du_ref[r0:r0 + CONV_SUB, :C_CONV] = (dglu * sg).astype(BF16)
            du_ref[r0:r0 + CONV_SUB, C_CONV:] = (dglu * cv * sg * (1.0 - sg)).astype(BF16)

        @pl.when(jnp.logical_and(b == n_ex - 1, i == nt - 1))
        def _():
            dw_ref[...] = jnp.sum(dwacc[...], axis=1)

    t = n_ex * lp
    cur_rows = lambda w, col: pl.BlockSpec((r, w), lambda b, i: (b * nt + i, col))
    nxt_rows = lambda w, col: pl.BlockSpec(
        (CONV_SUB, w), lambda b, i: (jnp.minimum((b * nt + i + 1) * hb, n_ex * nt * hb - 1), col))
    return _call(
        body, name="conv_bwd", grid=(n_ex, nt),
        in_specs=[cur_rows(C_CONV, 0), nxt_rows(C_CONV, 0), cur_rows(C_CONV, 0), nxt_rows(C_CONV, 0),
                  cur_rows(2 * C_CONV, 0),
                  pl.BlockSpec((CONV_SUB, 2 * C_CONV), lambda b, i: (jnp.maximum((b * nt + i) * hb - 1, 0), 0)),
                  _const_spec((32, C_CONV)), _const_spec((1, C_CONV)), _const_spec((1, C_CONV))],
        out_specs=[cur_rows(2 * C_CONV, 0), _acc_spec((32, C_CONV)), _acc_spec((1, C_CONV)),
                   _acc_spec((1, C_CONV)), _acc_spec((1, C_CONV))],
        out_shape=[jax.ShapeDtypeStruct((t, 2 * C_CONV), BF16), jax.ShapeDtypeStruct((32, C_CONV), F32),
                   jax.ShapeDtypeStruct((1, C_CONV), F32), jax.ShapeDtypeStruct((1, C_CONV), F32),
                   jax.ShapeDtypeStruct((1, C_CONV), F32)],
        scratch_shapes=[pltpu.VMEM((r + CONV_SUB, C_CONV), F32), pltpu.VMEM((r + CONV_SUB, C_CONV), F32),
                        pltpu.VMEM((32, 8, C_CONV), F32)],
        plan=plan,
    )(dycat, dycat, yc, yc, u, u, conv_w, ln_g, ln_b)


def _gla_bwd(dycat, u, states, w2, gb, ng, n_ex, lp, plan=None):
    nc = lp // CHUNK
    t = n_ex * lp

    def body(dy_ref, qk_ref, v_ref, g_ref, lr_ref, st_ref, w2_ref, gb_ref, ng_ref,
             du_ref, dw2_ref, dgb_ref, dng_ref, dstate):
        bi = pl.program_id(0)
        n = pl.program_id(1)
        chunk = nc - 1 - n

        @pl.when(jnp.logical_and(bi == 0, n == 0))
        def _():
            dw2_ref[...] = jnp.zeros_like(dw2_ref)
            dgb_ref[...] = jnp.zeros_like(dgb_ref)
            dng_ref[...] = jnp.zeros_like(dng_ref)

        @pl.when(n == 0)
        def _():
            dstate[...] = jnp.zeros_like(dstate)

        qk = qk_ref[...]
        q, k = qk[:, :GLA_K], qk[:, GLA_K:]
        lr = lr_ref[...]
        z, a, live = _gla_gates(lr, w2_ref[...], gb_ref[...], chunk == 0)
        causal = _tri(True)
        b = jnp.dot(causal.astype(F32), a, preferred_element_type=F32, precision=lax.Precision.HIGHEST)
        bl = b[CHUNK - 1:CHUNK, :]
        e_pos, e_neg, e_dec = jnp.exp(b), jnp.exp(-b), jnp.exp(bl - b)
        q_f = q * (DK ** -0.5) * e_pos
        k_f = k * e_neg
        kd_f = k * e_dec
        q_in, k_in, k_dec = q_f.astype(BF16), k_f.astype(BF16), kd_f.astype(BF16)
        decay = jnp.exp(bl)
        v = v_ref[...]
        g = g_ref[...]
        dy = dy_ref[...]
        ngv = ng_ref[...]
        st = st_ref[...]
        st_b = st.astype(BF16)
        dst = dstate[...]
        dst_b = dst.astype(BF16)
        dqs, dks, dvs, dgs, dbs, dbls, new_dst = [], [], [], [], [], [], []
        dng = jnp.zeros((1, DV), F32)
        for h in range(N_HEADS):
            ks = slice(h * DK, (h + 1) * DK)
            vs = slice(h * DV, (h + 1) * DV)
            qh, kh, kdh = q_in[:, ks], k_in[:, ks], k_dec[:, ks]
            vh = v[:, vs].astype(BF16)
            s = jnp.where(causal, _dot_nt(qh, kh), 0.0).astype(BF16)
            o = _dot(s, vh) + _dot_nt(qh, st_b[:, ks])
            rstd = lax.rsqrt(jnp.mean(o * o, axis=-1, keepdims=True) + RMS_EPS)
            nrm = o * rstd
            gh = g[:, vs]
            sg = _sigmoid(gh)
            dyh = dy[:, vs]
            dgs.append(dyh * nrm * ngv * (sg * (1.0 + gh * (1.0 - sg))))
            dt = dyh * (gh * sg)
            dng = dng + jnp.sum(dt * nrm, axis=0, keepdims=True)
            dn = dt * ngv
            do = (rstd * (dn - nrm * jnp.mean(dn * nrm, axis=-1, keepdims=True))).astype(BF16)
            da = jnp.where(causal, _dot_nt(do, vh), 0.0).astype(BF16)
            dvs.append(_dot_tn(s, do) + _dot_nt(kdh, dst_b[:, ks]))
            dq_in = _dot(da, kh) + _dot(do, st_b[:, ks])
            dk_in = _dot_tn(da, qh)
            dk_dec = _dot(vh, dst_b[:, ks])
            new_dst.append(_dot_tn(do, qh) + decay[:, ks] * dst[:, ks])
            dbls.append(jnp.sum(dk_dec * kd_f[:, ks], axis=0, keepdims=True)
                        + decay[:, ks] * jnp.sum(dst[:, ks] * st[:, ks], axis=0, keepdims=True))
            dqs.append(dq_in * (DK ** -0.5) * e_pos[:, ks])
            dks.append(dk_in * e_neg[:, ks] + dk_dec * e_dec[:, ks])
            dbs.append(dq_in * q_f[:, ks] - dk_in * k_f[:, ks] - dk_dec * kd_f[:, ks])
        dstate[...] = jnp.concatenate(new_dst, axis=1)
        row = lax.broadcasted_iota(jnp.int32, (CHUNK, 1), 0)
        db = jnp.concatenate(dbs, axis=1) + jnp.where(row == CHUNK - 1, jnp.concatenate(dbls, axis=1), 0.0)
        da_log = jnp.dot(_tri(False).astype(F32), db, preferred_element_type=F32, precision=lax.Precision.HIGHEST)
        dz = jnp.where(live, da_log * (1.0 - _sigmoid(z)) * (1.0 / GATE_TAU), 0.0)
        dz_b = dz.astype(BF16)
        du_ref[:, 0:GLA_K] = jnp.concatenate(dqs, axis=1).astype(BF16)
        du_ref[:, GLA_K:2 * GLA_K] = jnp.concatenate(dks, axis=1).astype(BF16)
        du_ref[:, 2 * GLA_K:2 * GLA_K + GLA_V] = jnp.concatenate(dvs, axis=1).astype(BF16)
        du_ref[:, 2 * GLA_K + GLA_V:2 * GLA_K + 2 * GLA_V] = jnp.concatenate(dgs, axis=1).astype(BF16)
        du_ref[:, 2 * GLA_K + 2 * GLA_V:] = _dot_nt(dz_b, w2_ref[...]).astype(BF16)
        dw2_ref[...] += _dot_tn(lr.astype(BF16), dz_b)
        dgb_ref[...] += jnp.sum(dz, axis=0, keepdims=True)
        dng_ref[...] += dng

    rev = lambda w, col: pl.BlockSpec((CHUNK, w), lambda b, n: (b * nc + nc - 1 - n, col))
    return _call(
        body, name="gla_bwd", grid=(n_ex, nc),
        in_specs=[rev(GLA_V, 1), rev(2 * GLA_K, 2), rev(GLA_V, 3), rev(GLA_V, 4), rev(128, 20),
                  pl.BlockSpec((DV, GLA_K), lambda b, n: (b * nc + nc - 1 - n, 0)),
                  _const_spec((128, GLA_K)), _const_spec((1, GLA_K)), _const_spec((1, DV))],
        out_specs=[rev(D_GLA_IN, 0), _acc_spec((128, GLA_K)), _acc_spec((1, GLA_K)), _acc_spec((1, DV))],
        out_shape=[jax.ShapeDtypeStruct((t, D_GLA_IN), BF16), jax.ShapeDtypeStruct((128, GLA_K), F32),
                   jax.ShapeDtypeStruct((1, GLA_K), F32), jax.ShapeDtypeStruct((1, DV), F32)],
        scratch_shapes=[pltpu.VMEM((DV, GLA_K), F32)],
        plan=plan,
    )(dycat, u, u, u, u, states, w2, gb, ng)


def _in_proj_bwd(du_conv, du_gla, w_in_t_conv, w_in_t_gla, h0, dh1, g_mix, plan=None):
    t = h0.shape[0]
    r = _row_tile(t, 384)

    def body(dc_ref, dg_ref, wc_ref, wg_ref, h_ref, dh1_ref, g_ref, dh0_ref, dgm_ref):
        @pl.when(pl.program_id(0) == 0)
        def _():
            dgm_ref[...] = jnp.zeros_like(dgm_ref)

        dhn = _dot(dc_ref[...], wc_ref[...]) + _dot(dg_ref[...], wg_ref[...])
        h = h_ref[...]
        rstd = lax.rsqrt(jnp.mean(h * h, axis=-1, keepdims=True) + RMS_EPS)
        nrm = h * rstd
        dgm_ref[...] += jnp.sum(dhn * nrm, axis=0, keepdims=True)
        dn = dhn * g_ref[...]
        dh0_ref[...] = dh1_ref[...] + rstd * (dn - nrm * jnp.mean(dn * nrm, axis=-1, keepdims=True))

    rows = lambda w: pl.BlockSpec((r, w), lambda i: (i, 0))
    return _call(
        body, name="in_proj_bwd", grid=(t // r,),
        in_specs=[rows(2 * C_CONV), rows(D_GLA_IN), _const_spec((2 * C_CONV, D)), _const_spec((D_GLA_IN, D)),
                  rows(D), rows(D), _const_spec((1, D))],
        out_specs=[rows(D), _acc_spec((1, D))],
        out_shape=[jax.ShapeDtypeStruct((t, D), F32), jax.ShapeDtypeStruct((1, D), F32)],
        plan=plan,
    )(du_conv, du_gla, w_in_t_conv, w_in_t_gla, h0, dh1, g_mix)


def _wgrad(x, dy, name):
    t, m = x.shape
    n = dy.shape[1]
    tk = _row_tile(t, 384)
    tm = m if m <= D_GLA_IN else m // 2
    tn = n

    def body(x_ref, dy_ref, o_ref):
        @pl.when(pl.program_id(2) == 0)
        def _():
            o_ref[...] = jnp.zeros_like(o_ref)

        o_ref[...] += _dot_tn(x_ref[...].astype(BF16), dy_ref[...].astype(BF16))

    return pl.pallas_call(
        body, name=name, grid=(m // tm, n // tn, t // tk),
        in_specs=[pl.BlockSpec((tk, tm), lambda i, j, k: (k, i)), pl.BlockSpec((tk, tn), lambda i, j, k: (k, j))],
        out_specs=pl.BlockSpec((tm, tn), lambda i, j, k: (i, j)),
        out_shape=jax.ShapeDtypeStruct((m, n), F32),
        compiler_params=_params(3),
    )(x, dy)


def _mesh_pos():
    return lax.axis_index("x"), lax.axis_index("y"), lax.axis_index("c")


def _other_chips(x, y):
    return [(1 - x, y), (x, 1 - y), (1 - x, 1 - y)]


HBM_SPEC = pl.BlockSpec(memory_space=pltpu.HBM)


def _gather_shards(shards):
    n = len(shards)

    def body(*refs):
        ins, outs = refs[:n], refs[n:2 * n]
        send_sems, recv_sems, local_sems = refs[2 * n:]
        x, y, c = _mesh_pos()
        mine = 2 * x + y
        chips = _other_chips(x, y)
        local = [pltpu.make_async_copy(ins[a], outs[a].at[mine], local_sems.at[a]) for a in range(n)]
        for cp in local:
            cp.start()

        def remote(a, k, block):
            px, py = chips[k]
            return pltpu.make_async_remote_copy(
                src_ref=ins[a], dst_ref=outs[a].at[block], send_sem=send_sems.at[3 * a + k],
                recv_sem=recv_sems.at[3 * a + k], device_id=(px, py, c), device_id_type=MESH)

        sends = [remote(a, k, mine) for a in range(n) for k in range(3)]
        for cp in sends:
            cp.start()
        for a in range(n):
            for k, (px, py) in enumerate(chips):
                remote(a, k, 2 * px + py).wait_recv()
        for cp in sends:
            cp.wait_send()
        for cp in local:
            cp.wait()

    return pl.pallas_call(
        body, name="gather_shards",
        in_specs=[HBM_SPEC] * n, out_specs=[HBM_SPEC] * n,
        out_shape=[jax.ShapeDtypeStruct((N_CHIPS,) + s.shape, s.dtype) for s in shards],
        scratch_shapes=[pltpu.SemaphoreType.DMA((3 * n,)), pltpu.SemaphoreType.DMA((3 * n,)),
                        pltpu.SemaphoreType.DMA((n,))],
        compiler_params=pltpu.CompilerParams(has_side_effects=True),
    )(*shards)


def _send_half_to_sibling(g2):
    def body(g_ref, recv_ref, send_sem, recv_sem):
        x, y, c = _mesh_pos()
        cp = pltpu.make_async_remote_copy(
            src_ref=g_ref.at[1 - c], dst_ref=recv_ref, send_sem=send_sem, recv_sem=recv_sem,
            device_id=(x, y, 1 - c), device_id_type=MESH)
        cp.start()
        cp.wait()

    return pl.pallas_call(
        body, name="rs_to_sibling", in_specs=[HBM_SPEC], out_specs=HBM_SPEC,
        out_shape=jax.ShapeDtypeStruct(g2.shape[1:], g2.dtype),
        scratch_shapes=[pltpu.SemaphoreType.DMA(()), pltpu.SemaphoreType.DMA(())],
        compiler_params=pltpu.CompilerParams(has_side_effects=True),
    )(g2)


def _add_own_half(g2, recv, c):
    rows = N_CHIPS * HALF_ROWS
    tr = 512
    g2f = g2.reshape(2, rows, D)
    recvf = recv.reshape(rows, D)

    def body(c_ref, a_ref, b_ref, o_ref):
        o_ref[...] = a_ref[0] + b_ref[...]

    out = pl.pallas_call(
        body, name="rs_add_halves",
        grid_spec=pltpu.PrefetchScalarGridSpec(
            num_scalar_prefetch=1, grid=(rows // tr,),
            in_specs=[pl.BlockSpec((1, tr, D), lambda i, s: (s[0], i, 0)), pl.BlockSpec((tr, D), lambda i, s: (i, 0))],
            out_specs=pl.BlockSpec((tr, D), lambda i, s: (i, 0))),
        out_shape=jax.ShapeDtypeStruct((rows, D), F32),
        compiler_params=_params(1),
    )(jnp.reshape(c, (1,)).astype(jnp.int32), g2f, recvf)
    return out.reshape(N_CHIPS, HALF_ROWS, D)


def _exchange_chip_sums(p):
    def body(p_ref, out_ref, send_sems, recv_sems, local_sem):
        x, y, c = _mesh_pos()
        mine = 2 * x + y
        chips = _other_chips(x, y)
        local = pltpu.make_async_copy(p_ref.at[mine], out_ref.at[mine], local_sem)
        local.start()

        def remote(k, src_block, dst_block):
            px, py = chips[k]
            return pltpu.make_async_remote_copy(
                src_ref=p_ref.at[src_block], dst_ref=out_ref.at[dst_block], send_sem=send_sems.at[k],
                recv_sem=recv_sems.at[k], device_id=(px, py, c), device_id_type=MESH)

        sends = [remote(k, 2 * px + py, mine) for k, (px, py) in enumerate(chips)]
        for cp in sends:
            cp.start()
        for k, (px, py) in enumerate(chips):
            remote(k, mine, 2 * px + py).wait_recv()
        for cp in sends:
            cp.wait_send()
        local.wait()

    return pl.pallas_call(
        body, name="rs_chip_exchange", in_specs=[HBM_SPEC], out_specs=HBM_SPEC,
        out_shape=jax.ShapeDtypeStruct(p.shape, p.dtype),
        scratch_shapes=[pltpu.SemaphoreType.DMA((3,)), pltpu.SemaphoreType.DMA((3,)), pltpu.SemaphoreType.DMA(())],
        compiler_params=pltpu.CompilerParams(has_side_effects=True),
    )(p)


def _sum_chips(parts):
    tr = 512

    def body(p_ref, o_ref):
        o_ref[...] = ((p_ref[0] + p_ref[1]) + p_ref[2]) + p_ref[3]

    return pl.pallas_call(
        body, name="rs_sum_chips", grid=(HALF_ROWS // tr,),
        in_specs=[pl.BlockSpec((N_CHIPS, tr, D), lambda i: (0, i, 0))],
        out_specs=pl.BlockSpec((tr, D), lambda i: (i, 0)),
        out_shape=jax.ShapeDtypeStruct((HALF_ROWS, D), F32),
        compiler_params=_params(1),
    )(parts)


def _share_with_sibling(half):
    def body(h_ref, out_ref, send_sem, recv_sem, local_sem):
        x, y, c = _mesh_pos()
        local = pltpu.make_async_copy(h_ref, out_ref.at[c], local_sem)
        local.start()
        cp = pltpu.make_async_remote_copy(
            src_ref=h_ref, dst_ref=out_ref.at[c], send_sem=send_sem, recv_sem=recv_sem,
            device_id=(x, y, 1 - c), device_id_type=MESH)
        cp.start()
        pltpu.make_async_remote_copy(
            src_ref=h_ref, dst_ref=out_ref.at[1 - c], send_sem=send_sem, recv_sem=recv_sem,
            device_id=(x, y, 1 - c), device_id_type=MESH).wait_recv()
        cp.wait_send()
        local.wait()

    return pl.pallas_call(
        body, name="rs_share_sibling", in_specs=[HBM_SPEC], out_specs=HBM_SPEC,
        out_shape=jax.ShapeDtypeStruct((2,) + half.shape, half.dtype),
        scratch_shapes=[pltpu.SemaphoreType.DMA(()), pltpu.SemaphoreType.DMA(()), pltpu.SemaphoreType.DMA(())],
        compiler_params=pltpu.CompilerParams(has_side_effects=True),
    )(half)


def _adam_update(g, w, m, v):
    m2 = ADAM_B1 * m + (1.0 - ADAM_B1) * g
    v2 = ADAM_B2 * v + (1.0 - ADAM_B2) * (g * g)
    m_hat = m2 / (1.0 - ADAM_B1 ** ADAM_STEP)
    v_hat = v2 / (1.0 - ADAM_B2 ** ADAM_STEP)
    delta = -ADAM_LR * (m_hat / (jnp.sqrt(v_hat) + ADAM_EPS) + ADAM_WD * w)
    return delta, m2, v2


def _adamw_slab(g, w, m, v):
    rows = g.shape[0]
    tr = 256

    def body(g_ref, w_ref, m_ref, v_ref, d_ref, m2_ref, v2_ref):
        d_ref[...], m2_ref[...], v2_ref[...] = _adam_update(g_ref[...], w_ref[...], m_ref[...], v_ref[...])

    spec = pl.BlockSpec((tr, D), lambda i: (i, 0))
    return pl.pallas_call(
        body, name="adamw_slab", grid=(rows // tr,), in_specs=[spec] * 4, out_specs=[spec] * 3,
        out_shape=[jax.ShapeDtypeStruct((rows, D), F32)] * 3,
        compiler_params=_params(1),
    )(g, w, m, v)


def _allreduce_small_adamw(part, w, m, v):
    def body(p_ref, w_ref, m_ref, v_ref, g_ref, d_ref, m2_ref, v2_ref, slots, send_sems, recv_sems):
        x, y, c = _mesh_pos()
        mine = 4 * x + 2 * y + c
        peers = [(px, py, pc) for px in (x, 1 - x) for py in (y, 1 - y) for pc in (c, 1 - c)][1:]

        def remote(k, slot):
            return pltpu.make_async_remote_copy(
                src_ref=p_ref, dst_ref=slots.at[slot], send_sem=send_sems.at[k], recv_sem=recv_sems.at[k],
                device_id=peers[k], device_id_type=MESH)

        sends = [remote(k, mine) for k in range(7)]
        for cp in sends:
            cp.start()
        slots[mine] = p_ref[...]
        for k, (px, py, pc) in enumerate(peers):
            remote(k, 4 * px + 2 * py + pc).wait_recv()
        for cp in sends:
            cp.wait_send()
        g = slots[0]
        for d in range(1, 8):
            g = g + slots[d]
        g_ref[...] = g
        d_ref[...], m2_ref[...], v2_ref[...] = _adam_update(g, w_ref[...], m_ref[...], v_ref[...])

    vm = pl.BlockSpec(memory_space=pltpu.VMEM)
    shape = jax.ShapeDtypeStruct(part.shape, F32)
    return pl.pallas_call(
        body, name="small_allreduce_adamw", in_specs=[vm] * 4, out_specs=[vm] * 4, out_shape=[shape] * 4,
        scratch_shapes=[pltpu.VMEM((8,) + part.shape, F32), pltpu.SemaphoreType.DMA((7,)),
                        pltpu.SemaphoreType.DMA((7,))],
        compiler_params=pltpu.CompilerParams(has_side_effects=True),
    )(part, w, m, v)


def _half(ref, c, axis):
    n = ref.shape[axis] // 2
    return ref.at[(slice(None),) * axis + (pl.ds(c * n, n),)]


def _remote(src, dst, send_sem, recv_sem, device):
    return pltpu.make_async_remote_copy(src_ref=src, dst_ref=dst, send_sem=send_sem, recv_sem=recv_sem,
                                        device_id=device, device_id_type=MESH)


def _gather_weights(split, axes, whole):
    ns, n = len(split), len(split) + len(whole)

    def body(*refs):
        ins, outs = refs[:n], refs[n:2 * n]
        ici_send, ici_recv, d2d_send, d2d_recv, local_sems = refs[2 * n:]
        x, y, c = _mesh_pos()
        mine = 2 * x + y
        chips = _other_chips(x, y)
        local = [pltpu.make_async_copy(ins[a], outs[a].at[mine], local_sems.at[a]) for a in range(n)]
        for cp in local:
            cp.start()

        def ici(a, k, block):
            px, py = chips[k]
            src, dst = ins[a], outs[a].at[block]
            if a < ns:
                src, dst = _half(src, c, axes[a]), _half(dst, c, axes[a])
            return _remote(src, dst, ici_send.at[3 * a + k], ici_recv.at[3 * a + k], (px, py, c))

        def d2d(a, k, block, half):
            part = _half(outs[a].at[block], half, axes[a])
            return _remote(part, part, d2d_send.at[3 * a + k], d2d_recv.at[3 * a + k], (x, y, 1 - c))

        sends = [ici(a, k, mine) for a in range(n) for k in range(3)]
        for cp in sends:
            cp.start()
        for a in range(n):
            for k, (px, py) in enumerate(chips):
                ici(a, k, 2 * px + py).wait_recv()
                if a < ns:
                    sends.append(d2d(a, k, 2 * px + py, c))
                    sends[-1].start()
        for a in range(ns):
            for k, (px, py) in enumerate(chips):
                d2d(a, k, 2 * px + py, 1 - c).wait_recv()
        for cp in sends:
            cp.wait_send()
        for cp in local:
            cp.wait()

    arrays = list(split) + list(whole)
    return pl.pallas_call(
        body, name="gather_weights", in_specs=[HBM_SPEC] * n, out_specs=[HBM_SPEC] * n,
        out_shape=[jax.ShapeDtypeStruct((N_CHIPS,) + s.shape, s.dtype) for s in arrays],
        scratch_shapes=[pltpu.SemaphoreType.DMA((3 * n,)), pltpu.SemaphoreType.DMA((3 * n,)),
                        pltpu.SemaphoreType.DMA((3 * ns,)), pltpu.SemaphoreType.DMA((3 * ns,)),
                        pltpu.SemaphoreType.DMA((n,))],
        compiler_params=pltpu.CompilerParams(has_side_effects=True),
    )(*arrays)


def _rs_to_sibling(gs):
    n = len(gs)

    def body(*refs):
        ins, outs, send_sems, recv_sems = refs[:n], refs[n:2 * n], refs[2 * n], refs[2 * n + 1]
        x, y, c = _mesh_pos()
        copies = [_remote(_half(ins[a], 1 - c, 2), outs[a], send_sems.at[a], recv_sems.at[a], (x, y, 1 - c))
                  for a in range(n)]
        for cp in copies:
            cp.start()
        for cp in copies:
            cp.wait()

    return pl.pallas_call(
        body, name="rs_to_sibling", in_specs=[HBM_SPEC] * n, out_specs=[HBM_SPEC] * n,
        out_shape=[jax.ShapeDtypeStruct(g.shape[:2] + (g.shape[2] // 2,), g.dtype) for g in gs],
        scratch_shapes=[pltpu.SemaphoreType.DMA((n,)), pltpu.SemaphoreType.DMA((n,))],
        compiler_params=pltpu.CompilerParams(has_side_effects=True),
    )(*gs)


def _rs_add_halves(g, recv, c, name):
    _, rows, w = g.shape
    h = w // 2
    tr = rows // 2 if rows % 16 == 0 and rows > 64 else rows

    def body(c_ref, a_ref, b_ref, o_ref):
        o_ref[...] = (a_ref[...] + b_ref[...]).astype(BF16)

    return pl.pallas_call(
        body, name=name,
        grid_spec=pltpu.PrefetchScalarGridSpec(
            num_scalar_prefetch=1, grid=(N_CHIPS, rows // tr),
            in_specs=[pl.BlockSpec((1, tr, h), lambda j, i, s: (j, i, s[0])),
                      pl.BlockSpec((1, tr, h), lambda j, i, s: (j, i, 0))],
            out_specs=pl.BlockSpec((1, tr, h), lambda j, i, s: (j, i, 0))),
        out_shape=jax.ShapeDtypeStruct((N_CHIPS, rows, h), BF16),
        compiler_params=_params(2),
    )(jnp.reshape(c, (1,)).astype(jnp.int32), g, recv)


def _rs_chip_exchange(ps):
    n = len(ps)

    def body(*refs):
        ins, outs = refs[:n], refs[n:2 * n]
        send_sems, recv_sems, local_sems = refs[2 * n:]
        x, y, c = _mesh_pos()
        mine = 2 * x + y
        chips = _other_chips(x, y)
        local = [pltpu.make_async_copy(ins[a].at[mine], outs[a].at[mine], local_sems.at[a]) for a in range(n)]
        for cp in local:
            cp.start()

        def ici(a, k, src_block, dst_block):
            px, py = chips[k]
            return _remote(ins[a].at[src_block], outs[a].at[dst_block], send_sems.at[3 * a + k],
                           recv_sems.at[3 * a + k], (px, py, c))

        sends = [ici(a, k, 2 * px + py, mine) for a in range(n) for k, (px, py) in enumerate(chips)]
        for cp in sends:
            cp.start()
        for a in range(n):
            for k, (px, py) in enumerate(chips):
                ici(a, k, mine, 2 * px + py).wait_recv()
        for cp in sends:
            cp.wait_send()
        for cp in local:
            cp.wait()

    return pl.pallas_call(
        body, name="rs_chip_exchange", in_specs=[HBM_SPEC] * n, out_specs=[HBM_SPEC] * n,
        out_shape=[jax.ShapeDtypeStruct(p.shape, p.dtype) for p in ps],
        scratch_shapes=[pltpu.SemaphoreType.DMA((3 * n,)), pltpu.SemaphoreType.DMA((3 * n,)),
                        pltpu.SemaphoreType.DMA((n,))],
        compiler_params=pltpu.CompilerParams(has_side_effects=True),
    )(*ps)


def _rs_sum_chips(parts, name):
    _, rows, h = parts.shape
    tr = rows // 2 if rows % 16 == 0 and rows > 64 else rows

    def body(p_ref, o_ref):
        p = p_ref[...].astype(F32)
        o_ref[...] = ((p[0] + p[1]) + p[2]) + p[3]

    return pl.pallas_call(
        body, name=name, grid=(rows // tr,),
        in_specs=[pl.BlockSpec((N_CHIPS, tr, h), lambda i: (0, i, 0))],
        out_specs=pl.BlockSpec((tr, h), lambda i: (i, 0)),
        out_shape=jax.ShapeDtypeStruct((rows, h), F32),
        compiler_params=_params(1),
    )(parts)


def _rs_share(halves):
    n = len(halves)

    def body(*refs):
        ins, outs = refs[:n], refs[n:2 * n]
        send_sems, recv_sems, local_sems = refs[2 * n:]
        x, y, c = _mesh_pos()
        local = [pltpu.make_async_copy(ins[a], _half(outs[a], c, 1), local_sems.at[a]) for a in range(n)]
        for cp in local:
            cp.start()
        sends = [_remote(ins[a], _half(outs[a], c, 1), send_sems.at[a], recv_sems.at[a], (x, y, 1 - c))
                 for a in range(n)]
        for cp in sends:
            cp.start()
        for a in range(n):
            _remote(ins[a], _half(outs[a], 1 - c, 1), send_sems.at[a], recv_sems.at[a], (x, y, 1 - c)).wait_recv()
        for cp in sends:
            cp.wait_send()
        for cp in local:
            cp.wait()

    return pl.pallas_call(
        body, name="rs_share", in_specs=[HBM_SPEC] * n, out_specs=[HBM_SPEC] * n,
        out_shape=[jax.ShapeDtypeStruct((p.shape[0], 2 * p.shape[1]), p.dtype) for p in halves],
        scratch_shapes=[pltpu.SemaphoreType.DMA((n,)), pltpu.SemaphoreType.DMA((n,)),
                        pltpu.SemaphoreType.DMA((n,))],
        compiler_params=pltpu.CompilerParams(has_side_effects=True),
    )(*halves)


def _adamw(g, w, m, v, name):
    rows, cols = g.shape
    tr = 256 if rows % 256 == 0 else (rows // 2 if rows % 16 == 0 and rows > 64 else rows)

    def body(g_ref, w_ref, m_ref, v_ref, d_ref, m2_ref, v2_ref):
        d_ref[...], m2_ref[...], v2_ref[...] = _adam_update(g_ref[...], w_ref[...], m_ref[...], v_ref[...])

    spec = pl.BlockSpec((tr, cols), lambda i: (i, 0))
    return pl.pallas_call(
        body, name=name, grid=(rows // tr,), in_specs=[spec] * 4, out_specs=[spec] * 3,
        out_shape=[jax.ShapeDtypeStruct((rows, cols), F32)] * 3,
        compiler_params=_params(1),
    )(g, w, m, v)


def _rows_of(a):
    flat = a.reshape(-1)
    pad = (-flat.shape[0]) % D
    if pad:
        flat = jnp.concatenate([flat, jnp.zeros((pad,), flat.dtype)])
    return flat.reshape(-1, D)


SLAB_PARTS = (("w_in", (D, D_IN // N_CHIPS)), ("w_out", (D // N_CHIPS, D)), ("w_ffn_gate", (D, D_FF // N_CHIPS)),
              ("w_ffn_up", (D, D_FF // N_CHIPS)), ("w_ffn_down", (D_FF // N_CHIPS, D)),
              ("meta_tokens", (N_META, D // N_CHIPS)), ("conv_w", (CONV_W, C_CONV // N_CHIPS)),
              ("gla_w_gate2", (RANK, GLA_K // N_CHIPS)))


def _pack_slab(parts):
    rows = [_rows_of(parts[name].reshape(shape)) for name, shape in SLAB_PARTS]
    used = sum(r.shape[0] for r in rows)
    rows.append(jnp.zeros((SLAB_ROWS - used, D), F32))
    return jnp.concatenate(rows, axis=0)


def _unpack_slab(slab, lead):
    out, r0 = {}, 0
    for name, shape in SLAB_PARTS:
        size = shape[0] * shape[1]
        nrows = -(-size // D)
        out[name] = slab[r0:r0 + nrows].reshape(-1)[:size].reshape(lead[name] + shape)
        r0 += nrows
    return out


SMALL_PARTS = (("norm_mix_g", 0, 0, D), ("norm_ffn_g", 1, 0, D), ("norm_final_g", 2, 0, D),
               ("conv_b", 3, 0, C_CONV), ("conv_ln_g", 3, C_CONV, C_CONV), ("conv_ln_b", 4, 0, C_CONV),
               ("gla_gate_b", 4, C_CONV, GLA_K), ("gla_norm_g", 4, C_CONV + GLA_K, DV))


def _pack_small(parts):
    slab = jnp.zeros((SMALL_ROWS, D), F32)
    for name, row, col, size in SMALL_PARTS:
        slab = lax.dynamic_update_slice(slab, parts[name].reshape(1, size).astype(F32), (row, col))
    return slab


def _unpack_small(slab, shapes):
    return {name: slab[row, col:col + size].reshape(shapes[name]) for name, row, col, size in SMALL_PARTS}


def _column_block(full, j, width):
    return lax.dynamic_slice_in_dim(full, j * width, width, axis=1)


def _local_step(x, target, w):
    n_ex, seq, _ = x.shape
    lp = HEAD_ROWS + seq
    t = n_ex * lp
    meta = jnp.broadcast_to(w["meta_tokens"][None], (n_ex, N_META, D))
    h0 = jnp.concatenate([jnp.zeros((n_ex, PAD_ROWS, D), F32), meta, x], axis=1).reshape(t, D)
    tgt = jnp.concatenate([jnp.zeros((n_ex, HEAD_ROWS, D), F32), target], axis=1).reshape(t, D)
    row_mask = jnp.concatenate([jnp.zeros((n_ex, HEAD_ROWS, 1), F32), jnp.ones((n_ex, seq, 1), F32)],
                               axis=1).reshape(t, 1)

    u, hn = _in_proj(h0, w["norm_mix_g"], w["w_in"])
    yc, y_conv = _conv_fwd(u, w["conv_w"], w["conv_b"], w["conv_ln_g"], w["conv_ln_b"], n_ex, lp)
    y_gla, states = _gla_fwd(u, w["gla_w_gate2"], w["gla_gate_b"], w["gla_norm_g"], n_ex, lp)
    h1, hn2, gate, up, act = _mix_out_ffn_up(h0, y_conv, y_gla, w["w_out"], w["norm_ffn_g"],
                                             w["w_ffn_gate_t"], w["w_ffn_up_t"])
    dh2, loss, d_final_g = _ffn_down_loss(act, w["w_ffn_down"], h1, tgt, w["norm_final_g"], row_mask)

    dgate, dup, dh1, dycat, d_ffn_g = _ffn_bwd(dh2, gate, up, h1, w["w_ffn_down"], w["w_ffn_gate_t"],
                                                w["w_ffn_up_t"], w["w_out"], w["norm_ffn_g"])
    du_conv, d_conv_w, d_conv_b, d_ln_g, d_ln_b = _conv_bwd(dycat, yc, u, w["conv_w"], w["conv_ln_g"],
                                                            w["conv_ln_b"], n_ex, lp)
    du_gla, d_w2, d_gate_b, d_norm_g = _gla_bwd(dycat, u, states, w["gla_w_gate2"], w["gla_gate_b"],
                                                w["gla_norm_g"], n_ex, lp)
    dh0, d_mix_g = _in_proj_bwd(du_conv, du_gla, w["w_in"][:, :2 * C_CONV], w["w_in"][:, 2 * C_CONV:],
                                h0, dh1, w["norm_mix_g"])

    d_w_in_t = jnp.concatenate([_wgrad(du_conv, hn, "wgrad_in_conv"), _wgrad(du_gla, hn, "wgrad_in_gla")],
                               axis=0)[:D_IN]
    d_w_out = jnp.concatenate([_wgrad(y_conv, dh1, "wgrad_out_conv"), _wgrad(y_gla, dh1, "wgrad_out_gla")], axis=0)
    dh0 = dh0.reshape(n_ex, lp, D)
    grads = {
        "w_in_t": d_w_in_t, "w_out": d_w_out,
        "w_ffn_gate_t": _wgrad(dgate, hn2, "wgrad_gate"), "w_ffn_up_t": _wgrad(dup, hn2, "wgrad_up"),
        "w_ffn_down": _wgrad(act, dh2, "wgrad_down"),
        "meta_tokens": jnp.sum(dh0[:, PAD_ROWS:HEAD_ROWS], axis=0),
        "conv_w": d_conv_w, "gla_w_gate2": d_w2[:RANK],
        "norm_mix_g": d_mix_g, "norm_ffn_g": d_ffn_g, "norm_final_g": d_final_g,
        "conv_b": d_conv_b, "conv_ln_g": d_ln_g, "conv_ln_b": d_ln_b,
        "gla_gate_b": d_gate_b, "gla_norm_g": d_norm_g,
    }
    return loss[0, 0], dh0[:, HEAD_ROWS:], grads


WEIGHT_NAMES = ("meta_tokens", "norm_mix_g", "w_in", "conv_w", "conv_b", "conv_ln_g", "conv_ln_b", "gla_w_gate2",
                "gla_gate_b", "gla_norm_g", "w_out", "norm_ffn_g", "w_ffn_gate", "w_ffn_up", "w_ffn_down",
                "norm_final_g")
MATMUL_WEIGHTS = ("w_in", "w_out", "w_ffn_gate", "w_ffn_up", "w_ffn_down")
ROW_SHARDED = ("w_out", "w_ffn_down")


def _full_weights(ws):
    sh = lambda name: ws[name].reshape(ws[name].shape[-2:])
    split = [sh("w_in").astype(BF16), sh("w_out").astype(BF16), sh("w_ffn_gate").T.astype(BF16),
             sh("w_ffn_up").T.astype(BF16), sh("w_ffn_down").astype(BF16)]
    whole = [sh("meta_tokens"), sh("conv_w"), sh("gla_w_gate2")]
    w_in, w_out, gate_t, up_t, down, meta, conv_w, w2 = _gather_weights(split, [0, 0, 0, 0, 0], whole)
    cols = lambda a: jnp.concatenate([a[j] for j in range(N_CHIPS)], axis=1)
    full = {name: ws[name].reshape(1, -1) for name, _, _, _ in SMALL_PARTS}
    full["w_in"] = jnp.concatenate([cols(w_in), jnp.zeros((D, D_IN_PAD - D_IN), BF16)], axis=1)
    full["w_out"] = w_out.reshape(D, D)
    full["w_ffn_gate_t"] = gate_t.reshape(D_FF, D)
    full["w_ffn_up_t"] = up_t.reshape(D_FF, D)
    full["w_ffn_down"] = down.reshape(D_FF, D)
    full["meta_tokens"] = cols(meta)
    full["conv_w"] = jnp.concatenate([cols(conv_w), jnp.zeros((32 - CONV_W, C_CONV), F32)], axis=0)
    full["gla_w_gate2"] = jnp.concatenate([cols(w2), jnp.zeros((128 - RANK, GLA_K), F32)], axis=0).astype(BF16)
    return full


SMALL_RS_ROWS = 48


def _pack_small_sharded(grads):
    by_chip = lambda g, w: jnp.transpose(g.reshape(g.shape[0], N_CHIPS, w), (1, 0, 2))
    meta = by_chip(grads["meta_tokens"], D // N_CHIPS)
    conv = by_chip(grads["conv_w"], C_CONV // N_CHIPS).reshape(N_CHIPS, 16, 256)
    w2 = by_chip(grads["gla_w_gate2"], GLA_K // N_CHIPS).reshape(N_CHIPS, 4, 256)
    pad = jnp.zeros((N_CHIPS, SMALL_RS_ROWS - 36, 256), F32)
    return jnp.concatenate([meta, conv, w2, pad], axis=1)


def _unpack_small_sharded(g):
    return {"meta_tokens": g[0:16], "conv_w": g[16:32].reshape(32, C_CONV // N_CHIPS)[:CONV_W],
            "gla_w_gate2": g[32:36].reshape(RANK, GLA_K // N_CHIPS)}


def _kernel_without_overlap(x, meta_tokens, norm_mix_g, w_in, conv_w, conv_b, conv_ln_g, conv_ln_b, gla_w_gate2, gla_gate_b, gla_norm_g, w_out, norm_ffn_g, w_ffn_gate, w_ffn_up, w_ffn_down, norm_final_g, loss_target, m_meta_tokens, m_norm_mix_g, m_w_in, m_conv_w, m_conv_b, m_conv_ln_g, m_conv_ln_b, m_gla_w_gate2, m_gla_gate_b, m_gla_norm_g, m_w_out, m_norm_ffn_g, m_w_ffn_gate, m_w_ffn_up, m_w_ffn_down, m_norm_final_g, v_meta_tokens, v_norm_mix_g, v_w_in, v_conv_w, v_conv_b, v_conv_ln_g, v_conv_ln_b, v_gla_w_gate2, v_gla_gate_b, v_gla_norm_g, v_w_out, v_norm_ffn_g, v_w_ffn_gate, v_w_ffn_up, v_w_ffn_down, v_norm_final_g):
    ws = dict(zip(WEIGHT_NAMES, (meta_tokens, norm_mix_g, w_in, conv_w, conv_b, conv_ln_g, conv_ln_b, gla_w_gate2,
                                 gla_gate_b, gla_norm_g, w_out, norm_ffn_g, w_ffn_gate, w_ffn_up, w_ffn_down,
                                 norm_final_g)))
    ms = dict(zip(WEIGHT_NAMES, (m_meta_tokens, m_norm_mix_g, m_w_in, m_conv_w, m_conv_b, m_conv_ln_g, m_conv_ln_b,
                                 m_gla_w_gate2, m_gla_gate_b, m_gla_norm_g, m_w_out, m_norm_ffn_g, m_w_ffn_gate,
                                 m_w_ffn_up, m_w_ffn_down, m_norm_final_g)))
    vs = dict(zip(WEIGHT_NAMES, (v_meta_tokens, v_norm_mix_g, v_w_in, v_conv_w, v_conv_b, v_conv_ln_g, v_conv_ln_b,
                                 v_gla_w_gate2, v_gla_gate_b, v_gla_norm_g, v_w_out, v_norm_ffn_g, v_w_ffn_gate,
                                 v_w_ffn_up, v_w_ffn_down, v_norm_final_g)))
    c = lax.axis_index("c")

    full = _full_weights(ws)
    loss, grad_x, grads = _local_step(x, loss_target, full)
    loss = lax.psum(loss, ("x", "y", "c"))

    rs_names = ("w_in", "w_out", "w_ffn_gate", "w_ffn_up", "w_ffn_down", "small")
    by_owner = [grads["w_in_t"].reshape(N_CHIPS, D_IN // N_CHIPS, D), grads["w_out"].reshape(N_CHIPS, D // N_CHIPS, D),
                grads["w_ffn_gate_t"].reshape(N_CHIPS, D_FF // N_CHIPS, D),
                grads["w_ffn_up_t"].reshape(N_CHIPS, D_FF // N_CHIPS, D),
                grads["w_ffn_down"].reshape(N_CHIPS, D_FF // N_CHIPS, D), _pack_small_sharded(grads)]
    from_sibling = _rs_to_sibling(by_owner)
    chip_sums = [_rs_add_halves(g, r, c, "rs_add_" + nm) for g, r, nm in zip(by_owner, from_sibling, rs_names)]
    halves = [_rs_sum_chips(p, "rs_sum_" + nm) for p, nm in zip(_rs_chip_exchange(chip_sums), rs_names)]
    reduced = dict(zip(rs_names, _rs_share(halves)))
    g_sharded = {"w_in": reduced["w_in"].T, "w_out": reduced["w_out"], "w_ffn_gate": reduced["w_ffn_gate"].T,
                 "w_ffn_up": reduced["w_ffn_up"].T, "w_ffn_down": reduced["w_ffn_down"],
                 **_unpack_small_sharded(reduced["small"])}
    out = {"grad": {}, "delta": {}, "new_m": {}, "new_v": {}}
    for name, g in g_sharded.items():
        shape = ws[name].shape
        flat = lambda a: a.reshape(shape[-2:])
        delta, new_m, new_v = _adamw(g, flat(ws[name]), flat(ms[name]), flat(vs[name]), "adamw_" + name)
        for kind, a in (("grad", g), ("delta", delta), ("new_m", new_m), ("new_v", new_v)):
            out[kind][name] = a.reshape(shape)

    small_shapes = {name: ws[name].shape for name, _, _, _ in SMALL_PARTS}
    g_s, d_s, m_s, v_s = _allreduce_small_adamw(_pack_small(grads), _pack_small(ws), _pack_small(ms), _pack_small(vs))
    for kind, slab in (("grad", g_s), ("delta", d_s), ("new_m", m_s), ("new_v", v_s)):
        out[kind].update(_unpack_small(slab, small_shapes))

    return (loss, grad_x, *[out[kind][name] for kind in ("grad", "delta", "new_m", "new_v") for name in WEIGHT_NAMES])


def _gather_plan(split, whole=(), axes=None):
    split, whole = list(split), list(whole)
    ns, n = len(split), len(split) + len(whole)

    def make(ins, outs, sems):
        ici_send, ici_recv, d2d_send, d2d_recv, own_send, own_recv = sems
        x, y, c = _mesh_pos()
        mine = 2 * x + y
        chips = _other_chips(x, y)
        blocks = [2 * px + py for px, py in chips]

        def own(a):
            return _remote(ins[a], outs[a].at[mine], own_send.at[a], own_recv.at[a], (x, y, 1 - c))

        def ici(a, k, block):
            px, py = chips[k]
            src, dst = ins[a], outs[a].at[block]
            if a < ns:
                src, dst = _half(src, c, axes[a]), _half(dst, c, axes[a])
            return _remote(src, dst, ici_send.at[3 * a + k], ici_recv.at[3 * a + k], (px, py, c))

        def d2d(a, k, half):
            part = _half(outs[a].at[blocks[k]], half, axes[a])
            return _remote(part, part, d2d_send.at[3 * a + k], d2d_recv.at[3 * a + k], (x, y, 1 - c))

        def start():
            for a in range(n):
                for k in range(3):
                    ici(a, k, mine).start()
                own(a).start()

        def finish():
            for a in range(n):
                for k in range(3):
                    ici(a, k, blocks[k]).wait_recv()
                    if a < ns:
                        d2d(a, k, c).start()
            for a in range(ns):
                for k in range(3):
                    d2d(a, k, 1 - c).wait_recv()
            for a in range(n):
                for k in range(3):
                    ici(a, k, mine).wait_send()
                    if a < ns:
                        d2d(a, k, c).wait_send()
                own(a).wait()

        return start, finish

    arrays = split + whole
    axes = [0] * ns if axes is None else list(axes)
    return _Plan(arrays, [jax.ShapeDtypeStruct((N_CHIPS,) + s.shape, s.dtype) for s in arrays],
                 [pltpu.SemaphoreType.DMA((3 * n,)), pltpu.SemaphoreType.DMA((3 * n,)),
                  pltpu.SemaphoreType.DMA((3 * ns,)), pltpu.SemaphoreType.DMA((3 * ns,)),
                  pltpu.SemaphoreType.DMA((n,)), pltpu.SemaphoreType.DMA((n,))], make)


def _to_sibling_plan(gs):
    n = len(gs)

    def make(ins, outs, sems):
        send_sems, recv_sems = sems
        x, y, c = _mesh_pos()

        def copy(a):
            return _remote(_half(ins[a], 1 - c, 2), outs[a], send_sems.at[a], recv_sems.at[a], (x, y, 1 - c))

        def start():
            for a in range(n):
                copy(a).start()

        def finish():
            for a in range(n):
                copy(a).wait()

        return start, finish

    return _Plan(list(gs), [jax.ShapeDtypeStruct(g.shape[:2] + (g.shape[2] // 2,), g.dtype) for g in gs],
                 [pltpu.SemaphoreType.DMA((n,)), pltpu.SemaphoreType.DMA((n,))], make)


def _chip_exchange_plan(ps):
    n = len(ps)

    def make(ins, outs, sems):
        send_sems, recv_sems = sems
        x, y, c = _mesh_pos()
        chips = _other_chips(x, y)

        def ici(a, k):
            px, py = chips[k]
            return _remote(ins[a].at[2 * px + py], outs[a].at[k], send_sems.at[3 * a + k],
                           recv_sems.at[3 * a + k], (px, py, c))

        def start():
            for a in range(n):
                for k in range(3):
                    ici(a, k).start()

        def finish():
            for a in range(n):
                for k in range(3):
                    ici(a, k).wait()

        return start, finish

    return _Plan(list(ps), [jax.ShapeDtypeStruct((3,) + p.shape[1:], p.dtype) for p in ps],
                 [pltpu.SemaphoreType.DMA((3 * n,)), pltpu.SemaphoreType.DMA((3 * n,))], make)


def _share_plan(halves):
    n = len(halves)

    def make(ins, outs, sems):
        send_sems, recv_sems = sems
        x, y, c = _mesh_pos()

        def d2d(a):
            return _remote(ins[a], outs[a], send_sems.at[a], recv_sems.at[a], (x, y, 1 - c))

        def start():
            for a in range(n):
                d2d(a).start()

        def finish():
            for a in range(n):
                d2d(a).wait()

        return start, finish

    return _Plan(list(halves), [jax.ShapeDtypeStruct(p.shape, p.dtype) for p in halves],
                 [pltpu.SemaphoreType.DMA((n,)), pltpu.SemaphoreType.DMA((n,))], make)


def _rs_sum(own, others, mine, name):
    _, rows, h = own.shape
    tr = rows // 2 if rows % 16 == 0 and rows > 64 else rows

    def body(mine_ref, own_ref, oth_ref, o_ref):
        p = oth_ref[...].astype(F32)
        o_ref[...] = ((own_ref[0].astype(F32) + p[0]) + p[1]) + p[2]

    return pl.pallas_call(
        body, name=name,
        grid_spec=pltpu.PrefetchScalarGridSpec(
            num_scalar_prefetch=1, grid=(rows // tr,),
            in_specs=[pl.BlockSpec((1, tr, h), lambda i, s: (s[0], i, 0)),
                      pl.BlockSpec((3, tr, h), lambda i, s: (0, i, 0))],
            out_specs=pl.BlockSpec((tr, h), lambda i, s: (i, 0))),
        out_shape=jax.ShapeDtypeStruct((rows, h), F32),
        compiler_params=_params(1),
    )(jnp.reshape(mine, (1,)).astype(jnp.int32), own, others)


def _join(mine, theirs, c):
    return jnp.where(c == 0, jnp.concatenate([mine, theirs], axis=1), jnp.concatenate([theirs, mine], axis=1))


def _exchange(plan, name):
    n_in, n_out = len(plan.arrays), len(plan.out_shape)

    def body(*refs):
        start, finish = plan.make(refs[:n_in], refs[n_in:n_in + n_out], refs[n_in + n_out:])
        start()
        finish()

    return pl.pallas_call(
        body, name=name, in_specs=[HBM_SPEC] * n_in, out_specs=[HBM_SPEC] * n_out, out_shape=list(plan.out_shape),
        scratch_shapes=list(plan.sems), compiler_params=pltpu.CompilerParams(has_side_effects=True),
    )(*plan.arrays)


def _adamw_halves(mine, theirs, c, w, m, v, name):
    rows, h = mine.shape
    tr = rows // 2 if rows % 16 == 0 else rows

    def body(c_ref, a_ref, b_ref, w_ref, m_ref, v_ref, go_ref, d_ref, m2_ref, v2_ref):
        g = jnp.where(pl.program_id(1) == c_ref[0], a_ref[...], b_ref[...])
        go_ref[...] = g
        d_ref[...], m2_ref[...], v2_ref[...] = _adam_update(g, w_ref[...], m_ref[...], v_ref[...])

    half = pl.BlockSpec((tr, h), lambda i, j, s: (i, 0))
    spec = pl.BlockSpec((tr, h), lambda i, j, s: (i, j))
    return pl.pallas_call(
        body, name=name,
        grid_spec=pltpu.PrefetchScalarGridSpec(num_scalar_prefetch=1, grid=(rows // tr, 2),
                                               in_specs=[half, half, spec, spec, spec], out_specs=[spec] * 4),
        out_shape=[jax.ShapeDtypeStruct((rows, 2 * h), F32)] * 4,
        compiler_params=_params(2),
    )(jnp.reshape(c, (1,)).astype(jnp.int32), mine, theirs, w, m, v)


def _columns(gathered):
    return jnp.concatenate([gathered[j] for j in range(N_CHIPS)], axis=1)


def kernel(x, meta_tokens, norm_mix_g, w_in, conv_w, conv_b, conv_ln_g, conv_ln_b, gla_w_gate2, gla_gate_b, gla_norm_g, w_out, norm_ffn_g, w_ffn_gate, w_ffn_up, w_ffn_down, norm_final_g, loss_target, m_meta_tokens, m_norm_mix_g, m_w_in, m_conv_w, m_conv_b, m_conv_ln_g, m_conv_ln_b, m_gla_w_gate2, m_gla_gate_b, m_gla_norm_g, m_w_out, m_norm_ffn_g, m_w_ffn_gate, m_w_ffn_up, m_w_ffn_down, m_norm_final_g, v_meta_tokens, v_norm_mix_g, v_w_in, v_conv_w, v_conv_b, v_conv_ln_g, v_conv_ln_b, v_gla_w_gate2, v_gla_gate_b, v_gla_norm_g, v_w_out, v_norm_ffn_g, v_w_ffn_gate, v_w_ffn_up, v_w_ffn_down, v_norm_final_g):
    ws = dict(zip(WEIGHT_NAMES, (meta_tokens, norm_mix_g, w_in, conv_w, conv_b, conv_ln_g, conv_ln_b, gla_w_gate2,
                                 gla_gate_b, gla_norm_g, w_out, norm_ffn_g, w_ffn_gate, w_ffn_up, w_ffn_down,
                                 norm_final_g)))
    ms = dict(zip(WEIGHT_NAMES, (m_meta_tokens, m_norm_mix_g, m_w_in, m_conv_w, m_conv_b, m_conv_ln_g, m_conv_ln_b,
                                 m_gla_w_gate2, m_gla_gate_b, m_gla_norm_g, m_w_out, m_norm_ffn_g, m_w_ffn_gate,
                                 m_w_ffn_up, m_w_ffn_down, m_norm_final_g)))
    vs = dict(zip(WEIGHT_NAMES, (v_meta_tokens, v_norm_mix_g, v_w_in, v_conv_w, v_conv_b, v_conv_ln_g, v_conv_ln_b,
                                 v_gla_w_gate2, v_gla_gate_b, v_gla_norm_g, v_w_out, v_norm_ffn_g, v_w_ffn_gate,
                                 v_w_ffn_up, v_w_ffn_down, v_norm_final_g)))
    c = lax.axis_index("c")
    shard = lambda d, name: d[name].reshape(d[name].shape[-2:])
    vec = {name: ws[name].reshape(1, -1) for name, _, _, _ in SMALL_PARTS}
    n_ex, seq, _ = x.shape
    lp = HEAD_ROWS + seq
    t = n_ex * lp

    w_in_g, meta_g, conv_w_g, w2_g = _exchange(
        _gather_plan([shard(ws, "w_in").T.astype(BF16)],
                     [shard(ws, "meta_tokens"), shard(ws, "conv_w"), shard(ws, "gla_w_gate2")], axes=[1]),
        "gather_first")
    w_in_t = jnp.concatenate([w_in_g.reshape(D_IN, D), jnp.zeros((D_IN_PAD - D_IN, D), BF16)], axis=0)
    w_in_full = w_in_t.T
    conv_w_full = jnp.concatenate([_columns(conv_w_g), jnp.zeros((32 - CONV_W, C_CONV), F32)], axis=0)
    w2_full = jnp.concatenate([_columns(w2_g), jnp.zeros((128 - RANK, GLA_K), F32)], axis=0).astype(BF16)

    meta = jnp.broadcast_to(_columns(meta_g)[None], (n_ex, N_META, D))
    h0 = jnp.concatenate([jnp.zeros((n_ex, PAD_ROWS, D), F32), meta, x], axis=1).reshape(t, D)
    tgt = jnp.concatenate([jnp.zeros((n_ex, HEAD_ROWS, D), F32), loss_target], axis=1).reshape(t, D)
    row_mask = jnp.concatenate([jnp.zeros((n_ex, HEAD_ROWS, 1), F32), jnp.ones((n_ex, seq, 1), F32)],
                               axis=1).reshape(t, 1)

    (u, hn), (w_out_g,) = _in_proj(h0, vec["norm_mix_g"], w_in_full,
                                   plan=_gather_plan([shard(ws, "w_out").astype(BF16)]))
    (yc, y_conv), (gate_g,) = _conv_fwd(
        u, conv_w_full, vec["conv_b"], vec["conv_ln_g"], vec["conv_ln_b"], n_ex, lp,
        plan=_gather_plan([shard(ws, "w_ffn_gate").T.astype(BF16)]))
    (y_gla, states), (up_g, down_g) = _gla_fwd(
        u, w2_full, vec["gla_gate_b"], vec["gla_norm_g"], n_ex, lp,
        plan=_gather_plan([shard(ws, "w_ffn_up").T.astype(BF16), shard(ws, "w_ffn_down").astype(BF16)]))
    w_out_full, w_down_full = w_out_g.reshape(D, D), down_g.reshape(D_FF, D)
    w_gate_t, w_up_t = gate_g.reshape(D_FF, D), up_g.reshape(D_FF, D)

    h1, hn2, gate, up, act = _mix_out_ffn_up(h0, y_conv, y_gla, w_out_full, vec["norm_ffn_g"], w_gate_t.T, w_up_t.T)
    dh2, loss, d_final_g = _ffn_down_loss(act, w_down_full, h1, tgt, vec["norm_final_g"], row_mask)
    loss = lax.psum(loss[0, 0], ("x", "y", "c"))
    dgate, dup, dh1, dycat, d_ffn_g = _ffn_bwd(dh2, gate, up, h1, w_down_full.T, w_gate_t, w_up_t, w_out_full.T,
                                                vec["norm_ffn_g"])

    early = ("w_out", "w_ffn_gate", "w_ffn_up", "w_ffn_down")
    d_w_out = jnp.concatenate([_wgrad(y_conv, dh1, "wgrad_out_conv"), _wgrad(y_gla, dh1, "wgrad_out_gla")], axis=0)
    by_owner = [d_w_out.reshape(N_CHIPS, D // N_CHIPS, D),
                _wgrad(dgate, hn2, "wgrad_gate").reshape(N_CHIPS, D_FF // N_CHIPS, D),
                _wgrad(dup, hn2, "wgrad_up").reshape(N_CHIPS, D_FF // N_CHIPS, D),
                _wgrad(act, dh2, "wgrad_down").reshape(N_CHIPS, D_FF // N_CHIPS, D)]
    (du_conv, d_conv_w, d_conv_b, d_ln_g, d_ln_b), from_sibling = _conv_bwd(
        dycat, yc, u, conv_w_full, vec["conv_ln_g"], vec["conv_ln_b"], n_ex, lp, plan=_to_sibling_plan(by_owner))
    chip_sums = [_rs_add_halves(g, r, c, "rs_add_" + nm) for g, r, nm in zip(by_owner, from_sibling, early)]
    (du_gla, d_w2, d_gate_b, d_norm_g), exchanged = _gla_bwd(
        dycat, u, states, w2_full, vec["gla_gate_b"], vec["gla_norm_g"], n_ex, lp,
        plan=_chip_exchange_plan(chip_sums))
    mine = 2 * lax.axis_index("x") + lax.axis_index("y")
    halves = [_rs_sum(own, oth, mine, "rs_sum_" + nm) for own, oth, nm in zip(chip_sums, exchanged, early)]
    (dh0, d_mix_g), shared = _in_proj_bwd(du_conv, du_gla, w_in_t[:2 * C_CONV], w_in_t[2 * C_CONV:],
                                          h0, dh1, vec["norm_mix_g"], plan=_share_plan(halves))
    dh0 = dh0.reshape(n_ex, lp, D)
    grad_x = dh0[:, HEAD_ROWS:]

    out = {"grad": {}, "delta": {}, "new_m": {}, "new_v": {}}

    def update(name, g=None, halves=None, transposed=False):
        shape = ws[name].shape
        lay = (lambda a: a.T) if transposed else (lambda a: a)
        w2d, m2d, v2d = lay(shard(ws, name)), lay(shard(ms, name)), lay(shard(vs, name))
        if halves is not None:
            res = _adamw_halves(*halves, c, w2d, m2d, v2d, "adamw_" + name)
        else:
            res = [g, *_adamw(g, w2d, m2d, v2d, "adamw_" + name)]
        for kind, a in zip(("grad", "delta", "new_m", "new_v"), res):
            out[kind][name] = lay(a).reshape(shape)

    update("w_out", halves=(halves[0], shared[0]))
    update("w_ffn_gate", halves=(halves[1], shared[1]), transposed=True)
    update("w_ffn_up", halves=(halves[2], shared[2]), transposed=True)
    update("w_ffn_down", halves=(halves[3], shared[3]))

    small = {"norm_mix_g": d_mix_g, "norm_ffn_g": d_ffn_g, "norm_final_g": d_final_g, "conv_b": d_conv_b,
             "conv_ln_g": d_ln_g, "conv_ln_b": d_ln_b, "gla_gate_b": d_gate_b, "gla_norm_g": d_norm_g}
    small_shapes = {name: ws[name].shape for name, _, _, _ in SMALL_PARTS}
    g_s, d_s, m_s, v_s = _allreduce_small_adamw(_pack_small(small), _pack_small(ws), _pack_small(ms), _pack_small(vs))
    for kind, slab in (("grad", g_s), ("delta", d_s), ("new_m", m_s), ("new_v", v_s)):
        out[kind].update(_unpack_small(slab, small_shapes))

    d_w_in_t = jnp.concatenate([_wgrad(du_conv, hn, "wgrad_in_conv"), _wgrad(du_gla, hn, "wgrad_in_gla")],
                               axis=0)[:D_IN]
    small_sharded = {"meta_tokens": jnp.sum(dh0[:, PAD_ROWS:HEAD_ROWS], axis=0), "conv_w": d_conv_w,
                     "gla_w_gate2": d_w2[:RANK]}
    late = ("w_in", "small")
    by_owner = [d_w_in_t.reshape(N_CHIPS, D_IN // N_CHIPS, D), _pack_small_sharded(small_sharded)]
    from_sibling = _exchange(_to_sibling_plan(by_owner), "rs_late_to_sibling")
    chip_sums = [_rs_add_halves(g, r, c, "rs_add_" + nm) for g, r, nm in zip(by_owner, from_sibling, late)]
    exchanged = _exchange(_chip_exchange_plan(chip_sums), "rs_late_chip_exchange")
    halves = [_rs_sum(own, oth, mine, "rs_sum_" + nm) for own, oth, nm in zip(chip_sums, exchanged, late)]
    shared = _exchange(_share_plan(halves), "rs_late_share")
    update("w_in", halves=(halves[0], shared[0]), transposed=True)
    for name, g in _unpack_small_sharded(_join(halves[1], shared[1], c)).items():
        update(name, g=g)

    return (loss, grad_x, *[out[kind][name] for kind in ("grad", "delta", "new_m", "new_v") for name in WEIGHT_NAMES])
```

```python
import functools
from typing import Any, Callable, NamedTuple, Sequence

import jax
import jax.numpy as jnp
from jax import lax
from jax.experimental import pallas as pl
from jax.experimental.pallas import tpu as pltpu

F32 = jnp.float32
BF16 = jnp.bfloat16
MESH = pl.DeviceIdType.MESH

D = 1024
N_META = 16
C_CONV = 512
CONV_W = 31
GLA_K = 256
GLA_V = 512
N_HEADS = 4
DK = 64
DV = 128
RANK = 16
CHUNK = 64
PAD_ROWS = CHUNK - N_META
HEAD_ROWS = CHUNK
D_IN = 2576
D_IN_PAD = 2688
D_GLA_IN = D_IN_PAD - 2 * C_CONV
D_FF = 2816
RMS_EPS = 1e-6
LN_EPS = 1e-5
GATE_TAU = 16.0
N_CHIPS = 4

ADAM_LR = 0.001
ADAM_B1 = 0.9
ADAM_B2 = 0.999
ADAM_EPS = 1e-08
ADAM_WD = 0.01
ADAM_STEP = 10

V7X_VMEM_BYTES = 64 * 1024 * 1024
VMEM_LIMIT = V7X_VMEM_BYTES - 8 * 1024 * 1024

SLAB_ROWS = 3072
HALF_ROWS = SLAB_ROWS // 2
SMALL_ROWS = 8


def _dot(a, b):
    return jnp.dot(a, b, preferred_element_type=F32)


def _dot_nt(a, b):
    return lax.dot_general(a, b, (((1,), (1,)), ((), ())), preferred_element_type=F32)


def _dot_tn(a, b):
    return lax.dot_general(a, b, (((0,), (0,)), ((), ())), preferred_element_type=F32)


def _sigmoid(x):
    return 1.0 / (1.0 + jnp.exp(-x))


def _const_spec(shape):
    return pl.BlockSpec(shape, lambda *_: (0,) * len(shape), pipeline_mode=pl.Buffered(1))


def _acc_spec(shape):
    return pl.BlockSpec(shape, lambda *_: (0,) * len(shape))


def _params(n_axes):
    return pltpu.CompilerParams(dimension_semantics=("arbitrary",) * n_axes, vmem_limit_bytes=VMEM_LIMIT)


def _row_tile(t, want):
    for r in (want, 384, 192, 128, 64):
        if r <= want and t % r == 0:
            return r
    raise ValueError(f"no row tile for {t}")


class _Plan(NamedTuple):
    arrays: Sequence[Any]
    out_shape: Sequence[Any]
    sems: Sequence[Any]
    make: Callable


def _call(body, *, name, grid, in_specs, out_specs, out_shape, scratch_shapes=(), plan=None):
    n_in, n_out, n_scr = len(in_specs), len(out_specs), len(scratch_shapes)
    if plan is None:
        plan = _Plan([], [], [], lambda ins, outs, sems: (lambda: None, lambda: None))
    nx_in, nx_out = len(plan.arrays), len(plan.out_shape)

    def hosted(*refs):
        ins, xins = refs[:n_in], refs[n_in:n_in + nx_in]
        o0 = n_in + nx_in
        outs, xouts = refs[o0:o0 + n_out], refs[o0 + n_out:o0 + n_out + nx_out]
        s0 = o0 + n_out + nx_out
        scr, sems = refs[s0:s0 + n_scr], refs[s0 + n_scr:]
        ids = [pl.program_id(a) for a in range(len(grid))]
        first = functools.reduce(jnp.logical_and, [i == 0 for i in ids])
        last = functools.reduce(jnp.logical_and, [i == g - 1 for i, g in zip(ids, grid)])
        start, finish = plan.make(xins, xouts, sems)
        pl.when(first)(start)
        body(*ins, *outs, *scr)
        pl.when(last)(finish)

    call = pl.pallas_call(
        hosted, name=name, grid=grid, in_specs=list(in_specs) + [HBM_SPEC] * nx_in,
        out_specs=list(out_specs) + [HBM_SPEC] * nx_out, out_shape=list(out_shape) + list(plan.out_shape),
        scratch_shapes=list(scratch_shapes) + list(plan.sems),
        compiler_params=pltpu.CompilerParams(dimension_semantics=("arbitrary",) * len(grid),
                                             vmem_limit_bytes=VMEM_LIMIT, has_side_effects=nx_in > 0))

    def run(*args):
        res = call(*args, *plan.arrays)
        return res[:n_out], res[n_out:]

    return run


def _in_proj(h0, g_mix, w_in, plan=None):
    t = h0.shape[0]
    r = _row_tile(t, 384)

    def body(h_ref, g_ref, w_ref, u_ref, hn_ref):
        h = h_ref[...]
        rstd = lax.rsqrt(jnp.mean(h * h, axis=-1, keepdims=True) + RMS_EPS)
        hn = (h * rstd * g_ref[...]).astype(BF16)
        hn_ref[...] = hn
        u_ref[...] = _dot(hn, w_ref[...])

    return _call(
        body, name="in_proj", grid=(t // r,),
        in_specs=[pl.BlockSpec((r, D), lambda i: (i, 0)), _const_spec((1, D)), _const_spec((D, D_IN_PAD))],
        out_specs=[pl.BlockSpec((r, D_IN_PAD), lambda i: (i, 0)), pl.BlockSpec((r, D), lambda i: (i, 0))],
        out_shape=[jax.ShapeDtypeStruct((t, D_IN_PAD), F32), jax.ShapeDtypeStruct((t, D), BF16)],
        plan=plan,
    )(h0, g_mix, w_in)


CONV_TILE = 192
CONV_SUB = 32
CONV_LEAD = CONV_SUB - (CONV_W - 1)
SUBLANES = 8


def _shifted_copies(src, dst, r):
    for s in range(1, SUBLANES):
        dst[s - 1] = src[s:s + r + CONV_SUB - SUBLANES, :]


def _shifted_rows(src, shifted, start):
    base, s = SUBLANES * (start // SUBLANES), start % SUBLANES
    if s == 0:
        return src[base:base + CONV_SUB, :]
    return shifted[s - 1, base:base + CONV_SUB, :]


def _conv_fwd(u, conv_w, conv_b, ln_g, ln_b, n_ex, lp, plan=None):
    r = CONV_TILE
    nt = lp // r
    hb = r // CONV_SUB

    def body(cur_ref, prev_ref, w_ref, b_ref, lg_ref, lb_ref, yc_ref, y_ref, glu, glu_sh):
        i = pl.program_id(1)
        cur = cur_ref[...]
        glu[CONV_SUB:CONV_SUB + r, :] = cur[:, :C_CONV] * _sigmoid(cur[:, C_CONV:])
        pv = prev_ref[...]
        halo = pv[:, :C_CONV] * _sigmoid(pv[:, C_CONV:])
        glu[0:CONV_SUB, :] = jnp.where(i > 0, halo, 0.0)
        _shifted_copies(glu, glu_sh, r)
        w = w_ref[...]
        for j in range(r // CONV_SUB):
            r0 = j * CONV_SUB
            acc = jnp.zeros((CONV_SUB, C_CONV), F32) + b_ref[...]
            for k in range(CONV_W):
                acc = acc + w[k:k + 1, :] * _shifted_rows(glu, glu_sh, r0 + CONV_LEAD + k)
            mu = jnp.mean(acc, axis=-1, keepdims=True)
            cen = acc - mu
            var = jnp.mean(cen * cen, axis=-1, keepdims=True)
            out = cen * lax.rsqrt(var + LN_EPS) * lg_ref[...] + lb_ref[...]
            y = out * _sigmoid(out)
            row = i * r + r0 + lax.broadcasted_iota(jnp.int32, (CONV_SUB, 1), 0)
            y = jnp.where(row >= PAD_ROWS, y, 0.0)
            yc_ref[r0:r0 + CONV_SUB, :] = acc
            y_ref[r0:r0 + CONV_SUB, :] = y.astype(BF16)

    t = n_ex * lp
    return _call(
        body, name="conv_fwd", grid=(n_ex, nt),
        in_specs=[pl.BlockSpec((r, 2 * C_CONV), lambda b, i: (b * nt + i, 0)),
                  pl.BlockSpec((CONV_SUB, 2 * C_CONV), lambda b, i: (jnp.maximum((b * nt + i) * hb - 1, 0), 0)),
                  _const_spec((32, C_CONV)), _const_spec((1, C_CONV)), _const_spec((1, C_CONV)), _const_spec((1, C_CONV))],
        out_specs=[pl.BlockSpec((r, C_CONV), lambda b, i: (b * nt + i, 0)),
                   pl.BlockSpec((r, C_CONV), lambda b, i: (b * nt + i, 0))],
        out_shape=[jax.ShapeDtypeStruct((t, C_CONV), F32), jax.ShapeDtypeStruct((t, C_CONV), BF16)],
        scratch_shapes=[pltpu.VMEM((r + CONV_SUB, C_CONV), F32),
                        pltpu.VMEM((SUBLANES - 1, r + CONV_SUB - SUBLANES, C_CONV), F32)],
        plan=plan,
    )(u, u, conv_w, conv_b, ln_g, ln_b)


def _gla_gates(lr, w2, gb, first_chunk):
    z = _dot(lr.astype(BF16), w2) + gb
    a = (jnp.minimum(z, 0.0) - jnp.log(1.0 + jnp.exp(-jnp.abs(z)))) * (1.0 / GATE_TAU)
    row = lax.broadcasted_iota(jnp.int32, (CHUNK, 1), 0)
    live = jnp.logical_or(jnp.logical_not(first_chunk), row >= PAD_ROWS)
    return z, jnp.where(live, a, 0.0), live


def _tri(lower):
    i = lax.broadcasted_iota(jnp.int32, (CHUNK, CHUNK), 0)
    j = lax.broadcasted_iota(jnp.int32, (CHUNK, CHUNK), 1)
    return (i >= j) if lower else (i <= j)


def _gla_fwd(u, w2, gb, ng, n_ex, lp, plan=None):
    nc = lp // CHUNK
    t = n_ex * lp

    def body(qk_ref, v_ref, g_ref, lr_ref, w2_ref, gb_ref, ng_ref, y_ref, st_ref, state):
        n = pl.program_id(0)

        @pl.when(n == 0)
        def _():
            state[...] = jnp.zeros_like(state)

        causal = _tri(True)
        for e in range(n_ex):
            st = state[e]
            st_ref[e] = st
            qk = qk_ref[e]
            q, k = qk[:, :GLA_K], qk[:, GLA_K:]
            _, a, _ = _gla_gates(lr_ref[e], w2_ref[...], gb_ref[...], n == 0)
            b = jnp.dot(causal.astype(F32), a, preferred_element_type=F32, precision=lax.Precision.HIGHEST)
            bl = b[CHUNK - 1:CHUNK, :]
            q_in = (q * (DK ** -0.5) * jnp.exp(b)).astype(BF16)
            k_in = (k * jnp.exp(-b)).astype(BF16)
            k_dec = (k * jnp.exp(bl - b)).astype(BF16)
            decay = jnp.exp(bl)
            v = v_ref[e]
            g = g_ref[e]
            st_b = st.astype(BF16)
            ys, new = [], []
            for h in range(N_HEADS):
                ks = slice(h * DK, (h + 1) * DK)
                vs = slice(h * DV, (h + 1) * DV)
                vh = v[:, vs].astype(BF16)
                s = jnp.where(causal, _dot_nt(q_in[:, ks], k_in[:, ks]), 0.0)
                o = _dot(s.astype(BF16), vh) + _dot_nt(q_in[:, ks], st_b[:, ks])
                new.append(decay[:, ks] * st[:, ks] + _dot_tn(vh, k_dec[:, ks]))
                rstd = lax.rsqrt(jnp.mean(o * o, axis=-1, keepdims=True) + RMS_EPS)
                gh = g[:, vs]
                ys.append(o * rstd * ng_ref[...] * (gh * _sigmoid(gh)))
            state[e] = jnp.concatenate(new, axis=1)
            y_ref[e] = jnp.concatenate(ys, axis=1).astype(BF16)

    u3 = u.reshape(n_ex, lp, D_IN_PAD)
    blk = lambda w, col: pl.BlockSpec((n_ex, CHUNK, w), lambda n: (0, n, col))
    (y, states), extra = _call(
        body, name="gla_fwd", grid=(nc,),
        in_specs=[blk(2 * GLA_K, 2), blk(GLA_V, 3), blk(GLA_V, 4), blk(128, 20),
                  _const_spec((128, GLA_K)), _const_spec((1, GLA_K)), _const_spec((1, DV))],
        out_specs=[blk(GLA_V, 0), pl.BlockSpec((n_ex, DV, GLA_K), lambda n: (0, n, 0))],
        out_shape=[jax.ShapeDtypeStruct((n_ex, lp, GLA_V), BF16),
                   jax.ShapeDtypeStruct((n_ex, nc * DV, GLA_K), F32)],
        scratch_shapes=[pltpu.VMEM((n_ex, DV, GLA_K), F32)],
        plan=plan,
    )(u3, u3, u3, u3, w2, gb, ng)
    return (y.reshape(t, GLA_V), states), extra


FFN_TILE = 192


def _mix_out_ffn_up(h0, y_conv, y_gla, w_out, g_ffn, w_gate, w_up):
    t = h0.shape[0]
    r = _row_tile(t, FFN_TILE)

    def body(h0_ref, yc_ref, yg_ref, wo_ref, g_ref, wg_ref, wu_ref, h1_ref, hn_ref, gate_ref, up_ref, act_ref):
        h1 = h0_ref[...] + _dot(yc_ref[...], wo_ref[0:C_CONV, :]) + _dot(yg_ref[...], wo_ref[C_CONV:D, :])
        h1_ref[...] = h1
        rstd = lax.rsqrt(jnp.mean(h1 * h1, axis=-1, keepdims=True) + RMS_EPS)
        hn = (h1 * rstd * g_ref[...]).astype(BF16)
        hn_ref[...] = hn
        gate = _dot(hn, wg_ref[...])
        up = _dot(hn, wu_ref[...])
        gate_ref[...] = gate
        up_ref[...] = up
        act_ref[...] = (gate * _sigmoid(gate) * up).astype(BF16)

    rows = lambda w: pl.BlockSpec((r, w), lambda i: (i, 0))
    return pl.pallas_call(
        body, name="mix_out_ffn_up", grid=(t // r,),
        in_specs=[rows(D), rows(C_CONV), rows(GLA_V), _const_spec((D, D)), _const_spec((1, D)),
                  _const_spec((D, D_FF)), _const_spec((D, D_FF))],
        out_specs=[rows(D), rows(D), rows(D_FF), rows(D_FF), rows(D_FF)],
        out_shape=[jax.ShapeDtypeStruct((t, D), F32), jax.ShapeDtypeStruct((t, D), BF16),
                   jax.ShapeDtypeStruct((t, D_FF), F32), jax.ShapeDtypeStruct((t, D_FF), F32),
                   jax.ShapeDtypeStruct((t, D_FF), BF16)],
        compiler_params=_params(1),
    )(h0, y_conv, y_gla, w_out, g_ffn, w_gate, w_up)


def _ffn_down_loss(act, w_down, h1, target, g_final, row_mask):
    t = h1.shape[0]
    r = _row_tile(t, 384)

    def body(act_ref, wd_ref, h1_ref, tgt_ref, gf_ref, mask_ref, dh2_ref, loss_ref, dgf_ref):
        @pl.when(pl.program_id(0) == 0)
        def _():
            loss_ref[...] = jnp.zeros_like(loss_ref)
            dgf_ref[...] = jnp.zeros_like(dgf_ref)

        h2 = h1_ref[...] + _dot(act_ref[...], wd_ref[...])
        rstd = lax.rsqrt(jnp.mean(h2 * h2, axis=-1, keepdims=True) + RMS_EPS)
        nrm = h2 * rstd
        gf = gf_ref[...]
        err = (nrm * gf - tgt_ref[...]) * mask_ref[...]
        loss_ref[...] += jnp.sum(err * err) * (0.5 / D)
        dy = err * (1.0 / D)
        dgf_ref[...] += jnp.sum(dy * nrm, axis=0, keepdims=True)
        dn = dy * gf
        dh2_ref[...] = rstd * (dn - nrm * jnp.mean(dn * nrm, axis=-1, keepdims=True))

    rows = lambda w: pl.BlockSpec((r, w), lambda i: (i, 0))
    return pl.pallas_call(
        body, name="ffn_down_loss", grid=(t // r,),
        in_specs=[rows(D_FF), _const_spec((D_FF, D)), rows(D), rows(D), _const_spec((1, D)), rows(1)],
        out_specs=[rows(D), _acc_spec((1, 128)), _acc_spec((1, D))],
        out_shape=[jax.ShapeDtypeStruct((t, D), F32), jax.ShapeDtypeStruct((1, 128), F32),
                   jax.ShapeDtypeStruct((1, D), F32)],
        compiler_params=_params(1),
    )(act, w_down, h1, target, g_final, row_mask)


def _ffn_bwd(dh2, gate, up, h1, w_down_t, w_gate_t, w_up_t, w_out_t, g_ffn):
    t = h1.shape[0]
    r = _row_tile(t, FFN_TILE)

    def body(dh2_ref, gate_ref, up_ref, h1_ref, wd_ref, wg_ref, wu_ref, wo_ref, g_ref,
             dgate_ref, dup_ref, dh1_ref, dycat_ref, dg_ref):
        @pl.when(pl.program_id(0) == 0)
        def _():
            dg_ref[...] = jnp.zeros_like(dg_ref)

        dh2 = dh2_ref[...]
        dact = _dot(dh2.astype(BF16), wd_ref[...])
        gate = gate_ref[...]
        sg = _sigmoid(gate)
        dgate = (dact * up_ref[...] * (sg * (1.0 + gate * (1.0 - sg)))).astype(BF16)
        dup = (dact * (gate * sg)).astype(BF16)
        dgate_ref[...] = dgate
        dup_ref[...] = dup
        dhn = _dot(dgate, wg_ref[...]) + _dot(dup, wu_ref[...])
        h1 = h1_ref[...]
        rstd = lax.rsqrt(jnp.mean(h1 * h1, axis=-1, keepdims=True) + RMS_EPS)
        nrm = h1 * rstd
        dg_ref[...] += jnp.sum(dhn * nrm, axis=0, keepdims=True)
        dn = dhn * g_ref[...]
        dh1 = dh2 + rstd * (dn - nrm * jnp.mean(dn * nrm, axis=-1, keepdims=True))
        dh1_ref[...] = dh1
        dycat_ref[...] = _dot(dh1.astype(BF16), wo_ref[...])

    rows = lambda w: pl.BlockSpec((r, w), lambda i: (i, 0))
    return pl.pallas_call(
        body, name="ffn_bwd", grid=(t // r,),
        in_specs=[rows(D), rows(D_FF), rows(D_FF), rows(D), _const_spec((D, D_FF)), _const_spec((D_FF, D)),
                  _const_spec((D_FF, D)), _const_spec((D, D)), _const_spec((1, D))],
        out_specs=[rows(D_FF), rows(D_FF), rows(D), rows(D), _acc_spec((1, D))],
        out_shape=[jax.ShapeDtypeStruct((t, D_FF), BF16), jax.ShapeDtypeStruct((t, D_FF), BF16),
                   jax.ShapeDtypeStruct((t, D), F32), jax.ShapeDtypeStruct((t, D), F32),
                   jax.ShapeDtypeStruct((1, D), F32)],
        compiler_params=_params(1),
    )(dh2, gate, up, h1, w_down_t, w_gate_t, w_up_t, w_out_t, g_ffn)


def _conv_bwd(dycat, yc, u, conv_w, ln_g, ln_b, n_ex, lp, plan=None):
    r = CONV_TILE
    nt = lp // r
    hb = r // CONV_SUB
    nsub = r // CONV_SUB

    def ln_bwd(dy, yc_rows, live, lg, lb):
        mu = jnp.mean(yc_rows, axis=-1, keepdims=True)
        cen = yc_rows - mu
        rs = lax.rsqrt(jnp.mean(cen * cen, axis=-1, keepdims=True) + LN_EPS)
        yn = cen * rs
        out = yn * lg + lb
        so = _sigmoid(out)
        dout = jnp.where(live, dy * (so * (1.0 + out * (1.0 - so))), 0.0)
        dyn = dout * lg
        dyc = rs * (dyn - jnp.mean(dyn, axis=-1, keepdims=True) - yn * jnp.mean(dyn * yn, axis=-1, keepdims=True))
        return dyc, dout, yn

    def body(dy_ref, dyn_ref, yc_ref, ycn_ref, cur_ref, prev_ref, w_ref, lg_ref, lb_ref,
             du_ref, dw_ref, db_ref, dlg_ref, dlb_ref, glu, dycs, dwacc, glu_sh, dycs_sh):
        b = pl.program_id(0)
        i = pl.program_id(1)
        first = jnp.logical_and(b == 0, i == 0)

        @pl.when(first)
        def _():
            dwacc[...] = jnp.zeros_like(dwacc)
            db_ref[...] = jnp.zeros_like(db_ref)
            dlg_ref[...] = jnp.zeros_like(dlg_ref)
            dlb_ref[...] = jnp.zeros_like(dlb_ref)

        lg, lb = lg_ref[...], lb_ref[...]
        cur = cur_ref[...]
        sig = _sigmoid(cur[:, C_CONV:])
        glu[CONV_SUB:CONV_SUB + r, :] = cur[:, :C_CONV] * sig
        pv = prev_ref[...]
        glu[0:CONV_SUB, :] = jnp.where(i > 0, pv[:, :C_CONV] * _sigmoid(pv[:, C_CONV:]), 0.0)

        row = i * r + lax.broadcasted_iota(jnp.int32, (r, 1), 0)
        dyc, dout, yn = ln_bwd(dy_ref[...], yc_ref[...], row >= PAD_ROWS, lg, lb)
        dycs[0:r, :] = dyc
        dycn, _, _ = ln_bwd(dyn_ref[...], ycn_ref[...], i < nt - 1, lg, lb)
        dycs[r:r + CONV_SUB, :] = dycn
        db_ref[...] += jnp.sum(dyc, axis=0, keepdims=True)
        dlg_ref[...] += jnp.sum(dout * yn, axis=0, keepdims=True)
        dlb_ref[...] += jnp.sum(dout, axis=0, keepdims=True)

        _shifted_copies(glu, glu_sh, r)
        _shifted_copies(dycs, dycs_sh, r)
        w = w_ref[...]
        for j in range(nsub):
            r0 = j * CONV_SUB
            dblk = dycs[r0:r0 + CONV_SUB, :]
            dglu = jnp.zeros((CONV_SUB, C_CONV), F32)
            for k in range(CONV_W):
                dglu = dglu + w[k:k + 1, :] * _shifted_rows(dycs, dycs_sh, r0 + (CONV_W - 1) - k)
                prod = dblk * _shifted_rows(glu, glu_sh, r0 + CONV_LEAD + k)
                dwacc[k] += prod.reshape(CONV_SUB // SUBLANES, SUBLANES, C_CONV).sum(axis=0)
            sg = sig[r0:r0 + CONV_SUB, :]
            cv = cur[r0:r0 + CONV_SUB, :C_CONV]
            du_ref[r0:r0 + CONV_SUB, :C_CONV] = (dglu * sg).astype(BF16)
            du_ref[r0:r0 + CONV_SUB, C_CONV:] = (dglu * cv * sg * (1.0 - sg)).astype(BF16)

        @pl.when(jnp.logical_and(b == n_ex - 1, i == nt - 1))
        def _():
            dw_ref[...] = jnp.sum(dwacc[...], axis=1)

    t = n_ex * lp
    cur_rows = lambda w, col: pl.BlockSpec((r, w), lambda b, i: (b * nt + i, col))
    nxt_rows = lambda w, col: pl.BlockSpec(
        (CONV_SUB, w), lambda b, i: (jnp.minimum((b * nt + i + 1) * hb, n_ex * nt * hb - 1), col))
    return _call(
        body, name="conv_bwd", grid=(n_ex, nt),
        in_specs=[cur_rows(C_CONV, 0), nxt_rows(C_CONV, 0), cur_rows(C_CONV, 0), nxt_rows(C_CONV, 0),
                  cur_rows(2 * C_CONV, 0),
                  pl.BlockSpec((CONV_SUB, 2 * C_CONV), lambda b, i: (jnp.maximum((b * nt + i) * hb - 1, 0), 0)),
                  _const_spec((32, C_CONV)), _const_spec((1, C_CONV)), _const_spec((1, C_CONV))],
        out_specs=[cur_rows(2 * C_CONV, 0), _acc_spec((32, C_CONV)), _acc_spec((1, C_CONV)),
                   _acc_spec((1, C_CONV)), _acc_spec((1, C_CONV))],
        out_shape=[jax.ShapeDtypeStruct((t, 2 * C_CONV), BF16), jax.ShapeDtypeStruct((32, C_CONV), F32),
                   jax.ShapeDtypeStruct((1, C_CONV), F32), jax.ShapeDtypeStruct((1, C_CONV), F32),
                   jax.ShapeDtypeStruct((1, C_CONV), F32)],
        scratch_shapes=[pltpu.VMEM((r + CONV_SUB, C_CONV), F32), pltpu.VMEM((r + CONV_SUB, C_CONV), F32),
                        pltpu.VMEM((32, 8, C_CONV), F32),
                        pltpu.VMEM((SUBLANES - 1, r + CONV_SUB - SUBLANES, C_CONV), F32),
                        pltpu.VMEM((SUBLANES - 1, r + CONV_SUB - SUBLANES, C_CONV), F32)],
        plan=plan,
    )(dycat, dycat, yc, yc, u, u, conv_w, ln_g, ln_b)


def _gla_bwd(dycat, u, states, w2, gb, ng, n_ex, lp, plan=None):
    nc = lp // CHUNK
    t = n_ex * lp

    def body(dy_ref, qk_ref, v_ref, g_ref, lr_ref, st_ref, w2_ref, gb_ref, ng_ref,
             du_ref, dw2_ref, dgb_ref, dng_ref, dstate):
        n = pl.program_id(0)
        chunk = nc - 1 - n

        @pl.when(n == 0)
        def _():
            dw2_ref[...] = jnp.zeros_like(dw2_ref)
            dgb_ref[...] = jnp.zeros_like(dgb_ref)
            dng_ref[...] = jnp.zeros_like(dng_ref)
            dstate[...] = jnp.zeros_like(dstate)

        for e in range(n_ex):
            one_example(e, chunk, dy_ref, qk_ref, v_ref, g_ref, lr_ref, st_ref, w2_ref, gb_ref, ng_ref,
                        du_ref, dw2_ref, dgb_ref, dng_ref, dstate)

    def one_example(e, chunk, dy_ref, qk_ref, v_ref, g_ref, lr_ref, st_ref, w2_ref, gb_ref, ng_ref,
                    du_ref, dw2_ref, dgb_ref, dng_ref, dstate):
        dy_ref, qk_ref, v_ref, g_ref, lr_ref, st_ref = (r.at[e] for r in (dy_ref, qk_ref, v_ref, g_ref, lr_ref, st_ref))
        du_ref, dstate = du_ref.at[e], dstate.at[e]
        qk = qk_ref[...]
        q, k = qk[:, :GLA_K], qk[:, GLA_K:]
        lr = lr_ref[...]
        z, a, live = _gla_gates(lr, w2_ref[...], gb_ref[...], chunk == 0)
        causal = _tri(True)
        b = jnp.dot(causal.astype(F32), a, preferred_element_type=F32, precision=lax.Precision.HIGHEST)
        bl = b[CHUNK - 1:CHUNK, :]
        e_pos, e_neg, e_dec = jnp.exp(b), jnp.exp(-b), jnp.exp(bl - b)
        q_f = q * (DK ** -0.5) * e_pos
        k_f = k * e_neg
        kd_f = k * e_dec
        q_in, k_in, k_dec = q_f.astype(BF16), k_f.astype(BF16), kd_f.astype(BF16)
        decay = jnp.exp(bl)
        v = v_ref[...]
        g = g_ref[...]
        dy = dy_ref[...]
        ngv = ng_ref[...]
        st = st_ref[...]
        st_b = st.astype(BF16)
        dst = dstate[...]
        dst_b = dst.astype(BF16)
        dqs, dks, dvs, dgs, dbs, dbls, new_dst = [], [], [], [], [], [], []
        dng = jnp.zeros((1, DV), F32)
        for h in range(N_HEADS):
            ks = slice(h * DK, (h + 1) * DK)
            vs = slice(h * DV, (h + 1) * DV)
            qh, kh, kdh = q_in[:, ks], k_in[:, ks], k_dec[:, ks]
            vh = v[:, vs].astype(BF16)
            s = jnp.where(causal, _dot_nt(qh, kh), 0.0).astype(BF16)
            o = _dot(s, vh) + _dot_nt(qh, st_b[:, ks])
            rstd = lax.rsqrt(jnp.mean(o * o, axis=-1, keepdims=True) + RMS_EPS)
            nrm = o * rstd
            gh = g[:, vs]
            sg = _sigmoid(gh)
            dyh = dy[:, vs]
            dgs.append(dyh * nrm * ngv * (sg * (1.0 + gh * (1.0 - sg))))
            dt = dyh * (gh * sg)
            dng = dng + jnp.sum(dt * nrm, axis=0, keepdims=True)
            dn = dt * ngv
            do = (rstd * (dn - nrm * jnp.mean(dn * nrm, axis=-1, keepdims=True))).astype(BF16)
            da = jnp.where(causal, _dot_nt(do, vh), 0.0).astype(BF16)
            dvs.append(_dot_tn(s, do) + _dot_nt(kdh, dst_b[:, ks]))
            dq_in = _dot(da, kh) + _dot(do, st_b[:, ks])
            dk_in = _dot_tn(da, qh)
            dk_dec = _dot(vh, dst_b[:, ks])
            new_dst.append(_dot_tn(do, qh) + decay[:, ks] * dst[:, ks])
            dbls.append(jnp.sum(dk_dec * kd_f[:, ks], axis=0, keepdims=True)
                        + decay[:, ks] * jnp.sum(dst[:, ks] * st[:, ks], axis=0, keepdims=True))
            dqs.append(dq_in * (DK ** -0.5) * e_pos[:, ks])
            dks.append(dk_in * e_neg[:, ks] + dk_dec * e_dec[:, ks])
            dbs.append(dq_in * q_f[:, ks] - dk_in * k_f[:, ks] - dk_dec * kd_f[:, ks])
        dstate[...] = jnp.concatenate(new_dst, axis=1)
        row = lax.broadcasted_iota(jnp.int32, (CHUNK, 1), 0)
        db = jnp.concatenate(dbs, axis=1) + jnp.where(row == CHUNK - 1, jnp.concatenate(dbls, axis=1), 0.0)
        da_log = jnp.dot(_tri(False).astype(F32), db, preferred_element_type=F32, precision=lax.Precision.HIGHEST)
        dz = jnp.where(live, da_log * (1.0 - _sigmoid(z)) * (1.0 / GATE_TAU), 0.0)
        dz_b = dz.astype(BF16)
        du_ref[:, 0:GLA_K] = jnp.concatenate(dqs, axis=1).astype(BF16)
        du_ref[:, GLA_K:2 * GLA_K] = jnp.concatenate(dks, axis=1).astype(BF16)
        du_ref[:, 2 * GLA_K:2 * GLA_K + GLA_V] = jnp.concatenate(dvs, axis=1).astype(BF16)
        du_ref[:, 2 * GLA_K + GLA_V:2 * GLA_K + 2 * GLA_V] = jnp.concatenate(dgs, axis=1).astype(BF16)
        du_ref[:, 2 * GLA_K + 2 * GLA_V:] = _dot_nt(dz_b, w2_ref[...]).astype(BF16)
        dw2_ref[...] += _dot_tn(lr.astype(BF16), dz_b)
        dgb_ref[...] += jnp.sum(dz, axis=0, keepdims=True)
        dng_ref[...] += dng

    u3 = u.reshape(n_ex, lp, D_IN_PAD)
    rev = lambda w, col: pl.BlockSpec((n_ex, CHUNK, w), lambda n: (0, nc - 1 - n, col))
    (du, d_w2, d_gb, d_ng), extra = _call(
        body, name="gla_bwd", grid=(nc,),
        in_specs=[rev(GLA_V, 1), rev(2 * GLA_K, 2), rev(GLA_V, 3), rev(GLA_V, 4), rev(128, 20),
                  pl.BlockSpec((n_ex, DV, GLA_K), lambda n: (0, nc - 1 - n, 0)),
                  _const_spec((128, GLA_K)), _const_spec((1, GLA_K)), _const_spec((1, DV))],
        out_specs=[rev(D_GLA_IN, 0), _acc_spec((128, GLA_K)), _acc_spec((1, GLA_K)), _acc_spec((1, DV))],
        out_shape=[jax.ShapeDtypeStruct((n_ex, lp, D_GLA_IN), BF16), jax.ShapeDtypeStruct((128, GLA_K), F32),
                   jax.ShapeDtypeStruct((1, GLA_K), F32), jax.ShapeDtypeStruct((1, DV), F32)],
        scratch_shapes=[pltpu.VMEM((n_ex, DV, GLA_K), F32)],
        plan=plan,
    )(dycat.reshape(n_ex, lp, D), u3, u3, u3, u3, states, w2, gb, ng)
    return (du.reshape(t, D_GLA_IN), d_w2, d_gb, d_ng), extra


def _in_proj_bwd(du_conv, du_gla, w_in_t_conv, w_in_t_gla, h0, dh1, g_mix, plan=None):
    t = h0.shape[0]
    r = _row_tile(t, 384)

    def body(dc_ref, dg_ref, wc_ref, wg_ref, h_ref, dh1_ref, g_ref, dh0_ref, dgm_ref):
        @pl.when(pl.program_id(0) == 0)
        def _():
            dgm_ref[...] = jnp.zeros_like(dgm_ref)

        dhn = _dot(dc_ref[...], wc_ref[...]) + _dot(dg_ref[...], wg_ref[...])
        h = h_ref[...]
        rstd = lax.rsqrt(jnp.mean(h * h, axis=-1, keepdims=True) + RMS_EPS)
        nrm = h * rstd
        dgm_ref[...] += jnp.sum(dhn * nrm, axis=0, keepdims=True)
        dn = dhn * g_ref[...]
        dh0_ref[...] = dh1_ref[...] + rstd * (dn - nrm * jnp.mean(dn * nrm, axis=-1, keepdims=True))

    rows = lambda w: pl.BlockSpec((r, w), lambda i: (i, 0))
    return _call(
        body, name="in_proj_bwd", grid=(t // r,),
        in_specs=[rows(2 * C_CONV), rows(D_GLA_IN), _const_spec((2 * C_CONV, D)), _const_spec((D_GLA_IN, D)),
                  rows(D), rows(D), _const_spec((1, D))],
        out_specs=[rows(D), _acc_spec((1, D))],
        out_shape=[jax.ShapeDtypeStruct((t, D), F32), jax.ShapeDtypeStruct((1, D), F32)],
        plan=plan,
    )(du_conv, du_gla, w_in_t_conv, w_in_t_gla, h0, dh1, g_mix)


def _wgrad(x, dy, name):
    t, m = x.shape
    n = dy.shape[1]
    tk = t // 3 if t % (3 * 128) == 0 else _row_tile(t, 384)
    tm = m if m <= D_GLA_IN else m // 2
    tn = n

    def body(x_ref, dy_ref, o_ref):
        @pl.when(pl.program_id(2) == 0)
        def _():
            o_ref[...] = jnp.zeros_like(o_ref)

        o_ref[...] += _dot_tn(x_ref[...].astype(BF16), dy_ref[...].astype(BF16))

    return pl.pallas_call(
        body, name=name, grid=(m // tm, n // tn, t // tk),
        in_specs=[pl.BlockSpec((tk, tm), lambda i, j, k: (k, i)), pl.BlockSpec((tk, tn), lambda i, j, k: (k, j))],
        out_specs=pl.BlockSpec((tm, tn), lambda i, j, k: (i, j)),
        out_shape=jax.ShapeDtypeStruct((m, n), F32),
        compiler_params=_params(3),
    )(x, dy)


def _mesh_pos():
    return lax.axis_index("x"), lax.axis_index("y"), lax.axis_index("c")


def _other_chips(x, y):
    return [(1 - x, y), (x, 1 - y), (1 - x, 1 - y)]


HBM_SPEC = pl.BlockSpec(memory_space=pltpu.HBM)


def _gather_shards(shards):
    n = len(shards)

    def body(*refs):
        ins, outs = refs[:n], refs[n:2 * n]
        send_sems, recv_sems, local_sems = refs[2 * n:]
        x, y, c = _mesh_pos()
        mine = 2 * x + y
        chips = _other_chips(x, y)
        local = [pltpu.make_async_copy(ins[a], outs[a].at[mine], local_sems.at[a]) for a in range(n)]
        for cp in local:
            cp.start()

        def remote(a, k, block):
            px, py = chips[k]
            return pltpu.make_async_remote_copy(
                src_ref=ins[a], dst_ref=outs[a].at[block], send_sem=send_sems.at[3 * a + k],
                recv_sem=recv_sems.at[3 * a + k], device_id=(px, py, c), device_id_type=MESH)

        sends = [remote(a, k, mine) for a in range(n) for k in range(3)]
        for cp in sends:
            cp.start()
        for a in range(n):
            for k, (px, py) in enumerate(chips):
                remote(a, k, 2 * px + py).wait_recv()
        for cp in sends:
            cp.wait_send()
        for cp in local:
            cp.wait()

    return pl.pallas_call(
        body, name="gather_shards",
        in_specs=[HBM_SPEC] * n, out_specs=[HBM_SPEC] * n,
        out_shape=[jax.ShapeDtypeStruct((N_CHIPS,) + s.shape, s.dtype) for s in shards],
        scratch_shapes=[pltpu.SemaphoreType.DMA((3 * n,)), pltpu.SemaphoreType.DMA((3 * n,)),
                        pltpu.SemaphoreType.DMA((n,))],
        compiler_params=pltpu.CompilerParams(has_side_effects=True),
    )(*shards)


def _send_half_to_sibling(g2):
    def body(g_ref, recv_ref, send_sem, recv_sem):
        x, y, c = _mesh_pos()
        cp = pltpu.make_async_remote_copy(
            src_ref=g_ref.at[1 - c], dst_ref=recv_ref, send_sem=send_sem, recv_sem=recv_sem,
            device_id=(x, y, 1 - c), device_id_type=MESH)
        cp.start()
        cp.wait()

    return pl.pallas_call(
        body, name="rs_to_sibling", in_specs=[HBM_SPEC], out_specs=HBM_SPEC,
        out_shape=jax.ShapeDtypeStruct(g2.shape[1:], g2.dtype),
        scratch_shapes=[pltpu.SemaphoreType.DMA(()), pltpu.SemaphoreType.DMA(())],
        compiler_params=pltpu.CompilerParams(has_side_effects=True),
    )(g2)


def _add_own_half(g2, recv, c):
    rows = N_CHIPS * HALF_ROWS
    tr = 512
    g2f = g2.reshape(2, rows, D)
    recvf = recv.reshape(rows, D)

    def body(c_ref, a_ref, b_ref, o_ref):
        o_ref[...] = a_ref[0] + b_ref[...]

    out = pl.pallas_call(
        body, name="rs_add_halves",
        grid_spec=pltpu.PrefetchScalarGridSpec(
            num_scalar_prefetch=1, grid=(rows // tr,),
            in_specs=[pl.BlockSpec((1, tr, D), lambda i, s: (s[0], i, 0)), pl.BlockSpec((tr, D), lambda i, s: (i, 0))],
            out_specs=pl.BlockSpec((tr, D), lambda i, s: (i, 0))),
        out_shape=jax.ShapeDtypeStruct((rows, D), F32),
        compiler_params=_params(1),
    )(jnp.reshape(c, (1,)).astype(jnp.int32), g2f, recvf)
    return out.reshape(N_CHIPS, HALF_ROWS, D)


def _exchange_chip_sums(p):
    def body(p_ref, out_ref, send_sems, recv_sems, local_sem):
        x, y, c = _mesh_pos()
        mine = 2 * x + y
        chips = _other_chips(x, y)
        local = pltpu.make_async_copy(p_ref.at[mine], out_ref.at[mine], local_sem)
        local.start()

        def remote(k, src_block, dst_block):
            px, py = chips[k]
            return pltpu.make_async_remote_copy(
                src_ref=p_ref.at[src_block], dst_ref=out_ref.at[dst_block], send_sem=send_sems.at[k],
                recv_sem=recv_sems.at[k], device_id=(px, py, c), device_id_type=MESH)

        sends = [remote(k, 2 * px + py, mine) for k, (px, py) in enumerate(chips)]
        for cp in sends:
            cp.start()
        for k, (px, py) in enumerate(chips):
            remote(k, mine, 2 * px + py).wait_recv()
        for cp in sends:
            cp.wait_send()
        local.wait()

    return pl.pallas_call(
        body, name="rs_chip_exchange", in_specs=[HBM_SPEC], out_specs=HBM_SPEC,
        out_shape=jax.ShapeDtypeStruct(p.shape, p.dtype),
        scratch_shapes=[pltpu.SemaphoreType.DMA((3,)), pltpu.SemaphoreType.DMA((3,)), pltpu.SemaphoreType.DMA(())],
        compiler_params=pltpu.CompilerParams(has_side_effects=True),
    )(p)


def _sum_chips(parts):
    tr = 512

    def body(p_ref, o_ref):
        o_ref[...] = ((p_ref[0] + p_ref[1]) + p_ref[2]) + p_ref[3]

    return pl.pallas_call(
        body, name="rs_sum_chips", grid=(HALF_ROWS // tr,),
        in_specs=[pl.BlockSpec((N_CHIPS, tr, D), lambda i: (0, i, 0))],
        out_specs=pl.BlockSpec((tr, D), lambda i: (i, 0)),
        out_shape=jax.ShapeDtypeStruct((HALF_ROWS, D), F32),
        compiler_params=_params(1),
    )(parts)


def _share_with_sibling(half):
    def body(h_ref, out_ref, send_sem, recv_sem, local_sem):
        x, y, c = _mesh_pos()
        local = pltpu.make_async_copy(h_ref, out_ref.at[c], local_sem)
        local.start()
        cp = pltpu.make_async_remote_copy(
            src_ref=h_ref, dst_ref=out_ref.at[c], send_sem=send_sem, recv_sem=recv_sem,
            device_id=(x, y, 1 - c), device_id_type=MESH)
        cp.start()
        pltpu.make_async_remote_copy(
            src_ref=h_ref, dst_ref=out_ref.at[1 - c], send_sem=send_sem, recv_sem=recv_sem,
            device_id=(x, y, 1 - c), device_id_type=MESH).wait_recv()
        cp.wait_send()
        local.wait()

    return pl.pallas_call(
        body, name="rs_share_sibling", in_specs=[HBM_SPEC], out_specs=HBM_SPEC,
        out_shape=jax.ShapeDtypeStruct((2,) + half.shape, half.dtype),
        scratch_shapes=[pltpu.SemaphoreType.DMA(()), pltpu.SemaphoreType.DMA(()), pltpu.SemaphoreType.DMA(())],
        compiler_params=pltpu.CompilerParams(has_side_effects=True),
    )(half)


def _adam_update(g, w, m, v):
    m2 = ADAM_B1 * m + (1.0 - ADAM_B1) * g
    v2 = ADAM_B2 * v + (1.0 - ADAM_B2) * (g * g)
    m_hat = m2 / (1.0 - ADAM_B1 ** ADAM_STEP)
    v_hat = v2 / (1.0 - ADAM_B2 ** ADAM_STEP)
    delta = -ADAM_LR * (m_hat / (jnp.sqrt(v_hat) + ADAM_EPS) + ADAM_WD * w)
    return delta, m2, v2


def _adamw_slab(g, w, m, v):
    rows = g.shape[0]
    tr = 256

    def body(g_ref, w_ref, m_ref, v_ref, d_ref, m2_ref, v2_ref):
        d_ref[...], m2_ref[...], v2_ref[...] = _adam_update(g_ref[...], w_ref[...], m_ref[...], v_ref[...])

    spec = pl.BlockSpec((tr, D), lambda i: (i, 0))
    return pl.pallas_call(
        body, name="adamw_slab", grid=(rows // tr,), in_specs=[spec] * 4, out_specs=[spec] * 3,
        out_shape=[jax.ShapeDtypeStruct((rows, D), F32)] * 3,
        compiler_params=_params(1),
    )(g, w, m, v)


def _allreduce_small_adamw(part, w, m, v):
    def body(p_ref, w_ref, m_ref, v_ref, g_ref, d_ref, m2_ref, v2_ref, slots, send_sems, recv_sems):
        x, y, c = _mesh_pos()
        mine = 4 * x + 2 * y + c
        peers = [(px, py, pc) for px in (x, 1 - x) for py in (y, 1 - y) for pc in (c, 1 - c)][1:]

        def remote(k, slot):
            return pltpu.make_async_remote_copy(
                src_ref=p_ref, dst_ref=slots.at[slot], send_sem=send_sems.at[k], recv_sem=recv_sems.at[k],
                device_id=peers[k], device_id_type=MESH)

        sends = [remote(k, mine) for k in range(7)]
        for cp in sends:
            cp.start()
        slots[mine] = p_ref[...]
        for k, (px, py, pc) in enumerate(peers):
            remote(k, 4 * px + 2 * py + pc).wait_recv()
        for cp in sends:
            cp.wait_send()
        g = slots[0]
        for d in range(1, 8):
            g = g + slots[d]
        g_ref[...] = g
        d_ref[...], m2_ref[...], v2_ref[...] = _adam_update(g, w_ref[...], m_ref[...], v_ref[...])

    vm = pl.BlockSpec(memory_space=pltpu.VMEM)
    shape = jax.ShapeDtypeStruct(part.shape, F32)
    return pl.pallas_call(
        body, name="small_allreduce_adamw", in_specs=[vm] * 4, out_specs=[vm] * 4, out_shape=[shape] * 4,
        scratch_shapes=[pltpu.VMEM((8,) + part.shape, F32), pltpu.SemaphoreType.DMA((7,)),
                        pltpu.SemaphoreType.DMA((7,))],
        compiler_params=pltpu.CompilerParams(has_side_effects=True),
    )(part, w, m, v)


def _half(ref, c, axis):
    n = ref.shape[axis] // 2
    return ref.at[(slice(None),) * axis + (pl.ds(c * n, n),)]


def _remote(src, dst, send_sem, recv_sem, device):
    return pltpu.make_async_remote_copy(src_ref=src, dst_ref=dst, send_sem=send_sem, recv_sem=recv_sem,
                                        device_id=device, device_id_type=MESH)


def _gather_weights(split, axes, whole):
    ns, n = len(split), len(split) + len(whole)

    def body(*refs):
        ins, outs = refs[:n], refs[n:2 * n]
        ici_send, ici_recv, d2d_send, d2d_recv, local_sems = refs[2 * n:]
        x, y, c = _mesh_pos()
        mine = 2 * x + y
        chips = _other_chips(x, y)
        local = [pltpu.make_async_copy(ins[a], outs[a].at[mine], local_sems.at[a]) for a in range(n)]
        for cp in local:
            cp.start()

        def ici(a, k, block):
            px, py = chips[k]
            src, dst = ins[a], outs[a].at[block]
            if a < ns:
                src, dst = _half(src, c, axes[a]), _half(dst, c, axes[a])
            return _remote(src, dst, ici_send.at[3 * a + k], ici_recv.at[3 * a + k], (px, py, c))

        def d2d(a, k, block, half):
            part = _half(outs[a].at[block], half, axes[a])
            return _remote(part, part, d2d_send.at[3 * a + k], d2d_recv.at[3 * a + k], (x, y, 1 - c))

        sends = [ici(a, k, mine) for a in range(n) for k in range(3)]
        for cp in sends:
            cp.start()
        for a in range(n):
            for k, (px, py) in enumerate(chips):
                ici(a, k, 2 * px + py).wait_recv()
                if a < ns:
                    sends.append(d2d(a, k, 2 * px + py, c))
                    sends[-1].start()
        for a in range(ns):
            for k, (px, py) in enumerate(chips):
                d2d(a, k, 2 * px + py, 1 - c).wait_recv()
        for cp in sends:
            cp.wait_send()
        for cp in local:
            cp.wait()

    arrays = list(split) + list(whole)
    return pl.pallas_call(
        body, name="gather_weights", in_specs=[HBM_SPEC] * n, out_specs=[HBM_SPEC] * n,
        out_shape=[jax.ShapeDtypeStruct((N_CHIPS,) + s.shape, s.dtype) for s in arrays],
        scratch_shapes=[pltpu.SemaphoreType.DMA((3 * n,)), pltpu.SemaphoreType.DMA((3 * n,)),
                        pltpu.SemaphoreType.DMA((3 * ns,)), pltpu.SemaphoreType.DMA((3 * ns,)),
                        pltpu.SemaphoreType.DMA((n,))],
        compiler_params=pltpu.CompilerParams(has_side_effects=True),
    )(*arrays)


def _rs_to_sibling(gs):
    n = len(gs)

    def body(*refs):
        ins, outs, send_sems, recv_sems = refs[:n], refs[n:2 * n], refs[2 * n], refs[2 * n + 1]
        x, y, c = _mesh_pos()
        copies = [_remote(_half(ins[a], 1 - c, 2), outs[a], send_sems.at[a], recv_sems.at[a], (x, y, 1 - c))
                  for a in range(n)]
        for cp in copies:
            cp.start()
        for cp in copies:
            cp.wait()

    return pl.pallas_call(
        body, name="rs_to_sibling", in_specs=[HBM_SPEC] * n, out_specs=[HBM_SPEC] * n,
        out_shape=[jax.ShapeDtypeStruct(g.shape[:2] + (g.shape[2] // 2,), g.dtype) for g in gs],
        scratch_shapes=[pltpu.SemaphoreType.DMA((n,)), pltpu.SemaphoreType.DMA((n,))],
        compiler_params=pltpu.CompilerParams(has_side_effects=True),
    )(*gs)


def _rs_add_halves(g, recv, c, name):
    _, rows, w = g.shape
    h = w // 2
    tr = rows // 2 if rows % 16 == 0 and rows > 64 else rows

    def body(c_ref, a_ref, b_ref, o_ref):
        o_ref[...] = (a_ref[...] + b_ref[...]).astype(BF16)

    return pl.pallas_call(
        body, name=name,
        grid_spec=pltpu.PrefetchScalarGridSpec(
            num_scalar_prefetch=1, grid=(N_CHIPS, rows // tr),
            in_specs=[pl.BlockSpec((1, tr, h), lambda j, i, s: (j, i, s[0])),
                      pl.BlockSpec((1, tr, h), lambda j, i, s: (j, i, 0))],
            out_specs=pl.BlockSpec((1, tr, h), lambda j, i, s: (j, i, 0))),
        out_shape=jax.ShapeDtypeStruct((N_CHIPS, rows, h), BF16),
        compiler_params=_params(2),
    )(jnp.reshape(c, (1,)).astype(jnp.int32), g, recv)


def _rs_chip_exchange(ps):
    n = len(ps)

    def body(*refs):
        ins, outs = refs[:n], refs[n:2 * n]
        send_sems, recv_sems, local_sems = refs[2 * n:]
        x, y, c = _mesh_pos()
        mine = 2 * x + y
        chips = _other_chips(x, y)
        local = [pltpu.make_async_copy(ins[a].at[mine], outs[a].at[mine], local_sems.at[a]) for a in range(n)]
        for cp in local:
            cp.start()

        def ici(a, k, src_block, dst_block):
            px, py = chips[k]
            return _remote(ins[a].at[src_block], outs[a].at[dst_block], send_sems.at[3 * a + k],
                           recv_sems.at[3 * a + k], (px, py, c))

        sends = [ici(a, k, 2 * px + py, mine) for a in range(n) for k, (px, py) in enumerate(chips)]
        for cp in sends:
            cp.start()
        for a in range(n):
            for k, (px, py) in enumerate(chips):
                ici(a, k, mine, 2 * px + py).wait_recv()
        for cp in sends:
            cp.wait_send()
        for cp in local:
            cp.wait()

    return pl.pallas_call(
        body, name="rs_chip_exchange", in_specs=[HBM_SPEC] * n, out_specs=[HBM_SPEC] * n,
        out_shape=[jax.ShapeDtypeStruct(p.shape, p.dtype) for p in ps],
        scratch_shapes=[pltpu.SemaphoreType.DMA((3 * n,)), pltpu.SemaphoreType.DMA((3 * n,)),
                        pltpu.SemaphoreType.DMA((n,))],
        compiler_params=pltpu.CompilerParams(has_side_effects=True),
    )(*ps)


def _rs_sum_chips(parts, name):
    _, rows, h = parts.shape
    tr = rows // 2 if rows % 16 == 0 and rows > 64 else rows

    def body(p_ref, o_ref):
        p = p_ref[...].astype(F32)
        o_ref[...] = ((p[0] + p[1]) + p[2]) + p[3]

    return pl.pallas_call(
        body, name=name, grid=(rows // tr,),
        in_specs=[pl.BlockSpec((N_CHIPS, tr, h), lambda i: (0, i, 0))],
        out_specs=pl.BlockSpec((tr, h), lambda i: (i, 0)),
        out_shape=jax.ShapeDtypeStruct((rows, h), F32),
        compiler_params=_params(1),
    )(parts)


def _rs_share(halves):
    n = len(halves)

    def body(*refs):
        ins, outs = refs[:n], refs[n:2 * n]
        send_sems, recv_sems, local_sems = refs[2 * n:]
        x, y, c = _mesh_pos()
        local = [pltpu.make_async_copy(ins[a], _half(outs[a], c, 1), local_sems.at[a]) for a in range(n)]
        for cp in local:
            cp.start()
        sends = [_remote(ins[a], _half(outs[a], c, 1), send_sems.at[a], recv_sems.at[a], (x, y, 1 - c))
                 for a in range(n)]
        for cp in sends:
            cp.start()
        for a in range(n):
            _remote(ins[a], _half(outs[a], 1 - c, 1), send_sems.at[a], recv_sems.at[a], (x, y, 1 - c)).wait_recv()
        for cp in sends:
            cp.wait_send()
        for cp in local:
            cp.wait()

    return pl.pallas_call(
        body, name="rs_share", in_specs=[HBM_SPEC] * n, out_specs=[HBM_SPEC] * n,
        out_shape=[jax.ShapeDtypeStruct((p.shape[0], 2 * p.shape[1]), p.dtype) for p in halves],
        scratch_shapes=[pltpu.SemaphoreType.DMA((n,)), pltpu.SemaphoreType.DMA((n,)),
                        pltpu.SemaphoreType.DMA((n,))],
        compiler_params=pltpu.CompilerParams(has_side_effects=True),
    )(*halves)


def _adamw(g, w, m, v, name):
    rows, cols = g.shape
    tr = 256 if rows % 256 == 0 else (rows // 2 if rows % 16 == 0 and rows > 64 else rows)

    def body(g_ref, w_ref, m_ref, v_ref, d_ref, m2_ref, v2_ref):
        d_ref[...], m2_ref[...], v2_ref[...] = _adam_update(g_ref[...], w_ref[...], m_ref[...], v_ref[...])

    spec = pl.BlockSpec((tr, cols), lambda i: (i, 0))
    return pl.pallas_call(
        body, name=name, grid=(rows // tr,), in_specs=[spec] * 4, out_specs=[spec] * 3,
        out_shape=[jax.ShapeDtypeStruct((rows, cols), F32)] * 3,
        compiler_params=_params(1),
    )(g, w, m, v)


def _rows_of(a):
    flat = a.reshape(-1)
    pad = (-flat.shape[0]) % D
    if pad:
        flat = jnp.concatenate([flat, jnp.zeros((pad,), flat.dtype)])
    return flat.reshape(-1, D)


SLAB_PARTS = (("w_in", (D, D_IN // N_CHIPS)), ("w_out", (D // N_CHIPS, D)), ("w_ffn_gate", (D, D_FF // N_CHIPS)),
              ("w_ffn_up", (D, D_FF // N_CHIPS)), ("w_ffn_down", (D_FF // N_CHIPS, D)),
              ("meta_tokens", (N_META, D // N_CHIPS)), ("conv_w", (CONV_W, C_CONV // N_CHIPS)),
              ("gla_w_gate2", (RANK, GLA_K // N_CHIPS)))


def _pack_slab(parts):
    rows = [_rows_of(parts[name].reshape(shape)) for name, shape in SLAB_PARTS]
    used = sum(r.shape[0] for r in rows)
    rows.append(jnp.zeros((SLAB_ROWS - used, D), F32))
    return jnp.concatenate(rows, axis=0)


def _unpack_slab(slab, lead):
    out, r0 = {}, 0
    for name, shape in SLAB_PARTS:
        size = shape[0] * shape[1]
        nrows = -(-size // D)
        out[name] = slab[r0:r0 + nrows].reshape(-1)[:size].reshape(lead[name] + shape)
        r0 += nrows
    return out


SMALL_PARTS = (("norm_mix_g", 0, 0, D), ("norm_ffn_g", 1, 0, D), ("norm_final_g", 2, 0, D),
               ("conv_b", 3, 0, C_CONV), ("conv_ln_g", 3, C_CONV, C_CONV), ("conv_ln_b", 4, 0, C_CONV),
               ("gla_gate_b", 4, C_CONV, GLA_K), ("gla_norm_g", 4, C_CONV + GLA_K, DV))


def _pack_small(parts):
    slab = jnp.zeros((SMALL_ROWS, D), F32)
    for name, row, col, size in SMALL_PARTS:
        slab = lax.dynamic_update_slice(slab, parts[name].reshape(1, size).astype(F32), (row, col))
    return slab


def _unpack_small(slab, shapes):
    return {name: slab[row, col:col + size].reshape(shapes[name]) for name, row, col, size in SMALL_PARTS}


def _column_block(full, j, width):
    return lax.dynamic_slice_in_dim(full, j * width, width, axis=1)


def _local_step(x, target, w):
    n_ex, seq, _ = x.shape
    lp = HEAD_ROWS + seq
    t = n_ex * lp
    meta = jnp.broadcast_to(w["meta_tokens"][None], (n_ex, N_META, D))
    h0 = jnp.concatenate([jnp.zeros((n_ex, PAD_ROWS, D), F32), meta, x], axis=1).reshape(t, D)
    tgt = jnp.concatenate([jnp.zeros((n_ex, HEAD_ROWS, D), F32), target], axis=1).reshape(t, D)
    row_mask = jnp.concatenate([jnp.zeros((n_ex, HEAD_ROWS, 1), F32), jnp.ones((n_ex, seq, 1), F32)],
                               axis=1).reshape(t, 1)

    u, hn = _in_proj(h0, w["norm_mix_g"], w["w_in"])
    yc, y_conv = _conv_fwd(u, w["conv_w"], w["conv_b"], w["conv_ln_g"], w["conv_ln_b"], n_ex, lp)
    y_gla, states = _gla_fwd(u, w["gla_w_gate2"], w["gla_gate_b"], w["gla_norm_g"], n_ex, lp)
    h1, hn2, gate, up, act = _mix_out_ffn_up(h0, y_conv, y_gla, w["w_out"], w["norm_ffn_g"],
                                             w["w_ffn_gate_t"], w["w_ffn_up_t"])
    dh2, loss, d_final_g = _ffn_down_loss(act, w["w_ffn_down"], h1, tgt, w["norm_final_g"], row_mask)

    dgate, dup, dh1, dycat, d_ffn_g = _ffn_bwd(dh2, gate, up, h1, w["w_ffn_down"], w["w_ffn_gate_t"],
                                                w["w_ffn_up_t"], w["w_out"], w["norm_ffn_g"])
    du_conv, d_conv_w, d_conv_b, d_ln_g, d_ln_b = _conv_bwd(dycat, yc, u, w["conv_w"], w["conv_ln_g"],
                                                            w["conv_ln_b"], n_ex, lp)
    du_gla, d_w2, d_gate_b, d_norm_g = _gla_bwd(dycat, u, states, w["gla_w_gate2"], w["gla_gate_b"],
                                                w["gla_norm_g"], n_ex, lp)
    dh0, d_mix_g = _in_proj_bwd(du_conv, du_gla, w["w_in"][:, :2 * C_CONV], w["w_in"][:, 2 * C_CONV:],
                                h0, dh1, w["norm_mix_g"])

    d_w_in_t = jnp.concatenate([_wgrad(du_conv, hn, "wgrad_in_conv"), _wgrad(du_gla, hn, "wgrad_in_gla")],
                               axis=0)[:D_IN]
    d_w_out = jnp.concatenate([_wgrad(y_conv, dh1, "wgrad_out_conv"), _wgrad(y_gla, dh1, "wgrad_out_gla")], axis=0)
    dh0 = dh0.reshape(n_ex, lp, D)
    grads = {
        "w_in_t": d_w_in_t, "w_out": d_w_out,
        "w_ffn_gate_t": _wgrad(dgate, hn2, "wgrad_gate"), "w_ffn_up_t": _wgrad(dup, hn2, "wgrad_up"),
        "w_ffn_down": _wgrad(act, dh2, "wgrad_down"),
        "meta_tokens": jnp.sum(dh0[:, PAD_ROWS:HEAD_ROWS], axis=0),
        "conv_w": d_conv_w, "gla_w_gate2": d_w2[:RANK],
        "norm_mix_g": d_mix_g, "norm_ffn_g": d_ffn_g, "norm_final_g": d_final_g,
        "conv_b": d_conv_b, "conv_ln_g": d_ln_g, "conv_ln_b": d_ln_b,
        "gla_gate_b": d_gate_b, "gla_norm_g": d_norm_g,
    }
    return loss[0, 0], dh0[:, HEAD_ROWS:], grads


WEIGHT_NAMES = ("meta_tokens", "norm_mix_g", "w_in", "conv_w", "conv_b", "conv_ln_g", "conv_ln_b", "gla_w_gate2",
                "gla_gate_b", "gla_norm_g", "w_out", "norm_ffn_g", "w_ffn_gate", "w_ffn_up", "w_ffn_down",
                "norm_final_g")
MATMUL_WEIGHTS = ("w_in", "w_out", "w_ffn_gate", "w_ffn_up", "w_ffn_down")
ROW_SHARDED = ("w_out", "w_ffn_down")


def _full_weights(ws):
    sh = lambda name: ws[name].reshape(ws[name].shape[-2:])
    split = [sh("w_in").astype(BF16), sh("w_out").astype(BF16), sh("w_ffn_gate").T.astype(BF16),
             sh("w_ffn_up").T.astype(BF16), sh("w_ffn_down").astype(BF16)]
    whole = [sh("meta_tokens"), sh("conv_w"), sh("gla_w_gate2")]
    w_in, w_out, gate_t, up_t, down, meta, conv_w, w2 = _gather_weights(split, [0, 0, 0, 0, 0], whole)
    cols = lambda a: jnp.concatenate([a[j] for j in range(N_CHIPS)], axis=1)
    full = {name: ws[name].reshape(1, -1) for name, _, _, _ in SMALL_PARTS}
    full["w_in"] = jnp.concatenate([cols(w_in), jnp.zeros((D, D_IN_PAD - D_IN), BF16)], axis=1)
    full["w_out"] = w_out.reshape(D, D)
    full["w_ffn_gate_t"] = gate_t.reshape(D_FF, D)
    full["w_ffn_up_t"] = up_t.reshape(D_FF, D)
    full["w_ffn_down"] = down.reshape(D_FF, D)
    full["meta_tokens"] = cols(meta)
    full["conv_w"] = jnp.concatenate([cols(conv_w), jnp.zeros((32 - CONV_W, C_CONV), F32)], axis=0)
    full["gla_w_gate2"] = jnp.concatenate([cols(w2), jnp.zeros((128 - RANK, GLA_K), F32)], axis=0).astype(BF16)
    return full


SMALL_RS_ROWS = 48


def _pack_small_sharded(grads):
    by_chip = lambda g, w: jnp.transpose(g.reshape(g.shape[0], N_CHIPS, w), (1, 0, 2))
    meta = by_chip(grads["meta_tokens"], D // N_CHIPS)
    conv = by_chip(grads["conv_w"], C_CONV // N_CHIPS).reshape(N_CHIPS, 16, 256)
    w2 = by_chip(grads["gla_w_gate2"], GLA_K // N_CHIPS).reshape(N_CHIPS, 4, 256)
    pad = jnp.zeros((N_CHIPS, SMALL_RS_ROWS - 36, 256), F32)
    return jnp.concatenate([meta, conv, w2, pad], axis=1)


def _unpack_small_sharded(g):
    return {"meta_tokens": g[0:16], "conv_w": g[16:32].reshape(32, C_CONV // N_CHIPS)[:CONV_W],
            "gla_w_gate2": g[32:36].reshape(RANK, GLA_K // N_CHIPS)}


def _kernel_without_overlap(x, meta_tokens, norm_mix_g, w_in, conv_w, conv_b, conv_ln_g, conv_ln_b, gla_w_gate2, gla_gate_b, gla_norm_g, w_out, norm_ffn_g, w_ffn_gate, w_ffn_up, w_ffn_down, norm_final_g, loss_target, m_meta_tokens, m_norm_mix_g, m_w_in, m_conv_w, m_conv_b, m_conv_ln_g, m_conv_ln_b, m_gla_w_gate2, m_gla_gate_b, m_gla_norm_g, m_w_out, m_norm_ffn_g, m_w_ffn_gate, m_w_ffn_up, m_w_ffn_down, m_norm_final_g, v_meta_tokens, v_norm_mix_g, v_w_in, v_conv_w, v_conv_b, v_conv_ln_g, v_conv_ln_b, v_gla_w_gate2, v_gla_gate_b, v_gla_norm_g, v_w_out, v_norm_ffn_g, v_w_ffn_gate, v_w_ffn_up, v_w_ffn_down, v_norm_final_g):
    ws = dict(zip(WEIGHT_NAMES, (meta_tokens, norm_mix_g, w_in, conv_w, conv_b, conv_ln_g, conv_ln_b, gla_w_gate2,
                                 gla_gate_b, gla_norm_g, w_out, norm_ffn_g, w_ffn_gate, w_ffn_up, w_ffn_down,
                                 norm_final_g)))
    ms = dict(zip(WEIGHT_NAMES, (m_meta_tokens, m_norm_mix_g, m_w_in, m_conv_w, m_conv_b, m_conv_ln_g, m_conv_ln_b,
                                 m_gla_w_gate2, m_gla_gate_b, m_gla_norm_g, m_w_out, m_norm_ffn_g, m_w_ffn_gate,
                                 m_w_ffn_up, m_w_ffn_down, m_norm_final_g)))
    vs = dict(zip(WEIGHT_NAMES, (v_meta_tokens, v_norm_mix_g, v_w_in, v_conv_w, v_conv_b, v_conv_ln_g, v_conv_ln_b,
                                 v_gla_w_gate2, v_gla_gate_b, v_gla_norm_g, v_w_out, v_norm_ffn_g, v_w_ffn_gate,
                                 v_w_ffn_up, v_w_ffn_down, v_norm_final_g)))
    c = lax.axis_index("c")

    full = _full_weights(ws)
    loss, grad_x, grads = _local_step(x, loss_target, full)
    loss = lax.psum(loss, ("x", "y", "c"))

    rs_names = ("w_in", "w_out", "w_ffn_gate", "w_ffn_up", "w_ffn_down", "small")
    by_owner = [grads["w_in_t"].reshape(N_CHIPS, D_IN // N_CHIPS, D), grads["w_out"].reshape(N_CHIPS, D // N_CHIPS, D),
                grads["w_ffn_gate_t"].reshape(N_CHIPS, D_FF // N_CHIPS, D),
                grads["w_ffn_up_t"].reshape(N_CHIPS, D_FF // N_CHIPS, D),
                grads["w_ffn_down"].reshape(N_CHIPS, D_FF // N_CHIPS, D), _pack_small_sharded(grads)]
    from_sibling = _rs_to_sibling(by_owner)
    chip_sums = [_rs_add_halves(g, r, c, "rs_add_" + nm) for g, r, nm in zip(by_owner, from_sibling, rs_names)]
    halves = [_rs_sum_chips(p, "rs_sum_" + nm) for p, nm in zip(_rs_chip_exchange(chip_sums), rs_names)]
    reduced = dict(zip(rs_names, _rs_share(halves)))
    g_sharded = {"w_in": reduced["w_in"].T, "w_out": reduced["w_out"], "w_ffn_gate": reduced["w_ffn_gate"].T,
                 "w_ffn_up": reduced["w_ffn_up"].T, "w_ffn_down": reduced["w_ffn_down"],
                 **_unpack_small_sharded(reduced["small"])}
    out = {"grad": {}, "delta": {}, "new_m": {}, "new_v": {}}
    for name, g in g_sharded.items():
        shape = ws[name].shape
        flat = lambda a: a.reshape(shape[-2:])
        delta, new_m, new_v = _adamw(g, flat(ws[name]), flat(ms[name]), flat(vs[name]), "adamw_" + name)
        for kind, a in (("grad", g), ("delta", delta), ("new_m", new_m), ("new_v", new_v)):
            out[kind][name] = a.reshape(shape)

    small_shapes = {name: ws[name].shape for name, _, _, _ in SMALL_PARTS}
    g_s, d_s, m_s, v_s = _allreduce_small_adamw(_pack_small(grads), _pack_small(ws), _pack_small(ms), _pack_small(vs))
    for kind, slab in (("grad", g_s), ("delta", d_s), ("new_m", m_s), ("new_v", v_s)):
        out[kind].update(_unpack_small(slab, small_shapes))

    return (loss, grad_x, *[out[kind][name] for kind in ("grad", "delta", "new_m", "new_v") for name in WEIGHT_NAMES])


def _gather_plan(split, whole=(), axes=None):
    split, whole = list(split), list(whole)
    ns, n = len(split), len(split) + len(whole)

    def make(ins, outs, sems):
        ici_send, ici_recv, d2d_send, d2d_recv, own_send, own_recv = sems
        x, y, c = _mesh_pos()
        mine = 2 * x + y
        chips = _other_chips(x, y)
        blocks = [2 * px + py for px, py in chips]

        def own(a):
            return _remote(ins[a], outs[a].at[mine], own_send.at[a], own_recv.at[a], (x, y, 1 - c))

        def ici(a, k, block):
            px, py = chips[k]
            src, dst = ins[a], outs[a].at[block]
            if a < ns:
                src, dst = _half(src, c, axes[a]), _half(dst, c, axes[a])
            return _remote(src, dst, ici_send.at[3 * a + k], ici_recv.at[3 * a + k], (px, py, c))

        def d2d(a, k, half):
            part = _half(outs[a].at[blocks[k]], half, axes[a])
            return _remote(part, part, d2d_send.at[3 * a + k], d2d_recv.at[3 * a + k], (x, y, 1 - c))

        def start():
            for a in range(n):
                for k in range(3):
                    ici(a, k, mine).start()
                own(a).start()

        def finish():
            for a in range(n):
                for k in range(3):
                    ici(a, k, blocks[k]).wait_recv()
                    if a < ns:
                        d2d(a, k, c).start()
            for a in range(ns):
                for k in range(3):
                    d2d(a, k, 1 - c).wait_recv()
            for a in range(n):
                for k in range(3):
                    ici(a, k, mine).wait_send()
                    if a < ns:
                        d2d(a, k, c).wait_send()
                own(a).wait()

        return start, finish

    arrays = split + whole
    axes = [0] * ns if axes is None else list(axes)
    return _Plan(arrays, [jax.ShapeDtypeStruct((N_CHIPS,) + s.shape, s.dtype) for s in arrays],
                 [pltpu.SemaphoreType.DMA((3 * n,)), pltpu.SemaphoreType.DMA((3 * n,)),
                  pltpu.SemaphoreType.DMA((3 * ns,)), pltpu.SemaphoreType.DMA((3 * ns,)),
                  pltpu.SemaphoreType.DMA((n,)), pltpu.SemaphoreType.DMA((n,))], make)


def _to_sibling_plan(gs):
    n = len(gs)

    def make(ins, outs, sems):
        send_sems, recv_sems = sems
        x, y, c = _mesh_pos()

        def copy(a):
            return _remote(_half(ins[a], 1 - c, 2), outs[a], send_sems.at[a], recv_sems.at[a], (x, y, 1 - c))

        def start():
            for a in range(n):
                copy(a).start()

        def finish():
            for a in range(n):
                copy(a).wait()

        return start, finish

    return _Plan(list(gs), [jax.ShapeDtypeStruct(g.shape[:2] + (g.shape[2] // 2,), g.dtype) for g in gs],
                 [pltpu.SemaphoreType.DMA((n,)), pltpu.SemaphoreType.DMA((n,))], make)


def _chip_exchange_plan(ps):
    n = len(ps)

    def make(ins, outs, sems):
        send_sems, recv_sems = sems
        x, y, c = _mesh_pos()
        chips = _other_chips(x, y)

        def ici(a, k):
            px, py = chips[k]
            return _remote(ins[a].at[2 * px + py], outs[a].at[k], send_sems.at[3 * a + k],
                           recv_sems.at[3 * a + k], (px, py, c))

        def start():
            for a in range(n):
                for k in range(3):
                    ici(a, k).start()

        def finish():
            for a in range(n):
                for k in range(3):
                    ici(a, k).wait()

        return start, finish

    return _Plan(list(ps), [jax.ShapeDtypeStruct((3,) + p.shape[1:], p.dtype) for p in ps],
                 [pltpu.SemaphoreType.DMA((3 * n,)), pltpu.SemaphoreType.DMA((3 * n,))], make)


def _share_plan(halves):
    n = len(halves)

    def make(ins, outs, sems):
        send_sems, recv_sems = sems
        x, y, c = _mesh_pos()

        def d2d(a):
            return _remote(ins[a], outs[a], send_sems.at[a], recv_sems.at[a], (x, y, 1 - c))

        def start():
            for a in range(n):
                d2d(a).start()

        def finish():
            for a in range(n):
                d2d(a).wait()

        return start, finish

    return _Plan(list(halves), [jax.ShapeDtypeStruct(p.shape, p.dtype) for p in halves],
                 [pltpu.SemaphoreType.DMA((n,)), pltpu.SemaphoreType.DMA((n,))], make)


def _rs_sum(own, others, mine, name):
    _, rows, h = own.shape
    tr = rows // 2 if rows % 16 == 0 and rows > 64 else rows

    def body(mine_ref, own_ref, oth_ref, o_ref):
        p = oth_ref[...].astype(F32)
        o_ref[...] = ((own_ref[0].astype(F32) + p[0]) + p[1]) + p[2]

    return pl.pallas_call(
        body, name=name,
        grid_spec=pltpu.PrefetchScalarGridSpec(
            num_scalar_prefetch=1, grid=(rows // tr,),
            in_specs=[pl.BlockSpec((1, tr, h), lambda i, s: (s[0], i, 0)),
                      pl.BlockSpec((3, tr, h), lambda i, s: (0, i, 0))],
            out_specs=pl.BlockSpec((tr, h), lambda i, s: (i, 0))),
        out_shape=jax.ShapeDtypeStruct((rows, h), F32),
        compiler_params=_params(1),
    )(jnp.reshape(mine, (1,)).astype(jnp.int32), own, others)


def _join(mine, theirs, c):
    return jnp.where(c == 0, jnp.concatenate([mine, theirs], axis=1), jnp.concatenate([theirs, mine], axis=1))


def _exchange(plan, name):
    n_in, n_out = len(plan.arrays), len(plan.out_shape)

    def body(*refs):
        start, finish = plan.make(refs[:n_in], refs[n_in:n_in + n_out], refs[n_in + n_out:])
        start()
        finish()

    return pl.pallas_call(
        body, name=name, in_specs=[HBM_SPEC] * n_in, out_specs=[HBM_SPEC] * n_out, out_shape=list(plan.out_shape),
        scratch_shapes=list(plan.sems), compiler_params=pltpu.CompilerParams(has_side_effects=True),
    )(*plan.arrays)


def _adamw_halves(mine, theirs, c, w, m, v, name):
    rows, h = mine.shape
    tr = rows // 2 if rows % 16 == 0 else rows

    def body(c_ref, a_ref, b_ref, w_ref, m_ref, v_ref, go_ref, d_ref, m2_ref, v2_ref):
        g = jnp.where(pl.program_id(1) == c_ref[0], a_ref[...], b_ref[...])
        go_ref[...] = g
        d_ref[...], m2_ref[...], v2_ref[...] = _adam_update(g, w_ref[...], m_ref[...], v_ref[...])

    half = pl.BlockSpec((tr, h), lambda i, j, s: (i, 0))
    spec = pl.BlockSpec((tr, h), lambda i, j, s: (i, j))
    return pl.pallas_call(
        body, name=name,
        grid_spec=pltpu.PrefetchScalarGridSpec(num_scalar_prefetch=1, grid=(rows // tr, 2),
                                               in_specs=[half, half, spec, spec, spec], out_specs=[spec] * 4),
        out_shape=[jax.ShapeDtypeStruct((rows, 2 * h), F32)] * 4,
        compiler_params=_params(2),
    )(jnp.reshape(c, (1,)).astype(jnp.int32), mine, theirs, w, m, v)


def _columns(gathered):
    return jnp.concatenate([gathered[j] for j in range(N_CHIPS)], axis=1)


def kernel(x, meta_tokens, norm_mix_g, w_in, conv_w, conv_b, conv_ln_g, conv_ln_b, gla_w_gate2, gla_gate_b, gla_norm_g, w_out, norm_ffn_g, w_ffn_gate, w_ffn_up, w_ffn_down, norm_final_g, loss_target, m_meta_tokens, m_norm_mix_g, m_w_in, m_conv_w, m_conv_b, m_conv_ln_g, m_conv_ln_b, m_gla_w_gate2, m_gla_gate_b, m_gla_norm_g, m_w_out, m_norm_ffn_g, m_w_ffn_gate, m_w_ffn_up, m_w_ffn_down, m_norm_final_g, v_meta_tokens, v_norm_mix_g, v_w_in, v_conv_w, v_conv_b, v_conv_ln_g, v_conv_ln_b, v_gla_w_gate2, v_gla_gate_b, v_gla_norm_g, v_w_out, v_norm_ffn_g, v_w_ffn_gate, v_w_ffn_up, v_w_ffn_down, v_norm_final_g):
    ws = dict(zip(WEIGHT_NAMES, (meta_tokens, norm_mix_g, w_in, conv_w, conv_b, conv_ln_g, conv_ln_b, gla_w_gate2,
                                 gla_gate_b, gla_norm_g, w_out, norm_ffn_g, w_ffn_gate, w_ffn_up, w_ffn_down,
                                 norm_final_g)))
    ms = dict(zip(WEIGHT_NAMES, (m_meta_tokens, m_norm_mix_g, m_w_in, m_conv_w, m_conv_b, m_conv_ln_g, m_conv_ln_b,
                                 m_gla_w_gate2, m_gla_gate_b, m_gla_norm_g, m_w_out, m_norm_ffn_g, m_w_ffn_gate,
                                 m_w_ffn_up, m_w_ffn_down, m_norm_final_g)))
    vs = dict(zip(WEIGHT_NAMES, (v_meta_tokens, v_norm_mix_g, v_w_in, v_conv_w, v_conv_b, v_conv_ln_g, v_conv_ln_b,
                                 v_gla_w_gate2, v_gla_gate_b, v_gla_norm_g, v_w_out, v_norm_ffn_g, v_w_ffn_gate,
                                 v_w_ffn_up, v_w_ffn_down, v_norm_final_g)))
    c = lax.axis_index("c")
    shard = lambda d, name: d[name].reshape(d[name].shape[-2:])
    vec = {name: ws[name].reshape(1, -1) for name, _, _, _ in SMALL_PARTS}
    n_ex, seq, _ = x.shape
    lp = HEAD_ROWS + seq
    t = n_ex * lp

    w_in_g, meta_g, conv_w_g, w2_g = _exchange(
        _gather_plan([shard(ws, "w_in").T.astype(BF16)],
                     [shard(ws, "meta_tokens"), shard(ws, "conv_w"), shard(ws, "gla_w_gate2")], axes=[1]),
        "gather_first")
    w_in_t = jnp.concatenate([w_in_g.reshape(D_IN, D), jnp.zeros((D_IN_PAD - D_IN, D), BF16)], axis=0)
    w_in_full = w_in_t.T
    conv_w_full = jnp.concatenate([_columns(conv_w_g), jnp.zeros((32 - CONV_W, C_CONV), F32)], axis=0)
    w2_full = jnp.concatenate([_columns(w2_g), jnp.zeros((128 - RANK, GLA_K), F32)], axis=0).astype(BF16)

    meta = jnp.broadcast_to(_columns(meta_g)[None], (n_ex, N_META, D))
    h0 = jnp.concatenate([jnp.zeros((n_ex, PAD_ROWS, D), F32), meta, x], axis=1).reshape(t, D)
    tgt = jnp.concatenate([jnp.zeros((n_ex, HEAD_ROWS, D), F32), loss_target], axis=1).reshape(t, D)
    row_mask = jnp.concatenate([jnp.zeros((n_ex, HEAD_ROWS, 1), F32), jnp.ones((n_ex, seq, 1), F32)],
                               axis=1).reshape(t, 1)

    (u, hn), (w_out_g,) = _in_proj(h0, vec["norm_mix_g"], w_in_full,
                                   plan=_gather_plan([shard(ws, "w_out").astype(BF16)]))
    (yc, y_conv), (gate_g,) = _conv_fwd(
        u, conv_w_full, vec["conv_b"], vec["conv_ln_g"], vec["conv_ln_b"], n_ex, lp,
        plan=_gather_plan([shard(ws, "w_ffn_gate").T.astype(BF16)]))
    (y_gla, states), (up_g, down_g) = _gla_fwd(
        u, w2_full, vec["gla_gate_b"], vec["gla_norm_g"], n_ex, lp,
        plan=_gather_plan([shard(ws, "w_ffn_up").T.astype(BF16), shard(ws, "w_ffn_down").astype(BF16)]))
    w_out_full, w_down_full = w_out_g.reshape(D, D), down_g.reshape(D_FF, D)
    w_gate_t, w_up_t = gate_g.reshape(D_FF, D), up_g.reshape(D_FF, D)

    h1, hn2, gate, up, act = _mix_out_ffn_up(h0, y_conv, y_gla, w_out_full, vec["norm_ffn_g"], w_gate_t.T, w_up_t.T)
    dh2, loss, d_final_g = _ffn_down_loss(act, w_down_full, h1, tgt, vec["norm_final_g"], row_mask)
    loss = lax.psum(loss[0, 0], ("x", "y", "c"))
    dgate, dup, dh1, dycat, d_ffn_g = _ffn_bwd(dh2, gate, up, h1, w_down_full.T, w_gate_t, w_up_t, w_out_full.T,
                                                vec["norm_ffn_g"])

    early = ("w_out", "w_ffn_gate", "w_ffn_up", "w_ffn_down")
    d_w_out = jnp.concatenate([_wgrad(y_conv, dh1, "wgrad_out_conv"), _wgrad(y_gla, dh1, "wgrad_out_gla")], axis=0)
    by_owner = [d_w_out.reshape(N_CHIPS, D // N_CHIPS, D),
                _wgrad(dgate, hn2, "wgrad_gate").reshape(N_CHIPS, D_FF // N_CHIPS, D),
                _wgrad(dup, hn2, "wgrad_up").reshape(N_CHIPS, D_FF // N_CHIPS, D),
                _wgrad(act, dh2, "wgrad_down").reshape(N_CHIPS, D_FF // N_CHIPS, D)]
    (du_conv, d_conv_w, d_conv_b, d_ln_g, d_ln_b), from_sibling = _conv_bwd(
        dycat, yc, u, conv_w_full, vec["conv_ln_g"], vec["conv_ln_b"], n_ex, lp, plan=_to_sibling_plan(by_owner))
    chip_sums = [_rs_add_halves(g, r, c, "rs_add_" + nm) for g, r, nm in zip(by_owner, from_sibling, early)]
    (du_gla, d_w2, d_gate_b, d_norm_g), exchanged = _gla_bwd(
        dycat, u, states, w2_full, vec["gla_gate_b"], vec["gla_norm_g"], n_ex, lp,
        plan=_chip_exchange_plan(chip_sums))
    mine = 2 * lax.axis_index("x") + lax.axis_index("y")
    halves = [_rs_sum(own, oth, mine, "rs_sum_" + nm) for own, oth, nm in zip(chip_sums, exchanged, early)]
    (dh0, d_mix_g), shared = _in_proj_bwd(du_conv, du_gla, w_in_t[:2 * C_CONV], w_in_t[2 * C_CONV:],
                                          h0, dh1, vec["norm_mix_g"], plan=_share_plan(halves))
    dh0 = dh0.reshape(n_ex, lp, D)
    grad_x = dh0[:, HEAD_ROWS:]

    out = {"grad": {}, "delta": {}, "new_m": {}, "new_v": {}}

    def update(name, g=None, halves=None, transposed=False):
        shape = ws[name].shape
        lay = (lambda a: a.T) if transposed else (lambda a: a)
        w2d, m2d, v2d = lay(shard(ws, name)), lay(shard(ms, name)), lay(shard(vs, name))
        if halves is not None:
            res = _adamw_halves(*halves, c, w2d, m2d, v2d, "adamw_" + name)
        else:
            res = [g, *_adamw(g, w2d, m2d, v2d, "adamw_" + name)]
        for kind, a in zip(("grad", "delta", "new_m", "new_v"), res):
            out[kind][name] = lay(a).reshape(shape)

    update("w_out", halves=(halves[0], shared[0]))
    update("w_ffn_gate", halves=(halves[1], shared[1]), transposed=True)
    update("w_ffn_up", halves=(halves[2], shared[2]), transposed=True)
    update("w_ffn_down", halves=(halves[3], shared[3]))

    small = {"norm_mix_g": d_mix_g, "norm_ffn_g": d_ffn_g, "norm_final_g": d_final_g, "conv_b": d_conv_b,
             "conv_ln_g": d_ln_g, "conv_ln_b": d_ln_b, "gla_gate_b": d_gate_b, "gla_norm_g": d_norm_g}
    small_shapes = {name: ws[name].shape for name, _, _, _ in SMALL_PARTS}
    g_s, d_s, m_s, v_s = _allreduce_small_adamw(_pack_small(small), _pack_small(ws), _pack_small(ms), _pack_small(vs))
    for kind, slab in (("grad", g_s), ("delta", d_s), ("new_m", m_s), ("new_v", v_s)):
        out[kind].update(_unpack_small(slab, small_shapes))

    d_w_in_t = jnp.concatenate([_wgrad(du_conv, hn, "wgrad_in_conv"), _wgrad(du_gla, hn, "wgrad_in_gla")],
                               axis=0)[:D_IN]
    small_sharded = {"meta_tokens": jnp.sum(dh0[:, PAD_ROWS:HEAD_ROWS], axis=0), "conv_w": d_conv_w,
                     "gla_w_gate2": d_w2[:RANK]}
    late = ("w_in", "small")
    by_owner = [d_w_in_t.reshape(N_CHIPS, D_IN // N_CHIPS, D), _pack_small_sharded(small_sharded)]
    from_sibling = _exchange(_to_sibling_plan(by_owner), "rs_late_to_sibling")
    chip_sums = [_rs_add_halves(g, r, c, "rs_add_" + nm) for g, r, nm in zip(by_owner, from_sibling, late)]
    exchanged = _exchange(_chip_exchange_plan(chip_sums), "rs_late_chip_exchange")
    halves = [_rs_sum(own, oth, mine, "rs_sum_" + nm) for own, oth, nm in zip(chip_sums, exchanged, late)]
    shared = _exchange(_share_plan(halves), "rs_late_share")
    update("w_in", halves=(halves[0], shared[0]), transposed=True)
    for name, g in _unpack_small_sharded(_join(halves[1], shared[1], c)).items():
        update(name, g=g)

    return (loss, grad_x, *[out[kind][name] for kind in ("grad", "delta", "new_m", "new_v") for name in WEIGHT_NAMES])
```

```python
import functools
from typing import Any, Callable, NamedTuple, Sequence

import jax
import jax.numpy as jnp
from jax import lax
from jax.experimental import pallas as pl
from jax.experimental.pallas import tpu as pltpu

F32 = jnp.float32
BF16 = jnp.bfloat16
MESH = pl.DeviceIdType.MESH

D = 1024
N_META = 16
C_CONV = 512
CONV_W = 31
GLA_K = 256
GLA_V = 512
N_HEADS = 4
DK = 64
DV = 128
RANK = 16
CHUNK = 64
PAD_ROWS = CHUNK - N_META
HEAD_ROWS = CHUNK
D_IN = 2576
D_IN_PAD = 2688
D_GLA_IN = D_IN_PAD - 2 * C_CONV
D_FF = 2816
RMS_EPS = 1e-6
LN_EPS = 1e-5
GATE_TAU = 16.0
N_CHIPS = 4

ADAM_LR = 0.001
ADAM_B1 = 0.9
ADAM_B2 = 0.999
ADAM_EPS = 1e-08
ADAM_WD = 0.01
ADAM_STEP = 10

V7X_VMEM_BYTES = 64 * 1024 * 1024
VMEM_LIMIT = V7X_VMEM_BYTES - 8 * 1024 * 1024

SLAB_ROWS = 3072
HALF_ROWS = SLAB_ROWS // 2
SMALL_ROWS = 8


def _dot(a, b):
    return jnp.dot(a, b, preferred_element_type=F32)


def _dot_nt(a, b):
    return lax.dot_general(a, b, (((1,), (1,)), ((), ())), preferred_element_type=F32)


def _dot_tn(a, b):
    return lax.dot_general(a, b, (((0,), (0,)), ((), ())), preferred_element_type=F32)


def _sigmoid(x):
    return 1.0 / (1.0 + jnp.exp(-x))


def _const_spec(shape):
    return pl.BlockSpec(shape, lambda *_: (0,) * len(shape), pipeline_mode=pl.Buffered(1))


def _acc_spec(shape):
    return pl.BlockSpec(shape, lambda *_: (0,) * len(shape))


def _params(n_axes):
    return pltpu.CompilerParams(dimension_semantics=("arbitrary",) * n_axes, vmem_limit_bytes=VMEM_LIMIT)


def _row_tile(t, want):
    for r in (want, 384, 192, 128, 64):
        if r <= want and t % r == 0:
            return r
    raise ValueError(f"no row tile for {t}")


class _Plan(NamedTuple):
    arrays: Sequence[Any]
    out_shape: Sequence[Any]
    sems: Sequence[Any]
    make: Callable


def _call(body, *, name, grid, in_specs, out_specs, out_shape, scratch_shapes=(), plan=None):
    n_in, n_out, n_scr = len(in_specs), len(out_specs), len(scratch_shapes)
    if plan is None:
        plan = _Plan([], [], [], lambda ins, outs, sems: (lambda: None, lambda: None))
    nx_in, nx_out = len(plan.arrays), len(plan.out_shape)

    def hosted(*refs):
        ins, xins = refs[:n_in], refs[n_in:n_in + nx_in]
        o0 = n_in + nx_in
        outs, xouts = refs[o0:o0 + n_out], refs[o0 + n_out:o0 + n_out + nx_out]
        s0 = o0 + n_out + nx_out
        scr, sems = refs[s0:s0 + n_scr], refs[s0 + n_scr:]
        ids = [pl.program_id(a) for a in range(len(grid))]
        first = functools.reduce(jnp.logical_and, [i == 0 for i in ids])
        last = functools.reduce(jnp.logical_and, [i == g - 1 for i, g in zip(ids, grid)])
        start, finish = plan.make(xins, xouts, sems)
        pl.when(first)(start)
        body(*ins, *outs, *scr)
        pl.when(last)(finish)

    call = pl.pallas_call(
        hosted, name=name, grid=grid, in_specs=list(in_specs) + [HBM_SPEC] * nx_in,
        out_specs=list(out_specs) + [HBM_SPEC] * nx_out, out_shape=list(out_shape) + list(plan.out_shape),
        scratch_shapes=list(scratch_shapes) + list(plan.sems),
        compiler_params=pltpu.CompilerParams(dimension_semantics=("arbitrary",) * len(grid),
                                             vmem_limit_bytes=VMEM_LIMIT, has_side_effects=nx_in > 0))

    def run(*args):
        res = call(*args, *plan.arrays)
        return res[:n_out], res[n_out:]

    return run


def _in_proj(h0, g_mix, w_in, plan=None):
    t = h0.shape[0]
    r = _row_tile(t, 384)

    def body(h_ref, g_ref, w_ref, u_ref, hn_ref):
        h = h_ref[...]
        rstd = lax.rsqrt(jnp.mean(h * h, axis=-1, keepdims=True) + RMS_EPS)
        hn = (h * rstd * g_ref[...]).astype(BF16)
        hn_ref[...] = hn
        u_ref[...] = _dot(hn, w_ref[...])

    return _call(
        body, name="in_proj", grid=(t // r,),
        in_specs=[pl.BlockSpec((r, D), lambda i: (i, 0)), _const_spec((1, D)), _const_spec((D, D_IN_PAD))],
        out_specs=[pl.BlockSpec((r, D_IN_PAD), lambda i: (i, 0)), pl.BlockSpec((r, D), lambda i: (i, 0))],
        out_shape=[jax.ShapeDtypeStruct((t, D_IN_PAD), F32), jax.ShapeDtypeStruct((t, D), BF16)],
        plan=plan,
    )(h0, g_mix, w_in)


CONV_TILE = 192
CONV_SUB = 32
CONV_LEAD = CONV_SUB - (CONV_W - 1)
SUBLANES = 8


def _shifted_copies(src, dst, r):
    for s in range(1, SUBLANES):
        dst[s - 1] = src[s:s + r + CONV_SUB - SUBLANES, :]


def _shifted_rows(src, shifted, start):
    base, s = SUBLANES * (start // SUBLANES), start % SUBLANES
    if s == 0:
        return src[base:base + CONV_SUB, :]
    return shifted[s - 1, base:base + CONV_SUB, :]


def _conv_fwd(u, conv_w, conv_b, ln_g, ln_b, n_ex, lp, plan=None):
    r = CONV_TILE
    nt = lp // r
    hb = r // CONV_SUB

    def body(cur_ref, prev_ref, w_ref, b_ref, lg_ref, lb_ref, yc_ref, y_ref, glu, glu_sh):
        i = pl.program_id(1)
        cur = cur_ref[...]
        glu[CONV_SUB:CONV_SUB + r, :] = cur[:, :C_CONV] * _sigmoid(cur[:, C_CONV:])
        pv = prev_ref[...]
        halo = pv[:, :C_CONV] * _sigmoid(pv[:, C_CONV:])
        glu[0:CONV_SUB, :] = jnp.where(i > 0, halo, 0.0)
        _shifted_copies(glu, glu_sh, r)
        w = w_ref[...]
        for j in range(r // CONV_SUB):
            r0 = j * CONV_SUB
            acc = jnp.zeros((CONV_SUB, C_CONV), F32) + b_ref[...]
            for k in range(CONV_W):
                acc = acc + w[k:k + 1, :] * _shifted_rows(glu, glu_sh, r0 + CONV_LEAD + k)
            mu = jnp.mean(acc, axis=-1, keepdims=True)
            cen = acc - mu
            var = jnp.mean(cen * cen, axis=-1, keepdims=True)
            out = cen * lax.rsqrt(var + LN_EPS) * lg_ref[...] + lb_ref[...]
            y = out * _sigmoid(out)
            row = i * r + r0 + lax.broadcasted_iota(jnp.int32, (CONV_SUB, 1), 0)
            y = jnp.where(row >= PAD_ROWS, y, 0.0)
            yc_ref[r0:r0 + CONV_SUB, :] = acc
            y_ref[r0:r0 + CONV_SUB, :] = y.astype(BF16)

    t = n_ex * lp
    return _call(
        body, name="conv_fwd", grid=(n_ex, nt),
        in_specs=[pl.BlockSpec((r, 2 * C_CONV), lambda b, i: (b * nt + i, 0)),
                  pl.BlockSpec((CONV_SUB, 2 * C_CONV), lambda b, i: (jnp.maximum((b * nt + i) * hb - 1, 0), 0)),
                  _const_spec((32, C_CONV)), _const_spec((1, C_CONV)), _const_spec((1, C_CONV)), _const_spec((1, C_CONV))],
        out_specs=[pl.BlockSpec((r, C_CONV), lambda b, i: (b * nt + i, 0)),
                   pl.BlockSpec((r, C_CONV), lambda b, i: (b * nt + i, 0))],
        out_shape=[jax.ShapeDtypeStruct((t, C_CONV), F32), jax.ShapeDtypeStruct((t, C_CONV), BF16)],
        scratch_shapes=[pltpu.VMEM((r + CONV_SUB, C_CONV), F32),
                        pltpu.VMEM((SUBLANES - 1, r + CONV_SUB - SUBLANES, C_CONV), F32)],
        plan=plan,
    )(u, u, conv_w, conv_b, ln_g, ln_b)


def _gla_gates(lr, w2, gb, first_chunk):
    z = _dot(lr.astype(BF16), w2) + gb
    a = (jnp.minimum(z, 0.0) - jnp.log(1.0 + jnp.exp(-jnp.abs(z)))) * (1.0 / GATE_TAU)
    row = lax.broadcasted_iota(jnp.int32, (CHUNK, 1), 0)
    live = jnp.logical_or(jnp.logical_not(first_chunk), row >= PAD_ROWS)
    return z, jnp.where(live, a, 0.0), live


def _tri(lower):
    i = lax.broadcasted_iota(jnp.int32, (CHUNK, CHUNK), 0)
    j = lax.broadcasted_iota(jnp.int32, (CHUNK, CHUNK), 1)
    return (i >= j) if lower else (i <= j)


def _gla_fwd(u, w2, gb, ng, n_ex, lp, plan=None):
    nc = lp // CHUNK
    t = n_ex * lp

    def body(qk_ref, v_ref, g_ref, lr_ref, w2_ref, gb_ref, ng_ref, y_ref, st_ref, state):
        n = pl.program_id(0)

        @pl.when(n == 0)
        def _():
            state[...] = jnp.zeros_like(state)

        causal = _tri(True)
        for e in range(n_ex):
            st = state[e]
            st_ref[e] = st
            qk = qk_ref[e]
            q, k = qk[:, :GLA_K], qk[:, GLA_K:]
            _, a, _ = _gla_gates(lr_ref[e], w2_ref[...], gb_ref[...], n == 0)
            b = jnp.dot(causal.astype(F32), a, preferred_element_type=F32, precision=lax.Precision.HIGHEST)
            bl = b[CHUNK - 1:CHUNK, :]
            q_in = (q * (DK ** -0.5) * jnp.exp(b)).astype(BF16)
            k_in = (k * jnp.exp(-b)).astype(BF16)
            k_dec = (k * jnp.exp(bl - b)).astype(BF16)
            decay = jnp.exp(bl)
            v = v_ref[e]
            g = g_ref[e]
            st_b = st.astype(BF16)
            ys, new = [], []
            for h in range(N_HEADS):
                ks = slice(h * DK, (h + 1) * DK)
                vs = slice(h * DV, (h + 1) * DV)
                vh = v[:, vs].astype(BF16)
                s = jnp.where(causal, _dot_nt(q_in[:, ks], k_in[:, ks]), 0.0)
                o = _dot(s.astype(BF16), vh) + _dot_nt(q_in[:, ks], st_b[:, ks])
                new.append(decay[:, ks] * st[:, ks] + _dot_tn(vh, k_dec[:, ks]))
                rstd = lax.rsqrt(jnp.mean(o * o, axis=-1, keepdims=True) + RMS_EPS)
                gh = g[:, vs]
                ys.append(o * rstd * ng_ref[...] * (gh * _sigmoid(gh)))
            state[e] = jnp.concatenate(new, axis=1)
            y_ref[e] = jnp.concatenate(ys, axis=1).astype(BF16)

    u3 = u.reshape(n_ex, lp, D_IN_PAD)
    blk = lambda w, col: pl.BlockSpec((n_ex, CHUNK, w), lambda n: (0, n, col))
    (y, states), extra = _call(
        body, name="gla_fwd", grid=(nc,),
        in_specs=[blk(2 * GLA_K, 2), blk(GLA_V, 3), blk(GLA_V, 4), blk(128, 20),
                  _const_spec((128, GLA_K)), _const_spec((1, GLA_K)), _const_spec((1, DV))],
        out_specs=[blk(GLA_V, 0), pl.BlockSpec((n_ex, DV, GLA_K), lambda n: (0, n, 0))],
        out_shape=[jax.ShapeDtypeStruct((n_ex, lp, GLA_V), BF16),
                   jax.ShapeDtypeStruct((n_ex, nc * DV, GLA_K), F32)],
        scratch_shapes=[pltpu.VMEM((n_ex, DV, GLA_K), F32)],
        plan=plan,
    )(u3, u3, u3, u3, w2, gb, ng)
    return (y.reshape(t, GLA_V), states), extra


FFN_TILE = 192


def _mix_out_ffn_up(h0, y_conv, y_gla, w_out, g_ffn, w_gate, w_up):
    t = h0.shape[0]
    r = _row_tile(t, FFN_TILE)

    def body(h0_ref, yc_ref, yg_ref, wo_ref, g_ref, wg_ref, wu_ref, h1_ref, hn_ref, gate_ref, up_ref, act_ref):
        h1 = h0_ref[...] + _dot(yc_ref[...], wo_ref[0:C_CONV, :]) + _dot(yg_ref[...], wo_ref[C_CONV:D, :])
        h1_ref[...] = h1
        rstd = lax.rsqrt(jnp.mean(h1 * h1, axis=-1, keepdims=True) + RMS_EPS)
        hn = (h1 * rstd * g_ref[...]).astype(BF16)
        hn_ref[...] = hn
        gate = _dot(hn, wg_ref[...])
        up = _dot(hn, wu_ref[...])
        gate_ref[...] = gate
        up_ref[...] = up
        act_ref[...] = (gate * _sigmoid(gate) * up).astype(BF16)

    rows = lambda w: pl.BlockSpec((r, w), lambda i: (i, 0))
    return pl.pallas_call(
        body, name="mix_out_ffn_up", grid=(t // r,),
        in_specs=[rows(D), rows(C_CONV), rows(GLA_V), _const_spec((D, D)), _const_spec((1, D)),
                  _const_spec((D, D_FF)), _const_spec((D, D_FF))],
        out_specs=[rows(D), rows(D), rows(D_FF), rows(D_FF), rows(D_FF)],
        out_shape=[jax.ShapeDtypeStruct((t, D), F32), jax.ShapeDtypeStruct((t, D), BF16),
                   jax.ShapeDtypeStruct((t, D_FF), F32), jax.ShapeDtypeStruct((t, D_FF), F32),
                   jax.ShapeDtypeStruct((t, D_FF), BF16)],
        compiler_params=_params(1),
    )(h0, y_conv, y_gla, w_out, g_ffn, w_gate, w_up)


def _ffn_down_loss(act, w_down, h1, target, g_final, row_mask):
    t = h1.shape[0]
    r = _row_tile(t, 384)

    def body(act_ref, wd_ref, h1_ref, tgt_ref, gf_ref, mask_ref, dh2_ref, loss_ref, dgf_ref):
        @pl.when(pl.program_id(0) == 0)
        def _():
            loss_ref[...] = jnp.zeros_like(loss_ref)
            dgf_ref[...] = jnp.zeros_like(dgf_ref)

        h2 = h1_ref[...] + _dot(act_ref[...], wd_ref[...])
        rstd = lax.rsqrt(jnp.mean(h2 * h2, axis=-1, keepdims=True) + RMS_EPS)
        nrm = h2 * rstd
        gf = gf_ref[...]
        err = (nrm * gf - tgt_ref[...]) * mask_ref[...]
        loss_ref[...] += jnp.sum(err * err) * (0.5 / D)
        dy = err * (1.0 / D)
        dgf_ref[...] += jnp.sum(dy * nrm, axis=0, keepdims=True)
        dn = dy * gf
        dh2_ref[...] = rstd * (dn - nrm * jnp.mean(dn * nrm, axis=-1, keepdims=True))

    rows = lambda w: pl.BlockSpec((r, w), lambda i: (i, 0))
    return pl.pallas_call(
        body, name="ffn_down_loss", grid=(t // r,),
        in_specs=[rows(D_FF), _const_spec((D_FF, D)), rows(D), rows(D), _const_spec((1, D)), rows(1)],
        out_specs=[rows(D), _acc_spec((1, 128)), _acc_spec((1, D))],
        out_shape=[jax.ShapeDtypeStruct((t, D), F32), jax.ShapeDtypeStruct((1, 128), F32),
                   jax.ShapeDtypeStruct((1, D), F32)],
        compiler_params=_params(1),
    )(act, w_down, h1, target, g_final, row_mask)


def _ffn_bwd(dh2, gate, up, h1, w_down_t, w_gate_t, w_up_t, w_out_t, g_ffn):
    t = h1.shape[0]
    r = _row_tile(t, FFN_TILE)

    def body(dh2_ref, gate_ref, up_ref, h1_ref, wd_ref, wg_ref, wu_ref, wo_ref, g_ref,
             dgate_ref, dup_ref, dh1_ref, dycat_ref, dg_ref):
        @pl.when(pl.program_id(0) == 0)
        def _():
            dg_ref[...] = jnp.zeros_like(dg_ref)

        dh2 = dh2_ref[...]
        dact = _dot(dh2.astype(BF16), wd_ref[...])
        gate = gate_ref[...]
        sg = _sigmoid(gate)
        dgate = (dact * up_ref[...] * (sg * (1.0 + gate * (1.0 - sg)))).astype(BF16)
        dup = (dact * (gate * sg)).astype(BF16)
        dgate_ref[...] = dgate
        dup_ref[...] = dup
        dhn = _dot(dgate, wg_ref[...]) + _dot(dup, wu_ref[...])
        h1 = h1_ref[...]
        rstd = lax.rsqrt(jnp.mean(h1 * h1, axis=-1, keepdims=True) + RMS_EPS)
        nrm = h1 * rstd
        dg_ref[...] += jnp.sum(dhn * nrm, axis=0, keepdims=True)
        dn = dhn * g_ref[...]
        dh1 = dh2 + rstd * (dn - nrm * jnp.mean(dn * nrm, axis=-1, keepdims=True))
        dh1_ref[...] = dh1
        dycat_ref[...] = _dot(dh1.astype(BF16), wo_ref[...])

    rows = lambda w: pl.BlockSpec((r, w), lambda i: (i, 0))
    return pl.pallas_call(
        body, name="ffn_bwd", grid=(t // r,),
        in_specs=[rows(D), rows(D_FF), rows(D_FF), rows(D), _const_spec((D, D_FF)), _const_spec((D_FF, D)),
                  _const_spec((D_FF, D)), _const_spec((D, D)), _const_spec((1, D))],
        out_specs=[rows(D_FF), rows(D_FF), rows(D), rows(D), _acc_spec((1, D))],
        out_shape=[jax.ShapeDtypeStruct((t, D_FF), BF16), jax.ShapeDtypeStruct((t, D_FF), BF16),
                   jax.ShapeDtypeStruct((t, D), F32), jax.ShapeDtypeStruct((t, D), F32),
                   jax.ShapeDtypeStruct((1, D), F32)],
        compiler_params=_params(1),
    )(dh2, gate, up, h1, w_down_t, w_gate_t, w_up_t, w_out_t, g_ffn)


def _conv_bwd(dycat, yc, u, conv_w, ln_g, ln_b, n_ex, lp, plan=None):
    r = CONV_TILE
    nt = lp // r
    hb = r // CONV_SUB
    nsub = r // CONV_SUB

    def ln_bwd(dy, yc_rows, live, lg, lb):
        mu = jnp.mean(yc_rows, axis=-1, keepdims=True)
        cen = yc_rows - mu
        rs = lax.rsqrt(jnp.mean(cen * cen, axis=-1, keepdims=True) + LN_EPS)
        yn = cen * rs
        out = yn * lg + lb
        so = _sigmoid(out)
        dout = jnp.where(live, dy * (so * (1.0 + out * (1.0 - so))), 0.0)
        dyn = dout * lg
        dyc = rs * (dyn - jnp.mean(dyn, axis=-1, keepdims=True) - yn * jnp.mean(dyn * yn, axis=-1, keepdims=True))
        return dyc, dout, yn

    def body(dy_ref, dyn_ref, yc_ref, ycn_ref, cur_ref, prev_ref, w_ref, lg_ref, lb_ref,
             du_ref, dw_ref, db_ref, dlg_ref, dlb_ref, glu, dycs, dwacc, glu_sh, dycs_sh):
        b = pl.program_id(0)
        i = pl.program_id(1)
        first = jnp.logical_and(b == 0, i == 0)

        @pl.when(first)
        def _():
            dwacc[...] = jnp.zeros_like(dwacc)
            db_ref[...] = jnp.zeros_like(db_ref)
            dlg_ref[...] = jnp.zeros_like(dlg_ref)
            dlb_ref[...] = jnp.zeros_like(dlb_ref)

        lg, lb = lg_ref[...], lb_ref[...]
        cur = cur_ref[...]
        sig = _sigmoid(cur[:, C_CONV:])
        glu[CONV_SUB:CONV_SUB + r, :] = cur[:, :C_CONV] * sig
        pv = prev_ref[...]
        glu[0:CONV_SUB, :] = jnp.where(i > 0, pv[:, :C_CONV] * _sigmoid(pv[:, C_CONV:]), 0.0)

        row = i * r + lax.broadcasted_iota(jnp.int32, (r, 1), 0)
        dyc, dout, yn = ln_bwd(dy_ref[...], yc_ref[...], row >= PAD_ROWS, lg, lb)
        dycs[0:r, :] = dyc
        dycn, _, _ = ln_bwd(dyn_ref[...], ycn_ref[...], i < nt - 1, lg, lb)
        dycs[r:r + CONV_SUB, :] = dycn
        db_ref[...] += jnp.sum(dyc, axis=0, keepdims=True)
        dlg_ref[...] += jnp.sum(dout * yn, axis=0, keepdims=True)
        dlb_ref[...] += jnp.sum(dout, axis=0, keepdims=True)

        _shifted_copies(glu, glu_sh, r)
        _shifted_copies(dycs, dycs_sh, r)
        w = w_ref[...]
        for j in range(nsub):
            r0 = j * CONV_SUB
            dblk = dycs[r0:r0 + CONV_SUB, :]
            dglu = jnp.zeros((CONV_SUB, C_CONV), F32)
            for k in range(CONV_W):
                dglu = dglu + w[k:k + 1, :] * _shifted_rows(dycs, dycs_sh, r0 + (CONV_W - 1) - k)
                prod = dblk * _shifted_rows(glu, glu_sh, r0 + CONV_LEAD + k)
                dwacc[k] += prod.reshape(CONV_SUB // SUBLANES, SUBLANES, C_CONV).sum(axis=0)
            sg = sig[r0:r0 + CONV_SUB, :]
            cv = cur[r0:r0 + CONV_SUB, :C_CONV]
            du_ref[r0:r0 + CONV_SUB, :C_CONV] = (dglu * sg).astype(BF16)
            du_ref[r0:r0 + CONV_SUB, C_CONV:] = (dglu * cv * sg * (1.0 - sg)).astype(BF16)

        @pl.when(jnp.logical_and(b == n_ex - 1, i == nt - 1))
        def _():
            dw_ref[...] = jnp.sum(dwacc[...], axis=1)

    t = n_ex * lp
    cur_rows = lambda w, col: pl.BlockSpec((r, w), lambda b, i: (b * nt + i, col))
    nxt_rows = lambda w, col: pl.BlockSpec(
        (CONV_SUB, w), lambda b, i: (jnp.minimum((b * nt + i + 1) * hb, n_ex * nt * hb - 1), col))
    return _call(
        body, name="conv_bwd", grid=(n_ex, nt),
        in_specs=[cur_rows(C_CONV, 0), nxt_rows(C_CONV, 0), cur_rows(C_CONV, 0), nxt_rows(C_CONV, 0),
                  cur_rows(2 * C_CONV, 0),
                  pl.BlockSpec((CONV_SUB, 2 * C_CONV), lambda b, i: (jnp.maximum((b * nt + i) * hb - 1, 0), 0)),
                  _const_spec((32, C_CONV)), _const_spec((1, C_CONV)), _const_spec((1, C_CONV))],
        out_specs=[cur_rows(2 * C_CONV, 0), _acc_spec((32, C_CONV)), _acc_spec((1, C_CONV)),
                   _acc_spec((1, C_CONV)), _acc_spec((1, C_CONV))],
        out_shape=[jax.ShapeDtypeStruct((t, 2 * C_CONV), BF16), jax.ShapeDtypeStruct((32, C_CONV), F32),
                   jax.ShapeDtypeStruct((1, C_CONV), F32), jax.ShapeDtypeStruct((1, C_CONV), F32),
                   jax.ShapeDtypeStruct((1, C_CONV), F32)],
        scratch_shapes=[pltpu.VMEM((r + CONV_SUB, C_CONV), F32), pltpu.VMEM((r + CONV_SUB, C_CONV), F32),
                        pltpu.VMEM((32, 8, C_CONV), F32),
                        pltpu.VMEM((SUBLANES - 1, r + CONV_SUB - SUBLANES, C_CONV), F32),
                        pltpu.VMEM((SUBLANES - 1, r + CONV_SUB - SUBLANES, C_CONV), F32)],
        plan=plan,
    )(dycat, dycat, yc, yc, u, u, conv_w, ln_g, ln_b)


def _gla_bwd(dycat, u, states, w2, gb, ng, n_ex, lp, plan=None):
    nc = lp // CHUNK
    t = n_ex * lp

    def body(dy_ref, qk_ref, v_ref, g_ref, lr_ref, st_ref, w2_ref, gb_ref, ng_ref,
             du_ref, dw2_ref, dgb_ref, dng_ref, dstate):
        n = pl.program_id(0)
        chunk = nc - 1 - n

        @pl.when(n == 0)
        def _():
            dw2_ref[...] = jnp.zeros_like(dw2_ref)
            dgb_ref[...] = jnp.zeros_like(dgb_ref)
            dng_ref[...] = jnp.zeros_like(dng_ref)
            dstate[...] = jnp.zeros_like(dstate)

        for e in range(n_ex):
            one_example(e, chunk, dy_ref, qk_ref, v_ref, g_ref, lr_ref, st_ref, w2_ref, gb_ref, ng_ref,
                        du_ref, dw2_ref, dgb_ref, dng_ref, dstate)

    def one_example(e, chunk, dy_ref, qk_ref, v_ref, g_ref, lr_ref, st_ref, w2_ref, gb_ref, ng_ref,
                    du_ref, dw2_ref, dgb_ref, dng_ref, dstate):
        dy_ref, qk_ref, v_ref, g_ref, lr_ref, st_ref = (r.at[e] for r in (dy_ref, qk_ref, v_ref, g_ref, lr_ref, st_ref))
        du_ref, dstate = du_ref.at[e], dstate.at[e]
        qk = qk_ref[...]
        q, k = qk[:, :GLA_K], qk[:, GLA_K:]
        lr = lr_ref[...]
        z, a, live = _gla_gates(lr, w2_ref[...], gb_ref[...], chunk == 0)
        causal = _tri(True)
        b = jnp.dot(causal.astype(F32), a, preferred_element_type=F32, precision=lax.Precision.HIGHEST)
        bl = b[CHUNK - 1:CHUNK, :]
        e_pos, e_neg, e_dec = jnp.exp(b), jnp.exp(-b), jnp.exp(bl - b)
        q_f = q * (DK ** -0.5) * e_pos
        k_f = k * e_neg
        kd_f = k * e_dec
        q_in, k_in, k_dec = q_f.astype(BF16), k_f.astype(BF16), kd_f.astype(BF16)
        decay = jnp.exp(bl)
        v = v_ref[...]
        g = g_ref[...]
        dy = dy_ref[...]
        ngv = ng_ref[...]
        st = st_ref[...]
        st_b = st.astype(BF16)
        dst = dstate[...]
        dst_b = dst.astype(BF16)
        dqs, dks, dvs, dgs, dbs, dbls, new_dst = [], [], [], [], [], [], []
        dng = jnp.zeros((1, DV), F32)
        for h in range(N_HEADS):
            ks = slice(h * DK, (h + 1) * DK)
            vs = slice(h * DV, (h + 1) * DV)
            qh, kh, kdh = q_in[:, ks], k_in[:, ks], k_dec[:, ks]
            vh = v[:, vs].astype(BF16)
            s = jnp.where(causal, _dot_nt(qh, kh), 0.0).astype(BF16)
            o = _dot(s, vh) + _dot_nt(qh, st_b[:, ks])
            rstd = lax.rsqrt(jnp.mean(o * o, axis=-1, keepdims=True) + RMS_EPS)
            nrm = o * rstd
            gh = g[:, vs]
            sg = _sigmoid(gh)
            dyh = dy[:, vs]
            dgs.append(dyh * nrm * ngv * (sg * (1.0 + gh * (1.0 - sg))))
            dt = dyh * (gh * sg)
            dng = dng + jnp.sum(dt * nrm, axis=0, keepdims=True)
            dn = dt * ngv
            do = (rstd * (dn - nrm * jnp.mean(dn * nrm, axis=-1, keepdims=True))).astype(BF16)
            da = jnp.where(causal, _dot_nt(do, vh), 0.0).astype(BF16)
            dvs.append(_dot_tn(s, do) + _dot_nt(kdh, dst_b[:, ks]))
            dq_in = _dot(da, kh) + _dot(do, st_b[:, ks])
            dk_in = _dot_tn(da, qh)
            dk_dec = _dot(vh, dst_b[:, ks])
            new_dst.append(_dot_tn(do, qh) + decay[:, ks] * dst[:, ks])
            dbls.append(jnp.sum(dk_dec * kd_f[:, ks], axis=0, keepdims=True)
                        + decay[:, ks] * jnp.sum(dst[:, ks] * st[:, ks], axis=0, keepdims=True))
            dqs.append(dq_in * (DK ** -0.5) * e_pos[:, ks])
            dks.append(dk_in * e_neg[:, ks] + dk_dec * e_dec[:, ks])
            dbs.append(dq_in * q_f[:, ks] - dk_in * k_f[:, ks] - dk_dec * kd_f[:, ks])
        dstate[...] = jnp.concatenate(new_dst, axis=1)
        row = lax.broadcasted_iota(jnp.int32, (CHUNK, 1), 0)
        db = jnp.concatenate(dbs, axis=1) + jnp.where(row == CHUNK - 1, jnp.concatenate(dbls, axis=1), 0.0)
        da_log = jnp.dot(_tri(False).astype(F32), db, preferred_element_type=F32, precision=lax.Precision.HIGHEST)
        dz = jnp.where(live, da_log * (1.0 - _sigmoid(z)) * (1.0 / GATE_TAU), 0.0)
        dz_b = dz.astype(BF16)
        du_ref[:, 0:GLA_K] = jnp.concatenate(dqs, axis=1).astype(BF16)
        du_ref[:, GLA_K:2 * GLA_K] = jnp.concatenate(dks, axis=1).astype(BF16)
        du_ref[:, 2 * GLA_K:2 * GLA_K + GLA_V] = jnp.concatenate(dvs, axis=1).astype(BF16)
        du_ref[:, 2 * GLA_K + GLA_V:2 * GLA_K + 2 * GLA_V] = jnp.concatenate(dgs, axis=1).astype(BF16)
        du_ref[:, 2 * GLA_K + 2 * GLA_V:] = _dot_nt(dz_b, w2_ref[...]).astype(BF16)
        dw2_ref[...] += _dot_tn(lr.astype(BF16), dz_b)
        dgb_ref[...] += jnp.sum(dz, axis=0, keepdims=True)
        dng_ref[...] += dng

    u3 = u.reshape(n_ex, lp, D_IN_PAD)
    rev = lambda w, col: pl.BlockSpec((n_ex, CHUNK, w), lambda n: (0, nc - 1 - n, col))
    (du, d_w2, d_gb, d_ng), extra = _call(
        body, name="gla_bwd", grid=(nc,),
        in_specs=[rev(GLA_V, 1), rev(2 * GLA_K, 2), rev(GLA_V, 3), rev(GLA_V, 4), rev(128, 20),
                  pl.BlockSpec((n_ex, DV, GLA_K), lambda n: (0, nc - 1 - n, 0)),
                  _const_spec((128, GLA_K)), _const_spec((1, GLA_K)), _const_spec((1, DV))],
        out_specs=[rev(D_GLA_IN, 0), _acc_spec((128, GLA_K)), _acc_spec((1, GLA_K)), _acc_spec((1, DV))],
        out_shape=[jax.ShapeDtypeStruct((n_ex, lp, D_GLA_IN), BF16), jax.ShapeDtypeStruct((128, GLA_K), F32),
                   jax.ShapeDtypeStruct((1, GLA_K), F32), jax.ShapeDtypeStruct((1, DV), F32)],
        scratch_shapes=[pltpu.VMEM((n_ex, DV, GLA_K), F32)],
        plan=plan,
    )(dycat.reshape(n_ex, lp, D), u3, u3, u3, u3, states, w2, gb, ng)
    return (du.reshape(t, D_GLA_IN), d_w2, d_gb, d_ng), extra


def _in_proj_bwd(du_conv, du_gla, w_in_t_conv, w_in_t_gla, h0, dh1, g_mix, plan=None):
    t = h0.shape[0]
    r = _row_tile(t, 384)

    def body(dc_ref, dg_ref, wc_ref, wg_ref, h_ref, dh1_ref, g_ref, dh0_ref, dgm_ref):
        @pl.when(pl.program_id(0) == 0)
        def _():
            dgm_ref[...] = jnp.zeros_like(dgm_ref)

        dhn = _dot(dc_ref[...], wc_ref[...]) + _dot(dg_ref[...], wg_ref[...])
        h = h_ref[...]
        rstd = lax.rsqrt(jnp.mean(h * h, axis=-1, keepdims=True) + RMS_EPS)
        nrm = h * rstd
        dgm_ref[...] += jnp.sum(dhn * nrm, axis=0, keepdims=True)
        dn = dhn * g_ref[...]
        dh0_ref[...] = dh1_ref[...] + rstd * (dn - nrm * jnp.mean(dn * nrm, axis=-1, keepdims=True))

    rows = lambda w: pl.BlockSpec((r, w), lambda i: (i, 0))
    return _call(
        body, name="in_proj_bwd", grid=(t // r,),
        in_specs=[rows(2 * C_CONV), rows(D_GLA_IN), _const_spec((2 * C_CONV, D)), _const_spec((D_GLA_IN, D)),
                  rows(D), rows(D), _const_spec((1, D))],
        out_specs=[rows(D), _acc_spec((1, D))],
        out_shape=[jax.ShapeDtypeStruct((t, D), F32), jax.ShapeDtypeStruct((1, D), F32)],
        plan=plan,
    )(du_conv, du_gla, w_in_t_conv, w_in_t_gla, h0, dh1, g_mix)


def _wgrad(x, dy, name):
    t, m = x.shape
    n = dy.shape[1]
    tk = t // 3 if t % (3 * 128) == 0 else _row_tile(t, 384)
    tm = m if m <= D_GLA_IN else m // 2
    tn = n

    def body(x_ref, dy_ref, o_ref):
        @pl.when(pl.program_id(2) == 0)
        def _():
            o_ref[...] = jnp.zeros_like(o_ref)

        o_ref[...] += _dot_tn(x_ref[...].astype(BF16), dy_ref[...].astype(BF16))

    return pl.pallas_call(
        body, name=name, grid=(m // tm, n // tn, t // tk),
        in_specs=[pl.BlockSpec((tk, tm), lambda i, j, k: (k, i)), pl.BlockSpec((tk, tn), lambda i, j, k: (k, j))],
        out_specs=pl.BlockSpec((tm, tn), lambda i, j, k: (i, j)),
        out_shape=jax.ShapeDtypeStruct((m, n), F32),
        compiler_params=_params(3),
    )(x, dy)


def _mesh_pos():
    return lax.axis_index("x"), lax.axis_index("y"), lax.axis_index("c")


def _other_chips(x, y):
    return [(1 - x, y), (x, 1 - y), (1 - x, 1 - y)]


HBM_SPEC = pl.BlockSpec(memory_space=pltpu.HBM)


def _gather_shards(shards):
    n = len(shards)

    def body(*refs):
        ins, outs = refs[:n], refs[n:2 * n]
        send_sems, recv_sems, local_sems = refs[2 * n:]
        x, y, c = _mesh_pos()
        mine = 2 * x + y
        chips = _other_chips(x, y)
        local = [pltpu.make_async_copy(ins[a], outs[a].at[mine], local_sems.at[a]) for a in range(n)]
        for cp in local:
            cp.start()

        def remote(a, k, block):
            px, py = chips[k]
            return pltpu.make_async_remote_copy(
                src_ref=ins[a], dst_ref=outs[a].at[block], send_sem=send_sems.at[3 * a + k],
                recv_sem=recv_sems.at[3 * a + k], device_id=(px, py, c), device_id_type=MESH)

        sends = [remote(a, k, mine) for a in range(n) for k in range(3)]
        for cp in sends:
            cp.start()
        for a in range(n):
            for k, (px, py) in enumerate(chips):
                remote(a, k, 2 * px + py).wait_recv()
        for cp in sends:
            cp.wait_send()
        for cp in local:
            cp.wait()

    return pl.pallas_call(
        body, name="gather_shards",
        in_specs=[HBM_SPEC] * n, out_specs=[HBM_SPEC] * n,
        out_shape=[jax.ShapeDtypeStruct((N_CHIPS,) + s.shape, s.dtype) for s in shards],
        scratch_shapes=[pltpu.SemaphoreType.DMA((3 * n,)), pltpu.SemaphoreType.DMA((3 * n,)),
                        pltpu.SemaphoreType.DMA((n,))],
        compiler_params=pltpu.CompilerParams(has_side_effects=True),
    )(*shards)


def _send_half_to_sibling(g2):
    def body(g_ref, recv_ref, send_sem, recv_sem):
        x, y, c = _mesh_pos()
        cp = pltpu.make_async_remote_copy(
            src_ref=g_ref.at[1 - c], dst_ref=recv_ref, send_sem=send_sem, recv_sem=recv_sem,
            device_id=(x, y, 1 - c), device_id_type=MESH)
        cp.start()
        cp.wait()

    return pl.pallas_call(
        body, name="rs_to_sibling", in_specs=[HBM_SPEC], out_specs=HBM_SPEC,
        out_shape=jax.ShapeDtypeStruct(g2.shape[1:], g2.dtype),
        scratch_shapes=[pltpu.SemaphoreType.DMA(()), pltpu.SemaphoreType.DMA(())],
        compiler_params=pltpu.CompilerParams(has_side_effects=True),
    )(g2)


def _add_own_half(g2, recv, c):
    rows = N_CHIPS * HALF_ROWS
    tr = 512
    g2f = g2.reshape(2, rows, D)
    recvf = recv.reshape(rows, D)

    def body(c_ref, a_ref, b_ref, o_ref):
        o_ref[...] = a_ref[0] + b_ref[...]

    out = pl.pallas_call(
        body, name="rs_add_halves",
        grid_spec=pltpu.PrefetchScalarGridSpec(
            num_scalar_prefetch=1, grid=(rows // tr,),
            in_specs=[pl.BlockSpec((1, tr, D), lambda i, s: (s[0], i, 0)), pl.BlockSpec((tr, D), lambda i, s: (i, 0))],
            out_specs=pl.BlockSpec((tr, D), lambda i, s: (i, 0))),
        out_shape=jax.ShapeDtypeStruct((rows, D), F32),
        compiler_params=_params(1),
    )(jnp.reshape(c, (1,)).astype(jnp.int32), g2f, recvf)
    return out.reshape(N_CHIPS, HALF_ROWS, D)


def _exchange_chip_sums(p):
    def body(p_ref, out_ref, send_sems, recv_sems, local_sem):
        x, y, c = _mesh_pos()
        mine = 2 * x + y
        chips = _other_chips(x, y)
        local = pltpu.make_async_copy(p_ref.at[mine], out_ref.at[mine], local_sem)
        local.start()

        def remote(k, src_block, dst_block):
            px, py = chips[k]
            return pltpu.make_async_remote_copy(
                src_ref=p_ref.at[src_block], dst_ref=out_ref.at[dst_block], send_sem=send_sems.at[k],
                recv_sem=recv_sems.at[k], device_id=(px, py, c), device_id_type=MESH)

        sends = [remote(k, 2 * px + py, mine) for k, (px, py) in enumerate(chips)]
        for cp in sends:
            cp.start()
        for k, (px, py) in enumerate(chips):
            remote(k, mine, 2 * px + py).wait_recv()
        for cp in sends:
            cp.wait_send()
        local.wait()

    return pl.pallas_call(
        body, name="rs_chip_exchange", in_specs=[HBM_SPEC], out_specs=HBM_SPEC,
        out_shape=jax.ShapeDtypeStruct(p.shape, p.dtype),
        scratch_shapes=[pltpu.SemaphoreType.DMA((3,)), pltpu.SemaphoreType.DMA((3,)), pltpu.SemaphoreType.DMA(())],
        compiler_params=pltpu.CompilerParams(has_side_effects=True),
    )(p)


def _sum_chips(parts):
    tr = 512

    def body(p_ref, o_ref):
        o_ref[...] = ((p_ref[0] + p_ref[1]) + p_ref[2]) + p_ref[3]

    return pl.pallas_call(
        body, name="rs_sum_chips", grid=(HALF_ROWS // tr,),
        in_specs=[pl.BlockSpec((N_CHIPS, tr, D), lambda i: (0, i, 0))],
        out_specs=pl.BlockSpec((tr, D), lambda i: (i, 0)),
        out_shape=jax.ShapeDtypeStruct((HALF_ROWS, D), F32),
        compiler_params=_params(1),
    )(parts)


def _share_with_sibling(half):
    def body(h_ref, out_ref, send_sem, recv_sem, local_sem):
        x, y, c = _mesh_pos()
        local = pltpu.make_async_copy(h_ref, out_ref.at[c], local_sem)
        local.start()
        cp = pltpu.make_async_remote_copy(
            src_ref=h_ref, dst_ref=out_ref.at[c], send_sem=send_sem, recv_sem=recv_sem,
            device_id=(x, y, 1 - c), device_id_type=MESH)
        cp.start()
        pltpu.make_async_remote_copy(
            src_ref=h_ref, dst_ref=out_ref.at[1 - c], send_sem=send_sem, recv_sem=recv_sem,
            device_id=(x, y, 1 - c), device_id_type=MESH).wait_recv()
        cp.wait_send()
        local.wait()

    return pl.pallas_call(
        body, name="rs_share_sibling", in_specs=[HBM_SPEC], out_specs=HBM_SPEC,
        out_shape=jax.ShapeDtypeStruct((2,) + half.shape, half.dtype),
        scratch_shapes=[pltpu.SemaphoreType.DMA(()), pltpu.SemaphoreType.DMA(()), pltpu.SemaphoreType.DMA(())],
        compiler_params=pltpu.CompilerParams(has_side_effects=True),
    )(half)


def _adam_update(g, w, m, v):
    m2 = ADAM_B1 * m + (1.0 - ADAM_B1) * g
    v2 = ADAM_B2 * v + (1.0 - ADAM_B2) * (g * g)
    m_hat = m2 / (1.0 - ADAM_B1 ** ADAM_STEP)
    v_hat = v2 / (1.0 - ADAM_B2 ** ADAM_STEP)
    delta = -ADAM_LR * (m_hat / (jnp.sqrt(v_hat) + ADAM_EPS) + ADAM_WD * w)
    return delta, m2, v2


def _adamw_slab(g, w, m, v):
    rows = g.shape[0]
    tr = 256

    def body(g_ref, w_ref, m_ref, v_ref, d_ref, m2_ref, v2_ref):
        d_ref[...], m2_ref[...], v2_ref[...] = _adam_update(g_ref[...], w_ref[...], m_ref[...], v_ref[...])

    spec = pl.BlockSpec((tr, D), lambda i: (i, 0))
    return pl.pallas_call(
        body, name="adamw_slab", grid=(rows // tr,), in_specs=[spec] * 4, out_specs=[spec] * 3,
        out_shape=[jax.ShapeDtypeStruct((rows, D), F32)] * 3,
        compiler_params=_params(1),
    )(g, w, m, v)


def _allreduce_small_adamw(part, w, m, v):
    def body(p_ref, w_ref, m_ref, v_ref, g_ref, d_ref, m2_ref, v2_ref, slots, send_sems, recv_sems):
        x, y, c = _mesh_pos()
        mine = 4 * x + 2 * y + c
        peers = [(px, py, pc) for px in (x, 1 - x) for py in (y, 1 - y) for pc in (c, 1 - c)][1:]

        def remote(k, slot):
            return pltpu.make_async_remote_copy(
                src_ref=p_ref, dst_ref=slots.at[slot], send_sem=send_sems.at[k], recv_sem=recv_sems.at[k],
                device_id=peers[k], device_id_type=MESH)

        sends = [remote(k, mine) for k in range(7)]
        for cp in sends:
            cp.start()
        slots[mine] = p_ref[...]
        for k, (px, py, pc) in enumerate(peers):
            remote(k, 4 * px + 2 * py + pc).wait_recv()
        for cp in sends:
            cp.wait_send()
        g = slots[0]
        for d in range(1, 8):
            g = g + slots[d]
        g_ref[...] = g
        d_ref[...], m2_ref[...], v2_ref[...] = _adam_update(g, w_ref[...], m_ref[...], v_ref[...])

    vm = pl.BlockSpec(memory_space=pltpu.VMEM)
    shape = jax.ShapeDtypeStruct(part.shape, F32)
    return pl.pallas_call(
        body, name="small_allreduce_adamw", in_specs=[vm] * 4, out_specs=[vm] * 4, out_shape=[shape] * 4,
        scratch_shapes=[pltpu.VMEM((8,) + part.shape, F32), pltpu.SemaphoreType.DMA((7,)),
                        pltpu.SemaphoreType.DMA((7,))],
        compiler_params=pltpu.CompilerParams(has_side_effects=True),
    )(part, w, m, v)


def _half(ref, c, axis):
    n = ref.shape[axis] // 2
    return ref.at[(slice(None),) * axis + (pl.ds(c * n, n),)]


def _remote(src, dst, send_sem, recv_sem, device):
    return pltpu.make_async_remote_copy(src_ref=src, dst_ref=dst, send_sem=send_sem, recv_sem=recv_sem,
                                        device_id=device, device_id_type=MESH)


def _gather_weights(split, axes, whole):
    ns, n = len(split), len(split) + len(whole)

    def body(*refs):
        ins, outs = refs[:n], refs[n:2 * n]
        ici_send, ici_recv, d2d_send, d2d_recv, local_sems = refs[2 * n:]
        x, y, c = _mesh_pos()
        mine = 2 * x + y
        chips = _other_chips(x, y)
        local = [pltpu.make_async_copy(ins[a], outs[a].at[mine], local_sems.at[a]) for a in range(n)]
        for cp in local:
            cp.start()

        def ici(a, k, block):
            px, py = chips[k]
            src, dst = ins[a], outs[a].at[block]
            if a < ns:
                src, dst = _half(src, c, axes[a]), _half(dst, c, axes[a])
            return _remote(src, dst, ici_send.at[3 * a + k], ici_recv.at[3 * a + k], (px, py, c))

        def d2d(a, k, block, half):
            part = _half(outs[a].at[block], half, axes[a])
            return _remote(part, part, d2d_send.at[3 * a + k], d2d_recv.at[3 * a + k], (x, y, 1 - c))

        sends = [ici(a, k, mine) for a in range(n) for k in range(3)]
        for cp in sends:
            cp.start()
        for a in range(n):
            for k, (px, py) in enumerate(chips):
                ici(a, k, 2 * px + py).wait_recv()
                if a < ns:
                    sends.append(d2d(a, k, 2 * px + py, c))
                    sends[-1].start()
        for a in range(ns):
            for k, (px, py) in enumerate(chips):
                d2d(a, k, 2 * px + py, 1 - c).wait_recv()
        for cp in sends:
            cp.wait_send()
        for cp in local:
            cp.wait()

    arrays = list(split) + list(whole)
    return pl.pallas_call(
        body, name="gather_weights", in_specs=[HBM_SPEC] * n, out_specs=[HBM_SPEC] * n,
        out_shape=[jax.ShapeDtypeStruct((N_CHIPS,) + s.shape, s.dtype) for s in arrays],
        scratch_shapes=[pltpu.SemaphoreType.DMA((3 * n,)), pltpu.SemaphoreType.DMA((3 * n,)),
                        pltpu.SemaphoreType.DMA((3 * ns,)), pltpu.SemaphoreType.DMA((3 * ns,)),
                        pltpu.SemaphoreType.DMA((n,))],
        compiler_params=pltpu.CompilerParams(has_side_effects=True),
    )(*arrays)


def _rs_to_sibling(gs):
    n = len(gs)

    def body(*refs):
        ins, outs, send_sems, recv_sems = refs[:n], refs[n:2 * n], refs[2 * n], refs[2 * n + 1]
        x, y, c = _mesh_pos()
        copies = [_remote(_half(ins[a], 1 - c, 2), outs[a], send_sems.at[a], recv_sems.at[a], (x, y, 1 - c))
                  for a in range(n)]
        for cp in copies:
            cp.start()
        for cp in copies:
            cp.wait()

    return pl.pallas_call(
        body, name="rs_to_sibling", in_specs=[HBM_SPEC] * n, out_specs=[HBM_SPEC] * n,
        out_shape=[jax.ShapeDtypeStruct(g.shape[:2] + (g.shape[2] // 2,), g.dtype) for g in gs],
        scratch_shapes=[pltpu.SemaphoreType.DMA((n,)), pltpu.SemaphoreType.DMA((n,))],
        compiler_params=pltpu.CompilerParams(has_side_effects=True),
    )(*gs)


def _rs_add_halves(g, recv, c, name):
    _, rows, w = g.shape
    h = w // 2
    tr = rows // 2 if rows % 16 == 0 and rows > 64 else rows

    def body(c_ref, a_ref, b_ref, o_ref):
        o_ref[...] = (a_ref[...] + b_ref[...]).astype(BF16)

    return pl.pallas_call(
        body, name=name,
        grid_spec=pltpu.PrefetchScalarGridSpec(
            num_scalar_prefetch=1, grid=(N_CHIPS, rows // tr),
            in_specs=[pl.BlockSpec((1, tr, h), lambda j, i, s: (j, i, s[0])),
                      pl.BlockSpec((1, tr, h), lambda j, i, s: (j, i, 0))],
            out_specs=pl.BlockSpec((1, tr, h), lambda j, i, s: (j, i, 0))),
        out_shape=jax.ShapeDtypeStruct((N_CHIPS, rows, h), BF16),
        compiler_params=_params(2),
    )(jnp.reshape(c, (1,)).astype(jnp.int32), g, recv)


def _rs_chip_exchange(ps):
    n = len(ps)

    def body(*refs):
        ins, outs = refs[:n], refs[n:2 * n]
        send_sems, recv_sems, local_sems = refs[2 * n:]
        x, y, c = _mesh_pos()
        mine = 2 * x + y
        chips = _other_chips(x, y)
        local = [pltpu.make_async_copy(ins[a].at[mine], outs[a].at[mine], local_sems.at[a]) for a in range(n)]
        for cp in local:
            cp.start()

        def ici(a, k, src_block, dst_block):
            px, py = chips[k]
            return _remote(ins[a].at[src_block], outs[a].at[dst_block], send_sems.at[3 * a + k],
                           recv_sems.at[3 * a + k], (px, py, c))

        sends = [ici(a, k, 2 * px + py, mine) for a in range(n) for k, (px, py) in enumerate(chips)]
        for cp in sends:
            cp.start()
        for a in range(n):
            for k, (px, py) in enumerate(chips):
                ici(a, k, mine, 2 * px + py).wait_recv()
        for cp in sends:
            cp.wait_send()
        for cp in local:
            cp.wait()

    return pl.pallas_call(
        body, name="rs_chip_exchange", in_specs=[HBM_SPEC] * n, out_specs=[HBM_SPEC] * n,
        out_shape=[jax.ShapeDtypeStruct(p.shape, p.dtype) for p in ps],
        scratch_shapes=[pltpu.SemaphoreType.DMA((3 * n,)), pltpu.SemaphoreType.DMA((3 * n,)),
                        pltpu.SemaphoreType.DMA((n,))],
        compiler_params=pltpu.CompilerParams(has_side_effects=True),
    )(*ps)


def _rs_sum_chips(parts, name):
    _, rows, h = parts.shape
    tr = rows // 2 if rows % 16 == 0 and rows > 64 else rows

    def body(p_ref, o_ref):
        p = p_ref[...].astype(F32)
        o_ref[...] = ((p[0] + p[1]) + p[2]) + p[3]

    return pl.pallas_call(
        body, name=name, grid=(rows // tr,),
        in_specs=[pl.BlockSpec((N_CHIPS, tr, h), lambda i: (0, i, 0))],
        out_specs=pl.BlockSpec((tr, h), lambda i: (i, 0)),
        out_shape=jax.ShapeDtypeStruct((rows, h), F32),
        compiler_params=_params(1),
    )(parts)


def _rs_share(halves):
    n = len(halves)

    def body(*refs):
        ins, outs = refs[:n], refs[n:2 * n]
        send_sems, recv_sems, local_sems = refs[2 * n:]
        x, y, c = _mesh_pos()
        local = [pltpu.make_async_copy(ins[a], _half(outs[a], c, 1), local_sems.at[a]) for a in range(n)]
        for cp in local:
            cp.start()
        sends = [_remote(ins[a], _half(outs[a], c, 1), send_sems.at[a], recv_sems.at[a], (x, y, 1 - c))
                 for a in range(n)]
        for cp in sends:
            cp.start()
        for a in range(n):
            _remote(ins[a], _half(outs[a], 1 - c, 1), send_sems.at[a], recv_sems.at[a], (x, y, 1 - c)).wait_recv()
        for cp in sends:
            cp.wait_send()
        for cp in local:
            cp.wait()

    return pl.pallas_call(
        body, name="rs_share", in_specs=[HBM_SPEC] * n, out_specs=[HBM_SPEC] * n,
        out_shape=[jax.ShapeDtypeStruct((p.shape[0], 2 * p.shape[1]), p.dtype) for p in halves],
        scratch_shapes=[pltpu.SemaphoreType.DMA((n,)), pltpu.SemaphoreType.DMA((n,)),
                        pltpu.SemaphoreType.DMA((n,))],
        compiler_params=pltpu.CompilerParams(has_side_effects=True),
    )(*halves)


def _adamw(g, w, m, v, name):
    rows, cols = g.shape
    tr = 256 if rows % 256 == 0 else (rows // 2 if rows % 16 == 0 and rows > 64 else rows)

    def body(g_ref, w_ref, m_ref, v_ref, d_ref, m2_ref, v2_ref):
        d_ref[...], m2_ref[...], v2_ref[...] = _adam_update(g_ref[...], w_ref[...], m_ref[...], v_ref[...])

    spec = pl.BlockSpec((tr, cols), lambda i: (i, 0))
    return pl.pallas_call(
        body, name=name, grid=(rows // tr,), in_specs=[spec] * 4, out_specs=[spec] * 3,
        out_shape=[jax.ShapeDtypeStruct((rows, cols), F32)] * 3,
        compiler_params=_params(1),
    )(g, w, m, v)


def _rows_of(a):
    flat = a.reshape(-1)
    pad = (-flat.shape[0]) % D
    if pad:
        flat = jnp.concatenate([flat, jnp.zeros((pad,), flat.dtype)])
    return flat.reshape(-1, D)


SLAB_PARTS = (("w_in", (D, D_IN // N_CHIPS)), ("w_out", (D // N_CHIPS, D)), ("w_ffn_gate", (D, D_FF // N_CHIPS)),
              ("w_ffn_up", (D, D_FF // N_CHIPS)), ("w_ffn_down", (D_FF // N_CHIPS, D)),
              ("meta_tokens", (N_META, D // N_CHIPS)), ("conv_w", (CONV_W, C_CONV // N_CHIPS)),
              ("gla_w_gate2", (RANK, GLA_K // N_CHIPS)))


def _pack_slab(parts):
    rows = [_rows_of(parts[name].reshape(shape)) for name, shape in SLAB_PARTS]
    used = sum(r.shape[0] for r in rows)
    rows.append(jnp.zeros((SLAB_ROWS - used, D), F32))
    return jnp.concatenate(rows, axis=0)


def _unpack_slab(slab, lead):
    out, r0 = {}, 0
    for name, shape in SLAB_PARTS:
        size = shape[0] * shape[1]
        nrows = -(-size // D)
        out[name] = slab[r0:r0 + nrows].reshape(-1)[:size].reshape(lead[name] + shape)
        r0 += nrows
    return out


SMALL_PARTS = (("norm_mix_g", 0, 0, D), ("norm_ffn_g", 1, 0, D), ("norm_final_g", 2, 0, D),
               ("conv_b", 3, 0, C_CONV), ("conv_ln_g", 3, C_CONV, C_CONV), ("conv_ln_b", 4, 0, C_CONV),
               ("gla_gate_b", 4, C_CONV, GLA_K), ("gla_norm_g", 4, C_CONV + GLA_K, DV))


def _pack_small(parts):
    slab = jnp.zeros((SMALL_ROWS, D), F32)
    for name, row, col, size in SMALL_PARTS:
        slab = lax.dynamic_update_slice(slab, parts[name].reshape(1, size).astype(F32), (row, col))
    return slab


def _unpack_small(slab, shapes):
    return {name: slab[row, col:col + size].reshape(shapes[name]) for name, row, col, size in SMALL_PARTS}


def _column_block(full, j, width):
    return lax.dynamic_slice_in_dim(full, j * width, width, axis=1)


def _local_step(x, target, w):
    n_ex, seq, _ = x.shape
    lp = HEAD_ROWS + seq
    t = n_ex * lp
    meta = jnp.broadcast_to(w["meta_tokens"][None], (n_ex, N_META, D))
    h0 = jnp.concatenate([jnp.zeros((n_ex, PAD_ROWS, D), F32), meta, x], axis=1).reshape(t, D)
    tgt = jnp.concatenate([jnp.zeros((n_ex, HEAD_ROWS, D), F32), target], axis=1).reshape(t, D)
    row_mask = jnp.concatenate([jnp.zeros((n_ex, HEAD_ROWS, 1), F32), jnp.ones((n_ex, seq, 1), F32)],
                               axis=1).reshape(t, 1)

    u, hn = _in_proj(h0, w["norm_mix_g"], w["w_in"])
    yc, y_conv = _conv_fwd(u, w["conv_w"], w["conv_b"], w["conv_ln_g"], w["conv_ln_b"], n_ex, lp)
    y_gla, states = _gla_fwd(u, w["gla_w_gate2"], w["gla_gate_b"], w["gla_norm_g"], n_ex, lp)
    h1, hn2, gate, up, act = _mix_out_ffn_up(h0, y_conv, y_gla, w["w_out"], w["norm_ffn_g"],
                                             w["w_ffn_gate_t"], w["w_ffn_up_t"])
    dh2, loss, d_final_g = _ffn_down_loss(act, w["w_ffn_down"], h1, tgt, w["norm_final_g"], row_mask)

    dgate, dup, dh1, dycat, d_ffn_g = _ffn_bwd(dh2, gate, up, h1, w["w_ffn_down"], w["w_ffn_gate_t"],
                                                w["w_ffn_up_t"], w["w_out"], w["norm_ffn_g"])
    du_conv, d_conv_w, d_conv_b, d_ln_g, d_ln_b = _conv_bwd(dycat, yc, u, w["conv_w"], w["conv_ln_g"],
                                                            w["conv_ln_b"], n_ex, lp)
    du_gla, d_w2, d_gate_b, d_norm_g = _gla_bwd(dycat, u, states, w["gla_w_gate2"], w["gla_gate_b"],
                                                w["gla_norm_g"], n_ex, lp)
    dh0, d_mix_g = _in_proj_bwd(du_conv, du_gla, w["w_in"][:, :2 * C_CONV], w["w_in"][:, 2 * C_CONV:],
                                h0, dh1, w["norm_mix_g"])

    d_w_in_t = jnp.concatenate([_wgrad(du_conv, hn, "wgrad_in_conv"), _wgrad(du_gla, hn, "wgrad_in_gla")],
                               axis=0)[:D_IN]
    d_w_out = jnp.concatenate([_wgrad(y_conv, dh1, "wgrad_out_conv"), _wgrad(y_gla, dh1, "wgrad_out_gla")], axis=0)
    dh0 = dh0.reshape(n_ex, lp, D)
    grads = {
        "w_in_t": d_w_in_t, "w_out": d_w_out,
        "w_ffn_gate_t": _wgrad(dgate, hn2, "wgrad_gate"), "w_ffn_up_t": _wgrad(dup, hn2, "wgrad_up"),
        "w_ffn_down": _wgrad(act, dh2, "wgrad_down"),
        "meta_tokens": jnp.sum(dh0[:, PAD_ROWS:HEAD_ROWS], axis=0),
        "conv_w": d_conv_w, "gla_w_gate2": d_w2[:RANK],
        "norm_mix_g": d_mix_g, "norm_ffn_g": d_ffn_g, "norm_final_g": d_final_g,
        "conv_b": d_conv_b, "conv_ln_g": d_ln_g, "conv_ln_b": d_ln_b,
        "gla_gate_b": d_gate_b, "gla_norm_g": d_norm_g,
    }
    return loss[0, 0], dh0[:, HEAD_ROWS:], grads


WEIGHT_NAMES = ("meta_tokens", "norm_mix_g", "w_in", "conv_w", "conv_b", "conv_ln_g", "conv_ln_b", "gla_w_gate2",
                "gla_gate_b", "gla_norm_g", "w_out", "norm_ffn_g", "w_ffn_gate", "w_ffn_up", "w_ffn_down",
                "norm_final_g")
MATMUL_WEIGHTS = ("w_in", "w_out", "w_ffn_gate", "w_ffn_up", "w_ffn_down")
ROW_SHARDED = ("w_out", "w_ffn_down")


def _full_weights(ws):
    sh = lambda name: ws[name].reshape(ws[name].shape[-2:])
    split = [sh("w_in").astype(BF16), sh("w_out").astype(BF16), sh("w_ffn_gate").T.astype(BF16),
             sh("w_ffn_up").T.astype(BF16), sh("w_ffn_down").astype(BF16)]
    whole = [sh("meta_tokens"), sh("conv_w"), sh("gla_w_gate2")]
    w_in, w_out, gate_t, up_t, down, meta, conv_w, w2 = _gather_weights(split, [0, 0, 0, 0, 0], whole)
    cols = lambda a: jnp.concatenate([a[j] for j in range(N_CHIPS)], axis=1)
    full = {name: ws[name].reshape(1, -1) for name, _, _, _ in SMALL_PARTS}
    full["w_in"] = jnp.concatenate([cols(w_in), jnp.zeros((D, D_IN_PAD - D_IN), BF16)], axis=1)
    full["w_out"] = w_out.reshape(D, D)
    full["w_ffn_gate_t"] = gate_t.reshape(D_FF, D)
    full["w_ffn_up_t"] = up_t.reshape(D_FF, D)
    full["w_ffn_down"] = down.reshape(D_FF, D)
    full["meta_tokens"] = cols(meta)
    full["conv_w"] = jnp.concatenate([cols(conv_w), jnp.zeros((32 - CONV_W, C_CONV), F32)], axis=0)
    full["gla_w_gate2"] = jnp.concatenate([cols(w2), jnp.zeros((128 - RANK, GLA_K), F32)], axis=0).astype(BF16)
    return full


SMALL_RS_ROWS = 48


def _pack_small_sharded(grads):
    by_chip = lambda g, w: jnp.transpose(g.reshape(g.shape[0], N_CHIPS, w), (1, 0, 2))
    meta = by_chip(grads["meta_tokens"], D // N_CHIPS)
    conv = by_chip(grads["conv_w"], C_CONV // N_CHIPS).reshape(N_CHIPS, 16, 256)
    w2 = by_chip(grads["gla_w_gate2"], GLA_K // N_CHIPS).reshape(N_CHIPS, 4, 256)
    pad = jnp.zeros((N_CHIPS, SMALL_RS_ROWS - 36, 256), F32)
    return jnp.concatenate([meta, conv, w2, pad], axis=1)


def _unpack_small_sharded(g):
    return {"meta_tokens": g[0:16], "conv_w": g[16:32].reshape(32, C_CONV // N_CHIPS)[:CONV_W],
            "gla_w_gate2": g[32:36].reshape(RANK, GLA_K // N_CHIPS)}


def _kernel_without_overlap(x, meta_tokens, norm_mix_g, w_in, conv_w, conv_b, conv_ln_g, conv_ln_b, gla_w_gate2, gla_gate_b, gla_norm_g, w_out, norm_ffn_g, w_ffn_gate, w_ffn_up, w_ffn_down, norm_final_g, loss_target, m_meta_tokens, m_norm_mix_g, m_w_in, m_conv_w, m_conv_b, m_conv_ln_g, m_conv_ln_b, m_gla_w_gate2, m_gla_gate_b, m_gla_norm_g, m_w_out, m_norm_ffn_g, m_w_ffn_gate, m_w_ffn_up, m_w_ffn_down, m_norm_final_g, v_meta_tokens, v_norm_mix_g, v_w_in, v_conv_w, v_conv_b, v_conv_ln_g, v_conv_ln_b, v_gla_w_gate2, v_gla_gate_b, v_gla_norm_g, v_w_out, v_norm_ffn_g, v_w_ffn_gate, v_w_ffn_up, v_w_ffn_down, v_norm_final_g):
    ws = dict(zip(WEIGHT_NAMES, (meta_tokens, norm_mix_g, w_in, conv_w, conv_b, conv_ln_g, conv_ln_b, gla_w_gate2,
                                 gla_gate_b, gla_norm_g, w_out, norm_ffn_g, w_ffn_gate, w_ffn_up, w_ffn_down,
                                 norm_final_g)))
    ms = dict(zip(WEIGHT_NAMES, (m_meta_tokens, m_norm_mix_g, m_w_in, m_conv_w, m_conv_b, m_conv_ln_g, m_conv_ln_b,
                                 m_gla_w_gate2, m_gla_gate_b, m_gla_norm_g, m_w_out, m_norm_ffn_g, m_w_ffn_gate,
                                 m_w_ffn_up, m_w_ffn_down, m_norm_final_g)))
    vs = dict(zip(WEIGHT_NAMES, (v_meta_tokens, v_norm_mix_g, v_w_in, v_conv_w, v_conv_b, v_conv_ln_g, v_conv_ln_b,
                                 v_gla_w_gate2, v_gla_gate_b, v_gla_norm_g, v_w_out, v_norm_ffn_g, v_w_ffn_gate,
                                 v_w_ffn_up, v_w_ffn_down, v_norm_final_g)))
    c = lax.axis_index("c")

    full = _full_weights(ws)
    loss, grad_x, grads = _local_step(x, loss_target, full)
    loss = lax.psum(loss, ("x", "y", "c"))

    rs_names = ("w_in", "w_out", "w_ffn_gate", "w_ffn_up", "w_ffn_down", "small")
    by_owner = [grads["w_in_t"].reshape(N_CHIPS, D_IN // N_CHIPS, D), grads["w_out"].reshape(N_CHIPS, D // N_CHIPS, D),
                grads["w_ffn_gate_t"].reshape(N_CHIPS, D_FF // N_CHIPS, D),
                grads["w_ffn_up_t"].reshape(N_CHIPS, D_FF // N_CHIPS, D),
                grads["w_ffn_down"].reshape(N_CHIPS, D_FF // N_CHIPS, D), _pack_small_sharded(grads)]
    from_sibling = _rs_to_sibling(by_owner)
    chip_sums = [_rs_add_halves(g, r, c, "rs_add_" + nm) for g, r, nm in zip(by_owner, from_sibling, rs_names)]
    halves = [_rs_sum_chips(p, "rs_sum_" + nm) for p, nm in zip(_rs_chip_exchange(chip_sums), rs_names)]
    reduced = dict(zip(rs_names, _rs_share(halves)))
    g_sharded = {"w_in": reduced["w_in"].T, "w_out": reduced["w_out"], "w_ffn_gate": reduced["w_ffn_gate"].T,
                 "w_ffn_up": reduced["w_ffn_up"].T, "w_ffn_down": reduced["w_ffn_down"],
                 **_unpack_small_sharded(reduced["small"])}
    out = {"grad": {}, "delta": {}, "new_m": {}, "new_v": {}}
    for name, g in g_sharded.items():
        shape = ws[name].shape
        flat = lambda a: a.reshape(shape[-2:])
        delta, new_m, new_v = _adamw(g, flat(ws[name]), flat(ms[name]), flat(vs[name]), "adamw_" + name)
        for kind, a in (("grad", g), ("delta", delta), ("new_m", new_m), ("new_v", new_v)):
            out[kind][name] = a.reshape(shape)

    small_shapes = {name: ws[name].shape for name, _, _, _ in SMALL_PARTS}
    g_s, d_s, m_s, v_s = _allreduce_small_adamw(_pack_small(grads), _pack_small(ws), _pack_small(ms), _pack_small(vs))
    for kind, slab in (("grad", g_s), ("delta", d_s), ("new_m", m_s), ("new_v", v_s)):
        out[kind].update(_unpack_small(slab, small_shapes))

    return (loss, grad_x, *[out[kind][name] for kind in ("grad", "delta", "new_m", "new_v") for name in WEIGHT_NAMES])


def _gather_plan(split, whole=(), axes=None):
    split, whole = list(split), list(whole)
    ns, n = len(split), len(split) + len(whole)

    def make(ins, outs, sems):
        ici_send, ici_recv, d2d_send, d2d_recv, own_send, own_recv = sems
        x, y, c = _mesh_pos()
        mine = 2 * x + y
        chips = _other_chips(x, y)
        blocks = [2 * px + py for px, py in chips]

        def own(a):
            return _remote(ins[a], outs[a].at[mine], own_send.at[a], own_recv.at[a], (x, y, 1 - c))

        def ici(a, k, block):
            px, py = chips[k]
            src, dst = ins[a], outs[a].at[block]
            if a < ns:
                src, dst = _half(src, c, axes[a]), _half(dst, c, axes[a])
            return _remote(src, dst, ici_send.at[3 * a + k], ici_recv.at[3 * a + k], (px, py, c))

        def d2d(a, k, half):
            part = _half(outs[a].at[blocks[k]], half, axes[a])
            return _remote(part, part, d2d_send.at[3 * a + k], d2d_recv.at[3 * a + k], (x, y, 1 - c))

        def start():
            for a in range(n):
                for k in range(3):
                    ici(a, k, mine).start()
                own(a).start()

        def finish():
            for a in range(n):
                for k in range(3):
                    ici(a, k, blocks[k]).wait_recv()
                    if a < ns:
                        d2d(a, k, c).start()
            for a in range(ns):
                for k in range(3):
                    d2d(a, k, 1 - c).wait_recv()
            for a in range(n):
                for k in range(3):
                    ici(a, k, mine).wait_send()
                    if a < ns:
                        d2d(a, k, c).wait_send()
                own(a).wait()

        return start, finish

    arrays = split + whole
    axes = [0] * ns if axes is None else list(axes)
    return _Plan(arrays, [jax.ShapeDtypeStruct((N_CHIPS,) + s.shape, s.dtype) for s in arrays],
                 [pltpu.SemaphoreType.DMA((3 * n,)), pltpu.SemaphoreType.DMA((3 * n,)),
                  pltpu.SemaphoreType.DMA((3 * ns,)), pltpu.SemaphoreType.DMA((3 * ns,)),
                  pltpu.SemaphoreType.DMA((n,)), pltpu.SemaphoreType.DMA((n,))], make)


def _to_sibling_plan(gs):
    n = len(gs)

    def make(ins, outs, sems):
        send_sems, recv_sems = sems
        x, y, c = _mesh_pos()

        def copy(a):
            return _remote(_half(ins[a], 1 - c, 2), outs[a], send_sems.at[a], recv_sems.at[a], (x, y, 1 - c))

        def start():
            for a in range(n):
                copy(a).start()

        def finish():
            for a in range(n):
                copy(a).wait()

        return start, finish

    return _Plan(list(gs), [jax.ShapeDtypeStruct(g.shape[:2] + (g.shape[2] // 2,), g.dtype) for g in gs],
                 [pltpu.SemaphoreType.DMA((n,)), pltpu.SemaphoreType.DMA((n,))], make)


def _chip_exchange_plan(ps):
    n = len(ps)

    def make(ins, outs, sems):
        send_sems, recv_sems = sems
        x, y, c = _mesh_pos()
        chips = _other_chips(x, y)

        def ici(a, k):
            px, py = chips[k]
            return _remote(ins[a].at[2 * px + py], outs[a].at[k], send_sems.at[3 * a + k],
                           recv_sems.at[3 * a + k], (px, py, c))

        def start():
            for a in range(n):
                for k in range(3):
                    ici(a, k).start()

        def finish():
            for a in range(n):
                for k in range(3):
                    ici(a, k).wait()

        return start, finish

    return _Plan(list(ps), [jax.ShapeDtypeStruct((3,) + p.shape[1:], p.dtype) for p in ps],
                 [pltpu.SemaphoreType.DMA((3 * n,)), pltpu.SemaphoreType.DMA((3 * n,))], make)


def _share_plan(halves):
    n = len(halves)

    def make(ins, outs, sems):
        send_sems, recv_sems = sems
        x, y, c = _mesh_pos()

        def d2d(a):
            return _remote(ins[a], outs[a], send_sems.at[a], recv_sems.at[a], (x, y, 1 - c))

        def start():
            for a in range(n):
                d2d(a).start()

        def finish():
            for a in range(n):
                d2d(a).wait()

        return start, finish

    return _Plan(list(halves), [jax.ShapeDtypeStruct(p.shape, p.dtype) for p in halves],
                 [pltpu.SemaphoreType.DMA((n,)), pltpu.SemaphoreType.DMA((n,))], make)


def _rs_sum(own, others, mine, name):
    _, rows, h = own.shape
    tr = rows // 2 if rows % 16 == 0 and rows > 64 else rows

    def body(mine_ref, own_ref, oth_ref, o_ref):
        p = oth_ref[...].astype(F32)
        o_ref[...] = ((own_ref[0].astype(F32) + p[0]) + p[1]) + p[2]

    return pl.pallas_call(
        body, name=name,
        grid_spec=pltpu.PrefetchScalarGridSpec(
            num_scalar_prefetch=1, grid=(rows // tr,),
            in_specs=[pl.BlockSpec((1, tr, h), lambda i, s: (s[0], i, 0)),
                      pl.BlockSpec((3, tr, h), lambda i, s: (0, i, 0))],
            out_specs=pl.BlockSpec((tr, h), lambda i, s: (i, 0))),
        out_shape=jax.ShapeDtypeStruct((rows, h), F32),
        compiler_params=_params(1),
    )(jnp.reshape(mine, (1,)).astype(jnp.int32), own, others)


def _join(mine, theirs, c):
    return jnp.where(c == 0, jnp.concatenate([mine, theirs], axis=1), jnp.concatenate([theirs, mine], axis=1))


LOSS_ROW = 5


def _merge_plans(a, b):
    na_in, na_out, na_sems = len(a.arrays), len(a.out_shape), len(a.sems)

    def make(ins, outs, sems):
        start_a, finish_a = a.make(ins[:na_in], outs[:na_out], sems[:na_sems])
        start_b, finish_b = b.make(ins[na_in:], outs[na_out:], sems[na_sems:])

        def start():
            start_a()
            start_b()

        def finish():
            finish_a()
            finish_b()

        return start, finish

    return _Plan(list(a.arrays) + list(b.arrays), list(a.out_shape) + list(b.out_shape),
                 list(a.sems) + list(b.sems), make)


def _exchange(plan, name):
    n_in, n_out = len(plan.arrays), len(plan.out_shape)

    def body(*refs):
        start, finish = plan.make(refs[:n_in], refs[n_in:n_in + n_out], refs[n_in + n_out:])
        start()
        finish()

    return pl.pallas_call(
        body, name=name, in_specs=[HBM_SPEC] * n_in, out_specs=[HBM_SPEC] * n_out, out_shape=list(plan.out_shape),
        scratch_shapes=list(plan.sems), compiler_params=pltpu.CompilerParams(has_side_effects=True),
    )(*plan.arrays)


def _adamw_halves(mine, theirs, c, w, m, v, name):
    rows, h = mine.shape
    tr = rows // 2 if rows % 16 == 0 else rows

    def body(c_ref, a_ref, b_ref, w_ref, m_ref, v_ref, go_ref, d_ref, m2_ref, v2_ref):
        g = jnp.where(pl.program_id(1) == c_ref[0], a_ref[...], b_ref[...])
        go_ref[...] = g
        d_ref[...], m2_ref[...], v2_ref[...] = _adam_update(g, w_ref[...], m_ref[...], v_ref[...])

    half = pl.BlockSpec((tr, h), lambda i, j, s: (i, 0))
    spec = pl.BlockSpec((tr, h), lambda i, j, s: (i, j))
    return pl.pallas_call(
        body, name=name,
        grid_spec=pltpu.PrefetchScalarGridSpec(num_scalar_prefetch=1, grid=(rows // tr, 2),
                                               in_specs=[half, half, spec, spec, spec], out_specs=[spec] * 4),
        out_shape=[jax.ShapeDtypeStruct((rows, 2 * h), F32)] * 4,
        compiler_params=_params(2),
    )(jnp.reshape(c, (1,)).astype(jnp.int32), mine, theirs, w, m, v)


def _columns(gathered):
    return jnp.concatenate([gathered[j] for j in range(N_CHIPS)], axis=1)


def kernel(x, meta_tokens, norm_mix_g, w_in, conv_w, conv_b, conv_ln_g, conv_ln_b, gla_w_gate2, gla_gate_b, gla_norm_g, w_out, norm_ffn_g, w_ffn_gate, w_ffn_up, w_ffn_down, norm_final_g, loss_target, m_meta_tokens, m_norm_mix_g, m_w_in, m_conv_w, m_conv_b, m_conv_ln_g, m_conv_ln_b, m_gla_w_gate2, m_gla_gate_b, m_gla_norm_g, m_w_out, m_norm_ffn_g, m_w_ffn_gate, m_w_ffn_up, m_w_ffn_down, m_norm_final_g, v_meta_tokens, v_norm_mix_g, v_w_in, v_conv_w, v_conv_b, v_conv_ln_g, v_conv_ln_b, v_gla_w_gate2, v_gla_gate_b, v_gla_norm_g, v_w_out, v_norm_ffn_g, v_w_ffn_gate, v_w_ffn_up, v_w_ffn_down, v_norm_final_g):
    ws = dict(zip(WEIGHT_NAMES, (meta_tokens, norm_mix_g, w_in, conv_w, conv_b, conv_ln_g, conv_ln_b, gla_w_gate2,
                                 gla_gate_b, gla_norm_g, w_out, norm_ffn_g, w_ffn_gate, w_ffn_up, w_ffn_down,
                                 norm_final_g)))
    ms = dict(zip(WEIGHT_NAMES, (m_meta_tokens, m_norm_mix_g, m_w_in, m_conv_w, m_conv_b, m_conv_ln_g, m_conv_ln_b,
                                 m_gla_w_gate2, m_gla_gate_b, m_gla_norm_g, m_w_out, m_norm_ffn_g, m_w_ffn_gate,
                                 m_w_ffn_up, m_w_ffn_down, m_norm_final_g)))
    vs = dict(zip(WEIGHT_NAMES, (v_meta_tokens, v_norm_mix_g, v_w_in, v_conv_w, v_conv_b, v_conv_ln_g, v_conv_ln_b,
                                 v_gla_w_gate2, v_gla_gate_b, v_gla_norm_g, v_w_out, v_norm_ffn_g, v_w_ffn_gate,
                                 v_w_ffn_up, v_w_ffn_down, v_norm_final_g)))
    c = lax.axis_index("c")
    shard = lambda d, name: d[name].reshape(d[name].shape[-2:])
    vec = {name: ws[name].reshape(1, -1) for name, _, _, _ in SMALL_PARTS}
    n_ex, seq, _ = x.shape
    lp = HEAD_ROWS + seq
    t = n_ex * lp

    w_in_g, meta_g, conv_w_g, w2_g = _exchange(
        _gather_plan([shard(ws, "w_in").T.astype(BF16)],
                     [shard(ws, "meta_tokens"), shard(ws, "conv_w"), shard(ws, "gla_w_gate2")], axes=[1]),
        "gather_first")
    w_in_t = jnp.concatenate([w_in_g.reshape(D_IN, D), jnp.zeros((D_IN_PAD - D_IN, D), BF16)], axis=0)
    w_in_full = w_in_t.T
    conv_w_full = jnp.concatenate([_columns(conv_w_g), jnp.zeros((32 - CONV_W, C_CONV), F32)], axis=0)
    w2_full = jnp.concatenate([_columns(w2_g), jnp.zeros((128 - RANK, GLA_K), F32)], axis=0).astype(BF16)

    meta = jnp.broadcast_to(_columns(meta_g)[None], (n_ex, N_META, D))
    h0 = jnp.concatenate([jnp.zeros((n_ex, PAD_ROWS, D), F32), meta, x], axis=1).reshape(t, D)
    tgt = jnp.concatenate([jnp.zeros((n_ex, HEAD_ROWS, D), F32), loss_target], axis=1).reshape(t, D)
    row_mask = jnp.concatenate([jnp.zeros((n_ex, HEAD_ROWS, 1), F32), jnp.ones((n_ex, seq, 1), F32)],
                               axis=1).reshape(t, 1)

    (u, hn), (w_out_g,) = _in_proj(h0, vec["norm_mix_g"], w_in_full,
                                   plan=_gather_plan([shard(ws, "w_out").astype(BF16)]))
    (yc, y_conv), (gate_g,) = _conv_fwd(
        u, conv_w_full, vec["conv_b"], vec["conv_ln_g"], vec["conv_ln_b"], n_ex, lp,
        plan=_gather_plan([shard(ws, "w_ffn_gate").T.astype(BF16)]))
    (y_gla, states), (up_g, down_g) = _gla_fwd(
        u, w2_full, vec["gla_gate_b"], vec["gla_norm_g"], n_ex, lp,
        plan=_gather_plan([shard(ws, "w_ffn_up").T.astype(BF16), shard(ws, "w_ffn_down").astype(BF16)]))
    w_out_full, w_down_full = w_out_g.reshape(D, D), down_g.reshape(D_FF, D)
    w_gate_t, w_up_t = gate_g.reshape(D_FF, D), up_g.reshape(D_FF, D)

    h1, hn2, gate, up, act = _mix_out_ffn_up(h0, y_conv, y_gla, w_out_full, vec["norm_ffn_g"], w_gate_t.T, w_up_t.T)
    dh2, loss, d_final_g = _ffn_down_loss(act, w_down_full, h1, tgt, vec["norm_final_g"], row_mask)
    dgate, dup, dh1, dycat, d_ffn_g = _ffn_bwd(dh2, gate, up, h1, w_down_full.T, w_gate_t, w_up_t, w_out_full.T,
                                                vec["norm_ffn_g"])

    early = ("w_out", "w_ffn_gate", "w_ffn_up", "w_ffn_down")
    d_w_out = jnp.concatenate([_wgrad(y_conv, dh1, "wgrad_out_conv"), _wgrad(y_gla, dh1, "wgrad_out_gla")], axis=0)
    by_owner = [d_w_out.reshape(N_CHIPS, D // N_CHIPS, D),
                _wgrad(dgate, hn2, "wgrad_gate").reshape(N_CHIPS, D_FF // N_CHIPS, D),
                _wgrad(dup, hn2, "wgrad_up").reshape(N_CHIPS, D_FF // N_CHIPS, D),
                _wgrad(act, dh2, "wgrad_down").reshape(N_CHIPS, D_FF // N_CHIPS, D)]
    (du_conv, d_conv_w, d_conv_b, d_ln_g, d_ln_b), from_sibling = _conv_bwd(
        dycat, yc, u, conv_w_full, vec["conv_ln_g"], vec["conv_ln_b"], n_ex, lp, plan=_to_sibling_plan(by_owner))
    chip_sums = [_rs_add_halves(g, r, c, "rs_add_" + nm) for g, r, nm in zip(by_owner, from_sibling, early)]
    (du_gla, d_w2, d_gate_b, d_norm_g), exchanged = _gla_bwd(
        dycat, u, states, w2_full, vec["gla_gate_b"], vec["gla_norm_g"], n_ex, lp,
        plan=_chip_exchange_plan(chip_sums))
    mine = 2 * lax.axis_index("x") + lax.axis_index("y")
    halves = [_rs_sum(own, oth, mine, "rs_sum_" + nm) for own, oth, nm in zip(chip_sums, exchanged, early)]

    d_w_in_t = jnp.concatenate([_wgrad(du_conv, hn, "wgrad_in_conv"), _wgrad(du_gla, hn, "wgrad_in_gla")],
                               axis=0)[:D_IN].reshape(N_CHIPS, D_IN // N_CHIPS, D)
    (in_from_sibling,) = _exchange(_to_sibling_plan([d_w_in_t]), "rs_late_to_sibling")
    in_chip_sum = _rs_add_halves(d_w_in_t, in_from_sibling, c, "rs_add_w_in")
    (dh0, d_mix_g), shared = _in_proj_bwd(
        du_conv, du_gla, w_in_t[:2 * C_CONV], w_in_t[2 * C_CONV:], h0, dh1, vec["norm_mix_g"],
        plan=_merge_plans(_share_plan(halves), _chip_exchange_plan([in_chip_sum])))
    dh0 = dh0.reshape(n_ex, lp, D)
    grad_x = dh0[:, HEAD_ROWS:]

    out = {"grad": {}, "delta": {}, "new_m": {}, "new_v": {}}

    def update(name, g=None, halves=None, transposed=False):
        shape = ws[name].shape
        lay = (lambda a: a.T) if transposed else (lambda a: a)
        w2d, m2d, v2d = lay(shard(ws, name)), lay(shard(ms, name)), lay(shard(vs, name))
        if halves is not None:
            res = _adamw_halves(*halves, c, w2d, m2d, v2d, "adamw_" + name)
        else:
            res = [g, *_adamw(g, w2d, m2d, v2d, "adamw_" + name)]
        for kind, a in zip(("grad", "delta", "new_m", "new_v"), res):
            out[kind][name] = lay(a).reshape(shape)

    update("w_out", halves=(halves[0], shared[0]))
    update("w_ffn_gate", halves=(halves[1], shared[1]), transposed=True)
    update("w_ffn_up", halves=(halves[2], shared[2]), transposed=True)
    update("w_ffn_down", halves=(halves[3], shared[3]))

    in_half = _rs_sum(in_chip_sum, shared[4], mine, "rs_sum_w_in")
    (in_shared,) = _exchange(_share_plan([in_half]), "rs_late_share")
    update("w_in", halves=(in_half, in_shared), transposed=True)

    small = {"norm_mix_g": d_mix_g, "norm_ffn_g": d_ffn_g, "norm_final_g": d_final_g, "conv_b": d_conv_b,
             "conv_ln_g": d_ln_g, "conv_ln_b": d_ln_b, "gla_gate_b": d_gate_b, "gla_norm_g": d_norm_g}
    small_shapes = {name: ws[name].shape for name, _, _, _ in SMALL_PARTS}
    part = lax.dynamic_update_slice(_pack_small(small), loss[:, :1], (LOSS_ROW, 0))
    part = jnp.concatenate([part, jnp.sum(dh0[:, PAD_ROWS:HEAD_ROWS], axis=0), d_conv_w.reshape(16, D),
                            d_w2[:RANK].reshape(4, D), jnp.zeros((4, D), F32)], axis=0)
    tall = lambda a: jnp.concatenate([a, jnp.zeros((part.shape[0] - SMALL_ROWS, D), F32)], axis=0)
    g_s, d_s, m_s, v_s = _allreduce_small_adamw(part, tall(_pack_small(ws)), tall(_pack_small(ms)),
                                                tall(_pack_small(vs)))
    for kind, slab in (("grad", g_s), ("delta", d_s), ("new_m", m_s), ("new_v", v_s)):
        out[kind].update(_unpack_small(slab, small_shapes))
    loss = g_s[LOSS_ROW, 0]
    block = lambda a, width: lax.dynamic_slice_in_dim(a, mine * width, width, axis=1)
    update("meta_tokens", g=block(g_s[8:24], D // N_CHIPS))
    update("conv_w", g=block(g_s[24:40].reshape(32, C_CONV), C_CONV // N_CHIPS)[:CONV_W])
    update("gla_w_gate2", g=block(g_s[40:44].reshape(RANK, GLA_K), GLA_K // N_CHIPS))

    return (loss, grad_x, *[out[kind][name] for kind in ("grad", "delta", "new_m", "new_v") for name in WEIGHT_NAMES])
```

```python
import functools
from typing import Any, Callable, NamedTuple, Sequence

import jax
import jax.numpy as jnp
from jax import lax
from jax.experimental import pallas as pl
from jax.experimental.pallas import tpu as pltpu

F32 = jnp.float32
BF16 = jnp.bfloat16
MESH = pl.DeviceIdType.MESH

D = 1024
N_META = 16
C_CONV = 512
CONV_W = 31
GLA_K = 256
GLA_V = 512
N_HEADS = 4
DK = 64
DV = 128
RANK = 16
CHUNK = 64
PAD_ROWS = CHUNK - N_META
HEAD_ROWS = CHUNK
D_IN = 2576
D_IN_PAD = 2688
D_GLA_IN = D_IN_PAD - 2 * C_CONV
D_FF = 2816
RMS_EPS = 1e-6
LN_EPS = 1e-5
GATE_TAU = 16.0
N_CHIPS = 4

ADAM_LR = 0.001
ADAM_B1 = 0.9
ADAM_B2 = 0.999
ADAM_EPS = 1e-08
ADAM_WD = 0.01
ADAM_STEP = 10

V7X_VMEM_BYTES = 64 * 1024 * 1024
VMEM_LIMIT = V7X_VMEM_BYTES - 8 * 1024 * 1024

SLAB_ROWS = 3072
HALF_ROWS = SLAB_ROWS // 2
SMALL_ROWS = 8


def _dot(a, b):
    return jnp.dot(a, b, preferred_element_type=F32)


def _dot_nt(a, b):
    return lax.dot_general(a, b, (((1,), (1,)), ((), ())), preferred_element_type=F32)


def _dot_tn(a, b):
    return lax.dot_general(a, b, (((0,), (0,)), ((), ())), preferred_element_type=F32)


def _sigmoid(x):
    return 1.0 / (1.0 + jnp.exp(-x))


def _const_spec(shape):
    return pl.BlockSpec(shape, lambda *_: (0,) * len(shape), pipeline_mode=pl.Buffered(1))


def _acc_spec(shape):
    return pl.BlockSpec(shape, lambda *_: (0,) * len(shape))


def _params(n_axes):
    return pltpu.CompilerParams(dimension_semantics=("arbitrary",) * n_axes, vmem_limit_bytes=VMEM_LIMIT)


def _row_tile(t, want):
    for r in (want, 384, 192, 128, 64):
        if r <= want and t % r == 0:
            return r
    raise ValueError(f"no row tile for {t}")


class _Plan(NamedTuple):
    arrays: Sequence[Any]
    out_shape: Sequence[Any]
    sems: Sequence[Any]
    make: Callable


def _call(body, *, name, grid, in_specs, out_specs, out_shape, scratch_shapes=(), plan=None):
    n_in, n_out, n_scr = len(in_specs), len(out_specs), len(scratch_shapes)
    if plan is None:
        plan = _Plan([], [], [], lambda ins, outs, sems: (lambda: None, lambda: None))
    nx_in, nx_out = len(plan.arrays), len(plan.out_shape)

    def hosted(*refs):
        ins, xins = refs[:n_in], refs[n_in:n_in + nx_in]
        o0 = n_in + nx_in
        outs, xouts = refs[o0:o0 + n_out], refs[o0 + n_out:o0 + n_out + nx_out]
        s0 = o0 + n_out + nx_out
        scr, sems = refs[s0:s0 + n_scr], refs[s0 + n_scr:]
        ids = [pl.program_id(a) for a in range(len(grid))]
        first = functools.reduce(jnp.logical_and, [i == 0 for i in ids])
        last = functools.reduce(jnp.logical_and, [i == g - 1 for i, g in zip(ids, grid)])
        start, finish = plan.make(xins, xouts, sems)
        pl.when(first)(start)
        body(*ins, *outs, *scr)
        pl.when(last)(finish)

    call = pl.pallas_call(
        hosted, name=name, grid=grid, in_specs=list(in_specs) + [HBM_SPEC] * nx_in,
        out_specs=list(out_specs) + [HBM_SPEC] * nx_out, out_shape=list(out_shape) + list(plan.out_shape),
        scratch_shapes=list(scratch_shapes) + list(plan.sems),
        compiler_params=pltpu.CompilerParams(dimension_semantics=("arbitrary",) * len(grid),
                                             vmem_limit_bytes=VMEM_LIMIT, has_side_effects=nx_in > 0))

    def run(*args):
        res = call(*args, *plan.arrays)
        return res[:n_out], res[n_out:]

    return run


def _in_proj(h0, g_mix, w_in, plan=None):
    t = h0.shape[0]
    r = _row_tile(t, 384)

    def body(h_ref, g_ref, w_ref, u_ref, hn_ref):
        h = h_ref[...]
        rstd = lax.rsqrt(jnp.mean(h * h, axis=-1, keepdims=True) + RMS_EPS)
        hn = (h * rstd * g_ref[...]).astype(BF16)
        hn_ref[...] = hn
        u_ref[...] = _dot(hn, w_ref[...])

    return _call(
        body, name="in_proj", grid=(t // r,),
        in_specs=[pl.BlockSpec((r, D), lambda i: (i, 0)), _const_spec((1, D)), _const_spec((D, D_IN_PAD))],
        out_specs=[pl.BlockSpec((r, D_IN_PAD), lambda i: (i, 0)), pl.BlockSpec((r, D), lambda i: (i, 0))],
        out_shape=[jax.ShapeDtypeStruct((t, D_IN_PAD), F32), jax.ShapeDtypeStruct((t, D), BF16)],
        plan=plan,
    )(h0, g_mix, w_in)


CONV_TILE = 192
CONV_SUB = 32
CONV_LEAD = CONV_SUB - (CONV_W - 1)
SUBLANES = 8


def _shifted_copies(src, dst, r):
    for s in range(1, SUBLANES):
        dst[s - 1] = src[s:s + r + CONV_SUB - SUBLANES, :]


def _shifted_rows(src, shifted, start):
    base, s = SUBLANES * (start // SUBLANES), start % SUBLANES
    if s == 0:
        return src[base:base + CONV_SUB, :]
    return shifted[s - 1, base:base + CONV_SUB, :]


def _conv_fwd(u, conv_w, conv_b, ln_g, ln_b, n_ex, lp, plan=None):
    r = CONV_TILE
    nt = lp // r
    hb = r // CONV_SUB

    def body(cur_ref, prev_ref, w_ref, b_ref, lg_ref, lb_ref, yc_ref, y_ref, glu, glu_sh):
        i = pl.program_id(1)
        cur = cur_ref[...]
        glu[CONV_SUB:CONV_SUB + r, :] = cur[:, :C_CONV] * _sigmoid(cur[:, C_CONV:])
        pv = prev_ref[...]
        halo = pv[:, :C_CONV] * _sigmoid(pv[:, C_CONV:])
        glu[0:CONV_SUB, :] = jnp.where(i > 0, halo, 0.0)
        _shifted_copies(glu, glu_sh, r)
        w = w_ref[...]
        for j in range(r // CONV_SUB):
            r0 = j * CONV_SUB
            acc = jnp.zeros((CONV_SUB, C_CONV), F32) + b_ref[...]
            for k in range(CONV_W):
                acc = acc + w[k:k + 1, :] * _shifted_rows(glu, glu_sh, r0 + CONV_LEAD + k)
            mu = jnp.mean(acc, axis=-1, keepdims=True)
            cen = acc - mu
            var = jnp.mean(cen * cen, axis=-1, keepdims=True)
            out = cen * lax.rsqrt(var + LN_EPS) * lg_ref[...] + lb_ref[...]
            y = out * _sigmoid(out)
            row = i * r + r0 + lax.broadcasted_iota(jnp.int32, (CONV_SUB, 1), 0)
            y = jnp.where(row >= PAD_ROWS, y, 0.0)
            yc_ref[r0:r0 + CONV_SUB, :] = acc
            y_ref[r0:r0 + CONV_SUB, :] = y.astype(BF16)

    t = n_ex * lp
    return _call(
        body, name="conv_fwd", grid=(n_ex, nt),
        in_specs=[pl.BlockSpec((r, 2 * C_CONV), lambda b, i: (b * nt + i, 0)),
                  pl.BlockSpec((CONV_SUB, 2 * C_CONV), lambda b, i: (jnp.maximum((b * nt + i) * hb - 1, 0), 0)),
                  _const_spec((32, C_CONV)), _const_spec((1, C_CONV)), _const_spec((1, C_CONV)), _const_spec((1, C_CONV))],
        out_specs=[pl.BlockSpec((r, C_CONV), lambda b, i: (b * nt + i, 0)),
                   pl.BlockSpec((r, C_CONV), lambda b, i: (b * nt + i, 0))],
        out_shape=[jax.ShapeDtypeStruct((t, C_CONV), F32), jax.ShapeDtypeStruct((t, C_CONV), BF16)],
        scratch_shapes=[pltpu.VMEM((r + CONV_SUB, C_CONV), F32),
                        pltpu.VMEM((SUBLANES - 1, r + CONV_SUB - SUBLANES, C_CONV), F32)],
        plan=plan,
    )(u, u, conv_w, conv_b, ln_g, ln_b)


def _gla_gates(lr, w2, gb, first_chunk):
    z = _dot(lr.astype(BF16), w2) + gb
    a = (jnp.minimum(z, 0.0) - jnp.log(1.0 + jnp.exp(-jnp.abs(z)))) * (1.0 / GATE_TAU)
    row = lax.broadcasted_iota(jnp.int32, (CHUNK, 1), 0)
    live = jnp.logical_or(jnp.logical_not(first_chunk), row >= PAD_ROWS)
    return z, jnp.where(live, a, 0.0), live


def _tri(lower):
    i = lax.broadcasted_iota(jnp.int32, (CHUNK, CHUNK), 0)
    j = lax.broadcasted_iota(jnp.int32, (CHUNK, CHUNK), 1)
    return (i >= j) if lower else (i <= j)


def _gla_fwd_per_head(u, w2, gb, ng, n_ex, lp, plan=None):
    nc = lp // CHUNK
    t = n_ex * lp

    def body(qk_ref, v_ref, g_ref, lr_ref, w2_ref, gb_ref, ng_ref, y_ref, st_ref, state):
        n = pl.program_id(0)

        @pl.when(n == 0)
        def _():
            state[...] = jnp.zeros_like(state)

        causal = _tri(True)
        for e in range(n_ex):
            st = state[e]
            st_ref[e] = st
            qk = qk_ref[e]
            q, k = qk[:, :GLA_K], qk[:, GLA_K:]
            _, a, _ = _gla_gates(lr_ref[e], w2_ref[...], gb_ref[...], n == 0)
            b = jnp.dot(causal.astype(F32), a, preferred_element_type=F32, precision=lax.Precision.HIGHEST)
            bl = b[CHUNK - 1:CHUNK, :]
            q_in = (q * (DK ** -0.5) * jnp.exp(b)).astype(BF16)
            k_in = (k * jnp.exp(-b)).astype(BF16)
            k_dec = (k * jnp.exp(bl - b)).astype(BF16)
            decay = jnp.exp(bl)
            v = v_ref[e]
            g = g_ref[e]
            st_b = st.astype(BF16)
            ys, new = [], []
            for h in range(N_HEADS):
                ks = slice(h * DK, (h + 1) * DK)
                vs = slice(h * DV, (h + 1) * DV)
                vh = v[:, vs].astype(BF16)
                s = jnp.where(causal, _dot_nt(q_in[:, ks], k_in[:, ks]), 0.0)
                o = _dot(s.astype(BF16), vh) + _dot_nt(q_in[:, ks], st_b[:, ks])
                new.append(decay[:, ks] * st[:, ks] + _dot_tn(vh, k_dec[:, ks]))
                rstd = lax.rsqrt(jnp.mean(o * o, axis=-1, keepdims=True) + RMS_EPS)
                gh = g[:, vs]
                ys.append(o * rstd * ng_ref[...] * (gh * _sigmoid(gh)))
            state[e] = jnp.concatenate(new, axis=1)
            y_ref[e] = jnp.concatenate(ys, axis=1).astype(BF16)

    u3 = u.reshape(n_ex, lp, D_IN_PAD)
    blk = lambda w, col: pl.BlockSpec((n_ex, CHUNK, w), lambda n: (0, n, col))
    (y, states), extra = _call(
        body, name="gla_fwd", grid=(nc,),
        in_specs=[blk(2 * GLA_K, 2), blk(GLA_V, 3), blk(GLA_V, 4), blk(128, 20),
                  _const_spec((128, GLA_K)), _const_spec((1, GLA_K)), _const_spec((1, DV))],
        out_specs=[blk(GLA_V, 0), pl.BlockSpec((n_ex, DV, GLA_K), lambda n: (0, n, 0))],
        out_shape=[jax.ShapeDtypeStruct((n_ex, lp, GLA_V), BF16),
                   jax.ShapeDtypeStruct((n_ex, nc * DV, GLA_K), F32)],
        scratch_shapes=[pltpu.VMEM((n_ex, DV, GLA_K), F32)],
        plan=plan,
    )(u3, u3, u3, u3, w2, gb, ng)
    return (y.reshape(t, GLA_V), states), extra


FFN_TILE = 192


def _mix_out_ffn_up(h0, y_conv, y_gla, w_out, g_ffn, w_gate, w_up):
    t = h0.shape[0]
    r = _row_tile(t, FFN_TILE)

    def body(h0_ref, yc_ref, yg_ref, wo_ref, g_ref, wg_ref, wu_ref, h1_ref, hn_ref, gate_ref, up_ref, act_ref):
        h1 = h0_ref[...] + _dot(yc_ref[...], wo_ref[0:C_CONV, :]) + _dot(yg_ref[...], wo_ref[C_CONV:D, :])
        h1_ref[...] = h1
        rstd = lax.rsqrt(jnp.mean(h1 * h1, axis=-1, keepdims=True) + RMS_EPS)
        hn = (h1 * rstd * g_ref[...]).astype(BF16)
        hn_ref[...] = hn
        gate = _dot(hn, wg_ref[...])
        up = _dot(hn, wu_ref[...])
        gate_ref[...] = gate
        up_ref[...] = up
        act_ref[...] = (gate * _sigmoid(gate) * up).astype(BF16)

    rows = lambda w: pl.BlockSpec((r, w), lambda i: (i, 0))
    return pl.pallas_call(
        body, name="mix_out_ffn_up", grid=(t // r,),
        in_specs=[rows(D), rows(C_CONV), rows(GLA_V), _const_spec((D, D)), _const_spec((1, D)),
                  _const_spec((D, D_FF)), _const_spec((D, D_FF))],
        out_specs=[rows(D), rows(D), rows(D_FF), rows(D_FF), rows(D_FF)],
        out_shape=[jax.ShapeDtypeStruct((t, D), F32), jax.ShapeDtypeStruct((t, D), BF16),
                   jax.ShapeDtypeStruct((t, D_FF), F32), jax.ShapeDtypeStruct((t, D_FF), F32),
                   jax.ShapeDtypeStruct((t, D_FF), BF16)],
        compiler_params=_params(1),
    )(h0, y_conv, y_gla, w_out, g_ffn, w_gate, w_up)


def _ffn_down_loss(act, w_down, h1, target, g_final, row_mask):
    t = h1.shape[0]
    r = _row_tile(t, 384)

    def body(act_ref, wd_ref, h1_ref, tgt_ref, gf_ref, mask_ref, dh2_ref, loss_ref, dgf_ref):
        @pl.when(pl.program_id(0) == 0)
        def _():
            loss_ref[...] = jnp.zeros_like(loss_ref)
            dgf_ref[...] = jnp.zeros_like(dgf_ref)

        h2 = h1_ref[...] + _dot(act_ref[...], wd_ref[...])
        rstd = lax.rsqrt(jnp.mean(h2 * h2, axis=-1, keepdims=True) + RMS_EPS)
        nrm = h2 * rstd
        gf = gf_ref[...]
        err = (nrm * gf - tgt_ref[...]) * mask_ref[...]
        loss_ref[...] += jnp.sum(err * err) * (0.5 / D)
        dy = err * (1.0 / D)
        dgf_ref[...] += jnp.sum(dy * nrm, axis=0, keepdims=True)
        dn = dy * gf
        dh2_ref[...] = rstd * (dn - nrm * jnp.mean(dn * nrm, axis=-1, keepdims=True))

    rows = lambda w: pl.BlockSpec((r, w), lambda i: (i, 0))
    return pl.pallas_call(
        body, name="ffn_down_loss", grid=(t // r,),
        in_specs=[rows(D_FF), _const_spec((D_FF, D)), rows(D), rows(D), _const_spec((1, D)), rows(1)],
        out_specs=[rows(D), _acc_spec((1, 128)), _acc_spec((1, D))],
        out_shape=[jax.ShapeDtypeStruct((t, D), F32), jax.ShapeDtypeStruct((1, 128), F32),
                   jax.ShapeDtypeStruct((1, D), F32)],
        compiler_params=_params(1),
    )(act, w_down, h1, target, g_final, row_mask)


def _ffn_bwd(dh2, gate, up, h1, w_down_t, w_gate_t, w_up_t, w_out_t, g_ffn):
    t = h1.shape[0]
    r = _row_tile(t, FFN_TILE)

    def body(dh2_ref, gate_ref, up_ref, h1_ref, wd_ref, wg_ref, wu_ref, wo_ref, g_ref,
             dgate_ref, dup_ref, dh1_ref, dycat_ref, dg_ref):
        @pl.when(pl.program_id(0) == 0)
        def _():
            dg_ref[...] = jnp.zeros_like(dg_ref)

        dh2 = dh2_ref[...]
        dact = _dot(dh2.astype(BF16), wd_ref[...])
        gate = gate_ref[...]
        sg = _sigmoid(gate)
        dgate = (dact * up_ref[...] * (sg * (1.0 + gate * (1.0 - sg)))).astype(BF16)
        dup = (dact * (gate * sg)).astype(BF16)
        dgate_ref[...] = dgate
        dup_ref[...] = dup
        dhn = _dot(dgate, wg_ref[...]) + _dot(dup, wu_ref[...])
        h1 = h1_ref[...]
        rstd = lax.rsqrt(jnp.mean(h1 * h1, axis=-1, keepdims=True) + RMS_EPS)
        nrm = h1 * rstd
        dg_ref[...] += jnp.sum(dhn * nrm, axis=0, keepdims=True)
        dn = dhn * g_ref[...]
        dh1 = dh2 + rstd * (dn - nrm * jnp.mean(dn * nrm, axis=-1, keepdims=True))
        dh1_ref[...] = dh1
        dycat_ref[...] = _dot(dh1.astype(BF16), wo_ref[...])

    rows = lambda w: pl.BlockSpec((r, w), lambda i: (i, 0))
    return pl.pallas_call(
        body, name="ffn_bwd", grid=(t // r,),
        in_specs=[rows(D), rows(D_FF), rows(D_FF), rows(D), _const_spec((D, D_FF)), _const_spec((D_FF, D)),
                  _const_spec((D_FF, D)), _const_spec((D, D)), _const_spec((1, D))],
        out_specs=[rows(D_FF), rows(D_FF), rows(D), rows(D), _acc_spec((1, D))],
        out_shape=[jax.ShapeDtypeStruct((t, D_FF), BF16), jax.ShapeDtypeStruct((t, D_FF), BF16),
                   jax.ShapeDtypeStruct((t, D), F32), jax.ShapeDtypeStruct((t, D), F32),
                   jax.ShapeDtypeStruct((1, D), F32)],
        compiler_params=_params(1),
    )(dh2, gate, up, h1, w_down_t, w_gate_t, w_up_t, w_out_t, g_ffn)


def _conv_bwd(dycat, yc, u, conv_w, ln_g, ln_b, n_ex, lp, plan=None):
    r = CONV_TILE
    nt = lp // r
    hb = r // CONV_SUB
    nsub = r // CONV_SUB

    def ln_bwd(dy, yc_rows, live, lg, lb):
        mu = jnp.mean(yc_rows, axis=-1, keepdims=True)
        cen = yc_rows - mu
        rs = lax.rsqrt(jnp.mean(cen * cen, axis=-1, keepdims=True) + LN_EPS)
        yn = cen * rs
        out = yn * lg + lb
        so = _sigmoid(out)
        dout = jnp.where(live, dy * (so * (1.0 + out * (1.0 - so))), 0.0)
        dyn = dout * lg
        dyc = rs * (dyn - jnp.mean(dyn, axis=-1, keepdims=True) - yn * jnp.mean(dyn * yn, axis=-1, keepdims=True))
        return dyc, dout, yn

    def body(dy_ref, dyn_ref, yc_ref, ycn_ref, cur_ref, prev_ref, w_ref, lg_ref, lb_ref,
             du_ref, dw_ref, db_ref, dlg_ref, dlb_ref, glu, dycs, dwacc, glu_sh, dycs_sh):
        b = pl.program_id(0)
        i = pl.program_id(1)
        first = jnp.logical_and(b == 0, i == 0)

        @pl.when(first)
        def _():
            dwacc[...] = jnp.zeros_like(dwacc)
            db_ref[...] = jnp.zeros_like(db_ref)
            dlg_ref[...] = jnp.zeros_like(dlg_ref)
            dlb_ref[...] = jnp.zeros_like(dlb_ref)

        lg, lb = lg_ref[...], lb_ref[...]
        cur = cur_ref[...]
        sig = _sigmoid(cur[:, C_CONV:])
        glu[CONV_SUB:CONV_SUB + r, :] = cur[:, :C_CONV] * sig
        pv = prev_ref[...]
        glu[0:CONV_SUB, :] = jnp.where(i > 0, pv[:, :C_CONV] * _sigmoid(pv[:, C_CONV:]), 0.0)

        row = i * r + lax.broadcasted_iota(jnp.int32, (r, 1), 0)
        dyc, dout, yn = ln_bwd(dy_ref[...], yc_ref[...], row >= PAD_ROWS, lg, lb)
        dycs[0:r, :] = dyc
        dycn, _, _ = ln_bwd(dyn_ref[...], ycn_ref[...], i < nt - 1, lg, lb)
        dycs[r:r + CONV_SUB, :] = dycn
        db_ref[...] += jnp.sum(dyc, axis=0, keepdims=True)
        dlg_ref[...] += jnp.sum(dout * yn, axis=0, keepdims=True)
        dlb_ref[...] += jnp.sum(dout, axis=0, keepdims=True)

        _shifted_copies(glu, glu_sh, r)
        _shifted_copies(dycs, dycs_sh, r)
        w = w_ref[...]
        for j in range(nsub):
            r0 = j * CONV_SUB
            dblk = dycs[r0:r0 + CONV_SUB, :]
            dglu = jnp.zeros((CONV_SUB, C_CONV), F32)
            for k in range(CONV_W):
                dglu = dglu + w[k:k + 1, :] * _shifted_rows(dycs, dycs_sh, r0 + (CONV_W - 1) - k)
                prod = dblk * _shifted_rows(glu, glu_sh, r0 + CONV_LEAD + k)
                dwacc[k] += prod.reshape(CONV_SUB // SUBLANES, SUBLANES, C_CONV).sum(axis=0)
            sg = sig[r0:r0 + CONV_SUB, :]
            cv = cur[r0:r0 + CONV_SUB, :C_CONV]
            du_ref[r0:r0 + CONV_SUB, :C_CONV] = (dglu * sg).astype(BF16)
            du_ref[r0:r0 + CONV_SUB, C_CONV:] = (dglu * cv * sg * (1.0 - sg)).astype(BF16)

        @pl.when(jnp.logical_and(b == n_ex - 1, i == nt - 1))
        def _():
            dw_ref[...] = jnp.sum(dwacc[...], axis=1)

    t = n_ex * lp
    cur_rows = lambda w, col: pl.BlockSpec((r, w), lambda b, i: (b * nt + i, col))
    nxt_rows = lambda w, col: pl.BlockSpec(
        (CONV_SUB, w), lambda b, i: (jnp.minimum((b * nt + i + 1) * hb, n_ex * nt * hb - 1), col))
    return _call(
        body, name="conv_bwd", grid=(n_ex, nt),
        in_specs=[cur_rows(C_CONV, 0), nxt_rows(C_CONV, 0), cur_rows(C_CONV, 0), nxt_rows(C_CONV, 0),
                  cur_rows(2 * C_CONV, 0),
                  pl.BlockSpec((CONV_SUB, 2 * C_CONV), lambda b, i: (jnp.maximum((b * nt + i) * hb - 1, 0), 0)),
                  _const_spec((32, C_CONV)), _const_spec((1, C_CONV)), _const_spec((1, C_CONV))],
        out_specs=[cur_rows(2 * C_CONV, 0), _acc_spec((32, C_CONV)), _acc_spec((1, C_CONV)),
                   _acc_spec((1, C_CONV)), _acc_spec((1, C_CONV))],
        out_shape=[jax.ShapeDtypeStruct((t, 2 * C_CONV), BF16), jax.ShapeDtypeStruct((32, C_CONV), F32),
                   jax.ShapeDtypeStruct((1, C_CONV), F32), jax.ShapeDtypeStruct((1, C_CONV), F32),
                   jax.ShapeDtypeStruct((1, C_CONV), F32)],
        scratch_shapes=[pltpu.VMEM((r + CONV_SUB, C_CONV), F32), pltpu.VMEM((r + CONV_SUB, C_CONV), F32),
                        pltpu.VMEM((32, 8, C_CONV), F32),
                        pltpu.VMEM((SUBLANES - 1, r + CONV_SUB - SUBLANES, C_CONV), F32),
                        pltpu.VMEM((SUBLANES - 1, r + CONV_SUB - SUBLANES, C_CONV), F32)],
        plan=plan,
    )(dycat, dycat, yc, yc, u, u, conv_w, ln_g, ln_b)


def _gla_bwd_per_head(dycat, u, states, w2, gb, ng, n_ex, lp, plan=None):
    nc = lp // CHUNK
    t = n_ex * lp

    def body(dy_ref, qk_ref, v_ref, g_ref, lr_ref, st_ref, w2_ref, gb_ref, ng_ref,
             du_ref, dw2_ref, dgb_ref, dng_ref, dstate):
        n = pl.program_id(0)
        chunk = nc - 1 - n

        @pl.when(n == 0)
        def _():
            dw2_ref[...] = jnp.zeros_like(dw2_ref)
            dgb_ref[...] = jnp.zeros_like(dgb_ref)
            dng_ref[...] = jnp.zeros_like(dng_ref)
            dstate[...] = jnp.zeros_like(dstate)

        for e in range(n_ex):
            one_example(e, chunk, dy_ref, qk_ref, v_ref, g_ref, lr_ref, st_ref, w2_ref, gb_ref, ng_ref,
                        du_ref, dw2_ref, dgb_ref, dng_ref, dstate)

    def one_example(e, chunk, dy_ref, qk_ref, v_ref, g_ref, lr_ref, st_ref, w2_ref, gb_ref, ng_ref,
                    du_ref, dw2_ref, dgb_ref, dng_ref, dstate):
        dy_ref, qk_ref, v_ref, g_ref, lr_ref, st_ref = (r.at[e] for r in (dy_ref, qk_ref, v_ref, g_ref, lr_ref, st_ref))
        du_ref, dstate = du_ref.at[e], dstate.at[e]
        qk = qk_ref[...]
        q, k = qk[:, :GLA_K], qk[:, GLA_K:]
        lr = lr_ref[...]
        z, a, live = _gla_gates(lr, w2_ref[...], gb_ref[...], chunk == 0)
        causal = _tri(True)
        b = jnp.dot(causal.astype(F32), a, preferred_element_type=F32, precision=lax.Precision.HIGHEST)
        bl = b[CHUNK - 1:CHUNK, :]
        e_pos, e_neg, e_dec = jnp.exp(b), jnp.exp(-b), jnp.exp(bl - b)
        q_f = q * (DK ** -0.5) * e_pos
        k_f = k * e_neg
        kd_f = k * e_dec
        q_in, k_in, k_dec = q_f.astype(BF16), k_f.astype(BF16), kd_f.astype(BF16)
        decay = jnp.exp(bl)
        v = v_ref[...]
        g = g_ref[...]
        dy = dy_ref[...]
        ngv = ng_ref[...]
        st = st_ref[...]
        st_b = st.astype(BF16)
        dst = dstate[...]
        dst_b = dst.astype(BF16)
        dqs, dks, dvs, dgs, dbs, dbls, new_dst = [], [], [], [], [], [], []
        dng = jnp.zeros((1, DV), F32)
        for h in range(N_HEADS):
            ks = slice(h * DK, (h + 1) * DK)
            vs = slice(h * DV, (h + 1) * DV)
            qh, kh, kdh = q_in[:, ks], k_in[:, ks], k_dec[:, ks]
            vh = v[:, vs].astype(BF16)
            s = jnp.where(causal, _dot_nt(qh, kh), 0.0).astype(BF16)
            o = _dot(s, vh) + _dot_nt(qh, st_b[:, ks])
            rstd = lax.rsqrt(jnp.mean(o * o, axis=-1, keepdims=True) + RMS_EPS)
            nrm = o * rstd
            gh = g[:, vs]
            sg = _sigmoid(gh)
            dyh = dy[:, vs]
            dgs.append(dyh * nrm * ngv * (sg * (1.0 + gh * (1.0 - sg))))
            dt = dyh * (gh * sg)
            dng = dng + jnp.sum(dt * nrm, axis=0, keepdims=True)
            dn = dt * ngv
            do = (rstd * (dn - nrm * jnp.mean(dn * nrm, axis=-1, keepdims=True))).astype(BF16)
            da = jnp.where(causal, _dot_nt(do, vh), 0.0).astype(BF16)
            dvs.append(_dot_tn(s, do) + _dot_nt(kdh, dst_b[:, ks]))
            dq_in = _dot(da, kh) + _dot(do, st_b[:, ks])
            dk_in = _dot_tn(da, qh)
            dk_dec = _dot(vh, dst_b[:, ks])
            new_dst.append(_dot_tn(do, qh) + decay[:, ks] * dst[:, ks])
            dbls.append(jnp.sum(dk_dec * kd_f[:, ks], axis=0, keepdims=True)
                        + decay[:, ks] * jnp.sum(dst[:, ks] * st[:, ks], axis=0, keepdims=True))
            dqs.append(dq_in * (DK ** -0.5) * e_pos[:, ks])
            dks.append(dk_in * e_neg[:, ks] + dk_dec * e_dec[:, ks])
            dbs.append(dq_in * q_f[:, ks] - dk_in * k_f[:, ks] - dk_dec * kd_f[:, ks])
        dstate[...] = jnp.concatenate(new_dst, axis=1)
        row = lax.broadcasted_iota(jnp.int32, (CHUNK, 1), 0)
        db = jnp.concatenate(dbs, axis=1) + jnp.where(row == CHUNK - 1, jnp.concatenate(dbls, axis=1), 0.0)
        da_log = jnp.dot(_tri(False).astype(F32), db, preferred_element_type=F32, precision=lax.Precision.HIGHEST)
        dz = jnp.where(live, da_log * (1.0 - _sigmoid(z)) * (1.0 / GATE_TAU), 0.0)
        dz_b = dz.astype(BF16)
        du_ref[:, 0:GLA_K] = jnp.concatenate(dqs, axis=1).astype(BF16)
        du_ref[:, GLA_K:2 * GLA_K] = jnp.concatenate(dks, axis=1).astype(BF16)
        du_ref[:, 2 * GLA_K:2 * GLA_K + GLA_V] = jnp.concatenate(dvs, axis=1).astype(BF16)
        du_ref[:, 2 * GLA_K + GLA_V:2 * GLA_K + 2 * GLA_V] = jnp.concatenate(dgs, axis=1).astype(BF16)
        du_ref[:, 2 * GLA_K + 2 * GLA_V:] = _dot_nt(dz_b, w2_ref[...]).astype(BF16)
        dw2_ref[...] += _dot_tn(lr.astype(BF16), dz_b)
        dgb_ref[...] += jnp.sum(dz, axis=0, keepdims=True)
        dng_ref[...] += dng

    u3 = u.reshape(n_ex, lp, D_IN_PAD)
    rev = lambda w, col: pl.BlockSpec((n_ex, CHUNK, w), lambda n: (0, nc - 1 - n, col))
    (du, d_w2, d_gb, d_ng), extra = _call(
        body, name="gla_bwd", grid=(nc,),
        in_specs=[rev(GLA_V, 1), rev(2 * GLA_K, 2), rev(GLA_V, 3), rev(GLA_V, 4), rev(128, 20),
                  pl.BlockSpec((n_ex, DV, GLA_K), lambda n: (0, nc - 1 - n, 0)),
                  _const_spec((128, GLA_K)), _const_spec((1, GLA_K)), _const_spec((1, DV))],
        out_specs=[rev(D_GLA_IN, 0), _acc_spec((128, GLA_K)), _acc_spec((1, GLA_K)), _acc_spec((1, DV))],
        out_shape=[jax.ShapeDtypeStruct((n_ex, lp, D_GLA_IN), BF16), jax.ShapeDtypeStruct((128, GLA_K), F32),
                   jax.ShapeDtypeStruct((1, GLA_K), F32), jax.ShapeDtypeStruct((1, DV), F32)],
        scratch_shapes=[pltpu.VMEM((n_ex, DV, GLA_K), F32)],
        plan=plan,
    )(dycat.reshape(n_ex, lp, D), u3, u3, u3, u3, states, w2, gb, ng)
    return (du.reshape(t, D_GLA_IN), d_w2, d_gb, d_ng), extra


HEAD_ROWS_ALL = N_HEADS * CHUNK


def _head_of(shape, axis, per_head):
    return lax.broadcasted_iota(jnp.int32, shape, axis) // per_head


def _expand(x, lanes_per_head):
    rows, lanes = HEAD_ROWS_ALL, x.shape[1]
    keep = _head_of((rows, lanes), 0, CHUNK) == _head_of((rows, lanes), 1, lanes_per_head)
    return jnp.where(keep, jnp.tile(x, (N_HEADS, 1)), 0.0)


def _expand_lanes(x):
    rows, w = x.shape
    keep = _head_of((rows, N_HEADS * w), 0, CHUNK) == _head_of((rows, N_HEADS * w), 1, w)
    return jnp.where(keep, jnp.tile(x, (1, N_HEADS)), 0.0)


def _expand_state(st):
    rows, lanes = N_HEADS * DV, st.shape[1]
    keep = _head_of((rows, lanes), 0, DV) == _head_of((rows, lanes), 1, DK)
    return jnp.where(keep, jnp.tile(st, (N_HEADS, 1)), 0.0)


def _fold(t, rows_per_head):
    lane_head = _head_of((rows_per_head, t.shape[1]), 1, DK)
    out = jnp.where(lane_head == 0, t[0:rows_per_head], 0.0)
    for h in range(1, N_HEADS):
        out = out + jnp.where(lane_head == h, t[h * rows_per_head:(h + 1) * rows_per_head], 0.0)
    return out


def _rows_by_head(x):
    return jnp.concatenate([x[:, h * DV:(h + 1) * DV] for h in range(N_HEADS)], axis=0)


def _lanes_by_head(x):
    return jnp.concatenate([x[h * CHUNK:(h + 1) * CHUNK] for h in range(N_HEADS)], axis=1)


def _running_sum(a, lower):
    hi = a.astype(BF16)
    rest = a - hi.astype(F32)
    mid = rest.astype(BF16)
    lo = (rest - mid.astype(F32)).astype(BF16)
    w = a.shape[1]
    parts = _dot(_tri(lower).astype(F32).astype(BF16), jnp.concatenate([hi, mid, lo], axis=1))
    return parts[:, :w] + parts[:, w:2 * w] + parts[:, 2 * w:]


def _stacked_causal():
    i = lax.broadcasted_iota(jnp.int32, (HEAD_ROWS_ALL, CHUNK), 0) % CHUNK
    j = lax.broadcasted_iota(jnp.int32, (HEAD_ROWS_ALL, CHUNK), 1)
    return i >= j


def _gla_chunk(q, k, v, lr, st, w2, gb, first_chunk):
    z, a, live = _gla_gates(lr, w2, gb, first_chunk)
    b = _running_sum(a, True)
    bl = b[CHUNK - 1:CHUNK, :]
    e_pos, e_neg, e_dec = jnp.exp(b), jnp.exp(-b), jnp.exp(bl - b)
    q_f, k_f, kd_f = q * (DK ** -0.5) * e_pos, k * e_neg, k * e_dec
    qx = _expand(q_f, DK).astype(BF16)
    k_in, k_dec, v_b = k_f.astype(BF16), kd_f.astype(BF16), v.astype(BF16)
    s = jnp.where(_stacked_causal(), _dot_nt(qx, k_in), 0.0).astype(BF16)
    p = _dot(s, v_b)
    o = (jnp.concatenate([p[h * CHUNK:(h + 1) * CHUNK, h * DV:(h + 1) * DV] for h in range(N_HEADS)], axis=0)
         + _dot_nt(qx, st.astype(BF16)))
    return dict(z=z, live=live, bl=bl, e_pos=e_pos, e_neg=e_neg, e_dec=e_dec, q_f=q_f, k_f=k_f, kd_f=kd_f,
                qx=qx, k_in=k_in, k_dec=k_dec, v_b=v_b, s=s, o=o, decay=jnp.exp(bl))


def _gla_fwd(u, w2, gb, ng, n_ex, lp, plan=None):
    nc = lp // CHUNK
    t = n_ex * lp

    def body(qk_ref, v_ref, g_ref, lr_ref, w2_ref, gb_ref, ng_ref, y_ref, st_ref, state):
        n = pl.program_id(0)

        @pl.when(n == 0)
        def _():
            state[...] = jnp.zeros_like(state)

        for e in range(n_ex):
            st = state[e]
            st_ref[e] = st
            qk = qk_ref[e]
            c = _gla_chunk(qk[:, :GLA_K], qk[:, GLA_K:], v_ref[e], lr_ref[e], st, w2_ref[...], gb_ref[...], n == 0)
            o = c["o"]
            rstd = lax.rsqrt(jnp.mean(o * o, axis=-1, keepdims=True) + RMS_EPS)
            g = _rows_by_head(g_ref[e])
            y_ref[e] = _lanes_by_head(o * rstd * ng_ref[...] * (g * _sigmoid(g))).astype(BF16)
            state[e] = c["decay"] * st + _fold(_dot_tn(c["v_b"], c["k_dec"]), DV)

    u3 = u.reshape(n_ex, lp, D_IN_PAD)
    blk = lambda w, col: pl.BlockSpec((n_ex, CHUNK, w), lambda n: (0, n, col))
    (y, states), extra = _call(
        body, name="gla_fwd", grid=(nc,),
        in_specs=[blk(2 * GLA_K, 2), blk(GLA_V, 3), blk(GLA_V, 4), blk(128, 20),
                  _const_spec((128, GLA_K)), _const_spec((1, GLA_K)), _const_spec((1, DV))],
        out_specs=[blk(GLA_V, 0), pl.BlockSpec((n_ex, DV, GLA_K), lambda n: (0, n, 0))],
        out_shape=[jax.ShapeDtypeStruct((n_ex, lp, GLA_V), BF16),
                   jax.ShapeDtypeStruct((n_ex, nc * DV, GLA_K), F32)],
        scratch_shapes=[pltpu.VMEM((n_ex, DV, GLA_K), F32)],
        plan=plan,
    )(u3, u3, u3, u3, w2, gb, ng)
    return (y.reshape(t, GLA_V), states), extra


def _gla_bwd(dycat, u, states, w2, gb, ng, n_ex, lp, plan=None):
    nc = lp // CHUNK
    t = n_ex * lp

    def body(dy_ref, qk_ref, v_ref, g_ref, lr_ref, st_ref, w2_ref, gb_ref, ng_ref,
             du_ref, dw2_ref, dgb_ref, dng_ref, dstate):
        n = pl.program_id(0)
        chunk = nc - 1 - n

        @pl.when(n == 0)
        def _():
            dw2_ref[...] = jnp.zeros_like(dw2_ref)
            dgb_ref[...] = jnp.zeros_like(dgb_ref)
            dng_ref[...] = jnp.zeros_like(dng_ref)
            dstate[...] = jnp.zeros_like(dstate)

        for e in range(n_ex):
            qk = qk_ref[e]
            lr = lr_ref[e]
            st = st_ref[e]
            dst = dstate[e]
            c = _gla_chunk(qk[:, :GLA_K], qk[:, GLA_K:], v_ref[e], lr, st, w2_ref[...], gb_ref[...], chunk == 0)
            qx, k_in, k_dec, v_b, s, o = c["qx"], c["k_in"], c["k_dec"], c["v_b"], c["s"], c["o"]
            ngv = ng_ref[...]
            rstd = lax.rsqrt(jnp.mean(o * o, axis=-1, keepdims=True) + RMS_EPS)
            nrm = o * rstd
            g = _rows_by_head(g_ref[e])
            dy = _rows_by_head(dy_ref[e])
            sg = _sigmoid(g)
            dg = dy * nrm * ngv * (sg * (1.0 + g * (1.0 - sg)))
            dt = dy * (g * sg)
            dng_ref[...] += jnp.sum(dt * nrm, axis=0, keepdims=True)
            dn = dt * ngv
            do = rstd * (dn - nrm * jnp.mean(dn * nrm, axis=-1, keepdims=True))
            do_b = do.astype(BF16)
            dox = _expand_lanes(do).astype(BF16)
            dstx = _expand_state(dst).astype(BF16)
            da = jnp.where(_stacked_causal(), _dot_nt(dox, v_b), 0.0).astype(BF16)
            dv = _dot_tn(s, dox) + _dot_nt(k_dec, dstx)
            dq_in = _fold(_dot(da, k_in) + _dot(do_b, st.astype(BF16)), CHUNK)
            dk_in = _dot_tn(da, qx)
            dk_dec = _dot(v_b, dstx)
            dstate[e] = _dot_tn(do_b, qx) + c["decay"] * dst
            dbl = (jnp.sum(dk_dec * c["kd_f"], axis=0, keepdims=True)
                   + c["decay"] * jnp.sum(dst * st, axis=0, keepdims=True))
            dq = dq_in * (DK ** -0.5) * c["e_pos"]
            dk = dk_in * c["e_neg"] + dk_dec * c["e_dec"]
            db = dq_in * c["q_f"] - dk_in * c["k_f"] - dk_dec * c["kd_f"]
            row = lax.broadcasted_iota(jnp.int32, (CHUNK, 1), 0)
            da_log = _running_sum(db + jnp.where(row == CHUNK - 1, dbl, 0.0), False)
            dz = jnp.where(c["live"], da_log * (1.0 - _sigmoid(c["z"])) * (1.0 / GATE_TAU), 0.0)
            dz_b = dz.astype(BF16)
            out = du_ref.at[e]
            out[:, 0:GLA_K] = dq.astype(BF16)
            out[:, GLA_K:2 * GLA_K] = dk.astype(BF16)
            out[:, 2 * GLA_K:2 * GLA_K + GLA_V] = dv.astype(BF16)
            out[:, 2 * GLA_K + GLA_V:2 * GLA_K + 2 * GLA_V] = _lanes_by_head(dg).astype(BF16)
            out[:, 2 * GLA_K + 2 * GLA_V:] = _dot_nt(dz_b, w2_ref[...]).astype(BF16)
            dw2_ref[...] += _dot_tn(lr.astype(BF16), dz_b)
            dgb_ref[...] += jnp.sum(dz, axis=0, keepdims=True)

    u3 = u.reshape(n_ex, lp, D_IN_PAD)
    rev = lambda w, col: pl.BlockSpec((n_ex, CHUNK, w), lambda n: (0, nc - 1 - n, col))
    (du, d_w2, d_gb, d_ng), extra = _call(
        body, name="gla_bwd", grid=(nc,),
        in_specs=[rev(GLA_V, 1), rev(2 * GLA_K, 2), rev(GLA_V, 3), rev(GLA_V, 4), rev(128, 20),
                  pl.BlockSpec((n_ex, DV, GLA_K), lambda n: (0, nc - 1 - n, 0)),
                  _const_spec((128, GLA_K)), _const_spec((1, GLA_K)), _const_spec((1, DV))],
        out_specs=[rev(D_GLA_IN, 0), _acc_spec((128, GLA_K)), _acc_spec((1, GLA_K)), _acc_spec((1, DV))],
        out_shape=[jax.ShapeDtypeStruct((n_ex, lp, D_GLA_IN), BF16), jax.ShapeDtypeStruct((128, GLA_K), F32),
                   jax.ShapeDtypeStruct((1, GLA_K), F32), jax.ShapeDtypeStruct((1, DV), F32)],
        scratch_shapes=[pltpu.VMEM((n_ex, DV, GLA_K), F32)],
        plan=plan,
    )(dycat.reshape(n_ex, lp, D), u3, u3, u3, u3, states, w2, gb, ng)
    return (du.reshape(t, D_GLA_IN), d_w2, d_gb, d_ng), extra


def _in_proj_bwd(du_conv, du_gla, w_in_t_conv, w_in_t_gla, h0, dh1, g_mix, plan=None):
    t = h0.shape[0]
    r = _row_tile(t, 384)

    def body(dc_ref, dg_ref, wc_ref, wg_ref, h_ref, dh1_ref, g_ref, dh0_ref, dgm_ref):
        @pl.when(pl.program_id(0) == 0)
        def _():
            dgm_ref[...] = jnp.zeros_like(dgm_ref)

        dhn = _dot(dc_ref[...], wc_ref[...]) + _dot(dg_ref[...], wg_ref[...])
        h = h_ref[...]
        rstd = lax.rsqrt(jnp.mean(h * h, axis=-1, keepdims=True) + RMS_EPS)
        nrm = h * rstd
        dgm_ref[...] += jnp.sum(dhn * nrm, axis=0, keepdims=True)
        dn = dhn * g_ref[...]
        dh0_ref[...] = dh1_ref[...] + rstd * (dn - nrm * jnp.mean(dn * nrm, axis=-1, keepdims=True))

    rows = lambda w: pl.BlockSpec((r, w), lambda i: (i, 0))
    return _call(
        body, name="in_proj_bwd", grid=(t // r,),
        in_specs=[rows(2 * C_CONV), rows(D_GLA_IN), _const_spec((2 * C_CONV, D)), _const_spec((D_GLA_IN, D)),
                  rows(D), rows(D), _const_spec((1, D))],
        out_specs=[rows(D), _acc_spec((1, D))],
        out_shape=[jax.ShapeDtypeStruct((t, D), F32), jax.ShapeDtypeStruct((1, D), F32)],
        plan=plan,
    )(du_conv, du_gla, w_in_t_conv, w_in_t_gla, h0, dh1, g_mix)


def _wgrad(x, dy, name):
    t, m = x.shape
    n = dy.shape[1]
    tk = t // 3 if t % (3 * 128) == 0 else _row_tile(t, 384)
    tm = m if m <= D_GLA_IN else m // 2
    tn = n

    def body(x_ref, dy_ref, o_ref):
        @pl.when(pl.program_id(2) == 0)
        def _():
            o_ref[...] = jnp.zeros_like(o_ref)

        o_ref[...] += _dot_tn(x_ref[...].astype(BF16), dy_ref[...].astype(BF16))

    return pl.pallas_call(
        body, name=name, grid=(m // tm, n // tn, t // tk),
        in_specs=[pl.BlockSpec((tk, tm), lambda i, j, k: (k, i)), pl.BlockSpec((tk, tn), lambda i, j, k: (k, j))],
        out_specs=pl.BlockSpec((tm, tn), lambda i, j, k: (i, j)),
        out_shape=jax.ShapeDtypeStruct((m, n), F32),
        compiler_params=_params(3),
    )(x, dy)


def _mesh_pos():
    return lax.axis_index("x"), lax.axis_index("y"), lax.axis_index("c")


def _other_chips(x, y):
    return [(1 - x, y), (x, 1 - y), (1 - x, 1 - y)]


HBM_SPEC = pl.BlockSpec(memory_space=pltpu.HBM)


def _gather_shards(shards):
    n = len(shards)

    def body(*refs):
        ins, outs = refs[:n], refs[n:2 * n]
        send_sems, recv_sems, local_sems = refs[2 * n:]
        x, y, c = _mesh_pos()
        mine = 2 * x + y
        chips = _other_chips(x, y)
        local = [pltpu.make_async_copy(ins[a], outs[a].at[mine], local_sems.at[a]) for a in range(n)]
        for cp in local:
            cp.start()

        def remote(a, k, block):
            px, py = chips[k]
            return pltpu.make_async_remote_copy(
                src_ref=ins[a], dst_ref=outs[a].at[block], send_sem=send_sems.at[3 * a + k],
                recv_sem=recv_sems.at[3 * a + k], device_id=(px, py, c), device_id_type=MESH)

        sends = [remote(a, k, mine) for a in range(n) for k in range(3)]
        for cp in sends:
            cp.start()
        for a in range(n):
            for k, (px, py) in enumerate(chips):
                remote(a, k, 2 * px + py).wait_recv()
        for cp in sends:
            cp.wait_send()
        for cp in local:
            cp.wait()

    return pl.pallas_call(
        body, name="gather_shards",
        in_specs=[HBM_SPEC] * n, out_specs=[HBM_SPEC] * n,
        out_shape=[jax.ShapeDtypeStruct((N_CHIPS,) + s.shape, s.dtype) for s in shards],
        scratch_shapes=[pltpu.SemaphoreType.DMA((3 * n,)), pltpu.SemaphoreType.DMA((3 * n,)),
                        pltpu.SemaphoreType.DMA((n,))],
        compiler_params=pltpu.CompilerParams(has_side_effects=True),
    )(*shards)


def _send_half_to_sibling(g2):
    def body(g_ref, recv_ref, send_sem, recv_sem):
        x, y, c = _mesh_pos()
        cp = pltpu.make_async_remote_copy(
            src_ref=g_ref.at[1 - c], dst_ref=recv_ref, send_sem=send_sem, recv_sem=recv_sem,
            device_id=(x, y, 1 - c), device_id_type=MESH)
        cp.start()
        cp.wait()

    return pl.pallas_call(
        body, name="rs_to_sibling", in_specs=[HBM_SPEC], out_specs=HBM_SPEC,
        out_shape=jax.ShapeDtypeStruct(g2.shape[1:], g2.dtype),
        scratch_shapes=[pltpu.SemaphoreType.DMA(()), pltpu.SemaphoreType.DMA(())],
        compiler_params=pltpu.CompilerParams(has_side_effects=True),
    )(g2)


def _add_own_half(g2, recv, c):
    rows = N_CHIPS * HALF_ROWS
    tr = 512
    g2f = g2.reshape(2, rows, D)
    recvf = recv.reshape(rows, D)

    def body(c_ref, a_ref, b_ref, o_ref):
        o_ref[...] = a_ref[0] + b_ref[...]

    out = pl.pallas_call(
        body, name="rs_add_halves",
        grid_spec=pltpu.PrefetchScalarGridSpec(
            num_scalar_prefetch=1, grid=(rows // tr,),
            in_specs=[pl.BlockSpec((1, tr, D), lambda i, s: (s[0], i, 0)), pl.BlockSpec((tr, D), lambda i, s: (i, 0))],
            out_specs=pl.BlockSpec((tr, D), lambda i, s: (i, 0))),
        out_shape=jax.ShapeDtypeStruct((rows, D), F32),
        compiler_params=_params(1),
    )(jnp.reshape(c, (1,)).astype(jnp.int32), g2f, recvf)
    return out.reshape(N_CHIPS, HALF_ROWS, D)


def _exchange_chip_sums(p):
    def body(p_ref, out_ref, send_sems, recv_sems, local_sem):
        x, y, c = _mesh_pos()
        mine = 2 * x + y
        chips = _other_chips(x, y)
        local = pltpu.make_async_copy(p_ref.at[mine], out_ref.at[mine], local_sem)
        local.start()

        def remote(k, src_block, dst_block):
            px, py = chips[k]
            return pltpu.make_async_remote_copy(
                src_ref=p_ref.at[src_block], dst_ref=out_ref.at[dst_block], send_sem=send_sems.at[k],
                recv_sem=recv_sems.at[k], device_id=(px, py, c), device_id_type=MESH)

        sends = [remote(k, 2 * px + py, mine) for k, (px, py) in enumerate(chips)]
        for cp in sends:
            cp.start()
        for k, (px, py) in enumerate(chips):
            remote(k, mine, 2 * px + py).wait_recv()
        for cp in sends:
            cp.wait_send()
        local.wait()

    return pl.pallas_call(
        body, name="rs_chip_exchange", in_specs=[HBM_SPEC], out_specs=HBM_SPEC,
        out_shape=jax.ShapeDtypeStruct(p.shape, p.dtype),
        scratch_shapes=[pltpu.SemaphoreType.DMA((3,)), pltpu.SemaphoreType.DMA((3,)), pltpu.SemaphoreType.DMA(())],
        compiler_params=pltpu.CompilerParams(has_side_effects=True),
    )(p)


def _sum_chips(parts):
    tr = 512

    def body(p_ref, o_ref):
        o_ref[...] = ((p_ref[0] + p_ref[1]) + p_ref[2]) + p_ref[3]

    return pl.pallas_call(
        body, name="rs_sum_chips", grid=(HALF_ROWS // tr,),
        in_specs=[pl.BlockSpec((N_CHIPS, tr, D), lambda i: (0, i, 0))],
        out_specs=pl.BlockSpec((tr, D), lambda i: (i, 0)),
        out_shape=jax.ShapeDtypeStruct((HALF_ROWS, D), F32),
        compiler_params=_params(1),
    )(parts)


def _share_with_sibling(half):
    def body(h_ref, out_ref, send_sem, recv_sem, local_sem):
        x, y, c = _mesh_pos()
        local = pltpu.make_async_copy(h_ref, out_ref.at[c], local_sem)
        local.start()
        cp = pltpu.make_async_remote_copy(
            src_ref=h_ref, dst_ref=out_ref.at[c], send_sem=send_sem, recv_sem=recv_sem,
            device_id=(x, y, 1 - c), device_id_type=MESH)
        cp.start()
        pltpu.make_async_remote_copy(
            src_ref=h_ref, dst_ref=out_ref.at[1 - c], send_sem=send_sem, recv_sem=recv_sem,
            device_id=(x, y, 1 - c), device_id_type=MESH).wait_recv()
        cp.wait_send()
        local.wait()

    return pl.pallas_call(
        body, name="rs_share_sibling", in_specs=[HBM_SPEC], out_specs=HBM_SPEC,
        out_shape=jax.ShapeDtypeStruct((2,) + half.shape, half.dtype),
        scratch_shapes=[pltpu.SemaphoreType.DMA(()), pltpu.SemaphoreType.DMA(()), pltpu.SemaphoreType.DMA(())],
        compiler_params=pltpu.CompilerParams(has_side_effects=True),
    )(half)


def _adam_update(g, w, m, v):
    m2 = ADAM_B1 * m + (1.0 - ADAM_B1) * g
    v2 = ADAM_B2 * v + (1.0 - ADAM_B2) * (g * g)
    m_hat = m2 / (1.0 - ADAM_B1 ** ADAM_STEP)
    v_hat = v2 / (1.0 - ADAM_B2 ** ADAM_STEP)
    delta = -ADAM_LR * (m_hat / (jnp.sqrt(v_hat) + ADAM_EPS) + ADAM_WD * w)
    return delta, m2, v2


def _adamw_slab(g, w, m, v):
    rows = g.shape[0]
    tr = 256

    def body(g_ref, w_ref, m_ref, v_ref, d_ref, m2_ref, v2_ref):
        d_ref[...], m2_ref[...], v2_ref[...] = _adam_update(g_ref[...], w_ref[...], m_ref[...], v_ref[...])

    spec = pl.BlockSpec((tr, D), lambda i: (i, 0))
    return pl.pallas_call(
        body, name="adamw_slab", grid=(rows // tr,), in_specs=[spec] * 4, out_specs=[spec] * 3,
        out_shape=[jax.ShapeDtypeStruct((rows, D), F32)] * 3,
        compiler_params=_params(1),
    )(g, w, m, v)


def _allreduce_small_adamw(part, w, m, v):
    def body(p_ref, w_ref, m_ref, v_ref, g_ref, d_ref, m2_ref, v2_ref, slots, send_sems, recv_sems):
        x, y, c = _mesh_pos()
        mine = 4 * x + 2 * y + c
        peers = [(px, py, pc) for px in (x, 1 - x) for py in (y, 1 - y) for pc in (c, 1 - c)][1:]

        def remote(k, slot):
            return pltpu.make_async_remote_copy(
                src_ref=p_ref, dst_ref=slots.at[slot], send_sem=send_sems.at[k], recv_sem=recv_sems.at[k],
                device_id=peers[k], device_id_type=MESH)

        sends = [remote(k, mine) for k in range(7)]
        for cp in sends:
            cp.start()
        slots[mine] = p_ref[...]
        for k, (px, py, pc) in enumerate(peers):
            remote(k, 4 * px + 2 * py + pc).wait_recv()
        for cp in sends:
            cp.wait_send()
        g = slots[0]
        for d in range(1, 8):
            g = g + slots[d]
        g_ref[...] = g
        d_ref[...], m2_ref[...], v2_ref[...] = _adam_update(g, w_ref[...], m_ref[...], v_ref[...])

    vm = pl.BlockSpec(memory_space=pltpu.VMEM)
    shape = jax.ShapeDtypeStruct(part.shape, F32)
    return pl.pallas_call(
        body, name="small_allreduce_adamw", in_specs=[vm] * 4, out_specs=[vm] * 4, out_shape=[shape] * 4,
        scratch_shapes=[pltpu.VMEM((8,) + part.shape, F32), pltpu.SemaphoreType.DMA((7,)),
                        pltpu.SemaphoreType.DMA((7,))],
        compiler_params=pltpu.CompilerParams(has_side_effects=True),
    )(part, w, m, v)


def _half(ref, c, axis):
    n = ref.shape[axis] // 2
    return ref.at[(slice(None),) * axis + (pl.ds(c * n, n),)]


def _remote(src, dst, send_sem, recv_sem, device):
    return pltpu.make_async_remote_copy(src_ref=src, dst_ref=dst, send_sem=send_sem, recv_sem=recv_sem,
                                        device_id=device, device_id_type=MESH)


def _gather_weights(split, axes, whole):
    ns, n = len(split), len(split) + len(whole)

    def body(*refs):
        ins, outs = refs[:n], refs[n:2 * n]
        ici_send, ici_recv, d2d_send, d2d_recv, local_sems = refs[2 * n:]
        x, y, c = _mesh_pos()
        mine = 2 * x + y
        chips = _other_chips(x, y)
        local = [pltpu.make_async_copy(ins[a], outs[a].at[mine], local_sems.at[a]) for a in range(n)]
        for cp in local:
            cp.start()

        def ici(a, k, block):
            px, py = chips[k]
            src, dst = ins[a], outs[a].at[block]
            if a < ns:
                src, dst = _half(src, c, axes[a]), _half(dst, c, axes[a])
            return _remote(src, dst, ici_send.at[3 * a + k], ici_recv.at[3 * a + k], (px, py, c))

        def d2d(a, k, block, half):
            part = _half(outs[a].at[block], half, axes[a])
            return _remote(part, part, d2d_send.at[3 * a + k], d2d_recv.at[3 * a + k], (x, y, 1 - c))

        sends = [ici(a, k, mine) for a in range(n) for k in range(3)]
        for cp in sends:
            cp.start()
        for a in range(n):
            for k, (px, py) in enumerate(chips):
                ici(a, k, 2 * px + py).wait_recv()
                if a < ns:
                    sends.append(d2d(a, k, 2 * px + py, c))
                    sends[-1].start()
        for a in range(ns):
            for k, (px, py) in enumerate(chips):
                d2d(a, k, 2 * px + py, 1 - c).wait_recv()
        for cp in sends:
            cp.wait_send()
        for cp in local:
            cp.wait()

    arrays = list(split) + list(whole)
    return pl.pallas_call(
        body, name="gather_weights", in_specs=[HBM_SPEC] * n, out_specs=[HBM_SPEC] * n,
        out_shape=[jax.ShapeDtypeStruct((N_CHIPS,) + s.shape, s.dtype) for s in arrays],
        scratch_shapes=[pltpu.SemaphoreType.DMA((3 * n,)), pltpu.SemaphoreType.DMA((3 * n,)),
                        pltpu.SemaphoreType.DMA((3 * ns,)), pltpu.SemaphoreType.DMA((3 * ns,)),
                        pltpu.SemaphoreType.DMA((n,))],
        compiler_params=pltpu.CompilerParams(has_side_effects=True),
    )(*arrays)


def _rs_to_sibling(gs):
    n = len(gs)

    def body(*refs):
        ins, outs, send_sems, recv_sems = refs[:n], refs[n:2 * n], refs[2 * n], refs[2 * n + 1]
        x, y, c = _mesh_pos()
        copies = [_remote(_half(ins[a], 1 - c, 2), outs[a], send_sems.at[a], recv_sems.at[a], (x, y, 1 - c))
                  for a in range(n)]
        for cp in copies:
            cp.start()
        for cp in copies:
            cp.wait()

    return pl.pallas_call(
        body, name="rs_to_sibling", in_specs=[HBM_SPEC] * n, out_specs=[HBM_SPEC] * n,
        out_shape=[jax.ShapeDtypeStruct(g.shape[:2] + (g.shape[2] // 2,), g.dtype) for g in gs],
        scratch_shapes=[pltpu.SemaphoreType.DMA((n,)), pltpu.SemaphoreType.DMA((n,))],
        compiler_params=pltpu.CompilerParams(has_side_effects=True),
    )(*gs)


def _rs_add_halves(g, recv, c, name):
    _, rows, w = g.shape
    h = w // 2
    tr = rows // 2 if rows % 16 == 0 and rows > 64 else rows

    def body(c_ref, a_ref, b_ref, o_ref):
        o_ref[...] = (a_ref[...] + b_ref[...]).astype(BF16)

    return pl.pallas_call(
        body, name=name,
        grid_spec=pltpu.PrefetchScalarGridSpec(
            num_scalar_prefetch=1, grid=(N_CHIPS, rows // tr),
            in_specs=[pl.BlockSpec((1, tr, h), lambda j, i, s: (j, i, s[0])),
                      pl.BlockSpec((1, tr, h), lambda j, i, s: (j, i, 0))],
            out_specs=pl.BlockSpec((1, tr, h), lambda j, i, s: (j, i, 0))),
        out_shape=jax.ShapeDtypeStruct((N_CHIPS, rows, h), BF16),
        compiler_params=_params(2),
    )(jnp.reshape(c, (1,)).astype(jnp.int32), g, recv)


def _rs_chip_exchange(ps):
    n = len(ps)

    def body(*refs):
        ins, outs = refs[:n], refs[n:2 * n]
        send_sems, recv_sems, local_sems = refs[2 * n:]
        x, y, c = _mesh_pos()
        mine = 2 * x + y
        chips = _other_chips(x, y)
        local = [pltpu.make_async_copy(ins[a].at[mine], outs[a].at[mine], local_sems.at[a]) for a in range(n)]
        for cp in local:
            cp.start()

        def ici(a, k, src_block, dst_block):
            px, py = chips[k]
            return _remote(ins[a].at[src_block], outs[a].at[dst_block], send_sems.at[3 * a + k],
                           recv_sems.at[3 * a + k], (px, py, c))

        sends = [ici(a, k, 2 * px + py, mine) for a in range(n) for k, (px, py) in enumerate(chips)]
        for cp in sends:
            cp.start()
        for a in range(n):
            for k, (px, py) in enumerate(chips):
                ici(a, k, mine, 2 * px + py).wait_recv()
        for cp in sends:
            cp.wait_send()
        for cp in local:
            cp.wait()

    return pl.pallas_call(
        body, name="rs_chip_exchange", in_specs=[HBM_SPEC] * n, out_specs=[HBM_SPEC] * n,
        out_shape=[jax.ShapeDtypeStruct(p.shape, p.dtype) for p in ps],
        scratch_shapes=[pltpu.SemaphoreType.DMA((3 * n,)), pltpu.SemaphoreType.DMA((3 * n,)),
                        pltpu.SemaphoreType.DMA((n,))],
        compiler_params=pltpu.CompilerParams(has_side_effects=True),
    )(*ps)


def _rs_sum_chips(parts, name):
    _, rows, h = parts.shape
    tr = rows // 2 if rows % 16 == 0 and rows > 64 else rows

    def body(p_ref, o_ref):
        p = p_ref[...].astype(F32)
        o_ref[...] = ((p[0] + p[1]) + p[2]) + p[3]

    return pl.pallas_call(
        body, name=name, grid=(rows // tr,),
        in_specs=[pl.BlockSpec((N_CHIPS, tr, h), lambda i: (0, i, 0))],
        out_specs=pl.BlockSpec((tr, h), lambda i: (i, 0)),
        out_shape=jax.ShapeDtypeStruct((rows, h), F32),
        compiler_params=_params(1),
    )(parts)


def _rs_share(halves):
    n = len(halves)

    def body(*refs):
        ins, outs = refs[:n], refs[n:2 * n]
        send_sems, recv_sems, local_sems = refs[2 * n:]
        x, y, c = _mesh_pos()
        local = [pltpu.make_async_copy(ins[a], _half(outs[a], c, 1), local_sems.at[a]) for a in range(n)]
        for cp in local:
            cp.start()
        sends = [_remote(ins[a], _half(outs[a], c, 1), send_sems.at[a], recv_sems.at[a], (x, y, 1 - c))
                 for a in range(n)]
        for cp in sends:
            cp.start()
        for a in range(n):
            _remote(ins[a], _half(outs[a], 1 - c, 1), send_sems.at[a], recv_sems.at[a], (x, y, 1 - c)).wait_recv()
        for cp in sends:
            cp.wait_send()
        for cp in local:
            cp.wait()

    return pl.pallas_call(
        body, name="rs_share", in_specs=[HBM_SPEC] * n, out_specs=[HBM_SPEC] * n,
        out_shape=[jax.ShapeDtypeStruct((p.shape[0], 2 * p.shape[1]), p.dtype) for p in halves],
        scratch_shapes=[pltpu.SemaphoreType.DMA((n,)), pltpu.SemaphoreType.DMA((n,)),
                        pltpu.SemaphoreType.DMA((n,))],
        compiler_params=pltpu.CompilerParams(has_side_effects=True),
    )(*halves)


def _adamw(g, w, m, v, name):
    rows, cols = g.shape
    tr = 256 if rows % 256 == 0 else (rows // 2 if rows % 16 == 0 and rows > 64 else rows)

    def body(g_ref, w_ref, m_ref, v_ref, d_ref, m2_ref, v2_ref):
        d_ref[...], m2_ref[...], v2_ref[...] = _adam_update(g_ref[...], w_ref[...], m_ref[...], v_ref[...])

    spec = pl.BlockSpec((tr, cols), lambda i: (i, 0))
    return pl.pallas_call(
        body, name=name, grid=(rows // tr,), in_specs=[spec] * 4, out_specs=[spec] * 3,
        out_shape=[jax.ShapeDtypeStruct((rows, cols), F32)] * 3,
        compiler_params=_params(1),
    )(g, w, m, v)


def _rows_of(a):
    flat = a.reshape(-1)
    pad = (-flat.shape[0]) % D
    if pad:
        flat = jnp.concatenate([flat, jnp.zeros((pad,), flat.dtype)])
    return flat.reshape(-1, D)


SLAB_PARTS = (("w_in", (D, D_IN // N_CHIPS)), ("w_out", (D // N_CHIPS, D)), ("w_ffn_gate", (D, D_FF // N_CHIPS)),
              ("w_ffn_up", (D, D_FF // N_CHIPS)), ("w_ffn_down", (D_FF // N_CHIPS, D)),
              ("meta_tokens", (N_META, D // N_CHIPS)), ("conv_w", (CONV_W, C_CONV // N_CHIPS)),
              ("gla_w_gate2", (RANK, GLA_K // N_CHIPS)))


def _pack_slab(parts):
    rows = [_rows_of(parts[name].reshape(shape)) for name, shape in SLAB_PARTS]
    used = sum(r.shape[0] for r in rows)
    rows.append(jnp.zeros((SLAB_ROWS - used, D), F32))
    return jnp.concatenate(rows, axis=0)


def _unpack_slab(slab, lead):
    out, r0 = {}, 0
    for name, shape in SLAB_PARTS:
        size = shape[0] * shape[1]
        nrows = -(-size // D)
        out[name] = slab[r0:r0 + nrows].reshape(-1)[:size].reshape(lead[name] + shape)
        r0 += nrows
    return out


SMALL_PARTS = (("norm_mix_g", 0, 0, D), ("norm_ffn_g", 1, 0, D), ("norm_final_g", 2, 0, D),
               ("conv_b", 3, 0, C_CONV), ("conv_ln_g", 3, C_CONV, C_CONV), ("conv_ln_b", 4, 0, C_CONV),
               ("gla_gate_b", 4, C_CONV, GLA_K), ("gla_norm_g", 4, C_CONV + GLA_K, DV))


def _pack_small(parts):
    slab = jnp.zeros((SMALL_ROWS, D), F32)
    for name, row, col, size in SMALL_PARTS:
        slab = lax.dynamic_update_slice(slab, parts[name].reshape(1, size).astype(F32), (row, col))
    return slab


def _unpack_small(slab, shapes):
    return {name: slab[row, col:col + size].reshape(shapes[name]) for name, row, col, size in SMALL_PARTS}


def _column_block(full, j, width):
    return lax.dynamic_slice_in_dim(full, j * width, width, axis=1)


def _local_step(x, target, w):
    n_ex, seq, _ = x.shape
    lp = HEAD_ROWS + seq
    t = n_ex * lp
    meta = jnp.broadcast_to(w["meta_tokens"][None], (n_ex, N_META, D))
    h0 = jnp.concatenate([jnp.zeros((n_ex, PAD_ROWS, D), F32), meta, x], axis=1).reshape(t, D)
    tgt = jnp.concatenate([jnp.zeros((n_ex, HEAD_ROWS, D), F32), target], axis=1).reshape(t, D)
    row_mask = jnp.concatenate([jnp.zeros((n_ex, HEAD_ROWS, 1), F32), jnp.ones((n_ex, seq, 1), F32)],
                               axis=1).reshape(t, 1)

    u, hn = _in_proj(h0, w["norm_mix_g"], w["w_in"])
    yc, y_conv = _conv_fwd(u, w["conv_w"], w["conv_b"], w["conv_ln_g"], w["conv_ln_b"], n_ex, lp)
    y_gla, states = _gla_fwd(u, w["gla_w_gate2"], w["gla_gate_b"], w["gla_norm_g"], n_ex, lp)
    h1, hn2, gate, up, act = _mix_out_ffn_up(h0, y_conv, y_gla, w["w_out"], w["norm_ffn_g"],
                                             w["w_ffn_gate_t"], w["w_ffn_up_t"])
    dh2, loss, d_final_g = _ffn_down_loss(act, w["w_ffn_down"], h1, tgt, w["norm_final_g"], row_mask)

    dgate, dup, dh1, dycat, d_ffn_g = _ffn_bwd(dh2, gate, up, h1, w["w_ffn_down"], w["w_ffn_gate_t"],
                                                w["w_ffn_up_t"], w["w_out"], w["norm_ffn_g"])
    du_conv, d_conv_w, d_conv_b, d_ln_g, d_ln_b = _conv_bwd(dycat, yc, u, w["conv_w"], w["conv_ln_g"],
                                                            w["conv_ln_b"], n_ex, lp)
    du_gla, d_w2, d_gate_b, d_norm_g = _gla_bwd(dycat, u, states, w["gla_w_gate2"], w["gla_gate_b"],
                                                w["gla_norm_g"], n_ex, lp)
    dh0, d_mix_g = _in_proj_bwd(du_conv, du_gla, w["w_in"][:, :2 * C_CONV], w["w_in"][:, 2 * C_CONV:],
                                h0, dh1, w["norm_mix_g"])

    d_w_in_t = jnp.concatenate([_wgrad(du_conv, hn, "wgrad_in_conv"), _wgrad(du_gla, hn, "wgrad_in_gla")],
                               axis=0)[:D_IN]
    d_w_out = jnp.concatenate([_wgrad(y_conv, dh1, "wgrad_out_conv"), _wgrad(y_gla, dh1, "wgrad_out_gla")], axis=0)
    dh0 = dh0.reshape(n_ex, lp, D)
    grads = {
        "w_in_t": d_w_in_t, "w_out": d_w_out,
        "w_ffn_gate_t": _wgrad(dgate, hn2, "wgrad_gate"), "w_ffn_up_t": _wgrad(dup, hn2, "wgrad_up"),
        "w_ffn_down": _wgrad(act, dh2, "wgrad_down"),
        "meta_tokens": jnp.sum(dh0[:, PAD_ROWS:HEAD_ROWS], axis=0),
        "conv_w": d_conv_w, "gla_w_gate2": d_w2[:RANK],
        "norm_mix_g": d_mix_g, "norm_ffn_g": d_ffn_g, "norm_final_g": d_final_g,
        "conv_b": d_conv_b, "conv_ln_g": d_ln_g, "conv_ln_b": d_ln_b,
        "gla_gate_b": d_gate_b, "gla_norm_g": d_norm_g,
    }
    return loss[0, 0], dh0[:, HEAD_ROWS:], grads


WEIGHT_NAMES = ("meta_tokens", "norm_mix_g", "w_in", "conv_w", "conv_b", "conv_ln_g", "conv_ln_b", "gla_w_gate2",
                "gla_gate_b", "gla_norm_g", "w_out", "norm_ffn_g", "w_ffn_gate", "w_ffn_up", "w_ffn_down",
                "norm_final_g")
MATMUL_WEIGHTS = ("w_in", "w_out", "w_ffn_gate", "w_ffn_up", "w_ffn_down")
ROW_SHARDED = ("w_out", "w_ffn_down")


def _full_weights(ws):
    sh = lambda name: ws[name].reshape(ws[name].shape[-2:])
    split = [sh("w_in").astype(BF16), sh("w_out").astype(BF16), sh("w_ffn_gate").T.astype(BF16),
             sh("w_ffn_up").T.astype(BF16), sh("w_ffn_down").astype(BF16)]
    whole = [sh("meta_tokens"), sh("conv_w"), sh("gla_w_gate2")]
    w_in, w_out, gate_t, up_t, down, meta, conv_w, w2 = _gather_weights(split, [0, 0, 0, 0, 0], whole)
    cols = lambda a: jnp.concatenate([a[j] for j in range(N_CHIPS)], axis=1)
    full = {name: ws[name].reshape(1, -1) for name, _, _, _ in SMALL_PARTS}
    full["w_in"] = jnp.concatenate([cols(w_in), jnp.zeros((D, D_IN_PAD - D_IN), BF16)], axis=1)
    full["w_out"] = w_out.reshape(D, D)
    full["w_ffn_gate_t"] = gate_t.reshape(D_FF, D)
    full["w_ffn_up_t"] = up_t.reshape(D_FF, D)
    full["w_ffn_down"] = down.reshape(D_FF, D)
    full["meta_tokens"] = cols(meta)
    full["conv_w"] = jnp.concatenate([cols(conv_w), jnp.zeros((32 - CONV_W, C_CONV), F32)], axis=0)
    full["gla_w_gate2"] = jnp.concatenate([cols(w2), jnp.zeros((128 - RANK, GLA_K), F32)], axis=0).astype(BF16)
    return full


SMALL_RS_ROWS = 48


def _pack_small_sharded(grads):
    by_chip = lambda g, w: jnp.transpose(g.reshape(g.shape[0], N_CHIPS, w), (1, 0, 2))
    meta = by_chip(grads["meta_tokens"], D // N_CHIPS)
    conv = by_chip(grads["conv_w"], C_CONV // N_CHIPS).reshape(N_CHIPS, 16, 256)
    w2 = by_chip(grads["gla_w_gate2"], GLA_K // N_CHIPS).reshape(N_CHIPS, 4, 256)
    pad = jnp.zeros((N_CHIPS, SMALL_RS_ROWS - 36, 256), F32)
    return jnp.concatenate([meta, conv, w2, pad], axis=1)


def _unpack_small_sharded(g):
    return {"meta_tokens": g[0:16], "conv_w": g[16:32].reshape(32, C_CONV // N_CHIPS)[:CONV_W],
            "gla_w_gate2": g[32:36].reshape(RANK, GLA_K // N_CHIPS)}


def _kernel_without_overlap(x, meta_tokens, norm_mix_g, w_in, conv_w, conv_b, conv_ln_g, conv_ln_b, gla_w_gate2, gla_gate_b, gla_norm_g, w_out, norm_ffn_g, w_ffn_gate, w_ffn_up, w_ffn_down, norm_final_g, loss_target, m_meta_tokens, m_norm_mix_g, m_w_in, m_conv_w, m_conv_b, m_conv_ln_g, m_conv_ln_b, m_gla_w_gate2, m_gla_gate_b, m_gla_norm_g, m_w_out, m_norm_ffn_g, m_w_ffn_gate, m_w_ffn_up, m_w_ffn_down, m_norm_final_g, v_meta_tokens, v_norm_mix_g, v_w_in, v_conv_w, v_conv_b, v_conv_ln_g, v_conv_ln_b, v_gla_w_gate2, v_gla_gate_b, v_gla_norm_g, v_w_out, v_norm_ffn_g, v_w_ffn_gate, v_w_ffn_up, v_w_ffn_down, v_norm_final_g):
    ws = dict(zip(WEIGHT_NAMES, (meta_tokens, norm_mix_g, w_in, conv_w, conv_b, conv_ln_g, conv_ln_b, gla_w_gate2,
                                 gla_gate_b, gla_norm_g, w_out, norm_ffn_g, w_ffn_gate, w_ffn_up, w_ffn_down,
                                 norm_final_g)))
    ms = dict(zip(WEIGHT_NAMES, (m_meta_tokens, m_norm_mix_g, m_w_in, m_conv_w, m_conv_b, m_conv_ln_g, m_conv_ln_b,
                                 m_gla_w_gate2, m_gla_gate_b, m_gla_norm_g, m_w_out, m_norm_ffn_g, m_w_ffn_gate,
                                 m_w_ffn_up, m_w_ffn_down, m_norm_final_g)))
    vs = dict(zip(WEIGHT_NAMES, (v_meta_tokens, v_norm_mix_g, v_w_in, v_conv_w, v_conv_b, v_conv_ln_g, v_conv_ln_b,
                                 v_gla_w_gate2, v_gla_gate_b, v_gla_norm_g, v_w_out, v_norm_ffn_g, v_w_ffn_gate,
                                 v_w_ffn_up, v_w_ffn_down, v_norm_final_g)))
    c = lax.axis_index("c")

    full = _full_weights(ws)
    loss, grad_x, grads = _local_step(x, loss_target, full)
    loss = lax.psum(loss, ("x", "y", "c"))

    rs_names = ("w_in", "w_out", "w_ffn_gate", "w_ffn_up", "w_ffn_down", "small")
    by_owner = [grads["w_in_t"].reshape(N_CHIPS, D_IN // N_CHIPS, D), grads["w_out"].reshape(N_CHIPS, D // N_CHIPS, D),
                grads["w_ffn_gate_t"].reshape(N_CHIPS, D_FF // N_CHIPS, D),
                grads["w_ffn_up_t"].reshape(N_CHIPS, D_FF // N_CHIPS, D),
                grads["w_ffn_down"].reshape(N_CHIPS, D_FF // N_CHIPS, D), _pack_small_sharded(grads)]
    from_sibling = _rs_to_sibling(by_owner)
    chip_sums = [_rs_add_halves(g, r, c, "rs_add_" + nm) for g, r, nm in zip(by_owner, from_sibling, rs_names)]
    halves = [_rs_sum_chips(p, "rs_sum_" + nm) for p, nm in zip(_rs_chip_exchange(chip_sums), rs_names)]
    reduced = dict(zip(rs_names, _rs_share(halves)))
    g_sharded = {"w_in": reduced["w_in"].T, "w_out": reduced["w_out"], "w_ffn_gate": reduced["w_ffn_gate"].T,
                 "w_ffn_up": reduced["w_ffn_up"].T, "w_ffn_down": reduced["w_ffn_down"],
                 **_unpack_small_sharded(reduced["small"])}
    out = {"grad": {}, "delta": {}, "new_m": {}, "new_v": {}}
    for name, g in g_sharded.items():
        shape = ws[name].shape
        flat = lambda a: a.reshape(shape[-2:])
        delta, new_m, new_v = _adamw(g, flat(ws[name]), flat(ms[name]), flat(vs[name]), "adamw_" + name)
        for kind, a in (("grad", g), ("delta", delta), ("new_m", new_m), ("new_v", new_v)):
            out[kind][name] = a.reshape(shape)

    small_shapes = {name: ws[name].shape for name, _, _, _ in SMALL_PARTS}
    g_s, d_s, m_s, v_s = _allreduce_small_adamw(_pack_small(grads), _pack_small(ws), _pack_small(ms), _pack_small(vs))
    for kind, slab in (("grad", g_s), ("delta", d_s), ("new_m", m_s), ("new_v", v_s)):
        out[kind].update(_unpack_small(slab, small_shapes))

    return (loss, grad_x, *[out[kind][name] for kind in ("grad", "delta", "new_m", "new_v") for name in WEIGHT_NAMES])


def _gather_plan(split, whole=(), axes=None):
    split, whole = list(split), list(whole)
    ns, n = len(split), len(split) + len(whole)

    def make(ins, outs, sems):
        ici_send, ici_recv, d2d_send, d2d_recv, own_send, own_recv = sems
        x, y, c = _mesh_pos()
        mine = 2 * x + y
        chips = _other_chips(x, y)
        blocks = [2 * px + py for px, py in chips]

        def own(a):
            return _remote(ins[a], outs[a].at[mine], own_send.at[a], own_recv.at[a], (x, y, 1 - c))

        def ici(a, k, block):
            px, py = chips[k]
            src, dst = ins[a], outs[a].at[block]
            if a < ns:
                src, dst = _half(src, c, axes[a]), _half(dst, c, axes[a])
            return _remote(src, dst, ici_send.at[3 * a + k], ici_recv.at[3 * a + k], (px, py, c))

        def d2d(a, k, half):
            part = _half(outs[a].at[blocks[k]], half, axes[a])
            return _remote(part, part, d2d_send.at[3 * a + k], d2d_recv.at[3 * a + k], (x, y, 1 - c))

        def start():
            for a in range(n):
                for k in range(3):
                    ici(a, k, mine).start()
                own(a).start()

        def finish():
            for a in range(n):
                for k in range(3):
                    ici(a, k, blocks[k]).wait_recv()
                    if a < ns:
                        d2d(a, k, c).start()
            for a in range(ns):
                for k in range(3):
                    d2d(a, k, 1 - c).wait_recv()
            for a in range(n):
                for k in range(3):
                    ici(a, k, mine).wait_send()
                    if a < ns:
                        d2d(a, k, c).wait_send()
                own(a).wait()

        return start, finish

    arrays = split + whole
    axes = [0] * ns if axes is None else list(axes)
    return _Plan(arrays, [jax.ShapeDtypeStruct((N_CHIPS,) + s.shape, s.dtype) for s in arrays],
                 [pltpu.SemaphoreType.DMA((3 * n,)), pltpu.SemaphoreType.DMA((3 * n,)),
                  pltpu.SemaphoreType.DMA((3 * ns,)), pltpu.SemaphoreType.DMA((3 * ns,)),
                  pltpu.SemaphoreType.DMA((n,)), pltpu.SemaphoreType.DMA((n,))], make)


def _to_sibling_plan(gs):
    n = len(gs)

    def make(ins, outs, sems):
        send_sems, recv_sems = sems
        x, y, c = _mesh_pos()

        def copy(a):
            return _remote(_half(ins[a], 1 - c, 2), outs[a], send_sems.at[a], recv_sems.at[a], (x, y, 1 - c))

        def start():
            for a in range(n):
                copy(a).start()

        def finish():
            for a in range(n):
                copy(a).wait()

        return start, finish

    return _Plan(list(gs), [jax.ShapeDtypeStruct(g.shape[:2] + (g.shape[2] // 2,), g.dtype) for g in gs],
                 [pltpu.SemaphoreType.DMA((n,)), pltpu.SemaphoreType.DMA((n,))], make)


def _chip_exchange_plan(ps):
    n = len(ps)

    def make(ins, outs, sems):
        send_sems, recv_sems = sems
        x, y, c = _mesh_pos()
        chips = _other_chips(x, y)

        def ici(a, k):
            px, py = chips[k]
            return _remote(ins[a].at[2 * px + py], outs[a].at[k], send_sems.at[3 * a + k],
                           recv_sems.at[3 * a + k], (px, py, c))

        def start():
            for a in range(n):
                for k in range(3):
                    ici(a, k).start()

        def finish():
            for a in range(n):
                for k in range(3):
                    ici(a, k).wait()

        return start, finish

    return _Plan(list(ps), [jax.ShapeDtypeStruct((3,) + p.shape[1:], p.dtype) for p in ps],
                 [pltpu.SemaphoreType.DMA((3 * n,)), pltpu.SemaphoreType.DMA((3 * n,))], make)


def _share_plan(halves):
    n = len(halves)

    def make(ins, outs, sems):
        send_sems, recv_sems = sems
        x, y, c = _mesh_pos()

        def d2d(a):
            return _remote(ins[a], outs[a], send_sems.at[a], recv_sems.at[a], (x, y, 1 - c))

        def start():
            for a in range(n):
                d2d(a).start()

        def finish():
            for a in range(n):
                d2d(a).wait()

        return start, finish

    return _Plan(list(halves), [jax.ShapeDtypeStruct(p.shape, p.dtype) for p in halves],
                 [pltpu.SemaphoreType.DMA((n,)), pltpu.SemaphoreType.DMA((n,))], make)


def _rs_sum(own, others, mine, name):
    _, rows, h = own.shape
    tr = rows // 2 if rows % 16 == 0 and rows > 64 else rows

    def body(mine_ref, own_ref, oth_ref, o_ref):
        p = oth_ref[...].astype(F32)
        o_ref[...] = ((own_ref[0].astype(F32) + p[0]) + p[1]) + p[2]

    return pl.pallas_call(
        body, name=name,
        grid_spec=pltpu.PrefetchScalarGridSpec(
            num_scalar_prefetch=1, grid=(rows // tr,),
            in_specs=[pl.BlockSpec((1, tr, h), lambda i, s: (s[0], i, 0)),
                      pl.BlockSpec((3, tr, h), lambda i, s: (0, i, 0))],
            out_specs=pl.BlockSpec((tr, h), lambda i, s: (i, 0))),
        out_shape=jax.ShapeDtypeStruct((rows, h), F32),
        compiler_params=_params(1),
    )(jnp.reshape(mine, (1,)).astype(jnp.int32), own, others)


def _join(mine, theirs, c):
    return jnp.where(c == 0, jnp.concatenate([mine, theirs], axis=1), jnp.concatenate([theirs, mine], axis=1))


LOSS_ROW = 5


def _merge_plans(a, b):
    na_in, na_out, na_sems = len(a.arrays), len(a.out_shape), len(a.sems)

    def make(ins, outs, sems):
        start_a, finish_a = a.make(ins[:na_in], outs[:na_out], sems[:na_sems])
        start_b, finish_b = b.make(ins[na_in:], outs[na_out:], sems[na_sems:])

        def start():
            start_a()
            start_b()

        def finish():
            finish_a()
            finish_b()

        return start, finish

    return _Plan(list(a.arrays) + list(b.arrays), list(a.out_shape) + list(b.out_shape),
                 list(a.sems) + list(b.sems), make)


def _exchange(plan, name):
    n_in, n_out = len(plan.arrays), len(plan.out_shape)

    def body(*refs):
        start, finish = plan.make(refs[:n_in], refs[n_in:n_in + n_out], refs[n_in + n_out:])
        start()
        finish()

    return pl.pallas_call(
        body, name=name, in_specs=[HBM_SPEC] * n_in, out_specs=[HBM_SPEC] * n_out, out_shape=list(plan.out_shape),
        scratch_shapes=list(plan.sems), compiler_params=pltpu.CompilerParams(has_side_effects=True),
    )(*plan.arrays)


def _adamw_halves(mine, theirs, c, w, m, v, name):
    rows, h = mine.shape
    tr = rows // 2 if rows % 16 == 0 else rows

    def body(c_ref, a_ref, b_ref, w_ref, m_ref, v_ref, go_ref, d_ref, m2_ref, v2_ref):
        g = jnp.where(pl.program_id(1) == c_ref[0], a_ref[...], b_ref[...])
        go_ref[...] = g
        d_ref[...], m2_ref[...], v2_ref[...] = _adam_update(g, w_ref[...], m_ref[...], v_ref[...])

    half = pl.BlockSpec((tr, h), lambda i, j, s: (i, 0))
    spec = pl.BlockSpec((tr, h), lambda i, j, s: (i, j))
    return pl.pallas_call(
        body, name=name,
        grid_spec=pltpu.PrefetchScalarGridSpec(num_scalar_prefetch=1, grid=(rows // tr, 2),
                                               in_specs=[half, half, spec, spec, spec], out_specs=[spec] * 4),
        out_shape=[jax.ShapeDtypeStruct((rows, 2 * h), F32)] * 4,
        compiler_params=_params(2),
    )(jnp.reshape(c, (1,)).astype(jnp.int32), mine, theirs, w, m, v)


def _columns(gathered):
    return jnp.concatenate([gathered[j] for j in range(N_CHIPS)], axis=1)


def kernel(x, meta_tokens, norm_mix_g, w_in, conv_w, conv_b, conv_ln_g, conv_ln_b, gla_w_gate2, gla_gate_b, gla_norm_g, w_out, norm_ffn_g, w_ffn_gate, w_ffn_up, w_ffn_down, norm_final_g, loss_target, m_meta_tokens, m_norm_mix_g, m_w_in, m_conv_w, m_conv_b, m_conv_ln_g, m_conv_ln_b, m_gla_w_gate2, m_gla_gate_b, m_gla_norm_g, m_w_out, m_norm_ffn_g, m_w_ffn_gate, m_w_ffn_up, m_w_ffn_down, m_norm_final_g, v_meta_tokens, v_norm_mix_g, v_w_in, v_conv_w, v_conv_b, v_conv_ln_g, v_conv_ln_b, v_gla_w_gate2, v_gla_gate_b, v_gla_norm_g, v_w_out, v_norm_ffn_g, v_w_ffn_gate, v_w_ffn_up, v_w_ffn_down, v_norm_final_g):
    ws = dict(zip(WEIGHT_NAMES, (meta_tokens, norm_mix_g, w_in, conv_w, conv_b, conv_ln_g, conv_ln_b, gla_w_gate2,
                                 gla_gate_b, gla_norm_g, w_out, norm_ffn_g, w_ffn_gate, w_ffn_up, w_ffn_down,
                                 norm_final_g)))
    ms = dict(zip(WEIGHT_NAMES, (m_meta_tokens, m_norm_mix_g, m_w_in, m_conv_w, m_conv_b, m_conv_ln_g, m_conv_ln_b,
                                 m_gla_w_gate2, m_gla_gate_b, m_gla_norm_g, m_w_out, m_norm_ffn_g, m_w_ffn_gate,
                                 m_w_ffn_up, m_w_ffn_down, m_norm_final_g)))
    vs = dict(zip(WEIGHT_NAMES, (v_meta_tokens, v_norm_mix_g, v_w_in, v_conv_w, v_conv_b, v_conv_ln_g, v_conv_ln_b,
                                 v_gla_w_gate2, v_gla_gate_b, v_gla_norm_g, v_w_out, v_norm_ffn_g, v_w_ffn_gate,
                                 v_w_ffn_up, v_w_ffn_down, v_norm_final_g)))
    c = lax.axis_index("c")
    shard = lambda d, name: d[name].reshape(d[name].shape[-2:])
    vec = {name: ws[name].reshape(1, -1) for name, _, _, _ in SMALL_PARTS}
    n_ex, seq, _ = x.shape
    lp = HEAD_ROWS + seq
    t = n_ex * lp

    w_in_g, meta_g, conv_w_g, w2_g = _exchange(
        _gather_plan([shard(ws, "w_in").T.astype(BF16)],
                     [shard(ws, "meta_tokens"), shard(ws, "conv_w"), shard(ws, "gla_w_gate2")], axes=[1]),
        "gather_first")
    w_in_t = jnp.concatenate([w_in_g.reshape(D_IN, D), jnp.zeros((D_IN_PAD - D_IN, D), BF16)], axis=0)
    w_in_full = w_in_t.T
    conv_w_full = jnp.concatenate([_columns(conv_w_g), jnp.zeros((32 - CONV_W, C_CONV), F32)], axis=0)
    w2_full = jnp.concatenate([_columns(w2_g), jnp.zeros((128 - RANK, GLA_K), F32)], axis=0).astype(BF16)

    meta = jnp.broadcast_to(_columns(meta_g)[None], (n_ex, N_META, D))
    h0 = jnp.concatenate([jnp.zeros((n_ex, PAD_ROWS, D), F32), meta, x], axis=1).reshape(t, D)
    tgt = jnp.concatenate([jnp.zeros((n_ex, HEAD_ROWS, D), F32), loss_target], axis=1).reshape(t, D)
    row_mask = jnp.concatenate([jnp.zeros((n_ex, HEAD_ROWS, 1), F32), jnp.ones((n_ex, seq, 1), F32)],
                               axis=1).reshape(t, 1)

    (u, hn), (w_out_g,) = _in_proj(h0, vec["norm_mix_g"], w_in_full,
                                   plan=_gather_plan([shard(ws, "w_out").astype(BF16)]))
    (yc, y_conv), (gate_g,) = _conv_fwd(
        u, conv_w_full, vec["conv_b"], vec["conv_ln_g"], vec["conv_ln_b"], n_ex, lp,
        plan=_gather_plan([shard(ws, "w_ffn_gate").T.astype(BF16)]))
    (y_gla, states), (up_g, down_g) = _gla_fwd(
        u, w2_full, vec["gla_gate_b"], vec["gla_norm_g"], n_ex, lp,
        plan=_gather_plan([shard(ws, "w_ffn_up").T.astype(BF16), shard(ws, "w_ffn_down").astype(BF16)]))
    w_out_full, w_down_full = w_out_g.reshape(D, D), down_g.reshape(D_FF, D)
    w_gate_t, w_up_t = gate_g.reshape(D_FF, D), up_g.reshape(D_FF, D)

    h1, hn2, gate, up, act = _mix_out_ffn_up(h0, y_conv, y_gla, w_out_full, vec["norm_ffn_g"], w_gate_t.T, w_up_t.T)
    dh2, loss, d_final_g = _ffn_down_loss(act, w_down_full, h1, tgt, vec["norm_final_g"], row_mask)
    dgate, dup, dh1, dycat, d_ffn_g = _ffn_bwd(dh2, gate, up, h1, w_down_full.T, w_gate_t, w_up_t, w_out_full.T,
                                                vec["norm_ffn_g"])

    early = ("w_out", "w_ffn_gate", "w_ffn_up", "w_ffn_down")
    d_w_out = jnp.concatenate([_wgrad(y_conv, dh1, "wgrad_out_conv"), _wgrad(y_gla, dh1, "wgrad_out_gla")], axis=0)
    by_owner = [d_w_out.reshape(N_CHIPS, D // N_CHIPS, D),
                _wgrad(dgate, hn2, "wgrad_gate").reshape(N_CHIPS, D_FF // N_CHIPS, D),
                _wgrad(dup, hn2, "wgrad_up").reshape(N_CHIPS, D_FF // N_CHIPS, D),
                _wgrad(act, dh2, "wgrad_down").reshape(N_CHIPS, D_FF // N_CHIPS, D)]
    (du_conv, d_conv_w, d_conv_b, d_ln_g, d_ln_b), from_sibling = _conv_bwd(
        dycat, yc, u, conv_w_full, vec["conv_ln_g"], vec["conv_ln_b"], n_ex, lp, plan=_to_sibling_plan(by_owner))
    chip_sums = [_rs_add_halves(g, r, c, "rs_add_" + nm) for g, r, nm in zip(by_owner, from_sibling, early)]
    (du_gla, d_w2, d_gate_b, d_norm_g), exchanged = _gla_bwd(
        dycat, u, states, w2_full, vec["gla_gate_b"], vec["gla_norm_g"], n_ex, lp,
        plan=_chip_exchange_plan(chip_sums))
    mine = 2 * lax.axis_index("x") + lax.axis_index("y")
    halves = [_rs_sum(own, oth, mine, "rs_sum_" + nm) for own, oth, nm in zip(chip_sums, exchanged, early)]

    d_w_in_t = jnp.concatenate([_wgrad(du_conv, hn, "wgrad_in_conv"), _wgrad(du_gla, hn, "wgrad_in_gla")],
                               axis=0)[:D_IN].reshape(N_CHIPS, D_IN // N_CHIPS, D)
    (in_from_sibling,) = _exchange(_to_sibling_plan([d_w_in_t]), "rs_late_to_sibling")
    in_chip_sum = _rs_add_halves(d_w_in_t, in_from_sibling, c, "rs_add_w_in")
    (dh0, d_mix_g), shared = _in_proj_bwd(
        du_conv, du_gla, w_in_t[:2 * C_CONV], w_in_t[2 * C_CONV:], h0, dh1, vec["norm_mix_g"],
        plan=_merge_plans(_share_plan(halves), _chip_exchange_plan([in_chip_sum])))
    dh0 = dh0.reshape(n_ex, lp, D)
    grad_x = dh0[:, HEAD_ROWS:]

    out = {"grad": {}, "delta": {}, "new_m": {}, "new_v": {}}

    def update(name, g=None, halves=None, transposed=False):
        shape = ws[name].shape
        lay = (lambda a: a.T) if transposed else (lambda a: a)
        w2d, m2d, v2d = lay(shard(ws, name)), lay(shard(ms, name)), lay(shard(vs, name))
        if halves is not None:
            res = _adamw_halves(*halves, c, w2d, m2d, v2d, "adamw_" + name)
        else:
            res = [g, *_adamw(g, w2d, m2d, v2d, "adamw_" + name)]
        for kind, a in zip(("grad", "delta", "new_m", "new_v"), res):
            out[kind][name] = lay(a).reshape(shape)

    update("w_out", halves=(halves[0], shared[0]))
    update("w_ffn_gate", halves=(halves[1], shared[1]), transposed=True)
    update("w_ffn_up", halves=(halves[2], shared[2]), transposed=True)
    update("w_ffn_down", halves=(halves[3], shared[3]))

    in_half = _rs_sum(in_chip_sum, shared[4], mine, "rs_sum_w_in")
    (in_shared,) = _exchange(_share_plan([in_half]), "rs_late_share")
    update("w_in", halves=(in_half, in_shared), transposed=True)

    small = {"norm_mix_g": d_mix_g, "norm_ffn_g": d_ffn_g, "norm_final_g": d_final_g, "conv_b": d_conv_b,
             "conv_ln_g": d_ln_g, "conv_ln_b": d_ln_b, "gla_gate_b": d_gate_b, "gla_norm_g": d_norm_g}
    small_shapes = {name: ws[name].shape for name, _, _, _ in SMALL_PARTS}
    part = lax.dynamic_update_slice(_pack_small(small), loss[:, :1], (LOSS_ROW, 0))
    part = jnp.concatenate([part, jnp.sum(dh0[:, PAD_ROWS:HEAD_ROWS], axis=0), d_conv_w.reshape(16, D),
                            d_w2[:RANK].reshape(4, D), jnp.zeros((4, D), F32)], axis=0)
    tall = lambda a: jnp.concatenate([a, jnp.zeros((part.shape[0] - SMALL_ROWS, D), F32)], axis=0)
    g_s, d_s, m_s, v_s = _allreduce_small_adamw(part, tall(_pack_small(ws)), tall(_pack_small(ms)),
                                                tall(_pack_small(vs)))
    for kind, slab in (("grad", g_s), ("delta", d_s), ("new_m", m_s), ("new_v", v_s)):
        out[kind].update(_unpack_small(slab, small_shapes))
    loss = g_s[LOSS_ROW, 0]
    block = lambda a, width: lax.dynamic_slice_in_dim(a, mine * width, width, axis=1)
    update("meta_tokens", g=block(g_s[8:24], D // N_CHIPS))
    update("conv_w", g=block(g_s[24:40].reshape(32, C_CONV), C_CONV // N_CHIPS)[:CONV_W])
    update("gla_w_gate2", g=block(g_s[40:44].reshape(RANK, GLA_K), GLA_K // N_CHIPS))

    return (loss, grad_x, *[out[kind][name] for kind in ("grad", "delta", "new_m", "new_v") for name in WEIGHT_NAMES])
```

```python
import functools
from typing import Any, Callable, NamedTuple, Sequence

import jax
import jax.numpy as jnp
from jax import lax
from jax.experimental import pallas as pl
from jax.experimental.pallas import tpu as pltpu

F32 = jnp.float32
BF16 = jnp.bfloat16
MESH = pl.DeviceIdType.MESH

D = 1024
N_META = 16
C_CONV = 512
CONV_W = 31
GLA_K = 256
GLA_V = 512
N_HEADS = 4
DK = 64
DV = 128
RANK = 16
CHUNK = 64
PAD_ROWS = CHUNK - N_META
HEAD_ROWS = CHUNK
D_IN = 2576
D_IN_PAD = 2688
D_GLA_IN = D_IN_PAD - 2 * C_CONV
D_FF = 2816
RMS_EPS = 1e-6
LN_EPS = 1e-5
GATE_TAU = 16.0
N_CHIPS = 4

ADAM_LR = 0.001
ADAM_B1 = 0.9
ADAM_B2 = 0.999
ADAM_EPS = 1e-08
ADAM_WD = 0.01
ADAM_STEP = 10

V7X_VMEM_BYTES = 64 * 1024 * 1024
VMEM_LIMIT = V7X_VMEM_BYTES - 8 * 1024 * 1024

SLAB_ROWS = 3072
HALF_ROWS = SLAB_ROWS // 2
SMALL_ROWS = 8


def _dot(a, b):
    return jnp.dot(a, b, preferred_element_type=F32)


def _dot_nt(a, b):
    return lax.dot_general(a, b, (((1,), (1,)), ((), ())), preferred_element_type=F32)


def _dot_tn(a, b):
    return lax.dot_general(a, b, (((0,), (0,)), ((), ())), preferred_element_type=F32)


def _sigmoid(x):
    return 1.0 / (1.0 + jnp.exp(-x))


def _const_spec(shape):
    return pl.BlockSpec(shape, lambda *_: (0,) * len(shape), pipeline_mode=pl.Buffered(1))


def _acc_spec(shape):
    return pl.BlockSpec(shape, lambda *_: (0,) * len(shape))


def _params(n_axes):
    return pltpu.CompilerParams(dimension_semantics=("arbitrary",) * n_axes, vmem_limit_bytes=VMEM_LIMIT)


def _row_tile(t, want):
    for r in (want, 384, 192, 128, 64):
        if r <= want and t % r == 0:
            return r
    raise ValueError(f"no row tile for {t}")


class _Plan(NamedTuple):
    arrays: Sequence[Any]
    out_shape: Sequence[Any]
    sems: Sequence[Any]
    make: Callable


def _call(body, *, name, grid, in_specs, out_specs, out_shape, scratch_shapes=(), plan=None):
    n_in, n_out, n_scr = len(in_specs), len(out_specs), len(scratch_shapes)
    if plan is None:
        plan = _Plan([], [], [], lambda ins, outs, sems: (lambda: None, lambda: None))
    nx_in, nx_out = len(plan.arrays), len(plan.out_shape)

    def hosted(*refs):
        ins, xins = refs[:n_in], refs[n_in:n_in + nx_in]
        o0 = n_in + nx_in
        outs, xouts = refs[o0:o0 + n_out], refs[o0 + n_out:o0 + n_out + nx_out]
        s0 = o0 + n_out + nx_out
        scr, sems = refs[s0:s0 + n_scr], refs[s0 + n_scr:]
        ids = [pl.program_id(a) for a in range(len(grid))]
        first = functools.reduce(jnp.logical_and, [i == 0 for i in ids])
        last = functools.reduce(jnp.logical_and, [i == g - 1 for i, g in zip(ids, grid)])
        start, finish = plan.make(xins, xouts, sems)
        pl.when(first)(start)
        body(*ins, *outs, *scr)
        pl.when(last)(finish)

    call = pl.pallas_call(
        hosted, name=name, grid=grid, in_specs=list(in_specs) + [HBM_SPEC] * nx_in,
        out_specs=list(out_specs) + [HBM_SPEC] * nx_out, out_shape=list(out_shape) + list(plan.out_shape),
        scratch_shapes=list(scratch_shapes) + list(plan.sems),
        compiler_params=pltpu.CompilerParams(dimension_semantics=("arbitrary",) * len(grid),
                                             vmem_limit_bytes=VMEM_LIMIT, has_side_effects=nx_in > 0))

    def run(*args):
        res = call(*args, *plan.arrays)
        return res[:n_out], res[n_out:]

    return run


def _pad_head_rows(a, plan=None):
    n_ex, seq, _ = a.shape
    nc = (HEAD_ROWS + seq) // CHUNK

    def body(a_ref, o_ref):
        o_ref[...] = jnp.where(pl.program_id(0) > 0, a_ref[...], 0.0)

    return _call(
        body, name="pad_head_rows", grid=(nc,),
        in_specs=[pl.BlockSpec((n_ex, CHUNK, D), lambda n: (0, jnp.maximum(n - 1, 0), 0))],
        out_specs=[pl.BlockSpec((n_ex, CHUNK, D), lambda n: (0, n, 0))],
        out_shape=[jax.ShapeDtypeStruct((n_ex, HEAD_ROWS + seq, D), F32)],
        plan=plan,
    )(a)


def _in_proj(h0, g_mix, w_in, plan=None):
    t = h0.shape[0]
    r = _row_tile(t, 384)

    def body(h_ref, g_ref, w_ref, u_ref, hn_ref):
        h = h_ref[...]
        rstd = lax.rsqrt(jnp.mean(h * h, axis=-1, keepdims=True) + RMS_EPS)
        hn = (h * rstd * g_ref[...]).astype(BF16)
        hn_ref[...] = hn
        u_ref[...] = _dot(hn, w_ref[...])

    return _call(
        body, name="in_proj", grid=(t // r,),
        in_specs=[pl.BlockSpec((r, D), lambda i: (i, 0)), _const_spec((1, D)), _const_spec((D, D_IN_PAD))],
        out_specs=[pl.BlockSpec((r, D_IN_PAD), lambda i: (i, 0)), pl.BlockSpec((r, D), lambda i: (i, 0))],
        out_shape=[jax.ShapeDtypeStruct((t, D_IN_PAD), F32), jax.ShapeDtypeStruct((t, D), BF16)],
        plan=plan,
    )(h0, g_mix, w_in)


CONV_TILE = 192
CONV_SUB = 32
CONV_LEAD = CONV_SUB - (CONV_W - 1)
SUBLANES = 8


def _shifted_copies(src, dst, r):
    for s in range(1, SUBLANES):
        dst[s - 1] = src[s:s + r + CONV_SUB - SUBLANES, :]


def _shifted_rows(src, shifted, start):
    base, s = SUBLANES * (start // SUBLANES), start % SUBLANES
    if s == 0:
        return src[base:base + CONV_SUB, :]
    return shifted[s - 1, base:base + CONV_SUB, :]


def _conv_fwd(u, conv_w, conv_b, ln_g, ln_b, n_ex, lp, plan=None):
    r = CONV_TILE
    nt = lp // r
    hb = r // CONV_SUB

    def body(cur_ref, prev_ref, w_ref, b_ref, lg_ref, lb_ref, yc_ref, y_ref, glu, glu_sh):
        i = pl.program_id(1)
        cur = cur_ref[...]
        glu[CONV_SUB:CONV_SUB + r, :] = cur[:, :C_CONV] * _sigmoid(cur[:, C_CONV:])
        pv = prev_ref[...]
        halo = pv[:, :C_CONV] * _sigmoid(pv[:, C_CONV:])
        glu[0:CONV_SUB, :] = jnp.where(i > 0, halo, 0.0)
        _shifted_copies(glu, glu_sh, r)
        w = w_ref[...]
        for j in range(r // CONV_SUB):
            r0 = j * CONV_SUB
            acc = jnp.zeros((CONV_SUB, C_CONV), F32) + b_ref[...]
            for k in range(CONV_W):
                acc = acc + w[k:k + 1, :] * _shifted_rows(glu, glu_sh, r0 + CONV_LEAD + k)
            mu = jnp.mean(acc, axis=-1, keepdims=True)
            cen = acc - mu
            var = jnp.mean(cen * cen, axis=-1, keepdims=True)
            out = cen * lax.rsqrt(var + LN_EPS) * lg_ref[...] + lb_ref[...]
            y = out * _sigmoid(out)
            row = i * r + r0 + lax.broadcasted_iota(jnp.int32, (CONV_SUB, 1), 0)
            y = jnp.where(row >= PAD_ROWS, y, 0.0)
            yc_ref[r0:r0 + CONV_SUB, :] = acc
            y_ref[r0:r0 + CONV_SUB, :] = y.astype(BF16)

    t = n_ex * lp
    return _call(
        body, name="conv_fwd", grid=(n_ex, nt),
        in_specs=[pl.BlockSpec((r, 2 * C_CONV), lambda b, i: (b * nt + i, 0)),
                  pl.BlockSpec((CONV_SUB, 2 * C_CONV), lambda b, i: (jnp.maximum((b * nt + i) * hb - 1, 0), 0)),
                  _const_spec((32, C_CONV)), _const_spec((1, C_CONV)), _const_spec((1, C_CONV)), _const_spec((1, C_CONV))],
        out_specs=[pl.BlockSpec((r, C_CONV), lambda b, i: (b * nt + i, 0)),
                   pl.BlockSpec((r, C_CONV), lambda b, i: (b * nt + i, 0))],
        out_shape=[jax.ShapeDtypeStruct((t, C_CONV), F32), jax.ShapeDtypeStruct((t, C_CONV), BF16)],
        scratch_shapes=[pltpu.VMEM((r + CONV_SUB, C_CONV), F32),
                        pltpu.VMEM((SUBLANES - 1, r + CONV_SUB - SUBLANES, C_CONV), F32)],
        plan=plan,
    )(u, u, conv_w, conv_b, ln_g, ln_b)


def _gla_gates(lr, w2, gb, first_chunk):
    z = _dot(lr.astype(BF16), w2) + gb
    a = (jnp.minimum(z, 0.0) - jnp.log(1.0 + jnp.exp(-jnp.abs(z)))) * (1.0 / GATE_TAU)
    row = lax.broadcasted_iota(jnp.int32, (CHUNK, 1), 0)
    live = jnp.logical_or(jnp.logical_not(first_chunk), row >= PAD_ROWS)
    return z, jnp.where(live, a, 0.0), live


def _tri(lower):
    i = lax.broadcasted_iota(jnp.int32, (CHUNK, CHUNK), 0)
    j = lax.broadcasted_iota(jnp.int32, (CHUNK, CHUNK), 1)
    return (i >= j) if lower else (i <= j)


def _gla_fwd_per_head(u, w2, gb, ng, n_ex, lp, plan=None):
    nc = lp // CHUNK
    t = n_ex * lp

    def body(qk_ref, v_ref, g_ref, lr_ref, w2_ref, gb_ref, ng_ref, y_ref, st_ref, state):
        n = pl.program_id(0)

        @pl.when(n == 0)
        def _():
            state[...] = jnp.zeros_like(state)

        causal = _tri(True)
        for e in range(n_ex):
            st = state[e]
            st_ref[e] = st
            qk = qk_ref[e]
            q, k = qk[:, :GLA_K], qk[:, GLA_K:]
            _, a, _ = _gla_gates(lr_ref[e], w2_ref[...], gb_ref[...], n == 0)
            b = jnp.dot(causal.astype(F32), a, preferred_element_type=F32, precision=lax.Precision.HIGHEST)
            bl = b[CHUNK - 1:CHUNK, :]
            q_in = (q * (DK ** -0.5) * jnp.exp(b)).astype(BF16)
            k_in = (k * jnp.exp(-b)).astype(BF16)
            k_dec = (k * jnp.exp(bl - b)).astype(BF16)
            decay = jnp.exp(bl)
            v = v_ref[e]
            g = g_ref[e]
            st_b = st.astype(BF16)
            ys, new = [], []
            for h in range(N_HEADS):
                ks = slice(h * DK, (h + 1) * DK)
                vs = slice(h * DV, (h + 1) * DV)
                vh = v[:, vs].astype(BF16)
                s = jnp.where(causal, _dot_nt(q_in[:, ks], k_in[:, ks]), 0.0)
                o = _dot(s.astype(BF16), vh) + _dot_nt(q_in[:, ks], st_b[:, ks])
                new.append(decay[:, ks] * st[:, ks] + _dot_tn(vh, k_dec[:, ks]))
                rstd = lax.rsqrt(jnp.mean(o * o, axis=-1, keepdims=True) + RMS_EPS)
                gh = g[:, vs]
                ys.append(o * rstd * ng_ref[...] * (gh * _sigmoid(gh)))
            state[e] = jnp.concatenate(new, axis=1)
            y_ref[e] = jnp.concatenate(ys, axis=1).astype(BF16)

    u3 = u.reshape(n_ex, lp, D_IN_PAD)
    blk = lambda w, col: pl.BlockSpec((n_ex, CHUNK, w), lambda n: (0, n, col))
    (y, states), extra = _call(
        body, name="gla_fwd", grid=(nc,),
        in_specs=[blk(2 * GLA_K, 2), blk(GLA_V, 3), blk(GLA_V, 4), blk(128, 20),
                  _const_spec((128, GLA_K)), _const_spec((1, GLA_K)), _const_spec((1, DV))],
        out_specs=[blk(GLA_V, 0), pl.BlockSpec((n_ex, DV, GLA_K), lambda n: (0, n, 0))],
        out_shape=[jax.ShapeDtypeStruct((n_ex, lp, GLA_V), BF16),
                   jax.ShapeDtypeStruct((n_ex, nc * DV, GLA_K), F32)],
        scratch_shapes=[pltpu.VMEM((n_ex, DV, GLA_K), F32)],
        plan=plan,
    )(u3, u3, u3, u3, w2, gb, ng)
    return (y.reshape(t, GLA_V), states), extra


FFN_TILE = 192


def _mix_out_ffn_up(h0, y_conv, y_gla, w_out, g_ffn, w_gate, w_up, plan=None):
    t = h0.shape[0]
    r = _row_tile(t, 384)

    def body(h0_ref, yc_ref, yg_ref, wo_ref, g_ref, wg_ref, wu_ref, h1_ref, hn_ref, gate_ref, up_ref, act_ref):
        h1 = h0_ref[...] + _dot(yc_ref[...], wo_ref[0:C_CONV, :]) + _dot(yg_ref[...], wo_ref[C_CONV:D, :])
        h1_ref[...] = h1
        rstd = lax.rsqrt(jnp.mean(h1 * h1, axis=-1, keepdims=True) + RMS_EPS)
        hn = (h1 * rstd * g_ref[...]).astype(BF16)
        hn_ref[...] = hn
        gate = _dot(hn, wg_ref[...])
        up = _dot(hn, wu_ref[...])
        gate_ref[...] = gate
        up_ref[...] = up
        act_ref[...] = (gate * _sigmoid(gate) * up).astype(BF16)

    rows = lambda w: pl.BlockSpec((r, w), lambda i: (i, 0))
    return _call(
        body, name="mix_out_ffn_up", grid=(t // r,),
        in_specs=[rows(D), rows(C_CONV), rows(GLA_V), _const_spec((D, D)), _const_spec((1, D)),
                  _const_spec((D, D_FF)), _const_spec((D, D_FF))],
        out_specs=[rows(D), rows(D), rows(D_FF), rows(D_FF), rows(D_FF)],
        out_shape=[jax.ShapeDtypeStruct((t, D), F32), jax.ShapeDtypeStruct((t, D), BF16),
                   jax.ShapeDtypeStruct((t, D_FF), F32), jax.ShapeDtypeStruct((t, D_FF), F32),
                   jax.ShapeDtypeStruct((t, D_FF), BF16)],
        plan=plan,
    )(h0, y_conv, y_gla, w_out, g_ffn, w_gate, w_up)


def _ffn_down_loss(act, w_down, h1, target, g_final, row_mask):
    t = h1.shape[0]
    r = _row_tile(t, 384)

    def body(act_ref, wd_ref, h1_ref, tgt_ref, gf_ref, mask_ref, dh2_ref, loss_ref, dgf_ref):
        @pl.when(pl.program_id(0) == 0)
        def _():
            loss_ref[...] = jnp.zeros_like(loss_ref)
            dgf_ref[...] = jnp.zeros_like(dgf_ref)

        h2 = h1_ref[...] + _dot(act_ref[...], wd_ref[...])
        rstd = lax.rsqrt(jnp.mean(h2 * h2, axis=-1, keepdims=True) + RMS_EPS)
        nrm = h2 * rstd
        gf = gf_ref[...]
        err = (nrm * gf - tgt_ref[...]) * mask_ref[...]
        loss_ref[...] += jnp.sum(err * err) * (0.5 / D)
        dy = err * (1.0 / D)
        dgf_ref[...] += jnp.sum(dy * nrm, axis=0, keepdims=True)
        dn = dy * gf
        dh2_ref[...] = rstd * (dn - nrm * jnp.mean(dn * nrm, axis=-1, keepdims=True))

    rows = lambda w: pl.BlockSpec((r, w), lambda i: (i, 0))
    return pl.pallas_call(
        body, name="ffn_down_loss", grid=(t // r,),
        in_specs=[rows(D_FF), _const_spec((D_FF, D)), rows(D), rows(D), _const_spec((1, D)), rows(1)],
        out_specs=[rows(D), _acc_spec((1, 128)), _acc_spec((1, D))],
        out_shape=[jax.ShapeDtypeStruct((t, D), F32), jax.ShapeDtypeStruct((1, 128), F32),
                   jax.ShapeDtypeStruct((1, D), F32)],
        compiler_params=_params(1),
    )(act, w_down, h1, target, g_final, row_mask)


def _ffn_bwd(dh2, gate, up, h1, w_down_t, w_gate_t, w_up_t, w_out_t, g_ffn):
    t = h1.shape[0]
    r = _row_tile(t, FFN_TILE)

    def body(dh2_ref, gate_ref, up_ref, h1_ref, wd_ref, wg_ref, wu_ref, wo_ref, g_ref,
             dgate_ref, dup_ref, dh1_ref, dycat_ref, dg_ref):
        @pl.when(pl.program_id(0) == 0)
        def _():
            dg_ref[...] = jnp.zeros_like(dg_ref)

        dh2 = dh2_ref[...]
        dact = _dot(dh2.astype(BF16), wd_ref[...])
        gate = gate_ref[...]
        sg = _sigmoid(gate)
        dgate = (dact * up_ref[...] * (sg * (1.0 + gate * (1.0 - sg)))).astype(BF16)
        dup = (dact * (gate * sg)).astype(BF16)
        dgate_ref[...] = dgate
        dup_ref[...] = dup
        dhn = _dot(dgate, wg_ref[...]) + _dot(dup, wu_ref[...])
        h1 = h1_ref[...]
        rstd = lax.rsqrt(jnp.mean(h1 * h1, axis=-1, keepdims=True) + RMS_EPS)
        nrm = h1 * rstd
        dg_ref[...] += jnp.sum(dhn * nrm, axis=0, keepdims=True)
        dn = dhn * g_ref[...]
        dh1 = dh2 + rstd * (dn - nrm * jnp.mean(dn * nrm, axis=-1, keepdims=True))
        dh1_ref[...] = dh1
        dycat_ref[...] = _dot(dh1.astype(BF16), wo_ref[...])

    rows = lambda w: pl.BlockSpec((r, w), lambda i: (i, 0))
    return pl.pallas_call(
        body, name="ffn_bwd", grid=(t // r,),
        in_specs=[rows(D), rows(D_FF), rows(D_FF), rows(D), _const_spec((D, D_FF)), _const_spec((D_FF, D)),
                  _const_spec((D_FF, D)), _const_spec((D, D)), _const_spec((1, D))],
        out_specs=[rows(D_FF), rows(D_FF), rows(D), rows(D), _acc_spec((1, D))],
        out_shape=[jax.ShapeDtypeStruct((t, D_FF), BF16), jax.ShapeDtypeStruct((t, D_FF), BF16),
                   jax.ShapeDtypeStruct((t, D), F32), jax.ShapeDtypeStruct((t, D), F32),
                   jax.ShapeDtypeStruct((1, D), F32)],
        compiler_params=_params(1),
    )(dh2, gate, up, h1, w_down_t, w_gate_t, w_up_t, w_out_t, g_ffn)


def _conv_bwd(dycat, yc, u, conv_w, ln_g, ln_b, n_ex, lp, plan=None):
    r = CONV_TILE
    nt = lp // r
    hb = r // CONV_SUB
    nsub = r // CONV_SUB

    def ln_bwd(dy, yc_rows, live, lg, lb):
        mu = jnp.mean(yc_rows, axis=-1, keepdims=True)
        cen = yc_rows - mu
        rs = lax.rsqrt(jnp.mean(cen * cen, axis=-1, keepdims=True) + LN_EPS)
        yn = cen * rs
        out = yn * lg + lb
        so = _sigmoid(out)
        dout = jnp.where(live, dy * (so * (1.0 + out * (1.0 - so))), 0.0)
        dyn = dout * lg
        dyc = rs * (dyn - jnp.mean(dyn, axis=-1, keepdims=True) - yn * jnp.mean(dyn * yn, axis=-1, keepdims=True))
        return dyc, dout, yn

    def body(dy_ref, dyn_ref, yc_ref, ycn_ref, cur_ref, prev_ref, w_ref, lg_ref, lb_ref,
             du_ref, dw_ref, db_ref, dlg_ref, dlb_ref, glu, dycs, dwacc, glu_sh, dycs_sh):
        b = pl.program_id(0)
        i = pl.program_id(1)
        first = jnp.logical_and(b == 0, i == 0)

        @pl.when(first)
        def _():
            dwacc[...] = jnp.zeros_like(dwacc)
            db_ref[...] = jnp.zeros_like(db_ref)
            dlg_ref[...] = jnp.zeros_like(dlg_ref)
            dlb_ref[...] = jnp.zeros_like(dlb_ref)

        lg, lb = lg_ref[...], lb_ref[...]
        cur = cur_ref[...]
        sig = _sigmoid(cur[:, C_CONV:])
        glu[CONV_SUB:CONV_SUB + r, :] = cur[:, :C_CONV] * sig
        pv = prev_ref[...]
        glu[0:CONV_SUB, :] = jnp.where(i > 0, pv[:, :C_CONV] * _sigmoid(pv[:, C_CONV:]), 0.0)

        row = i * r + lax.broadcasted_iota(jnp.int32, (r, 1), 0)
        dyc, dout, yn = ln_bwd(dy_ref[...], yc_ref[...], row >= PAD_ROWS, lg, lb)
        dycs[0:r, :] = dyc
        dycn, _, _ = ln_bwd(dyn_ref[...], ycn_ref[...], i < nt - 1, lg, lb)
        dycs[r:r + CONV_SUB, :] = dycn
        db_ref[...] += jnp.sum(dyc, axis=0, keepdims=True)
        dlg_ref[...] += jnp.sum(dout * yn, axis=0, keepdims=True)
        dlb_ref[...] += jnp.sum(dout, axis=0, keepdims=True)

        _shifted_copies(glu, glu_sh, r)
        _shifted_copies(dycs, dycs_sh, r)
        w = w_ref[...]
        for j in range(nsub):
            r0 = j * CONV_SUB
            dblk = dycs[r0:r0 + CONV_SUB, :]
            dglu = jnp.zeros((CONV_SUB, C_CONV), F32)
            for k in range(CONV_W):
                dglu = dglu + w[k:k + 1, :] * _shifted_rows(dycs, dycs_sh, r0 + (CONV_W - 1) - k)
                prod = dblk * _shifted_rows(glu, glu_sh, r0 + CONV_LEAD + k)
                dwacc[k] += prod.reshape(CONV_SUB // SUBLANES, SUBLANES, C_CONV).sum(axis=0)
            sg = sig[r0:r0 + CONV_SUB, :]
            cv = cur[r0:r0 + CONV_SUB, :C_CONV]
            du_ref[r0:r0 + CONV_SUB, :C_CONV] = (dglu * sg).astype(BF16)
            du_ref[r0:r0 + CONV_SUB, C_CONV:] = (dglu * cv * sg * (1.0 - sg)).astype(BF16)

        @pl.when(jnp.logical_and(b == n_ex - 1, i == nt - 1))
        def _():
            dw_ref[...] = jnp.sum(dwacc[...], axis=1)

    t = n_ex * lp
    cur_rows = lambda w, col: pl.BlockSpec((r, w), lambda b, i: (b * nt + i, col))
    nxt_rows = lambda w, col: pl.BlockSpec(
        (CONV_SUB, w), lambda b, i: (jnp.minimum((b * nt + i + 1) * hb, n_ex * nt * hb - 1), col))
    return _call(
        body, name="conv_bwd", grid=(n_ex, nt),
        in_specs=[cur_rows(C_CONV, 0), nxt_rows(C_CONV, 0), cur_rows(C_CONV, 0), nxt_rows(C_CONV, 0),
                  cur_rows(2 * C_CONV, 0),
                  pl.BlockSpec((CONV_SUB, 2 * C_CONV), lambda b, i: (jnp.maximum((b * nt + i) * hb - 1, 0), 0)),
                  _const_spec((32, C_CONV)), _const_spec((1, C_CONV)), _const_spec((1, C_CONV))],
        out_specs=[cur_rows(2 * C_CONV, 0), _acc_spec((32, C_CONV)), _acc_spec((1, C_CONV)),
                   _acc_spec((1, C_CONV)), _acc_spec((1, C_CONV))],
        out_shape=[jax.ShapeDtypeStruct((t, 2 * C_CONV), BF16), jax.ShapeDtypeStruct((32, C_CONV), F32),
                   jax.ShapeDtypeStruct((1, C_CONV), F32), jax.ShapeDtypeStruct((1, C_CONV), F32),
                   jax.ShapeDtypeStruct((1, C_CONV), F32)],
        scratch_shapes=[pltpu.VMEM((r + CONV_SUB, C_CONV), F32), pltpu.VMEM((r + CONV_SUB, C_CONV), F32),
                        pltpu.VMEM((32, 8, C_CONV), F32),
                        pltpu.VMEM((SUBLANES - 1, r + CONV_SUB - SUBLANES, C_CONV), F32),
                        pltpu.VMEM((SUBLANES - 1, r + CONV_SUB - SUBLANES, C_CONV), F32)],
        plan=plan,
    )(dycat, dycat, yc, yc, u, u, conv_w, ln_g, ln_b)


def _gla_bwd_per_head(dycat, u, states, w2, gb, ng, n_ex, lp, plan=None):
    nc = lp // CHUNK
    t = n_ex * lp

    def body(dy_ref, qk_ref, v_ref, g_ref, lr_ref, st_ref, w2_ref, gb_ref, ng_ref,
             du_ref, dw2_ref, dgb_ref, dng_ref, dstate):
        n = pl.program_id(0)
        chunk = nc - 1 - n

        @pl.when(n == 0)
        def _():
            dw2_ref[...] = jnp.zeros_like(dw2_ref)
            dgb_ref[...] = jnp.zeros_like(dgb_ref)
            dng_ref[...] = jnp.zeros_like(dng_ref)
            dstate[...] = jnp.zeros_like(dstate)

        for e in range(n_ex):
            one_example(e, chunk, dy_ref, qk_ref, v_ref, g_ref, lr_ref, st_ref, w2_ref, gb_ref, ng_ref,
                        du_ref, dw2_ref, dgb_ref, dng_ref, dstate)

    def one_example(e, chunk, dy_ref, qk_ref, v_ref, g_ref, lr_ref, st_ref, w2_ref, gb_ref, ng_ref,
                    du_ref, dw2_ref, dgb_ref, dng_ref, dstate):
        dy_ref, qk_ref, v_ref, g_ref, lr_ref, st_ref = (r.at[e] for r in (dy_ref, qk_ref, v_ref, g_ref, lr_ref, st_ref))
        du_ref, dstate = du_ref.at[e], dstate.at[e]
        qk = qk_ref[...]
        q, k = qk[:, :GLA_K], qk[:, GLA_K:]
        lr = lr_ref[...]
        z, a, live = _gla_gates(lr, w2_ref[...], gb_ref[...], chunk == 0)
        causal = _tri(True)
        b = jnp.dot(causal.astype(F32), a, preferred_element_type=F32, precision=lax.Precision.HIGHEST)
        bl = b[CHUNK - 1:CHUNK, :]
        e_pos, e_neg, e_dec = jnp.exp(b), jnp.exp(-b), jnp.exp(bl - b)
        q_f = q * (DK ** -0.5) * e_pos
        k_f = k * e_neg
        kd_f = k * e_dec
        q_in, k_in, k_dec = q_f.astype(BF16), k_f.astype(BF16), kd_f.astype(BF16)
        decay = jnp.exp(bl)
        v = v_ref[...]
        g = g_ref[...]
        dy = dy_ref[...]
        ngv = ng_ref[...]
        st = st_ref[...]
        st_b = st.astype(BF16)
        dst = dstate[...]
        dst_b = dst.astype(BF16)
        dqs, dks, dvs, dgs, dbs, dbls, new_dst = [], [], [], [], [], [], []
        dng = jnp.zeros((1, DV), F32)
        for h in range(N_HEADS):
            ks = slice(h * DK, (h + 1) * DK)
            vs = slice(h * DV, (h + 1) * DV)
            qh, kh, kdh = q_in[:, ks], k_in[:, ks], k_dec[:, ks]
            vh = v[:, vs].astype(BF16)
            s = jnp.where(causal, _dot_nt(qh, kh), 0.0).astype(BF16)
            o = _dot(s, vh) + _dot_nt(qh, st_b[:, ks])
            rstd = lax.rsqrt(jnp.mean(o * o, axis=-1, keepdims=True) + RMS_EPS)
            nrm = o * rstd
            gh = g[:, vs]
            sg = _sigmoid(gh)
            dyh = dy[:, vs]
            dgs.append(dyh * nrm * ngv * (sg * (1.0 + gh * (1.0 - sg))))
            dt = dyh * (gh * sg)
            dng = dng + jnp.sum(dt * nrm, axis=0, keepdims=True)
            dn = dt * ngv
            do = (rstd * (dn - nrm * jnp.mean(dn * nrm, axis=-1, keepdims=True))).astype(BF16)
            da = jnp.where(causal, _dot_nt(do, vh), 0.0).astype(BF16)
            dvs.append(_dot_tn(s, do) + _dot_nt(kdh, dst_b[:, ks]))
            dq_in = _dot(da, kh) + _dot(do, st_b[:, ks])
            dk_in = _dot_tn(da, qh)
            dk_dec = _dot(vh, dst_b[:, ks])
            new_dst.append(_dot_tn(do, qh) + decay[:, ks] * dst[:, ks])
            dbls.append(jnp.sum(dk_dec * kd_f[:, ks], axis=0, keepdims=True)
                        + decay[:, ks] * jnp.sum(dst[:, ks] * st[:, ks], axis=0, keepdims=True))
            dqs.append(dq_in * (DK ** -0.5) * e_pos[:, ks])
            dks.append(dk_in * e_neg[:, ks] + dk_dec * e_dec[:, ks])
            dbs.append(dq_in * q_f[:, ks] - dk_in * k_f[:, ks] - dk_dec * kd_f[:, ks])
        dstate[...] = jnp.concatenate(new_dst, axis=1)
        row = lax.broadcasted_iota(jnp.int32, (CHUNK, 1), 0)
        db = jnp.concatenate(dbs, axis=1) + jnp.where(row == CHUNK - 1, jnp.concatenate(dbls, axis=1), 0.0)
        da_log = jnp.dot(_tri(False).astype(F32), db, preferred_element_type=F32, precision=lax.Precision.HIGHEST)
        dz = jnp.where(live, da_log * (1.0 - _sigmoid(z)) * (1.0 / GATE_TAU), 0.0)
        dz_b = dz.astype(BF16)
        du_ref[:, 0:GLA_K] = jnp.concatenate(dqs, axis=1).astype(BF16)
        du_ref[:, GLA_K:2 * GLA_K] = jnp.concatenate(dks, axis=1).astype(BF16)
        du_ref[:, 2 * GLA_K:2 * GLA_K + GLA_V] = jnp.concatenate(dvs, axis=1).astype(BF16)
        du_ref[:, 2 * GLA_K + GLA_V:2 * GLA_K + 2 * GLA_V] = jnp.concatenate(dgs, axis=1).astype(BF16)
        du_ref[:, 2 * GLA_K + 2 * GLA_V:] = _dot_nt(dz_b, w2_ref[...]).astype(BF16)
        dw2_ref[...] += _dot_tn(lr.astype(BF16), dz_b)
        dgb_ref[...] += jnp.sum(dz, axis=0, keepdims=True)
        dng_ref[...] += dng

    u3 = u.reshape(n_ex, lp, D_IN_PAD)
    rev = lambda w, col: pl.BlockSpec((n_ex, CHUNK, w), lambda n: (0, nc - 1 - n, col))
    (du, d_w2, d_gb, d_ng), extra = _call(
        body, name="gla_bwd", grid=(nc,),
        in_specs=[rev(GLA_V, 1), rev(2 * GLA_K, 2), rev(GLA_V, 3), rev(GLA_V, 4), rev(128, 20),
                  pl.BlockSpec((n_ex, DV, GLA_K), lambda n: (0, nc - 1 - n, 0)),
                  _const_spec((128, GLA_K)), _const_spec((1, GLA_K)), _const_spec((1, DV))],
        out_specs=[rev(D_GLA_IN, 0), _acc_spec((128, GLA_K)), _acc_spec((1, GLA_K)), _acc_spec((1, DV))],
        out_shape=[jax.ShapeDtypeStruct((n_ex, lp, D_GLA_IN), BF16), jax.ShapeDtypeStruct((128, GLA_K), F32),
                   jax.ShapeDtypeStruct((1, GLA_K), F32), jax.ShapeDtypeStruct((1, DV), F32)],
        scratch_shapes=[pltpu.VMEM((n_ex, DV, GLA_K), F32)],
        plan=plan,
    )(dycat.reshape(n_ex, lp, D), u3, u3, u3, u3, states, w2, gb, ng)
    return (du.reshape(t, D_GLA_IN), d_w2, d_gb, d_ng), extra


HEAD_ROWS_ALL = N_HEADS * CHUNK


def _head_of(shape, axis, per_head):
    return lax.broadcasted_iota(jnp.int32, shape, axis) // per_head


def _expand(x, lanes_per_head):
    rows, lanes = HEAD_ROWS_ALL, x.shape[1]
    keep = _head_of((rows, lanes), 0, CHUNK) == _head_of((rows, lanes), 1, lanes_per_head)
    return jnp.where(keep, jnp.tile(x, (N_HEADS, 1)), 0.0)


def _expand_lanes(x):
    rows, w = x.shape
    keep = _head_of((rows, N_HEADS * w), 0, CHUNK) == _head_of((rows, N_HEADS * w), 1, w)
    return jnp.where(keep, jnp.tile(x, (1, N_HEADS)), 0.0)


def _expand_state(st):
    rows, lanes = N_HEADS * DV, st.shape[1]
    keep = _head_of((rows, lanes), 0, DV) == _head_of((rows, lanes), 1, DK)
    return jnp.where(keep, jnp.tile(st, (N_HEADS, 1)), 0.0)


def _fold(t, rows_per_head):
    lane_head = _head_of((rows_per_head, t.shape[1]), 1, DK)
    out = jnp.where(lane_head == 0, t[0:rows_per_head], 0.0)
    for h in range(1, N_HEADS):
        out = out + jnp.where(lane_head == h, t[h * rows_per_head:(h + 1) * rows_per_head], 0.0)
    return out


def _rows_by_head(x):
    return jnp.concatenate([x[:, h * DV:(h + 1) * DV] for h in range(N_HEADS)], axis=0)


def _lanes_by_head(x):
    return jnp.concatenate([x[h * CHUNK:(h + 1) * CHUNK] for h in range(N_HEADS)], axis=1)


def _running_sum(a, lower):
    hi = a.astype(BF16)
    rest = a - hi.astype(F32)
    mid = rest.astype(BF16)
    lo = (rest - mid.astype(F32)).astype(BF16)
    w = a.shape[1]
    parts = _dot(_tri(lower).astype(F32).astype(BF16), jnp.concatenate([hi, mid, lo], axis=1))
    return parts[:, :w] + parts[:, w:2 * w] + parts[:, 2 * w:]


def _stacked_causal():
    i = lax.broadcasted_iota(jnp.int32, (HEAD_ROWS_ALL, CHUNK), 0) % CHUNK
    j = lax.broadcasted_iota(jnp.int32, (HEAD_ROWS_ALL, CHUNK), 1)
    return i >= j


def _gla_chunk(q, k, v, lr, st, w2, gb, first_chunk):
    z, a, live = _gla_gates(lr, w2, gb, first_chunk)
    b = _running_sum(a, True)
    bl = b[CHUNK - 1:CHUNK, :]
    e_pos, e_neg, e_dec = jnp.exp(b), jnp.exp(-b), jnp.exp(bl - b)
    q_f, k_f, kd_f = q * (DK ** -0.5) * e_pos, k * e_neg, k * e_dec
    qx = _expand(q_f, DK).astype(BF16)
    k_in, k_dec, v_b = k_f.astype(BF16), kd_f.astype(BF16), v.astype(BF16)
    s = jnp.where(_stacked_causal(), _dot_nt(qx, k_in), 0.0).astype(BF16)
    p = _dot(s, v_b)
    o = (jnp.concatenate([p[h * CHUNK:(h + 1) * CHUNK, h * DV:(h + 1) * DV] for h in range(N_HEADS)], axis=0)
         + _dot_nt(qx, st.astype(BF16)))
    return dict(z=z, live=live, bl=bl, e_pos=e_pos, e_neg=e_neg, e_dec=e_dec, q_f=q_f, k_f=k_f, kd_f=kd_f,
                qx=qx, k_in=k_in, k_dec=k_dec, v_b=v_b, s=s, o=o, decay=jnp.exp(bl))


def _gla_fwd(u, w2, gb, ng, n_ex, lp, plan=None):
    nc = lp // CHUNK
    t = n_ex * lp

    def body(qk_ref, v_ref, g_ref, lr_ref, w2_ref, gb_ref, ng_ref, y_ref, st_ref, state):
        n = pl.program_id(0)

        @pl.when(n == 0)
        def _():
            state[...] = jnp.zeros_like(state)

        for e in range(n_ex):
            st = state[e]
            st_ref[e] = st
            qk = qk_ref[e]
            c = _gla_chunk(qk[:, :GLA_K], qk[:, GLA_K:], v_ref[e], lr_ref[e], st, w2_ref[...], gb_ref[...], n == 0)
            o = c["o"]
            rstd = lax.rsqrt(jnp.mean(o * o, axis=-1, keepdims=True) + RMS_EPS)
            g = _rows_by_head(g_ref[e])
            y_ref[e] = _lanes_by_head(o * rstd * ng_ref[...] * (g * _sigmoid(g))).astype(BF16)
            state[e] = c["decay"] * st + _fold(_dot_tn(c["v_b"], c["k_dec"]), DV)

    u3 = u.reshape(n_ex, lp, D_IN_PAD)
    blk = lambda w, col: pl.BlockSpec((n_ex, CHUNK, w), lambda n: (0, n, col))
    (y, states), extra = _call(
        body, name="gla_fwd", grid=(nc,),
        in_specs=[blk(2 * GLA_K, 2), blk(GLA_V, 3), blk(GLA_V, 4), blk(128, 20),
                  _const_spec((128, GLA_K)), _const_spec((1, GLA_K)), _const_spec((1, DV))],
        out_specs=[blk(GLA_V, 0), pl.BlockSpec((n_ex, DV, GLA_K), lambda n: (0, n, 0))],
        out_shape=[jax.ShapeDtypeStruct((n_ex, lp, GLA_V), BF16),
                   jax.ShapeDtypeStruct((n_ex, nc * DV, GLA_K), F32)],
        scratch_shapes=[pltpu.VMEM((n_ex, DV, GLA_K), F32)],
        plan=plan,
    )(u3, u3, u3, u3, w2, gb, ng)
    return (y.reshape(t, GLA_V), states), extra


def _gla_bwd(dycat, u, states, w2, gb, ng, n_ex, lp, plan=None):
    nc = lp // CHUNK
    t = n_ex * lp

    def body(dy_ref, qk_ref, v_ref, g_ref, lr_ref, st_ref, w2_ref, gb_ref, ng_ref,
             du_ref, dw2_ref, dgb_ref, dng_ref, dstate):
        n = pl.program_id(0)
        chunk = nc - 1 - n

        @pl.when(n == 0)
        def _():
            dw2_ref[...] = jnp.zeros_like(dw2_ref)
            dgb_ref[...] = jnp.zeros_like(dgb_ref)
            dng_ref[...] = jnp.zeros_like(dng_ref)
            dstate[...] = jnp.zeros_like(dstate)

        for e in range(n_ex):
            qk = qk_ref[e]
            lr = lr_ref[e]
            st = st_ref[e]
            dst = dstate[e]
            c = _gla_chunk(qk[:, :GLA_K], qk[:, GLA_K:], v_ref[e], lr, st, w2_ref[...], gb_ref[...], chunk == 0)
            qx, k_in, k_dec, v_b, s, o = c["qx"], c["k_in"], c["k_dec"], c["v_b"], c["s"], c["o"]
            ngv = ng_ref[...]
            rstd = lax.rsqrt(jnp.mean(o * o, axis=-1, keepdims=True) + RMS_EPS)
            nrm = o * rstd
            g = _rows_by_head(g_ref[e])
            dy = _rows_by_head(dy_ref[e])
            sg = _sigmoid(g)
            dg = dy * nrm * ngv * (sg * (1.0 + g * (1.0 - sg)))
            dt = dy * (g * sg)
            dng_ref[...] += jnp.sum(dt * nrm, axis=0, keepdims=True)
            dn = dt * ngv
            do = rstd * (dn - nrm * jnp.mean(dn * nrm, axis=-1, keepdims=True))
            do_b = do.astype(BF16)
            dox = _expand_lanes(do).astype(BF16)
            dstx = _expand_state(dst).astype(BF16)
            da = jnp.where(_stacked_causal(), _dot_nt(dox, v_b), 0.0).astype(BF16)
            dv = _dot_tn(s, dox) + _dot_nt(k_dec, dstx)
            dq_in = _fold(_dot(da, k_in) + _dot(do_b, st.astype(BF16)), CHUNK)
            dk_in = _dot_tn(da, qx)
            dk_dec = _dot(v_b, dstx)
            dstate[e] = _dot_tn(do_b, qx) + c["decay"] * dst
            dbl = (jnp.sum(dk_dec * c["kd_f"], axis=0, keepdims=True)
                   + c["decay"] * jnp.sum(dst * st, axis=0, keepdims=True))
            dq = dq_in * (DK ** -0.5) * c["e_pos"]
            dk = dk_in * c["e_neg"] + dk_dec * c["e_dec"]
            db = dq_in * c["q_f"] - dk_in * c["k_f"] - dk_dec * c["kd_f"]
            row = lax.broadcasted_iota(jnp.int32, (CHUNK, 1), 0)
            da_log = _running_sum(db + jnp.where(row == CHUNK - 1, dbl, 0.0), False)
            dz = jnp.where(c["live"], da_log * (1.0 - _sigmoid(c["z"])) * (1.0 / GATE_TAU), 0.0)
            dz_b = dz.astype(BF16)
            out = du_ref.at[e]
            out[:, 0:GLA_K] = dq.astype(BF16)
            out[:, GLA_K:2 * GLA_K] = dk.astype(BF16)
            out[:, 2 * GLA_K:2 * GLA_K + GLA_V] = dv.astype(BF16)
            out[:, 2 * GLA_K + GLA_V:2 * GLA_K + 2 * GLA_V] = _lanes_by_head(dg).astype(BF16)
            out[:, 2 * GLA_K + 2 * GLA_V:] = _dot_nt(dz_b, w2_ref[...]).astype(BF16)
            dw2_ref[...] += _dot_tn(lr.astype(BF16), dz_b)
            dgb_ref[...] += jnp.sum(dz, axis=0, keepdims=True)

    u3 = u.reshape(n_ex, lp, D_IN_PAD)
    rev = lambda w, col: pl.BlockSpec((n_ex, CHUNK, w), lambda n: (0, nc - 1 - n, col))
    (du, d_w2, d_gb, d_ng), extra = _call(
        body, name="gla_bwd", grid=(nc,),
        in_specs=[rev(GLA_V, 1), rev(2 * GLA_K, 2), rev(GLA_V, 3), rev(GLA_V, 4), rev(128, 20),
                  pl.BlockSpec((n_ex, DV, GLA_K), lambda n: (0, nc - 1 - n, 0)),
                  _const_spec((128, GLA_K)), _const_spec((1, GLA_K)), _const_spec((1, DV))],
        out_specs=[rev(D_GLA_IN, 0), _acc_spec((128, GLA_K)), _acc_spec((1, GLA_K)), _acc_spec((1, DV))],
        out_shape=[jax.ShapeDtypeStruct((n_ex, lp, D_GLA_IN), BF16), jax.ShapeDtypeStruct((128, GLA_K), F32),
                   jax.ShapeDtypeStruct((1, GLA_K), F32), jax.ShapeDtypeStruct((1, DV), F32)],
        scratch_shapes=[pltpu.VMEM((n_ex, DV, GLA_K), F32)],
        plan=plan,
    )(dycat.reshape(n_ex, lp, D), u3, u3, u3, u3, states, w2, gb, ng)
    return (du.reshape(t, D_GLA_IN), d_w2, d_gb, d_ng), extra


def _in_proj_bwd(du_conv, du_gla, w_in_t_conv, w_in_t_gla, h0, dh1, g_mix, plan=None):
    t = h0.shape[0]
    r = _row_tile(t, 384)

    def body(dc_ref, dg_ref, wc_ref, wg_ref, h_ref, dh1_ref, g_ref, dh0_ref, dgm_ref):
        @pl.when(pl.program_id(0) == 0)
        def _():
            dgm_ref[...] = jnp.zeros_like(dgm_ref)

        dhn = _dot(dc_ref[...], wc_ref[...]) + _dot(dg_ref[...], wg_ref[...])
        h = h_ref[...]
        rstd = lax.rsqrt(jnp.mean(h * h, axis=-1, keepdims=True) + RMS_EPS)
        nrm = h * rstd
        dgm_ref[...] += jnp.sum(dhn * nrm, axis=0, keepdims=True)
        dn = dhn * g_ref[...]
        dh0_ref[...] = dh1_ref[...] + rstd * (dn - nrm * jnp.mean(dn * nrm, axis=-1, keepdims=True))

    rows = lambda w: pl.BlockSpec((r, w), lambda i: (i, 0))
    return _call(
        body, name="in_proj_bwd", grid=(t // r,),
        in_specs=[rows(2 * C_CONV), rows(D_GLA_IN), _const_spec((2 * C_CONV, D)), _const_spec((D_GLA_IN, D)),
                  rows(D), rows(D), _const_spec((1, D))],
        out_specs=[rows(D), _acc_spec((1, D))],
        out_shape=[jax.ShapeDtypeStruct((t, D), F32), jax.ShapeDtypeStruct((1, D), F32)],
        plan=plan,
    )(du_conv, du_gla, w_in_t_conv, w_in_t_gla, h0, dh1, g_mix)


def _wgrad(x, dy, name):
    t, m = x.shape
    n = dy.shape[1]
    tk = t // 3 if t % (3 * 128) == 0 else _row_tile(t, 384)
    tm = m if m <= D_GLA_IN else m // 2
    tn = n

    def body(x_ref, dy_ref, o_ref):
        @pl.when(pl.program_id(2) == 0)
        def _():
            o_ref[...] = jnp.zeros_like(o_ref)

        o_ref[...] += _dot_tn(x_ref[...].astype(BF16), dy_ref[...].astype(BF16))

    return pl.pallas_call(
        body, name=name, grid=(m // tm, n // tn, t // tk),
        in_specs=[pl.BlockSpec((tk, tm), lambda i, j, k: (k, i)), pl.BlockSpec((tk, tn), lambda i, j, k: (k, j))],
        out_specs=pl.BlockSpec((tm, tn), lambda i, j, k: (i, j)),
        out_shape=jax.ShapeDtypeStruct((m, n), F32),
        compiler_params=_params(3),
    )(x, dy)


def _mesh_pos():
    return lax.axis_index("x"), lax.axis_index("y"), lax.axis_index("c")


def _other_chips(x, y):
    return [(1 - x, y), (x, 1 - y), (1 - x, 1 - y)]


HBM_SPEC = pl.BlockSpec(memory_space=pltpu.HBM)


def _gather_shards(shards):
    n = len(shards)

    def body(*refs):
        ins, outs = refs[:n], refs[n:2 * n]
        send_sems, recv_sems, local_sems = refs[2 * n:]
        x, y, c = _mesh_pos()
        mine = 2 * x + y
        chips = _other_chips(x, y)
        local = [pltpu.make_async_copy(ins[a], outs[a].at[mine], local_sems.at[a]) for a in range(n)]
        for cp in local:
            cp.start()

        def remote(a, k, block):
            px, py = chips[k]
            return pltpu.make_async_remote_copy(
                src_ref=ins[a], dst_ref=outs[a].at[block], send_sem=send_sems.at[3 * a + k],
                recv_sem=recv_sems.at[3 * a + k], device_id=(px, py, c), device_id_type=MESH)

        sends = [remote(a, k, mine) for a in range(n) for k in range(3)]
        for cp in sends:
            cp.start()
        for a in range(n):
            for k, (px, py) in enumerate(chips):
                remote(a, k, 2 * px + py).wait_recv()
        for cp in sends:
            cp.wait_send()
        for cp in local:
            cp.wait()

    return pl.pallas_call(
        body, name="gather_shards",
        in_specs=[HBM_SPEC] * n, out_specs=[HBM_SPEC] * n,
        out_shape=[jax.ShapeDtypeStruct((N_CHIPS,) + s.shape, s.dtype) for s in shards],
        scratch_shapes=[pltpu.SemaphoreType.DMA((3 * n,)), pltpu.SemaphoreType.DMA((3 * n,)),
                        pltpu.SemaphoreType.DMA((n,))],
        compiler_params=pltpu.CompilerParams(has_side_effects=True),
    )(*shards)


def _send_half_to_sibling(g2):
    def body(g_ref, recv_ref, send_sem, recv_sem):
        x, y, c = _mesh_pos()
        cp = pltpu.make_async_remote_copy(
            src_ref=g_ref.at[1 - c], dst_ref=recv_ref, send_sem=send_sem, recv_sem=recv_sem,
            device_id=(x, y, 1 - c), device_id_type=MESH)
        cp.start()
        cp.wait()

    return pl.pallas_call(
        body, name="rs_to_sibling", in_specs=[HBM_SPEC], out_specs=HBM_SPEC,
        out_shape=jax.ShapeDtypeStruct(g2.shape[1:], g2.dtype),
        scratch_shapes=[pltpu.SemaphoreType.DMA(()), pltpu.SemaphoreType.DMA(())],
        compiler_params=pltpu.CompilerParams(has_side_effects=True),
    )(g2)


def _add_own_half(g2, recv, c):
    rows = N_CHIPS * HALF_ROWS
    tr = 512
    g2f = g2.reshape(2, rows, D)
    recvf = recv.reshape(rows, D)

    def body(c_ref, a_ref, b_ref, o_ref):
        o_ref[...] = a_ref[0] + b_ref[...]

    out = pl.pallas_call(
        body, name="rs_add_halves",
        grid_spec=pltpu.PrefetchScalarGridSpec(
            num_scalar_prefetch=1, grid=(rows // tr,),
            in_specs=[pl.BlockSpec((1, tr, D), lambda i, s: (s[0], i, 0)), pl.BlockSpec((tr, D), lambda i, s: (i, 0))],
            out_specs=pl.BlockSpec((tr, D), lambda i, s: (i, 0))),
        out_shape=jax.ShapeDtypeStruct((rows, D), F32),
        compiler_params=_params(1),
    )(jnp.reshape(c, (1,)).astype(jnp.int32), g2f, recvf)
    return out.reshape(N_CHIPS, HALF_ROWS, D)


def _exchange_chip_sums(p):
    def body(p_ref, out_ref, send_sems, recv_sems, local_sem):
        x, y, c = _mesh_pos()
        mine = 2 * x + y
        chips = _other_chips(x, y)
        local = pltpu.make_async_copy(p_ref.at[mine], out_ref.at[mine], local_sem)
        local.start()

        def remote(k, src_block, dst_block):
            px, py = chips[k]
            return pltpu.make_async_remote_copy(
                src_ref=p_ref.at[src_block], dst_ref=out_ref.at[dst_block], send_sem=send_sems.at[k],
                recv_sem=recv_sems.at[k], device_id=(px, py, c), device_id_type=MESH)

        sends = [remote(k, 2 * px + py, mine) for k, (px, py) in enumerate(chips)]
        for cp in sends:
            cp.start()
        for k, (px, py) in enumerate(chips):
            remote(k, mine, 2 * px + py).wait_recv()
        for cp in sends:
            cp.wait_send()
        local.wait()

    return pl.pallas_call(
        body, name="rs_chip_exchange", in_specs=[HBM_SPEC], out_specs=HBM_SPEC,
        out_shape=jax.ShapeDtypeStruct(p.shape, p.dtype),
        scratch_shapes=[pltpu.SemaphoreType.DMA((3,)), pltpu.SemaphoreType.DMA((3,)), pltpu.SemaphoreType.DMA(())],
        compiler_params=pltpu.CompilerParams(has_side_effects=True),
    )(p)


def _sum_chips(parts):
    tr = 512

    def body(p_ref, o_ref):
        o_ref[...] = ((p_ref[0] + p_ref[1]) + p_ref[2]) + p_ref[3]

    return pl.pallas_call(
        body, name="rs_sum_chips", grid=(HALF_ROWS // tr,),
        in_specs=[pl.BlockSpec((N_CHIPS, tr, D), lambda i: (0, i, 0))],
        out_specs=pl.BlockSpec((tr, D), lambda i: (i, 0)),
        out_shape=jax.ShapeDtypeStruct((HALF_ROWS, D), F32),
        compiler_params=_params(1),
    )(parts)


def _share_with_sibling(half):
    def body(h_ref, out_ref, send_sem, recv_sem, local_sem):
        x, y, c = _mesh_pos()
        local = pltpu.make_async_copy(h_ref, out_ref.at[c], local_sem)
        local.start()
        cp = pltpu.make_async_remote_copy(
            src_ref=h_ref, dst_ref=out_ref.at[c], send_sem=send_sem, recv_sem=recv_sem,
            device_id=(x, y, 1 - c), device_id_type=MESH)
        cp.start()
        pltpu.make_async_remote_copy(
            src_ref=h_ref, dst_ref=out_ref.at[1 - c], send_sem=send_sem, recv_sem=recv_sem,
            device_id=(x, y, 1 - c), device_id_type=MESH).wait_recv()
        cp.wait_send()
        local.wait()

    return pl.pallas_call(
        body, name="rs_share_sibling", in_specs=[HBM_SPEC], out_specs=HBM_SPEC,
        out_shape=jax.ShapeDtypeStruct((2,) + half.shape, half.dtype),
        scratch_shapes=[pltpu.SemaphoreType.DMA(()), pltpu.SemaphoreType.DMA(()), pltpu.SemaphoreType.DMA(())],
        compiler_params=pltpu.CompilerParams(has_side_effects=True),
    )(half)


def _adam_update(g, w, m, v):
    m2 = ADAM_B1 * m + (1.0 - ADAM_B1) * g
    v2 = ADAM_B2 * v + (1.0 - ADAM_B2) * (g * g)
    m_hat = m2 / (1.0 - ADAM_B1 ** ADAM_STEP)
    v_hat = v2 / (1.0 - ADAM_B2 ** ADAM_STEP)
    delta = -ADAM_LR * (m_hat / (jnp.sqrt(v_hat) + ADAM_EPS) + ADAM_WD * w)
    return delta, m2, v2


def _adamw_slab(g, w, m, v):
    rows = g.shape[0]
    tr = 256

    def body(g_ref, w_ref, m_ref, v_ref, d_ref, m2_ref, v2_ref):
        d_ref[...], m2_ref[...], v2_ref[...] = _adam_update(g_ref[...], w_ref[...], m_ref[...], v_ref[...])

    spec = pl.BlockSpec((tr, D), lambda i: (i, 0))
    return pl.pallas_call(
        body, name="adamw_slab", grid=(rows // tr,), in_specs=[spec] * 4, out_specs=[spec] * 3,
        out_shape=[jax.ShapeDtypeStruct((rows, D), F32)] * 3,
        compiler_params=_params(1),
    )(g, w, m, v)


def _allreduce_small_adamw(part, w, m, v):
    def body(p_ref, w_ref, m_ref, v_ref, g_ref, d_ref, m2_ref, v2_ref, slots, send_sems, recv_sems):
        x, y, c = _mesh_pos()
        mine = 4 * x + 2 * y + c
        peers = [(px, py, pc) for px in (x, 1 - x) for py in (y, 1 - y) for pc in (c, 1 - c)][1:]

        def remote(k, slot):
            return pltpu.make_async_remote_copy(
                src_ref=p_ref, dst_ref=slots.at[slot], send_sem=send_sems.at[k], recv_sem=recv_sems.at[k],
                device_id=peers[k], device_id_type=MESH)

        sends = [remote(k, mine) for k in range(7)]
        for cp in sends:
            cp.start()
        slots[mine] = p_ref[...]
        for k, (px, py, pc) in enumerate(peers):
            remote(k, 4 * px + 2 * py + pc).wait_recv()
        for cp in sends:
            cp.wait_send()
        g = slots[0]
        for d in range(1, 8):
            g = g + slots[d]
        g_ref[...] = g
        d_ref[...], m2_ref[...], v2_ref[...] = _adam_update(g, w_ref[...], m_ref[...], v_ref[...])

    vm = pl.BlockSpec(memory_space=pltpu.VMEM)
    shape = jax.ShapeDtypeStruct(part.shape, F32)
    return pl.pallas_call(
        body, name="small_allreduce_adamw", in_specs=[vm] * 4, out_specs=[vm] * 4, out_shape=[shape] * 4,
        scratch_shapes=[pltpu.VMEM((8,) + part.shape, F32), pltpu.SemaphoreType.DMA((7,)),
                        pltpu.SemaphoreType.DMA((7,))],
        compiler_params=pltpu.CompilerParams(has_side_effects=True),
    )(part, w, m, v)


def _half(ref, c, axis):
    n = ref.shape[axis] // 2
    return ref.at[(slice(None),) * axis + (pl.ds(c * n, n),)]


def _remote(src, dst, send_sem, recv_sem, device):
    return pltpu.make_async_remote_copy(src_ref=src, dst_ref=dst, send_sem=send_sem, recv_sem=recv_sem,
                                        device_id=device, device_id_type=MESH)


def _gather_weights(split, axes, whole):
    ns, n = len(split), len(split) + len(whole)

    def body(*refs):
        ins, outs = refs[:n], refs[n:2 * n]
        ici_send, ici_recv, d2d_send, d2d_recv, local_sems = refs[2 * n:]
        x, y, c = _mesh_pos()
        mine = 2 * x + y
        chips = _other_chips(x, y)
        local = [pltpu.make_async_copy(ins[a], outs[a].at[mine], local_sems.at[a]) for a in range(n)]
        for cp in local:
            cp.start()

        def ici(a, k, block):
            px, py = chips[k]
            src, dst = ins[a], outs[a].at[block]
            if a < ns:
                src, dst = _half(src, c, axes[a]), _half(dst, c, axes[a])
            return _remote(src, dst, ici_send.at[3 * a + k], ici_recv.at[3 * a + k], (px, py, c))

        def d2d(a, k, block, half):
            part = _half(outs[a].at[block], half, axes[a])
            return _remote(part, part, d2d_send.at[3 * a + k], d2d_recv.at[3 * a + k], (x, y, 1 - c))

        sends = [ici(a, k, mine) for a in range(n) for k in range(3)]
        for cp in sends:
            cp.start()
        for a in range(n):
            for k, (px, py) in enumerate(chips):
                ici(a, k, 2 * px + py).wait_recv()
                if a < ns:
                    sends.append(d2d(a, k, 2 * px + py, c))
                    sends[-1].start()
        for a in range(ns):
            for k, (px, py) in enumerate(chips):
                d2d(a, k, 2 * px + py, 1 - c).wait_recv()
        for cp in sends:
            cp.wait_send()
        for cp in local:
            cp.wait()

    arrays = list(split) + list(whole)
    return pl.pallas_call(
        body, name="gather_weights", in_specs=[HBM_SPEC] * n, out_specs=[HBM_SPEC] * n,
        out_shape=[jax.ShapeDtypeStruct((N_CHIPS,) + s.shape, s.dtype) for s in arrays],
        scratch_shapes=[pltpu.SemaphoreType.DMA((3 * n,)), pltpu.SemaphoreType.DMA((3 * n,)),
                        pltpu.SemaphoreType.DMA((3 * ns,)), pltpu.SemaphoreType.DMA((3 * ns,)),
                        pltpu.SemaphoreType.DMA((n,))],
        compiler_params=pltpu.CompilerParams(has_side_effects=True),
    )(*arrays)


def _rs_to_sibling(gs):
    n = len(gs)

    def body(*refs):
        ins, outs, send_sems, recv_sems = refs[:n], refs[n:2 * n], refs[2 * n], refs[2 * n + 1]
        x, y, c = _mesh_pos()
        copies = [_remote(_half(ins[a], 1 - c, 2), outs[a], send_sems.at[a], recv_sems.at[a], (x, y, 1 - c))
                  for a in range(n)]
        for cp in copies:
            cp.start()
        for cp in copies:
            cp.wait()

    return pl.pallas_call(
        body, name="rs_to_sibling", in_specs=[HBM_SPEC] * n, out_specs=[HBM_SPEC] * n,
        out_shape=[jax.ShapeDtypeStruct(g.shape[:2] + (g.shape[2] // 2,), g.dtype) for g in gs],
        scratch_shapes=[pltpu.SemaphoreType.DMA((n,)), pltpu.SemaphoreType.DMA((n,))],
        compiler_params=pltpu.CompilerParams(has_side_effects=True),
    )(*gs)


def _rs_add_halves(g, recv, c, name):
    _, rows, w = g.shape
    h = w // 2
    tr = rows // 2 if rows % 16 == 0 and rows > 64 else rows

    def body(c_ref, a_ref, b_ref, o_ref):
        o_ref[...] = (a_ref[...] + b_ref[...]).astype(BF16)

    return pl.pallas_call(
        body, name=name,
        grid_spec=pltpu.PrefetchScalarGridSpec(
            num_scalar_prefetch=1, grid=(N_CHIPS, rows // tr),
            in_specs=[pl.BlockSpec((1, tr, h), lambda j, i, s: (j, i, s[0])),
                      pl.BlockSpec((1, tr, h), lambda j, i, s: (j, i, 0))],
            out_specs=pl.BlockSpec((1, tr, h), lambda j, i, s: (j, i, 0))),
        out_shape=jax.ShapeDtypeStruct((N_CHIPS, rows, h), BF16),
        compiler_params=_params(2),
    )(jnp.reshape(c, (1,)).astype(jnp.int32), g, recv)


def _rs_chip_exchange(ps):
    n = len(ps)

    def body(*refs):
        ins, outs = refs[:n], refs[n:2 * n]
        send_sems, recv_sems, local_sems = refs[2 * n:]
        x, y, c = _mesh_pos()
        mine = 2 * x + y
        chips = _other_chips(x, y)
        local = [pltpu.make_async_copy(ins[a].at[mine], outs[a].at[mine], local_sems.at[a]) for a in range(n)]
        for cp in local:
            cp.start()

        def ici(a, k, src_block, dst_block):
            px, py = chips[k]
            return _remote(ins[a].at[src_block], outs[a].at[dst_block], send_sems.at[3 * a + k],
                           recv_sems.at[3 * a + k], (px, py, c))

        sends = [ici(a, k, 2 * px + py, mine) for a in range(n) for k, (px, py) in enumerate(chips)]
        for cp in sends:
            cp.start()
        for a in range(n):
            for k, (px, py) in enumerate(chips):
                ici(a, k, mine, 2 * px + py).wait_recv()
        for cp in sends:
            cp.wait_send()
        for cp in local:
            cp.wait()

    return pl.pallas_call(
        body, name="rs_chip_exchange", in_specs=[HBM_SPEC] * n, out_specs=[HBM_SPEC] * n,
        out_shape=[jax.ShapeDtypeStruct(p.shape, p.dtype) for p in ps],
        scratch_shapes=[pltpu.SemaphoreType.DMA((3 * n,)), pltpu.SemaphoreType.DMA((3 * n,)),
                        pltpu.SemaphoreType.DMA((n,))],
        compiler_params=pltpu.CompilerParams(has_side_effects=True),
    )(*ps)


def _rs_sum_chips(parts, name):
    _, rows, h = parts.shape
    tr = rows // 2 if rows % 16 == 0 and rows > 64 else rows

    def body(p_ref, o_ref):
        p = p_ref[...].astype(F32)
        o_ref[...] = ((p[0] + p[1]) + p[2]) + p[3]

    return pl.pallas_call(
        body, name=name, grid=(rows // tr,),
        in_specs=[pl.BlockSpec((N_CHIPS, tr, h), lambda i: (0, i, 0))],
        out_specs=pl.BlockSpec((tr, h), lambda i: (i, 0)),
        out_shape=jax.ShapeDtypeStruct((rows, h), F32),
        compiler_params=_params(1),
    )(parts)


def _rs_share(halves):
    n = len(halves)

    def body(*refs):
        ins, outs = refs[:n], refs[n:2 * n]
        send_sems, recv_sems, local_sems = refs[2 * n:]
        x, y, c = _mesh_pos()
        local = [pltpu.make_async_copy(ins[a], _half(outs[a], c, 1), local_sems.at[a]) for a in range(n)]
        for cp in local:
            cp.start()
        sends = [_remote(ins[a], _half(outs[a], c, 1), send_sems.at[a], recv_sems.at[a], (x, y, 1 - c))
                 for a in range(n)]
        for cp in sends:
            cp.start()
        for a in range(n):
            _remote(ins[a], _half(outs[a], 1 - c, 1), send_sems.at[a], recv_sems.at[a], (x, y, 1 - c)).wait_recv()
        for cp in sends:
            cp.wait_send()
        for cp in local:
            cp.wait()

    return pl.pallas_call(
        body, name="rs_share", in_specs=[HBM_SPEC] * n, out_specs=[HBM_SPEC] * n,
        out_shape=[jax.ShapeDtypeStruct((p.shape[0], 2 * p.shape[1]), p.dtype) for p in halves],
        scratch_shapes=[pltpu.SemaphoreType.DMA((n,)), pltpu.SemaphoreType.DMA((n,)),
                        pltpu.SemaphoreType.DMA((n,))],
        compiler_params=pltpu.CompilerParams(has_side_effects=True),
    )(*halves)


def _adamw(g, w, m, v, name):
    rows, cols = g.shape
    tr = 256 if rows % 256 == 0 else (rows // 2 if rows % 16 == 0 and rows > 64 else rows)

    def body(g_ref, w_ref, m_ref, v_ref, d_ref, m2_ref, v2_ref):
        d_ref[...], m2_ref[...], v2_ref[...] = _adam_update(g_ref[...], w_ref[...], m_ref[...], v_ref[...])

    spec = pl.BlockSpec((tr, cols), lambda i: (i, 0))
    return pl.pallas_call(
        body, name=name, grid=(rows // tr,), in_specs=[spec] * 4, out_specs=[spec] * 3,
        out_shape=[jax.ShapeDtypeStruct((rows, cols), F32)] * 3,
        compiler_params=_params(1),
    )(g, w, m, v)


def _rows_of(a):
    flat = a.reshape(-1)
    pad = (-flat.shape[0]) % D
    if pad:
        flat = jnp.concatenate([flat, jnp.zeros((pad,), flat.dtype)])
    return flat.reshape(-1, D)


SLAB_PARTS = (("w_in", (D, D_IN // N_CHIPS)), ("w_out", (D // N_CHIPS, D)), ("w_ffn_gate", (D, D_FF // N_CHIPS)),
              ("w_ffn_up", (D, D_FF // N_CHIPS)), ("w_ffn_down", (D_FF // N_CHIPS, D)),
              ("meta_tokens", (N_META, D // N_CHIPS)), ("conv_w", (CONV_W, C_CONV // N_CHIPS)),
              ("gla_w_gate2", (RANK, GLA_K // N_CHIPS)))


def _pack_slab(parts):
    rows = [_rows_of(parts[name].reshape(shape)) for name, shape in SLAB_PARTS]
    used = sum(r.shape[0] for r in rows)
    rows.append(jnp.zeros((SLAB_ROWS - used, D), F32))
    return jnp.concatenate(rows, axis=0)


def _unpack_slab(slab, lead):
    out, r0 = {}, 0
    for name, shape in SLAB_PARTS:
        size = shape[0] * shape[1]
        nrows = -(-size // D)
        out[name] = slab[r0:r0 + nrows].reshape(-1)[:size].reshape(lead[name] + shape)
        r0 += nrows
    return out


SMALL_PARTS = (("norm_mix_g", 0, 0, D), ("norm_ffn_g", 1, 0, D), ("norm_final_g", 2, 0, D),
               ("conv_b", 3, 0, C_CONV), ("conv_ln_g", 3, C_CONV, C_CONV), ("conv_ln_b", 4, 0, C_CONV),
               ("gla_gate_b", 4, C_CONV, GLA_K), ("gla_norm_g", 4, C_CONV + GLA_K, DV))


def _pack_small(parts):
    slab = jnp.zeros((SMALL_ROWS, D), F32)
    for name, row, col, size in SMALL_PARTS:
        slab = lax.dynamic_update_slice(slab, parts[name].reshape(1, size).astype(F32), (row, col))
    return slab


def _unpack_small(slab, shapes):
    return {name: slab[row, col:col + size].reshape(shapes[name]) for name, row, col, size in SMALL_PARTS}


def _column_block(full, j, width):
    return lax.dynamic_slice_in_dim(full, j * width, width, axis=1)


def _local_step(x, target, w):
    n_ex, seq, _ = x.shape
    lp = HEAD_ROWS + seq
    t = n_ex * lp
    meta = jnp.broadcast_to(w["meta_tokens"][None], (n_ex, N_META, D))
    h0 = jnp.concatenate([jnp.zeros((n_ex, PAD_ROWS, D), F32), meta, x], axis=1).reshape(t, D)
    tgt = jnp.concatenate([jnp.zeros((n_ex, HEAD_ROWS, D), F32), target], axis=1).reshape(t, D)
    row_mask = jnp.concatenate([jnp.zeros((n_ex, HEAD_ROWS, 1), F32), jnp.ones((n_ex, seq, 1), F32)],
                               axis=1).reshape(t, 1)

    u, hn = _in_proj(h0, w["norm_mix_g"], w["w_in"])
    yc, y_conv = _conv_fwd(u, w["conv_w"], w["conv_b"], w["conv_ln_g"], w["conv_ln_b"], n_ex, lp)
    y_gla, states = _gla_fwd(u, w["gla_w_gate2"], w["gla_gate_b"], w["gla_norm_g"], n_ex, lp)
    h1, hn2, gate, up, act = _mix_out_ffn_up(h0, y_conv, y_gla, w["w_out"], w["norm_ffn_g"],
                                             w["w_ffn_gate_t"], w["w_ffn_up_t"])
    dh2, loss, d_final_g = _ffn_down_loss(act, w["w_ffn_down"], h1, tgt, w["norm_final_g"], row_mask)

    dgate, dup, dh1, dycat, d_ffn_g = _ffn_bwd(dh2, gate, up, h1, w["w_ffn_down"], w["w_ffn_gate_t"],
                                                w["w_ffn_up_t"], w["w_out"], w["norm_ffn_g"])
    du_conv, d_conv_w, d_conv_b, d_ln_g, d_ln_b = _conv_bwd(dycat, yc, u, w["conv_w"], w["conv_ln_g"],
                                                            w["conv_ln_b"], n_ex, lp)
    du_gla, d_w2, d_gate_b, d_norm_g = _gla_bwd(dycat, u, states, w["gla_w_gate2"], w["gla_gate_b"],
                                                w["gla_norm_g"], n_ex, lp)
    dh0, d_mix_g = _in_proj_bwd(du_conv, du_gla, w["w_in"][:, :2 * C_CONV], w["w_in"][:, 2 * C_CONV:],
                                h0, dh1, w["norm_mix_g"])

    d_w_in_t = jnp.concatenate([_wgrad(du_conv, hn, "wgrad_in_conv"), _wgrad(du_gla, hn, "wgrad_in_gla")],
                               axis=0)[:D_IN]
    d_w_out = jnp.concatenate([_wgrad(y_conv, dh1, "wgrad_out_conv"), _wgrad(y_gla, dh1, "wgrad_out_gla")], axis=0)
    dh0 = dh0.reshape(n_ex, lp, D)
    grads = {
        "w_in_t": d_w_in_t, "w_out": d_w_out,
        "w_ffn_gate_t": _wgrad(dgate, hn2, "wgrad_gate"), "w_ffn_up_t": _wgrad(dup, hn2, "wgrad_up"),
        "w_ffn_down": _wgrad(act, dh2, "wgrad_down"),
        "meta_tokens": jnp.sum(dh0[:, PAD_ROWS:HEAD_ROWS], axis=0),
        "conv_w": d_conv_w, "gla_w_gate2": d_w2[:RANK],
        "norm_mix_g": d_mix_g, "norm_ffn_g": d_ffn_g, "norm_final_g": d_final_g,
        "conv_b": d_conv_b, "conv_ln_g": d_ln_g, "conv_ln_b": d_ln_b,
        "gla_gate_b": d_gate_b, "gla_norm_g": d_norm_g,
    }
    return loss[0, 0], dh0[:, HEAD_ROWS:], grads


WEIGHT_NAMES = ("meta_tokens", "norm_mix_g", "w_in", "conv_w", "conv_b", "conv_ln_g", "conv_ln_b", "gla_w_gate2",
                "gla_gate_b", "gla_norm_g", "w_out", "norm_ffn_g", "w_ffn_gate", "w_ffn_up", "w_ffn_down",
                "norm_final_g")
MATMUL_WEIGHTS = ("w_in", "w_out", "w_ffn_gate", "w_ffn_up", "w_ffn_down")
ROW_SHARDED = ("w_out", "w_ffn_down")


def _full_weights(ws):
    sh = lambda name: ws[name].reshape(ws[name].shape[-2:])
    split = [sh("w_in").astype(BF16), sh("w_out").astype(BF16), sh("w_ffn_gate").T.astype(BF16),
             sh("w_ffn_up").T.astype(BF16), sh("w_ffn_down").astype(BF16)]
    whole = [sh("meta_tokens"), sh("conv_w"), sh("gla_w_gate2")]
    w_in, w_out, gate_t, up_t, down, meta, conv_w, w2 = _gather_weights(split, [0, 0, 0, 0, 0], whole)
    cols = lambda a: jnp.concatenate([a[j] for j in range(N_CHIPS)], axis=1)
    full = {name: ws[name].reshape(1, -1) for name, _, _, _ in SMALL_PARTS}
    full["w_in"] = jnp.concatenate([cols(w_in), jnp.zeros((D, D_IN_PAD - D_IN), BF16)], axis=1)
    full["w_out"] = w_out.reshape(D, D)
    full["w_ffn_gate_t"] = gate_t.reshape(D_FF, D)
    full["w_ffn_up_t"] = up_t.reshape(D_FF, D)
    full["w_ffn_down"] = down.reshape(D_FF, D)
    full["meta_tokens"] = cols(meta)
    full["conv_w"] = jnp.concatenate([cols(conv_w), jnp.zeros((32 - CONV_W, C_CONV), F32)], axis=0)
    full["gla_w_gate2"] = jnp.concatenate([cols(w2), jnp.zeros((128 - RANK, GLA_K), F32)], axis=0).astype(BF16)
    return full


SMALL_RS_ROWS = 48


def _pack_small_sharded(grads):
    by_chip = lambda g, w: jnp.transpose(g.reshape(g.shape[0], N_CHIPS, w), (1, 0, 2))
    meta = by_chip(grads["meta_tokens"], D // N_CHIPS)
    conv = by_chip(grads["conv_w"], C_CONV // N_CHIPS).reshape(N_CHIPS, 16, 256)
    w2 = by_chip(grads["gla_w_gate2"], GLA_K // N_CHIPS).reshape(N_CHIPS, 4, 256)
    pad = jnp.zeros((N_CHIPS, SMALL_RS_ROWS - 36, 256), F32)
    return jnp.concatenate([meta, conv, w2, pad], axis=1)


def _unpack_small_sharded(g):
    return {"meta_tokens": g[0:16], "conv_w": g[16:32].reshape(32, C_CONV // N_CHIPS)[:CONV_W],
            "gla_w_gate2": g[32:36].reshape(RANK, GLA_K // N_CHIPS)}


def _kernel_without_overlap(x, meta_tokens, norm_mix_g, w_in, conv_w, conv_b, conv_ln_g, conv_ln_b, gla_w_gate2, gla_gate_b, gla_norm_g, w_out, norm_ffn_g, w_ffn_gate, w_ffn_up, w_ffn_down, norm_final_g, loss_target, m_meta_tokens, m_norm_mix_g, m_w_in, m_conv_w, m_conv_b, m_conv_ln_g, m_conv_ln_b, m_gla_w_gate2, m_gla_gate_b, m_gla_norm_g, m_w_out, m_norm_ffn_g, m_w_ffn_gate, m_w_ffn_up, m_w_ffn_down, m_norm_final_g, v_meta_tokens, v_norm_mix_g, v_w_in, v_conv_w, v_conv_b, v_conv_ln_g, v_conv_ln_b, v_gla_w_gate2, v_gla_gate_b, v_gla_norm_g, v_w_out, v_norm_ffn_g, v_w_ffn_gate, v_w_ffn_up, v_w_ffn_down, v_norm_final_g):
    ws = dict(zip(WEIGHT_NAMES, (meta_tokens, norm_mix_g, w_in, conv_w, conv_b, conv_ln_g, conv_ln_b, gla_w_gate2,
                                 gla_gate_b, gla_norm_g, w_out, norm_ffn_g, w_ffn_gate, w_ffn_up, w_ffn_down,
                                 norm_final_g)))
    ms = dict(zip(WEIGHT_NAMES, (m_meta_tokens, m_norm_mix_g, m_w_in, m_conv_w, m_conv_b, m_conv_ln_g, m_conv_ln_b,
                                 m_gla_w_gate2, m_gla_gate_b, m_gla_norm_g, m_w_out, m_norm_ffn_g, m_w_ffn_gate,
                                 m_w_ffn_up, m_w_ffn_down, m_norm_final_g)))
    vs = dict(zip(WEIGHT_NAMES, (v_meta_tokens, v_norm_mix_g, v_w_in, v_conv_w, v_conv_b, v_conv_ln_g, v_conv_ln_b,
                                 v_gla_w_gate2, v_gla_gate_b, v_gla_norm_g, v_w_out, v_norm_ffn_g, v_w_ffn_gate,
                                 v_w_ffn_up, v_w_ffn_down, v_norm_final_g)))
    c = lax.axis_index("c")

    full = _full_weights(ws)
    loss, grad_x, grads = _local_step(x, loss_target, full)
    loss = lax.psum(loss, ("x", "y", "c"))

    rs_names = ("w_in", "w_out", "w_ffn_gate", "w_ffn_up", "w_ffn_down", "small")
    by_owner = [grads["w_in_t"].reshape(N_CHIPS, D_IN // N_CHIPS, D), grads["w_out"].reshape(N_CHIPS, D // N_CHIPS, D),
                grads["w_ffn_gate_t"].reshape(N_CHIPS, D_FF // N_CHIPS, D),
                grads["w_ffn_up_t"].reshape(N_CHIPS, D_FF // N_CHIPS, D),
                grads["w_ffn_down"].reshape(N_CHIPS, D_FF // N_CHIPS, D), _pack_small_sharded(grads)]
    from_sibling = _rs_to_sibling(by_owner)
    chip_sums = [_rs_add_halves(g, r, c, "rs_add_" + nm) for g, r, nm in zip(by_owner, from_sibling, rs_names)]
    halves = [_rs_sum_chips(p, "rs_sum_" + nm) for p, nm in zip(_rs_chip_exchange(chip_sums), rs_names)]
    reduced = dict(zip(rs_names, _rs_share(halves)))
    g_sharded = {"w_in": reduced["w_in"].T, "w_out": reduced["w_out"], "w_ffn_gate": reduced["w_ffn_gate"].T,
                 "w_ffn_up": reduced["w_ffn_up"].T, "w_ffn_down": reduced["w_ffn_down"],
                 **_unpack_small_sharded(reduced["small"])}
    out = {"grad": {}, "delta": {}, "new_m": {}, "new_v": {}}
    for name, g in g_sharded.items():
        shape = ws[name].shape
        flat = lambda a: a.reshape(shape[-2:])
        delta, new_m, new_v = _adamw(g, flat(ws[name]), flat(ms[name]), flat(vs[name]), "adamw_" + name)
        for kind, a in (("grad", g), ("delta", delta), ("new_m", new_m), ("new_v", new_v)):
            out[kind][name] = a.reshape(shape)

    small_shapes = {name: ws[name].shape for name, _, _, _ in SMALL_PARTS}
    g_s, d_s, m_s, v_s = _allreduce_small_adamw(_pack_small(grads), _pack_small(ws), _pack_small(ms), _pack_small(vs))
    for kind, slab in (("grad", g_s), ("delta", d_s), ("new_m", m_s), ("new_v", v_s)):
        out[kind].update(_unpack_small(slab, small_shapes))

    return (loss, grad_x, *[out[kind][name] for kind in ("grad", "delta", "new_m", "new_v") for name in WEIGHT_NAMES])


def _gather_plan(split, whole=(), axes=None):
    split, whole = list(split), list(whole)
    ns, n = len(split), len(split) + len(whole)

    def make(ins, outs, sems):
        ici_send, ici_recv, d2d_send, d2d_recv, own_send, own_recv = sems
        x, y, c = _mesh_pos()
        mine = 2 * x + y
        chips = _other_chips(x, y)
        blocks = [2 * px + py for px, py in chips]

        def own(a):
            return _remote(ins[a], outs[a].at[mine], own_send.at[a], own_recv.at[a], (x, y, 1 - c))

        def ici(a, k, block):
            px, py = chips[k]
            src, dst = ins[a], outs[a].at[block]
            if a < ns:
                src, dst = _half(src, c, axes[a]), _half(dst, c, axes[a])
            return _remote(src, dst, ici_send.at[3 * a + k], ici_recv.at[3 * a + k], (px, py, c))

        def d2d(a, k, half):
            part = _half(outs[a].at[blocks[k]], half, axes[a])
            return _remote(part, part, d2d_send.at[3 * a + k], d2d_recv.at[3 * a + k], (x, y, 1 - c))

        def start():
            for a in range(n):
                for k in range(3):
                    ici(a, k, mine).start()
                own(a).start()

        def finish():
            for a in range(n):
                for k in range(3):
                    ici(a, k, blocks[k]).wait_recv()
                    if a < ns:
                        d2d(a, k, c).start()
            for a in range(ns):
                for k in range(3):
                    d2d(a, k, 1 - c).wait_recv()
            for a in range(n):
                for k in range(3):
                    ici(a, k, mine).wait_send()
                    if a < ns:
                        d2d(a, k, c).wait_send()
                own(a).wait()

        return start, finish

    arrays = split + whole
    axes = [0] * ns if axes is None else list(axes)
    return _Plan(arrays, [jax.ShapeDtypeStruct((N_CHIPS,) + s.shape, s.dtype) for s in arrays],
                 [pltpu.SemaphoreType.DMA((3 * n,)), pltpu.SemaphoreType.DMA((3 * n,)),
                  pltpu.SemaphoreType.DMA((3 * ns,)), pltpu.SemaphoreType.DMA((3 * ns,)),
                  pltpu.SemaphoreType.DMA((n,)), pltpu.SemaphoreType.DMA((n,))], make)


def _to_sibling_plan(gs):
    n = len(gs)

    def make(ins, outs, sems):
        send_sems, recv_sems = sems
        x, y, c = _mesh_pos()

        def copy(a):
            return _remote(_half(ins[a], 1 - c, 2), outs[a], send_sems.at[a], recv_sems.at[a], (x, y, 1 - c))

        def start():
            for a in range(n):
                copy(a).start()

        def finish():
            for a in range(n):
                copy(a).wait()

        return start, finish

    return _Plan(list(gs), [jax.ShapeDtypeStruct(g.shape[:2] + (g.shape[2] // 2,), g.dtype) for g in gs],
                 [pltpu.SemaphoreType.DMA((n,)), pltpu.SemaphoreType.DMA((n,))], make)


def _chip_exchange_plan(ps):
    n = len(ps)

    def make(ins, outs, sems):
        send_sems, recv_sems = sems
        x, y, c = _mesh_pos()
        chips = _other_chips(x, y)

        def ici(a, k):
            px, py = chips[k]
            return _remote(ins[a].at[2 * px + py], outs[a].at[k], send_sems.at[3 * a + k],
                           recv_sems.at[3 * a + k], (px, py, c))

        def start():
            for a in range(n):
                for k in range(3):
                    ici(a, k).start()

        def finish():
            for a in range(n):
                for k in range(3):
                    ici(a, k).wait()

        return start, finish

    return _Plan(list(ps), [jax.ShapeDtypeStruct((3,) + p.shape[1:], p.dtype) for p in ps],
                 [pltpu.SemaphoreType.DMA((3 * n,)), pltpu.SemaphoreType.DMA((3 * n,))], make)


def _share_plan(halves):
    n = len(halves)

    def make(ins, outs, sems):
        send_sems, recv_sems = sems
        x, y, c = _mesh_pos()

        def d2d(a):
            return _remote(ins[a], outs[a], send_sems.at[a], recv_sems.at[a], (x, y, 1 - c))

        def start():
            for a in range(n):
                d2d(a).start()

        def finish():
            for a in range(n):
                d2d(a).wait()

        return start, finish

    return _Plan(list(halves), [jax.ShapeDtypeStruct(p.shape, p.dtype) for p in halves],
                 [pltpu.SemaphoreType.DMA((n,)), pltpu.SemaphoreType.DMA((n,))], make)


def _rs_sum(own, others, mine, name):
    _, rows, h = own.shape
    tr = rows // 2 if rows % 16 == 0 and rows > 64 else rows

    def body(mine_ref, own_ref, oth_ref, o_ref):
        p = oth_ref[...].astype(F32)
        o_ref[...] = ((own_ref[0].astype(F32) + p[0]) + p[1]) + p[2]

    return pl.pallas_call(
        body, name=name,
        grid_spec=pltpu.PrefetchScalarGridSpec(
            num_scalar_prefetch=1, grid=(rows // tr,),
            in_specs=[pl.BlockSpec((1, tr, h), lambda i, s: (s[0], i, 0)),
                      pl.BlockSpec((3, tr, h), lambda i, s: (0, i, 0))],
            out_specs=pl.BlockSpec((tr, h), lambda i, s: (i, 0))),
        out_shape=jax.ShapeDtypeStruct((rows, h), F32),
        compiler_params=_params(1),
    )(jnp.reshape(mine, (1,)).astype(jnp.int32), own, others)


def _join(mine, theirs, c):
    return jnp.where(c == 0, jnp.concatenate([mine, theirs], axis=1), jnp.concatenate([theirs, mine], axis=1))


LOSS_ROW = 5


def _merge_plans(a, b):
    na_in, na_out, na_sems = len(a.arrays), len(a.out_shape), len(a.sems)

    def make(ins, outs, sems):
        start_a, finish_a = a.make(ins[:na_in], outs[:na_out], sems[:na_sems])
        start_b, finish_b = b.make(ins[na_in:], outs[na_out:], sems[na_sems:])

        def start():
            start_a()
            start_b()

        def finish():
            finish_a()
            finish_b()

        return start, finish

    return _Plan(list(a.arrays) + list(b.arrays), list(a.out_shape) + list(b.out_shape),
                 list(a.sems) + list(b.sems), make)


def _exchange(plan, name):
    n_in, n_out = len(plan.arrays), len(plan.out_shape)

    def body(*refs):
        start, finish = plan.make(refs[:n_in], refs[n_in:n_in + n_out], refs[n_in + n_out:])
        start()
        finish()

    return pl.pallas_call(
        body, name=name, in_specs=[HBM_SPEC] * n_in, out_specs=[HBM_SPEC] * n_out, out_shape=list(plan.out_shape),
        scratch_shapes=list(plan.sems), compiler_params=pltpu.CompilerParams(has_side_effects=True),
    )(*plan.arrays)


def _adamw_halves(mine, theirs, c, w, m, v, name):
    rows, h = mine.shape
    tr = rows // 2 if rows % 16 == 0 else rows

    def body(c_ref, a_ref, b_ref, w_ref, m_ref, v_ref, go_ref, d_ref, m2_ref, v2_ref):
        g = jnp.where(pl.program_id(1) == c_ref[0], a_ref[...], b_ref[...])
        go_ref[...] = g
        d_ref[...], m2_ref[...], v2_ref[...] = _adam_update(g, w_ref[...], m_ref[...], v_ref[...])

    half = pl.BlockSpec((tr, h), lambda i, j, s: (i, 0))
    spec = pl.BlockSpec((tr, h), lambda i, j, s: (i, j))
    return pl.pallas_call(
        body, name=name,
        grid_spec=pltpu.PrefetchScalarGridSpec(num_scalar_prefetch=1, grid=(rows // tr, 2),
                                               in_specs=[half, half, spec, spec, spec], out_specs=[spec] * 4),
        out_shape=[jax.ShapeDtypeStruct((rows, 2 * h), F32)] * 4,
        compiler_params=_params(2),
    )(jnp.reshape(c, (1,)).astype(jnp.int32), mine, theirs, w, m, v)


def _columns(gathered):
    return jnp.concatenate([gathered[j] for j in range(N_CHIPS)], axis=1)


def kernel(x, meta_tokens, norm_mix_g, w_in, conv_w, conv_b, conv_ln_g, conv_ln_b, gla_w_gate2, gla_gate_b, gla_norm_g, w_out, norm_ffn_g, w_ffn_gate, w_ffn_up, w_ffn_down, norm_final_g, loss_target, m_meta_tokens, m_norm_mix_g, m_w_in, m_conv_w, m_conv_b, m_conv_ln_g, m_conv_ln_b, m_gla_w_gate2, m_gla_gate_b, m_gla_norm_g, m_w_out, m_norm_ffn_g, m_w_ffn_gate, m_w_ffn_up, m_w_ffn_down, m_norm_final_g, v_meta_tokens, v_norm_mix_g, v_w_in, v_conv_w, v_conv_b, v_conv_ln_g, v_conv_ln_b, v_gla_w_gate2, v_gla_gate_b, v_gla_norm_g, v_w_out, v_norm_ffn_g, v_w_ffn_gate, v_w_ffn_up, v_w_ffn_down, v_norm_final_g):
    ws = dict(zip(WEIGHT_NAMES, (meta_tokens, norm_mix_g, w_in, conv_w, conv_b, conv_ln_g, conv_ln_b, gla_w_gate2,
                                 gla_gate_b, gla_norm_g, w_out, norm_ffn_g, w_ffn_gate, w_ffn_up, w_ffn_down,
                                 norm_final_g)))
    ms = dict(zip(WEIGHT_NAMES, (m_meta_tokens, m_norm_mix_g, m_w_in, m_conv_w, m_conv_b, m_conv_ln_g, m_conv_ln_b,
                                 m_gla_w_gate2, m_gla_gate_b, m_gla_norm_g, m_w_out, m_norm_ffn_g, m_w_ffn_gate,
                                 m_w_ffn_up, m_w_ffn_down, m_norm_final_g)))
    vs = dict(zip(WEIGHT_NAMES, (v_meta_tokens, v_norm_mix_g, v_w_in, v_conv_w, v_conv_b, v_conv_ln_g, v_conv_ln_b,
                                 v_gla_w_gate2, v_gla_gate_b, v_gla_norm_g, v_w_out, v_norm_ffn_g, v_w_ffn_gate,
                                 v_w_ffn_up, v_w_ffn_down, v_norm_final_g)))
    c = lax.axis_index("c")
    shard = lambda d, name: d[name].reshape(d[name].shape[-2:])
    vec = {name: ws[name].reshape(1, -1) for name, _, _, _ in SMALL_PARTS}
    n_ex, seq, _ = x.shape
    lp = HEAD_ROWS + seq
    t = n_ex * lp

    (tgt,), (w_in_g, meta_g, conv_w_g, w2_g) = _pad_head_rows(loss_target, plan=_gather_plan(
        [shard(ws, "w_in").T.astype(BF16)],
        [shard(ws, "meta_tokens"), shard(ws, "conv_w"), shard(ws, "gla_w_gate2")], axes=[1]))
    w_in_t = jnp.concatenate([w_in_g.reshape(D_IN, D), jnp.zeros((D_IN_PAD - D_IN, D), BF16)], axis=0)
    w_in_full = w_in_t.T
    conv_w_full = jnp.concatenate([_columns(conv_w_g), jnp.zeros((32 - CONV_W, C_CONV), F32)], axis=0)
    w2_full = jnp.concatenate([_columns(w2_g), jnp.zeros((128 - RANK, GLA_K), F32)], axis=0).astype(BF16)

    meta = jnp.broadcast_to(_columns(meta_g)[None], (n_ex, N_META, D))
    h0 = jnp.concatenate([jnp.zeros((n_ex, PAD_ROWS, D), F32), meta, x], axis=1).reshape(t, D)
    tgt = tgt.reshape(t, D)
    row_mask = jnp.concatenate([jnp.zeros((n_ex, HEAD_ROWS, 1), F32), jnp.ones((n_ex, seq, 1), F32)],
                               axis=1).reshape(t, 1)

    (u, hn), (w_out_g,) = _in_proj(h0, vec["norm_mix_g"], w_in_full,
                                   plan=_gather_plan([shard(ws, "w_out").astype(BF16)]))
    (yc, y_conv), (gate_g,) = _conv_fwd(
        u, conv_w_full, vec["conv_b"], vec["conv_ln_g"], vec["conv_ln_b"], n_ex, lp,
        plan=_gather_plan([shard(ws, "w_ffn_gate").T.astype(BF16)]))
    (y_gla, states), (up_g,) = _gla_fwd(u, w2_full, vec["gla_gate_b"], vec["gla_norm_g"], n_ex, lp,
                                        plan=_gather_plan([shard(ws, "w_ffn_up").T.astype(BF16)]))
    w_out_full = w_out_g.reshape(D, D)
    w_gate_t, w_up_t = gate_g.reshape(D_FF, D), up_g.reshape(D_FF, D)

    (h1, hn2, gate, up, act), (down_g,) = _mix_out_ffn_up(
        h0, y_conv, y_gla, w_out_full, vec["norm_ffn_g"], w_gate_t.T, w_up_t.T,
        plan=_gather_plan([shard(ws, "w_ffn_down").astype(BF16)]))
    w_down_full = down_g.reshape(D_FF, D)
    dh2, loss, d_final_g = _ffn_down_loss(act, w_down_full, h1, tgt, vec["norm_final_g"], row_mask)
    dgate, dup, dh1, dycat, d_ffn_g = _ffn_bwd(dh2, gate, up, h1, w_down_full.T, w_gate_t, w_up_t, w_out_full.T,
                                                vec["norm_ffn_g"])

    early = ("w_out", "w_ffn_gate", "w_ffn_up", "w_ffn_down")
    d_w_out = jnp.concatenate([_wgrad(y_conv, dh1, "wgrad_out_conv"), _wgrad(y_gla, dh1, "wgrad_out_gla")], axis=0)
    by_owner = [d_w_out.reshape(N_CHIPS, D // N_CHIPS, D),
                _wgrad(dgate, hn2, "wgrad_gate").reshape(N_CHIPS, D_FF // N_CHIPS, D),
                _wgrad(dup, hn2, "wgrad_up").reshape(N_CHIPS, D_FF // N_CHIPS, D),
                _wgrad(act, dh2, "wgrad_down").reshape(N_CHIPS, D_FF // N_CHIPS, D)]
    (du_conv, d_conv_w, d_conv_b, d_ln_g, d_ln_b), from_sibling = _conv_bwd(
        dycat, yc, u, conv_w_full, vec["conv_ln_g"], vec["conv_ln_b"], n_ex, lp, plan=_to_sibling_plan(by_owner))
    chip_sums = [_rs_add_halves(g, r, c, "rs_add_" + nm) for g, r, nm in zip(by_owner, from_sibling, early)]
    (du_gla, d_w2, d_gate_b, d_norm_g), exchanged = _gla_bwd(
        dycat, u, states, w2_full, vec["gla_gate_b"], vec["gla_norm_g"], n_ex, lp,
        plan=_chip_exchange_plan(chip_sums))
    mine = 2 * lax.axis_index("x") + lax.axis_index("y")
    halves = [_rs_sum(own, oth, mine, "rs_sum_" + nm) for own, oth, nm in zip(chip_sums, exchanged, early)]

    d_w_in_t = jnp.concatenate([_wgrad(du_conv, hn, "wgrad_in_conv"), _wgrad(du_gla, hn, "wgrad_in_gla")],
                               axis=0)[:D_IN].reshape(N_CHIPS, D_IN // N_CHIPS, D)
    (in_from_sibling,) = _exchange(_to_sibling_plan([d_w_in_t]), "rs_late_to_sibling")
    in_chip_sum = _rs_add_halves(d_w_in_t, in_from_sibling, c, "rs_add_w_in")
    (dh0, d_mix_g), shared = _in_proj_bwd(
        du_conv, du_gla, w_in_t[:2 * C_CONV], w_in_t[2 * C_CONV:], h0, dh1, vec["norm_mix_g"],
        plan=_merge_plans(_share_plan(halves), _chip_exchange_plan([in_chip_sum])))
    dh0 = dh0.reshape(n_ex, lp, D)
    grad_x = dh0[:, HEAD_ROWS:]

    out = {"grad": {}, "delta": {}, "new_m": {}, "new_v": {}}

    def update(name, g=None, halves=None, transposed=False):
        shape = ws[name].shape
        lay = (lambda a: a.T) if transposed else (lambda a: a)
        w2d, m2d, v2d = lay(shard(ws, name)), lay(shard(ms, name)), lay(shard(vs, name))
        if halves is not None:
            res = _adamw_halves(*halves, c, w2d, m2d, v2d, "adamw_" + name)
        else:
            res = [g, *_adamw(g, w2d, m2d, v2d, "adamw_" + name)]
        for kind, a in zip(("grad", "delta", "new_m", "new_v"), res):
            out[kind][name] = lay(a).reshape(shape)

    update("w_out", halves=(halves[0], shared[0]))
    update("w_ffn_gate", halves=(halves[1], shared[1]), transposed=True)
    update("w_ffn_up", halves=(halves[2], shared[2]), transposed=True)
    update("w_ffn_down", halves=(halves[3], shared[3]))

    in_half = _rs_sum(in_chip_sum, shared[4], mine, "rs_sum_w_in")
    (in_shared,) = _exchange(_share_plan([in_half]), "rs_late_share")
    update("w_in", halves=(in_half, in_shared), transposed=True)

    small = {"norm_mix_g": d_mix_g, "norm_ffn_g": d_ffn_g, "norm_final_g": d_final_g, "conv_b": d_conv_b,
             "conv_ln_g": d_ln_g, "conv_ln_b": d_ln_b, "gla_gate_b": d_gate_b, "gla_norm_g": d_norm_g}
    small_shapes = {name: ws[name].shape for name, _, _, _ in SMALL_PARTS}
    part = lax.dynamic_update_slice(_pack_small(small), loss[:, :1], (LOSS_ROW, 0))
    part = jnp.concatenate([part, jnp.sum(dh0[:, PAD_ROWS:HEAD_ROWS], axis=0), d_conv_w.reshape(16, D),
                            d_w2[:RANK].reshape(4, D), jnp.zeros((4, D), F32)], axis=0)
    tall = lambda a: jnp.concatenate([a, jnp.zeros((part.shape[0] - SMALL_ROWS, D), F32)], axis=0)
    g_s, d_s, m_s, v_s = _allreduce_small_adamw(part, tall(_pack_small(ws)), tall(_pack_small(ms)),
                                                tall(_pack_small(vs)))
    for kind, slab in (("grad", g_s), ("delta", d_s), ("new_m", m_s), ("new_v", v_s)):
        out[kind].update(_unpack_small(slab, small_shapes))
    loss = g_s[LOSS_ROW, 0]
    block = lambda a, width: lax.dynamic_slice_in_dim(a, mine * width, width, axis=1)
    update("meta_tokens", g=block(g_s[8:24], D // N_CHIPS))
    update("conv_w", g=block(g_s[24:40].reshape(32, C_CONV), C_CONV // N_CHIPS)[:CONV_W])
    update("gla_w_gate2", g=block(g_s[40:44].reshape(RANK, GLA_K), GLA_K // N_CHIPS))

    return (loss, grad_x, *[out[kind][name] for kind in ("grad", "delta", "new_m", "new_v") for name in WEIGHT_NAMES])
```

```python
import functools
from typing import Any, Callable, NamedTuple, Sequence

import jax
import jax.numpy as jnp
from jax import lax
from jax.experimental import pallas as pl
from jax.experimental.pallas import tpu as pltpu

F32 = jnp.float32
BF16 = jnp.bfloat16
MESH = pl.DeviceIdType.MESH

D = 1024
N_META = 16
C_CONV = 512
CONV_W = 31
GLA_K = 256
GLA_V = 512
N_HEADS = 4
DK = 64
DV = 128
RANK = 16
CHUNK = 64
PAD_ROWS = CHUNK - N_META
HEAD_ROWS = CHUNK
D_IN = 2576
D_IN_PAD = 2688
D_GLA_IN = D_IN_PAD - 2 * C_CONV
D_FF = 2816
RMS_EPS = 1e-6
LN_EPS = 1e-5
GATE_TAU = 16.0
N_CHIPS = 4

ADAM_LR = 0.001
ADAM_B1 = 0.9
ADAM_B2 = 0.999
ADAM_EPS = 1e-08
ADAM_WD = 0.01
ADAM_STEP = 10

V7X_VMEM_BYTES = 64 * 1024 * 1024
VMEM_LIMIT = V7X_VMEM_BYTES - 8 * 1024 * 1024

SLAB_ROWS = 3072
HALF_ROWS = SLAB_ROWS // 2
SMALL_ROWS = 8


def _dot(a, b):
    return jnp.dot(a, b, preferred_element_type=F32)


def _dot_nt(a, b):
    return lax.dot_general(a, b, (((1,), (1,)), ((), ())), preferred_element_type=F32)


def _dot_tn(a, b):
    return lax.dot_general(a, b, (((0,), (0,)), ((), ())), preferred_element_type=F32)


def _sigmoid(x):
    return 1.0 / (1.0 + jnp.exp(-x))


def _const_spec(shape):
    return pl.BlockSpec(shape, lambda *_: (0,) * len(shape), pipeline_mode=pl.Buffered(1))


def _acc_spec(shape):
    return pl.BlockSpec(shape, lambda *_: (0,) * len(shape))


def _params(n_axes):
    return pltpu.CompilerParams(dimension_semantics=("arbitrary",) * n_axes, vmem_limit_bytes=VMEM_LIMIT)


def _row_tile(t, want):
    for r in (want, 384, 192, 128, 64):
        if r <= want and t % r == 0:
            return r
    raise ValueError(f"no row tile for {t}")


ROW_PART = 128


def _row_parts(r):
    if r % ROW_PART:
        return [slice(None)]
    return [pl.ds(i * ROW_PART, ROW_PART) for i in range(r // ROW_PART)]


def _in_lockstep(bodies):
    live = list(bodies)
    while live:
        still = []
        for g in live:
            try:
                next(g)
                still.append(g)
            except StopIteration:
                pass
        live = still


class _Plan(NamedTuple):
    arrays: Sequence[Any]
    out_shape: Sequence[Any]
    sems: Sequence[Any]
    make: Callable


def _call(body, *, name, grid, in_specs, out_specs, out_shape, scratch_shapes=(), plan=None):
    n_in, n_out, n_scr = len(in_specs), len(out_specs), len(scratch_shapes)
    if plan is None:
        plan = _Plan([], [], [], lambda ins, outs, sems: (lambda: None, lambda: None))
    nx_in, nx_out = len(plan.arrays), len(plan.out_shape)

    def hosted(*refs):
        ins, xins = refs[:n_in], refs[n_in:n_in + nx_in]
        o0 = n_in + nx_in
        outs, xouts = refs[o0:o0 + n_out], refs[o0 + n_out:o0 + n_out + nx_out]
        s0 = o0 + n_out + nx_out
        scr, sems = refs[s0:s0 + n_scr], refs[s0 + n_scr:]
        ids = [pl.program_id(a) for a in range(len(grid))]
        first = functools.reduce(jnp.logical_and, [i == 0 for i in ids])
        last = functools.reduce(jnp.logical_and, [i == g - 1 for i, g in zip(ids, grid)])
        start, finish = plan.make(xins, xouts, sems)
        pl.when(first)(start)
        body(*ins, *outs, *scr)
        pl.when(last)(finish)

    call = pl.pallas_call(
        hosted, name=name, grid=grid, in_specs=list(in_specs) + [HBM_SPEC] * nx_in,
        out_specs=list(out_specs) + [HBM_SPEC] * nx_out, out_shape=list(out_shape) + list(plan.out_shape),
        scratch_shapes=list(scratch_shapes) + list(plan.sems),
        compiler_params=pltpu.CompilerParams(dimension_semantics=("arbitrary",) * len(grid),
                                             vmem_limit_bytes=VMEM_LIMIT, has_side_effects=nx_in > 0))

    def run(*args):
        res = call(*args, *plan.arrays)
        return res[:n_out], res[n_out:]

    return run


def _pad_head_rows(a, plan=None):
    n_ex, seq, _ = a.shape
    nc = (HEAD_ROWS + seq) // CHUNK

    def body(a_ref, o_ref):
        o_ref[...] = jnp.where(pl.program_id(0) > 0, a_ref[...], 0.0)

    return _call(
        body, name="pad_head_rows", grid=(nc,),
        in_specs=[pl.BlockSpec((n_ex, CHUNK, D), lambda n: (0, jnp.maximum(n - 1, 0), 0))],
        out_specs=[pl.BlockSpec((n_ex, CHUNK, D), lambda n: (0, n, 0))],
        out_shape=[jax.ShapeDtypeStruct((n_ex, HEAD_ROWS + seq, D), F32)],
        plan=plan,
    )(a)


def _in_proj(h0, g_mix, w_in, plan=None):
    t = h0.shape[0]
    r = _row_tile(t, 384)

    def body(h_ref, g_ref, w_ref, u_ref, hn_ref):
        h = h_ref[...]
        rstd = lax.rsqrt(jnp.mean(h * h, axis=-1, keepdims=True) + RMS_EPS)
        hn = (h * rstd * g_ref[...]).astype(BF16)
        hn_ref[...] = hn
        u_ref[...] = _dot(hn, w_ref[...])

    return _call(
        body, name="in_proj", grid=(t // r,),
        in_specs=[pl.BlockSpec((r, D), lambda i: (i, 0)), _const_spec((1, D)), _const_spec((D, D_IN_PAD))],
        out_specs=[pl.BlockSpec((r, D_IN_PAD), lambda i: (i, 0)), pl.BlockSpec((r, D), lambda i: (i, 0))],
        out_shape=[jax.ShapeDtypeStruct((t, D_IN_PAD), F32), jax.ShapeDtypeStruct((t, D), BF16)],
        plan=plan,
    )(h0, g_mix, w_in)


CONV_TILE = 192
CONV_SUB = 32
CONV_LEAD = CONV_SUB - (CONV_W - 1)
SUBLANES = 8


def _shifted_copies(src, dst, r):
    for s in range(1, SUBLANES):
        dst[s - 1] = src[s:s + r + CONV_SUB - SUBLANES, :]


def _shifted_rows(src, shifted, start):
    base, s = SUBLANES * (start // SUBLANES), start % SUBLANES
    if s == 0:
        return src[base:base + CONV_SUB, :]
    return shifted[s - 1, base:base + CONV_SUB, :]


def _conv_fwd(u, conv_w, conv_b, ln_g, ln_b, n_ex, lp, plan=None):
    r = CONV_TILE
    nt = lp // r
    hb = r // CONV_SUB

    def body(cur_ref, prev_ref, w_ref, b_ref, lg_ref, lb_ref, yc_ref, y_ref, glu, glu_sh):
        i = pl.program_id(1)
        cur = cur_ref[...]
        glu[CONV_SUB:CONV_SUB + r, :] = cur[:, :C_CONV] * _sigmoid(cur[:, C_CONV:])
        pv = prev_ref[...]
        halo = pv[:, :C_CONV] * _sigmoid(pv[:, C_CONV:])
        glu[0:CONV_SUB, :] = jnp.where(i > 0, halo, 0.0)
        _shifted_copies(glu, glu_sh, r)
        w = w_ref[...]
        for j in range(r // CONV_SUB):
            r0 = j * CONV_SUB
            acc = jnp.zeros((CONV_SUB, C_CONV), F32) + b_ref[...]
            for k in range(CONV_W):
                acc = acc + w[k:k + 1, :] * _shifted_rows(glu, glu_sh, r0 + CONV_LEAD + k)
            mu = jnp.mean(acc, axis=-1, keepdims=True)
            cen = acc - mu
            var = jnp.mean(cen * cen, axis=-1, keepdims=True)
            out = cen * lax.rsqrt(var + LN_EPS) * lg_ref[...] + lb_ref[...]
            y = out * _sigmoid(out)
            row = i * r + r0 + lax.broadcasted_iota(jnp.int32, (CONV_SUB, 1), 0)
            y = jnp.where(row >= PAD_ROWS, y, 0.0)
            yc_ref[r0:r0 + CONV_SUB, :] = acc
            y_ref[r0:r0 + CONV_SUB, :] = y.astype(BF16)

    t = n_ex * lp
    return _call(
        body, name="conv_fwd", grid=(n_ex, nt),
        in_specs=[pl.BlockSpec((r, 2 * C_CONV), lambda b, i: (b * nt + i, 0)),
                  pl.BlockSpec((CONV_SUB, 2 * C_CONV), lambda b, i: (jnp.maximum((b * nt + i) * hb - 1, 0), 0)),
                  _const_spec((32, C_CONV)), _const_spec((1, C_CONV)), _const_spec((1, C_CONV)), _const_spec((1, C_CONV))],
        out_specs=[pl.BlockSpec((r, C_CONV), lambda b, i: (b * nt + i, 0)),
                   pl.BlockSpec((r, C_CONV), lambda b, i: (b * nt + i, 0))],
        out_shape=[jax.ShapeDtypeStruct((t, C_CONV), F32), jax.ShapeDtypeStruct((t, C_CONV), BF16)],
        scratch_shapes=[pltpu.VMEM((r + CONV_SUB, C_CONV), F32),
                        pltpu.VMEM((SUBLANES - 1, r + CONV_SUB - SUBLANES, C_CONV), F32)],
        plan=plan,
    )(u, u, conv_w, conv_b, ln_g, ln_b)


def _gla_gates(lr, w2, gb, first_chunk):
    z = _dot(lr.astype(BF16), w2) + gb
    a = (jnp.minimum(z, 0.0) - jnp.log(1.0 + jnp.exp(-jnp.abs(z)))) * (1.0 / GATE_TAU)
    row = lax.broadcasted_iota(jnp.int32, (CHUNK, 1), 0)
    live = jnp.logical_or(jnp.logical_not(first_chunk), row >= PAD_ROWS)
    return z, jnp.where(live, a, 0.0), live


def _tri(lower):
    i = lax.broadcasted_iota(jnp.int32, (CHUNK, CHUNK), 0)
    j = lax.broadcasted_iota(jnp.int32, (CHUNK, CHUNK), 1)
    return (i >= j) if lower else (i <= j)


def _gla_fwd_per_head(u, w2, gb, ng, n_ex, lp, plan=None):
    nc = lp // CHUNK
    t = n_ex * lp

    def body(qk_ref, v_ref, g_ref, lr_ref, w2_ref, gb_ref, ng_ref, y_ref, st_ref, state):
        n = pl.program_id(0)

        @pl.when(n == 0)
        def _():
            state[...] = jnp.zeros_like(state)

        causal = _tri(True)
        for e in range(n_ex):
            st = state[e]
            st_ref[e] = st
            qk = qk_ref[e]
            q, k = qk[:, :GLA_K], qk[:, GLA_K:]
            _, a, _ = _gla_gates(lr_ref[e], w2_ref[...], gb_ref[...], n == 0)
            b = jnp.dot(causal.astype(F32), a, preferred_element_type=F32, precision=lax.Precision.HIGHEST)
            bl = b[CHUNK - 1:CHUNK, :]
            q_in = (q * (DK ** -0.5) * jnp.exp(b)).astype(BF16)
            k_in = (k * jnp.exp(-b)).astype(BF16)
            k_dec = (k * jnp.exp(bl - b)).astype(BF16)
            decay = jnp.exp(bl)
            v = v_ref[e]
            g = g_ref[e]
            st_b = st.astype(BF16)
            ys, new = [], []
            for h in range(N_HEADS):
                ks = slice(h * DK, (h + 1) * DK)
                vs = slice(h * DV, (h + 1) * DV)
                vh = v[:, vs].astype(BF16)
                s = jnp.where(causal, _dot_nt(q_in[:, ks], k_in[:, ks]), 0.0)
                o = _dot(s.astype(BF16), vh) + _dot_nt(q_in[:, ks], st_b[:, ks])
                new.append(decay[:, ks] * st[:, ks] + _dot_tn(vh, k_dec[:, ks]))
                rstd = lax.rsqrt(jnp.mean(o * o, axis=-1, keepdims=True) + RMS_EPS)
                gh = g[:, vs]
                ys.append(o * rstd * ng_ref[...] * (gh * _sigmoid(gh)))
            state[e] = jnp.concatenate(new, axis=1)
            y_ref[e] = jnp.concatenate(ys, axis=1).astype(BF16)

    u3 = u.reshape(n_ex, lp, D_IN_PAD)
    blk = lambda w, col: pl.BlockSpec((n_ex, CHUNK, w), lambda n: (0, n, col))
    (y, states), extra = _call(
        body, name="gla_fwd", grid=(nc,),
        in_specs=[blk(2 * GLA_K, 2), blk(GLA_V, 3), blk(GLA_V, 4), blk(128, 20),
                  _const_spec((128, GLA_K)), _const_spec((1, GLA_K)), _const_spec((1, DV))],
        out_specs=[blk(GLA_V, 0), pl.BlockSpec((n_ex, DV, GLA_K), lambda n: (0, n, 0))],
        out_shape=[jax.ShapeDtypeStruct((n_ex, lp, GLA_V), BF16),
                   jax.ShapeDtypeStruct((n_ex, nc * DV, GLA_K), F32)],
        scratch_shapes=[pltpu.VMEM((n_ex, DV, GLA_K), F32)],
        plan=plan,
    )(u3, u3, u3, u3, w2, gb, ng)
    return (y.reshape(t, GLA_V), states), extra


FFN_TILE = 192


def _mix_out_ffn_up(h0, y_conv, y_gla, w_out, g_ffn, w_gate, w_up, plan=None):
    t = h0.shape[0]
    r = _row_tile(t, 384)

    def body(h0_ref, yc_ref, yg_ref, wo_ref, g_ref, wg_ref, wu_ref, h1_ref, hn_ref, gate_ref, up_ref, act_ref):
        h1 = h0_ref[...] + _dot(yc_ref[...], wo_ref[0:C_CONV, :]) + _dot(yg_ref[...], wo_ref[C_CONV:D, :])
        h1_ref[...] = h1
        rstd = lax.rsqrt(jnp.mean(h1 * h1, axis=-1, keepdims=True) + RMS_EPS)
        hn = (h1 * rstd * g_ref[...]).astype(BF16)
        hn_ref[...] = hn
        gate = _dot(hn, wg_ref[...])
        up = _dot(hn, wu_ref[...])
        gate_ref[...] = gate
        up_ref[...] = up
        act_ref[...] = (gate * _sigmoid(gate) * up).astype(BF16)

    rows = lambda w: pl.BlockSpec((r, w), lambda i: (i, 0))
    return _call(
        body, name="mix_out_ffn_up", grid=(t // r,),
        in_specs=[rows(D), rows(C_CONV), rows(GLA_V), _const_spec((D, D)), _const_spec((1, D)),
                  _const_spec((D, D_FF)), _const_spec((D, D_FF))],
        out_specs=[rows(D), rows(D), rows(D_FF), rows(D_FF), rows(D_FF)],
        out_shape=[jax.ShapeDtypeStruct((t, D), F32), jax.ShapeDtypeStruct((t, D), BF16),
                   jax.ShapeDtypeStruct((t, D_FF), F32), jax.ShapeDtypeStruct((t, D_FF), F32),
                   jax.ShapeDtypeStruct((t, D_FF), BF16)],
        plan=plan,
    )(h0, y_conv, y_gla, w_out, g_ffn, w_gate, w_up)


def _ffn_down_loss(act, w_down, h1, target, g_final, row_mask):
    t = h1.shape[0]
    r = _row_tile(t, 384)

    def body(act_ref, wd_ref, h1_ref, tgt_ref, gf_ref, mask_ref, dh2_ref, loss_ref, dgf_ref):
        @pl.when(pl.program_id(0) == 0)
        def _():
            loss_ref[...] = jnp.zeros_like(loss_ref)
            dgf_ref[...] = jnp.zeros_like(dgf_ref)

        gf = gf_ref[...]

        def part(rows):
            h2 = h1_ref[rows, :] + _dot(act_ref[rows, :], wd_ref[...])
            yield
            rstd = lax.rsqrt(jnp.mean(h2 * h2, axis=-1, keepdims=True) + RMS_EPS)
            nrm = h2 * rstd
            err = (nrm * gf - tgt_ref[rows, :]) * mask_ref[rows, :]
            loss_ref[...] += jnp.sum(err * err) * (0.5 / D)
            dy = err * (1.0 / D)
            dgf_ref[...] += jnp.sum(dy * nrm, axis=0, keepdims=True)
            dn = dy * gf
            dh2_ref[rows, :] = rstd * (dn - nrm * jnp.mean(dn * nrm, axis=-1, keepdims=True))

        _in_lockstep(part(rows) for rows in _row_parts(r))

    rows = lambda w: pl.BlockSpec((r, w), lambda i: (i, 0))
    return pl.pallas_call(
        body, name="ffn_down_loss", grid=(t // r,),
        in_specs=[rows(D_FF), _const_spec((D_FF, D)), rows(D), rows(D), _const_spec((1, D)), rows(1)],
        out_specs=[rows(D), _acc_spec((1, 128)), _acc_spec((1, D))],
        out_shape=[jax.ShapeDtypeStruct((t, D), F32), jax.ShapeDtypeStruct((1, 128), F32),
                   jax.ShapeDtypeStruct((1, D), F32)],
        compiler_params=_params(1),
    )(act, w_down, h1, target, g_final, row_mask)


def _ffn_bwd(dh2, gate, up, h1, w_down_t, w_gate_t, w_up_t, w_out_t, g_ffn):
    t = h1.shape[0]
    r = _row_tile(t, FFN_TILE)

    def body(dh2_ref, gate_ref, up_ref, h1_ref, wd_ref, wg_ref, wu_ref, wo_ref, g_ref,
             dgate_ref, dup_ref, dh1_ref, dycat_ref, dg_ref):
        @pl.when(pl.program_id(0) == 0)
        def _():
            dg_ref[...] = jnp.zeros_like(dg_ref)

        dh2 = dh2_ref[...]
        dact = _dot(dh2.astype(BF16), wd_ref[...])
        gate = gate_ref[...]
        sg = _sigmoid(gate)
        dgate = (dact * up_ref[...] * (sg * (1.0 + gate * (1.0 - sg)))).astype(BF16)
        dup = (dact * (gate * sg)).astype(BF16)
        dgate_ref[...] = dgate
        dup_ref[...] = dup
        dhn = _dot(dgate, wg_ref[...]) + _dot(dup, wu_ref[...])
        h1 = h1_ref[...]
        rstd = lax.rsqrt(jnp.mean(h1 * h1, axis=-1, keepdims=True) + RMS_EPS)
        nrm = h1 * rstd
        dg_ref[...] += jnp.sum(dhn * nrm, axis=0, keepdims=True)
        dn = dhn * g_ref[...]
        dh1 = dh2 + rstd * (dn - nrm * jnp.mean(dn * nrm, axis=-1, keepdims=True))
        dh1_ref[...] = dh1
        dycat_ref[...] = _dot(dh1.astype(BF16), wo_ref[...])

    rows = lambda w: pl.BlockSpec((r, w), lambda i: (i, 0))
    return pl.pallas_call(
        body, name="ffn_bwd", grid=(t // r,),
        in_specs=[rows(D), rows(D_FF), rows(D_FF), rows(D), _const_spec((D, D_FF)), _const_spec((D_FF, D)),
                  _const_spec((D_FF, D)), _const_spec((D, D)), _const_spec((1, D))],
        out_specs=[rows(D_FF), rows(D_FF), rows(D), rows(D), _acc_spec((1, D))],
        out_shape=[jax.ShapeDtypeStruct((t, D_FF), BF16), jax.ShapeDtypeStruct((t, D_FF), BF16),
                   jax.ShapeDtypeStruct((t, D), F32), jax.ShapeDtypeStruct((t, D), F32),
                   jax.ShapeDtypeStruct((1, D), F32)],
        compiler_params=_params(1),
    )(dh2, gate, up, h1, w_down_t, w_gate_t, w_up_t, w_out_t, g_ffn)


def _conv_bwd(dycat, yc, u, conv_w, ln_g, ln_b, n_ex, lp, plan=None):
    r = CONV_TILE
    nt = lp // r
    hb = r // CONV_SUB
    nsub = r // CONV_SUB

    def ln_bwd(dy, yc_rows, live, lg, lb):
        mu = jnp.mean(yc_rows, axis=-1, keepdims=True)
        cen = yc_rows - mu
        rs = lax.rsqrt(jnp.mean(cen * cen, axis=-1, keepdims=True) + LN_EPS)
        yn = cen * rs
        out = yn * lg + lb
        so = _sigmoid(out)
        dout = jnp.where(live, dy * (so * (1.0 + out * (1.0 - so))), 0.0)
        dyn = dout * lg
        dyc = rs * (dyn - jnp.mean(dyn, axis=-1, keepdims=True) - yn * jnp.mean(dyn * yn, axis=-1, keepdims=True))
        return dyc, dout, yn

    def body(dy_ref, dyn_ref, yc_ref, ycn_ref, cur_ref, prev_ref, w_ref, lg_ref, lb_ref,
             du_ref, dw_ref, db_ref, dlg_ref, dlb_ref, glu, dycs, dwacc, glu_sh, dycs_sh):
        b = pl.program_id(0)
        i = pl.program_id(1)
        first = jnp.logical_and(b == 0, i == 0)

        @pl.when(first)
        def _():
            dwacc[...] = jnp.zeros_like(dwacc)
            db_ref[...] = jnp.zeros_like(db_ref)
            dlg_ref[...] = jnp.zeros_like(dlg_ref)
            dlb_ref[...] = jnp.zeros_like(dlb_ref)

        lg, lb = lg_ref[...], lb_ref[...]
        cur = cur_ref[...]
        sig = _sigmoid(cur[:, C_CONV:])
        glu[CONV_SUB:CONV_SUB + r, :] = cur[:, :C_CONV] * sig
        pv = prev_ref[...]
        glu[0:CONV_SUB, :] = jnp.where(i > 0, pv[:, :C_CONV] * _sigmoid(pv[:, C_CONV:]), 0.0)

        row = i * r + lax.broadcasted_iota(jnp.int32, (r, 1), 0)
        dyc, dout, yn = ln_bwd(dy_ref[...], yc_ref[...], row >= PAD_ROWS, lg, lb)
        dycs[0:r, :] = dyc
        dycn, _, _ = ln_bwd(dyn_ref[...], ycn_ref[...], i < nt - 1, lg, lb)
        dycs[r:r + CONV_SUB, :] = dycn
        db_ref[...] += jnp.sum(dyc, axis=0, keepdims=True)
        dlg_ref[...] += jnp.sum(dout * yn, axis=0, keepdims=True)
        dlb_ref[...] += jnp.sum(dout, axis=0, keepdims=True)

        _shifted_copies(glu, glu_sh, r)
        _shifted_copies(dycs, dycs_sh, r)
        w = w_ref[...]
        for j in range(nsub):
            r0 = j * CONV_SUB
            dblk = dycs[r0:r0 + CONV_SUB, :]
            dglu = jnp.zeros((CONV_SUB, C_CONV), F32)
            for k in range(CONV_W):
                dglu = dglu + w[k:k + 1, :] * _shifted_rows(dycs, dycs_sh, r0 + (CONV_W - 1) - k)
                prod = dblk * _shifted_rows(glu, glu_sh, r0 + CONV_LEAD + k)
                dwacc[k] += prod.reshape(CONV_SUB // SUBLANES, SUBLANES, C_CONV).sum(axis=0)
            sg = sig[r0:r0 + CONV_SUB, :]
            cv = cur[r0:r0 + CONV_SUB, :C_CONV]
            du_ref[r0:r0 + CONV_SUB, :C_CONV] = (dglu * sg).astype(BF16)
            du_ref[r0:r0 + CONV_SUB, C_CONV:] = (dglu * cv * sg * (1.0 - sg)).astype(BF16)

        @pl.when(jnp.logical_and(b == n_ex - 1, i == nt - 1))
        def _():
            dw_ref[...] = jnp.sum(dwacc[...], axis=1)

    t = n_ex * lp
    cur_rows = lambda w, col: pl.BlockSpec((r, w), lambda b, i: (b * nt + i, col))
    nxt_rows = lambda w, col: pl.BlockSpec(
        (CONV_SUB, w), lambda b, i: (jnp.minimum((b * nt + i + 1) * hb, n_ex * nt * hb - 1), col))
    return _call(
        body, name="conv_bwd", grid=(n_ex, nt),
        in_specs=[cur_rows(C_CONV, 0), nxt_rows(C_CONV, 0), cur_rows(C_CONV, 0), nxt_rows(C_CONV, 0),
                  cur_rows(2 * C_CONV, 0),
                  pl.BlockSpec((CONV_SUB, 2 * C_CONV), lambda b, i: (jnp.maximum((b * nt + i) * hb - 1, 0), 0)),
                  _const_spec((32, C_CONV)), _const_spec((1, C_CONV)), _const_spec((1, C_CONV))],
        out_specs=[cur_rows(2 * C_CONV, 0), _acc_spec((32, C_CONV)), _acc_spec((1, C_CONV)),
                   _acc_spec((1, C_CONV)), _acc_spec((1, C_CONV))],
        out_shape=[jax.ShapeDtypeStruct((t, 2 * C_CONV), BF16), jax.ShapeDtypeStruct((32, C_CONV), F32),
                   jax.ShapeDtypeStruct((1, C_CONV), F32), jax.ShapeDtypeStruct((1, C_CONV), F32),
                   jax.ShapeDtypeStruct((1, C_CONV), F32)],
        scratch_shapes=[pltpu.VMEM((r + CONV_SUB, C_CONV), F32), pltpu.VMEM((r + CONV_SUB, C_CONV), F32),
                        pltpu.VMEM((32, 8, C_CONV), F32),
                        pltpu.VMEM((SUBLANES - 1, r + CONV_SUB - SUBLANES, C_CONV), F32),
                        pltpu.VMEM((SUBLANES - 1, r + CONV_SUB - SUBLANES, C_CONV), F32)],
        plan=plan,
    )(dycat, dycat, yc, yc, u, u, conv_w, ln_g, ln_b)


def _gla_bwd_per_head(dycat, u, states, w2, gb, ng, n_ex, lp, plan=None):
    nc = lp // CHUNK
    t = n_ex * lp

    def body(dy_ref, qk_ref, v_ref, g_ref, lr_ref, st_ref, w2_ref, gb_ref, ng_ref,
             du_ref, dw2_ref, dgb_ref, dng_ref, dstate):
        n = pl.program_id(0)
        chunk = nc - 1 - n

        @pl.when(n == 0)
        def _():
            dw2_ref[...] = jnp.zeros_like(dw2_ref)
            dgb_ref[...] = jnp.zeros_like(dgb_ref)
            dng_ref[...] = jnp.zeros_like(dng_ref)
            dstate[...] = jnp.zeros_like(dstate)

        for e in range(n_ex):
            one_example(e, chunk, dy_ref, qk_ref, v_ref, g_ref, lr_ref, st_ref, w2_ref, gb_ref, ng_ref,
                        du_ref, dw2_ref, dgb_ref, dng_ref, dstate)

    def one_example(e, chunk, dy_ref, qk_ref, v_ref, g_ref, lr_ref, st_ref, w2_ref, gb_ref, ng_ref,
                    du_ref, dw2_ref, dgb_ref, dng_ref, dstate):
        dy_ref, qk_ref, v_ref, g_ref, lr_ref, st_ref = (r.at[e] for r in (dy_ref, qk_ref, v_ref, g_ref, lr_ref, st_ref))
        du_ref, dstate = du_ref.at[e], dstate.at[e]
        qk = qk_ref[...]
        q, k = qk[:, :GLA_K], qk[:, GLA_K:]
        lr = lr_ref[...]
        z, a, live = _gla_gates(lr, w2_ref[...], gb_ref[...], chunk == 0)
        causal = _tri(True)
        b = jnp.dot(causal.astype(F32), a, preferred_element_type=F32, precision=lax.Precision.HIGHEST)
        bl = b[CHUNK - 1:CHUNK, :]
        e_pos, e_neg, e_dec = jnp.exp(b), jnp.exp(-b), jnp.exp(bl - b)
        q_f = q * (DK ** -0.5) * e_pos
        k_f = k * e_neg
        kd_f = k * e_dec
        q_in, k_in, k_dec = q_f.astype(BF16), k_f.astype(BF16), kd_f.astype(BF16)
        decay = jnp.exp(bl)
        v = v_ref[...]
        g = g_ref[...]
        dy = dy_ref[...]
        ngv = ng_ref[...]
        st = st_ref[...]
        st_b = st.astype(BF16)
        dst = dstate[...]
        dst_b = dst.astype(BF16)
        dqs, dks, dvs, dgs, dbs, dbls, new_dst = [], [], [], [], [], [], []
        dng = jnp.zeros((1, DV), F32)
        for h in range(N_HEADS):
            ks = slice(h * DK, (h + 1) * DK)
            vs = slice(h * DV, (h + 1) * DV)
            qh, kh, kdh = q_in[:, ks], k_in[:, ks], k_dec[:, ks]
            vh = v[:, vs].astype(BF16)
            s = jnp.where(causal, _dot_nt(qh, kh), 0.0).astype(BF16)
            o = _dot(s, vh) + _dot_nt(qh, st_b[:, ks])
            rstd = lax.rsqrt(jnp.mean(o * o, axis=-1, keepdims=True) + RMS_EPS)
            nrm = o * rstd
            gh = g[:, vs]
            sg = _sigmoid(gh)
            dyh = dy[:, vs]
            dgs.append(dyh * nrm * ngv * (sg * (1.0 + gh * (1.0 - sg))))
            dt = dyh * (gh * sg)
            dng = dng + jnp.sum(dt * nrm, axis=0, keepdims=True)
            dn = dt * ngv
            do = (rstd * (dn - nrm * jnp.mean(dn * nrm, axis=-1, keepdims=True))).astype(BF16)
            da = jnp.where(causal, _dot_nt(do, vh), 0.0).astype(BF16)
            dvs.append(_dot_tn(s, do) + _dot_nt(kdh, dst_b[:, ks]))
            dq_in = _dot(da, kh) + _dot(do, st_b[:, ks])
            dk_in = _dot_tn(da, qh)
            dk_dec = _dot(vh, dst_b[:, ks])
            new_dst.append(_dot_tn(do, qh) + decay[:, ks] * dst[:, ks])
            dbls.append(jnp.sum(dk_dec * kd_f[:, ks], axis=0, keepdims=True)
                        + decay[:, ks] * jnp.sum(dst[:, ks] * st[:, ks], axis=0, keepdims=True))
            dqs.append(dq_in * (DK ** -0.5) * e_pos[:, ks])
            dks.append(dk_in * e_neg[:, ks] + dk_dec * e_dec[:, ks])
            dbs.append(dq_in * q_f[:, ks] - dk_in * k_f[:, ks] - dk_dec * kd_f[:, ks])
        dstate[...] = jnp.concatenate(new_dst, axis=1)
        row = lax.broadcasted_iota(jnp.int32, (CHUNK, 1), 0)
        db = jnp.concatenate(dbs, axis=1) + jnp.where(row == CHUNK - 1, jnp.concatenate(dbls, axis=1), 0.0)
        da_log = jnp.dot(_tri(False).astype(F32), db, preferred_element_type=F32, precision=lax.Precision.HIGHEST)
        dz = jnp.where(live, da_log * (1.0 - _sigmoid(z)) * (1.0 / GATE_TAU), 0.0)
        dz_b = dz.astype(BF16)
        du_ref[:, 0:GLA_K] = jnp.concatenate(dqs, axis=1).astype(BF16)
        du_ref[:, GLA_K:2 * GLA_K] = jnp.concatenate(dks, axis=1).astype(BF16)
        du_ref[:, 2 * GLA_K:2 * GLA_K + GLA_V] = jnp.concatenate(dvs, axis=1).astype(BF16)
        du_ref[:, 2 * GLA_K + GLA_V:2 * GLA_K + 2 * GLA_V] = jnp.concatenate(dgs, axis=1).astype(BF16)
        du_ref[:, 2 * GLA_K + 2 * GLA_V:] = _dot_nt(dz_b, w2_ref[...]).astype(BF16)
        dw2_ref[...] += _dot_tn(lr.astype(BF16), dz_b)
        dgb_ref[...] += jnp.sum(dz, axis=0, keepdims=True)
        dng_ref[...] += dng

    u3 = u.reshape(n_ex, lp, D_IN_PAD)
    rev = lambda w, col: pl.BlockSpec((n_ex, CHUNK, w), lambda n: (0, nc - 1 - n, col))
    (du, d_w2, d_gb, d_ng), extra = _call(
        body, name="gla_bwd", grid=(nc,),
        in_specs=[rev(GLA_V, 1), rev(2 * GLA_K, 2), rev(GLA_V, 3), rev(GLA_V, 4), rev(128, 20),
                  pl.BlockSpec((n_ex, DV, GLA_K), lambda n: (0, nc - 1 - n, 0)),
                  _const_spec((128, GLA_K)), _const_spec((1, GLA_K)), _const_spec((1, DV))],
        out_specs=[rev(D_GLA_IN, 0), _acc_spec((128, GLA_K)), _acc_spec((1, GLA_K)), _acc_spec((1, DV))],
        out_shape=[jax.ShapeDtypeStruct((n_ex, lp, D_GLA_IN), BF16), jax.ShapeDtypeStruct((128, GLA_K), F32),
                   jax.ShapeDtypeStruct((1, GLA_K), F32), jax.ShapeDtypeStruct((1, DV), F32)],
        scratch_shapes=[pltpu.VMEM((n_ex, DV, GLA_K), F32)],
        plan=plan,
    )(dycat.reshape(n_ex, lp, D), u3, u3, u3, u3, states, w2, gb, ng)
    return (du.reshape(t, D_GLA_IN), d_w2, d_gb, d_ng), extra


HEAD_ROWS_ALL = N_HEADS * CHUNK


def _head_of(shape, axis, per_head):
    return lax.broadcasted_iota(jnp.int32, shape, axis) // per_head


def _expand(x, lanes_per_head):
    rows, lanes = HEAD_ROWS_ALL, x.shape[1]
    keep = _head_of((rows, lanes), 0, CHUNK) == _head_of((rows, lanes), 1, lanes_per_head)
    return jnp.where(keep, jnp.tile(x, (N_HEADS, 1)), 0.0)


def _expand_lanes(x):
    rows, w = x.shape
    keep = _head_of((rows, N_HEADS * w), 0, CHUNK) == _head_of((rows, N_HEADS * w), 1, w)
    return jnp.where(keep, jnp.tile(x, (1, N_HEADS)), 0.0)


def _expand_state(st):
    rows, lanes = N_HEADS * DV, st.shape[1]
    keep = _head_of((rows, lanes), 0, DV) == _head_of((rows, lanes), 1, DK)
    return jnp.where(keep, jnp.tile(st, (N_HEADS, 1)), 0.0)


def _fold(t, rows_per_head):
    lane_head = _head_of((rows_per_head, t.shape[1]), 1, DK)
    out = jnp.where(lane_head == 0, t[0:rows_per_head], 0.0)
    for h in range(1, N_HEADS):
        out = out + jnp.where(lane_head == h, t[h * rows_per_head:(h + 1) * rows_per_head], 0.0)
    return out


def _rows_by_head(x):
    return jnp.concatenate([x[:, h * DV:(h + 1) * DV] for h in range(N_HEADS)], axis=0)


def _lanes_by_head(x):
    return jnp.concatenate([x[h * CHUNK:(h + 1) * CHUNK] for h in range(N_HEADS)], axis=1)


def _running_sum(a, lower):
    hi = a.astype(BF16)
    rest = a - hi.astype(F32)
    mid = rest.astype(BF16)
    lo = (rest - mid.astype(F32)).astype(BF16)
    w = a.shape[1]
    parts = _dot(_tri(lower).astype(F32).astype(BF16), jnp.concatenate([hi, mid, lo], axis=1))
    return parts[:, :w] + parts[:, w:2 * w] + parts[:, 2 * w:]


def _stacked_causal():
    i = lax.broadcasted_iota(jnp.int32, (HEAD_ROWS_ALL, CHUNK), 0) % CHUNK
    j = lax.broadcasted_iota(jnp.int32, (HEAD_ROWS_ALL, CHUNK), 1)
    return i >= j


def _gla_chunk(q, k, v, lr, st, w2, gb, first_chunk):
    z, a, live = _gla_gates(lr, w2, gb, first_chunk)
    yield
    b = _running_sum(a, True)
    yield
    bl = b[CHUNK - 1:CHUNK, :]
    e_pos, e_neg, e_dec = jnp.exp(b), jnp.exp(-b), jnp.exp(bl - b)
    q_f, k_f, kd_f = q * (DK ** -0.5) * e_pos, k * e_neg, k * e_dec
    qx = _expand(q_f, DK).astype(BF16)
    k_in, k_dec, v_b = k_f.astype(BF16), kd_f.astype(BF16), v.astype(BF16)
    s = jnp.where(_stacked_causal(), _dot_nt(qx, k_in), 0.0).astype(BF16)
    o_inter = _dot_nt(qx, st.astype(BF16))
    yield
    p = _dot(s, v_b)
    yield
    o = jnp.concatenate([p[h * CHUNK:(h + 1) * CHUNK, h * DV:(h + 1) * DV] for h in range(N_HEADS)], axis=0) + o_inter
    return dict(z=z, live=live, bl=bl, e_pos=e_pos, e_neg=e_neg, e_dec=e_dec, q_f=q_f, k_f=k_f, kd_f=kd_f,
                qx=qx, k_in=k_in, k_dec=k_dec, v_b=v_b, s=s, o=o, decay=jnp.exp(bl))


def _gla_fwd(u, w2, gb, ng, n_ex, lp, plan=None):
    nc = lp // CHUNK
    t = n_ex * lp

    def body(qk_ref, v_ref, g_ref, lr_ref, w2_ref, gb_ref, ng_ref, y_ref, st_ref, state):
        n = pl.program_id(0)

        @pl.when(n == 0)
        def _():
            state[...] = jnp.zeros_like(state)

        def one_example(e):
            st = state[e]
            st_ref[e] = st
            qk = qk_ref[e]
            c = yield from _gla_chunk(qk[:, :GLA_K], qk[:, GLA_K:], v_ref[e], lr_ref[e], st, w2_ref[...],
                                      gb_ref[...], n == 0)
            o = c["o"]
            rstd = lax.rsqrt(jnp.mean(o * o, axis=-1, keepdims=True) + RMS_EPS)
            g = _rows_by_head(g_ref[e])
            y_ref[e] = _lanes_by_head(o * rstd * ng_ref[...] * (g * _sigmoid(g))).astype(BF16)
            state[e] = c["decay"] * st + _fold(_dot_tn(c["v_b"], c["k_dec"]), DV)

        _in_lockstep(one_example(e) for e in range(n_ex))

    u3 = u.reshape(n_ex, lp, D_IN_PAD)
    blk = lambda w, col: pl.BlockSpec((n_ex, CHUNK, w), lambda n: (0, n, col))
    (y, states), extra = _call(
        body, name="gla_fwd", grid=(nc,),
        in_specs=[blk(2 * GLA_K, 2), blk(GLA_V, 3), blk(GLA_V, 4), blk(128, 20),
                  _const_spec((128, GLA_K)), _const_spec((1, GLA_K)), _const_spec((1, DV))],
        out_specs=[blk(GLA_V, 0), pl.BlockSpec((n_ex, DV, GLA_K), lambda n: (0, n, 0))],
        out_shape=[jax.ShapeDtypeStruct((n_ex, lp, GLA_V), BF16),
                   jax.ShapeDtypeStruct((n_ex, nc * DV, GLA_K), F32)],
        scratch_shapes=[pltpu.VMEM((n_ex, DV, GLA_K), F32)],
        plan=plan,
    )(u3, u3, u3, u3, w2, gb, ng)
    return (y.reshape(t, GLA_V), states), extra


def _gla_bwd(dycat, u, states, w2, gb, ng, n_ex, lp, plan=None):
    nc = lp // CHUNK
    t = n_ex * lp

    def body(dy_ref, qk_ref, v_ref, g_ref, lr_ref, st_ref, w2_ref, gb_ref, ng_ref,
             du_ref, dw2_ref, dgb_ref, dng_ref, dstate):
        n = pl.program_id(0)
        chunk = nc - 1 - n

        @pl.when(n == 0)
        def _():
            dw2_ref[...] = jnp.zeros_like(dw2_ref)
            dgb_ref[...] = jnp.zeros_like(dgb_ref)
            dng_ref[...] = jnp.zeros_like(dng_ref)
            dstate[...] = jnp.zeros_like(dstate)

        def one_example(e):
            qk = qk_ref[e]
            lr = lr_ref[e]
            st = st_ref[e]
            dst = dstate[e]
            c = yield from _gla_chunk(qk[:, :GLA_K], qk[:, GLA_K:], v_ref[e], lr, st, w2_ref[...], gb_ref[...],
                                      chunk == 0)
            qx, k_in, k_dec, v_b, s, o = c["qx"], c["k_in"], c["k_dec"], c["v_b"], c["s"], c["o"]
            ngv = ng_ref[...]
            rstd = lax.rsqrt(jnp.mean(o * o, axis=-1, keepdims=True) + RMS_EPS)
            nrm = o * rstd
            g = _rows_by_head(g_ref[e])
            dy = _rows_by_head(dy_ref[e])
            sg = _sigmoid(g)
            dg = dy * nrm * ngv * (sg * (1.0 + g * (1.0 - sg)))
            dt = dy * (g * sg)
            dng_ref[...] += jnp.sum(dt * nrm, axis=0, keepdims=True)
            dn = dt * ngv
            do = rstd * (dn - nrm * jnp.mean(dn * nrm, axis=-1, keepdims=True))
            do_b = do.astype(BF16)
            dox = _expand_lanes(do).astype(BF16)
            dstx = _expand_state(dst).astype(BF16)
            yield
            da = jnp.where(_stacked_causal(), _dot_nt(dox, v_b), 0.0).astype(BF16)
            dv = _dot_tn(s, dox) + _dot_nt(k_dec, dstx)
            dk_dec = _dot(v_b, dstx)
            dstate[e] = _dot_tn(do_b, qx) + c["decay"] * dst
            yield
            dq_in = _fold(_dot(da, k_in) + _dot(do_b, st.astype(BF16)), CHUNK)
            dk_in = _dot_tn(da, qx)
            yield
            dbl = (jnp.sum(dk_dec * c["kd_f"], axis=0, keepdims=True)
                   + c["decay"] * jnp.sum(dst * st, axis=0, keepdims=True))
            dq = dq_in * (DK ** -0.5) * c["e_pos"]
            dk = dk_in * c["e_neg"] + dk_dec * c["e_dec"]
            db = dq_in * c["q_f"] - dk_in * c["k_f"] - dk_dec * c["kd_f"]
            row = lax.broadcasted_iota(jnp.int32, (CHUNK, 1), 0)
            da_log = _running_sum(db + jnp.where(row == CHUNK - 1, dbl, 0.0), False)
            yield
            dz = jnp.where(c["live"], da_log * (1.0 - _sigmoid(c["z"])) * (1.0 / GATE_TAU), 0.0)
            dz_b = dz.astype(BF16)
            out = du_ref.at[e]
            out[:, 0:GLA_K] = dq.astype(BF16)
            out[:, GLA_K:2 * GLA_K] = dk.astype(BF16)
            out[:, 2 * GLA_K:2 * GLA_K + GLA_V] = dv.astype(BF16)
            out[:, 2 * GLA_K + GLA_V:2 * GLA_K + 2 * GLA_V] = _lanes_by_head(dg).astype(BF16)
            out[:, 2 * GLA_K + 2 * GLA_V:] = _dot_nt(dz_b, w2_ref[...]).astype(BF16)
            dw2_ref[...] += _dot_tn(lr.astype(BF16), dz_b)
            dgb_ref[...] += jnp.sum(dz, axis=0, keepdims=True)

        _in_lockstep(one_example(e) for e in range(n_ex))

    u3 = u.reshape(n_ex, lp, D_IN_PAD)
    rev = lambda w, col: pl.BlockSpec((n_ex, CHUNK, w), lambda n: (0, nc - 1 - n, col))
    (du, d_w2, d_gb, d_ng), extra = _call(
        body, name="gla_bwd", grid=(nc,),
        in_specs=[rev(GLA_V, 1), rev(2 * GLA_K, 2), rev(GLA_V, 3), rev(GLA_V, 4), rev(128, 20),
                  pl.BlockSpec((n_ex, DV, GLA_K), lambda n: (0, nc - 1 - n, 0)),
                  _const_spec((128, GLA_K)), _const_spec((1, GLA_K)), _const_spec((1, DV))],
        out_specs=[rev(D_GLA_IN, 0), _acc_spec((128, GLA_K)), _acc_spec((1, GLA_K)), _acc_spec((1, DV))],
        out_shape=[jax.ShapeDtypeStruct((n_ex, lp, D_GLA_IN), BF16), jax.ShapeDtypeStruct((128, GLA_K), F32),
                   jax.ShapeDtypeStruct((1, GLA_K), F32), jax.ShapeDtypeStruct((1, DV), F32)],
        scratch_shapes=[pltpu.VMEM((n_ex, DV, GLA_K), F32)],
        plan=plan,
    )(dycat.reshape(n_ex, lp, D), u3, u3, u3, u3, states, w2, gb, ng)
    return (du.reshape(t, D_GLA_IN), d_w2, d_gb, d_ng), extra


def _in_proj_bwd(du_conv, du_gla, w_in_t_conv, w_in_t_gla, h0, dh1, g_mix, plan=None):
    t = h0.shape[0]
    r = _row_tile(t, 384)

    def body(dc_ref, dg_ref, wc_ref, wg_ref, h_ref, dh1_ref, g_ref, dh0_ref, dgm_ref):
        @pl.when(pl.program_id(0) == 0)
        def _():
            dgm_ref[...] = jnp.zeros_like(dgm_ref)

        dhn = _dot(dc_ref[...], wc_ref[...]) + _dot(dg_ref[...], wg_ref[...])
        h = h_ref[...]
        rstd = lax.rsqrt(jnp.mean(h * h, axis=-1, keepdims=True) + RMS_EPS)
        nrm = h * rstd
        dgm_ref[...] += jnp.sum(dhn * nrm, axis=0, keepdims=True)
        dn = dhn * g_ref[...]
        dh0_ref[...] = dh1_ref[...] + rstd * (dn - nrm * jnp.mean(dn * nrm, axis=-1, keepdims=True))

    rows = lambda w: pl.BlockSpec((r, w), lambda i: (i, 0))
    return _call(
        body, name="in_proj_bwd", grid=(t // r,),
        in_specs=[rows(2 * C_CONV), rows(D_GLA_IN), _const_spec((2 * C_CONV, D)), _const_spec((D_GLA_IN, D)),
                  rows(D), rows(D), _const_spec((1, D))],
        out_specs=[rows(D), _acc_spec((1, D))],
        out_shape=[jax.ShapeDtypeStruct((t, D), F32), jax.ShapeDtypeStruct((1, D), F32)],
        plan=plan,
    )(du_conv, du_gla, w_in_t_conv, w_in_t_gla, h0, dh1, g_mix)


def _wgrad(x, dy, name):
    t, m = x.shape
    n = dy.shape[1]
    tk = t // 3 if t % (3 * 128) == 0 else _row_tile(t, 384)
    tm = m if m <= D_GLA_IN else m // 2
    tn = n

    def body(x_ref, dy_ref, o_ref):
        @pl.when(pl.program_id(2) == 0)
        def _():
            o_ref[...] = jnp.zeros_like(o_ref)

        o_ref[...] += _dot_tn(x_ref[...].astype(BF16), dy_ref[...].astype(BF16))

    return pl.pallas_call(
        body, name=name, grid=(m // tm, n // tn, t // tk),
        in_specs=[pl.BlockSpec((tk, tm), lambda i, j, k: (k, i)), pl.BlockSpec((tk, tn), lambda i, j, k: (k, j))],
        out_specs=pl.BlockSpec((tm, tn), lambda i, j, k: (i, j)),
        out_shape=jax.ShapeDtypeStruct((m, n), F32),
        compiler_params=_params(3),
    )(x, dy)


def _mesh_pos():
    return lax.axis_index("x"), lax.axis_index("y"), lax.axis_index("c")


def _other_chips(x, y):
    return [(1 - x, y), (x, 1 - y), (1 - x, 1 - y)]


HBM_SPEC = pl.BlockSpec(memory_space=pltpu.HBM)


def _gather_shards(shards):
    n = len(shards)

    def body(*refs):
        ins, outs = refs[:n], refs[n:2 * n]
        send_sems, recv_sems, local_sems = refs[2 * n:]
        x, y, c = _mesh_pos()
        mine = 2 * x + y
        chips = _other_chips(x, y)
        local = [pltpu.make_async_copy(ins[a], outs[a].at[mine], local_sems.at[a]) for a in range(n)]
        for cp in local:
            cp.start()

        def remote(a, k, block):
            px, py = chips[k]
            return pltpu.make_async_remote_copy(
                src_ref=ins[a], dst_ref=outs[a].at[block], send_sem=send_sems.at[3 * a + k],
                recv_sem=recv_sems.at[3 * a + k], device_id=(px, py, c), device_id_type=MESH)

        sends = [remote(a, k, mine) for a in range(n) for k in range(3)]
        for cp in sends:
            cp.start()
        for a in range(n):
            for k, (px, py) in enumerate(chips):
                remote(a, k, 2 * px + py).wait_recv()
        for cp in sends:
            cp.wait_send()
        for cp in local:
            cp.wait()

    return pl.pallas_call(
        body, name="gather_shards",
        in_specs=[HBM_SPEC] * n, out_specs=[HBM_SPEC] * n,
        out_shape=[jax.ShapeDtypeStruct((N_CHIPS,) + s.shape, s.dtype) for s in shards],
        scratch_shapes=[pltpu.SemaphoreType.DMA((3 * n,)), pltpu.SemaphoreType.DMA((3 * n,)),
                        pltpu.SemaphoreType.DMA((n,))],
        compiler_params=pltpu.CompilerParams(has_side_effects=True),
    )(*shards)


def _send_half_to_sibling(g2):
    def body(g_ref, recv_ref, send_sem, recv_sem):
        x, y, c = _mesh_pos()
        cp = pltpu.make_async_remote_copy(
            src_ref=g_ref.at[1 - c], dst_ref=recv_ref, send_sem=send_sem, recv_sem=recv_sem,
            device_id=(x, y, 1 - c), device_id_type=MESH)
        cp.start()
        cp.wait()

    return pl.pallas_call(
        body, name="rs_to_sibling", in_specs=[HBM_SPEC], out_specs=HBM_SPEC,
        out_shape=jax.ShapeDtypeStruct(g2.shape[1:], g2.dtype),
        scratch_shapes=[pltpu.SemaphoreType.DMA(()), pltpu.SemaphoreType.DMA(())],
        compiler_params=pltpu.CompilerParams(has_side_effects=True),
    )(g2)


def _add_own_half(g2, recv, c):
    rows = N_CHIPS * HALF_ROWS
    tr = 512
    g2f = g2.reshape(2, rows, D)
    recvf = recv.reshape(rows, D)

    def body(c_ref, a_ref, b_ref, o_ref):
        o_ref[...] = a_ref[0] + b_ref[...]

    out = pl.pallas_call(
        body, name="rs_add_halves",
        grid_spec=pltpu.PrefetchScalarGridSpec(
            num_scalar_prefetch=1, grid=(rows // tr,),
            in_specs=[pl.BlockSpec((1, tr, D), lambda i, s: (s[0], i, 0)), pl.BlockSpec((tr, D), lambda i, s: (i, 0))],
            out_specs=pl.BlockSpec((tr, D), lambda i, s: (i, 0))),
        out_shape=jax.ShapeDtypeStruct((rows, D), F32),
        compiler_params=_params(1),
    )(jnp.reshape(c, (1,)).astype(jnp.int32), g2f, recvf)
    return out.reshape(N_CHIPS, HALF_ROWS, D)


def _exchange_chip_sums(p):
    def body(p_ref, out_ref, send_sems, recv_sems, local_sem):
        x, y, c = _mesh_pos()
        mine = 2 * x + y
        chips = _other_chips(x, y)
        local = pltpu.make_async_copy(p_ref.at[mine], out_ref.at[mine], local_sem)
        local.start()

        def remote(k, src_block, dst_block):
            px, py = chips[k]
            return pltpu.make_async_remote_copy(
                src_ref=p_ref.at[src_block], dst_ref=out_ref.at[dst_block], send_sem=send_sems.at[k],
                recv_sem=recv_sems.at[k], device_id=(px, py, c), device_id_type=MESH)

        sends = [remote(k, 2 * px + py, mine) for k, (px, py) in enumerate(chips)]
        for cp in sends:
            cp.start()
        for k, (px, py) in enumerate(chips):
            remote(k, mine, 2 * px + py).wait_recv()
        for cp in sends:
            cp.wait_send()
        local.wait()

    return pl.pallas_call(
        body, name="rs_chip_exchange", in_specs=[HBM_SPEC], out_specs=HBM_SPEC,
        out_shape=jax.ShapeDtypeStruct(p.shape, p.dtype),
        scratch_shapes=[pltpu.SemaphoreType.DMA((3,)), pltpu.SemaphoreType.DMA((3,)), pltpu.SemaphoreType.DMA(())],
        compiler_params=pltpu.CompilerParams(has_side_effects=True),
    )(p)


def _sum_chips(parts):
    tr = 512

    def body(p_ref, o_ref):
        o_ref[...] = ((p_ref[0] + p_ref[1]) + p_ref[2]) + p_ref[3]

    return pl.pallas_call(
        body, name="rs_sum_chips", grid=(HALF_ROWS // tr,),
        in_specs=[pl.BlockSpec((N_CHIPS, tr, D), lambda i: (0, i, 0))],
        out_specs=pl.BlockSpec((tr, D), lambda i: (i, 0)),
        out_shape=jax.ShapeDtypeStruct((HALF_ROWS, D), F32),
        compiler_params=_params(1),
    )(parts)


def _share_with_sibling(half):
    def body(h_ref, out_ref, send_sem, recv_sem, local_sem):
        x, y, c = _mesh_pos()
        local = pltpu.make_async_copy(h_ref, out_ref.at[c], local_sem)
        local.start()
        cp = pltpu.make_async_remote_copy(
            src_ref=h_ref, dst_ref=out_ref.at[c], send_sem=send_sem, recv_sem=recv_sem,
            device_id=(x, y, 1 - c), device_id_type=MESH)
        cp.start()
        pltpu.make_async_remote_copy(
            src_ref=h_ref, dst_ref=out_ref.at[1 - c], send_sem=send_sem, recv_sem=recv_sem,
            device_id=(x, y, 1 - c), device_id_type=MESH).wait_recv()
        cp.wait_send()
        local.wait()

    return pl.pallas_call(
        body, name="rs_share_sibling", in_specs=[HBM_SPEC], out_specs=HBM_SPEC,
        out_shape=jax.ShapeDtypeStruct((2,) + half.shape, half.dtype),
        scratch_shapes=[pltpu.SemaphoreType.DMA(()), pltpu.SemaphoreType.DMA(()), pltpu.SemaphoreType.DMA(())],
        compiler_params=pltpu.CompilerParams(has_side_effects=True),
    )(half)


def _adam_update(g, w, m, v):
    m2 = ADAM_B1 * m + (1.0 - ADAM_B1) * g
    v2 = ADAM_B2 * v + (1.0 - ADAM_B2) * (g * g)
    m_hat = m2 / (1.0 - ADAM_B1 ** ADAM_STEP)
    v_hat = v2 / (1.0 - ADAM_B2 ** ADAM_STEP)
    delta = -ADAM_LR * (m_hat / (jnp.sqrt(v_hat) + ADAM_EPS) + ADAM_WD * w)
    return delta, m2, v2


def _adamw_slab(g, w, m, v):
    rows = g.shape[0]
    tr = 256

    def body(g_ref, w_ref, m_ref, v_ref, d_ref, m2_ref, v2_ref):
        d_ref[...], m2_ref[...], v2_ref[...] = _adam_update(g_ref[...], w_ref[...], m_ref[...], v_ref[...])

    spec = pl.BlockSpec((tr, D), lambda i: (i, 0))
    return pl.pallas_call(
        body, name="adamw_slab", grid=(rows // tr,), in_specs=[spec] * 4, out_specs=[spec] * 3,
        out_shape=[jax.ShapeDtypeStruct((rows, D), F32)] * 3,
        compiler_params=_params(1),
    )(g, w, m, v)


def _allreduce_small_adamw(part, w, m, v):
    def body(p_ref, w_ref, m_ref, v_ref, g_ref, d_ref, m2_ref, v2_ref, slots, send_sems, recv_sems):
        x, y, c = _mesh_pos()
        mine = 4 * x + 2 * y + c
        peers = [(px, py, pc) for px in (x, 1 - x) for py in (y, 1 - y) for pc in (c, 1 - c)][1:]

        def remote(k, slot):
            return pltpu.make_async_remote_copy(
                src_ref=p_ref, dst_ref=slots.at[slot], send_sem=send_sems.at[k], recv_sem=recv_sems.at[k],
                device_id=peers[k], device_id_type=MESH)

        sends = [remote(k, mine) for k in range(7)]
        for cp in sends:
            cp.start()
        slots[mine] = p_ref[...]
        for k, (px, py, pc) in enumerate(peers):
            remote(k, 4 * px + 2 * py + pc).wait_recv()
        for cp in sends:
            cp.wait_send()
        g = slots[0]
        for d in range(1, 8):
            g = g + slots[d]
        g_ref[...] = g
        d_ref[...], m2_ref[...], v2_ref[...] = _adam_update(g, w_ref[...], m_ref[...], v_ref[...])

    vm = pl.BlockSpec(memory_space=pltpu.VMEM)
    shape = jax.ShapeDtypeStruct(part.shape, F32)
    return pl.pallas_call(
        body, name="small_allreduce_adamw", in_specs=[vm] * 4, out_specs=[vm] * 4, out_shape=[shape] * 4,
        scratch_shapes=[pltpu.VMEM((8,) + part.shape, F32), pltpu.SemaphoreType.DMA((7,)),
                        pltpu.SemaphoreType.DMA((7,))],
        compiler_params=pltpu.CompilerParams(has_side_effects=True),
    )(part, w, m, v)


def _half(ref, c, axis):
    n = ref.shape[axis] // 2
    return ref.at[(slice(None),) * axis + (pl.ds(c * n, n),)]


def _remote(src, dst, send_sem, recv_sem, device):
    return pltpu.make_async_remote_copy(src_ref=src, dst_ref=dst, send_sem=send_sem, recv_sem=recv_sem,
                                        device_id=device, device_id_type=MESH)


def _gather_weights(split, axes, whole):
    ns, n = len(split), len(split) + len(whole)

    def body(*refs):
        ins, outs = refs[:n], refs[n:2 * n]
        ici_send, ici_recv, d2d_send, d2d_recv, local_sems = refs[2 * n:]
        x, y, c = _mesh_pos()
        mine = 2 * x + y
        chips = _other_chips(x, y)
        local = [pltpu.make_async_copy(ins[a], outs[a].at[mine], local_sems.at[a]) for a in range(n)]
        for cp in local:
            cp.start()

        def ici(a, k, block):
            px, py = chips[k]
            src, dst = ins[a], outs[a].at[block]
            if a < ns:
                src, dst = _half(src, c, axes[a]), _half(dst, c, axes[a])
            return _remote(src, dst, ici_send.at[3 * a + k], ici_recv.at[3 * a + k], (px, py, c))

        def d2d(a, k, block, half):
            part = _half(outs[a].at[block], half, axes[a])
            return _remote(part, part, d2d_send.at[3 * a + k], d2d_recv.at[3 * a + k], (x, y, 1 - c))

        sends = [ici(a, k, mine) for a in range(n) for k in range(3)]
        for cp in sends:
            cp.start()
        for a in range(n):
            for k, (px, py) in enumerate(chips):
                ici(a, k, 2 * px + py).wait_recv()
                if a < ns:
                    sends.append(d2d(a, k, 2 * px + py, c))
                    sends[-1].start()
        for a in range(ns):
            for k, (px, py) in enumerate(chips):
                d2d(a, k, 2 * px + py, 1 - c).wait_recv()
        for cp in sends:
            cp.wait_send()
        for cp in local:
            cp.wait()

    arrays = list(split) + list(whole)
    return pl.pallas_call(
        body, name="gather_weights", in_specs=[HBM_SPEC] * n, out_specs=[HBM_SPEC] * n,
        out_shape=[jax.ShapeDtypeStruct((N_CHIPS,) + s.shape, s.dtype) for s in arrays],
        scratch_shapes=[pltpu.SemaphoreType.DMA((3 * n,)), pltpu.SemaphoreType.DMA((3 * n,)),
                        pltpu.SemaphoreType.DMA((3 * ns,)), pltpu.SemaphoreType.DMA((3 * ns,)),
                        pltpu.SemaphoreType.DMA((n,))],
        compiler_params=pltpu.CompilerParams(has_side_effects=True),
    )(*arrays)


def _rs_to_sibling(gs):
    n = len(gs)

    def body(*refs):
        ins, outs, send_sems, recv_sems = refs[:n], refs[n:2 * n], refs[2 * n], refs[2 * n + 1]
        x, y, c = _mesh_pos()
        copies = [_remote(_half(ins[a], 1 - c, 2), outs[a], send_sems.at[a], recv_sems.at[a], (x, y, 1 - c))
                  for a in range(n)]
        for cp in copies:
            cp.start()
        for cp in copies:
            cp.wait()

    return pl.pallas_call(
        body, name="rs_to_sibling", in_specs=[HBM_SPEC] * n, out_specs=[HBM_SPEC] * n,
        out_shape=[jax.ShapeDtypeStruct(g.shape[:2] + (g.shape[2] // 2,), g.dtype) for g in gs],
        scratch_shapes=[pltpu.SemaphoreType.DMA((n,)), pltpu.SemaphoreType.DMA((n,))],
        compiler_params=pltpu.CompilerParams(has_side_effects=True),
    )(*gs)


def _rs_add_halves(g, recv, c, name):
    _, rows, w = g.shape
    h = w // 2
    tr = rows // 2 if rows % 16 == 0 and rows > 64 else rows

    def body(c_ref, a_ref, b_ref, o_ref):
        o_ref[...] = (a_ref[...] + b_ref[...]).astype(BF16)

    return pl.pallas_call(
        body, name=name,
        grid_spec=pltpu.PrefetchScalarGridSpec(
            num_scalar_prefetch=1, grid=(N_CHIPS, rows // tr),
            in_specs=[pl.BlockSpec((1, tr, h), lambda j, i, s: (j, i, s[0])),
                      pl.BlockSpec((1, tr, h), lambda j, i, s: (j, i, 0))],
            out_specs=pl.BlockSpec((1, tr, h), lambda j, i, s: (j, i, 0))),
        out_shape=jax.ShapeDtypeStruct((N_CHIPS, rows, h), BF16),
        compiler_params=_params(2),
    )(jnp.reshape(c, (1,)).astype(jnp.int32), g, recv)


def _rs_chip_exchange(ps):
    n = len(ps)

    def body(*refs):
        ins, outs = refs[:n], refs[n:2 * n]
        send_sems, recv_sems, local_sems = refs[2 * n:]
        x, y, c = _mesh_pos()
        mine = 2 * x + y
        chips = _other_chips(x, y)
        local = [pltpu.make_async_copy(ins[a].at[mine], outs[a].at[mine], local_sems.at[a]) for a in range(n)]
        for cp in local:
            cp.start()

        def ici(a, k, src_block, dst_block):
            px, py = chips[k]
            return _remote(ins[a].at[src_block], outs[a].at[dst_block], send_sems.at[3 * a + k],
                           recv_sems.at[3 * a + k], (px, py, c))

        sends = [ici(a, k, 2 * px + py, mine) for a in range(n) for k, (px, py) in enumerate(chips)]
        for cp in sends:
            cp.start()
        for a in range(n):
            for k, (px, py) in enumerate(chips):
                ici(a, k, mine, 2 * px + py).wait_recv()
        for cp in sends:
            cp.wait_send()
        for cp in local:
            cp.wait()

    return pl.pallas_call(
        body, name="rs_chip_exchange", in_specs=[HBM_SPEC] * n, out_specs=[HBM_SPEC] * n,
        out_shape=[jax.ShapeDtypeStruct(p.shape, p.dtype) for p in ps],
        scratch_shapes=[pltpu.SemaphoreType.DMA((3 * n,)), pltpu.SemaphoreType.DMA((3 * n,)),
                        pltpu.SemaphoreType.DMA((n,))],
        compiler_params=pltpu.CompilerParams(has_side_effects=True),
    )(*ps)


def _rs_sum_chips(parts, name):
    _, rows, h = parts.shape
    tr = rows // 2 if rows % 16 == 0 and rows > 64 else rows

    def body(p_ref, o_ref):
        p = p_ref[...].astype(F32)
        o_ref[...] = ((p[0] + p[1]) + p[2]) + p[3]

    return pl.pallas_call(
        body, name=name, grid=(rows // tr,),
        in_specs=[pl.BlockSpec((N_CHIPS, tr, h), lambda i: (0, i, 0))],
        out_specs=pl.BlockSpec((tr, h), lambda i: (i, 0)),
        out_shape=jax.ShapeDtypeStruct((rows, h), F32),
        compiler_params=_params(1),
    )(parts)


def _rs_share(halves):
    n = len(halves)

    def body(*refs):
        ins, outs = refs[:n], refs[n:2 * n]
        send_sems, recv_sems, local_sems = refs[2 * n:]
        x, y, c = _mesh_pos()
        local = [pltpu.make_async_copy(ins[a], _half(outs[a], c, 1), local_sems.at[a]) for a in range(n)]
        for cp in local:
            cp.start()
        sends = [_remote(ins[a], _half(outs[a], c, 1), send_sems.at[a], recv_sems.at[a], (x, y, 1 - c))
                 for a in range(n)]
        for cp in sends:
            cp.start()
        for a in range(n):
            _remote(ins[a], _half(outs[a], 1 - c, 1), send_sems.at[a], recv_sems.at[a], (x, y, 1 - c)).wait_recv()
        for cp in sends:
            cp.wait_send()
        for cp in local:
            cp.wait()

    return pl.pallas_call(
        body, name="rs_share", in_specs=[HBM_SPEC] * n, out_specs=[HBM_SPEC] * n,
        out_shape=[jax.ShapeDtypeStruct((p.shape[0], 2 * p.shape[1]), p.dtype) for p in halves],
        scratch_shapes=[pltpu.SemaphoreType.DMA((n,)), pltpu.SemaphoreType.DMA((n,)),
                        pltpu.SemaphoreType.DMA((n,))],
        compiler_params=pltpu.CompilerParams(has_side_effects=True),
    )(*halves)


def _adamw(g, w, m, v, name):
    rows, cols = g.shape
    tr = 256 if rows % 256 == 0 else (rows // 2 if rows % 16 == 0 and rows > 64 else rows)

    def body(g_ref, w_ref, m_ref, v_ref, d_ref, m2_ref, v2_ref):
        d_ref[...], m2_ref[...], v2_ref[...] = _adam_update(g_ref[...], w_ref[...], m_ref[...], v_ref[...])

    spec = pl.BlockSpec((tr, cols), lambda i: (i, 0))
    return pl.pallas_call(
        body, name=name, grid=(rows // tr,), in_specs=[spec] * 4, out_specs=[spec] * 3,
        out_shape=[jax.ShapeDtypeStruct((rows, cols), F32)] * 3,
        compiler_params=_params(1),
    )(g, w, m, v)


def _rows_of(a):
    flat = a.reshape(-1)
    pad = (-flat.shape[0]) % D
    if pad:
        flat = jnp.concatenate([flat, jnp.zeros((pad,), flat.dtype)])
    return flat.reshape(-1, D)


SLAB_PARTS = (("w_in", (D, D_IN // N_CHIPS)), ("w_out", (D // N_CHIPS, D)), ("w_ffn_gate", (D, D_FF // N_CHIPS)),
              ("w_ffn_up", (D, D_FF // N_CHIPS)), ("w_ffn_down", (D_FF // N_CHIPS, D)),
              ("meta_tokens", (N_META, D // N_CHIPS)), ("conv_w", (CONV_W, C_CONV // N_CHIPS)),
              ("gla_w_gate2", (RANK, GLA_K // N_CHIPS)))


def _pack_slab(parts):
    rows = [_rows_of(parts[name].reshape(shape)) for name, shape in SLAB_PARTS]
    used = sum(r.shape[0] for r in rows)
    rows.append(jnp.zeros((SLAB_ROWS - used, D), F32))
    return jnp.concatenate(rows, axis=0)


def _unpack_slab(slab, lead):
    out, r0 = {}, 0
    for name, shape in SLAB_PARTS:
        size = shape[0] * shape[1]
        nrows = -(-size // D)
        out[name] = slab[r0:r0 + nrows].reshape(-1)[:size].reshape(lead[name] + shape)
        r0 += nrows
    return out


SMALL_PARTS = (("norm_mix_g", 0, 0, D), ("norm_ffn_g", 1, 0, D), ("norm_final_g", 2, 0, D),
               ("conv_b", 3, 0, C_CONV), ("conv_ln_g", 3, C_CONV, C_CONV), ("conv_ln_b", 4, 0, C_CONV),
               ("gla_gate_b", 4, C_CONV, GLA_K), ("gla_norm_g", 4, C_CONV + GLA_K, DV))


def _pack_small(parts):
    slab = jnp.zeros((SMALL_ROWS, D), F32)
    for name, row, col, size in SMALL_PARTS:
        slab = lax.dynamic_update_slice(slab, parts[name].reshape(1, size).astype(F32), (row, col))
    return slab


def _unpack_small(slab, shapes):
    return {name: slab[row, col:col + size].reshape(shapes[name]) for name, row, col, size in SMALL_PARTS}


def _column_block(full, j, width):
    return lax.dynamic_slice_in_dim(full, j * width, width, axis=1)


def _local_step(x, target, w):
    n_ex, seq, _ = x.shape
    lp = HEAD_ROWS + seq
    t = n_ex * lp
    meta = jnp.broadcast_to(w["meta_tokens"][None], (n_ex, N_META, D))
    h0 = jnp.concatenate([jnp.zeros((n_ex, PAD_ROWS, D), F32), meta, x], axis=1).reshape(t, D)
    tgt = jnp.concatenate([jnp.zeros((n_ex, HEAD_ROWS, D), F32), target], axis=1).reshape(t, D)
    row_mask = jnp.concatenate([jnp.zeros((n_ex, HEAD_ROWS, 1), F32), jnp.ones((n_ex, seq, 1), F32)],
                               axis=1).reshape(t, 1)

    u, hn = _in_proj(h0, w["norm_mix_g"], w["w_in"])
    yc, y_conv = _conv_fwd(u, w["conv_w"], w["conv_b"], w["conv_ln_g"], w["conv_ln_b"], n_ex, lp)
    y_gla, states = _gla_fwd(u, w["gla_w_gate2"], w["gla_gate_b"], w["gla_norm_g"], n_ex, lp)
    h1, hn2, gate, up, act = _mix_out_ffn_up(h0, y_conv, y_gla, w["w_out"], w["norm_ffn_g"],
                                             w["w_ffn_gate_t"], w["w_ffn_up_t"])
    dh2, loss, d_final_g = _ffn_down_loss(act, w["w_ffn_down"], h1, tgt, w["norm_final_g"], row_mask)

    dgate, dup, dh1, dycat, d_ffn_g = _ffn_bwd(dh2, gate, up, h1, w["w_ffn_down"], w["w_ffn_gate_t"],
                                                w["w_ffn_up_t"], w["w_out"], w["norm_ffn_g"])
    du_conv, d_conv_w, d_conv_b, d_ln_g, d_ln_b = _conv_bwd(dycat, yc, u, w["conv_w"], w["conv_ln_g"],
                                                            w["conv_ln_b"], n_ex, lp)
    du_gla, d_w2, d_gate_b, d_norm_g = _gla_bwd(dycat, u, states, w["gla_w_gate2"], w["gla_gate_b"],
                                                w["gla_norm_g"], n_ex, lp)
    dh0, d_mix_g = _in_proj_bwd(du_conv, du_gla, w["w_in"][:, :2 * C_CONV], w["w_in"][:, 2 * C_CONV:],
                                h0, dh1, w["norm_mix_g"])

    d_w_in_t = jnp.concatenate([_wgrad(du_conv, hn, "wgrad_in_conv"), _wgrad(du_gla, hn, "wgrad_in_gla")],
                               axis=0)[:D_IN]
    d_w_out = jnp.concatenate([_wgrad(y_conv, dh1, "wgrad_out_conv"), _wgrad(y_gla, dh1, "wgrad_out_gla")], axis=0)
    dh0 = dh0.reshape(n_ex, lp, D)
    grads = {
        "w_in_t": d_w_in_t, "w_out": d_w_out,
        "w_ffn_gate_t": _wgrad(dgate, hn2, "wgrad_gate"), "w_ffn_up_t": _wgrad(dup, hn2, "wgrad_up"),
        "w_ffn_down": _wgrad(act, dh2, "wgrad_down"),
        "meta_tokens": jnp.sum(dh0[:, PAD_ROWS:HEAD_ROWS], axis=0),
        "conv_w": d_conv_w, "gla_w_gate2": d_w2[:RANK],
        "norm_mix_g": d_mix_g, "norm_ffn_g": d_ffn_g, "norm_final_g": d_final_g,
        "conv_b": d_conv_b, "conv_ln_g": d_ln_g, "conv_ln_b": d_ln_b,
        "gla_gate_b": d_gate_b, "gla_norm_g": d_norm_g,
    }
    return loss[0, 0], dh0[:, HEAD_ROWS:], grads


WEIGHT_NAMES = ("meta_tokens", "norm_mix_g", "w_in", "conv_w", "conv_b", "conv_ln_g", "conv_ln_b", "gla_w_gate2",
                "gla_gate_b", "gla_norm_g", "w_out", "norm_ffn_g", "w_ffn_gate", "w_ffn_up", "w_ffn_down",
                "norm_final_g")
MATMUL_WEIGHTS = ("w_in", "w_out", "w_ffn_gate", "w_ffn_up", "w_ffn_down")
ROW_SHARDED = ("w_out", "w_ffn_down")


def _full_weights(ws):
    sh = lambda name: ws[name].reshape(ws[name].shape[-2:])
    split = [sh("w_in").astype(BF16), sh("w_out").astype(BF16), sh("w_ffn_gate").T.astype(BF16),
             sh("w_ffn_up").T.astype(BF16), sh("w_ffn_down").astype(BF16)]
    whole = [sh("meta_tokens"), sh("conv_w"), sh("gla_w_gate2")]
    w_in, w_out, gate_t, up_t, down, meta, conv_w, w2 = _gather_weights(split, [0, 0, 0, 0, 0], whole)
    cols = lambda a: jnp.concatenate([a[j] for j in range(N_CHIPS)], axis=1)
    full = {name: ws[name].reshape(1, -1) for name, _, _, _ in SMALL_PARTS}
    full["w_in"] = jnp.concatenate([cols(w_in), jnp.zeros((D, D_IN_PAD - D_IN), BF16)], axis=1)
    full["w_out"] = w_out.reshape(D, D)
    full["w_ffn_gate_t"] = gate_t.reshape(D_FF, D)
    full["w_ffn_up_t"] = up_t.reshape(D_FF, D)
    full["w_ffn_down"] = down.reshape(D_FF, D)
    full["meta_tokens"] = cols(meta)
    full["conv_w"] = jnp.concatenate([cols(conv_w), jnp.zeros((32 - CONV_W, C_CONV), F32)], axis=0)
    full["gla_w_gate2"] = jnp.concatenate([cols(w2), jnp.zeros((128 - RANK, GLA_K), F32)], axis=0).astype(BF16)
    return full


SMALL_RS_ROWS = 48


def _pack_small_sharded(grads):
    by_chip = lambda g, w: jnp.transpose(g.reshape(g.shape[0], N_CHIPS, w), (1, 0, 2))
    meta = by_chip(grads["meta_tokens"], D // N_CHIPS)
    conv = by_chip(grads["conv_w"], C_CONV // N_CHIPS).reshape(N_CHIPS, 16, 256)
    w2 = by_chip(grads["gla_w_gate2"], GLA_K // N_CHIPS).reshape(N_CHIPS, 4, 256)
    pad = jnp.zeros((N_CHIPS, SMALL_RS_ROWS - 36, 256), F32)
    return jnp.concatenate([meta, conv, w2, pad], axis=1)


def _unpack_small_sharded(g):
    return {"meta_tokens": g[0:16], "conv_w": g[16:32].reshape(32, C_CONV // N_CHIPS)[:CONV_W],
            "gla_w_gate2": g[32:36].reshape(RANK, GLA_K // N_CHIPS)}


def _kernel_without_overlap(x, meta_tokens, norm_mix_g, w_in, conv_w, conv_b, conv_ln_g, conv_ln_b, gla_w_gate2, gla_gate_b, gla_norm_g, w_out, norm_ffn_g, w_ffn_gate, w_ffn_up, w_ffn_down, norm_final_g, loss_target, m_meta_tokens, m_norm_mix_g, m_w_in, m_conv_w, m_conv_b, m_conv_ln_g, m_conv_ln_b, m_gla_w_gate2, m_gla_gate_b, m_gla_norm_g, m_w_out, m_norm_ffn_g, m_w_ffn_gate, m_w_ffn_up, m_w_ffn_down, m_norm_final_g, v_meta_tokens, v_norm_mix_g, v_w_in, v_conv_w, v_conv_b, v_conv_ln_g, v_conv_ln_b, v_gla_w_gate2, v_gla_gate_b, v_gla_norm_g, v_w_out, v_norm_ffn_g, v_w_ffn_gate, v_w_ffn_up, v_w_ffn_down, v_norm_final_g):
    ws = dict(zip(WEIGHT_NAMES, (meta_tokens, norm_mix_g, w_in, conv_w, conv_b, conv_ln_g, conv_ln_b, gla_w_gate2,
                                 gla_gate_b, gla_norm_g, w_out, norm_ffn_g, w_ffn_gate, w_ffn_up, w_ffn_down,
                                 norm_final_g)))
    ms = dict(zip(WEIGHT_NAMES, (m_meta_tokens, m_norm_mix_g, m_w_in, m_conv_w, m_conv_b, m_conv_ln_g, m_conv_ln_b,
                                 m_gla_w_gate2, m_gla_gate_b, m_gla_norm_g, m_w_out, m_norm_ffn_g, m_w_ffn_gate,
                                 m_w_ffn_up, m_w_ffn_down, m_norm_final_g)))
    vs = dict(zip(WEIGHT_NAMES, (v_meta_tokens, v_norm_mix_g, v_w_in, v_conv_w, v_conv_b, v_conv_ln_g, v_conv_ln_b,
                                 v_gla_w_gate2, v_gla_gate_b, v_gla_norm_g, v_w_out, v_norm_ffn_g, v_w_ffn_gate,
                                 v_w_ffn_up, v_w_ffn_down, v_norm_final_g)))
    c = lax.axis_index("c")

    full = _full_weights(ws)
    loss, grad_x, grads = _local_step(x, loss_target, full)
    loss = lax.psum(loss, ("x", "y", "c"))

    rs_names = ("w_in", "w_out", "w_ffn_gate", "w_ffn_up", "w_ffn_down", "small")
    by_owner = [grads["w_in_t"].reshape(N_CHIPS, D_IN // N_CHIPS, D), grads["w_out"].reshape(N_CHIPS, D // N_CHIPS, D),
                grads["w_ffn_gate_t"].reshape(N_CHIPS, D_FF // N_CHIPS, D),
                grads["w_ffn_up_t"].reshape(N_CHIPS, D_FF // N_CHIPS, D),
                grads["w_ffn_down"].reshape(N_CHIPS, D_FF // N_CHIPS, D), _pack_small_sharded(grads)]
    from_sibling = _rs_to_sibling(by_owner)
    chip_sums = [_rs_add_halves(g, r, c, "rs_add_" + nm) for g, r, nm in zip(by_owner, from_sibling, rs_names)]
    halves = [_rs_sum_chips(p, "rs_sum_" + nm) for p, nm in zip(_rs_chip_exchange(chip_sums), rs_names)]
    reduced = dict(zip(rs_names, _rs_share(halves)))
    g_sharded = {"w_in": reduced["w_in"].T, "w_out": reduced["w_out"], "w_ffn_gate": reduced["w_ffn_gate"].T,
                 "w_ffn_up": reduced["w_ffn_up"].T, "w_ffn_down": reduced["w_ffn_down"],
                 **_unpack_small_sharded(reduced["small"])}
    out = {"grad": {}, "delta": {}, "new_m": {}, "new_v": {}}
    for name, g in g_sharded.items():
        shape = ws[name].shape
        flat = lambda a: a.reshape(shape[-2:])
        delta, new_m, new_v = _adamw(g, flat(ws[name]), flat(ms[name]), flat(vs[name]), "adamw_" + name)
        for kind, a in (("grad", g), ("delta", delta), ("new_m", new_m), ("new_v", new_v)):
            out[kind][name] = a.reshape(shape)

    small_shapes = {name: ws[name].shape for name, _, _, _ in SMALL_PARTS}
    g_s, d_s, m_s, v_s = _allreduce_small_adamw(_pack_small(grads), _pack_small(ws), _pack_small(ms), _pack_small(vs))
    for kind, slab in (("grad", g_s), ("delta", d_s), ("new_m", m_s), ("new_v", v_s)):
        out[kind].update(_unpack_small(slab, small_shapes))

    return (loss, grad_x, *[out[kind][name] for kind in ("grad", "delta", "new_m", "new_v") for name in WEIGHT_NAMES])


def _gather_plan(split, whole=(), axes=None):
    split, whole = list(split), list(whole)
    ns, n = len(split), len(split) + len(whole)

    def make(ins, outs, sems):
        ici_send, ici_recv, d2d_send, d2d_recv, own_send, own_recv = sems
        x, y, c = _mesh_pos()
        mine = 2 * x + y
        chips = _other_chips(x, y)
        blocks = [2 * px + py for px, py in chips]

        def own(a):
            return _remote(ins[a], outs[a].at[mine], own_send.at[a], own_recv.at[a], (x, y, 1 - c))

        def ici(a, k, block):
            px, py = chips[k]
            src, dst = ins[a], outs[a].at[block]
            if a < ns:
                src, dst = _half(src, c, axes[a]), _half(dst, c, axes[a])
            return _remote(src, dst, ici_send.at[3 * a + k], ici_recv.at[3 * a + k], (px, py, c))

        def d2d(a, k, half):
            part = _half(outs[a].at[blocks[k]], half, axes[a])
            return _remote(part, part, d2d_send.at[3 * a + k], d2d_recv.at[3 * a + k], (x, y, 1 - c))

        def start():
            for a in range(n):
                for k in range(3):
                    ici(a, k, mine).start()
                own(a).start()

        def finish():
            for a in range(n):
                for k in range(3):
                    ici(a, k, blocks[k]).wait_recv()
                    if a < ns:
                        d2d(a, k, c).start()
            for a in range(ns):
                for k in range(3):
                    d2d(a, k, 1 - c).wait_recv()
            for a in range(n):
                for k in range(3):
                    ici(a, k, mine).wait_send()
                    if a < ns:
                        d2d(a, k, c).wait_send()
                own(a).wait()

        return start, finish

    arrays = split + whole
    axes = [0] * ns if axes is None else list(axes)
    return _Plan(arrays, [jax.ShapeDtypeStruct((N_CHIPS,) + s.shape, s.dtype) for s in arrays],
                 [pltpu.SemaphoreType.DMA((3 * n,)), pltpu.SemaphoreType.DMA((3 * n,)),
                  pltpu.SemaphoreType.DMA((3 * ns,)), pltpu.SemaphoreType.DMA((3 * ns,)),
                  pltpu.SemaphoreType.DMA((n,)), pltpu.SemaphoreType.DMA((n,))], make)


def _to_sibling_plan(gs):
    n = len(gs)

    def make(ins, outs, sems):
        send_sems, recv_sems = sems
        x, y, c = _mesh_pos()

        def copy(a):
            return _remote(_half(ins[a], 1 - c, 2), outs[a], send_sems.at[a], recv_sems.at[a], (x, y, 1 - c))

        def start():
            for a in range(n):
                copy(a).start()

        def finish():
            for a in range(n):
                copy(a).wait()

        return start, finish

    return _Plan(list(gs), [jax.ShapeDtypeStruct(g.shape[:2] + (g.shape[2] // 2,), g.dtype) for g in gs],
                 [pltpu.SemaphoreType.DMA((n,)), pltpu.SemaphoreType.DMA((n,))], make)


def _chip_exchange_plan(ps):
    n = len(ps)

    def make(ins, outs, sems):
        send_sems, recv_sems = sems
        x, y, c = _mesh_pos()
        chips = _other_chips(x, y)

        def ici(a, k):
            px, py = chips[k]
            return _remote(ins[a].at[2 * px + py], outs[a].at[k], send_sems.at[3 * a + k],
                           recv_sems.at[3 * a + k], (px, py, c))

        def start():
            for a in range(n):
                for k in range(3):
                    ici(a, k).start()

        def finish():
            for a in range(n):
                for k in range(3):
                    ici(a, k).wait()

        return start, finish

    return _Plan(list(ps), [jax.ShapeDtypeStruct((3,) + p.shape[1:], p.dtype) for p in ps],
                 [pltpu.SemaphoreType.DMA((3 * n,)), pltpu.SemaphoreType.DMA((3 * n,))], make)


def _share_plan(halves):
    n = len(halves)

    def make(ins, outs, sems):
        send_sems, recv_sems = sems
        x, y, c = _mesh_pos()

        def d2d(a):
            return _remote(ins[a], outs[a], send_sems.at[a], recv_sems.at[a], (x, y, 1 - c))

        def start():
            for a in range(n):
                d2d(a).start()

        def finish():
            for a in range(n):
                d2d(a).wait()

        return start, finish

    return _Plan(list(halves), [jax.ShapeDtypeStruct(p.shape, p.dtype) for p in halves],
                 [pltpu.SemaphoreType.DMA((n,)), pltpu.SemaphoreType.DMA((n,))], make)


def _rs_sum(own, others, mine, name):
    _, rows, h = own.shape
    tr = rows // 2 if rows % 16 == 0 and rows > 64 else rows

    def body(mine_ref, own_ref, oth_ref, o_ref):
        p = oth_ref[...].astype(F32)
        o_ref[...] = ((own_ref[0].astype(F32) + p[0]) + p[1]) + p[2]

    return pl.pallas_call(
        body, name=name,
        grid_spec=pltpu.PrefetchScalarGridSpec(
            num_scalar_prefetch=1, grid=(rows // tr,),
            in_specs=[pl.BlockSpec((1, tr, h), lambda i, s: (s[0], i, 0)),
                      pl.BlockSpec((3, tr, h), lambda i, s: (0, i, 0))],
            out_specs=pl.BlockSpec((tr, h), lambda i, s: (i, 0))),
        out_shape=jax.ShapeDtypeStruct((rows, h), F32),
        compiler_params=_params(1),
    )(jnp.reshape(mine, (1,)).astype(jnp.int32), own, others)


def _join(mine, theirs, c):
    return jnp.where(c == 0, jnp.concatenate([mine, theirs], axis=1), jnp.concatenate([theirs, mine], axis=1))


LOSS_ROW = 5


def _merge_plans(a, b):
    na_in, na_out, na_sems = len(a.arrays), len(a.out_shape), len(a.sems)

    def make(ins, outs, sems):
        start_a, finish_a = a.make(ins[:na_in], outs[:na_out], sems[:na_sems])
        start_b, finish_b = b.make(ins[na_in:], outs[na_out:], sems[na_sems:])

        def start():
            start_a()
            start_b()

        def finish():
            finish_a()
            finish_b()

        return start, finish

    return _Plan(list(a.arrays) + list(b.arrays), list(a.out_shape) + list(b.out_shape),
                 list(a.sems) + list(b.sems), make)


def _exchange(plan, name):
    n_in, n_out = len(plan.arrays), len(plan.out_shape)

    def body(*refs):
        start, finish = plan.make(refs[:n_in], refs[n_in:n_in + n_out], refs[n_in + n_out:])
        start()
        finish()

    return pl.pallas_call(
        body, name=name, in_specs=[HBM_SPEC] * n_in, out_specs=[HBM_SPEC] * n_out, out_shape=list(plan.out_shape),
        scratch_shapes=list(plan.sems), compiler_params=pltpu.CompilerParams(has_side_effects=True),
    )(*plan.arrays)


def _adamw_halves(mine, theirs, c, w, m, v, name):
    rows, h = mine.shape
    tr = rows // 2 if rows % 16 == 0 else rows

    def body(c_ref, a_ref, b_ref, w_ref, m_ref, v_ref, go_ref, d_ref, m2_ref, v2_ref):
        g = jnp.where(pl.program_id(1) == c_ref[0], a_ref[...], b_ref[...])
        go_ref[...] = g
        d_ref[...], m2_ref[...], v2_ref[...] = _adam_update(g, w_ref[...], m_ref[...], v_ref[...])

    half = pl.BlockSpec((tr, h), lambda i, j, s: (i, 0))
    spec = pl.BlockSpec((tr, h), lambda i, j, s: (i, j))
    return pl.pallas_call(
        body, name=name,
        grid_spec=pltpu.PrefetchScalarGridSpec(num_scalar_prefetch=1, grid=(rows // tr, 2),
                                               in_specs=[half, half, spec, spec, spec], out_specs=[spec] * 4),
        out_shape=[jax.ShapeDtypeStruct((rows, 2 * h), F32)] * 4,
        compiler_params=_params(2),
    )(jnp.reshape(c, (1,)).astype(jnp.int32), mine, theirs, w, m, v)


def _columns(gathered):
    return jnp.concatenate([gathered[j] for j in range(N_CHIPS)], axis=1)


def kernel(x, meta_tokens, norm_mix_g, w_in, conv_w, conv_b, conv_ln_g, conv_ln_b, gla_w_gate2, gla_gate_b, gla_norm_g, w_out, norm_ffn_g, w_ffn_gate, w_ffn_up, w_ffn_down, norm_final_g, loss_target, m_meta_tokens, m_norm_mix_g, m_w_in, m_conv_w, m_conv_b, m_conv_ln_g, m_conv_ln_b, m_gla_w_gate2, m_gla_gate_b, m_gla_norm_g, m_w_out, m_norm_ffn_g, m_w_ffn_gate, m_w_ffn_up, m_w_ffn_down, m_norm_final_g, v_meta_tokens, v_norm_mix_g, v_w_in, v_conv_w, v_conv_b, v_conv_ln_g, v_conv_ln_b, v_gla_w_gate2, v_gla_gate_b, v_gla_norm_g, v_w_out, v_norm_ffn_g, v_w_ffn_gate, v_w_ffn_up, v_w_ffn_down, v_norm_final_g):
    ws = dict(zip(WEIGHT_NAMES, (meta_tokens, norm_mix_g, w_in, conv_w, conv_b, conv_ln_g, conv_ln_b, gla_w_gate2,
                                 gla_gate_b, gla_norm_g, w_out, norm_ffn_g, w_ffn_gate, w_ffn_up, w_ffn_down,
                                 norm_final_g)))
    ms = dict(zip(WEIGHT_NAMES, (m_meta_tokens, m_norm_mix_g, m_w_in, m_conv_w, m_conv_b, m_conv_ln_g, m_conv_ln_b,
                                 m_gla_w_gate2, m_gla_gate_b, m_gla_norm_g, m_w_out, m_norm_ffn_g, m_w_ffn_gate,
                                 m_w_ffn_up, m_w_ffn_down, m_norm_final_g)))
    vs = dict(zip(WEIGHT_NAMES, (v_meta_tokens, v_norm_mix_g, v_w_in, v_conv_w, v_conv_b, v_conv_ln_g, v_conv_ln_b,
                                 v_gla_w_gate2, v_gla_gate_b, v_gla_norm_g, v_w_out, v_norm_ffn_g, v_w_ffn_gate,
                                 v_w_ffn_up, v_w_ffn_down, v_norm_final_g)))
    c = lax.axis_index("c")
    shard = lambda d, name: d[name].reshape(d[name].shape[-2:])
    vec = {name: ws[name].reshape(1, -1) for name, _, _, _ in SMALL_PARTS}
    n_ex, seq, _ = x.shape
    lp = HEAD_ROWS + seq
    t = n_ex * lp

    (tgt,), (w_in_g, meta_g, conv_w_g, w2_g) = _pad_head_rows(loss_target, plan=_gather_plan(
        [shard(ws, "w_in").T.astype(BF16)],
        [shard(ws, "meta_tokens"), shard(ws, "conv_w"), shard(ws, "gla_w_gate2")], axes=[1]))
    w_in_t = jnp.concatenate([w_in_g.reshape(D_IN, D), jnp.zeros((D_IN_PAD - D_IN, D), BF16)], axis=0)
    w_in_full = w_in_t.T
    conv_w_full = jnp.concatenate([_columns(conv_w_g), jnp.zeros((32 - CONV_W, C_CONV), F32)], axis=0)
    w2_full = jnp.concatenate([_columns(w2_g), jnp.zeros((128 - RANK, GLA_K), F32)], axis=0).astype(BF16)

    meta = jnp.broadcast_to(_columns(meta_g)[None], (n_ex, N_META, D))
    h0 = jnp.concatenate([jnp.zeros((n_ex, PAD_ROWS, D), F32), meta, x], axis=1).reshape(t, D)
    tgt = tgt.reshape(t, D)
    row_mask = jnp.concatenate([jnp.zeros((n_ex, HEAD_ROWS, 1), F32), jnp.ones((n_ex, seq, 1), F32)],
                               axis=1).reshape(t, 1)

    (u, hn), (w_out_g,) = _in_proj(h0, vec["norm_mix_g"], w_in_full,
                                   plan=_gather_plan([shard(ws, "w_out").astype(BF16)]))
    (yc, y_conv), (gate_g,) = _conv_fwd(
        u, conv_w_full, vec["conv_b"], vec["conv_ln_g"], vec["conv_ln_b"], n_ex, lp,
        plan=_gather_plan([shard(ws, "w_ffn_gate").T.astype(BF16)]))
    (y_gla, states), (up_g,) = _gla_fwd(u, w2_full, vec["gla_gate_b"], vec["gla_norm_g"], n_ex, lp,
                                        plan=_gather_plan([shard(ws, "w_ffn_up").T.astype(BF16)]))
    w_out_full = w_out_g.reshape(D, D)
    w_gate_t, w_up_t = gate_g.reshape(D_FF, D), up_g.reshape(D_FF, D)

    (h1, hn2, gate, up, act), (down_g,) = _mix_out_ffn_up(
        h0, y_conv, y_gla, w_out_full, vec["norm_ffn_g"], w_gate_t.T, w_up_t.T,
        plan=_gather_plan([shard(ws, "w_ffn_down").astype(BF16)]))
    w_down_full = down_g.reshape(D_FF, D)
    dh2, loss, d_final_g = _ffn_down_loss(act, w_down_full, h1, tgt, vec["norm_final_g"], row_mask)
    dgate, dup, dh1, dycat, d_ffn_g = _ffn_bwd(dh2, gate, up, h1, w_down_full.T, w_gate_t, w_up_t, w_out_full.T,
                                                vec["norm_ffn_g"])

    early = ("w_out", "w_ffn_gate", "w_ffn_up", "w_ffn_down")
    d_w_out = jnp.concatenate([_wgrad(y_conv, dh1, "wgrad_out_conv"), _wgrad(y_gla, dh1, "wgrad_out_gla")], axis=0)
    by_owner = [d_w_out.reshape(N_CHIPS, D // N_CHIPS, D),
                _wgrad(dgate, hn2, "wgrad_gate").reshape(N_CHIPS, D_FF // N_CHIPS, D),
                _wgrad(dup, hn2, "wgrad_up").reshape(N_CHIPS, D_FF // N_CHIPS, D),
                _wgrad(act, dh2, "wgrad_down").reshape(N_CHIPS, D_FF // N_CHIPS, D)]
    (du_conv, d_conv_w, d_conv_b, d_ln_g, d_ln_b), from_sibling = _conv_bwd(
        dycat, yc, u, conv_w_full, vec["conv_ln_g"], vec["conv_ln_b"], n_ex, lp, plan=_to_sibling_plan(by_owner))
    chip_sums = [_rs_add_halves(g, r, c, "rs_add_" + nm) for g, r, nm in zip(by_owner, from_sibling, early)]
    (du_gla, d_w2, d_gate_b, d_norm_g), exchanged = _gla_bwd(
        dycat, u, states, w2_full, vec["gla_gate_b"], vec["gla_norm_g"], n_ex, lp,
        plan=_chip_exchange_plan(chip_sums))
    mine = 2 * lax.axis_index("x") + lax.axis_index("y")
    halves = [_rs_sum(own, oth, mine, "rs_sum_" + nm) for own, oth, nm in zip(chip_sums, exchanged, early)]

    d_w_in_t = jnp.concatenate([_wgrad(du_conv, hn, "wgrad_in_conv"), _wgrad(du_gla, hn, "wgrad_in_gla")],
                               axis=0)[:D_IN].reshape(N_CHIPS, D_IN // N_CHIPS, D)
    (in_from_sibling,) = _exchange(_to_sibling_plan([d_w_in_t]), "rs_late_to_sibling")
    in_chip_sum = _rs_add_halves(d_w_in_t, in_from_sibling, c, "rs_add_w_in")
    (dh0, d_mix_g), shared = _in_proj_bwd(
        du_conv, du_gla, w_in_t[:2 * C_CONV], w_in_t[2 * C_CONV:], h0, dh1, vec["norm_mix_g"],
        plan=_merge_plans(_share_plan(halves), _chip_exchange_plan([in_chip_sum])))
    dh0 = dh0.reshape(n_ex, lp, D)
    grad_x = dh0[:, HEAD_ROWS:]

    out = {"grad": {}, "delta": {}, "new_m": {}, "new_v": {}}

    def update(name, g=None, halves=None, transposed=False):
        shape = ws[name].shape
        lay = (lambda a: a.T) if transposed else (lambda a: a)
        w2d, m2d, v2d = lay(shard(ws, name)), lay(shard(ms, name)), lay(shard(vs, name))
        if halves is not None:
            res = _adamw_halves(*halves, c, w2d, m2d, v2d, "adamw_" + name)
        else:
            res = [g, *_adamw(g, w2d, m2d, v2d, "adamw_" + name)]
        for kind, a in zip(("grad", "delta", "new_m", "new_v"), res):
            out[kind][name] = lay(a).reshape(shape)

    update("w_out", halves=(halves[0], shared[0]))
    update("w_ffn_gate", halves=(halves[1], shared[1]), transposed=True)
    update("w_ffn_up", halves=(halves[2], shared[2]), transposed=True)
    update("w_ffn_down", halves=(halves[3], shared[3]))

    in_half = _rs_sum(in_chip_sum, shared[4], mine, "rs_sum_w_in")
    (in_shared,) = _exchange(_share_plan([in_half]), "rs_late_share")
    update("w_in", halves=(in_half, in_shared), transposed=True)

    small = {"norm_mix_g": d_mix_g, "norm_ffn_g": d_ffn_g, "norm_final_g": d_final_g, "conv_b": d_conv_b,
             "conv_ln_g": d_ln_g, "conv_ln_b": d_ln_b, "gla_gate_b": d_gate_b, "gla_norm_g": d_norm_g}
    small_shapes = {name: ws[name].shape for name, _, _, _ in SMALL_PARTS}
    part = lax.dynamic_update_slice(_pack_small(small), loss[:, :1], (LOSS_ROW, 0))
    part = jnp.concatenate([part, jnp.sum(dh0[:, PAD_ROWS:HEAD_ROWS], axis=0), d_conv_w.reshape(16, D),
                            d_w2[:RANK].reshape(4, D), jnp.zeros((4, D), F32)], axis=0)
    tall = lambda a: jnp.concatenate([a, jnp.zeros((part.shape[0] - SMALL_ROWS, D), F32)], axis=0)
    g_s, d_s, m_s, v_s = _allreduce_small_adamw(part, tall(_pack_small(ws)), tall(_pack_small(ms)),
                                                tall(_pack_small(vs)))
    for kind, slab in (("grad", g_s), ("delta", d_s), ("new_m", m_s), ("new_v", v_s)):
        out[kind].update(_unpack_small(slab, small_shapes))
    loss = g_s[LOSS_ROW, 0]
    block = lambda a, width: lax.dynamic_slice_in_dim(a, mine * width, width, axis=1)
    update("meta_tokens", g=block(g_s[8:24], D // N_CHIPS))
    update("conv_w", g=block(g_s[24:40].reshape(32, C_CONV), C_CONV // N_CHIPS)[:CONV_W])
    update("gla_w_gate2", g=block(g_s[40:44].reshape(RANK, GLA_K), GLA_K // N_CHIPS))

    return (loss, grad_x, *[out[kind][name] for kind in ("grad", "delta", "new_m", "new_v") for name in WEIGHT_NAMES])
```

```python
import functools
from typing import Any, Callable, NamedTuple, Sequence

import jax
import jax.numpy as jnp
from jax import lax
from jax.experimental import pallas as pl
from jax.experimental.pallas import tpu as pltpu

F32 = jnp.float32
BF16 = jnp.bfloat16
MESH = pl.DeviceIdType.MESH

D = 1024
N_META = 16
C_CONV = 512
CONV_W = 31
GLA_K = 256
GLA_V = 512
N_HEADS = 4
DK = 64
DV = 128
RANK = 16
CHUNK = 64
PAD_ROWS = CHUNK - N_META
HEAD_ROWS = CHUNK
D_IN = 2576
D_IN_PAD = 2688
D_GLA_IN = D_IN_PAD - 2 * C_CONV
D_FF = 2816
RMS_EPS = 1e-6
LN_EPS = 1e-5
GATE_TAU = 16.0
N_CHIPS = 4

ADAM_LR = 0.001
ADAM_B1 = 0.9
ADAM_B2 = 0.999
ADAM_EPS = 1e-08
ADAM_WD = 0.01
ADAM_STEP = 10

V7X_VMEM_BYTES = 64 * 1024 * 1024
VMEM_LIMIT = V7X_VMEM_BYTES - 8 * 1024 * 1024

SLAB_ROWS = 3072
HALF_ROWS = SLAB_ROWS // 2
SMALL_ROWS = 8


def _dot(a, b):
    return jnp.dot(a, b, preferred_element_type=F32)


def _dot_nt(a, b):
    return lax.dot_general(a, b, (((1,), (1,)), ((), ())), preferred_element_type=F32)


def _dot_tn(a, b):
    return lax.dot_general(a, b, (((0,), (0,)), ((), ())), preferred_element_type=F32)


def _sigmoid(x):
    return 1.0 / (1.0 + jnp.exp(-x))


def _const_spec(shape):
    return pl.BlockSpec(shape, lambda *_: (0,) * len(shape), pipeline_mode=pl.Buffered(1))


def _acc_spec(shape):
    return pl.BlockSpec(shape, lambda *_: (0,) * len(shape))


def _params(n_axes):
    return pltpu.CompilerParams(dimension_semantics=("arbitrary",) * n_axes, vmem_limit_bytes=VMEM_LIMIT)


def _row_tile(t, want):
    for r in (want, 384, 192, 128, 64):
        if r <= want and t % r == 0:
            return r
    raise ValueError(f"no row tile for {t}")


ROW_PART = 128


def _row_parts(r):
    if r % ROW_PART:
        return [slice(None)]
    return [pl.ds(i * ROW_PART, ROW_PART) for i in range(r // ROW_PART)]


def _in_lockstep(bodies):
    live = list(bodies)
    while live:
        still = []
        for g in live:
            try:
                next(g)
                still.append(g)
            except StopIteration:
                pass
        live = still


class _Plan(NamedTuple):
    arrays: Sequence[Any]
    out_shape: Sequence[Any]
    sems: Sequence[Any]
    make: Callable


def _call(body, *, name, grid, in_specs, out_specs, out_shape, scratch_shapes=(), plan=None):
    n_in, n_out, n_scr = len(in_specs), len(out_specs), len(scratch_shapes)
    if plan is None:
        plan = _Plan([], [], [], lambda ins, outs, sems: (lambda: None, lambda: None))
    nx_in, nx_out = len(plan.arrays), len(plan.out_shape)

    def hosted(*refs):
        ins, xins = refs[:n_in], refs[n_in:n_in + nx_in]
        o0 = n_in + nx_in
        outs, xouts = refs[o0:o0 + n_out], refs[o0 + n_out:o0 + n_out + nx_out]
        s0 = o0 + n_out + nx_out
        scr, sems = refs[s0:s0 + n_scr], refs[s0 + n_scr:]
        ids = [pl.program_id(a) for a in range(len(grid))]
        first = functools.reduce(jnp.logical_and, [i == 0 for i in ids])
        last = functools.reduce(jnp.logical_and, [i == g - 1 for i, g in zip(ids, grid)])
        start, finish = plan.make(xins, xouts, sems)
        pl.when(first)(start)
        body(*ins, *outs, *scr)
        pl.when(last)(finish)

    call = pl.pallas_call(
        hosted, name=name, grid=grid, in_specs=list(in_specs) + [HBM_SPEC] * nx_in,
        out_specs=list(out_specs) + [HBM_SPEC] * nx_out, out_shape=list(out_shape) + list(plan.out_shape),
        scratch_shapes=list(scratch_shapes) + list(plan.sems),
        compiler_params=pltpu.CompilerParams(dimension_semantics=("arbitrary",) * len(grid),
                                             vmem_limit_bytes=VMEM_LIMIT, has_side_effects=nx_in > 0))

    def run(*args):
        res = call(*args, *plan.arrays)
        return res[:n_out], res[n_out:]

    return run


def _pad_head_rows(a, plan=None):
    n_ex, seq, _ = a.shape
    nc = (HEAD_ROWS + seq) // CHUNK

    def body(a_ref, o_ref):
        o_ref[...] = jnp.where(pl.program_id(0) > 0, a_ref[...], 0.0)

    return _call(
        body, name="pad_head_rows", grid=(nc,),
        in_specs=[pl.BlockSpec((n_ex, CHUNK, D), lambda n: (0, jnp.maximum(n - 1, 0), 0))],
        out_specs=[pl.BlockSpec((n_ex, CHUNK, D), lambda n: (0, n, 0))],
        out_shape=[jax.ShapeDtypeStruct((n_ex, HEAD_ROWS + seq, D), F32)],
        plan=plan,
    )(a)


def _in_proj(h0, g_mix, w_in, plan=None):
    t = h0.shape[0]
    r = _row_tile(t, 384)

    def body(h_ref, g_ref, w_ref, u_ref, hn_ref):
        h = h_ref[...]
        rstd = lax.rsqrt(jnp.mean(h * h, axis=-1, keepdims=True) + RMS_EPS)
        hn = (h * rstd * g_ref[...]).astype(BF16)
        hn_ref[...] = hn
        u_ref[...] = _dot(hn, w_ref[...])

    return _call(
        body, name="in_proj", grid=(t // r,),
        in_specs=[pl.BlockSpec((r, D), lambda i: (i, 0)), _const_spec((1, D)), _const_spec((D, D_IN_PAD))],
        out_specs=[pl.BlockSpec((r, D_IN_PAD), lambda i: (i, 0)), pl.BlockSpec((r, D), lambda i: (i, 0))],
        out_shape=[jax.ShapeDtypeStruct((t, D_IN_PAD), F32), jax.ShapeDtypeStruct((t, D), BF16)],
        plan=plan,
    )(h0, g_mix, w_in)


CONV_TILE = 192
CONV_SUB = 32
CONV_LEAD = CONV_SUB - (CONV_W - 1)
SUBLANES = 8


def _shifted_copies(src, dst, r):
    for s in range(1, SUBLANES):
        dst[s - 1] = src[s:s + r + CONV_SUB - SUBLANES, :]


def _shifted_rows(src, shifted, start):
    base, s = SUBLANES * (start // SUBLANES), start % SUBLANES
    if s == 0:
        return src[base:base + CONV_SUB, :]
    return shifted[s - 1, base:base + CONV_SUB, :]


def _conv_fwd(u, conv_w, conv_b, ln_g, ln_b, n_ex, lp, plan=None):
    r = CONV_TILE
    nt = lp // r
    hb = r // CONV_SUB

    def body(cur_ref, prev_ref, w_ref, b_ref, lg_ref, lb_ref, yc_ref, y_ref, glu, glu_sh):
        i = pl.program_id(1)
        cur = cur_ref[...]
        glu[CONV_SUB:CONV_SUB + r, :] = cur[:, :C_CONV] * _sigmoid(cur[:, C_CONV:])
        pv = prev_ref[...]
        halo = pv[:, :C_CONV] * _sigmoid(pv[:, C_CONV:])
        glu[0:CONV_SUB, :] = jnp.where(i > 0, halo, 0.0)
        _shifted_copies(glu, glu_sh, r)
        w = w_ref[...]
        for j in range(r // CONV_SUB):
            r0 = j * CONV_SUB
            acc = jnp.zeros((CONV_SUB, C_CONV), F32) + b_ref[...]
            for k in range(CONV_W):
                acc = acc + w[k:k + 1, :] * _shifted_rows(glu, glu_sh, r0 + CONV_LEAD + k)
            mu = jnp.mean(acc, axis=-1, keepdims=True)
            cen = acc - mu
            var = jnp.mean(cen * cen, axis=-1, keepdims=True)
            out = cen * lax.rsqrt(var + LN_EPS) * lg_ref[...] + lb_ref[...]
            y = out * _sigmoid(out)
            row = i * r + r0 + lax.broadcasted_iota(jnp.int32, (CONV_SUB, 1), 0)
            y = jnp.where(row >= PAD_ROWS, y, 0.0)
            yc_ref[r0:r0 + CONV_SUB, :] = acc
            y_ref[r0:r0 + CONV_SUB, :] = y.astype(BF16)

    t = n_ex * lp
    return _call(
        body, name="conv_fwd", grid=(n_ex, nt),
        in_specs=[pl.BlockSpec((r, 2 * C_CONV), lambda b, i: (b * nt + i, 0)),
                  pl.BlockSpec((CONV_SUB, 2 * C_CONV), lambda b, i: (jnp.maximum((b * nt + i) * hb - 1, 0), 0)),
                  _const_spec((32, C_CONV)), _const_spec((1, C_CONV)), _const_spec((1, C_CONV)), _const_spec((1, C_CONV))],
        out_specs=[pl.BlockSpec((r, C_CONV), lambda b, i: (b * nt + i, 0)),
                   pl.BlockSpec((r, C_CONV), lambda b, i: (b * nt + i, 0))],
        out_shape=[jax.ShapeDtypeStruct((t, C_CONV), F32), jax.ShapeDtypeStruct((t, C_CONV), BF16)],
        scratch_shapes=[pltpu.VMEM((r + CONV_SUB, C_CONV), F32),
                        pltpu.VMEM((SUBLANES - 1, r + CONV_SUB - SUBLANES, C_CONV), F32)],
        plan=plan,
    )(u, u, conv_w, conv_b, ln_g, ln_b)


def _gla_gates(lr, w2, gb, first_chunk):
    z = _dot(lr.astype(BF16), w2) + gb
    a = (jnp.minimum(z, 0.0) - jnp.log(1.0 + jnp.exp(-jnp.abs(z)))) * (1.0 / GATE_TAU)
    row = lax.broadcasted_iota(jnp.int32, (CHUNK, 1), 0)
    live = jnp.logical_or(jnp.logical_not(first_chunk), row >= PAD_ROWS)
    return z, jnp.where(live, a, 0.0), live


def _tri(lower):
    i = lax.broadcasted_iota(jnp.int32, (CHUNK, CHUNK), 0)
    j = lax.broadcasted_iota(jnp.int32, (CHUNK, CHUNK), 1)
    return (i >= j) if lower else (i <= j)


def _gla_fwd_per_head(u, w2, gb, ng, n_ex, lp, plan=None):
    nc = lp // CHUNK
    t = n_ex * lp

    def body(qk_ref, v_ref, g_ref, lr_ref, w2_ref, gb_ref, ng_ref, y_ref, st_ref, state):
        n = pl.program_id(0)

        @pl.when(n == 0)
        def _():
            state[...] = jnp.zeros_like(state)

        causal = _tri(True)
        for e in range(n_ex):
            st = state[e]
            st_ref[e] = st
            qk = qk_ref[e]
            q, k = qk[:, :GLA_K], qk[:, GLA_K:]
            _, a, _ = _gla_gates(lr_ref[e], w2_ref[...], gb_ref[...], n == 0)
            b = jnp.dot(causal.astype(F32), a, preferred_element_type=F32, precision=lax.Precision.HIGHEST)
            bl = b[CHUNK - 1:CHUNK, :]
            q_in = (q * (DK ** -0.5) * jnp.exp(b)).astype(BF16)
            k_in = (k * jnp.exp(-b)).astype(BF16)
            k_dec = (k * jnp.exp(bl - b)).astype(BF16)
            decay = jnp.exp(bl)
            v = v_ref[e]
            g = g_ref[e]
            st_b = st.astype(BF16)
            ys, new = [], []
            for h in range(N_HEADS):
                ks = slice(h * DK, (h + 1) * DK)
                vs = slice(h * DV, (h + 1) * DV)
                vh = v[:, vs].astype(BF16)
                s = jnp.where(causal, _dot_nt(q_in[:, ks], k_in[:, ks]), 0.0)
                o = _dot(s.astype(BF16), vh) + _dot_nt(q_in[:, ks], st_b[:, ks])
                new.append(decay[:, ks] * st[:, ks] + _dot_tn(vh, k_dec[:, ks]))
                rstd = lax.rsqrt(jnp.mean(o * o, axis=-1, keepdims=True) + RMS_EPS)
                gh = g[:, vs]
                ys.append(o * rstd * ng_ref[...] * (gh * _sigmoid(gh)))
            state[e] = jnp.concatenate(new, axis=1)
            y_ref[e] = jnp.concatenate(ys, axis=1).astype(BF16)

    u3 = u.reshape(n_ex, lp, D_IN_PAD)
    blk = lambda w, col: pl.BlockSpec((n_ex, CHUNK, w), lambda n: (0, n, col))
    (y, states), extra = _call(
        body, name="gla_fwd", grid=(nc,),
        in_specs=[blk(2 * GLA_K, 2), blk(GLA_V, 3), blk(GLA_V, 4), blk(128, 20),
                  _const_spec((128, GLA_K)), _const_spec((1, GLA_K)), _const_spec((1, DV))],
        out_specs=[blk(GLA_V, 0), pl.BlockSpec((n_ex, DV, GLA_K), lambda n: (0, n, 0))],
        out_shape=[jax.ShapeDtypeStruct((n_ex, lp, GLA_V), BF16),
                   jax.ShapeDtypeStruct((n_ex, nc * DV, GLA_K), F32)],
        scratch_shapes=[pltpu.VMEM((n_ex, DV, GLA_K), F32)],
        plan=plan,
    )(u3, u3, u3, u3, w2, gb, ng)
    return (y.reshape(t, GLA_V), states), extra


FFN_TILE = 192


def _mix_out_ffn_up(h0, y_conv, y_gla, w_out, g_ffn, w_gate, w_up, plan=None):
    t = h0.shape[0]
    r = _row_tile(t, 384)

    def body(h0_ref, yc_ref, yg_ref, wo_ref, g_ref, wg_ref, wu_ref, h1_ref, hn_ref, gate_ref, up_ref, act_ref):
        h1 = h0_ref[...] + _dot(yc_ref[...], wo_ref[0:C_CONV, :]) + _dot(yg_ref[...], wo_ref[C_CONV:D, :])
        h1_ref[...] = h1
        rstd = lax.rsqrt(jnp.mean(h1 * h1, axis=-1, keepdims=True) + RMS_EPS)
        hn = (h1 * rstd * g_ref[...]).astype(BF16)
        hn_ref[...] = hn
        gate = _dot(hn, wg_ref[...])
        up = _dot(hn, wu_ref[...])
        gate_ref[...] = gate
        up_ref[...] = up
        act_ref[...] = (gate * _sigmoid(gate) * up).astype(BF16)

    rows = lambda w: pl.BlockSpec((r, w), lambda i: (i, 0))
    return _call(
        body, name="mix_out_ffn_up", grid=(t // r,),
        in_specs=[rows(D), rows(C_CONV), rows(GLA_V), _const_spec((D, D)), _const_spec((1, D)),
                  _const_spec((D, D_FF)), _const_spec((D, D_FF))],
        out_specs=[rows(D), rows(D), rows(D_FF), rows(D_FF), rows(D_FF)],
        out_shape=[jax.ShapeDtypeStruct((t, D), F32), jax.ShapeDtypeStruct((t, D), BF16),
                   jax.ShapeDtypeStruct((t, D_FF), F32), jax.ShapeDtypeStruct((t, D_FF), F32),
                   jax.ShapeDtypeStruct((t, D_FF), BF16)],
        plan=plan,
    )(h0, y_conv, y_gla, w_out, g_ffn, w_gate, w_up)


def _ffn_down_loss(act, w_down, h1, target, g_final, row_mask):
    t = h1.shape[0]
    r = _row_tile(t, 384)

    def body(act_ref, wd_ref, h1_ref, tgt_ref, gf_ref, mask_ref, dh2_ref, loss_ref, dgf_ref):
        @pl.when(pl.program_id(0) == 0)
        def _():
            loss_ref[...] = jnp.zeros_like(loss_ref)
            dgf_ref[...] = jnp.zeros_like(dgf_ref)

        gf = gf_ref[...]

        def part(rows):
            h2 = h1_ref[rows, :] + _dot(act_ref[rows, :], wd_ref[...])
            yield
            rstd = lax.rsqrt(jnp.mean(h2 * h2, axis=-1, keepdims=True) + RMS_EPS)
            nrm = h2 * rstd
            err = (nrm * gf - tgt_ref[rows, :]) * mask_ref[rows, :]
            loss_ref[...] += jnp.sum(err * err) * (0.5 / D)
            dy = err * (1.0 / D)
            dgf_ref[...] += jnp.sum(dy * nrm, axis=0, keepdims=True)
            dn = dy * gf
            dh2_ref[rows, :] = rstd * (dn - nrm * jnp.mean(dn * nrm, axis=-1, keepdims=True))

        _in_lockstep(part(rows) for rows in _row_parts(r))

    rows = lambda w: pl.BlockSpec((r, w), lambda i: (i, 0))
    return pl.pallas_call(
        body, name="ffn_down_loss", grid=(t // r,),
        in_specs=[rows(D_FF), _const_spec((D_FF, D)), rows(D), rows(D), _const_spec((1, D)), rows(1)],
        out_specs=[rows(D), _acc_spec((1, 128)), _acc_spec((1, D))],
        out_shape=[jax.ShapeDtypeStruct((t, D), F32), jax.ShapeDtypeStruct((1, 128), F32),
                   jax.ShapeDtypeStruct((1, D), F32)],
        compiler_params=_params(1),
    )(act, w_down, h1, target, g_final, row_mask)


def _ffn_bwd(dh2, gate, up, h1, w_down_t, w_gate_t, w_up_t, w_out_t, g_ffn):
    t = h1.shape[0]
    r = _row_tile(t, FFN_TILE)

    def body(dh2_ref, gate_ref, up_ref, h1_ref, wd_ref, wg_ref, wu_ref, wo_ref, g_ref,
             dgate_ref, dup_ref, dh1_ref, dycat_ref, dg_ref):
        @pl.when(pl.program_id(0) == 0)
        def _():
            dg_ref[...] = jnp.zeros_like(dg_ref)

        dh2 = dh2_ref[...]
        dact = _dot(dh2.astype(BF16), wd_ref[...])
        gate = gate_ref[...]
        sg = _sigmoid(gate)
        dgate = (dact * up_ref[...] * (sg * (1.0 + gate * (1.0 - sg)))).astype(BF16)
        dup = (dact * (gate * sg)).astype(BF16)
        dgate_ref[...] = dgate
        dup_ref[...] = dup
        dhn = _dot(dgate, wg_ref[...]) + _dot(dup, wu_ref[...])
        h1 = h1_ref[...]
        rstd = lax.rsqrt(jnp.mean(h1 * h1, axis=-1, keepdims=True) + RMS_EPS)
        nrm = h1 * rstd
        dg_ref[...] += jnp.sum(dhn * nrm, axis=0, keepdims=True)
        dn = dhn * g_ref[...]
        dh1 = dh2 + rstd * (dn - nrm * jnp.mean(dn * nrm, axis=-1, keepdims=True))
        dh1_ref[...] = dh1
        dycat_ref[...] = _dot(dh1.astype(BF16), wo_ref[...])

    rows = lambda w: pl.BlockSpec((r, w), lambda i: (i, 0))
    return pl.pallas_call(
        body, name="ffn_bwd", grid=(t // r,),
        in_specs=[rows(D), rows(D_FF), rows(D_FF), rows(D), _const_spec((D, D_FF)), _const_spec((D_FF, D)),
                  _const_spec((D_FF, D)), _const_spec((D, D)), _const_spec((1, D))],
        out_specs=[rows(D_FF), rows(D_FF), rows(D), rows(D), _acc_spec((1, D))],
        out_shape=[jax.ShapeDtypeStruct((t, D_FF), BF16), jax.ShapeDtypeStruct((t, D_FF), BF16),
                   jax.ShapeDtypeStruct((t, D), F32), jax.ShapeDtypeStruct((t, D), F32),
                   jax.ShapeDtypeStruct((1, D), F32)],
        compiler_params=_params(1),
    )(dh2, gate, up, h1, w_down_t, w_gate_t, w_up_t, w_out_t, g_ffn)


def _conv_bwd(dycat, yc, u, conv_w, ln_g, ln_b, n_ex, lp, plan=None):
    r = CONV_TILE
    nt = lp // r
    hb = r // CONV_SUB
    nsub = r // CONV_SUB

    def ln_bwd(dy, yc_rows, live, lg, lb):
        mu = jnp.mean(yc_rows, axis=-1, keepdims=True)
        cen = yc_rows - mu
        rs = lax.rsqrt(jnp.mean(cen * cen, axis=-1, keepdims=True) + LN_EPS)
        yn = cen * rs
        out = yn * lg + lb
        so = _sigmoid(out)
        dout = jnp.where(live, dy * (so * (1.0 + out * (1.0 - so))), 0.0)
        dyn = dout * lg
        dyc = rs * (dyn - jnp.mean(dyn, axis=-1, keepdims=True) - yn * jnp.mean(dyn * yn, axis=-1, keepdims=True))
        return dyc, dout, yn

    def body(dy_ref, dyn_ref, yc_ref, ycn_ref, cur_ref, prev_ref, w_ref, lg_ref, lb_ref,
             du_ref, dw_ref, db_ref, dlg_ref, dlb_ref, glu, dycs, dwacc, glu_sh, dycs_sh):
        b = pl.program_id(0)
        i = pl.program_id(1)
        first = jnp.logical_and(b == 0, i == 0)

        @pl.when(first)
        def _():
            dwacc[...] = jnp.zeros_like(dwacc)
            db_ref[...] = jnp.zeros_like(db_ref)
            dlg_ref[...] = jnp.zeros_like(dlg_ref)
            dlb_ref[...] = jnp.zeros_like(dlb_ref)

        lg, lb = lg_ref[...], lb_ref[...]
        cur = cur_ref[...]
        sig = _sigmoid(cur[:, C_CONV:])
        glu[CONV_SUB:CONV_SUB + r, :] = cur[:, :C_CONV] * sig
        pv = prev_ref[...]
        glu[0:CONV_SUB, :] = jnp.where(i > 0, pv[:, :C_CONV] * _sigmoid(pv[:, C_CONV:]), 0.0)

        row = i * r + lax.broadcasted_iota(jnp.int32, (r, 1), 0)
        dyc, dout, yn = ln_bwd(dy_ref[...], yc_ref[...], row >= PAD_ROWS, lg, lb)
        dycs[0:r, :] = dyc
        dycn, _, _ = ln_bwd(dyn_ref[...], ycn_ref[...], i < nt - 1, lg, lb)
        dycs[r:r + CONV_SUB, :] = dycn
        db_ref[...] += jnp.sum(dyc, axis=0, keepdims=True)
        dlg_ref[...] += jnp.sum(dout * yn, axis=0, keepdims=True)
        dlb_ref[...] += jnp.sum(dout, axis=0, keepdims=True)

        _shifted_copies(glu, glu_sh, r)
        _shifted_copies(dycs, dycs_sh, r)
        w = w_ref[...]
        for j in range(nsub):
            r0 = j * CONV_SUB
            dblk = dycs[r0:r0 + CONV_SUB, :]
            dglu = jnp.zeros((CONV_SUB, C_CONV), F32)
            for k in range(CONV_W):
                dglu = dglu + w[k:k + 1, :] * _shifted_rows(dycs, dycs_sh, r0 + (CONV_W - 1) - k)
                prod = dblk * _shifted_rows(glu, glu_sh, r0 + CONV_LEAD + k)
                dwacc[k] += prod.reshape(CONV_SUB // SUBLANES, SUBLANES, C_CONV).sum(axis=0)
            sg = sig[r0:r0 + CONV_SUB, :]
            cv = cur[r0:r0 + CONV_SUB, :C_CONV]
            du_ref[r0:r0 + CONV_SUB, :C_CONV] = (dglu * sg).astype(BF16)
            du_ref[r0:r0 + CONV_SUB, C_CONV:] = (dglu * cv * sg * (1.0 - sg)).astype(BF16)

        @pl.when(jnp.logical_and(b == n_ex - 1, i == nt - 1))
        def _():
            dw_ref[...] = jnp.sum(dwacc[...], axis=1)

    t = n_ex * lp
    cur_rows = lambda w, col: pl.BlockSpec((r, w), lambda b, i: (b * nt + i, col))
    nxt_rows = lambda w, col: pl.BlockSpec(
        (CONV_SUB, w), lambda b, i: (jnp.minimum((b * nt + i + 1) * hb, n_ex * nt * hb - 1), col))
    return _call(
        body, name="conv_bwd", grid=(n_ex, nt),
        in_specs=[cur_rows(C_CONV, 0), nxt_rows(C_CONV, 0), cur_rows(C_CONV, 0), nxt_rows(C_CONV, 0),
                  cur_rows(2 * C_CONV, 0),
                  pl.BlockSpec((CONV_SUB, 2 * C_CONV), lambda b, i: (jnp.maximum((b * nt + i) * hb - 1, 0), 0)),
                  _const_spec((32, C_CONV)), _const_spec((1, C_CONV)), _const_spec((1, C_CONV))],
        out_specs=[cur_rows(2 * C_CONV, 0), _acc_spec((32, C_CONV)), _acc_spec((1, C_CONV)),
                   _acc_spec((1, C_CONV)), _acc_spec((1, C_CONV))],
        out_shape=[jax.ShapeDtypeStruct((t, 2 * C_CONV), BF16), jax.ShapeDtypeStruct((32, C_CONV), F32),
                   jax.ShapeDtypeStruct((1, C_CONV), F32), jax.ShapeDtypeStruct((1, C_CONV), F32),
                   jax.ShapeDtypeStruct((1, C_CONV), F32)],
        scratch_shapes=[pltpu.VMEM((r + CONV_SUB, C_CONV), F32), pltpu.VMEM((r + CONV_SUB, C_CONV), F32),
                        pltpu.VMEM((32, 8, C_CONV), F32),
                        pltpu.VMEM((SUBLANES - 1, r + CONV_SUB - SUBLANES, C_CONV), F32),
                        pltpu.VMEM((SUBLANES - 1, r + CONV_SUB - SUBLANES, C_CONV), F32)],
        plan=plan,
    )(dycat, dycat, yc, yc, u, u, conv_w, ln_g, ln_b)


def _gla_bwd_per_head(dycat, u, states, w2, gb, ng, n_ex, lp, plan=None):
    nc = lp // CHUNK
    t = n_ex * lp

    def body(dy_ref, qk_ref, v_ref, g_ref, lr_ref, st_ref, w2_ref, gb_ref, ng_ref,
             du_ref, dw2_ref, dgb_ref, dng_ref, dstate):
        n = pl.program_id(0)
        chunk = nc - 1 - n

        @pl.when(n == 0)
        def _():
            dw2_ref[...] = jnp.zeros_like(dw2_ref)
            dgb_ref[...] = jnp.zeros_like(dgb_ref)
            dng_ref[...] = jnp.zeros_like(dng_ref)
            dstate[...] = jnp.zeros_like(dstate)

        for e in range(n_ex):
            one_example(e, chunk, dy_ref, qk_ref, v_ref, g_ref, lr_ref, st_ref, w2_ref, gb_ref, ng_ref,
                        du_ref, dw2_ref, dgb_ref, dng_ref, dstate)

    def one_example(e, chunk, dy_ref, qk_ref, v_ref, g_ref, lr_ref, st_ref, w2_ref, gb_ref, ng_ref,
                    du_ref, dw2_ref, dgb_ref, dng_ref, dstate):
        dy_ref, qk_ref, v_ref, g_ref, lr_ref, st_ref = (r.at[e] for r in (dy_ref, qk_ref, v_ref, g_ref, lr_ref, st_ref))
        du_ref, dstate = du_ref.at[e], dstate.at[e]
        qk = qk_ref[...]
        q, k = qk[:, :GLA_K], qk[:, GLA_K:]
        lr = lr_ref[...]
        z, a, live = _gla_gates(lr, w2_ref[...], gb_ref[...], chunk == 0)
        causal = _tri(True)
        b = jnp.dot(causal.astype(F32), a, preferred_element_type=F32, precision=lax.Precision.HIGHEST)
        bl = b[CHUNK - 1:CHUNK, :]
        e_pos, e_neg, e_dec = jnp.exp(b), jnp.exp(-b), jnp.exp(bl - b)
        q_f = q * (DK ** -0.5) * e_pos
        k_f = k * e_neg
        kd_f = k * e_dec
        q_in, k_in, k_dec = q_f.astype(BF16), k_f.astype(BF16), kd_f.astype(BF16)
        decay = jnp.exp(bl)
        v = v_ref[...]
        g = g_ref[...]
        dy = dy_ref[...]
        ngv = ng_ref[...]
        st = st_ref[...]
        st_b = st.astype(BF16)
        dst = dstate[...]
        dst_b = dst.astype(BF16)
        dqs, dks, dvs, dgs, dbs, dbls, new_dst = [], [], [], [], [], [], []
        dng = jnp.zeros((1, DV), F32)
        for h in range(N_HEADS):
            ks = slice(h * DK, (h + 1) * DK)
            vs = slice(h * DV, (h + 1) * DV)
            qh, kh, kdh = q_in[:, ks], k_in[:, ks], k_dec[:, ks]
            vh = v[:, vs].astype(BF16)
            s = jnp.where(causal, _dot_nt(qh, kh), 0.0).astype(BF16)
            o = _dot(s, vh) + _dot_nt(qh, st_b[:, ks])
            rstd = lax.rsqrt(jnp.mean(o * o, axis=-1, keepdims=True) + RMS_EPS)
            nrm = o * rstd
            gh = g[:, vs]
            sg = _sigmoid(gh)
            dyh = dy[:, vs]
            dgs.append(dyh * nrm * ngv * (sg * (1.0 + gh * (1.0 - sg))))
            dt = dyh * (gh * sg)
            dng = dng + jnp.sum(dt * nrm, axis=0, keepdims=True)
            dn = dt * ngv
            do = (rstd * (dn - nrm * jnp.mean(dn * nrm, axis=-1, keepdims=True))).astype(BF16)
            da = jnp.where(causal, _dot_nt(do, vh), 0.0).astype(BF16)
            dvs.append(_dot_tn(s, do) + _dot_nt(kdh, dst_b[:, ks]))
            dq_in = _dot(da, kh) + _dot(do, st_b[:, ks])
            dk_in = _dot_tn(da, qh)
            dk_dec = _dot(vh, dst_b[:, ks])
            new_dst.append(_dot_tn(do, qh) + decay[:, ks] * dst[:, ks])
            dbls.append(jnp.sum(dk_dec * kd_f[:, ks], axis=0, keepdims=True)
                        + decay[:, ks] * jnp.sum(dst[:, ks] * st[:, ks], axis=0, keepdims=True))
            dqs.append(dq_in * (DK ** -0.5) * e_pos[:, ks])
            dks.append(dk_in * e_neg[:, ks] + dk_dec * e_dec[:, ks])
            dbs.append(dq_in * q_f[:, ks] - dk_in * k_f[:, ks] - dk_dec * kd_f[:, ks])
        dstate[...] = jnp.concatenate(new_dst, axis=1)
        row = lax.broadcasted_iota(jnp.int32, (CHUNK, 1), 0)
        db = jnp.concatenate(dbs, axis=1) + jnp.where(row == CHUNK - 1, jnp.concatenate(dbls, axis=1), 0.0)
        da_log = jnp.dot(_tri(False).astype(F32), db, preferred_element_type=F32, precision=lax.Precision.HIGHEST)
        dz = jnp.where(live, da_log * (1.0 - _sigmoid(z)) * (1.0 / GATE_TAU), 0.0)
        dz_b = dz.astype(BF16)
        du_ref[:, 0:GLA_K] = jnp.concatenate(dqs, axis=1).astype(BF16)
        du_ref[:, GLA_K:2 * GLA_K] = jnp.concatenate(dks, axis=1).astype(BF16)
        du_ref[:, 2 * GLA_K:2 * GLA_K + GLA_V] = jnp.concatenate(dvs, axis=1).astype(BF16)
        du_ref[:, 2 * GLA_K + GLA_V:2 * GLA_K + 2 * GLA_V] = jnp.concatenate(dgs, axis=1).astype(BF16)
        du_ref[:, 2 * GLA_K + 2 * GLA_V:] = _dot_nt(dz_b, w2_ref[...]).astype(BF16)
        dw2_ref[...] += _dot_tn(lr.astype(BF16), dz_b)
        dgb_ref[...] += jnp.sum(dz, axis=0, keepdims=True)
        dng_ref[...] += dng

    u3 = u.reshape(n_ex, lp, D_IN_PAD)
    rev = lambda w, col: pl.BlockSpec((n_ex, CHUNK, w), lambda n: (0, nc - 1 - n, col))
    (du, d_w2, d_gb, d_ng), extra = _call(
        body, name="gla_bwd", grid=(nc,),
        in_specs=[rev(GLA_V, 1), rev(2 * GLA_K, 2), rev(GLA_V, 3), rev(GLA_V, 4), rev(128, 20),
                  pl.BlockSpec((n_ex, DV, GLA_K), lambda n: (0, nc - 1 - n, 0)),
                  _const_spec((128, GLA_K)), _const_spec((1, GLA_K)), _const_spec((1, DV))],
        out_specs=[rev(D_GLA_IN, 0), _acc_spec((128, GLA_K)), _acc_spec((1, GLA_K)), _acc_spec((1, DV))],
        out_shape=[jax.ShapeDtypeStruct((n_ex, lp, D_GLA_IN), BF16), jax.ShapeDtypeStruct((128, GLA_K), F32),
                   jax.ShapeDtypeStruct((1, GLA_K), F32), jax.ShapeDtypeStruct((1, DV), F32)],
        scratch_shapes=[pltpu.VMEM((n_ex, DV, GLA_K), F32)],
        plan=plan,
    )(dycat.reshape(n_ex, lp, D), u3, u3, u3, u3, states, w2, gb, ng)
    return (du.reshape(t, D_GLA_IN), d_w2, d_gb, d_ng), extra


HEAD_ROWS_ALL = N_HEADS * CHUNK


def _head_of(shape, axis, per_head):
    return lax.broadcasted_iota(jnp.int32, shape, axis) // per_head


def _expand(x, lanes_per_head):
    rows, lanes = HEAD_ROWS_ALL, x.shape[1]
    keep = _head_of((rows, lanes), 0, CHUNK) == _head_of((rows, lanes), 1, lanes_per_head)
    return jnp.where(keep, jnp.tile(x, (N_HEADS, 1)), 0.0)


def _expand_lanes(x):
    rows, w = x.shape
    keep = _head_of((rows, N_HEADS * w), 0, CHUNK) == _head_of((rows, N_HEADS * w), 1, w)
    return jnp.where(keep, jnp.tile(x, (1, N_HEADS)), 0.0)


def _expand_state(st):
    rows, lanes = N_HEADS * DV, st.shape[1]
    keep = _head_of((rows, lanes), 0, DV) == _head_of((rows, lanes), 1, DK)
    return jnp.where(keep, jnp.tile(st, (N_HEADS, 1)), 0.0)


def _fold(t, rows_per_head):
    lane_head = _head_of((rows_per_head, t.shape[1]), 1, DK)
    out = jnp.where(lane_head == 0, t[0:rows_per_head], 0.0)
    for h in range(1, N_HEADS):
        out = out + jnp.where(lane_head == h, t[h * rows_per_head:(h + 1) * rows_per_head], 0.0)
    return out


def _rows_by_head(x):
    return jnp.concatenate([x[:, h * DV:(h + 1) * DV] for h in range(N_HEADS)], axis=0)


def _lanes_by_head(x):
    return jnp.concatenate([x[h * CHUNK:(h + 1) * CHUNK] for h in range(N_HEADS)], axis=1)


def _running_sum(a, lower):
    hi = a.astype(BF16)
    rest = a - hi.astype(F32)
    mid = rest.astype(BF16)
    lo = (rest - mid.astype(F32)).astype(BF16)
    w = a.shape[1]
    parts = _dot(_tri(lower).astype(F32).astype(BF16), jnp.concatenate([hi, mid, lo], axis=1))
    return parts[:, :w] + parts[:, w:2 * w] + parts[:, 2 * w:]


def _stacked_causal():
    i = lax.broadcasted_iota(jnp.int32, (HEAD_ROWS_ALL, CHUNK), 0) % CHUNK
    j = lax.broadcasted_iota(jnp.int32, (HEAD_ROWS_ALL, CHUNK), 1)
    return i >= j


def _gla_chunk(q, k, v, lr, st, w2, gb, first_chunk):
    z, a, live = _gla_gates(lr, w2, gb, first_chunk)
    yield
    b = _running_sum(a, True)
    yield
    bl = b[CHUNK - 1:CHUNK, :]
    e_pos, e_neg, e_dec = jnp.exp(b), jnp.exp(-b), jnp.exp(bl - b)
    q_f, k_f, kd_f = q * (DK ** -0.5) * e_pos, k * e_neg, k * e_dec
    qx = _expand(q_f, DK).astype(BF16)
    k_in, k_dec, v_b = k_f.astype(BF16), kd_f.astype(BF16), v.astype(BF16)
    s = jnp.where(_stacked_causal(), _dot_nt(qx, k_in), 0.0).astype(BF16)
    o_inter = _dot_nt(qx, st.astype(BF16))
    yield
    p = _dot(s, v_b)
    yield
    o = jnp.concatenate([p[h * CHUNK:(h + 1) * CHUNK, h * DV:(h + 1) * DV] for h in range(N_HEADS)], axis=0) + o_inter
    return dict(z=z, live=live, bl=bl, e_pos=e_pos, e_neg=e_neg, e_dec=e_dec, q_f=q_f, k_f=k_f, kd_f=kd_f,
                qx=qx, k_in=k_in, k_dec=k_dec, v_b=v_b, s=s, o=o, decay=jnp.exp(bl))


def _gla_fwd(u, w2, gb, ng, n_ex, lp, plan=None):
    nc = lp // CHUNK
    t = n_ex * lp

    def body(qk_ref, v_ref, g_ref, lr_ref, w2_ref, gb_ref, ng_ref, y_ref, st_ref, state):
        n = pl.program_id(0)

        @pl.when(n == 0)
        def _():
            state[...] = jnp.zeros_like(state)

        def one_example(e):
            st = state[e]
            st_ref[e] = st
            qk = qk_ref[e]
            c = yield from _gla_chunk(qk[:, :GLA_K], qk[:, GLA_K:], v_ref[e], lr_ref[e], st, w2_ref[...],
                                      gb_ref[...], n == 0)
            o = c["o"]
            rstd = lax.rsqrt(jnp.mean(o * o, axis=-1, keepdims=True) + RMS_EPS)
            g = _rows_by_head(g_ref[e])
            y_ref[e] = _lanes_by_head(o * rstd * ng_ref[...] * (g * _sigmoid(g))).astype(BF16)
            state[e] = c["decay"] * st + _fold(_dot_tn(c["v_b"], c["k_dec"]), DV)

        _in_lockstep(one_example(e) for e in range(n_ex))

    u3 = u.reshape(n_ex, lp, D_IN_PAD)
    blk = lambda w, col: pl.BlockSpec((n_ex, CHUNK, w), lambda n: (0, n, col))
    (y, states), extra = _call(
        body, name="gla_fwd", grid=(nc,),
        in_specs=[blk(2 * GLA_K, 2), blk(GLA_V, 3), blk(GLA_V, 4), blk(128, 20),
                  _const_spec((128, GLA_K)), _const_spec((1, GLA_K)), _const_spec((1, DV))],
        out_specs=[blk(GLA_V, 0), pl.BlockSpec((n_ex, DV, GLA_K), lambda n: (0, n, 0))],
        out_shape=[jax.ShapeDtypeStruct((n_ex, lp, GLA_V), BF16),
                   jax.ShapeDtypeStruct((n_ex, nc * DV, GLA_K), F32)],
        scratch_shapes=[pltpu.VMEM((n_ex, DV, GLA_K), F32)],
        plan=plan,
    )(u3, u3, u3, u3, w2, gb, ng)
    return (y.reshape(t, GLA_V), states), extra


def _gla_bwd(dycat, u, states, w2, gb, ng, n_ex, lp, plan=None):
    nc = lp // CHUNK
    t = n_ex * lp

    def body(dy_ref, qk_ref, v_ref, g_ref, lr_ref, st_ref, w2_ref, gb_ref, ng_ref,
             du_ref, dw2_ref, dgb_ref, dng_ref, dstate):
        n = pl.program_id(0)
        chunk = nc - 1 - n

        @pl.when(n == 0)
        def _():
            dw2_ref[...] = jnp.zeros_like(dw2_ref)
            dgb_ref[...] = jnp.zeros_like(dgb_ref)
            dng_ref[...] = jnp.zeros_like(dng_ref)
            dstate[...] = jnp.zeros_like(dstate)

        def one_example(e):
            qk = qk_ref[e]
            lr = lr_ref[e]
            st = st_ref[e]
            dst = dstate[e]
            c = yield from _gla_chunk(qk[:, :GLA_K], qk[:, GLA_K:], v_ref[e], lr, st, w2_ref[...], gb_ref[...],
                                      chunk == 0)
            qx, k_in, k_dec, v_b, s, o = c["qx"], c["k_in"], c["k_dec"], c["v_b"], c["s"], c["o"]
            ngv = ng_ref[...]
            rstd = lax.rsqrt(jnp.mean(o * o, axis=-1, keepdims=True) + RMS_EPS)
            nrm = o * rstd
            g = _rows_by_head(g_ref[e])
            dy = _rows_by_head(dy_ref[e])
            sg = _sigmoid(g)
            dg = dy * nrm * ngv * (sg * (1.0 + g * (1.0 - sg)))
            dt = dy * (g * sg)
            dng_ref[...] += jnp.sum(dt * nrm, axis=0, keepdims=True)
            dn = dt * ngv
            do = rstd * (dn - nrm * jnp.mean(dn * nrm, axis=-1, keepdims=True))
            do_b = do.astype(BF16)
            dox = _expand_lanes(do).astype(BF16)
            dstx = _expand_state(dst).astype(BF16)
            yield
            da = jnp.where(_stacked_causal(), _dot_nt(dox, v_b), 0.0).astype(BF16)
            dv = _dot_tn(s, dox) + _dot_nt(k_dec, dstx)
            dk_dec = _dot(v_b, dstx)
            dstate[e] = _dot_tn(do_b, qx) + c["decay"] * dst
            yield
            dq_in = _fold(_dot(da, k_in) + _dot(do_b, st.astype(BF16)), CHUNK)
            dk_in = _dot_tn(da, qx)
            yield
            dbl = (jnp.sum(dk_dec * c["kd_f"], axis=0, keepdims=True)
                   + c["decay"] * jnp.sum(dst * st, axis=0, keepdims=True))
            dq = dq_in * (DK ** -0.5) * c["e_pos"]
            dk = dk_in * c["e_neg"] + dk_dec * c["e_dec"]
            db = dq_in * c["q_f"] - dk_in * c["k_f"] - dk_dec * c["kd_f"]
            row = lax.broadcasted_iota(jnp.int32, (CHUNK, 1), 0)
            da_log = _running_sum(db + jnp.where(row == CHUNK - 1, dbl, 0.0), False)
            yield
            dz = jnp.where(c["live"], da_log * (1.0 - _sigmoid(c["z"])) * (1.0 / GATE_TAU), 0.0)
            dz_b = dz.astype(BF16)
            out = du_ref.at[e]
            out[:, 0:GLA_K] = dq.astype(BF16)
            out[:, GLA_K:2 * GLA_K] = dk.astype(BF16)
            out[:, 2 * GLA_K:2 * GLA_K + GLA_V] = dv.astype(BF16)
            out[:, 2 * GLA_K + GLA_V:2 * GLA_K + 2 * GLA_V] = _lanes_by_head(dg).astype(BF16)
            out[:, 2 * GLA_K + 2 * GLA_V:] = _dot_nt(dz_b, w2_ref[...]).astype(BF16)
            dw2_ref[...] += _dot_tn(lr.astype(BF16), dz_b)
            dgb_ref[...] += jnp.sum(dz, axis=0, keepdims=True)

        _in_lockstep(one_example(e) for e in range(n_ex))

    u3 = u.reshape(n_ex, lp, D_IN_PAD)
    rev = lambda w, col: pl.BlockSpec((n_ex, CHUNK, w), lambda n: (0, nc - 1 - n, col))
    (du, d_w2, d_gb, d_ng), extra = _call(
        body, name="gla_bwd", grid=(nc,),
        in_specs=[rev(GLA_V, 1), rev(2 * GLA_K, 2), rev(GLA_V, 3), rev(GLA_V, 4), rev(128, 20),
                  pl.BlockSpec((n_ex, DV, GLA_K), lambda n: (0, nc - 1 - n, 0)),
                  _const_spec((128, GLA_K)), _const_spec((1, GLA_K)), _const_spec((1, DV))],
        out_specs=[rev(D_GLA_IN, 0), _acc_spec((128, GLA_K)), _acc_spec((1, GLA_K)), _acc_spec((1, DV))],
        out_shape=[jax.ShapeDtypeStruct((n_ex, lp, D_GLA_IN), BF16), jax.ShapeDtypeStruct((128, GLA_K), F32),
                   jax.ShapeDtypeStruct((1, GLA_K), F32), jax.ShapeDtypeStruct((1, DV), F32)],
        scratch_shapes=[pltpu.VMEM((n_ex, DV, GLA_K), F32)],
        plan=plan,
    )(dycat.reshape(n_ex, lp, D), u3, u3, u3, u3, states, w2, gb, ng)
    return (du.reshape(t, D_GLA_IN), d_w2, d_gb, d_ng), extra


def _in_proj_bwd(du_conv, du_gla, w_in_t_conv, w_in_t_gla, h0, dh1, g_mix, plan=None):
    t = h0.shape[0]
    r = _row_tile(t, 384)

    def body(dc_ref, dg_ref, wc_ref, wg_ref, h_ref, dh1_ref, g_ref, dh0_ref, dgm_ref):
        @pl.when(pl.program_id(0) == 0)
        def _():
            dgm_ref[...] = jnp.zeros_like(dgm_ref)

        dhn = _dot(dc_ref[...], wc_ref[...]) + _dot(dg_ref[...], wg_ref[...])
        h = h_ref[...]
        rstd = lax.rsqrt(jnp.mean(h * h, axis=-1, keepdims=True) + RMS_EPS)
        nrm = h * rstd
        dgm_ref[...] += jnp.sum(dhn * nrm, axis=0, keepdims=True)
        dn = dhn * g_ref[...]
        dh0_ref[...] = dh1_ref[...] + rstd * (dn - nrm * jnp.mean(dn * nrm, axis=-1, keepdims=True))

    rows = lambda w: pl.BlockSpec((r, w), lambda i: (i, 0))
    return _call(
        body, name="in_proj_bwd", grid=(t // r,),
        in_specs=[rows(2 * C_CONV), rows(D_GLA_IN), _const_spec((2 * C_CONV, D)), _const_spec((D_GLA_IN, D)),
                  rows(D), rows(D), _const_spec((1, D))],
        out_specs=[rows(D), _acc_spec((1, D))],
        out_shape=[jax.ShapeDtypeStruct((t, D), F32), jax.ShapeDtypeStruct((1, D), F32)],
        plan=plan,
    )(du_conv, du_gla, w_in_t_conv, w_in_t_gla, h0, dh1, g_mix)


def _wgrad_hosting(x, dy, name, plan):
    t, m = x.shape
    n = dy.shape[1]
    tk = t // 3 if t % (3 * 128) == 0 else _row_tile(t, 384)
    tm = m if m <= D_GLA_IN else m // 2

    def body(x_ref, dy_ref, o_ref):
        @pl.when(pl.program_id(2) == 0)
        def _():
            o_ref[...] = jnp.zeros_like(o_ref)

        o_ref[...] += _dot_tn(x_ref[...].astype(BF16), dy_ref[...].astype(BF16))

    (out,), extra = _call(
        body, name=name, grid=(m // tm, 1, t // tk),
        in_specs=[pl.BlockSpec((tk, tm), lambda i, j, k: (k, i)), pl.BlockSpec((tk, n), lambda i, j, k: (k, j))],
        out_specs=[pl.BlockSpec((tm, n), lambda i, j, k: (i, j))],
        out_shape=[jax.ShapeDtypeStruct((m, n), F32)],
        plan=plan,
    )(x, dy)
    return out, extra


def _wgrad(x, dy, name):
    t, m = x.shape
    n = dy.shape[1]
    tk = t // 3 if t % (3 * 128) == 0 else _row_tile(t, 384)
    tm = m if m <= D_GLA_IN else m // 2
    tn = n

    def body(x_ref, dy_ref, o_ref):
        @pl.when(pl.program_id(2) == 0)
        def _():
            o_ref[...] = jnp.zeros_like(o_ref)

        o_ref[...] += _dot_tn(x_ref[...].astype(BF16), dy_ref[...].astype(BF16))

    return pl.pallas_call(
        body, name=name, grid=(m // tm, n // tn, t // tk),
        in_specs=[pl.BlockSpec((tk, tm), lambda i, j, k: (k, i)), pl.BlockSpec((tk, tn), lambda i, j, k: (k, j))],
        out_specs=pl.BlockSpec((tm, tn), lambda i, j, k: (i, j)),
        out_shape=jax.ShapeDtypeStruct((m, n), F32),
        compiler_params=_params(3),
    )(x, dy)


def _mesh_pos():
    return lax.axis_index("x"), lax.axis_index("y"), lax.axis_index("c")


def _other_chips(x, y):
    return [(1 - x, y), (x, 1 - y), (1 - x, 1 - y)]


HBM_SPEC = pl.BlockSpec(memory_space=pltpu.HBM)


def _gather_shards(shards):
    n = len(shards)

    def body(*refs):
        ins, outs = refs[:n], refs[n:2 * n]
        send_sems, recv_sems, local_sems = refs[2 * n:]
        x, y, c = _mesh_pos()
        mine = 2 * x + y
        chips = _other_chips(x, y)
        local = [pltpu.make_async_copy(ins[a], outs[a].at[mine], local_sems.at[a]) for a in range(n)]
        for cp in local:
            cp.start()

        def remote(a, k, block):
            px, py = chips[k]
            return pltpu.make_async_remote_copy(
                src_ref=ins[a], dst_ref=outs[a].at[block], send_sem=send_sems.at[3 * a + k],
                recv_sem=recv_sems.at[3 * a + k], device_id=(px, py, c), device_id_type=MESH)

        sends = [remote(a, k, mine) for a in range(n) for k in range(3)]
        for cp in sends:
            cp.start()
        for a in range(n):
            for k, (px, py) in enumerate(chips):
                remote(a, k, 2 * px + py).wait_recv()
        for cp in sends:
            cp.wait_send()
        for cp in local:
            cp.wait()

    return pl.pallas_call(
        body, name="gather_shards",
        in_specs=[HBM_SPEC] * n, out_specs=[HBM_SPEC] * n,
        out_shape=[jax.ShapeDtypeStruct((N_CHIPS,) + s.shape, s.dtype) for s in shards],
        scratch_shapes=[pltpu.SemaphoreType.DMA((3 * n,)), pltpu.SemaphoreType.DMA((3 * n,)),
                        pltpu.SemaphoreType.DMA((n,))],
        compiler_params=pltpu.CompilerParams(has_side_effects=True),
    )(*shards)


def _send_half_to_sibling(g2):
    def body(g_ref, recv_ref, send_sem, recv_sem):
        x, y, c = _mesh_pos()
        cp = pltpu.make_async_remote_copy(
            src_ref=g_ref.at[1 - c], dst_ref=recv_ref, send_sem=send_sem, recv_sem=recv_sem,
            device_id=(x, y, 1 - c), device_id_type=MESH)
        cp.start()
        cp.wait()

    return pl.pallas_call(
        body, name="rs_to_sibling", in_specs=[HBM_SPEC], out_specs=HBM_SPEC,
        out_shape=jax.ShapeDtypeStruct(g2.shape[1:], g2.dtype),
        scratch_shapes=[pltpu.SemaphoreType.DMA(()), pltpu.SemaphoreType.DMA(())],
        compiler_params=pltpu.CompilerParams(has_side_effects=True),
    )(g2)


def _add_own_half(g2, recv, c):
    rows = N_CHIPS * HALF_ROWS
    tr = 512
    g2f = g2.reshape(2, rows, D)
    recvf = recv.reshape(rows, D)

    def body(c_ref, a_ref, b_ref, o_ref):
        o_ref[...] = a_ref[0] + b_ref[...]

    out = pl.pallas_call(
        body, name="rs_add_halves",
        grid_spec=pltpu.PrefetchScalarGridSpec(
            num_scalar_prefetch=1, grid=(rows // tr,),
            in_specs=[pl.BlockSpec((1, tr, D), lambda i, s: (s[0], i, 0)), pl.BlockSpec((tr, D), lambda i, s: (i, 0))],
            out_specs=pl.BlockSpec((tr, D), lambda i, s: (i, 0))),
        out_shape=jax.ShapeDtypeStruct((rows, D), F32),
        compiler_params=_params(1),
    )(jnp.reshape(c, (1,)).astype(jnp.int32), g2f, recvf)
    return out.reshape(N_CHIPS, HALF_ROWS, D)


def _exchange_chip_sums(p):
    def body(p_ref, out_ref, send_sems, recv_sems, local_sem):
        x, y, c = _mesh_pos()
        mine = 2 * x + y
        chips = _other_chips(x, y)
        local = pltpu.make_async_copy(p_ref.at[mine], out_ref.at[mine], local_sem)
        local.start()

        def remote(k, src_block, dst_block):
            px, py = chips[k]
            return pltpu.make_async_remote_copy(
                src_ref=p_ref.at[src_block], dst_ref=out_ref.at[dst_block], send_sem=send_sems.at[k],
                recv_sem=recv_sems.at[k], device_id=(px, py, c), device_id_type=MESH)

        sends = [remote(k, 2 * px + py, mine) for k, (px, py) in enumerate(chips)]
        for cp in sends:
            cp.start()
        for k, (px, py) in enumerate(chips):
            remote(k, mine, 2 * px + py).wait_recv()
        for cp in sends:
            cp.wait_send()
        local.wait()

    return pl.pallas_call(
        body, name="rs_chip_exchange", in_specs=[HBM_SPEC], out_specs=HBM_SPEC,
        out_shape=jax.ShapeDtypeStruct(p.shape, p.dtype),
        scratch_shapes=[pltpu.SemaphoreType.DMA((3,)), pltpu.SemaphoreType.DMA((3,)), pltpu.SemaphoreType.DMA(())],
        compiler_params=pltpu.CompilerParams(has_side_effects=True),
    )(p)


def _sum_chips(parts):
    tr = 512

    def body(p_ref, o_ref):
        o_ref[...] = ((p_ref[0] + p_ref[1]) + p_ref[2]) + p_ref[3]

    return pl.pallas_call(
        body, name="rs_sum_chips", grid=(HALF_ROWS // tr,),
        in_specs=[pl.BlockSpec((N_CHIPS, tr, D), lambda i: (0, i, 0))],
        out_specs=pl.BlockSpec((tr, D), lambda i: (i, 0)),
        out_shape=jax.ShapeDtypeStruct((HALF_ROWS, D), F32),
        compiler_params=_params(1),
    )(parts)


def _share_with_sibling(half):
    def body(h_ref, out_ref, send_sem, recv_sem, local_sem):
        x, y, c = _mesh_pos()
        local = pltpu.make_async_copy(h_ref, out_ref.at[c], local_sem)
        local.start()
        cp = pltpu.make_async_remote_copy(
            src_ref=h_ref, dst_ref=out_ref.at[c], send_sem=send_sem, recv_sem=recv_sem,
            device_id=(x, y, 1 - c), device_id_type=MESH)
        cp.start()
        pltpu.make_async_remote_copy(
            src_ref=h_ref, dst_ref=out_ref.at[1 - c], send_sem=send_sem, recv_sem=recv_sem,
            device_id=(x, y, 1 - c), device_id_type=MESH).wait_recv()
        cp.wait_send()
        local.wait()

    return pl.pallas_call(
        body, name="rs_share_sibling", in_specs=[HBM_SPEC], out_specs=HBM_SPEC,
        out_shape=jax.ShapeDtypeStruct((2,) + half.shape, half.dtype),
        scratch_shapes=[pltpu.SemaphoreType.DMA(()), pltpu.SemaphoreType.DMA(()), pltpu.SemaphoreType.DMA(())],
        compiler_params=pltpu.CompilerParams(has_side_effects=True),
    )(half)


def _adam_update(g, w, m, v):
    m2 = ADAM_B1 * m + (1.0 - ADAM_B1) * g
    v2 = ADAM_B2 * v + (1.0 - ADAM_B2) * (g * g)
    m_hat = m2 / (1.0 - ADAM_B1 ** ADAM_STEP)
    v_hat = v2 / (1.0 - ADAM_B2 ** ADAM_STEP)
    delta = -ADAM_LR * (m_hat / (jnp.sqrt(v_hat) + ADAM_EPS) + ADAM_WD * w)
    return delta, m2, v2


def _adamw_slab(g, w, m, v):
    rows = g.shape[0]
    tr = 256

    def body(g_ref, w_ref, m_ref, v_ref, d_ref, m2_ref, v2_ref):
        d_ref[...], m2_ref[...], v2_ref[...] = _adam_update(g_ref[...], w_ref[...], m_ref[...], v_ref[...])

    spec = pl.BlockSpec((tr, D), lambda i: (i, 0))
    return pl.pallas_call(
        body, name="adamw_slab", grid=(rows // tr,), in_specs=[spec] * 4, out_specs=[spec] * 3,
        out_shape=[jax.ShapeDtypeStruct((rows, D), F32)] * 3,
        compiler_params=_params(1),
    )(g, w, m, v)


def _allreduce_small_adamw(part, w, m, v):
    def body(p_ref, w_ref, m_ref, v_ref, g_ref, d_ref, m2_ref, v2_ref, slots, send_sems, recv_sems):
        x, y, c = _mesh_pos()
        mine = 4 * x + 2 * y + c
        peers = [(px, py, pc) for px in (x, 1 - x) for py in (y, 1 - y) for pc in (c, 1 - c)][1:]

        def remote(k, slot):
            return pltpu.make_async_remote_copy(
                src_ref=p_ref, dst_ref=slots.at[slot], send_sem=send_sems.at[k], recv_sem=recv_sems.at[k],
                device_id=peers[k], device_id_type=MESH)

        sends = [remote(k, mine) for k in range(7)]
        for cp in sends:
            cp.start()
        slots[mine] = p_ref[...]
        for k, (px, py, pc) in enumerate(peers):
            remote(k, 4 * px + 2 * py + pc).wait_recv()
        for cp in sends:
            cp.wait_send()
        g = slots[0]
        for d in range(1, 8):
            g = g + slots[d]
        g_ref[...] = g
        d_ref[...], m2_ref[...], v2_ref[...] = _adam_update(g, w_ref[...], m_ref[...], v_ref[...])

    vm = pl.BlockSpec(memory_space=pltpu.VMEM)
    shape = jax.ShapeDtypeStruct(part.shape, F32)
    return pl.pallas_call(
        body, name="small_allreduce_adamw", in_specs=[vm] * 4, out_specs=[vm] * 4, out_shape=[shape] * 4,
        scratch_shapes=[pltpu.VMEM((8,) + part.shape, F32), pltpu.SemaphoreType.DMA((7,)),
                        pltpu.SemaphoreType.DMA((7,))],
        compiler_params=pltpu.CompilerParams(has_side_effects=True),
    )(part, w, m, v)


def _half(ref, c, axis):
    n = ref.shape[axis] // 2
    return ref.at[(slice(None),) * axis + (pl.ds(c * n, n),)]


def _remote(src, dst, send_sem, recv_sem, device):
    return pltpu.make_async_remote_copy(src_ref=src, dst_ref=dst, send_sem=send_sem, recv_sem=recv_sem,
                                        device_id=device, device_id_type=MESH)


def _gather_weights(split, axes, whole):
    ns, n = len(split), len(split) + len(whole)

    def body(*refs):
        ins, outs = refs[:n], refs[n:2 * n]
        ici_send, ici_recv, d2d_send, d2d_recv, local_sems = refs[2 * n:]
        x, y, c = _mesh_pos()
        mine = 2 * x + y
        chips = _other_chips(x, y)
        local = [pltpu.make_async_copy(ins[a], outs[a].at[mine], local_sems.at[a]) for a in range(n)]
        for cp in local:
            cp.start()

        def ici(a, k, block):
            px, py = chips[k]
            src, dst = ins[a], outs[a].at[block]
            if a < ns:
                src, dst = _half(src, c, axes[a]), _half(dst, c, axes[a])
            return _remote(src, dst, ici_send.at[3 * a + k], ici_recv.at[3 * a + k], (px, py, c))

        def d2d(a, k, block, half):
            part = _half(outs[a].at[block], half, axes[a])
            return _remote(part, part, d2d_send.at[3 * a + k], d2d_recv.at[3 * a + k], (x, y, 1 - c))

        sends = [ici(a, k, mine) for a in range(n) for k in range(3)]
        for cp in sends:
            cp.start()
        for a in range(n):
            for k, (px, py) in enumerate(chips):
                ici(a, k, 2 * px + py).wait_recv()
                if a < ns:
                    sends.append(d2d(a, k, 2 * px + py, c))
                    sends[-1].start()
        for a in range(ns):
            for k, (px, py) in enumerate(chips):
                d2d(a, k, 2 * px + py, 1 - c).wait_recv()
        for cp in sends:
            cp.wait_send()
        for cp in local:
            cp.wait()

    arrays = list(split) + list(whole)
    return pl.pallas_call(
        body, name="gather_weights", in_specs=[HBM_SPEC] * n, out_specs=[HBM_SPEC] * n,
        out_shape=[jax.ShapeDtypeStruct((N_CHIPS,) + s.shape, s.dtype) for s in arrays],
        scratch_shapes=[pltpu.SemaphoreType.DMA((3 * n,)), pltpu.SemaphoreType.DMA((3 * n,)),
                        pltpu.SemaphoreType.DMA((3 * ns,)), pltpu.SemaphoreType.DMA((3 * ns,)),
                        pltpu.SemaphoreType.DMA((n,))],
        compiler_params=pltpu.CompilerParams(has_side_effects=True),
    )(*arrays)


def _rs_to_sibling(gs):
    n = len(gs)

    def body(*refs):
        ins, outs, send_sems, recv_sems = refs[:n], refs[n:2 * n], refs[2 * n], refs[2 * n + 1]
        x, y, c = _mesh_pos()
        copies = [_remote(_half(ins[a], 1 - c, 2), outs[a], send_sems.at[a], recv_sems.at[a], (x, y, 1 - c))
                  for a in range(n)]
        for cp in copies:
            cp.start()
        for cp in copies:
            cp.wait()

    return pl.pallas_call(
        body, name="rs_to_sibling", in_specs=[HBM_SPEC] * n, out_specs=[HBM_SPEC] * n,
        out_shape=[jax.ShapeDtypeStruct(g.shape[:2] + (g.shape[2] // 2,), g.dtype) for g in gs],
        scratch_shapes=[pltpu.SemaphoreType.DMA((n,)), pltpu.SemaphoreType.DMA((n,))],
        compiler_params=pltpu.CompilerParams(has_side_effects=True),
    )(*gs)


def _rs_add_halves(g, recv, c, name):
    _, rows, w = g.shape
    h = w // 2
    tr = rows // 2 if rows % 16 == 0 and rows > 64 else rows

    def body(c_ref, a_ref, b_ref, o_ref):
        o_ref[...] = (a_ref[...] + b_ref[...]).astype(BF16)

    return pl.pallas_call(
        body, name=name,
        grid_spec=pltpu.PrefetchScalarGridSpec(
            num_scalar_prefetch=1, grid=(N_CHIPS, rows // tr),
            in_specs=[pl.BlockSpec((1, tr, h), lambda j, i, s: (j, i, s[0])),
                      pl.BlockSpec((1, tr, h), lambda j, i, s: (j, i, 0))],
            out_specs=pl.BlockSpec((1, tr, h), lambda j, i, s: (j, i, 0))),
        out_shape=jax.ShapeDtypeStruct((N_CHIPS, rows, h), BF16),
        compiler_params=_params(2),
    )(jnp.reshape(c, (1,)).astype(jnp.int32), g, recv)


def _rs_chip_exchange(ps):
    n = len(ps)

    def body(*refs):
        ins, outs = refs[:n], refs[n:2 * n]
        send_sems, recv_sems, local_sems = refs[2 * n:]
        x, y, c = _mesh_pos()
        mine = 2 * x + y
        chips = _other_chips(x, y)
        local = [pltpu.make_async_copy(ins[a].at[mine], outs[a].at[mine], local_sems.at[a]) for a in range(n)]
        for cp in local:
            cp.start()

        def ici(a, k, src_block, dst_block):
            px, py = chips[k]
            return _remote(ins[a].at[src_block], outs[a].at[dst_block], send_sems.at[3 * a + k],
                           recv_sems.at[3 * a + k], (px, py, c))

        sends = [ici(a, k, 2 * px + py, mine) for a in range(n) for k, (px, py) in enumerate(chips)]
        for cp in sends:
            cp.start()
        for a in range(n):
            for k, (px, py) in enumerate(chips):
                ici(a, k, mine, 2 * px + py).wait_recv()
        for cp in sends:
            cp.wait_send()
        for cp in local:
            cp.wait()

    return pl.pallas_call(
        body, name="rs_chip_exchange", in_specs=[HBM_SPEC] * n, out_specs=[HBM_SPEC] * n,
        out_shape=[jax.ShapeDtypeStruct(p.shape, p.dtype) for p in ps],
        scratch_shapes=[pltpu.SemaphoreType.DMA((3 * n,)), pltpu.SemaphoreType.DMA((3 * n,)),
                        pltpu.SemaphoreType.DMA((n,))],
        compiler_params=pltpu.CompilerParams(has_side_effects=True),
    )(*ps)


def _rs_sum_chips(parts, name):
    _, rows, h = parts.shape
    tr = rows // 2 if rows % 16 == 0 and rows > 64 else rows

    def body(p_ref, o_ref):
        p = p_ref[...].astype(F32)
        o_ref[...] = ((p[0] + p[1]) + p[2]) + p[3]

    return pl.pallas_call(
        body, name=name, grid=(rows // tr,),
        in_specs=[pl.BlockSpec((N_CHIPS, tr, h), lambda i: (0, i, 0))],
        out_specs=pl.BlockSpec((tr, h), lambda i: (i, 0)),
        out_shape=jax.ShapeDtypeStruct((rows, h), F32),
        compiler_params=_params(1),
    )(parts)


def _rs_share(halves):
    n = len(halves)

    def body(*refs):
        ins, outs = refs[:n], refs[n:2 * n]
        send_sems, recv_sems, local_sems = refs[2 * n:]
        x, y, c = _mesh_pos()
        local = [pltpu.make_async_copy(ins[a], _half(outs[a], c, 1), local_sems.at[a]) for a in range(n)]
        for cp in local:
            cp.start()
        sends = [_remote(ins[a], _half(outs[a], c, 1), send_sems.at[a], recv_sems.at[a], (x, y, 1 - c))
                 for a in range(n)]
        for cp in sends:
            cp.start()
        for a in range(n):
            _remote(ins[a], _half(outs[a], 1 - c, 1), send_sems.at[a], recv_sems.at[a], (x, y, 1 - c)).wait_recv()
        for cp in sends:
            cp.wait_send()
        for cp in local:
            cp.wait()

    return pl.pallas_call(
        body, name="rs_share", in_specs=[HBM_SPEC] * n, out_specs=[HBM_SPEC] * n,
        out_shape=[jax.ShapeDtypeStruct((p.shape[0], 2 * p.shape[1]), p.dtype) for p in halves],
        scratch_shapes=[pltpu.SemaphoreType.DMA((n,)), pltpu.SemaphoreType.DMA((n,)),
                        pltpu.SemaphoreType.DMA((n,))],
        compiler_params=pltpu.CompilerParams(has_side_effects=True),
    )(*halves)


def _adamw(g, w, m, v, name):
    rows, cols = g.shape
    tr = 256 if rows % 256 == 0 else (rows // 2 if rows % 16 == 0 and rows > 64 else rows)

    def body(g_ref, w_ref, m_ref, v_ref, d_ref, m2_ref, v2_ref):
        d_ref[...], m2_ref[...], v2_ref[...] = _adam_update(g_ref[...], w_ref[...], m_ref[...], v_ref[...])

    spec = pl.BlockSpec((tr, cols), lambda i: (i, 0))
    return pl.pallas_call(
        body, name=name, grid=(rows // tr,), in_specs=[spec] * 4, out_specs=[spec] * 3,
        out_shape=[jax.ShapeDtypeStruct((rows, cols), F32)] * 3,
        compiler_params=_params(1),
    )(g, w, m, v)


def _rows_of(a):
    flat = a.reshape(-1)
    pad = (-flat.shape[0]) % D
    if pad:
        flat = jnp.concatenate([flat, jnp.zeros((pad,), flat.dtype)])
    return flat.reshape(-1, D)


SLAB_PARTS = (("w_in", (D, D_IN // N_CHIPS)), ("w_out", (D // N_CHIPS, D)), ("w_ffn_gate", (D, D_FF // N_CHIPS)),
              ("w_ffn_up", (D, D_FF // N_CHIPS)), ("w_ffn_down", (D_FF // N_CHIPS, D)),
              ("meta_tokens", (N_META, D // N_CHIPS)), ("conv_w", (CONV_W, C_CONV // N_CHIPS)),
              ("gla_w_gate2", (RANK, GLA_K // N_CHIPS)))


def _pack_slab(parts):
    rows = [_rows_of(parts[name].reshape(shape)) for name, shape in SLAB_PARTS]
    used = sum(r.shape[0] for r in rows)
    rows.append(jnp.zeros((SLAB_ROWS - used, D), F32))
    return jnp.concatenate(rows, axis=0)


def _unpack_slab(slab, lead):
    out, r0 = {}, 0
    for name, shape in SLAB_PARTS:
        size = shape[0] * shape[1]
        nrows = -(-size // D)
        out[name] = slab[r0:r0 + nrows].reshape(-1)[:size].reshape(lead[name] + shape)
        r0 += nrows
    return out


SMALL_PARTS = (("norm_mix_g", 0, 0, D), ("norm_ffn_g", 1, 0, D), ("norm_final_g", 2, 0, D),
               ("conv_b", 3, 0, C_CONV), ("conv_ln_g", 3, C_CONV, C_CONV), ("conv_ln_b", 4, 0, C_CONV),
               ("gla_gate_b", 4, C_CONV, GLA_K), ("gla_norm_g", 4, C_CONV + GLA_K, DV))


def _pack_small(parts):
    slab = jnp.zeros((SMALL_ROWS, D), F32)
    for name, row, col, size in SMALL_PARTS:
        slab = lax.dynamic_update_slice(slab, parts[name].reshape(1, size).astype(F32), (row, col))
    return slab


def _unpack_small(slab, shapes):
    return {name: slab[row, col:col + size].reshape(shapes[name]) for name, row, col, size in SMALL_PARTS}


def _column_block(full, j, width):
    return lax.dynamic_slice_in_dim(full, j * width, width, axis=1)


def _local_step(x, target, w):
    n_ex, seq, _ = x.shape
    lp = HEAD_ROWS + seq
    t = n_ex * lp
    meta = jnp.broadcast_to(w["meta_tokens"][None], (n_ex, N_META, D))
    h0 = jnp.concatenate([jnp.zeros((n_ex, PAD_ROWS, D), F32), meta, x], axis=1).reshape(t, D)
    tgt = jnp.concatenate([jnp.zeros((n_ex, HEAD_ROWS, D), F32), target], axis=1).reshape(t, D)
    row_mask = jnp.concatenate([jnp.zeros((n_ex, HEAD_ROWS, 1), F32), jnp.ones((n_ex, seq, 1), F32)],
                               axis=1).reshape(t, 1)

    u, hn = _in_proj(h0, w["norm_mix_g"], w["w_in"])
    yc, y_conv = _conv_fwd(u, w["conv_w"], w["conv_b"], w["conv_ln_g"], w["conv_ln_b"], n_ex, lp)
    y_gla, states = _gla_fwd(u, w["gla_w_gate2"], w["gla_gate_b"], w["gla_norm_g"], n_ex, lp)
    h1, hn2, gate, up, act = _mix_out_ffn_up(h0, y_conv, y_gla, w["w_out"], w["norm_ffn_g"],
                                             w["w_ffn_gate_t"], w["w_ffn_up_t"])
    dh2, loss, d_final_g = _ffn_down_loss(act, w["w_ffn_down"], h1, tgt, w["norm_final_g"], row_mask)

    dgate, dup, dh1, dycat, d_ffn_g = _ffn_bwd(dh2, gate, up, h1, w["w_ffn_down"], w["w_ffn_gate_t"],
                                                w["w_ffn_up_t"], w["w_out"], w["norm_ffn_g"])
    du_conv, d_conv_w, d_conv_b, d_ln_g, d_ln_b = _conv_bwd(dycat, yc, u, w["conv_w"], w["conv_ln_g"],
                                                            w["conv_ln_b"], n_ex, lp)
    du_gla, d_w2, d_gate_b, d_norm_g = _gla_bwd(dycat, u, states, w["gla_w_gate2"], w["gla_gate_b"],
                                                w["gla_norm_g"], n_ex, lp)
    dh0, d_mix_g = _in_proj_bwd(du_conv, du_gla, w["w_in"][:, :2 * C_CONV], w["w_in"][:, 2 * C_CONV:],
                                h0, dh1, w["norm_mix_g"])

    d_w_in_t = jnp.concatenate([_wgrad(du_conv, hn, "wgrad_in_conv"), _wgrad(du_gla, hn, "wgrad_in_gla")],
                               axis=0)[:D_IN]
    d_w_out = jnp.concatenate([_wgrad(y_conv, dh1, "wgrad_out_conv"), _wgrad(y_gla, dh1, "wgrad_out_gla")], axis=0)
    dh0 = dh0.reshape(n_ex, lp, D)
    grads = {
        "w_in_t": d_w_in_t, "w_out": d_w_out,
        "w_ffn_gate_t": _wgrad(dgate, hn2, "wgrad_gate"), "w_ffn_up_t": _wgrad(dup, hn2, "wgrad_up"),
        "w_ffn_down": _wgrad(act, dh2, "wgrad_down"),
        "meta_tokens": jnp.sum(dh0[:, PAD_ROWS:HEAD_ROWS], axis=0),
        "conv_w": d_conv_w, "gla_w_gate2": d_w2[:RANK],
        "norm_mix_g": d_mix_g, "norm_ffn_g": d_ffn_g, "norm_final_g": d_final_g,
        "conv_b": d_conv_b, "conv_ln_g": d_ln_g, "conv_ln_b": d_ln_b,
        "gla_gate_b": d_gate_b, "gla_norm_g": d_norm_g,
    }
    return loss[0, 0], dh0[:, HEAD_ROWS:], grads


WEIGHT_NAMES = ("meta_tokens", "norm_mix_g", "w_in", "conv_w", "conv_b", "conv_ln_g", "conv_ln_b", "gla_w_gate2",
                "gla_gate_b", "gla_norm_g", "w_out", "norm_ffn_g", "w_ffn_gate", "w_ffn_up", "w_ffn_down",
                "norm_final_g")
MATMUL_WEIGHTS = ("w_in", "w_out", "w_ffn_gate", "w_ffn_up", "w_ffn_down")
ROW_SHARDED = ("w_out", "w_ffn_down")


def _full_weights(ws):
    sh = lambda name: ws[name].reshape(ws[name].shape[-2:])
    split = [sh("w_in").astype(BF16), sh("w_out").astype(BF16), sh("w_ffn_gate").T.astype(BF16),
             sh("w_ffn_up").T.astype(BF16), sh("w_ffn_down").astype(BF16)]
    whole = [sh("meta_tokens"), sh("conv_w"), sh("gla_w_gate2")]
    w_in, w_out, gate_t, up_t, down, meta, conv_w, w2 = _gather_weights(split, [0, 0, 0, 0, 0], whole)
    cols = lambda a: jnp.concatenate([a[j] for j in range(N_CHIPS)], axis=1)
    full = {name: ws[name].reshape(1, -1) for name, _, _, _ in SMALL_PARTS}
    full["w_in"] = jnp.concatenate([cols(w_in), jnp.zeros((D, D_IN_PAD - D_IN), BF16)], axis=1)
    full["w_out"] = w_out.reshape(D, D)
    full["w_ffn_gate_t"] = gate_t.reshape(D_FF, D)
    full["w_ffn_up_t"] = up_t.reshape(D_FF, D)
    full["w_ffn_down"] = down.reshape(D_FF, D)
    full["meta_tokens"] = cols(meta)
    full["conv_w"] = jnp.concatenate([cols(conv_w), jnp.zeros((32 - CONV_W, C_CONV), F32)], axis=0)
    full["gla_w_gate2"] = jnp.concatenate([cols(w2), jnp.zeros((128 - RANK, GLA_K), F32)], axis=0).astype(BF16)
    return full


SMALL_RS_ROWS = 48


def _pack_small_sharded(grads):
    by_chip = lambda g, w: jnp.transpose(g.reshape(g.shape[0], N_CHIPS, w), (1, 0, 2))
    meta = by_chip(grads["meta_tokens"], D // N_CHIPS)
    conv = by_chip(grads["conv_w"], C_CONV // N_CHIPS).reshape(N_CHIPS, 16, 256)
    w2 = by_chip(grads["gla_w_gate2"], GLA_K // N_CHIPS).reshape(N_CHIPS, 4, 256)
    pad = jnp.zeros((N_CHIPS, SMALL_RS_ROWS - 36, 256), F32)
    return jnp.concatenate([meta, conv, w2, pad], axis=1)


def _unpack_small_sharded(g):
    return {"meta_tokens": g[0:16], "conv_w": g[16:32].reshape(32, C_CONV // N_CHIPS)[:CONV_W],
            "gla_w_gate2": g[32:36].reshape(RANK, GLA_K // N_CHIPS)}


def _kernel_without_overlap(x, meta_tokens, norm_mix_g, w_in, conv_w, conv_b, conv_ln_g, conv_ln_b, gla_w_gate2, gla_gate_b, gla_norm_g, w_out, norm_ffn_g, w_ffn_gate, w_ffn_up, w_ffn_down, norm_final_g, loss_target, m_meta_tokens, m_norm_mix_g, m_w_in, m_conv_w, m_conv_b, m_conv_ln_g, m_conv_ln_b, m_gla_w_gate2, m_gla_gate_b, m_gla_norm_g, m_w_out, m_norm_ffn_g, m_w_ffn_gate, m_w_ffn_up, m_w_ffn_down, m_norm_final_g, v_meta_tokens, v_norm_mix_g, v_w_in, v_conv_w, v_conv_b, v_conv_ln_g, v_conv_ln_b, v_gla_w_gate2, v_gla_gate_b, v_gla_norm_g, v_w_out, v_norm_ffn_g, v_w_ffn_gate, v_w_ffn_up, v_w_ffn_down, v_norm_final_g):
    ws = dict(zip(WEIGHT_NAMES, (meta_tokens, norm_mix_g, w_in, conv_w, conv_b, conv_ln_g, conv_ln_b, gla_w_gate2,
                                 gla_gate_b, gla_norm_g, w_out, norm_ffn_g, w_ffn_gate, w_ffn_up, w_ffn_down,
                                 norm_final_g)))
    ms = dict(zip(WEIGHT_NAMES, (m_meta_tokens, m_norm_mix_g, m_w_in, m_conv_w, m_conv_b, m_conv_ln_g, m_conv_ln_b,
                                 m_gla_w_gate2, m_gla_gate_b, m_gla_norm_g, m_w_out, m_norm_ffn_g, m_w_ffn_gate,
                                 m_w_ffn_up, m_w_ffn_down, m_norm_final_g)))
    vs = dict(zip(WEIGHT_NAMES, (v_meta_tokens, v_norm_mix_g, v_w_in, v_conv_w, v_conv_b, v_conv_ln_g, v_conv_ln_b,
                                 v_gla_w_gate2, v_gla_gate_b, v_gla_norm_g, v_w_out, v_norm_ffn_g, v_w_ffn_gate,
                                 v_w_ffn_up, v_w_ffn_down, v_norm_final_g)))
    c = lax.axis_index("c")

    full = _full_weights(ws)
    loss, grad_x, grads = _local_step(x, loss_target, full)
    loss = lax.psum(loss, ("x", "y", "c"))

    rs_names = ("w_in", "w_out", "w_ffn_gate", "w_ffn_up", "w_ffn_down", "small")
    by_owner = [grads["w_in_t"].reshape(N_CHIPS, D_IN // N_CHIPS, D), grads["w_out"].reshape(N_CHIPS, D // N_CHIPS, D),
                grads["w_ffn_gate_t"].reshape(N_CHIPS, D_FF // N_CHIPS, D),
                grads["w_ffn_up_t"].reshape(N_CHIPS, D_FF // N_CHIPS, D),
                grads["w_ffn_down"].reshape(N_CHIPS, D_FF // N_CHIPS, D), _pack_small_sharded(grads)]
    from_sibling = _rs_to_sibling(by_owner)
    chip_sums = [_rs_add_halves(g, r, c, "rs_add_" + nm) for g, r, nm in zip(by_owner, from_sibling, rs_names)]
    halves = [_rs_sum_chips(p, "rs_sum_" + nm) for p, nm in zip(_rs_chip_exchange(chip_sums), rs_names)]
    reduced = dict(zip(rs_names, _rs_share(halves)))
    g_sharded = {"w_in": reduced["w_in"].T, "w_out": reduced["w_out"], "w_ffn_gate": reduced["w_ffn_gate"].T,
                 "w_ffn_up": reduced["w_ffn_up"].T, "w_ffn_down": reduced["w_ffn_down"],
                 **_unpack_small_sharded(reduced["small"])}
    out = {"grad": {}, "delta": {}, "new_m": {}, "new_v": {}}
    for name, g in g_sharded.items():
        shape = ws[name].shape
        flat = lambda a: a.reshape(shape[-2:])
        delta, new_m, new_v = _adamw(g, flat(ws[name]), flat(ms[name]), flat(vs[name]), "adamw_" + name)
        for kind, a in (("grad", g), ("delta", delta), ("new_m", new_m), ("new_v", new_v)):
            out[kind][name] = a.reshape(shape)

    small_shapes = {name: ws[name].shape for name, _, _, _ in SMALL_PARTS}
    g_s, d_s, m_s, v_s = _allreduce_small_adamw(_pack_small(grads), _pack_small(ws), _pack_small(ms), _pack_small(vs))
    for kind, slab in (("grad", g_s), ("delta", d_s), ("new_m", m_s), ("new_v", v_s)):
        out[kind].update(_unpack_small(slab, small_shapes))

    return (loss, grad_x, *[out[kind][name] for kind in ("grad", "delta", "new_m", "new_v") for name in WEIGHT_NAMES])


def _gather_plan(split, whole=(), axes=None):
    split, whole = list(split), list(whole)
    ns, n = len(split), len(split) + len(whole)

    def make(ins, outs, sems):
        ici_send, ici_recv, d2d_send, d2d_recv, own_send, own_recv = sems
        x, y, c = _mesh_pos()
        mine = 2 * x + y
        chips = _other_chips(x, y)
        blocks = [2 * px + py for px, py in chips]

        def own(a):
            return _remote(ins[a], outs[a].at[mine], own_send.at[a], own_recv.at[a], (x, y, 1 - c))

        def ici(a, k, block):
            px, py = chips[k]
            src, dst = ins[a], outs[a].at[block]
            if a < ns:
                src, dst = _half(src, c, axes[a]), _half(dst, c, axes[a])
            return _remote(src, dst, ici_send.at[3 * a + k], ici_recv.at[3 * a + k], (px, py, c))

        def d2d(a, k, half):
            part = _half(outs[a].at[blocks[k]], half, axes[a])
            return _remote(part, part, d2d_send.at[3 * a + k], d2d_recv.at[3 * a + k], (x, y, 1 - c))

        def start():
            for a in range(n):
                for k in range(3):
                    ici(a, k, mine).start()
                own(a).start()

        def finish():
            for a in range(n):
                for k in range(3):
                    ici(a, k, blocks[k]).wait_recv()
                    if a < ns:
                        d2d(a, k, c).start()
            for a in range(ns):
                for k in range(3):
                    d2d(a, k, 1 - c).wait_recv()
            for a in range(n):
                for k in range(3):
                    ici(a, k, mine).wait_send()
                    if a < ns:
                        d2d(a, k, c).wait_send()
                own(a).wait()

        return start, finish

    arrays = split + whole
    axes = [0] * ns if axes is None else list(axes)
    return _Plan(arrays, [jax.ShapeDtypeStruct((N_CHIPS,) + s.shape, s.dtype) for s in arrays],
                 [pltpu.SemaphoreType.DMA((3 * n,)), pltpu.SemaphoreType.DMA((3 * n,)),
                  pltpu.SemaphoreType.DMA((3 * ns,)), pltpu.SemaphoreType.DMA((3 * ns,)),
                  pltpu.SemaphoreType.DMA((n,)), pltpu.SemaphoreType.DMA((n,))], make)


def _to_sibling_plan(gs):
    n = len(gs)

    def make(ins, outs, sems):
        send_sems, recv_sems = sems
        x, y, c = _mesh_pos()

        def copy(a):
            return _remote(_half(ins[a], 1 - c, 2), outs[a], send_sems.at[a], recv_sems.at[a], (x, y, 1 - c))

        def start():
            for a in range(n):
                copy(a).start()

        def finish():
            for a in range(n):
                copy(a).wait()

        return start, finish

    return _Plan(list(gs), [jax.ShapeDtypeStruct(g.shape[:2] + (g.shape[2] // 2,), g.dtype) for g in gs],
                 [pltpu.SemaphoreType.DMA((n,)), pltpu.SemaphoreType.DMA((n,))], make)


def _chip_exchange_plan(ps):
    n = len(ps)

    def make(ins, outs, sems):
        send_sems, recv_sems = sems
        x, y, c = _mesh_pos()
        chips = _other_chips(x, y)

        def ici(a, k):
            px, py = chips[k]
            return _remote(ins[a].at[2 * px + py], outs[a].at[k], send_sems.at[3 * a + k],
                           recv_sems.at[3 * a + k], (px, py, c))

        def start():
            for a in range(n):
                for k in range(3):
                    ici(a, k).start()

        def finish():
            for a in range(n):
                for k in range(3):
                    ici(a, k).wait()

        return start, finish

    return _Plan(list(ps), [jax.ShapeDtypeStruct((3,) + p.shape[1:], p.dtype) for p in ps],
                 [pltpu.SemaphoreType.DMA((3 * n,)), pltpu.SemaphoreType.DMA((3 * n,))], make)


def _share_plan(halves):
    n = len(halves)

    def make(ins, outs, sems):
        send_sems, recv_sems = sems
        x, y, c = _mesh_pos()

        def d2d(a):
            return _remote(ins[a], outs[a], send_sems.at[a], recv_sems.at[a], (x, y, 1 - c))

        def start():
            for a in range(n):
                d2d(a).start()

        def finish():
            for a in range(n):
                d2d(a).wait()

        return start, finish

    return _Plan(list(halves), [jax.ShapeDtypeStruct(p.shape, p.dtype) for p in halves],
                 [pltpu.SemaphoreType.DMA((n,)), pltpu.SemaphoreType.DMA((n,))], make)


def _rs_sum(own, others, mine, name):
    _, rows, h = own.shape
    tr = rows // 2 if rows % 16 == 0 and rows > 64 else rows

    def body(mine_ref, own_ref, oth_ref, o_ref):
        p = oth_ref[...].astype(F32)
        o_ref[...] = ((own_ref[0].astype(F32) + p[0]) + p[1]) + p[2]

    return pl.pallas_call(
        body, name=name,
        grid_spec=pltpu.PrefetchScalarGridSpec(
            num_scalar_prefetch=1, grid=(rows // tr,),
            in_specs=[pl.BlockSpec((1, tr, h), lambda i, s: (s[0], i, 0)),
                      pl.BlockSpec((3, tr, h), lambda i, s: (0, i, 0))],
            out_specs=pl.BlockSpec((tr, h), lambda i, s: (i, 0))),
        out_shape=jax.ShapeDtypeStruct((rows, h), F32),
        compiler_params=_params(1),
    )(jnp.reshape(mine, (1,)).astype(jnp.int32), own, others)


def _join(mine, theirs, c):
    return jnp.where(c == 0, jnp.concatenate([mine, theirs], axis=1), jnp.concatenate([theirs, mine], axis=1))


LOSS_ROW = 5


def _merge_plans(a, b):
    na_in, na_out, na_sems = len(a.arrays), len(a.out_shape), len(a.sems)

    def make(ins, outs, sems):
        start_a, finish_a = a.make(ins[:na_in], outs[:na_out], sems[:na_sems])
        start_b, finish_b = b.make(ins[na_in:], outs[na_out:], sems[na_sems:])

        def start():
            start_a()
            start_b()

        def finish():
            finish_a()
            finish_b()

        return start, finish

    return _Plan(list(a.arrays) + list(b.arrays), list(a.out_shape) + list(b.out_shape),
                 list(a.sems) + list(b.sems), make)


def _exchange(plan, name):
    n_in, n_out = len(plan.arrays), len(plan.out_shape)

    def body(*refs):
        start, finish = plan.make(refs[:n_in], refs[n_in:n_in + n_out], refs[n_in + n_out:])
        start()
        finish()

    return pl.pallas_call(
        body, name=name, in_specs=[HBM_SPEC] * n_in, out_specs=[HBM_SPEC] * n_out, out_shape=list(plan.out_shape),
        scratch_shapes=list(plan.sems), compiler_params=pltpu.CompilerParams(has_side_effects=True),
    )(*plan.arrays)


def _adamw_halves(mine, theirs, c, w, m, v, name):
    rows, h = mine.shape
    tr = rows // 2 if rows % 16 == 0 else rows

    def body(c_ref, a_ref, b_ref, w_ref, m_ref, v_ref, go_ref, d_ref, m2_ref, v2_ref):
        g = jnp.where(pl.program_id(1) == c_ref[0], a_ref[...], b_ref[...])
        go_ref[...] = g
        d_ref[...], m2_ref[...], v2_ref[...] = _adam_update(g, w_ref[...], m_ref[...], v_ref[...])

    half = pl.BlockSpec((tr, h), lambda i, j, s: (i, 0))
    spec = pl.BlockSpec((tr, h), lambda i, j, s: (i, j))
    return pl.pallas_call(
        body, name=name,
        grid_spec=pltpu.PrefetchScalarGridSpec(num_scalar_prefetch=1, grid=(rows // tr, 2),
                                               in_specs=[half, half, spec, spec, spec], out_specs=[spec] * 4),
        out_shape=[jax.ShapeDtypeStruct((rows, 2 * h), F32)] * 4,
        compiler_params=_params(2),
    )(jnp.reshape(c, (1,)).astype(jnp.int32), mine, theirs, w, m, v)


def _columns(gathered):
    return jnp.concatenate([gathered[j] for j in range(N_CHIPS)], axis=1)


def kernel(x, meta_tokens, norm_mix_g, w_in, conv_w, conv_b, conv_ln_g, conv_ln_b, gla_w_gate2, gla_gate_b, gla_norm_g, w_out, norm_ffn_g, w_ffn_gate, w_ffn_up, w_ffn_down, norm_final_g, loss_target, m_meta_tokens, m_norm_mix_g, m_w_in, m_conv_w, m_conv_b, m_conv_ln_g, m_conv_ln_b, m_gla_w_gate2, m_gla_gate_b, m_gla_norm_g, m_w_out, m_norm_ffn_g, m_w_ffn_gate, m_w_ffn_up, m_w_ffn_down, m_norm_final_g, v_meta_tokens, v_norm_mix_g, v_w_in, v_conv_w, v_conv_b, v_conv_ln_g, v_conv_ln_b, v_gla_w_gate2, v_gla_gate_b, v_gla_norm_g, v_w_out, v_norm_ffn_g, v_w_ffn_gate, v_w_ffn_up, v_w_ffn_down, v_norm_final_g):
    ws = dict(zip(WEIGHT_NAMES, (meta_tokens, norm_mix_g, w_in, conv_w, conv_b, conv_ln_g, conv_ln_b, gla_w_gate2,
                                 gla_gate_b, gla_norm_g, w_out, norm_ffn_g, w_ffn_gate, w_ffn_up, w_ffn_down,
                                 norm_final_g)))
    ms = dict(zip(WEIGHT_NAMES, (m_meta_tokens, m_norm_mix_g, m_w_in, m_conv_w, m_conv_b, m_conv_ln_g, m_conv_ln_b,
                                 m_gla_w_gate2, m_gla_gate_b, m_gla_norm_g, m_w_out, m_norm_ffn_g, m_w_ffn_gate,
                                 m_w_ffn_up, m_w_ffn_down, m_norm_final_g)))
    vs = dict(zip(WEIGHT_NAMES, (v_meta_tokens, v_norm_mix_g, v_w_in, v_conv_w, v_conv_b, v_conv_ln_g, v_conv_ln_b,
                                 v_gla_w_gate2, v_gla_gate_b, v_gla_norm_g, v_w_out, v_norm_ffn_g, v_w_ffn_gate,
                                 v_w_ffn_up, v_w_ffn_down, v_norm_final_g)))
    c = lax.axis_index("c")
    shard = lambda d, name: d[name].reshape(d[name].shape[-2:])
    vec = {name: ws[name].reshape(1, -1) for name, _, _, _ in SMALL_PARTS}
    n_ex, seq, _ = x.shape
    lp = HEAD_ROWS + seq
    t = n_ex * lp

    (tgt,), (w_in_g, meta_g, conv_w_g, w2_g) = _pad_head_rows(loss_target, plan=_gather_plan(
        [shard(ws, "w_in").T.astype(BF16)],
        [shard(ws, "meta_tokens"), shard(ws, "conv_w"), shard(ws, "gla_w_gate2")], axes=[1]))
    w_in_t = jnp.concatenate([w_in_g.reshape(D_IN, D), jnp.zeros((D_IN_PAD - D_IN, D), BF16)], axis=0)
    w_in_full = w_in_t.T
    conv_w_full = jnp.concatenate([_columns(conv_w_g), jnp.zeros((32 - CONV_W, C_CONV), F32)], axis=0)
    w2_full = jnp.concatenate([_columns(w2_g), jnp.zeros((128 - RANK, GLA_K), F32)], axis=0).astype(BF16)

    meta = jnp.broadcast_to(_columns(meta_g)[None], (n_ex, N_META, D))
    h0 = jnp.concatenate([jnp.zeros((n_ex, PAD_ROWS, D), F32), meta, x], axis=1).reshape(t, D)
    tgt = tgt.reshape(t, D)
    row_mask = jnp.concatenate([jnp.zeros((n_ex, HEAD_ROWS, 1), F32), jnp.ones((n_ex, seq, 1), F32)],
                               axis=1).reshape(t, 1)

    (u, hn), (w_out_g,) = _in_proj(h0, vec["norm_mix_g"], w_in_full,
                                   plan=_gather_plan([shard(ws, "w_out").astype(BF16)]))
    (yc, y_conv), (gate_g,) = _conv_fwd(
        u, conv_w_full, vec["conv_b"], vec["conv_ln_g"], vec["conv_ln_b"], n_ex, lp,
        plan=_gather_plan([shard(ws, "w_ffn_gate").T.astype(BF16)]))
    (y_gla, states), (up_g,) = _gla_fwd(u, w2_full, vec["gla_gate_b"], vec["gla_norm_g"], n_ex, lp,
                                        plan=_gather_plan([shard(ws, "w_ffn_up").T.astype(BF16)]))
    w_out_full = w_out_g.reshape(D, D)
    w_gate_t, w_up_t = gate_g.reshape(D_FF, D), up_g.reshape(D_FF, D)

    (h1, hn2, gate, up, act), (down_g,) = _mix_out_ffn_up(
        h0, y_conv, y_gla, w_out_full, vec["norm_ffn_g"], w_gate_t.T, w_up_t.T,
        plan=_gather_plan([shard(ws, "w_ffn_down").astype(BF16)]))
    w_down_full = down_g.reshape(D_FF, D)
    dh2, loss, d_final_g = _ffn_down_loss(act, w_down_full, h1, tgt, vec["norm_final_g"], row_mask)
    dgate, dup, dh1, dycat, d_ffn_g = _ffn_bwd(dh2, gate, up, h1, w_down_full.T, w_gate_t, w_up_t, w_out_full.T,
                                                vec["norm_ffn_g"])

    early = ("w_ffn_gate", "w_ffn_up", "w_ffn_down", "w_out")
    ffn_block = lambda g: g.reshape(N_CHIPS, D_FF // N_CHIPS, D)
    g_gate = ffn_block(_wgrad(dgate, hn2, "wgrad_gate"))
    g_up, (gate_sib,) = _wgrad_hosting(dup, hn2, "wgrad_up", _to_sibling_plan([g_gate]))
    g_up = ffn_block(g_up)
    g_down, (up_sib,) = _wgrad_hosting(act, dh2, "wgrad_down", _to_sibling_plan([g_up]))
    g_down = ffn_block(g_down)
    g_out_conv, (down_sib,) = _wgrad_hosting(y_conv, dh1, "wgrad_out_conv", _to_sibling_plan([g_down]))
    g_out = jnp.concatenate([g_out_conv, _wgrad(y_gla, dh1, "wgrad_out_gla")], axis=0).reshape(
        N_CHIPS, D // N_CHIPS, D)
    cs_gate = _rs_add_halves(g_gate, gate_sib, c, "rs_add_w_ffn_gate")
    cs_up = _rs_add_halves(g_up, up_sib, c, "rs_add_w_ffn_up")
    (du_conv, d_conv_w, d_conv_b, d_ln_g, d_ln_b), (ex_gate, ex_up, out_sib) = _conv_bwd(
        dycat, yc, u, conv_w_full, vec["conv_ln_g"], vec["conv_ln_b"], n_ex, lp,
        plan=_merge_plans(_chip_exchange_plan([cs_gate, cs_up]), _to_sibling_plan([g_out])))
    cs_down = _rs_add_halves(g_down, down_sib, c, "rs_add_w_ffn_down")
    cs_out = _rs_add_halves(g_out, out_sib, c, "rs_add_w_out")
    (du_gla, d_w2, d_gate_b, d_norm_g), (ex_down, ex_out) = _gla_bwd(
        dycat, u, states, w2_full, vec["gla_gate_b"], vec["gla_norm_g"], n_ex, lp,
        plan=_chip_exchange_plan([cs_down, cs_out]))
    mine = 2 * lax.axis_index("x") + lax.axis_index("y")
    halves = [_rs_sum(own, oth, mine, "rs_sum_" + nm)
              for own, oth, nm in zip((cs_gate, cs_up, cs_down, cs_out), (ex_gate, ex_up, ex_down, ex_out), early)]

    d_w_in_t = jnp.concatenate([_wgrad(du_conv, hn, "wgrad_in_conv"), _wgrad(du_gla, hn, "wgrad_in_gla")],
                               axis=0)[:D_IN].reshape(N_CHIPS, D_IN // N_CHIPS, D)
    (in_from_sibling,) = _exchange(_to_sibling_plan([d_w_in_t]), "rs_late_to_sibling")
    in_chip_sum = _rs_add_halves(d_w_in_t, in_from_sibling, c, "rs_add_w_in")
    (dh0, d_mix_g), shared = _in_proj_bwd(
        du_conv, du_gla, w_in_t[:2 * C_CONV], w_in_t[2 * C_CONV:], h0, dh1, vec["norm_mix_g"],
        plan=_merge_plans(_share_plan(halves), _chip_exchange_plan([in_chip_sum])))
    dh0 = dh0.reshape(n_ex, lp, D)
    grad_x = dh0[:, HEAD_ROWS:]

    out = {"grad": {}, "delta": {}, "new_m": {}, "new_v": {}}

    def update(name, g=None, halves=None, transposed=False):
        shape = ws[name].shape
        lay = (lambda a: a.T) if transposed else (lambda a: a)
        w2d, m2d, v2d = lay(shard(ws, name)), lay(shard(ms, name)), lay(shard(vs, name))
        if halves is not None:
            res = _adamw_halves(*halves, c, w2d, m2d, v2d, "adamw_" + name)
        else:
            res = [g, *_adamw(g, w2d, m2d, v2d, "adamw_" + name)]
        for kind, a in zip(("grad", "delta", "new_m", "new_v"), res):
            out[kind][name] = lay(a).reshape(shape)

    update("w_ffn_gate", halves=(halves[0], shared[0]), transposed=True)
    update("w_ffn_up", halves=(halves[1], shared[1]), transposed=True)
    update("w_ffn_down", halves=(halves[2], shared[2]))
    update("w_out", halves=(halves[3], shared[3]))

    in_half = _rs_sum(in_chip_sum, shared[4], mine, "rs_sum_w_in")
    (in_shared,) = _exchange(_share_plan([in_half]), "rs_late_share")
    update("w_in", halves=(in_half, in_shared), transposed=True)

    small = {"norm_mix_g": d_mix_g, "norm_ffn_g": d_ffn_g, "norm_final_g": d_final_g, "conv_b": d_conv_b,
             "conv_ln_g": d_ln_g, "conv_ln_b": d_ln_b, "gla_gate_b": d_gate_b, "gla_norm_g": d_norm_g}
    small_shapes = {name: ws[name].shape for name, _, _, _ in SMALL_PARTS}
    part = lax.dynamic_update_slice(_pack_small(small), loss[:, :1], (LOSS_ROW, 0))
    part = jnp.concatenate([part, jnp.sum(dh0[:, PAD_ROWS:HEAD_ROWS], axis=0), d_conv_w.reshape(16, D),
                            d_w2[:RANK].reshape(4, D), jnp.zeros((4, D), F32)], axis=0)
    tall = lambda a: jnp.concatenate([a, jnp.zeros((part.shape[0] - SMALL_ROWS, D), F32)], axis=0)
    g_s, d_s, m_s, v_s = _allreduce_small_adamw(part, tall(_pack_small(ws)), tall(_pack_small(ms)),
                                                tall(_pack_small(vs)))
    for kind, slab in (("grad", g_s), ("delta", d_s), ("new_m", m_s), ("new_v", v_s)):
        out[kind].update(_unpack_small(slab, small_shapes))
    loss = g_s[LOSS_ROW, 0]
    block = lambda a, width: lax.dynamic_slice_in_dim(a, mine * width, width, axis=1)
    update("meta_tokens", g=block(g_s[8:24], D // N_CHIPS))
    update("conv_w", g=block(g_s[24:40].reshape(32, C_CONV), C_CONV // N_CHIPS)[:CONV_W])
    update("gla_w_gate2", g=block(g_s[40:44].reshape(RANK, GLA_K), GLA_K // N_CHIPS))

    return (loss, grad_x, *[out[kind][name] for kind in ("grad", "delta", "new_m", "new_v") for name in WEIGHT_NAMES])
```

```python
import functools
from typing import Any, Callable, NamedTuple, Sequence

import jax
import jax.numpy as jnp
from jax import lax
from jax.experimental import pallas as pl
from jax.experimental.pallas import tpu as pltpu

F32 = jnp.float32
BF16 = jnp.bfloat16
MESH = pl.DeviceIdType.MESH

D = 1024
N_META = 16
C_CONV = 512
CONV_W = 31
GLA_K = 256
GLA_V = 512
N_HEADS = 4
DK = 64
DV = 128
RANK = 16
CHUNK = 64
PAD_ROWS = CHUNK - N_META
HEAD_ROWS = CHUNK
D_IN = 2576
D_IN_PAD = 2688
D_GLA_IN = D_IN_PAD - 2 * C_CONV
D_FF = 2816
RMS_EPS = 1e-6
LN_EPS = 1e-5
GATE_TAU = 16.0
N_CHIPS = 4

ADAM_LR = 0.001
ADAM_B1 = 0.9
ADAM_B2 = 0.999
ADAM_EPS = 1e-08
ADAM_WD = 0.01
ADAM_STEP = 10

V7X_VMEM_BYTES = 64 * 1024 * 1024
VMEM_LIMIT = V7X_VMEM_BYTES - 8 * 1024 * 1024

SLAB_ROWS = 3072
HALF_ROWS = SLAB_ROWS // 2
SMALL_ROWS = 8


def _dot(a, b):
    return jnp.dot(a, b, preferred_element_type=F32)


def _dot_nt(a, b):
    return lax.dot_general(a, b, (((1,), (1,)), ((), ())), preferred_element_type=F32)


def _dot_tn(a, b):
    return lax.dot_general(a, b, (((0,), (0,)), ((), ())), preferred_element_type=F32)


def _sigmoid(x):
    return 1.0 / (1.0 + jnp.exp(-x))


def _const_spec(shape):
    return pl.BlockSpec(shape, lambda *_: (0,) * len(shape), pipeline_mode=pl.Buffered(1))


def _acc_spec(shape):
    return pl.BlockSpec(shape, lambda *_: (0,) * len(shape))


def _params(n_axes):
    return pltpu.CompilerParams(dimension_semantics=("arbitrary",) * n_axes, vmem_limit_bytes=VMEM_LIMIT)


def _row_tile(t, want):
    for r in (want, 384, 192, 128, 64):
        if r <= want and t % r == 0:
            return r
    raise ValueError(f"no row tile for {t}")


ROW_PART = 128


def _row_parts(r):
    if r % ROW_PART:
        return [slice(None)]
    return [pl.ds(i * ROW_PART, ROW_PART) for i in range(r // ROW_PART)]


def _in_lockstep(bodies):
    live = list(bodies)
    while live:
        still = []
        for g in live:
            try:
                next(g)
                still.append(g)
            except StopIteration:
                pass
        live = still


class _Plan(NamedTuple):
    arrays: Sequence[Any]
    out_shape: Sequence[Any]
    sems: Sequence[Any]
    make: Callable


def _call(body, *, name, grid, in_specs, out_specs, out_shape, scratch_shapes=(), plan=None):
    n_in, n_out, n_scr = len(in_specs), len(out_specs), len(scratch_shapes)
    if plan is None:
        plan = _Plan([], [], [], lambda ins, outs, sems: (lambda: None, lambda: None))
    nx_in, nx_out = len(plan.arrays), len(plan.out_shape)

    def hosted(*refs):
        ins, xins = refs[:n_in], refs[n_in:n_in + nx_in]
        o0 = n_in + nx_in
        outs, xouts = refs[o0:o0 + n_out], refs[o0 + n_out:o0 + n_out + nx_out]
        s0 = o0 + n_out + nx_out
        scr, sems = refs[s0:s0 + n_scr], refs[s0 + n_scr:]
        ids = [pl.program_id(a) for a in range(len(grid))]
        first = functools.reduce(jnp.logical_and, [i == 0 for i in ids])
        last = functools.reduce(jnp.logical_and, [i == g - 1 for i, g in zip(ids, grid)])
        start, finish = plan.make(xins, xouts, sems)
        pl.when(first)(start)
        body(*ins, *outs, *scr)
        pl.when(last)(finish)

    call = pl.pallas_call(
        hosted, name=name, grid=grid, in_specs=list(in_specs) + [HBM_SPEC] * nx_in,
        out_specs=list(out_specs) + [HBM_SPEC] * nx_out, out_shape=list(out_shape) + list(plan.out_shape),
        scratch_shapes=list(scratch_shapes) + list(plan.sems),
        compiler_params=pltpu.CompilerParams(dimension_semantics=("arbitrary",) * len(grid),
                                             vmem_limit_bytes=VMEM_LIMIT, has_side_effects=nx_in > 0))

    def run(*args):
        res = call(*args, *plan.arrays)
        return res[:n_out], res[n_out:]

    return run


def _pad_head_rows(a, plan=None):
    n_ex, seq, _ = a.shape
    nc = (HEAD_ROWS + seq) // CHUNK

    def body(a_ref, o_ref):
        o_ref[...] = jnp.where(pl.program_id(0) > 0, a_ref[...], 0.0)

    return _call(
        body, name="pad_head_rows", grid=(nc,),
        in_specs=[pl.BlockSpec((n_ex, CHUNK, D), lambda n: (0, jnp.maximum(n - 1, 0), 0))],
        out_specs=[pl.BlockSpec((n_ex, CHUNK, D), lambda n: (0, n, 0))],
        out_shape=[jax.ShapeDtypeStruct((n_ex, HEAD_ROWS + seq, D), F32)],
        plan=plan,
    )(a)


def _in_proj(h0, g_mix, w_in, plan=None):
    t = h0.shape[0]
    r = _row_tile(t, 384)

    def body(h_ref, g_ref, w_ref, u_ref, hn_ref):
        def part(rows):
            h = h_ref[rows, :]
            rstd = lax.rsqrt(jnp.mean(h * h, axis=-1, keepdims=True) + RMS_EPS)
            hn = (h * rstd * g_ref[...]).astype(BF16)
            hn_ref[rows, :] = hn
            yield
            u_ref[rows, :] = _dot(hn, w_ref[...])

        _in_lockstep(part(rows) for rows in _row_parts(r))

    return _call(
        body, name="in_proj", grid=(t // r,),
        in_specs=[pl.BlockSpec((r, D), lambda i: (i, 0)), _const_spec((1, D)), _const_spec((D, D_IN_PAD))],
        out_specs=[pl.BlockSpec((r, D_IN_PAD), lambda i: (i, 0)), pl.BlockSpec((r, D), lambda i: (i, 0))],
        out_shape=[jax.ShapeDtypeStruct((t, D_IN_PAD), F32), jax.ShapeDtypeStruct((t, D), BF16)],
        plan=plan,
    )(h0, g_mix, w_in)


CONV_TILE = 192
CONV_SUB = 32
CONV_LEAD = CONV_SUB - (CONV_W - 1)
SUBLANES = 8


def _shifted_copies(src, dst, r):
    for s in range(1, SUBLANES):
        dst[s - 1] = src[s:s + r + CONV_SUB - SUBLANES, :]


def _shifted_rows(src, shifted, start):
    base, s = SUBLANES * (start // SUBLANES), start % SUBLANES
    if s == 0:
        return src[base:base + CONV_SUB, :]
    return shifted[s - 1, base:base + CONV_SUB, :]


def _conv_fwd(u, conv_w, conv_b, ln_g, ln_b, n_ex, lp, plan=None):
    r = CONV_TILE
    nt = lp // r
    hb = r // CONV_SUB

    def body(cur_ref, prev_ref, w_ref, b_ref, lg_ref, lb_ref, yc_ref, y_ref, glu, glu_sh):
        i = pl.program_id(1)
        cur = cur_ref[...]
        glu[CONV_SUB:CONV_SUB + r, :] = cur[:, :C_CONV] * _sigmoid(cur[:, C_CONV:])
        pv = prev_ref[...]
        halo = pv[:, :C_CONV] * _sigmoid(pv[:, C_CONV:])
        glu[0:CONV_SUB, :] = jnp.where(i > 0, halo, 0.0)
        _shifted_copies(glu, glu_sh, r)
        w = w_ref[...]
        for j in range(r // CONV_SUB):
            r0 = j * CONV_SUB
            acc = jnp.zeros((CONV_SUB, C_CONV), F32) + b_ref[...]
            for k in range(CONV_W):
                acc = acc + w[k:k + 1, :] * _shifted_rows(glu, glu_sh, r0 + CONV_LEAD + k)
            mu = jnp.mean(acc, axis=-1, keepdims=True)
            cen = acc - mu
            var = jnp.mean(cen * cen, axis=-1, keepdims=True)
            out = cen * lax.rsqrt(var + LN_EPS) * lg_ref[...] + lb_ref[...]
            y = out * _sigmoid(out)
            row = i * r + r0 + lax.broadcasted_iota(jnp.int32, (CONV_SUB, 1), 0)
            y = jnp.where(row >= PAD_ROWS, y, 0.0)
            yc_ref[r0:r0 + CONV_SUB, :] = acc
            y_ref[r0:r0 + CONV_SUB, :] = y.astype(BF16)

    t = n_ex * lp
    return _call(
        body, name="conv_fwd", grid=(n_ex, nt),
        in_specs=[pl.BlockSpec((r, 2 * C_CONV), lambda b, i: (b * nt + i, 0)),
                  pl.BlockSpec((CONV_SUB, 2 * C_CONV), lambda b, i: (jnp.maximum((b * nt + i) * hb - 1, 0), 0)),
                  _const_spec((32, C_CONV)), _const_spec((1, C_CONV)), _const_spec((1, C_CONV)), _const_spec((1, C_CONV))],
        out_specs=[pl.BlockSpec((r, C_CONV), lambda b, i: (b * nt + i, 0)),
                   pl.BlockSpec((r, C_CONV), lambda b, i: (b * nt + i, 0))],
        out_shape=[jax.ShapeDtypeStruct((t, C_CONV), F32), jax.ShapeDtypeStruct((t, C_CONV), BF16)],
        scratch_shapes=[pltpu.VMEM((r + CONV_SUB, C_CONV), F32),
                        pltpu.VMEM((SUBLANES - 1, r + CONV_SUB - SUBLANES, C_CONV), F32)],
        plan=plan,
    )(u, u, conv_w, conv_b, ln_g, ln_b)


def _gla_gates(lr, w2, gb, first_chunk):
    z = _dot(lr.astype(BF16), w2) + gb
    a = (jnp.minimum(z, 0.0) - jnp.log(1.0 + jnp.exp(-jnp.abs(z)))) * (1.0 / GATE_TAU)
    row = lax.broadcasted_iota(jnp.int32, (CHUNK, 1), 0)
    live = jnp.logical_or(jnp.logical_not(first_chunk), row >= PAD_ROWS)
    return z, jnp.where(live, a, 0.0), live


def _tri(lower):
    i = lax.broadcasted_iota(jnp.int32, (CHUNK, CHUNK), 0)
    j = lax.broadcasted_iota(jnp.int32, (CHUNK, CHUNK), 1)
    return (i >= j) if lower else (i <= j)


def _gla_fwd_per_head(u, w2, gb, ng, n_ex, lp, plan=None):
    nc = lp // CHUNK
    t = n_ex * lp

    def body(qk_ref, v_ref, g_ref, lr_ref, w2_ref, gb_ref, ng_ref, y_ref, st_ref, state):
        n = pl.program_id(0)

        @pl.when(n == 0)
        def _():
            state[...] = jnp.zeros_like(state)

        causal = _tri(True)
        for e in range(n_ex):
            st = state[e]
            st_ref[e] = st
            qk = qk_ref[e]
            q, k = qk[:, :GLA_K], qk[:, GLA_K:]
            _, a, _ = _gla_gates(lr_ref[e], w2_ref[...], gb_ref[...], n == 0)
            b = jnp.dot(causal.astype(F32), a, preferred_element_type=F32, precision=lax.Precision.HIGHEST)
            bl = b[CHUNK - 1:CHUNK, :]
            q_in = (q * (DK ** -0.5) * jnp.exp(b)).astype(BF16)
            k_in = (k * jnp.exp(-b)).astype(BF16)
            k_dec = (k * jnp.exp(bl - b)).astype(BF16)
            decay = jnp.exp(bl)
            v = v_ref[e]
            g = g_ref[e]
            st_b = st.astype(BF16)
            ys, new = [], []
            for h in range(N_HEADS):
                ks = slice(h * DK, (h + 1) * DK)
                vs = slice(h * DV, (h + 1) * DV)
                vh = v[:, vs].astype(BF16)
                s = jnp.where(causal, _dot_nt(q_in[:, ks], k_in[:, ks]), 0.0)
                o = _dot(s.astype(BF16), vh) + _dot_nt(q_in[:, ks], st_b[:, ks])
                new.append(decay[:, ks] * st[:, ks] + _dot_tn(vh, k_dec[:, ks]))
                rstd = lax.rsqrt(jnp.mean(o * o, axis=-1, keepdims=True) + RMS_EPS)
                gh = g[:, vs]
                ys.append(o * rstd * ng_ref[...] * (gh * _sigmoid(gh)))
            state[e] = jnp.concatenate(new, axis=1)
            y_ref[e] = jnp.concatenate(ys, axis=1).astype(BF16)

    u3 = u.reshape(n_ex, lp, D_IN_PAD)
    blk = lambda w, col: pl.BlockSpec((n_ex, CHUNK, w), lambda n: (0, n, col))
    (y, states), extra = _call(
        body, name="gla_fwd", grid=(nc,),
        in_specs=[blk(2 * GLA_K, 2), blk(GLA_V, 3), blk(GLA_V, 4), blk(128, 20),
                  _const_spec((128, GLA_K)), _const_spec((1, GLA_K)), _const_spec((1, DV))],
        out_specs=[blk(GLA_V, 0), pl.BlockSpec((n_ex, DV, GLA_K), lambda n: (0, n, 0))],
        out_shape=[jax.ShapeDtypeStruct((n_ex, lp, GLA_V), BF16),
                   jax.ShapeDtypeStruct((n_ex, nc * DV, GLA_K), F32)],
        scratch_shapes=[pltpu.VMEM((n_ex, DV, GLA_K), F32)],
        plan=plan,
    )(u3, u3, u3, u3, w2, gb, ng)
    return (y.reshape(t, GLA_V), states), extra


FFN_TILE = 192


def _mix_out_ffn_up(h0, y_conv, y_gla, w_out, g_ffn, w_gate, w_up, plan=None):
    t = h0.shape[0]
    r = _row_tile(t, 384)

    def body(h0_ref, yc_ref, yg_ref, wo_ref, g_ref, wg_ref, wu_ref, h1_ref, hn_ref, gate_ref, up_ref, act_ref):
        h1 = h0_ref[...] + _dot(yc_ref[...], wo_ref[0:C_CONV, :]) + _dot(yg_ref[...], wo_ref[C_CONV:D, :])
        h1_ref[...] = h1
        rstd = lax.rsqrt(jnp.mean(h1 * h1, axis=-1, keepdims=True) + RMS_EPS)
        hn = (h1 * rstd * g_ref[...]).astype(BF16)
        hn_ref[...] = hn
        gate = _dot(hn, wg_ref[...])
        up = _dot(hn, wu_ref[...])
        gate_ref[...] = gate
        up_ref[...] = up
        act_ref[...] = (gate * _sigmoid(gate) * up).astype(BF16)

    rows = lambda w: pl.BlockSpec((r, w), lambda i: (i, 0))
    return _call(
        body, name="mix_out_ffn_up", grid=(t // r,),
        in_specs=[rows(D), rows(C_CONV), rows(GLA_V), _const_spec((D, D)), _const_spec((1, D)),
                  _const_spec((D, D_FF)), _const_spec((D, D_FF))],
        out_specs=[rows(D), rows(D), rows(D_FF), rows(D_FF), rows(D_FF)],
        out_shape=[jax.ShapeDtypeStruct((t, D), F32), jax.ShapeDtypeStruct((t, D), BF16),
                   jax.ShapeDtypeStruct((t, D_FF), F32), jax.ShapeDtypeStruct((t, D_FF), F32),
                   jax.ShapeDtypeStruct((t, D_FF), BF16)],
        plan=plan,
    )(h0, y_conv, y_gla, w_out, g_ffn, w_gate, w_up)


def _ffn_down_loss(act, w_down, h1, target, g_final, row_mask):
    t = h1.shape[0]
    r = _row_tile(t, 384)

    def body(act_ref, wd_ref, h1_ref, tgt_ref, gf_ref, mask_ref, dh2_ref, loss_ref, dgf_ref):
        @pl.when(pl.program_id(0) == 0)
        def _():
            loss_ref[...] = jnp.zeros_like(loss_ref)
            dgf_ref[...] = jnp.zeros_like(dgf_ref)

        gf = gf_ref[...]

        def part(rows):
            h2 = h1_ref[rows, :] + _dot(act_ref[rows, :], wd_ref[...])
            yield
            rstd = lax.rsqrt(jnp.mean(h2 * h2, axis=-1, keepdims=True) + RMS_EPS)
            nrm = h2 * rstd
            err = (nrm * gf - tgt_ref[rows, :]) * mask_ref[rows, :]
            loss_ref[...] += jnp.sum(err * err) * (0.5 / D)
            dy = err * (1.0 / D)
            dgf_ref[...] += jnp.sum(dy * nrm, axis=0, keepdims=True)
            dn = dy * gf
            dh2_ref[rows, :] = rstd * (dn - nrm * jnp.mean(dn * nrm, axis=-1, keepdims=True))

        _in_lockstep(part(rows) for rows in _row_parts(r))

    rows = lambda w: pl.BlockSpec((r, w), lambda i: (i, 0))
    return pl.pallas_call(
        body, name="ffn_down_loss", grid=(t // r,),
        in_specs=[rows(D_FF), _const_spec((D_FF, D)), rows(D), rows(D), _const_spec((1, D)), rows(1)],
        out_specs=[rows(D), _acc_spec((1, 128)), _acc_spec((1, D))],
        out_shape=[jax.ShapeDtypeStruct((t, D), F32), jax.ShapeDtypeStruct((1, 128), F32),
                   jax.ShapeDtypeStruct((1, D), F32)],
        compiler_params=_params(1),
    )(act, w_down, h1, target, g_final, row_mask)


def _ffn_bwd(dh2, gate, up, h1, w_down_t, w_gate_t, w_up_t, w_out_t, g_ffn):
    t = h1.shape[0]
    r = _row_tile(t, FFN_TILE)

    def body(dh2_ref, gate_ref, up_ref, h1_ref, wd_ref, wg_ref, wu_ref, wo_ref, g_ref,
             dgate_ref, dup_ref, dh1_ref, dycat_ref, dg_ref):
        @pl.when(pl.program_id(0) == 0)
        def _():
            dg_ref[...] = jnp.zeros_like(dg_ref)

        dh2 = dh2_ref[...]
        dact = _dot(dh2.astype(BF16), wd_ref[...])
        gate = gate_ref[...]
        sg = _sigmoid(gate)
        dgate = (dact * up_ref[...] * (sg * (1.0 + gate * (1.0 - sg)))).astype(BF16)
        dup = (dact * (gate * sg)).astype(BF16)
        dgate_ref[...] = dgate
        dup_ref[...] = dup
        dhn = _dot(dgate, wg_ref[...]) + _dot(dup, wu_ref[...])
        h1 = h1_ref[...]
        rstd = lax.rsqrt(jnp.mean(h1 * h1, axis=-1, keepdims=True) + RMS_EPS)
        nrm = h1 * rstd
        dg_ref[...] += jnp.sum(dhn * nrm, axis=0, keepdims=True)
        dn = dhn * g_ref[...]
        dh1 = dh2 + rstd * (dn - nrm * jnp.mean(dn * nrm, axis=-1, keepdims=True))
        dh1_ref[...] = dh1
        dycat_ref[...] = _dot(dh1.astype(BF16), wo_ref[...])

    rows = lambda w: pl.BlockSpec((r, w), lambda i: (i, 0))
    return pl.pallas_call(
        body, name="ffn_bwd", grid=(t // r,),
        in_specs=[rows(D), rows(D_FF), rows(D_FF), rows(D), _const_spec((D, D_FF)), _const_spec((D_FF, D)),
                  _const_spec((D_FF, D)), _const_spec((D, D)), _const_spec((1, D))],
        out_specs=[rows(D_FF), rows(D_FF), rows(D), rows(D), _acc_spec((1, D))],
        out_shape=[jax.ShapeDtypeStruct((t, D_FF), BF16), jax.ShapeDtypeStruct((t, D_FF), BF16),
                   jax.ShapeDtypeStruct((t, D), F32), jax.ShapeDtypeStruct((t, D), F32),
                   jax.ShapeDtypeStruct((1, D), F32)],
        compiler_params=_params(1),
    )(dh2, gate, up, h1, w_down_t, w_gate_t, w_up_t, w_out_t, g_ffn)


def _conv_bwd(dycat, yc, u, conv_w, ln_g, ln_b, n_ex, lp, plan=None):
    r = CONV_TILE
    nt = lp // r
    hb = r // CONV_SUB
    nsub = r // CONV_SUB

    def ln_bwd(dy, yc_rows, live, lg, lb):
        mu = jnp.mean(yc_rows, axis=-1, keepdims=True)
        cen = yc_rows - mu
        rs = lax.rsqrt(jnp.mean(cen * cen, axis=-1, keepdims=True) + LN_EPS)
        yn = cen * rs
        out = yn * lg + lb
        so = _sigmoid(out)
        dout = jnp.where(live, dy * (so * (1.0 + out * (1.0 - so))), 0.0)
        dyn = dout * lg
        dyc = rs * (dyn - jnp.mean(dyn, axis=-1, keepdims=True) - yn * jnp.mean(dyn * yn, axis=-1, keepdims=True))
        return dyc, dout, yn

    def body(dy_ref, dyn_ref, yc_ref, ycn_ref, cur_ref, prev_ref, w_ref, lg_ref, lb_ref,
             du_ref, dw_ref, db_ref, dlg_ref, dlb_ref, glu, dycs, dwacc, glu_sh, dycs_sh):
        b = pl.program_id(0)
        i = pl.program_id(1)
        first = jnp.logical_and(b == 0, i == 0)

        @pl.when(first)
        def _():
            dwacc[...] = jnp.zeros_like(dwacc)
            db_ref[...] = jnp.zeros_like(db_ref)
            dlg_ref[...] = jnp.zeros_like(dlg_ref)
            dlb_ref[...] = jnp.zeros_like(dlb_ref)

        lg, lb = lg_ref[...], lb_ref[...]
        cur = cur_ref[...]
        sig = _sigmoid(cur[:, C_CONV:])
        glu[CONV_SUB:CONV_SUB + r, :] = cur[:, :C_CONV] * sig
        pv = prev_ref[...]
        glu[0:CONV_SUB, :] = jnp.where(i > 0, pv[:, :C_CONV] * _sigmoid(pv[:, C_CONV:]), 0.0)

        row = i * r + lax.broadcasted_iota(jnp.int32, (r, 1), 0)
        dyc, dout, yn = ln_bwd(dy_ref[...], yc_ref[...], row >= PAD_ROWS, lg, lb)
        dycs[0:r, :] = dyc
        dycn, _, _ = ln_bwd(dyn_ref[...], ycn_ref[...], i < nt - 1, lg, lb)
        dycs[r:r + CONV_SUB, :] = dycn
        db_ref[...] += jnp.sum(dyc, axis=0, keepdims=True)
        dlg_ref[...] += jnp.sum(dout * yn, axis=0, keepdims=True)
        dlb_ref[...] += jnp.sum(dout, axis=0, keepdims=True)

        _shifted_copies(glu, glu_sh, r)
        _shifted_copies(dycs, dycs_sh, r)
        w = w_ref[...]
        for j in range(nsub):
            r0 = j * CONV_SUB
            dblk = dycs[r0:r0 + CONV_SUB, :]
            dglu = jnp.zeros((CONV_SUB, C_CONV), F32)
            for k in range(CONV_W):
                dglu = dglu + w[k:k + 1, :] * _shifted_rows(dycs, dycs_sh, r0 + (CONV_W - 1) - k)
                prod = dblk * _shifted_rows(glu, glu_sh, r0 + CONV_LEAD + k)
                dwacc[k] += prod.reshape(CONV_SUB // SUBLANES, SUBLANES, C_CONV).sum(axis=0)
            sg = sig[r0:r0 + CONV_SUB, :]
            cv = cur[r0:r0 + CONV_SUB, :C_CONV]
            du_ref[r0:r0 + CONV_SUB, :C_CONV] = (dglu * sg).astype(BF16)
            du_ref[r0:r0 + CONV_SUB, C_CONV:] = (dglu * cv * sg * (1.0 - sg)).astype(BF16)

        @pl.when(jnp.logical_and(b == n_ex - 1, i == nt - 1))
        def _():
            dw_ref[...] = jnp.sum(dwacc[...], axis=1)

    t = n_ex * lp
    cur_rows = lambda w, col: pl.BlockSpec((r, w), lambda b, i: (b * nt + i, col))
    nxt_rows = lambda w, col: pl.BlockSpec(
        (CONV_SUB, w), lambda b, i: (jnp.minimum((b * nt + i + 1) * hb, n_ex * nt * hb - 1), col))
    return _call(
        body, name="conv_bwd", grid=(n_ex, nt),
        in_specs=[cur_rows(C_CONV, 0), nxt_rows(C_CONV, 0), cur_rows(C_CONV, 0), nxt_rows(C_CONV, 0),
                  cur_rows(2 * C_CONV, 0),
                  pl.BlockSpec((CONV_SUB, 2 * C_CONV), lambda b, i: (jnp.maximum((b * nt + i) * hb - 1, 0), 0)),
                  _const_spec((32, C_CONV)), _const_spec((1, C_CONV)), _const_spec((1, C_CONV))],
        out_specs=[cur_rows(2 * C_CONV, 0), _acc_spec((32, C_CONV)), _acc_spec((1, C_CONV)),
                   _acc_spec((1, C_CONV)), _acc_spec((1, C_CONV))],
        out_shape=[jax.ShapeDtypeStruct((t, 2 * C_CONV), BF16), jax.ShapeDtypeStruct((32, C_CONV), F32),
                   jax.ShapeDtypeStruct((1, C_CONV), F32), jax.ShapeDtypeStruct((1, C_CONV), F32),
                   jax.ShapeDtypeStruct((1, C_CONV), F32)],
        scratch_shapes=[pltpu.VMEM((r + CONV_SUB, C_CONV), F32), pltpu.VMEM((r + CONV_SUB, C_CONV), F32),
                        pltpu.VMEM((32, 8, C_CONV), F32),
                        pltpu.VMEM((SUBLANES - 1, r + CONV_SUB - SUBLANES, C_CONV), F32),
                        pltpu.VMEM((SUBLANES - 1, r + CONV_SUB - SUBLANES, C_CONV), F32)],
        plan=plan,
    )(dycat, dycat, yc, yc, u, u, conv_w, ln_g, ln_b)


def _gla_bwd_per_head(dycat, u, states, w2, gb, ng, n_ex, lp, plan=None):
    nc = lp // CHUNK
    t = n_ex * lp

    def body(dy_ref, qk_ref, v_ref, g_ref, lr_ref, st_ref, w2_ref, gb_ref, ng_ref,
             du_ref, dw2_ref, dgb_ref, dng_ref, dstate):
        n = pl.program_id(0)
        chunk = nc - 1 - n

        @pl.when(n == 0)
        def _():
            dw2_ref[...] = jnp.zeros_like(dw2_ref)
            dgb_ref[...] = jnp.zeros_like(dgb_ref)
            dng_ref[...] = jnp.zeros_like(dng_ref)
            dstate[...] = jnp.zeros_like(dstate)

        for e in range(n_ex):
            one_example(e, chunk, dy_ref, qk_ref, v_ref, g_ref, lr_ref, st_ref, w2_ref, gb_ref, ng_ref,
                        du_ref, dw2_ref, dgb_ref, dng_ref, dstate)

    def one_example(e, chunk, dy_ref, qk_ref, v_ref, g_ref, lr_ref, st_ref, w2_ref, gb_ref, ng_ref,
                    du_ref, dw2_ref, dgb_ref, dng_ref, dstate):
        dy_ref, qk_ref, v_ref, g_ref, lr_ref, st_ref = (r.at[e] for r in (dy_ref, qk_ref, v_ref, g_ref, lr_ref, st_ref))
        du_ref, dstate = du_ref.at[e], dstate.at[e]
        qk = qk_ref[...]
        q, k = qk[:, :GLA_K], qk[:, GLA_K:]
        lr = lr_ref[...]
        z, a, live = _gla_gates(lr, w2_ref[...], gb_ref[...], chunk == 0)
        causal = _tri(True)
        b = jnp.dot(causal.astype(F32), a, preferred_element_type=F32, precision=lax.Precision.HIGHEST)
        bl = b[CHUNK - 1:CHUNK, :]
        e_pos, e_neg, e_dec = jnp.exp(b), jnp.exp(-b), jnp.exp(bl - b)
        q_f = q * (DK ** -0.5) * e_pos
        k_f = k * e_neg
        kd_f = k * e_dec
        q_in, k_in, k_dec = q_f.astype(BF16), k_f.astype(BF16), kd_f.astype(BF16)
        decay = jnp.exp(bl)
        v = v_ref[...]
        g = g_ref[...]
        dy = dy_ref[...]
        ngv = ng_ref[...]
        st = st_ref[...]
        st_b = st.astype(BF16)
        dst = dstate[...]
        dst_b = dst.astype(BF16)
        dqs, dks, dvs, dgs, dbs, dbls, new_dst = [], [], [], [], [], [], []
        dng = jnp.zeros((1, DV), F32)
        for h in range(N_HEADS):
            ks = slice(h * DK, (h + 1) * DK)
            vs = slice(h * DV, (h + 1) * DV)
            qh, kh, kdh = q_in[:, ks], k_in[:, ks], k_dec[:, ks]
            vh = v[:, vs].astype(BF16)
            s = jnp.where(causal, _dot_nt(qh, kh), 0.0).astype(BF16)
            o = _dot(s, vh) + _dot_nt(qh, st_b[:, ks])
            rstd = lax.rsqrt(jnp.mean(o * o, axis=-1, keepdims=True) + RMS_EPS)
            nrm = o * rstd
            gh = g[:, vs]
            sg = _sigmoid(gh)
            dyh = dy[:, vs]
            dgs.append(dyh * nrm * ngv * (sg * (1.0 + gh * (1.0 - sg))))
            dt = dyh * (gh * sg)
            dng = dng + jnp.sum(dt * nrm, axis=0, keepdims=True)
            dn = dt * ngv
            do = (rstd * (dn - nrm * jnp.mean(dn * nrm, axis=-1, keepdims=True))).astype(BF16)
            da = jnp.where(causal, _dot_nt(do, vh), 0.0).astype(BF16)
            dvs.append(_dot_tn(s, do) + _dot_nt(kdh, dst_b[:, ks]))
            dq_in = _dot(da, kh) + _dot(do, st_b[:, ks])
            dk_in = _dot_tn(da, qh)
            dk_dec = _dot(vh, dst_b[:, ks])
            new_dst.append(_dot_tn(do, qh) + decay[:, ks] * dst[:, ks])
            dbls.append(jnp.sum(dk_dec * kd_f[:, ks], axis=0, keepdims=True)
                        + decay[:, ks] * jnp.sum(dst[:, ks] * st[:, ks], axis=0, keepdims=True))
            dqs.append(dq_in * (DK ** -0.5) * e_pos[:, ks])
            dks.append(dk_in * e_neg[:, ks] + dk_dec * e_dec[:, ks])
            dbs.append(dq_in * q_f[:, ks] - dk_in * k_f[:, ks] - dk_dec * kd_f[:, ks])
        dstate[...] = jnp.concatenate(new_dst, axis=1)
        row = lax.broadcasted_iota(jnp.int32, (CHUNK, 1), 0)
        db = jnp.concatenate(dbs, axis=1) + jnp.where(row == CHUNK - 1, jnp.concatenate(dbls, axis=1), 0.0)
        da_log = jnp.dot(_tri(False).astype(F32), db, preferred_element_type=F32, precision=lax.Precision.HIGHEST)
        dz = jnp.where(live, da_log * (1.0 - _sigmoid(z)) * (1.0 / GATE_TAU), 0.0)
        dz_b = dz.astype(BF16)
        du_ref[:, 0:GLA_K] = jnp.concatenate(dqs, axis=1).astype(BF16)
        du_ref[:, GLA_K:2 * GLA_K] = jnp.concatenate(dks, axis=1).astype(BF16)
        du_ref[:, 2 * GLA_K:2 * GLA_K + GLA_V] = jnp.concatenate(dvs, axis=1).astype(BF16)
        du_ref[:, 2 * GLA_K + GLA_V:2 * GLA_K + 2 * GLA_V] = jnp.concatenate(dgs, axis=1).astype(BF16)
        du_ref[:, 2 * GLA_K + 2 * GLA_V:] = _dot_nt(dz_b, w2_ref[...]).astype(BF16)
        dw2_ref[...] += _dot_tn(lr.astype(BF16), dz_b)
        dgb_ref[...] += jnp.sum(dz, axis=0, keepdims=True)
        dng_ref[...] += dng

    u3 = u.reshape(n_ex, lp, D_IN_PAD)
    rev = lambda w, col: pl.BlockSpec((n_ex, CHUNK, w), lambda n: (0, nc - 1 - n, col))
    (du, d_w2, d_gb, d_ng), extra = _call(
        body, name="gla_bwd", grid=(nc,),
        in_specs=[rev(GLA_V, 1), rev(2 * GLA_K, 2), rev(GLA_V, 3), rev(GLA_V, 4), rev(128, 20),
                  pl.BlockSpec((n_ex, DV, GLA_K), lambda n: (0, nc - 1 - n, 0)),
                  _const_spec((128, GLA_K)), _const_spec((1, GLA_K)), _const_spec((1, DV))],
        out_specs=[rev(D_GLA_IN, 0), _acc_spec((128, GLA_K)), _acc_spec((1, GLA_K)), _acc_spec((1, DV))],
        out_shape=[jax.ShapeDtypeStruct((n_ex, lp, D_GLA_IN), BF16), jax.ShapeDtypeStruct((128, GLA_K), F32),
                   jax.ShapeDtypeStruct((1, GLA_K), F32), jax.ShapeDtypeStruct((1, DV), F32)],
        scratch_shapes=[pltpu.VMEM((n_ex, DV, GLA_K), F32)],
        plan=plan,
    )(dycat.reshape(n_ex, lp, D), u3, u3, u3, u3, states, w2, gb, ng)
    return (du.reshape(t, D_GLA_IN), d_w2, d_gb, d_ng), extra


HEAD_ROWS_ALL = N_HEADS * CHUNK


def _head_of(shape, axis, per_head):
    return lax.broadcasted_iota(jnp.int32, shape, axis) // per_head


def _expand(x, lanes_per_head):
    rows, lanes = HEAD_ROWS_ALL, x.shape[1]
    keep = _head_of((rows, lanes), 0, CHUNK) == _head_of((rows, lanes), 1, lanes_per_head)
    return jnp.where(keep, jnp.tile(x, (N_HEADS, 1)), 0.0)


def _expand_lanes(x):
    rows, w = x.shape
    keep = _head_of((rows, N_HEADS * w), 0, CHUNK) == _head_of((rows, N_HEADS * w), 1, w)
    return jnp.where(keep, jnp.tile(x, (1, N_HEADS)), 0.0)


def _expand_state(st):
    rows, lanes = N_HEADS * DV, st.shape[1]
    keep = _head_of((rows, lanes), 0, DV) == _head_of((rows, lanes), 1, DK)
    return jnp.where(keep, jnp.tile(st, (N_HEADS, 1)), 0.0)


def _fold(t, rows_per_head):
    lane_head = _head_of((rows_per_head, t.shape[1]), 1, DK)
    out = jnp.where(lane_head == 0, t[0:rows_per_head], 0.0)
    for h in range(1, N_HEADS):
        out = out + jnp.where(lane_head == h, t[h * rows_per_head:(h + 1) * rows_per_head], 0.0)
    return out


def _rows_by_head(x):
    return jnp.concatenate([x[:, h * DV:(h + 1) * DV] for h in range(N_HEADS)], axis=0)


def _lanes_by_head(x):
    return jnp.concatenate([x[h * CHUNK:(h + 1) * CHUNK] for h in range(N_HEADS)], axis=1)


def _running_sum(a, lower):
    hi = a.astype(BF16)
    rest = a - hi.astype(F32)
    mid = rest.astype(BF16)
    lo = (rest - mid.astype(F32)).astype(BF16)
    w = a.shape[1]
    parts = _dot(_tri(lower).astype(F32).astype(BF16), jnp.concatenate([hi, mid, lo], axis=1))
    return parts[:, :w] + parts[:, w:2 * w] + parts[:, 2 * w:]


def _stacked_causal():
    i = lax.broadcasted_iota(jnp.int32, (HEAD_ROWS_ALL, CHUNK), 0) % CHUNK
    j = lax.broadcasted_iota(jnp.int32, (HEAD_ROWS_ALL, CHUNK), 1)
    return i >= j


def _gla_chunk(q, k, v, lr, st, w2, gb, first_chunk):
    z, a, live = _gla_gates(lr, w2, gb, first_chunk)
    yield
    b = _running_sum(a, True)
    yield
    bl = b[CHUNK - 1:CHUNK, :]
    e_pos, e_neg, e_dec = jnp.exp(b), jnp.exp(-b), jnp.exp(bl - b)
    q_f, k_f, kd_f = q * (DK ** -0.5) * e_pos, k * e_neg, k * e_dec
    qx = _expand(q_f, DK).astype(BF16)
    k_in, k_dec, v_b = k_f.astype(BF16), kd_f.astype(BF16), v.astype(BF16)
    s = jnp.where(_stacked_causal(), _dot_nt(qx, k_in), 0.0).astype(BF16)
    o_inter = _dot_nt(qx, st.astype(BF16))
    yield
    p = _dot(s, v_b)
    yield
    o = jnp.concatenate([p[h * CHUNK:(h + 1) * CHUNK, h * DV:(h + 1) * DV] for h in range(N_HEADS)], axis=0) + o_inter
    return dict(z=z, live=live, bl=bl, e_pos=e_pos, e_neg=e_neg, e_dec=e_dec, q_f=q_f, k_f=k_f, kd_f=kd_f,
                qx=qx, k_in=k_in, k_dec=k_dec, v_b=v_b, s=s, o=o, decay=jnp.exp(bl))


def _gla_fwd(u, w2, gb, ng, n_ex, lp, plan=None):
    nc = lp // CHUNK
    t = n_ex * lp

    def body(qk_ref, v_ref, g_ref, lr_ref, w2_ref, gb_ref, ng_ref, y_ref, st_ref, state):
        n = pl.program_id(0)

        @pl.when(n == 0)
        def _():
            state[...] = jnp.zeros_like(state)

        def one_example(e):
            st = state[e]
            st_ref[e] = st
            qk = qk_ref[e]
            c = yield from _gla_chunk(qk[:, :GLA_K], qk[:, GLA_K:], v_ref[e], lr_ref[e], st, w2_ref[...],
                                      gb_ref[...], n == 0)
            o = c["o"]
            rstd = lax.rsqrt(jnp.mean(o * o, axis=-1, keepdims=True) + RMS_EPS)
            g = _rows_by_head(g_ref[e])
            y_ref[e] = _lanes_by_head(o * rstd * ng_ref[...] * (g * _sigmoid(g))).astype(BF16)
            state[e] = c["decay"] * st + _fold(_dot_tn(c["v_b"], c["k_dec"]), DV)

        _in_lockstep(one_example(e) for e in range(n_ex))

    u3 = u.reshape(n_ex, lp, D_IN_PAD)
    blk = lambda w, col: pl.BlockSpec((n_ex, CHUNK, w), lambda n: (0, n, col))
    (y, states), extra = _call(
        body, name="gla_fwd", grid=(nc,),
        in_specs=[blk(2 * GLA_K, 2), blk(GLA_V, 3), blk(GLA_V, 4), blk(128, 20),
                  _const_spec((128, GLA_K)), _const_spec((1, GLA_K)), _const_spec((1, DV))],
        out_specs=[blk(GLA_V, 0), pl.BlockSpec((n_ex, DV, GLA_K), lambda n: (0, n, 0))],
        out_shape=[jax.ShapeDtypeStruct((n_ex, lp, GLA_V), BF16),
                   jax.ShapeDtypeStruct((n_ex, nc * DV, GLA_K), F32)],
        scratch_shapes=[pltpu.VMEM((n_ex, DV, GLA_K), F32)],
        plan=plan,
    )(u3, u3, u3, u3, w2, gb, ng)
    return (y.reshape(t, GLA_V), states), extra


def _gla_bwd(dycat, u, states, w2, gb, ng, n_ex, lp, plan=None):
    nc = lp // CHUNK
    t = n_ex * lp

    def body(dy_ref, qk_ref, v_ref, g_ref, lr_ref, st_ref, w2_ref, gb_ref, ng_ref,
             du_ref, dw2_ref, dgb_ref, dng_ref, dstate):
        n = pl.program_id(0)
        chunk = nc - 1 - n

        @pl.when(n == 0)
        def _():
            dw2_ref[...] = jnp.zeros_like(dw2_ref)
            dgb_ref[...] = jnp.zeros_like(dgb_ref)
            dng_ref[...] = jnp.zeros_like(dng_ref)
            dstate[...] = jnp.zeros_like(dstate)

        def one_example(e):
            qk = qk_ref[e]
            lr = lr_ref[e]
            st = st_ref[e]
            dst = dstate[e]
            c = yield from _gla_chunk(qk[:, :GLA_K], qk[:, GLA_K:], v_ref[e], lr, st, w2_ref[...], gb_ref[...],
                                      chunk == 0)
            qx, k_in, k_dec, v_b, s, o = c["qx"], c["k_in"], c["k_dec"], c["v_b"], c["s"], c["o"]
            ngv = ng_ref[...]
            rstd = lax.rsqrt(jnp.mean(o * o, axis=-1, keepdims=True) + RMS_EPS)
            nrm = o * rstd
            g = _rows_by_head(g_ref[e])
            dy = _rows_by_head(dy_ref[e])
            sg = _sigmoid(g)
            dg = dy * nrm * ngv * (sg * (1.0 + g * (1.0 - sg)))
            dt = dy * (g * sg)
            dng_ref[...] += jnp.sum(dt * nrm, axis=0, keepdims=True)
            dn = dt * ngv
            do = rstd * (dn - nrm * jnp.mean(dn * nrm, axis=-1, keepdims=True))
            do_b = do.astype(BF16)
            dox = _expand_lanes(do).astype(BF16)
            dstx = _expand_state(dst).astype(BF16)
            yield
            da = jnp.where(_stacked_causal(), _dot_nt(dox, v_b), 0.0).astype(BF16)
            dv = _dot_tn(s, dox) + _dot_nt(k_dec, dstx)
            dk_dec = _dot(v_b, dstx)
            dstate[e] = _dot_tn(do_b, qx) + c["decay"] * dst
            yield
            dq_in = _fold(_dot(da, k_in) + _dot(do_b, st.astype(BF16)), CHUNK)
            dk_in = _dot_tn(da, qx)
            yield
            dbl = (jnp.sum(dk_dec * c["kd_f"], axis=0, keepdims=True)
                   + c["decay"] * jnp.sum(dst * st, axis=0, keepdims=True))
            dq = dq_in * (DK ** -0.5) * c["e_pos"]
            dk = dk_in * c["e_neg"] + dk_dec * c["e_dec"]
            db = dq_in * c["q_f"] - dk_in * c["k_f"] - dk_dec * c["kd_f"]
            row = lax.broadcasted_iota(jnp.int32, (CHUNK, 1), 0)
            da_log = _running_sum(db + jnp.where(row == CHUNK - 1, dbl, 0.0), False)
            yield
            dz = jnp.where(c["live"], da_log * (1.0 - _sigmoid(c["z"])) * (1.0 / GATE_TAU), 0.0)
            dz_b = dz.astype(BF16)
            out = du_ref.at[e]
            out[:, 0:GLA_K] = dq.astype(BF16)
            out[:, GLA_K:2 * GLA_K] = dk.astype(BF16)
            out[:, 2 * GLA_K:2 * GLA_K + GLA_V] = dv.astype(BF16)
            out[:, 2 * GLA_K + GLA_V:2 * GLA_K + 2 * GLA_V] = _lanes_by_head(dg).astype(BF16)
            out[:, 2 * GLA_K + 2 * GLA_V:] = _dot_nt(dz_b, w2_ref[...]).astype(BF16)
            dw2_ref[...] += _dot_tn(lr.astype(BF16), dz_b)
            dgb_ref[...] += jnp.sum(dz, axis=0, keepdims=True)

        _in_lockstep(one_example(e) for e in range(n_ex))

    u3 = u.reshape(n_ex, lp, D_IN_PAD)
    rev = lambda w, col: pl.BlockSpec((n_ex, CHUNK, w), lambda n: (0, nc - 1 - n, col))
    (du, d_w2, d_gb, d_ng), extra = _call(
        body, name="gla_bwd", grid=(nc,),
        in_specs=[rev(GLA_V, 1), rev(2 * GLA_K, 2), rev(GLA_V, 3), rev(GLA_V, 4), rev(128, 20),
                  pl.BlockSpec((n_ex, DV, GLA_K), lambda n: (0, nc - 1 - n, 0)),
                  _const_spec((128, GLA_K)), _const_spec((1, GLA_K)), _const_spec((1, DV))],
        out_specs=[rev(D_GLA_IN, 0), _acc_spec((128, GLA_K)), _acc_spec((1, GLA_K)), _acc_spec((1, DV))],
        out_shape=[jax.ShapeDtypeStruct((n_ex, lp, D_GLA_IN), BF16), jax.ShapeDtypeStruct((128, GLA_K), F32),
                   jax.ShapeDtypeStruct((1, GLA_K), F32), jax.ShapeDtypeStruct((1, DV), F32)],
        scratch_shapes=[pltpu.VMEM((n_ex, DV, GLA_K), F32)],
        plan=plan,
    )(dycat.reshape(n_ex, lp, D), u3, u3, u3, u3, states, w2, gb, ng)
    return (du.reshape(t, D_GLA_IN), d_w2, d_gb, d_ng), extra


def _in_proj_bwd(du_conv, du_gla, w_in_t_conv, w_in_t_gla, h0, dh1, g_mix, plan=None):
    t = h0.shape[0]
    r = _row_tile(t, 384)

    def body(dc_ref, dg_ref, wc_ref, wg_ref, h_ref, dh1_ref, g_ref, dh0_ref, dgm_ref):
        @pl.when(pl.program_id(0) == 0)
        def _():
            dgm_ref[...] = jnp.zeros_like(dgm_ref)

        dhn = _dot(dc_ref[...], wc_ref[...]) + _dot(dg_ref[...], wg_ref[...])
        h = h_ref[...]
        rstd = lax.rsqrt(jnp.mean(h * h, axis=-1, keepdims=True) + RMS_EPS)
        nrm = h * rstd
        dgm_ref[...] += jnp.sum(dhn * nrm, axis=0, keepdims=True)
        dn = dhn * g_ref[...]
        dh0_ref[...] = dh1_ref[...] + rstd * (dn - nrm * jnp.mean(dn * nrm, axis=-1, keepdims=True))

    rows = lambda w: pl.BlockSpec((r, w), lambda i: (i, 0))
    return _call(
        body, name="in_proj_bwd", grid=(t // r,),
        in_specs=[rows(2 * C_CONV), rows(D_GLA_IN), _const_spec((2 * C_CONV, D)), _const_spec((D_GLA_IN, D)),
                  rows(D), rows(D), _const_spec((1, D))],
        out_specs=[rows(D), _acc_spec((1, D))],
        out_shape=[jax.ShapeDtypeStruct((t, D), F32), jax.ShapeDtypeStruct((1, D), F32)],
        plan=plan,
    )(du_conv, du_gla, w_in_t_conv, w_in_t_gla, h0, dh1, g_mix)


def _wgrad_hosting(x, dy, name, plan):
    t, m = x.shape
    n = dy.shape[1]
    tk = t // 3 if t % (3 * 128) == 0 else _row_tile(t, 384)
    tm = m if m <= D_GLA_IN else m // 2

    def body(x_ref, dy_ref, o_ref):
        @pl.when(pl.program_id(2) == 0)
        def _():
            o_ref[...] = jnp.zeros_like(o_ref)

        o_ref[...] += _dot_tn(x_ref[...].astype(BF16), dy_ref[...].astype(BF16))

    (out,), extra = _call(
        body, name=name, grid=(m // tm, 1, t // tk),
        in_specs=[pl.BlockSpec((tk, tm), lambda i, j, k: (k, i)), pl.BlockSpec((tk, n), lambda i, j, k: (k, j))],
        out_specs=[pl.BlockSpec((tm, n), lambda i, j, k: (i, j))],
        out_shape=[jax.ShapeDtypeStruct((m, n), F32)],
        plan=plan,
    )(x, dy)
    return out, extra


def _wgrad(x, dy, name):
    t, m = x.shape
    n = dy.shape[1]
    tk = t // 3 if t % (3 * 128) == 0 else _row_tile(t, 384)
    tm = m if m <= D_GLA_IN else m // 2
    tn = n

    def body(x_ref, dy_ref, o_ref):
        @pl.when(pl.program_id(2) == 0)
        def _():
            o_ref[...] = jnp.zeros_like(o_ref)

        o_ref[...] += _dot_tn(x_ref[...].astype(BF16), dy_ref[...].astype(BF16))

    return pl.pallas_call(
        body, name=name, grid=(m // tm, n // tn, t // tk),
        in_specs=[pl.BlockSpec((tk, tm), lambda i, j, k: (k, i)), pl.BlockSpec((tk, tn), lambda i, j, k: (k, j))],
        out_specs=pl.BlockSpec((tm, tn), lambda i, j, k: (i, j)),
        out_shape=jax.ShapeDtypeStruct((m, n), F32),
        compiler_params=_params(3),
    )(x, dy)


def _mesh_pos():
    return lax.axis_index("x"), lax.axis_index("y"), lax.axis_index("c")


def _other_chips(x, y):
    return [(1 - x, y), (x, 1 - y), (1 - x, 1 - y)]


HBM_SPEC = pl.BlockSpec(memory_space=pltpu.HBM)


def _gather_shards(shards):
    n = len(shards)

    def body(*refs):
        ins, outs = refs[:n], refs[n:2 * n]
        send_sems, recv_sems, local_sems = refs[2 * n:]
        x, y, c = _mesh_pos()
        mine = 2 * x + y
        chips = _other_chips(x, y)
        local = [pltpu.make_async_copy(ins[a], outs[a].at[mine], local_sems.at[a]) for a in range(n)]
        for cp in local:
            cp.start()

        def remote(a, k, block):
            px, py = chips[k]
            return pltpu.make_async_remote_copy(
                src_ref=ins[a], dst_ref=outs[a].at[block], send_sem=send_sems.at[3 * a + k],
                recv_sem=recv_sems.at[3 * a + k], device_id=(px, py, c), device_id_type=MESH)

        sends = [remote(a, k, mine) for a in range(n) for k in range(3)]
        for cp in sends:
            cp.start()
        for a in range(n):
            for k, (px, py) in enumerate(chips):
                remote(a, k, 2 * px + py).wait_recv()
        for cp in sends:
            cp.wait_send()
        for cp in local:
            cp.wait()

    return pl.pallas_call(
        body, name="gather_shards",
        in_specs=[HBM_SPEC] * n, out_specs=[HBM_SPEC] * n,
        out_shape=[jax.ShapeDtypeStruct((N_CHIPS,) + s.shape, s.dtype) for s in shards],
        scratch_shapes=[pltpu.SemaphoreType.DMA((3 * n,)), pltpu.SemaphoreType.DMA((3 * n,)),
                        pltpu.SemaphoreType.DMA((n,))],
        compiler_params=pltpu.CompilerParams(has_side_effects=True),
    )(*shards)


def _send_half_to_sibling(g2):
    def body(g_ref, recv_ref, send_sem, recv_sem):
        x, y, c = _mesh_pos()
        cp = pltpu.make_async_remote_copy(
            src_ref=g_ref.at[1 - c], dst_ref=recv_ref, send_sem=send_sem, recv_sem=recv_sem,
            device_id=(x, y, 1 - c), device_id_type=MESH)
        cp.start()
        cp.wait()

    return pl.pallas_call(
        body, name="rs_to_sibling", in_specs=[HBM_SPEC], out_specs=HBM_SPEC,
        out_shape=jax.ShapeDtypeStruct(g2.shape[1:], g2.dtype),
        scratch_shapes=[pltpu.SemaphoreType.DMA(()), pltpu.SemaphoreType.DMA(())],
        compiler_params=pltpu.CompilerParams(has_side_effects=True),
    )(g2)


def _add_own_half(g2, recv, c):
    rows = N_CHIPS * HALF_ROWS
    tr = 512
    g2f = g2.reshape(2, rows, D)
    recvf = recv.reshape(rows, D)

    def body(c_ref, a_ref, b_ref, o_ref):
        o_ref[...] = a_ref[0] + b_ref[...]

    out = pl.pallas_call(
        body, name="rs_add_halves",
        grid_spec=pltpu.PrefetchScalarGridSpec(
            num_scalar_prefetch=1, grid=(rows // tr,),
            in_specs=[pl.BlockSpec((1, tr, D), lambda i, s: (s[0], i, 0)), pl.BlockSpec((tr, D), lambda i, s: (i, 0))],
            out_specs=pl.BlockSpec((tr, D), lambda i, s: (i, 0))),
        out_shape=jax.ShapeDtypeStruct((rows, D), F32),
        compiler_params=_params(1),
    )(jnp.reshape(c, (1,)).astype(jnp.int32), g2f, recvf)
    return out.reshape(N_CHIPS, HALF_ROWS, D)


def _exchange_chip_sums(p):
    def body(p_ref, out_ref, send_sems, recv_sems, local_sem):
        x, y, c = _mesh_pos()
        mine = 2 * x + y
        chips = _other_chips(x, y)
        local = pltpu.make_async_copy(p_ref.at[mine], out_ref.at[mine], local_sem)
        local.start()

        def remote(k, src_block, dst_block):
            px, py = chips[k]
            return pltpu.make_async_remote_copy(
                src_ref=p_ref.at[src_block], dst_ref=out_ref.at[dst_block], send_sem=send_sems.at[k],
                recv_sem=recv_sems.at[k], device_id=(px, py, c), device_id_type=MESH)

        sends = [remote(k, 2 * px + py, mine) for k, (px, py) in enumerate(chips)]
        for cp in sends:
            cp.start()
        for k, (px, py) in enumerate(chips):
            remote(k, mine, 2 * px + py).wait_recv()
        for cp in sends:
            cp.wait_send()
        local.wait()

    return pl.pallas_call(
        body, name="rs_chip_exchange", in_specs=[HBM_SPEC], out_specs=HBM_SPEC,
        out_shape=jax.ShapeDtypeStruct(p.shape, p.dtype),
        scratch_shapes=[pltpu.SemaphoreType.DMA((3,)), pltpu.SemaphoreType.DMA((3,)), pltpu.SemaphoreType.DMA(())],
        compiler_params=pltpu.CompilerParams(has_side_effects=True),
    )(p)


def _sum_chips(parts):
    tr = 512

    def body(p_ref, o_ref):
        o_ref[...] = ((p_ref[0] + p_ref[1]) + p_ref[2]) + p_ref[3]

    return pl.pallas_call(
        body, name="rs_sum_chips", grid=(HALF_ROWS // tr,),
        in_specs=[pl.BlockSpec((N_CHIPS, tr, D), lambda i: (0, i, 0))],
        out_specs=pl.BlockSpec((tr, D), lambda i: (i, 0)),
        out_shape=jax.ShapeDtypeStruct((HALF_ROWS, D), F32),
        compiler_params=_params(1),
    )(parts)


def _share_with_sibling(half):
    def body(h_ref, out_ref, send_sem, recv_sem, local_sem):
        x, y, c = _mesh_pos()
        local = pltpu.make_async_copy(h_ref, out_ref.at[c], local_sem)
        local.start()
        cp = pltpu.make_async_remote_copy(
            src_ref=h_ref, dst_ref=out_ref.at[c], send_sem=send_sem, recv_sem=recv_sem,
            device_id=(x, y, 1 - c), device_id_type=MESH)
        cp.start()
        pltpu.make_async_remote_copy(
            src_ref=h_ref, dst_ref=out_ref.at[1 - c], send_sem=send_sem, recv_sem=recv_sem,
            device_id=(x, y, 1 - c), device_id_type=MESH).wait_recv()
        cp.wait_send()
        local.wait()

    return pl.pallas_call(
        body, name="rs_share_sibling", in_specs=[HBM_SPEC], out_specs=HBM_SPEC,
        out_shape=jax.ShapeDtypeStruct((2,) + half.shape, half.dtype),
        scratch_shapes=[pltpu.SemaphoreType.DMA(()), pltpu.SemaphoreType.DMA(()), pltpu.SemaphoreType.DMA(())],
        compiler_params=pltpu.CompilerParams(has_side_effects=True),
    )(half)


def _adam_update(g, w, m, v):
    m2 = ADAM_B1 * m + (1.0 - ADAM_B1) * g
    v2 = ADAM_B2 * v + (1.0 - ADAM_B2) * (g * g)
    m_hat = m2 / (1.0 - ADAM_B1 ** ADAM_STEP)
    v_hat = v2 / (1.0 - ADAM_B2 ** ADAM_STEP)
    delta = -ADAM_LR * (m_hat / (jnp.sqrt(v_hat) + ADAM_EPS) + ADAM_WD * w)
    return delta, m2, v2


def _adamw_slab(g, w, m, v):
    rows = g.shape[0]
    tr = 256

    def body(g_ref, w_ref, m_ref, v_ref, d_ref, m2_ref, v2_ref):
        d_ref[...], m2_ref[...], v2_ref[...] = _adam_update(g_ref[...], w_ref[...], m_ref[...], v_ref[...])

    spec = pl.BlockSpec((tr, D), lambda i: (i, 0))
    return pl.pallas_call(
        body, name="adamw_slab", grid=(rows // tr,), in_specs=[spec] * 4, out_specs=[spec] * 3,
        out_shape=[jax.ShapeDtypeStruct((rows, D), F32)] * 3,
        compiler_params=_params(1),
    )(g, w, m, v)


def _allreduce_small_adamw(part, w, m, v):
    def body(p_ref, w_ref, m_ref, v_ref, g_ref, d_ref, m2_ref, v2_ref, slots, send_sems, recv_sems):
        x, y, c = _mesh_pos()
        mine = 4 * x + 2 * y + c
        peers = [(px, py, pc) for px in (x, 1 - x) for py in (y, 1 - y) for pc in (c, 1 - c)][1:]

        def remote(k, slot):
            return pltpu.make_async_remote_copy(
                src_ref=p_ref, dst_ref=slots.at[slot], send_sem=send_sems.at[k], recv_sem=recv_sems.at[k],
                device_id=peers[k], device_id_type=MESH)

        sends = [remote(k, mine) for k in range(7)]
        for cp in sends:
            cp.start()
        slots[mine] = p_ref[...]
        for k, (px, py, pc) in enumerate(peers):
            remote(k, 4 * px + 2 * py + pc).wait_recv()
        for cp in sends:
            cp.wait_send()
        g = slots[0]
        for d in range(1, 8):
            g = g + slots[d]
        g_ref[...] = g
        d_ref[...], m2_ref[...], v2_ref[...] = _adam_update(g, w_ref[...], m_ref[...], v_ref[...])

    vm = pl.BlockSpec(memory_space=pltpu.VMEM)
    shape = jax.ShapeDtypeStruct(part.shape, F32)
    return pl.pallas_call(
        body, name="small_allreduce_adamw", in_specs=[vm] * 4, out_specs=[vm] * 4, out_shape=[shape] * 4,
        scratch_shapes=[pltpu.VMEM((8,) + part.shape, F32), pltpu.SemaphoreType.DMA((7,)),
                        pltpu.SemaphoreType.DMA((7,))],
        compiler_params=pltpu.CompilerParams(has_side_effects=True),
    )(part, w, m, v)


def _half(ref, c, axis):
    n = ref.shape[axis] // 2
    return ref.at[(slice(None),) * axis + (pl.ds(c * n, n),)]


def _remote(src, dst, send_sem, recv_sem, device):
    return pltpu.make_async_remote_copy(src_ref=src, dst_ref=dst, send_sem=send_sem, recv_sem=recv_sem,
                                        device_id=device, device_id_type=MESH)


def _gather_weights(split, axes, whole):
    ns, n = len(split), len(split) + len(whole)

    def body(*refs):
        ins, outs = refs[:n], refs[n:2 * n]
        ici_send, ici_recv, d2d_send, d2d_recv, local_sems = refs[2 * n:]
        x, y, c = _mesh_pos()
        mine = 2 * x + y
        chips = _other_chips(x, y)
        local = [pltpu.make_async_copy(ins[a], outs[a].at[mine], local_sems.at[a]) for a in range(n)]
        for cp in local:
            cp.start()

        def ici(a, k, block):
            px, py = chips[k]
            src, dst = ins[a], outs[a].at[block]
            if a < ns:
                src, dst = _half(src, c, axes[a]), _half(dst, c, axes[a])
            return _remote(src, dst, ici_send.at[3 * a + k], ici_recv.at[3 * a + k], (px, py, c))

        def d2d(a, k, block, half):
            part = _half(outs[a].at[block], half, axes[a])
            return _remote(part, part, d2d_send.at[3 * a + k], d2d_recv.at[3 * a + k], (x, y, 1 - c))

        sends = [ici(a, k, mine) for a in range(n) for k in range(3)]
        for cp in sends:
            cp.start()
        for a in range(n):
            for k, (px, py) in enumerate(chips):
                ici(a, k, 2 * px + py).wait_recv()
                if a < ns:
                    sends.append(d2d(a, k, 2 * px + py, c))
                    sends[-1].start()
        for a in range(ns):
            for k, (px, py) in enumerate(chips):
                d2d(a, k, 2 * px + py, 1 - c).wait_recv()
        for cp in sends:
            cp.wait_send()
        for cp in local:
            cp.wait()

    arrays = list(split) + list(whole)
    return pl.pallas_call(
        body, name="gather_weights", in_specs=[HBM_SPEC] * n, out_specs=[HBM_SPEC] * n,
        out_shape=[jax.ShapeDtypeStruct((N_CHIPS,) + s.shape, s.dtype) for s in arrays],
        scratch_shapes=[pltpu.SemaphoreType.DMA((3 * n,)), pltpu.SemaphoreType.DMA((3 * n,)),
                        pltpu.SemaphoreType.DMA((3 * ns,)), pltpu.SemaphoreType.DMA((3 * ns,)),
                        pltpu.SemaphoreType.DMA((n,))],
        compiler_params=pltpu.CompilerParams(has_side_effects=True),
    )(*arrays)


def _rs_to_sibling(gs):
    n = len(gs)

    def body(*refs):
        ins, outs, send_sems, recv_sems = refs[:n], refs[n:2 * n], refs[2 * n], refs[2 * n + 1]
        x, y, c = _mesh_pos()
        copies = [_remote(_half(ins[a], 1 - c, 2), outs[a], send_sems.at[a], recv_sems.at[a], (x, y, 1 - c))
                  for a in range(n)]
        for cp in copies:
            cp.start()
        for cp in copies:
            cp.wait()

    return pl.pallas_call(
        body, name="rs_to_sibling", in_specs=[HBM_SPEC] * n, out_specs=[HBM_SPEC] * n,
        out_shape=[jax.ShapeDtypeStruct(g.shape[:2] + (g.shape[2] // 2,), g.dtype) for g in gs],
        scratch_shapes=[pltpu.SemaphoreType.DMA((n,)), pltpu.SemaphoreType.DMA((n,))],
        compiler_params=pltpu.CompilerParams(has_side_effects=True),
    )(*gs)


def _rs_add_halves(g, recv, c, name):
    _, rows, w = g.shape
    h = w // 2
    tr = rows // 2 if rows % 16 == 0 and rows > 64 else rows

    def body(c_ref, a_ref, b_ref, o_ref):
        o_ref[...] = (a_ref[...] + b_ref[...]).astype(BF16)

    return pl.pallas_call(
        body, name=name,
        grid_spec=pltpu.PrefetchScalarGridSpec(
            num_scalar_prefetch=1, grid=(N_CHIPS, rows // tr),
            in_specs=[pl.BlockSpec((1, tr, h), lambda j, i, s: (j, i, s[0])),
                      pl.BlockSpec((1, tr, h), lambda j, i, s: (j, i, 0))],
            out_specs=pl.BlockSpec((1, tr, h), lambda j, i, s: (j, i, 0))),
        out_shape=jax.ShapeDtypeStruct((N_CHIPS, rows, h), BF16),
        compiler_params=_params(2),
    )(jnp.reshape(c, (1,)).astype(jnp.int32), g, recv)


def _rs_chip_exchange(ps):
    n = len(ps)

    def body(*refs):
        ins, outs = refs[:n], refs[n:2 * n]
        send_sems, recv_sems, local_sems = refs[2 * n:]
        x, y, c = _mesh_pos()
        mine = 2 * x + y
        chips = _other_chips(x, y)
        local = [pltpu.make_async_copy(ins[a].at[mine], outs[a].at[mine], local_sems.at[a]) for a in range(n)]
        for cp in local:
            cp.start()

        def ici(a, k, src_block, dst_block):
            px, py = chips[k]
            return _remote(ins[a].at[src_block], outs[a].at[dst_block], send_sems.at[3 * a + k],
                           recv_sems.at[3 * a + k], (px, py, c))

        sends = [ici(a, k, 2 * px + py, mine) for a in range(n) for k, (px, py) in enumerate(chips)]
        for cp in sends:
            cp.start()
        for a in range(n):
            for k, (px, py) in enumerate(chips):
                ici(a, k, mine, 2 * px + py).wait_recv()
        for cp in sends:
            cp.wait_send()
        for cp in local:
            cp.wait()

    return pl.pallas_call(
        body, name="rs_chip_exchange", in_specs=[HBM_SPEC] * n, out_specs=[HBM_SPEC] * n,
        out_shape=[jax.ShapeDtypeStruct(p.shape, p.dtype) for p in ps],
        scratch_shapes=[pltpu.SemaphoreType.DMA((3 * n,)), pltpu.SemaphoreType.DMA((3 * n,)),
                        pltpu.SemaphoreType.DMA((n,))],
        compiler_params=pltpu.CompilerParams(has_side_effects=True),
    )(*ps)


def _rs_sum_chips(parts, name):
    _, rows, h = parts.shape
    tr = rows // 2 if rows % 16 == 0 and rows > 64 else rows

    def body(p_ref, o_ref):
        p = p_ref[...].astype(F32)
        o_ref[...] = ((p[0] + p[1]) + p[2]) + p[3]

    return pl.pallas_call(
        body, name=name, grid=(rows // tr,),
        in_specs=[pl.BlockSpec((N_CHIPS, tr, h), lambda i: (0, i, 0))],
        out_specs=pl.BlockSpec((tr, h), lambda i: (i, 0)),
        out_shape=jax.ShapeDtypeStruct((rows, h), F32),
        compiler_params=_params(1),
    )(parts)


def _rs_share(halves):
    n = len(halves)

    def body(*refs):
        ins, outs = refs[:n], refs[n:2 * n]
        send_sems, recv_sems, local_sems = refs[2 * n:]
        x, y, c = _mesh_pos()
        local = [pltpu.make_async_copy(ins[a], _half(outs[a], c, 1), local_sems.at[a]) for a in range(n)]
        for cp in local:
            cp.start()
        sends = [_remote(ins[a], _half(outs[a], c, 1), send_sems.at[a], recv_sems.at[a], (x, y, 1 - c))
                 for a in range(n)]
        for cp in sends:
            cp.start()
        for a in range(n):
            _remote(ins[a], _half(outs[a], 1 - c, 1), send_sems.at[a], recv_sems.at[a], (x, y, 1 - c)).wait_recv()
        for cp in sends:
            cp.wait_send()
        for cp in local:
            cp.wait()

    return pl.pallas_call(
        body, name="rs_share", in_specs=[HBM_SPEC] * n, out_specs=[HBM_SPEC] * n,
        out_shape=[jax.ShapeDtypeStruct((p.shape[0], 2 * p.shape[1]), p.dtype) for p in halves],
        scratch_shapes=[pltpu.SemaphoreType.DMA((n,)), pltpu.SemaphoreType.DMA((n,)),
                        pltpu.SemaphoreType.DMA((n,))],
        compiler_params=pltpu.CompilerParams(has_side_effects=True),
    )(*halves)


def _adamw(g, w, m, v, name):
    rows, cols = g.shape
    tr = 256 if rows % 256 == 0 else (rows // 2 if rows % 16 == 0 and rows > 64 else rows)

    def body(g_ref, w_ref, m_ref, v_ref, d_ref, m2_ref, v2_ref):
        d_ref[...], m2_ref[...], v2_ref[...] = _adam_update(g_ref[...], w_ref[...], m_ref[...], v_ref[...])

    spec = pl.BlockSpec((tr, cols), lambda i: (i, 0))
    return pl.pallas_call(
        body, name=name, grid=(rows // tr,), in_specs=[spec] * 4, out_specs=[spec] * 3,
        out_shape=[jax.ShapeDtypeStruct((rows, cols), F32)] * 3,
        compiler_params=_params(1),
    )(g, w, m, v)


def _rows_of(a):
    flat = a.reshape(-1)
    pad = (-flat.shape[0]) % D
    if pad:
        flat = jnp.concatenate([flat, jnp.zeros((pad,), flat.dtype)])
    return flat.reshape(-1, D)


SLAB_PARTS = (("w_in", (D, D_IN // N_CHIPS)), ("w_out", (D // N_CHIPS, D)), ("w_ffn_gate", (D, D_FF // N_CHIPS)),
              ("w_ffn_up", (D, D_FF // N_CHIPS)), ("w_ffn_down", (D_FF // N_CHIPS, D)),
              ("meta_tokens", (N_META, D // N_CHIPS)), ("conv_w", (CONV_W, C_CONV // N_CHIPS)),
              ("gla_w_gate2", (RANK, GLA_K // N_CHIPS)))


def _pack_slab(parts):
    rows = [_rows_of(parts[name].reshape(shape)) for name, shape in SLAB_PARTS]
    used = sum(r.shape[0] for r in rows)
    rows.append(jnp.zeros((SLAB_ROWS - used, D), F32))
    return jnp.concatenate(rows, axis=0)


def _unpack_slab(slab, lead):
    out, r0 = {}, 0
    for name, shape in SLAB_PARTS:
        size = shape[0] * shape[1]
        nrows = -(-size // D)
        out[name] = slab[r0:r0 + nrows].reshape(-1)[:size].reshape(lead[name] + shape)
        r0 += nrows
    return out


SMALL_PARTS = (("norm_mix_g", 0, 0, D), ("norm_ffn_g", 1, 0, D), ("norm_final_g", 2, 0, D),
               ("conv_b", 3, 0, C_CONV), ("conv_ln_g", 3, C_CONV, C_CONV), ("conv_ln_b", 4, 0, C_CONV),
               ("gla_gate_b", 4, C_CONV, GLA_K), ("gla_norm_g", 4, C_CONV + GLA_K, DV))


def _pack_small(parts):
    slab = jnp.zeros((SMALL_ROWS, D), F32)
    for name, row, col, size in SMALL_PARTS:
        slab = lax.dynamic_update_slice(slab, parts[name].reshape(1, size).astype(F32), (row, col))
    return slab


def _unpack_small(slab, shapes):
    return {name: slab[row, col:col + size].reshape(shapes[name]) for name, row, col, size in SMALL_PARTS}


def _column_block(full, j, width):
    return lax.dynamic_slice_in_dim(full, j * width, width, axis=1)


def _local_step(x, target, w):
    n_ex, seq, _ = x.shape
    lp = HEAD_ROWS + seq
    t = n_ex * lp
    meta = jnp.broadcast_to(w["meta_tokens"][None], (n_ex, N_META, D))
    h0 = jnp.concatenate([jnp.zeros((n_ex, PAD_ROWS, D), F32), meta, x], axis=1).reshape(t, D)
    tgt = jnp.concatenate([jnp.zeros((n_ex, HEAD_ROWS, D), F32), target], axis=1).reshape(t, D)
    row_mask = jnp.concatenate([jnp.zeros((n_ex, HEAD_ROWS, 1), F32), jnp.ones((n_ex, seq, 1), F32)],
                               axis=1).reshape(t, 1)

    u, hn = _in_proj(h0, w["norm_mix_g"], w["w_in"])
    yc, y_conv = _conv_fwd(u, w["conv_w"], w["conv_b"], w["conv_ln_g"], w["conv_ln_b"], n_ex, lp)
    y_gla, states = _gla_fwd(u, w["gla_w_gate2"], w["gla_gate_b"], w["gla_norm_g"], n_ex, lp)
    h1, hn2, gate, up, act = _mix_out_ffn_up(h0, y_conv, y_gla, w["w_out"], w["norm_ffn_g"],
                                             w["w_ffn_gate_t"], w["w_ffn_up_t"])
    dh2, loss, d_final_g = _ffn_down_loss(act, w["w_ffn_down"], h1, tgt, w["norm_final_g"], row_mask)

    dgate, dup, dh1, dycat, d_ffn_g = _ffn_bwd(dh2, gate, up, h1, w["w_ffn_down"], w["w_ffn_gate_t"],
                                                w["w_ffn_up_t"], w["w_out"], w["norm_ffn_g"])
    du_conv, d_conv_w, d_conv_b, d_ln_g, d_ln_b = _conv_bwd(dycat, yc, u, w["conv_w"], w["conv_ln_g"],
                                                            w["conv_ln_b"], n_ex, lp)
    du_gla, d_w2, d_gate_b, d_norm_g = _gla_bwd(dycat, u, states, w["gla_w_gate2"], w["gla_gate_b"],
                                                w["gla_norm_g"], n_ex, lp)
    dh0, d_mix_g = _in_proj_bwd(du_conv, du_gla, w["w_in"][:, :2 * C_CONV], w["w_in"][:, 2 * C_CONV:],
                                h0, dh1, w["norm_mix_g"])

    d_w_in_t = jnp.concatenate([_wgrad(du_conv, hn, "wgrad_in_conv"), _wgrad(du_gla, hn, "wgrad_in_gla")],
                               axis=0)[:D_IN]
    d_w_out = jnp.concatenate([_wgrad(y_conv, dh1, "wgrad_out_conv"), _wgrad(y_gla, dh1, "wgrad_out_gla")], axis=0)
    dh0 = dh0.reshape(n_ex, lp, D)
    grads = {
        "w_in_t": d_w_in_t, "w_out": d_w_out,
        "w_ffn_gate_t": _wgrad(dgate, hn2, "wgrad_gate"), "w_ffn_up_t": _wgrad(dup, hn2, "wgrad_up"),
        "w_ffn_down": _wgrad(act, dh2, "wgrad_down"),
        "meta_tokens": jnp.sum(dh0[:, PAD_ROWS:HEAD_ROWS], axis=0),
        "conv_w": d_conv_w, "gla_w_gate2": d_w2[:RANK],
        "norm_mix_g": d_mix_g, "norm_ffn_g": d_ffn_g, "norm_final_g": d_final_g,
        "conv_b": d_conv_b, "conv_ln_g": d_ln_g, "conv_ln_b": d_ln_b,
        "gla_gate_b": d_gate_b, "gla_norm_g": d_norm_g,
    }
    return loss[0, 0], dh0[:, HEAD_ROWS:], grads


WEIGHT_NAMES = ("meta_tokens", "norm_mix_g", "w_in", "conv_w", "conv_b", "conv_ln_g", "conv_ln_b", "gla_w_gate2",
                "gla_gate_b", "gla_norm_g", "w_out", "norm_ffn_g", "w_ffn_gate", "w_ffn_up", "w_ffn_down",
                "norm_final_g")
MATMUL_WEIGHTS = ("w_in", "w_out", "w_ffn_gate", "w_ffn_up", "w_ffn_down")
ROW_SHARDED = ("w_out", "w_ffn_down")


def _full_weights(ws):
    sh = lambda name: ws[name].reshape(ws[name].shape[-2:])
    split = [sh("w_in").astype(BF16), sh("w_out").astype(BF16), sh("w_ffn_gate").T.astype(BF16),
             sh("w_ffn_up").T.astype(BF16), sh("w_ffn_down").astype(BF16)]
    whole = [sh("meta_tokens"), sh("conv_w"), sh("gla_w_gate2")]
    w_in, w_out, gate_t, up_t, down, meta, conv_w, w2 = _gather_weights(split, [0, 0, 0, 0, 0], whole)
    cols = lambda a: jnp.concatenate([a[j] for j in range(N_CHIPS)], axis=1)
    full = {name: ws[name].reshape(1, -1) for name, _, _, _ in SMALL_PARTS}
    full["w_in"] = jnp.concatenate([cols(w_in), jnp.zeros((D, D_IN_PAD - D_IN), BF16)], axis=1)
    full["w_out"] = w_out.reshape(D, D)
    full["w_ffn_gate_t"] = gate_t.reshape(D_FF, D)
    full["w_ffn_up_t"] = up_t.reshape(D_FF, D)
    full["w_ffn_down"] = down.reshape(D_FF, D)
    full["meta_tokens"] = cols(meta)
    full["conv_w"] = jnp.concatenate([cols(conv_w), jnp.zeros((32 - CONV_W, C_CONV), F32)], axis=0)
    full["gla_w_gate2"] = jnp.concatenate([cols(w2), jnp.zeros((128 - RANK, GLA_K), F32)], axis=0).astype(BF16)
    return full


SMALL_RS_ROWS = 48


def _pack_small_sharded(grads):
    by_chip = lambda g, w: jnp.transpose(g.reshape(g.shape[0], N_CHIPS, w), (1, 0, 2))
    meta = by_chip(grads["meta_tokens"], D // N_CHIPS)
    conv = by_chip(grads["conv_w"], C_CONV // N_CHIPS).reshape(N_CHIPS, 16, 256)
    w2 = by_chip(grads["gla_w_gate2"], GLA_K // N_CHIPS).reshape(N_CHIPS, 4, 256)
    pad = jnp.zeros((N_CHIPS, SMALL_RS_ROWS - 36, 256), F32)
    return jnp.concatenate([meta, conv, w2, pad], axis=1)


def _unpack_small_sharded(g):
    return {"meta_tokens": g[0:16], "conv_w": g[16:32].reshape(32, C_CONV // N_CHIPS)[:CONV_W],
            "gla_w_gate2": g[32:36].reshape(RANK, GLA_K // N_CHIPS)}


def _kernel_without_overlap(x, meta_tokens, norm_mix_g, w_in, conv_w, conv_b, conv_ln_g, conv_ln_b, gla_w_gate2, gla_gate_b, gla_norm_g, w_out, norm_ffn_g, w_ffn_gate, w_ffn_up, w_ffn_down, norm_final_g, loss_target, m_meta_tokens, m_norm_mix_g, m_w_in, m_conv_w, m_conv_b, m_conv_ln_g, m_conv_ln_b, m_gla_w_gate2, m_gla_gate_b, m_gla_norm_g, m_w_out, m_norm_ffn_g, m_w_ffn_gate, m_w_ffn_up, m_w_ffn_down, m_norm_final_g, v_meta_tokens, v_norm_mix_g, v_w_in, v_conv_w, v_conv_b, v_conv_ln_g, v_conv_ln_b, v_gla_w_gate2, v_gla_gate_b, v_gla_norm_g, v_w_out, v_norm_ffn_g, v_w_ffn_gate, v_w_ffn_up, v_w_ffn_down, v_norm_final_g):
    ws = dict(zip(WEIGHT_NAMES, (meta_tokens, norm_mix_g, w_in, conv_w, conv_b, conv_ln_g, conv_ln_b, gla_w_gate2,
                                 gla_gate_b, gla_norm_g, w_out, norm_ffn_g, w_ffn_gate, w_ffn_up, w_ffn_down,
                                 norm_final_g)))
    ms = dict(zip(WEIGHT_NAMES, (m_meta_tokens, m_norm_mix_g, m_w_in, m_conv_w, m_conv_b, m_conv_ln_g, m_conv_ln_b,
                                 m_gla_w_gate2, m_gla_gate_b, m_gla_norm_g, m_w_out, m_norm_ffn_g, m_w_ffn_gate,
                                 m_w_ffn_up, m_w_ffn_down, m_norm_final_g)))
    vs = dict(zip(WEIGHT_NAMES, (v_meta_tokens, v_norm_mix_g, v_w_in, v_conv_w, v_conv_b, v_conv_ln_g, v_conv_ln_b,
                                 v_gla_w_gate2, v_gla_gate_b, v_gla_norm_g, v_w_out, v_norm_ffn_g, v_w_ffn_gate,
                                 v_w_ffn_up, v_w_ffn_down, v_norm_final_g)))
    c = lax.axis_index("c")

    full = _full_weights(ws)
    loss, grad_x, grads = _local_step(x, loss_target, full)
    loss = lax.psum(loss, ("x", "y", "c"))

    rs_names = ("w_in", "w_out", "w_ffn_gate", "w_ffn_up", "w_ffn_down", "small")
    by_owner = [grads["w_in_t"].reshape(N_CHIPS, D_IN // N_CHIPS, D), grads["w_out"].reshape(N_CHIPS, D // N_CHIPS, D),
                grads["w_ffn_gate_t"].reshape(N_CHIPS, D_FF // N_CHIPS, D),
                grads["w_ffn_up_t"].reshape(N_CHIPS, D_FF // N_CHIPS, D),
                grads["w_ffn_down"].reshape(N_CHIPS, D_FF // N_CHIPS, D), _pack_small_sharded(grads)]
    from_sibling = _rs_to_sibling(by_owner)
    chip_sums = [_rs_add_halves(g, r, c, "rs_add_" + nm) for g, r, nm in zip(by_owner, from_sibling, rs_names)]
    halves = [_rs_sum_chips(p, "rs_sum_" + nm) for p, nm in zip(_rs_chip_exchange(chip_sums), rs_names)]
    reduced = dict(zip(rs_names, _rs_share(halves)))
    g_sharded = {"w_in": reduced["w_in"].T, "w_out": reduced["w_out"], "w_ffn_gate": reduced["w_ffn_gate"].T,
                 "w_ffn_up": reduced["w_ffn_up"].T, "w_ffn_down": reduced["w_ffn_down"],
                 **_unpack_small_sharded(reduced["small"])}
    out = {"grad": {}, "delta": {}, "new_m": {}, "new_v": {}}
    for name, g in g_sharded.items():
        shape = ws[name].shape
        flat = lambda a: a.reshape(shape[-2:])
        delta, new_m, new_v = _adamw(g, flat(ws[name]), flat(ms[name]), flat(vs[name]), "adamw_" + name)
        for kind, a in (("grad", g), ("delta", delta), ("new_m", new_m), ("new_v", new_v)):
            out[kind][name] = a.reshape(shape)

    small_shapes = {name: ws[name].shape for name, _, _, _ in SMALL_PARTS}
    g_s, d_s, m_s, v_s = _allreduce_small_adamw(_pack_small(grads), _pack_small(ws), _pack_small(ms), _pack_small(vs))
    for kind, slab in (("grad", g_s), ("delta", d_s), ("new_m", m_s), ("new_v", v_s)):
        out[kind].update(_unpack_small(slab, small_shapes))

    return (loss, grad_x, *[out[kind][name] for kind in ("grad", "delta", "new_m", "new_v") for name in WEIGHT_NAMES])


def _gather_plan(split, whole=(), axes=None):
    split, whole = list(split), list(whole)
    ns, n = len(split), len(split) + len(whole)

    def make(ins, outs, sems):
        ici_send, ici_recv, d2d_send, d2d_recv, own_send, own_recv = sems
        x, y, c = _mesh_pos()
        mine = 2 * x + y
        chips = _other_chips(x, y)
        blocks = [2 * px + py for px, py in chips]

        def own(a):
            return _remote(ins[a], outs[a].at[mine], own_send.at[a], own_recv.at[a], (x, y, 1 - c))

        def ici(a, k, block):
            px, py = chips[k]
            src, dst = ins[a], outs[a].at[block]
            if a < ns:
                src, dst = _half(src, c, axes[a]), _half(dst, c, axes[a])
            return _remote(src, dst, ici_send.at[3 * a + k], ici_recv.at[3 * a + k], (px, py, c))

        def d2d(a, k, half):
            part = _half(outs[a].at[blocks[k]], half, axes[a])
            return _remote(part, part, d2d_send.at[3 * a + k], d2d_recv.at[3 * a + k], (x, y, 1 - c))

        def start():
            for a in range(n):
                for k in range(3):
                    ici(a, k, mine).start()
                own(a).start()

        def finish():
            for a in range(n):
                for k in range(3):
                    ici(a, k, blocks[k]).wait_recv()
                    if a < ns:
                        d2d(a, k, c).start()
            for a in range(ns):
                for k in range(3):
                    d2d(a, k, 1 - c).wait_recv()
            for a in range(n):
                for k in range(3):
                    ici(a, k, mine).wait_send()
                    if a < ns:
                        d2d(a, k, c).wait_send()
                own(a).wait()

        return start, finish

    arrays = split + whole
    axes = [0] * ns if axes is None else list(axes)
    return _Plan(arrays, [jax.ShapeDtypeStruct((N_CHIPS,) + s.shape, s.dtype) for s in arrays],
                 [pltpu.SemaphoreType.DMA((3 * n,)), pltpu.SemaphoreType.DMA((3 * n,)),
                  pltpu.SemaphoreType.DMA((3 * ns,)), pltpu.SemaphoreType.DMA((3 * ns,)),
                  pltpu.SemaphoreType.DMA((n,)), pltpu.SemaphoreType.DMA((n,))], make)


def _to_sibling_plan(gs):
    n = len(gs)

    def make(ins, outs, sems):
        send_sems, recv_sems = sems
        x, y, c = _mesh_pos()

        def copy(a):
            return _remote(_half(ins[a], 1 - c, 2), outs[a], send_sems.at[a], recv_sems.at[a], (x, y, 1 - c))

        def start():
            for a in range(n):
                copy(a).start()

        def finish():
            for a in range(n):
                copy(a).wait()

        return start, finish

    return _Plan(list(gs), [jax.ShapeDtypeStruct(g.shape[:2] + (g.shape[2] // 2,), g.dtype) for g in gs],
                 [pltpu.SemaphoreType.DMA((n,)), pltpu.SemaphoreType.DMA((n,))], make)


def _chip_exchange_plan(ps):
    n = len(ps)

    def make(ins, outs, sems):
        send_sems, recv_sems = sems
        x, y, c = _mesh_pos()
        chips = _other_chips(x, y)

        def ici(a, k):
            px, py = chips[k]
            return _remote(ins[a].at[2 * px + py], outs[a].at[k], send_sems.at[3 * a + k],
                           recv_sems.at[3 * a + k], (px, py, c))

        def start():
            for a in range(n):
                for k in range(3):
                    ici(a, k).start()

        def finish():
            for a in range(n):
                for k in range(3):
                    ici(a, k).wait()

        return start, finish

    return _Plan(list(ps), [jax.ShapeDtypeStruct((3,) + p.shape[1:], p.dtype) for p in ps],
                 [pltpu.SemaphoreType.DMA((3 * n,)), pltpu.SemaphoreType.DMA((3 * n,))], make)


def _share_plan(halves):
    n = len(halves)

    def make(ins, outs, sems):
        send_sems, recv_sems = sems
        x, y, c = _mesh_pos()

        def d2d(a):
            return _remote(ins[a], outs[a], send_sems.at[a], recv_sems.at[a], (x, y, 1 - c))

        def start():
            for a in range(n):
                d2d(a).start()

        def finish():
            for a in range(n):
                d2d(a).wait()

        return start, finish

    return _Plan(list(halves), [jax.ShapeDtypeStruct(p.shape, p.dtype) for p in halves],
                 [pltpu.SemaphoreType.DMA((n,)), pltpu.SemaphoreType.DMA((n,))], make)


def _rs_sum(own, others, mine, name):
    _, rows, h = own.shape
    tr = rows // 2 if rows % 16 == 0 and rows > 64 else rows

    def body(mine_ref, own_ref, oth_ref, o_ref):
        p = oth_ref[...].astype(F32)
        o_ref[...] = ((own_ref[0].astype(F32) + p[0]) + p[1]) + p[2]

    return pl.pallas_call(
        body, name=name,
        grid_spec=pltpu.PrefetchScalarGridSpec(
            num_scalar_prefetch=1, grid=(rows // tr,),
            in_specs=[pl.BlockSpec((1, tr, h), lambda i, s: (s[0], i, 0)),
                      pl.BlockSpec((3, tr, h), lambda i, s: (0, i, 0))],
            out_specs=pl.BlockSpec((tr, h), lambda i, s: (i, 0))),
        out_shape=jax.ShapeDtypeStruct((rows, h), F32),
        compiler_params=_params(1),
    )(jnp.reshape(mine, (1,)).astype(jnp.int32), own, others)


def _join(mine, theirs, c):
    return jnp.where(c == 0, jnp.concatenate([mine, theirs], axis=1), jnp.concatenate([theirs, mine], axis=1))


LOSS_ROW = 5


def _merge_plans(a, b):
    na_in, na_out, na_sems = len(a.arrays), len(a.out_shape), len(a.sems)

    def make(ins, outs, sems):
        start_a, finish_a = a.make(ins[:na_in], outs[:na_out], sems[:na_sems])
        start_b, finish_b = b.make(ins[na_in:], outs[na_out:], sems[na_sems:])

        def start():
            start_a()
            start_b()

        def finish():
            finish_a()
            finish_b()

        return start, finish

    return _Plan(list(a.arrays) + list(b.arrays), list(a.out_shape) + list(b.out_shape),
                 list(a.sems) + list(b.sems), make)


def _exchange(plan, name):
    n_in, n_out = len(plan.arrays), len(plan.out_shape)

    def body(*refs):
        start, finish = plan.make(refs[:n_in], refs[n_in:n_in + n_out], refs[n_in + n_out:])
        start()
        finish()

    return pl.pallas_call(
        body, name=name, in_specs=[HBM_SPEC] * n_in, out_specs=[HBM_SPEC] * n_out, out_shape=list(plan.out_shape),
        scratch_shapes=list(plan.sems), compiler_params=pltpu.CompilerParams(has_side_effects=True),
    )(*plan.arrays)


def _adamw_halves(mine, theirs, c, w, m, v, name):
    rows, h = mine.shape
    tr = rows // 2 if rows % 16 == 0 else rows

    def body(c_ref, a_ref, b_ref, w_ref, m_ref, v_ref, go_ref, d_ref, m2_ref, v2_ref):
        g = jnp.where(pl.program_id(1) == c_ref[0], a_ref[...], b_ref[...])
        go_ref[...] = g
        d_ref[...], m2_ref[...], v2_ref[...] = _adam_update(g, w_ref[...], m_ref[...], v_ref[...])

    half = pl.BlockSpec((tr, h), lambda i, j, s: (i, 0))
    spec = pl.BlockSpec((tr, h), lambda i, j, s: (i, j))
    return pl.pallas_call(
        body, name=name,
        grid_spec=pltpu.PrefetchScalarGridSpec(num_scalar_prefetch=1, grid=(rows // tr, 2),
                                               in_specs=[half, half, spec, spec, spec], out_specs=[spec] * 4),
        out_shape=[jax.ShapeDtypeStruct((rows, 2 * h), F32)] * 4,
        compiler_params=_params(2),
    )(jnp.reshape(c, (1,)).astype(jnp.int32), mine, theirs, w, m, v)


ADAMW_STEPS = 4


def _adamw_many(items, c, plan=None):
    n = len(items)
    tiles = [it[0].shape[0] // ADAMW_STEPS for it in items]
    h = items[0][0].shape[1]

    def body(c_ref, *refs):
        ins, outs = refs[:5 * n], refs[5 * n:]
        own = pl.program_id(1) == c_ref[0]
        for i in range(n):
            a_ref, b_ref, w_ref, m_ref, v_ref = ins[5 * i:5 * i + 5]
            go_ref, d_ref, m2_ref, v2_ref = outs[4 * i:4 * i + 4]
            g = jnp.where(own, a_ref[...], b_ref[...])
            go_ref[...] = g
            d_ref[...], m2_ref[...], v2_ref[...] = _adam_update(g, w_ref[...], m_ref[...], v_ref[...])

    in_specs, out_specs, out_shape, args = [pl.BlockSpec(memory_space=pltpu.SMEM)], [], [], []
    for (mine, theirs, w, m, v), tr in zip(items, tiles):
        half = pl.BlockSpec((tr, h), lambda i, j: (i, 0))
        full = pl.BlockSpec((tr, h), lambda i, j: (i, j))
        in_specs += [half, half, full, full, full]
        out_specs += [full] * 4
        out_shape += [jax.ShapeDtypeStruct(w.shape, F32)] * 4
        args += [mine, theirs, w, m, v]
    res, extra = _call(body, name="adamw_early", grid=(ADAMW_STEPS, 2), in_specs=in_specs, out_specs=out_specs,
                       out_shape=out_shape, plan=plan)(jnp.reshape(c, (1,)).astype(jnp.int32), *args)
    return [res[4 * i:4 * i + 4] for i in range(n)], extra


def _all_to_all_plan(part):
    def make(ins, outs, sems):
        send_sems, recv_sems, local_sem = sems
        (p_ref,), (slots,) = ins, outs
        x, y, c = _mesh_pos()
        me = 4 * x + 2 * y + c
        peers = [(px, py, pc) for px in (x, 1 - x) for py in (y, 1 - y) for pc in (c, 1 - c)][1:]

        def remote(k, slot):
            return _remote(p_ref, slots.at[slot], send_sems.at[k], recv_sems.at[k], peers[k])

        def local():
            return pltpu.make_async_copy(p_ref, slots.at[me], local_sem)

        def start():
            for k in range(7):
                remote(k, me).start()
            local().start()

        def finish():
            for k, (px, py, pc) in enumerate(peers):
                remote(k, 4 * px + 2 * py + pc).wait_recv()
            for k in range(7):
                remote(k, me).wait_send()
            local().wait()

        return start, finish

    return _Plan([part], [jax.ShapeDtypeStruct((8,) + part.shape, part.dtype)],
                 [pltpu.SemaphoreType.DMA((7,)), pltpu.SemaphoreType.DMA((7,)), pltpu.SemaphoreType.DMA(())], make)


def _sum_slots_adamw(slots, w, m, v):
    def body(s_ref, w_ref, m_ref, v_ref, g_ref, d_ref, m2_ref, v2_ref):
        g = s_ref[0]
        for d in range(1, 8):
            g = g + s_ref[d]
        g_ref[...] = g
        d_ref[...], m2_ref[...], v2_ref[...] = _adam_update(g, w_ref[...], m_ref[...], v_ref[...])

    vm = pl.BlockSpec(memory_space=pltpu.VMEM)
    shape = jax.ShapeDtypeStruct(w.shape, F32)
    return pl.pallas_call(body, name="small_sum_adamw", in_specs=[vm] * 4, out_specs=[vm] * 4,
                          out_shape=[shape] * 4)(slots, w, m, v)


def _columns(gathered):
    return jnp.concatenate([gathered[j] for j in range(N_CHIPS)], axis=1)


def kernel(x, meta_tokens, norm_mix_g, w_in, conv_w, conv_b, conv_ln_g, conv_ln_b, gla_w_gate2, gla_gate_b, gla_norm_g, w_out, norm_ffn_g, w_ffn_gate, w_ffn_up, w_ffn_down, norm_final_g, loss_target, m_meta_tokens, m_norm_mix_g, m_w_in, m_conv_w, m_conv_b, m_conv_ln_g, m_conv_ln_b, m_gla_w_gate2, m_gla_gate_b, m_gla_norm_g, m_w_out, m_norm_ffn_g, m_w_ffn_gate, m_w_ffn_up, m_w_ffn_down, m_norm_final_g, v_meta_tokens, v_norm_mix_g, v_w_in, v_conv_w, v_conv_b, v_conv_ln_g, v_conv_ln_b, v_gla_w_gate2, v_gla_gate_b, v_gla_norm_g, v_w_out, v_norm_ffn_g, v_w_ffn_gate, v_w_ffn_up, v_w_ffn_down, v_norm_final_g):
    ws = dict(zip(WEIGHT_NAMES, (meta_tokens, norm_mix_g, w_in, conv_w, conv_b, conv_ln_g, conv_ln_b, gla_w_gate2,
                                 gla_gate_b, gla_norm_g, w_out, norm_ffn_g, w_ffn_gate, w_ffn_up, w_ffn_down,
                                 norm_final_g)))
    ms = dict(zip(WEIGHT_NAMES, (m_meta_tokens, m_norm_mix_g, m_w_in, m_conv_w, m_conv_b, m_conv_ln_g, m_conv_ln_b,
                                 m_gla_w_gate2, m_gla_gate_b, m_gla_norm_g, m_w_out, m_norm_ffn_g, m_w_ffn_gate,
                                 m_w_ffn_up, m_w_ffn_down, m_norm_final_g)))
    vs = dict(zip(WEIGHT_NAMES, (v_meta_tokens, v_norm_mix_g, v_w_in, v_conv_w, v_conv_b, v_conv_ln_g, v_conv_ln_b,
                                 v_gla_w_gate2, v_gla_gate_b, v_gla_norm_g, v_w_out, v_norm_ffn_g, v_w_ffn_gate,
                                 v_w_ffn_up, v_w_ffn_down, v_norm_final_g)))
    c = lax.axis_index("c")
    shard = lambda d, name: d[name].reshape(d[name].shape[-2:])
    vec = {name: ws[name].reshape(1, -1) for name, _, _, _ in SMALL_PARTS}
    n_ex, seq, _ = x.shape
    lp = HEAD_ROWS + seq
    t = n_ex * lp

    (tgt,), (w_in_g, meta_g, conv_w_g, w2_g) = _pad_head_rows(loss_target, plan=_gather_plan(
        [shard(ws, "w_in").T.astype(BF16)],
        [shard(ws, "meta_tokens"), shard(ws, "conv_w"), shard(ws, "gla_w_gate2")], axes=[1]))
    w_in_t = jnp.concatenate([w_in_g.reshape(D_IN, D), jnp.zeros((D_IN_PAD - D_IN, D), BF16)], axis=0)
    w_in_full = w_in_t.T
    conv_w_full = jnp.concatenate([_columns(conv_w_g), jnp.zeros((32 - CONV_W, C_CONV), F32)], axis=0)
    w2_full = jnp.concatenate([_columns(w2_g), jnp.zeros((128 - RANK, GLA_K), F32)], axis=0).astype(BF16)

    meta = jnp.broadcast_to(_columns(meta_g)[None], (n_ex, N_META, D))
    h0 = jnp.concatenate([jnp.zeros((n_ex, PAD_ROWS, D), F32), meta, x], axis=1).reshape(t, D)
    tgt = tgt.reshape(t, D)
    row_mask = jnp.concatenate([jnp.zeros((n_ex, HEAD_ROWS, 1), F32), jnp.ones((n_ex, seq, 1), F32)],
                               axis=1).reshape(t, 1)

    (u, hn), (w_out_g,) = _in_proj(h0, vec["norm_mix_g"], w_in_full,
                                   plan=_gather_plan([shard(ws, "w_out").astype(BF16)]))
    (yc, y_conv), (gate_g,) = _conv_fwd(
        u, conv_w_full, vec["conv_b"], vec["conv_ln_g"], vec["conv_ln_b"], n_ex, lp,
        plan=_gather_plan([shard(ws, "w_ffn_gate").T.astype(BF16)]))
    (y_gla, states), (up_g,) = _gla_fwd(u, w2_full, vec["gla_gate_b"], vec["gla_norm_g"], n_ex, lp,
                                        plan=_gather_plan([shard(ws, "w_ffn_up").T.astype(BF16)]))
    w_out_full = w_out_g.reshape(D, D)
    w_gate_t, w_up_t = gate_g.reshape(D_FF, D), up_g.reshape(D_FF, D)

    (h1, hn2, gate, up, act), (down_g,) = _mix_out_ffn_up(
        h0, y_conv, y_gla, w_out_full, vec["norm_ffn_g"], w_gate_t.T, w_up_t.T,
        plan=_gather_plan([shard(ws, "w_ffn_down").astype(BF16)]))
    w_down_full = down_g.reshape(D_FF, D)
    dh2, loss, d_final_g = _ffn_down_loss(act, w_down_full, h1, tgt, vec["norm_final_g"], row_mask)
    dgate, dup, dh1, dycat, d_ffn_g = _ffn_bwd(dh2, gate, up, h1, w_down_full.T, w_gate_t, w_up_t, w_out_full.T,
                                                vec["norm_ffn_g"])

    early = ("w_ffn_gate", "w_ffn_up", "w_ffn_down", "w_out")
    ffn_block = lambda g: g.reshape(N_CHIPS, D_FF // N_CHIPS, D)
    g_gate = ffn_block(_wgrad(dgate, hn2, "wgrad_gate"))
    g_up, (gate_sib,) = _wgrad_hosting(dup, hn2, "wgrad_up", _to_sibling_plan([g_gate]))
    g_up = ffn_block(g_up)
    g_down, (up_sib,) = _wgrad_hosting(act, dh2, "wgrad_down", _to_sibling_plan([g_up]))
    g_down = ffn_block(g_down)
    g_out = jnp.concatenate([_wgrad(y_conv, dh1, "wgrad_out_conv"), _wgrad(y_gla, dh1, "wgrad_out_gla")],
                            axis=0).reshape(N_CHIPS, D // N_CHIPS, D)
    cs_gate = _rs_add_halves(g_gate, gate_sib, c, "rs_add_w_ffn_gate")
    cs_up = _rs_add_halves(g_up, up_sib, c, "rs_add_w_ffn_up")
    (du_conv, d_conv_w, d_conv_b, d_ln_g, d_ln_b), (ex_gate, ex_up, down_sib, out_sib) = _conv_bwd(
        dycat, yc, u, conv_w_full, vec["conv_ln_g"], vec["conv_ln_b"], n_ex, lp,
        plan=_merge_plans(_chip_exchange_plan([cs_gate, cs_up]), _to_sibling_plan([g_down, g_out])))
    cs_down = _rs_add_halves(g_down, down_sib, c, "rs_add_w_ffn_down")
    cs_out = _rs_add_halves(g_out, out_sib, c, "rs_add_w_out")
    (du_gla, d_w2, d_gate_b, d_norm_g), (ex_down, ex_out) = _gla_bwd(
        dycat, u, states, w2_full, vec["gla_gate_b"], vec["gla_norm_g"], n_ex, lp,
        plan=_chip_exchange_plan([cs_down, cs_out]))
    mine = 2 * lax.axis_index("x") + lax.axis_index("y")
    halves = [_rs_sum(own, oth, mine, "rs_sum_" + nm)
              for own, oth, nm in zip((cs_gate, cs_up, cs_down, cs_out), (ex_gate, ex_up, ex_down, ex_out), early)]

    d_w_in_t = jnp.concatenate([_wgrad(du_conv, hn, "wgrad_in_conv"), _wgrad(du_gla, hn, "wgrad_in_gla")],
                               axis=0)[:D_IN].reshape(N_CHIPS, D_IN // N_CHIPS, D)
    (in_from_sibling,) = _exchange(_to_sibling_plan([d_w_in_t]), "rs_late_to_sibling")
    in_chip_sum = _rs_add_halves(d_w_in_t, in_from_sibling, c, "rs_add_w_in")
    (dh0, d_mix_g), shared = _in_proj_bwd(
        du_conv, du_gla, w_in_t[:2 * C_CONV], w_in_t[2 * C_CONV:], h0, dh1, vec["norm_mix_g"],
        plan=_merge_plans(_share_plan(halves), _chip_exchange_plan([in_chip_sum])))
    dh0 = dh0.reshape(n_ex, lp, D)
    grad_x = dh0[:, HEAD_ROWS:]

    out = {"grad": {}, "delta": {}, "new_m": {}, "new_v": {}}

    def update(name, g=None, halves=None, transposed=False):
        shape = ws[name].shape
        lay = (lambda a: a.T) if transposed else (lambda a: a)
        w2d, m2d, v2d = lay(shard(ws, name)), lay(shard(ms, name)), lay(shard(vs, name))
        if halves is not None:
            res = _adamw_halves(*halves, c, w2d, m2d, v2d, "adamw_" + name)
        else:
            res = [g, *_adamw(g, w2d, m2d, v2d, "adamw_" + name)]
        for kind, a in zip(("grad", "delta", "new_m", "new_v"), res):
            out[kind][name] = lay(a).reshape(shape)

    small = {"norm_mix_g": d_mix_g, "norm_ffn_g": d_ffn_g, "norm_final_g": d_final_g, "conv_b": d_conv_b,
             "conv_ln_g": d_ln_g, "conv_ln_b": d_ln_b, "gla_gate_b": d_gate_b, "gla_norm_g": d_norm_g}
    small_shapes = {name: ws[name].shape for name, _, _, _ in SMALL_PARTS}
    part = lax.dynamic_update_slice(_pack_small(small), loss[:, :1], (LOSS_ROW, 0))
    part = jnp.concatenate([part, jnp.sum(dh0[:, PAD_ROWS:HEAD_ROWS], axis=0), d_conv_w.reshape(16, D),
                            d_w2[:RANK].reshape(4, D), jnp.zeros((4, D), F32)], axis=0)

    early_layout = (("w_ffn_gate", True), ("w_ffn_up", True), ("w_ffn_down", False), ("w_out", False))
    items = []
    for (name, transposed), mine_half, their_half in zip(early_layout, halves, shared):
        lay = (lambda a: a.T) if transposed else (lambda a: a)
        items.append((mine_half, their_half, lay(shard(ws, name)), lay(shard(ms, name)), lay(shard(vs, name))))
    updated, (slots,) = _adamw_many(items, c, plan=_all_to_all_plan(part))
    for (name, transposed), res in zip(early_layout, updated):
        lay = (lambda a: a.T) if transposed else (lambda a: a)
        for kind, a in zip(("grad", "delta", "new_m", "new_v"), res):
            out[kind][name] = lay(a).reshape(ws[name].shape)

    in_half = _rs_sum(in_chip_sum, shared[4], mine, "rs_sum_w_in")
    (in_shared,) = _exchange(_share_plan([in_half]), "rs_late_share")
    update("w_in", halves=(in_half, in_shared), transposed=True)

    tall = lambda a: jnp.concatenate([a, jnp.zeros((part.shape[0] - SMALL_ROWS, D), F32)], axis=0)
    g_s, d_s, m_s, v_s = _sum_slots_adamw(slots, tall(_pack_small(ws)), tall(_pack_small(ms)), tall(_pack_small(vs)))
    for kind, slab in (("grad", g_s), ("delta", d_s), ("new_m", m_s), ("new_v", v_s)):
        out[kind].update(_unpack_small(slab, small_shapes))
    loss = g_s[LOSS_ROW, 0]
    block = lambda a, width: lax.dynamic_slice_in_dim(a, mine * width, width, axis=1)
    update("meta_tokens", g=block(g_s[8:24], D // N_CHIPS))
    update("conv_w", g=block(g_s[24:40].reshape(32, C_CONV), C_CONV // N_CHIPS)[:CONV_W])
    update("gla_w_gate2", g=block(g_s[40:44].reshape(RANK, GLA_K), GLA_K // N_CHIPS))

    return (loss, grad_x, *[out[kind][name] for kind in ("grad", "delta", "new_m", "new_v") for name in WEIGHT_NAMES])
```

```python
import functools
from typing import Any, Callable, NamedTuple, Sequence

import jax
import jax.numpy as jnp
from jax import lax
from jax.experimental import pallas as pl
from jax.experimental.pallas import tpu as pltpu

F32 = jnp.float32
BF16 = jnp.bfloat16
MESH = pl.DeviceIdType.MESH

D = 1024
N_META = 16
C_CONV = 512
CONV_W = 31
GLA_K = 256
GLA_V = 512
N_HEADS = 4
DK = 64
DV = 128
RANK = 16
CHUNK = 64
PAD_ROWS = CHUNK - N_META
HEAD_ROWS = CHUNK
D_IN = 2576
D_IN_PAD = 2688
D_GLA_IN = D_IN_PAD - 2 * C_CONV
D_FF = 2816
RMS_EPS = 1e-6
LN_EPS = 1e-5
GATE_TAU = 16.0
N_CHIPS = 4

ADAM_LR = 0.001
ADAM_B1 = 0.9
ADAM_B2 = 0.999
ADAM_EPS = 1e-08
ADAM_WD = 0.01
ADAM_STEP = 10

V7X_VMEM_BYTES = 64 * 1024 * 1024
VMEM_LIMIT = V7X_VMEM_BYTES - 8 * 1024 * 1024

SLAB_ROWS = 3072
HALF_ROWS = SLAB_ROWS // 2
SMALL_ROWS = 8


def _dot(a, b):
    return jnp.dot(a, b, preferred_element_type=F32)


def _dot_nt(a, b):
    return lax.dot_general(a, b, (((1,), (1,)), ((), ())), preferred_element_type=F32)


def _dot_tn(a, b):
    return lax.dot_general(a, b, (((0,), (0,)), ((), ())), preferred_element_type=F32)


def _sigmoid(x):
    return 1.0 / (1.0 + jnp.exp(-x))


def _const_spec(shape):
    return pl.BlockSpec(shape, lambda *_: (0,) * len(shape), pipeline_mode=pl.Buffered(1))


def _acc_spec(shape):
    return pl.BlockSpec(shape, lambda *_: (0,) * len(shape))


def _params(n_axes):
    return pltpu.CompilerParams(dimension_semantics=("arbitrary",) * n_axes, vmem_limit_bytes=VMEM_LIMIT)


def _row_tile(t, want):
    for r in (want, 384, 192, 128, 64):
        if r <= want and t % r == 0:
            return r
    raise ValueError(f"no row tile for {t}")


ROW_PART = 128


def _row_parts(r):
    if r % ROW_PART:
        return [slice(None)]
    return [pl.ds(i * ROW_PART, ROW_PART) for i in range(r // ROW_PART)]


def _in_lockstep(bodies):
    live = list(bodies)
    while live:
        still = []
        for g in live:
            try:
                next(g)
                still.append(g)
            except StopIteration:
                pass
        live = still


class _Plan(NamedTuple):
    arrays: Sequence[Any]
    out_shape: Sequence[Any]
    sems: Sequence[Any]
    make: Callable


def _call(body, *, name, grid, in_specs, out_specs, out_shape, scratch_shapes=(), plan=None):
    n_in, n_out, n_scr = len(in_specs), len(out_specs), len(scratch_shapes)
    if plan is None:
        plan = _Plan([], [], [], lambda ins, outs, sems: (lambda: None, lambda: None))
    nx_in, nx_out = len(plan.arrays), len(plan.out_shape)

    def hosted(*refs):
        ins, xins = refs[:n_in], refs[n_in:n_in + nx_in]
        o0 = n_in + nx_in
        outs, xouts = refs[o0:o0 + n_out], refs[o0 + n_out:o0 + n_out + nx_out]
        s0 = o0 + n_out + nx_out
        scr, sems = refs[s0:s0 + n_scr], refs[s0 + n_scr:]
        ids = [pl.program_id(a) for a in range(len(grid))]
        first = functools.reduce(jnp.logical_and, [i == 0 for i in ids])
        last = functools.reduce(jnp.logical_and, [i == g - 1 for i, g in zip(ids, grid)])
        start, finish = plan.make(xins, xouts, sems)
        pl.when(first)(start)
        body(*ins, *outs, *scr)
        pl.when(last)(finish)

    call = pl.pallas_call(
        hosted, name=name, grid=grid, in_specs=list(in_specs) + [HBM_SPEC] * nx_in,
        out_specs=list(out_specs) + [HBM_SPEC] * nx_out, out_shape=list(out_shape) + list(plan.out_shape),
        scratch_shapes=list(scratch_shapes) + list(plan.sems),
        compiler_params=pltpu.CompilerParams(dimension_semantics=("arbitrary",) * len(grid),
                                             vmem_limit_bytes=VMEM_LIMIT, has_side_effects=nx_in > 0))

    def run(*args):
        res = call(*args, *plan.arrays)
        return res[:n_out], res[n_out:]

    return run


def _pad_head_rows(a, plan=None):
    n_ex, seq, _ = a.shape
    nc = (HEAD_ROWS + seq) // CHUNK

    def body(a_ref, o_ref):
        o_ref[...] = jnp.where(pl.program_id(0) > 0, a_ref[...], 0.0)

    return _call(
        body, name="pad_head_rows", grid=(nc,),
        in_specs=[pl.BlockSpec((n_ex, CHUNK, D), lambda n: (0, jnp.maximum(n - 1, 0), 0))],
        out_specs=[pl.BlockSpec((n_ex, CHUNK, D), lambda n: (0, n, 0))],
        out_shape=[jax.ShapeDtypeStruct((n_ex, HEAD_ROWS + seq, D), F32)],
        plan=plan,
    )(a)


def _in_proj(h0, g_mix, w_in, plan=None):
    t = h0.shape[0]
    r = _row_tile(t, 384)

    def body(h_ref, g_ref, w_ref, u_ref, hn_ref):
        def part(rows):
            h = h_ref[rows, :]
            rstd = lax.rsqrt(jnp.mean(h * h, axis=-1, keepdims=True) + RMS_EPS)
            hn = (h * rstd * g_ref[...]).astype(BF16)
            hn_ref[rows, :] = hn
            yield
            u_ref[rows, :] = _dot(hn, w_ref[...])

        _in_lockstep(part(rows) for rows in _row_parts(r))

    return _call(
        body, name="in_proj", grid=(t // r,),
        in_specs=[pl.BlockSpec((r, D), lambda i: (i, 0)), _const_spec((1, D)), _const_spec((D, D_IN_PAD))],
        out_specs=[pl.BlockSpec((r, D_IN_PAD), lambda i: (i, 0)), pl.BlockSpec((r, D), lambda i: (i, 0))],
        out_shape=[jax.ShapeDtypeStruct((t, D_IN_PAD), F32), jax.ShapeDtypeStruct((t, D), BF16)],
        plan=plan,
    )(h0, g_mix, w_in)


CONV_TILE = 192
CONV_SUB = 32
CONV_LEAD = CONV_SUB - (CONV_W - 1)
SUBLANES = 8


def _shifted_copies(src, dst, r):
    for s in range(1, SUBLANES):
        dst[s - 1] = src[s:s + r + CONV_SUB - SUBLANES, :]


def _shifted_rows(src, shifted, start):
    base, s = SUBLANES * (start // SUBLANES), start % SUBLANES
    if s == 0:
        return src[base:base + CONV_SUB, :]
    return shifted[s - 1, base:base + CONV_SUB, :]


def _conv_fwd(u, conv_w, conv_b, ln_g, ln_b, n_ex, lp, plan=None):
    r = CONV_TILE
    nt = lp // r
    hb = r // CONV_SUB

    def body(cur_ref, prev_ref, w_ref, b_ref, lg_ref, lb_ref, yc_ref, y_ref, glu, glu_sh):
        i = pl.program_id(1)
        cur = cur_ref[...]
        glu[CONV_SUB:CONV_SUB + r, :] = cur[:, :C_CONV] * _sigmoid(cur[:, C_CONV:])
        pv = prev_ref[...]
        halo = pv[:, :C_CONV] * _sigmoid(pv[:, C_CONV:])
        glu[0:CONV_SUB, :] = jnp.where(i > 0, halo, 0.0)
        _shifted_copies(glu, glu_sh, r)
        w = w_ref[...]
        for j in range(r // CONV_SUB):
            r0 = j * CONV_SUB
            acc = jnp.zeros((CONV_SUB, C_CONV), F32) + b_ref[...]
            for k in range(CONV_W):
                acc = acc + w[k:k + 1, :] * _shifted_rows(glu, glu_sh, r0 + CONV_LEAD + k)
            mu = jnp.mean(acc, axis=-1, keepdims=True)
            cen = acc - mu
            var = jnp.mean(cen * cen, axis=-1, keepdims=True)
            out = cen * lax.rsqrt(var + LN_EPS) * lg_ref[...] + lb_ref[...]
            y = out * _sigmoid(out)
            row = i * r + r0 + lax.broadcasted_iota(jnp.int32, (CONV_SUB, 1), 0)
            y = jnp.where(row >= PAD_ROWS, y, 0.0)
            yc_ref[r0:r0 + CONV_SUB, :] = acc
            y_ref[r0:r0 + CONV_SUB, :] = y.astype(BF16)

    t = n_ex * lp
    return _call(
        body, name="conv_fwd", grid=(n_ex, nt),
        in_specs=[pl.BlockSpec((r, 2 * C_CONV), lambda b, i: (b * nt + i, 0)),
                  pl.BlockSpec((CONV_SUB, 2 * C_CONV), lambda b, i: (jnp.maximum((b * nt + i) * hb - 1, 0), 0)),
                  _const_spec((32, C_CONV)), _const_spec((1, C_CONV)), _const_spec((1, C_CONV)), _const_spec((1, C_CONV))],
        out_specs=[pl.BlockSpec((r, C_CONV), lambda b, i: (b * nt + i, 0)),
                   pl.BlockSpec((r, C_CONV), lambda b, i: (b * nt + i, 0))],
        out_shape=[jax.ShapeDtypeStruct((t, C_CONV), F32), jax.ShapeDtypeStruct((t, C_CONV), BF16)],
        scratch_shapes=[pltpu.VMEM((r + CONV_SUB, C_CONV), F32),
                        pltpu.VMEM((SUBLANES - 1, r + CONV_SUB - SUBLANES, C_CONV), F32)],
        plan=plan,
    )(u, u, conv_w, conv_b, ln_g, ln_b)


def _gla_gates(lr, w2, gb, first_chunk):
    z = _dot(lr.astype(BF16), w2) + gb
    a = (jnp.minimum(z, 0.0) - jnp.log(1.0 + jnp.exp(-jnp.abs(z)))) * (1.0 / GATE_TAU)
    row = lax.broadcasted_iota(jnp.int32, (CHUNK, 1), 0)
    live = jnp.logical_or(jnp.logical_not(first_chunk), row >= PAD_ROWS)
    return z, jnp.where(live, a, 0.0), live


def _tri(lower):
    i = lax.broadcasted_iota(jnp.int32, (CHUNK, CHUNK), 0)
    j = lax.broadcasted_iota(jnp.int32, (CHUNK, CHUNK), 1)
    return (i >= j) if lower else (i <= j)


def _gla_fwd_per_head(u, w2, gb, ng, n_ex, lp, plan=None):
    nc = lp // CHUNK
    t = n_ex * lp

    def body(qk_ref, v_ref, g_ref, lr_ref, w2_ref, gb_ref, ng_ref, y_ref, st_ref, state):
        n = pl.program_id(0)

        @pl.when(n == 0)
        def _():
            state[...] = jnp.zeros_like(state)

        causal = _tri(True)
        for e in range(n_ex):
            st = state[e]
            st_ref[e] = st
            qk = qk_ref[e]
            q, k = qk[:, :GLA_K], qk[:, GLA_K:]
            _, a, _ = _gla_gates(lr_ref[e], w2_ref[...], gb_ref[...], n == 0)
            b = jnp.dot(causal.astype(F32), a, preferred_element_type=F32, precision=lax.Precision.HIGHEST)
            bl = b[CHUNK - 1:CHUNK, :]
            q_in = (q * (DK ** -0.5) * jnp.exp(b)).astype(BF16)
            k_in = (k * jnp.exp(-b)).astype(BF16)
            k_dec = (k * jnp.exp(bl - b)).astype(BF16)
            decay = jnp.exp(bl)
            v = v_ref[e]
            g = g_ref[e]
            st_b = st.astype(BF16)
            ys, new = [], []
            for h in range(N_HEADS):
                ks = slice(h * DK, (h + 1) * DK)
                vs = slice(h * DV, (h + 1) * DV)
                vh = v[:, vs].astype(BF16)
                s = jnp.where(causal, _dot_nt(q_in[:, ks], k_in[:, ks]), 0.0)
                o = _dot(s.astype(BF16), vh) + _dot_nt(q_in[:, ks], st_b[:, ks])
                new.append(decay[:, ks] * st[:, ks] + _dot_tn(vh, k_dec[:, ks]))
                rstd = lax.rsqrt(jnp.mean(o * o, axis=-1, keepdims=True) + RMS_EPS)
                gh = g[:, vs]
                ys.append(o * rstd * ng_ref[...] * (gh * _sigmoid(gh)))
            state[e] = jnp.concatenate(new, axis=1)
            y_ref[e] = jnp.concatenate(ys, axis=1).astype(BF16)

    u3 = u.reshape(n_ex, lp, D_IN_PAD)
    blk = lambda w, col: pl.BlockSpec((n_ex, CHUNK, w), lambda n: (0, n, col))
    (y, states), extra = _call(
        body, name="gla_fwd", grid=(nc,),
        in_specs=[blk(2 * GLA_K, 2), blk(GLA_V, 3), blk(GLA_V, 4), blk(128, 20),
                  _const_spec((128, GLA_K)), _const_spec((1, GLA_K)), _const_spec((1, DV))],
        out_specs=[blk(GLA_V, 0), pl.BlockSpec((n_ex, DV, GLA_K), lambda n: (0, n, 0))],
        out_shape=[jax.ShapeDtypeStruct((n_ex, lp, GLA_V), BF16),
                   jax.ShapeDtypeStruct((n_ex, nc * DV, GLA_K), F32)],
        scratch_shapes=[pltpu.VMEM((n_ex, DV, GLA_K), F32)],
        plan=plan,
    )(u3, u3, u3, u3, w2, gb, ng)
    return (y.reshape(t, GLA_V), states), extra


FFN_TILE = 192


def _mix_out_ffn_up(h0, y_conv, y_gla, w_out, g_ffn, w_gate, w_up, plan=None):
    t = h0.shape[0]
    r = _row_tile(t, 384)

    def body(h0_ref, yc_ref, yg_ref, wo_ref, g_ref, wg_ref, wu_ref, h1_ref, hn_ref, gate_ref, up_ref, act_ref):
        h1 = h0_ref[...] + _dot(yc_ref[...], wo_ref[0:C_CONV, :]) + _dot(yg_ref[...], wo_ref[C_CONV:D, :])
        h1_ref[...] = h1
        rstd = lax.rsqrt(jnp.mean(h1 * h1, axis=-1, keepdims=True) + RMS_EPS)
        hn = (h1 * rstd * g_ref[...]).astype(BF16)
        hn_ref[...] = hn
        gate = _dot(hn, wg_ref[...])
        up = _dot(hn, wu_ref[...])
        gate_ref[...] = gate
        up_ref[...] = up
        act_ref[...] = (gate * _sigmoid(gate) * up).astype(BF16)

    rows = lambda w: pl.BlockSpec((r, w), lambda i: (i, 0))
    return _call(
        body, name="mix_out_ffn_up", grid=(t // r,),
        in_specs=[rows(D), rows(C_CONV), rows(GLA_V), _const_spec((D, D)), _const_spec((1, D)),
                  _const_spec((D, D_FF)), _const_spec((D, D_FF))],
        out_specs=[rows(D), rows(D), rows(D_FF), rows(D_FF), rows(D_FF)],
        out_shape=[jax.ShapeDtypeStruct((t, D), F32), jax.ShapeDtypeStruct((t, D), BF16),
                   jax.ShapeDtypeStruct((t, D_FF), F32), jax.ShapeDtypeStruct((t, D_FF), F32),
                   jax.ShapeDtypeStruct((t, D_FF), BF16)],
        plan=plan,
    )(h0, y_conv, y_gla, w_out, g_ffn, w_gate, w_up)


def _ffn_down_loss(act, w_down, h1, target, g_final, row_mask):
    t = h1.shape[0]
    r = _row_tile(t, 384)

    def body(act_ref, wd_ref, h1_ref, tgt_ref, gf_ref, mask_ref, dh2_ref, loss_ref, dgf_ref):
        @pl.when(pl.program_id(0) == 0)
        def _():
            loss_ref[...] = jnp.zeros_like(loss_ref)
            dgf_ref[...] = jnp.zeros_like(dgf_ref)

        gf = gf_ref[...]

        def part(rows):
            h2 = h1_ref[rows, :] + _dot(act_ref[rows, :], wd_ref[...])
            yield
            rstd = lax.rsqrt(jnp.mean(h2 * h2, axis=-1, keepdims=True) + RMS_EPS)
            nrm = h2 * rstd
            err = (nrm * gf - tgt_ref[rows, :]) * mask_ref[rows, :]
            loss_ref[...] += jnp.sum(err * err) * (0.5 / D)
            dy = err * (1.0 / D)
            dgf_ref[...] += jnp.sum(dy * nrm, axis=0, keepdims=True)
            dn = dy * gf
            dh2_ref[rows, :] = rstd * (dn - nrm * jnp.mean(dn * nrm, axis=-1, keepdims=True))

        _in_lockstep(part(rows) for rows in _row_parts(r))

    rows = lambda w: pl.BlockSpec((r, w), lambda i: (i, 0))
    return pl.pallas_call(
        body, name="ffn_down_loss", grid=(t // r,),
        in_specs=[rows(D_FF), _const_spec((D_FF, D)), rows(D), rows(D), _const_spec((1, D)), rows(1)],
        out_specs=[rows(D), _acc_spec((1, 128)), _acc_spec((1, D))],
        out_shape=[jax.ShapeDtypeStruct((t, D), F32), jax.ShapeDtypeStruct((1, 128), F32),
                   jax.ShapeDtypeStruct((1, D), F32)],
        compiler_params=_params(1),
    )(act, w_down, h1, target, g_final, row_mask)


def _ffn_bwd(dh2, gate, up, h1, w_down_t, w_gate_t, w_up_t, w_out_t, g_ffn):
    t = h1.shape[0]
    r = _row_tile(t, FFN_TILE)

    def body(dh2_ref, gate_ref, up_ref, h1_ref, wd_ref, wg_ref, wu_ref, wo_ref, g_ref,
             dgate_ref, dup_ref, dh1_ref, dycat_ref, dg_ref):
        @pl.when(pl.program_id(0) == 0)
        def _():
            dg_ref[...] = jnp.zeros_like(dg_ref)

        dh2 = dh2_ref[...]
        dact = _dot(dh2.astype(BF16), wd_ref[...])
        gate = gate_ref[...]
        sg = _sigmoid(gate)
        dgate = (dact * up_ref[...] * (sg * (1.0 + gate * (1.0 - sg)))).astype(BF16)
        dup = (dact * (gate * sg)).astype(BF16)
        dgate_ref[...] = dgate
        dup_ref[...] = dup
        dhn = _dot(dgate, wg_ref[...]) + _dot(dup, wu_ref[...])
        h1 = h1_ref[...]
        rstd = lax.rsqrt(jnp.mean(h1 * h1, axis=-1, keepdims=True) + RMS_EPS)
        nrm = h1 * rstd
        dg_ref[...] += jnp.sum(dhn * nrm, axis=0, keepdims=True)
        dn = dhn * g_ref[...]
        dh1 = dh2 + rstd * (dn - nrm * jnp.mean(dn * nrm, axis=-1, keepdims=True))
        dh1_ref[...] = dh1
        dycat_ref[...] = _dot(dh1.astype(BF16), wo_ref[...])

    rows = lambda w: pl.BlockSpec((r, w), lambda i: (i, 0))
    return pl.pallas_call(
        body, name="ffn_bwd", grid=(t // r,),
        in_specs=[rows(D), rows(D_FF), rows(D_FF), rows(D), _const_spec((D, D_FF)), _const_spec((D_FF, D)),
                  _const_spec((D_FF, D)), _const_spec((D, D)), _const_spec((1, D))],
        out_specs=[rows(D_FF), rows(D_FF), rows(D), rows(D), _acc_spec((1, D))],
        out_shape=[jax.ShapeDtypeStruct((t, D_FF), BF16), jax.ShapeDtypeStruct((t, D_FF), BF16),
                   jax.ShapeDtypeStruct((t, D), F32), jax.ShapeDtypeStruct((t, D), F32),
                   jax.ShapeDtypeStruct((1, D), F32)],
        compiler_params=_params(1),
    )(dh2, gate, up, h1, w_down_t, w_gate_t, w_up_t, w_out_t, g_ffn)


def _conv_bwd(dycat, yc, u, conv_w, ln_g, ln_b, n_ex, lp, plan=None):
    r = CONV_TILE
    nt = lp // r
    hb = r // CONV_SUB
    nsub = r // CONV_SUB

    def ln_bwd(dy, yc_rows, live, lg, lb):
        mu = jnp.mean(yc_rows, axis=-1, keepdims=True)
        cen = yc_rows - mu
        rs = lax.rsqrt(jnp.mean(cen * cen, axis=-1, keepdims=True) + LN_EPS)
        yn = cen * rs
        out = yn * lg + lb
        so = _sigmoid(out)
        dout = jnp.where(live, dy * (so * (1.0 + out * (1.0 - so))), 0.0)
        dyn = dout * lg
        dyc = rs * (dyn - jnp.mean(dyn, axis=-1, keepdims=True) - yn * jnp.mean(dyn * yn, axis=-1, keepdims=True))
        return dyc, dout, yn

    def body(dy_ref, dyn_ref, yc_ref, ycn_ref, cur_ref, prev_ref, w_ref, lg_ref, lb_ref,
             du_ref, dw_ref, db_ref, dlg_ref, dlb_ref, glu, dycs, dwacc, glu_sh, dycs_sh):
        b = pl.program_id(0)
        i = pl.program_id(1)
        first = jnp.logical_and(b == 0, i == 0)

        @pl.when(first)
        def _():
            dwacc[...] = jnp.zeros_like(dwacc)
            db_ref[...] = jnp.zeros_like(db_ref)
            dlg_ref[...] = jnp.zeros_like(dlg_ref)
            dlb_ref[...] = jnp.zeros_like(dlb_ref)

        lg, lb = lg_ref[...], lb_ref[...]
        cur = cur_ref[...]
        sig = _sigmoid(cur[:, C_CONV:])
        glu[CONV_SUB:CONV_SUB + r, :] = cur[:, :C_CONV] * sig
        pv = prev_ref[...]
        glu[0:CONV_SUB, :] = jnp.where(i > 0, pv[:, :C_CONV] * _sigmoid(pv[:, C_CONV:]), 0.0)

        row = i * r + lax.broadcasted_iota(jnp.int32, (r, 1), 0)
        dyc, dout, yn = ln_bwd(dy_ref[...], yc_ref[...], row >= PAD_ROWS, lg, lb)
        dycs[0:r, :] = dyc
        dycn, _, _ = ln_bwd(dyn_ref[...], ycn_ref[...], i < nt - 1, lg, lb)
        dycs[r:r + CONV_SUB, :] = dycn
        db_ref[...] += jnp.sum(dyc, axis=0, keepdims=True)
        dlg_ref[...] += jnp.sum(dout * yn, axis=0, keepdims=True)
        dlb_ref[...] += jnp.sum(dout, axis=0, keepdims=True)

        _shifted_copies(glu, glu_sh, r)
        _shifted_copies(dycs, dycs_sh, r)
        w = w_ref[...]
        for j in range(nsub):
            r0 = j * CONV_SUB
            dblk = dycs[r0:r0 + CONV_SUB, :]
            dglu = jnp.zeros((CONV_SUB, C_CONV), F32)
            for k in range(CONV_W):
                dglu = dglu + w[k:k + 1, :] * _shifted_rows(dycs, dycs_sh, r0 + (CONV_W - 1) - k)
                prod = dblk * _shifted_rows(glu, glu_sh, r0 + CONV_LEAD + k)
                dwacc[k] += prod.reshape(CONV_SUB // SUBLANES, SUBLANES, C_CONV).sum(axis=0)
            sg = sig[r0:r0 + CONV_SUB, :]
            cv = cur[r0:r0 + CONV_SUB, :C_CONV]
            du_ref[r0:r0 + CONV_SUB, :C_CONV] = (dglu * sg).astype(BF16)
            du_ref[r0:r0 + CONV_SUB, C_CONV:] = (dglu * cv * sg * (1.0 - sg)).astype(BF16)

        @pl.when(jnp.logical_and(b == n_ex - 1, i == nt - 1))
        def _():
            dw_ref[...] = jnp.sum(dwacc[...], axis=1)

    t = n_ex * lp
    cur_rows = lambda w, col: pl.BlockSpec((r, w), lambda b, i: (b * nt + i, col))
    nxt_rows = lambda w, col: pl.BlockSpec(
        (CONV_SUB, w), lambda b, i: (jnp.minimum((b * nt + i + 1) * hb, n_ex * nt * hb - 1), col))
    return _call(
        body, name="conv_bwd", grid=(n_ex, nt),
        in_specs=[cur_rows(C_CONV, 0), nxt_rows(C_CONV, 0), cur_rows(C_CONV, 0), nxt_rows(C_CONV, 0),
                  cur_rows(2 * C_CONV, 0),
                  pl.BlockSpec((CONV_SUB, 2 * C_CONV), lambda b, i: (jnp.maximum((b * nt + i) * hb - 1, 0), 0)),
                  _const_spec((32, C_CONV)), _const_spec((1, C_CONV)), _const_spec((1, C_CONV))],
        out_specs=[cur_rows(2 * C_CONV, 0), _acc_spec((32, C_CONV)), _acc_spec((1, C_CONV)),
                   _acc_spec((1, C_CONV)), _acc_spec((1, C_CONV))],
        out_shape=[jax.ShapeDtypeStruct((t, 2 * C_CONV), BF16), jax.ShapeDtypeStruct((32, C_CONV), F32),
                   jax.ShapeDtypeStruct((1, C_CONV), F32), jax.ShapeDtypeStruct((1, C_CONV), F32),
                   jax.ShapeDtypeStruct((1, C_CONV), F32)],
        scratch_shapes=[pltpu.VMEM((r + CONV_SUB, C_CONV), F32), pltpu.VMEM((r + CONV_SUB, C_CONV), F32),
                        pltpu.VMEM((32, 8, C_CONV), F32),
                        pltpu.VMEM((SUBLANES - 1, r + CONV_SUB - SUBLANES, C_CONV), F32),
                        pltpu.VMEM((SUBLANES - 1, r + CONV_SUB - SUBLANES, C_CONV), F32)],
        plan=plan,
    )(dycat, dycat, yc, yc, u, u, conv_w, ln_g, ln_b)


def _gla_bwd_per_head(dycat, u, states, w2, gb, ng, n_ex, lp, plan=None):
    nc = lp // CHUNK
    t = n_ex * lp

    def body(dy_ref, qk_ref, v_ref, g_ref, lr_ref, st_ref, w2_ref, gb_ref, ng_ref,
             du_ref, dw2_ref, dgb_ref, dng_ref, dstate):
        n = pl.program_id(0)
        chunk = nc - 1 - n

        @pl.when(n == 0)
        def _():
            dw2_ref[...] = jnp.zeros_like(dw2_ref)
            dgb_ref[...] = jnp.zeros_like(dgb_ref)
            dng_ref[...] = jnp.zeros_like(dng_ref)
            dstate[...] = jnp.zeros_like(dstate)

        for e in range(n_ex):
            one_example(e, chunk, dy_ref, qk_ref, v_ref, g_ref, lr_ref, st_ref, w2_ref, gb_ref, ng_ref,
                        du_ref, dw2_ref, dgb_ref, dng_ref, dstate)

    def one_example(e, chunk, dy_ref, qk_ref, v_ref, g_ref, lr_ref, st_ref, w2_ref, gb_ref, ng_ref,
                    du_ref, dw2_ref, dgb_ref, dng_ref, dstate):
        dy_ref, qk_ref, v_ref, g_ref, lr_ref, st_ref = (r.at[e] for r in (dy_ref, qk_ref, v_ref, g_ref, lr_ref, st_ref))
        du_ref, dstate = du_ref.at[e], dstate.at[e]
        qk = qk_ref[...]
        q, k = qk[:, :GLA_K], qk[:, GLA_K:]
        lr = lr_ref[...]
        z, a, live = _gla_gates(lr, w2_ref[...], gb_ref[...], chunk == 0)
        causal = _tri(True)
        b = jnp.dot(causal.astype(F32), a, preferred_element_type=F32, precision=lax.Precision.HIGHEST)
        bl = b[CHUNK - 1:CHUNK, :]
        e_pos, e_neg, e_dec = jnp.exp(b), jnp.exp(-b), jnp.exp(bl - b)
        q_f = q * (DK ** -0.5) * e_pos
        k_f = k * e_neg
        kd_f = k * e_dec
        q_in, k_in, k_dec = q_f.astype(BF16), k_f.astype(BF16), kd_f.astype(BF16)
        decay = jnp.exp(bl)
        v = v_ref[...]
        g = g_ref[...]
        dy = dy_ref[...]
        ngv = ng_ref[...]
        st = st_ref[...]
        st_b = st.astype(BF16)
        dst = dstate[...]
        dst_b = dst.astype(BF16)
        dqs, dks, dvs, dgs, dbs, dbls, new_dst = [], [], [], [], [], [], []
        dng = jnp.zeros((1, DV), F32)
        for h in range(N_HEADS):
            ks = slice(h * DK, (h + 1) * DK)
            vs = slice(h * DV, (h + 1) * DV)
            qh, kh, kdh = q_in[:, ks], k_in[:, ks], k_dec[:, ks]
            vh = v[:, vs].astype(BF16)
            s = jnp.where(causal, _dot_nt(qh, kh), 0.0).astype(BF16)
            o = _dot(s, vh) + _dot_nt(qh, st_b[:, ks])
            rstd = lax.rsqrt(jnp.mean(o * o, axis=-1, keepdims=True) + RMS_EPS)
            nrm = o * rstd
            gh = g[:, vs]
            sg = _sigmoid(gh)
            dyh = dy[:, vs]
            dgs.append(dyh * nrm * ngv * (sg * (1.0 + gh * (1.0 - sg))))
            dt = dyh * (gh * sg)
            dng = dng + jnp.sum(dt * nrm, axis=0, keepdims=True)
            dn = dt * ngv
            do = (rstd * (dn - nrm * jnp.mean(dn * nrm, axis=-1, keepdims=True))).astype(BF16)
            da = jnp.where(causal, _dot_nt(do, vh), 0.0).astype(BF16)
            dvs.append(_dot_tn(s, do) + _dot_nt(kdh, dst_b[:, ks]))
            dq_in = _dot(da, kh) + _dot(do, st_b[:, ks])
            dk_in = _dot_tn(da, qh)
            dk_dec = _dot(vh, dst_b[:, ks])
            new_dst.append(_dot_tn(do, qh) + decay[:, ks] * dst[:, ks])
            dbls.append(jnp.sum(dk_dec * kd_f[:, ks], axis=0, keepdims=True)
                        + decay[:, ks] * jnp.sum(dst[:, ks] * st[:, ks], axis=0, keepdims=True))
            dqs.append(dq_in * (DK ** -0.5) * e_pos[:, ks])
            dks.append(dk_in * e_neg[:, ks] + dk_dec * e_dec[:, ks])
            dbs.append(dq_in * q_f[:, ks] - dk_in * k_f[:, ks] - dk_dec * kd_f[:, ks])
        dstate[...] = jnp.concatenate(new_dst, axis=1)
        row = lax.broadcasted_iota(jnp.int32, (CHUNK, 1), 0)
        db = jnp.concatenate(dbs, axis=1) + jnp.where(row == CHUNK - 1, jnp.concatenate(dbls, axis=1), 0.0)
        da_log = jnp.dot(_tri(False).astype(F32), db, preferred_element_type=F32, precision=lax.Precision.HIGHEST)
        dz = jnp.where(live, da_log * (1.0 - _sigmoid(z)) * (1.0 / GATE_TAU), 0.0)
        dz_b = dz.astype(BF16)
        du_ref[:, 0:GLA_K] = jnp.concatenate(dqs, axis=1).astype(BF16)
        du_ref[:, GLA_K:2 * GLA_K] = jnp.concatenate(dks, axis=1).astype(BF16)
        du_ref[:, 2 * GLA_K:2 * GLA_K + GLA_V] = jnp.concatenate(dvs, axis=1).astype(BF16)
        du_ref[:, 2 * GLA_K + GLA_V:2 * GLA_K + 2 * GLA_V] = jnp.concatenate(dgs, axis=1).astype(BF16)
        du_ref[:, 2 * GLA_K + 2 * GLA_V:] = _dot_nt(dz_b, w2_ref[...]).astype(BF16)
        dw2_ref[...] += _dot_tn(lr.astype(BF16), dz_b)
        dgb_ref[...] += jnp.sum(dz, axis=0, keepdims=True)
        dng_ref[...] += dng

    u3 = u.reshape(n_ex, lp, D_IN_PAD)
    rev = lambda w, col: pl.BlockSpec((n_ex, CHUNK, w), lambda n: (0, nc - 1 - n, col))
    (du, d_w2, d_gb, d_ng), extra = _call(
        body, name="gla_bwd", grid=(nc,),
        in_specs=[rev(GLA_V, 1), rev(2 * GLA_K, 2), rev(GLA_V, 3), rev(GLA_V, 4), rev(128, 20),
                  pl.BlockSpec((n_ex, DV, GLA_K), lambda n: (0, nc - 1 - n, 0)),
                  _const_spec((128, GLA_K)), _const_spec((1, GLA_K)), _const_spec((1, DV))],
        out_specs=[rev(D_GLA_IN, 0), _acc_spec((128, GLA_K)), _acc_spec((1, GLA_K)), _acc_spec((1, DV))],
        out_shape=[jax.ShapeDtypeStruct((n_ex, lp, D_GLA_IN), BF16), jax.ShapeDtypeStruct((128, GLA_K), F32),
                   jax.ShapeDtypeStruct((1, GLA_K), F32), jax.ShapeDtypeStruct((1, DV), F32)],
        scratch_shapes=[pltpu.VMEM((n_ex, DV, GLA_K), F32)],
        plan=plan,
    )(dycat.reshape(n_ex, lp, D), u3, u3, u3, u3, states, w2, gb, ng)
    return (du.reshape(t, D_GLA_IN), d_w2, d_gb, d_ng), extra


HEAD_ROWS_ALL = N_HEADS * CHUNK


def _head_of(shape, axis, per_head):
    return lax.broadcasted_iota(jnp.int32, shape, axis) // per_head


def _expand(x, lanes_per_head):
    rows, lanes = HEAD_ROWS_ALL, x.shape[1]
    keep = _head_of((rows, lanes), 0, CHUNK) == _head_of((rows, lanes), 1, lanes_per_head)
    return jnp.where(keep, jnp.tile(x, (N_HEADS, 1)), 0.0)


def _expand_lanes(x):
    rows, w = x.shape
    keep = _head_of((rows, N_HEADS * w), 0, CHUNK) == _head_of((rows, N_HEADS * w), 1, w)
    return jnp.where(keep, jnp.tile(x, (1, N_HEADS)), 0.0)


def _expand_state(st):
    rows, lanes = N_HEADS * DV, st.shape[1]
    keep = _head_of((rows, lanes), 0, DV) == _head_of((rows, lanes), 1, DK)
    return jnp.where(keep, jnp.tile(st, (N_HEADS, 1)), 0.0)


def _fold(t, rows_per_head):
    lane_head = _head_of((rows_per_head, t.shape[1]), 1, DK)
    out = jnp.where(lane_head == 0, t[0:rows_per_head], 0.0)
    for h in range(1, N_HEADS):
        out = out + jnp.where(lane_head == h, t[h * rows_per_head:(h + 1) * rows_per_head], 0.0)
    return out


def _rows_by_head(x):
    return jnp.concatenate([x[:, h * DV:(h + 1) * DV] for h in range(N_HEADS)], axis=0)


def _lanes_by_head(x):
    return jnp.concatenate([x[h * CHUNK:(h + 1) * CHUNK] for h in range(N_HEADS)], axis=1)


def _running_sum(a, lower):
    hi = a.astype(BF16)
    rest = a - hi.astype(F32)
    mid = rest.astype(BF16)
    lo = (rest - mid.astype(F32)).astype(BF16)
    w = a.shape[1]
    parts = _dot(_tri(lower).astype(F32).astype(BF16), jnp.concatenate([hi, mid, lo], axis=1))
    return parts[:, :w] + parts[:, w:2 * w] + parts[:, 2 * w:]


def _stacked_causal():
    i = lax.broadcasted_iota(jnp.int32, (HEAD_ROWS_ALL, CHUNK), 0) % CHUNK
    j = lax.broadcasted_iota(jnp.int32, (HEAD_ROWS_ALL, CHUNK), 1)
    return i >= j


def _gla_chunk(q, k, v, lr, st, w2, gb, first_chunk):
    z, a, live = _gla_gates(lr, w2, gb, first_chunk)
    yield
    b = _running_sum(a, True)
    yield
    bl = b[CHUNK - 1:CHUNK, :]
    e_pos, e_neg, e_dec = jnp.exp(b), jnp.exp(-b), jnp.exp(bl - b)
    q_f, k_f, kd_f = q * (DK ** -0.5) * e_pos, k * e_neg, k * e_dec
    qx = _expand(q_f, DK).astype(BF16)
    k_in, k_dec, v_b = k_f.astype(BF16), kd_f.astype(BF16), v.astype(BF16)
    s = jnp.where(_stacked_causal(), _dot_nt(qx, k_in), 0.0).astype(BF16)
    o_inter = _dot_nt(qx, st.astype(BF16))
    yield
    p = _dot(s, v_b)
    yield
    o = jnp.concatenate([p[h * CHUNK:(h + 1) * CHUNK, h * DV:(h + 1) * DV] for h in range(N_HEADS)], axis=0) + o_inter
    return dict(z=z, live=live, bl=bl, e_pos=e_pos, e_neg=e_neg, e_dec=e_dec, q_f=q_f, k_f=k_f, kd_f=kd_f,
                qx=qx, k_in=k_in, k_dec=k_dec, v_b=v_b, s=s, o=o, decay=jnp.exp(bl))


def _gla_fwd(u, w2, gb, ng, n_ex, lp, plan=None):
    nc = lp // CHUNK
    t = n_ex * lp

    def body(qk_ref, v_ref, g_ref, lr_ref, w2_ref, gb_ref, ng_ref, y_ref, st_ref, state):
        n = pl.program_id(0)

        @pl.when(n == 0)
        def _():
            state[...] = jnp.zeros_like(state)

        def one_example(e):
            st = state[e]
            st_ref[e] = st
            qk = qk_ref[e]
            c = yield from _gla_chunk(qk[:, :GLA_K], qk[:, GLA_K:], v_ref[e], lr_ref[e], st, w2_ref[...],
                                      gb_ref[...], n == 0)
            o = c["o"]
            rstd = lax.rsqrt(jnp.mean(o * o, axis=-1, keepdims=True) + RMS_EPS)
            g = _rows_by_head(g_ref[e])
            y_ref[e] = _lanes_by_head(o * rstd * ng_ref[...] * (g * _sigmoid(g))).astype(BF16)
            state[e] = c["decay"] * st + _fold(_dot_tn(c["v_b"], c["k_dec"]), DV)

        _in_lockstep(one_example(e) for e in range(n_ex))

    u3 = u.reshape(n_ex, lp, D_IN_PAD)
    blk = lambda w, col: pl.BlockSpec((n_ex, CHUNK, w), lambda n: (0, n, col))
    (y, states), extra = _call(
        body, name="gla_fwd", grid=(nc,),
        in_specs=[blk(2 * GLA_K, 2), blk(GLA_V, 3), blk(GLA_V, 4), blk(128, 20),
                  _const_spec((128, GLA_K)), _const_spec((1, GLA_K)), _const_spec((1, DV))],
        out_specs=[blk(GLA_V, 0), pl.BlockSpec((n_ex, DV, GLA_K), lambda n: (0, n, 0))],
        out_shape=[jax.ShapeDtypeStruct((n_ex, lp, GLA_V), BF16),
                   jax.ShapeDtypeStruct((n_ex, nc * DV, GLA_K), F32)],
        scratch_shapes=[pltpu.VMEM((n_ex, DV, GLA_K), F32)],
        plan=plan,
    )(u3, u3, u3, u3, w2, gb, ng)
    return (y.reshape(t, GLA_V), states), extra


def _gla_bwd(dycat, u, states, w2, gb, ng, n_ex, lp, plan=None):
    nc = lp // CHUNK
    t = n_ex * lp

    def body(dy_ref, qk_ref, v_ref, g_ref, lr_ref, st_ref, w2_ref, gb_ref, ng_ref,
             du_ref, dw2_ref, dgb_ref, dng_ref, dstate):
        n = pl.program_id(0)
        chunk = nc - 1 - n

        @pl.when(n == 0)
        def _():
            dw2_ref[...] = jnp.zeros_like(dw2_ref)
            dgb_ref[...] = jnp.zeros_like(dgb_ref)
            dng_ref[...] = jnp.zeros_like(dng_ref)
            dstate[...] = jnp.zeros_like(dstate)

        def one_example(e):
            qk = qk_ref[e]
            lr = lr_ref[e]
            st = st_ref[e]
            dst = dstate[e]
            c = yield from _gla_chunk(qk[:, :GLA_K], qk[:, GLA_K:], v_ref[e], lr, st, w2_ref[...], gb_ref[...],
                                      chunk == 0)
            qx, k_in, k_dec, v_b, s, o = c["qx"], c["k_in"], c["k_dec"], c["v_b"], c["s"], c["o"]
            ngv = ng_ref[...]
            rstd = lax.rsqrt(jnp.mean(o * o, axis=-1, keepdims=True) + RMS_EPS)
            nrm = o * rstd
            g = _rows_by_head(g_ref[e])
            dy = _rows_by_head(dy_ref[e])
            sg = _sigmoid(g)
            dg = dy * nrm * ngv * (sg * (1.0 + g * (1.0 - sg)))
            dt = dy * (g * sg)
            dng_ref[...] += jnp.sum(dt * nrm, axis=0, keepdims=True)
            dn = dt * ngv
            do = rstd * (dn - nrm * jnp.mean(dn * nrm, axis=-1, keepdims=True))
            do_b = do.astype(BF16)
            dox = _expand_lanes(do).astype(BF16)
            dstx = _expand_state(dst).astype(BF16)
            yield
            da = jnp.where(_stacked_causal(), _dot_nt(dox, v_b), 0.0).astype(BF16)
            dv = _dot_tn(s, dox) + _dot_nt(k_dec, dstx)
            dk_dec = _dot(v_b, dstx)
            dstate[e] = _dot_tn(do_b, qx) + c["decay"] * dst
            yield
            dq_in = _fold(_dot(da, k_in) + _dot(do_b, st.astype(BF16)), CHUNK)
            dk_in = _dot_tn(da, qx)
            yield
            dbl = (jnp.sum(dk_dec * c["kd_f"], axis=0, keepdims=True)
                   + c["decay"] * jnp.sum(dst * st, axis=0, keepdims=True))
            dq = dq_in * (DK ** -0.5) * c["e_pos"]
            dk = dk_in * c["e_neg"] + dk_dec * c["e_dec"]
            db = dq_in * c["q_f"] - dk_in * c["k_f"] - dk_dec * c["kd_f"]
            row = lax.broadcasted_iota(jnp.int32, (CHUNK, 1), 0)
            da_log = _running_sum(db + jnp.where(row == CHUNK - 1, dbl, 0.0), False)
            yield
            dz = jnp.where(c["live"], da_log * (1.0 - _sigmoid(c["z"])) * (1.0 / GATE_TAU), 0.0)
            dz_b = dz.astype(BF16)
            out = du_ref.at[e]
            out[:, 0:GLA_K] = dq.astype(BF16)
            out[:, GLA_K:2 * GLA_K] = dk.astype(BF16)
            out[:, 2 * GLA_K:2 * GLA_K + GLA_V] = dv.astype(BF16)
            out[:, 2 * GLA_K + GLA_V:2 * GLA_K + 2 * GLA_V] = _lanes_by_head(dg).astype(BF16)
            out[:, 2 * GLA_K + 2 * GLA_V:] = _dot_nt(dz_b, w2_ref[...]).astype(BF16)
            dw2_ref[...] += _dot_tn(lr.astype(BF16), dz_b)
            dgb_ref[...] += jnp.sum(dz, axis=0, keepdims=True)

        _in_lockstep(one_example(e) for e in range(n_ex))

    u3 = u.reshape(n_ex, lp, D_IN_PAD)
    rev = lambda w, col: pl.BlockSpec((n_ex, CHUNK, w), lambda n: (0, nc - 1 - n, col))
    (du, d_w2, d_gb, d_ng), extra = _call(
        body, name="gla_bwd", grid=(nc,),
        in_specs=[rev(GLA_V, 1), rev(2 * GLA_K, 2), rev(GLA_V, 3), rev(GLA_V, 4), rev(128, 20),
                  pl.BlockSpec((n_ex, DV, GLA_K), lambda n: (0, nc - 1 - n, 0)),
                  _const_spec((128, GLA_K)), _const_spec((1, GLA_K)), _const_spec((1, DV))],
        out_specs=[rev(D_GLA_IN, 0), _acc_spec((128, GLA_K)), _acc_spec((1, GLA_K)), _acc_spec((1, DV))],
        out_shape=[jax.ShapeDtypeStruct((n_ex, lp, D_GLA_IN), BF16), jax.ShapeDtypeStruct((128, GLA_K), F32),
                   jax.ShapeDtypeStruct((1, GLA_K), F32), jax.ShapeDtypeStruct((1, DV), F32)],
        scratch_shapes=[pltpu.VMEM((n_ex, DV, GLA_K), F32)],
        plan=plan,
    )(dycat.reshape(n_ex, lp, D), u3, u3, u3, u3, states, w2, gb, ng)
    return (du.reshape(t, D_GLA_IN), d_w2, d_gb, d_ng), extra


def _in_proj_bwd(du_conv, du_gla, w_in_t_conv, w_in_t_gla, h0, dh1, g_mix, plan=None):
    t = h0.shape[0]
    r = _row_tile(t, 384)

    def body(dc_ref, dg_ref, wc_ref, wg_ref, h_ref, dh1_ref, g_ref, dh0_ref, dgm_ref):
        @pl.when(pl.program_id(0) == 0)
        def _():
            dgm_ref[...] = jnp.zeros_like(dgm_ref)

        dhn = _dot(dc_ref[...], wc_ref[...]) + _dot(dg_ref[...], wg_ref[...])
        h = h_ref[...]
        rstd = lax.rsqrt(jnp.mean(h * h, axis=-1, keepdims=True) + RMS_EPS)
        nrm = h * rstd
        dgm_ref[...] += jnp.sum(dhn * nrm, axis=0, keepdims=True)
        dn = dhn * g_ref[...]
        dh0_ref[...] = dh1_ref[...] + rstd * (dn - nrm * jnp.mean(dn * nrm, axis=-1, keepdims=True))

    rows = lambda w: pl.BlockSpec((r, w), lambda i: (i, 0))
    return _call(
        body, name="in_proj_bwd", grid=(t // r,),
        in_specs=[rows(2 * C_CONV), rows(D_GLA_IN), _const_spec((2 * C_CONV, D)), _const_spec((D_GLA_IN, D)),
                  rows(D), rows(D), _const_spec((1, D))],
        out_specs=[rows(D), _acc_spec((1, D))],
        out_shape=[jax.ShapeDtypeStruct((t, D), F32), jax.ShapeDtypeStruct((1, D), F32)],
        plan=plan,
    )(du_conv, du_gla, w_in_t_conv, w_in_t_gla, h0, dh1, g_mix)


def _wgrad_hosting(x, dy, name, plan):
    t, m = x.shape
    n = dy.shape[1]
    tk = t // 3 if t % (3 * 128) == 0 else _row_tile(t, 384)
    tm = m if m <= D_GLA_IN else m // 2

    def body(x_ref, dy_ref, o_ref):
        @pl.when(pl.program_id(2) == 0)
        def _():
            o_ref[...] = jnp.zeros_like(o_ref)

        o_ref[...] += _dot_tn(x_ref[...].astype(BF16), dy_ref[...].astype(BF16))

    (out,), extra = _call(
        body, name=name, grid=(m // tm, 1, t // tk),
        in_specs=[pl.BlockSpec((tk, tm), lambda i, j, k: (k, i)), pl.BlockSpec((tk, n), lambda i, j, k: (k, j))],
        out_specs=[pl.BlockSpec((tm, n), lambda i, j, k: (i, j))],
        out_shape=[jax.ShapeDtypeStruct((m, n), F32)],
        plan=plan,
    )(x, dy)
    return out, extra


def _wgrad(x, dy, name):
    t, m = x.shape
    n = dy.shape[1]
    tk = t // 3 if t % (3 * 128) == 0 else _row_tile(t, 384)
    tm = m if m <= D_GLA_IN else m // 2
    tn = n

    def body(x_ref, dy_ref, o_ref):
        @pl.when(pl.program_id(2) == 0)
        def _():
            o_ref[...] = jnp.zeros_like(o_ref)

        o_ref[...] += _dot_tn(x_ref[...].astype(BF16), dy_ref[...].astype(BF16))

    return pl.pallas_call(
        body, name=name, grid=(m // tm, n // tn, t // tk),
        in_specs=[pl.BlockSpec((tk, tm), lambda i, j, k: (k, i)), pl.BlockSpec((tk, tn), lambda i, j, k: (k, j))],
        out_specs=pl.BlockSpec((tm, tn), lambda i, j, k: (i, j)),
        out_shape=jax.ShapeDtypeStruct((m, n), F32),
        compiler_params=_params(3),
    )(x, dy)


def _mesh_pos():
    return lax.axis_index("x"), lax.axis_index("y"), lax.axis_index("c")


def _other_chips(x, y):
    return [(1 - x, y), (x, 1 - y), (1 - x, 1 - y)]


HBM_SPEC = pl.BlockSpec(memory_space=pltpu.HBM)


def _gather_shards(shards):
    n = len(shards)

    def body(*refs):
        ins, outs = refs[:n], refs[n:2 * n]
        send_sems, recv_sems, local_sems = refs[2 * n:]
        x, y, c = _mesh_pos()
        mine = 2 * x + y
        chips = _other_chips(x, y)
        local = [pltpu.make_async_copy(ins[a], outs[a].at[mine], local_sems.at[a]) for a in range(n)]
        for cp in local:
            cp.start()

        def remote(a, k, block):
            px, py = chips[k]
            return pltpu.make_async_remote_copy(
                src_ref=ins[a], dst_ref=outs[a].at[block], send_sem=send_sems.at[3 * a + k],
                recv_sem=recv_sems.at[3 * a + k], device_id=(px, py, c), device_id_type=MESH)

        sends = [remote(a, k, mine) for a in range(n) for k in range(3)]
        for cp in sends:
            cp.start()
        for a in range(n):
            for k, (px, py) in enumerate(chips):
                remote(a, k, 2 * px + py).wait_recv()
        for cp in sends:
            cp.wait_send()
        for cp in local:
            cp.wait()

    return pl.pallas_call(
        body, name="gather_shards",
        in_specs=[HBM_SPEC] * n, out_specs=[HBM_SPEC] * n,
        out_shape=[jax.ShapeDtypeStruct((N_CHIPS,) + s.shape, s.dtype) for s in shards],
        scratch_shapes=[pltpu.SemaphoreType.DMA((3 * n,)), pltpu.SemaphoreType.DMA((3 * n,)),
                        pltpu.SemaphoreType.DMA((n,))],
        compiler_params=pltpu.CompilerParams(has_side_effects=True),
    )(*shards)


def _send_half_to_sibling(g2):
    def body(g_ref, recv_ref, send_sem, recv_sem):
        x, y, c = _mesh_pos()
        cp = pltpu.make_async_remote_copy(
            src_ref=g_ref.at[1 - c], dst_ref=recv_ref, send_sem=send_sem, recv_sem=recv_sem,
            device_id=(x, y, 1 - c), device_id_type=MESH)
        cp.start()
        cp.wait()

    return pl.pallas_call(
        body, name="rs_to_sibling", in_specs=[HBM_SPEC], out_specs=HBM_SPEC,
        out_shape=jax.ShapeDtypeStruct(g2.shape[1:], g2.dtype),
        scratch_shapes=[pltpu.SemaphoreType.DMA(()), pltpu.SemaphoreType.DMA(())],
        compiler_params=pltpu.CompilerParams(has_side_effects=True),
    )(g2)


def _add_own_half(g2, recv, c):
    rows = N_CHIPS * HALF_ROWS
    tr = 512
    g2f = g2.reshape(2, rows, D)
    recvf = recv.reshape(rows, D)

    def body(c_ref, a_ref, b_ref, o_ref):
        o_ref[...] = a_ref[0] + b_ref[...]

    out = pl.pallas_call(
        body, name="rs_add_halves",
        grid_spec=pltpu.PrefetchScalarGridSpec(
            num_scalar_prefetch=1, grid=(rows // tr,),
            in_specs=[pl.BlockSpec((1, tr, D), lambda i, s: (s[0], i, 0)), pl.BlockSpec((tr, D), lambda i, s: (i, 0))],
            out_specs=pl.BlockSpec((tr, D), lambda i, s: (i, 0))),
        out_shape=jax.ShapeDtypeStruct((rows, D), F32),
        compiler_params=_params(1),
    )(jnp.reshape(c, (1,)).astype(jnp.int32), g2f, recvf)
    return out.reshape(N_CHIPS, HALF_ROWS, D)


def _exchange_chip_sums(p):
    def body(p_ref, out_ref, send_sems, recv_sems, local_sem):
        x, y, c = _mesh_pos()
        mine = 2 * x + y
        chips = _other_chips(x, y)
        local = pltpu.make_async_copy(p_ref.at[mine], out_ref.at[mine], local_sem)
        local.start()

        def remote(k, src_block, dst_block):
            px, py = chips[k]
            return pltpu.make_async_remote_copy(
                src_ref=p_ref.at[src_block], dst_ref=out_ref.at[dst_block], send_sem=send_sems.at[k],
                recv_sem=recv_sems.at[k], device_id=(px, py, c), device_id_type=MESH)

        sends = [remote(k, 2 * px + py, mine) for k, (px, py) in enumerate(chips)]
        for cp in sends:
            cp.start()
        for k, (px, py) in enumerate(chips):
            remote(k, mine, 2 * px + py).wait_recv()
        for cp in sends:
            cp.wait_send()
        local.wait()

    return pl.pallas_call(
        body, name="rs_chip_exchange", in_specs=[HBM_SPEC], out_specs=HBM_SPEC,
        out_shape=jax.ShapeDtypeStruct(p.shape, p.dtype),
        scratch_shapes=[pltpu.SemaphoreType.DMA((3,)), pltpu.SemaphoreType.DMA((3,)), pltpu.SemaphoreType.DMA(())],
        compiler_params=pltpu.CompilerParams(has_side_effects=True),
    )(p)


def _sum_chips(parts):
    tr = 512

    def body(p_ref, o_ref):
        o_ref[...] = ((p_ref[0] + p_ref[1]) + p_ref[2]) + p_ref[3]

    return pl.pallas_call(
        body, name="rs_sum_chips", grid=(HALF_ROWS // tr,),
        in_specs=[pl.BlockSpec((N_CHIPS, tr, D), lambda i: (0, i, 0))],
        out_specs=pl.BlockSpec((tr, D), lambda i: (i, 0)),
        out_shape=jax.ShapeDtypeStruct((HALF_ROWS, D), F32),
        compiler_params=_params(1),
    )(parts)


def _share_with_sibling(half):
    def body(h_ref, out_ref, send_sem, recv_sem, local_sem):
        x, y, c = _mesh_pos()
        local = pltpu.make_async_copy(h_ref, out_ref.at[c], local_sem)
        local.start()
        cp = pltpu.make_async_remote_copy(
            src_ref=h_ref, dst_ref=out_ref.at[c], send_sem=send_sem, recv_sem=recv_sem,
            device_id=(x, y, 1 - c), device_id_type=MESH)
        cp.start()
        pltpu.make_async_remote_copy(
            src_ref=h_ref, dst_ref=out_ref.at[1 - c], send_sem=send_sem, recv_sem=recv_sem,
            device_id=(x, y, 1 - c), device_id_type=MESH).wait_recv()
        cp.wait_send()
        local.wait()

    return pl.pallas_call(
        body, name="rs_share_sibling", in_specs=[HBM_SPEC], out_specs=HBM_SPEC,
        out_shape=jax.ShapeDtypeStruct((2,) + half.shape, half.dtype),
        scratch_shapes=[pltpu.SemaphoreType.DMA(()), pltpu.SemaphoreType.DMA(()), pltpu.SemaphoreType.DMA(())],
        compiler_params=pltpu.CompilerParams(has_side_effects=True),
    )(half)


def _adam_update(g, w, m, v):
    m2 = ADAM_B1 * m + (1.0 - ADAM_B1) * g
    v2 = ADAM_B2 * v + (1.0 - ADAM_B2) * (g * g)
    m_hat = m2 / (1.0 - ADAM_B1 ** ADAM_STEP)
    v_hat = v2 / (1.0 - ADAM_B2 ** ADAM_STEP)
    delta = -ADAM_LR * (m_hat / (jnp.sqrt(v_hat) + ADAM_EPS) + ADAM_WD * w)
    return delta, m2, v2


def _adamw_slab(g, w, m, v):
    rows = g.shape[0]
    tr = 256

    def body(g_ref, w_ref, m_ref, v_ref, d_ref, m2_ref, v2_ref):
        d_ref[...], m2_ref[...], v2_ref[...] = _adam_update(g_ref[...], w_ref[...], m_ref[...], v_ref[...])

    spec = pl.BlockSpec((tr, D), lambda i: (i, 0))
    return pl.pallas_call(
        body, name="adamw_slab", grid=(rows // tr,), in_specs=[spec] * 4, out_specs=[spec] * 3,
        out_shape=[jax.ShapeDtypeStruct((rows, D), F32)] * 3,
        compiler_params=_params(1),
    )(g, w, m, v)


def _allreduce_small_adamw(part, w, m, v):
    def body(p_ref, w_ref, m_ref, v_ref, g_ref, d_ref, m2_ref, v2_ref, slots, send_sems, recv_sems):
        x, y, c = _mesh_pos()
        mine = 4 * x + 2 * y + c
        peers = [(px, py, pc) for px in (x, 1 - x) for py in (y, 1 - y) for pc in (c, 1 - c)][1:]

        def remote(k, slot):
            return pltpu.make_async_remote_copy(
                src_ref=p_ref, dst_ref=slots.at[slot], send_sem=send_sems.at[k], recv_sem=recv_sems.at[k],
                device_id=peers[k], device_id_type=MESH)

        sends = [remote(k, mine) for k in range(7)]
        for cp in sends:
            cp.start()
        slots[mine] = p_ref[...]
        for k, (px, py, pc) in enumerate(peers):
            remote(k, 4 * px + 2 * py + pc).wait_recv()
        for cp in sends:
            cp.wait_send()
        g = slots[0]
        for d in range(1, 8):
            g = g + slots[d]
        g_ref[...] = g
        d_ref[...], m2_ref[...], v2_ref[...] = _adam_update(g, w_ref[...], m_ref[...], v_ref[...])

    vm = pl.BlockSpec(memory_space=pltpu.VMEM)
    shape = jax.ShapeDtypeStruct(part.shape, F32)
    return pl.pallas_call(
        body, name="small_allreduce_adamw", in_specs=[vm] * 4, out_specs=[vm] * 4, out_shape=[shape] * 4,
        scratch_shapes=[pltpu.VMEM((8,) + part.shape, F32), pltpu.SemaphoreType.DMA((7,)),
                        pltpu.SemaphoreType.DMA((7,))],
        compiler_params=pltpu.CompilerParams(has_side_effects=True),
    )(part, w, m, v)


def _half(ref, c, axis):
    n = ref.shape[axis] // 2
    return ref.at[(slice(None),) * axis + (pl.ds(c * n, n),)]


def _remote(src, dst, send_sem, recv_sem, device):
    return pltpu.make_async_remote_copy(src_ref=src, dst_ref=dst, send_sem=send_sem, recv_sem=recv_sem,
                                        device_id=device, device_id_type=MESH)


def _gather_weights(split, axes, whole):
    ns, n = len(split), len(split) + len(whole)

    def body(*refs):
        ins, outs = refs[:n], refs[n:2 * n]
        ici_send, ici_recv, d2d_send, d2d_recv, local_sems = refs[2 * n:]
        x, y, c = _mesh_pos()
        mine = 2 * x + y
        chips = _other_chips(x, y)
        local = [pltpu.make_async_copy(ins[a], outs[a].at[mine], local_sems.at[a]) for a in range(n)]
        for cp in local:
            cp.start()

        def ici(a, k, block):
            px, py = chips[k]
            src, dst = ins[a], outs[a].at[block]
            if a < ns:
                src, dst = _half(src, c, axes[a]), _half(dst, c, axes[a])
            return _remote(src, dst, ici_send.at[3 * a + k], ici_recv.at[3 * a + k], (px, py, c))

        def d2d(a, k, block, half):
            part = _half(outs[a].at[block], half, axes[a])
            return _remote(part, part, d2d_send.at[3 * a + k], d2d_recv.at[3 * a + k], (x, y, 1 - c))

        sends = [ici(a, k, mine) for a in range(n) for k in range(3)]
        for cp in sends:
            cp.start()
        for a in range(n):
            for k, (px, py) in enumerate(chips):
                ici(a, k, 2 * px + py).wait_recv()
                if a < ns:
                    sends.append(d2d(a, k, 2 * px + py, c))
                    sends[-1].start()
        for a in range(ns):
            for k, (px, py) in enumerate(chips):
                d2d(a, k, 2 * px + py, 1 - c).wait_recv()
        for cp in sends:
            cp.wait_send()
        for cp in local:
            cp.wait()

    arrays = list(split) + list(whole)
    return pl.pallas_call(
        body, name="gather_weights", in_specs=[HBM_SPEC] * n, out_specs=[HBM_SPEC] * n,
        out_shape=[jax.ShapeDtypeStruct((N_CHIPS,) + s.shape, s.dtype) for s in arrays],
        scratch_shapes=[pltpu.SemaphoreType.DMA((3 * n,)), pltpu.SemaphoreType.DMA((3 * n,)),
                        pltpu.SemaphoreType.DMA((3 * ns,)), pltpu.SemaphoreType.DMA((3 * ns,)),
                        pltpu.SemaphoreType.DMA((n,))],
        compiler_params=pltpu.CompilerParams(has_side_effects=True),
    )(*arrays)


def _rs_to_sibling(gs):
    n = len(gs)

    def body(*refs):
        ins, outs, send_sems, recv_sems = refs[:n], refs[n:2 * n], refs[2 * n], refs[2 * n + 1]
        x, y, c = _mesh_pos()
        copies = [_remote(_half(ins[a], 1 - c, 2), outs[a], send_sems.at[a], recv_sems.at[a], (x, y, 1 - c))
                  for a in range(n)]
        for cp in copies:
            cp.start()
        for cp in copies:
            cp.wait()

    return pl.pallas_call(
        body, name="rs_to_sibling", in_specs=[HBM_SPEC] * n, out_specs=[HBM_SPEC] * n,
        out_shape=[jax.ShapeDtypeStruct(g.shape[:2] + (g.shape[2] // 2,), g.dtype) for g in gs],
        scratch_shapes=[pltpu.SemaphoreType.DMA((n,)), pltpu.SemaphoreType.DMA((n,))],
        compiler_params=pltpu.CompilerParams(has_side_effects=True),
    )(*gs)


def _rs_add_halves(g, recv, c, name):
    _, rows, w = g.shape
    h = w // 2
    tr = rows // 2 if rows % 16 == 0 and rows > 64 else rows

    def body(c_ref, a_ref, b_ref, o_ref):
        o_ref[...] = (a_ref[...] + b_ref[...]).astype(BF16)

    return pl.pallas_call(
        body, name=name,
        grid_spec=pltpu.PrefetchScalarGridSpec(
            num_scalar_prefetch=1, grid=(N_CHIPS, rows // tr),
            in_specs=[pl.BlockSpec((1, tr, h), lambda j, i, s: (j, i, s[0])),
                      pl.BlockSpec((1, tr, h), lambda j, i, s: (j, i, 0))],
            out_specs=pl.BlockSpec((1, tr, h), lambda j, i, s: (j, i, 0))),
        out_shape=jax.ShapeDtypeStruct((N_CHIPS, rows, h), BF16),
        compiler_params=_params(2),
    )(jnp.reshape(c, (1,)).astype(jnp.int32), g, recv)


def _rs_chip_exchange(ps):
    n = len(ps)

    def body(*refs):
        ins, outs = refs[:n], refs[n:2 * n]
        send_sems, recv_sems, local_sems = refs[2 * n:]
        x, y, c = _mesh_pos()
        mine = 2 * x + y
        chips = _other_chips(x, y)
        local = [pltpu.make_async_copy(ins[a].at[mine], outs[a].at[mine], local_sems.at[a]) for a in range(n)]
        for cp in local:
            cp.start()

        def ici(a, k, src_block, dst_block):
            px, py = chips[k]
            return _remote(ins[a].at[src_block], outs[a].at[dst_block], send_sems.at[3 * a + k],
                           recv_sems.at[3 * a + k], (px, py, c))

        sends = [ici(a, k, 2 * px + py, mine) for a in range(n) for k, (px, py) in enumerate(chips)]
        for cp in sends:
            cp.start()
        for a in range(n):
            for k, (px, py) in enumerate(chips):
                ici(a, k, mine, 2 * px + py).wait_recv()
        for cp in sends:
            cp.wait_send()
        for cp in local:
            cp.wait()

    return pl.pallas_call(
        body, name="rs_chip_exchange", in_specs=[HBM_SPEC] * n, out_specs=[HBM_SPEC] * n,
        out_shape=[jax.ShapeDtypeStruct(p.shape, p.dtype) for p in ps],
        scratch_shapes=[pltpu.SemaphoreType.DMA((3 * n,)), pltpu.SemaphoreType.DMA((3 * n,)),
                        pltpu.SemaphoreType.DMA((n,))],
        compiler_params=pltpu.CompilerParams(has_side_effects=True),
    )(*ps)


def _rs_sum_chips(parts, name):
    _, rows, h = parts.shape
    tr = rows // 2 if rows % 16 == 0 and rows > 64 else rows

    def body(p_ref, o_ref):
        p = p_ref[...].astype(F32)
        o_ref[...] = ((p[0] + p[1]) + p[2]) + p[3]

    return pl.pallas_call(
        body, name=name, grid=(rows // tr,),
        in_specs=[pl.BlockSpec((N_CHIPS, tr, h), lambda i: (0, i, 0))],
        out_specs=pl.BlockSpec((tr, h), lambda i: (i, 0)),
        out_shape=jax.ShapeDtypeStruct((rows, h), F32),
        compiler_params=_params(1),
    )(parts)


def _rs_share(halves):
    n = len(halves)

    def body(*refs):
        ins, outs = refs[:n], refs[n:2 * n]
        send_sems, recv_sems, local_sems = refs[2 * n:]
        x, y, c = _mesh_pos()
        local = [pltpu.make_async_copy(ins[a], _half(outs[a], c, 1), local_sems.at[a]) for a in range(n)]
        for cp in local:
            cp.start()
        sends = [_remote(ins[a], _half(outs[a], c, 1), send_sems.at[a], recv_sems.at[a], (x, y, 1 - c))
                 for a in range(n)]
        for cp in sends:
            cp.start()
        for a in range(n):
            _remote(ins[a], _half(outs[a], 1 - c, 1), send_sems.at[a], recv_sems.at[a], (x, y, 1 - c)).wait_recv()
        for cp in sends:
            cp.wait_send()
        for cp in local:
            cp.wait()

    return pl.pallas_call(
        body, name="rs_share", in_specs=[HBM_SPEC] * n, out_specs=[HBM_SPEC] * n,
        out_shape=[jax.ShapeDtypeStruct((p.shape[0], 2 * p.shape[1]), p.dtype) for p in halves],
        scratch_shapes=[pltpu.SemaphoreType.DMA((n,)), pltpu.SemaphoreType.DMA((n,)),
                        pltpu.SemaphoreType.DMA((n,))],
        compiler_params=pltpu.CompilerParams(has_side_effects=True),
    )(*halves)


def _adamw(g, w, m, v, name):
    rows, cols = g.shape
    tr = 256 if rows % 256 == 0 else (rows // 2 if rows % 16 == 0 and rows > 64 else rows)

    def body(g_ref, w_ref, m_ref, v_ref, d_ref, m2_ref, v2_ref):
        d_ref[...], m2_ref[...], v2_ref[...] = _adam_update(g_ref[...], w_ref[...], m_ref[...], v_ref[...])

    spec = pl.BlockSpec((tr, cols), lambda i: (i, 0))
    return pl.pallas_call(
        body, name=name, grid=(rows // tr,), in_specs=[spec] * 4, out_specs=[spec] * 3,
        out_shape=[jax.ShapeDtypeStruct((rows, cols), F32)] * 3,
        compiler_params=_params(1),
    )(g, w, m, v)


def _rows_of(a):
    flat = a.reshape(-1)
    pad = (-flat.shape[0]) % D
    if pad:
        flat = jnp.concatenate([flat, jnp.zeros((pad,), flat.dtype)])
    return flat.reshape(-1, D)


SLAB_PARTS = (("w_in", (D, D_IN // N_CHIPS)), ("w_out", (D // N_CHIPS, D)), ("w_ffn_gate", (D, D_FF // N_CHIPS)),
              ("w_ffn_up", (D, D_FF // N_CHIPS)), ("w_ffn_down", (D_FF // N_CHIPS, D)),
              ("meta_tokens", (N_META, D // N_CHIPS)), ("conv_w", (CONV_W, C_CONV // N_CHIPS)),
              ("gla_w_gate2", (RANK, GLA_K // N_CHIPS)))


def _pack_slab(parts):
    rows = [_rows_of(parts[name].reshape(shape)) for name, shape in SLAB_PARTS]
    used = sum(r.shape[0] for r in rows)
    rows.append(jnp.zeros((SLAB_ROWS - used, D), F32))
    return jnp.concatenate(rows, axis=0)


def _unpack_slab(slab, lead):
    out, r0 = {}, 0
    for name, shape in SLAB_PARTS:
        size = shape[0] * shape[1]
        nrows = -(-size // D)
        out[name] = slab[r0:r0 + nrows].reshape(-1)[:size].reshape(lead[name] + shape)
        r0 += nrows
    return out


SMALL_PARTS = (("norm_mix_g", 0, 0, D), ("norm_ffn_g", 1, 0, D), ("norm_final_g", 2, 0, D),
               ("conv_b", 3, 0, C_CONV), ("conv_ln_g", 3, C_CONV, C_CONV), ("conv_ln_b", 4, 0, C_CONV),
               ("gla_gate_b", 4, C_CONV, GLA_K), ("gla_norm_g", 4, C_CONV + GLA_K, DV))


def _pack_small(parts):
    slab = jnp.zeros((SMALL_ROWS, D), F32)
    for name, row, col, size in SMALL_PARTS:
        slab = lax.dynamic_update_slice(slab, parts[name].reshape(1, size).astype(F32), (row, col))
    return slab


def _unpack_small(slab, shapes):
    return {name: slab[row, col:col + size].reshape(shapes[name]) for name, row, col, size in SMALL_PARTS}


def _column_block(full, j, width):
    return lax.dynamic_slice_in_dim(full, j * width, width, axis=1)


def _local_step(x, target, w):
    n_ex, seq, _ = x.shape
    lp = HEAD_ROWS + seq
    t = n_ex * lp
    meta = jnp.broadcast_to(w["meta_tokens"][None], (n_ex, N_META, D))
    h0 = jnp.concatenate([jnp.zeros((n_ex, PAD_ROWS, D), F32), meta, x], axis=1).reshape(t, D)
    tgt = jnp.concatenate([jnp.zeros((n_ex, HEAD_ROWS, D), F32), target], axis=1).reshape(t, D)
    row_mask = jnp.concatenate([jnp.zeros((n_ex, HEAD_ROWS, 1), F32), jnp.ones((n_ex, seq, 1), F32)],
                               axis=1).reshape(t, 1)

    u, hn = _in_proj(h0, w["norm_mix_g"], w["w_in"])
    yc, y_conv = _conv_fwd(u, w["conv_w"], w["conv_b"], w["conv_ln_g"], w["conv_ln_b"], n_ex, lp)
    y_gla, states = _gla_fwd(u, w["gla_w_gate2"], w["gla_gate_b"], w["gla_norm_g"], n_ex, lp)
    h1, hn2, gate, up, act = _mix_out_ffn_up(h0, y_conv, y_gla, w["w_out"], w["norm_ffn_g"],
                                             w["w_ffn_gate_t"], w["w_ffn_up_t"])
    dh2, loss, d_final_g = _ffn_down_loss(act, w["w_ffn_down"], h1, tgt, w["norm_final_g"], row_mask)

    dgate, dup, dh1, dycat, d_ffn_g = _ffn_bwd(dh2, gate, up, h1, w["w_ffn_down"], w["w_ffn_gate_t"],
                                                w["w_ffn_up_t"], w["w_out"], w["norm_ffn_g"])
    du_conv, d_conv_w, d_conv_b, d_ln_g, d_ln_b = _conv_bwd(dycat, yc, u, w["conv_w"], w["conv_ln_g"],
                                                            w["conv_ln_b"], n_ex, lp)
    du_gla, d_w2, d_gate_b, d_norm_g = _gla_bwd(dycat, u, states, w["gla_w_gate2"], w["gla_gate_b"],
                                                w["gla_norm_g"], n_ex, lp)
    dh0, d_mix_g = _in_proj_bwd(du_conv, du_gla, w["w_in"][:, :2 * C_CONV], w["w_in"][:, 2 * C_CONV:],
                                h0, dh1, w["norm_mix_g"])

    d_w_in_t = jnp.concatenate([_wgrad(du_conv, hn, "wgrad_in_conv"), _wgrad(du_gla, hn, "wgrad_in_gla")],
                               axis=0)[:D_IN]
    d_w_out = jnp.concatenate([_wgrad(y_conv, dh1, "wgrad_out_conv"), _wgrad(y_gla, dh1, "wgrad_out_gla")], axis=0)
    dh0 = dh0.reshape(n_ex, lp, D)
    grads = {
        "w_in_t": d_w_in_t, "w_out": d_w_out,
        "w_ffn_gate_t": _wgrad(dgate, hn2, "wgrad_gate"), "w_ffn_up_t": _wgrad(dup, hn2, "wgrad_up"),
        "w_ffn_down": _wgrad(act, dh2, "wgrad_down"),
        "meta_tokens": jnp.sum(dh0[:, PAD_ROWS:HEAD_ROWS], axis=0),
        "conv_w": d_conv_w, "gla_w_gate2": d_w2[:RANK],
        "norm_mix_g": d_mix_g, "norm_ffn_g": d_ffn_g, "norm_final_g": d_final_g,
        "conv_b": d_conv_b, "conv_ln_g": d_ln_g, "conv_ln_b": d_ln_b,
        "gla_gate_b": d_gate_b, "gla_norm_g": d_norm_g,
    }
    return loss[0, 0], dh0[:, HEAD_ROWS:], grads


WEIGHT_NAMES = ("meta_tokens", "norm_mix_g", "w_in", "conv_w", "conv_b", "conv_ln_g", "conv_ln_b", "gla_w_gate2",
                "gla_gate_b", "gla_norm_g", "w_out", "norm_ffn_g", "w_ffn_gate", "w_ffn_up", "w_ffn_down",
                "norm_final_g")
MATMUL_WEIGHTS = ("w_in", "w_out", "w_ffn_gate", "w_ffn_up", "w_ffn_down")
ROW_SHARDED = ("w_out", "w_ffn_down")


def _full_weights(ws):
    sh = lambda name: ws[name].reshape(ws[name].shape[-2:])
    split = [sh("w_in").astype(BF16), sh("w_out").astype(BF16), sh("w_ffn_gate").T.astype(BF16),
             sh("w_ffn_up").T.astype(BF16), sh("w_ffn_down").astype(BF16)]
    whole = [sh("meta_tokens"), sh("conv_w"), sh("gla_w_gate2")]
    w_in, w_out, gate_t, up_t, down, meta, conv_w, w2 = _gather_weights(split, [0, 0, 0, 0, 0], whole)
    cols = lambda a: jnp.concatenate([a[j] for j in range(N_CHIPS)], axis=1)
    full = {name: ws[name].reshape(1, -1) for name, _, _, _ in SMALL_PARTS}
    full["w_in"] = jnp.concatenate([cols(w_in), jnp.zeros((D, D_IN_PAD - D_IN), BF16)], axis=1)
    full["w_out"] = w_out.reshape(D, D)
    full["w_ffn_gate_t"] = gate_t.reshape(D_FF, D)
    full["w_ffn_up_t"] = up_t.reshape(D_FF, D)
    full["w_ffn_down"] = down.reshape(D_FF, D)
    full["meta_tokens"] = cols(meta)
    full["conv_w"] = jnp.concatenate([cols(conv_w), jnp.zeros((32 - CONV_W, C_CONV), F32)], axis=0)
    full["gla_w_gate2"] = jnp.concatenate([cols(w2), jnp.zeros((128 - RANK, GLA_K), F32)], axis=0).astype(BF16)
    return full


SMALL_RS_ROWS = 48


def _pack_small_sharded(grads):
    by_chip = lambda g, w: jnp.transpose(g.reshape(g.shape[0], N_CHIPS, w), (1, 0, 2))
    meta = by_chip(grads["meta_tokens"], D // N_CHIPS)
    conv = by_chip(grads["conv_w"], C_CONV // N_CHIPS).reshape(N_CHIPS, 16, 256)
    w2 = by_chip(grads["gla_w_gate2"], GLA_K // N_CHIPS).reshape(N_CHIPS, 4, 256)
    pad = jnp.zeros((N_CHIPS, SMALL_RS_ROWS - 36, 256), F32)
    return jnp.concatenate([meta, conv, w2, pad], axis=1)


def _unpack_small_sharded(g):
    return {"meta_tokens": g[0:16], "conv_w": g[16:32].reshape(32, C_CONV // N_CHIPS)[:CONV_W],
            "gla_w_gate2": g[32:36].reshape(RANK, GLA_K // N_CHIPS)}


def _kernel_without_overlap(x, meta_tokens, norm_mix_g, w_in, conv_w, conv_b, conv_ln_g, conv_ln_b, gla_w_gate2, gla_gate_b, gla_norm_g, w_out, norm_ffn_g, w_ffn_gate, w_ffn_up, w_ffn_down, norm_final_g, loss_target, m_meta_tokens, m_norm_mix_g, m_w_in, m_conv_w, m_conv_b, m_conv_ln_g, m_conv_ln_b, m_gla_w_gate2, m_gla_gate_b, m_gla_norm_g, m_w_out, m_norm_ffn_g, m_w_ffn_gate, m_w_ffn_up, m_w_ffn_down, m_norm_final_g, v_meta_tokens, v_norm_mix_g, v_w_in, v_conv_w, v_conv_b, v_conv_ln_g, v_conv_ln_b, v_gla_w_gate2, v_gla_gate_b, v_gla_norm_g, v_w_out, v_norm_ffn_g, v_w_ffn_gate, v_w_ffn_up, v_w_ffn_down, v_norm_final_g):
    ws = dict(zip(WEIGHT_NAMES, (meta_tokens, norm_mix_g, w_in, conv_w, conv_b, conv_ln_g, conv_ln_b, gla_w_gate2,
                                 gla_gate_b, gla_norm_g, w_out, norm_ffn_g, w_ffn_gate, w_ffn_up, w_ffn_down,
                                 norm_final_g)))
    ms = dict(zip(WEIGHT_NAMES, (m_meta_tokens, m_norm_mix_g, m_w_in, m_conv_w, m_conv_b, m_conv_ln_g, m_conv_ln_b,
                                 m_gla_w_gate2, m_gla_gate_b, m_gla_norm_g, m_w_out, m_norm_ffn_g, m_w_ffn_gate,
                                 m_w_ffn_up, m_w_ffn_down, m_norm_final_g)))
    vs = dict(zip(WEIGHT_NAMES, (v_meta_tokens, v_norm_mix_g, v_w_in, v_conv_w, v_conv_b, v_conv_ln_g, v_conv_ln_b,
                                 v_gla_w_gate2, v_gla_gate_b, v_gla_norm_g, v_w_out, v_norm_ffn_g, v_w_ffn_gate,
                                 v_w_ffn_up, v_w_ffn_down, v_norm_final_g)))
    c = lax.axis_index("c")

    full = _full_weights(ws)
    loss, grad_x, grads = _local_step(x, loss_target, full)
    loss = lax.psum(loss, ("x", "y", "c"))

    rs_names = ("w_in", "w_out", "w_ffn_gate", "w_ffn_up", "w_ffn_down", "small")
    by_owner = [grads["w_in_t"].reshape(N_CHIPS, D_IN // N_CHIPS, D), grads["w_out"].reshape(N_CHIPS, D // N_CHIPS, D),
                grads["w_ffn_gate_t"].reshape(N_CHIPS, D_FF // N_CHIPS, D),
                grads["w_ffn_up_t"].reshape(N_CHIPS, D_FF // N_CHIPS, D),
                grads["w_ffn_down"].reshape(N_CHIPS, D_FF // N_CHIPS, D), _pack_small_sharded(grads)]
    from_sibling = _rs_to_sibling(by_owner)
    chip_sums = [_rs_add_halves(g, r, c, "rs_add_" + nm) for g, r, nm in zip(by_owner, from_sibling, rs_names)]
    halves = [_rs_sum_chips(p, "rs_sum_" + nm) for p, nm in zip(_rs_chip_exchange(chip_sums), rs_names)]
    reduced = dict(zip(rs_names, _rs_share(halves)))
    g_sharded = {"w_in": reduced["w_in"].T, "w_out": reduced["w_out"], "w_ffn_gate": reduced["w_ffn_gate"].T,
                 "w_ffn_up": reduced["w_ffn_up"].T, "w_ffn_down": reduced["w_ffn_down"],
                 **_unpack_small_sharded(reduced["small"])}
    out = {"grad": {}, "delta": {}, "new_m": {}, "new_v": {}}
    for name, g in g_sharded.items():
        shape = ws[name].shape
        flat = lambda a: a.reshape(shape[-2:])
        delta, new_m, new_v = _adamw(g, flat(ws[name]), flat(ms[name]), flat(vs[name]), "adamw_" + name)
        for kind, a in (("grad", g), ("delta", delta), ("new_m", new_m), ("new_v", new_v)):
            out[kind][name] = a.reshape(shape)

    small_shapes = {name: ws[name].shape for name, _, _, _ in SMALL_PARTS}
    g_s, d_s, m_s, v_s = _allreduce_small_adamw(_pack_small(grads), _pack_small(ws), _pack_small(ms), _pack_small(vs))
    for kind, slab in (("grad", g_s), ("delta", d_s), ("new_m", m_s), ("new_v", v_s)):
        out[kind].update(_unpack_small(slab, small_shapes))

    return (loss, grad_x, *[out[kind][name] for kind in ("grad", "delta", "new_m", "new_v") for name in WEIGHT_NAMES])


def _gather_plan(split, whole=(), axes=None):
    split, whole = list(split), list(whole)
    ns, n = len(split), len(split) + len(whole)

    def make(ins, outs, sems):
        ici_send, ici_recv, d2d_send, d2d_recv, own_send, own_recv = sems
        x, y, c = _mesh_pos()
        mine = 2 * x + y
        chips = _other_chips(x, y)
        blocks = [2 * px + py for px, py in chips]

        def own(a):
            return _remote(ins[a], outs[a].at[mine], own_send.at[a], own_recv.at[a], (x, y, 1 - c))

        def ici(a, k, block):
            px, py = chips[k]
            src, dst = ins[a], outs[a].at[block]
            if a < ns:
                src, dst = _half(src, c, axes[a]), _half(dst, c, axes[a])
            return _remote(src, dst, ici_send.at[3 * a + k], ici_recv.at[3 * a + k], (px, py, c))

        def d2d(a, k, half):
            part = _half(outs[a].at[blocks[k]], half, axes[a])
            return _remote(part, part, d2d_send.at[3 * a + k], d2d_recv.at[3 * a + k], (x, y, 1 - c))

        def start():
            for a in range(n):
                for k in range(3):
                    ici(a, k, mine).start()
                own(a).start()

        def finish():
            for a in range(n):
                for k in range(3):
                    ici(a, k, blocks[k]).wait_recv()
                    if a < ns:
                        d2d(a, k, c).start()
            for a in range(ns):
                for k in range(3):
                    d2d(a, k, 1 - c).wait_recv()
            for a in range(n):
                for k in range(3):
                    ici(a, k, mine).wait_send()
                    if a < ns:
                        d2d(a, k, c).wait_send()
                own(a).wait()

        return start, finish

    arrays = split + whole
    axes = [0] * ns if axes is None else list(axes)
    return _Plan(arrays, [jax.ShapeDtypeStruct((N_CHIPS,) + s.shape, s.dtype) for s in arrays],
                 [pltpu.SemaphoreType.DMA((3 * n,)), pltpu.SemaphoreType.DMA((3 * n,)),
                  pltpu.SemaphoreType.DMA((3 * ns,)), pltpu.SemaphoreType.DMA((3 * ns,)),
                  pltpu.SemaphoreType.DMA((n,)), pltpu.SemaphoreType.DMA((n,))], make)


def _to_sibling_plan(gs):
    n = len(gs)

    def make(ins, outs, sems):
        send_sems, recv_sems = sems
        x, y, c = _mesh_pos()

        def copy(a):
            return _remote(_half(ins[a], 1 - c, 2), outs[a], send_sems.at[a], recv_sems.at[a], (x, y, 1 - c))

        def start():
            for a in range(n):
                copy(a).start()

        def finish():
            for a in range(n):
                copy(a).wait()

        return start, finish

    return _Plan(list(gs), [jax.ShapeDtypeStruct(g.shape[:2] + (g.shape[2] // 2,), g.dtype) for g in gs],
                 [pltpu.SemaphoreType.DMA((n,)), pltpu.SemaphoreType.DMA((n,))], make)


def _chip_exchange_plan(ps):
    n = len(ps)

    def make(ins, outs, sems):
        send_sems, recv_sems = sems
        x, y, c = _mesh_pos()
        chips = _other_chips(x, y)

        def ici(a, k):
            px, py = chips[k]
            return _remote(ins[a].at[2 * px + py], outs[a].at[k], send_sems.at[3 * a + k],
                           recv_sems.at[3 * a + k], (px, py, c))

        def start():
            for a in range(n):
                for k in range(3):
                    ici(a, k).start()

        def finish():
            for a in range(n):
                for k in range(3):
                    ici(a, k).wait()

        return start, finish

    return _Plan(list(ps), [jax.ShapeDtypeStruct((3,) + p.shape[1:], p.dtype) for p in ps],
                 [pltpu.SemaphoreType.DMA((3 * n,)), pltpu.SemaphoreType.DMA((3 * n,))], make)


def _share_plan(halves):
    n = len(halves)

    def make(ins, outs, sems):
        send_sems, recv_sems = sems
        x, y, c = _mesh_pos()

        def d2d(a):
            return _remote(ins[a], outs[a], send_sems.at[a], recv_sems.at[a], (x, y, 1 - c))

        def start():
            for a in range(n):
                d2d(a).start()

        def finish():
            for a in range(n):
                d2d(a).wait()

        return start, finish

    return _Plan(list(halves), [jax.ShapeDtypeStruct(p.shape, p.dtype) for p in halves],
                 [pltpu.SemaphoreType.DMA((n,)), pltpu.SemaphoreType.DMA((n,))], make)


def _rs_sum(own, others, mine, name):
    _, rows, h = own.shape
    tr = rows // 2 if rows % 16 == 0 and rows > 64 else rows

    def body(mine_ref, own_ref, oth_ref, o_ref):
        p = oth_ref[...].astype(F32)
        o_ref[...] = ((own_ref[0].astype(F32) + p[0]) + p[1]) + p[2]

    return pl.pallas_call(
        body, name=name,
        grid_spec=pltpu.PrefetchScalarGridSpec(
            num_scalar_prefetch=1, grid=(rows // tr,),
            in_specs=[pl.BlockSpec((1, tr, h), lambda i, s: (s[0], i, 0)),
                      pl.BlockSpec((3, tr, h), lambda i, s: (0, i, 0))],
            out_specs=pl.BlockSpec((tr, h), lambda i, s: (i, 0))),
        out_shape=jax.ShapeDtypeStruct((rows, h), F32),
        compiler_params=_params(1),
    )(jnp.reshape(mine, (1,)).astype(jnp.int32), own, others)


def _join(mine, theirs, c):
    return jnp.where(c == 0, jnp.concatenate([mine, theirs], axis=1), jnp.concatenate([theirs, mine], axis=1))


LOSS_ROW = 5


def _merge_plans(a, b):
    na_in, na_out, na_sems = len(a.arrays), len(a.out_shape), len(a.sems)

    def make(ins, outs, sems):
        start_a, finish_a = a.make(ins[:na_in], outs[:na_out], sems[:na_sems])
        start_b, finish_b = b.make(ins[na_in:], outs[na_out:], sems[na_sems:])

        def start():
            start_a()
            start_b()

        def finish():
            finish_a()
            finish_b()

        return start, finish

    return _Plan(list(a.arrays) + list(b.arrays), list(a.out_shape) + list(b.out_shape),
                 list(a.sems) + list(b.sems), make)


def _exchange(plan, name):
    n_in, n_out = len(plan.arrays), len(plan.out_shape)

    def body(*refs):
        start, finish = plan.make(refs[:n_in], refs[n_in:n_in + n_out], refs[n_in + n_out:])
        start()
        finish()

    return pl.pallas_call(
        body, name=name, in_specs=[HBM_SPEC] * n_in, out_specs=[HBM_SPEC] * n_out, out_shape=list(plan.out_shape),
        scratch_shapes=list(plan.sems), compiler_params=pltpu.CompilerParams(has_side_effects=True),
    )(*plan.arrays)


def _adamw_halves(mine, theirs, c, w, m, v, name):
    rows, h = mine.shape
    tr = rows // 2 if rows % 16 == 0 else rows

    def body(c_ref, a_ref, b_ref, w_ref, m_ref, v_ref, go_ref, d_ref, m2_ref, v2_ref):
        g = jnp.where(pl.program_id(1) == c_ref[0], a_ref[...], b_ref[...])
        go_ref[...] = g
        d_ref[...], m2_ref[...], v2_ref[...] = _adam_update(g, w_ref[...], m_ref[...], v_ref[...])

    half = pl.BlockSpec((tr, h), lambda i, j, s: (i, 0))
    spec = pl.BlockSpec((tr, h), lambda i, j, s: (i, j))
    return pl.pallas_call(
        body, name=name,
        grid_spec=pltpu.PrefetchScalarGridSpec(num_scalar_prefetch=1, grid=(rows // tr, 2),
                                               in_specs=[half, half, spec, spec, spec], out_specs=[spec] * 4),
        out_shape=[jax.ShapeDtypeStruct((rows, 2 * h), F32)] * 4,
        compiler_params=_params(2),
    )(jnp.reshape(c, (1,)).astype(jnp.int32), mine, theirs, w, m, v)


ADAMW_STEPS = 4


def _adamw_many(items, c, plan=None):
    n = len(items)
    tiles = [it[0].shape[0] // ADAMW_STEPS for it in items]
    h = items[0][0].shape[1]

    def body(c_ref, *refs):
        ins, outs = refs[:5 * n], refs[5 * n:]
        own = pl.program_id(1) == c_ref[0]
        for i in range(n):
            a_ref, b_ref, w_ref, m_ref, v_ref = ins[5 * i:5 * i + 5]
            go_ref, d_ref, m2_ref, v2_ref = outs[4 * i:4 * i + 4]
            g = jnp.where(own, a_ref[...], b_ref[...])
            go_ref[...] = g
            d_ref[...], m2_ref[...], v2_ref[...] = _adam_update(g, w_ref[...], m_ref[...], v_ref[...])

    in_specs, out_specs, out_shape, args = [pl.BlockSpec(memory_space=pltpu.SMEM)], [], [], []
    for (mine, theirs, w, m, v), tr in zip(items, tiles):
        half = pl.BlockSpec((tr, h), lambda i, j: (i, 0))
        full = pl.BlockSpec((tr, h), lambda i, j: (i, j))
        in_specs += [half, half, full, full, full]
        out_specs += [full] * 4
        out_shape += [jax.ShapeDtypeStruct(w.shape, F32)] * 4
        args += [mine, theirs, w, m, v]
    res, extra = _call(body, name="adamw_early", grid=(ADAMW_STEPS, 2), in_specs=in_specs, out_specs=out_specs,
                       out_shape=out_shape, plan=plan)(jnp.reshape(c, (1,)).astype(jnp.int32), *args)
    return [res[4 * i:4 * i + 4] for i in range(n)], extra


def _all_to_all_plan(part):
    def make(ins, outs, sems):
        send_sems, recv_sems, local_sem = sems
        (p_ref,), (slots,) = ins, outs
        x, y, c = _mesh_pos()
        me = 4 * x + 2 * y + c
        peers = [(px, py, pc) for px in (x, 1 - x) for py in (y, 1 - y) for pc in (c, 1 - c)][1:]

        def remote(k, slot):
            return _remote(p_ref, slots.at[slot], send_sems.at[k], recv_sems.at[k], peers[k])

        def local():
            return pltpu.make_async_copy(p_ref, slots.at[me], local_sem)

        def start():
            for k in range(7):
                remote(k, me).start()
            local().start()

        def finish():
            for k, (px, py, pc) in enumerate(peers):
                remote(k, 4 * px + 2 * py + pc).wait_recv()
            for k in range(7):
                remote(k, me).wait_send()
            local().wait()

        return start, finish

    return _Plan([part], [jax.ShapeDtypeStruct((8,) + part.shape, part.dtype)],
                 [pltpu.SemaphoreType.DMA((7,)), pltpu.SemaphoreType.DMA((7,)), pltpu.SemaphoreType.DMA(())], make)


def _sum_slots_adamw(slots, late_slots, w, m, v):
    late_rows = late_slots.shape[1]

    def body(s_ref, l_ref, w_ref, m_ref, v_ref, g_ref, d_ref, m2_ref, v2_ref):
        g, late = s_ref[0], l_ref[0]
        for d in range(1, 8):
            g = g + s_ref[d]
            late = late + l_ref[d]
        g = jnp.concatenate([g[:late_rows] + late, g[late_rows:]], axis=0)
        g_ref[...] = g
        d_ref[...], m2_ref[...], v2_ref[...] = _adam_update(g, w_ref[...], m_ref[...], v_ref[...])

    vm = pl.BlockSpec(memory_space=pltpu.VMEM)
    shape = jax.ShapeDtypeStruct(w.shape, F32)
    return pl.pallas_call(body, name="small_sum_adamw", in_specs=[vm] * 5, out_specs=[vm] * 4,
                          out_shape=[shape] * 4)(slots, late_slots, w, m, v)


def _columns(gathered):
    return jnp.concatenate([gathered[j] for j in range(N_CHIPS)], axis=1)


def kernel(x, meta_tokens, norm_mix_g, w_in, conv_w, conv_b, conv_ln_g, conv_ln_b, gla_w_gate2, gla_gate_b, gla_norm_g, w_out, norm_ffn_g, w_ffn_gate, w_ffn_up, w_ffn_down, norm_final_g, loss_target, m_meta_tokens, m_norm_mix_g, m_w_in, m_conv_w, m_conv_b, m_conv_ln_g, m_conv_ln_b, m_gla_w_gate2, m_gla_gate_b, m_gla_norm_g, m_w_out, m_norm_ffn_g, m_w_ffn_gate, m_w_ffn_up, m_w_ffn_down, m_norm_final_g, v_meta_tokens, v_norm_mix_g, v_w_in, v_conv_w, v_conv_b, v_conv_ln_g, v_conv_ln_b, v_gla_w_gate2, v_gla_gate_b, v_gla_norm_g, v_w_out, v_norm_ffn_g, v_w_ffn_gate, v_w_ffn_up, v_w_ffn_down, v_norm_final_g):
    ws = dict(zip(WEIGHT_NAMES, (meta_tokens, norm_mix_g, w_in, conv_w, conv_b, conv_ln_g, conv_ln_b, gla_w_gate2,
                                 gla_gate_b, gla_norm_g, w_out, norm_ffn_g, w_ffn_gate, w_ffn_up, w_ffn_down,
                                 norm_final_g)))
    ms = dict(zip(WEIGHT_NAMES, (m_meta_tokens, m_norm_mix_g, m_w_in, m_conv_w, m_conv_b, m_conv_ln_g, m_conv_ln_b,
                                 m_gla_w_gate2, m_gla_gate_b, m_gla_norm_g, m_w_out, m_norm_ffn_g, m_w_ffn_gate,
                                 m_w_ffn_up, m_w_ffn_down, m_norm_final_g)))
    vs = dict(zip(WEIGHT_NAMES, (v_meta_tokens, v_norm_mix_g, v_w_in, v_conv_w, v_conv_b, v_conv_ln_g, v_conv_ln_b,
                                 v_gla_w_gate2, v_gla_gate_b, v_gla_norm_g, v_w_out, v_norm_ffn_g, v_w_ffn_gate,
                                 v_w_ffn_up, v_w_ffn_down, v_norm_final_g)))
    c = lax.axis_index("c")
    shard = lambda d, name: d[name].reshape(d[name].shape[-2:])
    vec = {name: ws[name].reshape(1, -1) for name, _, _, _ in SMALL_PARTS}
    n_ex, seq, _ = x.shape
    lp = HEAD_ROWS + seq
    t = n_ex * lp

    (tgt,), (w_in_g, meta_g, conv_w_g, w2_g) = _pad_head_rows(loss_target, plan=_gather_plan(
        [shard(ws, "w_in").T.astype(BF16)],
        [shard(ws, "meta_tokens"), shard(ws, "conv_w"), shard(ws, "gla_w_gate2")], axes=[1]))
    w_in_t = jnp.concatenate([w_in_g.reshape(D_IN, D), jnp.zeros((D_IN_PAD - D_IN, D), BF16)], axis=0)
    w_in_full = w_in_t.T
    conv_w_full = jnp.concatenate([_columns(conv_w_g), jnp.zeros((32 - CONV_W, C_CONV), F32)], axis=0)
    w2_full = jnp.concatenate([_columns(w2_g), jnp.zeros((128 - RANK, GLA_K), F32)], axis=0).astype(BF16)

    meta = jnp.broadcast_to(_columns(meta_g)[None], (n_ex, N_META, D))
    h0 = jnp.concatenate([jnp.zeros((n_ex, PAD_ROWS, D), F32), meta, x], axis=1).reshape(t, D)
    tgt = tgt.reshape(t, D)
    row_mask = jnp.concatenate([jnp.zeros((n_ex, HEAD_ROWS, 1), F32), jnp.ones((n_ex, seq, 1), F32)],
                               axis=1).reshape(t, 1)

    (u, hn), (w_out_g,) = _in_proj(h0, vec["norm_mix_g"], w_in_full,
                                   plan=_gather_plan([shard(ws, "w_out").astype(BF16)]))
    (yc, y_conv), (gate_g,) = _conv_fwd(
        u, conv_w_full, vec["conv_b"], vec["conv_ln_g"], vec["conv_ln_b"], n_ex, lp,
        plan=_gather_plan([shard(ws, "w_ffn_gate").T.astype(BF16)]))
    (y_gla, states), (up_g,) = _gla_fwd(u, w2_full, vec["gla_gate_b"], vec["gla_norm_g"], n_ex, lp,
                                        plan=_gather_plan([shard(ws, "w_ffn_up").T.astype(BF16)]))
    w_out_full = w_out_g.reshape(D, D)
    w_gate_t, w_up_t = gate_g.reshape(D_FF, D), up_g.reshape(D_FF, D)

    (h1, hn2, gate, up, act), (down_g,) = _mix_out_ffn_up(
        h0, y_conv, y_gla, w_out_full, vec["norm_ffn_g"], w_gate_t.T, w_up_t.T,
        plan=_gather_plan([shard(ws, "w_ffn_down").astype(BF16)]))
    w_down_full = down_g.reshape(D_FF, D)
    dh2, loss, d_final_g = _ffn_down_loss(act, w_down_full, h1, tgt, vec["norm_final_g"], row_mask)
    dgate, dup, dh1, dycat, d_ffn_g = _ffn_bwd(dh2, gate, up, h1, w_down_full.T, w_gate_t, w_up_t, w_out_full.T,
                                                vec["norm_ffn_g"])

    early = ("w_ffn_gate", "w_ffn_up", "w_ffn_down", "w_out")
    ffn_block = lambda g: g.reshape(N_CHIPS, D_FF // N_CHIPS, D)
    g_gate = ffn_block(_wgrad(dgate, hn2, "wgrad_gate"))
    g_up, (gate_sib,) = _wgrad_hosting(dup, hn2, "wgrad_up", _to_sibling_plan([g_gate]))
    g_up = ffn_block(g_up)
    g_down, (up_sib,) = _wgrad_hosting(act, dh2, "wgrad_down", _to_sibling_plan([g_up]))
    g_down = ffn_block(g_down)
    g_out = jnp.concatenate([_wgrad(y_conv, dh1, "wgrad_out_conv"), _wgrad(y_gla, dh1, "wgrad_out_gla")],
                            axis=0).reshape(N_CHIPS, D // N_CHIPS, D)
    cs_gate = _rs_add_halves(g_gate, gate_sib, c, "rs_add_w_ffn_gate")
    cs_up = _rs_add_halves(g_up, up_sib, c, "rs_add_w_ffn_up")
    (du_conv, d_conv_w, d_conv_b, d_ln_g, d_ln_b), (ex_gate, ex_up, down_sib, out_sib) = _conv_bwd(
        dycat, yc, u, conv_w_full, vec["conv_ln_g"], vec["conv_ln_b"], n_ex, lp,
        plan=_merge_plans(_chip_exchange_plan([cs_gate, cs_up]), _to_sibling_plan([g_down, g_out])))
    cs_down = _rs_add_halves(g_down, down_sib, c, "rs_add_w_ffn_down")
    cs_out = _rs_add_halves(g_out, out_sib, c, "rs_add_w_out")
    (du_gla, d_w2, d_gate_b, d_norm_g), (ex_down, ex_out) = _gla_bwd(
        dycat, u, states, w2_full, vec["gla_gate_b"], vec["gla_norm_g"], n_ex, lp,
        plan=_chip_exchange_plan([cs_down, cs_out]))
    mine = 2 * lax.axis_index("x") + lax.axis_index("y")
    halves = [_rs_sum(own, oth, mine, "rs_sum_" + nm)
              for own, oth, nm in zip((cs_gate, cs_up, cs_down, cs_out), (ex_gate, ex_up, ex_down, ex_out), early)]

    d_w_in_t = jnp.concatenate([_wgrad(du_conv, hn, "wgrad_in_conv"), _wgrad(du_gla, hn, "wgrad_in_gla")],
                               axis=0)[:D_IN].reshape(N_CHIPS, D_IN // N_CHIPS, D)
    (in_from_sibling,) = _exchange(_to_sibling_plan([d_w_in_t]), "rs_late_to_sibling")
    in_chip_sum = _rs_add_halves(d_w_in_t, in_from_sibling, c, "rs_add_w_in")
    small = {"norm_mix_g": jnp.zeros((1, D), F32), "norm_ffn_g": d_ffn_g, "norm_final_g": d_final_g,
             "conv_b": d_conv_b, "conv_ln_g": d_ln_g, "conv_ln_b": d_ln_b, "gla_gate_b": d_gate_b,
             "gla_norm_g": d_norm_g}
    part = lax.dynamic_update_slice(_pack_small(small), loss[:, :1], (LOSS_ROW, 0))
    part = jnp.concatenate([part, jnp.zeros((N_META, D), F32), d_conv_w.reshape(16, D), d_w2[:RANK].reshape(4, D),
                            jnp.zeros((4, D), F32)], axis=0)
    (dh0, d_mix_g), shared = _in_proj_bwd(
        du_conv, du_gla, w_in_t[:2 * C_CONV], w_in_t[2 * C_CONV:], h0, dh1, vec["norm_mix_g"],
        plan=_merge_plans(_merge_plans(_share_plan(halves), _chip_exchange_plan([in_chip_sum])),
                          _all_to_all_plan(part)))
    dh0 = dh0.reshape(n_ex, lp, D)
    grad_x = dh0[:, HEAD_ROWS:]
    late_part = jnp.concatenate([d_mix_g, jnp.zeros((SMALL_ROWS - 1, D), F32),
                                 jnp.sum(dh0[:, PAD_ROWS:HEAD_ROWS], axis=0)], axis=0)
    (late_slots,) = _exchange(_all_to_all_plan(late_part), "small_late_all_to_all")

    out = {"grad": {}, "delta": {}, "new_m": {}, "new_v": {}}

    def update(name, g=None, halves=None, transposed=False):
        shape = ws[name].shape
        lay = (lambda a: a.T) if transposed else (lambda a: a)
        w2d, m2d, v2d = lay(shard(ws, name)), lay(shard(ms, name)), lay(shard(vs, name))
        if halves is not None:
            res = _adamw_halves(*halves, c, w2d, m2d, v2d, "adamw_" + name)
        else:
            res = [g, *_adamw(g, w2d, m2d, v2d, "adamw_" + name)]
        for kind, a in zip(("grad", "delta", "new_m", "new_v"), res):
            out[kind][name] = lay(a).reshape(shape)

    small_shapes = {name: ws[name].shape for name, _, _, _ in SMALL_PARTS}
    early_layout = (("w_ffn_gate", True), ("w_ffn_up", True), ("w_ffn_down", False), ("w_out", False))
    items = []
    for (name, transposed), mine_half, their_half in zip(early_layout, halves, shared):
        lay = (lambda a: a.T) if transposed else (lambda a: a)
        items.append((mine_half, their_half, lay(shard(ws, name)), lay(shard(ms, name)), lay(shard(vs, name))))
    slots = shared[5]
    updated, _ = _adamw_many(items, c)
    for (name, transposed), res in zip(early_layout, updated):
        lay = (lambda a: a.T) if transposed else (lambda a: a)
        for kind, a in zip(("grad", "delta", "new_m", "new_v"), res):
            out[kind][name] = lay(a).reshape(ws[name].shape)

    in_half = _rs_sum(in_chip_sum, shared[4], mine, "rs_sum_w_in")
    (in_shared,) = _exchange(_share_plan([in_half]), "rs_late_share")
    update("w_in", halves=(in_half, in_shared), transposed=True)

    tall = lambda a: jnp.concatenate([a, jnp.zeros((part.shape[0] - SMALL_ROWS, D), F32)], axis=0)
    g_s, d_s, m_s, v_s = _sum_slots_adamw(slots, late_slots, tall(_pack_small(ws)), tall(_pack_small(ms)),
                                          tall(_pack_small(vs)))
    for kind, slab in (("grad", g_s), ("delta", d_s), ("new_m", m_s), ("new_v", v_s)):
        out[kind].update(_unpack_small(slab, small_shapes))
    loss = g_s[LOSS_ROW, 0]
    block = lambda a, width: lax.dynamic_slice_in_dim(a, mine * width, width, axis=1)
    update("meta_tokens", g=block(g_s[8:24], D // N_CHIPS))
    update("conv_w", g=block(g_s[24:40].reshape(32, C_CONV), C_CONV // N_CHIPS)[:CONV_W])
    update("gla_w_gate2", g=block(g_s[40:44].reshape(RANK, GLA_K), GLA_K // N_CHIPS))

    return (loss, grad_x, *[out[kind][name] for kind in ("grad", "delta", "new_m", "new_v") for name in WEIGHT_NAMES])
```

```python
import functools
from typing import Any, Callable, NamedTuple, Sequence

import jax
import jax.numpy as jnp
from jax import lax
from jax.experimental import pallas as pl
from jax.experimental.pallas import tpu as pltpu

F32 = jnp.float32
BF16 = jnp.bfloat16
MESH = pl.DeviceIdType.MESH

D = 1024
N_META = 16
C_CONV = 512
CONV_W = 31
GLA_K = 256
GLA_V = 512
N_HEADS = 4
DK = 64
DV = 128
RANK = 16
CHUNK = 64
PAD_ROWS = CHUNK - N_META
HEAD_ROWS = CHUNK
D_IN = 2576
D_IN_PAD = 2688
D_GLA_IN = D_IN_PAD - 2 * C_CONV
D_FF = 2816
RMS_EPS = 1e-6
LN_EPS = 1e-5
GATE_TAU = 16.0
N_CHIPS = 4

ADAM_LR = 0.001
ADAM_B1 = 0.9
ADAM_B2 = 0.999
ADAM_EPS = 1e-08
ADAM_WD = 0.01
ADAM_STEP = 10

V7X_VMEM_BYTES = 64 * 1024 * 1024
VMEM_LIMIT = V7X_VMEM_BYTES - 8 * 1024 * 1024

SLAB_ROWS = 3072
HALF_ROWS = SLAB_ROWS // 2
SMALL_ROWS = 8


def _dot(a, b):
    return jnp.dot(a, b, preferred_element_type=F32)


def _dot_nt(a, b):
    return lax.dot_general(a, b, (((1,), (1,)), ((), ())), preferred_element_type=F32)


def _dot_tn(a, b):
    return lax.dot_general(a, b, (((0,), (0,)), ((), ())), preferred_element_type=F32)


def _sigmoid(x):
    return 1.0 / (1.0 + jnp.exp(-x))


def _const_spec(shape):
    return pl.BlockSpec(shape, lambda *_: (0,) * len(shape), pipeline_mode=pl.Buffered(1))


def _acc_spec(shape):
    return pl.BlockSpec(shape, lambda *_: (0,) * len(shape))


def _params(n_axes):
    return pltpu.CompilerParams(dimension_semantics=("arbitrary",) * n_axes, vmem_limit_bytes=VMEM_LIMIT)


def _row_tile(t, want):
    for r in (want, 384, 192, 128, 64):
        if r <= want and t % r == 0:
            return r
    raise ValueError(f"no row tile for {t}")


ROW_PART = 128


def _row_parts(r):
    if r % ROW_PART:
        return [slice(None)]
    return [pl.ds(i * ROW_PART, ROW_PART) for i in range(r // ROW_PART)]


def _in_lockstep(bodies):
    live = list(bodies)
    while live:
        still = []
        for g in live:
            try:
                next(g)
                still.append(g)
            except StopIteration:
                pass
        live = still


class _Plan(NamedTuple):
    arrays: Sequence[Any]
    out_shape: Sequence[Any]
    sems: Sequence[Any]
    make: Callable


def _call(body, *, name, grid, in_specs, out_specs, out_shape, scratch_shapes=(), plan=None):
    n_in, n_out, n_scr = len(in_specs), len(out_specs), len(scratch_shapes)
    if plan is None:
        plan = _Plan([], [], [], lambda ins, outs, sems: (lambda: None, lambda: None))
    nx_in, nx_out = len(plan.arrays), len(plan.out_shape)

    def hosted(*refs):
        ins, xins = refs[:n_in], refs[n_in:n_in + nx_in]
        o0 = n_in + nx_in
        outs, xouts = refs[o0:o0 + n_out], refs[o0 + n_out:o0 + n_out + nx_out]
        s0 = o0 + n_out + nx_out
        scr, sems = refs[s0:s0 + n_scr], refs[s0 + n_scr:]
        ids = [pl.program_id(a) for a in range(len(grid))]
        first = functools.reduce(jnp.logical_and, [i == 0 for i in ids])
        last = functools.reduce(jnp.logical_and, [i == g - 1 for i, g in zip(ids, grid)])
        start, finish = plan.make(xins, xouts, sems)
        pl.when(first)(start)
        body(*ins, *outs, *scr)
        pl.when(last)(finish)

    call = pl.pallas_call(
        hosted, name=name, grid=grid, in_specs=list(in_specs) + [HBM_SPEC] * nx_in,
        out_specs=list(out_specs) + [HBM_SPEC] * nx_out, out_shape=list(out_shape) + list(plan.out_shape),
        scratch_shapes=list(scratch_shapes) + list(plan.sems),
        compiler_params=pltpu.CompilerParams(dimension_semantics=("arbitrary",) * len(grid),
                                             vmem_limit_bytes=VMEM_LIMIT, has_side_effects=nx_in > 0))

    def run(*args):
        res = call(*args, *plan.arrays)
        return res[:n_out], res[n_out:]

    return run


def _pad_head_rows(arrays, plan=None):
    n_ex, seq, _ = arrays[0].shape
    nc = (HEAD_ROWS + seq) // CHUNK
    n = len(arrays)

    def body(*refs):
        for a_ref, o_ref in zip(refs[:n], refs[n:]):
            o_ref[...] = jnp.where(pl.program_id(0) > 0, a_ref[...], 0.0)

    return _call(
        body, name="pad_head_rows", grid=(nc,),
        in_specs=[pl.BlockSpec((n_ex, CHUNK, D), lambda i: (0, jnp.maximum(i - 1, 0), 0))] * n,
        out_specs=[pl.BlockSpec((n_ex, CHUNK, D), lambda i: (0, i, 0))] * n,
        out_shape=[jax.ShapeDtypeStruct((n_ex, HEAD_ROWS + seq, D), F32)] * n,
        plan=plan,
    )(*arrays)


def _set_meta_rows(h0, meta):
    n_ex = h0.shape[0]

    def body(h_ref, meta_ref, o_ref):
        o_ref[...] = jnp.concatenate(
            [h_ref[:, :PAD_ROWS, :], jnp.broadcast_to(meta_ref[...][None], (n_ex, N_META, D))], axis=1)

    head = pl.BlockSpec((n_ex, HEAD_ROWS, D), lambda i: (0, 0, 0))
    return pl.pallas_call(
        body, name="set_meta_rows", grid=(1,), in_specs=[head, pl.BlockSpec((N_META, D), lambda i: (0, 0))],
        out_specs=head, out_shape=jax.ShapeDtypeStruct(h0.shape, F32), input_output_aliases={0: 0},
        compiler_params=_params(1),
    )(h0, meta)


def _in_proj(h0, g_mix, w_in, plan=None):
    t = h0.shape[0]
    r = _row_tile(t, 384)

    def body(h_ref, g_ref, w_ref, u_ref, hn_ref):
        def part(rows):
            h = h_ref[rows, :]
            rstd = lax.rsqrt(jnp.mean(h * h, axis=-1, keepdims=True) + RMS_EPS)
            hn = (h * rstd * g_ref[...]).astype(BF16)
            hn_ref[rows, :] = hn
            yield
            u_ref[rows, :] = _dot(hn, w_ref[...])

        _in_lockstep(part(rows) for rows in _row_parts(r))

    return _call(
        body, name="in_proj", grid=(t // r,),
        in_specs=[pl.BlockSpec((r, D), lambda i: (i, 0)), _const_spec((1, D)), _const_spec((D, D_IN_PAD))],
        out_specs=[pl.BlockSpec((r, D_IN_PAD), lambda i: (i, 0)), pl.BlockSpec((r, D), lambda i: (i, 0))],
        out_shape=[jax.ShapeDtypeStruct((t, D_IN_PAD), F32), jax.ShapeDtypeStruct((t, D), BF16)],
        plan=plan,
    )(h0, g_mix, w_in)


CONV_TILE = 192
CONV_SUB = 32
CONV_LEAD = CONV_SUB - (CONV_W - 1)
SUBLANES = 8


def _shifted_copies(src, dst, r):
    for s in range(1, SUBLANES):
        dst[s - 1] = src[s:s + r + CONV_SUB - SUBLANES, :]


def _shifted_rows(src, shifted, start):
    base, s = SUBLANES * (start // SUBLANES), start % SUBLANES
    if s == 0:
        return src[base:base + CONV_SUB, :]
    return shifted[s - 1, base:base + CONV_SUB, :]


def _conv_fwd(u, conv_w, conv_b, ln_g, ln_b, n_ex, lp, plan=None):
    r = CONV_TILE
    nt = lp // r
    hb = r // CONV_SUB

    def body(cur_ref, prev_ref, w_ref, b_ref, lg_ref, lb_ref, yc_ref, y_ref, glu, glu_sh):
        i = pl.program_id(1)
        cur = cur_ref[...]
        glu[CONV_SUB:CONV_SUB + r, :] = cur[:, :C_CONV] * _sigmoid(cur[:, C_CONV:])
        pv = prev_ref[...]
        halo = pv[:, :C_CONV] * _sigmoid(pv[:, C_CONV:])
        glu[0:CONV_SUB, :] = jnp.where(i > 0, halo, 0.0)
        _shifted_copies(glu, glu_sh, r)
        w = w_ref[...]
        for j in range(r // CONV_SUB):
            r0 = j * CONV_SUB
            acc = jnp.zeros((CONV_SUB, C_CONV), F32) + b_ref[...]
            for k in range(CONV_W):
                acc = acc + w[k:k + 1, :] * _shifted_rows(glu, glu_sh, r0 + CONV_LEAD + k)
            mu = jnp.mean(acc, axis=-1, keepdims=True)
            cen = acc - mu
            var = jnp.mean(cen * cen, axis=-1, keepdims=True)
            out = cen * lax.rsqrt(var + LN_EPS) * lg_ref[...] + lb_ref[...]
            y = out * _sigmoid(out)
            row = i * r + r0 + lax.broadcasted_iota(jnp.int32, (CONV_SUB, 1), 0)
            y = jnp.where(row >= PAD_ROWS, y, 0.0)
            yc_ref[r0:r0 + CONV_SUB, :] = acc
            y_ref[r0:r0 + CONV_SUB, :] = y.astype(BF16)

    t = n_ex * lp
    return _call(
        body, name="conv_fwd", grid=(n_ex, nt),
        in_specs=[pl.BlockSpec((r, 2 * C_CONV), lambda b, i: (b * nt + i, 0)),
                  pl.BlockSpec((CONV_SUB, 2 * C_CONV), lambda b, i: (jnp.maximum((b * nt + i) * hb - 1, 0), 0)),
                  _const_spec((32, C_CONV)), _const_spec((1, C_CONV)), _const_spec((1, C_CONV)), _const_spec((1, C_CONV))],
        out_specs=[pl.BlockSpec((r, C_CONV), lambda b, i: (b * nt + i, 0)),
                   pl.BlockSpec((r, C_CONV), lambda b, i: (b * nt + i, 0))],
        out_shape=[jax.ShapeDtypeStruct((t, C_CONV), F32), jax.ShapeDtypeStruct((t, C_CONV), BF16)],
        scratch_shapes=[pltpu.VMEM((r + CONV_SUB, C_CONV), F32),
                        pltpu.VMEM((SUBLANES - 1, r + CONV_SUB - SUBLANES, C_CONV), F32)],
        plan=plan,
    )(u, u, conv_w, conv_b, ln_g, ln_b)


def _gla_gates(lr, w2, gb, first_chunk):
    z = _dot(lr.astype(BF16), w2) + gb
    a = (jnp.minimum(z, 0.0) - jnp.log(1.0 + jnp.exp(-jnp.abs(z)))) * (1.0 / GATE_TAU)
    row = lax.broadcasted_iota(jnp.int32, (CHUNK, 1), 0)
    live = jnp.logical_or(jnp.logical_not(first_chunk), row >= PAD_ROWS)
    return z, jnp.where(live, a, 0.0), live


def _tri(lower):
    i = lax.broadcasted_iota(jnp.int32, (CHUNK, CHUNK), 0)
    j = lax.broadcasted_iota(jnp.int32, (CHUNK, CHUNK), 1)
    return (i >= j) if lower else (i <= j)


def _gla_fwd_per_head(u, w2, gb, ng, n_ex, lp, plan=None):
    nc = lp // CHUNK
    t = n_ex * lp

    def body(qk_ref, v_ref, g_ref, lr_ref, w2_ref, gb_ref, ng_ref, y_ref, st_ref, state):
        n = pl.program_id(0)

        @pl.when(n == 0)
        def _():
            state[...] = jnp.zeros_like(state)

        causal = _tri(True)
        for e in range(n_ex):
            st = state[e]
            st_ref[e] = st
            qk = qk_ref[e]
            q, k = qk[:, :GLA_K], qk[:, GLA_K:]
            _, a, _ = _gla_gates(lr_ref[e], w2_ref[...], gb_ref[...], n == 0)
            b = jnp.dot(causal.astype(F32), a, preferred_element_type=F32, precision=lax.Precision.HIGHEST)
            bl = b[CHUNK - 1:CHUNK, :]
            q_in = (q * (DK ** -0.5) * jnp.exp(b)).astype(BF16)
            k_in = (k * jnp.exp(-b)).astype(BF16)
            k_dec = (k * jnp.exp(bl - b)).astype(BF16)
            decay = jnp.exp(bl)
            v = v_ref[e]
            g = g_ref[e]
            st_b = st.astype(BF16)
            ys, new = [], []
            for h in range(N_HEADS):
                ks = slice(h * DK, (h + 1) * DK)
                vs = slice(h * DV, (h + 1) * DV)
                vh = v[:, vs].astype(BF16)
                s = jnp.where(causal, _dot_nt(q_in[:, ks], k_in[:, ks]), 0.0)
                o = _dot(s.astype(BF16), vh) + _dot_nt(q_in[:, ks], st_b[:, ks])
                new.append(decay[:, ks] * st[:, ks] + _dot_tn(vh, k_dec[:, ks]))
                rstd = lax.rsqrt(jnp.mean(o * o, axis=-1, keepdims=True) + RMS_EPS)
                gh = g[:, vs]
                ys.append(o * rstd * ng_ref[...] * (gh * _sigmoid(gh)))
            state[e] = jnp.concatenate(new, axis=1)
            y_ref[e] = jnp.concatenate(ys, axis=1).astype(BF16)

    u3 = u.reshape(n_ex, lp, D_IN_PAD)
    blk = lambda w, col: pl.BlockSpec((n_ex, CHUNK, w), lambda n: (0, n, col))
    (y, states), extra = _call(
        body, name="gla_fwd", grid=(nc,),
        in_specs=[blk(2 * GLA_K, 2), blk(GLA_V, 3), blk(GLA_V, 4), blk(128, 20),
                  _const_spec((128, GLA_K)), _const_spec((1, GLA_K)), _const_spec((1, DV))],
        out_specs=[blk(GLA_V, 0), pl.BlockSpec((n_ex, DV, GLA_K), lambda n: (0, n, 0))],
        out_shape=[jax.ShapeDtypeStruct((n_ex, lp, GLA_V), BF16),
                   jax.ShapeDtypeStruct((n_ex, nc * DV, GLA_K), F32)],
        scratch_shapes=[pltpu.VMEM((n_ex, DV, GLA_K), F32)],
        plan=plan,
    )(u3, u3, u3, u3, w2, gb, ng)
    return (y.reshape(t, GLA_V), states), extra


FFN_TILE = 192


def _mix_out_ffn_up(h0, y_conv, y_gla, w_out, g_ffn, w_gate, w_up, plan=None):
    t = h0.shape[0]
    r = _row_tile(t, 384)

    def body(h0_ref, yc_ref, yg_ref, wo_ref, g_ref, wg_ref, wu_ref, h1_ref, hn_ref, gate_ref, up_ref, act_ref):
        h1 = h0_ref[...] + _dot(yc_ref[...], wo_ref[0:C_CONV, :]) + _dot(yg_ref[...], wo_ref[C_CONV:D, :])
        h1_ref[...] = h1
        rstd = lax.rsqrt(jnp.mean(h1 * h1, axis=-1, keepdims=True) + RMS_EPS)
        hn = (h1 * rstd * g_ref[...]).astype(BF16)
        hn_ref[...] = hn
        gate = _dot(hn, wg_ref[...])
        up = _dot(hn, wu_ref[...])
        gate_ref[...] = gate
        up_ref[...] = up
        act_ref[...] = (gate * _sigmoid(gate) * up).astype(BF16)

    rows = lambda w: pl.BlockSpec((r, w), lambda i: (i, 0))
    return _call(
        body, name="mix_out_ffn_up", grid=(t // r,),
        in_specs=[rows(D), rows(C_CONV), rows(GLA_V), _const_spec((D, D)), _const_spec((1, D)),
                  _const_spec((D, D_FF)), _const_spec((D, D_FF))],
        out_specs=[rows(D), rows(D), rows(D_FF), rows(D_FF), rows(D_FF)],
        out_shape=[jax.ShapeDtypeStruct((t, D), F32), jax.ShapeDtypeStruct((t, D), BF16),
                   jax.ShapeDtypeStruct((t, D_FF), F32), jax.ShapeDtypeStruct((t, D_FF), F32),
                   jax.ShapeDtypeStruct((t, D_FF), BF16)],
        plan=plan,
    )(h0, y_conv, y_gla, w_out, g_ffn, w_gate, w_up)


def _ffn_down_loss(act, w_down, h1, target, g_final, row_mask):
    t = h1.shape[0]
    r = _row_tile(t, 384)

    def body(act_ref, wd_ref, h1_ref, tgt_ref, gf_ref, mask_ref, dh2_ref, loss_ref, dgf_ref):
        @pl.when(pl.program_id(0) == 0)
        def _():
            loss_ref[...] = jnp.zeros_like(loss_ref)
            dgf_ref[...] = jnp.zeros_like(dgf_ref)

        gf = gf_ref[...]

        def part(rows):
            h2 = h1_ref[rows, :] + _dot(act_ref[rows, :], wd_ref[...])
            yield
            rstd = lax.rsqrt(jnp.mean(h2 * h2, axis=-1, keepdims=True) + RMS_EPS)
            nrm = h2 * rstd
            err = (nrm * gf - tgt_ref[rows, :]) * mask_ref[rows, :]
            loss_ref[...] += jnp.sum(err * err) * (0.5 / D)
            dy = err * (1.0 / D)
            dgf_ref[...] += jnp.sum(dy * nrm, axis=0, keepdims=True)
            dn = dy * gf
            dh2_ref[rows, :] = rstd * (dn - nrm * jnp.mean(dn * nrm, axis=-1, keepdims=True))

        _in_lockstep(part(rows) for rows in _row_parts(r))

    rows = lambda w: pl.BlockSpec((r, w), lambda i: (i, 0))
    return pl.pallas_call(
        body, name="ffn_down_loss", grid=(t // r,),
        in_specs=[rows(D_FF), _const_spec((D_FF, D)), rows(D), rows(D), _const_spec((1, D)), rows(1)],
        out_specs=[rows(D), _acc_spec((1, 128)), _acc_spec((1, D))],
        out_shape=[jax.ShapeDtypeStruct((t, D), F32), jax.ShapeDtypeStruct((1, 128), F32),
                   jax.ShapeDtypeStruct((1, D), F32)],
        compiler_params=_params(1),
    )(act, w_down, h1, target, g_final, row_mask)


def _ffn_bwd(dh2, gate, up, h1, w_down_t, w_gate_t, w_up_t, w_out_t, g_ffn):
    t = h1.shape[0]
    r = _row_tile(t, FFN_TILE)

    def body(dh2_ref, gate_ref, up_ref, h1_ref, wd_ref, wg_ref, wu_ref, wo_ref, g_ref,
             dgate_ref, dup_ref, dh1_ref, dycat_ref, dg_ref):
        @pl.when(pl.program_id(0) == 0)
        def _():
            dg_ref[...] = jnp.zeros_like(dg_ref)

        dh2 = dh2_ref[...]
        dact = _dot(dh2.astype(BF16), wd_ref[...])
        gate = gate_ref[...]
        sg = _sigmoid(gate)
        dgate = (dact * up_ref[...] * (sg * (1.0 + gate * (1.0 - sg)))).astype(BF16)
        dup = (dact * (gate * sg)).astype(BF16)
        dgate_ref[...] = dgate
        dup_ref[...] = dup
        dhn = _dot(dgate, wg_ref[...]) + _dot(dup, wu_ref[...])
        h1 = h1_ref[...]
        rstd = lax.rsqrt(jnp.mean(h1 * h1, axis=-1, keepdims=True) + RMS_EPS)
        nrm = h1 * rstd
        dg_ref[...] += jnp.sum(dhn * nrm, axis=0, keepdims=True)
        dn = dhn * g_ref[...]
        dh1 = dh2 + rstd * (dn - nrm * jnp.mean(dn * nrm, axis=-1, keepdims=True))
        dh1_ref[...] = dh1
        dycat_ref[...] = _dot(dh1.astype(BF16), wo_ref[...])

    rows = lambda w: pl.BlockSpec((r, w), lambda i: (i, 0))
    return pl.pallas_call(
        body, name="ffn_bwd", grid=(t // r,),
        in_specs=[rows(D), rows(D_FF), rows(D_FF), rows(D), _const_spec((D, D_FF)), _const_spec((D_FF, D)),
                  _const_spec((D_FF, D)), _const_spec((D, D)), _const_spec((1, D))],
        out_specs=[rows(D_FF), rows(D_FF), rows(D), rows(D), _acc_spec((1, D))],
        out_shape=[jax.ShapeDtypeStruct((t, D_FF), BF16), jax.ShapeDtypeStruct((t, D_FF), BF16),
                   jax.ShapeDtypeStruct((t, D), F32), jax.ShapeDtypeStruct((t, D), F32),
                   jax.ShapeDtypeStruct((1, D), F32)],
        compiler_params=_params(1),
    )(dh2, gate, up, h1, w_down_t, w_gate_t, w_up_t, w_out_t, g_ffn)


def _conv_bwd(dycat, yc, u, conv_w, ln_g, ln_b, n_ex, lp, plan=None):
    r = CONV_TILE
    nt = lp // r
    hb = r // CONV_SUB
    nsub = r // CONV_SUB

    def ln_bwd(dy, yc_rows, live, lg, lb):
        mu = jnp.mean(yc_rows, axis=-1, keepdims=True)
        cen = yc_rows - mu
        rs = lax.rsqrt(jnp.mean(cen * cen, axis=-1, keepdims=True) + LN_EPS)
        yn = cen * rs
        out = yn * lg + lb
        so = _sigmoid(out)
        dout = jnp.where(live, dy * (so * (1.0 + out * (1.0 - so))), 0.0)
        dyn = dout * lg
        dyc = rs * (dyn - jnp.mean(dyn, axis=-1, keepdims=True) - yn * jnp.mean(dyn * yn, axis=-1, keepdims=True))
        return dyc, dout, yn

    def body(dy_ref, dyn_ref, yc_ref, ycn_ref, cur_ref, prev_ref, w_ref, lg_ref, lb_ref,
             du_ref, dw_ref, db_ref, dlg_ref, dlb_ref, glu, dycs, dwacc, glu_sh, dycs_sh):
        b = pl.program_id(0)
        i = pl.program_id(1)
        first = jnp.logical_and(b == 0, i == 0)

        @pl.when(first)
        def _():
            dwacc[...] = jnp.zeros_like(dwacc)
            db_ref[...] = jnp.zeros_like(db_ref)
            dlg_ref[...] = jnp.zeros_like(dlg_ref)
            dlb_ref[...] = jnp.zeros_like(dlb_ref)

        lg, lb = lg_ref[...], lb_ref[...]
        cur = cur_ref[...]
        sig = _sigmoid(cur[:, C_CONV:])
        glu[CONV_SUB:CONV_SUB + r, :] = cur[:, :C_CONV] * sig
        pv = prev_ref[...]
        glu[0:CONV_SUB, :] = jnp.where(i > 0, pv[:, :C_CONV] * _sigmoid(pv[:, C_CONV:]), 0.0)

        row = i * r + lax.broadcasted_iota(jnp.int32, (r, 1), 0)
        dyc, dout, yn = ln_bwd(dy_ref[...], yc_ref[...], row >= PAD_ROWS, lg, lb)
        dycs[0:r, :] = dyc
        dycn, _, _ = ln_bwd(dyn_ref[...], ycn_ref[...], i < nt - 1, lg, lb)
        dycs[r:r + CONV_SUB, :] = dycn
        db_ref[...] += jnp.sum(dyc, axis=0, keepdims=True)
        dlg_ref[...] += jnp.sum(dout * yn, axis=0, keepdims=True)
        dlb_ref[...] += jnp.sum(dout, axis=0, keepdims=True)

        _shifted_copies(glu, glu_sh, r)
        _shifted_copies(dycs, dycs_sh, r)
        w = w_ref[...]
        for j in range(nsub):
            r0 = j * CONV_SUB
            dblk = dycs[r0:r0 + CONV_SUB, :]
            dglu = jnp.zeros((CONV_SUB, C_CONV), F32)
            for k in range(CONV_W):
                dglu = dglu + w[k:k + 1, :] * _shifted_rows(dycs, dycs_sh, r0 + (CONV_W - 1) - k)
                prod = dblk * _shifted_rows(glu, glu_sh, r0 + CONV_LEAD + k)
                dwacc[k] += prod.reshape(CONV_SUB // SUBLANES, SUBLANES, C_CONV).sum(axis=0)
            sg = sig[r0:r0 + CONV_SUB, :]
            cv = cur[r0:r0 + CONV_SUB, :C_CONV]
            du_ref[r0:r0 + CONV_SUB, :C_CONV] = (dglu * sg).astype(BF16)
            du_ref[r0:r0 + CONV_SUB, C_CONV:] = (dglu * cv * sg * (1.0 - sg)).astype(BF16)

        @pl.when(jnp.logical_and(b == n_ex - 1, i == nt - 1))
        def _():
            dw_ref[...] = jnp.sum(dwacc[...], axis=1)

    t = n_ex * lp
    cur_rows = lambda w, col: pl.BlockSpec((r, w), lambda b, i: (b * nt + i, col))
    nxt_rows = lambda w, col: pl.BlockSpec(
        (CONV_SUB, w), lambda b, i: (jnp.minimum((b * nt + i + 1) * hb, n_ex * nt * hb - 1), col))
    return _call(
        body, name="conv_bwd", grid=(n_ex, nt),
        in_specs=[cur_rows(C_CONV, 0), nxt_rows(C_CONV, 0), cur_rows(C_CONV, 0), nxt_rows(C_CONV, 0),
                  cur_rows(2 * C_CONV, 0),
                  pl.BlockSpec((CONV_SUB, 2 * C_CONV), lambda b, i: (jnp.maximum((b * nt + i) * hb - 1, 0), 0)),
                  _const_spec((32, C_CONV)), _const_spec((1, C_CONV)), _const_spec((1, C_CONV))],
        out_specs=[cur_rows(2 * C_CONV, 0), _acc_spec((32, C_CONV)), _acc_spec((1, C_CONV)),
                   _acc_spec((1, C_CONV)), _acc_spec((1, C_CONV))],
        out_shape=[jax.ShapeDtypeStruct((t, 2 * C_CONV), BF16), jax.ShapeDtypeStruct((32, C_CONV), F32),
                   jax.ShapeDtypeStruct((1, C_CONV), F32), jax.ShapeDtypeStruct((1, C_CONV), F32),
                   jax.ShapeDtypeStruct((1, C_CONV), F32)],
        scratch_shapes=[pltpu.VMEM((r + CONV_SUB, C_CONV), F32), pltpu.VMEM((r + CONV_SUB, C_CONV), F32),
                        pltpu.VMEM((32, 8, C_CONV), F32),
                        pltpu.VMEM((SUBLANES - 1, r + CONV_SUB - SUBLANES, C_CONV), F32),
                        pltpu.VMEM((SUBLANES - 1, r + CONV_SUB - SUBLANES, C_CONV), F32)],
        plan=plan,
    )(dycat, dycat, yc, yc, u, u, conv_w, ln_g, ln_b)


def _gla_bwd_per_head(dycat, u, states, w2, gb, ng, n_ex, lp, plan=None):
    nc = lp // CHUNK
    t = n_ex * lp

    def body(dy_ref, qk_ref, v_ref, g_ref, lr_ref, st_ref, w2_ref, gb_ref, ng_ref,
             du_ref, dw2_ref, dgb_ref, dng_ref, dstate):
        n = pl.program_id(0)
        chunk = nc - 1 - n

        @pl.when(n == 0)
        def _():
            dw2_ref[...] = jnp.zeros_like(dw2_ref)
            dgb_ref[...] = jnp.zeros_like(dgb_ref)
            dng_ref[...] = jnp.zeros_like(dng_ref)
            dstate[...] = jnp.zeros_like(dstate)

        for e in range(n_ex):
            one_example(e, chunk, dy_ref, qk_ref, v_ref, g_ref, lr_ref, st_ref, w2_ref, gb_ref, ng_ref,
                        du_ref, dw2_ref, dgb_ref, dng_ref, dstate)

    def one_example(e, chunk, dy_ref, qk_ref, v_ref, g_ref, lr_ref, st_ref, w2_ref, gb_ref, ng_ref,
                    du_ref, dw2_ref, dgb_ref, dng_ref, dstate):
        dy_ref, qk_ref, v_ref, g_ref, lr_ref, st_ref = (r.at[e] for r in (dy_ref, qk_ref, v_ref, g_ref, lr_ref, st_ref))
        du_ref, dstate = du_ref.at[e], dstate.at[e]
        qk = qk_ref[...]
        q, k = qk[:, :GLA_K], qk[:, GLA_K:]
        lr = lr_ref[...]
        z, a, live = _gla_gates(lr, w2_ref[...], gb_ref[...], chunk == 0)
        causal = _tri(True)
        b = jnp.dot(causal.astype(F32), a, preferred_element_type=F32, precision=lax.Precision.HIGHEST)
        bl = b[CHUNK - 1:CHUNK, :]
        e_pos, e_neg, e_dec = jnp.exp(b), jnp.exp(-b), jnp.exp(bl - b)
        q_f = q * (DK ** -0.5) * e_pos
        k_f = k * e_neg
        kd_f = k * e_dec
        q_in, k_in, k_dec = q_f.astype(BF16), k_f.astype(BF16), kd_f.astype(BF16)
        decay = jnp.exp(bl)
        v = v_ref[...]
        g = g_ref[...]
        dy = dy_ref[...]
        ngv = ng_ref[...]
        st = st_ref[...]
        st_b = st.astype(BF16)
        dst = dstate[...]
        dst_b = dst.astype(BF16)
        dqs, dks, dvs, dgs, dbs, dbls, new_dst = [], [], [], [], [], [], []
        dng = jnp.zeros((1, DV), F32)
        for h in range(N_HEADS):
            ks = slice(h * DK, (h + 1) * DK)
            vs = slice(h * DV, (h + 1) * DV)
            qh, kh, kdh = q_in[:, ks], k_in[:, ks], k_dec[:, ks]
            vh = v[:, vs].astype(BF16)
            s = jnp.where(causal, _dot_nt(qh, kh), 0.0).astype(BF16)
            o = _dot(s, vh) + _dot_nt(qh, st_b[:, ks])
            rstd = lax.rsqrt(jnp.mean(o * o, axis=-1, keepdims=True) + RMS_EPS)
            nrm = o * rstd
            gh = g[:, vs]
            sg = _sigmoid(gh)
            dyh = dy[:, vs]
            dgs.append(dyh * nrm * ngv * (sg * (1.0 + gh * (1.0 - sg))))
            dt = dyh * (gh * sg)
            dng = dng + jnp.sum(dt * nrm, axis=0, keepdims=True)
            dn = dt * ngv
            do = (rstd * (dn - nrm * jnp.mean(dn * nrm, axis=-1, keepdims=True))).astype(BF16)
            da = jnp.where(causal, _dot_nt(do, vh), 0.0).astype(BF16)
            dvs.append(_dot_tn(s, do) + _dot_nt(kdh, dst_b[:, ks]))
            dq_in = _dot(da, kh) + _dot(do, st_b[:, ks])
            dk_in = _dot_tn(da, qh)
            dk_dec = _dot(vh, dst_b[:, ks])
            new_dst.append(_dot_tn(do, qh) + decay[:, ks] * dst[:, ks])
            dbls.append(jnp.sum(dk_dec * kd_f[:, ks], axis=0, keepdims=True)
                        + decay[:, ks] * jnp.sum(dst[:, ks] * st[:, ks], axis=0, keepdims=True))
            dqs.append(dq_in * (DK ** -0.5) * e_pos[:, ks])
            dks.append(dk_in * e_neg[:, ks] + dk_dec * e_dec[:, ks])
            dbs.append(dq_in * q_f[:, ks] - dk_in * k_f[:, ks] - dk_dec * kd_f[:, ks])
        dstate[...] = jnp.concatenate(new_dst, axis=1)
        row = lax.broadcasted_iota(jnp.int32, (CHUNK, 1), 0)
        db = jnp.concatenate(dbs, axis=1) + jnp.where(row == CHUNK - 1, jnp.concatenate(dbls, axis=1), 0.0)
        da_log = jnp.dot(_tri(False).astype(F32), db, preferred_element_type=F32, precision=lax.Precision.HIGHEST)
        dz = jnp.where(live, da_log * (1.0 - _sigmoid(z)) * (1.0 / GATE_TAU), 0.0)
        dz_b = dz.astype(BF16)
        du_ref[:, 0:GLA_K] = jnp.concatenate(dqs, axis=1).astype(BF16)
        du_ref[:, GLA_K:2 * GLA_K] = jnp.concatenate(dks, axis=1).astype(BF16)
        du_ref[:, 2 * GLA_K:2 * GLA_K + GLA_V] = jnp.concatenate(dvs, axis=1).astype(BF16)
        du_ref[:, 2 * GLA_K + GLA_V:2 * GLA_K + 2 * GLA_V] = jnp.concatenate(dgs, axis=1).astype(BF16)
        du_ref[:, 2 * GLA_K + 2 * GLA_V:] = _dot_nt(dz_b, w2_ref[...]).astype(BF16)
        dw2_ref[...] += _dot_tn(lr.astype(BF16), dz_b)
        dgb_ref[...] += jnp.sum(dz, axis=0, keepdims=True)
        dng_ref[...] += dng

    u3 = u.reshape(n_ex, lp, D_IN_PAD)
    rev = lambda w, col: pl.BlockSpec((n_ex, CHUNK, w), lambda n: (0, nc - 1 - n, col))
    (du, d_w2, d_gb, d_ng), extra = _call(
        body, name="gla_bwd", grid=(nc,),
        in_specs=[rev(GLA_V, 1), rev(2 * GLA_K, 2), rev(GLA_V, 3), rev(GLA_V, 4), rev(128, 20),
                  pl.BlockSpec((n_ex, DV, GLA_K), lambda n: (0, nc - 1 - n, 0)),
                  _const_spec((128, GLA_K)), _const_spec((1, GLA_K)), _const_spec((1, DV))],
        out_specs=[rev(D_GLA_IN, 0), _acc_spec((128, GLA_K)), _acc_spec((1, GLA_K)), _acc_spec((1, DV))],
        out_shape=[jax.ShapeDtypeStruct((n_ex, lp, D_GLA_IN), BF16), jax.ShapeDtypeStruct((128, GLA_K), F32),
                   jax.ShapeDtypeStruct((1, GLA_K), F32), jax.ShapeDtypeStruct((1, DV), F32)],
        scratch_shapes=[pltpu.VMEM((n_ex, DV, GLA_K), F32)],
        plan=plan,
    )(dycat.reshape(n_ex, lp, D), u3, u3, u3, u3, states, w2, gb, ng)
    return (du.reshape(t, D_GLA_IN), d_w2, d_gb, d_ng), extra


HEAD_ROWS_ALL = N_HEADS * CHUNK


def _head_of(shape, axis, per_head):
    return lax.broadcasted_iota(jnp.int32, shape, axis) // per_head


def _expand(x, lanes_per_head):
    rows, lanes = HEAD_ROWS_ALL, x.shape[1]
    keep = _head_of((rows, lanes), 0, CHUNK) == _head_of((rows, lanes), 1, lanes_per_head)
    return jnp.where(keep, jnp.tile(x, (N_HEADS, 1)), 0.0)


def _expand_lanes(x):
    rows, w = x.shape
    keep = _head_of((rows, N_HEADS * w), 0, CHUNK) == _head_of((rows, N_HEADS * w), 1, w)
    return jnp.where(keep, jnp.tile(x, (1, N_HEADS)), 0.0)


def _expand_state(st):
    rows, lanes = N_HEADS * DV, st.shape[1]
    keep = _head_of((rows, lanes), 0, DV) == _head_of((rows, lanes), 1, DK)
    return jnp.where(keep, jnp.tile(st, (N_HEADS, 1)), 0.0)


def _fold(t, rows_per_head):
    lane_head = _head_of((rows_per_head, t.shape[1]), 1, DK)
    out = jnp.where(lane_head == 0, t[0:rows_per_head], 0.0)
    for h in range(1, N_HEADS):
        out = out + jnp.where(lane_head == h, t[h * rows_per_head:(h + 1) * rows_per_head], 0.0)
    return out


def _rows_by_head(x):
    return jnp.concatenate([x[:, h * DV:(h + 1) * DV] for h in range(N_HEADS)], axis=0)


def _lanes_by_head(x):
    return jnp.concatenate([x[h * CHUNK:(h + 1) * CHUNK] for h in range(N_HEADS)], axis=1)


def _running_sum(a, lower):
    hi = a.astype(BF16)
    rest = a - hi.astype(F32)
    mid = rest.astype(BF16)
    lo = (rest - mid.astype(F32)).astype(BF16)
    w = a.shape[1]
    parts = _dot(_tri(lower).astype(F32).astype(BF16), jnp.concatenate([hi, mid, lo], axis=1))
    return parts[:, :w] + parts[:, w:2 * w] + parts[:, 2 * w:]


def _stacked_causal():
    i = lax.broadcasted_iota(jnp.int32, (HEAD_ROWS_ALL, CHUNK), 0) % CHUNK
    j = lax.broadcasted_iota(jnp.int32, (HEAD_ROWS_ALL, CHUNK), 1)
    return i >= j


def _gla_chunk(q, k, v, lr, st, w2, gb, first_chunk):
    z, a, live = _gla_gates(lr, w2, gb, first_chunk)
    yield
    b = _running_sum(a, True)
    yield
    bl = b[CHUNK - 1:CHUNK, :]
    e_pos, e_neg, e_dec = jnp.exp(b), jnp.exp(-b), jnp.exp(bl - b)
    q_f, k_f, kd_f = q * (DK ** -0.5) * e_pos, k * e_neg, k * e_dec
    qx = _expand(q_f, DK).astype(BF16)
    k_in, k_dec, v_b = k_f.astype(BF16), kd_f.astype(BF16), v.astype(BF16)
    s = jnp.where(_stacked_causal(), _dot_nt(qx, k_in), 0.0).astype(BF16)
    o_inter = _dot_nt(qx, st.astype(BF16))
    yield
    p = _dot(s, v_b)
    yield
    o = jnp.concatenate([p[h * CHUNK:(h + 1) * CHUNK, h * DV:(h + 1) * DV] for h in range(N_HEADS)], axis=0) + o_inter
    return dict(z=z, live=live, bl=bl, e_pos=e_pos, e_neg=e_neg, e_dec=e_dec, q_f=q_f, k_f=k_f, kd_f=kd_f,
                qx=qx, k_in=k_in, k_dec=k_dec, v_b=v_b, s=s, o=o, decay=jnp.exp(bl))


def _gla_fwd(u, w2, gb, ng, n_ex, lp, plan=None):
    nc = lp // CHUNK
    t = n_ex * lp

    def body(qk_ref, v_ref, g_ref, lr_ref, w2_ref, gb_ref, ng_ref, y_ref, st_ref, state):
        n = pl.program_id(0)

        @pl.when(n == 0)
        def _():
            state[...] = jnp.zeros_like(state)

        def one_example(e):
            st = state[e]
            st_ref[e] = st
            qk = qk_ref[e]
            c = yield from _gla_chunk(qk[:, :GLA_K], qk[:, GLA_K:], v_ref[e], lr_ref[e], st, w2_ref[...],
                                      gb_ref[...], n == 0)
            o = c["o"]
            rstd = lax.rsqrt(jnp.mean(o * o, axis=-1, keepdims=True) + RMS_EPS)
            g = _rows_by_head(g_ref[e])
            y_ref[e] = _lanes_by_head(o * rstd * ng_ref[...] * (g * _sigmoid(g))).astype(BF16)
            state[e] = c["decay"] * st + _fold(_dot_tn(c["v_b"], c["k_dec"]), DV)

        _in_lockstep(one_example(e) for e in range(n_ex))

    u3 = u.reshape(n_ex, lp, D_IN_PAD)
    blk = lambda w, col: pl.BlockSpec((n_ex, CHUNK, w), lambda n: (0, n, col))
    (y, states), extra = _call(
        body, name="gla_fwd", grid=(nc,),
        in_specs=[blk(2 * GLA_K, 2), blk(GLA_V, 3), blk(GLA_V, 4), blk(128, 20),
                  _const_spec((128, GLA_K)), _const_spec((1, GLA_K)), _const_spec((1, DV))],
        out_specs=[blk(GLA_V, 0), pl.BlockSpec((n_ex, DV, GLA_K), lambda n: (0, n, 0))],
        out_shape=[jax.ShapeDtypeStruct((n_ex, lp, GLA_V), BF16),
                   jax.ShapeDtypeStruct((n_ex, nc * DV, GLA_K), F32)],
        scratch_shapes=[pltpu.VMEM((n_ex, DV, GLA_K), F32)],
        plan=plan,
    )(u3, u3, u3, u3, w2, gb, ng)
    return (y.reshape(t, GLA_V), states), extra


def _gla_bwd(dycat, u, states, w2, gb, ng, n_ex, lp, plan=None):
    nc = lp // CHUNK
    t = n_ex * lp

    def body(dy_ref, qk_ref, v_ref, g_ref, lr_ref, st_ref, w2_ref, gb_ref, ng_ref,
             du_ref, dw2_ref, dgb_ref, dng_ref, dstate):
        n = pl.program_id(0)
        chunk = nc - 1 - n

        @pl.when(n == 0)
        def _():
            dw2_ref[...] = jnp.zeros_like(dw2_ref)
            dgb_ref[...] = jnp.zeros_like(dgb_ref)
            dng_ref[...] = jnp.zeros_like(dng_ref)
            dstate[...] = jnp.zeros_like(dstate)

        def one_example(e):
            qk = qk_ref[e]
            lr = lr_ref[e]
            st = st_ref[e]
            dst = dstate[e]
            c = yield from _gla_chunk(qk[:, :GLA_K], qk[:, GLA_K:], v_ref[e], lr, st, w2_ref[...], gb_ref[...],
                                      chunk == 0)
            qx, k_in, k_dec, v_b, s, o = c["qx"], c["k_in"], c["k_dec"], c["v_b"], c["s"], c["o"]
            ngv = ng_ref[...]
            rstd = lax.rsqrt(jnp.mean(o * o, axis=-1, keepdims=True) + RMS_EPS)
            nrm = o * rstd
            g = _rows_by_head(g_ref[e])
            dy = _rows_by_head(dy_ref[e])
            sg = _sigmoid(g)
            dg = dy * nrm * ngv * (sg * (1.0 + g * (1.0 - sg)))
            dt = dy * (g * sg)
            dng_ref[...] += jnp.sum(dt * nrm, axis=0, keepdims=True)
            dn = dt * ngv
            do = rstd * (dn - nrm * jnp.mean(dn * nrm, axis=-1, keepdims=True))
            do_b = do.astype(BF16)
            dox = _expand_lanes(do).astype(BF16)
            dstx = _expand_state(dst).astype(BF16)
            yield
            da = jnp.where(_stacked_causal(), _dot_nt(dox, v_b), 0.0).astype(BF16)
            dv = _dot_tn(s, dox) + _dot_nt(k_dec, dstx)
            dk_dec = _dot(v_b, dstx)
            dstate[e] = _dot_tn(do_b, qx) + c["decay"] * dst
            yield
            dq_in = _fold(_dot(da, k_in) + _dot(do_b, st.astype(BF16)), CHUNK)
            dk_in = _dot_tn(da, qx)
            yield
            dbl = (jnp.sum(dk_dec * c["kd_f"], axis=0, keepdims=True)
                   + c["decay"] * jnp.sum(dst * st, axis=0, keepdims=True))
            dq = dq_in * (DK ** -0.5) * c["e_pos"]
            dk = dk_in * c["e_neg"] + dk_dec * c["e_dec"]
            db = dq_in * c["q_f"] - dk_in * c["k_f"] - dk_dec * c["kd_f"]
            row = lax.broadcasted_iota(jnp.int32, (CHUNK, 1), 0)
            da_log = _running_sum(db + jnp.where(row == CHUNK - 1, dbl, 0.0), False)
            yield
            dz = jnp.where(c["live"], da_log * (1.0 - _sigmoid(c["z"])) * (1.0 / GATE_TAU), 0.0)
            dz_b = dz.astype(BF16)
            out = du_ref.at[e]
            out[:, 0:GLA_K] = dq.astype(BF16)
            out[:, GLA_K:2 * GLA_K] = dk.astype(BF16)
            out[:, 2 * GLA_K:2 * GLA_K + GLA_V] = dv.astype(BF16)
            out[:, 2 * GLA_K + GLA_V:2 * GLA_K + 2 * GLA_V] = _lanes_by_head(dg).astype(BF16)
            out[:, 2 * GLA_K + 2 * GLA_V:] = _dot_nt(dz_b, w2_ref[...]).astype(BF16)
            dw2_ref[...] += _dot_tn(lr.astype(BF16), dz_b)
            dgb_ref[...] += jnp.sum(dz, axis=0, keepdims=True)

        _in_lockstep(one_example(e) for e in range(n_ex))

    u3 = u.reshape(n_ex, lp, D_IN_PAD)
    rev = lambda w, col: pl.BlockSpec((n_ex, CHUNK, w), lambda n: (0, nc - 1 - n, col))
    (du, d_w2, d_gb, d_ng), extra = _call(
        body, name="gla_bwd", grid=(nc,),
        in_specs=[rev(GLA_V, 1), rev(2 * GLA_K, 2), rev(GLA_V, 3), rev(GLA_V, 4), rev(128, 20),
                  pl.BlockSpec((n_ex, DV, GLA_K), lambda n: (0, nc - 1 - n, 0)),
                  _const_spec((128, GLA_K)), _const_spec((1, GLA_K)), _const_spec((1, DV))],
        out_specs=[rev(D_GLA_IN, 0), _acc_spec((128, GLA_K)), _acc_spec((1, GLA_K)), _acc_spec((1, DV))],
        out_shape=[jax.ShapeDtypeStruct((n_ex, lp, D_GLA_IN), BF16), jax.ShapeDtypeStruct((128, GLA_K), F32),
                   jax.ShapeDtypeStruct((1, GLA_K), F32), jax.ShapeDtypeStruct((1, DV), F32)],
        scratch_shapes=[pltpu.VMEM((n_ex, DV, GLA_K), F32)],
        plan=plan,
    )(dycat.reshape(n_ex, lp, D), u3, u3, u3, u3, states, w2, gb, ng)
    return (du.reshape(t, D_GLA_IN), d_w2, d_gb, d_ng), extra


def _in_proj_bwd(du_conv, du_gla, w_in_t_conv, w_in_t_gla, h0, dh1, g_mix, plan=None):
    t = h0.shape[0]
    r = _row_tile(t, 384)

    def body(dc_ref, dg_ref, wc_ref, wg_ref, h_ref, dh1_ref, g_ref, dh0_ref, dgm_ref):
        @pl.when(pl.program_id(0) == 0)
        def _():
            dgm_ref[...] = jnp.zeros_like(dgm_ref)

        dhn = _dot(dc_ref[...], wc_ref[...]) + _dot(dg_ref[...], wg_ref[...])
        h = h_ref[...]
        rstd = lax.rsqrt(jnp.mean(h * h, axis=-1, keepdims=True) + RMS_EPS)
        nrm = h * rstd
        dgm_ref[...] += jnp.sum(dhn * nrm, axis=0, keepdims=True)
        dn = dhn * g_ref[...]
        dh0_ref[...] = dh1_ref[...] + rstd * (dn - nrm * jnp.mean(dn * nrm, axis=-1, keepdims=True))

    rows = lambda w: pl.BlockSpec((r, w), lambda i: (i, 0))
    return _call(
        body, name="in_proj_bwd", grid=(t // r,),
        in_specs=[rows(2 * C_CONV), rows(D_GLA_IN), _const_spec((2 * C_CONV, D)), _const_spec((D_GLA_IN, D)),
                  rows(D), rows(D), _const_spec((1, D))],
        out_specs=[rows(D), _acc_spec((1, D))],
        out_shape=[jax.ShapeDtypeStruct((t, D), F32), jax.ShapeDtypeStruct((1, D), F32)],
        plan=plan,
    )(du_conv, du_gla, w_in_t_conv, w_in_t_gla, h0, dh1, g_mix)


def _wgrad_hosting(x, dy, name, plan):
    t, m = x.shape
    n = dy.shape[1]
    tk = t // 3 if t % (3 * 128) == 0 else _row_tile(t, 384)
    tm = m if m <= D_GLA_IN else m // 2

    def body(x_ref, dy_ref, o_ref):
        @pl.when(pl.program_id(2) == 0)
        def _():
            o_ref[...] = jnp.zeros_like(o_ref)

        o_ref[...] += _dot_tn(x_ref[...].astype(BF16), dy_ref[...].astype(BF16))

    (out,), extra = _call(
        body, name=name, grid=(m // tm, 1, t // tk),
        in_specs=[pl.BlockSpec((tk, tm), lambda i, j, k: (k, i)), pl.BlockSpec((tk, n), lambda i, j, k: (k, j))],
        out_specs=[pl.BlockSpec((tm, n), lambda i, j, k: (i, j))],
        out_shape=[jax.ShapeDtypeStruct((m, n), F32)],
        plan=plan,
    )(x, dy)
    return out, extra


def _wgrad(x, dy, name):
    t, m = x.shape
    n = dy.shape[1]
    tk = t // 3 if t % (3 * 128) == 0 else _row_tile(t, 384)
    tm = m if m <= D_GLA_IN else m // 2
    tn = n

    def body(x_ref, dy_ref, o_ref):
        @pl.when(pl.program_id(2) == 0)
        def _():
            o_ref[...] = jnp.zeros_like(o_ref)

        o_ref[...] += _dot_tn(x_ref[...].astype(BF16), dy_ref[...].astype(BF16))

    return pl.pallas_call(
        body, name=name, grid=(m // tm, n // tn, t // tk),
        in_specs=[pl.BlockSpec((tk, tm), lambda i, j, k: (k, i)), pl.BlockSpec((tk, tn), lambda i, j, k: (k, j))],
        out_specs=pl.BlockSpec((tm, tn), lambda i, j, k: (i, j)),
        out_shape=jax.ShapeDtypeStruct((m, n), F32),
        compiler_params=_params(3),
    )(x, dy)


def _mesh_pos():
    return lax.axis_index("x"), lax.axis_index("y"), lax.axis_index("c")


def _other_chips(x, y):
    return [(1 - x, y), (x, 1 - y), (1 - x, 1 - y)]


HBM_SPEC = pl.BlockSpec(memory_space=pltpu.HBM)


def _gather_shards(shards):
    n = len(shards)

    def body(*refs):
        ins, outs = refs[:n], refs[n:2 * n]
        send_sems, recv_sems, local_sems = refs[2 * n:]
        x, y, c = _mesh_pos()
        mine = 2 * x + y
        chips = _other_chips(x, y)
        local = [pltpu.make_async_copy(ins[a], outs[a].at[mine], local_sems.at[a]) for a in range(n)]
        for cp in local:
            cp.start()

        def remote(a, k, block):
            px, py = chips[k]
            return pltpu.make_async_remote_copy(
                src_ref=ins[a], dst_ref=outs[a].at[block], send_sem=send_sems.at[3 * a + k],
                recv_sem=recv_sems.at[3 * a + k], device_id=(px, py, c), device_id_type=MESH)

        sends = [remote(a, k, mine) for a in range(n) for k in range(3)]
        for cp in sends:
            cp.start()
        for a in range(n):
            for k, (px, py) in enumerate(chips):
                remote(a, k, 2 * px + py).wait_recv()
        for cp in sends:
            cp.wait_send()
        for cp in local:
            cp.wait()

    return pl.pallas_call(
        body, name="gather_shards",
        in_specs=[HBM_SPEC] * n, out_specs=[HBM_SPEC] * n,
        out_shape=[jax.ShapeDtypeStruct((N_CHIPS,) + s.shape, s.dtype) for s in shards],
        scratch_shapes=[pltpu.SemaphoreType.DMA((3 * n,)), pltpu.SemaphoreType.DMA((3 * n,)),
                        pltpu.SemaphoreType.DMA((n,))],
        compiler_params=pltpu.CompilerParams(has_side_effects=True),
    )(*shards)


def _send_half_to_sibling(g2):
    def body(g_ref, recv_ref, send_sem, recv_sem):
        x, y, c = _mesh_pos()
        cp = pltpu.make_async_remote_copy(
            src_ref=g_ref.at[1 - c], dst_ref=recv_ref, send_sem=send_sem, recv_sem=recv_sem,
            device_id=(x, y, 1 - c), device_id_type=MESH)
        cp.start()
        cp.wait()

    return pl.pallas_call(
        body, name="rs_to_sibling", in_specs=[HBM_SPEC], out_specs=HBM_SPEC,
        out_shape=jax.ShapeDtypeStruct(g2.shape[1:], g2.dtype),
        scratch_shapes=[pltpu.SemaphoreType.DMA(()), pltpu.SemaphoreType.DMA(())],
        compiler_params=pltpu.CompilerParams(has_side_effects=True),
    )(g2)


def _add_own_half(g2, recv, c):
    rows = N_CHIPS * HALF_ROWS
    tr = 512
    g2f = g2.reshape(2, rows, D)
    recvf = recv.reshape(rows, D)

    def body(c_ref, a_ref, b_ref, o_ref):
        o_ref[...] = a_ref[0] + b_ref[...]

    out = pl.pallas_call(
        body, name="rs_add_halves",
        grid_spec=pltpu.PrefetchScalarGridSpec(
            num_scalar_prefetch=1, grid=(rows // tr,),
            in_specs=[pl.BlockSpec((1, tr, D), lambda i, s: (s[0], i, 0)), pl.BlockSpec((tr, D), lambda i, s: (i, 0))],
            out_specs=pl.BlockSpec((tr, D), lambda i, s: (i, 0))),
        out_shape=jax.ShapeDtypeStruct((rows, D), F32),
        compiler_params=_params(1),
    )(jnp.reshape(c, (1,)).astype(jnp.int32), g2f, recvf)
    return out.reshape(N_CHIPS, HALF_ROWS, D)


def _exchange_chip_sums(p):
    def body(p_ref, out_ref, send_sems, recv_sems, local_sem):
        x, y, c = _mesh_pos()
        mine = 2 * x + y
        chips = _other_chips(x, y)
        local = pltpu.make_async_copy(p_ref.at[mine], out_ref.at[mine], local_sem)
        local.start()

        def remote(k, src_block, dst_block):
            px, py = chips[k]
            return pltpu.make_async_remote_copy(
                src_ref=p_ref.at[src_block], dst_ref=out_ref.at[dst_block], send_sem=send_sems.at[k],
                recv_sem=recv_sems.at[k], device_id=(px, py, c), device_id_type=MESH)

        sends = [remote(k, 2 * px + py, mine) for k, (px, py) in enumerate(chips)]
        for cp in sends:
            cp.start()
        for k, (px, py) in enumerate(chips):
            remote(k, mine, 2 * px + py).wait_recv()
        for cp in sends:
            cp.wait_send()
        local.wait()

    return pl.pallas_call(
        body, name="rs_chip_exchange", in_specs=[HBM_SPEC], out_specs=HBM_SPEC,
        out_shape=jax.ShapeDtypeStruct(p.shape, p.dtype),
        scratch_shapes=[pltpu.SemaphoreType.DMA((3,)), pltpu.SemaphoreType.DMA((3,)), pltpu.SemaphoreType.DMA(())],
        compiler_params=pltpu.CompilerParams(has_side_effects=True),
    )(p)


def _sum_chips(parts):
    tr = 512

    def body(p_ref, o_ref):
        o_ref[...] = ((p_ref[0] + p_ref[1]) + p_ref[2]) + p_ref[3]

    return pl.pallas_call(
        body, name="rs_sum_chips", grid=(HALF_ROWS // tr,),
        in_specs=[pl.BlockSpec((N_CHIPS, tr, D), lambda i: (0, i, 0))],
        out_specs=pl.BlockSpec((tr, D), lambda i: (i, 0)),
        out_shape=jax.ShapeDtypeStruct((HALF_ROWS, D), F32),
        compiler_params=_params(1),
    )(parts)


def _share_with_sibling(half):
    def body(h_ref, out_ref, send_sem, recv_sem, local_sem):
        x, y, c = _mesh_pos()
        local = pltpu.make_async_copy(h_ref, out_ref.at[c], local_sem)
        local.start()
        cp = pltpu.make_async_remote_copy(
            src_ref=h_ref, dst_ref=out_ref.at[c], send_sem=send_sem, recv_sem=recv_sem,
            device_id=(x, y, 1 - c), device_id_type=MESH)
        cp.start()
        pltpu.make_async_remote_copy(
            src_ref=h_ref, dst_ref=out_ref.at[1 - c], send_sem=send_sem, recv_sem=recv_sem,
            device_id=(x, y, 1 - c), device_id_type=MESH).wait_recv()
        cp.wait_send()
        local.wait()

    return pl.pallas_call(
        body, name="rs_share_sibling", in_specs=[HBM_SPEC], out_specs=HBM_SPEC,
        out_shape=jax.ShapeDtypeStruct((2,) + half.shape, half.dtype),
        scratch_shapes=[pltpu.SemaphoreType.DMA(()), pltpu.SemaphoreType.DMA(()), pltpu.SemaphoreType.DMA(())],
        compiler_params=pltpu.CompilerParams(has_side_effects=True),
    )(half)


def _adam_update(g, w, m, v):
    m2 = ADAM_B1 * m + (1.0 - ADAM_B1) * g
    v2 = ADAM_B2 * v + (1.0 - ADAM_B2) * (g * g)
    m_hat = m2 / (1.0 - ADAM_B1 ** ADAM_STEP)
    v_hat = v2 / (1.0 - ADAM_B2 ** ADAM_STEP)
    delta = -ADAM_LR * (m_hat / (jnp.sqrt(v_hat) + ADAM_EPS) + ADAM_WD * w)
    return delta, m2, v2


def _adamw_slab(g, w, m, v):
    rows = g.shape[0]
    tr = 256

    def body(g_ref, w_ref, m_ref, v_ref, d_ref, m2_ref, v2_ref):
        d_ref[...], m2_ref[...], v2_ref[...] = _adam_update(g_ref[...], w_ref[...], m_ref[...], v_ref[...])

    spec = pl.BlockSpec((tr, D), lambda i: (i, 0))
    return pl.pallas_call(
        body, name="adamw_slab", grid=(rows // tr,), in_specs=[spec] * 4, out_specs=[spec] * 3,
        out_shape=[jax.ShapeDtypeStruct((rows, D), F32)] * 3,
        compiler_params=_params(1),
    )(g, w, m, v)


def _allreduce_small_adamw(part, w, m, v):
    def body(p_ref, w_ref, m_ref, v_ref, g_ref, d_ref, m2_ref, v2_ref, slots, send_sems, recv_sems):
        x, y, c = _mesh_pos()
        mine = 4 * x + 2 * y + c
        peers = [(px, py, pc) for px in (x, 1 - x) for py in (y, 1 - y) for pc in (c, 1 - c)][1:]

        def remote(k, slot):
            return pltpu.make_async_remote_copy(
                src_ref=p_ref, dst_ref=slots.at[slot], send_sem=send_sems.at[k], recv_sem=recv_sems.at[k],
                device_id=peers[k], device_id_type=MESH)

        sends = [remote(k, mine) for k in range(7)]
        for cp in sends:
            cp.start()
        slots[mine] = p_ref[...]
        for k, (px, py, pc) in enumerate(peers):
            remote(k, 4 * px + 2 * py + pc).wait_recv()
        for cp in sends:
            cp.wait_send()
        g = slots[0]
        for d in range(1, 8):
            g = g + slots[d]
        g_ref[...] = g
        d_ref[...], m2_ref[...], v2_ref[...] = _adam_update(g, w_ref[...], m_ref[...], v_ref[...])

    vm = pl.BlockSpec(memory_space=pltpu.VMEM)
    shape = jax.ShapeDtypeStruct(part.shape, F32)
    return pl.pallas_call(
        body, name="small_allreduce_adamw", in_specs=[vm] * 4, out_specs=[vm] * 4, out_shape=[shape] * 4,
        scratch_shapes=[pltpu.VMEM((8,) + part.shape, F32), pltpu.SemaphoreType.DMA((7,)),
                        pltpu.SemaphoreType.DMA((7,))],
        compiler_params=pltpu.CompilerParams(has_side_effects=True),
    )(part, w, m, v)


def _half(ref, c, axis):
    n = ref.shape[axis] // 2
    return ref.at[(slice(None),) * axis + (pl.ds(c * n, n),)]


def _remote(src, dst, send_sem, recv_sem, device):
    return pltpu.make_async_remote_copy(src_ref=src, dst_ref=dst, send_sem=send_sem, recv_sem=recv_sem,
                                        device_id=device, device_id_type=MESH)


def _gather_weights(split, axes, whole):
    ns, n = len(split), len(split) + len(whole)

    def body(*refs):
        ins, outs = refs[:n], refs[n:2 * n]
        ici_send, ici_recv, d2d_send, d2d_recv, local_sems = refs[2 * n:]
        x, y, c = _mesh_pos()
        mine = 2 * x + y
        chips = _other_chips(x, y)
        local = [pltpu.make_async_copy(ins[a], outs[a].at[mine], local_sems.at[a]) for a in range(n)]
        for cp in local:
            cp.start()

        def ici(a, k, block):
            px, py = chips[k]
            src, dst = ins[a], outs[a].at[block]
            if a < ns:
                src, dst = _half(src, c, axes[a]), _half(dst, c, axes[a])
            return _remote(src, dst, ici_send.at[3 * a + k], ici_recv.at[3 * a + k], (px, py, c))

        def d2d(a, k, block, half):
            part = _half(outs[a].at[block], half, axes[a])
            return _remote(part, part, d2d_send.at[3 * a + k], d2d_recv.at[3 * a + k], (x, y, 1 - c))

        sends = [ici(a, k, mine) for a in range(n) for k in range(3)]
        for cp in sends:
            cp.start()
        for a in range(n):
            for k, (px, py) in enumerate(chips):
                ici(a, k, 2 * px + py).wait_recv()
                if a < ns:
                    sends.append(d2d(a, k, 2 * px + py, c))
                    sends[-1].start()
        for a in range(ns):
            for k, (px, py) in enumerate(chips):
                d2d(a, k, 2 * px + py, 1 - c).wait_recv()
        for cp in sends:
            cp.wait_send()
        for cp in local:
            cp.wait()

    arrays = list(split) + list(whole)
    return pl.pallas_call(
        body, name="gather_weights", in_specs=[HBM_SPEC] * n, out_specs=[HBM_SPEC] * n,
        out_shape=[jax.ShapeDtypeStruct((N_CHIPS,) + s.shape, s.dtype) for s in arrays],
        scratch_shapes=[pltpu.SemaphoreType.DMA((3 * n,)), pltpu.SemaphoreType.DMA((3 * n,)),
                        pltpu.SemaphoreType.DMA((3 * ns,)), pltpu.SemaphoreType.DMA((3 * ns,)),
                        pltpu.SemaphoreType.DMA((n,))],
        compiler_params=pltpu.CompilerParams(has_side_effects=True),
    )(*arrays)


def _rs_to_sibling(gs):
    n = len(gs)

    def body(*refs):
        ins, outs, send_sems, recv_sems = refs[:n], refs[n:2 * n], refs[2 * n], refs[2 * n + 1]
        x, y, c = _mesh_pos()
        copies = [_remote(_half(ins[a], 1 - c, 2), outs[a], send_sems.at[a], recv_sems.at[a], (x, y, 1 - c))
                  for a in range(n)]
        for cp in copies:
            cp.start()
        for cp in copies:
            cp.wait()

    return pl.pallas_call(
        body, name="rs_to_sibling", in_specs=[HBM_SPEC] * n, out_specs=[HBM_SPEC] * n,
        out_shape=[jax.ShapeDtypeStruct(g.shape[:2] + (g.shape[2] // 2,), g.dtype) for g in gs],
        scratch_shapes=[pltpu.SemaphoreType.DMA((n,)), pltpu.SemaphoreType.DMA((n,))],
        compiler_params=pltpu.CompilerParams(has_side_effects=True),
    )(*gs)


def _rs_add_halves(g, recv, c, name):
    _, rows, w = g.shape
    h = w // 2
    tr = rows // 2 if rows % 16 == 0 and rows > 64 else rows

    def body(c_ref, a_ref, b_ref, o_ref):
        o_ref[...] = (a_ref[...] + b_ref[...]).astype(BF16)

    return pl.pallas_call(
        body, name=name,
        grid_spec=pltpu.PrefetchScalarGridSpec(
            num_scalar_prefetch=1, grid=(N_CHIPS, rows // tr),
            in_specs=[pl.BlockSpec((1, tr, h), lambda j, i, s: (j, i, s[0])),
                      pl.BlockSpec((1, tr, h), lambda j, i, s: (j, i, 0))],
            out_specs=pl.BlockSpec((1, tr, h), lambda j, i, s: (j, i, 0))),
        out_shape=jax.ShapeDtypeStruct((N_CHIPS, rows, h), BF16),
        compiler_params=_params(2),
    )(jnp.reshape(c, (1,)).astype(jnp.int32), g, recv)


def _rs_chip_exchange(ps):
    n = len(ps)

    def body(*refs):
        ins, outs = refs[:n], refs[n:2 * n]
        send_sems, recv_sems, local_sems = refs[2 * n:]
        x, y, c = _mesh_pos()
        mine = 2 * x + y
        chips = _other_chips(x, y)
        local = [pltpu.make_async_copy(ins[a].at[mine], outs[a].at[mine], local_sems.at[a]) for a in range(n)]
        for cp in local:
            cp.start()

        def ici(a, k, src_block, dst_block):
            px, py = chips[k]
            return _remote(ins[a].at[src_block], outs[a].at[dst_block], send_sems.at[3 * a + k],
                           recv_sems.at[3 * a + k], (px, py, c))

        sends = [ici(a, k, 2 * px + py, mine) for a in range(n) for k, (px, py) in enumerate(chips)]
        for cp in sends:
            cp.start()
        for a in range(n):
            for k, (px, py) in enumerate(chips):
                ici(a, k, mine, 2 * px + py).wait_recv()
        for cp in sends:
            cp.wait_send()
        for cp in local:
            cp.wait()

    return pl.pallas_call(
        body, name="rs_chip_exchange", in_specs=[HBM_SPEC] * n, out_specs=[HBM_SPEC] * n,
        out_shape=[jax.ShapeDtypeStruct(p.shape, p.dtype) for p in ps],
        scratch_shapes=[pltpu.SemaphoreType.DMA((3 * n,)), pltpu.SemaphoreType.DMA((3 * n,)),
                        pltpu.SemaphoreType.DMA((n,))],
        compiler_params=pltpu.CompilerParams(has_side_effects=True),
    )(*ps)


def _rs_sum_chips(parts, name):
    _, rows, h = parts.shape
    tr = rows // 2 if rows % 16 == 0 and rows > 64 else rows

    def body(p_ref, o_ref):
        p = p_ref[...].astype(F32)
        o_ref[...] = ((p[0] + p[1]) + p[2]) + p[3]

    return pl.pallas_call(
        body, name=name, grid=(rows // tr,),
        in_specs=[pl.BlockSpec((N_CHIPS, tr, h), lambda i: (0, i, 0))],
        out_specs=pl.BlockSpec((tr, h), lambda i: (i, 0)),
        out_shape=jax.ShapeDtypeStruct((rows, h), F32),
        compiler_params=_params(1),
    )(parts)


def _rs_share(halves):
    n = len(halves)

    def body(*refs):
        ins, outs = refs[:n], refs[n:2 * n]
        send_sems, recv_sems, local_sems = refs[2 * n:]
        x, y, c = _mesh_pos()
        local = [pltpu.make_async_copy(ins[a], _half(outs[a], c, 1), local_sems.at[a]) for a in range(n)]
        for cp in local:
            cp.start()
        sends = [_remote(ins[a], _half(outs[a], c, 1), send_sems.at[a], recv_sems.at[a], (x, y, 1 - c))
                 for a in range(n)]
        for cp in sends:
            cp.start()
        for a in range(n):
            _remote(ins[a], _half(outs[a], 1 - c, 1), send_sems.at[a], recv_sems.at[a], (x, y, 1 - c)).wait_recv()
        for cp in sends:
            cp.wait_send()
        for cp in local:
            cp.wait()

    return pl.pallas_call(
        body, name="rs_share", in_specs=[HBM_SPEC] * n, out_specs=[HBM_SPEC] * n,
        out_shape=[jax.ShapeDtypeStruct((p.shape[0], 2 * p.shape[1]), p.dtype) for p in halves],
        scratch_shapes=[pltpu.SemaphoreType.DMA((n,)), pltpu.SemaphoreType.DMA((n,)),
                        pltpu.SemaphoreType.DMA((n,))],
        compiler_params=pltpu.CompilerParams(has_side_effects=True),
    )(*halves)


def _adamw(g, w, m, v, name):
    rows, cols = g.shape
    tr = 256 if rows % 256 == 0 else (rows // 2 if rows % 16 == 0 and rows > 64 else rows)

    def body(g_ref, w_ref, m_ref, v_ref, d_ref, m2_ref, v2_ref):
        d_ref[...], m2_ref[...], v2_ref[...] = _adam_update(g_ref[...], w_ref[...], m_ref[...], v_ref[...])

    spec = pl.BlockSpec((tr, cols), lambda i: (i, 0))
    return pl.pallas_call(
        body, name=name, grid=(rows // tr,), in_specs=[spec] * 4, out_specs=[spec] * 3,
        out_shape=[jax.ShapeDtypeStruct((rows, cols), F32)] * 3,
        compiler_params=_params(1),
    )(g, w, m, v)


def _rows_of(a):
    flat = a.reshape(-1)
    pad = (-flat.shape[0]) % D
    if pad:
        flat = jnp.concatenate([flat, jnp.zeros((pad,), flat.dtype)])
    return flat.reshape(-1, D)


SLAB_PARTS = (("w_in", (D, D_IN // N_CHIPS)), ("w_out", (D // N_CHIPS, D)), ("w_ffn_gate", (D, D_FF // N_CHIPS)),
              ("w_ffn_up", (D, D_FF // N_CHIPS)), ("w_ffn_down", (D_FF // N_CHIPS, D)),
              ("meta_tokens", (N_META, D // N_CHIPS)), ("conv_w", (CONV_W, C_CONV // N_CHIPS)),
              ("gla_w_gate2", (RANK, GLA_K // N_CHIPS)))


def _pack_slab(parts):
    rows = [_rows_of(parts[name].reshape(shape)) for name, shape in SLAB_PARTS]
    used = sum(r.shape[0] for r in rows)
    rows.append(jnp.zeros((SLAB_ROWS - used, D), F32))
    return jnp.concatenate(rows, axis=0)


def _unpack_slab(slab, lead):
    out, r0 = {}, 0
    for name, shape in SLAB_PARTS:
        size = shape[0] * shape[1]
        nrows = -(-size // D)
        out[name] = slab[r0:r0 + nrows].reshape(-1)[:size].reshape(lead[name] + shape)
        r0 += nrows
    return out


SMALL_PARTS = (("norm_mix_g", 0, 0, D), ("norm_ffn_g", 1, 0, D), ("norm_final_g", 2, 0, D),
               ("conv_b", 3, 0, C_CONV), ("conv_ln_g", 3, C_CONV, C_CONV), ("conv_ln_b", 4, 0, C_CONV),
               ("gla_gate_b", 4, C_CONV, GLA_K), ("gla_norm_g", 4, C_CONV + GLA_K, DV))


def _pack_small(parts):
    slab = jnp.zeros((SMALL_ROWS, D), F32)
    for name, row, col, size in SMALL_PARTS:
        slab = lax.dynamic_update_slice(slab, parts[name].reshape(1, size).astype(F32), (row, col))
    return slab


def _unpack_small(slab, shapes):
    return {name: slab[row, col:col + size].reshape(shapes[name]) for name, row, col, size in SMALL_PARTS}


def _column_block(full, j, width):
    return lax.dynamic_slice_in_dim(full, j * width, width, axis=1)


def _local_step(x, target, w):
    n_ex, seq, _ = x.shape
    lp = HEAD_ROWS + seq
    t = n_ex * lp
    meta = jnp.broadcast_to(w["meta_tokens"][None], (n_ex, N_META, D))
    h0 = jnp.concatenate([jnp.zeros((n_ex, PAD_ROWS, D), F32), meta, x], axis=1).reshape(t, D)
    tgt = jnp.concatenate([jnp.zeros((n_ex, HEAD_ROWS, D), F32), target], axis=1).reshape(t, D)
    row_mask = jnp.concatenate([jnp.zeros((n_ex, HEAD_ROWS, 1), F32), jnp.ones((n_ex, seq, 1), F32)],
                               axis=1).reshape(t, 1)

    u, hn = _in_proj(h0, w["norm_mix_g"], w["w_in"])
    yc, y_conv = _conv_fwd(u, w["conv_w"], w["conv_b"], w["conv_ln_g"], w["conv_ln_b"], n_ex, lp)
    y_gla, states = _gla_fwd(u, w["gla_w_gate2"], w["gla_gate_b"], w["gla_norm_g"], n_ex, lp)
    h1, hn2, gate, up, act = _mix_out_ffn_up(h0, y_conv, y_gla, w["w_out"], w["norm_ffn_g"],
                                             w["w_ffn_gate_t"], w["w_ffn_up_t"])
    dh2, loss, d_final_g = _ffn_down_loss(act, w["w_ffn_down"], h1, tgt, w["norm_final_g"], row_mask)

    dgate, dup, dh1, dycat, d_ffn_g = _ffn_bwd(dh2, gate, up, h1, w["w_ffn_down"], w["w_ffn_gate_t"],
                                                w["w_ffn_up_t"], w["w_out"], w["norm_ffn_g"])
    du_conv, d_conv_w, d_conv_b, d_ln_g, d_ln_b = _conv_bwd(dycat, yc, u, w["conv_w"], w["conv_ln_g"],
                                                            w["conv_ln_b"], n_ex, lp)
    du_gla, d_w2, d_gate_b, d_norm_g = _gla_bwd(dycat, u, states, w["gla_w_gate2"], w["gla_gate_b"],
                                                w["gla_norm_g"], n_ex, lp)
    dh0, d_mix_g = _in_proj_bwd(du_conv, du_gla, w["w_in"][:, :2 * C_CONV], w["w_in"][:, 2 * C_CONV:],
                                h0, dh1, w["norm_mix_g"])

    d_w_in_t = jnp.concatenate([_wgrad(du_conv, hn, "wgrad_in_conv"), _wgrad(du_gla, hn, "wgrad_in_gla")],
                               axis=0)[:D_IN]
    d_w_out = jnp.concatenate([_wgrad(y_conv, dh1, "wgrad_out_conv"), _wgrad(y_gla, dh1, "wgrad_out_gla")], axis=0)
    dh0 = dh0.reshape(n_ex, lp, D)
    grads = {
        "w_in_t": d_w_in_t, "w_out": d_w_out,
        "w_ffn_gate_t": _wgrad(dgate, hn2, "wgrad_gate"), "w_ffn_up_t": _wgrad(dup, hn2, "wgrad_up"),
        "w_ffn_down": _wgrad(act, dh2, "wgrad_down"),
        "meta_tokens": jnp.sum(dh0[:, PAD_ROWS:HEAD_ROWS], axis=0),
        "conv_w": d_conv_w, "gla_w_gate2": d_w2[:RANK],
        "norm_mix_g": d_mix_g, "norm_ffn_g": d_ffn_g, "norm_final_g": d_final_g,
        "conv_b": d_conv_b, "conv_ln_g": d_ln_g, "conv_ln_b": d_ln_b,
        "gla_gate_b": d_gate_b, "gla_norm_g": d_norm_g,
    }
    return loss[0, 0], dh0[:, HEAD_ROWS:], grads


WEIGHT_NAMES = ("meta_tokens", "norm_mix_g", "w_in", "conv_w", "conv_b", "conv_ln_g", "conv_ln_b", "gla_w_gate2",
                "gla_gate_b", "gla_norm_g", "w_out", "norm_ffn_g", "w_ffn_gate", "w_ffn_up", "w_ffn_down",
                "norm_final_g")
MATMUL_WEIGHTS = ("w_in", "w_out", "w_ffn_gate", "w_ffn_up", "w_ffn_down")
ROW_SHARDED = ("w_out", "w_ffn_down")


def _full_weights(ws):
    sh = lambda name: ws[name].reshape(ws[name].shape[-2:])
    split = [sh("w_in").astype(BF16), sh("w_out").astype(BF16), sh("w_ffn_gate").T.astype(BF16),
             sh("w_ffn_up").T.astype(BF16), sh("w_ffn_down").astype(BF16)]
    whole = [sh("meta_tokens"), sh("conv_w"), sh("gla_w_gate2")]
    w_in, w_out, gate_t, up_t, down, meta, conv_w, w2 = _gather_weights(split, [0, 0, 0, 0, 0], whole)
    cols = lambda a: jnp.concatenate([a[j] for j in range(N_CHIPS)], axis=1)
    full = {name: ws[name].reshape(1, -1) for name, _, _, _ in SMALL_PARTS}
    full["w_in"] = jnp.concatenate([cols(w_in), jnp.zeros((D, D_IN_PAD - D_IN), BF16)], axis=1)
    full["w_out"] = w_out.reshape(D, D)
    full["w_ffn_gate_t"] = gate_t.reshape(D_FF, D)
    full["w_ffn_up_t"] = up_t.reshape(D_FF, D)
    full["w_ffn_down"] = down.reshape(D_FF, D)
    full["meta_tokens"] = cols(meta)
    full["conv_w"] = jnp.concatenate([cols(conv_w), jnp.zeros((32 - CONV_W, C_CONV), F32)], axis=0)
    full["gla_w_gate2"] = jnp.concatenate([cols(w2), jnp.zeros((128 - RANK, GLA_K), F32)], axis=0).astype(BF16)
    return full


SMALL_RS_ROWS = 48


def _pack_small_sharded(grads):
    by_chip = lambda g, w: jnp.transpose(g.reshape(g.shape[0], N_CHIPS, w), (1, 0, 2))
    meta = by_chip(grads["meta_tokens"], D // N_CHIPS)
    conv = by_chip(grads["conv_w"], C_CONV // N_CHIPS).reshape(N_CHIPS, 16, 256)
    w2 = by_chip(grads["gla_w_gate2"], GLA_K // N_CHIPS).reshape(N_CHIPS, 4, 256)
    pad = jnp.zeros((N_CHIPS, SMALL_RS_ROWS - 36, 256), F32)
    return jnp.concatenate([meta, conv, w2, pad], axis=1)


def _unpack_small_sharded(g):
    return {"meta_tokens": g[0:16], "conv_w": g[16:32].reshape(32, C_CONV // N_CHIPS)[:CONV_W],
            "gla_w_gate2": g[32:36].reshape(RANK, GLA_K // N_CHIPS)}


def _kernel_without_overlap(x, meta_tokens, norm_mix_g, w_in, conv_w, conv_b, conv_ln_g, conv_ln_b, gla_w_gate2, gla_gate_b, gla_norm_g, w_out, norm_ffn_g, w_ffn_gate, w_ffn_up, w_ffn_down, norm_final_g, loss_target, m_meta_tokens, m_norm_mix_g, m_w_in, m_conv_w, m_conv_b, m_conv_ln_g, m_conv_ln_b, m_gla_w_gate2, m_gla_gate_b, m_gla_norm_g, m_w_out, m_norm_ffn_g, m_w_ffn_gate, m_w_ffn_up, m_w_ffn_down, m_norm_final_g, v_meta_tokens, v_norm_mix_g, v_w_in, v_conv_w, v_conv_b, v_conv_ln_g, v_conv_ln_b, v_gla_w_gate2, v_gla_gate_b, v_gla_norm_g, v_w_out, v_norm_ffn_g, v_w_ffn_gate, v_w_ffn_up, v_w_ffn_down, v_norm_final_g):
    ws = dict(zip(WEIGHT_NAMES, (meta_tokens, norm_mix_g, w_in, conv_w, conv_b, conv_ln_g, conv_ln_b, gla_w_gate2,
                                 gla_gate_b, gla_norm_g, w_out, norm_ffn_g, w_ffn_gate, w_ffn_up, w_ffn_down,
                                 norm_final_g)))
    ms = dict(zip(WEIGHT_NAMES, (m_meta_tokens, m_norm_mix_g, m_w_in, m_conv_w, m_conv_b, m_conv_ln_g, m_conv_ln_b,
                                 m_gla_w_gate2, m_gla_gate_b, m_gla_norm_g, m_w_out, m_norm_ffn_g, m_w_ffn_gate,
                                 m_w_ffn_up, m_w_ffn_down, m_norm_final_g)))
    vs = dict(zip(WEIGHT_NAMES, (v_meta_tokens, v_norm_mix_g, v_w_in, v_conv_w, v_conv_b, v_conv_ln_g, v_conv_ln_b,
                                 v_gla_w_gate2, v_gla_gate_b, v_gla_norm_g, v_w_out, v_norm_ffn_g, v_w_ffn_gate,
                                 v_w_ffn_up, v_w_ffn_down, v_norm_final_g)))
    c = lax.axis_index("c")

    full = _full_weights(ws)
    loss, grad_x, grads = _local_step(x, loss_target, full)
    loss = lax.psum(loss, ("x", "y", "c"))

    rs_names = ("w_in", "w_out", "w_ffn_gate", "w_ffn_up", "w_ffn_down", "small")
    by_owner = [grads["w_in_t"].reshape(N_CHIPS, D_IN // N_CHIPS, D), grads["w_out"].reshape(N_CHIPS, D // N_CHIPS, D),
                grads["w_ffn_gate_t"].reshape(N_CHIPS, D_FF // N_CHIPS, D),
                grads["w_ffn_up_t"].reshape(N_CHIPS, D_FF // N_CHIPS, D),
                grads["w_ffn_down"].reshape(N_CHIPS, D_FF // N_CHIPS, D), _pack_small_sharded(grads)]
    from_sibling = _rs_to_sibling(by_owner)
    chip_sums = [_rs_add_halves(g, r, c, "rs_add_" + nm) for g, r, nm in zip(by_owner, from_sibling, rs_names)]
    halves = [_rs_sum_chips(p, "rs_sum_" + nm) for p, nm in zip(_rs_chip_exchange(chip_sums), rs_names)]
    reduced = dict(zip(rs_names, _rs_share(halves)))
    g_sharded = {"w_in": reduced["w_in"].T, "w_out": reduced["w_out"], "w_ffn_gate": reduced["w_ffn_gate"].T,
                 "w_ffn_up": reduced["w_ffn_up"].T, "w_ffn_down": reduced["w_ffn_down"],
                 **_unpack_small_sharded(reduced["small"])}
    out = {"grad": {}, "delta": {}, "new_m": {}, "new_v": {}}
    for name, g in g_sharded.items():
        shape = ws[name].shape
        flat = lambda a: a.reshape(shape[-2:])
        delta, new_m, new_v = _adamw(g, flat(ws[name]), flat(ms[name]), flat(vs[name]), "adamw_" + name)
        for kind, a in (("grad", g), ("delta", delta), ("new_m", new_m), ("new_v", new_v)):
            out[kind][name] = a.reshape(shape)

    small_shapes = {name: ws[name].shape for name, _, _, _ in SMALL_PARTS}
    g_s, d_s, m_s, v_s = _allreduce_small_adamw(_pack_small(grads), _pack_small(ws), _pack_small(ms), _pack_small(vs))
    for kind, slab in (("grad", g_s), ("delta", d_s), ("new_m", m_s), ("new_v", v_s)):
        out[kind].update(_unpack_small(slab, small_shapes))

    return (loss, grad_x, *[out[kind][name] for kind in ("grad", "delta", "new_m", "new_v") for name in WEIGHT_NAMES])


def _gather_plan(split, whole=(), axes=None):
    split, whole = list(split), list(whole)
    ns, n = len(split), len(split) + len(whole)

    def make(ins, outs, sems):
        ici_send, ici_recv, d2d_send, d2d_recv, own_send, own_recv = sems
        x, y, c = _mesh_pos()
        mine = 2 * x + y
        chips = _other_chips(x, y)
        blocks = [2 * px + py for px, py in chips]

        def own(a):
            return _remote(ins[a], outs[a].at[mine], own_send.at[a], own_recv.at[a], (x, y, 1 - c))

        def ici(a, k, block):
            px, py = chips[k]
            src, dst = ins[a], outs[a].at[block]
            if a < ns:
                src, dst = _half(src, c, axes[a]), _half(dst, c, axes[a])
            return _remote(src, dst, ici_send.at[3 * a + k], ici_recv.at[3 * a + k], (px, py, c))

        def d2d(a, k, half):
            part = _half(outs[a].at[blocks[k]], half, axes[a])
            return _remote(part, part, d2d_send.at[3 * a + k], d2d_recv.at[3 * a + k], (x, y, 1 - c))

        def start():
            for a in range(n):
                for k in range(3):
                    ici(a, k, mine).start()
                own(a).start()

        def finish():
            for a in range(n):
                for k in range(3):
                    ici(a, k, blocks[k]).wait_recv()
                    if a < ns:
                        d2d(a, k, c).start()
            for a in range(ns):
                for k in range(3):
                    d2d(a, k, 1 - c).wait_recv()
            for a in range(n):
                for k in range(3):
                    ici(a, k, mine).wait_send()
                    if a < ns:
                        d2d(a, k, c).wait_send()
                own(a).wait()

        return start, finish

    arrays = split + whole
    axes = [0] * ns if axes is None else list(axes)
    return _Plan(arrays, [jax.ShapeDtypeStruct((N_CHIPS,) + s.shape, s.dtype) for s in arrays],
                 [pltpu.SemaphoreType.DMA((3 * n,)), pltpu.SemaphoreType.DMA((3 * n,)),
                  pltpu.SemaphoreType.DMA((3 * ns,)), pltpu.SemaphoreType.DMA((3 * ns,)),
                  pltpu.SemaphoreType.DMA((n,)), pltpu.SemaphoreType.DMA((n,))], make)


def _to_sibling_plan(gs):
    n = len(gs)

    def make(ins, outs, sems):
        send_sems, recv_sems = sems
        x, y, c = _mesh_pos()

        def copy(a):
            return _remote(_half(ins[a], 1 - c, 2), outs[a], send_sems.at[a], recv_sems.at[a], (x, y, 1 - c))

        def start():
            for a in range(n):
                copy(a).start()

        def finish():
            for a in range(n):
                copy(a).wait()

        return start, finish

    return _Plan(list(gs), [jax.ShapeDtypeStruct(g.shape[:2] + (g.shape[2] // 2,), g.dtype) for g in gs],
                 [pltpu.SemaphoreType.DMA((n,)), pltpu.SemaphoreType.DMA((n,))], make)


def _chip_exchange_plan(ps):
    n = len(ps)

    def make(ins, outs, sems):
        send_sems, recv_sems = sems
        x, y, c = _mesh_pos()
        chips = _other_chips(x, y)

        def ici(a, k):
            px, py = chips[k]
            return _remote(ins[a].at[2 * px + py], outs[a].at[k], send_sems.at[3 * a + k],
                           recv_sems.at[3 * a + k], (px, py, c))

        def start():
            for a in range(n):
                for k in range(3):
                    ici(a, k).start()

        def finish():
            for a in range(n):
                for k in range(3):
                    ici(a, k).wait()

        return start, finish

    return _Plan(list(ps), [jax.ShapeDtypeStruct((3,) + p.shape[1:], p.dtype) for p in ps],
                 [pltpu.SemaphoreType.DMA((3 * n,)), pltpu.SemaphoreType.DMA((3 * n,))], make)


def _share_plan(halves):
    n = len(halves)

    def make(ins, outs, sems):
        send_sems, recv_sems = sems
        x, y, c = _mesh_pos()

        def d2d(a):
            return _remote(ins[a], outs[a], send_sems.at[a], recv_sems.at[a], (x, y, 1 - c))

        def start():
            for a in range(n):
                d2d(a).start()

        def finish():
            for a in range(n):
                d2d(a).wait()

        return start, finish

    return _Plan(list(halves), [jax.ShapeDtypeStruct(p.shape, p.dtype) for p in halves],
                 [pltpu.SemaphoreType.DMA((n,)), pltpu.SemaphoreType.DMA((n,))], make)


def _rs_sum(own, others, mine, name):
    _, rows, h = own.shape
    tr = rows // 2 if rows % 16 == 0 and rows > 64 else rows

    def body(mine_ref, own_ref, oth_ref, o_ref):
        p = oth_ref[...].astype(F32)
        o_ref[...] = ((own_ref[0].astype(F32) + p[0]) + p[1]) + p[2]

    return pl.pallas_call(
        body, name=name,
        grid_spec=pltpu.PrefetchScalarGridSpec(
            num_scalar_prefetch=1, grid=(rows // tr,),
            in_specs=[pl.BlockSpec((1, tr, h), lambda i, s: (s[0], i, 0)),
                      pl.BlockSpec((3, tr, h), lambda i, s: (0, i, 0))],
            out_specs=pl.BlockSpec((tr, h), lambda i, s: (i, 0))),
        out_shape=jax.ShapeDtypeStruct((rows, h), F32),
        compiler_params=_params(1),
    )(jnp.reshape(mine, (1,)).astype(jnp.int32), own, others)


def _join(mine, theirs, c):
    return jnp.where(c == 0, jnp.concatenate([mine, theirs], axis=1), jnp.concatenate([theirs, mine], axis=1))


LOSS_ROW = 5


def _merge_plans(a, b):
    na_in, na_out, na_sems = len(a.arrays), len(a.out_shape), len(a.sems)

    def make(ins, outs, sems):
        start_a, finish_a = a.make(ins[:na_in], outs[:na_out], sems[:na_sems])
        start_b, finish_b = b.make(ins[na_in:], outs[na_out:], sems[na_sems:])

        def start():
            start_a()
            start_b()

        def finish():
            finish_a()
            finish_b()

        return start, finish

    return _Plan(list(a.arrays) + list(b.arrays), list(a.out_shape) + list(b.out_shape),
                 list(a.sems) + list(b.sems), make)


def _exchange(plan, name):
    n_in, n_out = len(plan.arrays), len(plan.out_shape)

    def body(*refs):
        start, finish = plan.make(refs[:n_in], refs[n_in:n_in + n_out], refs[n_in + n_out:])
        start()
        finish()

    return pl.pallas_call(
        body, name=name, in_specs=[HBM_SPEC] * n_in, out_specs=[HBM_SPEC] * n_out, out_shape=list(plan.out_shape),
        scratch_shapes=list(plan.sems), compiler_params=pltpu.CompilerParams(has_side_effects=True),
    )(*plan.arrays)


def _adamw_halves(mine, theirs, c, w, m, v, name):
    rows, h = mine.shape
    tr = rows // 2 if rows % 16 == 0 else rows

    def body(c_ref, a_ref, b_ref, w_ref, m_ref, v_ref, go_ref, d_ref, m2_ref, v2_ref):
        g = jnp.where(pl.program_id(1) == c_ref[0], a_ref[...], b_ref[...])
        go_ref[...] = g
        d_ref[...], m2_ref[...], v2_ref[...] = _adam_update(g, w_ref[...], m_ref[...], v_ref[...])

    half = pl.BlockSpec((tr, h), lambda i, j, s: (i, 0))
    spec = pl.BlockSpec((tr, h), lambda i, j, s: (i, j))
    return pl.pallas_call(
        body, name=name,
        grid_spec=pltpu.PrefetchScalarGridSpec(num_scalar_prefetch=1, grid=(rows // tr, 2),
                                               in_specs=[half, half, spec, spec, spec], out_specs=[spec] * 4),
        out_shape=[jax.ShapeDtypeStruct((rows, 2 * h), F32)] * 4,
        compiler_params=_params(2),
    )(jnp.reshape(c, (1,)).astype(jnp.int32), mine, theirs, w, m, v)


ADAMW_STEPS = 4


def _adamw_many(items, c, plan=None):
    n = len(items)
    tiles = [it[0].shape[0] // ADAMW_STEPS for it in items]
    h = items[0][0].shape[1]

    def body(c_ref, *refs):
        ins, outs = refs[:5 * n], refs[5 * n:]
        own = pl.program_id(1) == c_ref[0]
        for i in range(n):
            a_ref, b_ref, w_ref, m_ref, v_ref = ins[5 * i:5 * i + 5]
            go_ref, d_ref, m2_ref, v2_ref = outs[4 * i:4 * i + 4]
            g = jnp.where(own, a_ref[...], b_ref[...])
            go_ref[...] = g
            d_ref[...], m2_ref[...], v2_ref[...] = _adam_update(g, w_ref[...], m_ref[...], v_ref[...])

    in_specs, out_specs, out_shape, args = [pl.BlockSpec(memory_space=pltpu.SMEM)], [], [], []
    for (mine, theirs, w, m, v), tr in zip(items, tiles):
        half = pl.BlockSpec((tr, h), lambda i, j: (i, 0))
        full = pl.BlockSpec((tr, h), lambda i, j: (i, j))
        in_specs += [half, half, full, full, full]
        out_specs += [full] * 4
        out_shape += [jax.ShapeDtypeStruct(w.shape, F32)] * 4
        args += [mine, theirs, w, m, v]
    res, extra = _call(body, name="adamw_early", grid=(ADAMW_STEPS, 2), in_specs=in_specs, out_specs=out_specs,
                       out_shape=out_shape, plan=plan)(jnp.reshape(c, (1,)).astype(jnp.int32), *args)
    return [res[4 * i:4 * i + 4] for i in range(n)], extra


def _all_to_all_plan(part):
    def make(ins, outs, sems):
        send_sems, recv_sems, local_sem = sems
        (p_ref,), (slots,) = ins, outs
        x, y, c = _mesh_pos()
        me = 4 * x + 2 * y + c
        peers = [(px, py, pc) for px in (x, 1 - x) for py in (y, 1 - y) for pc in (c, 1 - c)][1:]

        def remote(k, slot):
            return _remote(p_ref, slots.at[slot], send_sems.at[k], recv_sems.at[k], peers[k])

        def local():
            return pltpu.make_async_copy(p_ref, slots.at[me], local_sem)

        def start():
            for k in range(7):
                remote(k, me).start()
            local().start()

        def finish():
            for k, (px, py, pc) in enumerate(peers):
                remote(k, 4 * px + 2 * py + pc).wait_recv()
            for k in range(7):
                remote(k, me).wait_send()
            local().wait()

        return start, finish

    return _Plan([part], [jax.ShapeDtypeStruct((8,) + part.shape, part.dtype)],
                 [pltpu.SemaphoreType.DMA((7,)), pltpu.SemaphoreType.DMA((7,)), pltpu.SemaphoreType.DMA(())], make)


def _sum_slots_adamw(slots, late_slots, w, m, v):
    late_rows = late_slots.shape[1]

    def body(s_ref, l_ref, w_ref, m_ref, v_ref, g_ref, d_ref, m2_ref, v2_ref):
        g, late = s_ref[0], l_ref[0]
        for d in range(1, 8):
            g = g + s_ref[d]
            late = late + l_ref[d]
        g = jnp.concatenate([g[:late_rows] + late, g[late_rows:]], axis=0)
        g_ref[...] = g
        d_ref[...], m2_ref[...], v2_ref[...] = _adam_update(g, w_ref[...], m_ref[...], v_ref[...])

    vm = pl.BlockSpec(memory_space=pltpu.VMEM)
    shape = jax.ShapeDtypeStruct(w.shape, F32)
    return pl.pallas_call(body, name="small_sum_adamw", in_specs=[vm] * 5, out_specs=[vm] * 4,
                          out_shape=[shape] * 4)(slots, late_slots, w, m, v)


def _columns(gathered):
    return jnp.concatenate([gathered[j] for j in range(N_CHIPS)], axis=1)


def kernel(x, meta_tokens, norm_mix_g, w_in, conv_w, conv_b, conv_ln_g, conv_ln_b, gla_w_gate2, gla_gate_b, gla_norm_g, w_out, norm_ffn_g, w_ffn_gate, w_ffn_up, w_ffn_down, norm_final_g, loss_target, m_meta_tokens, m_norm_mix_g, m_w_in, m_conv_w, m_conv_b, m_conv_ln_g, m_conv_ln_b, m_gla_w_gate2, m_gla_gate_b, m_gla_norm_g, m_w_out, m_norm_ffn_g, m_w_ffn_gate, m_w_ffn_up, m_w_ffn_down, m_norm_final_g, v_meta_tokens, v_norm_mix_g, v_w_in, v_conv_w, v_conv_b, v_conv_ln_g, v_conv_ln_b, v_gla_w_gate2, v_gla_gate_b, v_gla_norm_g, v_w_out, v_norm_ffn_g, v_w_ffn_gate, v_w_ffn_up, v_w_ffn_down, v_norm_final_g):
    ws = dict(zip(WEIGHT_NAMES, (meta_tokens, norm_mix_g, w_in, conv_w, conv_b, conv_ln_g, conv_ln_b, gla_w_gate2,
                                 gla_gate_b, gla_norm_g, w_out, norm_ffn_g, w_ffn_gate, w_ffn_up, w_ffn_down,
                                 norm_final_g)))
    ms = dict(zip(WEIGHT_NAMES, (m_meta_tokens, m_norm_mix_g, m_w_in, m_conv_w, m_conv_b, m_conv_ln_g, m_conv_ln_b,
                                 m_gla_w_gate2, m_gla_gate_b, m_gla_norm_g, m_w_out, m_norm_ffn_g, m_w_ffn_gate,
                                 m_w_ffn_up, m_w_ffn_down, m_norm_final_g)))
    vs = dict(zip(WEIGHT_NAMES, (v_meta_tokens, v_norm_mix_g, v_w_in, v_conv_w, v_conv_b, v_conv_ln_g, v_conv_ln_b,
                                 v_gla_w_gate2, v_gla_gate_b, v_gla_norm_g, v_w_out, v_norm_ffn_g, v_w_ffn_gate,
                                 v_w_ffn_up, v_w_ffn_down, v_norm_final_g)))
    c = lax.axis_index("c")
    shard = lambda d, name: d[name].reshape(d[name].shape[-2:])
    vec = {name: ws[name].reshape(1, -1) for name, _, _, _ in SMALL_PARTS}
    n_ex, seq, _ = x.shape
    lp = HEAD_ROWS + seq
    t = n_ex * lp

    (tgt, h0), (w_in_g, meta_g, conv_w_g, w2_g) = _pad_head_rows([loss_target, x], plan=_gather_plan(
        [shard(ws, "w_in").T.astype(BF16)],
        [shard(ws, "meta_tokens"), shard(ws, "conv_w"), shard(ws, "gla_w_gate2")], axes=[1]))
    w_in_t = jnp.concatenate([w_in_g.reshape(D_IN, D), jnp.zeros((D_IN_PAD - D_IN, D), BF16)], axis=0)
    w_in_full = w_in_t.T
    conv_w_full = jnp.concatenate([_columns(conv_w_g), jnp.zeros((32 - CONV_W, C_CONV), F32)], axis=0)
    w2_full = jnp.concatenate([_columns(w2_g), jnp.zeros((128 - RANK, GLA_K), F32)], axis=0).astype(BF16)

    h0 = _set_meta_rows(h0, _columns(meta_g)).reshape(t, D)
    tgt = tgt.reshape(t, D)
    row_mask = jnp.concatenate([jnp.zeros((n_ex, HEAD_ROWS, 1), F32), jnp.ones((n_ex, seq, 1), F32)],
                               axis=1).reshape(t, 1)

    (u, hn), (w_out_g,) = _in_proj(h0, vec["norm_mix_g"], w_in_full,
                                   plan=_gather_plan([shard(ws, "w_out").astype(BF16)]))
    (yc, y_conv), (gate_g,) = _conv_fwd(
        u, conv_w_full, vec["conv_b"], vec["conv_ln_g"], vec["conv_ln_b"], n_ex, lp,
        plan=_gather_plan([shard(ws, "w_ffn_gate").T.astype(BF16)]))
    (y_gla, states), (up_g,) = _gla_fwd(u, w2_full, vec["gla_gate_b"], vec["gla_norm_g"], n_ex, lp,
                                        plan=_gather_plan([shard(ws, "w_ffn_up").T.astype(BF16)]))
    w_out_full = w_out_g.reshape(D, D)
    w_gate_t, w_up_t = gate_g.reshape(D_FF, D), up_g.reshape(D_FF, D)

    (h1, hn2, gate, up, act), (down_g,) = _mix_out_ffn_up(
        h0, y_conv, y_gla, w_out_full, vec["norm_ffn_g"], w_gate_t.T, w_up_t.T,
        plan=_gather_plan([shard(ws, "w_ffn_down").astype(BF16)]))
    w_down_full = down_g.reshape(D_FF, D)
    dh2, loss, d_final_g = _ffn_down_loss(act, w_down_full, h1, tgt, vec["norm_final_g"], row_mask)
    dgate, dup, dh1, dycat, d_ffn_g = _ffn_bwd(dh2, gate, up, h1, w_down_full.T, w_gate_t, w_up_t, w_out_full.T,
                                                vec["norm_ffn_g"])

    early = ("w_ffn_gate", "w_ffn_up", "w_ffn_down", "w_out")
    ffn_block = lambda g: g.reshape(N_CHIPS, D_FF // N_CHIPS, D)
    g_gate = ffn_block(_wgrad(dgate, hn2, "wgrad_gate"))
    g_up, (gate_sib,) = _wgrad_hosting(dup, hn2, "wgrad_up", _to_sibling_plan([g_gate]))
    g_up = ffn_block(g_up)
    g_down, (up_sib,) = _wgrad_hosting(act, dh2, "wgrad_down", _to_sibling_plan([g_up]))
    g_down = ffn_block(g_down)
    g_out = jnp.concatenate([_wgrad(y_conv, dh1, "wgrad_out_conv"), _wgrad(y_gla, dh1, "wgrad_out_gla")],
                            axis=0).reshape(N_CHIPS, D // N_CHIPS, D)
    cs_gate = _rs_add_halves(g_gate, gate_sib, c, "rs_add_w_ffn_gate")
    cs_up = _rs_add_halves(g_up, up_sib, c, "rs_add_w_ffn_up")
    (du_conv, d_conv_w, d_conv_b, d_ln_g, d_ln_b), (ex_gate, ex_up, down_sib, out_sib) = _conv_bwd(
        dycat, yc, u, conv_w_full, vec["conv_ln_g"], vec["conv_ln_b"], n_ex, lp,
        plan=_merge_plans(_chip_exchange_plan([cs_gate, cs_up]), _to_sibling_plan([g_down, g_out])))
    cs_down = _rs_add_halves(g_down, down_sib, c, "rs_add_w_ffn_down")
    cs_out = _rs_add_halves(g_out, out_sib, c, "rs_add_w_out")
    (du_gla, d_w2, d_gate_b, d_norm_g), (ex_down, ex_out) = _gla_bwd(
        dycat, u, states, w2_full, vec["gla_gate_b"], vec["gla_norm_g"], n_ex, lp,
        plan=_chip_exchange_plan([cs_down, cs_out]))
    mine = 2 * lax.axis_index("x") + lax.axis_index("y")
    halves = [_rs_sum(own, oth, mine, "rs_sum_" + nm)
              for own, oth, nm in zip((cs_gate, cs_up, cs_down, cs_out), (ex_gate, ex_up, ex_down, ex_out), early)]

    small = {"norm_mix_g": jnp.zeros((1, D), F32), "norm_ffn_g": d_ffn_g, "norm_final_g": d_final_g,
             "conv_b": d_conv_b, "conv_ln_g": d_ln_g, "conv_ln_b": d_ln_b, "gla_gate_b": d_gate_b,
             "gla_norm_g": d_norm_g}
    part = lax.dynamic_update_slice(_pack_small(small), loss[:, :1], (LOSS_ROW, 0))
    part = jnp.concatenate([part, jnp.zeros((N_META, D), F32), d_conv_w.reshape(16, D), d_w2[:RANK].reshape(4, D),
                            jnp.zeros((4, D), F32)], axis=0)
    g_in_gla, (slots,) = _wgrad_hosting(du_gla, hn, "wgrad_in_gla", _all_to_all_plan(part))
    d_w_in_t = jnp.concatenate([_wgrad(du_conv, hn, "wgrad_in_conv"), g_in_gla],
                               axis=0)[:D_IN].reshape(N_CHIPS, D_IN // N_CHIPS, D)
    (in_from_sibling,) = _exchange(_to_sibling_plan([d_w_in_t]), "rs_late_to_sibling")
    in_chip_sum = _rs_add_halves(d_w_in_t, in_from_sibling, c, "rs_add_w_in")
    (dh0, d_mix_g), shared = _in_proj_bwd(
        du_conv, du_gla, w_in_t[:2 * C_CONV], w_in_t[2 * C_CONV:], h0, dh1, vec["norm_mix_g"],
        plan=_merge_plans(_share_plan(halves), _chip_exchange_plan([in_chip_sum])))
    dh0 = dh0.reshape(n_ex, lp, D)
    grad_x = dh0[:, HEAD_ROWS:]
    late_part = jnp.concatenate([d_mix_g, jnp.zeros((SMALL_ROWS - 1, D), F32),
                                 jnp.sum(dh0[:, PAD_ROWS:HEAD_ROWS], axis=0)], axis=0)
    (late_slots,) = _exchange(_all_to_all_plan(late_part), "small_late_all_to_all")

    out = {"grad": {}, "delta": {}, "new_m": {}, "new_v": {}}

    def update(name, g=None, halves=None, transposed=False):
        shape = ws[name].shape
        lay = (lambda a: a.T) if transposed else (lambda a: a)
        w2d, m2d, v2d = lay(shard(ws, name)), lay(shard(ms, name)), lay(shard(vs, name))
        if halves is not None:
            res = _adamw_halves(*halves, c, w2d, m2d, v2d, "adamw_" + name)
        else:
            res = [g, *_adamw(g, w2d, m2d, v2d, "adamw_" + name)]
        for kind, a in zip(("grad", "delta", "new_m", "new_v"), res):
            out[kind][name] = lay(a).reshape(shape)

    small_shapes = {name: ws[name].shape for name, _, _, _ in SMALL_PARTS}
    early_layout = (("w_ffn_gate", True), ("w_ffn_up", True), ("w_ffn_down", False), ("w_out", False))
    items = []
    for (name, transposed), mine_half, their_half in zip(early_layout, halves, shared):
        lay = (lambda a: a.T) if transposed else (lambda a: a)
        items.append((mine_half, their_half, lay(shard(ws, name)), lay(shard(ms, name)), lay(shard(vs, name))))
    updated, _ = _adamw_many(items, c)
    for (name, transposed), res in zip(early_layout, updated):
        lay = (lambda a: a.T) if transposed else (lambda a: a)
        for kind, a in zip(("grad", "delta", "new_m", "new_v"), res):
            out[kind][name] = lay(a).reshape(ws[name].shape)

    in_half = _rs_sum(in_chip_sum, shared[4], mine, "rs_sum_w_in")
    (in_shared,) = _exchange(_share_plan([in_half]), "rs_late_share")
    update("w_in", halves=(in_half, in_shared), transposed=True)

    tall = lambda a: jnp.concatenate([a, jnp.zeros((part.shape[0] - SMALL_ROWS, D), F32)], axis=0)
    g_s, d_s, m_s, v_s = _sum_slots_adamw(slots, late_slots, tall(_pack_small(ws)), tall(_pack_small(ms)),
                                          tall(_pack_small(vs)))
    for kind, slab in (("grad", g_s), ("delta", d_s), ("new_m", m_s), ("new_v", v_s)):
        out[kind].update(_unpack_small(slab, small_shapes))
    loss = g_s[LOSS_ROW, 0]
    block = lambda a, width: lax.dynamic_slice_in_dim(a, mine * width, width, axis=1)
    update("meta_tokens", g=block(g_s[8:24], D // N_CHIPS))
    update("conv_w", g=block(g_s[24:40].reshape(32, C_CONV), C_CONV // N_CHIPS)[:CONV_W])
    update("gla_w_gate2", g=block(g_s[40:44].reshape(RANK, GLA_K), GLA_K // N_CHIPS))

    return (loss, grad_x, *[out[kind][name] for kind in ("grad", "delta", "new_m", "new_v") for name in WEIGHT_NAMES])
```

```python
import functools
from typing import Any, Callable, NamedTuple, Sequence

import jax
import jax.numpy as jnp
from jax import lax
from jax.experimental import pallas as pl
from jax.experimental.pallas import tpu as pltpu

F32 = jnp.float32
BF16 = jnp.bfloat16
MESH = pl.DeviceIdType.MESH

D = 1024
N_META = 16
C_CONV = 512
CONV_W = 31
GLA_K = 256
GLA_V = 512
N_HEADS = 4
DK = 64
DV = 128
RANK = 16
CHUNK = 64
PAD_ROWS = CHUNK - N_META
HEAD_ROWS = CHUNK
D_IN = 2576
D_IN_PAD = 2688
D_GLA_IN = D_IN_PAD - 2 * C_CONV
D_FF = 2816
RMS_EPS = 1e-6
LN_EPS = 1e-5
GATE_TAU = 16.0
N_CHIPS = 4

ADAM_LR = 0.001
ADAM_B1 = 0.9
ADAM_B2 = 0.999
ADAM_EPS = 1e-08
ADAM_WD = 0.01
ADAM_STEP = 10

V7X_VMEM_BYTES = 64 * 1024 * 1024
VMEM_LIMIT = V7X_VMEM_BYTES - 8 * 1024 * 1024
SUBLANES = 8
ROW_PART = 128
FFN_BWD_TILE = 192

WEIGHT_NAMES = ("meta_tokens", "norm_mix_g", "w_in", "conv_w", "conv_b", "conv_ln_g", "conv_ln_b", "gla_w_gate2",
                "gla_gate_b", "gla_norm_g", "w_out", "norm_ffn_g", "w_ffn_gate", "w_ffn_up", "w_ffn_down",
                "norm_final_g")

SMALL_ROWS = 8
SMALL_PARTS = (("norm_mix_g", 0, 0, D), ("norm_ffn_g", 1, 0, D), ("norm_final_g", 2, 0, D),
               ("conv_b", 3, 0, C_CONV), ("conv_ln_g", 3, C_CONV, C_CONV), ("conv_ln_b", 4, 0, C_CONV),
               ("gla_gate_b", 4, C_CONV, GLA_K), ("gla_norm_g", 4, C_CONV + GLA_K, DV))
LOSS_ROW = 5

HBM_SPEC = pl.BlockSpec(memory_space=pltpu.HBM)


def _dot(a, b):
    return jnp.dot(a, b, preferred_element_type=F32)


def _dot_nt(a, b):
    return lax.dot_general(a, b, (((1,), (1,)), ((), ())), preferred_element_type=F32)


def _dot_tn(a, b):
    return lax.dot_general(a, b, (((0,), (0,)), ((), ())), preferred_element_type=F32)


def _sigmoid(x):
    return 1.0 / (1.0 + jnp.exp(-x))


def _const_spec(shape):
    return pl.BlockSpec(shape, lambda *_: (0,) * len(shape), pipeline_mode=pl.Buffered(1))


def _acc_spec(shape):
    return pl.BlockSpec(shape, lambda *_: (0,) * len(shape))


def _params(n_axes):
    return pltpu.CompilerParams(dimension_semantics=("arbitrary",) * n_axes, vmem_limit_bytes=VMEM_LIMIT)


def _row_tile(t, want):
    for r in (want, 384, 192, 128, 64):
        if r <= want and t % r == 0:
            return r
    raise ValueError(f"no row tile for {t}")


def _row_parts(r):
    if r % ROW_PART:
        return [slice(None)]
    return [pl.ds(i * ROW_PART, ROW_PART) for i in range(r // ROW_PART)]


def _in_lockstep(bodies):
    live = list(bodies)
    while live:
        still = []
        for g in live:
            try:
                next(g)
                still.append(g)
            except StopIteration:
                pass
        live = still


class _Plan(NamedTuple):
    arrays: Sequence[Any]
    out_shape: Sequence[Any]
    sems: Sequence[Any]
    make: Callable


def _call(body, *, name, grid, in_specs, out_specs, out_shape, scratch_shapes=(), plan=None):
    n_in, n_out, n_scr = len(in_specs), len(out_specs), len(scratch_shapes)
    if plan is None:
        plan = _Plan([], [], [], lambda ins, outs, sems: (lambda: None, lambda: None))
    nx_in, nx_out = len(plan.arrays), len(plan.out_shape)

    def hosted(*refs):
        ins, xins = refs[:n_in], refs[n_in:n_in + nx_in]
        o0 = n_in + nx_in
        outs, xouts = refs[o0:o0 + n_out], refs[o0 + n_out:o0 + n_out + nx_out]
        s0 = o0 + n_out + nx_out
        scr, sems = refs[s0:s0 + n_scr], refs[s0 + n_scr:]
        ids = [pl.program_id(a) for a in range(len(grid))]
        first = functools.reduce(jnp.logical_and, [i == 0 for i in ids])
        last = functools.reduce(jnp.logical_and, [i == g - 1 for i, g in zip(ids, grid)])
        start, finish = plan.make(xins, xouts, sems)
        pl.when(first)(start)
        body(*ins, *outs, *scr)
        pl.when(last)(finish)

    call = pl.pallas_call(
        hosted, name=name, grid=grid, in_specs=list(in_specs) + [HBM_SPEC] * nx_in,
        out_specs=list(out_specs) + [HBM_SPEC] * nx_out, out_shape=list(out_shape) + list(plan.out_shape),
        scratch_shapes=list(scratch_shapes) + list(plan.sems),
        compiler_params=pltpu.CompilerParams(dimension_semantics=("arbitrary",) * len(grid),
                                             vmem_limit_bytes=VMEM_LIMIT, has_side_effects=nx_in > 0))

    def run(*args):
        res = call(*args, *plan.arrays)
        return res[:n_out], res[n_out:]

    return run


def _pad_head_rows(arrays, plan=None):
    n_ex, seq, _ = arrays[0].shape
    nc = (HEAD_ROWS + seq) // CHUNK
    n = len(arrays)

    def body(*refs):
        for a_ref, o_ref in zip(refs[:n], refs[n:]):
            o_ref[...] = jnp.where(pl.program_id(0) > 0, a_ref[...], 0.0)

    return _call(
        body, name="pad_head_rows", grid=(nc,),
        in_specs=[pl.BlockSpec((n_ex, CHUNK, D), lambda i: (0, jnp.maximum(i - 1, 0), 0))] * n,
        out_specs=[pl.BlockSpec((n_ex, CHUNK, D), lambda i: (0, i, 0))] * n,
        out_shape=[jax.ShapeDtypeStruct((n_ex, HEAD_ROWS + seq, D), F32)] * n,
        plan=plan,
    )(*arrays)


def _set_meta_rows(h0, meta):
    n_ex = h0.shape[0]

    def body(h_ref, meta_ref, o_ref):
        o_ref[...] = jnp.concatenate(
            [h_ref[:, :PAD_ROWS, :], jnp.broadcast_to(meta_ref[...][None], (n_ex, N_META, D))], axis=1)

    head = pl.BlockSpec((n_ex, HEAD_ROWS, D), lambda i: (0, 0, 0))
    return pl.pallas_call(
        body, name="set_meta_rows", grid=(1,), in_specs=[head, pl.BlockSpec((N_META, D), lambda i: (0, 0))],
        out_specs=head, out_shape=jax.ShapeDtypeStruct(h0.shape, F32), input_output_aliases={0: 0},
        compiler_params=_params(1),
    )(h0, meta)


def _in_proj(h0, g_mix, w_in, plan=None):
    t = h0.shape[0]
    r = _row_tile(t, 384)

    def body(h_ref, g_ref, w_ref, u_ref, hn_ref):
        h = h_ref[...]
        rstd = lax.rsqrt(jnp.mean(h * h, axis=-1, keepdims=True) + RMS_EPS)
        hn = (h * rstd * g_ref[...]).astype(BF16)
        hn_ref[...] = hn
        u_ref[...] = _dot(hn, w_ref[...])

    return _call(
        body, name="in_proj", grid=(t // r,),
        in_specs=[pl.BlockSpec((r, D), lambda i: (i, 0)), _const_spec((1, D)), _const_spec((D, D_IN_PAD))],
        out_specs=[pl.BlockSpec((r, D_IN_PAD), lambda i: (i, 0)), pl.BlockSpec((r, D), lambda i: (i, 0))],
        out_shape=[jax.ShapeDtypeStruct((t, D_IN_PAD), F32), jax.ShapeDtypeStruct((t, D), BF16)],
        plan=plan,
    )(h0, g_mix, w_in)


CONV_TILE = 192
CONV_SUB = 32
CONV_LEAD = CONV_SUB - (CONV_W - 1)


def _shifted_copies(src, dst, r):
    for s in range(1, SUBLANES):
        dst[s - 1] = src[s:s + r + CONV_SUB - SUBLANES, :]


def _shifted_rows(src, shifted, start):
    base, s = SUBLANES * (start // SUBLANES), start % SUBLANES
    if s == 0:
        return src[base:base + CONV_SUB, :]
    return shifted[s - 1, base:base + CONV_SUB, :]


def _conv_fwd(u, conv_w, conv_b, ln_g, ln_b, n_ex, lp, plan=None):
    r = CONV_TILE
    nt = lp // r
    hb = r // CONV_SUB

    def body(cur_ref, prev_ref, w_ref, b_ref, lg_ref, lb_ref, yc_ref, y_ref, glu, glu_sh):
        i = pl.program_id(1)
        cur = cur_ref[...]
        glu[CONV_SUB:CONV_SUB + r, :] = cur[:, :C_CONV] * _sigmoid(cur[:, C_CONV:])
        pv = prev_ref[...]
        halo = pv[:, :C_CONV] * _sigmoid(pv[:, C_CONV:])
        glu[0:CONV_SUB, :] = jnp.where(i > 0, halo, 0.0)
        _shifted_copies(glu, glu_sh, r)
        w = w_ref[...]
        for j in range(r // CONV_SUB):
            r0 = j * CONV_SUB
            acc = jnp.zeros((CONV_SUB, C_CONV), F32) + b_ref[...]
            for k in range(CONV_W):
                acc = acc + w[k:k + 1, :] * _shifted_rows(glu, glu_sh, r0 + CONV_LEAD + k)
            mu = jnp.mean(acc, axis=-1, keepdims=True)
            cen = acc - mu
            var = jnp.mean(cen * cen, axis=-1, keepdims=True)
            out = cen * lax.rsqrt(var + LN_EPS) * lg_ref[...] + lb_ref[...]
            y = out * _sigmoid(out)
            row = i * r + r0 + lax.broadcasted_iota(jnp.int32, (CONV_SUB, 1), 0)
            y = jnp.where(row >= PAD_ROWS, y, 0.0)
            yc_ref[r0:r0 + CONV_SUB, :] = acc
            y_ref[r0:r0 + CONV_SUB, :] = y.astype(BF16)

    t = n_ex * lp
    return _call(
        body, name="conv_fwd", grid=(n_ex, nt),
        in_specs=[pl.BlockSpec((r, 2 * C_CONV), lambda b, i: (b * nt + i, 0)),
                  pl.BlockSpec((CONV_SUB, 2 * C_CONV), lambda b, i: (jnp.maximum((b * nt + i) * hb - 1, 0), 0)),
                  _const_spec((32, C_CONV)), _const_spec((1, C_CONV)), _const_spec((1, C_CONV)), _const_spec((1, C_CONV))],
        out_specs=[pl.BlockSpec((r, C_CONV), lambda b, i: (b * nt + i, 0)),
                   pl.BlockSpec((r, C_CONV), lambda b, i: (b * nt + i, 0))],
        out_shape=[jax.ShapeDtypeStruct((t, C_CONV), F32), jax.ShapeDtypeStruct((t, C_CONV), BF16)],
        scratch_shapes=[pltpu.VMEM((r + CONV_SUB, C_CONV), F32),
                        pltpu.VMEM((SUBLANES - 1, r + CONV_SUB - SUBLANES, C_CONV), F32)],
        plan=plan,
    )(u, u, conv_w, conv_b, ln_g, ln_b)


def _mix_out_ffn_up(h0, y_conv, y_gla, w_out, g_ffn, w_gate, w_up, plan=None):
    t = h0.shape[0]
    r = _row_tile(t, 384)

    def body(h0_ref, yc_ref, yg_ref, wo_ref, g_ref, wg_ref, wu_ref, h1_ref, hn_ref, gate_ref, up_ref, act_ref):
        h1 = h0_ref[...] + _dot(yc_ref[...], wo_ref[0:C_CONV, :]) + _dot(yg_ref[...], wo_ref[C_CONV:D, :])
        h1_ref[...] = h1
        rstd = lax.rsqrt(jnp.mean(h1 * h1, axis=-1, keepdims=True) + RMS_EPS)
        hn = (h1 * rstd * g_ref[...]).astype(BF16)
        hn_ref[...] = hn
        gate = _dot(hn, wg_ref[...])
        up = _dot(hn, wu_ref[...])
        gate_ref[...] = gate
        up_ref[...] = up
        act_ref[...] = (gate * _sigmoid(gate) * up).astype(BF16)

    rows = lambda w: pl.BlockSpec((r, w), lambda i: (i, 0))
    return _call(
        body, name="mix_out_ffn_up", grid=(t // r,),
        in_specs=[rows(D), rows(C_CONV), rows(GLA_V), _const_spec((D, D)), _const_spec((1, D)),
                  _const_spec((D, D_FF)), _const_spec((D, D_FF))],
        out_specs=[rows(D), rows(D), rows(D_FF), rows(D_FF), rows(D_FF)],
        out_shape=[jax.ShapeDtypeStruct((t, D), F32), jax.ShapeDtypeStruct((t, D), BF16),
                   jax.ShapeDtypeStruct((t, D_FF), F32), jax.ShapeDtypeStruct((t, D_FF), F32),
                   jax.ShapeDtypeStruct((t, D_FF), BF16)],
        plan=plan,
    )(h0, y_conv, y_gla, w_out, g_ffn, w_gate, w_up)


def _ffn_down_loss(act, w_down, h1, target, g_final, row_mask):
    t = h1.shape[0]
    r = _row_tile(t, 384)

    def body(act_ref, wd_ref, h1_ref, tgt_ref, gf_ref, mask_ref, dh2_ref, loss_ref, dgf_ref):
        @pl.when(pl.program_id(0) == 0)
        def _():
            loss_ref[...] = jnp.zeros_like(loss_ref)
            dgf_ref[...] = jnp.zeros_like(dgf_ref)

        gf = gf_ref[...]

        def part(rows):
            h2 = h1_ref[rows, :] + _dot(act_ref[rows, :], wd_ref[...])
            yield
            rstd = lax.rsqrt(jnp.mean(h2 * h2, axis=-1, keepdims=True) + RMS_EPS)
            nrm = h2 * rstd
            err = (nrm * gf - tgt_ref[rows, :]) * mask_ref[rows, :]
            loss_ref[...] += jnp.sum(err * err) * (0.5 / D)
            dy = err * (1.0 / D)
            dgf_ref[...] += jnp.sum(dy * nrm, axis=0, keepdims=True)
            dn = dy * gf
            dh2_ref[rows, :] = rstd * (dn - nrm * jnp.mean(dn * nrm, axis=-1, keepdims=True))

        _in_lockstep(part(rows) for rows in _row_parts(r))

    rows = lambda w: pl.BlockSpec((r, w), lambda i: (i, 0))
    return pl.pallas_call(
        body, name="ffn_down_loss", grid=(t // r,),
        in_specs=[rows(D_FF), _const_spec((D_FF, D)), rows(D), rows(D), _const_spec((1, D)), rows(1)],
        out_specs=[rows(D), _acc_spec((1, 128)), _acc_spec((1, D))],
        out_shape=[jax.ShapeDtypeStruct((t, D), F32), jax.ShapeDtypeStruct((1, 128), F32),
                   jax.ShapeDtypeStruct((1, D), F32)],
        compiler_params=_params(1),
    )(act, w_down, h1, target, g_final, row_mask)


def _ffn_bwd(dh2, gate, up, h1, w_down_t, w_gate_t, w_up_t, w_out_t, g_ffn):
    t = h1.shape[0]
    r = _row_tile(t, FFN_BWD_TILE)

    def body(dh2_ref, gate_ref, up_ref, h1_ref, wd_ref, wg_ref, wu_ref, wo_ref, g_ref,
             dgate_ref, dup_ref, dh1_ref, dycat_ref, dg_ref):
        @pl.when(pl.program_id(0) == 0)
        def _():
            dg_ref[...] = jnp.zeros_like(dg_ref)

        dh2 = dh2_ref[...]
        dact = _dot(dh2.astype(BF16), wd_ref[...])
        gate = gate_ref[...]
        sg = _sigmoid(gate)
        dgate = (dact * up_ref[...] * (sg * (1.0 + gate * (1.0 - sg)))).astype(BF16)
        dup = (dact * (gate * sg)).astype(BF16)
        dgate_ref[...] = dgate
        dup_ref[...] = dup
        dhn = _dot(dgate, wg_ref[...]) + _dot(dup, wu_ref[...])
        h1 = h1_ref[...]
        rstd = lax.rsqrt(jnp.mean(h1 * h1, axis=-1, keepdims=True) + RMS_EPS)
        nrm = h1 * rstd
        dg_ref[...] += jnp.sum(dhn * nrm, axis=0, keepdims=True)
        dn = dhn * g_ref[...]
        dh1 = dh2 + rstd * (dn - nrm * jnp.mean(dn * nrm, axis=-1, keepdims=True))
        dh1_ref[...] = dh1
        dycat_ref[...] = _dot(dh1.astype(BF16), wo_ref[...])

    rows = lambda w: pl.BlockSpec((r, w), lambda i: (i, 0))
    return pl.pallas_call(
        body, name="ffn_bwd", grid=(t // r,),
        in_specs=[rows(D), rows(D_FF), rows(D_FF), rows(D), _const_spec((D, D_FF)), _const_spec((D_FF, D)),
                  _const_spec((D_FF, D)), _const_spec((D, D)), _const_spec((1, D))],
        out_specs=[rows(D_FF), rows(D_FF), rows(D), rows(D), _acc_spec((1, D))],
        out_shape=[jax.ShapeDtypeStruct((t, D_FF), BF16), jax.ShapeDtypeStruct((t, D_FF), BF16),
                   jax.ShapeDtypeStruct((t, D), F32), jax.ShapeDtypeStruct((t, D), F32),
                   jax.ShapeDtypeStruct((1, D), F32)],
        compiler_params=_params(1),
    )(dh2, gate, up, h1, w_down_t, w_gate_t, w_up_t, w_out_t, g_ffn)


def _conv_bwd(dycat, yc, u, conv_w, ln_g, ln_b, n_ex, lp, plan=None):
    r = CONV_TILE
    nt = lp // r
    hb = r // CONV_SUB
    nsub = r // CONV_SUB

    def ln_bwd(dy, yc_rows, live, lg, lb):
        mu = jnp.mean(yc_rows, axis=-1, keepdims=True)
        cen = yc_rows - mu
        rs = lax.rsqrt(jnp.mean(cen * cen, axis=-1, keepdims=True) + LN_EPS)
        yn = cen * rs
        out = yn * lg + lb
        so = _sigmoid(out)
        dout = jnp.where(live, dy * (so * (1.0 + out * (1.0 - so))), 0.0)
        dyn = dout * lg
        dyc = rs * (dyn - jnp.mean(dyn, axis=-1, keepdims=True) - yn * jnp.mean(dyn * yn, axis=-1, keepdims=True))
        return dyc, dout, yn

    def body(dy_ref, dyn_ref, yc_ref, ycn_ref, cur_ref, prev_ref, w_ref, lg_ref, lb_ref,
             du_ref, dw_ref, db_ref, dlg_ref, dlb_ref, glu, dycs, dwacc, glu_sh, dycs_sh):
        b = pl.program_id(0)
        i = pl.program_id(1)
        first = jnp.logical_and(b == 0, i == 0)

        @pl.when(first)
        def _():
            dwacc[...] = jnp.zeros_like(dwacc)
            db_ref[...] = jnp.zeros_like(db_ref)
            dlg_ref[...] = jnp.zeros_like(dlg_ref)
            dlb_ref[...] = jnp.zeros_like(dlb_ref)

        lg, lb = lg_ref[...], lb_ref[...]
        cur = cur_ref[...]
        sig = _sigmoid(cur[:, C_CONV:])
        glu[CONV_SUB:CONV_SUB + r, :] = cur[:, :C_CONV] * sig
        pv = prev_ref[...]
        glu[0:CONV_SUB, :] = jnp.where(i > 0, pv[:, :C_CONV] * _sigmoid(pv[:, C_CONV:]), 0.0)

        row = i * r + lax.broadcasted_iota(jnp.int32, (r, 1), 0)
        dyc, dout, yn = ln_bwd(dy_ref[...], yc_ref[...], row >= PAD_ROWS, lg, lb)
        dycs[0:r, :] = dyc
        dycn, _, _ = ln_bwd(dyn_ref[...], ycn_ref[...], i < nt - 1, lg, lb)
        dycs[r:r + CONV_SUB, :] = dycn
        db_ref[...] += jnp.sum(dyc, axis=0, keepdims=True)
        dlg_ref[...] += jnp.sum(dout * yn, axis=0, keepdims=True)
        dlb_ref[...] += jnp.sum(dout, axis=0, keepdims=True)

        _shifted_copies(glu, glu_sh, r)
        _shifted_copies(dycs, dycs_sh, r)
        w = w_ref[...]
        for j in range(nsub):
            r0 = j * CONV_SUB
            dblk = dycs[r0:r0 + CONV_SUB, :]
            dglu = jnp.zeros((CONV_SUB, C_CONV), F32)
            for k in range(CONV_W):
                dglu = dglu + w[k:k + 1, :] * _shifted_rows(dycs, dycs_sh, r0 + (CONV_W - 1) - k)
                prod = dblk * _shifted_rows(glu, glu_sh, r0 + CONV_LEAD + k)
                dwacc[k] += prod.reshape(CONV_SUB // SUBLANES, SUBLANES, C_CONV).sum(axis=0)
            sg = sig[r0:r0 + CONV_SUB, :]
            cv = cur[r0:r0 + CONV_SUB, :C_CONV]
            du_ref[r0:r0 + CONV_SUB, :C_CONV] = (dglu * sg).astype(BF16)
            du_ref[r0:r0 + CONV_SUB, C_CONV:] = (dglu * cv * sg * (1.0 - sg)).astype(BF16)

        @pl.when(jnp.logical_and(b == n_ex - 1, i == nt - 1))
        def _():
            dw_ref[...] = jnp.sum(dwacc[...], axis=1)

    t = n_ex * lp
    cur_rows = lambda w, col: pl.BlockSpec((r, w), lambda b, i: (b * nt + i, col))
    nxt_rows = lambda w, col: pl.BlockSpec(
        (CONV_SUB, w), lambda b, i: (jnp.minimum((b * nt + i + 1) * hb, n_ex * nt * hb - 1), col))
    return _call(
        body, name="conv_bwd", grid=(n_ex, nt),
        in_specs=[cur_rows(C_CONV, 0), nxt_rows(C_CONV, 0), cur_rows(C_CONV, 0), nxt_rows(C_CONV, 0),
                  cur_rows(2 * C_CONV, 0),
                  pl.BlockSpec((CONV_SUB, 2 * C_CONV), lambda b, i: (jnp.maximum((b * nt + i) * hb - 1, 0), 0)),
                  _const_spec((32, C_CONV)), _const_spec((1, C_CONV)), _const_spec((1, C_CONV))],
        out_specs=[cur_rows(2 * C_CONV, 0), _acc_spec((32, C_CONV)), _acc_spec((1, C_CONV)),
                   _acc_spec((1, C_CONV)), _acc_spec((1, C_CONV))],
        out_shape=[jax.ShapeDtypeStruct((t, 2 * C_CONV), BF16), jax.ShapeDtypeStruct((32, C_CONV), F32),
                   jax.ShapeDtypeStruct((1, C_CONV), F32), jax.ShapeDtypeStruct((1, C_CONV), F32),
                   jax.ShapeDtypeStruct((1, C_CONV), F32)],
        scratch_shapes=[pltpu.VMEM((r + CONV_SUB, C_CONV), F32), pltpu.VMEM((r + CONV_SUB, C_CONV), F32),
                        pltpu.VMEM((32, SUBLANES, C_CONV), F32),
                        pltpu.VMEM((SUBLANES - 1, r + CONV_SUB - SUBLANES, C_CONV), F32),
                        pltpu.VMEM((SUBLANES - 1, r + CONV_SUB - SUBLANES, C_CONV), F32)],
        plan=plan,
    )(dycat, dycat, yc, yc, u, u, conv_w, ln_g, ln_b)


HEAD_ROWS_ALL = N_HEADS * CHUNK


def _gla_gates(lr, w2, gb, first_chunk):
    z = _dot(lr.astype(BF16), w2) + gb
    a = (jnp.minimum(z, 0.0) - jnp.log(1.0 + jnp.exp(-jnp.abs(z)))) * (1.0 / GATE_TAU)
    row = lax.broadcasted_iota(jnp.int32, (CHUNK, 1), 0)
    live = jnp.logical_or(jnp.logical_not(first_chunk), row >= PAD_ROWS)
    return z, jnp.where(live, a, 0.0), live


def _tri(lower):
    i = lax.broadcasted_iota(jnp.int32, (CHUNK, CHUNK), 0)
    j = lax.broadcasted_iota(jnp.int32, (CHUNK, CHUNK), 1)
    return (i >= j) if lower else (i <= j)


def _head_of(shape, axis, per_head):
    return lax.broadcasted_iota(jnp.int32, shape, axis) // per_head


def _expand(x, lanes_per_head):
    rows, lanes = HEAD_ROWS_ALL, x.shape[1]
    keep = _head_of((rows, lanes), 0, CHUNK) == _head_of((rows, lanes), 1, lanes_per_head)
    return jnp.where(keep, jnp.tile(x, (N_HEADS, 1)), 0.0)


def _expand_lanes(x):
    rows, w = x.shape
    keep = _head_of((rows, N_HEADS * w), 0, CHUNK) == _head_of((rows, N_HEADS * w), 1, w)
    return jnp.where(keep, jnp.tile(x, (1, N_HEADS)), 0.0)


def _expand_state(st):
    rows, lanes = N_HEADS * DV, st.shape[1]
    keep = _head_of((rows, lanes), 0, DV) == _head_of((rows, lanes), 1, DK)
    return jnp.where(keep, jnp.tile(st, (N_HEADS, 1)), 0.0)


def _fold(t, rows_per_head):
    lane_head = _head_of((rows_per_head, t.shape[1]), 1, DK)
    out = jnp.where(lane_head == 0, t[0:rows_per_head], 0.0)
    for h in range(1, N_HEADS):
        out = out + jnp.where(lane_head == h, t[h * rows_per_head:(h + 1) * rows_per_head], 0.0)
    return out


def _rows_by_head(x):
    return jnp.concatenate([x[:, h * DV:(h + 1) * DV] for h in range(N_HEADS)], axis=0)


def _lanes_by_head(x):
    return jnp.concatenate([x[h * CHUNK:(h + 1) * CHUNK] for h in range(N_HEADS)], axis=1)


def _running_sum(a, lower):
    hi = a.astype(BF16)
    rest = a - hi.astype(F32)
    mid = rest.astype(BF16)
    lo = (rest - mid.astype(F32)).astype(BF16)
    w = a.shape[1]
    parts = _dot(_tri(lower).astype(F32).astype(BF16), jnp.concatenate([hi, mid, lo], axis=1))
    return parts[:, :w] + parts[:, w:2 * w] + parts[:, 2 * w:]


def _stacked_causal():
    i = lax.broadcasted_iota(jnp.int32, (HEAD_ROWS_ALL, CHUNK), 0) % CHUNK
    j = lax.broadcasted_iota(jnp.int32, (HEAD_ROWS_ALL, CHUNK), 1)
    return i >= j


def _gla_chunk(q, k, v, lr, st, w2, gb, first_chunk):
    z, a, live = _gla_gates(lr, w2, gb, first_chunk)
    yield
    b = _running_sum(a, True)
    yield
    bl = b[CHUNK - 1:CHUNK, :]
    e_pos, e_neg, e_dec = jnp.exp(b), jnp.exp(-b), jnp.exp(bl - b)
    q_f, k_f, kd_f = q * (DK ** -0.5) * e_pos, k * e_neg, k * e_dec
    qx = _expand(q_f, DK).astype(BF16)
    k_in, k_dec, v_b = k_f.astype(BF16), kd_f.astype(BF16), v.astype(BF16)
    s = jnp.where(_stacked_causal(), _dot_nt(qx, k_in), 0.0).astype(BF16)
    o_inter = _dot_nt(qx, st.astype(BF16))
    yield
    p = _dot(s, v_b)
    yield
    o = jnp.concatenate([p[h * CHUNK:(h + 1) * CHUNK, h * DV:(h + 1) * DV] for h in range(N_HEADS)], axis=0) + o_inter
    return dict(z=z, live=live, bl=bl, e_pos=e_pos, e_neg=e_neg, e_dec=e_dec, q_f=q_f, k_f=k_f, kd_f=kd_f,
                qx=qx, k_in=k_in, k_dec=k_dec, v_b=v_b, s=s, o=o, decay=jnp.exp(bl))


def _gla_fwd(u, w2, gb, ng, n_ex, lp, plan=None):
    nc = lp // CHUNK
    t = n_ex * lp

    def body(qk_ref, v_ref, g_ref, lr_ref, w2_ref, gb_ref, ng_ref, y_ref, st_ref, state):
        n = pl.program_id(0)

        @pl.when(n == 0)
        def _():
            state[...] = jnp.zeros_like(state)

        def one_example(e):
            st = state[e]
            st_ref[e] = st
            qk = qk_ref[e]
            c = yield from _gla_chunk(qk[:, :GLA_K], qk[:, GLA_K:], v_ref[e], lr_ref[e], st, w2_ref[...],
                                      gb_ref[...], n == 0)
            o = c["o"]
            rstd = lax.rsqrt(jnp.mean(o * o, axis=-1, keepdims=True) + RMS_EPS)
            g = _rows_by_head(g_ref[e])
            y_ref[e] = _lanes_by_head(o * rstd * ng_ref[...] * (g * _sigmoid(g))).astype(BF16)
            state[e] = c["decay"] * st + _fold(_dot_tn(c["v_b"], c["k_dec"]), DV)

        _in_lockstep(one_example(e) for e in range(n_ex))

    u3 = u.reshape(n_ex, lp, D_IN_PAD)
    blk = lambda w, col: pl.BlockSpec((n_ex, CHUNK, w), lambda n: (0, n, col))
    (y, states), extra = _call(
        body, name="gla_fwd", grid=(nc,),
        in_specs=[blk(2 * GLA_K, 2), blk(GLA_V, 3), blk(GLA_V, 4), blk(128, 20),
                  _const_spec((128, GLA_K)), _const_spec((1, GLA_K)), _const_spec((1, DV))],
        out_specs=[blk(GLA_V, 0), pl.BlockSpec((n_ex, DV, GLA_K), lambda n: (0, n, 0))],
        out_shape=[jax.ShapeDtypeStruct((n_ex, lp, GLA_V), BF16),
                   jax.ShapeDtypeStruct((n_ex, nc * DV, GLA_K), F32)],
        scratch_shapes=[pltpu.VMEM((n_ex, DV, GLA_K), F32)],
        plan=plan,
    )(u3, u3, u3, u3, w2, gb, ng)
    return (y.reshape(t, GLA_V), states), extra


def _gla_bwd(dycat, u, states, w2, gb, ng, n_ex, lp, plan=None):
    nc = lp // CHUNK
    t = n_ex * lp

    def body(dy_ref, qk_ref, v_ref, g_ref, lr_ref, st_ref, w2_ref, gb_ref, ng_ref,
             du_ref, dw2_ref, dgb_ref, dng_ref, dstate):
        n = pl.program_id(0)
        chunk = nc - 1 - n

        @pl.when(n == 0)
        def _():
            dw2_ref[...] = jnp.zeros_like(dw2_ref)
            dgb_ref[...] = jnp.zeros_like(dgb_ref)
            dng_ref[...] = jnp.zeros_like(dng_ref)
            dstate[...] = jnp.zeros_like(dstate)

        def one_example(e):
            qk = qk_ref[e]
            lr = lr_ref[e]
            st = st_ref[e]
            dst = dstate[e]
            c = yield from _gla_chunk(qk[:, :GLA_K], qk[:, GLA_K:], v_ref[e], lr, st, w2_ref[...], gb_ref[...],
                                      chunk == 0)
            qx, k_in, k_dec, v_b, s, o = c["qx"], c["k_in"], c["k_dec"], c["v_b"], c["s"], c["o"]
            ngv = ng_ref[...]
            rstd = lax.rsqrt(jnp.mean(o * o, axis=-1, keepdims=True) + RMS_EPS)
            nrm = o * rstd
            g = _rows_by_head(g_ref[e])
            dy = _rows_by_head(dy_ref[e])
            sg = _sigmoid(g)
            dg = dy * nrm * ngv * (sg * (1.0 + g * (1.0 - sg)))
            dt = dy * (g * sg)
            dng_ref[...] += jnp.sum(dt * nrm, axis=0, keepdims=True)
            dn = dt * ngv
            do = rstd * (dn - nrm * jnp.mean(dn * nrm, axis=-1, keepdims=True))
            do_b = do.astype(BF16)
            dox = _expand_lanes(do).astype(BF16)
            dstx = _expand_state(dst).astype(BF16)
            yield
            da = jnp.where(_stacked_causal(), _dot_nt(dox, v_b), 0.0).astype(BF16)
            dv = _dot_tn(s, dox) + _dot_nt(k_dec, dstx)
            dk_dec = _dot(v_b, dstx)
            dstate[e] = _dot_tn(do_b, qx) + c["decay"] * dst
            yield
            dq_in = _fold(_dot(da, k_in) + _dot(do_b, st.astype(BF16)), CHUNK)
            dk_in = _dot_tn(da, qx)
            yield
            dbl = (jnp.sum(dk_dec * c["kd_f"], axis=0, keepdims=True)
                   + c["decay"] * jnp.sum(dst * st, axis=0, keepdims=True))
            dq = dq_in * (DK ** -0.5) * c["e_pos"]
            dk = dk_in * c["e_neg"] + dk_dec * c["e_dec"]
            db = dq_in * c["q_f"] - dk_in * c["k_f"] - dk_dec * c["kd_f"]
            row = lax.broadcasted_iota(jnp.int32, (CHUNK, 1), 0)
            da_log = _running_sum(db + jnp.where(row == CHUNK - 1, dbl, 0.0), False)
            yield
            dz = jnp.where(c["live"], da_log * (1.0 - _sigmoid(c["z"])) * (1.0 / GATE_TAU), 0.0)
            dz_b = dz.astype(BF16)
            out = du_ref.at[e]
            out[:, 0:GLA_K] = dq.astype(BF16)
            out[:, GLA_K:2 * GLA_K] = dk.astype(BF16)
            out[:, 2 * GLA_K:2 * GLA_K + GLA_V] = dv.astype(BF16)
            out[:, 2 * GLA_K + GLA_V:2 * GLA_K + 2 * GLA_V] = _lanes_by_head(dg).astype(BF16)
            out[:, 2 * GLA_K + 2 * GLA_V:] = _dot_nt(dz_b, w2_ref[...]).astype(BF16)
            dw2_ref[...] += _dot_tn(lr.astype(BF16), dz_b)
            dgb_ref[...] += jnp.sum(dz, axis=0, keepdims=True)

        _in_lockstep(one_example(e) for e in range(n_ex))

    u3 = u.reshape(n_ex, lp, D_IN_PAD)
    rev = lambda w, col: pl.BlockSpec((n_ex, CHUNK, w), lambda n: (0, nc - 1 - n, col))
    (du, d_w2, d_gb, d_ng), extra = _call(
        body, name="gla_bwd", grid=(nc,),
        in_specs=[rev(GLA_V, 1), rev(2 * GLA_K, 2), rev(GLA_V, 3), rev(GLA_V, 4), rev(128, 20),
                  pl.BlockSpec((n_ex, DV, GLA_K), lambda n: (0, nc - 1 - n, 0)),
                  _const_spec((128, GLA_K)), _const_spec((1, GLA_K)), _const_spec((1, DV))],
        out_specs=[rev(D_GLA_IN, 0), _acc_spec((128, GLA_K)), _acc_spec((1, GLA_K)), _acc_spec((1, DV))],
        out_shape=[jax.ShapeDtypeStruct((n_ex, lp, D_GLA_IN), BF16), jax.ShapeDtypeStruct((128, GLA_K), F32),
                   jax.ShapeDtypeStruct((1, GLA_K), F32), jax.ShapeDtypeStruct((1, DV), F32)],
        scratch_shapes=[pltpu.VMEM((n_ex, DV, GLA_K), F32)],
        plan=plan,
    )(dycat.reshape(n_ex, lp, D), u3, u3, u3, u3, states, w2, gb, ng)
    return (du.reshape(t, D_GLA_IN), d_w2, d_gb, d_ng), extra


def _in_proj_bwd(du_conv, du_gla, w_in_t_conv, w_in_t_gla, h0, dh1, g_mix, plan=None):
    t = h0.shape[0]
    r = _row_tile(t, 384)

    def body(dc_ref, dg_ref, wc_ref, wg_ref, h_ref, dh1_ref, g_ref, dh0_ref, dgm_ref):
        @pl.when(pl.program_id(0) == 0)
        def _():
            dgm_ref[...] = jnp.zeros_like(dgm_ref)

        dhn = _dot(dc_ref[...], wc_ref[...]) + _dot(dg_ref[...], wg_ref[...])
        h = h_ref[...]
        rstd = lax.rsqrt(jnp.mean(h * h, axis=-1, keepdims=True) + RMS_EPS)
        nrm = h * rstd
        dgm_ref[...] += jnp.sum(dhn * nrm, axis=0, keepdims=True)
        dn = dhn * g_ref[...]
        dh0_ref[...] = dh1_ref[...] + rstd * (dn - nrm * jnp.mean(dn * nrm, axis=-1, keepdims=True))

    rows = lambda w: pl.BlockSpec((r, w), lambda i: (i, 0))
    return _call(
        body, name="in_proj_bwd", grid=(t // r,),
        in_specs=[rows(2 * C_CONV), rows(D_GLA_IN), _const_spec((2 * C_CONV, D)), _const_spec((D_GLA_IN, D)),
                  rows(D), rows(D), _const_spec((1, D))],
        out_specs=[rows(D), _acc_spec((1, D))],
        out_shape=[jax.ShapeDtypeStruct((t, D), F32), jax.ShapeDtypeStruct((1, D), F32)],
        plan=plan,
    )(du_conv, du_gla, w_in_t_conv, w_in_t_gla, h0, dh1, g_mix)


def _wgrad(x, dy, name, plan=None):
    t, m = x.shape
    n = dy.shape[1]
    tk = t // 3 if t % (3 * 128) == 0 else _row_tile(t, 384)
    tm = m if m <= D_GLA_IN else m // 2

    def body(x_ref, dy_ref, o_ref):
        @pl.when(pl.program_id(1) == 0)
        def _():
            o_ref[...] = jnp.zeros_like(o_ref)

        o_ref[...] += _dot_tn(x_ref[...].astype(BF16), dy_ref[...].astype(BF16))

    (out,), extra = _call(
        body, name=name, grid=(m // tm, t // tk),
        in_specs=[pl.BlockSpec((tk, tm), lambda i, k: (k, i)), pl.BlockSpec((tk, n), lambda i, k: (k, 0))],
        out_specs=[pl.BlockSpec((tm, n), lambda i, k: (i, 0))],
        out_shape=[jax.ShapeDtypeStruct((m, n), F32)],
        plan=plan,
    )(x, dy)
    return out, extra


def _adam_update(g, w, m, v):
    m2 = ADAM_B1 * m + (1.0 - ADAM_B1) * g
    v2 = ADAM_B2 * v + (1.0 - ADAM_B2) * (g * g)
    m_hat = m2 / (1.0 - ADAM_B1 ** ADAM_STEP)
    v_hat = v2 / (1.0 - ADAM_B2 ** ADAM_STEP)
    delta = -ADAM_LR * (m_hat / (jnp.sqrt(v_hat) + ADAM_EPS) + ADAM_WD * w)
    return delta, m2, v2


def _adamw(g, w, m, v, name):
    def body(g_ref, w_ref, m_ref, v_ref, d_ref, m2_ref, v2_ref):
        d_ref[...], m2_ref[...], v2_ref[...] = _adam_update(g_ref[...], w_ref[...], m_ref[...], v_ref[...])

    spec = pl.BlockSpec(g.shape, lambda i: (0, 0))
    return pl.pallas_call(
        body, name=name, grid=(1,), in_specs=[spec] * 4, out_specs=[spec] * 3,
        out_shape=[jax.ShapeDtypeStruct(g.shape, F32)] * 3, compiler_params=_params(1),
    )(g, w, m, v)


ADAMW_STEPS = 4


def _adamw_halves(items, c, name):
    n = len(items)
    h = items[0][0].shape[1]
    steps = ADAMW_STEPS if all(it[0].shape[0] % (ADAMW_STEPS * SUBLANES) == 0 for it in items) else 1

    def body(c_ref, *refs):
        ins, outs = refs[:5 * n], refs[5 * n:]
        own = pl.program_id(1) == c_ref[0]
        for i in range(n):
            a_ref, b_ref, w_ref, m_ref, v_ref = ins[5 * i:5 * i + 5]
            go_ref, d_ref, m2_ref, v2_ref = outs[4 * i:4 * i + 4]
            g = jnp.where(own, a_ref[...], b_ref[...])
            go_ref[...] = g
            d_ref[...], m2_ref[...], v2_ref[...] = _adam_update(g, w_ref[...], m_ref[...], v_ref[...])

    in_specs, out_specs, out_shape, args = [pl.BlockSpec(memory_space=pltpu.SMEM)], [], [], []
    for mine, theirs, w, m, v in items:
        tr = mine.shape[0] // steps
        half = pl.BlockSpec((tr, h), lambda i, j: (i, 0))
        full = pl.BlockSpec((tr, h), lambda i, j: (i, j))
        in_specs += [half, half, full, full, full]
        out_specs += [full] * 4
        out_shape += [jax.ShapeDtypeStruct(w.shape, F32)] * 4
        args += [mine, theirs, w, m, v]
    res = pl.pallas_call(
        body, name=name, grid=(steps, 2), in_specs=in_specs, out_specs=out_specs, out_shape=out_shape,
        compiler_params=_params(2),
    )(jnp.reshape(c, (1,)).astype(jnp.int32), *args)
    return [res[4 * i:4 * i + 4] for i in range(n)]


def _rs_add_halves(g, recv, c, name):
    _, rows, w = g.shape
    h = w // 2
    tr = rows // 2 if rows % 16 == 0 and rows > 64 else rows

    def body(c_ref, a_ref, b_ref, o_ref):
        o_ref[...] = (a_ref[...] + b_ref[...]).astype(BF16)

    return pl.pallas_call(
        body, name=name,
        grid_spec=pltpu.PrefetchScalarGridSpec(
            num_scalar_prefetch=1, grid=(N_CHIPS, rows // tr),
            in_specs=[pl.BlockSpec((1, tr, h), lambda j, i, s: (j, i, s[0])),
                      pl.BlockSpec((1, tr, h), lambda j, i, s: (j, i, 0))],
            out_specs=pl.BlockSpec((1, tr, h), lambda j, i, s: (j, i, 0))),
        out_shape=jax.ShapeDtypeStruct((N_CHIPS, rows, h), BF16),
        compiler_params=_params(2),
    )(jnp.reshape(c, (1,)).astype(jnp.int32), g, recv)


def _rs_sum(own, others, mine, name):
    _, rows, h = own.shape
    tr = rows // 2 if rows % 16 == 0 and rows > 64 else rows

    def body(mine_ref, own_ref, oth_ref, o_ref):
        p = oth_ref[...].astype(F32)
        o_ref[...] = ((own_ref[0].astype(F32) + p[0]) + p[1]) + p[2]

    return pl.pallas_call(
        body, name=name,
        grid_spec=pltpu.PrefetchScalarGridSpec(
            num_scalar_prefetch=1, grid=(rows // tr,),
            in_specs=[pl.BlockSpec((1, tr, h), lambda i, s: (s[0], i, 0)),
                      pl.BlockSpec((3, tr, h), lambda i, s: (0, i, 0))],
            out_specs=pl.BlockSpec((tr, h), lambda i, s: (i, 0))),
        out_shape=jax.ShapeDtypeStruct((rows, h), F32),
        compiler_params=_params(1),
    )(jnp.reshape(mine, (1,)).astype(jnp.int32), own, others)


def _sum_slots_adamw(slots, late_slots, w, m, v):
    late_rows = late_slots.shape[1]

    def body(s_ref, l_ref, w_ref, m_ref, v_ref, g_ref, d_ref, m2_ref, v2_ref):
        g, late = s_ref[0], l_ref[0]
        for d in range(1, 8):
            g = g + s_ref[d]
            late = late + l_ref[d]
        g = jnp.concatenate([g[:late_rows] + late, g[late_rows:]], axis=0)
        g_ref[...] = g
        d_ref[...], m2_ref[...], v2_ref[...] = _adam_update(g, w_ref[...], m_ref[...], v_ref[...])

    vm = pl.BlockSpec(memory_space=pltpu.VMEM)
    shape = jax.ShapeDtypeStruct(w.shape, F32)
    return pl.pallas_call(body, name="small_sum_adamw", in_specs=[vm] * 5, out_specs=[vm] * 4,
                          out_shape=[shape] * 4)(slots, late_slots, w, m, v)


def _mesh_pos():
    return lax.axis_index("x"), lax.axis_index("y"), lax.axis_index("c")


def _other_chips(x, y):
    return [(1 - x, y), (x, 1 - y), (1 - x, 1 - y)]


def _half(ref, c, axis):
    n = ref.shape[axis] // 2
    return ref.at[(slice(None),) * axis + (pl.ds(c * n, n),)]


def _remote(src, dst, send_sem, recv_sem, device):
    return pltpu.make_async_remote_copy(src_ref=src, dst_ref=dst, send_sem=send_sem, recv_sem=recv_sem,
                                        device_id=device, device_id_type=MESH)


def _gather_plan(split, whole=(), axes=None):
    split, whole = list(split), list(whole)
    ns, n = len(split), len(split) + len(whole)

    def make(ins, outs, sems):
        ici_send, ici_recv, d2d_send, d2d_recv, own_send, own_recv = sems
        x, y, c = _mesh_pos()
        mine = 2 * x + y
        chips = _other_chips(x, y)
        blocks = [2 * px + py for px, py in chips]

        def own(a):
            return _remote(ins[a], outs[a].at[mine], own_send.at[a], own_recv.at[a], (x, y, 1 - c))

        def ici(a, k, block):
            px, py = chips[k]
            src, dst = ins[a], outs[a].at[block]
            if a < ns:
                src, dst = _half(src, c, axes[a]), _half(dst, c, axes[a])
            return _remote(src, dst, ici_send.at[3 * a + k], ici_recv.at[3 * a + k], (px, py, c))

        def d2d(a, k, half):
            part = _half(outs[a].at[blocks[k]], half, axes[a])
            return _remote(part, part, d2d_send.at[3 * a + k], d2d_recv.at[3 * a + k], (x, y, 1 - c))

        def start():
            for a in range(n):
                for k in range(3):
                    ici(a, k, mine).start()
                own(a).start()

        def finish():
            for a in range(n):
                for k in range(3):
                    ici(a, k, blocks[k]).wait_recv()
                    if a < ns:
                        d2d(a, k, c).start()
            for a in range(ns):
                for k in range(3):
                    d2d(a, k, 1 - c).wait_recv()
            for a in range(n):
                for k in range(3):
                    ici(a, k, mine).wait_send()
                    if a < ns:
                        d2d(a, k, c).wait_send()
                own(a).wait()

        return start, finish

    arrays = split + whole
    axes = [0] * ns if axes is None else list(axes)
    return _Plan(arrays, [jax.ShapeDtypeStruct((N_CHIPS,) + s.shape, s.dtype) for s in arrays],
                 [pltpu.SemaphoreType.DMA((3 * n,)), pltpu.SemaphoreType.DMA((3 * n,)),
                  pltpu.SemaphoreType.DMA((3 * ns,)), pltpu.SemaphoreType.DMA((3 * ns,)),
                  pltpu.SemaphoreType.DMA((n,)), pltpu.SemaphoreType.DMA((n,))], make)


def _to_sibling_plan(gs):
    n = len(gs)

    def make(ins, outs, sems):
        send_sems, recv_sems = sems
        x, y, c = _mesh_pos()

        def copy(a):
            return _remote(_half(ins[a], 1 - c, 2), outs[a], send_sems.at[a], recv_sems.at[a], (x, y, 1 - c))

        def start():
            for a in range(n):
                copy(a).start()

        def finish():
            for a in range(n):
                copy(a).wait()

        return start, finish

    return _Plan(list(gs), [jax.ShapeDtypeStruct(g.shape[:2] + (g.shape[2] // 2,), g.dtype) for g in gs],
                 [pltpu.SemaphoreType.DMA((n,)), pltpu.SemaphoreType.DMA((n,))], make)


def _chip_exchange_plan(ps):
    n = len(ps)

    def make(ins, outs, sems):
        send_sems, recv_sems = sems
        x, y, c = _mesh_pos()
        chips = _other_chips(x, y)

        def ici(a, k):
            px, py = chips[k]
            return _remote(ins[a].at[2 * px + py], outs[a].at[k], send_sems.at[3 * a + k],
                           recv_sems.at[3 * a + k], (px, py, c))

        def start():
            for a in range(n):
                for k in range(3):
                    ici(a, k).start()

        def finish():
            for a in range(n):
                for k in range(3):
                    ici(a, k).wait()

        return start, finish

    return _Plan(list(ps), [jax.ShapeDtypeStruct((3,) + p.shape[1:], p.dtype) for p in ps],
                 [pltpu.SemaphoreType.DMA((3 * n,)), pltpu.SemaphoreType.DMA((3 * n,))], make)


def _share_plan(halves):
    n = len(halves)

    def make(ins, outs, sems):
        send_sems, recv_sems = sems
        x, y, c = _mesh_pos()

        def d2d(a):
            return _remote(ins[a], outs[a], send_sems.at[a], recv_sems.at[a], (x, y, 1 - c))

        def start():
            for a in range(n):
                d2d(a).start()

        def finish():
            for a in range(n):
                d2d(a).wait()

        return start, finish

    return _Plan(list(halves), [jax.ShapeDtypeStruct(p.shape, p.dtype) for p in halves],
                 [pltpu.SemaphoreType.DMA((n,)), pltpu.SemaphoreType.DMA((n,))], make)


def _all_to_all_plan(part):
    def make(ins, outs, sems):
        send_sems, recv_sems, local_sem = sems
        (p_ref,), (slots,) = ins, outs
        x, y, c = _mesh_pos()
        me = 4 * x + 2 * y + c
        peers = [(px, py, pc) for px in (x, 1 - x) for py in (y, 1 - y) for pc in (c, 1 - c)][1:]

        def remote(k, slot):
            return _remote(p_ref, slots.at[slot], send_sems.at[k], recv_sems.at[k], peers[k])

        def local():
            return pltpu.make_async_copy(p_ref, slots.at[me], local_sem)

        def start():
            for k in range(7):
                remote(k, me).start()
            local().start()

        def finish():
            for k, (px, py, pc) in enumerate(peers):
                remote(k, 4 * px + 2 * py + pc).wait_recv()
            for k in range(7):
                remote(k, me).wait_send()
            local().wait()

        return start, finish

    return _Plan([part], [jax.ShapeDtypeStruct((8,) + part.shape, part.dtype)],
                 [pltpu.SemaphoreType.DMA((7,)), pltpu.SemaphoreType.DMA((7,)), pltpu.SemaphoreType.DMA(())], make)


def _merge_plans(a, b):
    na_in, na_out, na_sems = len(a.arrays), len(a.out_shape), len(a.sems)

    def make(ins, outs, sems):
        start_a, finish_a = a.make(ins[:na_in], outs[:na_out], sems[:na_sems])
        start_b, finish_b = b.make(ins[na_in:], outs[na_out:], sems[na_sems:])

        def start():
            start_a()
            start_b()

        def finish():
            finish_a()
            finish_b()

        return start, finish

    return _Plan(list(a.arrays) + list(b.arrays), list(a.out_shape) + list(b.out_shape),
                 list(a.sems) + list(b.sems), make)


def _exchange(plan, name):
    n_in, n_out = len(plan.arrays), len(plan.out_shape)

    def body(*refs):
        start, finish = plan.make(refs[:n_in], refs[n_in:n_in + n_out], refs[n_in + n_out:])
        start()
        finish()

    return pl.pallas_call(
        body, name=name, in_specs=[HBM_SPEC] * n_in, out_specs=[HBM_SPEC] * n_out, out_shape=list(plan.out_shape),
        scratch_shapes=list(plan.sems), compiler_params=pltpu.CompilerParams(has_side_effects=True),
    )(*plan.arrays)


def _pack_small(parts):
    rows = []
    for r in range(SMALL_ROWS):
        pieces, col = [], 0
        for name, row, start, size in SMALL_PARTS:
            if row == r:
                assert start == col
                pieces.append(parts[name].reshape(1, size).astype(F32))
                col += size
        rows.append(jnp.concatenate(pieces + [jnp.zeros((1, D - col), F32)], axis=1))
    return jnp.concatenate(rows, axis=0)


def _unpack_small(slab, shapes):
    return {name: slab[row, col:col + size].reshape(shapes[name]) for name, row, col, size in SMALL_PARTS}


def _columns(gathered):
    return jnp.concatenate([gathered[j] for j in range(N_CHIPS)], axis=1)


def kernel(x, meta_tokens, norm_mix_g, w_in, conv_w, conv_b, conv_ln_g, conv_ln_b, gla_w_gate2, gla_gate_b, gla_norm_g, w_out, norm_ffn_g, w_ffn_gate, w_ffn_up, w_ffn_down, norm_final_g, loss_target, m_meta_tokens, m_norm_mix_g, m_w_in, m_conv_w, m_conv_b, m_conv_ln_g, m_conv_ln_b, m_gla_w_gate2, m_gla_gate_b, m_gla_norm_g, m_w_out, m_norm_ffn_g, m_w_ffn_gate, m_w_ffn_up, m_w_ffn_down, m_norm_final_g, v_meta_tokens, v_norm_mix_g, v_w_in, v_conv_w, v_conv_b, v_conv_ln_g, v_conv_ln_b, v_gla_w_gate2, v_gla_gate_b, v_gla_norm_g, v_w_out, v_norm_ffn_g, v_w_ffn_gate, v_w_ffn_up, v_w_ffn_down, v_norm_final_g):
    ws = dict(zip(WEIGHT_NAMES, (meta_tokens, norm_mix_g, w_in, conv_w, conv_b, conv_ln_g, conv_ln_b, gla_w_gate2,
                                 gla_gate_b, gla_norm_g, w_out, norm_ffn_g, w_ffn_gate, w_ffn_up, w_ffn_down,
                                 norm_final_g)))
    ms = dict(zip(WEIGHT_NAMES, (m_meta_tokens, m_norm_mix_g, m_w_in, m_conv_w, m_conv_b, m_conv_ln_g, m_conv_ln_b,
                                 m_gla_w_gate2, m_gla_gate_b, m_gla_norm_g, m_w_out, m_norm_ffn_g, m_w_ffn_gate,
                                 m_w_ffn_up, m_w_ffn_down, m_norm_final_g)))
    vs = dict(zip(WEIGHT_NAMES, (v_meta_tokens, v_norm_mix_g, v_w_in, v_conv_w, v_conv_b, v_conv_ln_g, v_conv_ln_b,
                                 v_gla_w_gate2, v_gla_gate_b, v_gla_norm_g, v_w_out, v_norm_ffn_g, v_w_ffn_gate,
                                 v_w_ffn_up, v_w_ffn_down, v_norm_final_g)))
    c = lax.axis_index("c")
    mine = 2 * lax.axis_index("x") + lax.axis_index("y")
    shard = lambda d, name: d[name].reshape(d[name].shape[-2:])
    vec = {name: ws[name].reshape(1, -1) for name, _, _, _ in SMALL_PARTS}
    n_ex, seq, _ = x.shape
    lp = HEAD_ROWS + seq
    t = n_ex * lp

    (tgt, h0), (w_in_g, meta_g, conv_w_g, w2_g) = _pad_head_rows([loss_target, x], plan=_gather_plan(
        [shard(ws, "w_in").T.astype(BF16)],
        [shard(ws, "meta_tokens"), shard(ws, "conv_w"), shard(ws, "gla_w_gate2")], axes=[1]))
    w_in_t = jnp.concatenate([w_in_g.reshape(D_IN, D), jnp.zeros((D_IN_PAD - D_IN, D), BF16)], axis=0)
    conv_w_full = jnp.concatenate([_columns(conv_w_g), jnp.zeros((32 - CONV_W, C_CONV), F32)], axis=0)
    w2_full = jnp.concatenate([_columns(w2_g), jnp.zeros((128 - RANK, GLA_K), F32)], axis=0).astype(BF16)
    h0 = _set_meta_rows(h0, _columns(meta_g)).reshape(t, D)
    tgt = tgt.reshape(t, D)
    row_mask = jnp.concatenate([jnp.zeros((n_ex, HEAD_ROWS, 1), F32), jnp.ones((n_ex, seq, 1), F32)],
                               axis=1).reshape(t, 1)

    (u, hn), (w_out_g,) = _in_proj(h0, vec["norm_mix_g"], w_in_t.T,
                                   plan=_gather_plan([shard(ws, "w_out").astype(BF16)]))
    (yc, y_conv), (gate_g,) = _conv_fwd(
        u, conv_w_full, vec["conv_b"], vec["conv_ln_g"], vec["conv_ln_b"], n_ex, lp,
        plan=_gather_plan([shard(ws, "w_ffn_gate").T.astype(BF16)]))
    (y_gla, states), (up_g,) = _gla_fwd(u, w2_full, vec["gla_gate_b"], vec["gla_norm_g"], n_ex, lp,
                                        plan=_gather_plan([shard(ws, "w_ffn_up").T.astype(BF16)]))
    w_out_full = w_out_g.reshape(D, D)
    w_gate_t, w_up_t = gate_g.reshape(D_FF, D), up_g.reshape(D_FF, D)
    (h1, hn2, gate, up, act), (down_g,) = _mix_out_ffn_up(
        h0, y_conv, y_gla, w_out_full, vec["norm_ffn_g"], w_gate_t.T, w_up_t.T,
        plan=_gather_plan([shard(ws, "w_ffn_down").astype(BF16)]))
    w_down_full = down_g.reshape(D_FF, D)
    dh2, loss, d_final_g = _ffn_down_loss(act, w_down_full, h1, tgt, vec["norm_final_g"], row_mask)
    dgate, dup, dh1, dycat, d_ffn_g = _ffn_bwd(dh2, gate, up, h1, w_down_full.T, w_gate_t, w_up_t, w_out_full.T,
                                                vec["norm_ffn_g"])

    early = ("w_ffn_gate", "w_ffn_up", "w_ffn_down", "w_out")
    ffn_block = lambda g: g.reshape(N_CHIPS, D_FF // N_CHIPS, D)
    g_gate = ffn_block(_wgrad(dgate, hn2, "wgrad_gate")[0])
    g_up, (gate_sib,) = _wgrad(dup, hn2, "wgrad_up", _to_sibling_plan([g_gate]))
    g_up = ffn_block(g_up)
    g_down, (up_sib,) = _wgrad(act, dh2, "wgrad_down", _to_sibling_plan([g_up]))
    g_down = ffn_block(g_down)
    g_out = jnp.concatenate([_wgrad(y_conv, dh1, "wgrad_out_conv")[0], _wgrad(y_gla, dh1, "wgrad_out_gla")[0]],
                            axis=0).reshape(N_CHIPS, D // N_CHIPS, D)
    cs_gate = _rs_add_halves(g_gate, gate_sib, c, "rs_add_w_ffn_gate")
    cs_up = _rs_add_halves(g_up, up_sib, c, "rs_add_w_ffn_up")
    (du_conv, d_conv_w, d_conv_b, d_ln_g, d_ln_b), (ex_gate, ex_up, down_sib, out_sib) = _conv_bwd(
        dycat, yc, u, conv_w_full, vec["conv_ln_g"], vec["conv_ln_b"], n_ex, lp,
        plan=_merge_plans(_chip_exchange_plan([cs_gate, cs_up]), _to_sibling_plan([g_down, g_out])))
    cs_down = _rs_add_halves(g_down, down_sib, c, "rs_add_w_ffn_down")
    cs_out = _rs_add_halves(g_out, out_sib, c, "rs_add_w_out")
    (du_gla, d_w2, d_gate_b, d_norm_g), (ex_down, ex_out) = _gla_bwd(
        dycat, u, states, w2_full, vec["gla_gate_b"], vec["gla_norm_g"], n_ex, lp,
        plan=_chip_exchange_plan([cs_down, cs_out]))
    halves = [_rs_sum(own, oth, mine, "rs_sum_" + nm)
              for own, oth, nm in zip((cs_gate, cs_up, cs_down, cs_out), (ex_gate, ex_up, ex_down, ex_out), early)]

    small = {"norm_mix_g": jnp.zeros((1, D), F32), "norm_ffn_g": d_ffn_g, "norm_final_g": d_final_g,
             "conv_b": d_conv_b, "conv_ln_g": d_ln_g, "conv_ln_b": d_ln_b, "gla_gate_b": d_gate_b,
             "gla_norm_g": d_norm_g}
    part = lax.dynamic_update_slice(_pack_small(small), loss[:, :1], (LOSS_ROW, 0))
    part = jnp.concatenate([part, jnp.zeros((N_META, D), F32), d_conv_w.reshape(16, D), d_w2[:RANK].reshape(4, D),
                            jnp.zeros((4, D), F32)], axis=0)
    g_in_gla, (slots,) = _wgrad(du_gla, hn, "wgrad_in_gla", _all_to_all_plan(part))

    d_w_in_t = jnp.concatenate([_wgrad(du_conv, hn, "wgrad_in_conv")[0], g_in_gla],
                               axis=0)[:D_IN].reshape(N_CHIPS, D_IN // N_CHIPS, D)
    (in_from_sibling,) = _exchange(_to_sibling_plan([d_w_in_t]), "rs_late_to_sibling")
    in_chip_sum = _rs_add_halves(d_w_in_t, in_from_sibling, c, "rs_add_w_in")
    (dh0, d_mix_g), shared = _in_proj_bwd(
        du_conv, du_gla, w_in_t[:2 * C_CONV], w_in_t[2 * C_CONV:], h0, dh1, vec["norm_mix_g"],
        plan=_merge_plans(_share_plan(halves), _chip_exchange_plan([in_chip_sum])))
    dh0 = dh0.reshape(n_ex, lp, D)
    grad_x = dh0[:, HEAD_ROWS:]
    late_part = jnp.concatenate([d_mix_g, jnp.zeros((SMALL_ROWS - 1, D), F32),
                                 jnp.sum(dh0[:, PAD_ROWS:HEAD_ROWS], axis=0)], axis=0)
    in_half = _rs_sum(in_chip_sum, shared[4], mine, "rs_sum_w_in")
    in_shared, late_slots = _exchange(_merge_plans(_share_plan([in_half]), _all_to_all_plan(late_part)),
                                      "late_exchange")

    out = {"grad": {}, "delta": {}, "new_m": {}, "new_v": {}}

    def record(name, res, transposed=False):
        for kind, a in zip(("grad", "delta", "new_m", "new_v"), res):
            out[kind][name] = (a.T if transposed else a).reshape(ws[name].shape)

    def operands(name, transposed):
        lay = (lambda a: a.T) if transposed else (lambda a: a)
        return lay(shard(ws, name)), lay(shard(ms, name)), lay(shard(vs, name))

    early_layout = (("w_ffn_gate", True), ("w_ffn_up", True), ("w_ffn_down", False), ("w_out", False))
    items = [(mine_half, their_half, *operands(name, transposed))
             for (name, transposed), mine_half, their_half in zip(early_layout, halves, shared)]
    for (name, transposed), res in zip(early_layout, _adamw_halves(items, c, "adamw_early")):
        record(name, res, transposed)

    record("w_in", _adamw_halves([(in_half, in_shared, *operands("w_in", True))], c, "adamw_w_in")[0], True)

    tall = lambda a: jnp.concatenate([a, jnp.zeros((part.shape[0] - SMALL_ROWS, D), F32)], axis=0)
    g_s, d_s, m_s, v_s = _sum_slots_adamw(slots, late_slots, tall(_pack_small(ws)), tall(_pack_small(ms)),
                                          tall(_pack_small(vs)))
    small_shapes = {name: ws[name].shape for name, _, _, _ in SMALL_PARTS}
    for kind, slab in (("grad", g_s), ("delta", d_s), ("new_m", m_s), ("new_v", v_s)):
        out[kind].update(_unpack_small(slab, small_shapes))
    loss = g_s[LOSS_ROW, 0]
    block = lambda a, width: lax.dynamic_slice_in_dim(a, mine * width, width, axis=1)
    small_sharded = {"meta_tokens": block(g_s[8:24], D // N_CHIPS),
                     "conv_w": block(g_s[24:40].reshape(32, C_CONV), C_CONV // N_CHIPS)[:CONV_W],
                     "gla_w_gate2": block(g_s[40:44].reshape(RANK, GLA_K), GLA_K // N_CHIPS)}
    for name, g in small_sharded.items():
        record(name, [g, *_adamw(g, *operands(name, False), "adamw_" + name)])

    return (loss, grad_x, *[out[kind][name] for kind in ("grad", "delta", "new_m", "new_v") for name in WEIGHT_NAMES])
```

```python
import functools
from typing import Any, Callable, NamedTuple, Sequence

import jax
import jax.numpy as jnp
from jax import lax
from jax.experimental import pallas as pl
from jax.experimental.pallas import tpu as pltpu

F32 = jnp.float32
BF16 = jnp.bfloat16
MESH = pl.DeviceIdType.MESH

D = 1024
N_META = 16
C_CONV = 512
CONV_W = 31
GLA_K = 256
GLA_V = 512
N_HEADS = 4
DK = 64
DV = 128
RANK = 16
CHUNK = 64
PAD_ROWS = CHUNK - N_META
HEAD_ROWS = CHUNK
D_IN = 2576
D_IN_PAD = 2688
D_GLA_IN = D_IN_PAD - 2 * C_CONV
D_FF = 2816
RMS_EPS = 1e-6
LN_EPS = 1e-5
GATE_TAU = 16.0
N_CHIPS = 4

ADAM_LR = 0.001
ADAM_B1 = 0.9
ADAM_B2 = 0.999
ADAM_EPS = 1e-08
ADAM_WD = 0.01
ADAM_STEP = 10

V7X_VMEM_BYTES = 64 * 1024 * 1024
VMEM_LIMIT = V7X_VMEM_BYTES - 8 * 1024 * 1024
SUBLANES = 8
ROW_PART = 128
FFN_BWD_TILE = 192

WEIGHT_NAMES = ("meta_tokens", "norm_mix_g", "w_in", "conv_w", "conv_b", "conv_ln_g", "conv_ln_b", "gla_w_gate2",
                "gla_gate_b", "gla_norm_g", "w_out", "norm_ffn_g", "w_ffn_gate", "w_ffn_up", "w_ffn_down",
                "norm_final_g")

SMALL_ROWS = 8
SMALL_PARTS = (("norm_mix_g", 0, 0, D), ("norm_ffn_g", 1, 0, D), ("norm_final_g", 2, 0, D),
               ("conv_b", 3, 0, C_CONV), ("conv_ln_g", 3, C_CONV, C_CONV), ("conv_ln_b", 4, 0, C_CONV),
               ("gla_gate_b", 4, C_CONV, GLA_K), ("gla_norm_g", 4, C_CONV + GLA_K, DV))
LOSS_ROW = 5

HBM_SPEC = pl.BlockSpec(memory_space=pltpu.HBM)


def _dot(a, b):
    return jnp.dot(a, b, preferred_element_type=F32)


def _dot_nt(a, b):
    return lax.dot_general(a, b, (((1,), (1,)), ((), ())), preferred_element_type=F32)


def _dot_tn(a, b):
    return lax.dot_general(a, b, (((0,), (0,)), ((), ())), preferred_element_type=F32)


def _sigmoid(x):
    return 1.0 / (1.0 + jnp.exp(-x))


def _const_spec(shape):
    return pl.BlockSpec(shape, lambda *_: (0,) * len(shape), pipeline_mode=pl.Buffered(1))


def _acc_spec(shape):
    return pl.BlockSpec(shape, lambda *_: (0,) * len(shape))


def _params(n_axes):
    return pltpu.CompilerParams(dimension_semantics=("arbitrary",) * n_axes, vmem_limit_bytes=VMEM_LIMIT)


def _row_tile(t, want):
    for r in (want, 384, 192, 128, 64):
        if r <= want and t % r == 0:
            return r
    raise ValueError(f"no row tile for {t}")


def _row_parts(r):
    if r % ROW_PART:
        return [slice(None)]
    return [pl.ds(i * ROW_PART, ROW_PART) for i in range(r // ROW_PART)]


def _in_lockstep(bodies):
    live = list(bodies)
    while live:
        still = []
        for g in live:
            try:
                next(g)
                still.append(g)
            except StopIteration:
                pass
        live = still


class _Plan(NamedTuple):
    arrays: Sequence[Any]
    out_shape: Sequence[Any]
    sems: Sequence[Any]
    make: Callable


def _call(body, *, name, grid, in_specs, out_specs, out_shape, scratch_shapes=(), plan=None):
    n_in, n_out, n_scr = len(in_specs), len(out_specs), len(scratch_shapes)
    if plan is None:
        plan = _Plan([], [], [], lambda ins, outs, sems: (lambda: None, lambda: None))
    nx_in, nx_out = len(plan.arrays), len(plan.out_shape)

    def hosted(*refs):
        ins, xins = refs[:n_in], refs[n_in:n_in + nx_in]
        o0 = n_in + nx_in
        outs, xouts = refs[o0:o0 + n_out], refs[o0 + n_out:o0 + n_out + nx_out]
        s0 = o0 + n_out + nx_out
        scr, sems = refs[s0:s0 + n_scr], refs[s0 + n_scr:]
        ids = [pl.program_id(a) for a in range(len(grid))]
        first = functools.reduce(jnp.logical_and, [i == 0 for i in ids])
        last = functools.reduce(jnp.logical_and, [i == g - 1 for i, g in zip(ids, grid)])
        start, finish = plan.make(xins, xouts, sems)
        pl.when(first)(start)
        body(*ins, *outs, *scr)
        pl.when(last)(finish)

    call = pl.pallas_call(
        hosted, name=name, grid=grid, in_specs=list(in_specs) + [HBM_SPEC] * nx_in,
        out_specs=list(out_specs) + [HBM_SPEC] * nx_out, out_shape=list(out_shape) + list(plan.out_shape),
        scratch_shapes=list(scratch_shapes) + list(plan.sems),
        compiler_params=pltpu.CompilerParams(dimension_semantics=("arbitrary",) * len(grid),
                                             vmem_limit_bytes=VMEM_LIMIT, has_side_effects=nx_in > 0))

    def run(*args):
        res = call(*args, *plan.arrays)
        return res[:n_out], res[n_out:]

    return run


def _pad_head_rows(arrays, plan=None):
    n_ex, seq, _ = arrays[0].shape
    nc = (HEAD_ROWS + seq) // CHUNK
    n = len(arrays)

    def body(*refs):
        for a_ref, o_ref in zip(refs[:n], refs[n:]):
            o_ref[...] = jnp.where(pl.program_id(0) > 0, a_ref[...], 0.0)

    return _call(
        body, name="pad_head_rows", grid=(nc,),
        in_specs=[pl.BlockSpec((n_ex, CHUNK, D), lambda i: (0, jnp.maximum(i - 1, 0), 0))] * n,
        out_specs=[pl.BlockSpec((n_ex, CHUNK, D), lambda i: (0, i, 0))] * n,
        out_shape=[jax.ShapeDtypeStruct((n_ex, HEAD_ROWS + seq, D), F32)] * n,
        plan=plan,
    )(*arrays)


def _set_meta_rows(h0, meta):
    n_ex = h0.shape[0]

    def body(h_ref, meta_ref, o_ref):
        o_ref[...] = jnp.concatenate(
            [h_ref[:, :PAD_ROWS, :], jnp.broadcast_to(meta_ref[...][None], (n_ex, N_META, D))], axis=1)

    head = pl.BlockSpec((n_ex, HEAD_ROWS, D), lambda i: (0, 0, 0))
    return pl.pallas_call(
        body, name="set_meta_rows", grid=(1,), in_specs=[head, pl.BlockSpec((N_META, D), lambda i: (0, 0))],
        out_specs=head, out_shape=jax.ShapeDtypeStruct(h0.shape, F32), input_output_aliases={0: 0},
        compiler_params=_params(1),
    )(h0, meta)


def _in_proj(h0, g_mix, w_in, plan=None):
    t = h0.shape[0]
    r = _row_tile(t, 384)

    def body(h_ref, g_ref, w_ref, u_ref, hn_ref):
        h = h_ref[...]
        rstd = lax.rsqrt(jnp.mean(h * h, axis=-1, keepdims=True) + RMS_EPS)
        hn = (h * rstd * g_ref[...]).astype(BF16)
        hn_ref[...] = hn
        u_ref[...] = _dot(hn, w_ref[...])

    return _call(
        body, name="in_proj", grid=(t // r,),
        in_specs=[pl.BlockSpec((r, D), lambda i: (i, 0)), _const_spec((1, D)), _const_spec((D, D_IN_PAD))],
        out_specs=[pl.BlockSpec((r, D_IN_PAD), lambda i: (i, 0)), pl.BlockSpec((r, D), lambda i: (i, 0))],
        out_shape=[jax.ShapeDtypeStruct((t, D_IN_PAD), F32), jax.ShapeDtypeStruct((t, D), BF16)],
        plan=plan,
    )(h0, g_mix, w_in)


CONV_TILE = 192
CONV_SUB = 32
CONV_LEAD = CONV_SUB - (CONV_W - 1)


def _shifted_copies(src, dst, r):
    for s in range(1, SUBLANES):
        dst[s - 1] = src[s:s + r + CONV_SUB - SUBLANES, :]


def _shifted_rows(src, shifted, start):
    base, s = SUBLANES * (start // SUBLANES), start % SUBLANES
    if s == 0:
        return src[base:base + CONV_SUB, :]
    return shifted[s - 1, base:base + CONV_SUB, :]


def _conv_fwd(u, conv_w, conv_b, ln_g, ln_b, n_ex, lp, plan=None):
    r = CONV_TILE
    nt = lp // r
    hb = r // CONV_SUB

    def body(cur_ref, prev_ref, w_ref, b_ref, lg_ref, lb_ref, yc_ref, y_ref, glu, glu_sh):
        i = pl.program_id(1)
        cur = cur_ref[...]
        glu[CONV_SUB:CONV_SUB + r, :] = cur[:, :C_CONV] * _sigmoid(cur[:, C_CONV:])
        pv = prev_ref[...]
        halo = pv[:, :C_CONV] * _sigmoid(pv[:, C_CONV:])
        glu[0:CONV_SUB, :] = jnp.where(i > 0, halo, 0.0)
        _shifted_copies(glu, glu_sh, r)
        w = w_ref[...]
        for j in range(r // CONV_SUB):
            r0 = j * CONV_SUB
            acc = jnp.zeros((CONV_SUB, C_CONV), F32) + b_ref[...]
            for k in range(CONV_W):
                acc = acc + w[k:k + 1, :] * _shifted_rows(glu, glu_sh, r0 + CONV_LEAD + k)
            mu = jnp.mean(acc, axis=-1, keepdims=True)
            cen = acc - mu
            var = jnp.mean(cen * cen, axis=-1, keepdims=True)
            out = cen * lax.rsqrt(var + LN_EPS) * lg_ref[...] + lb_ref[...]
            y = out * _sigmoid(out)
            row = i * r + r0 + lax.broadcasted_iota(jnp.int32, (CONV_SUB, 1), 0)
            y = jnp.where(row >= PAD_ROWS, y, 0.0)
            yc_ref[r0:r0 + CONV_SUB, :] = acc
            y_ref[r0:r0 + CONV_SUB, :] = y.astype(BF16)

    t = n_ex * lp
    return _call(
        body, name="conv_fwd", grid=(n_ex, nt),
        in_specs=[pl.BlockSpec((r, 2 * C_CONV), lambda b, i: (b * nt + i, 0)),
                  pl.BlockSpec((CONV_SUB, 2 * C_CONV), lambda b, i: (jnp.maximum((b * nt + i) * hb - 1, 0), 0)),
                  _const_spec((32, C_CONV)), _const_spec((1, C_CONV)), _const_spec((1, C_CONV)), _const_spec((1, C_CONV))],
        out_specs=[pl.BlockSpec((r, C_CONV), lambda b, i: (b * nt + i, 0)),
                   pl.BlockSpec((r, C_CONV), lambda b, i: (b * nt + i, 0))],
        out_shape=[jax.ShapeDtypeStruct((t, C_CONV), F32), jax.ShapeDtypeStruct((t, C_CONV), BF16)],
        scratch_shapes=[pltpu.VMEM((r + CONV_SUB, C_CONV), F32),
                        pltpu.VMEM((SUBLANES - 1, r + CONV_SUB - SUBLANES, C_CONV), F32)],
        plan=plan,
    )(u, u, conv_w, conv_b, ln_g, ln_b)


def _mix_out_ffn_up(h0, y_conv, y_gla, w_out, g_ffn, w_gate, w_up, plan=None):
    t = h0.shape[0]
    r = _row_tile(t, 384)

    def body(h0_ref, yc_ref, yg_ref, wo_ref, g_ref, wg_ref, wu_ref, h1_ref, hn_ref, gate_ref, up_ref, act_ref):
        h1 = h0_ref[...] + _dot(yc_ref[...], wo_ref[0:C_CONV, :]) + _dot(yg_ref[...], wo_ref[C_CONV:D, :])
        h1_ref[...] = h1
        rstd = lax.rsqrt(jnp.mean(h1 * h1, axis=-1, keepdims=True) + RMS_EPS)
        hn = (h1 * rstd * g_ref[...]).astype(BF16)
        hn_ref[...] = hn
        gate = _dot(hn, wg_ref[...])
        up = _dot(hn, wu_ref[...])
        gate_ref[...] = gate
        up_ref[...] = up
        act_ref[...] = (gate * _sigmoid(gate) * up).astype(BF16)

    rows = lambda w: pl.BlockSpec((r, w), lambda i: (i, 0))
    return _call(
        body, name="mix_out_ffn_up", grid=(t // r,),
        in_specs=[rows(D), rows(C_CONV), rows(GLA_V), _const_spec((D, D)), _const_spec((1, D)),
                  _const_spec((D, D_FF)), _const_spec((D, D_FF))],
        out_specs=[rows(D), rows(D), rows(D_FF), rows(D_FF), rows(D_FF)],
        out_shape=[jax.ShapeDtypeStruct((t, D), F32), jax.ShapeDtypeStruct((t, D), BF16),
                   jax.ShapeDtypeStruct((t, D_FF), F32), jax.ShapeDtypeStruct((t, D_FF), F32),
                   jax.ShapeDtypeStruct((t, D_FF), BF16)],
        plan=plan,
    )(h0, y_conv, y_gla, w_out, g_ffn, w_gate, w_up)


def _ffn_down_loss(act, w_down, h1, target, g_final, row_mask):
    t = h1.shape[0]
    r = _row_tile(t, 384)

    def body(act_ref, wd_ref, h1_ref, tgt_ref, gf_ref, mask_ref, dh2_ref, loss_ref, dgf_ref):
        @pl.when(pl.program_id(0) == 0)
        def _():
            loss_ref[...] = jnp.zeros_like(loss_ref)
            dgf_ref[...] = jnp.zeros_like(dgf_ref)

        gf = gf_ref[...]

        def part(rows):
            h2 = h1_ref[rows, :] + _dot(act_ref[rows, :], wd_ref[...])
            yield
            rstd = lax.rsqrt(jnp.mean(h2 * h2, axis=-1, keepdims=True) + RMS_EPS)
            nrm = h2 * rstd
            err = (nrm * gf - tgt_ref[rows, :]) * mask_ref[rows, :]
            loss_ref[...] += jnp.sum(err * err) * (0.5 / D)
            dy = err * (1.0 / D)
            dgf_ref[...] += jnp.sum(dy * nrm, axis=0, keepdims=True)
            dn = dy * gf
            dh2_ref[rows, :] = rstd * (dn - nrm * jnp.mean(dn * nrm, axis=-1, keepdims=True))

        _in_lockstep(part(rows) for rows in _row_parts(r))

    rows = lambda w: pl.BlockSpec((r, w), lambda i: (i, 0))
    return pl.pallas_call(
        body, name="ffn_down_loss", grid=(t // r,),
        in_specs=[rows(D_FF), _const_spec((D_FF, D)), rows(D), rows(D), _const_spec((1, D)), rows(1)],
        out_specs=[rows(D), _acc_spec((1, 128)), _acc_spec((1, D))],
        out_shape=[jax.ShapeDtypeStruct((t, D), F32), jax.ShapeDtypeStruct((1, 128), F32),
                   jax.ShapeDtypeStruct((1, D), F32)],
        compiler_params=_params(1),
    )(act, w_down, h1, target, g_final, row_mask)


def _ffn_bwd(dh2, gate, up, h1, w_down_t, w_gate_t, w_up_t, w_out_t, g_ffn):
    t = h1.shape[0]
    r = _row_tile(t, FFN_BWD_TILE)

    def body(dh2_ref, gate_ref, up_ref, h1_ref, wd_ref, wg_ref, wu_ref, wo_ref, g_ref,
             dgate_ref, dup_ref, dh1_ref, dycat_ref, dg_ref):
        @pl.when(pl.program_id(0) == 0)
        def _():
            dg_ref[...] = jnp.zeros_like(dg_ref)

        dh2 = dh2_ref[...]
        dact = _dot(dh2.astype(BF16), wd_ref[...])
        gate = gate_ref[...]
        sg = _sigmoid(gate)
        dgate = (dact * up_ref[...] * (sg * (1.0 + gate * (1.0 - sg)))).astype(BF16)
        dup = (dact * (gate * sg)).astype(BF16)
        dgate_ref[...] = dgate
        dup_ref[...] = dup
        dhn = _dot(dgate, wg_ref[...]) + _dot(dup, wu_ref[...])
        h1 = h1_ref[...]
        rstd = lax.rsqrt(jnp.mean(h1 * h1, axis=-1, keepdims=True) + RMS_EPS)
        nrm = h1 * rstd
        dg_ref[...] += jnp.sum(dhn * nrm, axis=0, keepdims=True)
        dn = dhn * g_ref[...]
        dh1 = dh2 + rstd * (dn - nrm * jnp.mean(dn * nrm, axis=-1, keepdims=True))
        dh1_ref[...] = dh1
        dycat_ref[...] = _dot(dh1.astype(BF16), wo_ref[...])

    rows = lambda w: pl.BlockSpec((r, w), lambda i: (i, 0))
    return pl.pallas_call(
        body, name="ffn_bwd", grid=(t // r,),
        in_specs=[rows(D), rows(D_FF), rows(D_FF), rows(D), _const_spec((D, D_FF)), _const_spec((D_FF, D)),
                  _const_spec((D_FF, D)), _const_spec((D, D)), _const_spec((1, D))],
        out_specs=[rows(D_FF), rows(D_FF), rows(D), rows(D), _acc_spec((1, D))],
        out_shape=[jax.ShapeDtypeStruct((t, D_FF), BF16), jax.ShapeDtypeStruct((t, D_FF), BF16),
                   jax.ShapeDtypeStruct((t, D), F32), jax.ShapeDtypeStruct((t, D), F32),
                   jax.ShapeDtypeStruct((1, D), F32)],
        compiler_params=_params(1),
    )(dh2, gate, up, h1, w_down_t, w_gate_t, w_up_t, w_out_t, g_ffn)


def _conv_bwd(dycat, yc, u, conv_w, ln_g, ln_b, n_ex, lp, plan=None):
    r = CONV_TILE
    nt = lp // r
    hb = r // CONV_SUB
    nsub = r // CONV_SUB

    def ln_bwd(dy, yc_rows, live, lg, lb):
        mu = jnp.mean(yc_rows, axis=-1, keepdims=True)
        cen = yc_rows - mu
        rs = lax.rsqrt(jnp.mean(cen * cen, axis=-1, keepdims=True) + LN_EPS)
        yn = cen * rs
        out = yn * lg + lb
        so = _sigmoid(out)
        dout = jnp.where(live, dy * (so * (1.0 + out * (1.0 - so))), 0.0)
        dyn = dout * lg
        dyc = rs * (dyn - jnp.mean(dyn, axis=-1, keepdims=True) - yn * jnp.mean(dyn * yn, axis=-1, keepdims=True))
        return dyc, dout, yn

    def body(dy_ref, dyn_ref, yc_ref, ycn_ref, cur_ref, prev_ref, w_ref, lg_ref, lb_ref,
             du_ref, dw_ref, db_ref, dlg_ref, dlb_ref, glu, dycs, dwacc, glu_sh, dycs_sh):
        b = pl.program_id(0)
        i = pl.program_id(1)
        first = jnp.logical_and(b == 0, i == 0)

        @pl.when(first)
        def _():
            dwacc[...] = jnp.zeros_like(dwacc)
            db_ref[...] = jnp.zeros_like(db_ref)
            dlg_ref[...] = jnp.zeros_like(dlg_ref)
            dlb_ref[...] = jnp.zeros_like(dlb_ref)

        lg, lb = lg_ref[...], lb_ref[...]
        cur = cur_ref[...]
        sig = _sigmoid(cur[:, C_CONV:])
        glu[CONV_SUB:CONV_SUB + r, :] = cur[:, :C_CONV] * sig
        pv = prev_ref[...]
        glu[0:CONV_SUB, :] = jnp.where(i > 0, pv[:, :C_CONV] * _sigmoid(pv[:, C_CONV:]), 0.0)

        row = i * r + lax.broadcasted_iota(jnp.int32, (r, 1), 0)
        dyc, dout, yn = ln_bwd(dy_ref[...], yc_ref[...], row >= PAD_ROWS, lg, lb)
        dycs[0:r, :] = dyc
        dycn, _, _ = ln_bwd(dyn_ref[...], ycn_ref[...], i < nt - 1, lg, lb)
        dycs[r:r + CONV_SUB, :] = dycn
        db_ref[...] += jnp.sum(dyc, axis=0, keepdims=True)
        dlg_ref[...] += jnp.sum(dout * yn, axis=0, keepdims=True)
        dlb_ref[...] += jnp.sum(dout, axis=0, keepdims=True)

        _shifted_copies(glu, glu_sh, r)
        _shifted_copies(dycs, dycs_sh, r)
        w = w_ref[...]
        for j in range(nsub):
            r0 = j * CONV_SUB
            dblk = dycs[r0:r0 + CONV_SUB, :]
            dglu = jnp.zeros((CONV_SUB, C_CONV), F32)
            for k in range(CONV_W):
                dglu = dglu + w[k:k + 1, :] * _shifted_rows(dycs, dycs_sh, r0 + (CONV_W - 1) - k)
                prod = dblk * _shifted_rows(glu, glu_sh, r0 + CONV_LEAD + k)
                dwacc[k] += prod.reshape(CONV_SUB // SUBLANES, SUBLANES, C_CONV).sum(axis=0)
            sg = sig[r0:r0 + CONV_SUB, :]
            cv = cur[r0:r0 + CONV_SUB, :C_CONV]
            du_ref[r0:r0 + CONV_SUB, :C_CONV] = (dglu * sg).astype(BF16)
            du_ref[r0:r0 + CONV_SUB, C_CONV:] = (dglu * cv * sg * (1.0 - sg)).astype(BF16)

        @pl.when(jnp.logical_and(b == n_ex - 1, i == nt - 1))
        def _():
            dw_ref[...] = jnp.sum(dwacc[...], axis=1)

    t = n_ex * lp
    cur_rows = lambda w, col: pl.BlockSpec((r, w), lambda b, i: (b * nt + i, col))
    nxt_rows = lambda w, col: pl.BlockSpec(
        (CONV_SUB, w), lambda b, i: (jnp.minimum((b * nt + i + 1) * hb, n_ex * nt * hb - 1), col))
    return _call(
        body, name="conv_bwd", grid=(n_ex, nt),
        in_specs=[cur_rows(C_CONV, 0), nxt_rows(C_CONV, 0), cur_rows(C_CONV, 0), nxt_rows(C_CONV, 0),
                  cur_rows(2 * C_CONV, 0),
                  pl.BlockSpec((CONV_SUB, 2 * C_CONV), lambda b, i: (jnp.maximum((b * nt + i) * hb - 1, 0), 0)),
                  _const_spec((32, C_CONV)), _const_spec((1, C_CONV)), _const_spec((1, C_CONV))],
        out_specs=[cur_rows(2 * C_CONV, 0), _acc_spec((32, C_CONV)), _acc_spec((1, C_CONV)),
                   _acc_spec((1, C_CONV)), _acc_spec((1, C_CONV))],
        out_shape=[jax.ShapeDtypeStruct((t, 2 * C_CONV), BF16), jax.ShapeDtypeStruct((32, C_CONV), F32),
                   jax.ShapeDtypeStruct((1, C_CONV), F32), jax.ShapeDtypeStruct((1, C_CONV), F32),
                   jax.ShapeDtypeStruct((1, C_CONV), F32)],
        scratch_shapes=[pltpu.VMEM((r + CONV_SUB, C_CONV), F32), pltpu.VMEM((r + CONV_SUB, C_CONV), F32),
                        pltpu.VMEM((32, SUBLANES, C_CONV), F32),
                        pltpu.VMEM((SUBLANES - 1, r + CONV_SUB - SUBLANES, C_CONV), F32),
                        pltpu.VMEM((SUBLANES - 1, r + CONV_SUB - SUBLANES, C_CONV), F32)],
        plan=plan,
    )(dycat, dycat, yc, yc, u, u, conv_w, ln_g, ln_b)


HEAD_ROWS_ALL = N_HEADS * CHUNK


def _gla_gates(lr, w2, gb, first_chunk):
    z = _dot(lr.astype(BF16), w2) + gb
    a = (jnp.minimum(z, 0.0) - jnp.log(1.0 + jnp.exp(-jnp.abs(z)))) * (1.0 / GATE_TAU)
    row = lax.broadcasted_iota(jnp.int32, (CHUNK, 1), 0)
    live = jnp.logical_or(jnp.logical_not(first_chunk), row >= PAD_ROWS)
    return z, jnp.where(live, a, 0.0), live


def _tri(lower):
    i = lax.broadcasted_iota(jnp.int32, (CHUNK, CHUNK), 0)
    j = lax.broadcasted_iota(jnp.int32, (CHUNK, CHUNK), 1)
    return (i >= j) if lower else (i <= j)


def _head_of(shape, axis, per_head):
    return lax.broadcasted_iota(jnp.int32, shape, axis) // per_head


def _expand(x, lanes_per_head):
    rows, lanes = HEAD_ROWS_ALL, x.shape[1]
    keep = _head_of((rows, lanes), 0, CHUNK) == _head_of((rows, lanes), 1, lanes_per_head)
    return jnp.where(keep, jnp.tile(x, (N_HEADS, 1)), 0.0)


def _expand_lanes(x):
    rows, w = x.shape
    keep = _head_of((rows, N_HEADS * w), 0, CHUNK) == _head_of((rows, N_HEADS * w), 1, w)
    return jnp.where(keep, jnp.tile(x, (1, N_HEADS)), 0.0)


def _expand_state(st):
    rows, lanes = N_HEADS * DV, st.shape[1]
    keep = _head_of((rows, lanes), 0, DV) == _head_of((rows, lanes), 1, DK)
    return jnp.where(keep, jnp.tile(st, (N_HEADS, 1)), 0.0)


def _fold(t, rows_per_head):
    lane_head = _head_of((rows_per_head, t.shape[1]), 1, DK)
    out = jnp.where(lane_head == 0, t[0:rows_per_head], 0.0)
    for h in range(1, N_HEADS):
        out = out + jnp.where(lane_head == h, t[h * rows_per_head:(h + 1) * rows_per_head], 0.0)
    return out


def _rows_by_head(x):
    return jnp.concatenate([x[:, h * DV:(h + 1) * DV] for h in range(N_HEADS)], axis=0)


def _lanes_by_head(x):
    return jnp.concatenate([x[h * CHUNK:(h + 1) * CHUNK] for h in range(N_HEADS)], axis=1)


def _running_sum(a, lower):
    hi = a.astype(BF16)
    rest = a - hi.astype(F32)
    mid = rest.astype(BF16)
    lo = (rest - mid.astype(F32)).astype(BF16)
    w = a.shape[1]
    parts = _dot(_tri(lower).astype(F32).astype(BF16), jnp.concatenate([hi, mid, lo], axis=1))
    return parts[:, :w] + parts[:, w:2 * w] + parts[:, 2 * w:]


def _stacked_causal():
    i = lax.broadcasted_iota(jnp.int32, (HEAD_ROWS_ALL, CHUNK), 0) % CHUNK
    j = lax.broadcasted_iota(jnp.int32, (HEAD_ROWS_ALL, CHUNK), 1)
    return i >= j


GLA_GROUP = 3


def _gla_chunk(q, k, v, lr, w2, gb, first_chunk):
    z, a, live = _gla_gates(lr, w2, gb, first_chunk)
    yield
    b = _running_sum(a, True)
    yield
    bl = b[CHUNK - 1:CHUNK, :]
    e_pos, e_neg, e_dec = jnp.exp(b), jnp.exp(-b), jnp.exp(bl - b)
    q_f, k_f, kd_f = q * (DK ** -0.5) * e_pos, k * e_neg, k * e_dec
    qx = _expand(q_f, DK).astype(BF16)
    k_in, k_dec, v_b = k_f.astype(BF16), kd_f.astype(BF16), v.astype(BF16)
    s = jnp.where(_stacked_causal(), _dot_nt(qx, k_in), 0.0).astype(BF16)
    yield
    p = _dot(s, v_b)
    yield
    o_intra = jnp.concatenate([p[h * CHUNK:(h + 1) * CHUNK, h * DV:(h + 1) * DV] for h in range(N_HEADS)], axis=0)
    return dict(z=z, live=live, bl=bl, e_pos=e_pos, e_neg=e_neg, e_dec=e_dec, q_f=q_f, k_f=k_f, kd_f=kd_f,
                qx=qx, k_in=k_in, k_dec=k_dec, v_b=v_b, s=s, o_intra=o_intra, decay=jnp.exp(bl))


def _gla_fwd(u, w2, gb, ng, n_ex, lp, plan=None):
    nc = lp // CHUNK
    t = n_ex * lp
    rows_of = lambda j: pl.ds(j * CHUNK, CHUNK)

    def body(qk_ref, v_ref, g_ref, lr_ref, w2_ref, gb_ref, ng_ref, y_ref, st_ref, state):
        n = pl.program_id(0)

        @pl.when(n == 0)
        def _():
            state[...] = jnp.zeros_like(state)

        carried = [state[e] for e in range(n_ex)]

        def one_chunk(e, j):
            rows = rows_of(j)
            qk = qk_ref[e, rows, :]
            first = jnp.logical_and(n == 0, j == 0)
            c = yield from _gla_chunk(qk[:, :GLA_K], qk[:, GLA_K:], v_ref[e, rows, :], lr_ref[e, rows, :],
                                      w2_ref[...], gb_ref[...], first)
            kv = _fold(_dot_tn(c["v_b"], c["k_dec"]), DV)
            g = _rows_by_head(g_ref[e, rows, :])
            gate = ng_ref[...] * (g * _sigmoid(g))
            yield
            for _ in range(j):
                yield
            st = carried[e]
            st_ref[e, pl.ds(j * DV, DV), :] = st
            o = c["o_intra"] + _dot_nt(c["qx"], st.astype(BF16))
            rstd = lax.rsqrt(jnp.mean(o * o, axis=-1, keepdims=True) + RMS_EPS)
            y_ref[e, rows, :] = _lanes_by_head(o * rstd * gate).astype(BF16)
            carried[e] = c["decay"] * st + kv

        _in_lockstep(one_chunk(e, j) for j in range(GLA_GROUP) for e in range(n_ex))
        for e in range(n_ex):
            state[e] = carried[e]

    u3 = u.reshape(n_ex, lp, D_IN_PAD)
    blk = lambda w, col: pl.BlockSpec((n_ex, GLA_GROUP * CHUNK, w), lambda n: (0, n, col))
    (y, states), extra = _call(
        body, name="gla_fwd", grid=(nc // GLA_GROUP,),
        in_specs=[blk(2 * GLA_K, 2), blk(GLA_V, 3), blk(GLA_V, 4), blk(128, 20),
                  _const_spec((128, GLA_K)), _const_spec((1, GLA_K)), _const_spec((1, DV))],
        out_specs=[blk(GLA_V, 0), pl.BlockSpec((n_ex, GLA_GROUP * DV, GLA_K), lambda n: (0, n, 0))],
        out_shape=[jax.ShapeDtypeStruct((n_ex, lp, GLA_V), BF16),
                   jax.ShapeDtypeStruct((n_ex, nc * DV, GLA_K), F32)],
        scratch_shapes=[pltpu.VMEM((n_ex, DV, GLA_K), F32)],
        plan=plan,
    )(u3, u3, u3, u3, w2, gb, ng)
    return (y.reshape(t, GLA_V), states), extra


def _gla_bwd(dycat, u, states, w2, gb, ng, n_ex, lp, plan=None):
    nc = lp // CHUNK
    t = n_ex * lp

    def body(dy_ref, qk_ref, v_ref, g_ref, lr_ref, st_ref, w2_ref, gb_ref, ng_ref,
             du_ref, dw2_ref, dgb_ref, dng_ref, dstate):
        n = pl.program_id(0)
        group = nc // GLA_GROUP - 1 - n

        @pl.when(n == 0)
        def _():
            dw2_ref[...] = jnp.zeros_like(dw2_ref)
            dgb_ref[...] = jnp.zeros_like(dgb_ref)
            dng_ref[...] = jnp.zeros_like(dng_ref)
            dstate[...] = jnp.zeros_like(dstate)

        carried = [dstate[e] for e in range(n_ex)]

        def one_chunk(e, order):
            j = GLA_GROUP - 1 - order
            rows = pl.ds(j * CHUNK, CHUNK)
            qk = qk_ref[e, rows, :]
            lr = lr_ref[e, rows, :]
            st = st_ref[e, pl.ds(j * DV, DV), :]
            first = jnp.logical_and(group == 0, j == 0)
            c = yield from _gla_chunk(qk[:, :GLA_K], qk[:, GLA_K:], v_ref[e, rows, :], lr, w2_ref[...], gb_ref[...],
                                      first)
            qx, k_in, k_dec, v_b, s = c["qx"], c["k_in"], c["k_dec"], c["v_b"], c["s"]
            st_b = st.astype(BF16)
            o = c["o_intra"] + _dot_nt(qx, st_b)
            ngv = ng_ref[...]
            yield
            rstd = lax.rsqrt(jnp.mean(o * o, axis=-1, keepdims=True) + RMS_EPS)
            nrm = o * rstd
            g = _rows_by_head(g_ref[e, rows, :])
            dy = _rows_by_head(dy_ref[e, rows, :])
            sg = _sigmoid(g)
            dg = dy * nrm * ngv * (sg * (1.0 + g * (1.0 - sg)))
            dt = dy * (g * sg)
            dng_ref[...] += jnp.sum(dt * nrm, axis=0, keepdims=True)
            dn = dt * ngv
            do = rstd * (dn - nrm * jnp.mean(dn * nrm, axis=-1, keepdims=True))
            do_b = do.astype(BF16)
            dox = _expand_lanes(do).astype(BF16)
            yield
            da = jnp.where(_stacked_causal(), _dot_nt(dox, v_b), 0.0).astype(BF16)
            dv_intra = _dot_tn(s, dox)
            dst_own = _dot_tn(do_b, qx)
            yield
            dq_in = _fold(_dot(da, k_in) + _dot(do_b, st_b), CHUNK)
            dk_in = _dot_tn(da, qx)
            dq = dq_in * (DK ** -0.5) * c["e_pos"]
            yield
            for _ in range(order):
                yield
            dst = carried[e]
            dstx = _expand_state(dst).astype(BF16)
            dv = dv_intra + _dot_nt(k_dec, dstx)
            dk_dec = _dot(v_b, dstx)
            carried[e] = dst_own + c["decay"] * dst
            yield
            dbl = (jnp.sum(dk_dec * c["kd_f"], axis=0, keepdims=True)
                   + c["decay"] * jnp.sum(dst * st, axis=0, keepdims=True))
            dk = dk_in * c["e_neg"] + dk_dec * c["e_dec"]
            db = dq_in * c["q_f"] - dk_in * c["k_f"] - dk_dec * c["kd_f"]
            row = lax.broadcasted_iota(jnp.int32, (CHUNK, 1), 0)
            da_log = _running_sum(db + jnp.where(row == CHUNK - 1, dbl, 0.0), False)
            yield
            dz = jnp.where(c["live"], da_log * (1.0 - _sigmoid(c["z"])) * (1.0 / GATE_TAU), 0.0)
            dz_b = dz.astype(BF16)
            out = du_ref.at[e, rows, :]
            out[:, 0:GLA_K] = dq.astype(BF16)
            out[:, GLA_K:2 * GLA_K] = dk.astype(BF16)
            out[:, 2 * GLA_K:2 * GLA_K + GLA_V] = dv.astype(BF16)
            out[:, 2 * GLA_K + GLA_V:2 * GLA_K + 2 * GLA_V] = _lanes_by_head(dg).astype(BF16)
            out[:, 2 * GLA_K + 2 * GLA_V:] = _dot_nt(dz_b, w2_ref[...]).astype(BF16)
            dw2_ref[...] += _dot_tn(lr.astype(BF16), dz_b)
            dgb_ref[...] += jnp.sum(dz, axis=0, keepdims=True)

        _in_lockstep(one_chunk(e, order) for order in range(GLA_GROUP) for e in range(n_ex))
        for e in range(n_ex):
            dstate[e] = carried[e]

    u3 = u.reshape(n_ex, lp, D_IN_PAD)
    rev = lambda w, col: pl.BlockSpec((n_ex, GLA_GROUP * CHUNK, w), lambda n: (0, nc // GLA_GROUP - 1 - n, col))
    (du, d_w2, d_gb, d_ng), extra = _call(
        body, name="gla_bwd", grid=(nc // GLA_GROUP,),
        in_specs=[rev(GLA_V, 1), rev(2 * GLA_K, 2), rev(GLA_V, 3), rev(GLA_V, 4), rev(128, 20),
                  pl.BlockSpec((n_ex, GLA_GROUP * DV, GLA_K), lambda n: (0, nc // GLA_GROUP - 1 - n, 0)),
                  _const_spec((128, GLA_K)), _const_spec((1, GLA_K)), _const_spec((1, DV))],
        out_specs=[rev(D_GLA_IN, 0), _acc_spec((128, GLA_K)), _acc_spec((1, GLA_K)), _acc_spec((1, DV))],
        out_shape=[jax.ShapeDtypeStruct((n_ex, lp, D_GLA_IN), BF16), jax.ShapeDtypeStruct((128, GLA_K), F32),
                   jax.ShapeDtypeStruct((1, GLA_K), F32), jax.ShapeDtypeStruct((1, DV), F32)],
        scratch_shapes=[pltpu.VMEM((n_ex, DV, GLA_K), F32)],
        plan=plan,
    )(dycat.reshape(n_ex, lp, D), u3, u3, u3, u3, states, w2, gb, ng)
    return (du.reshape(t, D_GLA_IN), d_w2, d_gb, d_ng), extra


def _in_proj_bwd(du_conv, du_gla, w_in_t_conv, w_in_t_gla, h0, dh1, g_mix, plan=None):
    t = h0.shape[0]
    r = _row_tile(t, 384)

    def body(dc_ref, dg_ref, wc_ref, wg_ref, h_ref, dh1_ref, g_ref, dh0_ref, dgm_ref):
        @pl.when(pl.program_id(0) == 0)
        def _():
            dgm_ref[...] = jnp.zeros_like(dgm_ref)

        dhn = _dot(dc_ref[...], wc_ref[...]) + _dot(dg_ref[...], wg_ref[...])
        h = h_ref[...]
        rstd = lax.rsqrt(jnp.mean(h * h, axis=-1, keepdims=True) + RMS_EPS)
        nrm = h * rstd
        dgm_ref[...] += jnp.sum(dhn * nrm, axis=0, keepdims=True)
        dn = dhn * g_ref[...]
        dh0_ref[...] = dh1_ref[...] + rstd * (dn - nrm * jnp.mean(dn * nrm, axis=-1, keepdims=True))

    rows = lambda w: pl.BlockSpec((r, w), lambda i: (i, 0))
    return _call(
        body, name="in_proj_bwd", grid=(t // r,),
        in_specs=[rows(2 * C_CONV), rows(D_GLA_IN), _const_spec((2 * C_CONV, D)), _const_spec((D_GLA_IN, D)),
                  rows(D), rows(D), _const_spec((1, D))],
        out_specs=[rows(D), _acc_spec((1, D))],
        out_shape=[jax.ShapeDtypeStruct((t, D), F32), jax.ShapeDtypeStruct((1, D), F32)],
        plan=plan,
    )(du_conv, du_gla, w_in_t_conv, w_in_t_gla, h0, dh1, g_mix)


def _wgrad(x, dy, name, plan=None):
    t, m = x.shape
    n = dy.shape[1]
    tk = t // 3 if t % (3 * 128) == 0 else _row_tile(t, 384)
    tm = m if m <= D_GLA_IN else m // 2

    def body(x_ref, dy_ref, o_ref):
        @pl.when(pl.program_id(1) == 0)
        def _():
            o_ref[...] = jnp.zeros_like(o_ref)

        o_ref[...] += _dot_tn(x_ref[...].astype(BF16), dy_ref[...].astype(BF16))

    (out,), extra = _call(
        body, name=name, grid=(m // tm, t // tk),
        in_specs=[pl.BlockSpec((tk, tm), lambda i, k: (k, i)), pl.BlockSpec((tk, n), lambda i, k: (k, 0))],
        out_specs=[pl.BlockSpec((tm, n), lambda i, k: (i, 0))],
        out_shape=[jax.ShapeDtypeStruct((m, n), F32)],
        plan=plan,
    )(x, dy)
    return out, extra


def _adam_update(g, w, m, v):
    m2 = ADAM_B1 * m + (1.0 - ADAM_B1) * g
    v2 = ADAM_B2 * v + (1.0 - ADAM_B2) * (g * g)
    m_hat = m2 / (1.0 - ADAM_B1 ** ADAM_STEP)
    v_hat = v2 / (1.0 - ADAM_B2 ** ADAM_STEP)
    delta = -ADAM_LR * (m_hat / (jnp.sqrt(v_hat) + ADAM_EPS) + ADAM_WD * w)
    return delta, m2, v2


def _adamw(g, w, m, v, name):
    def body(g_ref, w_ref, m_ref, v_ref, d_ref, m2_ref, v2_ref):
        d_ref[...], m2_ref[...], v2_ref[...] = _adam_update(g_ref[...], w_ref[...], m_ref[...], v_ref[...])

    spec = pl.BlockSpec(g.shape, lambda i: (0, 0))
    return pl.pallas_call(
        body, name=name, grid=(1,), in_specs=[spec] * 4, out_specs=[spec] * 3,
        out_shape=[jax.ShapeDtypeStruct(g.shape, F32)] * 3, compiler_params=_params(1),
    )(g, w, m, v)


ADAMW_STEPS = 4


def _adamw_halves(items, c, name):
    n = len(items)
    h = items[0][0].shape[1]
    steps = ADAMW_STEPS if all(it[0].shape[0] % (ADAMW_STEPS * SUBLANES) == 0 for it in items) else 1

    def body(c_ref, *refs):
        ins, outs = refs[:5 * n], refs[5 * n:]
        own = pl.program_id(1) == c_ref[0]
        for i in range(n):
            a_ref, b_ref, w_ref, m_ref, v_ref = ins[5 * i:5 * i + 5]
            go_ref, d_ref, m2_ref, v2_ref = outs[4 * i:4 * i + 4]
            g = jnp.where(own, a_ref[...], b_ref[...])
            go_ref[...] = g
            d_ref[...], m2_ref[...], v2_ref[...] = _adam_update(g, w_ref[...], m_ref[...], v_ref[...])

    in_specs, out_specs, out_shape, args = [pl.BlockSpec(memory_space=pltpu.SMEM)], [], [], []
    for mine, theirs, w, m, v in items:
        tr = mine.shape[0] // steps
        half = pl.BlockSpec((tr, h), lambda i, j: (i, 0))
        full = pl.BlockSpec((tr, h), lambda i, j: (i, j))
        in_specs += [half, half, full, full, full]
        out_specs += [full] * 4
        out_shape += [jax.ShapeDtypeStruct(w.shape, F32)] * 4
        args += [mine, theirs, w, m, v]
    res = pl.pallas_call(
        body, name=name, grid=(steps, 2), in_specs=in_specs, out_specs=out_specs, out_shape=out_shape,
        compiler_params=_params(2),
    )(jnp.reshape(c, (1,)).astype(jnp.int32), *args)
    return [res[4 * i:4 * i + 4] for i in range(n)]


def _rs_add_halves(g, recv, c, name):
    _, rows, w = g.shape
    h = w // 2
    tr = rows // 2 if rows % 16 == 0 and rows > 64 else rows

    def body(c_ref, a_ref, b_ref, o_ref):
        o_ref[...] = (a_ref[...] + b_ref[...]).astype(BF16)

    return pl.pallas_call(
        body, name=name,
        grid_spec=pltpu.PrefetchScalarGridSpec(
            num_scalar_prefetch=1, grid=(N_CHIPS, rows // tr),
            in_specs=[pl.BlockSpec((1, tr, h), lambda j, i, s: (j, i, s[0])),
                      pl.BlockSpec((1, tr, h), lambda j, i, s: (j, i, 0))],
            out_specs=pl.BlockSpec((1, tr, h), lambda j, i, s: (j, i, 0))),
        out_shape=jax.ShapeDtypeStruct((N_CHIPS, rows, h), BF16),
        compiler_params=_params(2),
    )(jnp.reshape(c, (1,)).astype(jnp.int32), g, recv)


def _rs_sum(own, others, mine, name):
    _, rows, h = own.shape
    tr = rows // 2 if rows % 16 == 0 and rows > 64 else rows

    def body(mine_ref, own_ref, oth_ref, o_ref):
        p = oth_ref[...].astype(F32)
        o_ref[...] = ((own_ref[0].astype(F32) + p[0]) + p[1]) + p[2]

    return pl.pallas_call(
        body, name=name,
        grid_spec=pltpu.PrefetchScalarGridSpec(
            num_scalar_prefetch=1, grid=(rows // tr,),
            in_specs=[pl.BlockSpec((1, tr, h), lambda i, s: (s[0], i, 0)),
                      pl.BlockSpec((3, tr, h), lambda i, s: (0, i, 0))],
            out_specs=pl.BlockSpec((tr, h), lambda i, s: (i, 0))),
        out_shape=jax.ShapeDtypeStruct((rows, h), F32),
        compiler_params=_params(1),
    )(jnp.reshape(mine, (1,)).astype(jnp.int32), own, others)


def _sum_slots_adamw(slots, late_slots, w, m, v):
    late_rows = late_slots.shape[1]

    def body(s_ref, l_ref, w_ref, m_ref, v_ref, g_ref, d_ref, m2_ref, v2_ref):
        g, late = s_ref[0], l_ref[0]
        for d in range(1, 8):
            g = g + s_ref[d]
            late = late + l_ref[d]
        g = jnp.concatenate([g[:late_rows] + late, g[late_rows:]], axis=0)
        g_ref[...] = g
        d_ref[...], m2_ref[...], v2_ref[...] = _adam_update(g, w_ref[...], m_ref[...], v_ref[...])

    vm = pl.BlockSpec(memory_space=pltpu.VMEM)
    shape = jax.ShapeDtypeStruct(w.shape, F32)
    return pl.pallas_call(body, name="small_sum_adamw", in_specs=[vm] * 5, out_specs=[vm] * 4,
                          out_shape=[shape] * 4)(slots, late_slots, w, m, v)


def _mesh_pos():
    return lax.axis_index("x"), lax.axis_index("y"), lax.axis_index("c")


def _other_chips(x, y):
    return [(1 - x, y), (x, 1 - y), (1 - x, 1 - y)]


def _half(ref, c, axis):
    n = ref.shape[axis] // 2
    return ref.at[(slice(None),) * axis + (pl.ds(c * n, n),)]


def _remote(src, dst, send_sem, recv_sem, device):
    return pltpu.make_async_remote_copy(src_ref=src, dst_ref=dst, send_sem=send_sem, recv_sem=recv_sem,
                                        device_id=device, device_id_type=MESH)


def _gather_plan(split, whole=(), axes=None):
    split, whole = list(split), list(whole)
    ns, n = len(split), len(split) + len(whole)

    def make(ins, outs, sems):
        ici_send, ici_recv, d2d_send, d2d_recv, own_send, own_recv = sems
        x, y, c = _mesh_pos()
        mine = 2 * x + y
        chips = _other_chips(x, y)
        blocks = [2 * px + py for px, py in chips]

        def own(a):
            return _remote(ins[a], outs[a].at[mine], own_send.at[a], own_recv.at[a], (x, y, 1 - c))

        def ici(a, k, block):
            px, py = chips[k]
            src, dst = ins[a], outs[a].at[block]
            if a < ns:
                src, dst = _half(src, c, axes[a]), _half(dst, c, axes[a])
            return _remote(src, dst, ici_send.at[3 * a + k], ici_recv.at[3 * a + k], (px, py, c))

        def d2d(a, k, half):
            part = _half(outs[a].at[blocks[k]], half, axes[a])
            return _remote(part, part, d2d_send.at[3 * a + k], d2d_recv.at[3 * a + k], (x, y, 1 - c))

        def start():
            for a in range(n):
                for k in range(3):
                    ici(a, k, mine).start()
                own(a).start()

        def finish():
            for a in range(n):
                for k in range(3):
                    ici(a, k, blocks[k]).wait_recv()
                    if a < ns:
                        d2d(a, k, c).start()
            for a in range(ns):
                for k in range(3):
                    d2d(a, k, 1 - c).wait_recv()
            for a in range(n):
                for k in range(3):
                    ici(a, k, mine).wait_send()
                    if a < ns:
                        d2d(a, k, c).wait_send()
                own(a).wait()

        return start, finish

    arrays = split + whole
    axes = [0] * ns if axes is None else list(axes)
    return _Plan(arrays, [jax.ShapeDtypeStruct((N_CHIPS,) + s.shape, s.dtype) for s in arrays],
                 [pltpu.SemaphoreType.DMA((3 * n,)), pltpu.SemaphoreType.DMA((3 * n,)),
                  pltpu.SemaphoreType.DMA((3 * ns,)), pltpu.SemaphoreType.DMA((3 * ns,)),
                  pltpu.SemaphoreType.DMA((n,)), pltpu.SemaphoreType.DMA((n,))], make)


def _to_sibling_plan(gs):
    n = len(gs)

    def make(ins, outs, sems):
        send_sems, recv_sems = sems
        x, y, c = _mesh_pos()

        def copy(a):
            return _remote(_half(ins[a], 1 - c, 2), outs[a], send_sems.at[a], recv_sems.at[a], (x, y, 1 - c))

        def start():
            for a in range(n):
                copy(a).start()

        def finish():
            for a in range(n):
                copy(a).wait()

        return start, finish

    return _Plan(list(gs), [jax.ShapeDtypeStruct(g.shape[:2] + (g.shape[2] // 2,), g.dtype) for g in gs],
                 [pltpu.SemaphoreType.DMA((n,)), pltpu.SemaphoreType.DMA((n,))], make)


def _chip_exchange_plan(ps):
    n = len(ps)

    def make(ins, outs, sems):
        send_sems, recv_sems = sems
        x, y, c = _mesh_pos()
        chips = _other_chips(x, y)

        def ici(a, k):
            px, py = chips[k]
            return _remote(ins[a].at[2 * px + py], outs[a].at[k], send_sems.at[3 * a + k],
                           recv_sems.at[3 * a + k], (px, py, c))

        def start():
            for a in range(n):
                for k in range(3):
                    ici(a, k).start()

        def finish():
            for a in range(n):
                for k in range(3):
                    ici(a, k).wait()

        return start, finish

    return _Plan(list(ps), [jax.ShapeDtypeStruct((3,) + p.shape[1:], p.dtype) for p in ps],
                 [pltpu.SemaphoreType.DMA((3 * n,)), pltpu.SemaphoreType.DMA((3 * n,))], make)


def _share_plan(halves):
    n = len(halves)

    def make(ins, outs, sems):
        send_sems, recv_sems = sems
        x, y, c = _mesh_pos()

        def d2d(a):
            return _remote(ins[a], outs[a], send_sems.at[a], recv_sems.at[a], (x, y, 1 - c))

        def start():
            for a in range(n):
                d2d(a).start()

        def finish():
            for a in range(n):
                d2d(a).wait()

        return start, finish

    return _Plan(list(halves), [jax.ShapeDtypeStruct(p.shape, p.dtype) for p in halves],
                 [pltpu.SemaphoreType.DMA((n,)), pltpu.SemaphoreType.DMA((n,))], make)


def _all_to_all_plan(part):
    def make(ins, outs, sems):
        send_sems, recv_sems, local_sem = sems
        (p_ref,), (slots,) = ins, outs
        x, y, c = _mesh_pos()
        me = 4 * x + 2 * y + c
        peers = [(px, py, pc) for px in (x, 1 - x) for py in (y, 1 - y) for pc in (c, 1 - c)][1:]

        def remote(k, slot):
            return _remote(p_ref, slots.at[slot], send_sems.at[k], recv_sems.at[k], peers[k])

        def local():
            return pltpu.make_async_copy(p_ref, slots.at[me], local_sem)

        def start():
            for k in range(7):
                remote(k, me).start()
            local().start()

        def finish():
            for k, (px, py, pc) in enumerate(peers):
                remote(k, 4 * px + 2 * py + pc).wait_recv()
            for k in range(7):
                remote(k, me).wait_send()
            local().wait()

        return start, finish

    return _Plan([part], [jax.ShapeDtypeStruct((8,) + part.shape, part.dtype)],
                 [pltpu.SemaphoreType.DMA((7,)), pltpu.SemaphoreType.DMA((7,)), pltpu.SemaphoreType.DMA(())], make)


def _merge_plans(a, b):
    na_in, na_out, na_sems = len(a.arrays), len(a.out_shape), len(a.sems)

    def make(ins, outs, sems):
        start_a, finish_a = a.make(ins[:na_in], outs[:na_out], sems[:na_sems])
        start_b, finish_b = b.make(ins[na_in:], outs[na_out:], sems[na_sems:])

        def start():
            start_a()
            start_b()

        def finish():
            finish_a()
            finish_b()

        return start, finish

    return _Plan(list(a.arrays) + list(b.arrays), list(a.out_shape) + list(b.out_shape),
                 list(a.sems) + list(b.sems), make)


def _exchange(plan, name):
    n_in, n_out = len(plan.arrays), len(plan.out_shape)

    def body(*refs):
        start, finish = plan.make(refs[:n_in], refs[n_in:n_in + n_out], refs[n_in + n_out:])
        start()
        finish()

    return pl.pallas_call(
        body, name=name, in_specs=[HBM_SPEC] * n_in, out_specs=[HBM_SPEC] * n_out, out_shape=list(plan.out_shape),
        scratch_shapes=list(plan.sems), compiler_params=pltpu.CompilerParams(has_side_effects=True),
    )(*plan.arrays)


def _pack_small(parts):
    rows = []
    for r in range(SMALL_ROWS):
        pieces, col = [], 0
        for name, row, start, size in SMALL_PARTS:
            if row == r:
                assert start == col
                pieces.append(parts[name].reshape(1, size).astype(F32))
                col += size
        rows.append(jnp.concatenate(pieces + [jnp.zeros((1, D - col), F32)], axis=1))
    return jnp.concatenate(rows, axis=0)


def _unpack_small(slab, shapes):
    return {name: slab[row, col:col + size].reshape(shapes[name]) for name, row, col, size in SMALL_PARTS}


def _columns(gathered):
    return jnp.concatenate([gathered[j] for j in range(N_CHIPS)], axis=1)


def kernel(x, meta_tokens, norm_mix_g, w_in, conv_w, conv_b, conv_ln_g, conv_ln_b, gla_w_gate2, gla_gate_b, gla_norm_g, w_out, norm_ffn_g, w_ffn_gate, w_ffn_up, w_ffn_down, norm_final_g, loss_target, m_meta_tokens, m_norm_mix_g, m_w_in, m_conv_w, m_conv_b, m_conv_ln_g, m_conv_ln_b, m_gla_w_gate2, m_gla_gate_b, m_gla_norm_g, m_w_out, m_norm_ffn_g, m_w_ffn_gate, m_w_ffn_up, m_w_ffn_down, m_norm_final_g, v_meta_tokens, v_norm_mix_g, v_w_in, v_conv_w, v_conv_b, v_conv_ln_g, v_conv_ln_b, v_gla_w_gate2, v_gla_gate_b, v_gla_norm_g, v_w_out, v_norm_ffn_g, v_w_ffn_gate, v_w_ffn_up, v_w_ffn_down, v_norm_final_g):
    ws = dict(zip(WEIGHT_NAMES, (meta_tokens, norm_mix_g, w_in, conv_w, conv_b, conv_ln_g, conv_ln_b, gla_w_gate2,
                                 gla_gate_b, gla_norm_g, w_out, norm_ffn_g, w_ffn_gate, w_ffn_up, w_ffn_down,
                                 norm_final_g)))
    ms = dict(zip(WEIGHT_NAMES, (m_meta_tokens, m_norm_mix_g, m_w_in, m_conv_w, m_conv_b, m_conv_ln_g, m_conv_ln_b,
                                 m_gla_w_gate2, m_gla_gate_b, m_gla_norm_g, m_w_out, m_norm_ffn_g, m_w_ffn_gate,
                                 m_w_ffn_up, m_w_ffn_down, m_norm_final_g)))
    vs = dict(zip(WEIGHT_NAMES, (v_meta_tokens, v_norm_mix_g, v_w_in, v_conv_w, v_conv_b, v_conv_ln_g, v_conv_ln_b,
                                 v_gla_w_gate2, v_gla_gate_b, v_gla_norm_g, v_w_out, v_norm_ffn_g, v_w_ffn_gate,
                                 v_w_ffn_up, v_w_ffn_down, v_norm_final_g)))
    c = lax.axis_index("c")
    mine = 2 * lax.axis_index("x") + lax.axis_index("y")
    shard = lambda d, name: d[name].reshape(d[name].shape[-2:])
    vec = {name: ws[name].reshape(1, -1) for name, _, _, _ in SMALL_PARTS}
    n_ex, seq, _ = x.shape
    lp = HEAD_ROWS + seq
    t = n_ex * lp

    (tgt, h0), (w_in_g, meta_g, conv_w_g, w2_g) = _pad_head_rows([loss_target, x], plan=_gather_plan(
        [shard(ws, "w_in").T.astype(BF16)],
        [shard(ws, "meta_tokens"), shard(ws, "conv_w"), shard(ws, "gla_w_gate2")], axes=[1]))
    w_in_t = jnp.concatenate([w_in_g.reshape(D_IN, D), jnp.zeros((D_IN_PAD - D_IN, D), BF16)], axis=0)
    conv_w_full = jnp.concatenate([_columns(conv_w_g), jnp.zeros((32 - CONV_W, C_CONV), F32)], axis=0)
    w2_full = jnp.concatenate([_columns(w2_g), jnp.zeros((128 - RANK, GLA_K), F32)], axis=0).astype(BF16)
    h0 = _set_meta_rows(h0, _columns(meta_g)).reshape(t, D)
    tgt = tgt.reshape(t, D)
    row_mask = jnp.concatenate([jnp.zeros((n_ex, HEAD_ROWS, 1), F32), jnp.ones((n_ex, seq, 1), F32)],
                               axis=1).reshape(t, 1)

    (u, hn), (w_out_g,) = _in_proj(h0, vec["norm_mix_g"], w_in_t.T,
                                   plan=_gather_plan([shard(ws, "w_out").astype(BF16)]))
    (yc, y_conv), (gate_g,) = _conv_fwd(
        u, conv_w_full, vec["conv_b"], vec["conv_ln_g"], vec["conv_ln_b"], n_ex, lp,
        plan=_gather_plan([shard(ws, "w_ffn_gate").T.astype(BF16)]))
    (y_gla, states), (up_g,) = _gla_fwd(u, w2_full, vec["gla_gate_b"], vec["gla_norm_g"], n_ex, lp,
                                        plan=_gather_plan([shard(ws, "w_ffn_up").T.astype(BF16)]))
    w_out_full = w_out_g.reshape(D, D)
    w_gate_t, w_up_t = gate_g.reshape(D_FF, D), up_g.reshape(D_FF, D)
    (h1, hn2, gate, up, act), (down_g,) = _mix_out_ffn_up(
        h0, y_conv, y_gla, w_out_full, vec["norm_ffn_g"], w_gate_t.T, w_up_t.T,
        plan=_gather_plan([shard(ws, "w_ffn_down").astype(BF16)]))
    w_down_full = down_g.reshape(D_FF, D)
    dh2, loss, d_final_g = _ffn_down_loss(act, w_down_full, h1, tgt, vec["norm_final_g"], row_mask)
    dgate, dup, dh1, dycat, d_ffn_g = _ffn_bwd(dh2, gate, up, h1, w_down_full.T, w_gate_t, w_up_t, w_out_full.T,
                                                vec["norm_ffn_g"])

    early = ("w_ffn_gate", "w_ffn_up", "w_ffn_down", "w_out")
    ffn_block = lambda g: g.reshape(N_CHIPS, D_FF // N_CHIPS, D)
    g_gate = ffn_block(_wgrad(dgate, hn2, "wgrad_gate")[0])
    g_up, (gate_sib,) = _wgrad(dup, hn2, "wgrad_up", _to_sibling_plan([g_gate]))
    g_up = ffn_block(g_up)
    g_down, (up_sib,) = _wgrad(act, dh2, "wgrad_down", _to_sibling_plan([g_up]))
    g_down = ffn_block(g_down)
    g_out = jnp.concatenate([_wgrad(y_conv, dh1, "wgrad_out_conv")[0], _wgrad(y_gla, dh1, "wgrad_out_gla")[0]],
                            axis=0).reshape(N_CHIPS, D // N_CHIPS, D)
    cs_gate = _rs_add_halves(g_gate, gate_sib, c, "rs_add_w_ffn_gate")
    cs_up = _rs_add_halves(g_up, up_sib, c, "rs_add_w_ffn_up")
    (du_conv, d_conv_w, d_conv_b, d_ln_g, d_ln_b), (ex_gate, ex_up, down_sib, out_sib) = _conv_bwd(
        dycat, yc, u, conv_w_full, vec["conv_ln_g"], vec["conv_ln_b"], n_ex, lp,
        plan=_merge_plans(_chip_exchange_plan([cs_gate, cs_up]), _to_sibling_plan([g_down, g_out])))
    cs_down = _rs_add_halves(g_down, down_sib, c, "rs_add_w_ffn_down")
    cs_out = _rs_add_halves(g_out, out_sib, c, "rs_add_w_out")
    (du_gla, d_w2, d_gate_b, d_norm_g), (ex_down, ex_out) = _gla_bwd(
        dycat, u, states, w2_full, vec["gla_gate_b"], vec["gla_norm_g"], n_ex, lp,
        plan=_chip_exchange_plan([cs_down, cs_out]))
    halves = [_rs_sum(own, oth, mine, "rs_sum_" + nm)
              for own, oth, nm in zip((cs_gate, cs_up, cs_down, cs_out), (ex_gate, ex_up, ex_down, ex_out), early)]

    small = {"norm_mix_g": jnp.zeros((1, D), F32), "norm_ffn_g": d_ffn_g, "norm_final_g": d_final_g,
             "conv_b": d_conv_b, "conv_ln_g": d_ln_g, "conv_ln_b": d_ln_b, "gla_gate_b": d_gate_b,
             "gla_norm_g": d_norm_g}
    part = lax.dynamic_update_slice(_pack_small(small), loss[:, :1], (LOSS_ROW, 0))
    part = jnp.concatenate([part, jnp.zeros((N_META, D), F32), d_conv_w.reshape(16, D), d_w2[:RANK].reshape(4, D),
                            jnp.zeros((4, D), F32)], axis=0)
    g_in_gla, (slots,) = _wgrad(du_gla, hn, "wgrad_in_gla", _all_to_all_plan(part))

    d_w_in_t = jnp.concatenate([_wgrad(du_conv, hn, "wgrad_in_conv")[0], g_in_gla],
                               axis=0)[:D_IN].reshape(N_CHIPS, D_IN // N_CHIPS, D)
    (in_from_sibling,) = _exchange(_to_sibling_plan([d_w_in_t]), "rs_late_to_sibling")
    in_chip_sum = _rs_add_halves(d_w_in_t, in_from_sibling, c, "rs_add_w_in")
    (dh0, d_mix_g), shared = _in_proj_bwd(
        du_conv, du_gla, w_in_t[:2 * C_CONV], w_in_t[2 * C_CONV:], h0, dh1, vec["norm_mix_g"],
        plan=_merge_plans(_share_plan(halves), _chip_exchange_plan([in_chip_sum])))
    dh0 = dh0.reshape(n_ex, lp, D)
    grad_x = dh0[:, HEAD_ROWS:]
    late_part = jnp.concatenate([d_mix_g, jnp.zeros((SMALL_ROWS - 1, D), F32),
                                 jnp.sum(dh0[:, PAD_ROWS:HEAD_ROWS], axis=0)], axis=0)
    in_half = _rs_sum(in_chip_sum, shared[4], mine, "rs_sum_w_in")
    in_shared, late_slots = _exchange(_merge_plans(_share_plan([in_half]), _all_to_all_plan(late_part)),
                                      "late_exchange")

    out = {"grad": {}, "delta": {}, "new_m": {}, "new_v": {}}

    def record(name, res, transposed=False):
        for kind, a in zip(("grad", "delta", "new_m", "new_v"), res):
            out[kind][name] = (a.T if transposed else a).reshape(ws[name].shape)

    def operands(name, transposed):
        lay = (lambda a: a.T) if transposed else (lambda a: a)
        return lay(shard(ws, name)), lay(shard(ms, name)), lay(shard(vs, name))

    early_layout = (("w_ffn_gate", True), ("w_ffn_up", True), ("w_ffn_down", False), ("w_out", False))
    items = [(mine_half, their_half, *operands(name, transposed))
             for (name, transposed), mine_half, their_half in zip(early_layout, halves, shared)]
    for (name, transposed), res in zip(early_layout, _adamw_halves(items, c, "adamw_early")):
        record(name, res, transposed)

    record("w_in", _adamw_halves([(in_half, in_shared, *operands("w_in", True))], c, "adamw_w_in")[0], True)

    tall = lambda a: jnp.concatenate([a, jnp.zeros((part.shape[0] - SMALL_ROWS, D), F32)], axis=0)
    g_s, d_s, m_s, v_s = _sum_slots_adamw(slots, late_slots, tall(_pack_small(ws)), tall(_pack_small(ms)),
                                          tall(_pack_small(vs)))
    small_shapes = {name: ws[name].shape for name, _, _, _ in SMALL_PARTS}
    for kind, slab in (("grad", g_s), ("delta", d_s), ("new_m", m_s), ("new_v", v_s)):
        out[kind].update(_unpack_small(slab, small_shapes))
    loss = g_s[LOSS_ROW, 0]
    block = lambda a, width: lax.dynamic_slice_in_dim(a, mine * width, width, axis=1)
    small_sharded = {"meta_tokens": block(g_s[8:24], D // N_CHIPS),
                     "conv_w": block(g_s[24:40].reshape(32, C_CONV), C_CONV // N_CHIPS)[:CONV_W],
                     "gla_w_gate2": block(g_s[40:44].reshape(RANK, GLA_K), GLA_K // N_CHIPS)}
    for name, g in small_sharded.items():
        record(name, [g, *_adamw(g, *operands(name, False), "adamw_" + name)])

    return (loss, grad_x, *[out[kind][name] for kind in ("grad", "delta", "new_m", "new_v") for name in WEIGHT_NAMES])
```

```python
import functools
from typing import Any, Callable, NamedTuple, Sequence

import jax
import jax.numpy as jnp
from jax import lax
from jax.experimental import pallas as pl
from jax.experimental.pallas import tpu as pltpu

F32 = jnp.float32
BF16 = jnp.bfloat16
MESH = pl.DeviceIdType.MESH

D = 1024
N_META = 16
C_CONV = 512
CONV_W = 31
GLA_K = 256
GLA_V = 512
N_HEADS = 4
DK = 64
DV = 128
RANK = 16
CHUNK = 64
PAD_ROWS = CHUNK - N_META
HEAD_ROWS = CHUNK
D_IN = 2576
D_IN_PAD = 2688
D_GLA_IN = D_IN_PAD - 2 * C_CONV
D_FF = 2816
RMS_EPS = 1e-6
LN_EPS = 1e-5
GATE_TAU = 16.0
N_CHIPS = 4

ADAM_LR = 0.001
ADAM_B1 = 0.9
ADAM_B2 = 0.999
ADAM_EPS = 1e-08
ADAM_WD = 0.01
ADAM_STEP = 10

V7X_VMEM_BYTES = 64 * 1024 * 1024
VMEM_LIMIT = V7X_VMEM_BYTES - 8 * 1024 * 1024
SUBLANES = 8
ROW_PART = 128
FFN_BWD_TILE = 192

WEIGHT_NAMES = ("meta_tokens", "norm_mix_g", "w_in", "conv_w", "conv_b", "conv_ln_g", "conv_ln_b", "gla_w_gate2",
                "gla_gate_b", "gla_norm_g", "w_out", "norm_ffn_g", "w_ffn_gate", "w_ffn_up", "w_ffn_down",
                "norm_final_g")

SMALL_ROWS = 8
SMALL_PARTS = (("norm_mix_g", 0, 0, D), ("norm_ffn_g", 1, 0, D), ("norm_final_g", 2, 0, D),
               ("conv_b", 3, 0, C_CONV), ("conv_ln_g", 3, C_CONV, C_CONV), ("conv_ln_b", 4, 0, C_CONV),
               ("gla_gate_b", 4, C_CONV, GLA_K), ("gla_norm_g", 4, C_CONV + GLA_K, DV))
LOSS_ROW = 5

HBM_SPEC = pl.BlockSpec(memory_space=pltpu.HBM)


def _dot(a, b):
    return jnp.dot(a, b, preferred_element_type=F32)


def _dot_nt(a, b):
    return lax.dot_general(a, b, (((1,), (1,)), ((), ())), preferred_element_type=F32)


def _dot_tn(a, b):
    return lax.dot_general(a, b, (((0,), (0,)), ((), ())), preferred_element_type=F32)


def _sigmoid(x):
    return 1.0 / (1.0 + jnp.exp(-x))


def _const_spec(shape):
    return pl.BlockSpec(shape, lambda *_: (0,) * len(shape), pipeline_mode=pl.Buffered(1))


def _acc_spec(shape):
    return pl.BlockSpec(shape, lambda *_: (0,) * len(shape))


def _params(n_axes):
    return pltpu.CompilerParams(dimension_semantics=("arbitrary",) * n_axes, vmem_limit_bytes=VMEM_LIMIT)


def _row_tile(t, want):
    for r in (want, 384, 192, 128, 64):
        if r <= want and t % r == 0:
            return r
    raise ValueError(f"no row tile for {t}")


def _row_parts(r):
    if r % ROW_PART:
        return [slice(None)]
    return [pl.ds(i * ROW_PART, ROW_PART) for i in range(r // ROW_PART)]


def _in_lockstep(bodies):
    live = list(bodies)
    while live:
        still = []
        for g in live:
            try:
                next(g)
                still.append(g)
            except StopIteration:
                pass
        live = still


class _Plan(NamedTuple):
    arrays: Sequence[Any]
    out_shape: Sequence[Any]
    sems: Sequence[Any]
    make: Callable


def _call(body, *, name, grid, in_specs, out_specs, out_shape, scratch_shapes=(), plan=None):
    n_in, n_out, n_scr = len(in_specs), len(out_specs), len(scratch_shapes)
    if plan is None:
        plan = _Plan([], [], [], lambda ins, outs, sems: (lambda: None, lambda: None))
    nx_in, nx_out = len(plan.arrays), len(plan.out_shape)

    def hosted(*refs):
        ins, xins = refs[:n_in], refs[n_in:n_in + nx_in]
        o0 = n_in + nx_in
        outs, xouts = refs[o0:o0 + n_out], refs[o0 + n_out:o0 + n_out + nx_out]
        s0 = o0 + n_out + nx_out
        scr, sems = refs[s0:s0 + n_scr], refs[s0 + n_scr:]
        ids = [pl.program_id(a) for a in range(len(grid))]
        first = functools.reduce(jnp.logical_and, [i == 0 for i in ids])
        last = functools.reduce(jnp.logical_and, [i == g - 1 for i, g in zip(ids, grid)])
        start, finish = plan.make(xins, xouts, sems)
        pl.when(first)(start)
        body(*ins, *outs, *scr)
        pl.when(last)(finish)

    call = pl.pallas_call(
        hosted, name=name, grid=grid, in_specs=list(in_specs) + [HBM_SPEC] * nx_in,
        out_specs=list(out_specs) + [HBM_SPEC] * nx_out, out_shape=list(out_shape) + list(plan.out_shape),
        scratch_shapes=list(scratch_shapes) + list(plan.sems),
        compiler_params=pltpu.CompilerParams(dimension_semantics=("arbitrary",) * len(grid),
                                             vmem_limit_bytes=VMEM_LIMIT, has_side_effects=nx_in > 0))

    def run(*args):
        res = call(*args, *plan.arrays)
        return res[:n_out], res[n_out:]

    return run


def _pad_head_rows(arrays, plan=None):
    n_ex, seq, _ = arrays[0].shape
    nc = (HEAD_ROWS + seq) // CHUNK
    n = len(arrays)

    def body(*refs):
        for a_ref, o_ref in zip(refs[:n], refs[n:]):
            o_ref[...] = jnp.where(pl.program_id(0) > 0, a_ref[...], 0.0)

    return _call(
        body, name="pad_head_rows", grid=(nc,),
        in_specs=[pl.BlockSpec((n_ex, CHUNK, D), lambda i: (0, jnp.maximum(i - 1, 0), 0))] * n,
        out_specs=[pl.BlockSpec((n_ex, CHUNK, D), lambda i: (0, i, 0))] * n,
        out_shape=[jax.ShapeDtypeStruct((n_ex, HEAD_ROWS + seq, D), F32)] * n,
        plan=plan,
    )(*arrays)


def _set_meta_rows(h0, meta):
    n_ex = h0.shape[0]

    def body(h_ref, meta_ref, o_ref):
        o_ref[...] = jnp.concatenate(
            [h_ref[:, :PAD_ROWS, :], jnp.broadcast_to(meta_ref[...][None], (n_ex, N_META, D))], axis=1)

    head = pl.BlockSpec((n_ex, HEAD_ROWS, D), lambda i: (0, 0, 0))
    return pl.pallas_call(
        body, name="set_meta_rows", grid=(1,), in_specs=[head, pl.BlockSpec((N_META, D), lambda i: (0, 0))],
        out_specs=head, out_shape=jax.ShapeDtypeStruct(h0.shape, F32), input_output_aliases={0: 0},
        compiler_params=_params(1),
    )(h0, meta)


def _in_proj(h0, g_mix, w_in, plan=None):
    t = h0.shape[0]
    r = _row_tile(t, 384)

    def body(h_ref, g_ref, w_ref, u_ref, hn_ref):
        h = h_ref[...]
        rstd = lax.rsqrt(jnp.mean(h * h, axis=-1, keepdims=True) + RMS_EPS)
        hn = (h * rstd * g_ref[...]).astype(BF16)
        hn_ref[...] = hn
        u_ref[...] = _dot(hn, w_ref[...])

    return _call(
        body, name="in_proj", grid=(t // r,),
        in_specs=[pl.BlockSpec((r, D), lambda i: (i, 0)), _const_spec((1, D)), _const_spec((D, D_IN_PAD))],
        out_specs=[pl.BlockSpec((r, D_IN_PAD), lambda i: (i, 0)), pl.BlockSpec((r, D), lambda i: (i, 0))],
        out_shape=[jax.ShapeDtypeStruct((t, D_IN_PAD), F32), jax.ShapeDtypeStruct((t, D), BF16)],
        plan=plan,
    )(h0, g_mix, w_in)


CONV_TILE = 192
CONV_SUB = 32
CONV_LEAD = CONV_SUB - (CONV_W - 1)


def _shifted_copies(src, dst, r):
    for s in range(1, SUBLANES):
        dst[s - 1] = src[s:s + r + CONV_SUB - SUBLANES, :]


def _shifted_rows(src, shifted, start):
    base, s = SUBLANES * (start // SUBLANES), start % SUBLANES
    if s == 0:
        return src[base:base + CONV_SUB, :]
    return shifted[s - 1, base:base + CONV_SUB, :]


def _conv_fwd(u, conv_w, conv_b, ln_g, ln_b, n_ex, lp, plan=None):
    r = CONV_TILE
    nt = lp // r
    hb = r // CONV_SUB

    def body(cur_ref, prev_ref, w_ref, b_ref, lg_ref, lb_ref, yc_ref, y_ref, glu, glu_sh):
        i = pl.program_id(1)
        cur = cur_ref[...]
        glu[CONV_SUB:CONV_SUB + r, :] = cur[:, :C_CONV] * _sigmoid(cur[:, C_CONV:])
        pv = prev_ref[...]
        halo = pv[:, :C_CONV] * _sigmoid(pv[:, C_CONV:])
        glu[0:CONV_SUB, :] = jnp.where(i > 0, halo, 0.0)
        _shifted_copies(glu, glu_sh, r)
        w = w_ref[...]
        for j in range(r // CONV_SUB):
            r0 = j * CONV_SUB
            acc = jnp.zeros((CONV_SUB, C_CONV), F32) + b_ref[...]
            for k in range(CONV_W):
                acc = acc + w[k:k + 1, :] * _shifted_rows(glu, glu_sh, r0 + CONV_LEAD + k)
            mu = jnp.mean(acc, axis=-1, keepdims=True)
            cen = acc - mu
            var = jnp.mean(cen * cen, axis=-1, keepdims=True)
            out = cen * lax.rsqrt(var + LN_EPS) * lg_ref[...] + lb_ref[...]
            y = out * _sigmoid(out)
            row = i * r + r0 + lax.broadcasted_iota(jnp.int32, (CONV_SUB, 1), 0)
            y = jnp.where(row >= PAD_ROWS, y, 0.0)
            yc_ref[r0:r0 + CONV_SUB, :] = acc
            y_ref[r0:r0 + CONV_SUB, :] = y.astype(BF16)

    t = n_ex * lp
    return _call(
        body, name="conv_fwd", grid=(n_ex, nt),
        in_specs=[pl.BlockSpec((r, 2 * C_CONV), lambda b, i: (b * nt + i, 0)),
                  pl.BlockSpec((CONV_SUB, 2 * C_CONV), lambda b, i: (jnp.maximum((b * nt + i) * hb - 1, 0), 0)),
                  _const_spec((32, C_CONV)), _const_spec((1, C_CONV)), _const_spec((1, C_CONV)), _const_spec((1, C_CONV))],
        out_specs=[pl.BlockSpec((r, C_CONV), lambda b, i: (b * nt + i, 0)),
                   pl.BlockSpec((r, C_CONV), lambda b, i: (b * nt + i, 0))],
        out_shape=[jax.ShapeDtypeStruct((t, C_CONV), F32), jax.ShapeDtypeStruct((t, C_CONV), BF16)],
        scratch_shapes=[pltpu.VMEM((r + CONV_SUB, C_CONV), F32),
                        pltpu.VMEM((SUBLANES - 1, r + CONV_SUB - SUBLANES, C_CONV), F32)],
        plan=plan,
    )(u, u, conv_w, conv_b, ln_g, ln_b)


def _mix_out_ffn_up(h0, y_conv, y_gla, w_out, g_ffn, w_gate, w_up, plan=None):
    t = h0.shape[0]
    r = _row_tile(t, 384)

    def body(h0_ref, yc_ref, yg_ref, wo_ref, g_ref, wg_ref, wu_ref, h1_ref, hn_ref, gate_ref, up_ref, act_ref):
        h1 = h0_ref[...] + _dot(yc_ref[...], wo_ref[0:C_CONV, :]) + _dot(yg_ref[...], wo_ref[C_CONV:D, :])
        h1_ref[...] = h1
        rstd = lax.rsqrt(jnp.mean(h1 * h1, axis=-1, keepdims=True) + RMS_EPS)
        hn = (h1 * rstd * g_ref[...]).astype(BF16)
        hn_ref[...] = hn
        gate = _dot(hn, wg_ref[...])
        up = _dot(hn, wu_ref[...])
        gate_ref[...] = gate
        up_ref[...] = up
        act_ref[...] = (gate * _sigmoid(gate) * up).astype(BF16)

    rows = lambda w: pl.BlockSpec((r, w), lambda i: (i, 0))
    return _call(
        body, name="mix_out_ffn_up", grid=(t // r,),
        in_specs=[rows(D), rows(C_CONV), rows(GLA_V), _const_spec((D, D)), _const_spec((1, D)),
                  _const_spec((D, D_FF)), _const_spec((D, D_FF))],
        out_specs=[rows(D), rows(D), rows(D_FF), rows(D_FF), rows(D_FF)],
        out_shape=[jax.ShapeDtypeStruct((t, D), F32), jax.ShapeDtypeStruct((t, D), BF16),
                   jax.ShapeDtypeStruct((t, D_FF), F32), jax.ShapeDtypeStruct((t, D_FF), F32),
                   jax.ShapeDtypeStruct((t, D_FF), BF16)],
        plan=plan,
    )(h0, y_conv, y_gla, w_out, g_ffn, w_gate, w_up)


def _ffn_down_loss(act, w_down, h1, target, g_final, row_mask):
    t = h1.shape[0]
    r = _row_tile(t, 384)

    def body(act_ref, wd_ref, h1_ref, tgt_ref, gf_ref, mask_ref, dh2_ref, loss_ref, dgf_ref):
        @pl.when(pl.program_id(0) == 0)
        def _():
            loss_ref[...] = jnp.zeros_like(loss_ref)
            dgf_ref[...] = jnp.zeros_like(dgf_ref)

        gf = gf_ref[...]

        def part(rows):
            h2 = h1_ref[rows, :] + _dot(act_ref[rows, :], wd_ref[...])
            yield
            rstd = lax.rsqrt(jnp.mean(h2 * h2, axis=-1, keepdims=True) + RMS_EPS)
            nrm = h2 * rstd
            err = (nrm * gf - tgt_ref[rows, :]) * mask_ref[rows, :]
            loss_ref[...] += jnp.sum(err * err) * (0.5 / D)
            dy = err * (1.0 / D)
            dgf_ref[...] += jnp.sum(dy * nrm, axis=0, keepdims=True)
            dn = dy * gf
            dh2_ref[rows, :] = rstd * (dn - nrm * jnp.mean(dn * nrm, axis=-1, keepdims=True))

        _in_lockstep(part(rows) for rows in _row_parts(r))

    rows = lambda w: pl.BlockSpec((r, w), lambda i: (i, 0))
    return pl.pallas_call(
        body, name="ffn_down_loss", grid=(t // r,),
        in_specs=[rows(D_FF), _const_spec((D_FF, D)), rows(D), rows(D), _const_spec((1, D)), rows(1)],
        out_specs=[rows(D), _acc_spec((1, 128)), _acc_spec((1, D))],
        out_shape=[jax.ShapeDtypeStruct((t, D), F32), jax.ShapeDtypeStruct((1, 128), F32),
                   jax.ShapeDtypeStruct((1, D), F32)],
        compiler_params=_params(1),
    )(act, w_down, h1, target, g_final, row_mask)


def _ffn_bwd(dh2, gate, up, h1, w_down_t, w_gate_t, w_up_t, w_out_t, g_ffn):
    t = h1.shape[0]
    r = _row_tile(t, FFN_BWD_TILE)

    def body(dh2_ref, gate_ref, up_ref, h1_ref, wd_ref, wg_ref, wu_ref, wo_ref, g_ref,
             dgate_ref, dup_ref, dh1_ref, dycat_ref, dg_ref):
        @pl.when(pl.program_id(0) == 0)
        def _():
            dg_ref[...] = jnp.zeros_like(dg_ref)

        dh2 = dh2_ref[...]
        dact = _dot(dh2.astype(BF16), wd_ref[...])
        gate = gate_ref[...]
        sg = _sigmoid(gate)
        dgate = (dact * up_ref[...] * (sg * (1.0 + gate * (1.0 - sg)))).astype(BF16)
        dup = (dact * (gate * sg)).astype(BF16)
        dgate_ref[...] = dgate
        dup_ref[...] = dup
        dhn = _dot(dgate, wg_ref[...]) + _dot(dup, wu_ref[...])
        h1 = h1_ref[...]
        rstd = lax.rsqrt(jnp.mean(h1 * h1, axis=-1, keepdims=True) + RMS_EPS)
        nrm = h1 * rstd
        dg_ref[...] += jnp.sum(dhn * nrm, axis=0, keepdims=True)
        dn = dhn * g_ref[...]
        dh1 = dh2 + rstd * (dn - nrm * jnp.mean(dn * nrm, axis=-1, keepdims=True))
        dh1_ref[...] = dh1
        dycat_ref[...] = _dot(dh1.astype(BF16), wo_ref[...])

    rows = lambda w: pl.BlockSpec((r, w), lambda i: (i, 0))
    return pl.pallas_call(
        body, name="ffn_bwd", grid=(t // r,),
        in_specs=[rows(D), rows(D_FF), rows(D_FF), rows(D), _const_spec((D, D_FF)), _const_spec((D_FF, D)),
                  _const_spec((D_FF, D)), _const_spec((D, D)), _const_spec((1, D))],
        out_specs=[rows(D_FF), rows(D_FF), rows(D), rows(D), _acc_spec((1, D))],
        out_shape=[jax.ShapeDtypeStruct((t, D_FF), BF16), jax.ShapeDtypeStruct((t, D_FF), BF16),
                   jax.ShapeDtypeStruct((t, D), F32), jax.ShapeDtypeStruct((t, D), F32),
                   jax.ShapeDtypeStruct((1, D), F32)],
        compiler_params=_params(1),
    )(dh2, gate, up, h1, w_down_t, w_gate_t, w_up_t, w_out_t, g_ffn)


def _conv_bwd(dycat, yc, u, conv_w, ln_g, ln_b, n_ex, lp, plan=None):
    r = CONV_TILE
    nt = lp // r
    hb = r // CONV_SUB
    nsub = r // CONV_SUB

    def ln_bwd(dy, yc_rows, live, lg, lb):
        mu = jnp.mean(yc_rows, axis=-1, keepdims=True)
        cen = yc_rows - mu
        rs = lax.rsqrt(jnp.mean(cen * cen, axis=-1, keepdims=True) + LN_EPS)
        yn = cen * rs
        out = yn * lg + lb
        so = _sigmoid(out)
        dout = jnp.where(live, dy * (so * (1.0 + out * (1.0 - so))), 0.0)
        dyn = dout * lg
        dyc = rs * (dyn - jnp.mean(dyn, axis=-1, keepdims=True) - yn * jnp.mean(dyn * yn, axis=-1, keepdims=True))
        return dyc, dout, yn

    def body(dy_ref, dyn_ref, yc_ref, ycn_ref, cur_ref, prev_ref, w_ref, lg_ref, lb_ref,
             du_ref, dw_ref, db_ref, dlg_ref, dlb_ref, glu, dycs, dwacc, glu_sh, dycs_sh):
        b = pl.program_id(0)
        i = pl.program_id(1)
        first = jnp.logical_and(b == 0, i == 0)

        @pl.when(first)
        def _():
            dwacc[...] = jnp.zeros_like(dwacc)
            db_ref[...] = jnp.zeros_like(db_ref)
            dlg_ref[...] = jnp.zeros_like(dlg_ref)
            dlb_ref[...] = jnp.zeros_like(dlb_ref)

        lg, lb = lg_ref[...], lb_ref[...]
        cur = cur_ref[...]
        sig = _sigmoid(cur[:, C_CONV:])
        glu[CONV_SUB:CONV_SUB + r, :] = cur[:, :C_CONV] * sig
        pv = prev_ref[...]
        glu[0:CONV_SUB, :] = jnp.where(i > 0, pv[:, :C_CONV] * _sigmoid(pv[:, C_CONV:]), 0.0)

        row = i * r + lax.broadcasted_iota(jnp.int32, (r, 1), 0)
        dyc, dout, yn = ln_bwd(dy_ref[...], yc_ref[...], row >= PAD_ROWS, lg, lb)
        dycs[0:r, :] = dyc
        dycn, _, _ = ln_bwd(dyn_ref[...], ycn_ref[...], i < nt - 1, lg, lb)
        dycs[r:r + CONV_SUB, :] = dycn
        db_ref[...] += jnp.sum(dyc, axis=0, keepdims=True)
        dlg_ref[...] += jnp.sum(dout * yn, axis=0, keepdims=True)
        dlb_ref[...] += jnp.sum(dout, axis=0, keepdims=True)

        _shifted_copies(glu, glu_sh, r)
        _shifted_copies(dycs, dycs_sh, r)
        w = w_ref[...]
        for j in range(nsub):
            r0 = j * CONV_SUB
            dblk = dycs[r0:r0 + CONV_SUB, :]
            dglu = jnp.zeros((CONV_SUB, C_CONV), F32)
            for k in range(CONV_W):
                dglu = dglu + w[k:k + 1, :] * _shifted_rows(dycs, dycs_sh, r0 + (CONV_W - 1) - k)
                prod = dblk * _shifted_rows(glu, glu_sh, r0 + CONV_LEAD + k)
                dwacc[k] += prod.reshape(CONV_SUB // SUBLANES, SUBLANES, C_CONV).sum(axis=0)
            sg = sig[r0:r0 + CONV_SUB, :]
            cv = cur[r0:r0 + CONV_SUB, :C_CONV]
            du_ref[r0:r0 + CONV_SUB, :C_CONV] = (dglu * sg).astype(BF16)
            du_ref[r0:r0 + CONV_SUB, C_CONV:] = (dglu * cv * sg * (1.0 - sg)).astype(BF16)

        @pl.when(jnp.logical_and(b == n_ex - 1, i == nt - 1))
        def _():
            dw_ref[...] = jnp.sum(dwacc[...], axis=1)

    t = n_ex * lp
    cur_rows = lambda w, col: pl.BlockSpec((r, w), lambda b, i: (b * nt + i, col))
    nxt_rows = lambda w, col: pl.BlockSpec(
        (CONV_SUB, w), lambda b, i: (jnp.minimum((b * nt + i + 1) * hb, n_ex * nt * hb - 1), col))
    return _call(
        body, name="conv_bwd", grid=(n_ex, nt),
        in_specs=[cur_rows(C_CONV, 0), nxt_rows(C_CONV, 0), cur_rows(C_CONV, 0), nxt_rows(C_CONV, 0),
                  cur_rows(2 * C_CONV, 0),
                  pl.BlockSpec((CONV_SUB, 2 * C_CONV), lambda b, i: (jnp.maximum((b * nt + i) * hb - 1, 0), 0)),
                  _const_spec((32, C_CONV)), _const_spec((1, C_CONV)), _const_spec((1, C_CONV))],
        out_specs=[cur_rows(2 * C_CONV, 0), _acc_spec((32, C_CONV)), _acc_spec((1, C_CONV)),
                   _acc_spec((1, C_CONV)), _acc_spec((1, C_CONV))],
        out_shape=[jax.ShapeDtypeStruct((t, 2 * C_CONV), BF16), jax.ShapeDtypeStruct((32, C_CONV), F32),
                   jax.ShapeDtypeStruct((1, C_CONV), F32), jax.ShapeDtypeStruct((1, C_CONV), F32),
                   jax.ShapeDtypeStruct((1, C_CONV), F32)],
        scratch_shapes=[pltpu.VMEM((r + CONV_SUB, C_CONV), F32), pltpu.VMEM((r + CONV_SUB, C_CONV), F32),
                        pltpu.VMEM((32, SUBLANES, C_CONV), F32),
                        pltpu.VMEM((SUBLANES - 1, r + CONV_SUB - SUBLANES, C_CONV), F32),
                        pltpu.VMEM((SUBLANES - 1, r + CONV_SUB - SUBLANES, C_CONV), F32)],
        plan=plan,
    )(dycat, dycat, yc, yc, u, u, conv_w, ln_g, ln_b)


HEAD_ROWS_ALL = N_HEADS * CHUNK


def _gla_gates(lr, w2, gb, first_chunk):
    z = _dot(lr.astype(BF16), w2) + gb
    a = (jnp.minimum(z, 0.0) - jnp.log(1.0 + jnp.exp(-jnp.abs(z)))) * (1.0 / GATE_TAU)
    row = lax.broadcasted_iota(jnp.int32, (CHUNK, 1), 0)
    live = jnp.logical_or(jnp.logical_not(first_chunk), row >= PAD_ROWS)
    return z, jnp.where(live, a, 0.0), live


def _tri(lower):
    i = lax.broadcasted_iota(jnp.int32, (CHUNK, CHUNK), 0)
    j = lax.broadcasted_iota(jnp.int32, (CHUNK, CHUNK), 1)
    return (i >= j) if lower else (i <= j)


def _head_of(shape, axis, per_head):
    return lax.broadcasted_iota(jnp.int32, shape, axis) // per_head


def _expand(x, lanes_per_head):
    rows, lanes = HEAD_ROWS_ALL, x.shape[1]
    keep = _head_of((rows, lanes), 0, CHUNK) == _head_of((rows, lanes), 1, lanes_per_head)
    return jnp.where(keep, jnp.tile(x, (N_HEADS, 1)), 0.0)


def _expand_lanes(x):
    rows, w = x.shape
    keep = _head_of((rows, N_HEADS * w), 0, CHUNK) == _head_of((rows, N_HEADS * w), 1, w)
    return jnp.where(keep, jnp.tile(x, (1, N_HEADS)), 0.0)


def _expand_state(st):
    rows, lanes = N_HEADS * DV, st.shape[1]
    keep = _head_of((rows, lanes), 0, DV) == _head_of((rows, lanes), 1, DK)
    return jnp.where(keep, jnp.tile(st, (N_HEADS, 1)), 0.0)


def _fold(t, rows_per_head):
    lane_head = _head_of((rows_per_head, t.shape[1]), 1, DK)
    out = jnp.where(lane_head == 0, t[0:rows_per_head], 0.0)
    for h in range(1, N_HEADS):
        out = out + jnp.where(lane_head == h, t[h * rows_per_head:(h + 1) * rows_per_head], 0.0)
    return out


def _rows_by_head(x):
    return jnp.concatenate([x[:, h * DV:(h + 1) * DV] for h in range(N_HEADS)], axis=0)


def _lanes_by_head(x):
    return jnp.concatenate([x[h * CHUNK:(h + 1) * CHUNK] for h in range(N_HEADS)], axis=1)


def _running_sum(a, lower):
    hi = a.astype(BF16)
    rest = a - hi.astype(F32)
    mid = rest.astype(BF16)
    lo = (rest - mid.astype(F32)).astype(BF16)
    w = a.shape[1]
    parts = _dot(_tri(lower).astype(F32).astype(BF16), jnp.concatenate([hi, mid, lo], axis=1))
    return parts[:, :w] + parts[:, w:2 * w] + parts[:, 2 * w:]


def _stacked_causal():
    i = lax.broadcasted_iota(jnp.int32, (HEAD_ROWS_ALL, CHUNK), 0) % CHUNK
    j = lax.broadcasted_iota(jnp.int32, (HEAD_ROWS_ALL, CHUNK), 1)
    return i >= j


GLA_GROUP = 3


def _gla_chunk(q, k, v, lr, w2, gb, first_chunk):
    z, a, live = _gla_gates(lr, w2, gb, first_chunk)
    yield
    b = _running_sum(a, True)
    yield
    bl = b[CHUNK - 1:CHUNK, :]
    e_pos, e_neg, e_dec = jnp.exp(b), jnp.exp(-b), jnp.exp(bl - b)
    q_f, k_f, kd_f = q * (DK ** -0.5) * e_pos, k * e_neg, k * e_dec
    qx = _expand(q_f, DK).astype(BF16)
    k_in, k_dec, v_b = k_f.astype(BF16), kd_f.astype(BF16), v.astype(BF16)
    s = jnp.where(_stacked_causal(), _dot_nt(qx, k_in), 0.0).astype(BF16)
    yield
    p = _dot(s, v_b)
    yield
    o_intra = jnp.concatenate([p[h * CHUNK:(h + 1) * CHUNK, h * DV:(h + 1) * DV] for h in range(N_HEADS)], axis=0)
    return dict(z=z, live=live, bl=bl, e_pos=e_pos, e_neg=e_neg, e_dec=e_dec, q_f=q_f, k_f=k_f, kd_f=kd_f,
                qx=qx, k_in=k_in, k_dec=k_dec, v_b=v_b, s=s, o_intra=o_intra, decay=jnp.exp(bl))


def _gla_fwd(u, w2, gb, ng, n_ex, lp, plan=None):
    nc = lp // CHUNK
    t = n_ex * lp
    rows_of = lambda j: pl.ds(j * CHUNK, CHUNK)

    def body(qk_ref, v_ref, g_ref, lr_ref, w2_ref, gb_ref, ng_ref, y_ref, st_ref, state):
        n = pl.program_id(0)

        @pl.when(n == 0)
        def _():
            state[...] = jnp.zeros_like(state)

        carried = [state[e] for e in range(n_ex)]

        def one_chunk(e, j):
            rows = rows_of(j)
            qk = qk_ref[e, rows, :]
            first = jnp.logical_and(n == 0, j == 0)
            c = yield from _gla_chunk(qk[:, :GLA_K], qk[:, GLA_K:], v_ref[e, rows, :], lr_ref[e, rows, :],
                                      w2_ref[...], gb_ref[...], first)
            kv = _fold(_dot_tn(c["v_b"], c["k_dec"]), DV)
            g = _rows_by_head(g_ref[e, rows, :])
            gate = ng_ref[...] * (g * _sigmoid(g))
            yield
            for _ in range(j):
                yield
            st = carried[e]
            st_ref[e, pl.ds(j * DV, DV), :] = st
            o = c["o_intra"] + _dot_nt(c["qx"], st.astype(BF16))
            rstd = lax.rsqrt(jnp.mean(o * o, axis=-1, keepdims=True) + RMS_EPS)
            y_ref[e, rows, :] = _lanes_by_head(o * rstd * gate).astype(BF16)
            carried[e] = c["decay"] * st + kv

        _in_lockstep(one_chunk(e, j) for j in range(GLA_GROUP) for e in range(n_ex))
        for e in range(n_ex):
            state[e] = carried[e]

    u3 = u.reshape(n_ex, lp, D_IN_PAD)
    blk = lambda w, col: pl.BlockSpec((n_ex, GLA_GROUP * CHUNK, w), lambda n: (0, n, col))
    (y, states), extra = _call(
        body, name="gla_fwd", grid=(nc // GLA_GROUP,),
        in_specs=[blk(2 * GLA_K, 2), blk(GLA_V, 3), blk(GLA_V, 4), blk(128, 20),
                  _const_spec((128, GLA_K)), _const_spec((1, GLA_K)), _const_spec((1, DV))],
        out_specs=[blk(GLA_V, 0), pl.BlockSpec((n_ex, GLA_GROUP * DV, GLA_K), lambda n: (0, n, 0))],
        out_shape=[jax.ShapeDtypeStruct((n_ex, lp, GLA_V), BF16),
                   jax.ShapeDtypeStruct((n_ex, nc * DV, GLA_K), F32)],
        scratch_shapes=[pltpu.VMEM((n_ex, DV, GLA_K), F32)],
        plan=plan,
    )(u3, u3, u3, u3, w2, gb, ng)
    return (y.reshape(t, GLA_V), states), extra


def _gla_bwd(dycat, u, states, w2, gb, ng, n_ex, lp, plan=None):
    nc = lp // CHUNK
    t = n_ex * lp

    def body(dy_ref, qk_ref, v_ref, g_ref, lr_ref, st_ref, w2_ref, gb_ref, ng_ref,
             du_ref, dw2_ref, dgb_ref, dng_ref, dstate):
        n = pl.program_id(0)
        group = nc // GLA_GROUP - 1 - n

        @pl.when(n == 0)
        def _():
            dw2_ref[...] = jnp.zeros_like(dw2_ref)
            dgb_ref[...] = jnp.zeros_like(dgb_ref)
            dng_ref[...] = jnp.zeros_like(dng_ref)
            dstate[...] = jnp.zeros_like(dstate)

        carried = [dstate[e] for e in range(n_ex)]

        def one_chunk(e, order):
            j = GLA_GROUP - 1 - order
            rows = pl.ds(j * CHUNK, CHUNK)
            qk = qk_ref[e, rows, :]
            lr = lr_ref[e, rows, :]
            st = st_ref[e, pl.ds(j * DV, DV), :]
            first = jnp.logical_and(group == 0, j == 0)
            c = yield from _gla_chunk(qk[:, :GLA_K], qk[:, GLA_K:], v_ref[e, rows, :], lr, w2_ref[...], gb_ref[...],
                                      first)
            qx, k_in, k_dec, v_b, s = c["qx"], c["k_in"], c["k_dec"], c["v_b"], c["s"]
            st_b = st.astype(BF16)
            o = c["o_intra"] + _dot_nt(qx, st_b)
            ngv = ng_ref[...]
            yield
            rstd = lax.rsqrt(jnp.mean(o * o, axis=-1, keepdims=True) + RMS_EPS)
            nrm = o * rstd
            g = _rows_by_head(g_ref[e, rows, :])
            dy = _rows_by_head(dy_ref[e, rows, :])
            sg = _sigmoid(g)
            dg = dy * nrm * ngv * (sg * (1.0 + g * (1.0 - sg)))
            dt = dy * (g * sg)
            dng_ref[...] += jnp.sum(dt * nrm, axis=0, keepdims=True)
            dn = dt * ngv
            do = rstd * (dn - nrm * jnp.mean(dn * nrm, axis=-1, keepdims=True))
            do_b = do.astype(BF16)
            dox = _expand_lanes(do).astype(BF16)
            yield
            da = jnp.where(_stacked_causal(), _dot_nt(dox, v_b), 0.0).astype(BF16)
            dv_intra = _dot_tn(s, dox)
            dst_own = _dot_tn(do_b, qx)
            yield
            dq_in = _fold(_dot(da, k_in) + _dot(do_b, st_b), CHUNK)
            dk_in = _dot_tn(da, qx)
            dq = dq_in * (DK ** -0.5) * c["e_pos"]
            yield
            for _ in range(order):
                yield
            dst = carried[e]
            dstx = _expand_state(dst).astype(BF16)
            dv = dv_intra + _dot_nt(k_dec, dstx)
            dk_dec = _dot(v_b, dstx)
            carried[e] = dst_own + c["decay"] * dst
            yield
            dbl = (jnp.sum(dk_dec * c["kd_f"], axis=0, keepdims=True)
                   + c["decay"] * jnp.sum(dst * st, axis=0, keepdims=True))
            dk = dk_in * c["e_neg"] + dk_dec * c["e_dec"]
            db = dq_in * c["q_f"] - dk_in * c["k_f"] - dk_dec * c["kd_f"]
            row = lax.broadcasted_iota(jnp.int32, (CHUNK, 1), 0)
            da_log = _running_sum(db + jnp.where(row == CHUNK - 1, dbl, 0.0), False)
            yield
            dz = jnp.where(c["live"], da_log * (1.0 - _sigmoid(c["z"])) * (1.0 / GATE_TAU), 0.0)
            dz_b = dz.astype(BF16)
            out = du_ref.at[e, rows, :]
            out[:, 0:GLA_K] = dq.astype(BF16)
            out[:, GLA_K:2 * GLA_K] = dk.astype(BF16)
            out[:, 2 * GLA_K:2 * GLA_K + GLA_V] = dv.astype(BF16)
            out[:, 2 * GLA_K + GLA_V:2 * GLA_K + 2 * GLA_V] = _lanes_by_head(dg).astype(BF16)
            out[:, 2 * GLA_K + 2 * GLA_V:] = _dot_nt(dz_b, w2_ref[...]).astype(BF16)
            dw2_ref[...] += _dot_tn(lr.astype(BF16), dz_b)
            dgb_ref[...] += jnp.sum(dz, axis=0, keepdims=True)

        _in_lockstep(one_chunk(e, order) for order in range(GLA_GROUP) for e in range(n_ex))
        for e in range(n_ex):
            dstate[e] = carried[e]

    u3 = u.reshape(n_ex, lp, D_IN_PAD)
    rev = lambda w, col: pl.BlockSpec((n_ex, GLA_GROUP * CHUNK, w), lambda n: (0, nc // GLA_GROUP - 1 - n, col))
    (du, d_w2, d_gb, d_ng), extra = _call(
        body, name="gla_bwd", grid=(nc // GLA_GROUP,),
        in_specs=[rev(GLA_V, 1), rev(2 * GLA_K, 2), rev(GLA_V, 3), rev(GLA_V, 4), rev(128, 20),
                  pl.BlockSpec((n_ex, GLA_GROUP * DV, GLA_K), lambda n: (0, nc // GLA_GROUP - 1 - n, 0)),
                  _const_spec((128, GLA_K)), _const_spec((1, GLA_K)), _const_spec((1, DV))],
        out_specs=[rev(D_GLA_IN, 0), _acc_spec((128, GLA_K)), _acc_spec((1, GLA_K)), _acc_spec((1, DV))],
        out_shape=[jax.ShapeDtypeStruct((n_ex, lp, D_GLA_IN), BF16), jax.ShapeDtypeStruct((128, GLA_K), F32),
                   jax.ShapeDtypeStruct((1, GLA_K), F32), jax.ShapeDtypeStruct((1, DV), F32)],
        scratch_shapes=[pltpu.VMEM((n_ex, DV, GLA_K), F32)],
        plan=plan,
    )(dycat.reshape(n_ex, lp, D), u3, u3, u3, u3, states, w2, gb, ng)
    return (du.reshape(t, D_GLA_IN), d_w2, d_gb, d_ng), extra


def _in_proj_bwd(du_conv, du_gla, w_in_t_conv, w_in_t_gla, h0, dh1, g_mix, plan=None):
    t = h0.shape[0]
    r = _row_tile(t, 384)

    def body(dc_ref, dg_ref, wc_ref, wg_ref, h_ref, dh1_ref, g_ref, dh0_ref, dgm_ref):
        @pl.when(pl.program_id(0) == 0)
        def _():
            dgm_ref[...] = jnp.zeros_like(dgm_ref)

        dhn = _dot(dc_ref[...], wc_ref[...]) + _dot(dg_ref[...], wg_ref[...])
        h = h_ref[...]
        rstd = lax.rsqrt(jnp.mean(h * h, axis=-1, keepdims=True) + RMS_EPS)
        nrm = h * rstd
        dgm_ref[...] += jnp.sum(dhn * nrm, axis=0, keepdims=True)
        dn = dhn * g_ref[...]
        dh0_ref[...] = dh1_ref[...] + rstd * (dn - nrm * jnp.mean(dn * nrm, axis=-1, keepdims=True))

    rows = lambda w: pl.BlockSpec((r, w), lambda i: (i, 0))
    return _call(
        body, name="in_proj_bwd", grid=(t // r,),
        in_specs=[rows(2 * C_CONV), rows(D_GLA_IN), _const_spec((2 * C_CONV, D)), _const_spec((D_GLA_IN, D)),
                  rows(D), rows(D), _const_spec((1, D))],
        out_specs=[rows(D), _acc_spec((1, D))],
        out_shape=[jax.ShapeDtypeStruct((t, D), F32), jax.ShapeDtypeStruct((1, D), F32)],
        plan=plan,
    )(du_conv, du_gla, w_in_t_conv, w_in_t_gla, h0, dh1, g_mix)


def _wgrad(x, dy, name, plan=None):
    t, m = x.shape
    n = dy.shape[1]
    tk = t // 3 if t % (3 * 128) == 0 else _row_tile(t, 384)
    tm = m if m <= D_GLA_IN else m // 2

    def body(x_ref, dy_ref, o_ref):
        @pl.when(pl.program_id(1) == 0)
        def _():
            o_ref[...] = jnp.zeros_like(o_ref)

        o_ref[...] += _dot_tn(x_ref[...].astype(BF16), dy_ref[...].astype(BF16))

    (out,), extra = _call(
        body, name=name, grid=(m // tm, t // tk),
        in_specs=[pl.BlockSpec((tk, tm), lambda i, k: (k, i)), pl.BlockSpec((tk, n), lambda i, k: (k, 0))],
        out_specs=[pl.BlockSpec((tm, n), lambda i, k: (i, 0))],
        out_shape=[jax.ShapeDtypeStruct((m, n), F32)],
        plan=plan,
    )(x, dy)
    return out, extra


def _adam_update(g, w, m, v):
    m2 = ADAM_B1 * m + (1.0 - ADAM_B1) * g
    v2 = ADAM_B2 * v + (1.0 - ADAM_B2) * (g * g)
    m_hat = m2 / (1.0 - ADAM_B1 ** ADAM_STEP)
    v_hat = v2 / (1.0 - ADAM_B2 ** ADAM_STEP)
    delta = -ADAM_LR * (m_hat / (jnp.sqrt(v_hat) + ADAM_EPS) + ADAM_WD * w)
    return delta, m2, v2


def _adamw(g, w, m, v, name):
    def body(g_ref, w_ref, m_ref, v_ref, d_ref, m2_ref, v2_ref):
        d_ref[...], m2_ref[...], v2_ref[...] = _adam_update(g_ref[...], w_ref[...], m_ref[...], v_ref[...])

    spec = pl.BlockSpec(g.shape, lambda i: (0, 0))
    return pl.pallas_call(
        body, name=name, grid=(1,), in_specs=[spec] * 4, out_specs=[spec] * 3,
        out_shape=[jax.ShapeDtypeStruct(g.shape, F32)] * 3, compiler_params=_params(1),
    )(g, w, m, v)


ADAMW_STEPS = 4


def _adamw_halves(items, c, name):
    n = len(items)
    h = items[0][0].shape[1]
    steps = ADAMW_STEPS if all(it[0].shape[0] % (ADAMW_STEPS * SUBLANES) == 0 for it in items) else 1

    def body(c_ref, *refs):
        ins, outs = refs[:5 * n], refs[5 * n:]
        own = pl.program_id(1) == c_ref[0]
        for i in range(n):
            a_ref, b_ref, w_ref, m_ref, v_ref = ins[5 * i:5 * i + 5]
            go_ref, d_ref, m2_ref, v2_ref = outs[4 * i:4 * i + 4]
            g = jnp.where(own, a_ref[...], b_ref[...])
            go_ref[...] = g
            d_ref[...], m2_ref[...], v2_ref[...] = _adam_update(g, w_ref[...], m_ref[...], v_ref[...])

    in_specs, out_specs, out_shape, args = [pl.BlockSpec(memory_space=pltpu.SMEM)], [], [], []
    for mine, theirs, w, m, v in items:
        tr = mine.shape[0] // steps
        half = pl.BlockSpec((tr, h), lambda i, j: (i, 0))
        full = pl.BlockSpec((tr, h), lambda i, j: (i, j))
        in_specs += [half, half, full, full, full]
        out_specs += [full] * 4
        out_shape += [jax.ShapeDtypeStruct(w.shape, F32)] * 4
        args += [mine, theirs, w, m, v]
    res = pl.pallas_call(
        body, name=name, grid=(steps, 2), in_specs=in_specs, out_specs=out_specs, out_shape=out_shape,
        compiler_params=_params(2),
    )(jnp.reshape(c, (1,)).astype(jnp.int32), *args)
    return [res[4 * i:4 * i + 4] for i in range(n)]


def _rs_add_halves(g, recv, c, name):
    _, rows, w = g.shape
    h = w // 2
    tr = rows // 2 if rows % 16 == 0 and rows > 64 else rows

    def body(c_ref, a_ref, b_ref, o_ref):
        o_ref[...] = (a_ref[...] + b_ref[...]).astype(BF16)

    return pl.pallas_call(
        body, name=name,
        grid_spec=pltpu.PrefetchScalarGridSpec(
            num_scalar_prefetch=1, grid=(N_CHIPS, rows // tr),
            in_specs=[pl.BlockSpec((1, tr, h), lambda j, i, s: (j, i, s[0])),
                      pl.BlockSpec((1, tr, h), lambda j, i, s: (j, i, 0))],
            out_specs=pl.BlockSpec((1, tr, h), lambda j, i, s: (j, i, 0))),
        out_shape=jax.ShapeDtypeStruct((N_CHIPS, rows, h), BF16),
        compiler_params=_params(2),
    )(jnp.reshape(c, (1,)).astype(jnp.int32), g, recv)


def _rs_sum(own, others, mine, name):
    _, rows, h = own.shape
    tr = rows // 2 if rows % 16 == 0 and rows > 64 else rows

    def body(mine_ref, own_ref, oth_ref, o_ref):
        p = oth_ref[...].astype(F32)
        o_ref[...] = ((own_ref[0].astype(F32) + p[0]) + p[1]) + p[2]

    return pl.pallas_call(
        body, name=name,
        grid_spec=pltpu.PrefetchScalarGridSpec(
            num_scalar_prefetch=1, grid=(rows // tr,),
            in_specs=[pl.BlockSpec((1, tr, h), lambda i, s: (s[0], i, 0)),
                      pl.BlockSpec((3, tr, h), lambda i, s: (0, i, 0))],
            out_specs=pl.BlockSpec((tr, h), lambda i, s: (i, 0))),
        out_shape=jax.ShapeDtypeStruct((rows, h), F32),
        compiler_params=_params(1),
    )(jnp.reshape(mine, (1,)).astype(jnp.int32), own, others)


def _sum_slots_adamw(slots, late_slots, w, m, v):
    late_rows = late_slots.shape[1]

    def body(s_ref, l_ref, w_ref, m_ref, v_ref, g_ref, d_ref, m2_ref, v2_ref):
        g, late = s_ref[0], l_ref[0]
        for d in range(1, 8):
            g = g + s_ref[d]
            late = late + l_ref[d]
        g = jnp.concatenate([g[:late_rows] + late, g[late_rows:]], axis=0)
        g_ref[...] = g
        d_ref[...], m2_ref[...], v2_ref[...] = _adam_update(g, w_ref[...], m_ref[...], v_ref[...])

    vm = pl.BlockSpec(memory_space=pltpu.VMEM)
    shape = jax.ShapeDtypeStruct(w.shape, F32)
    return pl.pallas_call(body, name="small_sum_adamw", in_specs=[vm] * 5, out_specs=[vm] * 4,
                          out_shape=[shape] * 4)(slots, late_slots, w, m, v)


def _mesh_pos():
    return lax.axis_index("x"), lax.axis_index("y"), lax.axis_index("c")


def _other_chips(x, y):
    return [(1 - x, y), (x, 1 - y), (1 - x, 1 - y)]


def _half(ref, c, axis):
    n = ref.shape[axis] // 2
    return ref.at[(slice(None),) * axis + (pl.ds(c * n, n),)]


def _remote(src, dst, send_sem, recv_sem, device):
    return pltpu.make_async_remote_copy(src_ref=src, dst_ref=dst, send_sem=send_sem, recv_sem=recv_sem,
                                        device_id=device, device_id_type=MESH)


def _gather_plan(split, whole=(), axes=None):
    split, whole = list(split), list(whole)
    ns, n = len(split), len(split) + len(whole)

    def make(ins, outs, sems):
        ici_send, ici_recv, d2d_send, d2d_recv, own_send, own_recv = sems
        x, y, c = _mesh_pos()
        mine = 2 * x + y
        chips = _other_chips(x, y)
        blocks = [2 * px + py for px, py in chips]

        def own(a):
            return _remote(ins[a], outs[a].at[mine], own_send.at[a], own_recv.at[a], (x, y, 1 - c))

        def ici(a, k, block):
            px, py = chips[k]
            src, dst = ins[a], outs[a].at[block]
            if a < ns:
                src, dst = _half(src, c, axes[a]), _half(dst, c, axes[a])
            return _remote(src, dst, ici_send.at[3 * a + k], ici_recv.at[3 * a + k], (px, py, c))

        def d2d(a, k, half):
            part = _half(outs[a].at[blocks[k]], half, axes[a])
            return _remote(part, part, d2d_send.at[3 * a + k], d2d_recv.at[3 * a + k], (x, y, 1 - c))

        def start():
            for a in range(n):
                for k in range(3):
                    ici(a, k, mine).start()
                own(a).start()

        def finish():
            for a in range(n):
                for k in range(3):
                    ici(a, k, blocks[k]).wait_recv()
                    if a < ns:
                        d2d(a, k, c).start()
            for a in range(ns):
                for k in range(3):
                    d2d(a, k, 1 - c).wait_recv()
            for a in range(n):
                for k in range(3):
                    ici(a, k, mine).wait_send()
                    if a < ns:
                        d2d(a, k, c).wait_send()
                own(a).wait()

        return start, finish

    arrays = split + whole
    axes = [0] * ns if axes is None else list(axes)
    return _Plan(arrays, [jax.ShapeDtypeStruct((N_CHIPS,) + s.shape, s.dtype) for s in arrays],
                 [pltpu.SemaphoreType.DMA((3 * n,)), pltpu.SemaphoreType.DMA((3 * n,)),
                  pltpu.SemaphoreType.DMA((3 * ns,)), pltpu.SemaphoreType.DMA((3 * ns,)),
                  pltpu.SemaphoreType.DMA((n,)), pltpu.SemaphoreType.DMA((n,))], make)


def _to_sibling_plan(gs):
    n = len(gs)

    def make(ins, outs, sems):
        send_sems, recv_sems = sems
        x, y, c = _mesh_pos()

        def copy(a):
            return _remote(_half(ins[a], 1 - c, 2), outs[a], send_sems.at[a], recv_sems.at[a], (x, y, 1 - c))

        def start():
            for a in range(n):
                copy(a).start()

        def finish():
            for a in range(n):
                copy(a).wait()

        return start, finish

    return _Plan(list(gs), [jax.ShapeDtypeStruct(g.shape[:2] + (g.shape[2] // 2,), g.dtype) for g in gs],
                 [pltpu.SemaphoreType.DMA((n,)), pltpu.SemaphoreType.DMA((n,))], make)


def _chip_exchange_plan(ps):
    n = len(ps)

    def make(ins, outs, sems):
        send_sems, recv_sems = sems
        x, y, c = _mesh_pos()
        chips = _other_chips(x, y)

        def ici(a, k):
            px, py = chips[k]
            return _remote(ins[a].at[2 * px + py], outs[a].at[k], send_sems.at[3 * a + k],
                           recv_sems.at[3 * a + k], (px, py, c))

        def start():
            for a in range(n):
                for k in range(3):
                    ici(a, k).start()

        def finish():
            for a in range(n):
                for k in range(3):
                    ici(a, k).wait()

        return start, finish

    return _Plan(list(ps), [jax.ShapeDtypeStruct((3,) + p.shape[1:], p.dtype) for p in ps],
                 [pltpu.SemaphoreType.DMA((3 * n,)), pltpu.SemaphoreType.DMA((3 * n,))], make)


def _share_plan(halves):
    n = len(halves)

    def make(ins, outs, sems):
        send_sems, recv_sems = sems
        x, y, c = _mesh_pos()

        def d2d(a):
            return _remote(ins[a], outs[a], send_sems.at[a], recv_sems.at[a], (x, y, 1 - c))

        def start():
            for a in range(n):
                d2d(a).start()

        def finish():
            for a in range(n):
                d2d(a).wait()

        return start, finish

    return _Plan(list(halves), [jax.ShapeDtypeStruct(p.shape, p.dtype) for p in halves],
                 [pltpu.SemaphoreType.DMA((n,)), pltpu.SemaphoreType.DMA((n,))], make)


def _all_to_all_plan(part):
    def make(ins, outs, sems):
        send_sems, recv_sems, local_sem = sems
        (p_ref,), (slots,) = ins, outs
        x, y, c = _mesh_pos()
        me = 4 * x + 2 * y + c
        peers = [(px, py, pc) for px in (x, 1 - x) for py in (y, 1 - y) for pc in (c, 1 - c)][1:]

        def remote(k, slot):
            return _remote(p_ref, slots.at[slot], send_sems.at[k], recv_sems.at[k], peers[k])

        def local():
            return pltpu.make_async_copy(p_ref, slots.at[me], local_sem)

        def start():
            for k in range(7):
                remote(k, me).start()
            local().start()

        def finish():
            for k, (px, py, pc) in enumerate(peers):
                remote(k, 4 * px + 2 * py + pc).wait_recv()
            for k in range(7):
                remote(k, me).wait_send()
            local().wait()

        return start, finish

    return _Plan([part], [jax.ShapeDtypeStruct((8,) + part.shape, part.dtype)],
                 [pltpu.SemaphoreType.DMA((7,)), pltpu.SemaphoreType.DMA((7,)), pltpu.SemaphoreType.DMA(())], make)


def _merge_plans(a, b):
    na_in, na_out, na_sems = len(a.arrays), len(a.out_shape), len(a.sems)

    def make(ins, outs, sems):
        start_a, finish_a = a.make(ins[:na_in], outs[:na_out], sems[:na_sems])
        start_b, finish_b = b.make(ins[na_in:], outs[na_out:], sems[na_sems:])

        def start():
            start_a()
            start_b()

        def finish():
            finish_a()
            finish_b()

        return start, finish

    return _Plan(list(a.arrays) + list(b.arrays), list(a.out_shape) + list(b.out_shape),
                 list(a.sems) + list(b.sems), make)


def _exchange(plan, name):
    n_in, n_out = len(plan.arrays), len(plan.out_shape)

    def body(*refs):
        start, finish = plan.make(refs[:n_in], refs[n_in:n_in + n_out], refs[n_in + n_out:])
        start()
        finish()

    return pl.pallas_call(
        body, name=name, in_specs=[HBM_SPEC] * n_in, out_specs=[HBM_SPEC] * n_out, out_shape=list(plan.out_shape),
        scratch_shapes=list(plan.sems), compiler_params=pltpu.CompilerParams(has_side_effects=True),
    )(*plan.arrays)


def _pack_small(parts):
    rows = []
    for r in range(SMALL_ROWS):
        pieces, col = [], 0
        for name, row, start, size in SMALL_PARTS:
            if row == r:
                assert start == col
                pieces.append(parts[name].reshape(1, size).astype(F32))
                col += size
        rows.append(jnp.concatenate(pieces + [jnp.zeros((1, D - col), F32)], axis=1))
    return jnp.concatenate(rows, axis=0)


def _unpack_small(slab, shapes):
    return {name: slab[row, col:col + size].reshape(shapes[name]) for name, row, col, size in SMALL_PARTS}


def _columns(gathered):
    return jnp.concatenate([gathered[j] for j in range(N_CHIPS)], axis=1)


def kernel(x, meta_tokens, norm_mix_g, w_in, conv_w, conv_b, conv_ln_g, conv_ln_b, gla_w_gate2, gla_gate_b, gla_norm_g, w_out, norm_ffn_g, w_ffn_gate, w_ffn_up, w_ffn_down, norm_final_g, loss_target, m_meta_tokens, m_norm_mix_g, m_w_in, m_conv_w, m_conv_b, m_conv_ln_g, m_conv_ln_b, m_gla_w_gate2, m_gla_gate_b, m_gla_norm_g, m_w_out, m_norm_ffn_g, m_w_ffn_gate, m_w_ffn_up, m_w_ffn_down, m_norm_final_g, v_meta_tokens, v_norm_mix_g, v_w_in, v_conv_w, v_conv_b, v_conv_ln_g, v_conv_ln_b, v_gla_w_gate2, v_gla_gate_b, v_gla_norm_g, v_w_out, v_norm_ffn_g, v_w_ffn_gate, v_w_ffn_up, v_w_ffn_down, v_norm_final_g):
    ws = dict(zip(WEIGHT_NAMES, (meta_tokens, norm_mix_g, w_in, conv_w, conv_b, conv_ln_g, conv_ln_b, gla_w_gate2,
                                 gla_gate_b, gla_norm_g, w_out, norm_ffn_g, w_ffn_gate, w_ffn_up, w_ffn_down,
                                 norm_final_g)))
    ms = dict(zip(WEIGHT_NAMES, (m_meta_tokens, m_norm_mix_g, m_w_in, m_conv_w, m_conv_b, m_conv_ln_g, m_conv_ln_b,
                                 m_gla_w_gate2, m_gla_gate_b, m_gla_norm_g, m_w_out, m_norm_ffn_g, m_w_ffn_gate,
                                 m_w_ffn_up, m_w_ffn_down, m_norm_final_g)))
    vs = dict(zip(WEIGHT_NAMES, (v_meta_tokens, v_norm_mix_g, v_w_in, v_conv_w, v_conv_b, v_conv_ln_g, v_conv_ln_b,
                                 v_gla_w_gate2, v_gla_gate_b, v_gla_norm_g, v_w_out, v_norm_ffn_g, v_w_ffn_gate,
                                 v_w_ffn_up, v_w_ffn_down, v_norm_final_g)))
    c = lax.axis_index("c")
    mine = 2 * lax.axis_index("x") + lax.axis_index("y")
    shard = lambda d, name: d[name].reshape(d[name].shape[-2:])
    vec = {name: ws[name].reshape(1, -1) for name, _, _, _ in SMALL_PARTS}
    n_ex, seq, _ = x.shape
    lp = HEAD_ROWS + seq
    t = n_ex * lp

    (tgt, h0), (w_in_g, meta_g, conv_w_g, w2_g) = _pad_head_rows([loss_target, x], plan=_gather_plan(
        [shard(ws, "w_in").T.astype(BF16)],
        [shard(ws, "meta_tokens"), shard(ws, "conv_w"), shard(ws, "gla_w_gate2")], axes=[1]))
    w_in_t = jnp.concatenate([w_in_g.reshape(D_IN, D), jnp.zeros((D_IN_PAD - D_IN, D), BF16)], axis=0)
    conv_w_full = jnp.concatenate([_columns(conv_w_g), jnp.zeros((32 - CONV_W, C_CONV), F32)], axis=0)
    w2_full = jnp.concatenate([_columns(w2_g), jnp.zeros((128 - RANK, GLA_K), F32)], axis=0).astype(BF16)
    h0 = _set_meta_rows(h0, _columns(meta_g)).reshape(t, D)
    tgt = tgt.reshape(t, D)
    row_mask = jnp.concatenate([jnp.zeros((n_ex, HEAD_ROWS, 1), F32), jnp.ones((n_ex, seq, 1), F32)],
                               axis=1).reshape(t, 1)

    (u, hn), (gate_g,) = _in_proj(h0, vec["norm_mix_g"], w_in_t.T,
                                  plan=_gather_plan([shard(ws, "w_ffn_gate").T.astype(BF16)]))
    (yc, y_conv), (up_g, w_out_g) = _conv_fwd(
        u, conv_w_full, vec["conv_b"], vec["conv_ln_g"], vec["conv_ln_b"], n_ex, lp,
        plan=_gather_plan([shard(ws, "w_ffn_up").T.astype(BF16), shard(ws, "w_out").astype(BF16)]))
    (y_gla, states), _ = _gla_fwd(u, w2_full, vec["gla_gate_b"], vec["gla_norm_g"], n_ex, lp)
    w_out_full = w_out_g.reshape(D, D)
    w_gate_t, w_up_t = gate_g.reshape(D_FF, D), up_g.reshape(D_FF, D)
    (h1, hn2, gate, up, act), (down_g,) = _mix_out_ffn_up(
        h0, y_conv, y_gla, w_out_full, vec["norm_ffn_g"], w_gate_t.T, w_up_t.T,
        plan=_gather_plan([shard(ws, "w_ffn_down").astype(BF16)]))
    w_down_full = down_g.reshape(D_FF, D)
    dh2, loss, d_final_g = _ffn_down_loss(act, w_down_full, h1, tgt, vec["norm_final_g"], row_mask)
    dgate, dup, dh1, dycat, d_ffn_g = _ffn_bwd(dh2, gate, up, h1, w_down_full.T, w_gate_t, w_up_t, w_out_full.T,
                                                vec["norm_ffn_g"])

    early = ("w_ffn_gate", "w_ffn_up", "w_ffn_down", "w_out")
    ffn_block = lambda g: g.reshape(N_CHIPS, D_FF // N_CHIPS, D)
    g_gate = ffn_block(_wgrad(dgate, hn2, "wgrad_gate")[0])
    g_up, (gate_sib,) = _wgrad(dup, hn2, "wgrad_up", _to_sibling_plan([g_gate]))
    g_up = ffn_block(g_up)
    g_down, (up_sib,) = _wgrad(act, dh2, "wgrad_down", _to_sibling_plan([g_up]))
    g_down = ffn_block(g_down)
    g_out = jnp.concatenate([_wgrad(y_conv, dh1, "wgrad_out_conv")[0], _wgrad(y_gla, dh1, "wgrad_out_gla")[0]],
                            axis=0).reshape(N_CHIPS, D // N_CHIPS, D)
    cs_gate = _rs_add_halves(g_gate, gate_sib, c, "rs_add_w_ffn_gate")
    cs_up = _rs_add_halves(g_up, up_sib, c, "rs_add_w_ffn_up")
    (du_conv, d_conv_w, d_conv_b, d_ln_g, d_ln_b), (ex_gate, ex_up, down_sib, out_sib) = _conv_bwd(
        dycat, yc, u, conv_w_full, vec["conv_ln_g"], vec["conv_ln_b"], n_ex, lp,
        plan=_merge_plans(_chip_exchange_plan([cs_gate, cs_up]), _to_sibling_plan([g_down, g_out])))
    cs_down = _rs_add_halves(g_down, down_sib, c, "rs_add_w_ffn_down")
    cs_out = _rs_add_halves(g_out, out_sib, c, "rs_add_w_out")
    (du_gla, d_w2, d_gate_b, d_norm_g), (ex_down, ex_out) = _gla_bwd(
        dycat, u, states, w2_full, vec["gla_gate_b"], vec["gla_norm_g"], n_ex, lp,
        plan=_chip_exchange_plan([cs_down, cs_out]))
    halves = [_rs_sum(own, oth, mine, "rs_sum_" + nm)
              for own, oth, nm in zip((cs_gate, cs_up, cs_down, cs_out), (ex_gate, ex_up, ex_down, ex_out), early)]

    small = {"norm_mix_g": jnp.zeros((1, D), F32), "norm_ffn_g": d_ffn_g, "norm_final_g": d_final_g,
             "conv_b": d_conv_b, "conv_ln_g": d_ln_g, "conv_ln_b": d_ln_b, "gla_gate_b": d_gate_b,
             "gla_norm_g": d_norm_g}
    part = lax.dynamic_update_slice(_pack_small(small), loss[:, :1], (LOSS_ROW, 0))
    part = jnp.concatenate([part, jnp.zeros((N_META, D), F32), d_conv_w.reshape(16, D), d_w2[:RANK].reshape(4, D),
                            jnp.zeros((4, D), F32)], axis=0)
    g_in_gla, (slots,) = _wgrad(du_gla, hn, "wgrad_in_gla", _all_to_all_plan(part))

    d_w_in_t = jnp.concatenate([_wgrad(du_conv, hn, "wgrad_in_conv")[0], g_in_gla],
                               axis=0)[:D_IN].reshape(N_CHIPS, D_IN // N_CHIPS, D)
    (in_from_sibling,) = _exchange(_to_sibling_plan([d_w_in_t]), "rs_late_to_sibling")
    in_chip_sum = _rs_add_halves(d_w_in_t, in_from_sibling, c, "rs_add_w_in")
    (dh0, d_mix_g), shared = _in_proj_bwd(
        du_conv, du_gla, w_in_t[:2 * C_CONV], w_in_t[2 * C_CONV:], h0, dh1, vec["norm_mix_g"],
        plan=_merge_plans(_share_plan(halves), _chip_exchange_plan([in_chip_sum])))
    dh0 = dh0.reshape(n_ex, lp, D)
    grad_x = dh0[:, HEAD_ROWS:]
    late_part = jnp.concatenate([d_mix_g, jnp.zeros((SMALL_ROWS - 1, D), F32),
                                 jnp.sum(dh0[:, PAD_ROWS:HEAD_ROWS], axis=0)], axis=0)
    in_half = _rs_sum(in_chip_sum, shared[4], mine, "rs_sum_w_in")
    in_shared, late_slots = _exchange(_merge_plans(_share_plan([in_half]), _all_to_all_plan(late_part)),
                                      "late_exchange")

    out = {"grad": {}, "delta": {}, "new_m": {}, "new_v": {}}

    def record(name, res, transposed=False):
        for kind, a in zip(("grad", "delta", "new_m", "new_v"), res):
            out[kind][name] = (a.T if transposed else a).reshape(ws[name].shape)

    def operands(name, transposed):
        lay = (lambda a: a.T) if transposed else (lambda a: a)
        return lay(shard(ws, name)), lay(shard(ms, name)), lay(shard(vs, name))

    early_layout = (("w_ffn_gate", True), ("w_ffn_up", True), ("w_ffn_down", False), ("w_out", False))
    items = [(mine_half, their_half, *operands(name, transposed))
             for (name, transposed), mine_half, their_half in zip(early_layout, halves, shared)]
    for (name, transposed), res in zip(early_layout, _adamw_halves(items, c, "adamw_early")):
        record(name, res, transposed)

    record("w_in", _adamw_halves([(in_half, in_shared, *operands("w_in", True))], c, "adamw_w_in")[0], True)

    tall = lambda a: jnp.concatenate([a, jnp.zeros((part.shape[0] - SMALL_ROWS, D), F32)], axis=0)
    g_s, d_s, m_s, v_s = _sum_slots_adamw(slots, late_slots, tall(_pack_small(ws)), tall(_pack_small(ms)),
                                          tall(_pack_small(vs)))
    small_shapes = {name: ws[name].shape for name, _, _, _ in SMALL_PARTS}
    for kind, slab in (("grad", g_s), ("delta", d_s), ("new_m", m_s), ("new_v", v_s)):
        out[kind].update(_unpack_small(slab, small_shapes))
    loss = g_s[LOSS_ROW, 0]
    block = lambda a, width: lax.dynamic_slice_in_dim(a, mine * width, width, axis=1)
    small_sharded = {"meta_tokens": block(g_s[8:24], D // N_CHIPS),
                     "conv_w": block(g_s[24:40].reshape(32, C_CONV), C_CONV // N_CHIPS)[:CONV_W],
                     "gla_w_gate2": block(g_s[40:44].reshape(RANK, GLA_K), GLA_K // N_CHIPS)}
    for name, g in small_sharded.items():
        record(name, [g, *_adamw(g, *operands(name, False), "adamw_" + name)])

    return (loss, grad_x, *[out[kind][name] for kind in ("grad", "delta", "new_m", "new_v") for name in WEIGHT_NAMES])
```

```python
import functools
from typing import Any, Callable, NamedTuple, Sequence

import jax
import jax.numpy as jnp
from jax import lax
from jax.experimental import pallas as pl
from jax.experimental.pallas import tpu as pltpu

F32 = jnp.float32
BF16 = jnp.bfloat16
MESH = pl.DeviceIdType.MESH

D = 1024
N_META = 16
C_CONV = 512
CONV_W = 31
GLA_K = 256
GLA_V = 512
N_HEADS = 4
DK = 64
DV = 128
RANK = 16
CHUNK = 64
PAD_ROWS = CHUNK - N_META
HEAD_ROWS = CHUNK
D_IN = 2576
D_IN_PAD = 2688
D_GLA_IN = D_IN_PAD - 2 * C_CONV
D_FF = 2816
RMS_EPS = 1e-6
LN_EPS = 1e-5
GATE_TAU = 16.0
N_CHIPS = 4

ADAM_LR = 0.001
ADAM_B1 = 0.9
ADAM_B2 = 0.999
ADAM_EPS = 1e-08
ADAM_WD = 0.01
ADAM_STEP = 10

V7X_VMEM_BYTES = 64 * 1024 * 1024
VMEM_LIMIT = V7X_VMEM_BYTES - 8 * 1024 * 1024
SUBLANES = 8
ROW_PART = 128
FFN_BWD_TILE = 192

WEIGHT_NAMES = ("meta_tokens", "norm_mix_g", "w_in", "conv_w", "conv_b", "conv_ln_g", "conv_ln_b", "gla_w_gate2",
                "gla_gate_b", "gla_norm_g", "w_out", "norm_ffn_g", "w_ffn_gate", "w_ffn_up", "w_ffn_down",
                "norm_final_g")

SMALL_ROWS = 8
SMALL_PARTS = (("norm_mix_g", 0, 0, D), ("norm_ffn_g", 1, 0, D), ("norm_final_g", 2, 0, D),
               ("conv_b", 3, 0, C_CONV), ("conv_ln_g", 3, C_CONV, C_CONV), ("conv_ln_b", 4, 0, C_CONV),
               ("gla_gate_b", 4, C_CONV, GLA_K), ("gla_norm_g", 4, C_CONV + GLA_K, DV))
LOSS_ROW = 5

HBM_SPEC = pl.BlockSpec(memory_space=pltpu.HBM)


def _dot(a, b):
    return jnp.dot(a, b, preferred_element_type=F32)


def _dot_nt(a, b):
    return lax.dot_general(a, b, (((1,), (1,)), ((), ())), preferred_element_type=F32)


def _dot_tn(a, b):
    return lax.dot_general(a, b, (((0,), (0,)), ((), ())), preferred_element_type=F32)


def _sigmoid(x):
    return 1.0 / (1.0 + jnp.exp(-x))


def _const_spec(shape):
    return pl.BlockSpec(shape, lambda *_: (0,) * len(shape), pipeline_mode=pl.Buffered(1))


def _acc_spec(shape):
    return pl.BlockSpec(shape, lambda *_: (0,) * len(shape))


def _params(n_axes):
    return pltpu.CompilerParams(dimension_semantics=("arbitrary",) * n_axes, vmem_limit_bytes=VMEM_LIMIT)


def _row_tile(t, want):
    for r in (want, 384, 192, 128, 64):
        if r <= want and t % r == 0:
            return r
    raise ValueError(f"no row tile for {t}")


def _row_parts(r):
    if r % ROW_PART:
        return [slice(None)]
    return [pl.ds(i * ROW_PART, ROW_PART) for i in range(r // ROW_PART)]


def _in_lockstep(bodies):
    live = list(bodies)
    while live:
        still = []
        for g in live:
            try:
                next(g)
                still.append(g)
            except StopIteration:
                pass
        live = still


class _Plan(NamedTuple):
    arrays: Sequence[Any]
    out_shape: Sequence[Any]
    sems: Sequence[Any]
    make: Callable


def _phases(made):
    return made if len(made) == 3 else (made[0], lambda: None, made[1])


def _call(body, *, name, grid, in_specs, out_specs, out_shape, scratch_shapes=(), plan=None):
    n_in, n_out, n_scr = len(in_specs), len(out_specs), len(scratch_shapes)
    if plan is None:
        plan = _Plan([], [], [], lambda ins, outs, sems: (lambda: None, lambda: None))
    nx_in, nx_out = len(plan.arrays), len(plan.out_shape)
    n_steps = functools.reduce(lambda a, b: a * b, grid)

    def hosted(*refs):
        ins, xins = refs[:n_in], refs[n_in:n_in + nx_in]
        o0 = n_in + nx_in
        outs, xouts = refs[o0:o0 + n_out], refs[o0 + n_out:o0 + n_out + nx_out]
        s0 = o0 + n_out + nx_out
        scr, sems = refs[s0:s0 + n_scr], refs[s0 + n_scr:]
        step = functools.reduce(lambda acc, a: acc * grid[a] + pl.program_id(a), range(len(grid)), 0)
        start, relay, finish = _phases(plan.make(xins, xouts, sems))
        pl.when(step == 0)(start)
        pl.when(step == (3 * n_steps) // 4)(relay)
        body(*ins, *outs, *scr)
        pl.when(step == n_steps - 1)(finish)

    call = pl.pallas_call(
        hosted, name=name, grid=grid, in_specs=list(in_specs) + [HBM_SPEC] * nx_in,
        out_specs=list(out_specs) + [HBM_SPEC] * nx_out, out_shape=list(out_shape) + list(plan.out_shape),
        scratch_shapes=list(scratch_shapes) + list(plan.sems),
        compiler_params=pltpu.CompilerParams(dimension_semantics=("arbitrary",) * len(grid),
                                             vmem_limit_bytes=VMEM_LIMIT, has_side_effects=nx_in > 0))

    def run(*args):
        res = call(*args, *plan.arrays)
        return res[:n_out], res[n_out:]

    return run


def _pad_head_rows(arrays, plan=None):
    n_ex, seq, _ = arrays[0].shape
    nc = (HEAD_ROWS + seq) // CHUNK
    n = len(arrays)

    def body(*refs):
        for a_ref, o_ref in zip(refs[:n], refs[n:]):
            o_ref[...] = jnp.where(pl.program_id(0) > 0, a_ref[...], 0.0)

    return _call(
        body, name="pad_head_rows", grid=(nc,),
        in_specs=[pl.BlockSpec((n_ex, CHUNK, D), lambda i: (0, jnp.maximum(i - 1, 0), 0))] * n,
        out_specs=[pl.BlockSpec((n_ex, CHUNK, D), lambda i: (0, i, 0))] * n,
        out_shape=[jax.ShapeDtypeStruct((n_ex, HEAD_ROWS + seq, D), F32)] * n,
        plan=plan,
    )(*arrays)


def _set_meta_rows(h0, meta):
    n_ex = h0.shape[0]

    def body(h_ref, meta_ref, o_ref):
        o_ref[...] = jnp.concatenate(
            [h_ref[:, :PAD_ROWS, :], jnp.broadcast_to(meta_ref[...][None], (n_ex, N_META, D))], axis=1)

    head = pl.BlockSpec((n_ex, HEAD_ROWS, D), lambda i: (0, 0, 0))
    return pl.pallas_call(
        body, name="set_meta_rows", grid=(1,), in_specs=[head, pl.BlockSpec((N_META, D), lambda i: (0, 0))],
        out_specs=head, out_shape=jax.ShapeDtypeStruct(h0.shape, F32), input_output_aliases={0: 0},
        compiler_params=_params(1),
    )(h0, meta)


def _in_proj(h0, g_mix, w_in, plan=None):
    t = h0.shape[0]
    r = _row_tile(t, 384)

    def body(h_ref, g_ref, w_ref, u_ref, hn_ref):
        h = h_ref[...]
        rstd = lax.rsqrt(jnp.mean(h * h, axis=-1, keepdims=True) + RMS_EPS)
        hn = (h * rstd * g_ref[...]).astype(BF16)
        hn_ref[...] = hn
        u_ref[...] = _dot(hn, w_ref[...])

    return _call(
        body, name="in_proj", grid=(t // r,),
        in_specs=[pl.BlockSpec((r, D), lambda i: (i, 0)), _const_spec((1, D)), _const_spec((D, D_IN_PAD))],
        out_specs=[pl.BlockSpec((r, D_IN_PAD), lambda i: (i, 0)), pl.BlockSpec((r, D), lambda i: (i, 0))],
        out_shape=[jax.ShapeDtypeStruct((t, D_IN_PAD), F32), jax.ShapeDtypeStruct((t, D), BF16)],
        plan=plan,
    )(h0, g_mix, w_in)


CONV_TILE = 192
CONV_SUB = 32
CONV_LEAD = CONV_SUB - (CONV_W - 1)


def _shifted_copies(src, dst, r):
    for s in range(1, SUBLANES):
        dst[s - 1] = src[s:s + r + CONV_SUB - SUBLANES, :]


def _shifted_rows(src, shifted, start):
    base, s = SUBLANES * (start // SUBLANES), start % SUBLANES
    if s == 0:
        return src[base:base + CONV_SUB, :]
    return shifted[s - 1, base:base + CONV_SUB, :]


def _conv_fwd(u, conv_w, conv_b, ln_g, ln_b, n_ex, lp, plan=None):
    r = CONV_TILE
    nt = lp // r
    hb = r // CONV_SUB

    def body(cur_ref, prev_ref, w_ref, b_ref, lg_ref, lb_ref, yc_ref, y_ref, glu, glu_sh):
        i = pl.program_id(1)
        cur = cur_ref[...]
        glu[CONV_SUB:CONV_SUB + r, :] = cur[:, :C_CONV] * _sigmoid(cur[:, C_CONV:])
        pv = prev_ref[...]
        halo = pv[:, :C_CONV] * _sigmoid(pv[:, C_CONV:])
        glu[0:CONV_SUB, :] = jnp.where(i > 0, halo, 0.0)
        _shifted_copies(glu, glu_sh, r)
        w = w_ref[...]
        for j in range(r // CONV_SUB):
            r0 = j * CONV_SUB
            acc = jnp.zeros((CONV_SUB, C_CONV), F32) + b_ref[...]
            for k in range(CONV_W):
                acc = acc + w[k:k + 1, :] * _shifted_rows(glu, glu_sh, r0 + CONV_LEAD + k)
            mu = jnp.mean(acc, axis=-1, keepdims=True)
            cen = acc - mu
            var = jnp.mean(cen * cen, axis=-1, keepdims=True)
            out = cen * lax.rsqrt(var + LN_EPS) * lg_ref[...] + lb_ref[...]
            y = out * _sigmoid(out)
            row = i * r + r0 + lax.broadcasted_iota(jnp.int32, (CONV_SUB, 1), 0)
            y = jnp.where(row >= PAD_ROWS, y, 0.0)
            yc_ref[r0:r0 + CONV_SUB, :] = acc
            y_ref[r0:r0 + CONV_SUB, :] = y.astype(BF16)

    t = n_ex * lp
    return _call(
        body, name="conv_fwd", grid=(n_ex, nt),
        in_specs=[pl.BlockSpec((r, 2 * C_CONV), lambda b, i: (b * nt + i, 0)),
                  pl.BlockSpec((CONV_SUB, 2 * C_CONV), lambda b, i: (jnp.maximum((b * nt + i) * hb - 1, 0), 0)),
                  _const_spec((32, C_CONV)), _const_spec((1, C_CONV)), _const_spec((1, C_CONV)), _const_spec((1, C_CONV))],
        out_specs=[pl.BlockSpec((r, C_CONV), lambda b, i: (b * nt + i, 0)),
                   pl.BlockSpec((r, C_CONV), lambda b, i: (b * nt + i, 0))],
        out_shape=[jax.ShapeDtypeStruct((t, C_CONV), F32), jax.ShapeDtypeStruct((t, C_CONV), BF16)],
        scratch_shapes=[pltpu.VMEM((r + CONV_SUB, C_CONV), F32),
                        pltpu.VMEM((SUBLANES - 1, r + CONV_SUB - SUBLANES, C_CONV), F32)],
        plan=plan,
    )(u, u, conv_w, conv_b, ln_g, ln_b)


def _mix_out_ffn_up(h0, y_conv, y_gla, w_out, g_ffn, w_gate, w_up, plan=None):
    t = h0.shape[0]
    r = _row_tile(t, 384)

    def body(h0_ref, yc_ref, yg_ref, wo_ref, g_ref, wg_ref, wu_ref, h1_ref, hn_ref, gate_ref, up_ref, act_ref):
        h1 = h0_ref[...] + _dot(yc_ref[...], wo_ref[0:C_CONV, :]) + _dot(yg_ref[...], wo_ref[C_CONV:D, :])
        h1_ref[...] = h1
        rstd = lax.rsqrt(jnp.mean(h1 * h1, axis=-1, keepdims=True) + RMS_EPS)
        hn = (h1 * rstd * g_ref[...]).astype(BF16)
        hn_ref[...] = hn
        gate = _dot(hn, wg_ref[...])
        up = _dot(hn, wu_ref[...])
        gate_ref[...] = gate
        up_ref[...] = up
        act_ref[...] = (gate * _sigmoid(gate) * up).astype(BF16)

    rows = lambda w: pl.BlockSpec((r, w), lambda i: (i, 0))
    return _call(
        body, name="mix_out_ffn_up", grid=(t // r,),
        in_specs=[rows(D), rows(C_CONV), rows(GLA_V), _const_spec((D, D)), _const_spec((1, D)),
                  _const_spec((D, D_FF)), _const_spec((D, D_FF))],
        out_specs=[rows(D), rows(D), rows(D_FF), rows(D_FF), rows(D_FF)],
        out_shape=[jax.ShapeDtypeStruct((t, D), F32), jax.ShapeDtypeStruct((t, D), BF16),
                   jax.ShapeDtypeStruct((t, D_FF), F32), jax.ShapeDtypeStruct((t, D_FF), F32),
                   jax.ShapeDtypeStruct((t, D_FF), BF16)],
        plan=plan,
    )(h0, y_conv, y_gla, w_out, g_ffn, w_gate, w_up)


def _ffn_down_loss(act, w_down, h1, target, g_final, row_mask):
    t = h1.shape[0]
    r = _row_tile(t, 384)

    def body(act_ref, wd_ref, h1_ref, tgt_ref, gf_ref, mask_ref, dh2_ref, loss_ref, dgf_ref):
        @pl.when(pl.program_id(0) == 0)
        def _():
            loss_ref[...] = jnp.zeros_like(loss_ref)
            dgf_ref[...] = jnp.zeros_like(dgf_ref)

        gf = gf_ref[...]

        def part(rows):
            h2 = h1_ref[rows, :] + _dot(act_ref[rows, :], wd_ref[...])
            yield
            rstd = lax.rsqrt(jnp.mean(h2 * h2, axis=-1, keepdims=True) + RMS_EPS)
            nrm = h2 * rstd
            err = (nrm * gf - tgt_ref[rows, :]) * mask_ref[rows, :]
            loss_ref[...] += jnp.sum(err * err) * (0.5 / D)
            dy = err * (1.0 / D)
            dgf_ref[...] += jnp.sum(dy * nrm, axis=0, keepdims=True)
            dn = dy * gf
            dh2_ref[rows, :] = rstd * (dn - nrm * jnp.mean(dn * nrm, axis=-1, keepdims=True))

        _in_lockstep(part(rows) for rows in _row_parts(r))

    rows = lambda w: pl.BlockSpec((r, w), lambda i: (i, 0))
    return pl.pallas_call(
        body, name="ffn_down_loss", grid=(t // r,),
        in_specs=[rows(D_FF), _const_spec((D_FF, D)), rows(D), rows(D), _const_spec((1, D)), rows(1)],
        out_specs=[rows(D), _acc_spec((1, 128)), _acc_spec((1, D))],
        out_shape=[jax.ShapeDtypeStruct((t, D), F32), jax.ShapeDtypeStruct((1, 128), F32),
                   jax.ShapeDtypeStruct((1, D), F32)],
        compiler_params=_params(1),
    )(act, w_down, h1, target, g_final, row_mask)


def _ffn_bwd(dh2, gate, up, h1, w_down_t, w_gate_t, w_up_t, w_out_t, g_ffn):
    t = h1.shape[0]
    r = _row_tile(t, FFN_BWD_TILE)

    def body(dh2_ref, gate_ref, up_ref, h1_ref, wd_ref, wg_ref, wu_ref, wo_ref, g_ref,
             dgate_ref, dup_ref, dh1_ref, dycat_ref, dg_ref):
        @pl.when(pl.program_id(0) == 0)
        def _():
            dg_ref[...] = jnp.zeros_like(dg_ref)

        dh2 = dh2_ref[...]
        dact = _dot(dh2.astype(BF16), wd_ref[...])
        gate = gate_ref[...]
        sg = _sigmoid(gate)
        dgate = (dact * up_ref[...] * (sg * (1.0 + gate * (1.0 - sg)))).astype(BF16)
        dup = (dact * (gate * sg)).astype(BF16)
        dgate_ref[...] = dgate
        dup_ref[...] = dup
        dhn = _dot(dgate, wg_ref[...]) + _dot(dup, wu_ref[...])
        h1 = h1_ref[...]
        rstd = lax.rsqrt(jnp.mean(h1 * h1, axis=-1, keepdims=True) + RMS_EPS)
        nrm = h1 * rstd
        dg_ref[...] += jnp.sum(dhn * nrm, axis=0, keepdims=True)
        dn = dhn * g_ref[...]
        dh1 = dh2 + rstd * (dn - nrm * jnp.mean(dn * nrm, axis=-1, keepdims=True))
        dh1_ref[...] = dh1
        dycat_ref[...] = _dot(dh1.astype(BF16), wo_ref[...])

    rows = lambda w: pl.BlockSpec((r, w), lambda i: (i, 0))
    return pl.pallas_call(
        body, name="ffn_bwd", grid=(t // r,),
        in_specs=[rows(D), rows(D_FF), rows(D_FF), rows(D), _const_spec((D, D_FF)), _const_spec((D_FF, D)),
                  _const_spec((D_FF, D)), _const_spec((D, D)), _const_spec((1, D))],
        out_specs=[rows(D_FF), rows(D_FF), rows(D), rows(D), _acc_spec((1, D))],
        out_shape=[jax.ShapeDtypeStruct((t, D_FF), BF16), jax.ShapeDtypeStruct((t, D_FF), BF16),
                   jax.ShapeDtypeStruct((t, D), F32), jax.ShapeDtypeStruct((t, D), F32),
                   jax.ShapeDtypeStruct((1, D), F32)],
        compiler_params=_params(1),
    )(dh2, gate, up, h1, w_down_t, w_gate_t, w_up_t, w_out_t, g_ffn)


def _conv_bwd(dycat, yc, u, conv_w, ln_g, ln_b, n_ex, lp, plan=None):
    r = CONV_TILE
    nt = lp // r
    hb = r // CONV_SUB
    nsub = r // CONV_SUB

    def ln_bwd(dy, yc_rows, live, lg, lb):
        mu = jnp.mean(yc_rows, axis=-1, keepdims=True)
        cen = yc_rows - mu
        rs = lax.rsqrt(jnp.mean(cen * cen, axis=-1, keepdims=True) + LN_EPS)
        yn = cen * rs
        out = yn * lg + lb
        so = _sigmoid(out)
        dout = jnp.where(live, dy * (so * (1.0 + out * (1.0 - so))), 0.0)
        dyn = dout * lg
        dyc = rs * (dyn - jnp.mean(dyn, axis=-1, keepdims=True) - yn * jnp.mean(dyn * yn, axis=-1, keepdims=True))
        return dyc, dout, yn

    def body(dy_ref, dyn_ref, yc_ref, ycn_ref, cur_ref, prev_ref, w_ref, lg_ref, lb_ref,
             du_ref, dw_ref, db_ref, dlg_ref, dlb_ref, glu, dycs, dwacc, glu_sh, dycs_sh):
        b = pl.program_id(0)
        i = pl.program_id(1)
        first = jnp.logical_and(b == 0, i == 0)

        @pl.when(first)
        def _():
            dwacc[...] = jnp.zeros_like(dwacc)
            db_ref[...] = jnp.zeros_like(db_ref)
            dlg_ref[...] = jnp.zeros_like(dlg_ref)
            dlb_ref[...] = jnp.zeros_like(dlb_ref)

        lg, lb = lg_ref[...], lb_ref[...]
        cur = cur_ref[...]
        sig = _sigmoid(cur[:, C_CONV:])
        glu[CONV_SUB:CONV_SUB + r, :] = cur[:, :C_CONV] * sig
        pv = prev_ref[...]
        glu[0:CONV_SUB, :] = jnp.where(i > 0, pv[:, :C_CONV] * _sigmoid(pv[:, C_CONV:]), 0.0)

        row = i * r + lax.broadcasted_iota(jnp.int32, (r, 1), 0)
        dyc, dout, yn = ln_bwd(dy_ref[...], yc_ref[...], row >= PAD_ROWS, lg, lb)
        dycs[0:r, :] = dyc
        dycn, _, _ = ln_bwd(dyn_ref[...], ycn_ref[...], i < nt - 1, lg, lb)
        dycs[r:r + CONV_SUB, :] = dycn
        db_ref[...] += jnp.sum(dyc, axis=0, keepdims=True)
        dlg_ref[...] += jnp.sum(dout * yn, axis=0, keepdims=True)
        dlb_ref[...] += jnp.sum(dout, axis=0, keepdims=True)

        _shifted_copies(glu, glu_sh, r)
        _shifted_copies(dycs, dycs_sh, r)
        w = w_ref[...]
        for j in range(nsub):
            r0 = j * CONV_SUB
            dblk = dycs[r0:r0 + CONV_SUB, :]
            dglu = jnp.zeros((CONV_SUB, C_CONV), F32)
            for k in range(CONV_W):
                dglu = dglu + w[k:k + 1, :] * _shifted_rows(dycs, dycs_sh, r0 + (CONV_W - 1) - k)
                prod = dblk * _shifted_rows(glu, glu_sh, r0 + CONV_LEAD + k)
                dwacc[k] += prod.reshape(CONV_SUB // SUBLANES, SUBLANES, C_CONV).sum(axis=0)
            sg = sig[r0:r0 + CONV_SUB, :]
            cv = cur[r0:r0 + CONV_SUB, :C_CONV]
            du_ref[r0:r0 + CONV_SUB, :C_CONV] = (dglu * sg).astype(BF16)
            du_ref[r0:r0 + CONV_SUB, C_CONV:] = (dglu * cv * sg * (1.0 - sg)).astype(BF16)

        @pl.when(jnp.logical_and(b == n_ex - 1, i == nt - 1))
        def _():
            dw_ref[...] = jnp.sum(dwacc[...], axis=1)

    t = n_ex * lp
    cur_rows = lambda w, col: pl.BlockSpec((r, w), lambda b, i: (b * nt + i, col))
    nxt_rows = lambda w, col: pl.BlockSpec(
        (CONV_SUB, w), lambda b, i: (jnp.minimum((b * nt + i + 1) * hb, n_ex * nt * hb - 1), col))
    return _call(
        body, name="conv_bwd", grid=(n_ex, nt),
        in_specs=[cur_rows(C_CONV, 0), nxt_rows(C_CONV, 0), cur_rows(C_CONV, 0), nxt_rows(C_CONV, 0),
                  cur_rows(2 * C_CONV, 0),
                  pl.BlockSpec((CONV_SUB, 2 * C_CONV), lambda b, i: (jnp.maximum((b * nt + i) * hb - 1, 0), 0)),
                  _const_spec((32, C_CONV)), _const_spec((1, C_CONV)), _const_spec((1, C_CONV))],
        out_specs=[cur_rows(2 * C_CONV, 0), _acc_spec((32, C_CONV)), _acc_spec((1, C_CONV)),
                   _acc_spec((1, C_CONV)), _acc_spec((1, C_CONV))],
        out_shape=[jax.ShapeDtypeStruct((t, 2 * C_CONV), BF16), jax.ShapeDtypeStruct((32, C_CONV), F32),
                   jax.ShapeDtypeStruct((1, C_CONV), F32), jax.ShapeDtypeStruct((1, C_CONV), F32),
                   jax.ShapeDtypeStruct((1, C_CONV), F32)],
        scratch_shapes=[pltpu.VMEM((r + CONV_SUB, C_CONV), F32), pltpu.VMEM((r + CONV_SUB, C_CONV), F32),
                        pltpu.VMEM((32, SUBLANES, C_CONV), F32),
                        pltpu.VMEM((SUBLANES - 1, r + CONV_SUB - SUBLANES, C_CONV), F32),
                        pltpu.VMEM((SUBLANES - 1, r + CONV_SUB - SUBLANES, C_CONV), F32)],
        plan=plan,
    )(dycat, dycat, yc, yc, u, u, conv_w, ln_g, ln_b)


HEAD_ROWS_ALL = N_HEADS * CHUNK


def _gla_gates(lr, w2, gb, first_chunk):
    z = _dot(lr.astype(BF16), w2) + gb
    a = (jnp.minimum(z, 0.0) - jnp.log(1.0 + jnp.exp(-jnp.abs(z)))) * (1.0 / GATE_TAU)
    row = lax.broadcasted_iota(jnp.int32, (CHUNK, 1), 0)
    live = jnp.logical_or(jnp.logical_not(first_chunk), row >= PAD_ROWS)
    return z, jnp.where(live, a, 0.0), live


def _tri(lower):
    i = lax.broadcasted_iota(jnp.int32, (CHUNK, CHUNK), 0)
    j = lax.broadcasted_iota(jnp.int32, (CHUNK, CHUNK), 1)
    return (i >= j) if lower else (i <= j)


def _head_of(shape, axis, per_head):
    return lax.broadcasted_iota(jnp.int32, shape, axis) // per_head


def _expand(x, lanes_per_head):
    rows, lanes = HEAD_ROWS_ALL, x.shape[1]
    keep = _head_of((rows, lanes), 0, CHUNK) == _head_of((rows, lanes), 1, lanes_per_head)
    return jnp.where(keep, jnp.tile(x, (N_HEADS, 1)), 0.0)


def _expand_lanes(x):
    rows, w = x.shape
    keep = _head_of((rows, N_HEADS * w), 0, CHUNK) == _head_of((rows, N_HEADS * w), 1, w)
    return jnp.where(keep, jnp.tile(x, (1, N_HEADS)), 0.0)


def _expand_state(st):
    rows, lanes = N_HEADS * DV, st.shape[1]
    keep = _head_of((rows, lanes), 0, DV) == _head_of((rows, lanes), 1, DK)
    return jnp.where(keep, jnp.tile(st, (N_HEADS, 1)), 0.0)


def _fold(t, rows_per_head):
    lane_head = _head_of((rows_per_head, t.shape[1]), 1, DK)
    out = jnp.where(lane_head == 0, t[0:rows_per_head], 0.0)
    for h in range(1, N_HEADS):
        out = out + jnp.where(lane_head == h, t[h * rows_per_head:(h + 1) * rows_per_head], 0.0)
    return out


def _rows_by_head(x):
    return jnp.concatenate([x[:, h * DV:(h + 1) * DV] for h in range(N_HEADS)], axis=0)


def _lanes_by_head(x):
    return jnp.concatenate([x[h * CHUNK:(h + 1) * CHUNK] for h in range(N_HEADS)], axis=1)


def _running_sum(a, lower):
    hi = a.astype(BF16)
    rest = a - hi.astype(F32)
    mid = rest.astype(BF16)
    lo = (rest - mid.astype(F32)).astype(BF16)
    w = a.shape[1]
    parts = _dot(_tri(lower).astype(F32).astype(BF16), jnp.concatenate([hi, mid, lo], axis=1))
    return parts[:, :w] + parts[:, w:2 * w] + parts[:, 2 * w:]


def _stacked_causal():
    i = lax.broadcasted_iota(jnp.int32, (HEAD_ROWS_ALL, CHUNK), 0) % CHUNK
    j = lax.broadcasted_iota(jnp.int32, (HEAD_ROWS_ALL, CHUNK), 1)
    return i >= j


GLA_GROUP = 3


def _gla_chunk(q, k, v, lr, w2, gb, first_chunk):
    z, a, live = _gla_gates(lr, w2, gb, first_chunk)
    yield
    b = _running_sum(a, True)
    yield
    bl = b[CHUNK - 1:CHUNK, :]
    e_pos, e_neg, e_dec = jnp.exp(b), jnp.exp(-b), jnp.exp(bl - b)
    q_f, k_f, kd_f = q * (DK ** -0.5) * e_pos, k * e_neg, k * e_dec
    qx = _expand(q_f, DK).astype(BF16)
    k_in, k_dec, v_b = k_f.astype(BF16), kd_f.astype(BF16), v.astype(BF16)
    s = jnp.where(_stacked_causal(), _dot_nt(qx, k_in), 0.0).astype(BF16)
    yield
    p = _dot(s, v_b)
    yield
    o_intra = jnp.concatenate([p[h * CHUNK:(h + 1) * CHUNK, h * DV:(h + 1) * DV] for h in range(N_HEADS)], axis=0)
    return dict(z=z, live=live, bl=bl, e_pos=e_pos, e_neg=e_neg, e_dec=e_dec, q_f=q_f, k_f=k_f, kd_f=kd_f,
                qx=qx, k_in=k_in, k_dec=k_dec, v_b=v_b, s=s, o_intra=o_intra, decay=jnp.exp(bl))


def _gla_fwd(u, w2, gb, ng, n_ex, lp, plan=None):
    nc = lp // CHUNK
    t = n_ex * lp
    rows_of = lambda j: pl.ds(j * CHUNK, CHUNK)

    def body(qk_ref, v_ref, g_ref, lr_ref, w2_ref, gb_ref, ng_ref, y_ref, st_ref, state):
        n = pl.program_id(0)

        @pl.when(n == 0)
        def _():
            state[...] = jnp.zeros_like(state)

        carried = [state[e] for e in range(n_ex)]

        def one_chunk(e, j):
            rows = rows_of(j)
            qk = qk_ref[e, rows, :]
            first = jnp.logical_and(n == 0, j == 0)
            c = yield from _gla_chunk(qk[:, :GLA_K], qk[:, GLA_K:], v_ref[e, rows, :], lr_ref[e, rows, :],
                                      w2_ref[...], gb_ref[...], first)
            kv = _fold(_dot_tn(c["v_b"], c["k_dec"]), DV)
            g = _rows_by_head(g_ref[e, rows, :])
            gate = ng_ref[...] * (g * _sigmoid(g))
            yield
            for _ in range(j):
                yield
            st = carried[e]
            st_ref[e, pl.ds(j * DV, DV), :] = st
            o = c["o_intra"] + _dot_nt(c["qx"], st.astype(BF16))
            rstd = lax.rsqrt(jnp.mean(o * o, axis=-1, keepdims=True) + RMS_EPS)
            y_ref[e, rows, :] = _lanes_by_head(o * rstd * gate).astype(BF16)
            carried[e] = c["decay"] * st + kv

        _in_lockstep(one_chunk(e, j) for j in range(GLA_GROUP) for e in range(n_ex))
        for e in range(n_ex):
            state[e] = carried[e]

    u3 = u.reshape(n_ex, lp, D_IN_PAD)
    blk = lambda w, col: pl.BlockSpec((n_ex, GLA_GROUP * CHUNK, w), lambda n: (0, n, col))
    (y, states), extra = _call(
        body, name="gla_fwd", grid=(nc // GLA_GROUP,),
        in_specs=[blk(2 * GLA_K, 2), blk(GLA_V, 3), blk(GLA_V, 4), blk(128, 20),
                  _const_spec((128, GLA_K)), _const_spec((1, GLA_K)), _const_spec((1, DV))],
        out_specs=[blk(GLA_V, 0), pl.BlockSpec((n_ex, GLA_GROUP * DV, GLA_K), lambda n: (0, n, 0))],
        out_shape=[jax.ShapeDtypeStruct((n_ex, lp, GLA_V), BF16),
                   jax.ShapeDtypeStruct((n_ex, nc * DV, GLA_K), F32)],
        scratch_shapes=[pltpu.VMEM((n_ex, DV, GLA_K), F32)],
        plan=plan,
    )(u3, u3, u3, u3, w2, gb, ng)
    return (y.reshape(t, GLA_V), states), extra


def _gla_bwd(dycat, u, states, w2, gb, ng, n_ex, lp, plan=None):
    nc = lp // CHUNK
    t = n_ex * lp

    def body(dy_ref, qk_ref, v_ref, g_ref, lr_ref, st_ref, w2_ref, gb_ref, ng_ref,
             du_ref, dw2_ref, dgb_ref, dng_ref, dstate):
        n = pl.program_id(0)
        group = nc // GLA_GROUP - 1 - n

        @pl.when(n == 0)
        def _():
            dw2_ref[...] = jnp.zeros_like(dw2_ref)
            dgb_ref[...] = jnp.zeros_like(dgb_ref)
            dng_ref[...] = jnp.zeros_like(dng_ref)
            dstate[...] = jnp.zeros_like(dstate)

        carried = [dstate[e] for e in range(n_ex)]

        def one_chunk(e, order):
            j = GLA_GROUP - 1 - order
            rows = pl.ds(j * CHUNK, CHUNK)
            qk = qk_ref[e, rows, :]
            lr = lr_ref[e, rows, :]
            st = st_ref[e, pl.ds(j * DV, DV), :]
            first = jnp.logical_and(group == 0, j == 0)
            c = yield from _gla_chunk(qk[:, :GLA_K], qk[:, GLA_K:], v_ref[e, rows, :], lr, w2_ref[...], gb_ref[...],
                                      first)
            qx, k_in, k_dec, v_b, s = c["qx"], c["k_in"], c["k_dec"], c["v_b"], c["s"]
            st_b = st.astype(BF16)
            o = c["o_intra"] + _dot_nt(qx, st_b)
            ngv = ng_ref[...]
            yield
            rstd = lax.rsqrt(jnp.mean(o * o, axis=-1, keepdims=True) + RMS_EPS)
            nrm = o * rstd
            g = _rows_by_head(g_ref[e, rows, :])
            dy = _rows_by_head(dy_ref[e, rows, :])
            sg = _sigmoid(g)
            dg = dy * nrm * ngv * (sg * (1.0 + g * (1.0 - sg)))
            dt = dy * (g * sg)
            dng_ref[...] += jnp.sum(dt * nrm, axis=0, keepdims=True)
            dn = dt * ngv
            do = rstd * (dn - nrm * jnp.mean(dn * nrm, axis=-1, keepdims=True))
            do_b = do.astype(BF16)
            dox = _expand_lanes(do).astype(BF16)
            yield
            da = jnp.where(_stacked_causal(), _dot_nt(dox, v_b), 0.0).astype(BF16)
            dv_intra = _dot_tn(s, dox)
            dst_own = _dot_tn(do_b, qx)
            yield
            dq_in = _fold(_dot(da, k_in) + _dot(do_b, st_b), CHUNK)
            dk_in = _dot_tn(da, qx)
            dq = dq_in * (DK ** -0.5) * c["e_pos"]
            yield
            for _ in range(order):
                yield
            dst = carried[e]
            dstx = _expand_state(dst).astype(BF16)
            dv = dv_intra + _dot_nt(k_dec, dstx)
            dk_dec = _dot(v_b, dstx)
            carried[e] = dst_own + c["decay"] * dst
            yield
            dbl = (jnp.sum(dk_dec * c["kd_f"], axis=0, keepdims=True)
                   + c["decay"] * jnp.sum(dst * st, axis=0, keepdims=True))
            dk = dk_in * c["e_neg"] + dk_dec * c["e_dec"]
            db = dq_in * c["q_f"] - dk_in * c["k_f"] - dk_dec * c["kd_f"]
            row = lax.broadcasted_iota(jnp.int32, (CHUNK, 1), 0)
            da_log = _running_sum(db + jnp.where(row == CHUNK - 1, dbl, 0.0), False)
            yield
            dz = jnp.where(c["live"], da_log * (1.0 - _sigmoid(c["z"])) * (1.0 / GATE_TAU), 0.0)
            dz_b = dz.astype(BF16)
            out = du_ref.at[e, rows, :]
            out[:, 0:GLA_K] = dq.astype(BF16)
            out[:, GLA_K:2 * GLA_K] = dk.astype(BF16)
            out[:, 2 * GLA_K:2 * GLA_K + GLA_V] = dv.astype(BF16)
            out[:, 2 * GLA_K + GLA_V:2 * GLA_K + 2 * GLA_V] = _lanes_by_head(dg).astype(BF16)
            out[:, 2 * GLA_K + 2 * GLA_V:] = _dot_nt(dz_b, w2_ref[...]).astype(BF16)
            dw2_ref[...] += _dot_tn(lr.astype(BF16), dz_b)
            dgb_ref[...] += jnp.sum(dz, axis=0, keepdims=True)

        _in_lockstep(one_chunk(e, order) for order in range(GLA_GROUP) for e in range(n_ex))
        for e in range(n_ex):
            dstate[e] = carried[e]

    u3 = u.reshape(n_ex, lp, D_IN_PAD)
    rev = lambda w, col: pl.BlockSpec((n_ex, GLA_GROUP * CHUNK, w), lambda n: (0, nc // GLA_GROUP - 1 - n, col))
    (du, d_w2, d_gb, d_ng), extra = _call(
        body, name="gla_bwd", grid=(nc // GLA_GROUP,),
        in_specs=[rev(GLA_V, 1), rev(2 * GLA_K, 2), rev(GLA_V, 3), rev(GLA_V, 4), rev(128, 20),
                  pl.BlockSpec((n_ex, GLA_GROUP * DV, GLA_K), lambda n: (0, nc // GLA_GROUP - 1 - n, 0)),
                  _const_spec((128, GLA_K)), _const_spec((1, GLA_K)), _const_spec((1, DV))],
        out_specs=[rev(D_GLA_IN, 0), _acc_spec((128, GLA_K)), _acc_spec((1, GLA_K)), _acc_spec((1, DV))],
        out_shape=[jax.ShapeDtypeStruct((n_ex, lp, D_GLA_IN), BF16), jax.ShapeDtypeStruct((128, GLA_K), F32),
                   jax.ShapeDtypeStruct((1, GLA_K), F32), jax.ShapeDtypeStruct((1, DV), F32)],
        scratch_shapes=[pltpu.VMEM((n_ex, DV, GLA_K), F32)],
        plan=plan,
    )(dycat.reshape(n_ex, lp, D), u3, u3, u3, u3, states, w2, gb, ng)
    return (du.reshape(t, D_GLA_IN), d_w2, d_gb, d_ng), extra


def _in_proj_bwd(du_conv, du_gla, w_in_t_conv, w_in_t_gla, h0, dh1, g_mix, plan=None):
    t = h0.shape[0]
    r = _row_tile(t, 384)

    def body(dc_ref, dg_ref, wc_ref, wg_ref, h_ref, dh1_ref, g_ref, dh0_ref, dgm_ref):
        @pl.when(pl.program_id(0) == 0)
        def _():
            dgm_ref[...] = jnp.zeros_like(dgm_ref)

        dhn = _dot(dc_ref[...], wc_ref[...]) + _dot(dg_ref[...], wg_ref[...])
        h = h_ref[...]
        rstd = lax.rsqrt(jnp.mean(h * h, axis=-1, keepdims=True) + RMS_EPS)
        nrm = h * rstd
        dgm_ref[...] += jnp.sum(dhn * nrm, axis=0, keepdims=True)
        dn = dhn * g_ref[...]
        dh0_ref[...] = dh1_ref[...] + rstd * (dn - nrm * jnp.mean(dn * nrm, axis=-1, keepdims=True))

    rows = lambda w: pl.BlockSpec((r, w), lambda i: (i, 0))
    return _call(
        body, name="in_proj_bwd", grid=(t // r,),
        in_specs=[rows(2 * C_CONV), rows(D_GLA_IN), _const_spec((2 * C_CONV, D)), _const_spec((D_GLA_IN, D)),
                  rows(D), rows(D), _const_spec((1, D))],
        out_specs=[rows(D), _acc_spec((1, D))],
        out_shape=[jax.ShapeDtypeStruct((t, D), F32), jax.ShapeDtypeStruct((1, D), F32)],
        plan=plan,
    )(du_conv, du_gla, w_in_t_conv, w_in_t_gla, h0, dh1, g_mix)


def _wgrad(x, dy, name, plan=None):
    t, m = x.shape
    n = dy.shape[1]
    tk = t // 3 if t % (3 * 128) == 0 else _row_tile(t, 384)
    tm = m if m <= D_GLA_IN else m // 2

    def body(x_ref, dy_ref, o_ref):
        @pl.when(pl.program_id(1) == 0)
        def _():
            o_ref[...] = jnp.zeros_like(o_ref)

        o_ref[...] += _dot_tn(x_ref[...].astype(BF16), dy_ref[...].astype(BF16))

    (out,), extra = _call(
        body, name=name, grid=(m // tm, t // tk),
        in_specs=[pl.BlockSpec((tk, tm), lambda i, k: (k, i)), pl.BlockSpec((tk, n), lambda i, k: (k, 0))],
        out_specs=[pl.BlockSpec((tm, n), lambda i, k: (i, 0))],
        out_shape=[jax.ShapeDtypeStruct((m, n), F32)],
        plan=plan,
    )(x, dy)
    return out, extra


def _adam_update(g, w, m, v):
    m2 = ADAM_B1 * m + (1.0 - ADAM_B1) * g
    v2 = ADAM_B2 * v + (1.0 - ADAM_B2) * (g * g)
    m_hat = m2 / (1.0 - ADAM_B1 ** ADAM_STEP)
    v_hat = v2 / (1.0 - ADAM_B2 ** ADAM_STEP)
    delta = -ADAM_LR * (m_hat / (jnp.sqrt(v_hat) + ADAM_EPS) + ADAM_WD * w)
    return delta, m2, v2


ADAMW_STEPS = 4


def _adamw(g, w, m, v, name):
    rows, cols = g.shape
    steps = ADAMW_STEPS if rows % (ADAMW_STEPS * SUBLANES) == 0 else 1

    def body(g_ref, w_ref, m_ref, v_ref, d_ref, m2_ref, v2_ref):
        d_ref[...], m2_ref[...], v2_ref[...] = _adam_update(g_ref[...], w_ref[...], m_ref[...], v_ref[...])

    spec = pl.BlockSpec((rows // steps, cols), lambda i: (i, 0))
    return pl.pallas_call(
        body, name=name, grid=(steps,), in_specs=[spec] * 4, out_specs=[spec] * 3,
        out_shape=[jax.ShapeDtypeStruct(g.shape, F32)] * 3, compiler_params=_params(1),
    )(g, w, m, v)


def _adamw_halves(items, c, name):
    n = len(items)
    h = items[0][0].shape[1]
    steps = ADAMW_STEPS if all(it[0].shape[0] % (ADAMW_STEPS * SUBLANES) == 0 for it in items) else 1

    def body(c_ref, *refs):
        ins, outs = refs[:5 * n], refs[5 * n:]
        own = pl.program_id(1) == c_ref[0]
        for i in range(n):
            a_ref, b_ref, w_ref, m_ref, v_ref = ins[5 * i:5 * i + 5]
            go_ref, d_ref, m2_ref, v2_ref = outs[4 * i:4 * i + 4]
            g = jnp.where(own, a_ref[...], b_ref[...])
            go_ref[...] = g
            d_ref[...], m2_ref[...], v2_ref[...] = _adam_update(g, w_ref[...], m_ref[...], v_ref[...])

    in_specs, out_specs, out_shape, args = [pl.BlockSpec(memory_space=pltpu.SMEM)], [], [], []
    for mine, theirs, w, m, v in items:
        tr = mine.shape[0] // steps
        half = pl.BlockSpec((tr, h), lambda i, j: (i, 0))
        full = pl.BlockSpec((tr, h), lambda i, j: (i, j))
        in_specs += [half, half, full, full, full]
        out_specs += [full] * 4
        out_shape += [jax.ShapeDtypeStruct(w.shape, F32)] * 4
        args += [mine, theirs, w, m, v]
    res = pl.pallas_call(
        body, name=name, grid=(steps, 2), in_specs=in_specs, out_specs=out_specs, out_shape=out_shape,
        compiler_params=_params(2),
    )(jnp.reshape(c, (1,)).astype(jnp.int32), *args)
    return [res[4 * i:4 * i + 4] for i in range(n)]


def _rs_add_halves(g, recv, c, name):
    _, rows, w = g.shape
    h = w // 2
    tr = rows // 2 if rows % 16 == 0 and rows > 64 else rows

    def body(c_ref, a_ref, b_ref, o_ref):
        o_ref[...] = (a_ref[...] + b_ref[...]).astype(BF16)

    return pl.pallas_call(
        body, name=name,
        grid_spec=pltpu.PrefetchScalarGridSpec(
            num_scalar_prefetch=1, grid=(N_CHIPS, rows // tr),
            in_specs=[pl.BlockSpec((1, tr, h), lambda j, i, s: (j, i, s[0])),
                      pl.BlockSpec((1, tr, h), lambda j, i, s: (j, i, 0))],
            out_specs=pl.BlockSpec((1, tr, h), lambda j, i, s: (j, i, 0))),
        out_shape=jax.ShapeDtypeStruct((N_CHIPS, rows, h), BF16),
        compiler_params=_params(2),
    )(jnp.reshape(c, (1,)).astype(jnp.int32), g, recv)


def _rs_sum(own, others, mine, name):
    _, rows, h = own.shape
    tr = rows // 2 if rows % 16 == 0 and rows > 64 else rows

    def body(mine_ref, own_ref, oth_ref, o_ref):
        p = oth_ref[...].astype(F32)
        o_ref[...] = ((own_ref[0].astype(F32) + p[0]) + p[1]) + p[2]

    return pl.pallas_call(
        body, name=name,
        grid_spec=pltpu.PrefetchScalarGridSpec(
            num_scalar_prefetch=1, grid=(rows // tr,),
            in_specs=[pl.BlockSpec((1, tr, h), lambda i, s: (s[0], i, 0)),
                      pl.BlockSpec((3, tr, h), lambda i, s: (0, i, 0))],
            out_specs=pl.BlockSpec((tr, h), lambda i, s: (i, 0))),
        out_shape=jax.ShapeDtypeStruct((rows, h), F32),
        compiler_params=_params(1),
    )(jnp.reshape(mine, (1,)).astype(jnp.int32), own, others)


def _sum_slots_adamw(slots, late_slots, w, m, v):
    late_rows = late_slots.shape[1]

    def body(s_ref, l_ref, w_ref, m_ref, v_ref, g_ref, d_ref, m2_ref, v2_ref):
        g, late = s_ref[0], l_ref[0]
        for d in range(1, 8):
            g = g + s_ref[d]
            late = late + l_ref[d]
        g = jnp.concatenate([g[:late_rows] + late, g[late_rows:]], axis=0)
        g_ref[...] = g
        d_ref[...], m2_ref[...], v2_ref[...] = _adam_update(g, w_ref[...], m_ref[...], v_ref[...])

    vm = pl.BlockSpec(memory_space=pltpu.VMEM)
    shape = jax.ShapeDtypeStruct(w.shape, F32)
    return pl.pallas_call(body, name="small_sum_adamw", in_specs=[vm] * 5, out_specs=[vm] * 4,
                          out_shape=[shape] * 4)(slots, late_slots, w, m, v)


def _mesh_pos():
    return lax.axis_index("x"), lax.axis_index("y"), lax.axis_index("c")


def _other_chips(x, y):
    return [(1 - x, y), (x, 1 - y), (1 - x, 1 - y)]


def _half(ref, c, axis):
    n = ref.shape[axis] // 2
    return ref.at[(slice(None),) * axis + (pl.ds(c * n, n),)]


def _remote(src, dst, send_sem, recv_sem, device):
    return pltpu.make_async_remote_copy(src_ref=src, dst_ref=dst, send_sem=send_sem, recv_sem=recv_sem,
                                        device_id=device, device_id_type=MESH)


def _gather_plan(split, whole=(), axes=None):
    split, whole = list(split), list(whole)
    ns, n = len(split), len(split) + len(whole)

    def make(ins, outs, sems):
        ici_send, ici_recv, d2d_send, d2d_recv, own_send, own_recv = sems
        x, y, c = _mesh_pos()
        mine = 2 * x + y
        chips = _other_chips(x, y)
        blocks = [2 * px + py for px, py in chips]

        def own(a):
            return _remote(ins[a], outs[a].at[mine], own_send.at[a], own_recv.at[a], (x, y, 1 - c))

        def ici(a, k, block):
            px, py = chips[k]
            src, dst = ins[a], outs[a].at[block]
            if a < ns:
                src, dst = _half(src, c, axes[a]), _half(dst, c, axes[a])
            return _remote(src, dst, ici_send.at[3 * a + k], ici_recv.at[3 * a + k], (px, py, c))

        def d2d(a, k, half):
            part = _half(outs[a].at[blocks[k]], half, axes[a])
            return _remote(part, part, d2d_send.at[3 * a + k], d2d_recv.at[3 * a + k], (x, y, 1 - c))

        def start():
            for a in range(n):
                for k in range(3):
                    ici(a, k, mine).start()
                own(a).start()

        def relay():
            for a in range(n):
                for k in range(3):
                    ici(a, k, blocks[k]).wait_recv()
                    if a < ns:
                        d2d(a, k, c).start()

        def finish():
            for a in range(ns):
                for k in range(3):
                    d2d(a, k, 1 - c).wait_recv()
            for a in range(n):
                for k in range(3):
                    ici(a, k, mine).wait_send()
                    if a < ns:
                        d2d(a, k, c).wait_send()
                own(a).wait()

        return start, relay, finish

    arrays = split + whole
    axes = [0] * ns if axes is None else list(axes)
    return _Plan(arrays, [jax.ShapeDtypeStruct((N_CHIPS,) + s.shape, s.dtype) for s in arrays],
                 [pltpu.SemaphoreType.DMA((3 * n,)), pltpu.SemaphoreType.DMA((3 * n,)),
                  pltpu.SemaphoreType.DMA((3 * ns,)), pltpu.SemaphoreType.DMA((3 * ns,)),
                  pltpu.SemaphoreType.DMA((n,)), pltpu.SemaphoreType.DMA((n,))], make)


def _to_sibling_plan(gs):
    n = len(gs)

    def make(ins, outs, sems):
        send_sems, recv_sems = sems
        x, y, c = _mesh_pos()

        def copy(a):
            return _remote(_half(ins[a], 1 - c, 2), outs[a], send_sems.at[a], recv_sems.at[a], (x, y, 1 - c))

        def start():
            for a in range(n):
                copy(a).start()

        def finish():
            for a in range(n):
                copy(a).wait()

        return start, finish

    return _Plan(list(gs), [jax.ShapeDtypeStruct(g.shape[:2] + (g.shape[2] // 2,), g.dtype) for g in gs],
                 [pltpu.SemaphoreType.DMA((n,)), pltpu.SemaphoreType.DMA((n,))], make)


def _chip_exchange_plan(ps):
    n = len(ps)

    def make(ins, outs, sems):
        send_sems, recv_sems = sems
        x, y, c = _mesh_pos()
        chips = _other_chips(x, y)

        def ici(a, k):
            px, py = chips[k]
            return _remote(ins[a].at[2 * px + py], outs[a].at[k], send_sems.at[3 * a + k],
                           recv_sems.at[3 * a + k], (px, py, c))

        def start():
            for a in range(n):
                for k in range(3):
                    ici(a, k).start()

        def finish():
            for a in range(n):
                for k in range(3):
                    ici(a, k).wait()

        return start, finish

    return _Plan(list(ps), [jax.ShapeDtypeStruct((3,) + p.shape[1:], p.dtype) for p in ps],
                 [pltpu.SemaphoreType.DMA((3 * n,)), pltpu.SemaphoreType.DMA((3 * n,))], make)


def _share_plan(halves):
    n = len(halves)

    def make(ins, outs, sems):
        send_sems, recv_sems = sems
        x, y, c = _mesh_pos()

        def d2d(a):
            return _remote(ins[a], outs[a], send_sems.at[a], recv_sems.at[a], (x, y, 1 - c))

        def start():
            for a in range(n):
                d2d(a).start()

        def finish():
            for a in range(n):
                d2d(a).wait()

        return start, finish

    return _Plan(list(halves), [jax.ShapeDtypeStruct(p.shape, p.dtype) for p in halves],
                 [pltpu.SemaphoreType.DMA((n,)), pltpu.SemaphoreType.DMA((n,))], make)


def _all_to_all_plan(part):
    def make(ins, outs, sems):
        send_sems, recv_sems, local_sem = sems
        (p_ref,), (slots,) = ins, outs
        x, y, c = _mesh_pos()
        me = 4 * x + 2 * y + c
        peers = [(px, py, pc) for px in (x, 1 - x) for py in (y, 1 - y) for pc in (c, 1 - c)][1:]

        def remote(k, slot):
            return _remote(p_ref, slots.at[slot], send_sems.at[k], recv_sems.at[k], peers[k])

        def local():
            return pltpu.make_async_copy(p_ref, slots.at[me], local_sem)

        def start():
            for k in range(7):
                remote(k, me).start()
            local().start()

        def finish():
            for k, (px, py, pc) in enumerate(peers):
                remote(k, 4 * px + 2 * py + pc).wait_recv()
            for k in range(7):
                remote(k, me).wait_send()
            local().wait()

        return start, finish

    return _Plan([part], [jax.ShapeDtypeStruct((8,) + part.shape, part.dtype)],
                 [pltpu.SemaphoreType.DMA((7,)), pltpu.SemaphoreType.DMA((7,)), pltpu.SemaphoreType.DMA(())], make)


def _merge_plans(a, b):
    na_in, na_out, na_sems = len(a.arrays), len(a.out_shape), len(a.sems)

    def make(ins, outs, sems):
        phases_a = _phases(a.make(ins[:na_in], outs[:na_out], sems[:na_sems]))
        phases_b = _phases(b.make(ins[na_in:], outs[na_out:], sems[na_sems:]))

        def both(i):
            def run():
                phases_a[i]()
                phases_b[i]()
            return run

        return both(0), both(1), both(2)

    return _Plan(list(a.arrays) + list(b.arrays), list(a.out_shape) + list(b.out_shape),
                 list(a.sems) + list(b.sems), make)


def _exchange(plan, name):
    n_in, n_out = len(plan.arrays), len(plan.out_shape)

    def body(*refs):
        for phase in _phases(plan.make(refs[:n_in], refs[n_in:n_in + n_out], refs[n_in + n_out:])):
            phase()

    return pl.pallas_call(
        body, name=name, in_specs=[HBM_SPEC] * n_in, out_specs=[HBM_SPEC] * n_out, out_shape=list(plan.out_shape),
        scratch_shapes=list(plan.sems), compiler_params=pltpu.CompilerParams(has_side_effects=True),
    )(*plan.arrays)


def _pack_small(parts):
    rows = []
    for r in range(SMALL_ROWS):
        pieces, col = [], 0
        for name, row, start, size in SMALL_PARTS:
            if row == r:
                assert start == col
                pieces.append(parts[name].reshape(1, size).astype(F32))
                col += size
        rows.append(jnp.concatenate(pieces + [jnp.zeros((1, D - col), F32)], axis=1))
    return jnp.concatenate(rows, axis=0)


def _unpack_small(slab, shapes):
    return {name: slab[row, col:col + size].reshape(shapes[name]) for name, row, col, size in SMALL_PARTS}


def _columns(gathered):
    return jnp.concatenate([gathered[j] for j in range(N_CHIPS)], axis=1)


def kernel(x, meta_tokens, norm_mix_g, w_in, conv_w, conv_b, conv_ln_g, conv_ln_b, gla_w_gate2, gla_gate_b, gla_norm_g, w_out, norm_ffn_g, w_ffn_gate, w_ffn_up, w_ffn_down, norm_final_g, loss_target, m_meta_tokens, m_norm_mix_g, m_w_in, m_conv_w, m_conv_b, m_conv_ln_g, m_conv_ln_b, m_gla_w_gate2, m_gla_gate_b, m_gla_norm_g, m_w_out, m_norm_ffn_g, m_w_ffn_gate, m_w_ffn_up, m_w_ffn_down, m_norm_final_g, v_meta_tokens, v_norm_mix_g, v_w_in, v_conv_w, v_conv_b, v_conv_ln_g, v_conv_ln_b, v_gla_w_gate2, v_gla_gate_b, v_gla_norm_g, v_w_out, v_norm_ffn_g, v_w_ffn_gate, v_w_ffn_up, v_w_ffn_down, v_norm_final_g):
    ws = dict(zip(WEIGHT_NAMES, (meta_tokens, norm_mix_g, w_in, conv_w, conv_b, conv_ln_g, conv_ln_b, gla_w_gate2,
                                 gla_gate_b, gla_norm_g, w_out, norm_ffn_g, w_ffn_gate, w_ffn_up, w_ffn_down,
                                 norm_final_g)))
    ms = dict(zip(WEIGHT_NAMES, (m_meta_tokens, m_norm_mix_g, m_w_in, m_conv_w, m_conv_b, m_conv_ln_g, m_conv_ln_b,
                                 m_gla_w_gate2, m_gla_gate_b, m_gla_norm_g, m_w_out, m_norm_ffn_g, m_w_ffn_gate,
                                 m_w_ffn_up, m_w_ffn_down, m_norm_final_g)))
    vs = dict(zip(WEIGHT_NAMES, (v_meta_tokens, v_norm_mix_g, v_w_in, v_conv_w, v_conv_b, v_conv_ln_g, v_conv_ln_b,
                                 v_gla_w_gate2, v_gla_gate_b, v_gla_norm_g, v_w_out, v_norm_ffn_g, v_w_ffn_gate,
                                 v_w_ffn_up, v_w_ffn_down, v_norm_final_g)))
    c = lax.axis_index("c")
    mine = 2 * lax.axis_index("x") + lax.axis_index("y")
    shard = lambda d, name: d[name].reshape(d[name].shape[-2:])
    vec = {name: ws[name].reshape(1, -1) for name, _, _, _ in SMALL_PARTS}
    n_ex, seq, _ = x.shape
    lp = HEAD_ROWS + seq
    t = n_ex * lp

    (tgt, h0), (w_in_g, meta_g, conv_w_g, w2_g) = _pad_head_rows([loss_target, x], plan=_gather_plan(
        [shard(ws, "w_in").T.astype(BF16)],
        [shard(ws, "meta_tokens"), shard(ws, "conv_w"), shard(ws, "gla_w_gate2")], axes=[1]))
    w_in_t = jnp.concatenate([w_in_g.reshape(D_IN, D), jnp.zeros((D_IN_PAD - D_IN, D), BF16)], axis=0)
    conv_w_full = jnp.concatenate([_columns(conv_w_g), jnp.zeros((32 - CONV_W, C_CONV), F32)], axis=0)
    w2_full = jnp.concatenate([_columns(w2_g), jnp.zeros((128 - RANK, GLA_K), F32)], axis=0).astype(BF16)
    h0 = _set_meta_rows(h0, _columns(meta_g)).reshape(t, D)
    tgt = tgt.reshape(t, D)
    row_mask = jnp.concatenate([jnp.zeros((n_ex, HEAD_ROWS, 1), F32), jnp.ones((n_ex, seq, 1), F32)],
                               axis=1).reshape(t, 1)

    (u, hn), (gate_g,) = _in_proj(h0, vec["norm_mix_g"], w_in_t.T,
                                  plan=_gather_plan([shard(ws, "w_ffn_gate").T.astype(BF16)]))
    (yc, y_conv), (up_g, w_out_g) = _conv_fwd(
        u, conv_w_full, vec["conv_b"], vec["conv_ln_g"], vec["conv_ln_b"], n_ex, lp,
        plan=_gather_plan([shard(ws, "w_ffn_up").T.astype(BF16), shard(ws, "w_out").astype(BF16)]))
    (y_gla, states), _ = _gla_fwd(u, w2_full, vec["gla_gate_b"], vec["gla_norm_g"], n_ex, lp)
    w_out_full = w_out_g.reshape(D, D)
    w_gate_t, w_up_t = gate_g.reshape(D_FF, D), up_g.reshape(D_FF, D)
    (h1, hn2, gate, up, act), (down_g,) = _mix_out_ffn_up(
        h0, y_conv, y_gla, w_out_full, vec["norm_ffn_g"], w_gate_t.T, w_up_t.T,
        plan=_gather_plan([shard(ws, "w_ffn_down").astype(BF16)]))
    w_down_full = down_g.reshape(D_FF, D)
    dh2, loss, d_final_g = _ffn_down_loss(act, w_down_full, h1, tgt, vec["norm_final_g"], row_mask)
    dgate, dup, dh1, dycat, d_ffn_g = _ffn_bwd(dh2, gate, up, h1, w_down_full.T, w_gate_t, w_up_t, w_out_full.T,
                                                vec["norm_ffn_g"])

    early = ("w_ffn_gate", "w_ffn_up", "w_ffn_down", "w_out")
    ffn_block = lambda g: g.reshape(N_CHIPS, D_FF // N_CHIPS, D)
    g_gate = ffn_block(_wgrad(dgate, hn2, "wgrad_gate")[0])
    g_up, (gate_sib,) = _wgrad(dup, hn2, "wgrad_up", _to_sibling_plan([g_gate]))
    g_up = ffn_block(g_up)
    g_down, (up_sib,) = _wgrad(act, dh2, "wgrad_down", _to_sibling_plan([g_up]))
    g_down = ffn_block(g_down)
    g_out = jnp.concatenate([_wgrad(y_conv, dh1, "wgrad_out_conv")[0], _wgrad(y_gla, dh1, "wgrad_out_gla")[0]],
                            axis=0).reshape(N_CHIPS, D // N_CHIPS, D)
    cs_gate = _rs_add_halves(g_gate, gate_sib, c, "rs_add_w_ffn_gate")
    cs_up = _rs_add_halves(g_up, up_sib, c, "rs_add_w_ffn_up")
    (du_conv, d_conv_w, d_conv_b, d_ln_g, d_ln_b), (ex_gate, ex_up, down_sib, out_sib) = _conv_bwd(
        dycat, yc, u, conv_w_full, vec["conv_ln_g"], vec["conv_ln_b"], n_ex, lp,
        plan=_merge_plans(_chip_exchange_plan([cs_gate, cs_up]), _to_sibling_plan([g_down, g_out])))
    cs_down = _rs_add_halves(g_down, down_sib, c, "rs_add_w_ffn_down")
    cs_out = _rs_add_halves(g_out, out_sib, c, "rs_add_w_out")
    (du_gla, d_w2, d_gate_b, d_norm_g), (ex_down, ex_out) = _gla_bwd(
        dycat, u, states, w2_full, vec["gla_gate_b"], vec["gla_norm_g"], n_ex, lp,
        plan=_chip_exchange_plan([cs_down, cs_out]))
    halves = [_rs_sum(own, oth, mine, "rs_sum_" + nm)
              for own, oth, nm in zip((cs_gate, cs_up, cs_down, cs_out), (ex_gate, ex_up, ex_down, ex_out), early)]

    small = {"norm_mix_g": jnp.zeros((1, D), F32), "norm_ffn_g": d_ffn_g, "norm_final_g": d_final_g,
             "conv_b": d_conv_b, "conv_ln_g": d_ln_g, "conv_ln_b": d_ln_b, "gla_gate_b": d_gate_b,
             "gla_norm_g": d_norm_g}
    part = lax.dynamic_update_slice(_pack_small(small), loss[:, :1], (LOSS_ROW, 0))
    part = jnp.concatenate([part, jnp.zeros((N_META, D), F32), d_conv_w.reshape(16, D), d_w2[:RANK].reshape(4, D),
                            jnp.zeros((4, D), F32)], axis=0)
    g_in_gla, (slots,) = _wgrad(du_gla, hn, "wgrad_in_gla", _all_to_all_plan(part))

    d_w_in_t = jnp.concatenate([_wgrad(du_conv, hn, "wgrad_in_conv")[0], g_in_gla],
                               axis=0)[:D_IN].reshape(N_CHIPS, D_IN // N_CHIPS, D)
    (in_from_sibling,) = _exchange(_to_sibling_plan([d_w_in_t]), "rs_late_to_sibling")
    in_chip_sum = _rs_add_halves(d_w_in_t, in_from_sibling, c, "rs_add_w_in")
    (dh0, d_mix_g), shared = _in_proj_bwd(
        du_conv, du_gla, w_in_t[:2 * C_CONV], w_in_t[2 * C_CONV:], h0, dh1, vec["norm_mix_g"],
        plan=_merge_plans(_share_plan(halves), _chip_exchange_plan([in_chip_sum])))
    dh0 = dh0.reshape(n_ex, lp, D)
    grad_x = dh0[:, HEAD_ROWS:]
    late_part = jnp.concatenate([d_mix_g, jnp.zeros((SMALL_ROWS - 1, D), F32),
                                 jnp.sum(dh0[:, PAD_ROWS:HEAD_ROWS], axis=0)], axis=0)
    in_half = _rs_sum(in_chip_sum, shared[4], mine, "rs_sum_w_in")
    in_shared, late_slots = _exchange(_merge_plans(_share_plan([in_half]), _all_to_all_plan(late_part)),
                                      "late_exchange")

    out = {"grad": {}, "delta": {}, "new_m": {}, "new_v": {}}

    def record(name, res, transposed=False):
        for kind, a in zip(("grad", "delta", "new_m", "new_v"), res):
            out[kind][name] = (a.T if transposed else a).reshape(ws[name].shape)

    def operands(name, transposed):
        lay = (lambda a: a.T) if transposed else (lambda a: a)
        return lay(shard(ws, name)), lay(shard(ms, name)), lay(shard(vs, name))

    early_layout = (("w_ffn_gate", True), ("w_ffn_up", True), ("w_ffn_down", False), ("w_out", False))
    items = [(mine_half, their_half, *operands(name, transposed))
             for (name, transposed), mine_half, their_half in zip(early_layout, halves, shared)]
    for (name, transposed), res in zip(early_layout, _adamw_halves(items, c, "adamw_early")):
        record(name, res, transposed)

    record("w_in", _adamw_halves([(in_half, in_shared, *operands("w_in", True))], c, "adamw_w_in")[0], True)

    tall = lambda a: jnp.concatenate([a, jnp.zeros((part.shape[0] - SMALL_ROWS, D), F32)], axis=0)
    g_s, d_s, m_s, v_s = _sum_slots_adamw(slots, late_slots, tall(_pack_small(ws)), tall(_pack_small(ms)),
                                          tall(_pack_small(vs)))
    small_shapes = {name: ws[name].shape for name, _, _, _ in SMALL_PARTS}
    for kind, slab in (("grad", g_s), ("delta", d_s), ("new_m", m_s), ("new_v", v_s)):
        out[kind].update(_unpack_small(slab, small_shapes))
    loss = g_s[LOSS_ROW, 0]
    block = lambda a, width: lax.dynamic_slice_in_dim(a, mine * width, width, axis=1)
    small_sharded = {"meta_tokens": block(g_s[8:24], D // N_CHIPS),
                     "conv_w": block(g_s[24:40].reshape(32, C_CONV), C_CONV // N_CHIPS)[:CONV_W],
                     "gla_w_gate2": block(g_s[40:44].reshape(RANK, GLA_K), GLA_K // N_CHIPS)}
    for name, g in small_sharded.items():
        record(name, [g, *_adamw(g, *operands(name, False), "adamw_" + name)])

    return (loss, grad_x, *[out[kind][name] for kind in ("grad", "delta", "new_m", "new_v") for name in WEIGHT_NAMES])
```

```python
import functools
from typing import Any, Callable, NamedTuple, Sequence

import jax
import jax.numpy as jnp
from jax import lax
from jax.experimental import pallas as pl
from jax.experimental.pallas import tpu as pltpu

F32 = jnp.float32
BF16 = jnp.bfloat16
MESH = pl.DeviceIdType.MESH

D = 1024
N_META = 16
C_CONV = 512
CONV_W = 31
GLA_K = 256
GLA_V = 512
N_HEADS = 4
DK = 64
DV = 128
RANK = 16
CHUNK = 64
PAD_ROWS = CHUNK - N_META
HEAD_ROWS = CHUNK
D_IN = 2576
D_IN_PAD = 2688
D_GLA_IN = D_IN_PAD - 2 * C_CONV
D_FF = 2816
RMS_EPS = 1e-6
LN_EPS = 1e-5
GATE_TAU = 16.0
N_CHIPS = 4

ADAM_LR = 0.001
ADAM_B1 = 0.9
ADAM_B2 = 0.999
ADAM_EPS = 1e-08
ADAM_WD = 0.01
ADAM_STEP = 10

V7X_VMEM_BYTES = 64 * 1024 * 1024
VMEM_LIMIT = V7X_VMEM_BYTES - 8 * 1024 * 1024
SUBLANES = 8
ROW_PART = 128
FFN_BWD_TILE = 192

WEIGHT_NAMES = ("meta_tokens", "norm_mix_g", "w_in", "conv_w", "conv_b", "conv_ln_g", "conv_ln_b", "gla_w_gate2",
                "gla_gate_b", "gla_norm_g", "w_out", "norm_ffn_g", "w_ffn_gate", "w_ffn_up", "w_ffn_down",
                "norm_final_g")

SMALL_ROWS = 8
SMALL_PARTS = (("norm_mix_g", 0, 0, D), ("norm_ffn_g", 1, 0, D), ("norm_final_g", 2, 0, D),
               ("conv_b", 3, 0, C_CONV), ("conv_ln_g", 3, C_CONV, C_CONV), ("conv_ln_b", 4, 0, C_CONV),
               ("gla_gate_b", 4, C_CONV, GLA_K), ("gla_norm_g", 4, C_CONV + GLA_K, DV))
LOSS_ROW = 5

HBM_SPEC = pl.BlockSpec(memory_space=pltpu.HBM)


def _dot(a, b):
    return jnp.dot(a, b, preferred_element_type=F32)


def _dot_nt(a, b):
    return lax.dot_general(a, b, (((1,), (1,)), ((), ())), preferred_element_type=F32)


def _dot_tn(a, b):
    return lax.dot_general(a, b, (((0,), (0,)), ((), ())), preferred_element_type=F32)


def _sigmoid(x):
    return 1.0 / (1.0 + jnp.exp(-x))


def _const_spec(shape):
    return pl.BlockSpec(shape, lambda *_: (0,) * len(shape), pipeline_mode=pl.Buffered(1))


def _acc_spec(shape):
    return pl.BlockSpec(shape, lambda *_: (0,) * len(shape))


def _params(n_axes):
    return pltpu.CompilerParams(dimension_semantics=("arbitrary",) * n_axes, vmem_limit_bytes=VMEM_LIMIT)


def _row_tile(t, want):
    for r in (want, 384, 192, 128, 64):
        if r <= want and t % r == 0:
            return r
    raise ValueError(f"no row tile for {t}")


def _row_parts(r):
    if r % ROW_PART:
        return [slice(None)]
    return [pl.ds(i * ROW_PART, ROW_PART) for i in range(r // ROW_PART)]


def _in_lockstep(bodies):
    live = list(bodies)
    while live:
        still = []
        for g in live:
            try:
                next(g)
                still.append(g)
            except StopIteration:
                pass
        live = still


class _Plan(NamedTuple):
    arrays: Sequence[Any]
    out_shape: Sequence[Any]
    sems: Sequence[Any]
    make: Callable


def _phases(made):
    return made if len(made) == 3 else (made[0], lambda: None, made[1])


def _call(body, *, name, grid, in_specs, out_specs, out_shape, scratch_shapes=(), plan=None):
    n_in, n_out, n_scr = len(in_specs), len(out_specs), len(scratch_shapes)
    if plan is None:
        plan = _Plan([], [], [], lambda ins, outs, sems: (lambda: None, lambda: None))
    nx_in, nx_out = len(plan.arrays), len(plan.out_shape)
    n_steps = functools.reduce(lambda a, b: a * b, grid)

    def hosted(*refs):
        ins, xins = refs[:n_in], refs[n_in:n_in + nx_in]
        o0 = n_in + nx_in
        outs, xouts = refs[o0:o0 + n_out], refs[o0 + n_out:o0 + n_out + nx_out]
        s0 = o0 + n_out + nx_out
        scr, sems = refs[s0:s0 + n_scr], refs[s0 + n_scr:]
        step = functools.reduce(lambda acc, a: acc * grid[a] + pl.program_id(a), range(len(grid)), 0)
        start, relay, finish = _phases(plan.make(xins, xouts, sems))
        pl.when(step == 0)(start)
        pl.when(step == n_steps - 1)(relay)
        body(*ins, *outs, *scr)
        pl.when(step == n_steps - 1)(finish)

    call = pl.pallas_call(
        hosted, name=name, grid=grid, in_specs=list(in_specs) + [HBM_SPEC] * nx_in,
        out_specs=list(out_specs) + [HBM_SPEC] * nx_out, out_shape=list(out_shape) + list(plan.out_shape),
        scratch_shapes=list(scratch_shapes) + list(plan.sems),
        compiler_params=pltpu.CompilerParams(dimension_semantics=("arbitrary",) * len(grid),
                                             vmem_limit_bytes=VMEM_LIMIT, has_side_effects=nx_in > 0))

    def run(*args):
        res = call(*args, *plan.arrays)
        return res[:n_out], res[n_out:]

    return run


def _pad_head_rows(arrays, plan=None):
    n_ex, seq, _ = arrays[0].shape
    nc = (HEAD_ROWS + seq) // CHUNK
    n = len(arrays)

    def body(*refs):
        for a_ref, o_ref in zip(refs[:n], refs[n:]):
            o_ref[...] = jnp.where(pl.program_id(0) > 0, a_ref[...], 0.0)

    return _call(
        body, name="pad_head_rows", grid=(nc,),
        in_specs=[pl.BlockSpec((n_ex, CHUNK, D), lambda i: (0, jnp.maximum(i - 1, 0), 0))] * n,
        out_specs=[pl.BlockSpec((n_ex, CHUNK, D), lambda i: (0, i, 0))] * n,
        out_shape=[jax.ShapeDtypeStruct((n_ex, HEAD_ROWS + seq, D), F32)] * n,
        plan=plan,
    )(*arrays)


def _set_meta_rows(h0, meta):
    n_ex = h0.shape[0]

    def body(h_ref, meta_ref, o_ref):
        o_ref[...] = jnp.concatenate(
            [h_ref[:, :PAD_ROWS, :], jnp.broadcast_to(meta_ref[...][None], (n_ex, N_META, D))], axis=1)

    head = pl.BlockSpec((n_ex, HEAD_ROWS, D), lambda i: (0, 0, 0))
    return pl.pallas_call(
        body, name="set_meta_rows", grid=(1,), in_specs=[head, pl.BlockSpec((N_META, D), lambda i: (0, 0))],
        out_specs=head, out_shape=jax.ShapeDtypeStruct(h0.shape, F32), input_output_aliases={0: 0},
        compiler_params=_params(1),
    )(h0, meta)


def _in_proj(h0, g_mix, w_in, plan=None):
    t = h0.shape[0]
    r = _row_tile(t, 384)

    def body(h_ref, g_ref, w_ref, u_ref, hn_ref):
        h = h_ref[...]
        rstd = lax.rsqrt(jnp.mean(h * h, axis=-1, keepdims=True) + RMS_EPS)
        hn = (h * rstd * g_ref[...]).astype(BF16)
        hn_ref[...] = hn
        u_ref[...] = _dot(hn, w_ref[...])

    return _call(
        body, name="in_proj", grid=(t // r,),
        in_specs=[pl.BlockSpec((r, D), lambda i: (i, 0)), _const_spec((1, D)), _const_spec((D, D_IN_PAD))],
        out_specs=[pl.BlockSpec((r, D_IN_PAD), lambda i: (i, 0)), pl.BlockSpec((r, D), lambda i: (i, 0))],
        out_shape=[jax.ShapeDtypeStruct((t, D_IN_PAD), F32), jax.ShapeDtypeStruct((t, D), BF16)],
        plan=plan,
    )(h0, g_mix, w_in)


CONV_TILE = 192
CONV_SUB = 32
CONV_LEAD = CONV_SUB - (CONV_W - 1)


def _shifted_copies(src, dst, r):
    for s in range(1, SUBLANES):
        dst[s - 1] = src[s:s + r + CONV_SUB - SUBLANES, :]


def _shifted_rows(src, shifted, start):
    base, s = SUBLANES * (start // SUBLANES), start % SUBLANES
    if s == 0:
        return src[base:base + CONV_SUB, :]
    return shifted[s - 1, base:base + CONV_SUB, :]


def _conv_fwd(u, conv_w, conv_b, ln_g, ln_b, n_ex, lp, plan=None):
    r = CONV_TILE
    nt = lp // r
    hb = r // CONV_SUB

    def body(cur_ref, prev_ref, w_ref, b_ref, lg_ref, lb_ref, yc_ref, y_ref, glu, glu_sh):
        i = pl.program_id(1)
        cur = cur_ref[...]
        glu[CONV_SUB:CONV_SUB + r, :] = cur[:, :C_CONV] * _sigmoid(cur[:, C_CONV:])
        pv = prev_ref[...]
        halo = pv[:, :C_CONV] * _sigmoid(pv[:, C_CONV:])
        glu[0:CONV_SUB, :] = jnp.where(i > 0, halo, 0.0)
        _shifted_copies(glu, glu_sh, r)
        w = w_ref[...]
        for j in range(r // CONV_SUB):
            r0 = j * CONV_SUB
            acc = jnp.zeros((CONV_SUB, C_CONV), F32) + b_ref[...]
            for k in range(CONV_W):
                acc = acc + w[k:k + 1, :] * _shifted_rows(glu, glu_sh, r0 + CONV_LEAD + k)
            mu = jnp.mean(acc, axis=-1, keepdims=True)
            cen = acc - mu
            var = jnp.mean(cen * cen, axis=-1, keepdims=True)
            out = cen * lax.rsqrt(var + LN_EPS) * lg_ref[...] + lb_ref[...]
            y = out * _sigmoid(out)
            row = i * r + r0 + lax.broadcasted_iota(jnp.int32, (CONV_SUB, 1), 0)
            y = jnp.where(row >= PAD_ROWS, y, 0.0)
            yc_ref[r0:r0 + CONV_SUB, :] = acc
            y_ref[r0:r0 + CONV_SUB, :] = y.astype(BF16)

    t = n_ex * lp
    return _call(
        body, name="conv_fwd", grid=(n_ex, nt),
        in_specs=[pl.BlockSpec((r, 2 * C_CONV), lambda b, i: (b * nt + i, 0)),
                  pl.BlockSpec((CONV_SUB, 2 * C_CONV), lambda b, i: (jnp.maximum((b * nt + i) * hb - 1, 0), 0)),
                  _const_spec((32, C_CONV)), _const_spec((1, C_CONV)), _const_spec((1, C_CONV)), _const_spec((1, C_CONV))],
        out_specs=[pl.BlockSpec((r, C_CONV), lambda b, i: (b * nt + i, 0)),
                   pl.BlockSpec((r, C_CONV), lambda b, i: (b * nt + i, 0))],
        out_shape=[jax.ShapeDtypeStruct((t, C_CONV), F32), jax.ShapeDtypeStruct((t, C_CONV), BF16)],
        scratch_shapes=[pltpu.VMEM((r + CONV_SUB, C_CONV), F32),
                        pltpu.VMEM((SUBLANES - 1, r + CONV_SUB - SUBLANES, C_CONV), F32)],
        plan=plan,
    )(u, u, conv_w, conv_b, ln_g, ln_b)


def _mix_out_ffn_up(h0, y_conv, y_gla, w_out, g_ffn, w_gate, w_up, plan=None):
    t = h0.shape[0]
    r = _row_tile(t, 384)

    def body(h0_ref, yc_ref, yg_ref, wo_ref, g_ref, wg_ref, wu_ref, h1_ref, hn_ref, gate_ref, up_ref, act_ref):
        h1 = h0_ref[...] + _dot(yc_ref[...], wo_ref[0:C_CONV, :]) + _dot(yg_ref[...], wo_ref[C_CONV:D, :])
        h1_ref[...] = h1
        rstd = lax.rsqrt(jnp.mean(h1 * h1, axis=-1, keepdims=True) + RMS_EPS)
        hn = (h1 * rstd * g_ref[...]).astype(BF16)
        hn_ref[...] = hn
        gate = _dot(hn, wg_ref[...])
        up = _dot(hn, wu_ref[...])
        gate_ref[...] = gate
        up_ref[...] = up
        act_ref[...] = (gate * _sigmoid(gate) * up).astype(BF16)

    rows = lambda w: pl.BlockSpec((r, w), lambda i: (i, 0))
    return _call(
        body, name="mix_out_ffn_up", grid=(t // r,),
        in_specs=[rows(D), rows(C_CONV), rows(GLA_V), _const_spec((D, D)), _const_spec((1, D)),
                  _const_spec((D, D_FF)), _const_spec((D, D_FF))],
        out_specs=[rows(D), rows(D), rows(D_FF), rows(D_FF), rows(D_FF)],
        out_shape=[jax.ShapeDtypeStruct((t, D), F32), jax.ShapeDtypeStruct((t, D), BF16),
                   jax.ShapeDtypeStruct((t, D_FF), F32), jax.ShapeDtypeStruct((t, D_FF), F32),
                   jax.ShapeDtypeStruct((t, D_FF), BF16)],
        plan=plan,
    )(h0, y_conv, y_gla, w_out, g_ffn, w_gate, w_up)


def _ffn_down_loss(act, w_down, h1, target, g_final, row_mask):
    t = h1.shape[0]
    r = _row_tile(t, 384)

    def body(act_ref, wd_ref, h1_ref, tgt_ref, gf_ref, mask_ref, dh2_ref, loss_ref, dgf_ref):
        @pl.when(pl.program_id(0) == 0)
        def _():
            loss_ref[...] = jnp.zeros_like(loss_ref)
            dgf_ref[...] = jnp.zeros_like(dgf_ref)

        gf = gf_ref[...]

        def part(rows):
            h2 = h1_ref[rows, :] + _dot(act_ref[rows, :], wd_ref[...])
            yield
            rstd = lax.rsqrt(jnp.mean(h2 * h2, axis=-1, keepdims=True) + RMS_EPS)
            nrm = h2 * rstd
            err = (nrm * gf - tgt_ref[rows, :]) * mask_ref[rows, :]
            loss_ref[...] += jnp.sum(err * err) * (0.5 / D)
            dy = err * (1.0 / D)
            dgf_ref[...] += jnp.sum(dy * nrm, axis=0, keepdims=True)
            dn = dy * gf
            dh2_ref[rows, :] = rstd * (dn - nrm * jnp.mean(dn * nrm, axis=-1, keepdims=True))

        _in_lockstep(part(rows) for rows in _row_parts(r))

    rows = lambda w: pl.BlockSpec((r, w), lambda i: (i, 0))
    return pl.pallas_call(
        body, name="ffn_down_loss", grid=(t // r,),
        in_specs=[rows(D_FF), _const_spec((D_FF, D)), rows(D), rows(D), _const_spec((1, D)), rows(1)],
        out_specs=[rows(D), _acc_spec((1, 128)), _acc_spec((1, D))],
        out_shape=[jax.ShapeDtypeStruct((t, D), F32), jax.ShapeDtypeStruct((1, 128), F32),
                   jax.ShapeDtypeStruct((1, D), F32)],
        compiler_params=_params(1),
    )(act, w_down, h1, target, g_final, row_mask)


def _ffn_bwd(dh2, gate, up, h1, w_down_t, w_gate_t, w_up_t, w_out_t, g_ffn):
    t = h1.shape[0]
    r = _row_tile(t, FFN_BWD_TILE)

    def body(dh2_ref, gate_ref, up_ref, h1_ref, wd_ref, wg_ref, wu_ref, wo_ref, g_ref,
             dgate_ref, dup_ref, dh1_ref, dycat_ref, dg_ref):
        @pl.when(pl.program_id(0) == 0)
        def _():
            dg_ref[...] = jnp.zeros_like(dg_ref)

        dh2 = dh2_ref[...]
        dact = _dot(dh2.astype(BF16), wd_ref[...])
        gate = gate_ref[...]
        sg = _sigmoid(gate)
        dgate = (dact * up_ref[...] * (sg * (1.0 + gate * (1.0 - sg)))).astype(BF16)
        dup = (dact * (gate * sg)).astype(BF16)
        dgate_ref[...] = dgate
        dup_ref[...] = dup
        dhn = _dot(dgate, wg_ref[...]) + _dot(dup, wu_ref[...])
        h1 = h1_ref[...]
        rstd = lax.rsqrt(jnp.mean(h1 * h1, axis=-1, keepdims=True) + RMS_EPS)
        nrm = h1 * rstd
        dg_ref[...] += jnp.sum(dhn * nrm, axis=0, keepdims=True)
        dn = dhn * g_ref[...]
        dh1 = dh2 + rstd * (dn - nrm * jnp.mean(dn * nrm, axis=-1, keepdims=True))
        dh1_ref[...] = dh1
        dycat_ref[...] = _dot(dh1.astype(BF16), wo_ref[...])

    rows = lambda w: pl.BlockSpec((r, w), lambda i: (i, 0))
    return pl.pallas_call(
        body, name="ffn_bwd", grid=(t // r,),
        in_specs=[rows(D), rows(D_FF), rows(D_FF), rows(D), _const_spec((D, D_FF)), _const_spec((D_FF, D)),
                  _const_spec((D_FF, D)), _const_spec((D, D)), _const_spec((1, D))],
        out_specs=[rows(D_FF), rows(D_FF), rows(D), rows(D), _acc_spec((1, D))],
        out_shape=[jax.ShapeDtypeStruct((t, D_FF), BF16), jax.ShapeDtypeStruct((t, D_FF), BF16),
                   jax.ShapeDtypeStruct((t, D), F32), jax.ShapeDtypeStruct((t, D), F32),
                   jax.ShapeDtypeStruct((1, D), F32)],
        compiler_params=_params(1),
    )(dh2, gate, up, h1, w_down_t, w_gate_t, w_up_t, w_out_t, g_ffn)


def _conv_bwd(dycat, yc, u, conv_w, ln_g, ln_b, n_ex, lp, plan=None):
    r = CONV_TILE
    nt = lp // r
    hb = r // CONV_SUB
    nsub = r // CONV_SUB

    def ln_bwd(dy, yc_rows, live, lg, lb):
        mu = jnp.mean(yc_rows, axis=-1, keepdims=True)
        cen = yc_rows - mu
        rs = lax.rsqrt(jnp.mean(cen * cen, axis=-1, keepdims=True) + LN_EPS)
        yn = cen * rs
        out = yn * lg + lb
        so = _sigmoid(out)
        dout = jnp.where(live, dy * (so * (1.0 + out * (1.0 - so))), 0.0)
        dyn = dout * lg
        dyc = rs * (dyn - jnp.mean(dyn, axis=-1, keepdims=True) - yn * jnp.mean(dyn * yn, axis=-1, keepdims=True))
        return dyc, dout, yn

    def body(dy_ref, dyn_ref, yc_ref, ycn_ref, cur_ref, prev_ref, w_ref, lg_ref, lb_ref,
             du_ref, dw_ref, db_ref, dlg_ref, dlb_ref, glu, dycs, dwacc, glu_sh, dycs_sh):
        b = pl.program_id(0)
        i = pl.program_id(1)
        first = jnp.logical_and(b == 0, i == 0)

        @pl.when(first)
        def _():
            dwacc[...] = jnp.zeros_like(dwacc)
            db_ref[...] = jnp.zeros_like(db_ref)
            dlg_ref[...] = jnp.zeros_like(dlg_ref)
            dlb_ref[...] = jnp.zeros_like(dlb_ref)

        lg, lb = lg_ref[...], lb_ref[...]
        cur = cur_ref[...]
        sig = _sigmoid(cur[:, C_CONV:])
        glu[CONV_SUB:CONV_SUB + r, :] = cur[:, :C_CONV] * sig
        pv = prev_ref[...]
        glu[0:CONV_SUB, :] = jnp.where(i > 0, pv[:, :C_CONV] * _sigmoid(pv[:, C_CONV:]), 0.0)

        row = i * r + lax.broadcasted_iota(jnp.int32, (r, 1), 0)
        dyc, dout, yn = ln_bwd(dy_ref[...], yc_ref[...], row >= PAD_ROWS, lg, lb)
        dycs[0:r, :] = dyc
        dycn, _, _ = ln_bwd(dyn_ref[...], ycn_ref[...], i < nt - 1, lg, lb)
        dycs[r:r + CONV_SUB, :] = dycn
        db_ref[...] += jnp.sum(dyc, axis=0, keepdims=True)
        dlg_ref[...] += jnp.sum(dout * yn, axis=0, keepdims=True)
        dlb_ref[...] += jnp.sum(dout, axis=0, keepdims=True)

        _shifted_copies(glu, glu_sh, r)
        _shifted_copies(dycs, dycs_sh, r)
        w = w_ref[...]
        for j in range(nsub):
            r0 = j * CONV_SUB
            dblk = dycs[r0:r0 + CONV_SUB, :]
            dglu = jnp.zeros((CONV_SUB, C_CONV), F32)
            for k in range(CONV_W):
                dglu = dglu + w[k:k + 1, :] * _shifted_rows(dycs, dycs_sh, r0 + (CONV_W - 1) - k)
                prod = dblk * _shifted_rows(glu, glu_sh, r0 + CONV_LEAD + k)
                dwacc[k] += prod.reshape(CONV_SUB // SUBLANES, SUBLANES, C_CONV).sum(axis=0)
            sg = sig[r0:r0 + CONV_SUB, :]
            cv = cur[r0:r0 + CONV_SUB, :C_CONV]
            du_ref[r0:r0 + CONV_SUB, :C_CONV] = (dglu * sg).astype(BF16)
            du_ref[r0:r0 + CONV_SUB, C_CONV:] = (dglu * cv * sg * (1.0 - sg)).astype(BF16)

        @pl.when(jnp.logical_and(b == n_ex - 1, i == nt - 1))
        def _():
            dw_ref[...] = jnp.sum(dwacc[...], axis=1)

    t = n_ex * lp
    cur_rows = lambda w, col: pl.BlockSpec((r, w), lambda b, i: (b * nt + i, col))
    nxt_rows = lambda w, col: pl.BlockSpec(
        (CONV_SUB, w), lambda b, i: (jnp.minimum((b * nt + i + 1) * hb, n_ex * nt * hb - 1), col))
    return _call(
        body, name="conv_bwd", grid=(n_ex, nt),
        in_specs=[cur_rows(C_CONV, 0), nxt_rows(C_CONV, 0), cur_rows(C_CONV, 0), nxt_rows(C_CONV, 0),
                  cur_rows(2 * C_CONV, 0),
                  pl.BlockSpec((CONV_SUB, 2 * C_CONV), lambda b, i: (jnp.maximum((b * nt + i) * hb - 1, 0), 0)),
                  _const_spec((32, C_CONV)), _const_spec((1, C_CONV)), _const_spec((1, C_CONV))],
        out_specs=[cur_rows(2 * C_CONV, 0), _acc_spec((32, C_CONV)), _acc_spec((1, C_CONV)),
                   _acc_spec((1, C_CONV)), _acc_spec((1, C_CONV))],
        out_shape=[jax.ShapeDtypeStruct((t, 2 * C_CONV), BF16), jax.ShapeDtypeStruct((32, C_CONV), F32),
                   jax.ShapeDtypeStruct((1, C_CONV), F32), jax.ShapeDtypeStruct((1, C_CONV), F32),
                   jax.ShapeDtypeStruct((1, C_CONV), F32)],
        scratch_shapes=[pltpu.VMEM((r + CONV_SUB, C_CONV), F32), pltpu.VMEM((r + CONV_SUB, C_CONV), F32),
                        pltpu.VMEM((32, SUBLANES, C_CONV), F32),
                        pltpu.VMEM((SUBLANES - 1, r + CONV_SUB - SUBLANES, C_CONV), F32),
                        pltpu.VMEM((SUBLANES - 1, r + CONV_SUB - SUBLANES, C_CONV), F32)],
        plan=plan,
    )(dycat, dycat, yc, yc, u, u, conv_w, ln_g, ln_b)


HEAD_ROWS_ALL = N_HEADS * CHUNK


def _gla_gates(lr, w2, gb, first_chunk):
    z = _dot(lr.astype(BF16), w2) + gb
    a = (jnp.minimum(z, 0.0) - jnp.log(1.0 + jnp.exp(-jnp.abs(z)))) * (1.0 / GATE_TAU)
    row = lax.broadcasted_iota(jnp.int32, (CHUNK, 1), 0)
    live = jnp.logical_or(jnp.logical_not(first_chunk), row >= PAD_ROWS)
    return z, jnp.where(live, a, 0.0), live


def _tri(lower):
    i = lax.broadcasted_iota(jnp.int32, (CHUNK, CHUNK), 0)
    j = lax.broadcasted_iota(jnp.int32, (CHUNK, CHUNK), 1)
    return (i >= j) if lower else (i <= j)


def _head_of(shape, axis, per_head):
    return lax.broadcasted_iota(jnp.int32, shape, axis) // per_head


def _expand(x, lanes_per_head):
    rows, lanes = HEAD_ROWS_ALL, x.shape[1]
    keep = _head_of((rows, lanes), 0, CHUNK) == _head_of((rows, lanes), 1, lanes_per_head)
    return jnp.where(keep, jnp.tile(x, (N_HEADS, 1)), 0.0)


def _expand_lanes(x):
    rows, w = x.shape
    keep = _head_of((rows, N_HEADS * w), 0, CHUNK) == _head_of((rows, N_HEADS * w), 1, w)
    return jnp.where(keep, jnp.tile(x, (1, N_HEADS)), 0.0)


def _expand_state(st):
    rows, lanes = N_HEADS * DV, st.shape[1]
    keep = _head_of((rows, lanes), 0, DV) == _head_of((rows, lanes), 1, DK)
    return jnp.where(keep, jnp.tile(st, (N_HEADS, 1)), 0.0)


def _fold(t, rows_per_head):
    lane_head = _head_of((rows_per_head, t.shape[1]), 1, DK)
    out = jnp.where(lane_head == 0, t[0:rows_per_head], 0.0)
    for h in range(1, N_HEADS):
        out = out + jnp.where(lane_head == h, t[h * rows_per_head:(h + 1) * rows_per_head], 0.0)
    return out


def _rows_by_head(x):
    return jnp.concatenate([x[:, h * DV:(h + 1) * DV] for h in range(N_HEADS)], axis=0)


def _lanes_by_head(x):
    return jnp.concatenate([x[h * CHUNK:(h + 1) * CHUNK] for h in range(N_HEADS)], axis=1)


def _running_sum(a, lower):
    hi = a.astype(BF16)
    rest = a - hi.astype(F32)
    mid = rest.astype(BF16)
    lo = (rest - mid.astype(F32)).astype(BF16)
    w = a.shape[1]
    parts = _dot(_tri(lower).astype(F32).astype(BF16), jnp.concatenate([hi, mid, lo], axis=1))
    return parts[:, :w] + parts[:, w:2 * w] + parts[:, 2 * w:]


def _stacked_causal():
    i = lax.broadcasted_iota(jnp.int32, (HEAD_ROWS_ALL, CHUNK), 0) % CHUNK
    j = lax.broadcasted_iota(jnp.int32, (HEAD_ROWS_ALL, CHUNK), 1)
    return i >= j


GLA_GROUP = 3


def _gla_chunk(q, k, v, lr, w2, gb, first_chunk):
    z, a, live = _gla_gates(lr, w2, gb, first_chunk)
    yield
    b = _running_sum(a, True)
    yield
    bl = b[CHUNK - 1:CHUNK, :]
    e_pos, e_neg, e_dec = jnp.exp(b), jnp.exp(-b), jnp.exp(bl - b)
    q_f, k_f, kd_f = q * (DK ** -0.5) * e_pos, k * e_neg, k * e_dec
    qx = _expand(q_f, DK).astype(BF16)
    k_in, k_dec, v_b = k_f.astype(BF16), kd_f.astype(BF16), v.astype(BF16)
    s = jnp.where(_stacked_causal(), _dot_nt(qx, k_in), 0.0).astype(BF16)
    yield
    p = _dot(s, v_b)
    yield
    o_intra = jnp.concatenate([p[h * CHUNK:(h + 1) * CHUNK, h * DV:(h + 1) * DV] for h in range(N_HEADS)], axis=0)
    return dict(z=z, live=live, bl=bl, e_pos=e_pos, e_neg=e_neg, e_dec=e_dec, q_f=q_f, k_f=k_f, kd_f=kd_f,
                qx=qx, k_in=k_in, k_dec=k_dec, v_b=v_b, s=s, o_intra=o_intra, decay=jnp.exp(bl))


def _gla_fwd(u, w2, gb, ng, n_ex, lp, plan=None):
    nc = lp // CHUNK
    t = n_ex * lp
    rows_of = lambda j: pl.ds(j * CHUNK, CHUNK)

    def body(qk_ref, v_ref, g_ref, lr_ref, w2_ref, gb_ref, ng_ref, y_ref, st_ref, state):
        n = pl.program_id(0)

        @pl.when(n == 0)
        def _():
            state[...] = jnp.zeros_like(state)

        carried = [state[e] for e in range(n_ex)]

        def one_chunk(e, j):
            rows = rows_of(j)
            qk = qk_ref[e, rows, :]
            first = jnp.logical_and(n == 0, j == 0)
            c = yield from _gla_chunk(qk[:, :GLA_K], qk[:, GLA_K:], v_ref[e, rows, :], lr_ref[e, rows, :],
                                      w2_ref[...], gb_ref[...], first)
            kv = _fold(_dot_tn(c["v_b"], c["k_dec"]), DV)
            g = _rows_by_head(g_ref[e, rows, :])
            gate = ng_ref[...] * (g * _sigmoid(g))
            yield
            for _ in range(j):
                yield
            st = carried[e]
            st_ref[e, pl.ds(j * DV, DV), :] = st
            o = c["o_intra"] + _dot_nt(c["qx"], st.astype(BF16))
            rstd = lax.rsqrt(jnp.mean(o * o, axis=-1, keepdims=True) + RMS_EPS)
            y_ref[e, rows, :] = _lanes_by_head(o * rstd * gate).astype(BF16)
            carried[e] = c["decay"] * st + kv

        _in_lockstep(one_chunk(e, j) for j in range(GLA_GROUP) for e in range(n_ex))
        for e in range(n_ex):
            state[e] = carried[e]

    u3 = u.reshape(n_ex, lp, D_IN_PAD)
    blk = lambda w, col: pl.BlockSpec((n_ex, GLA_GROUP * CHUNK, w), lambda n: (0, n, col))
    (y, states), extra = _call(
        body, name="gla_fwd", grid=(nc // GLA_GROUP,),
        in_specs=[blk(2 * GLA_K, 2), blk(GLA_V, 3), blk(GLA_V, 4), blk(128, 20),
                  _const_spec((128, GLA_K)), _const_spec((1, GLA_K)), _const_spec((1, DV))],
        out_specs=[blk(GLA_V, 0), pl.BlockSpec((n_ex, GLA_GROUP * DV, GLA_K), lambda n: (0, n, 0))],
        out_shape=[jax.ShapeDtypeStruct((n_ex, lp, GLA_V), BF16),
                   jax.ShapeDtypeStruct((n_ex, nc * DV, GLA_K), F32)],
        scratch_shapes=[pltpu.VMEM((n_ex, DV, GLA_K), F32)],
        plan=plan,
    )(u3, u3, u3, u3, w2, gb, ng)
    return (y.reshape(t, GLA_V), states), extra


def _gla_bwd(dycat, u, states, w2, gb, ng, n_ex, lp, plan=None):
    nc = lp // CHUNK
    t = n_ex * lp

    def body(dy_ref, qk_ref, v_ref, g_ref, lr_ref, st_ref, w2_ref, gb_ref, ng_ref,
             du_ref, dw2_ref, dgb_ref, dng_ref, dstate):
        n = pl.program_id(0)
        group = nc // GLA_GROUP - 1 - n

        @pl.when(n == 0)
        def _():
            dw2_ref[...] = jnp.zeros_like(dw2_ref)
            dgb_ref[...] = jnp.zeros_like(dgb_ref)
            dng_ref[...] = jnp.zeros_like(dng_ref)
            dstate[...] = jnp.zeros_like(dstate)

        carried = [dstate[e] for e in range(n_ex)]

        def one_chunk(e, order):
            j = GLA_GROUP - 1 - order
            rows = pl.ds(j * CHUNK, CHUNK)
            qk = qk_ref[e, rows, :]
            lr = lr_ref[e, rows, :]
            st = st_ref[e, pl.ds(j * DV, DV), :]
            first = jnp.logical_and(group == 0, j == 0)
            c = yield from _gla_chunk(qk[:, :GLA_K], qk[:, GLA_K:], v_ref[e, rows, :], lr, w2_ref[...], gb_ref[...],
                                      first)
            qx, k_in, k_dec, v_b, s = c["qx"], c["k_in"], c["k_dec"], c["v_b"], c["s"]
            st_b = st.astype(BF16)
            o = c["o_intra"] + _dot_nt(qx, st_b)
            ngv = ng_ref[...]
            yield
            rstd = lax.rsqrt(jnp.mean(o * o, axis=-1, keepdims=True) + RMS_EPS)
            nrm = o * rstd
            g = _rows_by_head(g_ref[e, rows, :])
            dy = _rows_by_head(dy_ref[e, rows, :])
            sg = _sigmoid(g)
            dg = dy * nrm * ngv * (sg * (1.0 + g * (1.0 - sg)))
            dt = dy * (g * sg)
            dng_ref[...] += jnp.sum(dt * nrm, axis=0, keepdims=True)
            dn = dt * ngv
            do = rstd * (dn - nrm * jnp.mean(dn * nrm, axis=-1, keepdims=True))
            do_b = do.astype(BF16)
            dox = _expand_lanes(do).astype(BF16)
            yield
            da = jnp.where(_stacked_causal(), _dot_nt(dox, v_b), 0.0).astype(BF16)
            dv_intra = _dot_tn(s, dox)
            dst_own = _dot_tn(do_b, qx)
            yield
            dq_in = _fold(_dot(da, k_in) + _dot(do_b, st_b), CHUNK)
            dk_in = _dot_tn(da, qx)
            dq = dq_in * (DK ** -0.5) * c["e_pos"]
            yield
            for _ in range(order):
                yield
            dst = carried[e]
            dstx = _expand_state(dst).astype(BF16)
            dv = dv_intra + _dot_nt(k_dec, dstx)
            dk_dec = _dot(v_b, dstx)
            carried[e] = dst_own + c["decay"] * dst
            yield
            dbl = (jnp.sum(dk_dec * c["kd_f"], axis=0, keepdims=True)
                   + c["decay"] * jnp.sum(dst * st, axis=0, keepdims=True))
            dk = dk_in * c["e_neg"] + dk_dec * c["e_dec"]
            db = dq_in * c["q_f"] - dk_in * c["k_f"] - dk_dec * c["kd_f"]
            row = lax.broadcasted_iota(jnp.int32, (CHUNK, 1), 0)
            da_log = _running_sum(db + jnp.where(row == CHUNK - 1, dbl, 0.0), False)
            yield
            dz = jnp.where(c["live"], da_log * (1.0 - _sigmoid(c["z"])) * (1.0 / GATE_TAU), 0.0)
            dz_b = dz.astype(BF16)
            out = du_ref.at[e, rows, :]
            out[:, 0:GLA_K] = dq.astype(BF16)
            out[:, GLA_K:2 * GLA_K] = dk.astype(BF16)
            out[:, 2 * GLA_K:2 * GLA_K + GLA_V] = dv.astype(BF16)
            out[:, 2 * GLA_K + GLA_V:2 * GLA_K + 2 * GLA_V] = _lanes_by_head(dg).astype(BF16)
            out[:, 2 * GLA_K + 2 * GLA_V:] = _dot_nt(dz_b, w2_ref[...]).astype(BF16)
            dw2_ref[...] += _dot_tn(lr.astype(BF16), dz_b)
            dgb_ref[...] += jnp.sum(dz, axis=0, keepdims=True)

        _in_lockstep(one_chunk(e, order) for order in range(GLA_GROUP) for e in range(n_ex))
        for e in range(n_ex):
            dstate[e] = carried[e]

    u3 = u.reshape(n_ex, lp, D_IN_PAD)
    rev = lambda w, col: pl.BlockSpec((n_ex, GLA_GROUP * CHUNK, w), lambda n: (0, nc // GLA_GROUP - 1 - n, col))
    (du, d_w2, d_gb, d_ng), extra = _call(
        body, name="gla_bwd", grid=(nc // GLA_GROUP,),
        in_specs=[rev(GLA_V, 1), rev(2 * GLA_K, 2), rev(GLA_V, 3), rev(GLA_V, 4), rev(128, 20),
                  pl.BlockSpec((n_ex, GLA_GROUP * DV, GLA_K), lambda n: (0, nc // GLA_GROUP - 1 - n, 0)),
                  _const_spec((128, GLA_K)), _const_spec((1, GLA_K)), _const_spec((1, DV))],
        out_specs=[rev(D_GLA_IN, 0), _acc_spec((128, GLA_K)), _acc_spec((1, GLA_K)), _acc_spec((1, DV))],
        out_shape=[jax.ShapeDtypeStruct((n_ex, lp, D_GLA_IN), BF16), jax.ShapeDtypeStruct((128, GLA_K), F32),
                   jax.ShapeDtypeStruct((1, GLA_K), F32), jax.ShapeDtypeStruct((1, DV), F32)],
        scratch_shapes=[pltpu.VMEM((n_ex, DV, GLA_K), F32)],
        plan=plan,
    )(dycat.reshape(n_ex, lp, D), u3, u3, u3, u3, states, w2, gb, ng)
    return (du.reshape(t, D_GLA_IN), d_w2, d_gb, d_ng), extra


def _in_proj_bwd(du_conv, du_gla, w_in_t_conv, w_in_t_gla, h0, dh1, g_mix, plan=None):
    t = h0.shape[0]
    r = _row_tile(t, 384)

    def body(dc_ref, dg_ref, wc_ref, wg_ref, h_ref, dh1_ref, g_ref, dh0_ref, dgm_ref):
        @pl.when(pl.program_id(0) == 0)
        def _():
            dgm_ref[...] = jnp.zeros_like(dgm_ref)

        dhn = _dot(dc_ref[...], wc_ref[...]) + _dot(dg_ref[...], wg_ref[...])
        h = h_ref[...]
        rstd = lax.rsqrt(jnp.mean(h * h, axis=-1, keepdims=True) + RMS_EPS)
        nrm = h * rstd
        dgm_ref[...] += jnp.sum(dhn * nrm, axis=0, keepdims=True)
        dn = dhn * g_ref[...]
        dh0_ref[...] = dh1_ref[...] + rstd * (dn - nrm * jnp.mean(dn * nrm, axis=-1, keepdims=True))

    rows = lambda w: pl.BlockSpec((r, w), lambda i: (i, 0))
    return _call(
        body, name="in_proj_bwd", grid=(t // r,),
        in_specs=[rows(2 * C_CONV), rows(D_GLA_IN), _const_spec((2 * C_CONV, D)), _const_spec((D_GLA_IN, D)),
                  rows(D), rows(D), _const_spec((1, D))],
        out_specs=[rows(D), _acc_spec((1, D))],
        out_shape=[jax.ShapeDtypeStruct((t, D), F32), jax.ShapeDtypeStruct((1, D), F32)],
        plan=plan,
    )(du_conv, du_gla, w_in_t_conv, w_in_t_gla, h0, dh1, g_mix)


def _wgrad(x, dy, name, plan=None):
    t, m = x.shape
    n = dy.shape[1]
    tk = t // 3 if t % (3 * 128) == 0 else _row_tile(t, 384)
    tm = m if m <= D_GLA_IN else m // 2

    def body(x_ref, dy_ref, o_ref):
        @pl.when(pl.program_id(1) == 0)
        def _():
            o_ref[...] = jnp.zeros_like(o_ref)

        o_ref[...] += _dot_tn(x_ref[...].astype(BF16), dy_ref[...].astype(BF16))

    (out,), extra = _call(
        body, name=name, grid=(m // tm, t // tk),
        in_specs=[pl.BlockSpec((tk, tm), lambda i, k: (k, i)), pl.BlockSpec((tk, n), lambda i, k: (k, 0))],
        out_specs=[pl.BlockSpec((tm, n), lambda i, k: (i, 0))],
        out_shape=[jax.ShapeDtypeStruct((m, n), F32)],
        plan=plan,
    )(x, dy)
    return out, extra


def _wgrad_pair(xa, xb, dy, name):
    t, m = xa.shape
    n = dy.shape[1]
    tk = t // 3 if t % (3 * 128) == 0 else _row_tile(t, 384)

    def body(xa_ref, xb_ref, dy_ref, o_ref):
        @pl.when(pl.program_id(1) == 0)
        def _():
            o_ref[...] = jnp.zeros_like(o_ref)

        x = jnp.where(pl.program_id(0) == 0, xa_ref[...], xb_ref[...])
        o_ref[...] += _dot_tn(x.astype(BF16), dy_ref[...].astype(BF16))

    rows = lambda w: pl.BlockSpec((tk, w), lambda i, k: (k, 0))
    return pl.pallas_call(
        body, name=name, grid=(2, t // tk), in_specs=[rows(m), rows(m), rows(n)],
        out_specs=pl.BlockSpec((m, n), lambda i, k: (i, 0)),
        out_shape=jax.ShapeDtypeStruct((2 * m, n), F32), compiler_params=_params(2),
    )(xa, xb, dy)


def _adam_update(g, w, m, v):
    m2 = ADAM_B1 * m + (1.0 - ADAM_B1) * g
    v2 = ADAM_B2 * v + (1.0 - ADAM_B2) * (g * g)
    m_hat = m2 / (1.0 - ADAM_B1 ** ADAM_STEP)
    v_hat = v2 / (1.0 - ADAM_B2 ** ADAM_STEP)
    delta = -ADAM_LR * (m_hat / (jnp.sqrt(v_hat) + ADAM_EPS) + ADAM_WD * w)
    return delta, m2, v2


ADAMW_STEPS = 4


def _adamw(g, w, m, v, name):
    rows, cols = g.shape
    steps = ADAMW_STEPS if rows % (ADAMW_STEPS * SUBLANES) == 0 else 1

    def body(g_ref, w_ref, m_ref, v_ref, d_ref, m2_ref, v2_ref):
        d_ref[...], m2_ref[...], v2_ref[...] = _adam_update(g_ref[...], w_ref[...], m_ref[...], v_ref[...])

    spec = pl.BlockSpec((rows // steps, cols), lambda i: (i, 0))
    return pl.pallas_call(
        body, name=name, grid=(steps,), in_specs=[spec] * 4, out_specs=[spec] * 3,
        out_shape=[jax.ShapeDtypeStruct(g.shape, F32)] * 3, compiler_params=_params(1),
    )(g, w, m, v)


def _adamw_halves(items, c, name):
    n = len(items)
    h = items[0][0].shape[1]
    steps = ADAMW_STEPS if all(it[0].shape[0] % (ADAMW_STEPS * SUBLANES) == 0 for it in items) else 1

    def body(c_ref, *refs):
        ins, outs = refs[:5 * n], refs[5 * n:]
        own = pl.program_id(1) == c_ref[0]
        for i in range(n):
            a_ref, b_ref, w_ref, m_ref, v_ref = ins[5 * i:5 * i + 5]
            go_ref, d_ref, m2_ref, v2_ref = outs[4 * i:4 * i + 4]
            g = jnp.where(own, a_ref[...], b_ref[...])
            go_ref[...] = g
            d_ref[...], m2_ref[...], v2_ref[...] = _adam_update(g, w_ref[...], m_ref[...], v_ref[...])

    in_specs, out_specs, out_shape, args = [pl.BlockSpec(memory_space=pltpu.SMEM)], [], [], []
    for mine, theirs, w, m, v in items:
        tr = mine.shape[0] // steps
        half = pl.BlockSpec((tr, h), lambda i, j: (i, 0))
        full = pl.BlockSpec((tr, h), lambda i, j: (i, j))
        in_specs += [half, half, full, full, full]
        out_specs += [full] * 4
        out_shape += [jax.ShapeDtypeStruct(w.shape, F32)] * 4
        args += [mine, theirs, w, m, v]
    res = pl.pallas_call(
        body, name=name, grid=(steps, 2), in_specs=in_specs, out_specs=out_specs, out_shape=out_shape,
        compiler_params=_params(2),
    )(jnp.reshape(c, (1,)).astype(jnp.int32), *args)
    return [res[4 * i:4 * i + 4] for i in range(n)]


def _rs_add_halves(g, recv, c, name):
    _, rows, w = g.shape
    h = w // 2

    def body(c_ref, a_ref, b_ref, o_ref):
        o_ref[...] = (a_ref[...] + b_ref[...]).astype(BF16)

    return pl.pallas_call(
        body, name=name,
        grid_spec=pltpu.PrefetchScalarGridSpec(
            num_scalar_prefetch=1, grid=(N_CHIPS,),
            in_specs=[pl.BlockSpec((1, rows, h), lambda j, s: (j, 0, s[0])),
                      pl.BlockSpec((1, rows, h), lambda j, s: (j, 0, 0))],
            out_specs=pl.BlockSpec((1, rows, h), lambda j, s: (j, 0, 0))),
        out_shape=jax.ShapeDtypeStruct((N_CHIPS, rows, h), BF16),
        compiler_params=_params(1),
    )(jnp.reshape(c, (1,)).astype(jnp.int32), g, recv)


def _rs_sum(own, others, mine, name):
    _, rows, h = own.shape
    tr = rows // 2 if rows % 16 == 0 and rows > 64 else rows

    def body(mine_ref, own_ref, oth_ref, o_ref):
        p = oth_ref[...].astype(F32)
        o_ref[...] = ((own_ref[0].astype(F32) + p[0]) + p[1]) + p[2]

    return pl.pallas_call(
        body, name=name,
        grid_spec=pltpu.PrefetchScalarGridSpec(
            num_scalar_prefetch=1, grid=(rows // tr,),
            in_specs=[pl.BlockSpec((1, tr, h), lambda i, s: (s[0], i, 0)),
                      pl.BlockSpec((3, tr, h), lambda i, s: (0, i, 0))],
            out_specs=pl.BlockSpec((tr, h), lambda i, s: (i, 0))),
        out_shape=jax.ShapeDtypeStruct((rows, h), F32),
        compiler_params=_params(1),
    )(jnp.reshape(mine, (1,)).astype(jnp.int32), own, others)


def _sum_slots_adamw(slots, late_slots, w, m, v):
    late_rows = late_slots.shape[1]

    def body(s_ref, l_ref, w_ref, m_ref, v_ref, g_ref, d_ref, m2_ref, v2_ref):
        g, late = s_ref[0], l_ref[0]
        for d in range(1, 8):
            g = g + s_ref[d]
            late = late + l_ref[d]
        g = jnp.concatenate([g[:late_rows] + late, g[late_rows:]], axis=0)
        g_ref[...] = g
        d_ref[...], m2_ref[...], v2_ref[...] = _adam_update(g, w_ref[...], m_ref[...], v_ref[...])

    vm = pl.BlockSpec(memory_space=pltpu.VMEM)
    shape = jax.ShapeDtypeStruct(w.shape, F32)
    return pl.pallas_call(body, name="small_sum_adamw", in_specs=[vm] * 5, out_specs=[vm] * 4,
                          out_shape=[shape] * 4)(slots, late_slots, w, m, v)


def _mesh_pos():
    return lax.axis_index("x"), lax.axis_index("y"), lax.axis_index("c")


def _other_chips(x, y):
    return [(1 - x, y), (x, 1 - y), (1 - x, 1 - y)]


def _half(ref, c, axis):
    n = ref.shape[axis] // 2
    return ref.at[(slice(None),) * axis + (pl.ds(c * n, n),)]


def _remote(src, dst, send_sem, recv_sem, device):
    return pltpu.make_async_remote_copy(src_ref=src, dst_ref=dst, send_sem=send_sem, recv_sem=recv_sem,
                                        device_id=device, device_id_type=MESH)


def _gather_plan(split, whole=(), axes=None):
    split, whole = list(split), list(whole)
    ns, n = len(split), len(split) + len(whole)

    def make(ins, outs, sems):
        ici_send, ici_recv, d2d_send, d2d_recv, own_send, own_recv = sems
        x, y, c = _mesh_pos()
        mine = 2 * x + y
        chips = _other_chips(x, y)
        blocks = [2 * px + py for px, py in chips]

        def own(a):
            return _remote(ins[a], outs[a].at[mine], own_send.at[a], own_recv.at[a], (x, y, 1 - c))

        def ici(a, k, block):
            px, py = chips[k]
            src, dst = ins[a], outs[a].at[block]
            if a < ns:
                src, dst = _half(src, c, axes[a]), _half(dst, c, axes[a])
            return _remote(src, dst, ici_send.at[3 * a + k], ici_recv.at[3 * a + k], (px, py, c))

        def d2d(a, k, half):
            part = _half(outs[a].at[blocks[k]], half, axes[a])
            return _remote(part, part, d2d_send.at[3 * a + k], d2d_recv.at[3 * a + k], (x, y, 1 - c))

        def start():
            for a in range(n):
                for k in range(3):
                    ici(a, k, mine).start()
                own(a).start()

        def relay():
            for a in range(n):
                for k in range(3):
                    ici(a, k, blocks[k]).wait_recv()
                    if a < ns:
                        d2d(a, k, c).start()

        def finish():
            for a in range(ns):
                for k in range(3):
                    d2d(a, k, 1 - c).wait_recv()
            for a in range(n):
                for k in range(3):
                    ici(a, k, mine).wait_send()
                    if a < ns:
                        d2d(a, k, c).wait_send()
                own(a).wait()

        return start, relay, finish

    arrays = split + whole
    axes = [0] * ns if axes is None else list(axes)
    return _Plan(arrays, [jax.ShapeDtypeStruct((N_CHIPS,) + s.shape, s.dtype) for s in arrays],
                 [pltpu.SemaphoreType.DMA((3 * n,)), pltpu.SemaphoreType.DMA((3 * n,)),
                  pltpu.SemaphoreType.DMA((3 * ns,)), pltpu.SemaphoreType.DMA((3 * ns,)),
                  pltpu.SemaphoreType.DMA((n,)), pltpu.SemaphoreType.DMA((n,))], make)


def _to_sibling_plan(gs):
    n = len(gs)

    def make(ins, outs, sems):
        send_sems, recv_sems = sems
        x, y, c = _mesh_pos()

        def copy(a):
            return _remote(_half(ins[a], 1 - c, 2), outs[a], send_sems.at[a], recv_sems.at[a], (x, y, 1 - c))

        def start():
            for a in range(n):
                copy(a).start()

        def finish():
            for a in range(n):
                copy(a).wait()

        return start, finish

    return _Plan(list(gs), [jax.ShapeDtypeStruct(g.shape[:2] + (g.shape[2] // 2,), g.dtype) for g in gs],
                 [pltpu.SemaphoreType.DMA((n,)), pltpu.SemaphoreType.DMA((n,))], make)


def _chip_exchange_plan(ps):
    n = len(ps)

    def make(ins, outs, sems):
        send_sems, recv_sems = sems
        x, y, c = _mesh_pos()
        chips = _other_chips(x, y)

        def ici(a, k):
            px, py = chips[k]
            return _remote(ins[a].at[2 * px + py], outs[a].at[k], send_sems.at[3 * a + k],
                           recv_sems.at[3 * a + k], (px, py, c))

        def start():
            for a in range(n):
                for k in range(3):
                    ici(a, k).start()

        def finish():
            for a in range(n):
                for k in range(3):
                    ici(a, k).wait()

        return start, finish

    return _Plan(list(ps), [jax.ShapeDtypeStruct((3,) + p.shape[1:], p.dtype) for p in ps],
                 [pltpu.SemaphoreType.DMA((3 * n,)), pltpu.SemaphoreType.DMA((3 * n,))], make)


def _share_plan(halves):
    n = len(halves)

    def make(ins, outs, sems):
        send_sems, recv_sems = sems
        x, y, c = _mesh_pos()

        def d2d(a):
            return _remote(ins[a], outs[a], send_sems.at[a], recv_sems.at[a], (x, y, 1 - c))

        def start():
            for a in range(n):
                d2d(a).start()

        def finish():
            for a in range(n):
                d2d(a).wait()

        return start, finish

    return _Plan(list(halves), [jax.ShapeDtypeStruct(p.shape, p.dtype) for p in halves],
                 [pltpu.SemaphoreType.DMA((n,)), pltpu.SemaphoreType.DMA((n,))], make)


def _all_to_all_plan(part):
    def make(ins, outs, sems):
        send_sems, recv_sems, local_sem = sems
        (p_ref,), (slots,) = ins, outs
        x, y, c = _mesh_pos()
        me = 4 * x + 2 * y + c
        peers = [(px, py, pc) for px in (x, 1 - x) for py in (y, 1 - y) for pc in (c, 1 - c)][1:]

        def remote(k, slot):
            return _remote(p_ref, slots.at[slot], send_sems.at[k], recv_sems.at[k], peers[k])

        def local():
            return pltpu.make_async_copy(p_ref, slots.at[me], local_sem)

        def start():
            for k in range(7):
                remote(k, me).start()
            local().start()

        def finish():
            for k, (px, py, pc) in enumerate(peers):
                remote(k, 4 * px + 2 * py + pc).wait_recv()
            for k in range(7):
                remote(k, me).wait_send()
            local().wait()

        return start, finish

    return _Plan([part], [jax.ShapeDtypeStruct((8,) + part.shape, part.dtype)],
                 [pltpu.SemaphoreType.DMA((7,)), pltpu.SemaphoreType.DMA((7,)), pltpu.SemaphoreType.DMA(())], make)


def _merge_plans(a, b):
    na_in, na_out, na_sems = len(a.arrays), len(a.out_shape), len(a.sems)

    def make(ins, outs, sems):
        phases_a = _phases(a.make(ins[:na_in], outs[:na_out], sems[:na_sems]))
        phases_b = _phases(b.make(ins[na_in:], outs[na_out:], sems[na_sems:]))

        def both(i):
            def run():
                phases_a[i]()
                phases_b[i]()
            return run

        return both(0), both(1), both(2)

    return _Plan(list(a.arrays) + list(b.arrays), list(a.out_shape) + list(b.out_shape),
                 list(a.sems) + list(b.sems), make)


def _exchange(plan, name):
    n_in, n_out = len(plan.arrays), len(plan.out_shape)

    def body(*refs):
        for phase in _phases(plan.make(refs[:n_in], refs[n_in:n_in + n_out], refs[n_in + n_out:])):
            phase()

    return pl.pallas_call(
        body, name=name, in_specs=[HBM_SPEC] * n_in, out_specs=[HBM_SPEC] * n_out, out_shape=list(plan.out_shape),
        scratch_shapes=list(plan.sems), compiler_params=pltpu.CompilerParams(has_side_effects=True),
    )(*plan.arrays)


def _pack_small(parts):
    rows = []
    for r in range(SMALL_ROWS):
        pieces, col = [], 0
        for name, row, start, size in SMALL_PARTS:
            if row == r:
                assert start == col
                pieces.append(parts[name].reshape(1, size).astype(F32))
                col += size
        rows.append(jnp.concatenate(pieces + [jnp.zeros((1, D - col), F32)], axis=1))
    return jnp.concatenate(rows, axis=0)


def _unpack_small(slab, shapes):
    return {name: slab[row, col:col + size].reshape(shapes[name]) for name, row, col, size in SMALL_PARTS}


def _columns(gathered):
    return jnp.concatenate([gathered[j] for j in range(N_CHIPS)], axis=1)


def kernel(x, meta_tokens, norm_mix_g, w_in, conv_w, conv_b, conv_ln_g, conv_ln_b, gla_w_gate2, gla_gate_b, gla_norm_g, w_out, norm_ffn_g, w_ffn_gate, w_ffn_up, w_ffn_down, norm_final_g, loss_target, m_meta_tokens, m_norm_mix_g, m_w_in, m_conv_w, m_conv_b, m_conv_ln_g, m_conv_ln_b, m_gla_w_gate2, m_gla_gate_b, m_gla_norm_g, m_w_out, m_norm_ffn_g, m_w_ffn_gate, m_w_ffn_up, m_w_ffn_down, m_norm_final_g, v_meta_tokens, v_norm_mix_g, v_w_in, v_conv_w, v_conv_b, v_conv_ln_g, v_conv_ln_b, v_gla_w_gate2, v_gla_gate_b, v_gla_norm_g, v_w_out, v_norm_ffn_g, v_w_ffn_gate, v_w_ffn_up, v_w_ffn_down, v_norm_final_g):
    ws = dict(zip(WEIGHT_NAMES, (meta_tokens, norm_mix_g, w_in, conv_w, conv_b, conv_ln_g, conv_ln_b, gla_w_gate2,
                                 gla_gate_b, gla_norm_g, w_out, norm_ffn_g, w_ffn_gate, w_ffn_up, w_ffn_down,
                                 norm_final_g)))
    ms = dict(zip(WEIGHT_NAMES, (m_meta_tokens, m_norm_mix_g, m_w_in, m_conv_w, m_conv_b, m_conv_ln_g, m_conv_ln_b,
                                 m_gla_w_gate2, m_gla_gate_b, m_gla_norm_g, m_w_out, m_norm_ffn_g, m_w_ffn_gate,
                                 m_w_ffn_up, m_w_ffn_down, m_norm_final_g)))
    vs = dict(zip(WEIGHT_NAMES, (v_meta_tokens, v_norm_mix_g, v_w_in, v_conv_w, v_conv_b, v_conv_ln_g, v_conv_ln_b,
                                 v_gla_w_gate2, v_gla_gate_b, v_gla_norm_g, v_w_out, v_norm_ffn_g, v_w_ffn_gate,
                                 v_w_ffn_up, v_w_ffn_down, v_norm_final_g)))
    c = lax.axis_index("c")
    mine = 2 * lax.axis_index("x") + lax.axis_index("y")
    shard = lambda d, name: d[name].reshape(d[name].shape[-2:])
    vec = {name: ws[name].reshape(1, -1) for name, _, _, _ in SMALL_PARTS}
    n_ex, seq, _ = x.shape
    lp = HEAD_ROWS + seq
    t = n_ex * lp

    (tgt, h0), (w_in_g, meta_g, conv_w_g, w2_g) = _pad_head_rows([loss_target, x], plan=_gather_plan(
        [shard(ws, "w_in").T.astype(BF16)],
        [shard(ws, "meta_tokens"), shard(ws, "conv_w"), shard(ws, "gla_w_gate2")], axes=[1]))
    w_in_t = jnp.concatenate([w_in_g.reshape(D_IN, D), jnp.zeros((D_IN_PAD - D_IN, D), BF16)], axis=0)
    conv_w_full = jnp.concatenate([_columns(conv_w_g), jnp.zeros((32 - CONV_W, C_CONV), F32)], axis=0)
    w2_full = jnp.concatenate([_columns(w2_g), jnp.zeros((128 - RANK, GLA_K), F32)], axis=0).astype(BF16)
    h0 = _set_meta_rows(h0, _columns(meta_g)).reshape(t, D)
    tgt = tgt.reshape(t, D)
    row_mask = jnp.concatenate([jnp.zeros((n_ex, HEAD_ROWS, 1), F32), jnp.ones((n_ex, seq, 1), F32)],
                               axis=1).reshape(t, 1)

    (u, hn), (gate_g,) = _in_proj(h0, vec["norm_mix_g"], w_in_t.T,
                                  plan=_gather_plan([shard(ws, "w_ffn_gate").T.astype(BF16)]))
    (yc, y_conv), (up_g, w_out_g) = _conv_fwd(
        u, conv_w_full, vec["conv_b"], vec["conv_ln_g"], vec["conv_ln_b"], n_ex, lp,
        plan=_gather_plan([shard(ws, "w_ffn_up").T.astype(BF16), shard(ws, "w_out").astype(BF16)]))
    (y_gla, states), _ = _gla_fwd(u, w2_full, vec["gla_gate_b"], vec["gla_norm_g"], n_ex, lp)
    w_out_full = w_out_g.reshape(D, D)
    w_gate_t, w_up_t = gate_g.reshape(D_FF, D), up_g.reshape(D_FF, D)
    (h1, hn2, gate, up, act), (down_g,) = _mix_out_ffn_up(
        h0, y_conv, y_gla, w_out_full, vec["norm_ffn_g"], w_gate_t.T, w_up_t.T,
        plan=_gather_plan([shard(ws, "w_ffn_down").astype(BF16)]))
    w_down_full = down_g.reshape(D_FF, D)
    dh2, loss, d_final_g = _ffn_down_loss(act, w_down_full, h1, tgt, vec["norm_final_g"], row_mask)
    dgate, dup, dh1, dycat, d_ffn_g = _ffn_bwd(dh2, gate, up, h1, w_down_full.T, w_gate_t, w_up_t, w_out_full.T,
                                                vec["norm_ffn_g"])

    early = ("w_ffn_gate", "w_ffn_up", "w_ffn_down", "w_out")
    ffn_block = lambda g: g.reshape(N_CHIPS, D_FF // N_CHIPS, D)
    g_gate = ffn_block(_wgrad(dgate, hn2, "wgrad_gate")[0])
    g_up, (gate_sib,) = _wgrad(dup, hn2, "wgrad_up", _to_sibling_plan([g_gate]))
    g_up = ffn_block(g_up)
    g_down, (up_sib,) = _wgrad(act, dh2, "wgrad_down", _to_sibling_plan([g_up]))
    g_down = ffn_block(g_down)
    g_out = _wgrad_pair(y_conv, y_gla, dh1, "wgrad_out").reshape(N_CHIPS, D // N_CHIPS, D)
    cs_gate = _rs_add_halves(g_gate, gate_sib, c, "rs_add_w_ffn_gate")
    cs_up = _rs_add_halves(g_up, up_sib, c, "rs_add_w_ffn_up")
    (du_conv, d_conv_w, d_conv_b, d_ln_g, d_ln_b), (ex_gate, ex_up, down_sib, out_sib) = _conv_bwd(
        dycat, yc, u, conv_w_full, vec["conv_ln_g"], vec["conv_ln_b"], n_ex, lp,
        plan=_merge_plans(_chip_exchange_plan([cs_gate, cs_up]), _to_sibling_plan([g_down, g_out])))
    cs_down = _rs_add_halves(g_down, down_sib, c, "rs_add_w_ffn_down")
    cs_out = _rs_add_halves(g_out, out_sib, c, "rs_add_w_out")
    (du_gla, d_w2, d_gate_b, d_norm_g), (ex_down, ex_out) = _gla_bwd(
        dycat, u, states, w2_full, vec["gla_gate_b"], vec["gla_norm_g"], n_ex, lp,
        plan=_chip_exchange_plan([cs_down, cs_out]))
    halves = [_rs_sum(own, oth, mine, "rs_sum_" + nm)
              for own, oth, nm in zip((cs_gate, cs_up, cs_down, cs_out), (ex_gate, ex_up, ex_down, ex_out), early)]

    small = {"norm_mix_g": jnp.zeros((1, D), F32), "norm_ffn_g": d_ffn_g, "norm_final_g": d_final_g,
             "conv_b": d_conv_b, "conv_ln_g": d_ln_g, "conv_ln_b": d_ln_b, "gla_gate_b": d_gate_b,
             "gla_norm_g": d_norm_g}
    part = lax.dynamic_update_slice(_pack_small(small), loss[:, :1], (LOSS_ROW, 0))
    part = jnp.concatenate([part, jnp.zeros((N_META, D), F32), d_conv_w.reshape(16, D), d_w2[:RANK].reshape(4, D),
                            jnp.zeros((4, D), F32)], axis=0)
    g_in_gla, (slots,) = _wgrad(du_gla, hn, "wgrad_in_gla", _all_to_all_plan(part))

    d_w_in_t = jnp.concatenate([_wgrad(du_conv, hn, "wgrad_in_conv")[0], g_in_gla],
                               axis=0)[:D_IN].reshape(N_CHIPS, D_IN // N_CHIPS, D)
    (in_from_sibling,) = _exchange(_to_sibling_plan([d_w_in_t]), "rs_late_to_sibling")
    in_chip_sum = _rs_add_halves(d_w_in_t, in_from_sibling, c, "rs_add_w_in")
    (dh0, d_mix_g), shared = _in_proj_bwd(
        du_conv, du_gla, w_in_t[:2 * C_CONV], w_in_t[2 * C_CONV:], h0, dh1, vec["norm_mix_g"],
        plan=_merge_plans(_share_plan(halves), _chip_exchange_plan([in_chip_sum])))
    dh0 = dh0.reshape(n_ex, lp, D)
    grad_x = dh0[:, HEAD_ROWS:]
    late_part = jnp.concatenate([d_mix_g, jnp.zeros((SMALL_ROWS - 1, D), F32),
                                 jnp.sum(dh0[:, PAD_ROWS:HEAD_ROWS], axis=0)], axis=0)
    in_half = _rs_sum(in_chip_sum, shared[4], mine, "rs_sum_w_in")
    in_shared, late_slots = _exchange(_merge_plans(_share_plan([in_half]), _all_to_all_plan(late_part)),
                                      "late_exchange")

    out = {"grad": {}, "delta": {}, "new_m": {}, "new_v": {}}

    def record(name, res, transposed=False):
        for kind, a in zip(("grad", "delta", "new_m", "new_v"), res):
            out[kind][name] = (a.T if transposed else a).reshape(ws[name].shape)

    def operands(name, transposed):
        lay = (lambda a: a.T) if transposed else (lambda a: a)
        return lay(shard(ws, name)), lay(shard(ms, name)), lay(shard(vs, name))

    early_layout = (("w_ffn_gate", True), ("w_ffn_up", True), ("w_ffn_down", False), ("w_out", False))
    items = [(mine_half, their_half, *operands(name, transposed))
             for (name, transposed), mine_half, their_half in zip(early_layout, halves, shared)]
    for (name, transposed), res in zip(early_layout, _adamw_halves(items, c, "adamw_early")):
        record(name, res, transposed)

    record("w_in", _adamw_halves([(in_half, in_shared, *operands("w_in", True))], c, "adamw_w_in")[0], True)

    tall = lambda a: jnp.concatenate([a, jnp.zeros((part.shape[0] - SMALL_ROWS, D), F32)], axis=0)
    g_s, d_s, m_s, v_s = _sum_slots_adamw(slots, late_slots, tall(_pack_small(ws)), tall(_pack_small(ms)),
                                          tall(_pack_small(vs)))
    small_shapes = {name: ws[name].shape for name, _, _, _ in SMALL_PARTS}
    for kind, slab in (("grad", g_s), ("delta", d_s), ("new_m", m_s), ("new_v", v_s)):
        out[kind].update(_unpack_small(slab, small_shapes))
    loss = g_s[LOSS_ROW, 0]
    block = lambda a, width: lax.dynamic_slice_in_dim(a, mine * width, width, axis=1)
    small_sharded = {"meta_tokens": block(g_s[8:24], D // N_CHIPS),
                     "conv_w": block(g_s[24:40].reshape(32, C_CONV), C_CONV // N_CHIPS)[:CONV_W],
                     "gla_w_gate2": block(g_s[40:44].reshape(RANK, GLA_K), GLA_K // N_CHIPS)}
    for name, g in small_sharded.items():
        record(name, [g, *_adamw(g, *operands(name, False), "adamw_" + name)])

    return (loss, grad_x, *[out[kind][name] for kind in ("grad", "delta", "new_m", "new_v") for name in WEIGHT_NAMES])
```

```python
import functools
from typing import Any, Callable, NamedTuple, Sequence

import jax
import jax.numpy as jnp
from jax import lax
from jax.experimental import pallas as pl
from jax.experimental.pallas import tpu as pltpu

F32 = jnp.float32
BF16 = jnp.bfloat16
MESH = pl.DeviceIdType.MESH

D = 1024
N_META = 16
C_CONV = 512
CONV_W = 31
GLA_K = 256
GLA_V = 512
N_HEADS = 4
DK = 64
DV = 128
RANK = 16
CHUNK = 64
PAD_ROWS = CHUNK - N_META
HEAD_ROWS = CHUNK
D_IN = 2576
D_IN_PAD = 2688
D_GLA_IN = D_IN_PAD - 2 * C_CONV
D_FF = 2816
RMS_EPS = 1e-6
LN_EPS = 1e-5
GATE_TAU = 16.0
N_CHIPS = 4

ADAM_LR = 0.001
ADAM_B1 = 0.9
ADAM_B2 = 0.999
ADAM_EPS = 1e-08
ADAM_WD = 0.01
ADAM_STEP = 10

V7X_VMEM_BYTES = 64 * 1024 * 1024
VMEM_LIMIT = V7X_VMEM_BYTES - 8 * 1024 * 1024
SUBLANES = 8
ROW_PART = 128
FFN_BWD_TILE = 192

WEIGHT_NAMES = ("meta_tokens", "norm_mix_g", "w_in", "conv_w", "conv_b", "conv_ln_g", "conv_ln_b", "gla_w_gate2",
                "gla_gate_b", "gla_norm_g", "w_out", "norm_ffn_g", "w_ffn_gate", "w_ffn_up", "w_ffn_down",
                "norm_final_g")

SMALL_ROWS = 8
SMALL_PARTS = (("norm_mix_g", 0, 0, D), ("norm_ffn_g", 1, 0, D), ("norm_final_g", 2, 0, D),
               ("conv_b", 3, 0, C_CONV), ("conv_ln_g", 3, C_CONV, C_CONV), ("conv_ln_b", 4, 0, C_CONV),
               ("gla_gate_b", 4, C_CONV, GLA_K), ("gla_norm_g", 4, C_CONV + GLA_K, DV))
LOSS_ROW = 5

HBM_SPEC = pl.BlockSpec(memory_space=pltpu.HBM)


def _dot(a, b):
    return jnp.dot(a, b, preferred_element_type=F32)


def _dot_nt(a, b):
    return lax.dot_general(a, b, (((1,), (1,)), ((), ())), preferred_element_type=F32)


def _dot_tn(a, b):
    return lax.dot_general(a, b, (((0,), (0,)), ((), ())), preferred_element_type=F32)


def _sigmoid(x):
    return 1.0 / (1.0 + jnp.exp(-x))


def _const_spec(shape):
    return pl.BlockSpec(shape, lambda *_: (0,) * len(shape), pipeline_mode=pl.Buffered(1))


def _acc_spec(shape):
    return pl.BlockSpec(shape, lambda *_: (0,) * len(shape))


def _params(n_axes):
    return pltpu.CompilerParams(dimension_semantics=("arbitrary",) * n_axes, vmem_limit_bytes=VMEM_LIMIT)


def _row_tile(t, want):
    for r in (want, 384, 192, 128, 64):
        if r <= want and t % r == 0:
            return r
    raise ValueError(f"no row tile for {t}")


def _row_parts(r):
    if r % ROW_PART:
        return [slice(None)]
    return [pl.ds(i * ROW_PART, ROW_PART) for i in range(r // ROW_PART)]


def _in_lockstep(bodies):
    live = list(bodies)
    while live:
        still = []
        for g in live:
            try:
                next(g)
                still.append(g)
            except StopIteration:
                pass
        live = still


class _Plan(NamedTuple):
    arrays: Sequence[Any]
    out_shape: Sequence[Any]
    sems: Sequence[Any]
    make: Callable


def _phases(made):
    return made if len(made) == 3 else (made[0], lambda: None, made[1])


def _call(body, *, name, grid, in_specs, out_specs, out_shape, scratch_shapes=(), plan=None):
    n_in, n_out, n_scr = len(in_specs), len(out_specs), len(scratch_shapes)
    if plan is None:
        plan = _Plan([], [], [], lambda ins, outs, sems: (lambda: None, lambda: None))
    nx_in, nx_out = len(plan.arrays), len(plan.out_shape)
    n_steps = functools.reduce(lambda a, b: a * b, grid)

    def hosted(*refs):
        ins, xins = refs[:n_in], refs[n_in:n_in + nx_in]
        o0 = n_in + nx_in
        outs, xouts = refs[o0:o0 + n_out], refs[o0 + n_out:o0 + n_out + nx_out]
        s0 = o0 + n_out + nx_out
        scr, sems = refs[s0:s0 + n_scr], refs[s0 + n_scr:]
        step = functools.reduce(lambda acc, a: acc * grid[a] + pl.program_id(a), range(len(grid)), 0)
        start, relay, finish = _phases(plan.make(xins, xouts, sems))
        pl.when(step == 0)(start)
        pl.when(step == n_steps - 1)(relay)
        body(*ins, *outs, *scr)
        pl.when(step == n_steps - 1)(finish)

    call = pl.pallas_call(
        hosted, name=name, grid=grid, in_specs=list(in_specs) + [HBM_SPEC] * nx_in,
        out_specs=list(out_specs) + [HBM_SPEC] * nx_out, out_shape=list(out_shape) + list(plan.out_shape),
        scratch_shapes=list(scratch_shapes) + list(plan.sems),
        compiler_params=pltpu.CompilerParams(dimension_semantics=("arbitrary",) * len(grid),
                                             vmem_limit_bytes=VMEM_LIMIT, has_side_effects=nx_in > 0))

    def run(*args):
        res = call(*args, *plan.arrays)
        return res[:n_out], res[n_out:]

    return run


def _pad_head_rows(arrays, plan=None):
    n_ex, seq, _ = arrays[0].shape
    nc = (HEAD_ROWS + seq) // CHUNK
    n = len(arrays)

    def body(*refs):
        for a_ref, o_ref in zip(refs[:n], refs[n:]):
            o_ref[...] = jnp.where(pl.program_id(0) > 0, a_ref[...], 0.0)

    return _call(
        body, name="pad_head_rows", grid=(nc,),
        in_specs=[pl.BlockSpec((n_ex, CHUNK, D), lambda i: (0, jnp.maximum(i - 1, 0), 0))] * n,
        out_specs=[pl.BlockSpec((n_ex, CHUNK, D), lambda i: (0, i, 0))] * n,
        out_shape=[jax.ShapeDtypeStruct((n_ex, HEAD_ROWS + seq, D), F32)] * n,
        plan=plan,
    )(*arrays)


def _set_meta_rows(h0, meta):
    n_ex = h0.shape[0]

    def body(h_ref, meta_ref, o_ref):
        o_ref[...] = jnp.concatenate(
            [h_ref[:, :PAD_ROWS, :], jnp.broadcast_to(meta_ref[...][None], (n_ex, N_META, D))], axis=1)

    head = pl.BlockSpec((n_ex, HEAD_ROWS, D), lambda i: (0, 0, 0))
    return pl.pallas_call(
        body, name="set_meta_rows", grid=(1,), in_specs=[head, pl.BlockSpec((N_META, D), lambda i: (0, 0))],
        out_specs=head, out_shape=jax.ShapeDtypeStruct(h0.shape, F32), input_output_aliases={0: 0},
        compiler_params=_params(1),
    )(h0, meta)


def _in_proj(h0, g_mix, w_in, plan=None):
    t = h0.shape[0]
    r = _row_tile(t, 384)

    def body(h_ref, g_ref, w_ref, u_ref, hn_ref):
        h = h_ref[...]
        rstd = lax.rsqrt(jnp.mean(h * h, axis=-1, keepdims=True) + RMS_EPS)
        hn = (h * rstd * g_ref[...]).astype(BF16)
        hn_ref[...] = hn
        u_ref[...] = _dot(hn, w_ref[...])

    return _call(
        body, name="in_proj", grid=(t // r,),
        in_specs=[pl.BlockSpec((r, D), lambda i: (i, 0)), _const_spec((1, D)), _const_spec((D, D_IN_PAD))],
        out_specs=[pl.BlockSpec((r, D_IN_PAD), lambda i: (i, 0)), pl.BlockSpec((r, D), lambda i: (i, 0))],
        out_shape=[jax.ShapeDtypeStruct((t, D_IN_PAD), F32), jax.ShapeDtypeStruct((t, D), BF16)],
        plan=plan,
    )(h0, g_mix, w_in)


CONV_TILE = 192
CONV_SUB = 32
CONV_LEAD = CONV_SUB - (CONV_W - 1)


def _shifted_copies(src, dst, r):
    for s in range(1, SUBLANES):
        dst[s - 1] = src[s:s + r + CONV_SUB - SUBLANES, :]


def _shifted_rows(src, shifted, start):
    base, s = SUBLANES * (start // SUBLANES), start % SUBLANES
    if s == 0:
        return src[base:base + CONV_SUB, :]
    return shifted[s - 1, base:base + CONV_SUB, :]


def _conv_fwd(u, conv_w, conv_b, ln_g, ln_b, n_ex, lp, plan=None):
    r = CONV_TILE
    nt = lp // r
    hb = r // CONV_SUB

    def body(cur_ref, prev_ref, w_ref, b_ref, lg_ref, lb_ref, yc_ref, y_ref, glu, glu_sh):
        i = pl.program_id(1)
        cur = cur_ref[...]
        glu[CONV_SUB:CONV_SUB + r, :] = cur[:, :C_CONV] * _sigmoid(cur[:, C_CONV:])
        pv = prev_ref[...]
        halo = pv[:, :C_CONV] * _sigmoid(pv[:, C_CONV:])
        glu[0:CONV_SUB, :] = jnp.where(i > 0, halo, 0.0)
        _shifted_copies(glu, glu_sh, r)
        w = w_ref[...]
        for j in range(r // CONV_SUB):
            r0 = j * CONV_SUB
            acc = jnp.zeros((CONV_SUB, C_CONV), F32) + b_ref[...]
            for k in range(CONV_W):
                acc = acc + w[k:k + 1, :] * _shifted_rows(glu, glu_sh, r0 + CONV_LEAD + k)
            mu = jnp.mean(acc, axis=-1, keepdims=True)
            cen = acc - mu
            var = jnp.mean(cen * cen, axis=-1, keepdims=True)
            out = cen * lax.rsqrt(var + LN_EPS) * lg_ref[...] + lb_ref[...]
            y = out * _sigmoid(out)
            row = i * r + r0 + lax.broadcasted_iota(jnp.int32, (CONV_SUB, 1), 0)
            y = jnp.where(row >= PAD_ROWS, y, 0.0)
            yc_ref[r0:r0 + CONV_SUB, :] = acc
            y_ref[r0:r0 + CONV_SUB, :] = y.astype(BF16)

    t = n_ex * lp
    return _call(
        body, name="conv_fwd", grid=(n_ex, nt),
        in_specs=[pl.BlockSpec((r, 2 * C_CONV), lambda b, i: (b * nt + i, 0)),
                  pl.BlockSpec((CONV_SUB, 2 * C_CONV), lambda b, i: (jnp.maximum((b * nt + i) * hb - 1, 0), 0)),
                  _const_spec((32, C_CONV)), _const_spec((1, C_CONV)), _const_spec((1, C_CONV)), _const_spec((1, C_CONV))],
        out_specs=[pl.BlockSpec((r, C_CONV), lambda b, i: (b * nt + i, 0)),
                   pl.BlockSpec((r, C_CONV), lambda b, i: (b * nt + i, 0))],
        out_shape=[jax.ShapeDtypeStruct((t, C_CONV), F32), jax.ShapeDtypeStruct((t, C_CONV), BF16)],
        scratch_shapes=[pltpu.VMEM((r + CONV_SUB, C_CONV), F32),
                        pltpu.VMEM((SUBLANES - 1, r + CONV_SUB - SUBLANES, C_CONV), F32)],
        plan=plan,
    )(u, u, conv_w, conv_b, ln_g, ln_b)


def _mix_out_ffn_up(h0, y_conv, y_gla, w_out, g_ffn, w_gate, w_up, plan=None):
    t = h0.shape[0]
    r = _row_tile(t, 384)

    def body(h0_ref, yc_ref, yg_ref, wo_ref, g_ref, wg_ref, wu_ref, h1_ref, hn_ref, gate_ref, up_ref, act_ref):
        h1 = h0_ref[...] + _dot(yc_ref[...], wo_ref[0:C_CONV, :]) + _dot(yg_ref[...], wo_ref[C_CONV:D, :])
        h1_ref[...] = h1
        rstd = lax.rsqrt(jnp.mean(h1 * h1, axis=-1, keepdims=True) + RMS_EPS)
        hn = (h1 * rstd * g_ref[...]).astype(BF16)
        hn_ref[...] = hn
        gate = _dot(hn, wg_ref[...])
        up = _dot(hn, wu_ref[...])
        gate_ref[...] = gate
        up_ref[...] = up
        act_ref[...] = (gate * _sigmoid(gate) * up).astype(BF16)

    rows = lambda w: pl.BlockSpec((r, w), lambda i: (i, 0))
    return _call(
        body, name="mix_out_ffn_up", grid=(t // r,),
        in_specs=[rows(D), rows(C_CONV), rows(GLA_V), _const_spec((D, D)), _const_spec((1, D)),
                  _const_spec((D, D_FF)), _const_spec((D, D_FF))],
        out_specs=[rows(D), rows(D), rows(D_FF), rows(D_FF), rows(D_FF)],
        out_shape=[jax.ShapeDtypeStruct((t, D), F32), jax.ShapeDtypeStruct((t, D), BF16),
                   jax.ShapeDtypeStruct((t, D_FF), F32), jax.ShapeDtypeStruct((t, D_FF), F32),
                   jax.ShapeDtypeStruct((t, D_FF), BF16)],
        plan=plan,
    )(h0, y_conv, y_gla, w_out, g_ffn, w_gate, w_up)


def _ffn_down_loss(act, w_down, h1, target, g_final, row_mask):
    t = h1.shape[0]
    r = _row_tile(t, 384)

    def body(act_ref, wd_ref, h1_ref, tgt_ref, gf_ref, mask_ref, dh2_ref, loss_ref, dgf_ref):
        @pl.when(pl.program_id(0) == 0)
        def _():
            loss_ref[...] = jnp.zeros_like(loss_ref)
            dgf_ref[...] = jnp.zeros_like(dgf_ref)

        gf = gf_ref[...]

        def part(rows):
            h2 = h1_ref[rows, :] + _dot(act_ref[rows, :], wd_ref[...])
            yield
            rstd = lax.rsqrt(jnp.mean(h2 * h2, axis=-1, keepdims=True) + RMS_EPS)
            nrm = h2 * rstd
            err = (nrm * gf - tgt_ref[rows, :]) * mask_ref[rows, :]
            loss_ref[...] += jnp.sum(err * err) * (0.5 / D)
            dy = err * (1.0 / D)
            dgf_ref[...] += jnp.sum(dy * nrm, axis=0, keepdims=True)
            dn = dy * gf
            dh2_ref[rows, :] = rstd * (dn - nrm * jnp.mean(dn * nrm, axis=-1, keepdims=True))

        _in_lockstep(part(rows) for rows in _row_parts(r))

    rows = lambda w: pl.BlockSpec((r, w), lambda i: (i, 0))
    return pl.pallas_call(
        body, name="ffn_down_loss", grid=(t // r,),
        in_specs=[rows(D_FF), _const_spec((D_FF, D)), rows(D), rows(D), _const_spec((1, D)), rows(1)],
        out_specs=[rows(D), _acc_spec((1, 128)), _acc_spec((1, D))],
        out_shape=[jax.ShapeDtypeStruct((t, D), F32), jax.ShapeDtypeStruct((1, 128), F32),
                   jax.ShapeDtypeStruct((1, D), F32)],
        compiler_params=_params(1),
    )(act, w_down, h1, target, g_final, row_mask)


def _ffn_bwd(dh2, gate, up, h1, w_down_t, w_gate_t, w_up_t, w_out_t, g_ffn):
    t = h1.shape[0]
    r = _row_tile(t, FFN_BWD_TILE)

    def body(dh2_ref, gate_ref, up_ref, h1_ref, wd_ref, wg_ref, wu_ref, wo_ref, g_ref,
             dgate_ref, dup_ref, dh1_ref, dycat_ref, dg_ref):
        @pl.when(pl.program_id(0) == 0)
        def _():
            dg_ref[...] = jnp.zeros_like(dg_ref)

        dh2 = dh2_ref[...]
        dact = _dot(dh2.astype(BF16), wd_ref[...])
        gate = gate_ref[...]
        sg = _sigmoid(gate)
        dgate = (dact * up_ref[...] * (sg * (1.0 + gate * (1.0 - sg)))).astype(BF16)
        dup = (dact * (gate * sg)).astype(BF16)
        dgate_ref[...] = dgate
        dup_ref[...] = dup
        dhn = _dot(dgate, wg_ref[...]) + _dot(dup, wu_ref[...])
        h1 = h1_ref[...]
        rstd = lax.rsqrt(jnp.mean(h1 * h1, axis=-1, keepdims=True) + RMS_EPS)
        nrm = h1 * rstd
        dg_ref[...] += jnp.sum(dhn * nrm, axis=0, keepdims=True)
        dn = dhn * g_ref[...]
        dh1 = dh2 + rstd * (dn - nrm * jnp.mean(dn * nrm, axis=-1, keepdims=True))
        dh1_ref[...] = dh1
        dycat_ref[...] = _dot(dh1.astype(BF16), wo_ref[...])

    rows = lambda w: pl.BlockSpec((r, w), lambda i: (i, 0))
    return pl.pallas_call(
        body, name="ffn_bwd", grid=(t // r,),
        in_specs=[rows(D), rows(D_FF), rows(D_FF), rows(D), _const_spec((D, D_FF)), _const_spec((D_FF, D)),
                  _const_spec((D_FF, D)), _const_spec((D, D)), _const_spec((1, D))],
        out_specs=[rows(D_FF), rows(D_FF), rows(D), rows(D), _acc_spec((1, D))],
        out_shape=[jax.ShapeDtypeStruct((t, D_FF), BF16), jax.ShapeDtypeStruct((t, D_FF), BF16),
                   jax.ShapeDtypeStruct((t, D), F32), jax.ShapeDtypeStruct((t, D), F32),
                   jax.ShapeDtypeStruct((1, D), F32)],
        compiler_params=_params(1),
    )(dh2, gate, up, h1, w_down_t, w_gate_t, w_up_t, w_out_t, g_ffn)


def _conv_bwd(dycat, yc, u, conv_w, ln_g, ln_b, n_ex, lp, plan=None):
    r = CONV_TILE
    nt = lp // r
    hb = r // CONV_SUB
    nsub = r // CONV_SUB

    def ln_bwd(dy, yc_rows, live, lg, lb):
        mu = jnp.mean(yc_rows, axis=-1, keepdims=True)
        cen = yc_rows - mu
        rs = lax.rsqrt(jnp.mean(cen * cen, axis=-1, keepdims=True) + LN_EPS)
        yn = cen * rs
        out = yn * lg + lb
        so = _sigmoid(out)
        dout = jnp.where(live, dy * (so * (1.0 + out * (1.0 - so))), 0.0)
        dyn = dout * lg
        dyc = rs * (dyn - jnp.mean(dyn, axis=-1, keepdims=True) - yn * jnp.mean(dyn * yn, axis=-1, keepdims=True))
        return dyc, dout, yn

    def body(dy_ref, dyn_ref, yc_ref, ycn_ref, cur_ref, prev_ref, w_ref, lg_ref, lb_ref,
             du_ref, dw_ref, db_ref, dlg_ref, dlb_ref, glu, dycs, dwacc, glu_sh, dycs_sh):
        b = pl.program_id(0)
        i = pl.program_id(1)
        first = jnp.logical_and(b == 0, i == 0)

        @pl.when(first)
        def _():
            dwacc[...] = jnp.zeros_like(dwacc)
            db_ref[...] = jnp.zeros_like(db_ref)
            dlg_ref[...] = jnp.zeros_like(dlg_ref)
            dlb_ref[...] = jnp.zeros_like(dlb_ref)

        lg, lb = lg_ref[...], lb_ref[...]
        cur = cur_ref[...]
        sig = _sigmoid(cur[:, C_CONV:])
        glu[CONV_SUB:CONV_SUB + r, :] = cur[:, :C_CONV] * sig
        pv = prev_ref[...]
        glu[0:CONV_SUB, :] = jnp.where(i > 0, pv[:, :C_CONV] * _sigmoid(pv[:, C_CONV:]), 0.0)

        row = i * r + lax.broadcasted_iota(jnp.int32, (r, 1), 0)
        dyc, dout, yn = ln_bwd(dy_ref[...], yc_ref[...], row >= PAD_ROWS, lg, lb)
        dycs[0:r, :] = dyc
        dycn, _, _ = ln_bwd(dyn_ref[...], ycn_ref[...], i < nt - 1, lg, lb)
        dycs[r:r + CONV_SUB, :] = dycn
        db_ref[...] += jnp.sum(dyc, axis=0, keepdims=True)
        dlg_ref[...] += jnp.sum(dout * yn, axis=0, keepdims=True)
        dlb_ref[...] += jnp.sum(dout, axis=0, keepdims=True)

        _shifted_copies(glu, glu_sh, r)
        _shifted_copies(dycs, dycs_sh, r)
        w = w_ref[...]
        for j in range(nsub):
            r0 = j * CONV_SUB
            dblk = dycs[r0:r0 + CONV_SUB, :]
            dglu = jnp.zeros((CONV_SUB, C_CONV), F32)
            for k in range(CONV_W):
                dglu = dglu + w[k:k + 1, :] * _shifted_rows(dycs, dycs_sh, r0 + (CONV_W - 1) - k)
                prod = dblk * _shifted_rows(glu, glu_sh, r0 + CONV_LEAD + k)
                dwacc[k] += prod.reshape(CONV_SUB // SUBLANES, SUBLANES, C_CONV).sum(axis=0)
            sg = sig[r0:r0 + CONV_SUB, :]
            cv = cur[r0:r0 + CONV_SUB, :C_CONV]
            du_ref[r0:r0 + CONV_SUB, :C_CONV] = (dglu * sg).astype(BF16)
            du_ref[r0:r0 + CONV_SUB, C_CONV:] = (dglu * cv * sg * (1.0 - sg)).astype(BF16)

        @pl.when(jnp.logical_and(b == n_ex - 1, i == nt - 1))
        def _():
            dw_ref[...] = jnp.sum(dwacc[...], axis=1)

    t = n_ex * lp
    cur_rows = lambda w, col: pl.BlockSpec((r, w), lambda b, i: (b * nt + i, col))
    nxt_rows = lambda w, col: pl.BlockSpec(
        (CONV_SUB, w), lambda b, i: (jnp.minimum((b * nt + i + 1) * hb, n_ex * nt * hb - 1), col))
    return _call(
        body, name="conv_bwd", grid=(n_ex, nt),
        in_specs=[cur_rows(C_CONV, 0), nxt_rows(C_CONV, 0), cur_rows(C_CONV, 0), nxt_rows(C_CONV, 0),
                  cur_rows(2 * C_CONV, 0),
                  pl.BlockSpec((CONV_SUB, 2 * C_CONV), lambda b, i: (jnp.maximum((b * nt + i) * hb - 1, 0), 0)),
                  _const_spec((32, C_CONV)), _const_spec((1, C_CONV)), _const_spec((1, C_CONV))],
        out_specs=[cur_rows(2 * C_CONV, 0), _acc_spec((32, C_CONV)), _acc_spec((1, C_CONV)),
                   _acc_spec((1, C_CONV)), _acc_spec((1, C_CONV))],
        out_shape=[jax.ShapeDtypeStruct((t, 2 * C_CONV), BF16), jax.ShapeDtypeStruct((32, C_CONV), F32),
                   jax.ShapeDtypeStruct((1, C_CONV), F32), jax.ShapeDtypeStruct((1, C_CONV), F32),
                   jax.ShapeDtypeStruct((1, C_CONV), F32)],
        scratch_shapes=[pltpu.VMEM((r + CONV_SUB, C_CONV), F32), pltpu.VMEM((r + CONV_SUB, C_CONV), F32),
                        pltpu.VMEM((32, SUBLANES, C_CONV), F32),
                        pltpu.VMEM((SUBLANES - 1, r + CONV_SUB - SUBLANES, C_CONV), F32),
                        pltpu.VMEM((SUBLANES - 1, r + CONV_SUB - SUBLANES, C_CONV), F32)],
        plan=plan,
    )(dycat, dycat, yc, yc, u, u, conv_w, ln_g, ln_b)


HEAD_ROWS_ALL = N_HEADS * CHUNK


def _gla_gates(lr, w2, gb, first_chunk):
    z = _dot(lr.astype(BF16), w2) + gb
    a = (jnp.minimum(z, 0.0) - jnp.log(1.0 + jnp.exp(-jnp.abs(z)))) * (1.0 / GATE_TAU)
    row = lax.broadcasted_iota(jnp.int32, (CHUNK, 1), 0)
    live = jnp.logical_or(jnp.logical_not(first_chunk), row >= PAD_ROWS)
    return z, jnp.where(live, a, 0.0), live


def _tri(lower):
    i = lax.broadcasted_iota(jnp.int32, (CHUNK, CHUNK), 0)
    j = lax.broadcasted_iota(jnp.int32, (CHUNK, CHUNK), 1)
    return (i >= j) if lower else (i <= j)


def _head_of(shape, axis, per_head):
    return lax.broadcasted_iota(jnp.int32, shape, axis) // per_head


def _expand(x, lanes_per_head):
    rows, lanes = HEAD_ROWS_ALL, x.shape[1]
    keep = _head_of((rows, lanes), 0, CHUNK) == _head_of((rows, lanes), 1, lanes_per_head)
    return jnp.where(keep, jnp.tile(x, (N_HEADS, 1)), 0.0)


def _expand_lanes(x):
    rows, w = x.shape
    keep = _head_of((rows, N_HEADS * w), 0, CHUNK) == _head_of((rows, N_HEADS * w), 1, w)
    return jnp.where(keep, jnp.tile(x, (1, N_HEADS)), 0.0)


def _expand_state(st):
    rows, lanes = N_HEADS * DV, st.shape[1]
    keep = _head_of((rows, lanes), 0, DV) == _head_of((rows, lanes), 1, DK)
    return jnp.where(keep, jnp.tile(st, (N_HEADS, 1)), 0.0)


def _fold(t, rows_per_head):
    lane_head = _head_of((rows_per_head, t.shape[1]), 1, DK)
    out = jnp.where(lane_head == 0, t[0:rows_per_head], 0.0)
    for h in range(1, N_HEADS):
        out = out + jnp.where(lane_head == h, t[h * rows_per_head:(h + 1) * rows_per_head], 0.0)
    return out


def _rows_by_head(x):
    return jnp.concatenate([x[:, h * DV:(h + 1) * DV] for h in range(N_HEADS)], axis=0)


def _lanes_by_head(x):
    return jnp.concatenate([x[h * CHUNK:(h + 1) * CHUNK] for h in range(N_HEADS)], axis=1)


def _running_sum(a, lower):
    hi = a.astype(BF16)
    rest = a - hi.astype(F32)
    mid = rest.astype(BF16)
    lo = (rest - mid.astype(F32)).astype(BF16)
    w = a.shape[1]
    parts = _dot(_tri(lower).astype(F32).astype(BF16), jnp.concatenate([hi, mid, lo], axis=1))
    return parts[:, :w] + parts[:, w:2 * w] + parts[:, 2 * w:]


def _stacked_causal():
    i = lax.broadcasted_iota(jnp.int32, (HEAD_ROWS_ALL, CHUNK), 0) % CHUNK
    j = lax.broadcasted_iota(jnp.int32, (HEAD_ROWS_ALL, CHUNK), 1)
    return i >= j


GLA_GROUP = 3


def _gla_chunk(q, k, v, lr, w2, gb, first_chunk):
    z, a, live = _gla_gates(lr, w2, gb, first_chunk)
    yield
    b = _running_sum(a, True)
    yield
    bl = b[CHUNK - 1:CHUNK, :]
    e_pos, e_neg, e_dec = jnp.exp(b), jnp.exp(-b), jnp.exp(bl - b)
    q_f, k_f, kd_f = q * (DK ** -0.5) * e_pos, k * e_neg, k * e_dec
    qx = _expand(q_f, DK).astype(BF16)
    k_in, k_dec, v_b = k_f.astype(BF16), kd_f.astype(BF16), v.astype(BF16)
    s = jnp.where(_stacked_causal(), _dot_nt(qx, k_in), 0.0).astype(BF16)
    yield
    p = _dot(s, v_b)
    yield
    o_intra = jnp.concatenate([p[h * CHUNK:(h + 1) * CHUNK, h * DV:(h + 1) * DV] for h in range(N_HEADS)], axis=0)
    return dict(z=z, live=live, bl=bl, e_pos=e_pos, e_neg=e_neg, e_dec=e_dec, q_f=q_f, k_f=k_f, kd_f=kd_f,
                qx=qx, k_in=k_in, k_dec=k_dec, v_b=v_b, s=s, o_intra=o_intra, decay=jnp.exp(bl))


def _gla_fwd(u, w2, gb, ng, n_ex, lp, plan=None):
    nc = lp // CHUNK
    t = n_ex * lp
    rows_of = lambda j: pl.ds(j * CHUNK, CHUNK)

    def body(qk_ref, v_ref, g_ref, lr_ref, w2_ref, gb_ref, ng_ref, y_ref, st_ref, state):
        n = pl.program_id(0)

        @pl.when(n == 0)
        def _():
            state[...] = jnp.zeros_like(state)

        carried = [state[e] for e in range(n_ex)]

        def one_chunk(e, j):
            rows = rows_of(j)
            qk = qk_ref[e, rows, :]
            first = jnp.logical_and(n == 0, j == 0)
            c = yield from _gla_chunk(qk[:, :GLA_K], qk[:, GLA_K:], v_ref[e, rows, :], lr_ref[e, rows, :],
                                      w2_ref[...], gb_ref[...], first)
            kv = _fold(_dot_tn(c["v_b"], c["k_dec"]), DV)
            g = _rows_by_head(g_ref[e, rows, :])
            gate = ng_ref[...] * (g * _sigmoid(g))
            yield
            for _ in range(j):
                yield
            st = carried[e]
            st_ref[e, pl.ds(j * DV, DV), :] = st
            o = c["o_intra"] + _dot_nt(c["qx"], st.astype(BF16))
            rstd = lax.rsqrt(jnp.mean(o * o, axis=-1, keepdims=True) + RMS_EPS)
            y_ref[e, rows, :] = _lanes_by_head(o * rstd * gate).astype(BF16)
            carried[e] = c["decay"] * st + kv

        _in_lockstep(one_chunk(e, j) for j in range(GLA_GROUP) for e in range(n_ex))
        for e in range(n_ex):
            state[e] = carried[e]

    u3 = u.reshape(n_ex, lp, D_IN_PAD)
    blk = lambda w, col: pl.BlockSpec((n_ex, GLA_GROUP * CHUNK, w), lambda n: (0, n, col))
    (y, states), extra = _call(
        body, name="gla_fwd", grid=(nc // GLA_GROUP,),
        in_specs=[blk(2 * GLA_K, 2), blk(GLA_V, 3), blk(GLA_V, 4), blk(128, 20),
                  _const_spec((128, GLA_K)), _const_spec((1, GLA_K)), _const_spec((1, DV))],
        out_specs=[blk(GLA_V, 0), pl.BlockSpec((n_ex, GLA_GROUP * DV, GLA_K), lambda n: (0, n, 0))],
        out_shape=[jax.ShapeDtypeStruct((n_ex, lp, GLA_V), BF16),
                   jax.ShapeDtypeStruct((n_ex, nc * DV, GLA_K), F32)],
        scratch_shapes=[pltpu.VMEM((n_ex, DV, GLA_K), F32)],
        plan=plan,
    )(u3, u3, u3, u3, w2, gb, ng)
    return (y.reshape(t, GLA_V), states), extra


def _gla_bwd(dycat, u, states, w2, gb, ng, n_ex, lp, plan=None):
    nc = lp // CHUNK
    t = n_ex * lp

    def body(dy_ref, qk_ref, v_ref, g_ref, lr_ref, st_ref, w2_ref, gb_ref, ng_ref,
             du_ref, dw2_ref, dgb_ref, dng_ref, dstate):
        n = pl.program_id(0)
        group = nc // GLA_GROUP - 1 - n

        @pl.when(n == 0)
        def _():
            dw2_ref[...] = jnp.zeros_like(dw2_ref)
            dgb_ref[...] = jnp.zeros_like(dgb_ref)
            dng_ref[...] = jnp.zeros_like(dng_ref)
            dstate[...] = jnp.zeros_like(dstate)

        carried = [dstate[e] for e in range(n_ex)]

        def one_chunk(e, order):
            j = GLA_GROUP - 1 - order
            rows = pl.ds(j * CHUNK, CHUNK)
            qk = qk_ref[e, rows, :]
            lr = lr_ref[e, rows, :]
            st = st_ref[e, pl.ds(j * DV, DV), :]
            first = jnp.logical_and(group == 0, j == 0)
            c = yield from _gla_chunk(qk[:, :GLA_K], qk[:, GLA_K:], v_ref[e, rows, :], lr, w2_ref[...], gb_ref[...],
                                      first)
            qx, k_in, k_dec, v_b, s = c["qx"], c["k_in"], c["k_dec"], c["v_b"], c["s"]
            st_b = st.astype(BF16)
            o = c["o_intra"] + _dot_nt(qx, st_b)
            ngv = ng_ref[...]
            yield
            rstd = lax.rsqrt(jnp.mean(o * o, axis=-1, keepdims=True) + RMS_EPS)
            nrm = o * rstd
            g = _rows_by_head(g_ref[e, rows, :])
            dy = _rows_by_head(dy_ref[e, rows, :])
            sg = _sigmoid(g)
            dg = dy * nrm * ngv * (sg * (1.0 + g * (1.0 - sg)))
            dt = dy * (g * sg)
            dng_ref[...] += jnp.sum(dt * nrm, axis=0, keepdims=True)
            dn = dt * ngv
            do = rstd * (dn - nrm * jnp.mean(dn * nrm, axis=-1, keepdims=True))
            do_b = do.astype(BF16)
            dox = _expand_lanes(do).astype(BF16)
            yield
            da = jnp.where(_stacked_causal(), _dot_nt(dox, v_b), 0.0).astype(BF16)
            dv_intra = _dot_tn(s, dox)
            dst_own = _dot_tn(do_b, qx)
            yield
            dq_in = _fold(_dot(da, k_in) + _dot(do_b, st_b), CHUNK)
            dk_in = _dot_tn(da, qx)
            dq = dq_in * (DK ** -0.5) * c["e_pos"]
            yield
            for _ in range(order):
                yield
            dst = carried[e]
            dstx = _expand_state(dst).astype(BF16)
            dv = dv_intra + _dot_nt(k_dec, dstx)
            dk_dec = _dot(v_b, dstx)
            carried[e] = dst_own + c["decay"] * dst
            yield
            dbl = (jnp.sum(dk_dec * c["kd_f"], axis=0, keepdims=True)
                   + c["decay"] * jnp.sum(dst * st, axis=0, keepdims=True))
            dk = dk_in * c["e_neg"] + dk_dec * c["e_dec"]
            db = dq_in * c["q_f"] - dk_in * c["k_f"] - dk_dec * c["kd_f"]
            row = lax.broadcasted_iota(jnp.int32, (CHUNK, 1), 0)
            da_log = _running_sum(db + jnp.where(row == CHUNK - 1, dbl, 0.0), False)
            yield
            dz = jnp.where(c["live"], da_log * (1.0 - _sigmoid(c["z"])) * (1.0 / GATE_TAU), 0.0)
            dz_b = dz.astype(BF16)
            out = du_ref.at[e, rows, :]
            out[:, 0:GLA_K] = dq.astype(BF16)
            out[:, GLA_K:2 * GLA_K] = dk.astype(BF16)
            out[:, 2 * GLA_K:2 * GLA_K + GLA_V] = dv.astype(BF16)
            out[:, 2 * GLA_K + GLA_V:2 * GLA_K + 2 * GLA_V] = _lanes_by_head(dg).astype(BF16)
            out[:, 2 * GLA_K + 2 * GLA_V:] = _dot_nt(dz_b, w2_ref[...]).astype(BF16)
            dw2_ref[...] += _dot_tn(lr.astype(BF16), dz_b)
            dgb_ref[...] += jnp.sum(dz, axis=0, keepdims=True)

        _in_lockstep(one_chunk(e, order) for order in range(GLA_GROUP) for e in range(n_ex))
        for e in range(n_ex):
            dstate[e] = carried[e]

    u3 = u.reshape(n_ex, lp, D_IN_PAD)
    rev = lambda w, col: pl.BlockSpec((n_ex, GLA_GROUP * CHUNK, w), lambda n: (0, nc // GLA_GROUP - 1 - n, col))
    (du, d_w2, d_gb, d_ng), extra = _call(
        body, name="gla_bwd", grid=(nc // GLA_GROUP,),
        in_specs=[rev(GLA_V, 1), rev(2 * GLA_K, 2), rev(GLA_V, 3), rev(GLA_V, 4), rev(128, 20),
                  pl.BlockSpec((n_ex, GLA_GROUP * DV, GLA_K), lambda n: (0, nc // GLA_GROUP - 1 - n, 0)),
                  _const_spec((128, GLA_K)), _const_spec((1, GLA_K)), _const_spec((1, DV))],
        out_specs=[rev(D_GLA_IN, 0), _acc_spec((128, GLA_K)), _acc_spec((1, GLA_K)), _acc_spec((1, DV))],
        out_shape=[jax.ShapeDtypeStruct((n_ex, lp, D_GLA_IN), BF16), jax.ShapeDtypeStruct((128, GLA_K), F32),
                   jax.ShapeDtypeStruct((1, GLA_K), F32), jax.ShapeDtypeStruct((1, DV), F32)],
        scratch_shapes=[pltpu.VMEM((n_ex, DV, GLA_K), F32)],
        plan=plan,
    )(dycat.reshape(n_ex, lp, D), u3, u3, u3, u3, states, w2, gb, ng)
    return (du.reshape(t, D_GLA_IN), d_w2, d_gb, d_ng), extra


def _in_proj_bwd(du_conv, du_gla, w_in_t_conv, w_in_t_gla, h0, dh1, g_mix, plan=None):
    t = h0.shape[0]
    r = _row_tile(t, 384)

    def body(dc_ref, dg_ref, wc_ref, wg_ref, h_ref, dh1_ref, g_ref, dh0_ref, dgm_ref):
        @pl.when(pl.program_id(0) == 0)
        def _():
            dgm_ref[...] = jnp.zeros_like(dgm_ref)

        dhn = _dot(dc_ref[...], wc_ref[...]) + _dot(dg_ref[...], wg_ref[...])
        h = h_ref[...]
        rstd = lax.rsqrt(jnp.mean(h * h, axis=-1, keepdims=True) + RMS_EPS)
        nrm = h * rstd
        dgm_ref[...] += jnp.sum(dhn * nrm, axis=0, keepdims=True)
        dn = dhn * g_ref[...]
        dh0_ref[...] = dh1_ref[...] + rstd * (dn - nrm * jnp.mean(dn * nrm, axis=-1, keepdims=True))

    rows = lambda w: pl.BlockSpec((r, w), lambda i: (i, 0))
    return _call(
        body, name="in_proj_bwd", grid=(t // r,),
        in_specs=[rows(2 * C_CONV), rows(D_GLA_IN), _const_spec((2 * C_CONV, D)), _const_spec((D_GLA_IN, D)),
                  rows(D), rows(D), _const_spec((1, D))],
        out_specs=[rows(D), _acc_spec((1, D))],
        out_shape=[jax.ShapeDtypeStruct((t, D), F32), jax.ShapeDtypeStruct((1, D), F32)],
        plan=plan,
    )(du_conv, du_gla, w_in_t_conv, w_in_t_gla, h0, dh1, g_mix)


def _wgrad(x, dy, name, plan=None):
    t, m = x.shape
    n = dy.shape[1]
    tk = t // 3 if t % (3 * 128) == 0 else _row_tile(t, 384)
    tm = m if m <= D_GLA_IN else m // 2

    def body(x_ref, dy_ref, o_ref):
        @pl.when(pl.program_id(1) == 0)
        def _():
            o_ref[...] = jnp.zeros_like(o_ref)

        o_ref[...] += _dot_tn(x_ref[...].astype(BF16), dy_ref[...].astype(BF16))

    (out,), extra = _call(
        body, name=name, grid=(m // tm, t // tk),
        in_specs=[pl.BlockSpec((tk, tm), lambda i, k: (k, i)), pl.BlockSpec((tk, n), lambda i, k: (k, 0))],
        out_specs=[pl.BlockSpec((tm, n), lambda i, k: (i, 0))],
        out_shape=[jax.ShapeDtypeStruct((m, n), F32)],
        plan=plan,
    )(x, dy)
    return out, extra


def _wgrad_pair(xa, xb, dy, name):
    t, m = xa.shape
    n = dy.shape[1]
    tk = t // 3 if t % (3 * 128) == 0 else _row_tile(t, 384)

    def body(xa_ref, xb_ref, dy_ref, o_ref):
        @pl.when(pl.program_id(1) == 0)
        def _():
            o_ref[...] = jnp.zeros_like(o_ref)

        x = jnp.where(pl.program_id(0) == 0, xa_ref[...], xb_ref[...])
        o_ref[...] += _dot_tn(x.astype(BF16), dy_ref[...].astype(BF16))

    rows = lambda w: pl.BlockSpec((tk, w), lambda i, k: (k, 0))
    return pl.pallas_call(
        body, name=name, grid=(2, t // tk), in_specs=[rows(m), rows(m), rows(n)],
        out_specs=pl.BlockSpec((m, n), lambda i, k: (i, 0)),
        out_shape=jax.ShapeDtypeStruct((2 * m, n), F32), compiler_params=_params(2),
    )(xa, xb, dy)


def _adam_update(g, w, m, v):
    m2 = ADAM_B1 * m + (1.0 - ADAM_B1) * g
    v2 = ADAM_B2 * v + (1.0 - ADAM_B2) * (g * g)
    m_hat = m2 / (1.0 - ADAM_B1 ** ADAM_STEP)
    v_hat = v2 / (1.0 - ADAM_B2 ** ADAM_STEP)
    delta = -ADAM_LR * (m_hat / (jnp.sqrt(v_hat) + ADAM_EPS) + ADAM_WD * w)
    return delta, m2, v2


ADAMW_STEPS = 4


def _adamw(g, w, m, v, name):
    rows, cols = g.shape
    steps = ADAMW_STEPS if rows % (ADAMW_STEPS * SUBLANES) == 0 else 1

    def body(g_ref, w_ref, m_ref, v_ref, d_ref, m2_ref, v2_ref):
        d_ref[...], m2_ref[...], v2_ref[...] = _adam_update(g_ref[...], w_ref[...], m_ref[...], v_ref[...])

    spec = pl.BlockSpec((rows // steps, cols), lambda i: (i, 0))
    return pl.pallas_call(
        body, name=name, grid=(steps,), in_specs=[spec] * 4, out_specs=[spec] * 3,
        out_shape=[jax.ShapeDtypeStruct(g.shape, F32)] * 3, compiler_params=_params(1),
    )(g, w, m, v)


def _adamw_halves(items, c, name):
    n = len(items)
    h = items[0][0].shape[1]
    steps = ADAMW_STEPS if all(it[0].shape[0] % (ADAMW_STEPS * SUBLANES) == 0 for it in items) else 1

    def body(c_ref, *refs):
        ins, outs = refs[:5 * n], refs[5 * n:]
        own = pl.program_id(1) == c_ref[0]
        for i in range(n):
            a_ref, b_ref, w_ref, m_ref, v_ref = ins[5 * i:5 * i + 5]
            go_ref, d_ref, m2_ref, v2_ref = outs[4 * i:4 * i + 4]
            g = jnp.where(own, a_ref[...], b_ref[...])
            go_ref[...] = g
            d_ref[...], m2_ref[...], v2_ref[...] = _adam_update(g, w_ref[...], m_ref[...], v_ref[...])

    in_specs, out_specs, out_shape, args = [pl.BlockSpec(memory_space=pltpu.SMEM)], [], [], []
    for mine, theirs, w, m, v in items:
        tr = mine.shape[0] // steps
        half = pl.BlockSpec((tr, h), lambda i, j: (i, 0))
        full = pl.BlockSpec((tr, h), lambda i, j: (i, j))
        in_specs += [half, half, full, full, full]
        out_specs += [full] * 4
        out_shape += [jax.ShapeDtypeStruct(w.shape, F32)] * 4
        args += [mine, theirs, w, m, v]
    res = pl.pallas_call(
        body, name=name, grid=(steps, 2), in_specs=in_specs, out_specs=out_specs, out_shape=out_shape,
        compiler_params=_params(2),
    )(jnp.reshape(c, (1,)).astype(jnp.int32), *args)
    return [res[4 * i:4 * i + 4] for i in range(n)]


def _rs_add_halves(g, recv, c, name):
    _, rows, w = g.shape
    h = w // 2

    def body(c_ref, a_ref, b_ref, o_ref):
        o_ref[...] = (a_ref[...] + b_ref[...]).astype(BF16)

    return pl.pallas_call(
        body, name=name,
        grid_spec=pltpu.PrefetchScalarGridSpec(
            num_scalar_prefetch=1, grid=(N_CHIPS,),
            in_specs=[pl.BlockSpec((1, rows, h), lambda j, s: (j, 0, s[0])),
                      pl.BlockSpec((1, rows, h), lambda j, s: (j, 0, 0))],
            out_specs=pl.BlockSpec((1, rows, h), lambda j, s: (j, 0, 0))),
        out_shape=jax.ShapeDtypeStruct((N_CHIPS, rows, h), BF16),
        compiler_params=_params(1),
    )(jnp.reshape(c, (1,)).astype(jnp.int32), g, recv)


def _rs_sum(own, others, mine, name):
    _, rows, h = own.shape
    tr = rows // 2 if rows % 16 == 0 and rows > 64 else rows

    def body(mine_ref, own_ref, oth_ref, o_ref):
        p = oth_ref[...].astype(F32)
        o_ref[...] = ((own_ref[0].astype(F32) + p[0]) + p[1]) + p[2]

    return pl.pallas_call(
        body, name=name,
        grid_spec=pltpu.PrefetchScalarGridSpec(
            num_scalar_prefetch=1, grid=(rows // tr,),
            in_specs=[pl.BlockSpec((1, tr, h), lambda i, s: (s[0], i, 0)),
                      pl.BlockSpec((3, tr, h), lambda i, s: (0, i, 0))],
            out_specs=pl.BlockSpec((tr, h), lambda i, s: (i, 0))),
        out_shape=jax.ShapeDtypeStruct((rows, h), F32),
        compiler_params=_params(1),
    )(jnp.reshape(mine, (1,)).astype(jnp.int32), own, others)


def _sum_slots_adamw(slots, late_slots, vectors):
    late_rows = late_slots.shape[1]
    n = len(SMALL_PARTS)

    def body(s_ref, l_ref, *refs):
        ins, g_ref, outs = refs[:3 * n], refs[3 * n], refs[3 * n + 1:]
        g, late = s_ref[0], l_ref[0]
        for d in range(1, 8):
            g = g + s_ref[d]
            late = late + l_ref[d]
        g = jnp.concatenate([g[:late_rows] + late, g[late_rows:]], axis=0)
        g_ref[...] = g
        for i, (_, row, col, size) in enumerate(SMALL_PARTS):
            w_ref, m_ref, v_ref = ins[3 * i:3 * i + 3]
            go_ref, d_ref, m2_ref, v2_ref = outs[4 * i:4 * i + 4]
            piece = g[row:row + 1, col:col + size]
            go_ref[...] = piece
            d_ref[...], m2_ref[...], v2_ref[...] = _adam_update(piece, w_ref[...], m_ref[...], v_ref[...])

    vm = pl.BlockSpec(memory_space=pltpu.VMEM)
    out_shape = [jax.ShapeDtypeStruct(slots.shape[1:], F32)]
    for _, _, _, size in SMALL_PARTS:
        out_shape += [jax.ShapeDtypeStruct((1, size), F32)] * 4
    res = pl.pallas_call(body, name="small_sum_adamw", in_specs=[vm] * (2 + 3 * n), out_specs=[vm] * len(out_shape),
                         out_shape=out_shape)(slots, late_slots, *[a for wmv in vectors for a in wmv])
    return res[0], [res[1 + 4 * i:5 + 4 * i] for i in range(n)]


def _mesh_pos():
    return lax.axis_index("x"), lax.axis_index("y"), lax.axis_index("c")


def _other_chips(x, y):
    return [(1 - x, y), (x, 1 - y), (1 - x, 1 - y)]


def _half(ref, c, axis):
    n = ref.shape[axis] // 2
    return ref.at[(slice(None),) * axis + (pl.ds(c * n, n),)]


def _remote(src, dst, send_sem, recv_sem, device):
    return pltpu.make_async_remote_copy(src_ref=src, dst_ref=dst, send_sem=send_sem, recv_sem=recv_sem,
                                        device_id=device, device_id_type=MESH)


def _gather_plan(split, whole=(), axes=None):
    split, whole = list(split), list(whole)
    ns, n = len(split), len(split) + len(whole)

    def make(ins, outs, sems):
        ici_send, ici_recv, d2d_send, d2d_recv, own_send, own_recv = sems
        x, y, c = _mesh_pos()
        mine = 2 * x + y
        chips = _other_chips(x, y)
        blocks = [2 * px + py for px, py in chips]

        def own(a):
            return _remote(ins[a], outs[a].at[mine], own_send.at[a], own_recv.at[a], (x, y, 1 - c))

        def ici(a, k, block):
            px, py = chips[k]
            src, dst = ins[a], outs[a].at[block]
            if a < ns:
                src, dst = _half(src, c, axes[a]), _half(dst, c, axes[a])
            return _remote(src, dst, ici_send.at[3 * a + k], ici_recv.at[3 * a + k], (px, py, c))

        def d2d(a, k, half):
            part = _half(outs[a].at[blocks[k]], half, axes[a])
            return _remote(part, part, d2d_send.at[3 * a + k], d2d_recv.at[3 * a + k], (x, y, 1 - c))

        def start():
            for a in range(n):
                for k in range(3):
                    ici(a, k, mine).start()
                own(a).start()

        def relay():
            for a in range(n):
                for k in range(3):
                    ici(a, k, blocks[k]).wait_recv()
                    if a < ns:
                        d2d(a, k, c).start()

        def finish():
            for a in range(ns):
                for k in range(3):
                    d2d(a, k, 1 - c).wait_recv()
            for a in range(n):
                for k in range(3):
                    ici(a, k, mine).wait_send()
                    if a < ns:
                        d2d(a, k, c).wait_send()
                own(a).wait()

        return start, relay, finish

    arrays = split + whole
    axes = [0] * ns if axes is None else list(axes)
    return _Plan(arrays, [jax.ShapeDtypeStruct((N_CHIPS,) + s.shape, s.dtype) for s in arrays],
                 [pltpu.SemaphoreType.DMA((3 * n,)), pltpu.SemaphoreType.DMA((3 * n,)),
                  pltpu.SemaphoreType.DMA((3 * ns,)), pltpu.SemaphoreType.DMA((3 * ns,)),
                  pltpu.SemaphoreType.DMA((n,)), pltpu.SemaphoreType.DMA((n,))], make)


def _to_sibling_plan(gs):
    n = len(gs)

    def make(ins, outs, sems):
        send_sems, recv_sems = sems
        x, y, c = _mesh_pos()

        def copy(a):
            return _remote(_half(ins[a], 1 - c, 2), outs[a], send_sems.at[a], recv_sems.at[a], (x, y, 1 - c))

        def start():
            for a in range(n):
                copy(a).start()

        def finish():
            for a in range(n):
                copy(a).wait()

        return start, finish

    return _Plan(list(gs), [jax.ShapeDtypeStruct(g.shape[:2] + (g.shape[2] // 2,), g.dtype) for g in gs],
                 [pltpu.SemaphoreType.DMA((n,)), pltpu.SemaphoreType.DMA((n,))], make)


def _chip_exchange_plan(ps):
    n = len(ps)

    def make(ins, outs, sems):
        send_sems, recv_sems = sems
        x, y, c = _mesh_pos()
        chips = _other_chips(x, y)

        def ici(a, k):
            px, py = chips[k]
            return _remote(ins[a].at[2 * px + py], outs[a].at[k], send_sems.at[3 * a + k],
                           recv_sems.at[3 * a + k], (px, py, c))

        def start():
            for a in range(n):
                for k in range(3):
                    ici(a, k).start()

        def finish():
            for a in range(n):
                for k in range(3):
                    ici(a, k).wait()

        return start, finish

    return _Plan(list(ps), [jax.ShapeDtypeStruct((3,) + p.shape[1:], p.dtype) for p in ps],
                 [pltpu.SemaphoreType.DMA((3 * n,)), pltpu.SemaphoreType.DMA((3 * n,))], make)


def _share_plan(halves):
    n = len(halves)

    def make(ins, outs, sems):
        send_sems, recv_sems = sems
        x, y, c = _mesh_pos()

        def d2d(a):
            return _remote(ins[a], outs[a], send_sems.at[a], recv_sems.at[a], (x, y, 1 - c))

        def start():
            for a in range(n):
                d2d(a).start()

        def finish():
            for a in range(n):
                d2d(a).wait()

        return start, finish

    return _Plan(list(halves), [jax.ShapeDtypeStruct(p.shape, p.dtype) for p in halves],
                 [pltpu.SemaphoreType.DMA((n,)), pltpu.SemaphoreType.DMA((n,))], make)


def _all_to_all_plan(part):
    def make(ins, outs, sems):
        send_sems, recv_sems, local_sem = sems
        (p_ref,), (slots,) = ins, outs
        x, y, c = _mesh_pos()
        me = 4 * x + 2 * y + c
        peers = [(px, py, pc) for px in (x, 1 - x) for py in (y, 1 - y) for pc in (c, 1 - c)][1:]

        def remote(k, slot):
            return _remote(p_ref, slots.at[slot], send_sems.at[k], recv_sems.at[k], peers[k])

        def local():
            return pltpu.make_async_copy(p_ref, slots.at[me], local_sem)

        def start():
            for k in range(7):
                remote(k, me).start()
            local().start()

        def finish():
            for k, (px, py, pc) in enumerate(peers):
                remote(k, 4 * px + 2 * py + pc).wait_recv()
            for k in range(7):
                remote(k, me).wait_send()
            local().wait()

        return start, finish

    return _Plan([part], [jax.ShapeDtypeStruct((8,) + part.shape, part.dtype)],
                 [pltpu.SemaphoreType.DMA((7,)), pltpu.SemaphoreType.DMA((7,)), pltpu.SemaphoreType.DMA(())], make)


def _merge_plans(a, b):
    na_in, na_out, na_sems = len(a.arrays), len(a.out_shape), len(a.sems)

    def make(ins, outs, sems):
        phases_a = _phases(a.make(ins[:na_in], outs[:na_out], sems[:na_sems]))
        phases_b = _phases(b.make(ins[na_in:], outs[na_out:], sems[na_sems:]))

        def both(i):
            def run():
                phases_a[i]()
                phases_b[i]()
            return run

        return both(0), both(1), both(2)

    return _Plan(list(a.arrays) + list(b.arrays), list(a.out_shape) + list(b.out_shape),
                 list(a.sems) + list(b.sems), make)


def _exchange(plan, name):
    n_in, n_out = len(plan.arrays), len(plan.out_shape)

    def body(*refs):
        for phase in _phases(plan.make(refs[:n_in], refs[n_in:n_in + n_out], refs[n_in + n_out:])):
            phase()

    return pl.pallas_call(
        body, name=name, in_specs=[HBM_SPEC] * n_in, out_specs=[HBM_SPEC] * n_out, out_shape=list(plan.out_shape),
        scratch_shapes=list(plan.sems), compiler_params=pltpu.CompilerParams(has_side_effects=True),
    )(*plan.arrays)


def _pack_small(parts):
    rows = []
    for r in range(SMALL_ROWS):
        pieces, col = [], 0
        for name, row, start, size in SMALL_PARTS:
            if row == r:
                assert start == col
                pieces.append(parts[name].reshape(1, size).astype(F32))
                col += size
        rows.append(jnp.concatenate(pieces + [jnp.zeros((1, D - col), F32)], axis=1))
    return jnp.concatenate(rows, axis=0)


def _columns(gathered):
    return jnp.concatenate([gathered[j] for j in range(N_CHIPS)], axis=1)


def kernel(x, meta_tokens, norm_mix_g, w_in, conv_w, conv_b, conv_ln_g, conv_ln_b, gla_w_gate2, gla_gate_b, gla_norm_g, w_out, norm_ffn_g, w_ffn_gate, w_ffn_up, w_ffn_down, norm_final_g, loss_target, m_meta_tokens, m_norm_mix_g, m_w_in, m_conv_w, m_conv_b, m_conv_ln_g, m_conv_ln_b, m_gla_w_gate2, m_gla_gate_b, m_gla_norm_g, m_w_out, m_norm_ffn_g, m_w_ffn_gate, m_w_ffn_up, m_w_ffn_down, m_norm_final_g, v_meta_tokens, v_norm_mix_g, v_w_in, v_conv_w, v_conv_b, v_conv_ln_g, v_conv_ln_b, v_gla_w_gate2, v_gla_gate_b, v_gla_norm_g, v_w_out, v_norm_ffn_g, v_w_ffn_gate, v_w_ffn_up, v_w_ffn_down, v_norm_final_g):
    ws = dict(zip(WEIGHT_NAMES, (meta_tokens, norm_mix_g, w_in, conv_w, conv_b, conv_ln_g, conv_ln_b, gla_w_gate2,
                                 gla_gate_b, gla_norm_g, w_out, norm_ffn_g, w_ffn_gate, w_ffn_up, w_ffn_down,
                                 norm_final_g)))
    ms = dict(zip(WEIGHT_NAMES, (m_meta_tokens, m_norm_mix_g, m_w_in, m_conv_w, m_conv_b, m_conv_ln_g, m_conv_ln_b,
                                 m_gla_w_gate2, m_gla_gate_b, m_gla_norm_g, m_w_out, m_norm_ffn_g, m_w_ffn_gate,
                                 m_w_ffn_up, m_w_ffn_down, m_norm_final_g)))
    vs = dict(zip(WEIGHT_NAMES, (v_meta_tokens, v_norm_mix_g, v_w_in, v_conv_w, v_conv_b, v_conv_ln_g, v_conv_ln_b,
                                 v_gla_w_gate2, v_gla_gate_b, v_gla_norm_g, v_w_out, v_norm_ffn_g, v_w_ffn_gate,
                                 v_w_ffn_up, v_w_ffn_down, v_norm_final_g)))
    c = lax.axis_index("c")
    mine = 2 * lax.axis_index("x") + lax.axis_index("y")
    shard = lambda d, name: d[name].reshape(d[name].shape[-2:])
    vec = {name: ws[name].reshape(1, -1) for name, _, _, _ in SMALL_PARTS}
    n_ex, seq, _ = x.shape
    lp = HEAD_ROWS + seq
    t = n_ex * lp

    (tgt, h0), (w_in_g, meta_g, conv_w_g, w2_g) = _pad_head_rows([loss_target, x], plan=_gather_plan(
        [shard(ws, "w_in").T.astype(BF16)],
        [shard(ws, "meta_tokens"), shard(ws, "conv_w"), shard(ws, "gla_w_gate2")], axes=[1]))
    w_in_t = jnp.concatenate([w_in_g.reshape(D_IN, D), jnp.zeros((D_IN_PAD - D_IN, D), BF16)], axis=0)
    conv_w_full = jnp.concatenate([_columns(conv_w_g), jnp.zeros((32 - CONV_W, C_CONV), F32)], axis=0)
    w2_full = jnp.concatenate([_columns(w2_g), jnp.zeros((128 - RANK, GLA_K), F32)], axis=0).astype(BF16)
    h0 = _set_meta_rows(h0, _columns(meta_g)).reshape(t, D)
    tgt = tgt.reshape(t, D)
    row_mask = jnp.concatenate([jnp.zeros((n_ex, HEAD_ROWS, 1), F32), jnp.ones((n_ex, seq, 1), F32)],
                               axis=1).reshape(t, 1)

    (u, hn), (gate_g,) = _in_proj(h0, vec["norm_mix_g"], w_in_t.T,
                                  plan=_gather_plan([shard(ws, "w_ffn_gate").T.astype(BF16)]))
    (yc, y_conv), (up_g, w_out_g) = _conv_fwd(
        u, conv_w_full, vec["conv_b"], vec["conv_ln_g"], vec["conv_ln_b"], n_ex, lp,
        plan=_gather_plan([shard(ws, "w_ffn_up").T.astype(BF16), shard(ws, "w_out").astype(BF16)]))
    (y_gla, states), _ = _gla_fwd(u, w2_full, vec["gla_gate_b"], vec["gla_norm_g"], n_ex, lp)
    w_out_full = w_out_g.reshape(D, D)
    w_gate_t, w_up_t = gate_g.reshape(D_FF, D), up_g.reshape(D_FF, D)
    (h1, hn2, gate, up, act), (down_g,) = _mix_out_ffn_up(
        h0, y_conv, y_gla, w_out_full, vec["norm_ffn_g"], w_gate_t.T, w_up_t.T,
        plan=_gather_plan([shard(ws, "w_ffn_down").astype(BF16)]))
    w_down_full = down_g.reshape(D_FF, D)
    dh2, loss, d_final_g = _ffn_down_loss(act, w_down_full, h1, tgt, vec["norm_final_g"], row_mask)
    dgate, dup, dh1, dycat, d_ffn_g = _ffn_bwd(dh2, gate, up, h1, w_down_full.T, w_gate_t, w_up_t, w_out_full.T,
                                                vec["norm_ffn_g"])

    early = ("w_ffn_gate", "w_ffn_up", "w_ffn_down", "w_out")
    ffn_block = lambda g: g.reshape(N_CHIPS, D_FF // N_CHIPS, D)
    g_gate = ffn_block(_wgrad(dgate, hn2, "wgrad_gate")[0])
    g_up, (gate_sib,) = _wgrad(dup, hn2, "wgrad_up", _to_sibling_plan([g_gate]))
    g_up = ffn_block(g_up)
    g_down, (up_sib,) = _wgrad(act, dh2, "wgrad_down", _to_sibling_plan([g_up]))
    g_down = ffn_block(g_down)
    g_out = _wgrad_pair(y_conv, y_gla, dh1, "wgrad_out").reshape(N_CHIPS, D // N_CHIPS, D)
    cs_gate = _rs_add_halves(g_gate, gate_sib, c, "rs_add_w_ffn_gate")
    cs_up = _rs_add_halves(g_up, up_sib, c, "rs_add_w_ffn_up")
    (du_conv, d_conv_w, d_conv_b, d_ln_g, d_ln_b), (ex_gate, ex_up, down_sib, out_sib) = _conv_bwd(
        dycat, yc, u, conv_w_full, vec["conv_ln_g"], vec["conv_ln_b"], n_ex, lp,
        plan=_merge_plans(_chip_exchange_plan([cs_gate, cs_up]), _to_sibling_plan([g_down, g_out])))
    cs_down = _rs_add_halves(g_down, down_sib, c, "rs_add_w_ffn_down")
    cs_out = _rs_add_halves(g_out, out_sib, c, "rs_add_w_out")
    (du_gla, d_w2, d_gate_b, d_norm_g), (ex_down, ex_out) = _gla_bwd(
        dycat, u, states, w2_full, vec["gla_gate_b"], vec["gla_norm_g"], n_ex, lp,
        plan=_chip_exchange_plan([cs_down, cs_out]))
    halves = [_rs_sum(own, oth, mine, "rs_sum_" + nm)
              for own, oth, nm in zip((cs_gate, cs_up, cs_down, cs_out), (ex_gate, ex_up, ex_down, ex_out), early)]

    small = {"norm_mix_g": jnp.zeros((1, D), F32), "norm_ffn_g": d_ffn_g, "norm_final_g": d_final_g,
             "conv_b": d_conv_b, "conv_ln_g": d_ln_g, "conv_ln_b": d_ln_b, "gla_gate_b": d_gate_b,
             "gla_norm_g": d_norm_g}
    part = lax.dynamic_update_slice(_pack_small(small), loss[:, :1], (LOSS_ROW, 0))
    part = jnp.concatenate([part, jnp.zeros((N_META, D), F32), d_conv_w.reshape(16, D), d_w2[:RANK].reshape(4, D),
                            jnp.zeros((4, D), F32)], axis=0)
    g_in_gla, (slots,) = _wgrad(du_gla, hn, "wgrad_in_gla", _all_to_all_plan(part))

    d_w_in_t = jnp.concatenate([_wgrad(du_conv, hn, "wgrad_in_conv")[0], g_in_gla],
                               axis=0)[:D_IN].reshape(N_CHIPS, D_IN // N_CHIPS, D)
    (in_from_sibling,) = _exchange(_to_sibling_plan([d_w_in_t]), "rs_late_to_sibling")
    in_chip_sum = _rs_add_halves(d_w_in_t, in_from_sibling, c, "rs_add_w_in")
    (dh0, d_mix_g), shared = _in_proj_bwd(
        du_conv, du_gla, w_in_t[:2 * C_CONV], w_in_t[2 * C_CONV:], h0, dh1, vec["norm_mix_g"],
        plan=_merge_plans(_share_plan(halves), _chip_exchange_plan([in_chip_sum])))
    dh0 = dh0.reshape(n_ex, lp, D)
    grad_x = dh0[:, HEAD_ROWS:]
    late_part = jnp.concatenate([d_mix_g, jnp.zeros((SMALL_ROWS - 1, D), F32),
                                 jnp.sum(dh0[:, PAD_ROWS:HEAD_ROWS], axis=0)], axis=0)
    in_half = _rs_sum(in_chip_sum, shared[4], mine, "rs_sum_w_in")
    in_shared, late_slots = _exchange(_merge_plans(_share_plan([in_half]), _all_to_all_plan(late_part)),
                                      "late_exchange")

    out = {"grad": {}, "delta": {}, "new_m": {}, "new_v": {}}

    def record(name, res, transposed=False):
        for kind, a in zip(("grad", "delta", "new_m", "new_v"), res):
            out[kind][name] = (a.T if transposed else a).reshape(ws[name].shape)

    def operands(name, transposed):
        lay = (lambda a: a.T) if transposed else (lambda a: a)
        return lay(shard(ws, name)), lay(shard(ms, name)), lay(shard(vs, name))

    early_layout = (("w_ffn_gate", True), ("w_ffn_up", True), ("w_ffn_down", False), ("w_out", False))
    items = [(mine_half, their_half, *operands(name, transposed))
             for (name, transposed), mine_half, their_half in zip(early_layout, halves, shared)]
    for (name, transposed), res in zip(early_layout, _adamw_halves(items, c, "adamw_early")):
        record(name, res, transposed)

    record("w_in", _adamw_halves([(in_half, in_shared, *operands("w_in", True))], c, "adamw_w_in")[0], True)

    flat = lambda d, name: d[name].reshape(1, -1)
    g_s, updated = _sum_slots_adamw(slots, late_slots,
                                    [(flat(ws, name), flat(ms, name), flat(vs, name)) for name, _, _, _ in SMALL_PARTS])
    for (name, _, _, _), res in zip(SMALL_PARTS, updated):
        record(name, res)
    loss = g_s[LOSS_ROW, 0]
    block = lambda a, width: lax.dynamic_slice_in_dim(a, mine * width, width, axis=1)
    small_sharded = {"meta_tokens": block(g_s[8:24], D // N_CHIPS),
                     "conv_w": block(g_s[24:40].reshape(32, C_CONV), C_CONV // N_CHIPS)[:CONV_W],
                     "gla_w_gate2": block(g_s[40:44].reshape(RANK, GLA_K), GLA_K // N_CHIPS)}
    for name, g in small_sharded.items():
        record(name, [g, *_adamw(g, *operands(name, False), "adamw_" + name)])

    return (loss, grad_x, *[out[kind][name] for kind in ("grad", "delta", "new_m", "new_v") for name in WEIGHT_NAMES])
```

```python
import functools
from typing import Any, Callable, NamedTuple, Sequence

import jax
import jax.numpy as jnp
from jax import lax
from jax.experimental import pallas as pl
from jax.experimental.pallas import tpu as pltpu

F32 = jnp.float32
BF16 = jnp.bfloat16
MESH = pl.DeviceIdType.MESH

D = 1024
N_META = 16
C_CONV = 512
CONV_W = 31
GLA_K = 256
GLA_V = 512
N_HEADS = 4
DK = 64
DV = 128
RANK = 16
CHUNK = 64
PAD_ROWS = CHUNK - N_META
HEAD_ROWS = CHUNK
D_IN = 2576
D_IN_PAD = 2688
D_GLA_IN = D_IN_PAD - 2 * C_CONV
D_FF = 2816
RMS_EPS = 1e-6
LN_EPS = 1e-5
GATE_TAU = 16.0
N_CHIPS = 4

ADAM_LR = 0.001
ADAM_B1 = 0.9
ADAM_B2 = 0.999
ADAM_EPS = 1e-08
ADAM_WD = 0.01
ADAM_STEP = 10

V7X_VMEM_BYTES = 64 * 1024 * 1024
VMEM_LIMIT = V7X_VMEM_BYTES - 8 * 1024 * 1024
SUBLANES = 8
ROW_PART = 128
FFN_BWD_TILE = 192

WEIGHT_NAMES = ("meta_tokens", "norm_mix_g", "w_in", "conv_w", "conv_b", "conv_ln_g", "conv_ln_b", "gla_w_gate2",
                "gla_gate_b", "gla_norm_g", "w_out", "norm_ffn_g", "w_ffn_gate", "w_ffn_up", "w_ffn_down",
                "norm_final_g")

SMALL_ROWS = 8
SMALL_PARTS = (("norm_mix_g", 0, 0, D), ("norm_ffn_g", 1, 0, D), ("norm_final_g", 2, 0, D),
               ("conv_b", 3, 0, C_CONV), ("conv_ln_g", 3, C_CONV, C_CONV), ("conv_ln_b", 4, 0, C_CONV),
               ("gla_gate_b", 4, C_CONV, GLA_K), ("gla_norm_g", 4, C_CONV + GLA_K, DV))
LOSS_ROW = 5

HBM_SPEC = pl.BlockSpec(memory_space=pltpu.HBM)


def _dot(a, b):
    return jnp.dot(a, b, preferred_element_type=F32)


def _dot_nt(a, b):
    return lax.dot_general(a, b, (((1,), (1,)), ((), ())), preferred_element_type=F32)


def _dot_tn(a, b):
    return lax.dot_general(a, b, (((0,), (0,)), ((), ())), preferred_element_type=F32)


def _sigmoid(x):
    return 1.0 / (1.0 + jnp.exp(-x))


def _const_spec(shape):
    return pl.BlockSpec(shape, lambda *_: (0,) * len(shape), pipeline_mode=pl.Buffered(1))


def _acc_spec(shape):
    return pl.BlockSpec(shape, lambda *_: (0,) * len(shape))


def _params(n_axes):
    return pltpu.CompilerParams(dimension_semantics=("arbitrary",) * n_axes, vmem_limit_bytes=VMEM_LIMIT)


def _row_tile(t, want):
    for r in (want, 384, 192, 128, 64):
        if r <= want and t % r == 0:
            return r
    raise ValueError(f"no row tile for {t}")


def _row_parts(r):
    if r % ROW_PART:
        return [slice(None)]
    return [pl.ds(i * ROW_PART, ROW_PART) for i in range(r // ROW_PART)]


def _in_lockstep(bodies):
    live = list(bodies)
    while live:
        still = []
        for g in live:
            try:
                next(g)
                still.append(g)
            except StopIteration:
                pass
        live = still


class _Plan(NamedTuple):
    arrays: Sequence[Any]
    out_shape: Sequence[Any]
    sems: Sequence[Any]
    make: Callable


def _phases(made):
    return made if len(made) == 3 else (made[0], lambda: None, made[1])


def _call(body, *, name, grid, in_specs, out_specs, out_shape, scratch_shapes=(), plan=None):
    n_in, n_out, n_scr = len(in_specs), len(out_specs), len(scratch_shapes)
    if plan is None:
        plan = _Plan([], [], [], lambda ins, outs, sems: (lambda: None, lambda: None))
    nx_in, nx_out = len(plan.arrays), len(plan.out_shape)
    n_steps = functools.reduce(lambda a, b: a * b, grid)

    def hosted(*refs):
        ins, xins = refs[:n_in], refs[n_in:n_in + nx_in]
        o0 = n_in + nx_in
        outs, xouts = refs[o0:o0 + n_out], refs[o0 + n_out:o0 + n_out + nx_out]
        s0 = o0 + n_out + nx_out
        scr, sems = refs[s0:s0 + n_scr], refs[s0 + n_scr:]
        step = functools.reduce(lambda acc, a: acc * grid[a] + pl.program_id(a), range(len(grid)), 0)
        start, relay, finish = _phases(plan.make(xins, xouts, sems))
        pl.when(step == 0)(start)
        pl.when(step == n_steps - 1)(relay)
        body(*ins, *outs, *scr)
        pl.when(step == n_steps - 1)(finish)

    call = pl.pallas_call(
        hosted, name=name, grid=grid, in_specs=list(in_specs) + [HBM_SPEC] * nx_in,
        out_specs=list(out_specs) + [HBM_SPEC] * nx_out, out_shape=list(out_shape) + list(plan.out_shape),
        scratch_shapes=list(scratch_shapes) + list(plan.sems),
        compiler_params=pltpu.CompilerParams(dimension_semantics=("arbitrary",) * len(grid),
                                             vmem_limit_bytes=VMEM_LIMIT, has_side_effects=nx_in > 0))

    def run(*args):
        res = call(*args, *plan.arrays)
        return res[:n_out], res[n_out:]

    return run


def _pad_head_rows(arrays, plan=None):
    n_ex, seq, _ = arrays[0].shape
    nc = (HEAD_ROWS + seq) // CHUNK
    n = len(arrays)

    def body(*refs):
        for a_ref, o_ref in zip(refs[:n], refs[n:]):
            o_ref[...] = jnp.where(pl.program_id(0) > 0, a_ref[...], 0.0)

    return _call(
        body, name="pad_head_rows", grid=(nc,),
        in_specs=[pl.BlockSpec((n_ex, CHUNK, D), lambda i: (0, jnp.maximum(i - 1, 0), 0))] * n,
        out_specs=[pl.BlockSpec((n_ex, CHUNK, D), lambda i: (0, i, 0))] * n,
        out_shape=[jax.ShapeDtypeStruct((n_ex, HEAD_ROWS + seq, D), F32)] * n,
        plan=plan,
    )(*arrays)


def _set_meta_rows(h0, meta):
    n_ex = h0.shape[0]

    def body(h_ref, meta_ref, o_ref):
        o_ref[...] = jnp.concatenate(
            [h_ref[:, :PAD_ROWS, :], jnp.broadcast_to(meta_ref[...][None], (n_ex, N_META, D))], axis=1)

    head = pl.BlockSpec((n_ex, HEAD_ROWS, D), lambda i: (0, 0, 0))
    return pl.pallas_call(
        body, name="set_meta_rows", grid=(1,), in_specs=[head, pl.BlockSpec((N_META, D), lambda i: (0, 0))],
        out_specs=head, out_shape=jax.ShapeDtypeStruct(h0.shape, F32), input_output_aliases={0: 0},
        compiler_params=_params(1),
    )(h0, meta)


def _in_proj(h0, g_mix, w_in, plan=None):
    t = h0.shape[0]
    r = _row_tile(t, 384)

    def body(h_ref, g_ref, w_ref, u_ref, hn_ref):
        h = h_ref[...]
        rstd = lax.rsqrt(jnp.mean(h * h, axis=-1, keepdims=True) + RMS_EPS)
        hn = (h * rstd * g_ref[...]).astype(BF16)
        hn_ref[...] = hn
        u_ref[...] = _dot(hn, w_ref[...])

    return _call(
        body, name="in_proj", grid=(t // r,),
        in_specs=[pl.BlockSpec((r, D), lambda i: (i, 0)), _const_spec((1, D)), _const_spec((D, D_IN_PAD))],
        out_specs=[pl.BlockSpec((r, D_IN_PAD), lambda i: (i, 0)), pl.BlockSpec((r, D), lambda i: (i, 0))],
        out_shape=[jax.ShapeDtypeStruct((t, D_IN_PAD), F32), jax.ShapeDtypeStruct((t, D), BF16)],
        plan=plan,
    )(h0, g_mix, w_in)


CONV_TILE = 192
CONV_SUB = 32
CONV_LEAD = CONV_SUB - (CONV_W - 1)


def _shifted_copies(src, dst, r):
    for s in range(1, SUBLANES):
        dst[s - 1] = src[s:s + r + CONV_SUB - SUBLANES, :]


def _shifted_rows(src, shifted, start):
    base, s = SUBLANES * (start // SUBLANES), start % SUBLANES
    if s == 0:
        return src[base:base + CONV_SUB, :]
    return shifted[s - 1, base:base + CONV_SUB, :]


def _conv_fwd(u, conv_w, conv_b, ln_g, ln_b, n_ex, lp, plan=None):
    r = CONV_TILE
    nt = lp // r
    hb = r // CONV_SUB

    def body(cur_ref, prev_ref, w_ref, b_ref, lg_ref, lb_ref, yc_ref, y_ref, glu, glu_sh):
        i = pl.program_id(1)
        cur = cur_ref[...]
        glu[CONV_SUB:CONV_SUB + r, :] = cur[:, :C_CONV] * _sigmoid(cur[:, C_CONV:])
        pv = prev_ref[...]
        halo = pv[:, :C_CONV] * _sigmoid(pv[:, C_CONV:])
        glu[0:CONV_SUB, :] = jnp.where(i > 0, halo, 0.0)
        _shifted_copies(glu, glu_sh, r)
        w = w_ref[...]
        for j in range(r // CONV_SUB):
            r0 = j * CONV_SUB
            acc = jnp.zeros((CONV_SUB, C_CONV), F32) + b_ref[...]
            for k in range(CONV_W):
                acc = acc + w[k:k + 1, :] * _shifted_rows(glu, glu_sh, r0 + CONV_LEAD + k)
            mu = jnp.mean(acc, axis=-1, keepdims=True)
            cen = acc - mu
            var = jnp.mean(cen * cen, axis=-1, keepdims=True)
            out = cen * lax.rsqrt(var + LN_EPS) * lg_ref[...] + lb_ref[...]
            y = out * _sigmoid(out)
            row = i * r + r0 + lax.broadcasted_iota(jnp.int32, (CONV_SUB, 1), 0)
            y = jnp.where(row >= PAD_ROWS, y, 0.0)
            yc_ref[r0:r0 + CONV_SUB, :] = acc
            y_ref[r0:r0 + CONV_SUB, :] = y.astype(BF16)

    t = n_ex * lp
    return _call(
        body, name="conv_fwd", grid=(n_ex, nt),
        in_specs=[pl.BlockSpec((r, 2 * C_CONV), lambda b, i: (b * nt + i, 0)),
                  pl.BlockSpec((CONV_SUB, 2 * C_CONV), lambda b, i: (jnp.maximum((b * nt + i) * hb - 1, 0), 0)),
                  _const_spec((32, C_CONV)), _const_spec((1, C_CONV)), _const_spec((1, C_CONV)), _const_spec((1, C_CONV))],
        out_specs=[pl.BlockSpec((r, C_CONV), lambda b, i: (b * nt + i, 0)),
                   pl.BlockSpec((r, C_CONV), lambda b, i: (b * nt + i, 0))],
        out_shape=[jax.ShapeDtypeStruct((t, C_CONV), F32), jax.ShapeDtypeStruct((t, C_CONV), BF16)],
        scratch_shapes=[pltpu.VMEM((r + CONV_SUB, C_CONV), F32),
                        pltpu.VMEM((SUBLANES - 1, r + CONV_SUB - SUBLANES, C_CONV), F32)],
        plan=plan,
    )(u, u, conv_w, conv_b, ln_g, ln_b)


def _mix_out_ffn_up(h0, y_conv, y_gla, w_out, g_ffn, w_gate, w_up, plan=None):
    t = h0.shape[0]
    r = _row_tile(t, 384)

    def body(h0_ref, yc_ref, yg_ref, wo_ref, g_ref, wg_ref, wu_ref, h1_ref, hn_ref, gate_ref, up_ref, act_ref):
        h1 = h0_ref[...] + _dot(yc_ref[...], wo_ref[0:C_CONV, :]) + _dot(yg_ref[...], wo_ref[C_CONV:D, :])
        h1_ref[...] = h1
        rstd = lax.rsqrt(jnp.mean(h1 * h1, axis=-1, keepdims=True) + RMS_EPS)
        hn = (h1 * rstd * g_ref[...]).astype(BF16)
        hn_ref[...] = hn
        gate = _dot(hn, wg_ref[...])
        up = _dot(hn, wu_ref[...])
        gate_ref[...] = gate
        up_ref[...] = up
        act_ref[...] = (gate * _sigmoid(gate) * up).astype(BF16)

    rows = lambda w: pl.BlockSpec((r, w), lambda i: (i, 0))
    return _call(
        body, name="mix_out_ffn_up", grid=(t // r,),
        in_specs=[rows(D), rows(C_CONV), rows(GLA_V), _const_spec((D, D)), _const_spec((1, D)),
                  _const_spec((D, D_FF)), _const_spec((D, D_FF))],
        out_specs=[rows(D), rows(D), rows(D_FF), rows(D_FF), rows(D_FF)],
        out_shape=[jax.ShapeDtypeStruct((t, D), F32), jax.ShapeDtypeStruct((t, D), BF16),
                   jax.ShapeDtypeStruct((t, D_FF), F32), jax.ShapeDtypeStruct((t, D_FF), F32),
                   jax.ShapeDtypeStruct((t, D_FF), BF16)],
        plan=plan,
    )(h0, y_conv, y_gla, w_out, g_ffn, w_gate, w_up)


def _ffn_down_loss(act, w_down, h1, target, g_final, row_mask):
    t = h1.shape[0]
    r = _row_tile(t, 384)

    def body(act_ref, wd_ref, h1_ref, tgt_ref, gf_ref, mask_ref, dh2_ref, loss_ref, dgf_ref):
        @pl.when(pl.program_id(0) == 0)
        def _():
            loss_ref[...] = jnp.zeros_like(loss_ref)
            dgf_ref[...] = jnp.zeros_like(dgf_ref)

        gf = gf_ref[...]

        def part(rows):
            h2 = h1_ref[rows, :] + _dot(act_ref[rows, :], wd_ref[...])
            yield
            rstd = lax.rsqrt(jnp.mean(h2 * h2, axis=-1, keepdims=True) + RMS_EPS)
            nrm = h2 * rstd
            err = (nrm * gf - tgt_ref[rows, :]) * mask_ref[rows, :]
            loss_ref[...] += jnp.sum(err * err) * (0.5 / D)
            dy = err * (1.0 / D)
            dgf_ref[...] += jnp.sum(dy * nrm, axis=0, keepdims=True)
            dn = dy * gf
            dh2_ref[rows, :] = rstd * (dn - nrm * jnp.mean(dn * nrm, axis=-1, keepdims=True))

        _in_lockstep(part(rows) for rows in _row_parts(r))

    rows = lambda w: pl.BlockSpec((r, w), lambda i: (i, 0))
    return pl.pallas_call(
        body, name="ffn_down_loss", grid=(t // r,),
        in_specs=[rows(D_FF), _const_spec((D_FF, D)), rows(D), rows(D), _const_spec((1, D)), rows(1)],
        out_specs=[rows(D), _acc_spec((1, 128)), _acc_spec((1, D))],
        out_shape=[jax.ShapeDtypeStruct((t, D), F32), jax.ShapeDtypeStruct((1, 128), F32),
                   jax.ShapeDtypeStruct((1, D), F32)],
        compiler_params=_params(1),
    )(act, w_down, h1, target, g_final, row_mask)


def _ffn_bwd(dh2, gate, up, h1, w_down_t, w_gate_t, w_up_t, w_out_t, g_ffn):
    t = h1.shape[0]
    r = _row_tile(t, FFN_BWD_TILE)

    def body(dh2_ref, gate_ref, up_ref, h1_ref, wd_ref, wg_ref, wu_ref, wo_ref, g_ref,
             dgate_ref, dup_ref, dh1_ref, dycat_ref, dg_ref):
        @pl.when(pl.program_id(0) == 0)
        def _():
            dg_ref[...] = jnp.zeros_like(dg_ref)

        dh2 = dh2_ref[...]
        dact = _dot(dh2.astype(BF16), wd_ref[...])
        gate = gate_ref[...]
        sg = _sigmoid(gate)
        dgate = (dact * up_ref[...] * (sg * (1.0 + gate * (1.0 - sg)))).astype(BF16)
        dup = (dact * (gate * sg)).astype(BF16)
        dgate_ref[...] = dgate
        dup_ref[...] = dup
        dhn = _dot(dgate, wg_ref[...]) + _dot(dup, wu_ref[...])
        h1 = h1_ref[...]
        rstd = lax.rsqrt(jnp.mean(h1 * h1, axis=-1, keepdims=True) + RMS_EPS)
        nrm = h1 * rstd
        dg_ref[...] += jnp.sum(dhn * nrm, axis=0, keepdims=True)
        dn = dhn * g_ref[...]
        dh1 = dh2 + rstd * (dn - nrm * jnp.mean(dn * nrm, axis=-1, keepdims=True))
        dh1_ref[...] = dh1
        dycat_ref[...] = _dot(dh1.astype(BF16), wo_ref[...])

    rows = lambda w: pl.BlockSpec((r, w), lambda i: (i, 0))
    return pl.pallas_call(
        body, name="ffn_bwd", grid=(t // r,),
        in_specs=[rows(D), rows(D_FF), rows(D_FF), rows(D), _const_spec((D, D_FF)), _const_spec((D_FF, D)),
                  _const_spec((D_FF, D)), _const_spec((D, D)), _const_spec((1, D))],
        out_specs=[rows(D_FF), rows(D_FF), rows(D), rows(D), _acc_spec((1, D))],
        out_shape=[jax.ShapeDtypeStruct((t, D_FF), BF16), jax.ShapeDtypeStruct((t, D_FF), BF16),
                   jax.ShapeDtypeStruct((t, D), F32), jax.ShapeDtypeStruct((t, D), F32),
                   jax.ShapeDtypeStruct((1, D), F32)],
        compiler_params=_params(1),
    )(dh2, gate, up, h1, w_down_t, w_gate_t, w_up_t, w_out_t, g_ffn)


def _conv_bwd(dycat, yc, u, conv_w, ln_g, ln_b, n_ex, lp, plan=None):
    r = CONV_TILE
    nt = lp // r
    hb = r // CONV_SUB
    nsub = r // CONV_SUB

    def ln_bwd(dy, yc_rows, live, lg, lb):
        mu = jnp.mean(yc_rows, axis=-1, keepdims=True)
        cen = yc_rows - mu
        rs = lax.rsqrt(jnp.mean(cen * cen, axis=-1, keepdims=True) + LN_EPS)
        yn = cen * rs
        out = yn * lg + lb
        so = _sigmoid(out)
        dout = jnp.where(live, dy * (so * (1.0 + out * (1.0 - so))), 0.0)
        dyn = dout * lg
        dyc = rs * (dyn - jnp.mean(dyn, axis=-1, keepdims=True) - yn * jnp.mean(dyn * yn, axis=-1, keepdims=True))
        return dyc, dout, yn

    def body(dy_ref, dyn_ref, yc_ref, ycn_ref, cur_ref, prev_ref, w_ref, lg_ref, lb_ref,
             du_ref, dw_ref, db_ref, dlg_ref, dlb_ref, glu, dycs, dwacc, glu_sh, dycs_sh):
        b = pl.program_id(0)
        i = pl.program_id(1)
        first = jnp.logical_and(b == 0, i == 0)

        @pl.when(first)
        def _():
            dwacc[...] = jnp.zeros_like(dwacc)
            db_ref[...] = jnp.zeros_like(db_ref)
            dlg_ref[...] = jnp.zeros_like(dlg_ref)
            dlb_ref[...] = jnp.zeros_like(dlb_ref)

        lg, lb = lg_ref[...], lb_ref[...]
        cur = cur_ref[...]
        sig = _sigmoid(cur[:, C_CONV:])
        glu[CONV_SUB:CONV_SUB + r, :] = cur[:, :C_CONV] * sig
        pv = prev_ref[...]
        glu[0:CONV_SUB, :] = jnp.where(i > 0, pv[:, :C_CONV] * _sigmoid(pv[:, C_CONV:]), 0.0)

        row = i * r + lax.broadcasted_iota(jnp.int32, (r, 1), 0)
        dyc, dout, yn = ln_bwd(dy_ref[...], yc_ref[...], row >= PAD_ROWS, lg, lb)
        dycs[0:r, :] = dyc
        dycn, _, _ = ln_bwd(dyn_ref[...], ycn_ref[...], i < nt - 1, lg, lb)
        dycs[r:r + CONV_SUB, :] = dycn
        db_ref[...] += jnp.sum(dyc, axis=0, keepdims=True)
        dlg_ref[...] += jnp.sum(dout * yn, axis=0, keepdims=True)
        dlb_ref[...] += jnp.sum(dout, axis=0, keepdims=True)

        _shifted_copies(glu, glu_sh, r)
        _shifted_copies(dycs, dycs_sh, r)
        w = w_ref[...]
        for j in range(nsub):
            r0 = j * CONV_SUB
            dblk = dycs[r0:r0 + CONV_SUB, :]
            dglu = jnp.zeros((CONV_SUB, C_CONV), F32)
            for k in range(CONV_W):
                dglu = dglu + w[k:k + 1, :] * _shifted_rows(dycs, dycs_sh, r0 + (CONV_W - 1) - k)
                prod = dblk * _shifted_rows(glu, glu_sh, r0 + CONV_LEAD + k)
                dwacc[k] += prod.reshape(CONV_SUB // SUBLANES, SUBLANES, C_CONV).sum(axis=0)
            sg = sig[r0:r0 + CONV_SUB, :]
            cv = cur[r0:r0 + CONV_SUB, :C_CONV]
            du_ref[r0:r0 + CONV_SUB, :C_CONV] = (dglu * sg).astype(BF16)
            du_ref[r0:r0 + CONV_SUB, C_CONV:] = (dglu * cv * sg * (1.0 - sg)).astype(BF16)

        @pl.when(jnp.logical_and(b == n_ex - 1, i == nt - 1))
        def _():
            dw_ref[...] = jnp.sum(dwacc[...], axis=1)

    t = n_ex * lp
    cur_rows = lambda w, col: pl.BlockSpec((r, w), lambda b, i: (b * nt + i, col))
    nxt_rows = lambda w, col: pl.BlockSpec(
        (CONV_SUB, w), lambda b, i: (jnp.minimum((b * nt + i + 1) * hb, n_ex * nt * hb - 1), col))
    return _call(
        body, name="conv_bwd", grid=(n_ex, nt),
        in_specs=[cur_rows(C_CONV, 0), nxt_rows(C_CONV, 0), cur_rows(C_CONV, 0), nxt_rows(C_CONV, 0),
                  cur_rows(2 * C_CONV, 0),
                  pl.BlockSpec((CONV_SUB, 2 * C_CONV), lambda b, i: (jnp.maximum((b * nt + i) * hb - 1, 0), 0)),
                  _const_spec((32, C_CONV)), _const_spec((1, C_CONV)), _const_spec((1, C_CONV))],
        out_specs=[cur_rows(2 * C_CONV, 0), _acc_spec((32, C_CONV)), _acc_spec((1, C_CONV)),
                   _acc_spec((1, C_CONV)), _acc_spec((1, C_CONV))],
        out_shape=[jax.ShapeDtypeStruct((t, 2 * C_CONV), BF16), jax.ShapeDtypeStruct((32, C_CONV), F32),
                   jax.ShapeDtypeStruct((1, C_CONV), F32), jax.ShapeDtypeStruct((1, C_CONV), F32),
                   jax.ShapeDtypeStruct((1, C_CONV), F32)],
        scratch_shapes=[pltpu.VMEM((r + CONV_SUB, C_CONV), F32), pltpu.VMEM((r + CONV_SUB, C_CONV), F32),
                        pltpu.VMEM((32, SUBLANES, C_CONV), F32),
                        pltpu.VMEM((SUBLANES - 1, r + CONV_SUB - SUBLANES, C_CONV), F32),
                        pltpu.VMEM((SUBLANES - 1, r + CONV_SUB - SUBLANES, C_CONV), F32)],
        plan=plan,
    )(dycat, dycat, yc, yc, u, u, conv_w, ln_g, ln_b)


HEAD_ROWS_ALL = N_HEADS * CHUNK


def _gla_gates(lr, w2, gb, first_chunk):
    z = _dot(lr.astype(BF16), w2) + gb
    a = (jnp.minimum(z, 0.0) - jnp.log(1.0 + jnp.exp(-jnp.abs(z)))) * (1.0 / GATE_TAU)
    row = lax.broadcasted_iota(jnp.int32, (CHUNK, 1), 0)
    live = jnp.logical_or(jnp.logical_not(first_chunk), row >= PAD_ROWS)
    return z, jnp.where(live, a, 0.0), live


def _tri(lower):
    i = lax.broadcasted_iota(jnp.int32, (CHUNK, CHUNK), 0)
    j = lax.broadcasted_iota(jnp.int32, (CHUNK, CHUNK), 1)
    return (i >= j) if lower else (i <= j)


def _head_of(shape, axis, per_head):
    return lax.broadcasted_iota(jnp.int32, shape, axis) // per_head


def _expand(x, lanes_per_head):
    rows, lanes = HEAD_ROWS_ALL, x.shape[1]
    keep = _head_of((rows, lanes), 0, CHUNK) == _head_of((rows, lanes), 1, lanes_per_head)
    return jnp.where(keep, jnp.tile(x, (N_HEADS, 1)), 0.0)


def _expand_lanes(x):
    rows, w = x.shape
    keep = _head_of((rows, N_HEADS * w), 0, CHUNK) == _head_of((rows, N_HEADS * w), 1, w)
    return jnp.where(keep, jnp.tile(x, (1, N_HEADS)), 0.0)


def _expand_state(st):
    rows, lanes = N_HEADS * DV, st.shape[1]
    keep = _head_of((rows, lanes), 0, DV) == _head_of((rows, lanes), 1, DK)
    return jnp.where(keep, jnp.tile(st, (N_HEADS, 1)), 0.0)


def _fold(t, rows_per_head):
    lane_head = _head_of((rows_per_head, t.shape[1]), 1, DK)
    out = jnp.where(lane_head == 0, t[0:rows_per_head], 0.0)
    for h in range(1, N_HEADS):
        out = out + jnp.where(lane_head == h, t[h * rows_per_head:(h + 1) * rows_per_head], 0.0)
    return out


def _rows_by_head(x):
    return jnp.concatenate([x[:, h * DV:(h + 1) * DV] for h in range(N_HEADS)], axis=0)


def _lanes_by_head(x):
    return jnp.concatenate([x[h * CHUNK:(h + 1) * CHUNK] for h in range(N_HEADS)], axis=1)


def _running_sum(a, lower):
    hi = a.astype(BF16)
    rest = a - hi.astype(F32)
    mid = rest.astype(BF16)
    lo = (rest - mid.astype(F32)).astype(BF16)
    w = a.shape[1]
    parts = _dot(_tri(lower).astype(F32).astype(BF16), jnp.concatenate([hi, mid, lo], axis=1))
    return parts[:, :w] + parts[:, w:2 * w] + parts[:, 2 * w:]


def _stacked_causal():
    i = lax.broadcasted_iota(jnp.int32, (HEAD_ROWS_ALL, CHUNK), 0) % CHUNK
    j = lax.broadcasted_iota(jnp.int32, (HEAD_ROWS_ALL, CHUNK), 1)
    return i >= j


GLA_GROUP = 3


def _gla_chunk(q, k, v, lr, w2, gb, first_chunk):
    z, a, live = _gla_gates(lr, w2, gb, first_chunk)
    yield
    b = _running_sum(a, True)
    yield
    bl = b[CHUNK - 1:CHUNK, :]
    e_pos, e_neg, e_dec = jnp.exp(b), jnp.exp(-b), jnp.exp(bl - b)
    q_f, k_f, kd_f = q * (DK ** -0.5) * e_pos, k * e_neg, k * e_dec
    qx = _expand(q_f, DK).astype(BF16)
    k_in, k_dec, v_b = k_f.astype(BF16), kd_f.astype(BF16), v.astype(BF16)
    s = jnp.where(_stacked_causal(), _dot_nt(qx, k_in), 0.0).astype(BF16)
    yield
    p = _dot(s, v_b)
    yield
    o_intra = jnp.concatenate([p[h * CHUNK:(h + 1) * CHUNK, h * DV:(h + 1) * DV] for h in range(N_HEADS)], axis=0)
    return dict(z=z, live=live, bl=bl, e_pos=e_pos, e_neg=e_neg, e_dec=e_dec, q_f=q_f, k_f=k_f, kd_f=kd_f,
                qx=qx, k_in=k_in, k_dec=k_dec, v_b=v_b, s=s, o_intra=o_intra, decay=jnp.exp(bl))


def _gla_fwd(u, w2, gb, ng, n_ex, lp, plan=None):
    nc = lp // CHUNK
    t = n_ex * lp
    rows_of = lambda j: pl.ds(j * CHUNK, CHUNK)

    def body(qk_ref, v_ref, g_ref, lr_ref, w2_ref, gb_ref, ng_ref, y_ref, st_ref, state):
        n = pl.program_id(0)

        @pl.when(n == 0)
        def _():
            state[...] = jnp.zeros_like(state)

        carried = [state[e] for e in range(n_ex)]

        def one_chunk(e, j):
            rows = rows_of(j)
            qk = qk_ref[e, rows, :]
            first = jnp.logical_and(n == 0, j == 0)
            c = yield from _gla_chunk(qk[:, :GLA_K], qk[:, GLA_K:], v_ref[e, rows, :], lr_ref[e, rows, :],
                                      w2_ref[...], gb_ref[...], first)
            kv = _fold(_dot_tn(c["v_b"], c["k_dec"]), DV)
            g = _rows_by_head(g_ref[e, rows, :])
            gate = ng_ref[...] * (g * _sigmoid(g))
            yield
            for _ in range(j):
                yield
            st = carried[e]
            st_ref[e, pl.ds(j * DV, DV), :] = st
            o = c["o_intra"] + _dot_nt(c["qx"], st.astype(BF16))
            rstd = lax.rsqrt(jnp.mean(o * o, axis=-1, keepdims=True) + RMS_EPS)
            y_ref[e, rows, :] = _lanes_by_head(o * rstd * gate).astype(BF16)
            carried[e] = c["decay"] * st + kv

        _in_lockstep(one_chunk(e, j) for j in range(GLA_GROUP) for e in range(n_ex))
        for e in range(n_ex):
            state[e] = carried[e]

    u3 = u.reshape(n_ex, lp, D_IN_PAD)
    blk = lambda w, col: pl.BlockSpec((n_ex, GLA_GROUP * CHUNK, w), lambda n: (0, n, col))
    (y, states), extra = _call(
        body, name="gla_fwd", grid=(nc // GLA_GROUP,),
        in_specs=[blk(2 * GLA_K, 2), blk(GLA_V, 3), blk(GLA_V, 4), blk(128, 20),
                  _const_spec((128, GLA_K)), _const_spec((1, GLA_K)), _const_spec((1, DV))],
        out_specs=[blk(GLA_V, 0), pl.BlockSpec((n_ex, GLA_GROUP * DV, GLA_K), lambda n: (0, n, 0))],
        out_shape=[jax.ShapeDtypeStruct((n_ex, lp, GLA_V), BF16),
                   jax.ShapeDtypeStruct((n_ex, nc * DV, GLA_K), F32)],
        scratch_shapes=[pltpu.VMEM((n_ex, DV, GLA_K), F32)],
        plan=plan,
    )(u3, u3, u3, u3, w2, gb, ng)
    return (y.reshape(t, GLA_V), states), extra


def _gla_bwd(dycat, u, states, w2, gb, ng, n_ex, lp, plan=None):
    nc = lp // CHUNK
    t = n_ex * lp

    def body(dy_ref, qk_ref, v_ref, g_ref, lr_ref, st_ref, w2_ref, gb_ref, ng_ref,
             du_ref, dw2_ref, dgb_ref, dng_ref, dstate):
        n = pl.program_id(0)
        group = nc // GLA_GROUP - 1 - n

        @pl.when(n == 0)
        def _():
            dw2_ref[...] = jnp.zeros_like(dw2_ref)
            dgb_ref[...] = jnp.zeros_like(dgb_ref)
            dng_ref[...] = jnp.zeros_like(dng_ref)
            dstate[...] = jnp.zeros_like(dstate)

        carried = [dstate[e] for e in range(n_ex)]

        def one_chunk(e, order):
            j = GLA_GROUP - 1 - order
            rows = pl.ds(j * CHUNK, CHUNK)
            qk = qk_ref[e, rows, :]
            lr = lr_ref[e, rows, :]
            st = st_ref[e, pl.ds(j * DV, DV), :]
            first = jnp.logical_and(group == 0, j == 0)
            c = yield from _gla_chunk(qk[:, :GLA_K], qk[:, GLA_K:], v_ref[e, rows, :], lr, w2_ref[...], gb_ref[...],
                                      first)
            qx, k_in, k_dec, v_b, s = c["qx"], c["k_in"], c["k_dec"], c["v_b"], c["s"]
            st_b = st.astype(BF16)
            o = c["o_intra"] + _dot_nt(qx, st_b)
            ngv = ng_ref[...]
            yield
            rstd = lax.rsqrt(jnp.mean(o * o, axis=-1, keepdims=True) + RMS_EPS)
            nrm = o * rstd
            g = _rows_by_head(g_ref[e, rows, :])
            dy = _rows_by_head(dy_ref[e, rows, :])
            sg = _sigmoid(g)
            dg = dy * nrm * ngv * (sg * (1.0 + g * (1.0 - sg)))
            dt = dy * (g * sg)
            dng_ref[...] += jnp.sum(dt * nrm, axis=0, keepdims=True)
            dn = dt * ngv
            do = rstd * (dn - nrm * jnp.mean(dn * nrm, axis=-1, keepdims=True))
            do_b = do.astype(BF16)
            dox = _expand_lanes(do).astype(BF16)
            yield
            da = jnp.where(_stacked_causal(), _dot_nt(dox, v_b), 0.0).astype(BF16)
            dv_intra = _dot_tn(s, dox)
            dst_own = _dot_tn(do_b, qx)
            yield
            dq_in = _fold(_dot(da, k_in) + _dot(do_b, st_b), CHUNK)
            dk_in = _dot_tn(da, qx)
            dq = dq_in * (DK ** -0.5) * c["e_pos"]
            yield
            for _ in range(order):
                yield
            dst = carried[e]
            dstx = _expand_state(dst).astype(BF16)
            dv = dv_intra + _dot_nt(k_dec, dstx)
            dk_dec = _dot(v_b, dstx)
            carried[e] = dst_own + c["decay"] * dst
            yield
            dbl = (jnp.sum(dk_dec * c["kd_f"], axis=0, keepdims=True)
                   + c["decay"] * jnp.sum(dst * st, axis=0, keepdims=True))
            dk = dk_in * c["e_neg"] + dk_dec * c["e_dec"]
            db = dq_in * c["q_f"] - dk_in * c["k_f"] - dk_dec * c["kd_f"]
            row = lax.broadcasted_iota(jnp.int32, (CHUNK, 1), 0)
            da_log = _running_sum(db + jnp.where(row == CHUNK - 1, dbl, 0.0), False)
            yield
            dz = jnp.where(c["live"], da_log * (1.0 - _sigmoid(c["z"])) * (1.0 / GATE_TAU), 0.0)
            dz_b = dz.astype(BF16)
            out = du_ref.at[e, rows, :]
            out[:, 0:GLA_K] = dq.astype(BF16)
            out[:, GLA_K:2 * GLA_K] = dk.astype(BF16)
            out[:, 2 * GLA_K:2 * GLA_K + GLA_V] = dv.astype(BF16)
            out[:, 2 * GLA_K + GLA_V:2 * GLA_K + 2 * GLA_V] = _lanes_by_head(dg).astype(BF16)
            out[:, 2 * GLA_K + 2 * GLA_V:] = _dot_nt(dz_b, w2_ref[...]).astype(BF16)
            dw2_ref[...] += _dot_tn(lr.astype(BF16), dz_b)
            dgb_ref[...] += jnp.sum(dz, axis=0, keepdims=True)

        _in_lockstep(one_chunk(e, order) for order in range(GLA_GROUP) for e in range(n_ex))
        for e in range(n_ex):
            dstate[e] = carried[e]

    u3 = u.reshape(n_ex, lp, D_IN_PAD)
    rev = lambda w, col: pl.BlockSpec((n_ex, GLA_GROUP * CHUNK, w), lambda n: (0, nc // GLA_GROUP - 1 - n, col))
    (du, d_w2, d_gb, d_ng), extra = _call(
        body, name="gla_bwd", grid=(nc // GLA_GROUP,),
        in_specs=[rev(GLA_V, 1), rev(2 * GLA_K, 2), rev(GLA_V, 3), rev(GLA_V, 4), rev(128, 20),
                  pl.BlockSpec((n_ex, GLA_GROUP * DV, GLA_K), lambda n: (0, nc // GLA_GROUP - 1 - n, 0)),
                  _const_spec((128, GLA_K)), _const_spec((1, GLA_K)), _const_spec((1, DV))],
        out_specs=[rev(D_GLA_IN, 0), _acc_spec((128, GLA_K)), _acc_spec((1, GLA_K)), _acc_spec((1, DV))],
        out_shape=[jax.ShapeDtypeStruct((n_ex, lp, D_GLA_IN), BF16), jax.ShapeDtypeStruct((128, GLA_K), F32),
                   jax.ShapeDtypeStruct((1, GLA_K), F32), jax.ShapeDtypeStruct((1, DV), F32)],
        scratch_shapes=[pltpu.VMEM((n_ex, DV, GLA_K), F32)],
        plan=plan,
    )(dycat.reshape(n_ex, lp, D), u3, u3, u3, u3, states, w2, gb, ng)
    return (du.reshape(t, D_GLA_IN), d_w2, d_gb, d_ng), extra


def _in_proj_bwd(du_conv, du_gla, w_in_t_conv, w_in_t_gla, h0, dh1, g_mix, plan=None):
    t = h0.shape[0]
    r = _row_tile(t, 384)

    def body(dc_ref, dg_ref, wc_ref, wg_ref, h_ref, dh1_ref, g_ref, dh0_ref, dgm_ref):
        @pl.when(pl.program_id(0) == 0)
        def _():
            dgm_ref[...] = jnp.zeros_like(dgm_ref)

        dhn = _dot(dc_ref[...], wc_ref[...]) + _dot(dg_ref[...], wg_ref[...])
        h = h_ref[...]
        rstd = lax.rsqrt(jnp.mean(h * h, axis=-1, keepdims=True) + RMS_EPS)
        nrm = h * rstd
        dgm_ref[...] += jnp.sum(dhn * nrm, axis=0, keepdims=True)
        dn = dhn * g_ref[...]
        dh0_ref[...] = dh1_ref[...] + rstd * (dn - nrm * jnp.mean(dn * nrm, axis=-1, keepdims=True))

    rows = lambda w: pl.BlockSpec((r, w), lambda i: (i, 0))
    return _call(
        body, name="in_proj_bwd", grid=(t // r,),
        in_specs=[rows(2 * C_CONV), rows(D_GLA_IN), _const_spec((2 * C_CONV, D)), _const_spec((D_GLA_IN, D)),
                  rows(D), rows(D), _const_spec((1, D))],
        out_specs=[rows(D), _acc_spec((1, D))],
        out_shape=[jax.ShapeDtypeStruct((t, D), F32), jax.ShapeDtypeStruct((1, D), F32)],
        plan=plan,
    )(du_conv, du_gla, w_in_t_conv, w_in_t_gla, h0, dh1, g_mix)


def _wgrad(x, dy, name, plan=None):
    t, m = x.shape
    n = dy.shape[1]
    tk = t // 3 if t % (3 * 128) == 0 else _row_tile(t, 384)
    tm = m if m <= D_GLA_IN else m // 2

    def body(x_ref, dy_ref, o_ref):
        @pl.when(pl.program_id(1) == 0)
        def _():
            o_ref[...] = jnp.zeros_like(o_ref)

        o_ref[...] += _dot_tn(x_ref[...].astype(BF16), dy_ref[...].astype(BF16))

    (out,), extra = _call(
        body, name=name, grid=(m // tm, t // tk),
        in_specs=[pl.BlockSpec((tk, tm), lambda i, k: (k, i)), pl.BlockSpec((tk, n), lambda i, k: (k, 0))],
        out_specs=[pl.BlockSpec((tm, n), lambda i, k: (i, 0))],
        out_shape=[jax.ShapeDtypeStruct((m, n), F32)],
        plan=plan,
    )(x, dy)
    return out, extra


def _wgrad_pair(xa, xb, dy, name):
    t, m = xa.shape
    n = dy.shape[1]
    tk = t // 3 if t % (3 * 128) == 0 else _row_tile(t, 384)

    def body(xa_ref, xb_ref, dy_ref, o_ref):
        @pl.when(pl.program_id(1) == 0)
        def _():
            o_ref[...] = jnp.zeros_like(o_ref)

        x = jnp.where(pl.program_id(0) == 0, xa_ref[...], xb_ref[...])
        o_ref[...] += _dot_tn(x.astype(BF16), dy_ref[...].astype(BF16))

    rows = lambda w: pl.BlockSpec((tk, w), lambda i, k: (k, 0))
    return pl.pallas_call(
        body, name=name, grid=(2, t // tk), in_specs=[rows(m), rows(m), rows(n)],
        out_specs=pl.BlockSpec((m, n), lambda i, k: (i, 0)),
        out_shape=jax.ShapeDtypeStruct((2 * m, n), F32), compiler_params=_params(2),
    )(xa, xb, dy)


def _adam_update(g, w, m, v):
    m2 = ADAM_B1 * m + (1.0 - ADAM_B1) * g
    v2 = ADAM_B2 * v + (1.0 - ADAM_B2) * (g * g)
    m_hat = m2 / (1.0 - ADAM_B1 ** ADAM_STEP)
    v_hat = v2 / (1.0 - ADAM_B2 ** ADAM_STEP)
    delta = -ADAM_LR * (m_hat / (jnp.sqrt(v_hat) + ADAM_EPS) + ADAM_WD * w)
    return delta, m2, v2


ADAMW_STEPS = 4


def _adamw(g, w, m, v, name):
    rows, cols = g.shape
    steps = ADAMW_STEPS if rows % (ADAMW_STEPS * SUBLANES) == 0 else 1

    def body(g_ref, w_ref, m_ref, v_ref, d_ref, m2_ref, v2_ref):
        d_ref[...], m2_ref[...], v2_ref[...] = _adam_update(g_ref[...], w_ref[...], m_ref[...], v_ref[...])

    spec = pl.BlockSpec((rows // steps, cols), lambda i: (i, 0))
    return pl.pallas_call(
        body, name=name, grid=(steps,), in_specs=[spec] * 4, out_specs=[spec] * 3,
        out_shape=[jax.ShapeDtypeStruct(g.shape, F32)] * 3, compiler_params=_params(1),
    )(g, w, m, v)


def _adamw_halves(items, c, name):
    n = len(items)
    h = items[0][0].shape[-1]
    splits = lambda a: a.shape[0] % (ADAMW_STEPS * (SUBLANES if a.ndim == 2 else 1)) == 0
    steps = ADAMW_STEPS if all(splits(it[0]) for it in items) else 1

    def body(c_ref, *refs):
        ins, outs = refs[:5 * n], refs[5 * n:]
        own = pl.program_id(1) == c_ref[0]
        for i in range(n):
            a_ref, b_ref, w_ref, m_ref, v_ref = ins[5 * i:5 * i + 5]
            go_ref, d_ref, m2_ref, v2_ref = outs[4 * i:4 * i + 4]
            g = jnp.where(own, a_ref[...], b_ref[...])
            go_ref[...] = g
            d_ref[...], m2_ref[...], v2_ref[...] = _adam_update(g, w_ref[...], m_ref[...], v_ref[...])

    in_specs, out_specs, out_shape, args = [pl.BlockSpec(memory_space=pltpu.SMEM)], [], [], []
    for mine, theirs, w, m, v in items:
        tr = mine.shape[0] // steps
        mid = (0,) * (mine.ndim - 2)
        half = pl.BlockSpec((tr,) + mine.shape[1:-1] + (h,), lambda i, j, mid=mid: (i, *mid, 0))
        full = pl.BlockSpec((tr,) + mine.shape[1:-1] + (h,), lambda i, j, mid=mid: (i, *mid, j))
        in_specs += [half, half, full, full, full]
        out_specs += [full] * 4
        out_shape += [jax.ShapeDtypeStruct(w.shape, F32)] * 4
        args += [mine, theirs, w, m, v]
    res = pl.pallas_call(
        body, name=name, grid=(steps, 2), in_specs=in_specs, out_specs=out_specs, out_shape=out_shape,
        compiler_params=_params(2),
    )(jnp.reshape(c, (1,)).astype(jnp.int32), *args)
    return [res[4 * i:4 * i + 4] for i in range(n)]


def _rs_add_halves(g, recv, c, name):
    blocks, rows, w = g.shape
    h = w // 2

    def body(c_ref, a_ref, b_ref, o_ref):
        o_ref[...] = (a_ref[...] + b_ref[...]).astype(BF16)

    return pl.pallas_call(
        body, name=name,
        grid_spec=pltpu.PrefetchScalarGridSpec(
            num_scalar_prefetch=1, grid=(blocks,),
            in_specs=[pl.BlockSpec((1, rows, h), lambda j, s: (j, 0, s[0])),
                      pl.BlockSpec((1, rows, h), lambda j, s: (j, 0, 0))],
            out_specs=pl.BlockSpec((1, rows, h), lambda j, s: (j, 0, 0))),
        out_shape=jax.ShapeDtypeStruct((blocks, rows, h), BF16),
        compiler_params=_params(1),
    )(jnp.reshape(c, (1,)).astype(jnp.int32), g, recv)


def _rs_sum(own, others, mine, name):
    _, rows, h = own.shape
    tr = rows // 2 if rows % 16 == 0 and rows > 64 else rows

    def body(mine_ref, own_ref, oth_ref, o_ref):
        p = oth_ref[...].astype(F32)
        o_ref[...] = ((own_ref[0].astype(F32) + p[0]) + p[1]) + p[2]

    return pl.pallas_call(
        body, name=name,
        grid_spec=pltpu.PrefetchScalarGridSpec(
            num_scalar_prefetch=1, grid=(rows // tr,),
            in_specs=[pl.BlockSpec((1, tr, h), lambda i, s: (s[0], i, 0)),
                      pl.BlockSpec((3, tr, h), lambda i, s: (0, i, 0))],
            out_specs=pl.BlockSpec((tr, h), lambda i, s: (i, 0))),
        out_shape=jax.ShapeDtypeStruct((rows, h), F32),
        compiler_params=_params(1),
    )(jnp.reshape(mine, (1,)).astype(jnp.int32), own, others)


def _sum_slots_adamw(slots, late_slots, vectors):
    late_rows = late_slots.shape[1]
    n = len(SMALL_PARTS)

    def body(s_ref, l_ref, *refs):
        ins, g_ref, outs = refs[:3 * n], refs[3 * n], refs[3 * n + 1:]
        g, late = s_ref[0], l_ref[0]
        for d in range(1, 8):
            g = g + s_ref[d]
            late = late + l_ref[d]
        g = jnp.concatenate([g[:late_rows] + late, g[late_rows:]], axis=0)
        g_ref[...] = g
        for i, (_, row, col, size) in enumerate(SMALL_PARTS):
            w_ref, m_ref, v_ref = ins[3 * i:3 * i + 3]
            go_ref, d_ref, m2_ref, v2_ref = outs[4 * i:4 * i + 4]
            piece = g[row:row + 1, col:col + size]
            go_ref[...] = piece
            d_ref[...], m2_ref[...], v2_ref[...] = _adam_update(piece, w_ref[...], m_ref[...], v_ref[...])

    vm = pl.BlockSpec(memory_space=pltpu.VMEM)
    out_shape = [jax.ShapeDtypeStruct(slots.shape[1:], F32)]
    for _, _, _, size in SMALL_PARTS:
        out_shape += [jax.ShapeDtypeStruct((1, size), F32)] * 4
    res = pl.pallas_call(body, name="small_sum_adamw", in_specs=[vm] * (2 + 3 * n), out_specs=[vm] * len(out_shape),
                         out_shape=out_shape)(slots, late_slots, *[a for wmv in vectors for a in wmv])
    return res[0], [res[1 + 4 * i:5 + 4 * i] for i in range(n)]


def _mesh_pos():
    return lax.axis_index("x"), lax.axis_index("y"), lax.axis_index("c")


def _other_chips(x, y):
    return [(1 - x, y), (x, 1 - y), (1 - x, 1 - y)]


def _half(ref, c, axis):
    n = ref.shape[axis] // 2
    return ref.at[(slice(None),) * axis + (pl.ds(c * n, n),)]


def _remote(src, dst, send_sem, recv_sem, device):
    return pltpu.make_async_remote_copy(src_ref=src, dst_ref=dst, send_sem=send_sem, recv_sem=recv_sem,
                                        device_id=device, device_id_type=MESH)


def _gather_plan(split, whole=(), axes=None):
    split, whole = list(split), list(whole)
    ns, n = len(split), len(split) + len(whole)

    def make(ins, outs, sems):
        ici_send, ici_recv, d2d_send, d2d_recv, own_send, own_recv = sems
        x, y, c = _mesh_pos()
        mine = 2 * x + y
        chips = _other_chips(x, y)
        blocks = [2 * px + py for px, py in chips]

        def own(a):
            return _remote(ins[a], outs[a].at[mine], own_send.at[a], own_recv.at[a], (x, y, 1 - c))

        def ici(a, k, block):
            px, py = chips[k]
            src, dst = ins[a], outs[a].at[block]
            if a < ns:
                src, dst = _half(src, c, axes[a]), _half(dst, c, axes[a])
            return _remote(src, dst, ici_send.at[3 * a + k], ici_recv.at[3 * a + k], (px, py, c))

        def d2d(a, k, half):
            part = _half(outs[a].at[blocks[k]], half, axes[a])
            return _remote(part, part, d2d_send.at[3 * a + k], d2d_recv.at[3 * a + k], (x, y, 1 - c))

        def start():
            for a in range(n):
                for k in range(3):
                    ici(a, k, mine).start()
                own(a).start()

        def relay():
            for a in range(n):
                for k in range(3):
                    ici(a, k, blocks[k]).wait_recv()
                    if a < ns:
                        d2d(a, k, c).start()

        def finish():
            for a in range(ns):
                for k in range(3):
                    d2d(a, k, 1 - c).wait_recv()
            for a in range(n):
                for k in range(3):
                    ici(a, k, mine).wait_send()
                    if a < ns:
                        d2d(a, k, c).wait_send()
                own(a).wait()

        return start, relay, finish

    arrays = split + whole
    axes = [0] * ns if axes is None else list(axes)
    return _Plan(arrays, [jax.ShapeDtypeStruct((N_CHIPS,) + s.shape, s.dtype) for s in arrays],
                 [pltpu.SemaphoreType.DMA((3 * n,)), pltpu.SemaphoreType.DMA((3 * n,)),
                  pltpu.SemaphoreType.DMA((3 * ns,)), pltpu.SemaphoreType.DMA((3 * ns,)),
                  pltpu.SemaphoreType.DMA((n,)), pltpu.SemaphoreType.DMA((n,))], make)


def _to_sibling_plan(gs):
    n = len(gs)

    def make(ins, outs, sems):
        send_sems, recv_sems = sems
        x, y, c = _mesh_pos()

        def copy(a):
            return _remote(_half(ins[a], 1 - c, len(ins[a].shape) - 1), outs[a], send_sems.at[a],
                           recv_sems.at[a], (x, y, 1 - c))

        def start():
            for a in range(n):
                copy(a).start()

        def finish():
            for a in range(n):
                copy(a).wait()

        return start, finish

    return _Plan(list(gs), [jax.ShapeDtypeStruct(g.shape[:-1] + (g.shape[-1] // 2,), g.dtype) for g in gs],
                 [pltpu.SemaphoreType.DMA((n,)), pltpu.SemaphoreType.DMA((n,))], make)


def _chip_exchange_plan(ps):
    n = len(ps)

    def make(ins, outs, sems):
        send_sems, recv_sems = sems
        x, y, c = _mesh_pos()
        chips = _other_chips(x, y)

        def ici(a, k):
            px, py = chips[k]
            return _remote(ins[a].at[2 * px + py], outs[a].at[k], send_sems.at[3 * a + k],
                           recv_sems.at[3 * a + k], (px, py, c))

        def start():
            for a in range(n):
                for k in range(3):
                    ici(a, k).start()

        def finish():
            for a in range(n):
                for k in range(3):
                    ici(a, k).wait()

        return start, finish

    return _Plan(list(ps), [jax.ShapeDtypeStruct((3,) + p.shape[1:], p.dtype) for p in ps],
                 [pltpu.SemaphoreType.DMA((3 * n,)), pltpu.SemaphoreType.DMA((3 * n,))], make)


def _share_plan(halves):
    n = len(halves)

    def make(ins, outs, sems):
        send_sems, recv_sems = sems
        x, y, c = _mesh_pos()

        def d2d(a):
            return _remote(ins[a], outs[a], send_sems.at[a], recv_sems.at[a], (x, y, 1 - c))

        def start():
            for a in range(n):
                d2d(a).start()

        def finish():
            for a in range(n):
                d2d(a).wait()

        return start, finish

    return _Plan(list(halves), [jax.ShapeDtypeStruct(p.shape, p.dtype) for p in halves],
                 [pltpu.SemaphoreType.DMA((n,)), pltpu.SemaphoreType.DMA((n,))], make)


def _all_to_all_plan(part):
    def make(ins, outs, sems):
        send_sems, recv_sems, local_sem = sems
        (p_ref,), (slots,) = ins, outs
        x, y, c = _mesh_pos()
        me = 4 * x + 2 * y + c
        peers = [(px, py, pc) for px in (x, 1 - x) for py in (y, 1 - y) for pc in (c, 1 - c)][1:]

        def remote(k, slot):
            return _remote(p_ref, slots.at[slot], send_sems.at[k], recv_sems.at[k], peers[k])

        def local():
            return pltpu.make_async_copy(p_ref, slots.at[me], local_sem)

        def start():
            for k in range(7):
                remote(k, me).start()
            local().start()

        def finish():
            for k, (px, py, pc) in enumerate(peers):
                remote(k, 4 * px + 2 * py + pc).wait_recv()
            for k in range(7):
                remote(k, me).wait_send()
            local().wait()

        return start, finish

    return _Plan([part], [jax.ShapeDtypeStruct((8,) + part.shape, part.dtype)],
                 [pltpu.SemaphoreType.DMA((7,)), pltpu.SemaphoreType.DMA((7,)), pltpu.SemaphoreType.DMA(())], make)


def _merge_plans(a, b):
    na_in, na_out, na_sems = len(a.arrays), len(a.out_shape), len(a.sems)

    def make(ins, outs, sems):
        phases_a = _phases(a.make(ins[:na_in], outs[:na_out], sems[:na_sems]))
        phases_b = _phases(b.make(ins[na_in:], outs[na_out:], sems[na_sems:]))

        def both(i):
            def run():
                phases_a[i]()
                phases_b[i]()
            return run

        return both(0), both(1), both(2)

    return _Plan(list(a.arrays) + list(b.arrays), list(a.out_shape) + list(b.out_shape),
                 list(a.sems) + list(b.sems), make)


def _exchange(plan, name):
    n_in, n_out = len(plan.arrays), len(plan.out_shape)

    def body(*refs):
        for phase in _phases(plan.make(refs[:n_in], refs[n_in:n_in + n_out], refs[n_in + n_out:])):
            phase()

    return pl.pallas_call(
        body, name=name, in_specs=[HBM_SPEC] * n_in, out_specs=[HBM_SPEC] * n_out, out_shape=list(plan.out_shape),
        scratch_shapes=list(plan.sems), compiler_params=pltpu.CompilerParams(has_side_effects=True),
    )(*plan.arrays)


def _pack_small(parts):
    rows = []
    for r in range(SMALL_ROWS):
        pieces, col = [], 0
        for name, row, start, size in SMALL_PARTS:
            if row == r:
                assert start == col
                pieces.append(parts[name].reshape(1, size).astype(F32))
                col += size
        rows.append(jnp.concatenate(pieces + [jnp.zeros((1, D - col), F32)], axis=1))
    return jnp.concatenate(rows, axis=0)


def _columns(gathered):
    return jnp.concatenate([gathered[j] for j in range(N_CHIPS)], axis=1)


def kernel(x, meta_tokens, norm_mix_g, w_in, conv_w, conv_b, conv_ln_g, conv_ln_b, gla_w_gate2, gla_gate_b, gla_norm_g, w_out, norm_ffn_g, w_ffn_gate, w_ffn_up, w_ffn_down, norm_final_g, loss_target, m_meta_tokens, m_norm_mix_g, m_w_in, m_conv_w, m_conv_b, m_conv_ln_g, m_conv_ln_b, m_gla_w_gate2, m_gla_gate_b, m_gla_norm_g, m_w_out, m_norm_ffn_g, m_w_ffn_gate, m_w_ffn_up, m_w_ffn_down, m_norm_final_g, v_meta_tokens, v_norm_mix_g, v_w_in, v_conv_w, v_conv_b, v_conv_ln_g, v_conv_ln_b, v_gla_w_gate2, v_gla_gate_b, v_gla_norm_g, v_w_out, v_norm_ffn_g, v_w_ffn_gate, v_w_ffn_up, v_w_ffn_down, v_norm_final_g):
    ws = dict(zip(WEIGHT_NAMES, (meta_tokens, norm_mix_g, w_in, conv_w, conv_b, conv_ln_g, conv_ln_b, gla_w_gate2,
                                 gla_gate_b, gla_norm_g, w_out, norm_ffn_g, w_ffn_gate, w_ffn_up, w_ffn_down,
                                 norm_final_g)))
    ms = dict(zip(WEIGHT_NAMES, (m_meta_tokens, m_norm_mix_g, m_w_in, m_conv_w, m_conv_b, m_conv_ln_g, m_conv_ln_b,
                                 m_gla_w_gate2, m_gla_gate_b, m_gla_norm_g, m_w_out, m_norm_ffn_g, m_w_ffn_gate,
                                 m_w_ffn_up, m_w_ffn_down, m_norm_final_g)))
    vs = dict(zip(WEIGHT_NAMES, (v_meta_tokens, v_norm_mix_g, v_w_in, v_conv_w, v_conv_b, v_conv_ln_g, v_conv_ln_b,
                                 v_gla_w_gate2, v_gla_gate_b, v_gla_norm_g, v_w_out, v_norm_ffn_g, v_w_ffn_gate,
                                 v_w_ffn_up, v_w_ffn_down, v_norm_final_g)))
    c = lax.axis_index("c")
    mine = 2 * lax.axis_index("x") + lax.axis_index("y")
    shard = lambda d, name: d[name].reshape(d[name].shape[-2:])
    vec = {name: ws[name].reshape(1, -1) for name, _, _, _ in SMALL_PARTS}
    n_ex, seq, _ = x.shape
    lp = HEAD_ROWS + seq
    t = n_ex * lp

    (tgt, h0), (w_in_g, meta_g, conv_w_g, w2_g) = _pad_head_rows([loss_target, x], plan=_gather_plan(
        [shard(ws, "w_in").T.astype(BF16)],
        [shard(ws, "meta_tokens"), shard(ws, "conv_w"), shard(ws, "gla_w_gate2")], axes=[1]))
    w_in_t = jnp.concatenate([w_in_g.reshape(D_IN, D), jnp.zeros((D_IN_PAD - D_IN, D), BF16)], axis=0)
    conv_w_full = jnp.concatenate([_columns(conv_w_g), jnp.zeros((32 - CONV_W, C_CONV), F32)], axis=0)
    w2_full = jnp.concatenate([_columns(w2_g), jnp.zeros((128 - RANK, GLA_K), F32)], axis=0).astype(BF16)
    h0 = _set_meta_rows(h0, _columns(meta_g)).reshape(t, D)
    tgt = tgt.reshape(t, D)
    row_mask = jnp.concatenate([jnp.zeros((n_ex, HEAD_ROWS, 1), F32), jnp.ones((n_ex, seq, 1), F32)],
                               axis=1).reshape(t, 1)

    (u, hn), (gate_g,) = _in_proj(h0, vec["norm_mix_g"], w_in_t.T,
                                  plan=_gather_plan([shard(ws, "w_ffn_gate").T.astype(BF16)]))
    (yc, y_conv), (up_g, w_out_g) = _conv_fwd(
        u, conv_w_full, vec["conv_b"], vec["conv_ln_g"], vec["conv_ln_b"], n_ex, lp,
        plan=_gather_plan([shard(ws, "w_ffn_up").T.astype(BF16), shard(ws, "w_out").astype(BF16)]))
    (y_gla, states), _ = _gla_fwd(u, w2_full, vec["gla_gate_b"], vec["gla_norm_g"], n_ex, lp)
    w_out_full = w_out_g.reshape(D, D)
    w_gate_t, w_up_t = gate_g.reshape(D_FF, D), up_g.reshape(D_FF, D)
    (h1, hn2, gate, up, act), (down_g,) = _mix_out_ffn_up(
        h0, y_conv, y_gla, w_out_full, vec["norm_ffn_g"], w_gate_t.T, w_up_t.T,
        plan=_gather_plan([shard(ws, "w_ffn_down").astype(BF16)]))
    w_down_full = down_g.reshape(D_FF, D)
    dh2, loss, d_final_g = _ffn_down_loss(act, w_down_full, h1, tgt, vec["norm_final_g"], row_mask)
    dgate, dup, dh1, dycat, d_ffn_g = _ffn_bwd(dh2, gate, up, h1, w_down_full.T, w_gate_t, w_up_t, w_out_full.T,
                                                vec["norm_ffn_g"])

    early = ("w_ffn_gate", "w_ffn_up", "w_ffn_down", "w_out")
    ffn_block = lambda g: g.reshape(N_CHIPS, D_FF // N_CHIPS, D)
    g_gate = ffn_block(_wgrad(dgate, hn2, "wgrad_gate")[0])
    g_up, (gate_sib,) = _wgrad(dup, hn2, "wgrad_up", _to_sibling_plan([g_gate]))
    g_up = ffn_block(g_up)
    g_down, (up_sib,) = _wgrad(act, dh2, "wgrad_down", _to_sibling_plan([g_up]))
    g_down = ffn_block(g_down)
    g_out = _wgrad_pair(y_conv, y_gla, dh1, "wgrad_out").reshape(N_CHIPS, D // N_CHIPS, D)
    cs_gate = _rs_add_halves(g_gate, gate_sib, c, "rs_add_w_ffn_gate")
    cs_up = _rs_add_halves(g_up, up_sib, c, "rs_add_w_ffn_up")
    (du_conv, d_conv_w, d_conv_b, d_ln_g, d_ln_b), (ex_gate, ex_up, down_sib, out_sib) = _conv_bwd(
        dycat, yc, u, conv_w_full, vec["conv_ln_g"], vec["conv_ln_b"], n_ex, lp,
        plan=_merge_plans(_chip_exchange_plan([cs_gate, cs_up]), _to_sibling_plan([g_down, g_out])))
    cs_down = _rs_add_halves(g_down, down_sib, c, "rs_add_w_ffn_down")
    cs_out = _rs_add_halves(g_out, out_sib, c, "rs_add_w_out")
    (du_gla, d_w2, d_gate_b, d_norm_g), (ex_down, ex_out) = _gla_bwd(
        dycat, u, states, w2_full, vec["gla_gate_b"], vec["gla_norm_g"], n_ex, lp,
        plan=_chip_exchange_plan([cs_down, cs_out]))
    halves = [_rs_sum(own, oth, mine, "rs_sum_" + nm)
              for own, oth, nm in zip((cs_gate, cs_up, cs_down, cs_out), (ex_gate, ex_up, ex_down, ex_out), early)]

    small = {"norm_mix_g": jnp.zeros((1, D), F32), "norm_ffn_g": d_ffn_g, "norm_final_g": d_final_g,
             "conv_b": d_conv_b, "conv_ln_g": d_ln_g, "conv_ln_b": d_ln_b, "gla_gate_b": d_gate_b,
             "gla_norm_g": d_norm_g}
    part = lax.dynamic_update_slice(_pack_small(small), loss[:, :1], (LOSS_ROW, 0))
    part = jnp.concatenate([part, jnp.zeros((N_META, D), F32), d_conv_w.reshape(16, D), d_w2[:RANK].reshape(4, D),
                            jnp.zeros((4, D), F32)], axis=0)
    g_in_gla, (slots,) = _wgrad(du_gla, hn, "wgrad_in_gla", _all_to_all_plan(part))

    pieces = [_wgrad(du_conv, hn, "wgrad_in_conv")[0][None], g_in_gla[None]]
    from_sibling = _exchange(_to_sibling_plan(pieces), "rs_late_to_sibling")
    sums = [_rs_add_halves(g, r, c, "rs_add_w_in_" + nm) for g, r, nm in zip(pieces, from_sibling, ("conv", "gla"))]
    in_chip_sum = jnp.concatenate([sums[0][0], sums[1][0]], axis=0)[:D_IN].reshape(N_CHIPS, D_IN // N_CHIPS, D // 2)
    (dh0, d_mix_g), shared = _in_proj_bwd(
        du_conv, du_gla, w_in_t[:2 * C_CONV], w_in_t[2 * C_CONV:], h0, dh1, vec["norm_mix_g"],
        plan=_merge_plans(_share_plan(halves), _chip_exchange_plan([in_chip_sum])))
    dh0 = dh0.reshape(n_ex, lp, D)
    grad_x = dh0[:, HEAD_ROWS:]
    late_part = jnp.concatenate([d_mix_g, jnp.zeros((SMALL_ROWS - 1, D), F32),
                                 jnp.sum(dh0[:, PAD_ROWS:HEAD_ROWS], axis=0)], axis=0)
    in_half = _rs_sum(in_chip_sum, shared[4], mine, "rs_sum_w_in")
    in_shared, late_slots = _exchange(_merge_plans(_share_plan([in_half]), _all_to_all_plan(late_part)),
                                      "late_exchange")

    out = {"grad": {}, "delta": {}, "new_m": {}, "new_v": {}}

    def record(name, res, transposed=False):
        for kind, a in zip(("grad", "delta", "new_m", "new_v"), res):
            out[kind][name] = (a.T if transposed else a).reshape(ws[name].shape)

    def operands(name, transposed):
        lay = (lambda a: a.T) if transposed else (lambda a: a)
        return lay(shard(ws, name)), lay(shard(ms, name)), lay(shard(vs, name))

    early_layout = (("w_ffn_gate", True), ("w_ffn_up", True), ("w_ffn_down", False), ("w_out", False))
    items = [(mine_half, their_half, *operands(name, transposed))
             for (name, transposed), mine_half, their_half in zip(early_layout, halves, shared)]
    for (name, transposed), res in zip(early_layout, _adamw_halves(items, c, "adamw_early")):
        record(name, res, transposed)

    tile_rows = lambda a: a.reshape(a.shape[0], 1, a.shape[1])
    by_output = lambda d: jnp.transpose(d["w_in"], (2, 0, 1))
    res = _adamw_halves([(tile_rows(in_half), tile_rows(in_shared), by_output(ws), by_output(ms), by_output(vs))],
                        c, "adamw_w_in")[0]
    for kind, a in zip(("grad", "delta", "new_m", "new_v"), res):
        out[kind]["w_in"] = jnp.transpose(a, (1, 2, 0))

    flat = lambda d, name: d[name].reshape(1, -1)
    g_s, updated = _sum_slots_adamw(slots, late_slots,
                                    [(flat(ws, name), flat(ms, name), flat(vs, name)) for name, _, _, _ in SMALL_PARTS])
    for (name, _, _, _), res in zip(SMALL_PARTS, updated):
        record(name, res)
    loss = g_s[LOSS_ROW, 0]
    block = lambda a, width: lax.dynamic_slice_in_dim(a, mine * width, width, axis=1)
    small_sharded = {"meta_tokens": block(g_s[8:24], D // N_CHIPS),
                     "conv_w": block(g_s[24:40].reshape(32, C_CONV), C_CONV // N_CHIPS)[:CONV_W],
                     "gla_w_gate2": block(g_s[40:44].reshape(RANK, GLA_K), GLA_K // N_CHIPS)}
    for name, g in small_sharded.items():
        record(name, [g, *_adamw(g, *operands(name, False), "adamw_" + name)])

    return (loss, grad_x, *[out[kind][name] for kind in ("grad", "delta", "new_m", "new_v") for name in WEIGHT_NAMES])
```

```python
import functools
from typing import Any, Callable, NamedTuple, Sequence

import jax
import jax.numpy as jnp
from jax import lax
from jax.experimental import pallas as pl
from jax.experimental.pallas import tpu as pltpu

F32 = jnp.float32
BF16 = jnp.bfloat16
MESH = pl.DeviceIdType.MESH

D = 1024
N_META = 16
C_CONV = 512
CONV_W = 31
GLA_K = 256
GLA_V = 512
N_HEADS = 4
DK = 64
DV = 128
RANK = 16
CHUNK = 64
PAD_ROWS = CHUNK - N_META
HEAD_ROWS = CHUNK
D_IN = 2576
D_IN_PAD = 2688
D_GLA_IN = D_IN_PAD - 2 * C_CONV
D_FF = 2816
RMS_EPS = 1e-6
LN_EPS = 1e-5
GATE_TAU = 16.0
N_CHIPS = 4

ADAM_LR = 0.001
ADAM_B1 = 0.9
ADAM_B2 = 0.999
ADAM_EPS = 1e-08
ADAM_WD = 0.01
ADAM_STEP = 10

V7X_VMEM_BYTES = 64 * 1024 * 1024
VMEM_LIMIT = V7X_VMEM_BYTES - 8 * 1024 * 1024
SUBLANES = 8
ROW_PART = 128
FFN_BWD_TILE = 192

WEIGHT_NAMES = ("meta_tokens", "norm_mix_g", "w_in", "conv_w", "conv_b", "conv_ln_g", "conv_ln_b", "gla_w_gate2",
                "gla_gate_b", "gla_norm_g", "w_out", "norm_ffn_g", "w_ffn_gate", "w_ffn_up", "w_ffn_down",
                "norm_final_g")

SMALL_ROWS = 8
SMALL_PARTS = (("norm_mix_g", 0, 0, D), ("norm_ffn_g", 1, 0, D), ("norm_final_g", 2, 0, D),
               ("conv_b", 3, 0, C_CONV), ("conv_ln_g", 3, C_CONV, C_CONV), ("conv_ln_b", 4, 0, C_CONV),
               ("gla_gate_b", 4, C_CONV, GLA_K), ("gla_norm_g", 4, C_CONV + GLA_K, DV))
LOSS_ROW = 5

HBM_SPEC = pl.BlockSpec(memory_space=pltpu.HBM)


def _dot(a, b):
    return jnp.dot(a, b, preferred_element_type=F32)


def _dot_nt(a, b):
    return lax.dot_general(a, b, (((1,), (1,)), ((), ())), preferred_element_type=F32)


def _dot_tn(a, b):
    return lax.dot_general(a, b, (((0,), (0,)), ((), ())), preferred_element_type=F32)


def _sigmoid(x):
    return 1.0 / (1.0 + jnp.exp(-x))


def _const_spec(shape):
    return pl.BlockSpec(shape, lambda *_: (0,) * len(shape), pipeline_mode=pl.Buffered(1))


def _acc_spec(shape):
    return pl.BlockSpec(shape, lambda *_: (0,) * len(shape))


def _params(n_axes):
    return pltpu.CompilerParams(dimension_semantics=("arbitrary",) * n_axes, vmem_limit_bytes=VMEM_LIMIT)


def _row_tile(t, want):
    for r in (want, 384, 192, 128, 64):
        if r <= want and t % r == 0:
            return r
    raise ValueError(f"no row tile for {t}")


def _row_parts(r):
    if r % ROW_PART:
        return [slice(None)]
    return [pl.ds(i * ROW_PART, ROW_PART) for i in range(r // ROW_PART)]


def _in_lockstep(bodies):
    live = list(bodies)
    while live:
        still = []
        for g in live:
            try:
                next(g)
                still.append(g)
            except StopIteration:
                pass
        live = still


class _Plan(NamedTuple):
    arrays: Sequence[Any]
    out_shape: Sequence[Any]
    sems: Sequence[Any]
    make: Callable


def _phases(made):
    return made if len(made) == 3 else (made[0], lambda: None, made[1])


def _call(body, *, name, grid, in_specs, out_specs, out_shape, scratch_shapes=(), plan=None):
    n_in, n_out, n_scr = len(in_specs), len(out_specs), len(scratch_shapes)
    if plan is None:
        plan = _Plan([], [], [], lambda ins, outs, sems: (lambda: None, lambda: None))
    nx_in, nx_out = len(plan.arrays), len(plan.out_shape)
    n_steps = functools.reduce(lambda a, b: a * b, grid)

    def hosted(*refs):
        ins, xins = refs[:n_in], refs[n_in:n_in + nx_in]
        o0 = n_in + nx_in
        outs, xouts = refs[o0:o0 + n_out], refs[o0 + n_out:o0 + n_out + nx_out]
        s0 = o0 + n_out + nx_out
        scr, sems = refs[s0:s0 + n_scr], refs[s0 + n_scr:]
        step = functools.reduce(lambda acc, a: acc * grid[a] + pl.program_id(a), range(len(grid)), 0)
        start, relay, finish = _phases(plan.make(xins, xouts, sems))
        pl.when(step == 0)(start)
        pl.when(step == n_steps - 1)(relay)
        body(*ins, *outs, *scr)
        pl.when(step == n_steps - 1)(finish)

    call = pl.pallas_call(
        hosted, name=name, grid=grid, in_specs=list(in_specs) + [HBM_SPEC] * nx_in,
        out_specs=list(out_specs) + [HBM_SPEC] * nx_out, out_shape=list(out_shape) + list(plan.out_shape),
        scratch_shapes=list(scratch_shapes) + list(plan.sems),
        compiler_params=pltpu.CompilerParams(dimension_semantics=("arbitrary",) * len(grid),
                                             vmem_limit_bytes=VMEM_LIMIT, has_side_effects=nx_in > 0))

    def run(*args):
        res = call(*args, *plan.arrays)
        return res[:n_out], res[n_out:]

    return run


def _pad_head_rows(arrays, casts, plan=None):
    n_ex, seq, _ = arrays[0].shape
    nc = (HEAD_ROWS + seq) // CHUNK
    n, k = len(arrays), len(casts)

    def body(*refs):
        ins, outs = refs[:n + k], refs[n + k:]
        for a_ref, o_ref in zip(ins[:n], outs[:n]):
            o_ref[...] = jnp.where(pl.program_id(0) > 0, a_ref[...], 0.0)

        @pl.when(pl.program_id(0) == 0)
        def _():
            for a_ref, o_ref in zip(ins[n:], outs[n:]):
                o_ref[...] = a_ref[...].astype(BF16)

    whole = lambda a: pl.BlockSpec(a.shape, lambda i: (0, 0))
    return _call(
        body, name="pad_head_rows", grid=(nc,),
        in_specs=([pl.BlockSpec((n_ex, CHUNK, D), lambda i: (0, jnp.maximum(i - 1, 0), 0))] * n
                  + [_const_spec(a.shape) for a in casts]),
        out_specs=[pl.BlockSpec((n_ex, CHUNK, D), lambda i: (0, i, 0))] * n + [whole(a) for a in casts],
        out_shape=([jax.ShapeDtypeStruct((n_ex, HEAD_ROWS + seq, D), F32)] * n
                   + [jax.ShapeDtypeStruct(a.shape, BF16) for a in casts]),
        plan=plan,
    )(*arrays, *casts)


def _set_meta_rows(h0, meta):
    n_ex = h0.shape[0]

    def body(h_ref, meta_ref, o_ref):
        o_ref[...] = jnp.concatenate(
            [h_ref[:, :PAD_ROWS, :], jnp.broadcast_to(meta_ref[...][None], (n_ex, N_META, D))], axis=1)

    head = pl.BlockSpec((n_ex, HEAD_ROWS, D), lambda i: (0, 0, 0))
    return pl.pallas_call(
        body, name="set_meta_rows", grid=(1,), in_specs=[head, pl.BlockSpec((N_META, D), lambda i: (0, 0))],
        out_specs=head, out_shape=jax.ShapeDtypeStruct(h0.shape, F32), input_output_aliases={0: 0},
        compiler_params=_params(1),
    )(h0, meta)


def _in_proj(h0, g_mix, w_in, plan=None):
    t = h0.shape[0]
    r = _row_tile(t, 384)

    def body(h_ref, g_ref, w_ref, u_ref, hn_ref):
        h = h_ref[...]
        rstd = lax.rsqrt(jnp.mean(h * h, axis=-1, keepdims=True) + RMS_EPS)
        hn = (h * rstd * g_ref[...]).astype(BF16)
        hn_ref[...] = hn
        u_ref[...] = _dot(hn, w_ref[...])

    return _call(
        body, name="in_proj", grid=(t // r,),
        in_specs=[pl.BlockSpec((r, D), lambda i: (i, 0)), _const_spec((1, D)), _const_spec((D, D_IN_PAD))],
        out_specs=[pl.BlockSpec((r, D_IN_PAD), lambda i: (i, 0)), pl.BlockSpec((r, D), lambda i: (i, 0))],
        out_shape=[jax.ShapeDtypeStruct((t, D_IN_PAD), F32), jax.ShapeDtypeStruct((t, D), BF16)],
        plan=plan,
    )(h0, g_mix, w_in)


CONV_TILE = 192
CONV_SUB = 32
CONV_LEAD = CONV_SUB - (CONV_W - 1)


def _shifted_copies(src, dst, r):
    for s in range(1, SUBLANES):
        dst[s - 1] = src[s:s + r + CONV_SUB - SUBLANES, :]


def _shifted_rows(src, shifted, start):
    base, s = SUBLANES * (start // SUBLANES), start % SUBLANES
    if s == 0:
        return src[base:base + CONV_SUB, :]
    return shifted[s - 1, base:base + CONV_SUB, :]


def _conv_fwd(u, conv_w, conv_b, ln_g, ln_b, n_ex, lp, plan=None):
    r = CONV_TILE
    nt = lp // r
    hb = r // CONV_SUB

    def body(cur_ref, prev_ref, w_ref, b_ref, lg_ref, lb_ref, yc_ref, y_ref, glu, glu_sh):
        i = pl.program_id(1)
        cur = cur_ref[...]
        glu[CONV_SUB:CONV_SUB + r, :] = cur[:, :C_CONV] * _sigmoid(cur[:, C_CONV:])
        pv = prev_ref[...]
        halo = pv[:, :C_CONV] * _sigmoid(pv[:, C_CONV:])
        glu[0:CONV_SUB, :] = jnp.where(i > 0, halo, 0.0)
        _shifted_copies(glu, glu_sh, r)
        w = w_ref[...]
        for j in range(r // CONV_SUB):
            r0 = j * CONV_SUB
            acc = jnp.zeros((CONV_SUB, C_CONV), F32) + b_ref[...]
            for k in range(CONV_W):
                acc = acc + w[k:k + 1, :] * _shifted_rows(glu, glu_sh, r0 + CONV_LEAD + k)
            mu = jnp.mean(acc, axis=-1, keepdims=True)
            cen = acc - mu
            var = jnp.mean(cen * cen, axis=-1, keepdims=True)
            out = cen * lax.rsqrt(var + LN_EPS) * lg_ref[...] + lb_ref[...]
            y = out * _sigmoid(out)
            row = i * r + r0 + lax.broadcasted_iota(jnp.int32, (CONV_SUB, 1), 0)
            y = jnp.where(row >= PAD_ROWS, y, 0.0)
            yc_ref[r0:r0 + CONV_SUB, :] = acc
            y_ref[r0:r0 + CONV_SUB, :] = y.astype(BF16)

    t = n_ex * lp
    return _call(
        body, name="conv_fwd", grid=(n_ex, nt),
        in_specs=[pl.BlockSpec((r, 2 * C_CONV), lambda b, i: (b * nt + i, 0)),
                  pl.BlockSpec((CONV_SUB, 2 * C_CONV), lambda b, i: (jnp.maximum((b * nt + i) * hb - 1, 0), 0)),
                  _const_spec((32, C_CONV)), _const_spec((1, C_CONV)), _const_spec((1, C_CONV)), _const_spec((1, C_CONV))],
        out_specs=[pl.BlockSpec((r, C_CONV), lambda b, i: (b * nt + i, 0)),
                   pl.BlockSpec((r, C_CONV), lambda b, i: (b * nt + i, 0))],
        out_shape=[jax.ShapeDtypeStruct((t, C_CONV), F32), jax.ShapeDtypeStruct((t, C_CONV), BF16)],
        scratch_shapes=[pltpu.VMEM((r + CONV_SUB, C_CONV), F32),
                        pltpu.VMEM((SUBLANES - 1, r + CONV_SUB - SUBLANES, C_CONV), F32)],
        plan=plan,
    )(u, u, conv_w, conv_b, ln_g, ln_b)


def _mix_out_ffn_up(h0, y_conv, y_gla, w_out, g_ffn, w_gate, w_up, plan=None):
    t = h0.shape[0]
    r = _row_tile(t, 384)

    def body(h0_ref, yc_ref, yg_ref, wo_ref, g_ref, wg_ref, wu_ref, h1_ref, hn_ref, gate_ref, up_ref, act_ref):
        h1 = h0_ref[...] + _dot(yc_ref[...], wo_ref[0:C_CONV, :]) + _dot(yg_ref[...], wo_ref[C_CONV:D, :])
        h1_ref[...] = h1
        rstd = lax.rsqrt(jnp.mean(h1 * h1, axis=-1, keepdims=True) + RMS_EPS)
        hn = (h1 * rstd * g_ref[...]).astype(BF16)
        hn_ref[...] = hn
        gate = _dot(hn, wg_ref[...])
        up = _dot(hn, wu_ref[...])
        gate_ref[...] = gate
        up_ref[...] = up
        act_ref[...] = (gate * _sigmoid(gate) * up).astype(BF16)

    rows = lambda w: pl.BlockSpec((r, w), lambda i: (i, 0))
    return _call(
        body, name="mix_out_ffn_up", grid=(t // r,),
        in_specs=[rows(D), rows(C_CONV), rows(GLA_V), _const_spec((D, D)), _const_spec((1, D)),
                  _const_spec((D, D_FF)), _const_spec((D, D_FF))],
        out_specs=[rows(D), rows(D), rows(D_FF), rows(D_FF), rows(D_FF)],
        out_shape=[jax.ShapeDtypeStruct((t, D), F32), jax.ShapeDtypeStruct((t, D), BF16),
                   jax.ShapeDtypeStruct((t, D_FF), F32), jax.ShapeDtypeStruct((t, D_FF), F32),
                   jax.ShapeDtypeStruct((t, D_FF), BF16)],
        plan=plan,
    )(h0, y_conv, y_gla, w_out, g_ffn, w_gate, w_up)


def _ffn_down_loss(act, w_down, h1, target, g_final, row_mask):
    t = h1.shape[0]
    r = _row_tile(t, 384)

    def body(act_ref, wd_ref, h1_ref, tgt_ref, gf_ref, mask_ref, dh2_ref, loss_ref, dgf_ref):
        @pl.when(pl.program_id(0) == 0)
        def _():
            loss_ref[...] = jnp.zeros_like(loss_ref)
            dgf_ref[...] = jnp.zeros_like(dgf_ref)

        gf = gf_ref[...]

        def part(rows):
            h2 = h1_ref[rows, :] + _dot(act_ref[rows, :], wd_ref[...])
            yield
            rstd = lax.rsqrt(jnp.mean(h2 * h2, axis=-1, keepdims=True) + RMS_EPS)
            nrm = h2 * rstd
            err = (nrm * gf - tgt_ref[rows, :]) * mask_ref[rows, :]
            loss_ref[...] += jnp.sum(err * err) * (0.5 / D)
            dy = err * (1.0 / D)
            dgf_ref[...] += jnp.sum(dy * nrm, axis=0, keepdims=True)
            dn = dy * gf
            dh2_ref[rows, :] = rstd * (dn - nrm * jnp.mean(dn * nrm, axis=-1, keepdims=True))

        _in_lockstep(part(rows) for rows in _row_parts(r))

    rows = lambda w: pl.BlockSpec((r, w), lambda i: (i, 0))
    return pl.pallas_call(
        body, name="ffn_down_loss", grid=(t // r,),
        in_specs=[rows(D_FF), _const_spec((D_FF, D)), rows(D), rows(D), _const_spec((1, D)), rows(1)],
        out_specs=[rows(D), _acc_spec((1, 128)), _acc_spec((1, D))],
        out_shape=[jax.ShapeDtypeStruct((t, D), F32), jax.ShapeDtypeStruct((1, 128), F32),
                   jax.ShapeDtypeStruct((1, D), F32)],
        compiler_params=_params(1),
    )(act, w_down, h1, target, g_final, row_mask)


def _ffn_bwd(dh2, gate, up, h1, w_down_t, w_gate_t, w_up_t, w_out_t, g_ffn):
    t = h1.shape[0]
    r = _row_tile(t, FFN_BWD_TILE)

    def body(dh2_ref, gate_ref, up_ref, h1_ref, wd_ref, wg_ref, wu_ref, wo_ref, g_ref,
             dgate_ref, dup_ref, dh1_ref, dycat_ref, dg_ref):
        @pl.when(pl.program_id(0) == 0)
        def _():
            dg_ref[...] = jnp.zeros_like(dg_ref)

        dh2 = dh2_ref[...]
        dact = _dot(dh2.astype(BF16), wd_ref[...])
        gate = gate_ref[...]
        sg = _sigmoid(gate)
        dgate = (dact * up_ref[...] * (sg * (1.0 + gate * (1.0 - sg)))).astype(BF16)
        dup = (dact * (gate * sg)).astype(BF16)
        dgate_ref[...] = dgate
        dup_ref[...] = dup
        dhn = _dot(dgate, wg_ref[...]) + _dot(dup, wu_ref[...])
        h1 = h1_ref[...]
        rstd = lax.rsqrt(jnp.mean(h1 * h1, axis=-1, keepdims=True) + RMS_EPS)
        nrm = h1 * rstd
        dg_ref[...] += jnp.sum(dhn * nrm, axis=0, keepdims=True)
        dn = dhn * g_ref[...]
        dh1 = dh2 + rstd * (dn - nrm * jnp.mean(dn * nrm, axis=-1, keepdims=True))
        dh1_ref[...] = dh1
        dycat_ref[...] = _dot(dh1.astype(BF16), wo_ref[...])

    rows = lambda w: pl.BlockSpec((r, w), lambda i: (i, 0))
    return pl.pallas_call(
        body, name="ffn_bwd", grid=(t // r,),
        in_specs=[rows(D), rows(D_FF), rows(D_FF), rows(D), _const_spec((D, D_FF)), _const_spec((D_FF, D)),
                  _const_spec((D_FF, D)), _const_spec((D, D)), _const_spec((1, D))],
        out_specs=[rows(D_FF), rows(D_FF), rows(D), rows(D), _acc_spec((1, D))],
        out_shape=[jax.ShapeDtypeStruct((t, D_FF), BF16), jax.ShapeDtypeStruct((t, D_FF), BF16),
                   jax.ShapeDtypeStruct((t, D), F32), jax.ShapeDtypeStruct((t, D), F32),
                   jax.ShapeDtypeStruct((1, D), F32)],
        compiler_params=_params(1),
    )(dh2, gate, up, h1, w_down_t, w_gate_t, w_up_t, w_out_t, g_ffn)


def _conv_bwd(dycat, yc, u, conv_w, ln_g, ln_b, n_ex, lp, plan=None):
    r = CONV_TILE
    nt = lp // r
    hb = r // CONV_SUB
    nsub = r // CONV_SUB

    def ln_bwd(dy, yc_rows, live, lg, lb):
        mu = jnp.mean(yc_rows, axis=-1, keepdims=True)
        cen = yc_rows - mu
        rs = lax.rsqrt(jnp.mean(cen * cen, axis=-1, keepdims=True) + LN_EPS)
        yn = cen * rs
        out = yn * lg + lb
        so = _sigmoid(out)
        dout = jnp.where(live, dy * (so * (1.0 + out * (1.0 - so))), 0.0)
        dyn = dout * lg
        dyc = rs * (dyn - jnp.mean(dyn, axis=-1, keepdims=True) - yn * jnp.mean(dyn * yn, axis=-1, keepdims=True))
        return dyc, dout, yn

    def body(dy_ref, dyn_ref, yc_ref, ycn_ref, cur_ref, prev_ref, w_ref, lg_ref, lb_ref,
             du_ref, dw_ref, db_ref, dlg_ref, dlb_ref, glu, dycs, dwacc, glu_sh, dycs_sh):
        b = pl.program_id(0)
        i = pl.program_id(1)
        first = jnp.logical_and(b == 0, i == 0)

        @pl.when(first)
        def _():
            dwacc[...] = jnp.zeros_like(dwacc)
            db_ref[...] = jnp.zeros_like(db_ref)
            dlg_ref[...] = jnp.zeros_like(dlg_ref)
            dlb_ref[...] = jnp.zeros_like(dlb_ref)

        lg, lb = lg_ref[...], lb_ref[...]
        cur = cur_ref[...]
        sig = _sigmoid(cur[:, C_CONV:])
        glu[CONV_SUB:CONV_SUB + r, :] = cur[:, :C_CONV] * sig
        pv = prev_ref[...]
        glu[0:CONV_SUB, :] = jnp.where(i > 0, pv[:, :C_CONV] * _sigmoid(pv[:, C_CONV:]), 0.0)

        row = i * r + lax.broadcasted_iota(jnp.int32, (r, 1), 0)
        dyc, dout, yn = ln_bwd(dy_ref[...], yc_ref[...], row >= PAD_ROWS, lg, lb)
        dycs[0:r, :] = dyc
        dycn, _, _ = ln_bwd(dyn_ref[...], ycn_ref[...], i < nt - 1, lg, lb)
        dycs[r:r + CONV_SUB, :] = dycn
        db_ref[...] += jnp.sum(dyc, axis=0, keepdims=True)
        dlg_ref[...] += jnp.sum(dout * yn, axis=0, keepdims=True)
        dlb_ref[...] += jnp.sum(dout, axis=0, keepdims=True)

        _shifted_copies(glu, glu_sh, r)
        _shifted_copies(dycs, dycs_sh, r)
        w = w_ref[...]
        for j in range(nsub):
            r0 = j * CONV_SUB
            dblk = dycs[r0:r0 + CONV_SUB, :]
            dglu = jnp.zeros((CONV_SUB, C_CONV), F32)
            for k in range(CONV_W):
                dglu = dglu + w[k:k + 1, :] * _shifted_rows(dycs, dycs_sh, r0 + (CONV_W - 1) - k)
                prod = dblk * _shifted_rows(glu, glu_sh, r0 + CONV_LEAD + k)
                dwacc[k] += prod.reshape(CONV_SUB // SUBLANES, SUBLANES, C_CONV).sum(axis=0)
            sg = sig[r0:r0 + CONV_SUB, :]
            cv = cur[r0:r0 + CONV_SUB, :C_CONV]
            du_ref[r0:r0 + CONV_SUB, :C_CONV] = (dglu * sg).astype(BF16)
            du_ref[r0:r0 + CONV_SUB, C_CONV:] = (dglu * cv * sg * (1.0 - sg)).astype(BF16)

        @pl.when(jnp.logical_and(b == n_ex - 1, i == nt - 1))
        def _():
            dw_ref[...] = jnp.sum(dwacc[...], axis=1)

    t = n_ex * lp
    cur_rows = lambda w, col: pl.BlockSpec((r, w), lambda b, i: (b * nt + i, col))
    nxt_rows = lambda w, col: pl.BlockSpec(
        (CONV_SUB, w), lambda b, i: (jnp.minimum((b * nt + i + 1) * hb, n_ex * nt * hb - 1), col))
    return _call(
        body, name="conv_bwd", grid=(n_ex, nt),
        in_specs=[cur_rows(C_CONV, 0), nxt_rows(C_CONV, 0), cur_rows(C_CONV, 0), nxt_rows(C_CONV, 0),
                  cur_rows(2 * C_CONV, 0),
                  pl.BlockSpec((CONV_SUB, 2 * C_CONV), lambda b, i: (jnp.maximum((b * nt + i) * hb - 1, 0), 0)),
                  _const_spec((32, C_CONV)), _const_spec((1, C_CONV)), _const_spec((1, C_CONV))],
        out_specs=[cur_rows(2 * C_CONV, 0), _acc_spec((32, C_CONV)), _acc_spec((1, C_CONV)),
                   _acc_spec((1, C_CONV)), _acc_spec((1, C_CONV))],
        out_shape=[jax.ShapeDtypeStruct((t, 2 * C_CONV), BF16), jax.ShapeDtypeStruct((32, C_CONV), F32),
                   jax.ShapeDtypeStruct((1, C_CONV), F32), jax.ShapeDtypeStruct((1, C_CONV), F32),
                   jax.ShapeDtypeStruct((1, C_CONV), F32)],
        scratch_shapes=[pltpu.VMEM((r + CONV_SUB, C_CONV), F32), pltpu.VMEM((r + CONV_SUB, C_CONV), F32),
                        pltpu.VMEM((32, SUBLANES, C_CONV), F32),
                        pltpu.VMEM((SUBLANES - 1, r + CONV_SUB - SUBLANES, C_CONV), F32),
                        pltpu.VMEM((SUBLANES - 1, r + CONV_SUB - SUBLANES, C_CONV), F32)],
        plan=plan,
    )(dycat, dycat, yc, yc, u, u, conv_w, ln_g, ln_b)


HEAD_ROWS_ALL = N_HEADS * CHUNK


def _gla_gates(lr, w2, gb, first_chunk):
    z = _dot(lr.astype(BF16), w2) + gb
    a = (jnp.minimum(z, 0.0) - jnp.log(1.0 + jnp.exp(-jnp.abs(z)))) * (1.0 / GATE_TAU)
    row = lax.broadcasted_iota(jnp.int32, (CHUNK, 1), 0)
    live = jnp.logical_or(jnp.logical_not(first_chunk), row >= PAD_ROWS)
    return z, jnp.where(live, a, 0.0), live


def _tri(lower):
    i = lax.broadcasted_iota(jnp.int32, (CHUNK, CHUNK), 0)
    j = lax.broadcasted_iota(jnp.int32, (CHUNK, CHUNK), 1)
    return (i >= j) if lower else (i <= j)


def _head_of(shape, axis, per_head):
    return lax.broadcasted_iota(jnp.int32, shape, axis) // per_head


def _expand(x, lanes_per_head):
    rows, lanes = HEAD_ROWS_ALL, x.shape[1]
    keep = _head_of((rows, lanes), 0, CHUNK) == _head_of((rows, lanes), 1, lanes_per_head)
    return jnp.where(keep, jnp.tile(x, (N_HEADS, 1)), 0.0)


def _expand_lanes(x):
    rows, w = x.shape
    keep = _head_of((rows, N_HEADS * w), 0, CHUNK) == _head_of((rows, N_HEADS * w), 1, w)
    return jnp.where(keep, jnp.tile(x, (1, N_HEADS)), 0.0)


def _expand_state(st):
    rows, lanes = N_HEADS * DV, st.shape[1]
    keep = _head_of((rows, lanes), 0, DV) == _head_of((rows, lanes), 1, DK)
    return jnp.where(keep, jnp.tile(st, (N_HEADS, 1)), 0.0)


def _fold(t, rows_per_head):
    lane_head = _head_of((rows_per_head, t.shape[1]), 1, DK)
    out = jnp.where(lane_head == 0, t[0:rows_per_head], 0.0)
    for h in range(1, N_HEADS):
        out = out + jnp.where(lane_head == h, t[h * rows_per_head:(h + 1) * rows_per_head], 0.0)
    return out


def _rows_by_head(x):
    return jnp.concatenate([x[:, h * DV:(h + 1) * DV] for h in range(N_HEADS)], axis=0)


def _lanes_by_head(x):
    return jnp.concatenate([x[h * CHUNK:(h + 1) * CHUNK] for h in range(N_HEADS)], axis=1)


def _running_sum(a, lower):
    hi = a.astype(BF16)
    rest = a - hi.astype(F32)
    mid = rest.astype(BF16)
    lo = (rest - mid.astype(F32)).astype(BF16)
    w = a.shape[1]
    parts = _dot(_tri(lower).astype(F32).astype(BF16), jnp.concatenate([hi, mid, lo], axis=1))
    return parts[:, :w] + parts[:, w:2 * w] + parts[:, 2 * w:]


def _stacked_causal():
    i = lax.broadcasted_iota(jnp.int32, (HEAD_ROWS_ALL, CHUNK), 0) % CHUNK
    j = lax.broadcasted_iota(jnp.int32, (HEAD_ROWS_ALL, CHUNK), 1)
    return i >= j


GLA_GROUP = 3


def _gla_chunk(q, k, v, lr, w2, gb, first_chunk):
    z, a, live = _gla_gates(lr, w2, gb, first_chunk)
    yield
    b = _running_sum(a, True)
    yield
    bl = b[CHUNK - 1:CHUNK, :]
    e_pos, e_neg, e_dec = jnp.exp(b), jnp.exp(-b), jnp.exp(bl - b)
    q_f, k_f, kd_f = q * (DK ** -0.5) * e_pos, k * e_neg, k * e_dec
    qx = _expand(q_f, DK).astype(BF16)
    k_in, k_dec, v_b = k_f.astype(BF16), kd_f.astype(BF16), v.astype(BF16)
    s = jnp.where(_stacked_causal(), _dot_nt(qx, k_in), 0.0).astype(BF16)
    yield
    p = _dot(s, v_b)
    yield
    o_intra = jnp.concatenate([p[h * CHUNK:(h + 1) * CHUNK, h * DV:(h + 1) * DV] for h in range(N_HEADS)], axis=0)
    return dict(z=z, live=live, bl=bl, e_pos=e_pos, e_neg=e_neg, e_dec=e_dec, q_f=q_f, k_f=k_f, kd_f=kd_f,
                qx=qx, k_in=k_in, k_dec=k_dec, v_b=v_b, s=s, o_intra=o_intra, decay=jnp.exp(bl))


def _gla_fwd(u, w2, gb, ng, n_ex, lp, plan=None):
    nc = lp // CHUNK
    t = n_ex * lp
    rows_of = lambda j: pl.ds(j * CHUNK, CHUNK)

    def body(qk_ref, v_ref, g_ref, lr_ref, w2_ref, gb_ref, ng_ref, y_ref, st_ref, state):
        n = pl.program_id(0)

        @pl.when(n == 0)
        def _():
            state[...] = jnp.zeros_like(state)

        carried = [state[e] for e in range(n_ex)]

        def one_chunk(e, j):
            rows = rows_of(j)
            qk = qk_ref[e, rows, :]
            first = jnp.logical_and(n == 0, j == 0)
            c = yield from _gla_chunk(qk[:, :GLA_K], qk[:, GLA_K:], v_ref[e, rows, :], lr_ref[e, rows, :],
                                      w2_ref[...], gb_ref[...], first)
            kv = _fold(_dot_tn(c["v_b"], c["k_dec"]), DV)
            g = _rows_by_head(g_ref[e, rows, :])
            gate = ng_ref[...] * (g * _sigmoid(g))
            yield
            for _ in range(j):
                yield
            st = carried[e]
            st_ref[e, pl.ds(j * DV, DV), :] = st
            o = c["o_intra"] + _dot_nt(c["qx"], st.astype(BF16))
            rstd = lax.rsqrt(jnp.mean(o * o, axis=-1, keepdims=True) + RMS_EPS)
            y_ref[e, rows, :] = _lanes_by_head(o * rstd * gate).astype(BF16)
            carried[e] = c["decay"] * st + kv

        _in_lockstep(one_chunk(e, j) for j in range(GLA_GROUP) for e in range(n_ex))
        for e in range(n_ex):
            state[e] = carried[e]

    u3 = u.reshape(n_ex, lp, D_IN_PAD)
    blk = lambda w, col: pl.BlockSpec((n_ex, GLA_GROUP * CHUNK, w), lambda n: (0, n, col))
    (y, states), extra = _call(
        body, name="gla_fwd", grid=(nc // GLA_GROUP,),
        in_specs=[blk(2 * GLA_K, 2), blk(GLA_V, 3), blk(GLA_V, 4), blk(128, 20),
                  _const_spec((128, GLA_K)), _const_spec((1, GLA_K)), _const_spec((1, DV))],
        out_specs=[blk(GLA_V, 0), pl.BlockSpec((n_ex, GLA_GROUP * DV, GLA_K), lambda n: (0, n, 0))],
        out_shape=[jax.ShapeDtypeStruct((n_ex, lp, GLA_V), BF16),
                   jax.ShapeDtypeStruct((n_ex, nc * DV, GLA_K), F32)],
        scratch_shapes=[pltpu.VMEM((n_ex, DV, GLA_K), F32)],
        plan=plan,
    )(u3, u3, u3, u3, w2, gb, ng)
    return (y.reshape(t, GLA_V), states), extra


def _gla_bwd(dycat, u, states, w2, gb, ng, n_ex, lp, plan=None):
    nc = lp // CHUNK
    t = n_ex * lp

    def body(dy_ref, qk_ref, v_ref, g_ref, lr_ref, st_ref, w2_ref, gb_ref, ng_ref,
             du_ref, dw2_ref, dgb_ref, dng_ref, dstate):
        n = pl.program_id(0)
        group = nc // GLA_GROUP - 1 - n

        @pl.when(n == 0)
        def _():
            dw2_ref[...] = jnp.zeros_like(dw2_ref)
            dgb_ref[...] = jnp.zeros_like(dgb_ref)
            dng_ref[...] = jnp.zeros_like(dng_ref)
            dstate[...] = jnp.zeros_like(dstate)

        carried = [dstate[e] for e in range(n_ex)]

        def one_chunk(e, order):
            j = GLA_GROUP - 1 - order
            rows = pl.ds(j * CHUNK, CHUNK)
            qk = qk_ref[e, rows, :]
            lr = lr_ref[e, rows, :]
            st = st_ref[e, pl.ds(j * DV, DV), :]
            first = jnp.logical_and(group == 0, j == 0)
            c = yield from _gla_chunk(qk[:, :GLA_K], qk[:, GLA_K:], v_ref[e, rows, :], lr, w2_ref[...], gb_ref[...],
                                      first)
            qx, k_in, k_dec, v_b, s = c["qx"], c["k_in"], c["k_dec"], c["v_b"], c["s"]
            st_b = st.astype(BF16)
            o = c["o_intra"] + _dot_nt(qx, st_b)
            ngv = ng_ref[...]
            yield
            rstd = lax.rsqrt(jnp.mean(o * o, axis=-1, keepdims=True) + RMS_EPS)
            nrm = o * rstd
            g = _rows_by_head(g_ref[e, rows, :])
            dy = _rows_by_head(dy_ref[e, rows, :])
            sg = _sigmoid(g)
            dg = dy * nrm * ngv * (sg * (1.0 + g * (1.0 - sg)))
            dt = dy * (g * sg)
            dng_ref[...] += jnp.sum(dt * nrm, axis=0, keepdims=True)
            dn = dt * ngv
            do = rstd * (dn - nrm * jnp.mean(dn * nrm, axis=-1, keepdims=True))
            do_b = do.astype(BF16)
            dox = _expand_lanes(do).astype(BF16)
            yield
            da = jnp.where(_stacked_causal(), _dot_nt(dox, v_b), 0.0).astype(BF16)
            dv_intra = _dot_tn(s, dox)
            dst_own = _dot_tn(do_b, qx)
            yield
            dq_in = _fold(_dot(da, k_in) + _dot(do_b, st_b), CHUNK)
            dk_in = _dot_tn(da, qx)
            dq = dq_in * (DK ** -0.5) * c["e_pos"]
            yield
            for _ in range(order):
                yield
            dst = carried[e]
            dstx = _expand_state(dst).astype(BF16)
            dv = dv_intra + _dot_nt(k_dec, dstx)
            dk_dec = _dot(v_b, dstx)
            carried[e] = dst_own + c["decay"] * dst
            yield
            dbl = (jnp.sum(dk_dec * c["kd_f"], axis=0, keepdims=True)
                   + c["decay"] * jnp.sum(dst * st, axis=0, keepdims=True))
            dk = dk_in * c["e_neg"] + dk_dec * c["e_dec"]
            db = dq_in * c["q_f"] - dk_in * c["k_f"] - dk_dec * c["kd_f"]
            row = lax.broadcasted_iota(jnp.int32, (CHUNK, 1), 0)
            da_log = _running_sum(db + jnp.where(row == CHUNK - 1, dbl, 0.0), False)
            yield
            dz = jnp.where(c["live"], da_log * (1.0 - _sigmoid(c["z"])) * (1.0 / GATE_TAU), 0.0)
            dz_b = dz.astype(BF16)
            out = du_ref.at[e, rows, :]
            out[:, 0:GLA_K] = dq.astype(BF16)
            out[:, GLA_K:2 * GLA_K] = dk.astype(BF16)
            out[:, 2 * GLA_K:2 * GLA_K + GLA_V] = dv.astype(BF16)
            out[:, 2 * GLA_K + GLA_V:2 * GLA_K + 2 * GLA_V] = _lanes_by_head(dg).astype(BF16)
            out[:, 2 * GLA_K + 2 * GLA_V:] = _dot_nt(dz_b, w2_ref[...]).astype(BF16)
            dw2_ref[...] += _dot_tn(lr.astype(BF16), dz_b)
            dgb_ref[...] += jnp.sum(dz, axis=0, keepdims=True)

        _in_lockstep(one_chunk(e, order) for order in range(GLA_GROUP) for e in range(n_ex))
        for e in range(n_ex):
            dstate[e] = carried[e]

    u3 = u.reshape(n_ex, lp, D_IN_PAD)
    rev = lambda w, col: pl.BlockSpec((n_ex, GLA_GROUP * CHUNK, w), lambda n: (0, nc // GLA_GROUP - 1 - n, col))
    (du, d_w2, d_gb, d_ng), extra = _call(
        body, name="gla_bwd", grid=(nc // GLA_GROUP,),
        in_specs=[rev(GLA_V, 1), rev(2 * GLA_K, 2), rev(GLA_V, 3), rev(GLA_V, 4), rev(128, 20),
                  pl.BlockSpec((n_ex, GLA_GROUP * DV, GLA_K), lambda n: (0, nc // GLA_GROUP - 1 - n, 0)),
                  _const_spec((128, GLA_K)), _const_spec((1, GLA_K)), _const_spec((1, DV))],
        out_specs=[rev(D_GLA_IN, 0), _acc_spec((128, GLA_K)), _acc_spec((1, GLA_K)), _acc_spec((1, DV))],
        out_shape=[jax.ShapeDtypeStruct((n_ex, lp, D_GLA_IN), BF16), jax.ShapeDtypeStruct((128, GLA_K), F32),
                   jax.ShapeDtypeStruct((1, GLA_K), F32), jax.ShapeDtypeStruct((1, DV), F32)],
        scratch_shapes=[pltpu.VMEM((n_ex, DV, GLA_K), F32)],
        plan=plan,
    )(dycat.reshape(n_ex, lp, D), u3, u3, u3, u3, states, w2, gb, ng)
    return (du.reshape(t, D_GLA_IN), d_w2, d_gb, d_ng), extra


def _in_proj_bwd(du_conv, du_gla, w_in_t_conv, w_in_t_gla, h0, dh1, g_mix, plan=None):
    t = h0.shape[0]
    r = _row_tile(t, 384)

    def body(dc_ref, dg_ref, wc_ref, wg_ref, h_ref, dh1_ref, g_ref, dh0_ref, dgm_ref):
        @pl.when(pl.program_id(0) == 0)
        def _():
            dgm_ref[...] = jnp.zeros_like(dgm_ref)

        dhn = _dot(dc_ref[...], wc_ref[...]) + _dot(dg_ref[...], wg_ref[...])
        h = h_ref[...]
        rstd = lax.rsqrt(jnp.mean(h * h, axis=-1, keepdims=True) + RMS_EPS)
        nrm = h * rstd
        dgm_ref[...] += jnp.sum(dhn * nrm, axis=0, keepdims=True)
        dn = dhn * g_ref[...]
        dh0_ref[...] = dh1_ref[...] + rstd * (dn - nrm * jnp.mean(dn * nrm, axis=-1, keepdims=True))

    rows = lambda w: pl.BlockSpec((r, w), lambda i: (i, 0))
    return _call(
        body, name="in_proj_bwd", grid=(t // r,),
        in_specs=[rows(2 * C_CONV), rows(D_GLA_IN), _const_spec((2 * C_CONV, D)), _const_spec((D_GLA_IN, D)),
                  rows(D), rows(D), _const_spec((1, D))],
        out_specs=[rows(D), _acc_spec((1, D))],
        out_shape=[jax.ShapeDtypeStruct((t, D), F32), jax.ShapeDtypeStruct((1, D), F32)],
        plan=plan,
    )(du_conv, du_gla, w_in_t_conv, w_in_t_gla, h0, dh1, g_mix)


def _wgrad(x, dy, name, plan=None):
    t, m = x.shape
    n = dy.shape[1]
    tk = t // 3 if t % (3 * 128) == 0 else _row_tile(t, 384)
    tm = m if m <= D_GLA_IN else m // 2

    def body(x_ref, dy_ref, o_ref):
        @pl.when(pl.program_id(1) == 0)
        def _():
            o_ref[...] = jnp.zeros_like(o_ref)

        o_ref[...] += _dot_tn(x_ref[...].astype(BF16), dy_ref[...].astype(BF16))

    (out,), extra = _call(
        body, name=name, grid=(m // tm, t // tk),
        in_specs=[pl.BlockSpec((tk, tm), lambda i, k: (k, i)), pl.BlockSpec((tk, n), lambda i, k: (k, 0))],
        out_specs=[pl.BlockSpec((tm, n), lambda i, k: (i, 0))],
        out_shape=[jax.ShapeDtypeStruct((m, n), F32)],
        plan=plan,
    )(x, dy)
    return out, extra


def _wgrad_pair(xa, xb, dy, name):
    t, m = xa.shape
    n = dy.shape[1]
    tk = t // 3 if t % (3 * 128) == 0 else _row_tile(t, 384)

    def body(xa_ref, xb_ref, dy_ref, o_ref):
        @pl.when(pl.program_id(1) == 0)
        def _():
            o_ref[...] = jnp.zeros_like(o_ref)

        x = jnp.where(pl.program_id(0) == 0, xa_ref[...], xb_ref[...])
        o_ref[...] += _dot_tn(x.astype(BF16), dy_ref[...].astype(BF16))

    rows = lambda w: pl.BlockSpec((tk, w), lambda i, k: (k, 0))
    return pl.pallas_call(
        body, name=name, grid=(2, t // tk), in_specs=[rows(m), rows(m), rows(n)],
        out_specs=pl.BlockSpec((m, n), lambda i, k: (i, 0)),
        out_shape=jax.ShapeDtypeStruct((2 * m, n), F32), compiler_params=_params(2),
    )(xa, xb, dy)


def _adam_update(g, w, m, v):
    m2 = ADAM_B1 * m + (1.0 - ADAM_B1) * g
    v2 = ADAM_B2 * v + (1.0 - ADAM_B2) * (g * g)
    m_hat = m2 / (1.0 - ADAM_B1 ** ADAM_STEP)
    v_hat = v2 / (1.0 - ADAM_B2 ** ADAM_STEP)
    delta = -ADAM_LR * (m_hat / (jnp.sqrt(v_hat) + ADAM_EPS) + ADAM_WD * w)
    return delta, m2, v2


ADAMW_STEPS = 4


def _adamw(g, w, m, v, name):
    rows, cols = g.shape
    steps = ADAMW_STEPS if rows % (ADAMW_STEPS * SUBLANES) == 0 else 1

    def body(g_ref, w_ref, m_ref, v_ref, d_ref, m2_ref, v2_ref):
        d_ref[...], m2_ref[...], v2_ref[...] = _adam_update(g_ref[...], w_ref[...], m_ref[...], v_ref[...])

    spec = pl.BlockSpec((rows // steps, cols), lambda i: (i, 0))
    return pl.pallas_call(
        body, name=name, grid=(steps,), in_specs=[spec] * 4, out_specs=[spec] * 3,
        out_shape=[jax.ShapeDtypeStruct(g.shape, F32)] * 3, compiler_params=_params(1),
    )(g, w, m, v)


def _adamw_halves(items, c, name):
    n = len(items)
    h = items[0][0].shape[-1]
    splits = lambda a: a.shape[0] % (ADAMW_STEPS * (SUBLANES if a.ndim == 2 else 1)) == 0
    steps = ADAMW_STEPS if all(splits(it[0]) for it in items) else 1

    def body(c_ref, *refs):
        ins, outs = refs[:5 * n], refs[5 * n:]
        own = pl.program_id(1) == c_ref[0]
        for i in range(n):
            a_ref, b_ref, w_ref, m_ref, v_ref = ins[5 * i:5 * i + 5]
            go_ref, d_ref, m2_ref, v2_ref = outs[4 * i:4 * i + 4]
            g = jnp.where(own, a_ref[...], b_ref[...])
            go_ref[...] = g
            d_ref[...], m2_ref[...], v2_ref[...] = _adam_update(g, w_ref[...], m_ref[...], v_ref[...])

    in_specs, out_specs, out_shape, args = [pl.BlockSpec(memory_space=pltpu.SMEM)], [], [], []
    for mine, theirs, w, m, v in items:
        tr = mine.shape[0] // steps
        mid = (0,) * (mine.ndim - 2)
        half = pl.BlockSpec((tr,) + mine.shape[1:-1] + (h,), lambda i, j, mid=mid: (i, *mid, 0))
        full = pl.BlockSpec((tr,) + mine.shape[1:-1] + (h,), lambda i, j, mid=mid: (i, *mid, j))
        in_specs += [half, half, full, full, full]
        out_specs += [full] * 4
        out_shape += [jax.ShapeDtypeStruct(w.shape, F32)] * 4
        args += [mine, theirs, w, m, v]
    res = pl.pallas_call(
        body, name=name, grid=(steps, 2), in_specs=in_specs, out_specs=out_specs, out_shape=out_shape,
        compiler_params=_params(2),
    )(jnp.reshape(c, (1,)).astype(jnp.int32), *args)
    return [res[4 * i:4 * i + 4] for i in range(n)]


def _rs_add_halves(pairs, c, name):
    blocks = pairs[0][0].shape[0]
    n = len(pairs)

    def body(c_ref, *refs):
        for i in range(n):
            refs[2 * n + i][...] = (refs[2 * i][...] + refs[2 * i + 1][...]).astype(BF16)

    in_specs, out_specs, out_shape = [], [], []
    for g, _ in pairs:
        _, rows, w = g.shape
        in_specs += [pl.BlockSpec((1, rows, w // 2), lambda j, s: (j, 0, s[0])),
                     pl.BlockSpec((1, rows, w // 2), lambda j, s: (j, 0, 0))]
        out_specs += [pl.BlockSpec((1, rows, w // 2), lambda j, s: (j, 0, 0))]
        out_shape += [jax.ShapeDtypeStruct((blocks, rows, w // 2), BF16)]
    return pl.pallas_call(
        body, name=name,
        grid_spec=pltpu.PrefetchScalarGridSpec(num_scalar_prefetch=1, grid=(blocks,), in_specs=in_specs,
                                               out_specs=out_specs),
        out_shape=out_shape, compiler_params=_params(1),
    )(jnp.reshape(c, (1,)).astype(jnp.int32), *[a for pair in pairs for a in pair])


def _rs_sum(pairs, mine, name):
    n = len(pairs)
    steps = 2 if all(own.shape[1] % (2 * 16) == 0 for own, _ in pairs) else 1

    def body(mine_ref, *refs):
        for i in range(n):
            p = refs[2 * i + 1][...].astype(F32)
            refs[2 * n + i][...] = ((refs[2 * i][0].astype(F32) + p[0]) + p[1]) + p[2]

    in_specs, out_specs, out_shape = [], [], []
    for own, _ in pairs:
        _, rows, h = own.shape
        tr = rows // steps
        in_specs += [pl.BlockSpec((1, tr, h), lambda i, s: (s[0], i, 0)),
                     pl.BlockSpec((3, tr, h), lambda i, s: (0, i, 0))]
        out_specs += [pl.BlockSpec((tr, h), lambda i, s: (i, 0))]
        out_shape += [jax.ShapeDtypeStruct((rows, h), F32)]
    return pl.pallas_call(
        body, name=name,
        grid_spec=pltpu.PrefetchScalarGridSpec(num_scalar_prefetch=1, grid=(steps,), in_specs=in_specs,
                                               out_specs=out_specs),
        out_shape=out_shape, compiler_params=_params(1),
    )(jnp.reshape(mine, (1,)).astype(jnp.int32), *[a for pair in pairs for a in pair])


def _sum_slots_adamw(slots, late_slots, vectors):
    late_rows = late_slots.shape[1]
    n = len(SMALL_PARTS)

    def body(s_ref, l_ref, *refs):
        ins, g_ref, outs = refs[:3 * n], refs[3 * n], refs[3 * n + 1:]
        g, late = s_ref[0], l_ref[0]
        for d in range(1, 8):
            g = g + s_ref[d]
            late = late + l_ref[d]
        g = jnp.concatenate([g[:late_rows] + late, g[late_rows:]], axis=0)
        g_ref[...] = g
        for i, (_, row, col, size) in enumerate(SMALL_PARTS):
            w_ref, m_ref, v_ref = ins[3 * i:3 * i + 3]
            go_ref, d_ref, m2_ref, v2_ref = outs[4 * i:4 * i + 4]
            piece = g[row:row + 1, col:col + size]
            go_ref[...] = piece
            d_ref[...], m2_ref[...], v2_ref[...] = _adam_update(piece, w_ref[...], m_ref[...], v_ref[...])

    vm = pl.BlockSpec(memory_space=pltpu.VMEM)
    out_shape = [jax.ShapeDtypeStruct(slots.shape[1:], F32)]
    for _, _, _, size in SMALL_PARTS:
        out_shape += [jax.ShapeDtypeStruct((1, size), F32)] * 4
    res = pl.pallas_call(body, name="small_sum_adamw", in_specs=[vm] * (2 + 3 * n), out_specs=[vm] * len(out_shape),
                         out_shape=out_shape)(slots, late_slots, *[a for wmv in vectors for a in wmv])
    return res[0], [res[1 + 4 * i:5 + 4 * i] for i in range(n)]


def _mesh_pos():
    return lax.axis_index("x"), lax.axis_index("y"), lax.axis_index("c")


def _other_chips(x, y):
    return [(1 - x, y), (x, 1 - y), (1 - x, 1 - y)]


def _half(ref, c, axis):
    n = ref.shape[axis] // 2
    return ref.at[(slice(None),) * axis + (pl.ds(c * n, n),)]


def _remote(src, dst, send_sem, recv_sem, device):
    return pltpu.make_async_remote_copy(src_ref=src, dst_ref=dst, send_sem=send_sem, recv_sem=recv_sem,
                                        device_id=device, device_id_type=MESH)


def _gather_plan(split, whole=(), axes=None):
    split, whole = list(split), list(whole)
    ns, n = len(split), len(split) + len(whole)

    def make(ins, outs, sems):
        ici_send, ici_recv, d2d_send, d2d_recv, own_send, own_recv = sems
        x, y, c = _mesh_pos()
        mine = 2 * x + y
        chips = _other_chips(x, y)
        blocks = [2 * px + py for px, py in chips]

        def own(a):
            return _remote(ins[a], outs[a].at[mine], own_send.at[a], own_recv.at[a], (x, y, 1 - c))

        def ici(a, k, block):
            px, py = chips[k]
            src, dst = ins[a], outs[a].at[block]
            if a < ns:
                src, dst = _half(src, c, axes[a]), _half(dst, c, axes[a])
            return _remote(src, dst, ici_send.at[3 * a + k], ici_recv.at[3 * a + k], (px, py, c))

        def d2d(a, k, half):
            part = _half(outs[a].at[blocks[k]], half, axes[a])
            return _remote(part, part, d2d_send.at[3 * a + k], d2d_recv.at[3 * a + k], (x, y, 1 - c))

        def start():
            for a in range(n):
                for k in range(3):
                    ici(a, k, mine).start()
                own(a).start()

        def relay():
            for a in range(n):
                for k in range(3):
                    ici(a, k, blocks[k]).wait_recv()
                    if a < ns:
                        d2d(a, k, c).start()

        def finish():
            for a in range(ns):
                for k in range(3):
                    d2d(a, k, 1 - c).wait_recv()
            for a in range(n):
                for k in range(3):
                    ici(a, k, mine).wait_send()
                    if a < ns:
                        d2d(a, k, c).wait_send()
                own(a).wait()

        return start, relay, finish

    arrays = split + whole
    axes = [0] * ns if axes is None else list(axes)
    return _Plan(arrays, [jax.ShapeDtypeStruct((N_CHIPS,) + s.shape, s.dtype) for s in arrays],
                 [pltpu.SemaphoreType.DMA((3 * n,)), pltpu.SemaphoreType.DMA((3 * n,)),
                  pltpu.SemaphoreType.DMA((3 * ns,)), pltpu.SemaphoreType.DMA((3 * ns,)),
                  pltpu.SemaphoreType.DMA((n,)), pltpu.SemaphoreType.DMA((n,))], make)


def _to_sibling_plan(gs):
    n = len(gs)

    def make(ins, outs, sems):
        send_sems, recv_sems = sems
        x, y, c = _mesh_pos()

        def copy(a):
            return _remote(_half(ins[a], 1 - c, len(ins[a].shape) - 1), outs[a], send_sems.at[a],
                           recv_sems.at[a], (x, y, 1 - c))

        def start():
            for a in range(n):
                copy(a).start()

        def finish():
            for a in range(n):
                copy(a).wait()

        return start, finish

    return _Plan(list(gs), [jax.ShapeDtypeStruct(g.shape[:-1] + (g.shape[-1] // 2,), g.dtype) for g in gs],
                 [pltpu.SemaphoreType.DMA((n,)), pltpu.SemaphoreType.DMA((n,))], make)


def _chip_exchange_plan(ps):
    n = len(ps)

    def make(ins, outs, sems):
        send_sems, recv_sems = sems
        x, y, c = _mesh_pos()
        chips = _other_chips(x, y)

        def ici(a, k):
            px, py = chips[k]
            return _remote(ins[a].at[2 * px + py], outs[a].at[k], send_sems.at[3 * a + k],
                           recv_sems.at[3 * a + k], (px, py, c))

        def start():
            for a in range(n):
                for k in range(3):
                    ici(a, k).start()

        def finish():
            for a in range(n):
                for k in range(3):
                    ici(a, k).wait()

        return start, finish

    return _Plan(list(ps), [jax.ShapeDtypeStruct((3,) + p.shape[1:], p.dtype) for p in ps],
                 [pltpu.SemaphoreType.DMA((3 * n,)), pltpu.SemaphoreType.DMA((3 * n,))], make)


def _share_plan(halves):
    n = len(halves)

    def make(ins, outs, sems):
        send_sems, recv_sems = sems
        x, y, c = _mesh_pos()

        def d2d(a):
            return _remote(ins[a], outs[a], send_sems.at[a], recv_sems.at[a], (x, y, 1 - c))

        def start():
            for a in range(n):
                d2d(a).start()

        def finish():
            for a in range(n):
                d2d(a).wait()

        return start, finish

    return _Plan(list(halves), [jax.ShapeDtypeStruct(p.shape, p.dtype) for p in halves],
                 [pltpu.SemaphoreType.DMA((n,)), pltpu.SemaphoreType.DMA((n,))], make)


def _all_to_all_plan(part):
    def make(ins, outs, sems):
        send_sems, recv_sems, local_sem = sems
        (p_ref,), (slots,) = ins, outs
        x, y, c = _mesh_pos()
        me = 4 * x + 2 * y + c
        peers = [(px, py, pc) for px in (x, 1 - x) for py in (y, 1 - y) for pc in (c, 1 - c)][1:]

        def remote(k, slot):
            return _remote(p_ref, slots.at[slot], send_sems.at[k], recv_sems.at[k], peers[k])

        def local():
            return pltpu.make_async_copy(p_ref, slots.at[me], local_sem)

        def start():
            for k in range(7):
                remote(k, me).start()
            local().start()

        def finish():
            for k, (px, py, pc) in enumerate(peers):
                remote(k, 4 * px + 2 * py + pc).wait_recv()
            for k in range(7):
                remote(k, me).wait_send()
            local().wait()

        return start, finish

    return _Plan([part], [jax.ShapeDtypeStruct((8,) + part.shape, part.dtype)],
                 [pltpu.SemaphoreType.DMA((7,)), pltpu.SemaphoreType.DMA((7,)), pltpu.SemaphoreType.DMA(())], make)


def _merge_plans(a, b):
    na_in, na_out, na_sems = len(a.arrays), len(a.out_shape), len(a.sems)

    def make(ins, outs, sems):
        phases_a = _phases(a.make(ins[:na_in], outs[:na_out], sems[:na_sems]))
        phases_b = _phases(b.make(ins[na_in:], outs[na_out:], sems[na_sems:]))

        def both(i):
            def run():
                phases_a[i]()
                phases_b[i]()
            return run

        return both(0), both(1), both(2)

    return _Plan(list(a.arrays) + list(b.arrays), list(a.out_shape) + list(b.out_shape),
                 list(a.sems) + list(b.sems), make)


def _exchange(plan, name):
    n_in, n_out = len(plan.arrays), len(plan.out_shape)

    def body(*refs):
        for phase in _phases(plan.make(refs[:n_in], refs[n_in:n_in + n_out], refs[n_in + n_out:])):
            phase()

    return pl.pallas_call(
        body, name=name, in_specs=[HBM_SPEC] * n_in, out_specs=[HBM_SPEC] * n_out, out_shape=list(plan.out_shape),
        scratch_shapes=list(plan.sems), compiler_params=pltpu.CompilerParams(has_side_effects=True),
    )(*plan.arrays)


def _pack_small(parts):
    rows = []
    for r in range(SMALL_ROWS):
        pieces, col = [], 0
        for name, row, start, size in SMALL_PARTS:
            if row == r:
                assert start == col
                pieces.append(parts[name].reshape(1, size).astype(F32))
                col += size
        rows.append(jnp.concatenate(pieces + [jnp.zeros((1, D - col), F32)], axis=1))
    return jnp.concatenate(rows, axis=0)


def _columns(gathered):
    return jnp.concatenate([gathered[j] for j in range(N_CHIPS)], axis=1)


def kernel(x, meta_tokens, norm_mix_g, w_in, conv_w, conv_b, conv_ln_g, conv_ln_b, gla_w_gate2, gla_gate_b, gla_norm_g, w_out, norm_ffn_g, w_ffn_gate, w_ffn_up, w_ffn_down, norm_final_g, loss_target, m_meta_tokens, m_norm_mix_g, m_w_in, m_conv_w, m_conv_b, m_conv_ln_g, m_conv_ln_b, m_gla_w_gate2, m_gla_gate_b, m_gla_norm_g, m_w_out, m_norm_ffn_g, m_w_ffn_gate, m_w_ffn_up, m_w_ffn_down, m_norm_final_g, v_meta_tokens, v_norm_mix_g, v_w_in, v_conv_w, v_conv_b, v_conv_ln_g, v_conv_ln_b, v_gla_w_gate2, v_gla_gate_b, v_gla_norm_g, v_w_out, v_norm_ffn_g, v_w_ffn_gate, v_w_ffn_up, v_w_ffn_down, v_norm_final_g):
    ws = dict(zip(WEIGHT_NAMES, (meta_tokens, norm_mix_g, w_in, conv_w, conv_b, conv_ln_g, conv_ln_b, gla_w_gate2,
                                 gla_gate_b, gla_norm_g, w_out, norm_ffn_g, w_ffn_gate, w_ffn_up, w_ffn_down,
                                 norm_final_g)))
    ms = dict(zip(WEIGHT_NAMES, (m_meta_tokens, m_norm_mix_g, m_w_in, m_conv_w, m_conv_b, m_conv_ln_g, m_conv_ln_b,
                                 m_gla_w_gate2, m_gla_gate_b, m_gla_norm_g, m_w_out, m_norm_ffn_g, m_w_ffn_gate,
                                 m_w_ffn_up, m_w_ffn_down, m_norm_final_g)))
    vs = dict(zip(WEIGHT_NAMES, (v_meta_tokens, v_norm_mix_g, v_w_in, v_conv_w, v_conv_b, v_conv_ln_g, v_conv_ln_b,
                                 v_gla_w_gate2, v_gla_gate_b, v_gla_norm_g, v_w_out, v_norm_ffn_g, v_w_ffn_gate,
                                 v_w_ffn_up, v_w_ffn_down, v_norm_final_g)))
    c = lax.axis_index("c")
    mine = 2 * lax.axis_index("x") + lax.axis_index("y")
    shard = lambda d, name: d[name].reshape(d[name].shape[-2:])
    vec = {name: ws[name].reshape(1, -1) for name, _, _, _ in SMALL_PARTS}
    n_ex, seq, _ = x.shape
    lp = HEAD_ROWS + seq
    t = n_ex * lp

    (tgt, h0, gate_s, up_s, out_s, down_s), (w_in_g, meta_g, conv_w_g, w2_g) = _pad_head_rows(
        [loss_target, x],
        [shard(ws, "w_ffn_gate").T, shard(ws, "w_ffn_up").T, shard(ws, "w_out"), shard(ws, "w_ffn_down")],
        plan=_gather_plan([shard(ws, "w_in").T.astype(BF16)],
                          [shard(ws, "meta_tokens"), shard(ws, "conv_w"), shard(ws, "gla_w_gate2")], axes=[1]))
    w_in_t = jnp.concatenate([w_in_g.reshape(D_IN, D), jnp.zeros((D_IN_PAD - D_IN, D), BF16)], axis=0)
    conv_w_full = jnp.concatenate([_columns(conv_w_g), jnp.zeros((32 - CONV_W, C_CONV), F32)], axis=0)
    w2_full = jnp.concatenate([_columns(w2_g), jnp.zeros((128 - RANK, GLA_K), F32)], axis=0).astype(BF16)
    h0 = _set_meta_rows(h0, _columns(meta_g)).reshape(t, D)
    tgt = tgt.reshape(t, D)
    row_mask = jnp.concatenate([jnp.zeros((n_ex, HEAD_ROWS, 1), F32), jnp.ones((n_ex, seq, 1), F32)],
                               axis=1).reshape(t, 1)

    (u, hn), (gate_g,) = _in_proj(h0, vec["norm_mix_g"], w_in_t.T, plan=_gather_plan([gate_s]))
    (yc, y_conv), (up_g, w_out_g) = _conv_fwd(
        u, conv_w_full, vec["conv_b"], vec["conv_ln_g"], vec["conv_ln_b"], n_ex, lp,
        plan=_gather_plan([up_s, out_s]))
    (y_gla, states), _ = _gla_fwd(u, w2_full, vec["gla_gate_b"], vec["gla_norm_g"], n_ex, lp)
    w_out_full = w_out_g.reshape(D, D)
    w_gate_t, w_up_t = gate_g.reshape(D_FF, D), up_g.reshape(D_FF, D)
    (h1, hn2, gate, up, act), (down_g,) = _mix_out_ffn_up(
        h0, y_conv, y_gla, w_out_full, vec["norm_ffn_g"], w_gate_t.T, w_up_t.T,
        plan=_gather_plan([down_s]))
    w_down_full = down_g.reshape(D_FF, D)
    dh2, loss, d_final_g = _ffn_down_loss(act, w_down_full, h1, tgt, vec["norm_final_g"], row_mask)
    dgate, dup, dh1, dycat, d_ffn_g = _ffn_bwd(dh2, gate, up, h1, w_down_full.T, w_gate_t, w_up_t, w_out_full.T,
                                                vec["norm_ffn_g"])

    ffn_block = lambda g: g.reshape(N_CHIPS, D_FF // N_CHIPS, D)
    g_gate = ffn_block(_wgrad(dgate, hn2, "wgrad_gate")[0])
    g_up, (gate_sib,) = _wgrad(dup, hn2, "wgrad_up", _to_sibling_plan([g_gate]))
    g_up = ffn_block(g_up)
    g_down, (up_sib,) = _wgrad(act, dh2, "wgrad_down", _to_sibling_plan([g_up]))
    g_down = ffn_block(g_down)
    g_out = _wgrad_pair(y_conv, y_gla, dh1, "wgrad_out").reshape(N_CHIPS, D // N_CHIPS, D)
    cs_gate, cs_up = _rs_add_halves([(g_gate, gate_sib), (g_up, up_sib)], c, "rs_add_gate_up")
    (du_conv, d_conv_w, d_conv_b, d_ln_g, d_ln_b), (ex_gate, ex_up, down_sib, out_sib) = _conv_bwd(
        dycat, yc, u, conv_w_full, vec["conv_ln_g"], vec["conv_ln_b"], n_ex, lp,
        plan=_merge_plans(_chip_exchange_plan([cs_gate, cs_up]), _to_sibling_plan([g_down, g_out])))
    cs_down, cs_out = _rs_add_halves([(g_down, down_sib), (g_out, out_sib)], c, "rs_add_down_out")
    (du_gla, d_w2, d_gate_b, d_norm_g), (ex_down, ex_out) = _gla_bwd(
        dycat, u, states, w2_full, vec["gla_gate_b"], vec["gla_norm_g"], n_ex, lp,
        plan=_chip_exchange_plan([cs_down, cs_out]))
    halves = _rs_sum([(cs_gate, ex_gate), (cs_up, ex_up), (cs_down, ex_down), (cs_out, ex_out)], mine,
                     "rs_sum_early")

    small = {"norm_mix_g": jnp.zeros((1, D), F32), "norm_ffn_g": d_ffn_g, "norm_final_g": d_final_g,
             "conv_b": d_conv_b, "conv_ln_g": d_ln_g, "conv_ln_b": d_ln_b, "gla_gate_b": d_gate_b,
             "gla_norm_g": d_norm_g}
    part = lax.dynamic_update_slice(_pack_small(small), loss[:, :1], (LOSS_ROW, 0))
    part = jnp.concatenate([part, jnp.zeros((N_META, D), F32), d_conv_w.reshape(16, D), d_w2[:RANK].reshape(4, D),
                            jnp.zeros((4, D), F32)], axis=0)
    g_in_gla, (slots,) = _wgrad(du_gla, hn, "wgrad_in_gla", _all_to_all_plan(part))

    pieces = [_wgrad(du_conv, hn, "wgrad_in_conv")[0][None], g_in_gla[None]]
    from_sibling = _exchange(_to_sibling_plan(pieces), "rs_late_to_sibling")
    sums = _rs_add_halves(list(zip(pieces, from_sibling)), c, "rs_add_w_in")
    in_chip_sum = jnp.concatenate([sums[0][0], sums[1][0]], axis=0)[:D_IN].reshape(N_CHIPS, D_IN // N_CHIPS, D // 2)
    (dh0, d_mix_g), shared = _in_proj_bwd(
        du_conv, du_gla, w_in_t[:2 * C_CONV], w_in_t[2 * C_CONV:], h0, dh1, vec["norm_mix_g"],
        plan=_merge_plans(_share_plan(halves), _chip_exchange_plan([in_chip_sum])))
    dh0 = dh0.reshape(n_ex, lp, D)
    grad_x = dh0[:, HEAD_ROWS:]
    late_part = jnp.concatenate([d_mix_g, jnp.zeros((SMALL_ROWS - 1, D), F32),
                                 jnp.sum(dh0[:, PAD_ROWS:HEAD_ROWS], axis=0)], axis=0)
    (in_half,) = _rs_sum([(in_chip_sum, shared[4])], mine, "rs_sum_w_in")
    in_shared, late_slots = _exchange(_merge_plans(_share_plan([in_half]), _all_to_all_plan(late_part)),
                                      "late_exchange")

    out = {"grad": {}, "delta": {}, "new_m": {}, "new_v": {}}

    def record(name, res, transposed=False):
        for kind, a in zip(("grad", "delta", "new_m", "new_v"), res):
            out[kind][name] = (a.T if transposed else a).reshape(ws[name].shape)

    def operands(name, transposed):
        lay = (lambda a: a.T) if transposed else (lambda a: a)
        return lay(shard(ws, name)), lay(shard(ms, name)), lay(shard(vs, name))

    early_layout = (("w_ffn_gate", True), ("w_ffn_up", True), ("w_ffn_down", False), ("w_out", False))
    items = [(mine_half, their_half, *operands(name, transposed))
             for (name, transposed), mine_half, their_half in zip(early_layout, halves, shared)]
    for (name, transposed), res in zip(early_layout, _adamw_halves(items, c, "adamw_early")):
        record(name, res, transposed)

    tile_rows = lambda a: a.reshape(a.shape[0], 1, a.shape[1])
    by_output = lambda d: jnp.transpose(d["w_in"], (2, 0, 1))
    res = _adamw_halves([(tile_rows(in_half), tile_rows(in_shared), by_output(ws), by_output(ms), by_output(vs))],
                        c, "adamw_w_in")[0]
    for kind, a in zip(("grad", "delta", "new_m", "new_v"), res):
        out[kind]["w_in"] = jnp.transpose(a, (1, 2, 0))

    flat = lambda d, name: d[name].reshape(1, -1)
    g_s, updated = _sum_slots_adamw(slots, late_slots,
                                    [(flat(ws, name), flat(ms, name), flat(vs, name)) for name, _, _, _ in SMALL_PARTS])
    for (name, _, _, _), res in zip(SMALL_PARTS, updated):
        record(name, res)
    loss = g_s[LOSS_ROW, 0]
    block = lambda a, width: lax.dynamic_slice_in_dim(a, mine * width, width, axis=1)
    small_sharded = {"meta_tokens": block(g_s[8:24], D // N_CHIPS),
                     "conv_w": block(g_s[24:40].reshape(32, C_CONV), C_CONV // N_CHIPS)[:CONV_W],
                     "gla_w_gate2": block(g_s[40:44].reshape(RANK, GLA_K), GLA_K // N_CHIPS)}
    for name, g in small_sharded.items():
        record(name, [g, *_adamw(g, *operands(name, False), "adamw_" + name)])

    return (loss, grad_x, *[out[kind][name] for kind in ("grad", "delta", "new_m", "new_v") for name in WEIGHT_NAMES])
```

```python
import functools
from typing import Any, Callable, NamedTuple, Sequence

import jax
import jax.numpy as jnp
from jax import lax
from jax.experimental import pallas as pl
from jax.experimental.pallas import tpu as pltpu

F32 = jnp.float32
BF16 = jnp.bfloat16
MESH = pl.DeviceIdType.MESH

D = 1024
N_META = 16
C_CONV = 512
CONV_W = 31
GLA_K = 256
GLA_V = 512
N_HEADS = 4
DK = 64
DV = 128
RANK = 16
CHUNK = 64
PAD_ROWS = CHUNK - N_META
HEAD_ROWS = CHUNK
D_IN = 2576
D_IN_PAD = 2688
D_GLA_IN = D_IN_PAD - 2 * C_CONV
D_FF = 2816
RMS_EPS = 1e-6
LN_EPS = 1e-5
GATE_TAU = 16.0
N_CHIPS = 4

ADAM_LR = 0.001
ADAM_B1 = 0.9
ADAM_B2 = 0.999
ADAM_EPS = 1e-08
ADAM_WD = 0.01
ADAM_STEP = 10

V7X_VMEM_BYTES = 64 * 1024 * 1024
VMEM_LIMIT = V7X_VMEM_BYTES - 8 * 1024 * 1024
SUBLANES = 8
ROW_PART = 128
FFN_BWD_TILE = 192

WEIGHT_NAMES = ("meta_tokens", "norm_mix_g", "w_in", "conv_w", "conv_b", "conv_ln_g", "conv_ln_b", "gla_w_gate2",
                "gla_gate_b", "gla_norm_g", "w_out", "norm_ffn_g", "w_ffn_gate", "w_ffn_up", "w_ffn_down",
                "norm_final_g")

SMALL_ROWS = 8
SMALL_PARTS = (("norm_mix_g", 0, 0, D), ("norm_ffn_g", 1, 0, D), ("norm_final_g", 2, 0, D),
               ("conv_b", 3, 0, C_CONV), ("conv_ln_g", 3, C_CONV, C_CONV), ("conv_ln_b", 4, 0, C_CONV),
               ("gla_gate_b", 4, C_CONV, GLA_K), ("gla_norm_g", 4, C_CONV + GLA_K, DV))
LOSS_ROW = 5

HBM_SPEC = pl.BlockSpec(memory_space=pltpu.HBM)


def _dot(a, b):
    return jnp.dot(a, b, preferred_element_type=F32)


def _dot_nt(a, b):
    return lax.dot_general(a, b, (((1,), (1,)), ((), ())), preferred_element_type=F32)


def _dot_tn(a, b):
    return lax.dot_general(a, b, (((0,), (0,)), ((), ())), preferred_element_type=F32)


def _sigmoid(x):
    return 1.0 / (1.0 + jnp.exp(-x))


def _const_spec(shape):
    return pl.BlockSpec(shape, lambda *_: (0,) * len(shape), pipeline_mode=pl.Buffered(1))


def _acc_spec(shape):
    return pl.BlockSpec(shape, lambda *_: (0,) * len(shape))


def _params(n_axes):
    return pltpu.CompilerParams(dimension_semantics=("arbitrary",) * n_axes, vmem_limit_bytes=VMEM_LIMIT)


def _row_tile(t, want):
    for r in (want, 384, 192, 128, 64):
        if r <= want and t % r == 0:
            return r
    raise ValueError(f"no row tile for {t}")


def _row_parts(r):
    if r % ROW_PART:
        return [slice(None)]
    return [pl.ds(i * ROW_PART, ROW_PART) for i in range(r // ROW_PART)]


def _in_lockstep(bodies):
    live = list(bodies)
    while live:
        still = []
        for g in live:
            try:
                next(g)
                still.append(g)
            except StopIteration:
                pass
        live = still


class _Plan(NamedTuple):
    arrays: Sequence[Any]
    out_shape: Sequence[Any]
    sems: Sequence[Any]
    make: Callable


def _phases(made):
    return made if len(made) == 3 else (made[0], lambda: None, made[1])


def _call(body, *, name, grid, in_specs, out_specs, out_shape, scratch_shapes=(), plan=None):
    n_in, n_out, n_scr = len(in_specs), len(out_specs), len(scratch_shapes)
    if plan is None:
        plan = _Plan([], [], [], lambda ins, outs, sems: (lambda: None, lambda: None))
    nx_in, nx_out = len(plan.arrays), len(plan.out_shape)
    n_steps = functools.reduce(lambda a, b: a * b, grid)

    def hosted(*refs):
        ins, xins = refs[:n_in], refs[n_in:n_in + nx_in]
        o0 = n_in + nx_in
        outs, xouts = refs[o0:o0 + n_out], refs[o0 + n_out:o0 + n_out + nx_out]
        s0 = o0 + n_out + nx_out
        scr, sems = refs[s0:s0 + n_scr], refs[s0 + n_scr:]
        step = functools.reduce(lambda acc, a: acc * grid[a] + pl.program_id(a), range(len(grid)), 0)
        start, relay, finish = _phases(plan.make(xins, xouts, sems))
        pl.when(step == 0)(start)
        pl.when(step == n_steps - 1)(relay)
        body(*ins, *outs, *scr)
        pl.when(step == n_steps - 1)(finish)

    call = pl.pallas_call(
        hosted, name=name, grid=grid, in_specs=list(in_specs) + [HBM_SPEC] * nx_in,
        out_specs=list(out_specs) + [HBM_SPEC] * nx_out, out_shape=list(out_shape) + list(plan.out_shape),
        scratch_shapes=list(scratch_shapes) + list(plan.sems),
        compiler_params=pltpu.CompilerParams(dimension_semantics=("arbitrary",) * len(grid),
                                             vmem_limit_bytes=VMEM_LIMIT, has_side_effects=nx_in > 0))

    def run(*args):
        res = call(*args, *plan.arrays)
        return res[:n_out], res[n_out:]

    return run


def _pad_head_rows(arrays, casts, plan=None):
    n_ex, seq, _ = arrays[0].shape
    nc = (HEAD_ROWS + seq) // CHUNK
    n, k = len(arrays), len(casts)

    def body(*refs):
        ins, outs = refs[:n + k], refs[n + k:]
        for a_ref, o_ref in zip(ins[:n], outs[:n]):
            o_ref[...] = jnp.where(pl.program_id(0) > 0, a_ref[...], 0.0)

        @pl.when(pl.program_id(0) == 0)
        def _():
            for a_ref, o_ref in zip(ins[n:], outs[n:]):
                o_ref[...] = a_ref[...].astype(BF16)

    whole = lambda a: pl.BlockSpec(a.shape, lambda i: (0, 0))
    return _call(
        body, name="pad_head_rows", grid=(nc,),
        in_specs=([pl.BlockSpec((n_ex, CHUNK, D), lambda i: (0, jnp.maximum(i - 1, 0), 0))] * n
                  + [_const_spec(a.shape) for a in casts]),
        out_specs=[pl.BlockSpec((n_ex, CHUNK, D), lambda i: (0, i, 0))] * n + [whole(a) for a in casts],
        out_shape=([jax.ShapeDtypeStruct((n_ex, HEAD_ROWS + seq, D), F32)] * n
                   + [jax.ShapeDtypeStruct(a.shape, BF16) for a in casts]),
        plan=plan,
    )(*arrays, *casts)


def _set_meta_rows(h0, meta):
    n_ex = h0.shape[0]

    def body(h_ref, meta_ref, o_ref):
        o_ref[...] = jnp.concatenate(
            [h_ref[:, :PAD_ROWS, :], jnp.broadcast_to(meta_ref[...][None], (n_ex, N_META, D))], axis=1)

    head = pl.BlockSpec((n_ex, HEAD_ROWS, D), lambda i: (0, 0, 0))
    return pl.pallas_call(
        body, name="set_meta_rows", grid=(1,), in_specs=[head, pl.BlockSpec((N_META, D), lambda i: (0, 0))],
        out_specs=head, out_shape=jax.ShapeDtypeStruct(h0.shape, F32), input_output_aliases={0: 0},
        compiler_params=_params(1),
    )(h0, meta)


def _in_proj(h0, g_mix, w_in, plan=None):
    t = h0.shape[0]
    r = _row_tile(t, 384)

    def body(h_ref, g_ref, w_ref, u_ref, hn_ref):
        h = h_ref[...]
        rstd = lax.rsqrt(jnp.mean(h * h, axis=-1, keepdims=True) + RMS_EPS)
        hn = (h * rstd * g_ref[...]).astype(BF16)
        hn_ref[...] = hn
        u_ref[...] = _dot(hn, w_ref[...])

    return _call(
        body, name="in_proj", grid=(t // r,),
        in_specs=[pl.BlockSpec((r, D), lambda i: (i, 0)), _const_spec((1, D)), _const_spec((D, D_IN_PAD))],
        out_specs=[pl.BlockSpec((r, D_IN_PAD), lambda i: (i, 0)), pl.BlockSpec((r, D), lambda i: (i, 0))],
        out_shape=[jax.ShapeDtypeStruct((t, D_IN_PAD), F32), jax.ShapeDtypeStruct((t, D), BF16)],
        plan=plan,
    )(h0, g_mix, w_in)


CONV_TILE = 192
CONV_SUB = 32
CONV_LEAD = CONV_SUB - (CONV_W - 1)


def _shifted_copies(src, dst, r):
    for s in range(1, SUBLANES):
        dst[s - 1] = src[s:s + r + CONV_SUB - SUBLANES, :]


def _shifted_rows(src, shifted, start):
    base, s = SUBLANES * (start // SUBLANES), start % SUBLANES
    if s == 0:
        return src[base:base + CONV_SUB, :]
    return shifted[s - 1, base:base + CONV_SUB, :]


def _conv_fwd(u, conv_w, conv_b, ln_g, ln_b, n_ex, lp, plan=None):
    r = CONV_TILE
    nt = lp // r
    hb = r // CONV_SUB

    def body(cur_ref, prev_ref, w_ref, b_ref, lg_ref, lb_ref, yc_ref, y_ref, glu, glu_sh):
        i = pl.program_id(1)
        cur = cur_ref[...]
        glu[CONV_SUB:CONV_SUB + r, :] = cur[:, :C_CONV] * _sigmoid(cur[:, C_CONV:])
        pv = prev_ref[...]
        halo = pv[:, :C_CONV] * _sigmoid(pv[:, C_CONV:])
        glu[0:CONV_SUB, :] = jnp.where(i > 0, halo, 0.0)
        _shifted_copies(glu, glu_sh, r)
        w = w_ref[...]
        for j in range(r // CONV_SUB):
            r0 = j * CONV_SUB
            acc = jnp.zeros((CONV_SUB, C_CONV), F32) + b_ref[...]
            for k in range(CONV_W):
                acc = acc + w[k:k + 1, :] * _shifted_rows(glu, glu_sh, r0 + CONV_LEAD + k)
            mu = jnp.mean(acc, axis=-1, keepdims=True)
            cen = acc - mu
            var = jnp.mean(cen * cen, axis=-1, keepdims=True)
            out = cen * lax.rsqrt(var + LN_EPS) * lg_ref[...] + lb_ref[...]
            y = out * _sigmoid(out)
            row = i * r + r0 + lax.broadcasted_iota(jnp.int32, (CONV_SUB, 1), 0)
            y = jnp.where(row >= PAD_ROWS, y, 0.0)
            yc_ref[r0:r0 + CONV_SUB, :] = acc
            y_ref[r0:r0 + CONV_SUB, :] = y.astype(BF16)

    t = n_ex * lp
    return _call(
        body, name="conv_fwd", grid=(n_ex, nt),
        in_specs=[pl.BlockSpec((r, 2 * C_CONV), lambda b, i: (b * nt + i, 0)),
                  pl.BlockSpec((CONV_SUB, 2 * C_CONV), lambda b, i: (jnp.maximum((b * nt + i) * hb - 1, 0), 0)),
                  _const_spec((32, C_CONV)), _const_spec((1, C_CONV)), _const_spec((1, C_CONV)), _const_spec((1, C_CONV))],
        out_specs=[pl.BlockSpec((r, C_CONV), lambda b, i: (b * nt + i, 0)),
                   pl.BlockSpec((r, C_CONV), lambda b, i: (b * nt + i, 0))],
        out_shape=[jax.ShapeDtypeStruct((t, C_CONV), F32), jax.ShapeDtypeStruct((t, C_CONV), BF16)],
        scratch_shapes=[pltpu.VMEM((r + CONV_SUB, C_CONV), F32),
                        pltpu.VMEM((SUBLANES - 1, r + CONV_SUB - SUBLANES, C_CONV), F32)],
        plan=plan,
    )(u, u, conv_w, conv_b, ln_g, ln_b)


def _mix_out_ffn_up(h0, y_conv, y_gla, w_out, g_ffn, w_gate, w_up, plan=None):
    t = h0.shape[0]
    r = _row_tile(t, 384)

    def body(h0_ref, yc_ref, yg_ref, wo_ref, g_ref, wg_ref, wu_ref, h1_ref, hn_ref, gate_ref, up_ref, act_ref):
        h1 = h0_ref[...] + _dot(yc_ref[...], wo_ref[0:C_CONV, :]) + _dot(yg_ref[...], wo_ref[C_CONV:D, :])
        h1_ref[...] = h1
        rstd = lax.rsqrt(jnp.mean(h1 * h1, axis=-1, keepdims=True) + RMS_EPS)
        hn = (h1 * rstd * g_ref[...]).astype(BF16)
        hn_ref[...] = hn
        gate = _dot(hn, wg_ref[...])
        up = _dot(hn, wu_ref[...])
        gate_ref[...] = gate
        up_ref[...] = up
        act_ref[...] = (gate * _sigmoid(gate) * up).astype(BF16)

    rows = lambda w: pl.BlockSpec((r, w), lambda i: (i, 0))
    return _call(
        body, name="mix_out_ffn_up", grid=(t // r,),
        in_specs=[rows(D), rows(C_CONV), rows(GLA_V), _const_spec((D, D)), _const_spec((1, D)),
                  _const_spec((D, D_FF)), _const_spec((D, D_FF))],
        out_specs=[rows(D), rows(D), rows(D_FF), rows(D_FF), rows(D_FF)],
        out_shape=[jax.ShapeDtypeStruct((t, D), F32), jax.ShapeDtypeStruct((t, D), BF16),
                   jax.ShapeDtypeStruct((t, D_FF), F32), jax.ShapeDtypeStruct((t, D_FF), F32),
                   jax.ShapeDtypeStruct((t, D_FF), BF16)],
        plan=plan,
    )(h0, y_conv, y_gla, w_out, g_ffn, w_gate, w_up)


def _ffn_down_loss(act, w_down, h1, target, g_final, row_mask):
    t = h1.shape[0]
    r = _row_tile(t, 384)

    def body(act_ref, wd_ref, h1_ref, tgt_ref, gf_ref, mask_ref, dh2_ref, loss_ref, dgf_ref):
        @pl.when(pl.program_id(0) == 0)
        def _():
            loss_ref[...] = jnp.zeros_like(loss_ref)
            dgf_ref[...] = jnp.zeros_like(dgf_ref)

        gf = gf_ref[...]

        def part(rows):
            h2 = h1_ref[rows, :] + _dot(act_ref[rows, :], wd_ref[...])
            yield
            rstd = lax.rsqrt(jnp.mean(h2 * h2, axis=-1, keepdims=True) + RMS_EPS)
            nrm = h2 * rstd
            err = (nrm * gf - tgt_ref[rows, :]) * mask_ref[rows, :]
            loss_ref[...] += jnp.sum(err * err) * (0.5 / D)
            dy = err * (1.0 / D)
            dgf_ref[...] += jnp.sum(dy * nrm, axis=0, keepdims=True)
            dn = dy * gf
            dh2_ref[rows, :] = rstd * (dn - nrm * jnp.mean(dn * nrm, axis=-1, keepdims=True))

        _in_lockstep(part(rows) for rows in _row_parts(r))

    rows = lambda w: pl.BlockSpec((r, w), lambda i: (i, 0))
    return pl.pallas_call(
        body, name="ffn_down_loss", grid=(t // r,),
        in_specs=[rows(D_FF), _const_spec((D_FF, D)), rows(D), rows(D), _const_spec((1, D)), rows(1)],
        out_specs=[rows(D), _acc_spec((1, 128)), _acc_spec((1, D))],
        out_shape=[jax.ShapeDtypeStruct((t, D), F32), jax.ShapeDtypeStruct((1, 128), F32),
                   jax.ShapeDtypeStruct((1, D), F32)],
        compiler_params=_params(1),
    )(act, w_down, h1, target, g_final, row_mask)


def _ffn_bwd(dh2, gate, up, h1, w_down_t, w_gate_t, w_up_t, w_out_t, g_ffn):
    t = h1.shape[0]
    r = _row_tile(t, FFN_BWD_TILE)

    def body(dh2_ref, gate_ref, up_ref, h1_ref, wd_ref, wg_ref, wu_ref, wo_ref, g_ref,
             dgate_ref, dup_ref, dh1_ref, dycat_ref, dg_ref):
        @pl.when(pl.program_id(0) == 0)
        def _():
            dg_ref[...] = jnp.zeros_like(dg_ref)

        dh2 = dh2_ref[...]
        dact = _dot(dh2.astype(BF16), wd_ref[...])
        gate = gate_ref[...]
        sg = _sigmoid(gate)
        dgate = (dact * up_ref[...] * (sg * (1.0 + gate * (1.0 - sg)))).astype(BF16)
        dup = (dact * (gate * sg)).astype(BF16)
        dgate_ref[...] = dgate
        dup_ref[...] = dup
        dhn = _dot(dgate, wg_ref[...]) + _dot(dup, wu_ref[...])
        h1 = h1_ref[...]
        rstd = lax.rsqrt(jnp.mean(h1 * h1, axis=-1, keepdims=True) + RMS_EPS)
        nrm = h1 * rstd
        dg_ref[...] += jnp.sum(dhn * nrm, axis=0, keepdims=True)
        dn = dhn * g_ref[...]
        dh1 = dh2 + rstd * (dn - nrm * jnp.mean(dn * nrm, axis=-1, keepdims=True))
        dh1_ref[...] = dh1
        dycat_ref[...] = _dot(dh1.astype(BF16), wo_ref[...])

    rows = lambda w: pl.BlockSpec((r, w), lambda i: (i, 0))
    return pl.pallas_call(
        body, name="ffn_bwd", grid=(t // r,),
        in_specs=[rows(D), rows(D_FF), rows(D_FF), rows(D), _const_spec((D, D_FF)), _const_spec((D_FF, D)),
                  _const_spec((D_FF, D)), _const_spec((D, D)), _const_spec((1, D))],
        out_specs=[rows(D_FF), rows(D_FF), rows(D), rows(D), _acc_spec((1, D))],
        out_shape=[jax.ShapeDtypeStruct((t, D_FF), BF16), jax.ShapeDtypeStruct((t, D_FF), BF16),
                   jax.ShapeDtypeStruct((t, D), F32), jax.ShapeDtypeStruct((t, D), F32),
                   jax.ShapeDtypeStruct((1, D), F32)],
        compiler_params=_params(1),
    )(dh2, gate, up, h1, w_down_t, w_gate_t, w_up_t, w_out_t, g_ffn)


def _conv_bwd(dycat, yc, u, conv_w, ln_g, ln_b, n_ex, lp, plan=None):
    r = CONV_TILE
    nt = lp // r
    hb = r // CONV_SUB
    nsub = r // CONV_SUB

    def ln_bwd(dy, yc_rows, live, lg, lb):
        mu = jnp.mean(yc_rows, axis=-1, keepdims=True)
        cen = yc_rows - mu
        rs = lax.rsqrt(jnp.mean(cen * cen, axis=-1, keepdims=True) + LN_EPS)
        yn = cen * rs
        out = yn * lg + lb
        so = _sigmoid(out)
        dout = jnp.where(live, dy * (so * (1.0 + out * (1.0 - so))), 0.0)
        dyn = dout * lg
        dyc = rs * (dyn - jnp.mean(dyn, axis=-1, keepdims=True) - yn * jnp.mean(dyn * yn, axis=-1, keepdims=True))
        return dyc, dout, yn

    def body(dy_ref, dyn_ref, yc_ref, ycn_ref, cur_ref, prev_ref, w_ref, lg_ref, lb_ref,
             du_ref, dw_ref, db_ref, dlg_ref, dlb_ref, glu, dycs, dwacc, glu_sh, dycs_sh):
        b = pl.program_id(0)
        i = pl.program_id(1)
        first = jnp.logical_and(b == 0, i == 0)

        @pl.when(first)
        def _():
            dwacc[...] = jnp.zeros_like(dwacc)
            db_ref[...] = jnp.zeros_like(db_ref)
            dlg_ref[...] = jnp.zeros_like(dlg_ref)
            dlb_ref[...] = jnp.zeros_like(dlb_ref)

        lg, lb = lg_ref[...], lb_ref[...]
        cur = cur_ref[...]
        sig = _sigmoid(cur[:, C_CONV:])
        glu[CONV_SUB:CONV_SUB + r, :] = cur[:, :C_CONV] * sig
        pv = prev_ref[...]
        glu[0:CONV_SUB, :] = jnp.where(i > 0, pv[:, :C_CONV] * _sigmoid(pv[:, C_CONV:]), 0.0)

        row = i * r + lax.broadcasted_iota(jnp.int32, (r, 1), 0)
        dyc, dout, yn = ln_bwd(dy_ref[...], yc_ref[...], row >= PAD_ROWS, lg, lb)
        dycs[0:r, :] = dyc
        dycn, _, _ = ln_bwd(dyn_ref[...], ycn_ref[...], i < nt - 1, lg, lb)
        dycs[r:r + CONV_SUB, :] = dycn
        db_ref[...] += jnp.sum(dyc, axis=0, keepdims=True)
        dlg_ref[...] += jnp.sum(dout * yn, axis=0, keepdims=True)
        dlb_ref[...] += jnp.sum(dout, axis=0, keepdims=True)

        _shifted_copies(glu, glu_sh, r)
        _shifted_copies(dycs, dycs_sh, r)
        w = w_ref[...]
        for j in range(nsub):
            r0 = j * CONV_SUB
            dblk = dycs[r0:r0 + CONV_SUB, :]
            dglu = jnp.zeros((CONV_SUB, C_CONV), F32)
            for k in range(CONV_W):
                dglu = dglu + w[k:k + 1, :] * _shifted_rows(dycs, dycs_sh, r0 + (CONV_W - 1) - k)
                prod = dblk * _shifted_rows(glu, glu_sh, r0 + CONV_LEAD + k)
                dwacc[k] += prod.reshape(CONV_SUB // SUBLANES, SUBLANES, C_CONV).sum(axis=0)
            sg = sig[r0:r0 + CONV_SUB, :]
            cv = cur[r0:r0 + CONV_SUB, :C_CONV]
            du_ref[r0:r0 + CONV_SUB, :C_CONV] = (dglu * sg).astype(BF16)
            du_ref[r0:r0 + CONV_SUB, C_CONV:] = (dglu * cv * sg * (1.0 - sg)).astype(BF16)

        @pl.when(jnp.logical_and(b == n_ex - 1, i == nt - 1))
        def _():
            dw_ref[...] = jnp.sum(dwacc[...], axis=1)

    t = n_ex * lp
    cur_rows = lambda w, col: pl.BlockSpec((r, w), lambda b, i: (b * nt + i, col))
    nxt_rows = lambda w, col: pl.BlockSpec(
        (CONV_SUB, w), lambda b, i: (jnp.minimum((b * nt + i + 1) * hb, n_ex * nt * hb - 1), col))
    return _call(
        body, name="conv_bwd", grid=(n_ex, nt),
        in_specs=[cur_rows(C_CONV, 0), nxt_rows(C_CONV, 0), cur_rows(C_CONV, 0), nxt_rows(C_CONV, 0),
                  cur_rows(2 * C_CONV, 0),
                  pl.BlockSpec((CONV_SUB, 2 * C_CONV), lambda b, i: (jnp.maximum((b * nt + i) * hb - 1, 0), 0)),
                  _const_spec((32, C_CONV)), _const_spec((1, C_CONV)), _const_spec((1, C_CONV))],
        out_specs=[cur_rows(2 * C_CONV, 0), _acc_spec((32, C_CONV)), _acc_spec((1, C_CONV)),
                   _acc_spec((1, C_CONV)), _acc_spec((1, C_CONV))],
        out_shape=[jax.ShapeDtypeStruct((t, 2 * C_CONV), BF16), jax.ShapeDtypeStruct((32, C_CONV), F32),
                   jax.ShapeDtypeStruct((1, C_CONV), F32), jax.ShapeDtypeStruct((1, C_CONV), F32),
                   jax.ShapeDtypeStruct((1, C_CONV), F32)],
        scratch_shapes=[pltpu.VMEM((r + CONV_SUB, C_CONV), F32), pltpu.VMEM((r + CONV_SUB, C_CONV), F32),
                        pltpu.VMEM((32, SUBLANES, C_CONV), F32),
                        pltpu.VMEM((SUBLANES - 1, r + CONV_SUB - SUBLANES, C_CONV), F32),
                        pltpu.VMEM((SUBLANES - 1, r + CONV_SUB - SUBLANES, C_CONV), F32)],
        plan=plan,
    )(dycat, dycat, yc, yc, u, u, conv_w, ln_g, ln_b)


HEAD_ROWS_ALL = N_HEADS * CHUNK


def _gla_gates(lr, w2, gb, first_chunk):
    z = _dot(lr.astype(BF16), w2) + gb
    a = (jnp.minimum(z, 0.0) - jnp.log(1.0 + jnp.exp(-jnp.abs(z)))) * (1.0 / GATE_TAU)
    row = lax.broadcasted_iota(jnp.int32, (CHUNK, 1), 0)
    live = jnp.logical_or(jnp.logical_not(first_chunk), row >= PAD_ROWS)
    return z, jnp.where(live, a, 0.0), live


def _tri(lower):
    i = lax.broadcasted_iota(jnp.int32, (CHUNK, CHUNK), 0)
    j = lax.broadcasted_iota(jnp.int32, (CHUNK, CHUNK), 1)
    return (i >= j) if lower else (i <= j)


def _head_of(shape, axis, per_head):
    return lax.broadcasted_iota(jnp.int32, shape, axis) // per_head


def _expand(x, lanes_per_head):
    rows, lanes = HEAD_ROWS_ALL, x.shape[1]
    keep = _head_of((rows, lanes), 0, CHUNK) == _head_of((rows, lanes), 1, lanes_per_head)
    return jnp.where(keep, jnp.tile(x, (N_HEADS, 1)), 0.0)


def _expand_lanes(x):
    rows, w = x.shape
    keep = _head_of((rows, N_HEADS * w), 0, CHUNK) == _head_of((rows, N_HEADS * w), 1, w)
    return jnp.where(keep, jnp.tile(x, (1, N_HEADS)), 0.0)


def _expand_state(st):
    rows, lanes = N_HEADS * DV, st.shape[1]
    keep = _head_of((rows, lanes), 0, DV) == _head_of((rows, lanes), 1, DK)
    return jnp.where(keep, jnp.tile(st, (N_HEADS, 1)), 0.0)


def _fold(t, rows_per_head):
    lane_head = _head_of((rows_per_head, t.shape[1]), 1, DK)
    out = jnp.where(lane_head == 0, t[0:rows_per_head], 0.0)
    for h in range(1, N_HEADS):
        out = out + jnp.where(lane_head == h, t[h * rows_per_head:(h + 1) * rows_per_head], 0.0)
    return out


def _rows_by_head(x):
    return jnp.concatenate([x[:, h * DV:(h + 1) * DV] for h in range(N_HEADS)], axis=0)


def _lanes_by_head(x):
    return jnp.concatenate([x[h * CHUNK:(h + 1) * CHUNK] for h in range(N_HEADS)], axis=1)


def _running_sum(a, lower):
    hi = a.astype(BF16)
    rest = a - hi.astype(F32)
    mid = rest.astype(BF16)
    lo = (rest - mid.astype(F32)).astype(BF16)
    w = a.shape[1]
    parts = _dot(_tri(lower).astype(F32).astype(BF16), jnp.concatenate([hi, mid, lo], axis=1))
    return parts[:, :w] + parts[:, w:2 * w] + parts[:, 2 * w:]


def _stacked_causal():
    i = lax.broadcasted_iota(jnp.int32, (HEAD_ROWS_ALL, CHUNK), 0) % CHUNK
    j = lax.broadcasted_iota(jnp.int32, (HEAD_ROWS_ALL, CHUNK), 1)
    return i >= j


GLA_GROUP = 3


def _gla_chunk(q, k, v, lr, w2, gb, first_chunk):
    z, a, live = _gla_gates(lr, w2, gb, first_chunk)
    yield
    b = _running_sum(a, True)
    yield
    bl = b[CHUNK - 1:CHUNK, :]
    e_pos, e_neg, e_dec = jnp.exp(b), jnp.exp(-b), jnp.exp(bl - b)
    q_f, k_f, kd_f = q * (DK ** -0.5) * e_pos, k * e_neg, k * e_dec
    qx = _expand(q_f, DK).astype(BF16)
    k_in, k_dec, v_b = k_f.astype(BF16), kd_f.astype(BF16), v.astype(BF16)
    s = jnp.where(_stacked_causal(), _dot_nt(qx, k_in), 0.0).astype(BF16)
    yield
    p = _dot(s, v_b)
    yield
    o_intra = jnp.concatenate([p[h * CHUNK:(h + 1) * CHUNK, h * DV:(h + 1) * DV] for h in range(N_HEADS)], axis=0)
    return dict(z=z, live=live, bl=bl, e_pos=e_pos, e_neg=e_neg, e_dec=e_dec, q_f=q_f, k_f=k_f, kd_f=kd_f,
                qx=qx, k_in=k_in, k_dec=k_dec, v_b=v_b, s=s, o_intra=o_intra, decay=jnp.exp(bl))


def _gla_fwd(u, w2, gb, ng, n_ex, lp, plan=None):
    nc = lp // CHUNK
    t = n_ex * lp
    rows_of = lambda j: pl.ds(j * CHUNK, CHUNK)

    def body(qk_ref, v_ref, g_ref, lr_ref, w2_ref, gb_ref, ng_ref, y_ref, st_ref, state):
        n = pl.program_id(0)

        @pl.when(n == 0)
        def _():
            state[...] = jnp.zeros_like(state)

        carried = [state[e] for e in range(n_ex)]

        def one_chunk(e, j):
            rows = rows_of(j)
            qk = qk_ref[e, rows, :]
            first = jnp.logical_and(n == 0, j == 0)
            c = yield from _gla_chunk(qk[:, :GLA_K], qk[:, GLA_K:], v_ref[e, rows, :], lr_ref[e, rows, :],
                                      w2_ref[...], gb_ref[...], first)
            kv = _fold(_dot_tn(c["v_b"], c["k_dec"]), DV)
            g = _rows_by_head(g_ref[e, rows, :])
            gate = ng_ref[...] * (g * _sigmoid(g))
            yield
            for _ in range(j):
                yield
            st = carried[e]
            st_ref[e, pl.ds(j * DV, DV), :] = st
            o = c["o_intra"] + _dot_nt(c["qx"], st.astype(BF16))
            rstd = lax.rsqrt(jnp.mean(o * o, axis=-1, keepdims=True) + RMS_EPS)
            y_ref[e, rows, :] = _lanes_by_head(o * rstd * gate).astype(BF16)
            carried[e] = c["decay"] * st + kv

        _in_lockstep(one_chunk(e, j) for j in range(GLA_GROUP) for e in range(n_ex))
        for e in range(n_ex):
            state[e] = carried[e]

    u3 = u.reshape(n_ex, lp, D_IN_PAD)
    blk = lambda w, col: pl.BlockSpec((n_ex, GLA_GROUP * CHUNK, w), lambda n: (0, n, col))
    (y, states), extra = _call(
        body, name="gla_fwd", grid=(nc // GLA_GROUP,),
        in_specs=[blk(2 * GLA_K, 2), blk(GLA_V, 3), blk(GLA_V, 4), blk(128, 20),
                  _const_spec((128, GLA_K)), _const_spec((1, GLA_K)), _const_spec((1, DV))],
        out_specs=[blk(GLA_V, 0), pl.BlockSpec((n_ex, GLA_GROUP * DV, GLA_K), lambda n: (0, n, 0))],
        out_shape=[jax.ShapeDtypeStruct((n_ex, lp, GLA_V), BF16),
                   jax.ShapeDtypeStruct((n_ex, nc * DV, GLA_K), F32)],
        scratch_shapes=[pltpu.VMEM((n_ex, DV, GLA_K), F32)],
        plan=plan,
    )(u3, u3, u3, u3, w2, gb, ng)
    return (y.reshape(t, GLA_V), states), extra


def _gla_bwd(dycat, u, states, w2, gb, ng, n_ex, lp, plan=None):
    nc = lp // CHUNK
    t = n_ex * lp

    def body(dy_ref, qk_ref, v_ref, g_ref, lr_ref, st_ref, w2_ref, gb_ref, ng_ref,
             du_ref, dw2_ref, dgb_ref, dng_ref, dstate):
        n = pl.program_id(0)
        group = nc // GLA_GROUP - 1 - n

        @pl.when(n == 0)
        def _():
            dw2_ref[...] = jnp.zeros_like(dw2_ref)
            dgb_ref[...] = jnp.zeros_like(dgb_ref)
            dng_ref[...] = jnp.zeros_like(dng_ref)
            dstate[...] = jnp.zeros_like(dstate)

        carried = [dstate[e] for e in range(n_ex)]

        def one_chunk(e, order):
            j = GLA_GROUP - 1 - order
            rows = pl.ds(j * CHUNK, CHUNK)
            qk = qk_ref[e, rows, :]
            lr = lr_ref[e, rows, :]
            st = st_ref[e, pl.ds(j * DV, DV), :]
            first = jnp.logical_and(group == 0, j == 0)
            c = yield from _gla_chunk(qk[:, :GLA_K], qk[:, GLA_K:], v_ref[e, rows, :], lr, w2_ref[...], gb_ref[...],
                                      first)
            qx, k_in, k_dec, v_b, s = c["qx"], c["k_in"], c["k_dec"], c["v_b"], c["s"]
            st_b = st.astype(BF16)
            o = c["o_intra"] + _dot_nt(qx, st_b)
            ngv = ng_ref[...]
            yield
            rstd = lax.rsqrt(jnp.mean(o * o, axis=-1, keepdims=True) + RMS_EPS)
            nrm = o * rstd
            g = _rows_by_head(g_ref[e, rows, :])
            dy = _rows_by_head(dy_ref[e, rows, :])
            sg = _sigmoid(g)
            dg = dy * nrm * ngv * (sg * (1.0 + g * (1.0 - sg)))
            dt = dy * (g * sg)
            dng_ref[...] += jnp.sum(dt * nrm, axis=0, keepdims=True)
            dn = dt * ngv
            do = rstd * (dn - nrm * jnp.mean(dn * nrm, axis=-1, keepdims=True))
            do_b = do.astype(BF16)
            dox = _expand_lanes(do).astype(BF16)
            yield
            da = jnp.where(_stacked_causal(), _dot_nt(dox, v_b), 0.0).astype(BF16)
            dv_intra = _dot_tn(s, dox)
            dst_own = _dot_tn(do_b, qx)
            yield
            dq_in = _fold(_dot(da, k_in) + _dot(do_b, st_b), CHUNK)
            dk_in = _dot_tn(da, qx)
            dq = dq_in * (DK ** -0.5) * c["e_pos"]
            yield
            for _ in range(order):
                yield
            dst = carried[e]
            dstx = _expand_state(dst).astype(BF16)
            dv = dv_intra + _dot_nt(k_dec, dstx)
            dk_dec = _dot(v_b, dstx)
            carried[e] = dst_own + c["decay"] * dst
            yield
            dbl = (jnp.sum(dk_dec * c["kd_f"], axis=0, keepdims=True)
                   + c["decay"] * jnp.sum(dst * st, axis=0, keepdims=True))
            dk = dk_in * c["e_neg"] + dk_dec * c["e_dec"]
            db = dq_in * c["q_f"] - dk_in * c["k_f"] - dk_dec * c["kd_f"]
            row = lax.broadcasted_iota(jnp.int32, (CHUNK, 1), 0)
            da_log = _running_sum(db + jnp.where(row == CHUNK - 1, dbl, 0.0), False)
            yield
            dz = jnp.where(c["live"], da_log * (1.0 - _sigmoid(c["z"])) * (1.0 / GATE_TAU), 0.0)
            dz_b = dz.astype(BF16)
            out = du_ref.at[e, rows, :]
            out[:, 0:GLA_K] = dq.astype(BF16)
            out[:, GLA_K:2 * GLA_K] = dk.astype(BF16)
            out[:, 2 * GLA_K:2 * GLA_K + GLA_V] = dv.astype(BF16)
            out[:, 2 * GLA_K + GLA_V:2 * GLA_K + 2 * GLA_V] = _lanes_by_head(dg).astype(BF16)
            out[:, 2 * GLA_K + 2 * GLA_V:] = _dot_nt(dz_b, w2_ref[...]).astype(BF16)
            dw2_ref[...] += _dot_tn(lr.astype(BF16), dz_b)
            dgb_ref[...] += jnp.sum(dz, axis=0, keepdims=True)

        _in_lockstep(one_chunk(e, order) for order in range(GLA_GROUP) for e in range(n_ex))
        for e in range(n_ex):
            dstate[e] = carried[e]

    u3 = u.reshape(n_ex, lp, D_IN_PAD)
    rev = lambda w, col: pl.BlockSpec((n_ex, GLA_GROUP * CHUNK, w), lambda n: (0, nc // GLA_GROUP - 1 - n, col))
    (du, d_w2, d_gb, d_ng), extra = _call(
        body, name="gla_bwd", grid=(nc // GLA_GROUP,),
        in_specs=[rev(GLA_V, 1), rev(2 * GLA_K, 2), rev(GLA_V, 3), rev(GLA_V, 4), rev(128, 20),
                  pl.BlockSpec((n_ex, GLA_GROUP * DV, GLA_K), lambda n: (0, nc // GLA_GROUP - 1 - n, 0)),
                  _const_spec((128, GLA_K)), _const_spec((1, GLA_K)), _const_spec((1, DV))],
        out_specs=[rev(D_GLA_IN, 0), _acc_spec((128, GLA_K)), _acc_spec((1, GLA_K)), _acc_spec((1, DV))],
        out_shape=[jax.ShapeDtypeStruct((n_ex, lp, D_GLA_IN), BF16), jax.ShapeDtypeStruct((128, GLA_K), F32),
                   jax.ShapeDtypeStruct((1, GLA_K), F32), jax.ShapeDtypeStruct((1, DV), F32)],
        scratch_shapes=[pltpu.VMEM((n_ex, DV, GLA_K), F32)],
        plan=plan,
    )(dycat.reshape(n_ex, lp, D), u3, u3, u3, u3, states, w2, gb, ng)
    return (du.reshape(t, D_GLA_IN), d_w2, d_gb, d_ng), extra


def _in_proj_bwd(du_conv, du_gla, w_in_t_conv, w_in_t_gla, h0, dh1, g_mix, plan=None):
    t = h0.shape[0]
    r = _row_tile(t, 384)

    def body(dc_ref, dg_ref, wc_ref, wg_ref, h_ref, dh1_ref, g_ref, dh0_ref, dgm_ref):
        @pl.when(pl.program_id(0) == 0)
        def _():
            dgm_ref[...] = jnp.zeros_like(dgm_ref)

        dhn = _dot(dc_ref[...], wc_ref[...]) + _dot(dg_ref[...], wg_ref[...])
        h = h_ref[...]
        rstd = lax.rsqrt(jnp.mean(h * h, axis=-1, keepdims=True) + RMS_EPS)
        nrm = h * rstd
        dgm_ref[...] += jnp.sum(dhn * nrm, axis=0, keepdims=True)
        dn = dhn * g_ref[...]
        dh0_ref[...] = dh1_ref[...] + rstd * (dn - nrm * jnp.mean(dn * nrm, axis=-1, keepdims=True))

    rows = lambda w: pl.BlockSpec((r, w), lambda i: (i, 0))
    return _call(
        body, name="in_proj_bwd", grid=(t // r,),
        in_specs=[rows(2 * C_CONV), rows(D_GLA_IN), _const_spec((2 * C_CONV, D)), _const_spec((D_GLA_IN, D)),
                  rows(D), rows(D), _const_spec((1, D))],
        out_specs=[rows(D), _acc_spec((1, D))],
        out_shape=[jax.ShapeDtypeStruct((t, D), F32), jax.ShapeDtypeStruct((1, D), F32)],
        plan=plan,
    )(du_conv, du_gla, w_in_t_conv, w_in_t_gla, h0, dh1, g_mix)


def _wgrad(x, dy, name, plan=None):
    t, m = x.shape
    n = dy.shape[1]
    tk = t // 3 if t % (3 * 128) == 0 else _row_tile(t, 384)
    tm = m if m <= D_GLA_IN else m // 2

    def body(x_ref, dy_ref, o_ref):
        @pl.when(pl.program_id(1) == 0)
        def _():
            o_ref[...] = jnp.zeros_like(o_ref)

        o_ref[...] += _dot_tn(x_ref[...].astype(BF16), dy_ref[...].astype(BF16))

    (out,), extra = _call(
        body, name=name, grid=(m // tm, t // tk),
        in_specs=[pl.BlockSpec((tk, tm), lambda i, k: (k, i)), pl.BlockSpec((tk, n), lambda i, k: (k, 0))],
        out_specs=[pl.BlockSpec((tm, n), lambda i, k: (i, 0))],
        out_shape=[jax.ShapeDtypeStruct((m, n), F32)],
        plan=plan,
    )(x, dy)
    return out, extra


def _wgrad_pair(xa, xb, dy, name):
    t, m = xa.shape
    n = dy.shape[1]
    tk = t // 3 if t % (3 * 128) == 0 else _row_tile(t, 384)

    def body(xa_ref, xb_ref, dy_ref, o_ref):
        @pl.when(pl.program_id(1) == 0)
        def _():
            o_ref[...] = jnp.zeros_like(o_ref)

        x = jnp.where(pl.program_id(0) == 0, xa_ref[...], xb_ref[...])
        o_ref[...] += _dot_tn(x.astype(BF16), dy_ref[...].astype(BF16))

    rows = lambda w: pl.BlockSpec((tk, w), lambda i, k: (k, 0))
    return pl.pallas_call(
        body, name=name, grid=(2, t // tk), in_specs=[rows(m), rows(m), rows(n)],
        out_specs=pl.BlockSpec((m, n), lambda i, k: (i, 0)),
        out_shape=jax.ShapeDtypeStruct((2 * m, n), F32), compiler_params=_params(2),
    )(xa, xb, dy)


def _adam_update(g, w, m, v):
    m2 = ADAM_B1 * m + (1.0 - ADAM_B1) * g
    v2 = ADAM_B2 * v + (1.0 - ADAM_B2) * (g * g)
    m_hat = m2 / (1.0 - ADAM_B1 ** ADAM_STEP)
    v_hat = v2 / (1.0 - ADAM_B2 ** ADAM_STEP)
    delta = -ADAM_LR * (m_hat / (jnp.sqrt(v_hat) + ADAM_EPS) + ADAM_WD * w)
    return delta, m2, v2


ADAMW_STEPS = 4


def _adamw(g, w, m, v, name):
    rows, cols = g.shape
    steps = ADAMW_STEPS if rows % (ADAMW_STEPS * SUBLANES) == 0 else 1

    def body(g_ref, w_ref, m_ref, v_ref, d_ref, m2_ref, v2_ref):
        d_ref[...], m2_ref[...], v2_ref[...] = _adam_update(g_ref[...], w_ref[...], m_ref[...], v_ref[...])

    spec = pl.BlockSpec((rows // steps, cols), lambda i: (i, 0))
    return pl.pallas_call(
        body, name=name, grid=(steps,), in_specs=[spec] * 4, out_specs=[spec] * 3,
        out_shape=[jax.ShapeDtypeStruct(g.shape, F32)] * 3, compiler_params=_params(1),
    )(g, w, m, v)


def _adamw_halves(items, c, name):
    n = len(items)
    h = items[0][0].shape[-1]
    splits = lambda a: a.shape[0] % (ADAMW_STEPS * (SUBLANES if a.ndim == 2 else 1)) == 0
    steps = ADAMW_STEPS if all(splits(it[0]) for it in items) else 1

    def body(c_ref, *refs):
        ins, outs = refs[:5 * n], refs[5 * n:]
        own = pl.program_id(1) == c_ref[0]
        for i in range(n):
            a_ref, b_ref, w_ref, m_ref, v_ref = ins[5 * i:5 * i + 5]
            go_ref, d_ref, m2_ref, v2_ref = outs[4 * i:4 * i + 4]
            g = jnp.where(own, a_ref[...], b_ref[...])
            go_ref[...] = g
            d_ref[...], m2_ref[...], v2_ref[...] = _adam_update(g, w_ref[...], m_ref[...], v_ref[...])

    in_specs, out_specs, out_shape, args = [pl.BlockSpec(memory_space=pltpu.SMEM)], [], [], []
    for mine, theirs, w, m, v in items:
        tr = mine.shape[0] // steps
        mid = (0,) * (mine.ndim - 2)
        half = pl.BlockSpec((tr,) + mine.shape[1:-1] + (h,), lambda i, j, mid=mid: (i, *mid, 0))
        full = pl.BlockSpec((tr,) + mine.shape[1:-1] + (h,), lambda i, j, mid=mid: (i, *mid, j))
        in_specs += [half, half, full, full, full]
        out_specs += [full] * 4
        out_shape += [jax.ShapeDtypeStruct(w.shape, F32)] * 4
        args += [mine, theirs, w, m, v]
    res = pl.pallas_call(
        body, name=name, grid=(steps, 2), in_specs=in_specs, out_specs=out_specs, out_shape=out_shape,
        compiler_params=_params(2),
    )(jnp.reshape(c, (1,)).astype(jnp.int32), *args)
    return [res[4 * i:4 * i + 4] for i in range(n)]


def _rs_add_halves(pairs, c, name):
    blocks = pairs[0][0].shape[0]
    n = len(pairs)

    def body(c_ref, *refs):
        for i in range(n):
            refs[2 * n + i][...] = (refs[2 * i][...] + refs[2 * i + 1][...]).astype(BF16)

    in_specs, out_specs, out_shape = [], [], []
    for g, _ in pairs:
        _, rows, w = g.shape
        in_specs += [pl.BlockSpec((1, rows, w // 2), lambda j, s: (j, 0, s[0])),
                     pl.BlockSpec((1, rows, w // 2), lambda j, s: (j, 0, 0))]
        out_specs += [pl.BlockSpec((1, rows, w // 2), lambda j, s: (j, 0, 0))]
        out_shape += [jax.ShapeDtypeStruct((blocks, rows, w // 2), BF16)]
    return pl.pallas_call(
        body, name=name,
        grid_spec=pltpu.PrefetchScalarGridSpec(num_scalar_prefetch=1, grid=(blocks,), in_specs=in_specs,
                                               out_specs=out_specs),
        out_shape=out_shape, compiler_params=_params(1),
    )(jnp.reshape(c, (1,)).astype(jnp.int32), *[a for pair in pairs for a in pair])


def _rs_sum(pairs, mine, name):
    n = len(pairs)
    steps = 2 if all(own.shape[1] % (2 * 16) == 0 for own, _ in pairs) else 1

    def body(mine_ref, *refs):
        for i in range(n):
            p = refs[2 * i + 1][...].astype(F32)
            refs[2 * n + i][...] = ((refs[2 * i][0].astype(F32) + p[0]) + p[1]) + p[2]

    in_specs, out_specs, out_shape = [], [], []
    for own, _ in pairs:
        _, rows, h = own.shape
        tr = rows // steps
        in_specs += [pl.BlockSpec((1, tr, h), lambda i, s: (s[0], i, 0)),
                     pl.BlockSpec((3, tr, h), lambda i, s: (0, i, 0))]
        out_specs += [pl.BlockSpec((tr, h), lambda i, s: (i, 0))]
        out_shape += [jax.ShapeDtypeStruct((rows, h), F32)]
    return pl.pallas_call(
        body, name=name,
        grid_spec=pltpu.PrefetchScalarGridSpec(num_scalar_prefetch=1, grid=(steps,), in_specs=in_specs,
                                               out_specs=out_specs),
        out_shape=out_shape, compiler_params=_params(1),
    )(jnp.reshape(mine, (1,)).astype(jnp.int32), *[a for pair in pairs for a in pair])


def _sum_slots_adamw(slots, late_slots, vectors):
    late_rows = late_slots.shape[1]
    n = len(SMALL_PARTS)

    def body(s_ref, l_ref, *refs):
        ins, g_ref, outs = refs[:3 * n], refs[3 * n], refs[3 * n + 1:]
        g, late = s_ref[0], l_ref[0]
        for d in range(1, 8):
            g = g + s_ref[d]
            late = late + l_ref[d]
        g = jnp.concatenate([g[:late_rows] + late, g[late_rows:]], axis=0)
        g_ref[...] = g
        for i, (_, row, col, size) in enumerate(SMALL_PARTS):
            w_ref, m_ref, v_ref = ins[3 * i:3 * i + 3]
            go_ref, d_ref, m2_ref, v2_ref = outs[4 * i:4 * i + 4]
            piece = g[row:row + 1, col:col + size]
            go_ref[...] = piece
            d_ref[...], m2_ref[...], v2_ref[...] = _adam_update(piece, w_ref[...], m_ref[...], v_ref[...])

    vm = pl.BlockSpec(memory_space=pltpu.VMEM)
    out_shape = [jax.ShapeDtypeStruct(slots.shape[1:], F32)]
    for _, _, _, size in SMALL_PARTS:
        out_shape += [jax.ShapeDtypeStruct((1, size), F32)] * 4
    res = pl.pallas_call(body, name="small_sum_adamw", in_specs=[vm] * (2 + 3 * n), out_specs=[vm] * len(out_shape),
                         out_shape=out_shape)(slots, late_slots, *[a for wmv in vectors for a in wmv])
    return res[0], [res[1 + 4 * i:5 + 4 * i] for i in range(n)]


def _mesh_pos():
    return lax.axis_index("x"), lax.axis_index("y"), lax.axis_index("c")


def _other_chips(x, y):
    return [(1 - x, y), (x, 1 - y), (1 - x, 1 - y)]


def _half(ref, c, axis):
    n = ref.shape[axis] // 2
    return ref.at[(slice(None),) * axis + (pl.ds(c * n, n),)]


def _remote(src, dst, send_sem, recv_sem, device):
    return pltpu.make_async_remote_copy(src_ref=src, dst_ref=dst, send_sem=send_sem, recv_sem=recv_sem,
                                        device_id=device, device_id_type=MESH)


def _gather_plan(split, whole=(), axes=None):
    split, whole = list(split), list(whole)
    ns, n = len(split), len(split) + len(whole)

    def make(ins, outs, sems):
        ici_send, ici_recv, d2d_send, d2d_recv, own_send, own_recv = sems
        x, y, c = _mesh_pos()
        mine = 2 * x + y
        chips = _other_chips(x, y)
        blocks = [2 * px + py for px, py in chips]

        def own(a):
            return _remote(ins[a], outs[a].at[mine], own_send.at[a], own_recv.at[a], (x, y, 1 - c))

        def ici(a, k, block):
            px, py = chips[k]
            src, dst = ins[a], outs[a].at[block]
            if a < ns:
                src, dst = _half(src, c, axes[a]), _half(dst, c, axes[a])
            return _remote(src, dst, ici_send.at[3 * a + k], ici_recv.at[3 * a + k], (px, py, c))

        def d2d(a, k, half):
            part = _half(outs[a].at[blocks[k]], half, axes[a])
            return _remote(part, part, d2d_send.at[3 * a + k], d2d_recv.at[3 * a + k], (x, y, 1 - c))

        def start():
            for a in range(n):
                for k in range(3):
                    ici(a, k, mine).start()
                own(a).start()

        def relay():
            for a in range(n):
                for k in range(3):
                    ici(a, k, blocks[k]).wait_recv()
                    if a < ns:
                        d2d(a, k, c).start()

        def finish():
            for a in range(ns):
                for k in range(3):
                    d2d(a, k, 1 - c).wait_recv()
            for a in range(n):
                for k in range(3):
                    ici(a, k, mine).wait_send()
                    if a < ns:
                        d2d(a, k, c).wait_send()
                own(a).wait()

        return start, relay, finish

    arrays = split + whole
    axes = [0] * ns if axes is None else list(axes)
    return _Plan(arrays, [jax.ShapeDtypeStruct((N_CHIPS,) + s.shape, s.dtype) for s in arrays],
                 [pltpu.SemaphoreType.DMA((3 * n,)), pltpu.SemaphoreType.DMA((3 * n,)),
                  pltpu.SemaphoreType.DMA((3 * ns,)), pltpu.SemaphoreType.DMA((3 * ns,)),
                  pltpu.SemaphoreType.DMA((n,)), pltpu.SemaphoreType.DMA((n,))], make)


def _to_sibling_plan(gs):
    n = len(gs)

    def make(ins, outs, sems):
        send_sems, recv_sems = sems
        x, y, c = _mesh_pos()

        def copy(a):
            return _remote(_half(ins[a], 1 - c, len(ins[a].shape) - 1), outs[a], send_sems.at[a],
                           recv_sems.at[a], (x, y, 1 - c))

        def start():
            for a in range(n):
                copy(a).start()

        def finish():
            for a in range(n):
                copy(a).wait()

        return start, finish

    return _Plan(list(gs), [jax.ShapeDtypeStruct(g.shape[:-1] + (g.shape[-1] // 2,), g.dtype) for g in gs],
                 [pltpu.SemaphoreType.DMA((n,)), pltpu.SemaphoreType.DMA((n,))], make)


def _chip_exchange_plan(ps):
    n = len(ps)

    def make(ins, outs, sems):
        send_sems, recv_sems = sems
        x, y, c = _mesh_pos()
        chips = _other_chips(x, y)

        def ici(a, k):
            px, py = chips[k]
            return _remote(ins[a].at[2 * px + py], outs[a].at[k], send_sems.at[3 * a + k],
                           recv_sems.at[3 * a + k], (px, py, c))

        def start():
            for a in range(n):
                for k in range(3):
                    ici(a, k).start()

        def finish():
            for a in range(n):
                for k in range(3):
                    ici(a, k).wait()

        return start, finish

    return _Plan(list(ps), [jax.ShapeDtypeStruct((3,) + p.shape[1:], p.dtype) for p in ps],
                 [pltpu.SemaphoreType.DMA((3 * n,)), pltpu.SemaphoreType.DMA((3 * n,))], make)


def _share_plan(halves):
    n = len(halves)

    def make(ins, outs, sems):
        send_sems, recv_sems = sems
        x, y, c = _mesh_pos()

        def d2d(a):
            return _remote(ins[a], outs[a], send_sems.at[a], recv_sems.at[a], (x, y, 1 - c))

        def start():
            for a in range(n):
                d2d(a).start()

        def finish():
            for a in range(n):
                d2d(a).wait()

        return start, finish

    return _Plan(list(halves), [jax.ShapeDtypeStruct(p.shape, p.dtype) for p in halves],
                 [pltpu.SemaphoreType.DMA((n,)), pltpu.SemaphoreType.DMA((n,))], make)


def _all_to_all_plan(part):
    def make(ins, outs, sems):
        send_sems, recv_sems, local_sem = sems
        (p_ref,), (slots,) = ins, outs
        x, y, c = _mesh_pos()
        me = 4 * x + 2 * y + c
        peers = [(px, py, pc) for px in (x, 1 - x) for py in (y, 1 - y) for pc in (c, 1 - c)][1:]

        def remote(k, slot):
            return _remote(p_ref, slots.at[slot], send_sems.at[k], recv_sems.at[k], peers[k])

        def local():
            return pltpu.make_async_copy(p_ref, slots.at[me], local_sem)

        def start():
            for k in range(7):
                remote(k, me).start()
            local().start()

        def finish():
            for k, (px, py, pc) in enumerate(peers):
                remote(k, 4 * px + 2 * py + pc).wait_recv()
            for k in range(7):
                remote(k, me).wait_send()
            local().wait()

        return start, finish

    return _Plan([part], [jax.ShapeDtypeStruct((8,) + part.shape, part.dtype)],
                 [pltpu.SemaphoreType.DMA((7,)), pltpu.SemaphoreType.DMA((7,)), pltpu.SemaphoreType.DMA(())], make)


def _merge_plans(a, b):
    na_in, na_out, na_sems = len(a.arrays), len(a.out_shape), len(a.sems)

    def make(ins, outs, sems):
        phases_a = _phases(a.make(ins[:na_in], outs[:na_out], sems[:na_sems]))
        phases_b = _phases(b.make(ins[na_in:], outs[na_out:], sems[na_sems:]))

        def both(i):
            def run():
                phases_a[i]()
                phases_b[i]()
            return run

        return both(0), both(1), both(2)

    return _Plan(list(a.arrays) + list(b.arrays), list(a.out_shape) + list(b.out_shape),
                 list(a.sems) + list(b.sems), make)


def _exchange(plan, name):
    n_in, n_out = len(plan.arrays), len(plan.out_shape)

    def body(*refs):
        for phase in _phases(plan.make(refs[:n_in], refs[n_in:n_in + n_out], refs[n_in + n_out:])):
            phase()

    return pl.pallas_call(
        body, name=name, in_specs=[HBM_SPEC] * n_in, out_specs=[HBM_SPEC] * n_out, out_shape=list(plan.out_shape),
        scratch_shapes=list(plan.sems), compiler_params=pltpu.CompilerParams(has_side_effects=True),
    )(*plan.arrays)


def _pack_small(parts):
    rows = []
    for r in range(SMALL_ROWS):
        pieces, col = [], 0
        for name, row, start, size in SMALL_PARTS:
            if row == r:
                assert start == col
                pieces.append(parts[name].reshape(1, size).astype(F32))
                col += size
        rows.append(jnp.concatenate(pieces + [jnp.zeros((1, D - col), F32)], axis=1))
    return jnp.concatenate(rows, axis=0)


def _columns(gathered):
    return jnp.concatenate([gathered[j] for j in range(N_CHIPS)], axis=1)


def kernel(x, meta_tokens, norm_mix_g, w_in, conv_w, conv_b, conv_ln_g, conv_ln_b, gla_w_gate2, gla_gate_b, gla_norm_g, w_out, norm_ffn_g, w_ffn_gate, w_ffn_up, w_ffn_down, norm_final_g, loss_target, m_meta_tokens, m_norm_mix_g, m_w_in, m_conv_w, m_conv_b, m_conv_ln_g, m_conv_ln_b, m_gla_w_gate2, m_gla_gate_b, m_gla_norm_g, m_w_out, m_norm_ffn_g, m_w_ffn_gate, m_w_ffn_up, m_w_ffn_down, m_norm_final_g, v_meta_tokens, v_norm_mix_g, v_w_in, v_conv_w, v_conv_b, v_conv_ln_g, v_conv_ln_b, v_gla_w_gate2, v_gla_gate_b, v_gla_norm_g, v_w_out, v_norm_ffn_g, v_w_ffn_gate, v_w_ffn_up, v_w_ffn_down, v_norm_final_g):
    ws = dict(zip(WEIGHT_NAMES, (meta_tokens, norm_mix_g, w_in, conv_w, conv_b, conv_ln_g, conv_ln_b, gla_w_gate2,
                                 gla_gate_b, gla_norm_g, w_out, norm_ffn_g, w_ffn_gate, w_ffn_up, w_ffn_down,
                                 norm_final_g)))
    ms = dict(zip(WEIGHT_NAMES, (m_meta_tokens, m_norm_mix_g, m_w_in, m_conv_w, m_conv_b, m_conv_ln_g, m_conv_ln_b,
                                 m_gla_w_gate2, m_gla_gate_b, m_gla_norm_g, m_w_out, m_norm_ffn_g, m_w_ffn_gate,
                                 m_w_ffn_up, m_w_ffn_down, m_norm_final_g)))
    vs = dict(zip(WEIGHT_NAMES, (v_meta_tokens, v_norm_mix_g, v_w_in, v_conv_w, v_conv_b, v_conv_ln_g, v_conv_ln_b,
                                 v_gla_w_gate2, v_gla_gate_b, v_gla_norm_g, v_w_out, v_norm_ffn_g, v_w_ffn_gate,
                                 v_w_ffn_up, v_w_ffn_down, v_norm_final_g)))
    c = lax.axis_index("c")
    mine = 2 * lax.axis_index("x") + lax.axis_index("y")
    shard = lambda d, name: d[name].reshape(d[name].shape[-2:])
    vec = {name: ws[name].reshape(1, -1) for name, _, _, _ in SMALL_PARTS}
    n_ex, seq, _ = x.shape
    lp = HEAD_ROWS + seq
    t = n_ex * lp

    (tgt, h0, gate_s, up_s, out_s, down_s), (w_in_g, meta_g, conv_w_g, w2_g) = _pad_head_rows(
        [loss_target, x],
        [shard(ws, "w_ffn_gate").T, shard(ws, "w_ffn_up").T, shard(ws, "w_out"), shard(ws, "w_ffn_down")],
        plan=_gather_plan([shard(ws, "w_in").T.astype(BF16)],
                          [shard(ws, "meta_tokens"), shard(ws, "conv_w"), shard(ws, "gla_w_gate2")], axes=[1]))
    w_in_t = jnp.concatenate([w_in_g.reshape(D_IN, D), jnp.zeros((D_IN_PAD - D_IN, D), BF16)], axis=0)
    conv_w_full = jnp.concatenate([_columns(conv_w_g), jnp.zeros((32 - CONV_W, C_CONV), F32)], axis=0)
    w2_full = jnp.concatenate([_columns(w2_g), jnp.zeros((128 - RANK, GLA_K), F32)], axis=0).astype(BF16)
    h0 = _set_meta_rows(h0, _columns(meta_g)).reshape(t, D)
    tgt = tgt.reshape(t, D)
    row_mask = jnp.concatenate([jnp.zeros((n_ex, HEAD_ROWS, 1), F32), jnp.ones((n_ex, seq, 1), F32)],
                               axis=1).reshape(t, 1)

    (u, hn), (gate_g,) = _in_proj(h0, vec["norm_mix_g"], w_in_t.T, plan=_gather_plan([gate_s]))
    (yc, y_conv), (up_g, w_out_g) = _conv_fwd(
        u, conv_w_full, vec["conv_b"], vec["conv_ln_g"], vec["conv_ln_b"], n_ex, lp,
        plan=_gather_plan([up_s, out_s]))
    (y_gla, states), _ = _gla_fwd(u, w2_full, vec["gla_gate_b"], vec["gla_norm_g"], n_ex, lp)
    w_out_full = w_out_g.reshape(D, D)
    w_gate_t, w_up_t = gate_g.reshape(D_FF, D), up_g.reshape(D_FF, D)
    (h1, hn2, gate, up, act), (down_g,) = _mix_out_ffn_up(
        h0, y_conv, y_gla, w_out_full, vec["norm_ffn_g"], w_gate_t.T, w_up_t.T,
        plan=_gather_plan([down_s]))
    w_down_full = down_g.reshape(D_FF, D)
    dh2, loss, d_final_g = _ffn_down_loss(act, w_down_full, h1, tgt, vec["norm_final_g"], row_mask)
    dgate, dup, dh1, dycat, d_ffn_g = _ffn_bwd(dh2, gate, up, h1, w_down_full.T, w_gate_t, w_up_t, w_out_full.T,
                                                vec["norm_ffn_g"])

    ffn_block = lambda g: g.reshape(N_CHIPS, D_FF // N_CHIPS, D)
    g_gate = ffn_block(_wgrad(dgate, hn2, "wgrad_gate")[0])
    g_up, (gate_sib,) = _wgrad(dup, hn2, "wgrad_up", _to_sibling_plan([g_gate]))
    g_up = ffn_block(g_up)
    g_down, (up_sib,) = _wgrad(act, dh2, "wgrad_down", _to_sibling_plan([g_up]))
    g_down = ffn_block(g_down)
    g_out = _wgrad_pair(y_conv, y_gla, dh1, "wgrad_out").reshape(N_CHIPS, D // N_CHIPS, D)
    cs_gate, cs_up = _rs_add_halves([(g_gate, gate_sib), (g_up, up_sib)], c, "rs_add_gate_up")
    (du_conv, d_conv_w, d_conv_b, d_ln_g, d_ln_b), (ex_gate, ex_up, down_sib, out_sib) = _conv_bwd(
        dycat, yc, u, conv_w_full, vec["conv_ln_g"], vec["conv_ln_b"], n_ex, lp,
        plan=_merge_plans(_chip_exchange_plan([cs_gate, cs_up]), _to_sibling_plan([g_down, g_out])))
    cs_down, cs_out = _rs_add_halves([(g_down, down_sib), (g_out, out_sib)], c, "rs_add_down_out")
    (du_gla, d_w2, d_gate_b, d_norm_g), (ex_down, ex_out) = _gla_bwd(
        dycat, u, states, w2_full, vec["gla_gate_b"], vec["gla_norm_g"], n_ex, lp,
        plan=_chip_exchange_plan([cs_down, cs_out]))
    halves = _rs_sum([(cs_gate, ex_gate), (cs_up, ex_up), (cs_down, ex_down), (cs_out, ex_out)], mine,
                     "rs_sum_early")

    small = {"norm_mix_g": jnp.zeros((1, D), F32), "norm_ffn_g": d_ffn_g, "norm_final_g": d_final_g,
             "conv_b": d_conv_b, "conv_ln_g": d_ln_g, "conv_ln_b": d_ln_b, "gla_gate_b": d_gate_b,
             "gla_norm_g": d_norm_g}
    part = lax.dynamic_update_slice(_pack_small(small), loss[:, :1], (LOSS_ROW, 0))
    part = jnp.concatenate([part, jnp.zeros((N_META, D), F32), d_conv_w.reshape(16, D), d_w2[:RANK].reshape(4, D),
                            jnp.zeros((4, D), F32)], axis=0)
    g_in_conv = _wgrad(du_conv, hn, "wgrad_in_conv")[0][None]
    g_in_gla, (slots, conv_sib) = _wgrad(du_gla, hn, "wgrad_in_gla",
                                         _merge_plans(_all_to_all_plan(part), _to_sibling_plan([g_in_conv])))
    pieces = [g_in_conv, g_in_gla[None]]
    (gla_sib,) = _exchange(_to_sibling_plan(pieces[1:]), "rs_late_to_sibling")
    sums = _rs_add_halves(list(zip(pieces, (conv_sib, gla_sib))), c, "rs_add_w_in")
    in_chip_sum = jnp.concatenate([sums[0][0], sums[1][0]], axis=0)[:D_IN].reshape(N_CHIPS, D_IN // N_CHIPS, D // 2)
    (dh0, d_mix_g), shared = _in_proj_bwd(
        du_conv, du_gla, w_in_t[:2 * C_CONV], w_in_t[2 * C_CONV:], h0, dh1, vec["norm_mix_g"],
        plan=_merge_plans(_share_plan(halves), _chip_exchange_plan([in_chip_sum])))
    dh0 = dh0.reshape(n_ex, lp, D)
    grad_x = dh0[:, HEAD_ROWS:]
    late_part = jnp.concatenate([d_mix_g, jnp.zeros((SMALL_ROWS - 1, D), F32),
                                 jnp.sum(dh0[:, PAD_ROWS:HEAD_ROWS], axis=0)], axis=0)
    (in_half,) = _rs_sum([(in_chip_sum, shared[4])], mine, "rs_sum_w_in")
    in_shared, late_slots = _exchange(_merge_plans(_share_plan([in_half]), _all_to_all_plan(late_part)),
                                      "late_exchange")

    out = {"grad": {}, "delta": {}, "new_m": {}, "new_v": {}}

    def record(name, res, transposed=False):
        for kind, a in zip(("grad", "delta", "new_m", "new_v"), res):
            out[kind][name] = (a.T if transposed else a).reshape(ws[name].shape)

    def operands(name, transposed):
        lay = (lambda a: a.T) if transposed else (lambda a: a)
        return lay(shard(ws, name)), lay(shard(ms, name)), lay(shard(vs, name))

    early_layout = (("w_ffn_gate", True), ("w_ffn_up", True), ("w_ffn_down", False), ("w_out", False))
    items = [(mine_half, their_half, *operands(name, transposed))
             for (name, transposed), mine_half, their_half in zip(early_layout, halves, shared)]
    for (name, transposed), res in zip(early_layout, _adamw_halves(items, c, "adamw_early")):
        record(name, res, transposed)

    tile_rows = lambda a: a.reshape(a.shape[0], 1, a.shape[1])
    by_output = lambda d: jnp.transpose(d["w_in"], (2, 0, 1))
    res = _adamw_halves([(tile_rows(in_half), tile_rows(in_shared), by_output(ws), by_output(ms), by_output(vs))],
                        c, "adamw_w_in")[0]
    for kind, a in zip(("grad", "delta", "new_m", "new_v"), res):
        out[kind]["w_in"] = jnp.transpose(a, (1, 2, 0))

    flat = lambda d, name: d[name].reshape(1, -1)
    g_s, updated = _sum_slots_adamw(slots, late_slots,
                                    [(flat(ws, name), flat(ms, name), flat(vs, name)) for name, _, _, _ in SMALL_PARTS])
    for (name, _, _, _), res in zip(SMALL_PARTS, updated):
        record(name, res)
    loss = g_s[LOSS_ROW, 0]
    block = lambda a, width: lax.dynamic_slice_in_dim(a, mine * width, width, axis=1)
    small_sharded = {"meta_tokens": block(g_s[8:24], D // N_CHIPS),
                     "conv_w": block(g_s[24:40].reshape(32, C_CONV), C_CONV // N_CHIPS)[:CONV_W],
                     "gla_w_gate2": block(g_s[40:44].reshape(RANK, GLA_K), GLA_K // N_CHIPS)}
    for name, g in small_sharded.items():
        record(name, [g, *_adamw(g, *operands(name, False), "adamw_" + name)])

    return (loss, grad_x, *[out[kind][name] for kind in ("grad", "delta", "new_m", "new_v") for name in WEIGHT_NAMES])
```

```python
import functools
from typing import Any, Callable, NamedTuple, Sequence

import jax
import jax.numpy as jnp
from jax import lax
from jax.experimental import pallas as pl
from jax.experimental.pallas import tpu as pltpu

F32 = jnp.float32
BF16 = jnp.bfloat16
MESH = pl.DeviceIdType.MESH

D = 1024
N_META = 16
C_CONV = 512
CONV_W = 31
GLA_K = 256
GLA_V = 512
N_HEADS = 4
DK = 64
DV = 128
RANK = 16
CHUNK = 64
PAD_ROWS = CHUNK - N_META
HEAD_ROWS = CHUNK
D_IN = 2576
D_IN_PAD = 2688
D_GLA_IN = D_IN_PAD - 2 * C_CONV
D_FF = 2816
RMS_EPS = 1e-6
LN_EPS = 1e-5
GATE_TAU = 16.0
N_CHIPS = 4

ADAM_LR = 0.001
ADAM_B1 = 0.9
ADAM_B2 = 0.999
ADAM_EPS = 1e-08
ADAM_WD = 0.01
ADAM_STEP = 10

V7X_VMEM_BYTES = 64 * 1024 * 1024
VMEM_LIMIT = V7X_VMEM_BYTES - 8 * 1024 * 1024
SUBLANES = 8
ROW_PART = 128
FFN_BWD_TILE = 192

WEIGHT_NAMES = ("meta_tokens", "norm_mix_g", "w_in", "conv_w", "conv_b", "conv_ln_g", "conv_ln_b", "gla_w_gate2",
                "gla_gate_b", "gla_norm_g", "w_out", "norm_ffn_g", "w_ffn_gate", "w_ffn_up", "w_ffn_down",
                "norm_final_g")

SMALL_ROWS = 8
SMALL_PARTS = (("norm_mix_g", 0, 0, D), ("norm_ffn_g", 1, 0, D), ("norm_final_g", 2, 0, D),
               ("conv_b", 3, 0, C_CONV), ("conv_ln_g", 3, C_CONV, C_CONV), ("conv_ln_b", 4, 0, C_CONV),
               ("gla_gate_b", 4, C_CONV, GLA_K), ("gla_norm_g", 4, C_CONV + GLA_K, DV))
LOSS_ROW = 5

HBM_SPEC = pl.BlockSpec(memory_space=pltpu.HBM)


def _dot(a, b):
    return jnp.dot(a, b, preferred_element_type=F32)


def _dot_nt(a, b):
    return lax.dot_general(a, b, (((1,), (1,)), ((), ())), preferred_element_type=F32)


def _dot_tn(a, b):
    return lax.dot_general(a, b, (((0,), (0,)), ((), ())), preferred_element_type=F32)


def _sigmoid(x):
    return 1.0 / (1.0 + jnp.exp(-x))


def _const_spec(shape):
    return pl.BlockSpec(shape, lambda *_: (0,) * len(shape), pipeline_mode=pl.Buffered(1))


def _acc_spec(shape):
    return pl.BlockSpec(shape, lambda *_: (0,) * len(shape))


def _params(n_axes):
    return pltpu.CompilerParams(dimension_semantics=("arbitrary",) * n_axes, vmem_limit_bytes=VMEM_LIMIT)


def _row_tile(t, want):
    for r in (want, 384, 192, 128, 64):
        if r <= want and t % r == 0:
            return r
    raise ValueError(f"no row tile for {t}")


def _row_parts(r):
    if r % ROW_PART:
        return [slice(None)]
    return [pl.ds(i * ROW_PART, ROW_PART) for i in range(r // ROW_PART)]


def _in_lockstep(bodies):
    live = list(bodies)
    while live:
        still = []
        for g in live:
            try:
                next(g)
                still.append(g)
            except StopIteration:
                pass
        live = still


class _Plan(NamedTuple):
    arrays: Sequence[Any]
    out_shape: Sequence[Any]
    sems: Sequence[Any]
    make: Callable


def _phases(made):
    return made if len(made) == 3 else (made[0], lambda: None, made[1])


def _call(body, *, name, grid, in_specs, out_specs, out_shape, scratch_shapes=(), plan=None):
    n_in, n_out, n_scr = len(in_specs), len(out_specs), len(scratch_shapes)
    if plan is None:
        plan = _Plan([], [], [], lambda ins, outs, sems: (lambda: None, lambda: None))
    nx_in, nx_out = len(plan.arrays), len(plan.out_shape)
    n_steps = functools.reduce(lambda a, b: a * b, grid)

    def hosted(*refs):
        ins, xins = refs[:n_in], refs[n_in:n_in + nx_in]
        o0 = n_in + nx_in
        outs, xouts = refs[o0:o0 + n_out], refs[o0 + n_out:o0 + n_out + nx_out]
        s0 = o0 + n_out + nx_out
        scr, sems = refs[s0:s0 + n_scr], refs[s0 + n_scr:]
        step = functools.reduce(lambda acc, a: acc * grid[a] + pl.program_id(a), range(len(grid)), 0)
        start, relay, finish = _phases(plan.make(xins, xouts, sems))
        pl.when(step == 0)(start)
        pl.when(step == n_steps - 1)(relay)
        body(*ins, *outs, *scr)
        pl.when(step == n_steps - 1)(finish)

    call = pl.pallas_call(
        hosted, name=name, grid=grid, in_specs=list(in_specs) + [HBM_SPEC] * nx_in,
        out_specs=list(out_specs) + [HBM_SPEC] * nx_out, out_shape=list(out_shape) + list(plan.out_shape),
        scratch_shapes=list(scratch_shapes) + list(plan.sems),
        compiler_params=pltpu.CompilerParams(dimension_semantics=("arbitrary",) * len(grid),
                                             vmem_limit_bytes=VMEM_LIMIT, has_side_effects=nx_in > 0))

    def run(*args):
        res = call(*args, *plan.arrays)
        return res[:n_out], res[n_out:]

    return run


def _pad_head_rows(arrays, casts, plan=None):
    n_ex, seq, _ = arrays[0].shape
    nc = (HEAD_ROWS + seq) // CHUNK
    n, k = len(arrays), len(casts)

    def body(*refs):
        ins, outs = refs[:n + k], refs[n + k:]
        for a_ref, o_ref in zip(ins[:n], outs[:n]):
            o_ref[...] = jnp.where(pl.program_id(0) > 0, a_ref[...], 0.0)

        @pl.when(pl.program_id(0) == 0)
        def _():
            for a_ref, o_ref in zip(ins[n:], outs[n:]):
                o_ref[...] = a_ref[...].astype(BF16)

    whole = lambda a: pl.BlockSpec(a.shape, lambda i: (0, 0))
    return _call(
        body, name="pad_head_rows", grid=(nc,),
        in_specs=([pl.BlockSpec((n_ex, CHUNK, D), lambda i: (0, jnp.maximum(i - 1, 0), 0))] * n
                  + [_const_spec(a.shape) for a in casts]),
        out_specs=[pl.BlockSpec((n_ex, CHUNK, D), lambda i: (0, i, 0))] * n + [whole(a) for a in casts],
        out_shape=([jax.ShapeDtypeStruct((n_ex, HEAD_ROWS + seq, D), F32)] * n
                   + [jax.ShapeDtypeStruct(a.shape, BF16) for a in casts]),
        plan=plan,
    )(*arrays, *casts)


def _set_meta_rows(h0, meta):
    n_ex = h0.shape[0]

    def body(h_ref, meta_ref, o_ref):
        o_ref[...] = jnp.concatenate(
            [h_ref[:, :PAD_ROWS, :], jnp.broadcast_to(meta_ref[...][None], (n_ex, N_META, D))], axis=1)

    head = pl.BlockSpec((n_ex, HEAD_ROWS, D), lambda i: (0, 0, 0))
    return pl.pallas_call(
        body, name="set_meta_rows", grid=(1,), in_specs=[head, pl.BlockSpec((N_META, D), lambda i: (0, 0))],
        out_specs=head, out_shape=jax.ShapeDtypeStruct(h0.shape, F32), input_output_aliases={0: 0},
        compiler_params=_params(1),
    )(h0, meta)


def _in_proj(h0, g_mix, w_in, plan=None):
    t = h0.shape[0]
    r = _row_tile(t, 384)

    def body(h_ref, g_ref, w_ref, u_ref, hn_ref):
        h = h_ref[...]
        rstd = lax.rsqrt(jnp.mean(h * h, axis=-1, keepdims=True) + RMS_EPS)
        hn = (h * rstd * g_ref[...]).astype(BF16)
        hn_ref[...] = hn
        u_ref[...] = _dot(hn, w_ref[...])

    return _call(
        body, name="in_proj", grid=(t // r,),
        in_specs=[pl.BlockSpec((r, D), lambda i: (i, 0)), _const_spec((1, D)), _const_spec((D, D_IN_PAD))],
        out_specs=[pl.BlockSpec((r, D_IN_PAD), lambda i: (i, 0)), pl.BlockSpec((r, D), lambda i: (i, 0))],
        out_shape=[jax.ShapeDtypeStruct((t, D_IN_PAD), F32), jax.ShapeDtypeStruct((t, D), BF16)],
        plan=plan,
    )(h0, g_mix, w_in)


CONV_TILE = 192
CONV_SUB = 32
CONV_LEAD = CONV_SUB - (CONV_W - 1)


def _shifted_copies(src, dst, r):
    for s in range(1, SUBLANES):
        dst[s - 1] = src[s:s + r + CONV_SUB - SUBLANES, :]


def _shifted_rows(src, shifted, start):
    base, s = SUBLANES * (start // SUBLANES), start % SUBLANES
    if s == 0:
        return src[base:base + CONV_SUB, :]
    return shifted[s - 1, base:base + CONV_SUB, :]


def _conv_fwd(u, conv_w, conv_b, ln_g, ln_b, n_ex, lp, plan=None):
    r = CONV_TILE
    nt = lp // r
    hb = r // CONV_SUB

    def body(cur_ref, prev_ref, w_ref, b_ref, lg_ref, lb_ref, yc_ref, y_ref, glu, glu_sh):
        i = pl.program_id(1)
        cur = cur_ref[...]
        glu[CONV_SUB:CONV_SUB + r, :] = cur[:, :C_CONV] * _sigmoid(cur[:, C_CONV:])
        pv = prev_ref[...]
        halo = pv[:, :C_CONV] * _sigmoid(pv[:, C_CONV:])
        glu[0:CONV_SUB, :] = jnp.where(i > 0, halo, 0.0)
        _shifted_copies(glu, glu_sh, r)
        w = w_ref[...]
        for j in range(r // CONV_SUB):
            r0 = j * CONV_SUB
            acc = jnp.zeros((CONV_SUB, C_CONV), F32) + b_ref[...]
            for k in range(CONV_W):
                acc = acc + w[k:k + 1, :] * _shifted_rows(glu, glu_sh, r0 + CONV_LEAD + k)
            mu = jnp.mean(acc, axis=-1, keepdims=True)
            cen = acc - mu
            var = jnp.mean(cen * cen, axis=-1, keepdims=True)
            out = cen * lax.rsqrt(var + LN_EPS) * lg_ref[...] + lb_ref[...]
            y = out * _sigmoid(out)
            row = i * r + r0 + lax.broadcasted_iota(jnp.int32, (CONV_SUB, 1), 0)
            y = jnp.where(row >= PAD_ROWS, y, 0.0)
            yc_ref[r0:r0 + CONV_SUB, :] = acc
            y_ref[r0:r0 + CONV_SUB, :] = y.astype(BF16)

    t = n_ex * lp
    return _call(
        body, name="conv_fwd", grid=(n_ex, nt),
        in_specs=[pl.BlockSpec((r, 2 * C_CONV), lambda b, i: (b * nt + i, 0)),
                  pl.BlockSpec((CONV_SUB, 2 * C_CONV), lambda b, i: (jnp.maximum((b * nt + i) * hb - 1, 0), 0)),
                  _const_spec((32, C_CONV)), _const_spec((1, C_CONV)), _const_spec((1, C_CONV)), _const_spec((1, C_CONV))],
        out_specs=[pl.BlockSpec((r, C_CONV), lambda b, i: (b * nt + i, 0)),
                   pl.BlockSpec((r, C_CONV), lambda b, i: (b * nt + i, 0))],
        out_shape=[jax.ShapeDtypeStruct((t, C_CONV), F32), jax.ShapeDtypeStruct((t, C_CONV), BF16)],
        scratch_shapes=[pltpu.VMEM((r + CONV_SUB, C_CONV), F32),
                        pltpu.VMEM((SUBLANES - 1, r + CONV_SUB - SUBLANES, C_CONV), F32)],
        plan=plan,
    )(u, u, conv_w, conv_b, ln_g, ln_b)


def _mix_out_ffn_up(h0, y_conv, y_gla, w_out, g_ffn, w_gate_t, w_up_t, plan=None):
    t = h0.shape[0]
    r = _row_tile(t, 384)

    def body(h0_ref, yc_ref, yg_ref, wo_ref, g_ref, wg_ref, wu_ref, h1_ref, hn_ref, gate_ref, up_ref, act_ref):
        h1 = h0_ref[...] + _dot(yc_ref[...], wo_ref[0:C_CONV, :]) + _dot(yg_ref[...], wo_ref[C_CONV:D, :])
        h1_ref[...] = h1
        rstd = lax.rsqrt(jnp.mean(h1 * h1, axis=-1, keepdims=True) + RMS_EPS)
        hn = (h1 * rstd * g_ref[...]).astype(BF16)
        hn_ref[...] = hn
        gate = _dot_nt(hn, wg_ref[...])
        up = _dot_nt(hn, wu_ref[...])
        gate_ref[...] = gate
        up_ref[...] = up
        act_ref[...] = (gate * _sigmoid(gate) * up).astype(BF16)

    rows = lambda w: pl.BlockSpec((r, w), lambda i: (i, 0))
    return _call(
        body, name="mix_out_ffn_up", grid=(t // r,),
        in_specs=[rows(D), rows(C_CONV), rows(GLA_V), _const_spec((D, D)), _const_spec((1, D)),
                  _const_spec((D_FF, D)), _const_spec((D_FF, D))],
        out_specs=[rows(D), rows(D), rows(D_FF), rows(D_FF), rows(D_FF)],
        out_shape=[jax.ShapeDtypeStruct((t, D), F32), jax.ShapeDtypeStruct((t, D), BF16),
                   jax.ShapeDtypeStruct((t, D_FF), F32), jax.ShapeDtypeStruct((t, D_FF), F32),
                   jax.ShapeDtypeStruct((t, D_FF), BF16)],
        plan=plan,
    )(h0, y_conv, y_gla, w_out, g_ffn, w_gate_t, w_up_t)


def _ffn_down_loss(act, w_down, h1, target, g_final, row_mask):
    t = h1.shape[0]
    r = _row_tile(t, 384)

    def body(act_ref, wd_ref, h1_ref, tgt_ref, gf_ref, mask_ref, dh2_ref, loss_ref, dgf_ref):
        @pl.when(pl.program_id(0) == 0)
        def _():
            loss_ref[...] = jnp.zeros_like(loss_ref)
            dgf_ref[...] = jnp.zeros_like(dgf_ref)

        gf = gf_ref[...]

        def part(rows):
            h2 = h1_ref[rows, :] + _dot(act_ref[rows, :], wd_ref[...])
            yield
            rstd = lax.rsqrt(jnp.mean(h2 * h2, axis=-1, keepdims=True) + RMS_EPS)
            nrm = h2 * rstd
            err = (nrm * gf - tgt_ref[rows, :]) * mask_ref[rows, :]
            loss_ref[...] += jnp.sum(err * err) * (0.5 / D)
            dy = err * (1.0 / D)
            dgf_ref[...] += jnp.sum(dy * nrm, axis=0, keepdims=True)
            dn = dy * gf
            dh2_ref[rows, :] = rstd * (dn - nrm * jnp.mean(dn * nrm, axis=-1, keepdims=True))

        _in_lockstep(part(rows) for rows in _row_parts(r))

    rows = lambda w: pl.BlockSpec((r, w), lambda i: (i, 0))
    return pl.pallas_call(
        body, name="ffn_down_loss", grid=(t // r,),
        in_specs=[rows(D_FF), _const_spec((D_FF, D)), rows(D), rows(D), _const_spec((1, D)), rows(1)],
        out_specs=[rows(D), _acc_spec((1, 128)), _acc_spec((1, D))],
        out_shape=[jax.ShapeDtypeStruct((t, D), F32), jax.ShapeDtypeStruct((1, 128), F32),
                   jax.ShapeDtypeStruct((1, D), F32)],
        compiler_params=_params(1),
    )(act, w_down, h1, target, g_final, row_mask)


def _ffn_bwd(dh2, gate, up, h1, w_down_t, w_gate_t, w_up_t, w_out_t, g_ffn):
    t = h1.shape[0]
    r = _row_tile(t, FFN_BWD_TILE)

    def body(dh2_ref, gate_ref, up_ref, h1_ref, wd_ref, wg_ref, wu_ref, wo_ref, g_ref,
             dgate_ref, dup_ref, dh1_ref, dycat_ref, dg_ref):
        @pl.when(pl.program_id(0) == 0)
        def _():
            dg_ref[...] = jnp.zeros_like(dg_ref)

        dh2 = dh2_ref[...]
        dact = _dot(dh2.astype(BF16), wd_ref[...])
        gate = gate_ref[...]
        sg = _sigmoid(gate)
        dgate = (dact * up_ref[...] * (sg * (1.0 + gate * (1.0 - sg)))).astype(BF16)
        dup = (dact * (gate * sg)).astype(BF16)
        dgate_ref[...] = dgate
        dup_ref[...] = dup
        dhn = _dot(dgate, wg_ref[...]) + _dot(dup, wu_ref[...])
        h1 = h1_ref[...]
        rstd = lax.rsqrt(jnp.mean(h1 * h1, axis=-1, keepdims=True) + RMS_EPS)
        nrm = h1 * rstd
        dg_ref[...] += jnp.sum(dhn * nrm, axis=0, keepdims=True)
        dn = dhn * g_ref[...]
        dh1 = dh2 + rstd * (dn - nrm * jnp.mean(dn * nrm, axis=-1, keepdims=True))
        dh1_ref[...] = dh1
        dycat_ref[...] = _dot(dh1.astype(BF16), wo_ref[...])

    rows = lambda w: pl.BlockSpec((r, w), lambda i: (i, 0))
    return pl.pallas_call(
        body, name="ffn_bwd", grid=(t // r,),
        in_specs=[rows(D), rows(D_FF), rows(D_FF), rows(D), _const_spec((D, D_FF)), _const_spec((D_FF, D)),
                  _const_spec((D_FF, D)), _const_spec((D, D)), _const_spec((1, D))],
        out_specs=[rows(D_FF), rows(D_FF), rows(D), rows(D), _acc_spec((1, D))],
        out_shape=[jax.ShapeDtypeStruct((t, D_FF), BF16), jax.ShapeDtypeStruct((t, D_FF), BF16),
                   jax.ShapeDtypeStruct((t, D), F32), jax.ShapeDtypeStruct((t, D), F32),
                   jax.ShapeDtypeStruct((1, D), F32)],
        compiler_params=_params(1),
    )(dh2, gate, up, h1, w_down_t, w_gate_t, w_up_t, w_out_t, g_ffn)


def _conv_bwd(dycat, yc, u, conv_w, ln_g, ln_b, n_ex, lp, plan=None):
    r = CONV_TILE
    nt = lp // r
    hb = r // CONV_SUB
    nsub = r // CONV_SUB

    def ln_bwd(dy, yc_rows, live, lg, lb):
        mu = jnp.mean(yc_rows, axis=-1, keepdims=True)
        cen = yc_rows - mu
        rs = lax.rsqrt(jnp.mean(cen * cen, axis=-1, keepdims=True) + LN_EPS)
        yn = cen * rs
        out = yn * lg + lb
        so = _sigmoid(out)
        dout = jnp.where(live, dy * (so * (1.0 + out * (1.0 - so))), 0.0)
        dyn = dout * lg
        dyc = rs * (dyn - jnp.mean(dyn, axis=-1, keepdims=True) - yn * jnp.mean(dyn * yn, axis=-1, keepdims=True))
        return dyc, dout, yn

    def body(dy_ref, dyn_ref, yc_ref, ycn_ref, cur_ref, prev_ref, w_ref, lg_ref, lb_ref,
             du_ref, dw_ref, db_ref, dlg_ref, dlb_ref, glu, dycs, dwacc, glu_sh, dycs_sh):
        b = pl.program_id(0)
        i = pl.program_id(1)
        first = jnp.logical_and(b == 0, i == 0)

        @pl.when(first)
        def _():
            dwacc[...] = jnp.zeros_like(dwacc)
            db_ref[...] = jnp.zeros_like(db_ref)
            dlg_ref[...] = jnp.zeros_like(dlg_ref)
            dlb_ref[...] = jnp.zeros_like(dlb_ref)

        lg, lb = lg_ref[...], lb_ref[...]
        cur = cur_ref[...]
        sig = _sigmoid(cur[:, C_CONV:])
        glu[CONV_SUB:CONV_SUB + r, :] = cur[:, :C_CONV] * sig
        pv = prev_ref[...]
        glu[0:CONV_SUB, :] = jnp.where(i > 0, pv[:, :C_CONV] * _sigmoid(pv[:, C_CONV:]), 0.0)

        row = i * r + lax.broadcasted_iota(jnp.int32, (r, 1), 0)
        dyc, dout, yn = ln_bwd(dy_ref[...], yc_ref[...], row >= PAD_ROWS, lg, lb)
        dycs[0:r, :] = dyc
        dycn, _, _ = ln_bwd(dyn_ref[...], ycn_ref[...], i < nt - 1, lg, lb)
        dycs[r:r + CONV_SUB, :] = dycn
        db_ref[...] += jnp.sum(dyc, axis=0, keepdims=True)
        dlg_ref[...] += jnp.sum(dout * yn, axis=0, keepdims=True)
        dlb_ref[...] += jnp.sum(dout, axis=0, keepdims=True)

        _shifted_copies(glu, glu_sh, r)
        _shifted_copies(dycs, dycs_sh, r)
        w = w_ref[...]
        for j in range(nsub):
            r0 = j * CONV_SUB
            dblk = dycs[r0:r0 + CONV_SUB, :]
            dglu = jnp.zeros((CONV_SUB, C_CONV), F32)
            for k in range(CONV_W):
                dglu = dglu + w[k:k + 1, :] * _shifted_rows(dycs, dycs_sh, r0 + (CONV_W - 1) - k)
                prod = dblk * _shifted_rows(glu, glu_sh, r0 + CONV_LEAD + k)
                dwacc[k] += prod.reshape(CONV_SUB // SUBLANES, SUBLANES, C_CONV).sum(axis=0)
            sg = sig[r0:r0 + CONV_SUB, :]
            cv = cur[r0:r0 + CONV_SUB, :C_CONV]
            du_ref[r0:r0 + CONV_SUB, :C_CONV] = (dglu * sg).astype(BF16)
            du_ref[r0:r0 + CONV_SUB, C_CONV:] = (dglu * cv * sg * (1.0 - sg)).astype(BF16)

        @pl.when(jnp.logical_and(b == n_ex - 1, i == nt - 1))
        def _():
            dw_ref[...] = jnp.sum(dwacc[...], axis=1)

    t = n_ex * lp
    cur_rows = lambda w, col: pl.BlockSpec((r, w), lambda b, i: (b * nt + i, col))
    nxt_rows = lambda w, col: pl.BlockSpec(
        (CONV_SUB, w), lambda b, i: (jnp.minimum((b * nt + i + 1) * hb, n_ex * nt * hb - 1), col))
    return _call(
        body, name="conv_bwd", grid=(n_ex, nt),
        in_specs=[cur_rows(C_CONV, 0), nxt_rows(C_CONV, 0), cur_rows(C_CONV, 0), nxt_rows(C_CONV, 0),
                  cur_rows(2 * C_CONV, 0),
                  pl.BlockSpec((CONV_SUB, 2 * C_CONV), lambda b, i: (jnp.maximum((b * nt + i) * hb - 1, 0), 0)),
                  _const_spec((32, C_CONV)), _const_spec((1, C_CONV)), _const_spec((1, C_CONV))],
        out_specs=[cur_rows(2 * C_CONV, 0), _acc_spec((32, C_CONV)), _acc_spec((1, C_CONV)),
                   _acc_spec((1, C_CONV)), _acc_spec((1, C_CONV))],
        out_shape=[jax.ShapeDtypeStruct((t, 2 * C_CONV), BF16), jax.ShapeDtypeStruct((32, C_CONV), F32),
                   jax.ShapeDtypeStruct((1, C_CONV), F32), jax.ShapeDtypeStruct((1, C_CONV), F32),
                   jax.ShapeDtypeStruct((1, C_CONV), F32)],
        scratch_shapes=[pltpu.VMEM((r + CONV_SUB, C_CONV), F32), pltpu.VMEM((r + CONV_SUB, C_CONV), F32),
                        pltpu.VMEM((32, SUBLANES, C_CONV), F32),
                        pltpu.VMEM((SUBLANES - 1, r + CONV_SUB - SUBLANES, C_CONV), F32),
                        pltpu.VMEM((SUBLANES - 1, r + CONV_SUB - SUBLANES, C_CONV), F32)],
        plan=plan,
    )(dycat, dycat, yc, yc, u, u, conv_w, ln_g, ln_b)


HEAD_ROWS_ALL = N_HEADS * CHUNK


def _gla_gates(lr, w2, gb, first_chunk):
    z = _dot(lr.astype(BF16), w2) + gb
    a = (jnp.minimum(z, 0.0) - jnp.log(1.0 + jnp.exp(-jnp.abs(z)))) * (1.0 / GATE_TAU)
    row = lax.broadcasted_iota(jnp.int32, (CHUNK, 1), 0)
    live = jnp.logical_or(jnp.logical_not(first_chunk), row >= PAD_ROWS)
    return z, jnp.where(live, a, 0.0), live


def _tri(lower):
    i = lax.broadcasted_iota(jnp.int32, (CHUNK, CHUNK), 0)
    j = lax.broadcasted_iota(jnp.int32, (CHUNK, CHUNK), 1)
    return (i >= j) if lower else (i <= j)


def _head_of(shape, axis, per_head):
    return lax.broadcasted_iota(jnp.int32, shape, axis) // per_head


def _expand(x, lanes_per_head):
    rows, lanes = HEAD_ROWS_ALL, x.shape[1]
    keep = _head_of((rows, lanes), 0, CHUNK) == _head_of((rows, lanes), 1, lanes_per_head)
    return jnp.where(keep, jnp.tile(x, (N_HEADS, 1)), 0.0)


def _expand_lanes(x):
    rows, w = x.shape
    keep = _head_of((rows, N_HEADS * w), 0, CHUNK) == _head_of((rows, N_HEADS * w), 1, w)
    return jnp.where(keep, jnp.tile(x, (1, N_HEADS)), 0.0)


def _expand_state(st):
    rows, lanes = N_HEADS * DV, st.shape[1]
    keep = _head_of((rows, lanes), 0, DV) == _head_of((rows, lanes), 1, DK)
    return jnp.where(keep, jnp.tile(st, (N_HEADS, 1)), 0.0)


def _fold(t, rows_per_head):
    lane_head = _head_of((rows_per_head, t.shape[1]), 1, DK)
    out = jnp.where(lane_head == 0, t[0:rows_per_head], 0.0)
    for h in range(1, N_HEADS):
        out = out + jnp.where(lane_head == h, t[h * rows_per_head:(h + 1) * rows_per_head], 0.0)
    return out


def _rows_by_head(x):
    return jnp.concatenate([x[:, h * DV:(h + 1) * DV] for h in range(N_HEADS)], axis=0)


def _lanes_by_head(x):
    return jnp.concatenate([x[h * CHUNK:(h + 1) * CHUNK] for h in range(N_HEADS)], axis=1)


def _running_sum(a, lower):
    hi = a.astype(BF16)
    rest = a - hi.astype(F32)
    mid = rest.astype(BF16)
    lo = (rest - mid.astype(F32)).astype(BF16)
    w = a.shape[1]
    parts = _dot(_tri(lower).astype(F32).astype(BF16), jnp.concatenate([hi, mid, lo], axis=1))
    return parts[:, :w] + parts[:, w:2 * w] + parts[:, 2 * w:]


def _stacked_causal():
    i = lax.broadcasted_iota(jnp.int32, (HEAD_ROWS_ALL, CHUNK), 0) % CHUNK
    j = lax.broadcasted_iota(jnp.int32, (HEAD_ROWS_ALL, CHUNK), 1)
    return i >= j


GLA_GROUP = 3


def _gla_chunk(q, k, v, lr, w2, gb, first_chunk):
    z, a, live = _gla_gates(lr, w2, gb, first_chunk)
    yield
    b = _running_sum(a, True)
    yield
    bl = b[CHUNK - 1:CHUNK, :]
    e_pos, e_neg, e_dec = jnp.exp(b), jnp.exp(-b), jnp.exp(bl - b)
    q_f, k_f, kd_f = q * (DK ** -0.5) * e_pos, k * e_neg, k * e_dec
    qx = _expand(q_f, DK).astype(BF16)
    k_in, k_dec, v_b = k_f.astype(BF16), kd_f.astype(BF16), v.astype(BF16)
    s = jnp.where(_stacked_causal(), _dot_nt(qx, k_in), 0.0).astype(BF16)
    yield
    p = _dot(s, v_b)
    yield
    o_intra = jnp.concatenate([p[h * CHUNK:(h + 1) * CHUNK, h * DV:(h + 1) * DV] for h in range(N_HEADS)], axis=0)
    return dict(z=z, live=live, bl=bl, e_pos=e_pos, e_neg=e_neg, e_dec=e_dec, q_f=q_f, k_f=k_f, kd_f=kd_f,
                qx=qx, k_in=k_in, k_dec=k_dec, v_b=v_b, s=s, o_intra=o_intra, decay=jnp.exp(bl))


def _gla_fwd(u, w2, gb, ng, n_ex, lp, plan=None):
    nc = lp // CHUNK
    t = n_ex * lp
    rows_of = lambda j: pl.ds(j * CHUNK, CHUNK)

    def body(qk_ref, v_ref, g_ref, lr_ref, w2_ref, gb_ref, ng_ref, y_ref, st_ref, state):
        n = pl.program_id(0)

        @pl.when(n == 0)
        def _():
            state[...] = jnp.zeros_like(state)

        carried = [state[e] for e in range(n_ex)]

        def one_chunk(e, j):
            rows = rows_of(j)
            qk = qk_ref[e, rows, :]
            first = jnp.logical_and(n == 0, j == 0)
            c = yield from _gla_chunk(qk[:, :GLA_K], qk[:, GLA_K:], v_ref[e, rows, :], lr_ref[e, rows, :],
                                      w2_ref[...], gb_ref[...], first)
            kv = _fold(_dot_tn(c["v_b"], c["k_dec"]), DV)
            g = _rows_by_head(g_ref[e, rows, :])
            gate = ng_ref[...] * (g * _sigmoid(g))
            yield
            for _ in range(j):
                yield
            st = carried[e]
            st_ref[e, pl.ds(j * DV, DV), :] = st
            o = c["o_intra"] + _dot_nt(c["qx"], st.astype(BF16))
            rstd = lax.rsqrt(jnp.mean(o * o, axis=-1, keepdims=True) + RMS_EPS)
            y_ref[e, rows, :] = _lanes_by_head(o * rstd * gate).astype(BF16)
            carried[e] = c["decay"] * st + kv

        _in_lockstep(one_chunk(e, j) for j in range(GLA_GROUP) for e in range(n_ex))
        for e in range(n_ex):
            state[e] = carried[e]

    u3 = u.reshape(n_ex, lp, D_IN_PAD)
    blk = lambda w, col: pl.BlockSpec((n_ex, GLA_GROUP * CHUNK, w), lambda n: (0, n, col))
    (y, states), extra = _call(
        body, name="gla_fwd", grid=(nc // GLA_GROUP,),
        in_specs=[blk(2 * GLA_K, 2), blk(GLA_V, 3), blk(GLA_V, 4), blk(128, 20),
                  _const_spec((128, GLA_K)), _const_spec((1, GLA_K)), _const_spec((1, DV))],
        out_specs=[blk(GLA_V, 0), pl.BlockSpec((n_ex, GLA_GROUP * DV, GLA_K), lambda n: (0, n, 0))],
        out_shape=[jax.ShapeDtypeStruct((n_ex, lp, GLA_V), BF16),
                   jax.ShapeDtypeStruct((n_ex, nc * DV, GLA_K), F32)],
        scratch_shapes=[pltpu.VMEM((n_ex, DV, GLA_K), F32)],
        plan=plan,
    )(u3, u3, u3, u3, w2, gb, ng)
    return (y.reshape(t, GLA_V), states), extra


def _gla_bwd(dycat, u, states, w2, gb, ng, n_ex, lp, plan=None):
    nc = lp // CHUNK
    t = n_ex * lp

    def body(dy_ref, qk_ref, v_ref, g_ref, lr_ref, st_ref, w2_ref, gb_ref, ng_ref,
             du_ref, dw2_ref, dgb_ref, dng_ref, dstate):
        n = pl.program_id(0)
        group = nc // GLA_GROUP - 1 - n

        @pl.when(n == 0)
        def _():
            dw2_ref[...] = jnp.zeros_like(dw2_ref)
            dgb_ref[...] = jnp.zeros_like(dgb_ref)
            dng_ref[...] = jnp.zeros_like(dng_ref)
            dstate[...] = jnp.zeros_like(dstate)

        carried = [dstate[e] for e in range(n_ex)]

        def one_chunk(e, order):
            j = GLA_GROUP - 1 - order
            rows = pl.ds(j * CHUNK, CHUNK)
            qk = qk_ref[e, rows, :]
            lr = lr_ref[e, rows, :]
            st = st_ref[e, pl.ds(j * DV, DV), :]
            first = jnp.logical_and(group == 0, j == 0)
            c = yield from _gla_chunk(qk[:, :GLA_K], qk[:, GLA_K:], v_ref[e, rows, :], lr, w2_ref[...], gb_ref[...],
                                      first)
            qx, k_in, k_dec, v_b, s = c["qx"], c["k_in"], c["k_dec"], c["v_b"], c["s"]
            st_b = st.astype(BF16)
            o = c["o_intra"] + _dot_nt(qx, st_b)
            ngv = ng_ref[...]
            yield
            rstd = lax.rsqrt(jnp.mean(o * o, axis=-1, keepdims=True) + RMS_EPS)
            nrm = o * rstd
            g = _rows_by_head(g_ref[e, rows, :])
            dy = _rows_by_head(dy_ref[e, rows, :])
            sg = _sigmoid(g)
            dg = dy * nrm * ngv * (sg * (1.0 + g * (1.0 - sg)))
            dt = dy * (g * sg)
            dng_ref[...] += jnp.sum(dt * nrm, axis=0, keepdims=True)
            dn = dt * ngv
            do = rstd * (dn - nrm * jnp.mean(dn * nrm, axis=-1, keepdims=True))
            do_b = do.astype(BF16)
            dox = _expand_lanes(do).astype(BF16)
            yield
            da = jnp.where(_stacked_causal(), _dot_nt(dox, v_b), 0.0).astype(BF16)
            dv_intra = _dot_tn(s, dox)
            dst_own = _dot_tn(do_b, qx)
            yield
            dq_in = _fold(_dot(da, k_in) + _dot(do_b, st_b), CHUNK)
            dk_in = _dot_tn(da, qx)
            dq = dq_in * (DK ** -0.5) * c["e_pos"]
            yield
            for _ in range(order):
                yield
            dst = carried[e]
            dstx = _expand_state(dst).astype(BF16)
            dv = dv_intra + _dot_nt(k_dec, dstx)
            dk_dec = _dot(v_b, dstx)
            carried[e] = dst_own + c["decay"] * dst
            yield
            dbl = (jnp.sum(dk_dec * c["kd_f"], axis=0, keepdims=True)
                   + c["decay"] * jnp.sum(dst * st, axis=0, keepdims=True))
            dk = dk_in * c["e_neg"] + dk_dec * c["e_dec"]
            db = dq_in * c["q_f"] - dk_in * c["k_f"] - dk_dec * c["kd_f"]
            row = lax.broadcasted_iota(jnp.int32, (CHUNK, 1), 0)
            da_log = _running_sum(db + jnp.where(row == CHUNK - 1, dbl, 0.0), False)
            yield
            dz = jnp.where(c["live"], da_log * (1.0 - _sigmoid(c["z"])) * (1.0 / GATE_TAU), 0.0)
            dz_b = dz.astype(BF16)
            out = du_ref.at[e, rows, :]
            out[:, 0:GLA_K] = dq.astype(BF16)
            out[:, GLA_K:2 * GLA_K] = dk.astype(BF16)
            out[:, 2 * GLA_K:2 * GLA_K + GLA_V] = dv.astype(BF16)
            out[:, 2 * GLA_K + GLA_V:2 * GLA_K + 2 * GLA_V] = _lanes_by_head(dg).astype(BF16)
            out[:, 2 * GLA_K + 2 * GLA_V:] = _dot_nt(dz_b, w2_ref[...]).astype(BF16)
            dw2_ref[...] += _dot_tn(lr.astype(BF16), dz_b)
            dgb_ref[...] += jnp.sum(dz, axis=0, keepdims=True)

        _in_lockstep(one_chunk(e, order) for order in range(GLA_GROUP) for e in range(n_ex))
        for e in range(n_ex):
            dstate[e] = carried[e]

    u3 = u.reshape(n_ex, lp, D_IN_PAD)
    rev = lambda w, col: pl.BlockSpec((n_ex, GLA_GROUP * CHUNK, w), lambda n: (0, nc // GLA_GROUP - 1 - n, col))
    (du, d_w2, d_gb, d_ng), extra = _call(
        body, name="gla_bwd", grid=(nc // GLA_GROUP,),
        in_specs=[rev(GLA_V, 1), rev(2 * GLA_K, 2), rev(GLA_V, 3), rev(GLA_V, 4), rev(128, 20),
                  pl.BlockSpec((n_ex, GLA_GROUP * DV, GLA_K), lambda n: (0, nc // GLA_GROUP - 1 - n, 0)),
                  _const_spec((128, GLA_K)), _const_spec((1, GLA_K)), _const_spec((1, DV))],
        out_specs=[rev(D_GLA_IN, 0), _acc_spec((128, GLA_K)), _acc_spec((1, GLA_K)), _acc_spec((1, DV))],
        out_shape=[jax.ShapeDtypeStruct((n_ex, lp, D_GLA_IN), BF16), jax.ShapeDtypeStruct((128, GLA_K), F32),
                   jax.ShapeDtypeStruct((1, GLA_K), F32), jax.ShapeDtypeStruct((1, DV), F32)],
        scratch_shapes=[pltpu.VMEM((n_ex, DV, GLA_K), F32)],
        plan=plan,
    )(dycat.reshape(n_ex, lp, D), u3, u3, u3, u3, states, w2, gb, ng)
    return (du.reshape(t, D_GLA_IN), d_w2, d_gb, d_ng), extra


def _in_proj_bwd(du_conv, du_gla, w_in_t_conv, w_in_t_gla, h0, dh1, g_mix, plan=None):
    t = h0.shape[0]
    r = _row_tile(t, 384)

    def body(dc_ref, dg_ref, wc_ref, wg_ref, h_ref, dh1_ref, g_ref, dh0_ref, dgm_ref):
        @pl.when(pl.program_id(0) == 0)
        def _():
            dgm_ref[...] = jnp.zeros_like(dgm_ref)

        dhn = _dot(dc_ref[...], wc_ref[...]) + _dot(dg_ref[...], wg_ref[...])
        h = h_ref[...]
        rstd = lax.rsqrt(jnp.mean(h * h, axis=-1, keepdims=True) + RMS_EPS)
        nrm = h * rstd
        dgm_ref[...] += jnp.sum(dhn * nrm, axis=0, keepdims=True)
        dn = dhn * g_ref[...]
        dh0_ref[...] = dh1_ref[...] + rstd * (dn - nrm * jnp.mean(dn * nrm, axis=-1, keepdims=True))

    rows = lambda w: pl.BlockSpec((r, w), lambda i: (i, 0))
    return _call(
        body, name="in_proj_bwd", grid=(t // r,),
        in_specs=[rows(2 * C_CONV), rows(D_GLA_IN), _const_spec((2 * C_CONV, D)), _const_spec((D_GLA_IN, D)),
                  rows(D), rows(D), _const_spec((1, D))],
        out_specs=[rows(D), _acc_spec((1, D))],
        out_shape=[jax.ShapeDtypeStruct((t, D), F32), jax.ShapeDtypeStruct((1, D), F32)],
        plan=plan,
    )(du_conv, du_gla, w_in_t_conv, w_in_t_gla, h0, dh1, g_mix)


def _wgrad(x, dy, name, plan=None):
    t, m = x.shape
    n = dy.shape[1]
    tk = t // 3 if t % (3 * 128) == 0 else _row_tile(t, 384)
    tm = m if m <= D_GLA_IN else m // 2

    def body(x_ref, dy_ref, o_ref):
        @pl.when(pl.program_id(1) == 0)
        def _():
            o_ref[...] = jnp.zeros_like(o_ref)

        o_ref[...] += _dot_tn(x_ref[...].astype(BF16), dy_ref[...].astype(BF16))

    (out,), extra = _call(
        body, name=name, grid=(m // tm, t // tk),
        in_specs=[pl.BlockSpec((tk, tm), lambda i, k: (k, i)), pl.BlockSpec((tk, n), lambda i, k: (k, 0))],
        out_specs=[pl.BlockSpec((tm, n), lambda i, k: (i, 0))],
        out_shape=[jax.ShapeDtypeStruct((m, n), F32)],
        plan=plan,
    )(x, dy)
    return out, extra


def _wgrad_pair(xa, xb, dy, name):
    t, m = xa.shape
    n = dy.shape[1]
    tk = t // 3 if t % (3 * 128) == 0 else _row_tile(t, 384)

    def body(xa_ref, xb_ref, dy_ref, o_ref):
        @pl.when(pl.program_id(1) == 0)
        def _():
            o_ref[...] = jnp.zeros_like(o_ref)

        x = jnp.where(pl.program_id(0) == 0, xa_ref[...], xb_ref[...])
        o_ref[...] += _dot_tn(x.astype(BF16), dy_ref[...].astype(BF16))

    rows = lambda w: pl.BlockSpec((tk, w), lambda i, k: (k, 0))
    return pl.pallas_call(
        body, name=name, grid=(2, t // tk), in_specs=[rows(m), rows(m), rows(n)],
        out_specs=pl.BlockSpec((m, n), lambda i, k: (i, 0)),
        out_shape=jax.ShapeDtypeStruct((2 * m, n), F32), compiler_params=_params(2),
    )(xa, xb, dy)


def _adam_update(g, w, m, v):
    m2 = ADAM_B1 * m + (1.0 - ADAM_B1) * g
    v2 = ADAM_B2 * v + (1.0 - ADAM_B2) * (g * g)
    m_hat = m2 / (1.0 - ADAM_B1 ** ADAM_STEP)
    v_hat = v2 / (1.0 - ADAM_B2 ** ADAM_STEP)
    delta = -ADAM_LR * (m_hat / (jnp.sqrt(v_hat) + ADAM_EPS) + ADAM_WD * w)
    return delta, m2, v2


ADAMW_STEPS = 4


def _adamw(g, w, m, v, name):
    rows, cols = g.shape
    steps = ADAMW_STEPS if rows % (ADAMW_STEPS * SUBLANES) == 0 else 1

    def body(g_ref, w_ref, m_ref, v_ref, d_ref, m2_ref, v2_ref):
        d_ref[...], m2_ref[...], v2_ref[...] = _adam_update(g_ref[...], w_ref[...], m_ref[...], v_ref[...])

    spec = pl.BlockSpec((rows // steps, cols), lambda i: (i, 0))
    return pl.pallas_call(
        body, name=name, grid=(steps,), in_specs=[spec] * 4, out_specs=[spec] * 3,
        out_shape=[jax.ShapeDtypeStruct(g.shape, F32)] * 3, compiler_params=_params(1),
    )(g, w, m, v)


def _adamw_halves(items, c, name):
    n = len(items)
    h = items[0][0].shape[-1]
    splits = lambda a: a.shape[0] % (ADAMW_STEPS * (SUBLANES if a.ndim == 2 else 1)) == 0
    steps = ADAMW_STEPS if all(splits(it[0]) for it in items) else 1

    def body(c_ref, *refs):
        ins, outs = refs[:5 * n], refs[5 * n:]
        own = pl.program_id(1) == c_ref[0]
        for i in range(n):
            a_ref, b_ref, w_ref, m_ref, v_ref = ins[5 * i:5 * i + 5]
            go_ref, d_ref, m2_ref, v2_ref = outs[4 * i:4 * i + 4]
            g = jnp.where(own, a_ref[...], b_ref[...])
            go_ref[...] = g
            d_ref[...], m2_ref[...], v2_ref[...] = _adam_update(g, w_ref[...], m_ref[...], v_ref[...])

    in_specs, out_specs, out_shape, args = [pl.BlockSpec(memory_space=pltpu.SMEM)], [], [], []
    for mine, theirs, w, m, v in items:
        tr = mine.shape[0] // steps
        mid = (0,) * (mine.ndim - 2)
        half = pl.BlockSpec((tr,) + mine.shape[1:-1] + (h,), lambda i, j, mid=mid: (i, *mid, 0))
        full = pl.BlockSpec((tr,) + mine.shape[1:-1] + (h,), lambda i, j, mid=mid: (i, *mid, j))
        in_specs += [half, half, full, full, full]
        out_specs += [full] * 4
        out_shape += [jax.ShapeDtypeStruct(w.shape, F32)] * 4
        args += [mine, theirs, w, m, v]
    res = pl.pallas_call(
        body, name=name, grid=(steps, 2), in_specs=in_specs, out_specs=out_specs, out_shape=out_shape,
        compiler_params=_params(2),
    )(jnp.reshape(c, (1,)).astype(jnp.int32), *args)
    return [res[4 * i:4 * i + 4] for i in range(n)]


def _rs_add_halves(pairs, c, name):
    blocks = pairs[0][0].shape[0]
    n = len(pairs)

    def body(c_ref, *refs):
        for i in range(n):
            refs[2 * n + i][...] = (refs[2 * i][...] + refs[2 * i + 1][...]).astype(BF16)

    in_specs, out_specs, out_shape = [], [], []
    for g, _ in pairs:
        _, rows, w = g.shape
        in_specs += [pl.BlockSpec((1, rows, w // 2), lambda j, s: (j, 0, s[0])),
                     pl.BlockSpec((1, rows, w // 2), lambda j, s: (j, 0, 0))]
        out_specs += [pl.BlockSpec((1, rows, w // 2), lambda j, s: (j, 0, 0))]
        out_shape += [jax.ShapeDtypeStruct((blocks, rows, w // 2), BF16)]
    return pl.pallas_call(
        body, name=name,
        grid_spec=pltpu.PrefetchScalarGridSpec(num_scalar_prefetch=1, grid=(blocks,), in_specs=in_specs,
                                               out_specs=out_specs),
        out_shape=out_shape, compiler_params=_params(1),
    )(jnp.reshape(c, (1,)).astype(jnp.int32), *[a for pair in pairs for a in pair])


def _rs_sum(pairs, mine, name):
    n = len(pairs)
    steps = 2 if all(own.shape[1] % (2 * 16) == 0 for own, _ in pairs) else 1

    def body(mine_ref, *refs):
        for i in range(n):
            p = refs[2 * i + 1][...].astype(F32)
            refs[2 * n + i][...] = ((refs[2 * i][0].astype(F32) + p[0]) + p[1]) + p[2]

    in_specs, out_specs, out_shape = [], [], []
    for own, _ in pairs:
        _, rows, h = own.shape
        tr = rows // steps
        in_specs += [pl.BlockSpec((1, tr, h), lambda i, s: (s[0], i, 0)),
                     pl.BlockSpec((3, tr, h), lambda i, s: (0, i, 0))]
        out_specs += [pl.BlockSpec((tr, h), lambda i, s: (i, 0))]
        out_shape += [jax.ShapeDtypeStruct((rows, h), F32)]
    return pl.pallas_call(
        body, name=name,
        grid_spec=pltpu.PrefetchScalarGridSpec(num_scalar_prefetch=1, grid=(steps,), in_specs=in_specs,
                                               out_specs=out_specs),
        out_shape=out_shape, compiler_params=_params(1),
    )(jnp.reshape(mine, (1,)).astype(jnp.int32), *[a for pair in pairs for a in pair])


def _sum_slots_adamw(slots, late_slots, vectors):
    late_rows = late_slots.shape[1]
    n = len(SMALL_PARTS)

    def body(s_ref, l_ref, *refs):
        ins, g_ref, outs = refs[:3 * n], refs[3 * n], refs[3 * n + 1:]
        g, late = s_ref[0], l_ref[0]
        for d in range(1, 8):
            g = g + s_ref[d]
            late = late + l_ref[d]
        g = jnp.concatenate([g[:late_rows] + late, g[late_rows:]], axis=0)
        g_ref[...] = g
        for i, (_, row, col, size) in enumerate(SMALL_PARTS):
            w_ref, m_ref, v_ref = ins[3 * i:3 * i + 3]
            go_ref, d_ref, m2_ref, v2_ref = outs[4 * i:4 * i + 4]
            piece = g[row:row + 1, col:col + size]
            go_ref[...] = piece
            d_ref[...], m2_ref[...], v2_ref[...] = _adam_update(piece, w_ref[...], m_ref[...], v_ref[...])

    vm = pl.BlockSpec(memory_space=pltpu.VMEM)
    out_shape = [jax.ShapeDtypeStruct(slots.shape[1:], F32)]
    for _, _, _, size in SMALL_PARTS:
        out_shape += [jax.ShapeDtypeStruct((1, size), F32)] * 4
    res = pl.pallas_call(body, name="small_sum_adamw", in_specs=[vm] * (2 + 3 * n), out_specs=[vm] * len(out_shape),
                         out_shape=out_shape)(slots, late_slots, *[a for wmv in vectors for a in wmv])
    return res[0], [res[1 + 4 * i:5 + 4 * i] for i in range(n)]


def _mesh_pos():
    return lax.axis_index("x"), lax.axis_index("y"), lax.axis_index("c")


def _other_chips(x, y):
    return [(1 - x, y), (x, 1 - y), (1 - x, 1 - y)]


def _half(ref, c, axis):
    n = ref.shape[axis] // 2
    return ref.at[(slice(None),) * axis + (pl.ds(c * n, n),)]


def _remote(src, dst, send_sem, recv_sem, device):
    return pltpu.make_async_remote_copy(src_ref=src, dst_ref=dst, send_sem=send_sem, recv_sem=recv_sem,
                                        device_id=device, device_id_type=MESH)


def _gather_plan(split, whole=(), axes=None):
    split, whole = list(split), list(whole)
    ns, n = len(split), len(split) + len(whole)

    def make(ins, outs, sems):
        ici_send, ici_recv, d2d_send, d2d_recv, own_send, own_recv = sems
        x, y, c = _mesh_pos()
        mine = 2 * x + y
        chips = _other_chips(x, y)
        blocks = [2 * px + py for px, py in chips]

        def own(a):
            return _remote(ins[a], outs[a].at[mine], own_send.at[a], own_recv.at[a], (x, y, 1 - c))

        def ici(a, k, block):
            px, py = chips[k]
            src, dst = ins[a], outs[a].at[block]
            if a < ns:
                src, dst = _half(src, c, axes[a]), _half(dst, c, axes[a])
            return _remote(src, dst, ici_send.at[3 * a + k], ici_recv.at[3 * a + k], (px, py, c))

        def d2d(a, k, half):
            part = _half(outs[a].at[blocks[k]], half, axes[a])
            return _remote(part, part, d2d_send.at[3 * a + k], d2d_recv.at[3 * a + k], (x, y, 1 - c))

        def start():
            for a in range(n):
                for k in range(3):
                    ici(a, k, mine).start()
                own(a).start()

        def relay():
            for a in range(n):
                for k in range(3):
                    ici(a, k, blocks[k]).wait_recv()
                    if a < ns:
                        d2d(a, k, c).start()

        def finish():
            for a in range(ns):
                for k in range(3):
                    d2d(a, k, 1 - c).wait_recv()
            for a in range(n):
                for k in range(3):
                    ici(a, k, mine).wait_send()
                    if a < ns:
                        d2d(a, k, c).wait_send()
                own(a).wait()

        return start, relay, finish

    arrays = split + whole
    axes = [0] * ns if axes is None else list(axes)
    return _Plan(arrays, [jax.ShapeDtypeStruct((N_CHIPS,) + s.shape, s.dtype) for s in arrays],
                 [pltpu.SemaphoreType.DMA((3 * n,)), pltpu.SemaphoreType.DMA((3 * n,)),
                  pltpu.SemaphoreType.DMA((3 * ns,)), pltpu.SemaphoreType.DMA((3 * ns,)),
                  pltpu.SemaphoreType.DMA((n,)), pltpu.SemaphoreType.DMA((n,))], make)


def _to_sibling_plan(gs):
    n = len(gs)

    def make(ins, outs, sems):
        send_sems, recv_sems = sems
        x, y, c = _mesh_pos()

        def copy(a):
            return _remote(_half(ins[a], 1 - c, len(ins[a].shape) - 1), outs[a], send_sems.at[a],
                           recv_sems.at[a], (x, y, 1 - c))

        def start():
            for a in range(n):
                copy(a).start()

        def finish():
            for a in range(n):
                copy(a).wait()

        return start, finish

    return _Plan(list(gs), [jax.ShapeDtypeStruct(g.shape[:-1] + (g.shape[-1] // 2,), g.dtype) for g in gs],
                 [pltpu.SemaphoreType.DMA((n,)), pltpu.SemaphoreType.DMA((n,))], make)


def _chip_exchange_plan(ps):
    n = len(ps)

    def make(ins, outs, sems):
        send_sems, recv_sems = sems
        x, y, c = _mesh_pos()
        chips = _other_chips(x, y)

        def ici(a, k):
            px, py = chips[k]
            return _remote(ins[a].at[2 * px + py], outs[a].at[k], send_sems.at[3 * a + k],
                           recv_sems.at[3 * a + k], (px, py, c))

        def start():
            for a in range(n):
                for k in range(3):
                    ici(a, k).start()

        def finish():
            for a in range(n):
                for k in range(3):
                    ici(a, k).wait()

        return start, finish

    return _Plan(list(ps), [jax.ShapeDtypeStruct((3,) + p.shape[1:], p.dtype) for p in ps],
                 [pltpu.SemaphoreType.DMA((3 * n,)), pltpu.SemaphoreType.DMA((3 * n,))], make)


def _share_plan(halves):
    n = len(halves)

    def make(ins, outs, sems):
        send_sems, recv_sems = sems
        x, y, c = _mesh_pos()

        def d2d(a):
            return _remote(ins[a], outs[a], send_sems.at[a], recv_sems.at[a], (x, y, 1 - c))

        def start():
            for a in range(n):
                d2d(a).start()

        def finish():
            for a in range(n):
                d2d(a).wait()

        return start, finish

    return _Plan(list(halves), [jax.ShapeDtypeStruct(p.shape, p.dtype) for p in halves],
                 [pltpu.SemaphoreType.DMA((n,)), pltpu.SemaphoreType.DMA((n,))], make)


def _all_to_all_plan(part):
    def make(ins, outs, sems):
        send_sems, recv_sems, local_sem = sems
        (p_ref,), (slots,) = ins, outs
        x, y, c = _mesh_pos()
        me = 4 * x + 2 * y + c
        peers = [(px, py, pc) for px in (x, 1 - x) for py in (y, 1 - y) for pc in (c, 1 - c)][1:]

        def remote(k, slot):
            return _remote(p_ref, slots.at[slot], send_sems.at[k], recv_sems.at[k], peers[k])

        def local():
            return pltpu.make_async_copy(p_ref, slots.at[me], local_sem)

        def start():
            for k in range(7):
                remote(k, me).start()
            local().start()

        def finish():
            for k, (px, py, pc) in enumerate(peers):
                remote(k, 4 * px + 2 * py + pc).wait_recv()
            for k in range(7):
                remote(k, me).wait_send()
            local().wait()

        return start, finish

    return _Plan([part], [jax.ShapeDtypeStruct((8,) + part.shape, part.dtype)],
                 [pltpu.SemaphoreType.DMA((7,)), pltpu.SemaphoreType.DMA((7,)), pltpu.SemaphoreType.DMA(())], make)


def _merge_plans(a, b):
    na_in, na_out, na_sems = len(a.arrays), len(a.out_shape), len(a.sems)

    def make(ins, outs, sems):
        phases_a = _phases(a.make(ins[:na_in], outs[:na_out], sems[:na_sems]))
        phases_b = _phases(b.make(ins[na_in:], outs[na_out:], sems[na_sems:]))

        def both(i):
            def run():
                phases_a[i]()
                phases_b[i]()
            return run

        return both(0), both(1), both(2)

    return _Plan(list(a.arrays) + list(b.arrays), list(a.out_shape) + list(b.out_shape),
                 list(a.sems) + list(b.sems), make)


def _exchange(plan, name):
    n_in, n_out = len(plan.arrays), len(plan.out_shape)

    def body(*refs):
        for phase in _phases(plan.make(refs[:n_in], refs[n_in:n_in + n_out], refs[n_in + n_out:])):
            phase()

    return pl.pallas_call(
        body, name=name, in_specs=[HBM_SPEC] * n_in, out_specs=[HBM_SPEC] * n_out, out_shape=list(plan.out_shape),
        scratch_shapes=list(plan.sems), compiler_params=pltpu.CompilerParams(has_side_effects=True),
    )(*plan.arrays)


def _pack_small(parts):
    rows = []
    for r in range(SMALL_ROWS):
        pieces, col = [], 0
        for name, row, start, size in SMALL_PARTS:
            if row == r:
                assert start == col
                pieces.append(parts[name].reshape(1, size).astype(F32))
                col += size
        rows.append(jnp.concatenate(pieces + [jnp.zeros((1, D - col), F32)], axis=1))
    return jnp.concatenate(rows, axis=0)


def _columns(gathered):
    return jnp.concatenate([gathered[j] for j in range(N_CHIPS)], axis=1)


def kernel(x, meta_tokens, norm_mix_g, w_in, conv_w, conv_b, conv_ln_g, conv_ln_b, gla_w_gate2, gla_gate_b, gla_norm_g, w_out, norm_ffn_g, w_ffn_gate, w_ffn_up, w_ffn_down, norm_final_g, loss_target, m_meta_tokens, m_norm_mix_g, m_w_in, m_conv_w, m_conv_b, m_conv_ln_g, m_conv_ln_b, m_gla_w_gate2, m_gla_gate_b, m_gla_norm_g, m_w_out, m_norm_ffn_g, m_w_ffn_gate, m_w_ffn_up, m_w_ffn_down, m_norm_final_g, v_meta_tokens, v_norm_mix_g, v_w_in, v_conv_w, v_conv_b, v_conv_ln_g, v_conv_ln_b, v_gla_w_gate2, v_gla_gate_b, v_gla_norm_g, v_w_out, v_norm_ffn_g, v_w_ffn_gate, v_w_ffn_up, v_w_ffn_down, v_norm_final_g):
    ws = dict(zip(WEIGHT_NAMES, (meta_tokens, norm_mix_g, w_in, conv_w, conv_b, conv_ln_g, conv_ln_b, gla_w_gate2,
                                 gla_gate_b, gla_norm_g, w_out, norm_ffn_g, w_ffn_gate, w_ffn_up, w_ffn_down,
                                 norm_final_g)))
    ms = dict(zip(WEIGHT_NAMES, (m_meta_tokens, m_norm_mix_g, m_w_in, m_conv_w, m_conv_b, m_conv_ln_g, m_conv_ln_b,
                                 m_gla_w_gate2, m_gla_gate_b, m_gla_norm_g, m_w_out, m_norm_ffn_g, m_w_ffn_gate,
                                 m_w_ffn_up, m_w_ffn_down, m_norm_final_g)))
    vs = dict(zip(WEIGHT_NAMES, (v_meta_tokens, v_norm_mix_g, v_w_in, v_conv_w, v_conv_b, v_conv_ln_g, v_conv_ln_b,
                                 v_gla_w_gate2, v_gla_gate_b, v_gla_norm_g, v_w_out, v_norm_ffn_g, v_w_ffn_gate,
                                 v_w_ffn_up, v_w_ffn_down, v_norm_final_g)))
    c = lax.axis_index("c")
    mine = 2 * lax.axis_index("x") + lax.axis_index("y")
    shard = lambda d, name: d[name].reshape(d[name].shape[-2:])
    vec = {name: ws[name].reshape(1, -1) for name, _, _, _ in SMALL_PARTS}
    n_ex, seq, _ = x.shape
    lp = HEAD_ROWS + seq
    t = n_ex * lp

    (tgt, h0, gate_s, up_s, out_s, down_s), (w_in_g, meta_g, conv_w_g, w2_g) = _pad_head_rows(
        [loss_target, x],
        [shard(ws, "w_ffn_gate").T, shard(ws, "w_ffn_up").T, shard(ws, "w_out"), shard(ws, "w_ffn_down")],
        plan=_gather_plan([shard(ws, "w_in").T.astype(BF16)],
                          [shard(ws, "meta_tokens"), shard(ws, "conv_w"), shard(ws, "gla_w_gate2")], axes=[1]))
    w_in_t = jnp.concatenate([w_in_g.reshape(D_IN, D), jnp.zeros((D_IN_PAD - D_IN, D), BF16)], axis=0)
    conv_w_full = jnp.concatenate([_columns(conv_w_g), jnp.zeros((32 - CONV_W, C_CONV), F32)], axis=0)
    w2_full = jnp.concatenate([_columns(w2_g), jnp.zeros((128 - RANK, GLA_K), F32)], axis=0).astype(BF16)
    h0 = _set_meta_rows(h0, _columns(meta_g)).reshape(t, D)
    tgt = tgt.reshape(t, D)
    row_mask = jnp.concatenate([jnp.zeros((n_ex, HEAD_ROWS, 1), F32), jnp.ones((n_ex, seq, 1), F32)],
                               axis=1).reshape(t, 1)

    (u, hn), (gate_g,) = _in_proj(h0, vec["norm_mix_g"], w_in_t.T, plan=_gather_plan([gate_s]))
    (yc, y_conv), (up_g, w_out_g) = _conv_fwd(
        u, conv_w_full, vec["conv_b"], vec["conv_ln_g"], vec["conv_ln_b"], n_ex, lp,
        plan=_gather_plan([up_s, out_s]))
    (y_gla, states), _ = _gla_fwd(u, w2_full, vec["gla_gate_b"], vec["gla_norm_g"], n_ex, lp)
    w_out_full = w_out_g.reshape(D, D)
    w_gate_t, w_up_t = gate_g.reshape(D_FF, D), up_g.reshape(D_FF, D)
    (h1, hn2, gate, up, act), (down_g,) = _mix_out_ffn_up(
        h0, y_conv, y_gla, w_out_full, vec["norm_ffn_g"], w_gate_t, w_up_t,
        plan=_gather_plan([down_s]))
    w_down_full = down_g.reshape(D_FF, D)
    dh2, loss, d_final_g = _ffn_down_loss(act, w_down_full, h1, tgt, vec["norm_final_g"], row_mask)
    dgate, dup, dh1, dycat, d_ffn_g = _ffn_bwd(dh2, gate, up, h1, w_down_full.T, w_gate_t, w_up_t, w_out_full.T,
                                                vec["norm_ffn_g"])

    ffn_block = lambda g: g.reshape(N_CHIPS, D_FF // N_CHIPS, D)
    g_gate = ffn_block(_wgrad(dgate, hn2, "wgrad_gate")[0])
    g_up, (gate_sib,) = _wgrad(dup, hn2, "wgrad_up", _to_sibling_plan([g_gate]))
    g_up = ffn_block(g_up)
    g_down, (up_sib,) = _wgrad(act, dh2, "wgrad_down", _to_sibling_plan([g_up]))
    g_down = ffn_block(g_down)
    g_out = _wgrad_pair(y_conv, y_gla, dh1, "wgrad_out").reshape(N_CHIPS, D // N_CHIPS, D)
    cs_gate, cs_up = _rs_add_halves([(g_gate, gate_sib), (g_up, up_sib)], c, "rs_add_gate_up")
    (du_conv, d_conv_w, d_conv_b, d_ln_g, d_ln_b), (ex_gate, ex_up, down_sib, out_sib) = _conv_bwd(
        dycat, yc, u, conv_w_full, vec["conv_ln_g"], vec["conv_ln_b"], n_ex, lp,
        plan=_merge_plans(_chip_exchange_plan([cs_gate, cs_up]), _to_sibling_plan([g_down, g_out])))
    cs_down, cs_out = _rs_add_halves([(g_down, down_sib), (g_out, out_sib)], c, "rs_add_down_out")
    (du_gla, d_w2, d_gate_b, d_norm_g), (ex_down, ex_out) = _gla_bwd(
        dycat, u, states, w2_full, vec["gla_gate_b"], vec["gla_norm_g"], n_ex, lp,
        plan=_chip_exchange_plan([cs_down, cs_out]))
    halves = _rs_sum([(cs_gate, ex_gate), (cs_up, ex_up), (cs_down, ex_down), (cs_out, ex_out)], mine,
                     "rs_sum_early")

    small = {"norm_mix_g": jnp.zeros((1, D), F32), "norm_ffn_g": d_ffn_g, "norm_final_g": d_final_g,
             "conv_b": d_conv_b, "conv_ln_g": d_ln_g, "conv_ln_b": d_ln_b, "gla_gate_b": d_gate_b,
             "gla_norm_g": d_norm_g}
    part = lax.dynamic_update_slice(_pack_small(small), loss[:, :1], (LOSS_ROW, 0))
    part = jnp.concatenate([part, jnp.zeros((N_META, D), F32), d_conv_w.reshape(16, D), d_w2[:RANK].reshape(4, D),
                            jnp.zeros((4, D), F32)], axis=0)
    g_in_conv = _wgrad(du_conv, hn, "wgrad_in_conv")[0][None]
    g_in_gla, (slots, conv_sib) = _wgrad(du_gla, hn, "wgrad_in_gla",
                                         _merge_plans(_all_to_all_plan(part), _to_sibling_plan([g_in_conv])))
    pieces = [g_in_conv, g_in_gla[None]]
    (gla_sib,) = _exchange(_to_sibling_plan(pieces[1:]), "rs_late_to_sibling")
    sums = _rs_add_halves(list(zip(pieces, (conv_sib, gla_sib))), c, "rs_add_w_in")
    in_chip_sum = jnp.concatenate([sums[0][0], sums[1][0]], axis=0)[:D_IN].reshape(N_CHIPS, D_IN // N_CHIPS, D // 2)
    (dh0, d_mix_g), shared = _in_proj_bwd(
        du_conv, du_gla, w_in_t[:2 * C_CONV], w_in_t[2 * C_CONV:], h0, dh1, vec["norm_mix_g"],
        plan=_merge_plans(_share_plan(halves), _chip_exchange_plan([in_chip_sum])))
    dh0 = dh0.reshape(n_ex, lp, D)
    grad_x = dh0[:, HEAD_ROWS:]
    late_part = jnp.concatenate([d_mix_g, jnp.zeros((SMALL_ROWS - 1, D), F32),
                                 jnp.sum(dh0[:, PAD_ROWS:HEAD_ROWS], axis=0)], axis=0)
    (in_half,) = _rs_sum([(in_chip_sum, shared[4])], mine, "rs_sum_w_in")
    in_shared, late_slots = _exchange(_merge_plans(_share_plan([in_half]), _all_to_all_plan(late_part)),
                                      "late_exchange")

    out = {"grad": {}, "delta": {}, "new_m": {}, "new_v": {}}

    def record(name, res, transposed=False):
        for kind, a in zip(("grad", "delta", "new_m", "new_v"), res):
            out[kind][name] = (a.T if transposed else a).reshape(ws[name].shape)

    def operands(name, transposed):
        lay = (lambda a: a.T) if transposed else (lambda a: a)
        return lay(shard(ws, name)), lay(shard(ms, name)), lay(shard(vs, name))

    early_layout = (("w_ffn_gate", True), ("w_ffn_up", True), ("w_ffn_down", False), ("w_out", False))
    items = [(mine_half, their_half, *operands(name, transposed))
             for (name, transposed), mine_half, their_half in zip(early_layout, halves, shared)]
    for (name, transposed), res in zip(early_layout, _adamw_halves(items, c, "adamw_early")):
        record(name, res, transposed)

    tile_rows = lambda a: a.reshape(a.shape[0], 1, a.shape[1])
    by_output = lambda d: jnp.transpose(d["w_in"], (2, 0, 1))
    res = _adamw_halves([(tile_rows(in_half), tile_rows(in_shared), by_output(ws), by_output(ms), by_output(vs))],
                        c, "adamw_w_in")[0]
    for kind, a in zip(("grad", "delta", "new_m", "new_v"), res):
        out[kind]["w_in"] = jnp.transpose(a, (1, 2, 0))

    flat = lambda d, name: d[name].reshape(1, -1)
    g_s, updated = _sum_slots_adamw(slots, late_slots,
                                    [(flat(ws, name), flat(ms, name), flat(vs, name)) for name, _, _, _ in SMALL_PARTS])
    for (name, _, _, _), res in zip(SMALL_PARTS, updated):
        record(name, res)
    loss = g_s[LOSS_ROW, 0]
    block = lambda a, width: lax.dynamic_slice_in_dim(a, mine * width, width, axis=1)
    small_sharded = {"meta_tokens": block(g_s[8:24], D // N_CHIPS),
                     "conv_w": block(g_s[24:40].reshape(32, C_CONV), C_CONV // N_CHIPS)[:CONV_W],
                     "gla_w_gate2": block(g_s[40:44].reshape(RANK, GLA_K), GLA_K // N_CHIPS)}
    for name, g in small_sharded.items():
        record(name, [g, *_adamw(g, *operands(name, False), "adamw_" + name)])

    return (loss, grad_x, *[out[kind][name] for kind in ("grad", "delta", "new_m", "new_v") for name in WEIGHT_NAMES])
```

```python
import functools
from typing import Any, Callable, NamedTuple, Sequence

import jax
import jax.numpy as jnp
from jax import lax
from jax.experimental import pallas as pl
from jax.experimental.pallas import tpu as pltpu

F32 = jnp.float32
BF16 = jnp.bfloat16
MESH = pl.DeviceIdType.MESH

D = 1024
N_META = 16
C_CONV = 512
CONV_W = 31
GLA_K = 256
GLA_V = 512
N_HEADS = 4
DK = 64
DV = 128
RANK = 16
CHUNK = 64
PAD_ROWS = CHUNK - N_META
HEAD_ROWS = CHUNK
D_IN = 2576
D_IN_PAD = 2688
D_GLA_IN = D_IN_PAD - 2 * C_CONV
D_FF = 2816
RMS_EPS = 1e-6
LN_EPS = 1e-5
GATE_TAU = 16.0
N_CHIPS = 4

ADAM_LR = 0.001
ADAM_B1 = 0.9
ADAM_B2 = 0.999
ADAM_EPS = 1e-08
ADAM_WD = 0.01
ADAM_STEP = 10

V7X_VMEM_BYTES = 64 * 1024 * 1024
VMEM_LIMIT = V7X_VMEM_BYTES - 8 * 1024 * 1024
SUBLANES = 8
ROW_PART = 128
FFN_BWD_TILE = 192

WEIGHT_NAMES = ("meta_tokens", "norm_mix_g", "w_in", "conv_w", "conv_b", "conv_ln_g", "conv_ln_b", "gla_w_gate2",
                "gla_gate_b", "gla_norm_g", "w_out", "norm_ffn_g", "w_ffn_gate", "w_ffn_up", "w_ffn_down",
                "norm_final_g")

SMALL_ROWS = 8
SMALL_PARTS = (("norm_mix_g", 0, 0, D), ("norm_ffn_g", 1, 0, D), ("norm_final_g", 2, 0, D),
               ("conv_b", 3, 0, C_CONV), ("conv_ln_g", 3, C_CONV, C_CONV), ("conv_ln_b", 4, 0, C_CONV),
               ("gla_gate_b", 4, C_CONV, GLA_K), ("gla_norm_g", 4, C_CONV + GLA_K, DV))
LOSS_ROW = 5

HBM_SPEC = pl.BlockSpec(memory_space=pltpu.HBM)


def _dot(a, b):
    return jnp.dot(a, b, preferred_element_type=F32)


def _dot_nt(a, b):
    return lax.dot_general(a, b, (((1,), (1,)), ((), ())), preferred_element_type=F32)


def _dot_tn(a, b):
    return lax.dot_general(a, b, (((0,), (0,)), ((), ())), preferred_element_type=F32)


def _sigmoid(x):
    return 1.0 / (1.0 + jnp.exp(-x))


def _const_spec(shape):
    return pl.BlockSpec(shape, lambda *_: (0,) * len(shape), pipeline_mode=pl.Buffered(1))


def _acc_spec(shape):
    return pl.BlockSpec(shape, lambda *_: (0,) * len(shape))


def _params(n_axes):
    return pltpu.CompilerParams(dimension_semantics=("arbitrary",) * n_axes, vmem_limit_bytes=VMEM_LIMIT)


def _row_tile(t, want):
    for r in (want, 384, 192, 128, 64):
        if r <= want and t % r == 0:
            return r
    raise ValueError(f"no row tile for {t}")


def _row_parts(r):
    if r % ROW_PART:
        return [slice(None)]
    return [pl.ds(i * ROW_PART, ROW_PART) for i in range(r // ROW_PART)]


def _in_lockstep(bodies):
    live = list(bodies)
    while live:
        still = []
        for g in live:
            try:
                next(g)
                still.append(g)
            except StopIteration:
                pass
        live = still


class _Plan(NamedTuple):
    arrays: Sequence[Any]
    out_shape: Sequence[Any]
    sems: Sequence[Any]
    make: Callable


def _phases(made):
    return made if len(made) == 3 else (made[0], lambda: None, made[1])


def _call(body, *, name, grid, in_specs, out_specs, out_shape, scratch_shapes=(), plan=None):
    n_in, n_out, n_scr = len(in_specs), len(out_specs), len(scratch_shapes)
    if plan is None:
        plan = _Plan([], [], [], lambda ins, outs, sems: (lambda: None, lambda: None))
    nx_in, nx_out = len(plan.arrays), len(plan.out_shape)
    n_steps = functools.reduce(lambda a, b: a * b, grid)

    def hosted(*refs):
        ins, xins = refs[:n_in], refs[n_in:n_in + nx_in]
        o0 = n_in + nx_in
        outs, xouts = refs[o0:o0 + n_out], refs[o0 + n_out:o0 + n_out + nx_out]
        s0 = o0 + n_out + nx_out
        scr, sems = refs[s0:s0 + n_scr], refs[s0 + n_scr:]
        step = functools.reduce(lambda acc, a: acc * grid[a] + pl.program_id(a), range(len(grid)), 0)
        start, relay, finish = _phases(plan.make(xins, xouts, sems))
        pl.when(step == 0)(start)
        pl.when(step == n_steps - 1)(relay)
        body(*ins, *outs, *scr)
        pl.when(step == n_steps - 1)(finish)

    call = pl.pallas_call(
        hosted, name=name, grid=grid, in_specs=list(in_specs) + [HBM_SPEC] * nx_in,
        out_specs=list(out_specs) + [HBM_SPEC] * nx_out, out_shape=list(out_shape) + list(plan.out_shape),
        scratch_shapes=list(scratch_shapes) + list(plan.sems),
        compiler_params=pltpu.CompilerParams(dimension_semantics=("arbitrary",) * len(grid),
                                             vmem_limit_bytes=VMEM_LIMIT, has_side_effects=nx_in > 0))

    def run(*args):
        res = call(*args, *plan.arrays)
        return res[:n_out], res[n_out:]

    return run


def _pad_head_rows(arrays, casts, plan=None):
    n_ex, seq, _ = arrays[0].shape
    nc = (HEAD_ROWS + seq) // CHUNK
    n, k = len(arrays), len(casts)

    def body(*refs):
        ins, outs = refs[:n + k], refs[n + k:]
        for a_ref, o_ref in zip(ins[:n], outs[:n]):
            o_ref[...] = jnp.where(pl.program_id(0) > 0, a_ref[...], 0.0)

        @pl.when(pl.program_id(0) == 0)
        def _():
            for a_ref, o_ref in zip(ins[n:], outs[n:]):
                o_ref[...] = a_ref[...].astype(BF16)

    whole = lambda a: pl.BlockSpec(a.shape, lambda i: (0, 0))
    return _call(
        body, name="pad_head_rows", grid=(nc,),
        in_specs=([pl.BlockSpec((n_ex, CHUNK, D), lambda i: (0, jnp.maximum(i - 1, 0), 0))] * n
                  + [_const_spec(a.shape) for a in casts]),
        out_specs=[pl.BlockSpec((n_ex, CHUNK, D), lambda i: (0, i, 0))] * n + [whole(a) for a in casts],
        out_shape=([jax.ShapeDtypeStruct((n_ex, HEAD_ROWS + seq, D), F32)] * n
                   + [jax.ShapeDtypeStruct(a.shape, BF16) for a in casts]),
        plan=plan,
    )(*arrays, *casts)


def _set_meta_rows(h0, meta):
    n_ex = h0.shape[0]

    def body(h_ref, meta_ref, o_ref):
        o_ref[...] = jnp.concatenate(
            [h_ref[:, :PAD_ROWS, :], jnp.broadcast_to(meta_ref[...][None], (n_ex, N_META, D))], axis=1)

    head = pl.BlockSpec((n_ex, HEAD_ROWS, D), lambda i: (0, 0, 0))
    return pl.pallas_call(
        body, name="set_meta_rows", grid=(1,), in_specs=[head, pl.BlockSpec((N_META, D), lambda i: (0, 0))],
        out_specs=head, out_shape=jax.ShapeDtypeStruct(h0.shape, F32), input_output_aliases={0: 0},
        compiler_params=_params(1),
    )(h0, meta)


def _in_proj(h0, g_mix, w_in, plan=None):
    t = h0.shape[0]
    r = _row_tile(t, 384)

    def body(h_ref, g_ref, w_ref, u_ref, hn_ref):
        h = h_ref[...]
        rstd = lax.rsqrt(jnp.mean(h * h, axis=-1, keepdims=True) + RMS_EPS)
        hn = (h * rstd * g_ref[...]).astype(BF16)
        hn_ref[...] = hn
        u_ref[...] = _dot(hn, w_ref[...])

    return _call(
        body, name="in_proj", grid=(t // r,),
        in_specs=[pl.BlockSpec((r, D), lambda i: (i, 0)), _const_spec((1, D)), _const_spec((D, D_IN_PAD))],
        out_specs=[pl.BlockSpec((r, D_IN_PAD), lambda i: (i, 0)), pl.BlockSpec((r, D), lambda i: (i, 0))],
        out_shape=[jax.ShapeDtypeStruct((t, D_IN_PAD), F32), jax.ShapeDtypeStruct((t, D), BF16)],
        plan=plan,
    )(h0, g_mix, w_in)


CONV_TILE = 192
CONV_SUB = 32
CONV_LEAD = CONV_SUB - (CONV_W - 1)


def _shifted_copies(src, dst, r):
    for s in range(1, SUBLANES):
        dst[s - 1] = src[s:s + r + CONV_SUB - SUBLANES, :]


def _shifted_rows(src, shifted, start):
    base, s = SUBLANES * (start // SUBLANES), start % SUBLANES
    if s == 0:
        return src[base:base + CONV_SUB, :]
    return shifted[s - 1, base:base + CONV_SUB, :]


def _conv_fwd(u, conv_w, conv_b, ln_g, ln_b, n_ex, lp, plan=None):
    r = CONV_TILE
    nt = lp // r
    hb = r // CONV_SUB

    def body(cur_ref, prev_ref, w_ref, b_ref, lg_ref, lb_ref, yc_ref, y_ref, glu, glu_sh):
        i = pl.program_id(1)
        cur = cur_ref[...]
        glu[CONV_SUB:CONV_SUB + r, :] = cur[:, :C_CONV] * _sigmoid(cur[:, C_CONV:])
        pv = prev_ref[...]
        halo = pv[:, :C_CONV] * _sigmoid(pv[:, C_CONV:])
        glu[0:CONV_SUB, :] = jnp.where(i > 0, halo, 0.0)
        _shifted_copies(glu, glu_sh, r)
        w = w_ref[...]
        for j in range(r // CONV_SUB):
            r0 = j * CONV_SUB
            acc = jnp.zeros((CONV_SUB, C_CONV), F32) + b_ref[...]
            for k in range(CONV_W):
                acc = acc + w[k:k + 1, :] * _shifted_rows(glu, glu_sh, r0 + CONV_LEAD + k)
            mu = jnp.mean(acc, axis=-1, keepdims=True)
            cen = acc - mu
            var = jnp.mean(cen * cen, axis=-1, keepdims=True)
            out = cen * lax.rsqrt(var + LN_EPS) * lg_ref[...] + lb_ref[...]
            y = out * _sigmoid(out)
            row = i * r + r0 + lax.broadcasted_iota(jnp.int32, (CONV_SUB, 1), 0)
            y = jnp.where(row >= PAD_ROWS, y, 0.0)
            yc_ref[r0:r0 + CONV_SUB, :] = acc
            y_ref[r0:r0 + CONV_SUB, :] = y.astype(BF16)

    t = n_ex * lp
    return _call(
        body, name="conv_fwd", grid=(n_ex, nt),
        in_specs=[pl.BlockSpec((r, 2 * C_CONV), lambda b, i: (b * nt + i, 0)),
                  pl.BlockSpec((CONV_SUB, 2 * C_CONV), lambda b, i: (jnp.maximum((b * nt + i) * hb - 1, 0), 0)),
                  _const_spec((32, C_CONV)), _const_spec((1, C_CONV)), _const_spec((1, C_CONV)), _const_spec((1, C_CONV))],
        out_specs=[pl.BlockSpec((r, C_CONV), lambda b, i: (b * nt + i, 0)),
                   pl.BlockSpec((r, C_CONV), lambda b, i: (b * nt + i, 0))],
        out_shape=[jax.ShapeDtypeStruct((t, C_CONV), F32), jax.ShapeDtypeStruct((t, C_CONV), BF16)],
        scratch_shapes=[pltpu.VMEM((r + CONV_SUB, C_CONV), F32),
                        pltpu.VMEM((SUBLANES - 1, r + CONV_SUB - SUBLANES, C_CONV), F32)],
        plan=plan,
    )(u, u, conv_w, conv_b, ln_g, ln_b)


def _mix_out_ffn_up(h0, y_conv, y_gla, w_out, g_ffn, w_gate_t, w_up_t, plan=None):
    t = h0.shape[0]
    r = _row_tile(t, 384)

    def body(h0_ref, yc_ref, yg_ref, wo_ref, g_ref, wg_ref, wu_ref, h1_ref, hn_ref, gate_ref, up_ref, act_ref):
        h1 = h0_ref[...] + _dot(yc_ref[...], wo_ref[0:C_CONV, :]) + _dot(yg_ref[...], wo_ref[C_CONV:D, :])
        h1_ref[...] = h1
        rstd = lax.rsqrt(jnp.mean(h1 * h1, axis=-1, keepdims=True) + RMS_EPS)
        hn = (h1 * rstd * g_ref[...]).astype(BF16)
        hn_ref[...] = hn
        gate = _dot_nt(hn, wg_ref[...])
        up = _dot_nt(hn, wu_ref[...])
        gate_ref[...] = gate
        up_ref[...] = up
        act_ref[...] = (gate * _sigmoid(gate) * up).astype(BF16)

    rows = lambda w: pl.BlockSpec((r, w), lambda i: (i, 0))
    return _call(
        body, name="mix_out_ffn_up", grid=(t // r,),
        in_specs=[rows(D), rows(C_CONV), rows(GLA_V), _const_spec((D, D)), _const_spec((1, D)),
                  _const_spec((D_FF, D)), _const_spec((D_FF, D))],
        out_specs=[rows(D), rows(D), rows(D_FF), rows(D_FF), rows(D_FF)],
        out_shape=[jax.ShapeDtypeStruct((t, D), F32), jax.ShapeDtypeStruct((t, D), BF16),
                   jax.ShapeDtypeStruct((t, D_FF), F32), jax.ShapeDtypeStruct((t, D_FF), F32),
                   jax.ShapeDtypeStruct((t, D_FF), BF16)],
        plan=plan,
    )(h0, y_conv, y_gla, w_out, g_ffn, w_gate_t, w_up_t)


def _ffn_down_loss(act, w_down, h1, target, g_final, row_mask):
    t = h1.shape[0]
    r = _row_tile(t, 384)

    def body(act_ref, wd_ref, h1_ref, tgt_ref, gf_ref, mask_ref, dh2_ref, loss_ref, dgf_ref):
        @pl.when(pl.program_id(0) == 0)
        def _():
            loss_ref[...] = jnp.zeros_like(loss_ref)
            dgf_ref[...] = jnp.zeros_like(dgf_ref)

        gf = gf_ref[...]

        def part(rows):
            h2 = h1_ref[rows, :] + _dot(act_ref[rows, :], wd_ref[...])
            yield
            rstd = lax.rsqrt(jnp.mean(h2 * h2, axis=-1, keepdims=True) + RMS_EPS)
            nrm = h2 * rstd
            err = (nrm * gf - tgt_ref[rows, :]) * mask_ref[rows, :]
            loss_ref[...] += jnp.sum(err * err) * (0.5 / D)
            dy = err * (1.0 / D)
            dgf_ref[...] += jnp.sum(dy * nrm, axis=0, keepdims=True)
            dn = dy * gf
            dh2_ref[rows, :] = rstd * (dn - nrm * jnp.mean(dn * nrm, axis=-1, keepdims=True))

        _in_lockstep(part(rows) for rows in _row_parts(r))

    rows = lambda w: pl.BlockSpec((r, w), lambda i: (i, 0))
    return pl.pallas_call(
        body, name="ffn_down_loss", grid=(t // r,),
        in_specs=[rows(D_FF), _const_spec((D_FF, D)), rows(D), rows(D), _const_spec((1, D)), rows(1)],
        out_specs=[rows(D), _acc_spec((1, 128)), _acc_spec((1, D))],
        out_shape=[jax.ShapeDtypeStruct((t, D), F32), jax.ShapeDtypeStruct((1, 128), F32),
                   jax.ShapeDtypeStruct((1, D), F32)],
        compiler_params=_params(1),
    )(act, w_down, h1, target, g_final, row_mask)


def _ffn_bwd(dh2, gate, up, h1, w_down, w_gate_t, w_up_t, w_out, g_ffn):
    t = h1.shape[0]
    r = _row_tile(t, FFN_BWD_TILE)

    def body(dh2_ref, gate_ref, up_ref, h1_ref, wd_ref, wg_ref, wu_ref, wo_ref, g_ref,
             dgate_ref, dup_ref, dh1_ref, dycat_ref, dg_ref):
        @pl.when(pl.program_id(0) == 0)
        def _():
            dg_ref[...] = jnp.zeros_like(dg_ref)

        dh2 = dh2_ref[...]
        dact = _dot_nt(dh2.astype(BF16), wd_ref[...])
        gate = gate_ref[...]
        sg = _sigmoid(gate)
        dgate = (dact * up_ref[...] * (sg * (1.0 + gate * (1.0 - sg)))).astype(BF16)
        dup = (dact * (gate * sg)).astype(BF16)
        dgate_ref[...] = dgate
        dup_ref[...] = dup
        dhn = _dot(dgate, wg_ref[...]) + _dot(dup, wu_ref[...])
        h1 = h1_ref[...]
        rstd = lax.rsqrt(jnp.mean(h1 * h1, axis=-1, keepdims=True) + RMS_EPS)
        nrm = h1 * rstd
        dg_ref[...] += jnp.sum(dhn * nrm, axis=0, keepdims=True)
        dn = dhn * g_ref[...]
        dh1 = dh2 + rstd * (dn - nrm * jnp.mean(dn * nrm, axis=-1, keepdims=True))
        dh1_ref[...] = dh1
        dycat_ref[...] = _dot_nt(dh1.astype(BF16), wo_ref[...])

    rows = lambda w: pl.BlockSpec((r, w), lambda i: (i, 0))
    return pl.pallas_call(
        body, name="ffn_bwd", grid=(t // r,),
        in_specs=[rows(D), rows(D_FF), rows(D_FF), rows(D), _const_spec((D_FF, D)), _const_spec((D_FF, D)),
                  _const_spec((D_FF, D)), _const_spec((D, D)), _const_spec((1, D))],
        out_specs=[rows(D_FF), rows(D_FF), rows(D), rows(D), _acc_spec((1, D))],
        out_shape=[jax.ShapeDtypeStruct((t, D_FF), BF16), jax.ShapeDtypeStruct((t, D_FF), BF16),
                   jax.ShapeDtypeStruct((t, D), F32), jax.ShapeDtypeStruct((t, D), F32),
                   jax.ShapeDtypeStruct((1, D), F32)],
        compiler_params=_params(1),
    )(dh2, gate, up, h1, w_down, w_gate_t, w_up_t, w_out, g_ffn)


def _conv_bwd(dycat, yc, u, conv_w, ln_g, ln_b, n_ex, lp, plan=None):
    r = CONV_TILE
    nt = lp // r
    hb = r // CONV_SUB
    nsub = r // CONV_SUB

    def ln_bwd(dy, yc_rows, live, lg, lb):
        mu = jnp.mean(yc_rows, axis=-1, keepdims=True)
        cen = yc_rows - mu
        rs = lax.rsqrt(jnp.mean(cen * cen, axis=-1, keepdims=True) + LN_EPS)
        yn = cen * rs
        out = yn * lg + lb
        so = _sigmoid(out)
        dout = jnp.where(live, dy * (so * (1.0 + out * (1.0 - so))), 0.0)
        dyn = dout * lg
        dyc = rs * (dyn - jnp.mean(dyn, axis=-1, keepdims=True) - yn * jnp.mean(dyn * yn, axis=-1, keepdims=True))
        return dyc, dout, yn

    def body(dy_ref, dyn_ref, yc_ref, ycn_ref, cur_ref, prev_ref, w_ref, lg_ref, lb_ref,
             du_ref, dw_ref, db_ref, dlg_ref, dlb_ref, glu, dycs, dwacc, glu_sh, dycs_sh):
        b = pl.program_id(0)
        i = pl.program_id(1)
        first = jnp.logical_and(b == 0, i == 0)

        @pl.when(first)
        def _():
            dwacc[...] = jnp.zeros_like(dwacc)
            db_ref[...] = jnp.zeros_like(db_ref)
            dlg_ref[...] = jnp.zeros_like(dlg_ref)
            dlb_ref[...] = jnp.zeros_like(dlb_ref)

        lg, lb = lg_ref[...], lb_ref[...]
        cur = cur_ref[...]
        sig = _sigmoid(cur[:, C_CONV:])
        glu[CONV_SUB:CONV_SUB + r, :] = cur[:, :C_CONV] * sig
        pv = prev_ref[...]
        glu[0:CONV_SUB, :] = jnp.where(i > 0, pv[:, :C_CONV] * _sigmoid(pv[:, C_CONV:]), 0.0)

        row = i * r + lax.broadcasted_iota(jnp.int32, (r, 1), 0)
        dyc, dout, yn = ln_bwd(dy_ref[...], yc_ref[...], row >= PAD_ROWS, lg, lb)
        dycs[0:r, :] = dyc
        dycn, _, _ = ln_bwd(dyn_ref[...], ycn_ref[...], i < nt - 1, lg, lb)
        dycs[r:r + CONV_SUB, :] = dycn
        db_ref[...] += jnp.sum(dyc, axis=0, keepdims=True)
        dlg_ref[...] += jnp.sum(dout * yn, axis=0, keepdims=True)
        dlb_ref[...] += jnp.sum(dout, axis=0, keepdims=True)

        _shifted_copies(glu, glu_sh, r)
        _shifted_copies(dycs, dycs_sh, r)
        w = w_ref[...]
        for j in range(nsub):
            r0 = j * CONV_SUB
            dblk = dycs[r0:r0 + CONV_SUB, :]
            dglu = jnp.zeros((CONV_SUB, C_CONV), F32)
            for k in range(CONV_W):
                dglu = dglu + w[k:k + 1, :] * _shifted_rows(dycs, dycs_sh, r0 + (CONV_W - 1) - k)
                prod = dblk * _shifted_rows(glu, glu_sh, r0 + CONV_LEAD + k)
                dwacc[k] += prod.reshape(CONV_SUB // SUBLANES, SUBLANES, C_CONV).sum(axis=0)
            sg = sig[r0:r0 + CONV_SUB, :]
            cv = cur[r0:r0 + CONV_SUB, :C_CONV]
            du_ref[r0:r0 + CONV_SUB, :C_CONV] = (dglu * sg).astype(BF16)
            du_ref[r0:r0 + CONV_SUB, C_CONV:] = (dglu * cv * sg * (1.0 - sg)).astype(BF16)

        @pl.when(jnp.logical_and(b == n_ex - 1, i == nt - 1))
        def _():
            dw_ref[...] = jnp.sum(dwacc[...], axis=1)

    t = n_ex * lp
    cur_rows = lambda w, col: pl.BlockSpec((r, w), lambda b, i: (b * nt + i, col))
    nxt_rows = lambda w, col: pl.BlockSpec(
        (CONV_SUB, w), lambda b, i: (jnp.minimum((b * nt + i + 1) * hb, n_ex * nt * hb - 1), col))
    return _call(
        body, name="conv_bwd", grid=(n_ex, nt),
        in_specs=[cur_rows(C_CONV, 0), nxt_rows(C_CONV, 0), cur_rows(C_CONV, 0), nxt_rows(C_CONV, 0),
                  cur_rows(2 * C_CONV, 0),
                  pl.BlockSpec((CONV_SUB, 2 * C_CONV), lambda b, i: (jnp.maximum((b * nt + i) * hb - 1, 0), 0)),
                  _const_spec((32, C_CONV)), _const_spec((1, C_CONV)), _const_spec((1, C_CONV))],
        out_specs=[cur_rows(2 * C_CONV, 0), _acc_spec((32, C_CONV)), _acc_spec((1, C_CONV)),
                   _acc_spec((1, C_CONV)), _acc_spec((1, C_CONV))],
        out_shape=[jax.ShapeDtypeStruct((t, 2 * C_CONV), BF16), jax.ShapeDtypeStruct((32, C_CONV), F32),
                   jax.ShapeDtypeStruct((1, C_CONV), F32), jax.ShapeDtypeStruct((1, C_CONV), F32),
                   jax.ShapeDtypeStruct((1, C_CONV), F32)],
        scratch_shapes=[pltpu.VMEM((r + CONV_SUB, C_CONV), F32), pltpu.VMEM((r + CONV_SUB, C_CONV), F32),
                        pltpu.VMEM((32, SUBLANES, C_CONV), F32),
                        pltpu.VMEM((SUBLANES - 1, r + CONV_SUB - SUBLANES, C_CONV), F32),
                        pltpu.VMEM((SUBLANES - 1, r + CONV_SUB - SUBLANES, C_CONV), F32)],
        plan=plan,
    )(dycat, dycat, yc, yc, u, u, conv_w, ln_g, ln_b)


HEAD_ROWS_ALL = N_HEADS * CHUNK


def _gla_gates(lr, w2, gb, first_chunk):
    z = _dot(lr.astype(BF16), w2) + gb
    a = (jnp.minimum(z, 0.0) - jnp.log(1.0 + jnp.exp(-jnp.abs(z)))) * (1.0 / GATE_TAU)
    row = lax.broadcasted_iota(jnp.int32, (CHUNK, 1), 0)
    live = jnp.logical_or(jnp.logical_not(first_chunk), row >= PAD_ROWS)
    return z, jnp.where(live, a, 0.0), live


def _tri(lower):
    i = lax.broadcasted_iota(jnp.int32, (CHUNK, CHUNK), 0)
    j = lax.broadcasted_iota(jnp.int32, (CHUNK, CHUNK), 1)
    return (i >= j) if lower else (i <= j)


def _head_of(shape, axis, per_head):
    return lax.broadcasted_iota(jnp.int32, shape, axis) // per_head


def _expand(x, lanes_per_head):
    rows, lanes = HEAD_ROWS_ALL, x.shape[1]
    keep = _head_of((rows, lanes), 0, CHUNK) == _head_of((rows, lanes), 1, lanes_per_head)
    return jnp.where(keep, jnp.tile(x, (N_HEADS, 1)), 0.0)


def _expand_lanes(x):
    rows, w = x.shape
    keep = _head_of((rows, N_HEADS * w), 0, CHUNK) == _head_of((rows, N_HEADS * w), 1, w)
    return jnp.where(keep, jnp.tile(x, (1, N_HEADS)), 0.0)


def _expand_state(st):
    rows, lanes = N_HEADS * DV, st.shape[1]
    keep = _head_of((rows, lanes), 0, DV) == _head_of((rows, lanes), 1, DK)
    return jnp.where(keep, jnp.tile(st, (N_HEADS, 1)), 0.0)


def _fold(t, rows_per_head):
    lane_head = _head_of((rows_per_head, t.shape[1]), 1, DK)
    out = jnp.where(lane_head == 0, t[0:rows_per_head], 0.0)
    for h in range(1, N_HEADS):
        out = out + jnp.where(lane_head == h, t[h * rows_per_head:(h + 1) * rows_per_head], 0.0)
    return out


def _rows_by_head(x):
    return jnp.concatenate([x[:, h * DV:(h + 1) * DV] for h in range(N_HEADS)], axis=0)


def _lanes_by_head(x):
    return jnp.concatenate([x[h * CHUNK:(h + 1) * CHUNK] for h in range(N_HEADS)], axis=1)


def _running_sum(a, lower):
    hi = a.astype(BF16)
    rest = a - hi.astype(F32)
    mid = rest.astype(BF16)
    lo = (rest - mid.astype(F32)).astype(BF16)
    w = a.shape[1]
    parts = _dot(_tri(lower).astype(F32).astype(BF16), jnp.concatenate([hi, mid, lo], axis=1))
    return parts[:, :w] + parts[:, w:2 * w] + parts[:, 2 * w:]


def _stacked_causal():
    i = lax.broadcasted_iota(jnp.int32, (HEAD_ROWS_ALL, CHUNK), 0) % CHUNK
    j = lax.broadcasted_iota(jnp.int32, (HEAD_ROWS_ALL, CHUNK), 1)
    return i >= j


GLA_GROUP = 3


def _gla_chunk(q, k, v, lr, w2, gb, first_chunk):
    z, a, live = _gla_gates(lr, w2, gb, first_chunk)
    yield
    b = _running_sum(a, True)
    yield
    bl = b[CHUNK - 1:CHUNK, :]
    e_pos, e_neg, e_dec = jnp.exp(b), jnp.exp(-b), jnp.exp(bl - b)
    q_f, k_f, kd_f = q * (DK ** -0.5) * e_pos, k * e_neg, k * e_dec
    qx = _expand(q_f, DK).astype(BF16)
    k_in, k_dec, v_b = k_f.astype(BF16), kd_f.astype(BF16), v.astype(BF16)
    s = jnp.where(_stacked_causal(), _dot_nt(qx, k_in), 0.0).astype(BF16)
    yield
    p = _dot(s, v_b)
    yield
    o_intra = jnp.concatenate([p[h * CHUNK:(h + 1) * CHUNK, h * DV:(h + 1) * DV] for h in range(N_HEADS)], axis=0)
    return dict(z=z, live=live, bl=bl, e_pos=e_pos, e_neg=e_neg, e_dec=e_dec, q_f=q_f, k_f=k_f, kd_f=kd_f,
                qx=qx, k_in=k_in, k_dec=k_dec, v_b=v_b, s=s, o_intra=o_intra, decay=jnp.exp(bl))


def _gla_fwd(u, w2, gb, ng, n_ex, lp, plan=None):
    nc = lp // CHUNK
    t = n_ex * lp
    rows_of = lambda j: pl.ds(j * CHUNK, CHUNK)

    def body(qk_ref, v_ref, g_ref, lr_ref, w2_ref, gb_ref, ng_ref, y_ref, st_ref, state):
        n = pl.program_id(0)

        @pl.when(n == 0)
        def _():
            state[...] = jnp.zeros_like(state)

        carried = [state[e] for e in range(n_ex)]

        def one_chunk(e, j):
            rows = rows_of(j)
            qk = qk_ref[e, rows, :]
            first = jnp.logical_and(n == 0, j == 0)
            c = yield from _gla_chunk(qk[:, :GLA_K], qk[:, GLA_K:], v_ref[e, rows, :], lr_ref[e, rows, :],
                                      w2_ref[...], gb_ref[...], first)
            kv = _fold(_dot_tn(c["v_b"], c["k_dec"]), DV)
            g = _rows_by_head(g_ref[e, rows, :])
            gate = ng_ref[...] * (g * _sigmoid(g))
            yield
            for _ in range(j):
                yield
            st = carried[e]
            st_ref[e, pl.ds(j * DV, DV), :] = st
            o = c["o_intra"] + _dot_nt(c["qx"], st.astype(BF16))
            rstd = lax.rsqrt(jnp.mean(o * o, axis=-1, keepdims=True) + RMS_EPS)
            y_ref[e, rows, :] = _lanes_by_head(o * rstd * gate).astype(BF16)
            carried[e] = c["decay"] * st + kv

        _in_lockstep(one_chunk(e, j) for j in range(GLA_GROUP) for e in range(n_ex))
        for e in range(n_ex):
            state[e] = carried[e]

    u3 = u.reshape(n_ex, lp, D_IN_PAD)
    blk = lambda w, col: pl.BlockSpec((n_ex, GLA_GROUP * CHUNK, w), lambda n: (0, n, col))
    (y, states), extra = _call(
        body, name="gla_fwd", grid=(nc // GLA_GROUP,),
        in_specs=[blk(2 * GLA_K, 2), blk(GLA_V, 3), blk(GLA_V, 4), blk(128, 20),
                  _const_spec((128, GLA_K)), _const_spec((1, GLA_K)), _const_spec((1, DV))],
        out_specs=[blk(GLA_V, 0), pl.BlockSpec((n_ex, GLA_GROUP * DV, GLA_K), lambda n: (0, n, 0))],
        out_shape=[jax.ShapeDtypeStruct((n_ex, lp, GLA_V), BF16),
                   jax.ShapeDtypeStruct((n_ex, nc * DV, GLA_K), F32)],
        scratch_shapes=[pltpu.VMEM((n_ex, DV, GLA_K), F32)],
        plan=plan,
    )(u3, u3, u3, u3, w2, gb, ng)
    return (y.reshape(t, GLA_V), states), extra


def _gla_bwd(dycat, u, states, w2, gb, ng, n_ex, lp, plan=None):
    nc = lp // CHUNK
    t = n_ex * lp

    def body(dy_ref, qk_ref, v_ref, g_ref, lr_ref, st_ref, w2_ref, gb_ref, ng_ref,
             du_ref, dw2_ref, dgb_ref, dng_ref, dstate):
        n = pl.program_id(0)
        group = nc // GLA_GROUP - 1 - n

        @pl.when(n == 0)
        def _():
            dw2_ref[...] = jnp.zeros_like(dw2_ref)
            dgb_ref[...] = jnp.zeros_like(dgb_ref)
            dng_ref[...] = jnp.zeros_like(dng_ref)
            dstate[...] = jnp.zeros_like(dstate)

        carried = [dstate[e] for e in range(n_ex)]

        def one_chunk(e, order):
            j = GLA_GROUP - 1 - order
            rows = pl.ds(j * CHUNK, CHUNK)
            qk = qk_ref[e, rows, :]
            lr = lr_ref[e, rows, :]
            st = st_ref[e, pl.ds(j * DV, DV), :]
            first = jnp.logical_and(group == 0, j == 0)
            c = yield from _gla_chunk(qk[:, :GLA_K], qk[:, GLA_K:], v_ref[e, rows, :], lr, w2_ref[...], gb_ref[...],
                                      first)
            qx, k_in, k_dec, v_b, s = c["qx"], c["k_in"], c["k_dec"], c["v_b"], c["s"]
            st_b = st.astype(BF16)
            o = c["o_intra"] + _dot_nt(qx, st_b)
            ngv = ng_ref[...]
            yield
            rstd = lax.rsqrt(jnp.mean(o * o, axis=-1, keepdims=True) + RMS_EPS)
            nrm = o * rstd
            g = _rows_by_head(g_ref[e, rows, :])
            dy = _rows_by_head(dy_ref[e, rows, :])
            sg = _sigmoid(g)
            dg = dy * nrm * ngv * (sg * (1.0 + g * (1.0 - sg)))
            dt = dy * (g * sg)
            dng_ref[...] += jnp.sum(dt * nrm, axis=0, keepdims=True)
            dn = dt * ngv
            do = rstd * (dn - nrm * jnp.mean(dn * nrm, axis=-1, keepdims=True))
            do_b = do.astype(BF16)
            dox = _expand_lanes(do).astype(BF16)
            yield
            da = jnp.where(_stacked_causal(), _dot_nt(dox, v_b), 0.0).astype(BF16)
            dv_intra = _dot_tn(s, dox)
            dst_own = _dot_tn(do_b, qx)
            yield
            dq_in = _fold(_dot(da, k_in) + _dot(do_b, st_b), CHUNK)
            dk_in = _dot_tn(da, qx)
            dq = dq_in * (DK ** -0.5) * c["e_pos"]
            yield
            for _ in range(order):
                yield
            dst = carried[e]
            dstx = _expand_state(dst).astype(BF16)
            dv = dv_intra + _dot_nt(k_dec, dstx)
            dk_dec = _dot(v_b, dstx)
            carried[e] = dst_own + c["decay"] * dst
            yield
            dbl = (jnp.sum(dk_dec * c["kd_f"], axis=0, keepdims=True)
                   + c["decay"] * jnp.sum(dst * st, axis=0, keepdims=True))
            dk = dk_in * c["e_neg"] + dk_dec * c["e_dec"]
            db = dq_in * c["q_f"] - dk_in * c["k_f"] - dk_dec * c["kd_f"]
            row = lax.broadcasted_iota(jnp.int32, (CHUNK, 1), 0)
            da_log = _running_sum(db + jnp.where(row == CHUNK - 1, dbl, 0.0), False)
            yield
            dz = jnp.where(c["live"], da_log * (1.0 - _sigmoid(c["z"])) * (1.0 / GATE_TAU), 0.0)
            dz_b = dz.astype(BF16)
            out = du_ref.at[e, rows, :]
            out[:, 0:GLA_K] = dq.astype(BF16)
            out[:, GLA_K:2 * GLA_K] = dk.astype(BF16)
            out[:, 2 * GLA_K:2 * GLA_K + GLA_V] = dv.astype(BF16)
            out[:, 2 * GLA_K + GLA_V:2 * GLA_K + 2 * GLA_V] = _lanes_by_head(dg).astype(BF16)
            out[:, 2 * GLA_K + 2 * GLA_V:] = _dot_nt(dz_b, w2_ref[...]).astype(BF16)
            dw2_ref[...] += _dot_tn(lr.astype(BF16), dz_b)
            dgb_ref[...] += jnp.sum(dz, axis=0, keepdims=True)

        _in_lockstep(one_chunk(e, order) for order in range(GLA_GROUP) for e in range(n_ex))
        for e in range(n_ex):
            dstate[e] = carried[e]

    u3 = u.reshape(n_ex, lp, D_IN_PAD)
    rev = lambda w, col: pl.BlockSpec((n_ex, GLA_GROUP * CHUNK, w), lambda n: (0, nc // GLA_GROUP - 1 - n, col))
    (du, d_w2, d_gb, d_ng), extra = _call(
        body, name="gla_bwd", grid=(nc // GLA_GROUP,),
        in_specs=[rev(GLA_V, 1), rev(2 * GLA_K, 2), rev(GLA_V, 3), rev(GLA_V, 4), rev(128, 20),
                  pl.BlockSpec((n_ex, GLA_GROUP * DV, GLA_K), lambda n: (0, nc // GLA_GROUP - 1 - n, 0)),
                  _const_spec((128, GLA_K)), _const_spec((1, GLA_K)), _const_spec((1, DV))],
        out_specs=[rev(D_GLA_IN, 0), _acc_spec((128, GLA_K)), _acc_spec((1, GLA_K)), _acc_spec((1, DV))],
        out_shape=[jax.ShapeDtypeStruct((n_ex, lp, D_GLA_IN), BF16), jax.ShapeDtypeStruct((128, GLA_K), F32),
                   jax.ShapeDtypeStruct((1, GLA_K), F32), jax.ShapeDtypeStruct((1, DV), F32)],
        scratch_shapes=[pltpu.VMEM((n_ex, DV, GLA_K), F32)],
        plan=plan,
    )(dycat.reshape(n_ex, lp, D), u3, u3, u3, u3, states, w2, gb, ng)
    return (du.reshape(t, D_GLA_IN), d_w2, d_gb, d_ng), extra


def _in_proj_bwd(du_conv, du_gla, w_in_t_conv, w_in_t_gla, h0, dh1, g_mix, plan=None):
    t = h0.shape[0]
    r = _row_tile(t, 384)

    def body(dc_ref, dg_ref, wc_ref, wg_ref, h_ref, dh1_ref, g_ref, dh0_ref, dgm_ref):
        @pl.when(pl.program_id(0) == 0)
        def _():
            dgm_ref[...] = jnp.zeros_like(dgm_ref)

        dhn = _dot(dc_ref[...], wc_ref[...]) + _dot(dg_ref[...], wg_ref[...])
        h = h_ref[...]
        rstd = lax.rsqrt(jnp.mean(h * h, axis=-1, keepdims=True) + RMS_EPS)
        nrm = h * rstd
        dgm_ref[...] += jnp.sum(dhn * nrm, axis=0, keepdims=True)
        dn = dhn * g_ref[...]
        dh0_ref[...] = dh1_ref[...] + rstd * (dn - nrm * jnp.mean(dn * nrm, axis=-1, keepdims=True))

    rows = lambda w: pl.BlockSpec((r, w), lambda i: (i, 0))
    return _call(
        body, name="in_proj_bwd", grid=(t // r,),
        in_specs=[rows(2 * C_CONV), rows(D_GLA_IN), _const_spec((2 * C_CONV, D)), _const_spec((D_GLA_IN, D)),
                  rows(D), rows(D), _const_spec((1, D))],
        out_specs=[rows(D), _acc_spec((1, D))],
        out_shape=[jax.ShapeDtypeStruct((t, D), F32), jax.ShapeDtypeStruct((1, D), F32)],
        plan=plan,
    )(du_conv, du_gla, w_in_t_conv, w_in_t_gla, h0, dh1, g_mix)


def _wgrad(x, dy, name, plan=None):
    t, m = x.shape
    n = dy.shape[1]
    tk = t // 3 if t % (3 * 128) == 0 else _row_tile(t, 384)
    tm = m if m <= D_GLA_IN else m // 2

    def body(x_ref, dy_ref, o_ref):
        @pl.when(pl.program_id(1) == 0)
        def _():
            o_ref[...] = jnp.zeros_like(o_ref)

        o_ref[...] += _dot_tn(x_ref[...].astype(BF16), dy_ref[...].astype(BF16))

    (out,), extra = _call(
        body, name=name, grid=(m // tm, t // tk),
        in_specs=[pl.BlockSpec((tk, tm), lambda i, k: (k, i)), pl.BlockSpec((tk, n), lambda i, k: (k, 0))],
        out_specs=[pl.BlockSpec((tm, n), lambda i, k: (i, 0))],
        out_shape=[jax.ShapeDtypeStruct((m, n), F32)],
        plan=plan,
    )(x, dy)
    return out, extra


def _wgrad_pair(xa, xb, dy, name):
    t, m = xa.shape
    n = dy.shape[1]
    tk = t // 3 if t % (3 * 128) == 0 else _row_tile(t, 384)

    def body(xa_ref, xb_ref, dy_ref, o_ref):
        @pl.when(pl.program_id(1) == 0)
        def _():
            o_ref[...] = jnp.zeros_like(o_ref)

        x = jnp.where(pl.program_id(0) == 0, xa_ref[...], xb_ref[...])
        o_ref[...] += _dot_tn(x.astype(BF16), dy_ref[...].astype(BF16))

    rows = lambda w: pl.BlockSpec((tk, w), lambda i, k: (k, 0))
    return pl.pallas_call(
        body, name=name, grid=(2, t // tk), in_specs=[rows(m), rows(m), rows(n)],
        out_specs=pl.BlockSpec((m, n), lambda i, k: (i, 0)),
        out_shape=jax.ShapeDtypeStruct((2 * m, n), F32), compiler_params=_params(2),
    )(xa, xb, dy)


def _adam_update(g, w, m, v):
    m2 = ADAM_B1 * m + (1.0 - ADAM_B1) * g
    v2 = ADAM_B2 * v + (1.0 - ADAM_B2) * (g * g)
    m_hat = m2 / (1.0 - ADAM_B1 ** ADAM_STEP)
    v_hat = v2 / (1.0 - ADAM_B2 ** ADAM_STEP)
    delta = -ADAM_LR * (m_hat / (jnp.sqrt(v_hat) + ADAM_EPS) + ADAM_WD * w)
    return delta, m2, v2


ADAMW_STEPS = 4


def _adamw(g, w, m, v, name):
    rows, cols = g.shape
    steps = ADAMW_STEPS if rows % (ADAMW_STEPS * SUBLANES) == 0 else 1

    def body(g_ref, w_ref, m_ref, v_ref, d_ref, m2_ref, v2_ref):
        d_ref[...], m2_ref[...], v2_ref[...] = _adam_update(g_ref[...], w_ref[...], m_ref[...], v_ref[...])

    spec = pl.BlockSpec((rows // steps, cols), lambda i: (i, 0))
    return pl.pallas_call(
        body, name=name, grid=(steps,), in_specs=[spec] * 4, out_specs=[spec] * 3,
        out_shape=[jax.ShapeDtypeStruct(g.shape, F32)] * 3, compiler_params=_params(1),
    )(g, w, m, v)


def _adamw_halves(items, c, name):
    n = len(items)
    h = items[0][0].shape[-1]
    splits = lambda a: a.shape[0] % (ADAMW_STEPS * (SUBLANES if a.ndim == 2 else 1)) == 0
    steps = ADAMW_STEPS if all(splits(it[0]) for it in items) else 1

    def body(c_ref, *refs):
        ins, outs = refs[:5 * n], refs[5 * n:]
        own = pl.program_id(1) == c_ref[0]
        for i in range(n):
            a_ref, b_ref, w_ref, m_ref, v_ref = ins[5 * i:5 * i + 5]
            go_ref, d_ref, m2_ref, v2_ref = outs[4 * i:4 * i + 4]
            g = jnp.where(own, a_ref[...], b_ref[...])
            go_ref[...] = g
            d_ref[...], m2_ref[...], v2_ref[...] = _adam_update(g, w_ref[...], m_ref[...], v_ref[...])

    in_specs, out_specs, out_shape, args = [pl.BlockSpec(memory_space=pltpu.SMEM)], [], [], []
    for mine, theirs, w, m, v in items:
        tr = mine.shape[0] // steps
        mid = (0,) * (mine.ndim - 2)
        half = pl.BlockSpec((tr,) + mine.shape[1:-1] + (h,), lambda i, j, mid=mid: (i, *mid, 0))
        full = pl.BlockSpec((tr,) + mine.shape[1:-1] + (h,), lambda i, j, mid=mid: (i, *mid, j))
        in_specs += [half, half, full, full, full]
        out_specs += [full] * 4
        out_shape += [jax.ShapeDtypeStruct(w.shape, F32)] * 4
        args += [mine, theirs, w, m, v]
    res = pl.pallas_call(
        body, name=name, grid=(steps, 2), in_specs=in_specs, out_specs=out_specs, out_shape=out_shape,
        compiler_params=_params(2),
    )(jnp.reshape(c, (1,)).astype(jnp.int32), *args)
    return [res[4 * i:4 * i + 4] for i in range(n)]


def _rs_add_halves(pairs, c, name):
    blocks = pairs[0][0].shape[0]
    n = len(pairs)

    def body(c_ref, *refs):
        for i in range(n):
            refs[2 * n + i][...] = (refs[2 * i][...] + refs[2 * i + 1][...]).astype(BF16)

    in_specs, out_specs, out_shape = [], [], []
    for g, _ in pairs:
        _, rows, w = g.shape
        in_specs += [pl.BlockSpec((1, rows, w // 2), lambda j, s: (j, 0, s[0])),
                     pl.BlockSpec((1, rows, w // 2), lambda j, s: (j, 0, 0))]
        out_specs += [pl.BlockSpec((1, rows, w // 2), lambda j, s: (j, 0, 0))]
        out_shape += [jax.ShapeDtypeStruct((blocks, rows, w // 2), BF16)]
    return pl.pallas_call(
        body, name=name,
        grid_spec=pltpu.PrefetchScalarGridSpec(num_scalar_prefetch=1, grid=(blocks,), in_specs=in_specs,
                                               out_specs=out_specs),
        out_shape=out_shape, compiler_params=_params(1),
    )(jnp.reshape(c, (1,)).astype(jnp.int32), *[a for pair in pairs for a in pair])


def _rs_sum(pairs, mine, name):
    n = len(pairs)
    steps = 2 if all(own.shape[1] % (2 * 16) == 0 for own, _ in pairs) else 1

    def body(mine_ref, *refs):
        for i in range(n):
            p = refs[2 * i + 1][...].astype(F32)
            refs[2 * n + i][...] = ((refs[2 * i][0].astype(F32) + p[0]) + p[1]) + p[2]

    in_specs, out_specs, out_shape = [], [], []
    for own, _ in pairs:
        _, rows, h = own.shape
        tr = rows // steps
        in_specs += [pl.BlockSpec((1, tr, h), lambda i, s: (s[0], i, 0)),
                     pl.BlockSpec((3, tr, h), lambda i, s: (0, i, 0))]
        out_specs += [pl.BlockSpec((tr, h), lambda i, s: (i, 0))]
        out_shape += [jax.ShapeDtypeStruct((rows, h), F32)]
    return pl.pallas_call(
        body, name=name,
        grid_spec=pltpu.PrefetchScalarGridSpec(num_scalar_prefetch=1, grid=(steps,), in_specs=in_specs,
                                               out_specs=out_specs),
        out_shape=out_shape, compiler_params=_params(1),
    )(jnp.reshape(mine, (1,)).astype(jnp.int32), *[a for pair in pairs for a in pair])


def _sum_slots_adamw(slots, late_slots, vectors):
    late_rows = late_slots.shape[1]
    n = len(SMALL_PARTS)

    def body(s_ref, l_ref, *refs):
        ins, g_ref, outs = refs[:3 * n], refs[3 * n], refs[3 * n + 1:]
        g, late = s_ref[0], l_ref[0]
        for d in range(1, 8):
            g = g + s_ref[d]
            late = late + l_ref[d]
        g = jnp.concatenate([g[:late_rows] + late, g[late_rows:]], axis=0)
        g_ref[...] = g
        for i, (_, row, col, size) in enumerate(SMALL_PARTS):
            w_ref, m_ref, v_ref = ins[3 * i:3 * i + 3]
            go_ref, d_ref, m2_ref, v2_ref = outs[4 * i:4 * i + 4]
            piece = g[row:row + 1, col:col + size]
            go_ref[...] = piece
            d_ref[...], m2_ref[...], v2_ref[...] = _adam_update(piece, w_ref[...], m_ref[...], v_ref[...])

    vm = pl.BlockSpec(memory_space=pltpu.VMEM)
    out_shape = [jax.ShapeDtypeStruct(slots.shape[1:], F32)]
    for _, _, _, size in SMALL_PARTS:
        out_shape += [jax.ShapeDtypeStruct((1, size), F32)] * 4
    res = pl.pallas_call(body, name="small_sum_adamw", in_specs=[vm] * (2 + 3 * n), out_specs=[vm] * len(out_shape),
                         out_shape=out_shape)(slots, late_slots, *[a for wmv in vectors for a in wmv])
    return res[0], [res[1 + 4 * i:5 + 4 * i] for i in range(n)]


def _mesh_pos():
    return lax.axis_index("x"), lax.axis_index("y"), lax.axis_index("c")


def _other_chips(x, y):
    return [(1 - x, y), (x, 1 - y), (1 - x, 1 - y)]


def _half(ref, c, axis):
    n = ref.shape[axis] // 2
    return ref.at[(slice(None),) * axis + (pl.ds(c * n, n),)]


def _remote(src, dst, send_sem, recv_sem, device):
    return pltpu.make_async_remote_copy(src_ref=src, dst_ref=dst, send_sem=send_sem, recv_sem=recv_sem,
                                        device_id=device, device_id_type=MESH)


def _gather_plan(split, whole=(), axes=None):
    split, whole = list(split), list(whole)
    ns, n = len(split), len(split) + len(whole)

    def make(ins, outs, sems):
        ici_send, ici_recv, d2d_send, d2d_recv, own_send, own_recv = sems
        x, y, c = _mesh_pos()
        mine = 2 * x + y
        chips = _other_chips(x, y)
        blocks = [2 * px + py for px, py in chips]

        def own(a):
            return _remote(ins[a], outs[a].at[mine], own_send.at[a], own_recv.at[a], (x, y, 1 - c))

        def ici(a, k, block):
            px, py = chips[k]
            src, dst = ins[a], outs[a].at[block]
            if a < ns:
                src, dst = _half(src, c, axes[a]), _half(dst, c, axes[a])
            return _remote(src, dst, ici_send.at[3 * a + k], ici_recv.at[3 * a + k], (px, py, c))

        def d2d(a, k, half):
            part = _half(outs[a].at[blocks[k]], half, axes[a])
            return _remote(part, part, d2d_send.at[3 * a + k], d2d_recv.at[3 * a + k], (x, y, 1 - c))

        def start():
            for a in range(n):
                for k in range(3):
                    ici(a, k, mine).start()
                own(a).start()

        def relay():
            for a in range(n):
                for k in range(3):
                    ici(a, k, blocks[k]).wait_recv()
                    if a < ns:
                        d2d(a, k, c).start()

        def finish():
            for a in range(ns):
                for k in range(3):
                    d2d(a, k, 1 - c).wait_recv()
            for a in range(n):
                for k in range(3):
                    ici(a, k, mine).wait_send()
                    if a < ns:
                        d2d(a, k, c).wait_send()
                own(a).wait()

        return start, relay, finish

    arrays = split + whole
    axes = [0] * ns if axes is None else list(axes)
    return _Plan(arrays, [jax.ShapeDtypeStruct((N_CHIPS,) + s.shape, s.dtype) for s in arrays],
                 [pltpu.SemaphoreType.DMA((3 * n,)), pltpu.SemaphoreType.DMA((3 * n,)),
                  pltpu.SemaphoreType.DMA((3 * ns,)), pltpu.SemaphoreType.DMA((3 * ns,)),
                  pltpu.SemaphoreType.DMA((n,)), pltpu.SemaphoreType.DMA((n,))], make)


def _to_sibling_plan(gs):
    n = len(gs)

    def make(ins, outs, sems):
        send_sems, recv_sems = sems
        x, y, c = _mesh_pos()

        def copy(a):
            return _remote(_half(ins[a], 1 - c, len(ins[a].shape) - 1), outs[a], send_sems.at[a],
                           recv_sems.at[a], (x, y, 1 - c))

        def start():
            for a in range(n):
                copy(a).start()

        def finish():
            for a in range(n):
                copy(a).wait()

        return start, finish

    return _Plan(list(gs), [jax.ShapeDtypeStruct(g.shape[:-1] + (g.shape[-1] // 2,), g.dtype) for g in gs],
                 [pltpu.SemaphoreType.DMA((n,)), pltpu.SemaphoreType.DMA((n,))], make)


def _chip_exchange_plan(ps):
    n = len(ps)

    def make(ins, outs, sems):
        send_sems, recv_sems = sems
        x, y, c = _mesh_pos()
        chips = _other_chips(x, y)

        def ici(a, k):
            px, py = chips[k]
            return _remote(ins[a].at[2 * px + py], outs[a].at[k], send_sems.at[3 * a + k],
                           recv_sems.at[3 * a + k], (px, py, c))

        def start():
            for a in range(n):
                for k in range(3):
                    ici(a, k).start()

        def finish():
            for a in range(n):
                for k in range(3):
                    ici(a, k).wait()

        return start, finish

    return _Plan(list(ps), [jax.ShapeDtypeStruct((3,) + p.shape[1:], p.dtype) for p in ps],
                 [pltpu.SemaphoreType.DMA((3 * n,)), pltpu.SemaphoreType.DMA((3 * n,))], make)


def _share_plan(halves):
    n = len(halves)

    def make(ins, outs, sems):
        send_sems, recv_sems = sems
        x, y, c = _mesh_pos()

        def d2d(a):
            return _remote(ins[a], outs[a], send_sems.at[a], recv_sems.at[a], (x, y, 1 - c))

        def start():
            for a in range(n):
                d2d(a).start()

        def finish():
            for a in range(n):
                d2d(a).wait()

        return start, finish

    return _Plan(list(halves), [jax.ShapeDtypeStruct(p.shape, p.dtype) for p in halves],
                 [pltpu.SemaphoreType.DMA((n,)), pltpu.SemaphoreType.DMA((n,))], make)


def _all_to_all_plan(part):
    def make(ins, outs, sems):
        send_sems, recv_sems, local_sem = sems
        (p_ref,), (slots,) = ins, outs
        x, y, c = _mesh_pos()
        me = 4 * x + 2 * y + c
        peers = [(px, py, pc) for px in (x, 1 - x) for py in (y, 1 - y) for pc in (c, 1 - c)][1:]

        def remote(k, slot):
            return _remote(p_ref, slots.at[slot], send_sems.at[k], recv_sems.at[k], peers[k])

        def local():
            return pltpu.make_async_copy(p_ref, slots.at[me], local_sem)

        def start():
            for k in range(7):
                remote(k, me).start()
            local().start()

        def finish():
            for k, (px, py, pc) in enumerate(peers):
                remote(k, 4 * px + 2 * py + pc).wait_recv()
            for k in range(7):
                remote(k, me).wait_send()
            local().wait()

        return start, finish

    return _Plan([part], [jax.ShapeDtypeStruct((8,) + part.shape, part.dtype)],
                 [pltpu.SemaphoreType.DMA((7,)), pltpu.SemaphoreType.DMA((7,)), pltpu.SemaphoreType.DMA(())], make)


def _merge_plans(a, b):
    na_in, na_out, na_sems = len(a.arrays), len(a.out_shape), len(a.sems)

    def make(ins, outs, sems):
        phases_a = _phases(a.make(ins[:na_in], outs[:na_out], sems[:na_sems]))
        phases_b = _phases(b.make(ins[na_in:], outs[na_out:], sems[na_sems:]))

        def both(i):
            def run():
                phases_a[i]()
                phases_b[i]()
            return run

        return both(0), both(1), both(2)

    return _Plan(list(a.arrays) + list(b.arrays), list(a.out_shape) + list(b.out_shape),
                 list(a.sems) + list(b.sems), make)


def _exchange(plan, name):
    n_in, n_out = len(plan.arrays), len(plan.out_shape)

    def body(*refs):
        for phase in _phases(plan.make(refs[:n_in], refs[n_in:n_in + n_out], refs[n_in + n_out:])):
            phase()

    return pl.pallas_call(
        body, name=name, in_specs=[HBM_SPEC] * n_in, out_specs=[HBM_SPEC] * n_out, out_shape=list(plan.out_shape),
        scratch_shapes=list(plan.sems), compiler_params=pltpu.CompilerParams(has_side_effects=True),
    )(*plan.arrays)


def _pack_small(parts):
    rows = []
    for r in range(SMALL_ROWS):
        pieces, col = [], 0
        for name, row, start, size in SMALL_PARTS:
            if row == r:
                assert start == col
                pieces.append(parts[name].reshape(1, size).astype(F32))
                col += size
        rows.append(jnp.concatenate(pieces + [jnp.zeros((1, D - col), F32)], axis=1))
    return jnp.concatenate(rows, axis=0)


def _columns(gathered):
    return jnp.concatenate([gathered[j] for j in range(N_CHIPS)], axis=1)


def kernel(x, meta_tokens, norm_mix_g, w_in, conv_w, conv_b, conv_ln_g, conv_ln_b, gla_w_gate2, gla_gate_b, gla_norm_g, w_out, norm_ffn_g, w_ffn_gate, w_ffn_up, w_ffn_down, norm_final_g, loss_target, m_meta_tokens, m_norm_mix_g, m_w_in, m_conv_w, m_conv_b, m_conv_ln_g, m_conv_ln_b, m_gla_w_gate2, m_gla_gate_b, m_gla_norm_g, m_w_out, m_norm_ffn_g, m_w_ffn_gate, m_w_ffn_up, m_w_ffn_down, m_norm_final_g, v_meta_tokens, v_norm_mix_g, v_w_in, v_conv_w, v_conv_b, v_conv_ln_g, v_conv_ln_b, v_gla_w_gate2, v_gla_gate_b, v_gla_norm_g, v_w_out, v_norm_ffn_g, v_w_ffn_gate, v_w_ffn_up, v_w_ffn_down, v_norm_final_g):
    ws = dict(zip(WEIGHT_NAMES, (meta_tokens, norm_mix_g, w_in, conv_w, conv_b, conv_ln_g, conv_ln_b, gla_w_gate2,
                                 gla_gate_b, gla_norm_g, w_out, norm_ffn_g, w_ffn_gate, w_ffn_up, w_ffn_down,
                                 norm_final_g)))
    ms = dict(zip(WEIGHT_NAMES, (m_meta_tokens, m_norm_mix_g, m_w_in, m_conv_w, m_conv_b, m_conv_ln_g, m_conv_ln_b,
                                 m_gla_w_gate2, m_gla_gate_b, m_gla_norm_g, m_w_out, m_norm_ffn_g, m_w_ffn_gate,
                                 m_w_ffn_up, m_w_ffn_down, m_norm_final_g)))
    vs = dict(zip(WEIGHT_NAMES, (v_meta_tokens, v_norm_mix_g, v_w_in, v_conv_w, v_conv_b, v_conv_ln_g, v_conv_ln_b,
                                 v_gla_w_gate2, v_gla_gate_b, v_gla_norm_g, v_w_out, v_norm_ffn_g, v_w_ffn_gate,
                                 v_w_ffn_up, v_w_ffn_down, v_norm_final_g)))
    c = lax.axis_index("c")
    mine = 2 * lax.axis_index("x") + lax.axis_index("y")
    shard = lambda d, name: d[name].reshape(d[name].shape[-2:])
    vec = {name: ws[name].reshape(1, -1) for name, _, _, _ in SMALL_PARTS}
    n_ex, seq, _ = x.shape
    lp = HEAD_ROWS + seq
    t = n_ex * lp

    (tgt, h0, gate_s, up_s, out_s, down_s), (w_in_g, meta_g, conv_w_g, w2_g) = _pad_head_rows(
        [loss_target, x],
        [shard(ws, "w_ffn_gate").T, shard(ws, "w_ffn_up").T, shard(ws, "w_out"), shard(ws, "w_ffn_down")],
        plan=_gather_plan([shard(ws, "w_in").T.astype(BF16)],
                          [shard(ws, "meta_tokens"), shard(ws, "conv_w"), shard(ws, "gla_w_gate2")], axes=[1]))
    w_in_t = jnp.concatenate([w_in_g.reshape(D_IN, D), jnp.zeros((D_IN_PAD - D_IN, D), BF16)], axis=0)
    conv_w_full = jnp.concatenate([_columns(conv_w_g), jnp.zeros((32 - CONV_W, C_CONV), F32)], axis=0)
    w2_full = jnp.concatenate([_columns(w2_g), jnp.zeros((128 - RANK, GLA_K), F32)], axis=0).astype(BF16)
    h0 = _set_meta_rows(h0, _columns(meta_g)).reshape(t, D)
    tgt = tgt.reshape(t, D)
    row_mask = jnp.concatenate([jnp.zeros((n_ex, HEAD_ROWS, 1), F32), jnp.ones((n_ex, seq, 1), F32)],
                               axis=1).reshape(t, 1)

    (u, hn), (gate_g,) = _in_proj(h0, vec["norm_mix_g"], w_in_t.T, plan=_gather_plan([gate_s]))
    (yc, y_conv), (up_g, w_out_g) = _conv_fwd(
        u, conv_w_full, vec["conv_b"], vec["conv_ln_g"], vec["conv_ln_b"], n_ex, lp,
        plan=_gather_plan([up_s, out_s]))
    (y_gla, states), _ = _gla_fwd(u, w2_full, vec["gla_gate_b"], vec["gla_norm_g"], n_ex, lp)
    w_out_full = w_out_g.reshape(D, D)
    w_gate_t, w_up_t = gate_g.reshape(D_FF, D), up_g.reshape(D_FF, D)
    (h1, hn2, gate, up, act), (down_g,) = _mix_out_ffn_up(
        h0, y_conv, y_gla, w_out_full, vec["norm_ffn_g"], w_gate_t, w_up_t,
        plan=_gather_plan([down_s]))
    w_down_full = down_g.reshape(D_FF, D)
    dh2, loss, d_final_g = _ffn_down_loss(act, w_down_full, h1, tgt, vec["norm_final_g"], row_mask)
    dgate, dup, dh1, dycat, d_ffn_g = _ffn_bwd(dh2, gate, up, h1, w_down_full, w_gate_t, w_up_t, w_out_full,
                                                vec["norm_ffn_g"])

    ffn_block = lambda g: g.reshape(N_CHIPS, D_FF // N_CHIPS, D)
    g_gate = ffn_block(_wgrad(dgate, hn2, "wgrad_gate")[0])
    g_up, (gate_sib,) = _wgrad(dup, hn2, "wgrad_up", _to_sibling_plan([g_gate]))
    g_up = ffn_block(g_up)
    g_down, (up_sib,) = _wgrad(act, dh2, "wgrad_down", _to_sibling_plan([g_up]))
    g_down = ffn_block(g_down)
    g_out = _wgrad_pair(y_conv, y_gla, dh1, "wgrad_out").reshape(N_CHIPS, D // N_CHIPS, D)
    cs_gate, cs_up = _rs_add_halves([(g_gate, gate_sib), (g_up, up_sib)], c, "rs_add_gate_up")
    (du_conv, d_conv_w, d_conv_b, d_ln_g, d_ln_b), (ex_gate, ex_up, down_sib, out_sib) = _conv_bwd(
        dycat, yc, u, conv_w_full, vec["conv_ln_g"], vec["conv_ln_b"], n_ex, lp,
        plan=_merge_plans(_chip_exchange_plan([cs_gate, cs_up]), _to_sibling_plan([g_down, g_out])))
    cs_down, cs_out = _rs_add_halves([(g_down, down_sib), (g_out, out_sib)], c, "rs_add_down_out")
    (du_gla, d_w2, d_gate_b, d_norm_g), (ex_down, ex_out) = _gla_bwd(
        dycat, u, states, w2_full, vec["gla_gate_b"], vec["gla_norm_g"], n_ex, lp,
        plan=_chip_exchange_plan([cs_down, cs_out]))
    halves = _rs_sum([(cs_gate, ex_gate), (cs_up, ex_up), (cs_down, ex_down), (cs_out, ex_out)], mine,
                     "rs_sum_early")

    small = {"norm_mix_g": jnp.zeros((1, D), F32), "norm_ffn_g": d_ffn_g, "norm_final_g": d_final_g,
             "conv_b": d_conv_b, "conv_ln_g": d_ln_g, "conv_ln_b": d_ln_b, "gla_gate_b": d_gate_b,
             "gla_norm_g": d_norm_g}
    part = lax.dynamic_update_slice(_pack_small(small), loss[:, :1], (LOSS_ROW, 0))
    part = jnp.concatenate([part, jnp.zeros((N_META, D), F32), d_conv_w.reshape(16, D), d_w2[:RANK].reshape(4, D),
                            jnp.zeros((4, D), F32)], axis=0)
    g_in_conv = _wgrad(du_conv, hn, "wgrad_in_conv")[0][None]
    g_in_gla, (slots, conv_sib) = _wgrad(du_gla, hn, "wgrad_in_gla",
                                         _merge_plans(_all_to_all_plan(part), _to_sibling_plan([g_in_conv])))
    pieces = [g_in_conv, g_in_gla[None]]
    (gla_sib,) = _exchange(_to_sibling_plan(pieces[1:]), "rs_late_to_sibling")
    sums = _rs_add_halves(list(zip(pieces, (conv_sib, gla_sib))), c, "rs_add_w_in")
    in_chip_sum = jnp.concatenate([sums[0][0], sums[1][0]], axis=0)[:D_IN].reshape(N_CHIPS, D_IN // N_CHIPS, D // 2)
    (dh0, d_mix_g), shared = _in_proj_bwd(
        du_conv, du_gla, w_in_t[:2 * C_CONV], w_in_t[2 * C_CONV:], h0, dh1, vec["norm_mix_g"],
        plan=_merge_plans(_share_plan(halves), _chip_exchange_plan([in_chip_sum])))
    dh0 = dh0.reshape(n_ex, lp, D)
    grad_x = dh0[:, HEAD_ROWS:]
    late_part = jnp.concatenate([d_mix_g, jnp.zeros((SMALL_ROWS - 1, D), F32),
                                 jnp.sum(dh0[:, PAD_ROWS:HEAD_ROWS], axis=0)], axis=0)
    (in_half,) = _rs_sum([(in_chip_sum, shared[4])], mine, "rs_sum_w_in")
    in_shared, late_slots = _exchange(_merge_plans(_share_plan([in_half]), _all_to_all_plan(late_part)),
                                      "late_exchange")

    out = {"grad": {}, "delta": {}, "new_m": {}, "new_v": {}}

    def record(name, res, transposed=False):
        for kind, a in zip(("grad", "delta", "new_m", "new_v"), res):
            out[kind][name] = (a.T if transposed else a).reshape(ws[name].shape)

    def operands(name, transposed):
        lay = (lambda a: a.T) if transposed else (lambda a: a)
        return lay(shard(ws, name)), lay(shard(ms, name)), lay(shard(vs, name))

    early_layout = (("w_ffn_gate", True), ("w_ffn_up", True), ("w_ffn_down", False), ("w_out", False))
    items = [(mine_half, their_half, *operands(name, transposed))
             for (name, transposed), mine_half, their_half in zip(early_layout, halves, shared)]
    for (name, transposed), res in zip(early_layout, _adamw_halves(items, c, "adamw_early")):
        record(name, res, transposed)

    tile_rows = lambda a: a.reshape(a.shape[0], 1, a.shape[1])
    by_output = lambda d: jnp.transpose(d["w_in"], (2, 0, 1))
    res = _adamw_halves([(tile_rows(in_half), tile_rows(in_shared), by_output(ws), by_output(ms), by_output(vs))],
                        c, "adamw_w_in")[0]
    for kind, a in zip(("grad", "delta", "new_m", "new_v"), res):
        out[kind]["w_in"] = jnp.transpose(a, (1, 2, 0))

    flat = lambda d, name: d[name].reshape(1, -1)
    g_s, updated = _sum_slots_adamw(slots, late_slots,
                                    [(flat(ws, name), flat(ms, name), flat(vs, name)) for name, _, _, _ in SMALL_PARTS])
    for (name, _, _, _), res in zip(SMALL_PARTS, updated):
        record(name, res)
    loss = g_s[LOSS_ROW, 0]
    block = lambda a, width: lax.dynamic_slice_in_dim(a, mine * width, width, axis=1)
    small_sharded = {"meta_tokens": block(g_s[8:24], D // N_CHIPS),
                     "conv_w": block(g_s[24:40].reshape(32, C_CONV), C_CONV // N_CHIPS)[:CONV_W],
                     "gla_w_gate2": block(g_s[40:44].reshape(RANK, GLA_K), GLA_K // N_CHIPS)}
    for name, g in small_sharded.items():
        record(name, [g, *_adamw(g, *operands(name, False), "adamw_" + name)])

    return (loss, grad_x, *[out[kind][name] for kind in ("grad", "delta", "new_m", "new_v") for name in WEIGHT_NAMES])
```

```python
import functools
from typing import Any, Callable, NamedTuple, Sequence

import jax
import jax.numpy as jnp
from jax import lax
from jax.experimental import pallas as pl
from jax.experimental.pallas import tpu as pltpu

F32 = jnp.float32
BF16 = jnp.bfloat16
MESH = pl.DeviceIdType.MESH

D = 1024
N_META = 16
C_CONV = 512
CONV_W = 31
GLA_K = 256
GLA_V = 512
N_HEADS = 4
DK = 64
DV = 128
RANK = 16
CHUNK = 64
PAD_ROWS = CHUNK - N_META
HEAD_ROWS = CHUNK
D_IN = 2576
D_IN_PAD = 2688
D_GLA_IN = D_IN_PAD - 2 * C_CONV
D_FF = 2816
RMS_EPS = 1e-6
LN_EPS = 1e-5
GATE_TAU = 16.0
N_CHIPS = 4

ADAM_LR = 0.001
ADAM_B1 = 0.9
ADAM_B2 = 0.999
ADAM_EPS = 1e-08
ADAM_WD = 0.01
ADAM_STEP = 10

V7X_VMEM_BYTES = 64 * 1024 * 1024
VMEM_LIMIT = V7X_VMEM_BYTES - 8 * 1024 * 1024
SUBLANES = 8
ROW_PART = 128
FFN_BWD_TILE = 192

WEIGHT_NAMES = ("meta_tokens", "norm_mix_g", "w_in", "conv_w", "conv_b", "conv_ln_g", "conv_ln_b", "gla_w_gate2",
                "gla_gate_b", "gla_norm_g", "w_out", "norm_ffn_g", "w_ffn_gate", "w_ffn_up", "w_ffn_down",
                "norm_final_g")

SMALL_ROWS = 8
SMALL_PARTS = (("norm_mix_g", 0, 0, D), ("norm_ffn_g", 1, 0, D), ("norm_final_g", 2, 0, D),
               ("conv_b", 3, 0, C_CONV), ("conv_ln_g", 3, C_CONV, C_CONV), ("conv_ln_b", 4, 0, C_CONV),
               ("gla_gate_b", 4, C_CONV, GLA_K), ("gla_norm_g", 4, C_CONV + GLA_K, DV))
LOSS_ROW = 5

HBM_SPEC = pl.BlockSpec(memory_space=pltpu.HBM)


def _dot(a, b):
    return jnp.dot(a, b, preferred_element_type=F32)


def _dot_nt(a, b):
    return lax.dot_general(a, b, (((1,), (1,)), ((), ())), preferred_element_type=F32)


def _dot_tn(a, b):
    return lax.dot_general(a, b, (((0,), (0,)), ((), ())), preferred_element_type=F32)


def _sigmoid(x):
    return 1.0 / (1.0 + jnp.exp(-x))


def _const_spec(shape):
    return pl.BlockSpec(shape, lambda *_: (0,) * len(shape), pipeline_mode=pl.Buffered(1))


def _acc_spec(shape):
    return pl.BlockSpec(shape, lambda *_: (0,) * len(shape))


def _params(n_axes):
    return pltpu.CompilerParams(dimension_semantics=("arbitrary",) * n_axes, vmem_limit_bytes=VMEM_LIMIT)


def _row_tile(t, want):
    for r in (want, 384, 192, 128, 64):
        if r <= want and t % r == 0:
            return r
    raise ValueError(f"no row tile for {t}")


def _row_parts(r):
    if r % ROW_PART:
        return [slice(None)]
    return [pl.ds(i * ROW_PART, ROW_PART) for i in range(r // ROW_PART)]


def _in_lockstep(bodies):
    live = list(bodies)
    while live:
        still = []
        for g in live:
            try:
                next(g)
                still.append(g)
            except StopIteration:
                pass
        live = still


class _Plan(NamedTuple):
    arrays: Sequence[Any]
    out_shape: Sequence[Any]
    sems: Sequence[Any]
    make: Callable


def _phases(made):
    return made if len(made) == 3 else (made[0], lambda: None, made[1])


def _call(body, *, name, grid, in_specs, out_specs, out_shape, scratch_shapes=(), plan=None):
    n_in, n_out, n_scr = len(in_specs), len(out_specs), len(scratch_shapes)
    if plan is None:
        plan = _Plan([], [], [], lambda ins, outs, sems: (lambda: None, lambda: None))
    nx_in, nx_out = len(plan.arrays), len(plan.out_shape)
    n_steps = functools.reduce(lambda a, b: a * b, grid)

    def hosted(*refs):
        ins, xins = refs[:n_in], refs[n_in:n_in + nx_in]
        o0 = n_in + nx_in
        outs, xouts = refs[o0:o0 + n_out], refs[o0 + n_out:o0 + n_out + nx_out]
        s0 = o0 + n_out + nx_out
        scr, sems = refs[s0:s0 + n_scr], refs[s0 + n_scr:]
        step = functools.reduce(lambda acc, a: acc * grid[a] + pl.program_id(a), range(len(grid)), 0)
        start, relay, finish = _phases(plan.make(xins, xouts, sems))
        pl.when(step == 0)(start)
        pl.when(step == n_steps - 1)(relay)
        body(*ins, *outs, *scr)
        pl.when(step == n_steps - 1)(finish)

    call = pl.pallas_call(
        hosted, name=name, grid=grid, in_specs=list(in_specs) + [HBM_SPEC] * nx_in,
        out_specs=list(out_specs) + [HBM_SPEC] * nx_out, out_shape=list(out_shape) + list(plan.out_shape),
        scratch_shapes=list(scratch_shapes) + list(plan.sems),
        compiler_params=pltpu.CompilerParams(dimension_semantics=("arbitrary",) * len(grid),
                                             vmem_limit_bytes=VMEM_LIMIT, has_side_effects=nx_in > 0))

    def run(*args):
        res = call(*args, *plan.arrays)
        return res[:n_out], res[n_out:]

    return run


def _pad_head_rows(arrays, casts, plan=None):
    n_ex, seq, _ = arrays[0].shape
    nc = (HEAD_ROWS + seq) // CHUNK
    n, k = len(arrays), len(casts)

    def body(*refs):
        ins, outs = refs[:n + k], refs[n + k:]
        for a_ref, o_ref in zip(ins[:n], outs[:n]):
            o_ref[...] = jnp.where(pl.program_id(0) > 0, a_ref[...], 0.0)

        @pl.when(pl.program_id(0) == 0)
        def _():
            for a_ref, o_ref in zip(ins[n:], outs[n:]):
                o_ref[...] = a_ref[...].astype(BF16)

    whole = lambda a: pl.BlockSpec(a.shape, lambda i: (0, 0))
    return _call(
        body, name="pad_head_rows", grid=(nc,),
        in_specs=([pl.BlockSpec((n_ex, CHUNK, D), lambda i: (0, jnp.maximum(i - 1, 0), 0))] * n
                  + [_const_spec(a.shape) for a in casts]),
        out_specs=[pl.BlockSpec((n_ex, CHUNK, D), lambda i: (0, i, 0))] * n + [whole(a) for a in casts],
        out_shape=([jax.ShapeDtypeStruct((n_ex, HEAD_ROWS + seq, D), F32)] * n
                   + [jax.ShapeDtypeStruct(a.shape, BF16) for a in casts]),
        plan=plan,
    )(*arrays, *casts)


def _set_meta_rows(h0, meta):
    n_ex = h0.shape[0]

    def body(h_ref, meta_ref, o_ref):
        o_ref[...] = jnp.concatenate(
            [h_ref[:, :PAD_ROWS, :], jnp.broadcast_to(meta_ref[...][None], (n_ex, N_META, D))], axis=1)

    head = pl.BlockSpec((n_ex, HEAD_ROWS, D), lambda i: (0, 0, 0))
    return pl.pallas_call(
        body, name="set_meta_rows", grid=(1,), in_specs=[head, pl.BlockSpec((N_META, D), lambda i: (0, 0))],
        out_specs=head, out_shape=jax.ShapeDtypeStruct(h0.shape, F32), input_output_aliases={0: 0},
        compiler_params=_params(1),
    )(h0, meta)


def _in_proj(h0, g_mix, w_in_t, plan=None):
    t = h0.shape[0]
    r = _row_tile(t, 384)

    def body(h_ref, g_ref, w_ref, u_ref, hn_ref):
        h = h_ref[...]
        rstd = lax.rsqrt(jnp.mean(h * h, axis=-1, keepdims=True) + RMS_EPS)
        hn = (h * rstd * g_ref[...]).astype(BF16)
        hn_ref[...] = hn
        u_ref[...] = _dot_nt(hn, w_ref[...])

    return _call(
        body, name="in_proj", grid=(t // r,),
        in_specs=[pl.BlockSpec((r, D), lambda i: (i, 0)), _const_spec((1, D)), _const_spec((D_IN_PAD, D))],
        out_specs=[pl.BlockSpec((r, D_IN_PAD), lambda i: (i, 0)), pl.BlockSpec((r, D), lambda i: (i, 0))],
        out_shape=[jax.ShapeDtypeStruct((t, D_IN_PAD), F32), jax.ShapeDtypeStruct((t, D), BF16)],
        plan=plan,
    )(h0, g_mix, w_in_t)


CONV_TILE = 192
CONV_SUB = 32
CONV_LEAD = CONV_SUB - (CONV_W - 1)


def _shifted_copies(src, dst, r):
    for s in range(1, SUBLANES):
        dst[s - 1] = src[s:s + r + CONV_SUB - SUBLANES, :]


def _shifted_rows(src, shifted, start):
    base, s = SUBLANES * (start // SUBLANES), start % SUBLANES
    if s == 0:
        return src[base:base + CONV_SUB, :]
    return shifted[s - 1, base:base + CONV_SUB, :]


def _conv_fwd(u, conv_w, conv_b, ln_g, ln_b, n_ex, lp, plan=None):
    r = CONV_TILE
    nt = lp // r
    hb = r // CONV_SUB

    def body(cur_ref, prev_ref, w_ref, b_ref, lg_ref, lb_ref, yc_ref, y_ref, glu, glu_sh):
        i = pl.program_id(1)
        cur = cur_ref[...]
        glu[CONV_SUB:CONV_SUB + r, :] = cur[:, :C_CONV] * _sigmoid(cur[:, C_CONV:])
        pv = prev_ref[...]
        halo = pv[:, :C_CONV] * _sigmoid(pv[:, C_CONV:])
        glu[0:CONV_SUB, :] = jnp.where(i > 0, halo, 0.0)
        _shifted_copies(glu, glu_sh, r)
        w = w_ref[...]
        for j in range(r // CONV_SUB):
            r0 = j * CONV_SUB
            acc = jnp.zeros((CONV_SUB, C_CONV), F32) + b_ref[...]
            for k in range(CONV_W):
                acc = acc + w[k:k + 1, :] * _shifted_rows(glu, glu_sh, r0 + CONV_LEAD + k)
            mu = jnp.mean(acc, axis=-1, keepdims=True)
            cen = acc - mu
            var = jnp.mean(cen * cen, axis=-1, keepdims=True)
            out = cen * lax.rsqrt(var + LN_EPS) * lg_ref[...] + lb_ref[...]
            y = out * _sigmoid(out)
            row = i * r + r0 + lax.broadcasted_iota(jnp.int32, (CONV_SUB, 1), 0)
            y = jnp.where(row >= PAD_ROWS, y, 0.0)
            yc_ref[r0:r0 + CONV_SUB, :] = acc
            y_ref[r0:r0 + CONV_SUB, :] = y.astype(BF16)

    t = n_ex * lp
    return _call(
        body, name="conv_fwd", grid=(n_ex, nt),
        in_specs=[pl.BlockSpec((r, 2 * C_CONV), lambda b, i: (b * nt + i, 0)),
                  pl.BlockSpec((CONV_SUB, 2 * C_CONV), lambda b, i: (jnp.maximum((b * nt + i) * hb - 1, 0), 0)),
                  _const_spec((32, C_CONV)), _const_spec((1, C_CONV)), _const_spec((1, C_CONV)), _const_spec((1, C_CONV))],
        out_specs=[pl.BlockSpec((r, C_CONV), lambda b, i: (b * nt + i, 0)),
                   pl.BlockSpec((r, C_CONV), lambda b, i: (b * nt + i, 0))],
        out_shape=[jax.ShapeDtypeStruct((t, C_CONV), F32), jax.ShapeDtypeStruct((t, C_CONV), BF16)],
        scratch_shapes=[pltpu.VMEM((r + CONV_SUB, C_CONV), F32),
                        pltpu.VMEM((SUBLANES - 1, r + CONV_SUB - SUBLANES, C_CONV), F32)],
        plan=plan,
    )(u, u, conv_w, conv_b, ln_g, ln_b)


def _mix_out_ffn_up(h0, y_conv, y_gla, w_out, g_ffn, w_gate_t, w_up_t, plan=None):
    t = h0.shape[0]
    r = _row_tile(t, 384)

    def body(h0_ref, yc_ref, yg_ref, wo_ref, g_ref, wg_ref, wu_ref, h1_ref, hn_ref, gate_ref, up_ref, act_ref):
        h1 = h0_ref[...] + _dot(yc_ref[...], wo_ref[0:C_CONV, :]) + _dot(yg_ref[...], wo_ref[C_CONV:D, :])
        h1_ref[...] = h1
        rstd = lax.rsqrt(jnp.mean(h1 * h1, axis=-1, keepdims=True) + RMS_EPS)
        hn = (h1 * rstd * g_ref[...]).astype(BF16)
        hn_ref[...] = hn
        gate = _dot_nt(hn, wg_ref[...])
        up = _dot_nt(hn, wu_ref[...])
        gate_ref[...] = gate
        up_ref[...] = up
        act_ref[...] = (gate * _sigmoid(gate) * up).astype(BF16)

    rows = lambda w: pl.BlockSpec((r, w), lambda i: (i, 0))
    return _call(
        body, name="mix_out_ffn_up", grid=(t // r,),
        in_specs=[rows(D), rows(C_CONV), rows(GLA_V), _const_spec((D, D)), _const_spec((1, D)),
                  _const_spec((D_FF, D)), _const_spec((D_FF, D))],
        out_specs=[rows(D), rows(D), rows(D_FF), rows(D_FF), rows(D_FF)],
        out_shape=[jax.ShapeDtypeStruct((t, D), F32), jax.ShapeDtypeStruct((t, D), BF16),
                   jax.ShapeDtypeStruct((t, D_FF), F32), jax.ShapeDtypeStruct((t, D_FF), F32),
                   jax.ShapeDtypeStruct((t, D_FF), BF16)],
        plan=plan,
    )(h0, y_conv, y_gla, w_out, g_ffn, w_gate_t, w_up_t)


def _ffn_down_loss(act, w_down, h1, target, g_final, row_mask):
    t = h1.shape[0]
    r = _row_tile(t, 384)

    def body(act_ref, wd_ref, h1_ref, tgt_ref, gf_ref, mask_ref, dh2_ref, loss_ref, dgf_ref):
        @pl.when(pl.program_id(0) == 0)
        def _():
            loss_ref[...] = jnp.zeros_like(loss_ref)
            dgf_ref[...] = jnp.zeros_like(dgf_ref)

        gf = gf_ref[...]

        def part(rows):
            h2 = h1_ref[rows, :] + _dot(act_ref[rows, :], wd_ref[...])
            yield
            rstd = lax.rsqrt(jnp.mean(h2 * h2, axis=-1, keepdims=True) + RMS_EPS)
            nrm = h2 * rstd
            err = (nrm * gf - tgt_ref[rows, :]) * mask_ref[rows, :]
            loss_ref[...] += jnp.sum(err * err) * (0.5 / D)
            dy = err * (1.0 / D)
            dgf_ref[...] += jnp.sum(dy * nrm, axis=0, keepdims=True)
            dn = dy * gf
            dh2_ref[rows, :] = rstd * (dn - nrm * jnp.mean(dn * nrm, axis=-1, keepdims=True))

        _in_lockstep(part(rows) for rows in _row_parts(r))

    rows = lambda w: pl.BlockSpec((r, w), lambda i: (i, 0))
    return pl.pallas_call(
        body, name="ffn_down_loss", grid=(t // r,),
        in_specs=[rows(D_FF), _const_spec((D_FF, D)), rows(D), rows(D), _const_spec((1, D)), rows(1)],
        out_specs=[rows(D), _acc_spec((1, 128)), _acc_spec((1, D))],
        out_shape=[jax.ShapeDtypeStruct((t, D), F32), jax.ShapeDtypeStruct((1, 128), F32),
                   jax.ShapeDtypeStruct((1, D), F32)],
        compiler_params=_params(1),
    )(act, w_down, h1, target, g_final, row_mask)


def _ffn_bwd(dh2, gate, up, h1, w_down_t, w_gate_t, w_up_t, w_out_t, g_ffn):
    t = h1.shape[0]
    r = _row_tile(t, FFN_BWD_TILE)

    def body(dh2_ref, gate_ref, up_ref, h1_ref, wd_ref, wg_ref, wu_ref, wo_ref, g_ref,
             dgate_ref, dup_ref, dh1_ref, dycat_ref, dg_ref):
        @pl.when(pl.program_id(0) == 0)
        def _():
            dg_ref[...] = jnp.zeros_like(dg_ref)

        dh2 = dh2_ref[...]
        dact = _dot(dh2.astype(BF16), wd_ref[...])
        gate = gate_ref[...]
        sg = _sigmoid(gate)
        dgate = (dact * up_ref[...] * (sg * (1.0 + gate * (1.0 - sg)))).astype(BF16)
        dup = (dact * (gate * sg)).astype(BF16)
        dgate_ref[...] = dgate
        dup_ref[...] = dup
        dhn = _dot(dgate, wg_ref[...]) + _dot(dup, wu_ref[...])
        h1 = h1_ref[...]
        rstd = lax.rsqrt(jnp.mean(h1 * h1, axis=-1, keepdims=True) + RMS_EPS)
        nrm = h1 * rstd
        dg_ref[...] += jnp.sum(dhn * nrm, axis=0, keepdims=True)
        dn = dhn * g_ref[...]
        dh1 = dh2 + rstd * (dn - nrm * jnp.mean(dn * nrm, axis=-1, keepdims=True))
        dh1_ref[...] = dh1
        dycat_ref[...] = _dot(dh1.astype(BF16), wo_ref[...])

    rows = lambda w: pl.BlockSpec((r, w), lambda i: (i, 0))
    return pl.pallas_call(
        body, name="ffn_bwd", grid=(t // r,),
        in_specs=[rows(D), rows(D_FF), rows(D_FF), rows(D), _const_spec((D, D_FF)), _const_spec((D_FF, D)),
                  _const_spec((D_FF, D)), _const_spec((D, D)), _const_spec((1, D))],
        out_specs=[rows(D_FF), rows(D_FF), rows(D), rows(D), _acc_spec((1, D))],
        out_shape=[jax.ShapeDtypeStruct((t, D_FF), BF16), jax.ShapeDtypeStruct((t, D_FF), BF16),
                   jax.ShapeDtypeStruct((t, D), F32), jax.ShapeDtypeStruct((t, D), F32),
                   jax.ShapeDtypeStruct((1, D), F32)],
        compiler_params=_params(1),
    )(dh2, gate, up, h1, w_down_t, w_gate_t, w_up_t, w_out_t, g_ffn)


def _conv_bwd(dycat, yc, u, conv_w, ln_g, ln_b, n_ex, lp, plan=None):
    r = CONV_TILE
    nt = lp // r
    hb = r // CONV_SUB
    nsub = r // CONV_SUB

    def ln_bwd(dy, yc_rows, live, lg, lb):
        mu = jnp.mean(yc_rows, axis=-1, keepdims=True)
        cen = yc_rows - mu
        rs = lax.rsqrt(jnp.mean(cen * cen, axis=-1, keepdims=True) + LN_EPS)
        yn = cen * rs
        out = yn * lg + lb
        so = _sigmoid(out)
        dout = jnp.where(live, dy * (so * (1.0 + out * (1.0 - so))), 0.0)
        dyn = dout * lg
        dyc = rs * (dyn - jnp.mean(dyn, axis=-1, keepdims=True) - yn * jnp.mean(dyn * yn, axis=-1, keepdims=True))
        return dyc, dout, yn

    def body(dy_ref, dyn_ref, yc_ref, ycn_ref, cur_ref, prev_ref, w_ref, lg_ref, lb_ref,
             du_ref, dw_ref, db_ref, dlg_ref, dlb_ref, glu, dycs, dwacc, glu_sh, dycs_sh):
        b = pl.program_id(0)
        i = pl.program_id(1)
        first = jnp.logical_and(b == 0, i == 0)

        @pl.when(first)
        def _():
            dwacc[...] = jnp.zeros_like(dwacc)
            db_ref[...] = jnp.zeros_like(db_ref)
            dlg_ref[...] = jnp.zeros_like(dlg_ref)
            dlb_ref[...] = jnp.zeros_like(dlb_ref)

        lg, lb = lg_ref[...], lb_ref[...]
        cur = cur_ref[...]
        sig = _sigmoid(cur[:, C_CONV:])
        glu[CONV_SUB:CONV_SUB + r, :] = cur[:, :C_CONV] * sig
        pv = prev_ref[...]
        glu[0:CONV_SUB, :] = jnp.where(i > 0, pv[:, :C_CONV] * _sigmoid(pv[:, C_CONV:]), 0.0)

        row = i * r + lax.broadcasted_iota(jnp.int32, (r, 1), 0)
        dyc, dout, yn = ln_bwd(dy_ref[...], yc_ref[...], row >= PAD_ROWS, lg, lb)
        dycs[0:r, :] = dyc
        dycn, _, _ = ln_bwd(dyn_ref[...], ycn_ref[...], i < nt - 1, lg, lb)
        dycs[r:r + CONV_SUB, :] = dycn
        db_ref[...] += jnp.sum(dyc, axis=0, keepdims=True)
        dlg_ref[...] += jnp.sum(dout * yn, axis=0, keepdims=True)
        dlb_ref[...] += jnp.sum(dout, axis=0, keepdims=True)

        _shifted_copies(glu, glu_sh, r)
        _shifted_copies(dycs, dycs_sh, r)
        w = w_ref[...]
        for j in range(nsub):
            r0 = j * CONV_SUB
            dblk = dycs[r0:r0 + CONV_SUB, :]
            dglu = jnp.zeros((CONV_SUB, C_CONV), F32)
            for k in range(CONV_W):
                dglu = dglu + w[k:k + 1, :] * _shifted_rows(dycs, dycs_sh, r0 + (CONV_W - 1) - k)
                prod = dblk * _shifted_rows(glu, glu_sh, r0 + CONV_LEAD + k)
                dwacc[k] += prod.reshape(CONV_SUB // SUBLANES, SUBLANES, C_CONV).sum(axis=0)
            sg = sig[r0:r0 + CONV_SUB, :]
            cv = cur[r0:r0 + CONV_SUB, :C_CONV]
            du_ref[r0:r0 + CONV_SUB, :C_CONV] = (dglu * sg).astype(BF16)
            du_ref[r0:r0 + CONV_SUB, C_CONV:] = (dglu * cv * sg * (1.0 - sg)).astype(BF16)

        @pl.when(jnp.logical_and(b == n_ex - 1, i == nt - 1))
        def _():
            dw_ref[...] = jnp.sum(dwacc[...], axis=1)

    t = n_ex * lp
    cur_rows = lambda w, col: pl.BlockSpec((r, w), lambda b, i: (b * nt + i, col))
    nxt_rows = lambda w, col: pl.BlockSpec(
        (CONV_SUB, w), lambda b, i: (jnp.minimum((b * nt + i + 1) * hb, n_ex * nt * hb - 1), col))
    return _call(
        body, name="conv_bwd", grid=(n_ex, nt),
        in_specs=[cur_rows(C_CONV, 0), nxt_rows(C_CONV, 0), cur_rows(C_CONV, 0), nxt_rows(C_CONV, 0),
                  cur_rows(2 * C_CONV, 0),
                  pl.BlockSpec((CONV_SUB, 2 * C_CONV), lambda b, i: (jnp.maximum((b * nt + i) * hb - 1, 0), 0)),
                  _const_spec((32, C_CONV)), _const_spec((1, C_CONV)), _const_spec((1, C_CONV))],
        out_specs=[cur_rows(2 * C_CONV, 0), _acc_spec((32, C_CONV)), _acc_spec((1, C_CONV)),
                   _acc_spec((1, C_CONV)), _acc_spec((1, C_CONV))],
        out_shape=[jax.ShapeDtypeStruct((t, 2 * C_CONV), BF16), jax.ShapeDtypeStruct((32, C_CONV), F32),
                   jax.ShapeDtypeStruct((1, C_CONV), F32), jax.ShapeDtypeStruct((1, C_CONV), F32),
                   jax.ShapeDtypeStruct((1, C_CONV), F32)],
        scratch_shapes=[pltpu.VMEM((r + CONV_SUB, C_CONV), F32), pltpu.VMEM((r + CONV_SUB, C_CONV), F32),
                        pltpu.VMEM((32, SUBLANES, C_CONV), F32),
                        pltpu.VMEM((SUBLANES - 1, r + CONV_SUB - SUBLANES, C_CONV), F32),
                        pltpu.VMEM((SUBLANES - 1, r + CONV_SUB - SUBLANES, C_CONV), F32)],
        plan=plan,
    )(dycat, dycat, yc, yc, u, u, conv_w, ln_g, ln_b)


HEAD_ROWS_ALL = N_HEADS * CHUNK


def _gla_gates(lr, w2, gb, first_chunk):
    z = _dot(lr.astype(BF16), w2) + gb
    a = (jnp.minimum(z, 0.0) - jnp.log(1.0 + jnp.exp(-jnp.abs(z)))) * (1.0 / GATE_TAU)
    row = lax.broadcasted_iota(jnp.int32, (CHUNK, 1), 0)
    live = jnp.logical_or(jnp.logical_not(first_chunk), row >= PAD_ROWS)
    return z, jnp.where(live, a, 0.0), live


def _tri(lower):
    i = lax.broadcasted_iota(jnp.int32, (CHUNK, CHUNK), 0)
    j = lax.broadcasted_iota(jnp.int32, (CHUNK, CHUNK), 1)
    return (i >= j) if lower else (i <= j)


def _head_of(shape, axis, per_head):
    return lax.broadcasted_iota(jnp.int32, shape, axis) // per_head


def _expand(x, lanes_per_head):
    rows, lanes = HEAD_ROWS_ALL, x.shape[1]
    keep = _head_of((rows, lanes), 0, CHUNK) == _head_of((rows, lanes), 1, lanes_per_head)
    return jnp.where(keep, jnp.tile(x, (N_HEADS, 1)), 0.0)


def _expand_lanes(x):
    rows, w = x.shape
    keep = _head_of((rows, N_HEADS * w), 0, CHUNK) == _head_of((rows, N_HEADS * w), 1, w)
    return jnp.where(keep, jnp.tile(x, (1, N_HEADS)), 0.0)


def _expand_state(st):
    rows, lanes = N_HEADS * DV, st.shape[1]
    keep = _head_of((rows, lanes), 0, DV) == _head_of((rows, lanes), 1, DK)
    return jnp.where(keep, jnp.tile(st, (N_HEADS, 1)), 0.0)


def _fold(t, rows_per_head):
    lane_head = _head_of((rows_per_head, t.shape[1]), 1, DK)
    out = jnp.where(lane_head == 0, t[0:rows_per_head], 0.0)
    for h in range(1, N_HEADS):
        out = out + jnp.where(lane_head == h, t[h * rows_per_head:(h + 1) * rows_per_head], 0.0)
    return out


def _rows_by_head(x):
    return jnp.concatenate([x[:, h * DV:(h + 1) * DV] for h in range(N_HEADS)], axis=0)


def _lanes_by_head(x):
    return jnp.concatenate([x[h * CHUNK:(h + 1) * CHUNK] for h in range(N_HEADS)], axis=1)


def _running_sum(a, lower):
    hi = a.astype(BF16)
    rest = a - hi.astype(F32)
    mid = rest.astype(BF16)
    lo = (rest - mid.astype(F32)).astype(BF16)
    w = a.shape[1]
    parts = _dot(_tri(lower).astype(F32).astype(BF16), jnp.concatenate([hi, mid, lo], axis=1))
    return parts[:, :w] + parts[:, w:2 * w] + parts[:, 2 * w:]


def _stacked_causal():
    i = lax.broadcasted_iota(jnp.int32, (HEAD_ROWS_ALL, CHUNK), 0) % CHUNK
    j = lax.broadcasted_iota(jnp.int32, (HEAD_ROWS_ALL, CHUNK), 1)
    return i >= j


GLA_GROUP = 3


def _gla_chunk(q, k, v, lr, w2, gb, first_chunk):
    z, a, live = _gla_gates(lr, w2, gb, first_chunk)
    yield
    b = _running_sum(a, True)
    yield
    bl = b[CHUNK - 1:CHUNK, :]
    e_pos, e_neg, e_dec = jnp.exp(b), jnp.exp(-b), jnp.exp(bl - b)
    q_f, k_f, kd_f = q * (DK ** -0.5) * e_pos, k * e_neg, k * e_dec
    qx = _expand(q_f, DK).astype(BF16)
    k_in, k_dec, v_b = k_f.astype(BF16), kd_f.astype(BF16), v.astype(BF16)
    s = jnp.where(_stacked_causal(), _dot_nt(qx, k_in), 0.0).astype(BF16)
    yield
    p = _dot(s, v_b)
    yield
    o_intra = jnp.concatenate([p[h * CHUNK:(h + 1) * CHUNK, h * DV:(h + 1) * DV] for h in range(N_HEADS)], axis=0)
    return dict(z=z, live=live, bl=bl, e_pos=e_pos, e_neg=e_neg, e_dec=e_dec, q_f=q_f, k_f=k_f, kd_f=kd_f,
                qx=qx, k_in=k_in, k_dec=k_dec, v_b=v_b, s=s, o_intra=o_intra, decay=jnp.exp(bl))


def _gla_fwd(u, w2, gb, ng, n_ex, lp, plan=None):
    nc = lp // CHUNK
    t = n_ex * lp
    rows_of = lambda j: pl.ds(j * CHUNK, CHUNK)

    def body(qk_ref, v_ref, g_ref, lr_ref, w2_ref, gb_ref, ng_ref, y_ref, st_ref, state):
        n = pl.program_id(0)

        @pl.when(n == 0)
        def _():
            state[...] = jnp.zeros_like(state)

        carried = [state[e] for e in range(n_ex)]

        def one_chunk(e, j):
            rows = rows_of(j)
            qk = qk_ref[e, rows, :]
            first = jnp.logical_and(n == 0, j == 0)
            c = yield from _gla_chunk(qk[:, :GLA_K], qk[:, GLA_K:], v_ref[e, rows, :], lr_ref[e, rows, :],
                                      w2_ref[...], gb_ref[...], first)
            kv = _fold(_dot_tn(c["v_b"], c["k_dec"]), DV)
            g = _rows_by_head(g_ref[e, rows, :])
            gate = ng_ref[...] * (g * _sigmoid(g))
            yield
            for _ in range(j):
                yield
            st = carried[e]
            st_ref[e, pl.ds(j * DV, DV), :] = st
            o = c["o_intra"] + _dot_nt(c["qx"], st.astype(BF16))
            rstd = lax.rsqrt(jnp.mean(o * o, axis=-1, keepdims=True) + RMS_EPS)
            y_ref[e, rows, :] = _lanes_by_head(o * rstd * gate).astype(BF16)
            carried[e] = c["decay"] * st + kv

        _in_lockstep(one_chunk(e, j) for j in range(GLA_GROUP) for e in range(n_ex))
        for e in range(n_ex):
            state[e] = carried[e]

    u3 = u.reshape(n_ex, lp, D_IN_PAD)
    blk = lambda w, col: pl.BlockSpec((n_ex, GLA_GROUP * CHUNK, w), lambda n: (0, n, col))
    (y, states), extra = _call(
        body, name="gla_fwd", grid=(nc // GLA_GROUP,),
        in_specs=[blk(2 * GLA_K, 2), blk(GLA_V, 3), blk(GLA_V, 4), blk(128, 20),
                  _const_spec((128, GLA_K)), _const_spec((1, GLA_K)), _const_spec((1, DV))],
        out_specs=[blk(GLA_V, 0), pl.BlockSpec((n_ex, GLA_GROUP * DV, GLA_K), lambda n: (0, n, 0))],
        out_shape=[jax.ShapeDtypeStruct((n_ex, lp, GLA_V), BF16),
                   jax.ShapeDtypeStruct((n_ex, nc * DV, GLA_K), F32)],
        scratch_shapes=[pltpu.VMEM((n_ex, DV, GLA_K), F32)],
        plan=plan,
    )(u3, u3, u3, u3, w2, gb, ng)
    return (y.reshape(t, GLA_V), states), extra


def _gla_bwd(dycat, u, states, w2, gb, ng, n_ex, lp, plan=None):
    nc = lp // CHUNK
    t = n_ex * lp

    def body(dy_ref, qk_ref, v_ref, g_ref, lr_ref, st_ref, w2_ref, gb_ref, ng_ref,
             du_ref, dw2_ref, dgb_ref, dng_ref, dstate):
        n = pl.program_id(0)
        group = nc // GLA_GROUP - 1 - n

        @pl.when(n == 0)
        def _():
            dw2_ref[...] = jnp.zeros_like(dw2_ref)
            dgb_ref[...] = jnp.zeros_like(dgb_ref)
            dng_ref[...] = jnp.zeros_like(dng_ref)
            dstate[...] = jnp.zeros_like(dstate)

        carried = [dstate[e] for e in range(n_ex)]

        def one_chunk(e, order):
            j = GLA_GROUP - 1 - order
            rows = pl.ds(j * CHUNK, CHUNK)
            qk = qk_ref[e, rows, :]
            lr = lr_ref[e, rows, :]
            st = st_ref[e, pl.ds(j * DV, DV), :]
            first = jnp.logical_and(group == 0, j == 0)
            c = yield from _gla_chunk(qk[:, :GLA_K], qk[:, GLA_K:], v_ref[e, rows, :], lr, w2_ref[...], gb_ref[...],
                                      first)
            qx, k_in, k_dec, v_b, s = c["qx"], c["k_in"], c["k_dec"], c["v_b"], c["s"]
            st_b = st.astype(BF16)
            o = c["o_intra"] + _dot_nt(qx, st_b)
            ngv = ng_ref[...]
            yield
            rstd = lax.rsqrt(jnp.mean(o * o, axis=-1, keepdims=True) + RMS_EPS)
            nrm = o * rstd
            g = _rows_by_head(g_ref[e, rows, :])
            dy = _rows_by_head(dy_ref[e, rows, :])
            sg = _sigmoid(g)
            dg = dy * nrm * ngv * (sg * (1.0 + g * (1.0 - sg)))
            dt = dy * (g * sg)
            dng_ref[...] += jnp.sum(dt * nrm, axis=0, keepdims=True)
            dn = dt * ngv
            do = rstd * (dn - nrm * jnp.mean(dn * nrm, axis=-1, keepdims=True))
            do_b = do.astype(BF16)
            dox = _expand_lanes(do).astype(BF16)
            yield
            da = jnp.where(_stacked_causal(), _dot_nt(dox, v_b), 0.0).astype(BF16)
            dv_intra = _dot_tn(s, dox)
            dst_own = _dot_tn(do_b, qx)
            yield
            dq_in = _fold(_dot(da, k_in) + _dot(do_b, st_b), CHUNK)
            dk_in = _dot_tn(da, qx)
            dq = dq_in * (DK ** -0.5) * c["e_pos"]
            yield
            for _ in range(order):
                yield
            dst = carried[e]
            dstx = _expand_state(dst).astype(BF16)
            dv = dv_intra + _dot_nt(k_dec, dstx)
            dk_dec = _dot(v_b, dstx)
            carried[e] = dst_own + c["decay"] * dst
            yield
            dbl = (jnp.sum(dk_dec * c["kd_f"], axis=0, keepdims=True)
                   + c["decay"] * jnp.sum(dst * st, axis=0, keepdims=True))
            dk = dk_in * c["e_neg"] + dk_dec * c["e_dec"]
            db = dq_in * c["q_f"] - dk_in * c["k_f"] - dk_dec * c["kd_f"]
            row = lax.broadcasted_iota(jnp.int32, (CHUNK, 1), 0)
            da_log = _running_sum(db + jnp.where(row == CHUNK - 1, dbl, 0.0), False)
            yield
            dz = jnp.where(c["live"], da_log * (1.0 - _sigmoid(c["z"])) * (1.0 / GATE_TAU), 0.0)
            dz_b = dz.astype(BF16)
            out = du_ref.at[e, rows, :]
            out[:, 0:GLA_K] = dq.astype(BF16)
            out[:, GLA_K:2 * GLA_K] = dk.astype(BF16)
            out[:, 2 * GLA_K:2 * GLA_K + GLA_V] = dv.astype(BF16)
            out[:, 2 * GLA_K + GLA_V:2 * GLA_K + 2 * GLA_V] = _lanes_by_head(dg).astype(BF16)
            out[:, 2 * GLA_K + 2 * GLA_V:] = _dot_nt(dz_b, w2_ref[...]).astype(BF16)
            dw2_ref[...] += _dot_tn(lr.astype(BF16), dz_b)
            dgb_ref[...] += jnp.sum(dz, axis=0, keepdims=True)

        _in_lockstep(one_chunk(e, order) for order in range(GLA_GROUP) for e in range(n_ex))
        for e in range(n_ex):
            dstate[e] = carried[e]

    u3 = u.reshape(n_ex, lp, D_IN_PAD)
    rev = lambda w, col: pl.BlockSpec((n_ex, GLA_GROUP * CHUNK, w), lambda n: (0, nc // GLA_GROUP - 1 - n, col))
    (du, d_w2, d_gb, d_ng), extra = _call(
        body, name="gla_bwd", grid=(nc // GLA_GROUP,),
        in_specs=[rev(GLA_V, 1), rev(2 * GLA_K, 2), rev(GLA_V, 3), rev(GLA_V, 4), rev(128, 20),
                  pl.BlockSpec((n_ex, GLA_GROUP * DV, GLA_K), lambda n: (0, nc // GLA_GROUP - 1 - n, 0)),
                  _const_spec((128, GLA_K)), _const_spec((1, GLA_K)), _const_spec((1, DV))],
        out_specs=[rev(D_GLA_IN, 0), _acc_spec((128, GLA_K)), _acc_spec((1, GLA_K)), _acc_spec((1, DV))],
        out_shape=[jax.ShapeDtypeStruct((n_ex, lp, D_GLA_IN), BF16), jax.ShapeDtypeStruct((128, GLA_K), F32),
                   jax.ShapeDtypeStruct((1, GLA_K), F32), jax.ShapeDtypeStruct((1, DV), F32)],
        scratch_shapes=[pltpu.VMEM((n_ex, DV, GLA_K), F32)],
        plan=plan,
    )(dycat.reshape(n_ex, lp, D), u3, u3, u3, u3, states, w2, gb, ng)
    return (du.reshape(t, D_GLA_IN), d_w2, d_gb, d_ng), extra


def _in_proj_bwd(du_conv, du_gla, w_in_t_conv, w_in_t_gla, h0, dh1, g_mix, plan=None):
    t = h0.shape[0]
    r = _row_tile(t, 384)

    def body(dc_ref, dg_ref, wc_ref, wg_ref, h_ref, dh1_ref, g_ref, dh0_ref, dgm_ref):
        @pl.when(pl.program_id(0) == 0)
        def _():
            dgm_ref[...] = jnp.zeros_like(dgm_ref)

        dhn = _dot(dc_ref[...], wc_ref[...]) + _dot(dg_ref[...], wg_ref[...])
        h = h_ref[...]
        rstd = lax.rsqrt(jnp.mean(h * h, axis=-1, keepdims=True) + RMS_EPS)
        nrm = h * rstd
        dgm_ref[...] += jnp.sum(dhn * nrm, axis=0, keepdims=True)
        dn = dhn * g_ref[...]
        dh0_ref[...] = dh1_ref[...] + rstd * (dn - nrm * jnp.mean(dn * nrm, axis=-1, keepdims=True))

    rows = lambda w: pl.BlockSpec((r, w), lambda i: (i, 0))
    return _call(
        body, name="in_proj_bwd", grid=(t // r,),
        in_specs=[rows(2 * C_CONV), rows(D_GLA_IN), _const_spec((2 * C_CONV, D)), _const_spec((D_GLA_IN, D)),
                  rows(D), rows(D), _const_spec((1, D))],
        out_specs=[rows(D), _acc_spec((1, D))],
        out_shape=[jax.ShapeDtypeStruct((t, D), F32), jax.ShapeDtypeStruct((1, D), F32)],
        plan=plan,
    )(du_conv, du_gla, w_in_t_conv, w_in_t_gla, h0, dh1, g_mix)


def _wgrad(x, dy, name, plan=None):
    t, m = x.shape
    n = dy.shape[1]
    tk = t // 3 if t % (3 * 128) == 0 else _row_tile(t, 384)
    tm = m if m <= D_GLA_IN else m // 2

    def body(x_ref, dy_ref, o_ref):
        @pl.when(pl.program_id(1) == 0)
        def _():
            o_ref[...] = jnp.zeros_like(o_ref)

        o_ref[...] += _dot_tn(x_ref[...].astype(BF16), dy_ref[...].astype(BF16))

    (out,), extra = _call(
        body, name=name, grid=(m // tm, t // tk),
        in_specs=[pl.BlockSpec((tk, tm), lambda i, k: (k, i)), pl.BlockSpec((tk, n), lambda i, k: (k, 0))],
        out_specs=[pl.BlockSpec((tm, n), lambda i, k: (i, 0))],
        out_shape=[jax.ShapeDtypeStruct((m, n), F32)],
        plan=plan,
    )(x, dy)
    return out, extra


def _wgrad_pair(xa, xb, dy, name):
    t, m = xa.shape
    n = dy.shape[1]
    tk = t // 3 if t % (3 * 128) == 0 else _row_tile(t, 384)

    def body(xa_ref, xb_ref, dy_ref, o_ref):
        @pl.when(pl.program_id(1) == 0)
        def _():
            o_ref[...] = jnp.zeros_like(o_ref)

        x = jnp.where(pl.program_id(0) == 0, xa_ref[...], xb_ref[...])
        o_ref[...] += _dot_tn(x.astype(BF16), dy_ref[...].astype(BF16))

    rows = lambda w: pl.BlockSpec((tk, w), lambda i, k: (k, 0))
    return pl.pallas_call(
        body, name=name, grid=(2, t // tk), in_specs=[rows(m), rows(m), rows(n)],
        out_specs=pl.BlockSpec((m, n), lambda i, k: (i, 0)),
        out_shape=jax.ShapeDtypeStruct((2 * m, n), F32), compiler_params=_params(2),
    )(xa, xb, dy)


def _adam_update(g, w, m, v):
    m2 = ADAM_B1 * m + (1.0 - ADAM_B1) * g
    v2 = ADAM_B2 * v + (1.0 - ADAM_B2) * (g * g)
    m_hat = m2 / (1.0 - ADAM_B1 ** ADAM_STEP)
    v_hat = v2 / (1.0 - ADAM_B2 ** ADAM_STEP)
    delta = -ADAM_LR * (m_hat / (jnp.sqrt(v_hat) + ADAM_EPS) + ADAM_WD * w)
    return delta, m2, v2


ADAMW_STEPS = 4


def _adamw(g, w, m, v, name):
    rows, cols = g.shape
    steps = ADAMW_STEPS if rows % (ADAMW_STEPS * SUBLANES) == 0 else 1

    def body(g_ref, w_ref, m_ref, v_ref, d_ref, m2_ref, v2_ref):
        d_ref[...], m2_ref[...], v2_ref[...] = _adam_update(g_ref[...], w_ref[...], m_ref[...], v_ref[...])

    spec = pl.BlockSpec((rows // steps, cols), lambda i: (i, 0))
    return pl.pallas_call(
        body, name=name, grid=(steps,), in_specs=[spec] * 4, out_specs=[spec] * 3,
        out_shape=[jax.ShapeDtypeStruct(g.shape, F32)] * 3, compiler_params=_params(1),
    )(g, w, m, v)


def _adamw_halves(items, c, name):
    n = len(items)
    h = items[0][0].shape[-1]
    splits = lambda a: a.shape[0] % (ADAMW_STEPS * (SUBLANES if a.ndim == 2 else 1)) == 0
    steps = ADAMW_STEPS if all(splits(it[0]) for it in items) else 1

    def body(c_ref, *refs):
        ins, outs = refs[:5 * n], refs[5 * n:]
        own = pl.program_id(1) == c_ref[0]
        for i in range(n):
            a_ref, b_ref, w_ref, m_ref, v_ref = ins[5 * i:5 * i + 5]
            go_ref, d_ref, m2_ref, v2_ref = outs[4 * i:4 * i + 4]
            g = jnp.where(own, a_ref[...], b_ref[...])
            go_ref[...] = g
            d_ref[...], m2_ref[...], v2_ref[...] = _adam_update(g, w_ref[...], m_ref[...], v_ref[...])

    in_specs, out_specs, out_shape, args = [pl.BlockSpec(memory_space=pltpu.SMEM)], [], [], []
    for mine, theirs, w, m, v in items:
        tr = mine.shape[0] // steps
        mid = (0,) * (mine.ndim - 2)
        half = pl.BlockSpec((tr,) + mine.shape[1:-1] + (h,), lambda i, j, mid=mid: (i, *mid, 0))
        full = pl.BlockSpec((tr,) + mine.shape[1:-1] + (h,), lambda i, j, mid=mid: (i, *mid, j))
        in_specs += [half, half, full, full, full]
        out_specs += [full] * 4
        out_shape += [jax.ShapeDtypeStruct(w.shape, F32)] * 4
        args += [mine, theirs, w, m, v]
    res = pl.pallas_call(
        body, name=name, grid=(steps, 2), in_specs=in_specs, out_specs=out_specs, out_shape=out_shape,
        compiler_params=_params(2),
    )(jnp.reshape(c, (1,)).astype(jnp.int32), *args)
    return [res[4 * i:4 * i + 4] for i in range(n)]


def _rs_add_halves(pairs, c, name):
    blocks = pairs[0][0].shape[0]
    n = len(pairs)

    def body(c_ref, *refs):
        for i in range(n):
            refs[2 * n + i][...] = (refs[2 * i][...] + refs[2 * i + 1][...]).astype(BF16)

    in_specs, out_specs, out_shape = [], [], []
    for g, _ in pairs:
        _, rows, w = g.shape
        in_specs += [pl.BlockSpec((1, rows, w // 2), lambda j, s: (j, 0, s[0])),
                     pl.BlockSpec((1, rows, w // 2), lambda j, s: (j, 0, 0))]
        out_specs += [pl.BlockSpec((1, rows, w // 2), lambda j, s: (j, 0, 0))]
        out_shape += [jax.ShapeDtypeStruct((blocks, rows, w // 2), BF16)]
    return pl.pallas_call(
        body, name=name,
        grid_spec=pltpu.PrefetchScalarGridSpec(num_scalar_prefetch=1, grid=(blocks,), in_specs=in_specs,
                                               out_specs=out_specs),
        out_shape=out_shape, compiler_params=_params(1),
    )(jnp.reshape(c, (1,)).astype(jnp.int32), *[a for pair in pairs for a in pair])


def _rs_sum(pairs, mine, name):
    n = len(pairs)
    steps = 2 if all(own.shape[1] % (2 * 16) == 0 for own, _ in pairs) else 1

    def body(mine_ref, *refs):
        for i in range(n):
            p = refs[2 * i + 1][...].astype(F32)
            refs[2 * n + i][...] = ((refs[2 * i][0].astype(F32) + p[0]) + p[1]) + p[2]

    in_specs, out_specs, out_shape = [], [], []
    for own, _ in pairs:
        _, rows, h = own.shape
        tr = rows // steps
        in_specs += [pl.BlockSpec((1, tr, h), lambda i, s: (s[0], i, 0)),
                     pl.BlockSpec((3, tr, h), lambda i, s: (0, i, 0))]
        out_specs += [pl.BlockSpec((tr, h), lambda i, s: (i, 0))]
        out_shape += [jax.ShapeDtypeStruct((rows, h), F32)]
    return pl.pallas_call(
        body, name=name,
        grid_spec=pltpu.PrefetchScalarGridSpec(num_scalar_prefetch=1, grid=(steps,), in_specs=in_specs,
                                               out_specs=out_specs),
        out_shape=out_shape, compiler_params=_params(1),
    )(jnp.reshape(mine, (1,)).astype(jnp.int32), *[a for pair in pairs for a in pair])


def _sum_slots_adamw(slots, late_slots, vectors):
    late_rows = late_slots.shape[1]
    n = len(SMALL_PARTS)

    def body(s_ref, l_ref, *refs):
        ins, g_ref, outs = refs[:3 * n], refs[3 * n], refs[3 * n + 1:]
        g, late = s_ref[0], l_ref[0]
        for d in range(1, 8):
            g = g + s_ref[d]
            late = late + l_ref[d]
        g = jnp.concatenate([g[:late_rows] + late, g[late_rows:]], axis=0)
        g_ref[...] = g
        for i, (_, row, col, size) in enumerate(SMALL_PARTS):
            w_ref, m_ref, v_ref = ins[3 * i:3 * i + 3]
            go_ref, d_ref, m2_ref, v2_ref = outs[4 * i:4 * i + 4]
            piece = g[row:row + 1, col:col + size]
            go_ref[...] = piece
            d_ref[...], m2_ref[...], v2_ref[...] = _adam_update(piece, w_ref[...], m_ref[...], v_ref[...])

    vm = pl.BlockSpec(memory_space=pltpu.VMEM)
    out_shape = [jax.ShapeDtypeStruct(slots.shape[1:], F32)]
    for _, _, _, size in SMALL_PARTS:
        out_shape += [jax.ShapeDtypeStruct((1, size), F32)] * 4
    res = pl.pallas_call(body, name="small_sum_adamw", in_specs=[vm] * (2 + 3 * n), out_specs=[vm] * len(out_shape),
                         out_shape=out_shape)(slots, late_slots, *[a for wmv in vectors for a in wmv])
    return res[0], [res[1 + 4 * i:5 + 4 * i] for i in range(n)]


def _mesh_pos():
    return lax.axis_index("x"), lax.axis_index("y"), lax.axis_index("c")


def _other_chips(x, y):
    return [(1 - x, y), (x, 1 - y), (1 - x, 1 - y)]


def _half(ref, c, axis):
    n = ref.shape[axis] // 2
    return ref.at[(slice(None),) * axis + (pl.ds(c * n, n),)]


def _remote(src, dst, send_sem, recv_sem, device):
    return pltpu.make_async_remote_copy(src_ref=src, dst_ref=dst, send_sem=send_sem, recv_sem=recv_sem,
                                        device_id=device, device_id_type=MESH)


def _gather_plan(split, whole=(), axes=None):
    split, whole = list(split), list(whole)
    ns, n = len(split), len(split) + len(whole)

    def make(ins, outs, sems):
        ici_send, ici_recv, d2d_send, d2d_recv, own_send, own_recv = sems
        x, y, c = _mesh_pos()
        mine = 2 * x + y
        chips = _other_chips(x, y)
        blocks = [2 * px + py for px, py in chips]

        def own(a):
            return _remote(ins[a], outs[a].at[mine], own_send.at[a], own_recv.at[a], (x, y, 1 - c))

        def ici(a, k, block):
            px, py = chips[k]
            src, dst = ins[a], outs[a].at[block]
            if a < ns:
                src, dst = _half(src, c, axes[a]), _half(dst, c, axes[a])
            return _remote(src, dst, ici_send.at[3 * a + k], ici_recv.at[3 * a + k], (px, py, c))

        def d2d(a, k, half):
            part = _half(outs[a].at[blocks[k]], half, axes[a])
            return _remote(part, part, d2d_send.at[3 * a + k], d2d_recv.at[3 * a + k], (x, y, 1 - c))

        def start():
            for a in range(n):
                for k in range(3):
                    ici(a, k, mine).start()
                own(a).start()

        def relay():
            for a in range(n):
                for k in range(3):
                    ici(a, k, blocks[k]).wait_recv()
                    if a < ns:
                        d2d(a, k, c).start()

        def finish():
            for a in range(ns):
                for k in range(3):
                    d2d(a, k, 1 - c).wait_recv()
            for a in range(n):
                for k in range(3):
                    ici(a, k, mine).wait_send()
                    if a < ns:
                        d2d(a, k, c).wait_send()
                own(a).wait()

        return start, relay, finish

    arrays = split + whole
    axes = [0] * ns if axes is None else list(axes)
    return _Plan(arrays, [jax.ShapeDtypeStruct((N_CHIPS,) + s.shape, s.dtype) for s in arrays],
                 [pltpu.SemaphoreType.DMA((3 * n,)), pltpu.SemaphoreType.DMA((3 * n,)),
                  pltpu.SemaphoreType.DMA((3 * ns,)), pltpu.SemaphoreType.DMA((3 * ns,)),
                  pltpu.SemaphoreType.DMA((n,)), pltpu.SemaphoreType.DMA((n,))], make)


def _to_sibling_plan(gs):
    n = len(gs)

    def make(ins, outs, sems):
        send_sems, recv_sems = sems
        x, y, c = _mesh_pos()

        def copy(a):
            return _remote(_half(ins[a], 1 - c, len(ins[a].shape) - 1), outs[a], send_sems.at[a],
                           recv_sems.at[a], (x, y, 1 - c))

        def start():
            for a in range(n):
                copy(a).start()

        def finish():
            for a in range(n):
                copy(a).wait()

        return start, finish

    return _Plan(list(gs), [jax.ShapeDtypeStruct(g.shape[:-1] + (g.shape[-1] // 2,), g.dtype) for g in gs],
                 [pltpu.SemaphoreType.DMA((n,)), pltpu.SemaphoreType.DMA((n,))], make)


def _chip_exchange_plan(ps):
    n = len(ps)

    def make(ins, outs, sems):
        send_sems, recv_sems = sems
        x, y, c = _mesh_pos()
        chips = _other_chips(x, y)

        def ici(a, k):
            px, py = chips[k]
            return _remote(ins[a].at[2 * px + py], outs[a].at[k], send_sems.at[3 * a + k],
                           recv_sems.at[3 * a + k], (px, py, c))

        def start():
            for a in range(n):
                for k in range(3):
                    ici(a, k).start()

        def finish():
            for a in range(n):
                for k in range(3):
                    ici(a, k).wait()

        return start, finish

    return _Plan(list(ps), [jax.ShapeDtypeStruct((3,) + p.shape[1:], p.dtype) for p in ps],
                 [pltpu.SemaphoreType.DMA((3 * n,)), pltpu.SemaphoreType.DMA((3 * n,))], make)


def _share_plan(halves):
    n = len(halves)

    def make(ins, outs, sems):
        send_sems, recv_sems = sems
        x, y, c = _mesh_pos()

        def d2d(a):
            return _remote(ins[a], outs[a], send_sems.at[a], recv_sems.at[a], (x, y, 1 - c))

        def start():
            for a in range(n):
                d2d(a).start()

        def finish():
            for a in range(n):
                d2d(a).wait()

        return start, finish

    return _Plan(list(halves), [jax.ShapeDtypeStruct(p.shape, p.dtype) for p in halves],
                 [pltpu.SemaphoreType.DMA((n,)), pltpu.SemaphoreType.DMA((n,))], make)


def _all_to_all_plan(part):
    def make(ins, outs, sems):
        send_sems, recv_sems, local_sem = sems
        (p_ref,), (slots,) = ins, outs
        x, y, c = _mesh_pos()
        me = 4 * x + 2 * y + c
        peers = [(px, py, pc) for px in (x, 1 - x) for py in (y, 1 - y) for pc in (c, 1 - c)][1:]

        def remote(k, slot):
            return _remote(p_ref, slots.at[slot], send_sems.at[k], recv_sems.at[k], peers[k])

        def local():
            return pltpu.make_async_copy(p_ref, slots.at[me], local_sem)

        def start():
            for k in range(7):
                remote(k, me).start()
            local().start()

        def finish():
            for k, (px, py, pc) in enumerate(peers):
                remote(k, 4 * px + 2 * py + pc).wait_recv()
            for k in range(7):
                remote(k, me).wait_send()
            local().wait()

        return start, finish

    return _Plan([part], [jax.ShapeDtypeStruct((8,) + part.shape, part.dtype)],
                 [pltpu.SemaphoreType.DMA((7,)), pltpu.SemaphoreType.DMA((7,)), pltpu.SemaphoreType.DMA(())], make)


def _merge_plans(a, b):
    na_in, na_out, na_sems = len(a.arrays), len(a.out_shape), len(a.sems)

    def make(ins, outs, sems):
        phases_a = _phases(a.make(ins[:na_in], outs[:na_out], sems[:na_sems]))
        phases_b = _phases(b.make(ins[na_in:], outs[na_out:], sems[na_sems:]))

        def both(i):
            def run():
                phases_a[i]()
                phases_b[i]()
            return run

        return both(0), both(1), both(2)

    return _Plan(list(a.arrays) + list(b.arrays), list(a.out_shape) + list(b.out_shape),
                 list(a.sems) + list(b.sems), make)


def _exchange(plan, name):
    n_in, n_out = len(plan.arrays), len(plan.out_shape)

    def body(*refs):
        for phase in _phases(plan.make(refs[:n_in], refs[n_in:n_in + n_out], refs[n_in + n_out:])):
            phase()

    return pl.pallas_call(
        body, name=name, in_specs=[HBM_SPEC] * n_in, out_specs=[HBM_SPEC] * n_out, out_shape=list(plan.out_shape),
        scratch_shapes=list(plan.sems), compiler_params=pltpu.CompilerParams(has_side_effects=True),
    )(*plan.arrays)


def _pack_small(parts):
    rows = []
    for r in range(SMALL_ROWS):
        pieces, col = [], 0
        for name, row, start, size in SMALL_PARTS:
            if row == r:
                assert start == col
                pieces.append(parts[name].reshape(1, size).astype(F32))
                col += size
        rows.append(jnp.concatenate(pieces + [jnp.zeros((1, D - col), F32)], axis=1))
    return jnp.concatenate(rows, axis=0)


def _columns(gathered):
    return jnp.concatenate([gathered[j] for j in range(N_CHIPS)], axis=1)


def kernel(x, meta_tokens, norm_mix_g, w_in, conv_w, conv_b, conv_ln_g, conv_ln_b, gla_w_gate2, gla_gate_b, gla_norm_g, w_out, norm_ffn_g, w_ffn_gate, w_ffn_up, w_ffn_down, norm_final_g, loss_target, m_meta_tokens, m_norm_mix_g, m_w_in, m_conv_w, m_conv_b, m_conv_ln_g, m_conv_ln_b, m_gla_w_gate2, m_gla_gate_b, m_gla_norm_g, m_w_out, m_norm_ffn_g, m_w_ffn_gate, m_w_ffn_up, m_w_ffn_down, m_norm_final_g, v_meta_tokens, v_norm_mix_g, v_w_in, v_conv_w, v_conv_b, v_conv_ln_g, v_conv_ln_b, v_gla_w_gate2, v_gla_gate_b, v_gla_norm_g, v_w_out, v_norm_ffn_g, v_w_ffn_gate, v_w_ffn_up, v_w_ffn_down, v_norm_final_g):
    ws = dict(zip(WEIGHT_NAMES, (meta_tokens, norm_mix_g, w_in, conv_w, conv_b, conv_ln_g, conv_ln_b, gla_w_gate2,
                                 gla_gate_b, gla_norm_g, w_out, norm_ffn_g, w_ffn_gate, w_ffn_up, w_ffn_down,
                                 norm_final_g)))
    ms = dict(zip(WEIGHT_NAMES, (m_meta_tokens, m_norm_mix_g, m_w_in, m_conv_w, m_conv_b, m_conv_ln_g, m_conv_ln_b,
                                 m_gla_w_gate2, m_gla_gate_b, m_gla_norm_g, m_w_out, m_norm_ffn_g, m_w_ffn_gate,
                                 m_w_ffn_up, m_w_ffn_down, m_norm_final_g)))
    vs = dict(zip(WEIGHT_NAMES, (v_meta_tokens, v_norm_mix_g, v_w_in, v_conv_w, v_conv_b, v_conv_ln_g, v_conv_ln_b,
                                 v_gla_w_gate2, v_gla_gate_b, v_gla_norm_g, v_w_out, v_norm_ffn_g, v_w_ffn_gate,
                                 v_w_ffn_up, v_w_ffn_down, v_norm_final_g)))
    c = lax.axis_index("c")
    mine = 2 * lax.axis_index("x") + lax.axis_index("y")
    shard = lambda d, name: d[name].reshape(d[name].shape[-2:])
    vec = {name: ws[name].reshape(1, -1) for name, _, _, _ in SMALL_PARTS}
    n_ex, seq, _ = x.shape
    lp = HEAD_ROWS + seq
    t = n_ex * lp

    (tgt, h0, gate_s, up_s, out_s, down_s), (w_in_g, meta_g, conv_w_g, w2_g) = _pad_head_rows(
        [loss_target, x],
        [shard(ws, "w_ffn_gate").T, shard(ws, "w_ffn_up").T, shard(ws, "w_out"), shard(ws, "w_ffn_down")],
        plan=_gather_plan([shard(ws, "w_in").T.astype(BF16)],
                          [shard(ws, "meta_tokens"), shard(ws, "conv_w"), shard(ws, "gla_w_gate2")], axes=[1]))
    w_in_t = jnp.concatenate([w_in_g.reshape(D_IN, D), jnp.zeros((D_IN_PAD - D_IN, D), BF16)], axis=0)
    conv_w_full = jnp.concatenate([_columns(conv_w_g), jnp.zeros((32 - CONV_W, C_CONV), F32)], axis=0)
    w2_full = jnp.concatenate([_columns(w2_g), jnp.zeros((128 - RANK, GLA_K), F32)], axis=0).astype(BF16)
    h0 = _set_meta_rows(h0, _columns(meta_g)).reshape(t, D)
    tgt = tgt.reshape(t, D)
    row_mask = jnp.concatenate([jnp.zeros((n_ex, HEAD_ROWS, 1), F32), jnp.ones((n_ex, seq, 1), F32)],
                               axis=1).reshape(t, 1)

    (u, hn), (gate_g,) = _in_proj(h0, vec["norm_mix_g"], w_in_t, plan=_gather_plan([gate_s]))
    (yc, y_conv), (up_g, w_out_g) = _conv_fwd(
        u, conv_w_full, vec["conv_b"], vec["conv_ln_g"], vec["conv_ln_b"], n_ex, lp,
        plan=_gather_plan([up_s, out_s]))
    (y_gla, states), _ = _gla_fwd(u, w2_full, vec["gla_gate_b"], vec["gla_norm_g"], n_ex, lp)
    w_out_full = w_out_g.reshape(D, D)
    w_gate_t, w_up_t = gate_g.reshape(D_FF, D), up_g.reshape(D_FF, D)
    (h1, hn2, gate, up, act), (down_g,) = _mix_out_ffn_up(
        h0, y_conv, y_gla, w_out_full, vec["norm_ffn_g"], w_gate_t, w_up_t,
        plan=_gather_plan([down_s]))
    w_down_full = down_g.reshape(D_FF, D)
    dh2, loss, d_final_g = _ffn_down_loss(act, w_down_full, h1, tgt, vec["norm_final_g"], row_mask)
    dgate, dup, dh1, dycat, d_ffn_g = _ffn_bwd(dh2, gate, up, h1, w_down_full.T, w_gate_t, w_up_t, w_out_full.T,
                                                vec["norm_ffn_g"])

    ffn_block = lambda g: g.reshape(N_CHIPS, D_FF // N_CHIPS, D)
    g_gate = ffn_block(_wgrad(dgate, hn2, "wgrad_gate")[0])
    g_up, (gate_sib,) = _wgrad(dup, hn2, "wgrad_up", _to_sibling_plan([g_gate]))
    g_up = ffn_block(g_up)
    g_down, (up_sib,) = _wgrad(act, dh2, "wgrad_down", _to_sibling_plan([g_up]))
    g_down = ffn_block(g_down)
    g_out = _wgrad_pair(y_conv, y_gla, dh1, "wgrad_out").reshape(N_CHIPS, D // N_CHIPS, D)
    cs_gate, cs_up = _rs_add_halves([(g_gate, gate_sib), (g_up, up_sib)], c, "rs_add_gate_up")
    (du_conv, d_conv_w, d_conv_b, d_ln_g, d_ln_b), (ex_gate, ex_up, down_sib, out_sib) = _conv_bwd(
        dycat, yc, u, conv_w_full, vec["conv_ln_g"], vec["conv_ln_b"], n_ex, lp,
        plan=_merge_plans(_chip_exchange_plan([cs_gate, cs_up]), _to_sibling_plan([g_down, g_out])))
    cs_down, cs_out = _rs_add_halves([(g_down, down_sib), (g_out, out_sib)], c, "rs_add_down_out")
    (du_gla, d_w2, d_gate_b, d_norm_g), (ex_down, ex_out) = _gla_bwd(
        dycat, u, states, w2_full, vec["gla_gate_b"], vec["gla_norm_g"], n_ex, lp,
        plan=_chip_exchange_plan([cs_down, cs_out]))
    halves = _rs_sum([(cs_gate, ex_gate), (cs_up, ex_up), (cs_down, ex_down), (cs_out, ex_out)], mine,
                     "rs_sum_early")

    small = {"norm_mix_g": jnp.zeros((1, D), F32), "norm_ffn_g": d_ffn_g, "norm_final_g": d_final_g,
             "conv_b": d_conv_b, "conv_ln_g": d_ln_g, "conv_ln_b": d_ln_b, "gla_gate_b": d_gate_b,
             "gla_norm_g": d_norm_g}
    part = lax.dynamic_update_slice(_pack_small(small), loss[:, :1], (LOSS_ROW, 0))
    part = jnp.concatenate([part, jnp.zeros((N_META, D), F32), d_conv_w.reshape(16, D), d_w2[:RANK].reshape(4, D),
                            jnp.zeros((4, D), F32)], axis=0)
    g_in_conv = _wgrad(du_conv, hn, "wgrad_in_conv")[0][None]
    g_in_gla, (slots, conv_sib) = _wgrad(du_gla, hn, "wgrad_in_gla",
                                         _merge_plans(_all_to_all_plan(part), _to_sibling_plan([g_in_conv])))
    pieces = [g_in_conv, g_in_gla[None]]
    (gla_sib,) = _exchange(_to_sibling_plan(pieces[1:]), "rs_late_to_sibling")
    sums = _rs_add_halves(list(zip(pieces, (conv_sib, gla_sib))), c, "rs_add_w_in")
    in_chip_sum = jnp.concatenate([sums[0][0], sums[1][0]], axis=0)[:D_IN].reshape(N_CHIPS, D_IN // N_CHIPS, D // 2)
    (dh0, d_mix_g), shared = _in_proj_bwd(
        du_conv, du_gla, w_in_t[:2 * C_CONV], w_in_t[2 * C_CONV:], h0, dh1, vec["norm_mix_g"],
        plan=_merge_plans(_share_plan(halves), _chip_exchange_plan([in_chip_sum])))
    dh0 = dh0.reshape(n_ex, lp, D)
    grad_x = dh0[:, HEAD_ROWS:]
    late_part = jnp.concatenate([d_mix_g, jnp.zeros((SMALL_ROWS - 1, D), F32),
                                 jnp.sum(dh0[:, PAD_ROWS:HEAD_ROWS], axis=0)], axis=0)
    (in_half,) = _rs_sum([(in_chip_sum, shared[4])], mine, "rs_sum_w_in")
    in_shared, late_slots = _exchange(_merge_plans(_share_plan([in_half]), _all_to_all_plan(late_part)),
                                      "late_exchange")

    out = {"grad": {}, "delta": {}, "new_m": {}, "new_v": {}}

    def record(name, res, transposed=False):
        for kind, a in zip(("grad", "delta", "new_m", "new_v"), res):
            out[kind][name] = (a.T if transposed else a).reshape(ws[name].shape)

    def operands(name, transposed):
        lay = (lambda a: a.T) if transposed else (lambda a: a)
        return lay(shard(ws, name)), lay(shard(ms, name)), lay(shard(vs, name))

    early_layout = (("w_ffn_gate", True), ("w_ffn_up", True), ("w_ffn_down", False), ("w_out", False))
    items = [(mine_half, their_half, *operands(name, transposed))
             for (name, transposed), mine_half, their_half in zip(early_layout, halves, shared)]
    for (name, transposed), res in zip(early_layout, _adamw_halves(items, c, "adamw_early")):
        record(name, res, transposed)

    tile_rows = lambda a: a.reshape(a.shape[0], 1, a.shape[1])
    by_output = lambda d: jnp.transpose(d["w_in"], (2, 0, 1))
    res = _adamw_halves([(tile_rows(in_half), tile_rows(in_shared), by_output(ws), by_output(ms), by_output(vs))],
                        c, "adamw_w_in")[0]
    for kind, a in zip(("grad", "delta", "new_m", "new_v"), res):
        out[kind]["w_in"] = jnp.transpose(a, (1, 2, 0))

    flat = lambda d, name: d[name].reshape(1, -1)
    g_s, updated = _sum_slots_adamw(slots, late_slots,
                                    [(flat(ws, name), flat(ms, name), flat(vs, name)) for name, _, _, _ in SMALL_PARTS])
    for (name, _, _, _), res in zip(SMALL_PARTS, updated):
        record(name, res)
    loss = g_s[LOSS_ROW, 0]
    block = lambda a, width: lax.dynamic_slice_in_dim(a, mine * width, width, axis=1)
    small_sharded = {"meta_tokens": block(g_s[8:24], D // N_CHIPS),
                     "conv_w": block(g_s[24:40].reshape(32, C_CONV), C_CONV // N_CHIPS)[:CONV_W],
                     "gla_w_gate2": block(g_s[40:44].reshape(RANK, GLA_K), GLA_K // N_CHIPS)}
    for name, g in small_sharded.items():
        record(name, [g, *_adamw(g, *operands(name, False), "adamw_" + name)])

    return (loss, grad_x, *[out[kind][name] for kind in ("grad", "delta", "new_m", "new_v") for name in WEIGHT_NAMES])
```

```python
import functools
from typing import Any, Callable, NamedTuple, Sequence

import jax
import jax.numpy as jnp
from jax import lax
from jax.experimental import pallas as pl
from jax.experimental.pallas import tpu as pltpu

F32 = jnp.float32
BF16 = jnp.bfloat16
MESH = pl.DeviceIdType.MESH

D = 1024
N_META = 16
C_CONV = 512
CONV_W = 31
GLA_K = 256
GLA_V = 512
N_HEADS = 4
DK = 64
DV = 128
RANK = 16
CHUNK = 64
PAD_ROWS = CHUNK - N_META
HEAD_ROWS = CHUNK
D_IN = 2576
D_IN_PAD = 2688
D_GLA_IN = D_IN_PAD - 2 * C_CONV
D_FF = 2816
RMS_EPS = 1e-6
LN_EPS = 1e-5
GATE_TAU = 16.0
N_CHIPS = 4

ADAM_LR = 0.001
ADAM_B1 = 0.9
ADAM_B2 = 0.999
ADAM_EPS = 1e-08
ADAM_WD = 0.01
ADAM_STEP = 10

V7X_VMEM_BYTES = 64 * 1024 * 1024
VMEM_LIMIT = V7X_VMEM_BYTES - 8 * 1024 * 1024
SUBLANES = 8
ROW_PART = 128
FFN_BWD_TILE = 192

WEIGHT_NAMES = ("meta_tokens", "norm_mix_g", "w_in", "conv_w", "conv_b", "conv_ln_g", "conv_ln_b", "gla_w_gate2",
                "gla_gate_b", "gla_norm_g", "w_out", "norm_ffn_g", "w_ffn_gate", "w_ffn_up", "w_ffn_down",
                "norm_final_g")

SMALL_ROWS = 8
SMALL_PARTS = (("norm_mix_g", 0, 0, D), ("norm_ffn_g", 1, 0, D), ("norm_final_g", 2, 0, D),
               ("conv_b", 3, 0, C_CONV), ("conv_ln_g", 3, C_CONV, C_CONV), ("conv_ln_b", 4, 0, C_CONV),
               ("gla_gate_b", 4, C_CONV, GLA_K), ("gla_norm_g", 4, C_CONV + GLA_K, DV))
LOSS_ROW = 5

HBM_SPEC = pl.BlockSpec(memory_space=pltpu.HBM)


def _dot(a, b):
    return jnp.dot(a, b, preferred_element_type=F32)


def _dot_nt(a, b):
    return lax.dot_general(a, b, (((1,), (1,)), ((), ())), preferred_element_type=F32)


def _dot_tn(a, b):
    return lax.dot_general(a, b, (((0,), (0,)), ((), ())), preferred_element_type=F32)


def _sigmoid(x):
    return 1.0 / (1.0 + jnp.exp(-x))


def _const_spec(shape):
    return pl.BlockSpec(shape, lambda *_: (0,) * len(shape), pipeline_mode=pl.Buffered(1))


def _acc_spec(shape):
    return pl.BlockSpec(shape, lambda *_: (0,) * len(shape))


def _params(n_axes):
    return pltpu.CompilerParams(dimension_semantics=("arbitrary",) * n_axes, vmem_limit_bytes=VMEM_LIMIT)


def _row_tile(t, want):
    for r in (want, 384, 192, 128, 64):
        if r <= want and t % r == 0:
            return r
    raise ValueError(f"no row tile for {t}")


def _row_parts(r):
    if r % ROW_PART:
        return [slice(None)]
    return [pl.ds(i * ROW_PART, ROW_PART) for i in range(r // ROW_PART)]


def _in_lockstep(bodies):
    live = list(bodies)
    while live:
        still = []
        for g in live:
            try:
                next(g)
                still.append(g)
            except StopIteration:
                pass
        live = still


class _Plan(NamedTuple):
    arrays: Sequence[Any]
    out_shape: Sequence[Any]
    sems: Sequence[Any]
    make: Callable


def _phases(made):
    return made if len(made) == 3 else (made[0], lambda: None, made[1])


def _call(body, *, name, grid, in_specs, out_specs, out_shape, scratch_shapes=(), plan=None):
    n_in, n_out, n_scr = len(in_specs), len(out_specs), len(scratch_shapes)
    if plan is None:
        plan = _Plan([], [], [], lambda ins, outs, sems: (lambda: None, lambda: None))
    nx_in, nx_out = len(plan.arrays), len(plan.out_shape)
    n_steps = functools.reduce(lambda a, b: a * b, grid)

    def hosted(*refs):
        ins, xins = refs[:n_in], refs[n_in:n_in + nx_in]
        o0 = n_in + nx_in
        outs, xouts = refs[o0:o0 + n_out], refs[o0 + n_out:o0 + n_out + nx_out]
        s0 = o0 + n_out + nx_out
        scr, sems = refs[s0:s0 + n_scr], refs[s0 + n_scr:]
        step = functools.reduce(lambda acc, a: acc * grid[a] + pl.program_id(a), range(len(grid)), 0)
        start, relay, finish = _phases(plan.make(xins, xouts, sems))
        pl.when(step == 0)(start)
        pl.when(step == n_steps - 1)(relay)
        body(*ins, *outs, *scr)
        pl.when(step == n_steps - 1)(finish)

    call = pl.pallas_call(
        hosted, name=name, grid=grid, in_specs=list(in_specs) + [HBM_SPEC] * nx_in,
        out_specs=list(out_specs) + [HBM_SPEC] * nx_out, out_shape=list(out_shape) + list(plan.out_shape),
        scratch_shapes=list(scratch_shapes) + list(plan.sems),
        compiler_params=pltpu.CompilerParams(dimension_semantics=("arbitrary",) * len(grid),
                                             vmem_limit_bytes=VMEM_LIMIT, has_side_effects=nx_in > 0))

    def run(*args):
        res = call(*args, *plan.arrays)
        return res[:n_out], res[n_out:]

    return run


def _pad_head_rows(arrays, casts, transposed_casts, plan=None):
    n_ex, seq, _ = arrays[0].shape
    nc = (HEAD_ROWS + seq) // CHUNK
    n, k = len(arrays), len(casts)
    kt = k + len(transposed_casts)

    def body(*refs):
        ins, outs = refs[:n + kt], refs[n + kt:]
        for a_ref, o_ref in zip(ins[:n], outs[:n]):
            o_ref[...] = jnp.where(pl.program_id(0) > 0, a_ref[...], 0.0)

        @pl.when(pl.program_id(0) == 0)
        def _():
            for a_ref, o_ref in zip(ins[n:n + k], outs[n:n + k]):
                o_ref[...] = a_ref[...].astype(BF16)
            for a_ref, o_ref in zip(ins[n + k:], outs[n + k:]):
                o_ref[...] = a_ref[...].T.astype(BF16)

    whole = lambda shape: pl.BlockSpec(shape, lambda i: (0, 0))
    cast_shapes = [a.shape for a in casts] + [a.shape[::-1] for a in transposed_casts]
    return _call(
        body, name="pad_head_rows", grid=(nc,),
        in_specs=([pl.BlockSpec((n_ex, CHUNK, D), lambda i: (0, jnp.maximum(i - 1, 0), 0))] * n
                  + [_const_spec(a.shape) for a in (*casts, *transposed_casts)]),
        out_specs=[pl.BlockSpec((n_ex, CHUNK, D), lambda i: (0, i, 0))] * n + [whole(s) for s in cast_shapes],
        out_shape=([jax.ShapeDtypeStruct((n_ex, HEAD_ROWS + seq, D), F32)] * n
                   + [jax.ShapeDtypeStruct(s, BF16) for s in cast_shapes]),
        plan=plan,
    )(*arrays, *casts, *transposed_casts)


def _set_meta_rows(h0, meta):
    n_ex = h0.shape[0]

    def body(h_ref, meta_ref, o_ref):
        o_ref[...] = jnp.concatenate(
            [h_ref[:, :PAD_ROWS, :], jnp.broadcast_to(meta_ref[...][None], (n_ex, N_META, D))], axis=1)

    head = pl.BlockSpec((n_ex, HEAD_ROWS, D), lambda i: (0, 0, 0))
    return pl.pallas_call(
        body, name="set_meta_rows", grid=(1,), in_specs=[head, pl.BlockSpec((N_META, D), lambda i: (0, 0))],
        out_specs=head, out_shape=jax.ShapeDtypeStruct(h0.shape, F32), input_output_aliases={0: 0},
        compiler_params=_params(1),
    )(h0, meta)


def _in_proj(h0, g_mix, w_in_t, plan=None):
    t = h0.shape[0]
    r = _row_tile(t, 384)

    def body(h_ref, g_ref, w_ref, u_ref, hn_ref):
        h = h_ref[...]
        rstd = lax.rsqrt(jnp.mean(h * h, axis=-1, keepdims=True) + RMS_EPS)
        hn = (h * rstd * g_ref[...]).astype(BF16)
        hn_ref[...] = hn
        u_ref[...] = _dot_nt(hn, w_ref[...])

    return _call(
        body, name="in_proj", grid=(t // r,),
        in_specs=[pl.BlockSpec((r, D), lambda i: (i, 0)), _const_spec((1, D)), _const_spec((D_IN_PAD, D))],
        out_specs=[pl.BlockSpec((r, D_IN_PAD), lambda i: (i, 0)), pl.BlockSpec((r, D), lambda i: (i, 0))],
        out_shape=[jax.ShapeDtypeStruct((t, D_IN_PAD), F32), jax.ShapeDtypeStruct((t, D), BF16)],
        plan=plan,
    )(h0, g_mix, w_in_t)


CONV_TILE = 192
CONV_SUB = 32
CONV_LEAD = CONV_SUB - (CONV_W - 1)


def _shifted_copies(src, dst, r):
    for s in range(1, SUBLANES):
        dst[s - 1] = src[s:s + r + CONV_SUB - SUBLANES, :]


def _shifted_rows(src, shifted, start):
    base, s = SUBLANES * (start // SUBLANES), start % SUBLANES
    if s == 0:
        return src[base:base + CONV_SUB, :]
    return shifted[s - 1, base:base + CONV_SUB, :]


def _conv_fwd(u, conv_w, conv_b, ln_g, ln_b, n_ex, lp, plan=None):
    r = CONV_TILE
    nt = lp // r
    hb = r // CONV_SUB

    def body(cur_ref, prev_ref, w_ref, b_ref, lg_ref, lb_ref, yc_ref, y_ref, glu, glu_sh):
        i = pl.program_id(1)
        cur = cur_ref[...]
        glu[CONV_SUB:CONV_SUB + r, :] = cur[:, :C_CONV] * _sigmoid(cur[:, C_CONV:])
        pv = prev_ref[...]
        halo = pv[:, :C_CONV] * _sigmoid(pv[:, C_CONV:])
        glu[0:CONV_SUB, :] = jnp.where(i > 0, halo, 0.0)
        _shifted_copies(glu, glu_sh, r)
        w = w_ref[...]
        for j in range(r // CONV_SUB):
            r0 = j * CONV_SUB
            acc = jnp.zeros((CONV_SUB, C_CONV), F32) + b_ref[...]
            for k in range(CONV_W):
                acc = acc + w[k:k + 1, :] * _shifted_rows(glu, glu_sh, r0 + CONV_LEAD + k)
            mu = jnp.mean(acc, axis=-1, keepdims=True)
            cen = acc - mu
            var = jnp.mean(cen * cen, axis=-1, keepdims=True)
            out = cen * lax.rsqrt(var + LN_EPS) * lg_ref[...] + lb_ref[...]
            y = out * _sigmoid(out)
            row = i * r + r0 + lax.broadcasted_iota(jnp.int32, (CONV_SUB, 1), 0)
            y = jnp.where(row >= PAD_ROWS, y, 0.0)
            yc_ref[r0:r0 + CONV_SUB, :] = acc
            y_ref[r0:r0 + CONV_SUB, :] = y.astype(BF16)

    t = n_ex * lp
    return _call(
        body, name="conv_fwd", grid=(n_ex, nt),
        in_specs=[pl.BlockSpec((r, 2 * C_CONV), lambda b, i: (b * nt + i, 0)),
                  pl.BlockSpec((CONV_SUB, 2 * C_CONV), lambda b, i: (jnp.maximum((b * nt + i) * hb - 1, 0), 0)),
                  _const_spec((32, C_CONV)), _const_spec((1, C_CONV)), _const_spec((1, C_CONV)), _const_spec((1, C_CONV))],
        out_specs=[pl.BlockSpec((r, C_CONV), lambda b, i: (b * nt + i, 0)),
                   pl.BlockSpec((r, C_CONV), lambda b, i: (b * nt + i, 0))],
        out_shape=[jax.ShapeDtypeStruct((t, C_CONV), F32), jax.ShapeDtypeStruct((t, C_CONV), BF16)],
        scratch_shapes=[pltpu.VMEM((r + CONV_SUB, C_CONV), F32),
                        pltpu.VMEM((SUBLANES - 1, r + CONV_SUB - SUBLANES, C_CONV), F32)],
        plan=plan,
    )(u, u, conv_w, conv_b, ln_g, ln_b)


def _mix_out_ffn_up(h0, y_conv, y_gla, w_out_t, g_ffn, w_gate_t, w_up_t, plan=None):
    t = h0.shape[0]
    r = _row_tile(t, 384)
    wb = D // N_CHIPS

    def body(h0_ref, yc_ref, yg_ref, wo_ref, g_ref, wg_ref, wu_ref, h1_ref, hn_ref, gate_ref, up_ref, act_ref):
        h1 = h0_ref[...]
        for j in range(N_CHIPS):
            y_ref, col = (yc_ref, j * wb) if j * wb < C_CONV else (yg_ref, j * wb - C_CONV)
            h1 = h1 + _dot_nt(y_ref[:, col:col + wb], wo_ref[j])
        h1_ref[...] = h1
        rstd = lax.rsqrt(jnp.mean(h1 * h1, axis=-1, keepdims=True) + RMS_EPS)
        hn = (h1 * rstd * g_ref[...]).astype(BF16)
        hn_ref[...] = hn
        gate = _dot_nt(hn, wg_ref[...])
        up = _dot_nt(hn, wu_ref[...])
        gate_ref[...] = gate
        up_ref[...] = up
        act_ref[...] = (gate * _sigmoid(gate) * up).astype(BF16)

    rows = lambda w: pl.BlockSpec((r, w), lambda i: (i, 0))
    return _call(
        body, name="mix_out_ffn_up", grid=(t // r,),
        in_specs=[rows(D), rows(C_CONV), rows(GLA_V), _const_spec((N_CHIPS, D, wb)), _const_spec((1, D)),
                  _const_spec((D_FF, D)), _const_spec((D_FF, D))],
        out_specs=[rows(D), rows(D), rows(D_FF), rows(D_FF), rows(D_FF)],
        out_shape=[jax.ShapeDtypeStruct((t, D), F32), jax.ShapeDtypeStruct((t, D), BF16),
                   jax.ShapeDtypeStruct((t, D_FF), F32), jax.ShapeDtypeStruct((t, D_FF), F32),
                   jax.ShapeDtypeStruct((t, D_FF), BF16)],
        plan=plan,
    )(h0, y_conv, y_gla, w_out_t, g_ffn, w_gate_t, w_up_t)


def _ffn_down_loss(act, w_down, h1, target, g_final, row_mask):
    t = h1.shape[0]
    r = _row_tile(t, 384)

    def body(act_ref, wd_ref, h1_ref, tgt_ref, gf_ref, mask_ref, dh2_ref, loss_ref, dgf_ref):
        @pl.when(pl.program_id(0) == 0)
        def _():
            loss_ref[...] = jnp.zeros_like(loss_ref)
            dgf_ref[...] = jnp.zeros_like(dgf_ref)

        gf = gf_ref[...]

        def part(rows):
            h2 = h1_ref[rows, :] + _dot(act_ref[rows, :], wd_ref[...])
            yield
            rstd = lax.rsqrt(jnp.mean(h2 * h2, axis=-1, keepdims=True) + RMS_EPS)
            nrm = h2 * rstd
            err = (nrm * gf - tgt_ref[rows, :]) * mask_ref[rows, :]
            loss_ref[...] += jnp.sum(err * err) * (0.5 / D)
            dy = err * (1.0 / D)
            dgf_ref[...] += jnp.sum(dy * nrm, axis=0, keepdims=True)
            dn = dy * gf
            dh2_ref[rows, :] = rstd * (dn - nrm * jnp.mean(dn * nrm, axis=-1, keepdims=True))

        _in_lockstep(part(rows) for rows in _row_parts(r))

    rows = lambda w: pl.BlockSpec((r, w), lambda i: (i, 0))
    return pl.pallas_call(
        body, name="ffn_down_loss", grid=(t // r,),
        in_specs=[rows(D_FF), _const_spec((D_FF, D)), rows(D), rows(D), _const_spec((1, D)), rows(1)],
        out_specs=[rows(D), _acc_spec((1, 128)), _acc_spec((1, D))],
        out_shape=[jax.ShapeDtypeStruct((t, D), F32), jax.ShapeDtypeStruct((1, 128), F32),
                   jax.ShapeDtypeStruct((1, D), F32)],
        compiler_params=_params(1),
    )(act, w_down, h1, target, g_final, row_mask)


def _ffn_bwd(dh2, gate, up, h1, w_down_t, w_gate_t, w_up_t, w_out_t, g_ffn):
    t = h1.shape[0]
    r = _row_tile(t, FFN_BWD_TILE)
    wb = D // N_CHIPS

    def body(dh2_ref, gate_ref, up_ref, h1_ref, wd_ref, wg_ref, wu_ref, wo_ref, g_ref,
             dgate_ref, dup_ref, dh1_ref, dycat_ref, dg_ref):
        @pl.when(pl.program_id(0) == 0)
        def _():
            dg_ref[...] = jnp.zeros_like(dg_ref)

        dh2 = dh2_ref[...]
        dact = _dot(dh2.astype(BF16), wd_ref[...])
        gate = gate_ref[...]
        sg = _sigmoid(gate)
        dgate = (dact * up_ref[...] * (sg * (1.0 + gate * (1.0 - sg)))).astype(BF16)
        dup = (dact * (gate * sg)).astype(BF16)
        dgate_ref[...] = dgate
        dup_ref[...] = dup
        dhn = _dot(dgate, wg_ref[...]) + _dot(dup, wu_ref[...])
        h1 = h1_ref[...]
        rstd = lax.rsqrt(jnp.mean(h1 * h1, axis=-1, keepdims=True) + RMS_EPS)
        nrm = h1 * rstd
        dg_ref[...] += jnp.sum(dhn * nrm, axis=0, keepdims=True)
        dn = dhn * g_ref[...]
        dh1 = dh2 + rstd * (dn - nrm * jnp.mean(dn * nrm, axis=-1, keepdims=True))
        dh1_ref[...] = dh1
        dh1 = dh1.astype(BF16)
        for j in range(N_CHIPS):
            dycat_ref[:, j * wb:(j + 1) * wb] = _dot(dh1, wo_ref[j])

    rows = lambda w: pl.BlockSpec((r, w), lambda i: (i, 0))
    return pl.pallas_call(
        body, name="ffn_bwd", grid=(t // r,),
        in_specs=[rows(D), rows(D_FF), rows(D_FF), rows(D), _const_spec((D, D_FF)), _const_spec((D_FF, D)),
                  _const_spec((D_FF, D)), _const_spec((N_CHIPS, D, wb)), _const_spec((1, D))],
        out_specs=[rows(D_FF), rows(D_FF), rows(D), rows(D), _acc_spec((1, D))],
        out_shape=[jax.ShapeDtypeStruct((t, D_FF), BF16), jax.ShapeDtypeStruct((t, D_FF), BF16),
                   jax.ShapeDtypeStruct((t, D), F32), jax.ShapeDtypeStruct((t, D), F32),
                   jax.ShapeDtypeStruct((1, D), F32)],
        compiler_params=_params(1),
    )(dh2, gate, up, h1, w_down_t, w_gate_t, w_up_t, w_out_t, g_ffn)


def _conv_bwd(dycat, yc, u, conv_w, ln_g, ln_b, n_ex, lp, plan=None):
    r = CONV_TILE
    nt = lp // r
    hb = r // CONV_SUB
    nsub = r // CONV_SUB

    def ln_bwd(dy, yc_rows, live, lg, lb):
        mu = jnp.mean(yc_rows, axis=-1, keepdims=True)
        cen = yc_rows - mu
        rs = lax.rsqrt(jnp.mean(cen * cen, axis=-1, keepdims=True) + LN_EPS)
        yn = cen * rs
        out = yn * lg + lb
        so = _sigmoid(out)
        dout = jnp.where(live, dy * (so * (1.0 + out * (1.0 - so))), 0.0)
        dyn = dout * lg
        dyc = rs * (dyn - jnp.mean(dyn, axis=-1, keepdims=True) - yn * jnp.mean(dyn * yn, axis=-1, keepdims=True))
        return dyc, dout, yn

    def body(dy_ref, dyn_ref, yc_ref, ycn_ref, cur_ref, prev_ref, w_ref, lg_ref, lb_ref,
             du_ref, dw_ref, db_ref, dlg_ref, dlb_ref, glu, dycs, dwacc, glu_sh, dycs_sh):
        b = pl.program_id(0)
        i = pl.program_id(1)
        first = jnp.logical_and(b == 0, i == 0)

        @pl.when(first)
        def _():
            dwacc[...] = jnp.zeros_like(dwacc)
            db_ref[...] = jnp.zeros_like(db_ref)
            dlg_ref[...] = jnp.zeros_like(dlg_ref)
            dlb_ref[...] = jnp.zeros_like(dlb_ref)

        lg, lb = lg_ref[...], lb_ref[...]
        cur = cur_ref[...]
        sig = _sigmoid(cur[:, C_CONV:])
        glu[CONV_SUB:CONV_SUB + r, :] = cur[:, :C_CONV] * sig
        pv = prev_ref[...]
        glu[0:CONV_SUB, :] = jnp.where(i > 0, pv[:, :C_CONV] * _sigmoid(pv[:, C_CONV:]), 0.0)

        row = i * r + lax.broadcasted_iota(jnp.int32, (r, 1), 0)
        dyc, dout, yn = ln_bwd(dy_ref[...], yc_ref[...], row >= PAD_ROWS, lg, lb)
        dycs[0:r, :] = dyc
        dycn, _, _ = ln_bwd(dyn_ref[...], ycn_ref[...], i < nt - 1, lg, lb)
        dycs[r:r + CONV_SUB, :] = dycn
        db_ref[...] += jnp.sum(dyc, axis=0, keepdims=True)
        dlg_ref[...] += jnp.sum(dout * yn, axis=0, keepdims=True)
        dlb_ref[...] += jnp.sum(dout, axis=0, keepdims=True)

        _shifted_copies(glu, glu_sh, r)
        _shifted_copies(dycs, dycs_sh, r)
        w = w_ref[...]
        for j in range(nsub):
            r0 = j * CONV_SUB
            dblk = dycs[r0:r0 + CONV_SUB, :]
            dglu = jnp.zeros((CONV_SUB, C_CONV), F32)
            for k in range(CONV_W):
                dglu = dglu + w[k:k + 1, :] * _shifted_rows(dycs, dycs_sh, r0 + (CONV_W - 1) - k)
                prod = dblk * _shifted_rows(glu, glu_sh, r0 + CONV_LEAD + k)
                dwacc[k] += prod.reshape(CONV_SUB // SUBLANES, SUBLANES, C_CONV).sum(axis=0)
            sg = sig[r0:r0 + CONV_SUB, :]
            cv = cur[r0:r0 + CONV_SUB, :C_CONV]
            du_ref[r0:r0 + CONV_SUB, :C_CONV] = (dglu * sg).astype(BF16)
            du_ref[r0:r0 + CONV_SUB, C_CONV:] = (dglu * cv * sg * (1.0 - sg)).astype(BF16)

        @pl.when(jnp.logical_and(b == n_ex - 1, i == nt - 1))
        def _():
            dw_ref[...] = jnp.sum(dwacc[...], axis=1)

    t = n_ex * lp
    cur_rows = lambda w, col: pl.BlockSpec((r, w), lambda b, i: (b * nt + i, col))
    nxt_rows = lambda w, col: pl.BlockSpec(
        (CONV_SUB, w), lambda b, i: (jnp.minimum((b * nt + i + 1) * hb, n_ex * nt * hb - 1), col))
    return _call(
        body, name="conv_bwd", grid=(n_ex, nt),
        in_specs=[cur_rows(C_CONV, 0), nxt_rows(C_CONV, 0), cur_rows(C_CONV, 0), nxt_rows(C_CONV, 0),
                  cur_rows(2 * C_CONV, 0),
                  pl.BlockSpec((CONV_SUB, 2 * C_CONV), lambda b, i: (jnp.maximum((b * nt + i) * hb - 1, 0), 0)),
                  _const_spec((32, C_CONV)), _const_spec((1, C_CONV)), _const_spec((1, C_CONV))],
        out_specs=[cur_rows(2 * C_CONV, 0), _acc_spec((32, C_CONV)), _acc_spec((1, C_CONV)),
                   _acc_spec((1, C_CONV)), _acc_spec((1, C_CONV))],
        out_shape=[jax.ShapeDtypeStruct((t, 2 * C_CONV), BF16), jax.ShapeDtypeStruct((32, C_CONV), F32),
                   jax.ShapeDtypeStruct((1, C_CONV), F32), jax.ShapeDtypeStruct((1, C_CONV), F32),
                   jax.ShapeDtypeStruct((1, C_CONV), F32)],
        scratch_shapes=[pltpu.VMEM((r + CONV_SUB, C_CONV), F32), pltpu.VMEM((r + CONV_SUB, C_CONV), F32),
                        pltpu.VMEM((32, SUBLANES, C_CONV), F32),
                        pltpu.VMEM((SUBLANES - 1, r + CONV_SUB - SUBLANES, C_CONV), F32),
                        pltpu.VMEM((SUBLANES - 1, r + CONV_SUB - SUBLANES, C_CONV), F32)],
        plan=plan,
    )(dycat, dycat, yc, yc, u, u, conv_w, ln_g, ln_b)


HEAD_ROWS_ALL = N_HEADS * CHUNK


def _gla_gates(lr, w2, gb, first_chunk):
    z = _dot(lr.astype(BF16), w2) + gb
    a = (jnp.minimum(z, 0.0) - jnp.log(1.0 + jnp.exp(-jnp.abs(z)))) * (1.0 / GATE_TAU)
    row = lax.broadcasted_iota(jnp.int32, (CHUNK, 1), 0)
    live = jnp.logical_or(jnp.logical_not(first_chunk), row >= PAD_ROWS)
    return z, jnp.where(live, a, 0.0), live


def _tri(lower):
    i = lax.broadcasted_iota(jnp.int32, (CHUNK, CHUNK), 0)
    j = lax.broadcasted_iota(jnp.int32, (CHUNK, CHUNK), 1)
    return (i >= j) if lower else (i <= j)


def _head_of(shape, axis, per_head):
    return lax.broadcasted_iota(jnp.int32, shape, axis) // per_head


def _expand(x, lanes_per_head):
    rows, lanes = HEAD_ROWS_ALL, x.shape[1]
    keep = _head_of((rows, lanes), 0, CHUNK) == _head_of((rows, lanes), 1, lanes_per_head)
    return jnp.where(keep, jnp.tile(x, (N_HEADS, 1)), 0.0)


def _expand_lanes(x):
    rows, w = x.shape
    keep = _head_of((rows, N_HEADS * w), 0, CHUNK) == _head_of((rows, N_HEADS * w), 1, w)
    return jnp.where(keep, jnp.tile(x, (1, N_HEADS)), 0.0)


def _expand_state(st):
    rows, lanes = N_HEADS * DV, st.shape[1]
    keep = _head_of((rows, lanes), 0, DV) == _head_of((rows, lanes), 1, DK)
    return jnp.where(keep, jnp.tile(st, (N_HEADS, 1)), 0.0)


def _fold(t, rows_per_head):
    lane_head = _head_of((rows_per_head, t.shape[1]), 1, DK)
    out = jnp.where(lane_head == 0, t[0:rows_per_head], 0.0)
    for h in range(1, N_HEADS):
        out = out + jnp.where(lane_head == h, t[h * rows_per_head:(h + 1) * rows_per_head], 0.0)
    return out


def _rows_by_head(x):
    return jnp.concatenate([x[:, h * DV:(h + 1) * DV] for h in range(N_HEADS)], axis=0)


def _lanes_by_head(x):
    return jnp.concatenate([x[h * CHUNK:(h + 1) * CHUNK] for h in range(N_HEADS)], axis=1)


def _running_sum(a, lower):
    hi = a.astype(BF16)
    rest = a - hi.astype(F32)
    mid = rest.astype(BF16)
    lo = (rest - mid.astype(F32)).astype(BF16)
    w = a.shape[1]
    parts = _dot(_tri(lower).astype(F32).astype(BF16), jnp.concatenate([hi, mid, lo], axis=1))
    return parts[:, :w] + parts[:, w:2 * w] + parts[:, 2 * w:]


def _stacked_causal():
    i = lax.broadcasted_iota(jnp.int32, (HEAD_ROWS_ALL, CHUNK), 0) % CHUNK
    j = lax.broadcasted_iota(jnp.int32, (HEAD_ROWS_ALL, CHUNK), 1)
    return i >= j


GLA_GROUP = 3


def _gla_chunk(q, k, v, lr, w2, gb, first_chunk):
    z, a, live = _gla_gates(lr, w2, gb, first_chunk)
    yield
    b = _running_sum(a, True)
    yield
    bl = b[CHUNK - 1:CHUNK, :]
    e_pos, e_neg, e_dec = jnp.exp(b), jnp.exp(-b), jnp.exp(bl - b)
    q_f, k_f, kd_f = q * (DK ** -0.5) * e_pos, k * e_neg, k * e_dec
    qx = _expand(q_f, DK).astype(BF16)
    k_in, k_dec, v_b = k_f.astype(BF16), kd_f.astype(BF16), v.astype(BF16)
    s = jnp.where(_stacked_causal(), _dot_nt(qx, k_in), 0.0).astype(BF16)
    yield
    p = _dot(s, v_b)
    yield
    o_intra = jnp.concatenate([p[h * CHUNK:(h + 1) * CHUNK, h * DV:(h + 1) * DV] for h in range(N_HEADS)], axis=0)
    return dict(z=z, live=live, bl=bl, e_pos=e_pos, e_neg=e_neg, e_dec=e_dec, q_f=q_f, k_f=k_f, kd_f=kd_f,
                qx=qx, k_in=k_in, k_dec=k_dec, v_b=v_b, s=s, o_intra=o_intra, decay=jnp.exp(bl))


def _gla_fwd(u, w2, gb, ng, n_ex, lp, plan=None):
    nc = lp // CHUNK
    t = n_ex * lp
    rows_of = lambda j: pl.ds(j * CHUNK, CHUNK)

    def body(qk_ref, v_ref, g_ref, lr_ref, w2_ref, gb_ref, ng_ref, y_ref, st_ref, state):
        n = pl.program_id(0)

        @pl.when(n == 0)
        def _():
            state[...] = jnp.zeros_like(state)

        carried = [state[e] for e in range(n_ex)]

        def one_chunk(e, j):
            rows = rows_of(j)
            qk = qk_ref[e, rows, :]
            first = jnp.logical_and(n == 0, j == 0)
            c = yield from _gla_chunk(qk[:, :GLA_K], qk[:, GLA_K:], v_ref[e, rows, :], lr_ref[e, rows, :],
                                      w2_ref[...], gb_ref[...], first)
            kv = _fold(_dot_tn(c["v_b"], c["k_dec"]), DV)
            g = _rows_by_head(g_ref[e, rows, :])
            gate = ng_ref[...] * (g * _sigmoid(g))
            yield
            for _ in range(j):
                yield
            st = carried[e]
            st_ref[e, pl.ds(j * DV, DV), :] = st
            o = c["o_intra"] + _dot_nt(c["qx"], st.astype(BF16))
            rstd = lax.rsqrt(jnp.mean(o * o, axis=-1, keepdims=True) + RMS_EPS)
            y_ref[e, rows, :] = _lanes_by_head(o * rstd * gate).astype(BF16)
            carried[e] = c["decay"] * st + kv

        _in_lockstep(one_chunk(e, j) for j in range(GLA_GROUP) for e in range(n_ex))
        for e in range(n_ex):
            state[e] = carried[e]

    u3 = u.reshape(n_ex, lp, D_IN_PAD)
    blk = lambda w, col: pl.BlockSpec((n_ex, GLA_GROUP * CHUNK, w), lambda n: (0, n, col))
    (y, states), extra = _call(
        body, name="gla_fwd", grid=(nc // GLA_GROUP,),
        in_specs=[blk(2 * GLA_K, 2), blk(GLA_V, 3), blk(GLA_V, 4), blk(128, 20),
                  _const_spec((128, GLA_K)), _const_spec((1, GLA_K)), _const_spec((1, DV))],
        out_specs=[blk(GLA_V, 0), pl.BlockSpec((n_ex, GLA_GROUP * DV, GLA_K), lambda n: (0, n, 0))],
        out_shape=[jax.ShapeDtypeStruct((n_ex, lp, GLA_V), BF16),
                   jax.ShapeDtypeStruct((n_ex, nc * DV, GLA_K), F32)],
        scratch_shapes=[pltpu.VMEM((n_ex, DV, GLA_K), F32)],
        plan=plan,
    )(u3, u3, u3, u3, w2, gb, ng)
    return (y.reshape(t, GLA_V), states), extra


def _gla_bwd(dycat, u, states, w2, gb, ng, n_ex, lp, plan=None):
    nc = lp // CHUNK
    t = n_ex * lp

    def body(dy_ref, qk_ref, v_ref, g_ref, lr_ref, st_ref, w2_ref, gb_ref, ng_ref,
             du_ref, dw2_ref, dgb_ref, dng_ref, dstate):
        n = pl.program_id(0)
        group = nc // GLA_GROUP - 1 - n

        @pl.when(n == 0)
        def _():
            dw2_ref[...] = jnp.zeros_like(dw2_ref)
            dgb_ref[...] = jnp.zeros_like(dgb_ref)
            dng_ref[...] = jnp.zeros_like(dng_ref)
            dstate[...] = jnp.zeros_like(dstate)

        carried = [dstate[e] for e in range(n_ex)]

        def one_chunk(e, order):
            j = GLA_GROUP - 1 - order
            rows = pl.ds(j * CHUNK, CHUNK)
            qk = qk_ref[e, rows, :]
            lr = lr_ref[e, rows, :]
            st = st_ref[e, pl.ds(j * DV, DV), :]
            first = jnp.logical_and(group == 0, j == 0)
            c = yield from _gla_chunk(qk[:, :GLA_K], qk[:, GLA_K:], v_ref[e, rows, :], lr, w2_ref[...], gb_ref[...],
                                      first)
            qx, k_in, k_dec, v_b, s = c["qx"], c["k_in"], c["k_dec"], c["v_b"], c["s"]
            st_b = st.astype(BF16)
            o = c["o_intra"] + _dot_nt(qx, st_b)
            ngv = ng_ref[...]
            yield
            rstd = lax.rsqrt(jnp.mean(o * o, axis=-1, keepdims=True) + RMS_EPS)
            nrm = o * rstd
            g = _rows_by_head(g_ref[e, rows, :])
            dy = _rows_by_head(dy_ref[e, rows, :])
            sg = _sigmoid(g)
            dg = dy * nrm * ngv * (sg * (1.0 + g * (1.0 - sg)))
            dt = dy * (g * sg)
            dng_ref[...] += jnp.sum(dt * nrm, axis=0, keepdims=True)
            dn = dt * ngv
            do = rstd * (dn - nrm * jnp.mean(dn * nrm, axis=-1, keepdims=True))
            do_b = do.astype(BF16)
            dox = _expand_lanes(do).astype(BF16)
            yield
            da = jnp.where(_stacked_causal(), _dot_nt(dox, v_b), 0.0).astype(BF16)
            dv_intra = _dot_tn(s, dox)
            dst_own = _dot_tn(do_b, qx)
            yield
            dq_in = _fold(_dot(da, k_in) + _dot(do_b, st_b), CHUNK)
            dk_in = _dot_tn(da, qx)
            dq = dq_in * (DK ** -0.5) * c["e_pos"]
            yield
            for _ in range(order):
                yield
            dst = carried[e]
            dstx = _expand_state(dst).astype(BF16)
            dv = dv_intra + _dot_nt(k_dec, dstx)
            dk_dec = _dot(v_b, dstx)
            carried[e] = dst_own + c["decay"] * dst
            yield
            dbl = (jnp.sum(dk_dec * c["kd_f"], axis=0, keepdims=True)
                   + c["decay"] * jnp.sum(dst * st, axis=0, keepdims=True))
            dk = dk_in * c["e_neg"] + dk_dec * c["e_dec"]
            db = dq_in * c["q_f"] - dk_in * c["k_f"] - dk_dec * c["kd_f"]
            row = lax.broadcasted_iota(jnp.int32, (CHUNK, 1), 0)
            da_log = _running_sum(db + jnp.where(row == CHUNK - 1, dbl, 0.0), False)
            yield
            dz = jnp.where(c["live"], da_log * (1.0 - _sigmoid(c["z"])) * (1.0 / GATE_TAU), 0.0)
            dz_b = dz.astype(BF16)
            out = du_ref.at[e, rows, :]
            out[:, 0:GLA_K] = dq.astype(BF16)
            out[:, GLA_K:2 * GLA_K] = dk.astype(BF16)
            out[:, 2 * GLA_K:2 * GLA_K + GLA_V] = dv.astype(BF16)
            out[:, 2 * GLA_K + GLA_V:2 * GLA_K + 2 * GLA_V] = _lanes_by_head(dg).astype(BF16)
            out[:, 2 * GLA_K + 2 * GLA_V:] = _dot_nt(dz_b, w2_ref[...]).astype(BF16)
            dw2_ref[...] += _dot_tn(lr.astype(BF16), dz_b)
            dgb_ref[...] += jnp.sum(dz, axis=0, keepdims=True)

        _in_lockstep(one_chunk(e, order) for order in range(GLA_GROUP) for e in range(n_ex))
        for e in range(n_ex):
            dstate[e] = carried[e]

    u3 = u.reshape(n_ex, lp, D_IN_PAD)
    rev = lambda w, col: pl.BlockSpec((n_ex, GLA_GROUP * CHUNK, w), lambda n: (0, nc // GLA_GROUP - 1 - n, col))
    (du, d_w2, d_gb, d_ng), extra = _call(
        body, name="gla_bwd", grid=(nc // GLA_GROUP,),
        in_specs=[rev(GLA_V, 1), rev(2 * GLA_K, 2), rev(GLA_V, 3), rev(GLA_V, 4), rev(128, 20),
                  pl.BlockSpec((n_ex, GLA_GROUP * DV, GLA_K), lambda n: (0, nc // GLA_GROUP - 1 - n, 0)),
                  _const_spec((128, GLA_K)), _const_spec((1, GLA_K)), _const_spec((1, DV))],
        out_specs=[rev(D_GLA_IN, 0), _acc_spec((128, GLA_K)), _acc_spec((1, GLA_K)), _acc_spec((1, DV))],
        out_shape=[jax.ShapeDtypeStruct((n_ex, lp, D_GLA_IN), BF16), jax.ShapeDtypeStruct((128, GLA_K), F32),
                   jax.ShapeDtypeStruct((1, GLA_K), F32), jax.ShapeDtypeStruct((1, DV), F32)],
        scratch_shapes=[pltpu.VMEM((n_ex, DV, GLA_K), F32)],
        plan=plan,
    )(dycat.reshape(n_ex, lp, D), u3, u3, u3, u3, states, w2, gb, ng)
    return (du.reshape(t, D_GLA_IN), d_w2, d_gb, d_ng), extra


def _in_proj_bwd(du_conv, du_gla, w_in_t_conv, w_in_t_gla, h0, dh1, g_mix, plan=None):
    t = h0.shape[0]
    r = _row_tile(t, 384)

    def body(dc_ref, dg_ref, wc_ref, wg_ref, h_ref, dh1_ref, g_ref, dh0_ref, dgm_ref):
        @pl.when(pl.program_id(0) == 0)
        def _():
            dgm_ref[...] = jnp.zeros_like(dgm_ref)

        dhn = _dot(dc_ref[...], wc_ref[...]) + _dot(dg_ref[...], wg_ref[...])
        h = h_ref[...]
        rstd = lax.rsqrt(jnp.mean(h * h, axis=-1, keepdims=True) + RMS_EPS)
        nrm = h * rstd
        dgm_ref[...] += jnp.sum(dhn * nrm, axis=0, keepdims=True)
        dn = dhn * g_ref[...]
        dh0_ref[...] = dh1_ref[...] + rstd * (dn - nrm * jnp.mean(dn * nrm, axis=-1, keepdims=True))

    rows = lambda w: pl.BlockSpec((r, w), lambda i: (i, 0))
    return _call(
        body, name="in_proj_bwd", grid=(t // r,),
        in_specs=[rows(2 * C_CONV), rows(D_GLA_IN), _const_spec((2 * C_CONV, D)), _const_spec((D_GLA_IN, D)),
                  rows(D), rows(D), _const_spec((1, D))],
        out_specs=[rows(D), _acc_spec((1, D))],
        out_shape=[jax.ShapeDtypeStruct((t, D), F32), jax.ShapeDtypeStruct((1, D), F32)],
        plan=plan,
    )(du_conv, du_gla, w_in_t_conv, w_in_t_gla, h0, dh1, g_mix)


def _wgrad(x, dy, name, plan=None):
    t, m = x.shape
    n = dy.shape[1]
    tk = t // 3 if t % (3 * 128) == 0 else _row_tile(t, 384)
    tm = m if m <= D_GLA_IN else m // 2

    def body(x_ref, dy_ref, o_ref):
        @pl.when(pl.program_id(1) == 0)
        def _():
            o_ref[...] = jnp.zeros_like(o_ref)

        o_ref[...] += _dot_tn(x_ref[...].astype(BF16), dy_ref[...].astype(BF16))

    (out,), extra = _call(
        body, name=name, grid=(m // tm, t // tk),
        in_specs=[pl.BlockSpec((tk, tm), lambda i, k: (k, i)), pl.BlockSpec((tk, n), lambda i, k: (k, 0))],
        out_specs=[pl.BlockSpec((tm, n), lambda i, k: (i, 0))],
        out_shape=[jax.ShapeDtypeStruct((m, n), F32)],
        plan=plan,
    )(x, dy)
    return out, extra


def _wgrad_pair(xa, xb, dy, name):
    t, m = xa.shape
    n = dy.shape[1]
    tk = t // 3 if t % (3 * 128) == 0 else _row_tile(t, 384)

    def body(xa_ref, xb_ref, dy_ref, o_ref):
        @pl.when(pl.program_id(1) == 0)
        def _():
            o_ref[...] = jnp.zeros_like(o_ref)

        x = jnp.where(pl.program_id(0) == 0, xa_ref[...], xb_ref[...])
        o_ref[...] += _dot_tn(x.astype(BF16), dy_ref[...].astype(BF16))

    rows = lambda w: pl.BlockSpec((tk, w), lambda i, k: (k, 0))
    return pl.pallas_call(
        body, name=name, grid=(2, t // tk), in_specs=[rows(m), rows(m), rows(n)],
        out_specs=pl.BlockSpec((m, n), lambda i, k: (i, 0)),
        out_shape=jax.ShapeDtypeStruct((2 * m, n), F32), compiler_params=_params(2),
    )(xa, xb, dy)


def _adam_update(g, w, m, v):
    m2 = ADAM_B1 * m + (1.0 - ADAM_B1) * g
    v2 = ADAM_B2 * v + (1.0 - ADAM_B2) * (g * g)
    m_hat = m2 / (1.0 - ADAM_B1 ** ADAM_STEP)
    v_hat = v2 / (1.0 - ADAM_B2 ** ADAM_STEP)
    delta = -ADAM_LR * (m_hat / (jnp.sqrt(v_hat) + ADAM_EPS) + ADAM_WD * w)
    return delta, m2, v2


ADAMW_STEPS = 4


def _adamw(g, w, m, v, name):
    rows, cols = g.shape
    steps = ADAMW_STEPS if rows % (ADAMW_STEPS * SUBLANES) == 0 else 1

    def body(g_ref, w_ref, m_ref, v_ref, d_ref, m2_ref, v2_ref):
        d_ref[...], m2_ref[...], v2_ref[...] = _adam_update(g_ref[...], w_ref[...], m_ref[...], v_ref[...])

    spec = pl.BlockSpec((rows // steps, cols), lambda i: (i, 0))
    return pl.pallas_call(
        body, name=name, grid=(steps,), in_specs=[spec] * 4, out_specs=[spec] * 3,
        out_shape=[jax.ShapeDtypeStruct(g.shape, F32)] * 3, compiler_params=_params(1),
    )(g, w, m, v)


def _adamw_halves(items, c, name):
    n = len(items)
    h = items[0][0].shape[-1]
    splits = lambda a: a.shape[0] % (ADAMW_STEPS * (SUBLANES if a.ndim == 2 else 1)) == 0
    steps = ADAMW_STEPS if all(splits(it[0]) for it in items) else 1

    def body(c_ref, *refs):
        ins, outs = refs[:5 * n], refs[5 * n:]
        own = pl.program_id(1) == c_ref[0]
        for i in range(n):
            a_ref, b_ref, w_ref, m_ref, v_ref = ins[5 * i:5 * i + 5]
            go_ref, d_ref, m2_ref, v2_ref = outs[4 * i:4 * i + 4]
            g = jnp.where(own, a_ref[...], b_ref[...])
            go_ref[...] = g
            d_ref[...], m2_ref[...], v2_ref[...] = _adam_update(g, w_ref[...], m_ref[...], v_ref[...])

    in_specs, out_specs, out_shape, args = [pl.BlockSpec(memory_space=pltpu.SMEM)], [], [], []
    for mine, theirs, w, m, v in items:
        tr = mine.shape[0] // steps
        mid = (0,) * (mine.ndim - 2)
        half = pl.BlockSpec((tr,) + mine.shape[1:-1] + (h,), lambda i, j, mid=mid: (i, *mid, 0))
        full = pl.BlockSpec((tr,) + mine.shape[1:-1] + (h,), lambda i, j, mid=mid: (i, *mid, j))
        in_specs += [half, half, full, full, full]
        out_specs += [full] * 4
        out_shape += [jax.ShapeDtypeStruct(w.shape, F32)] * 4
        args += [mine, theirs, w, m, v]
    res = pl.pallas_call(
        body, name=name, grid=(steps, 2), in_specs=in_specs, out_specs=out_specs, out_shape=out_shape,
        compiler_params=_params(2),
    )(jnp.reshape(c, (1,)).astype(jnp.int32), *args)
    return [res[4 * i:4 * i + 4] for i in range(n)]


def _rs_add_halves(pairs, c, name):
    blocks = pairs[0][0].shape[0]
    n = len(pairs)

    def body(c_ref, *refs):
        for i in range(n):
            refs[2 * n + i][...] = (refs[2 * i][...] + refs[2 * i + 1][...]).astype(BF16)

    in_specs, out_specs, out_shape = [], [], []
    for g, _ in pairs:
        _, rows, w = g.shape
        in_specs += [pl.BlockSpec((1, rows, w // 2), lambda j, s: (j, 0, s[0])),
                     pl.BlockSpec((1, rows, w // 2), lambda j, s: (j, 0, 0))]
        out_specs += [pl.BlockSpec((1, rows, w // 2), lambda j, s: (j, 0, 0))]
        out_shape += [jax.ShapeDtypeStruct((blocks, rows, w // 2), BF16)]
    return pl.pallas_call(
        body, name=name,
        grid_spec=pltpu.PrefetchScalarGridSpec(num_scalar_prefetch=1, grid=(blocks,), in_specs=in_specs,
                                               out_specs=out_specs),
        out_shape=out_shape, compiler_params=_params(1),
    )(jnp.reshape(c, (1,)).astype(jnp.int32), *[a for pair in pairs for a in pair])


def _rs_sum(pairs, mine, name):
    n = len(pairs)
    steps = 2 if all(own.shape[1] % (2 * 16) == 0 for own, _ in pairs) else 1

    def body(mine_ref, *refs):
        for i in range(n):
            p = refs[2 * i + 1][...].astype(F32)
            refs[2 * n + i][...] = ((refs[2 * i][0].astype(F32) + p[0]) + p[1]) + p[2]

    in_specs, out_specs, out_shape = [], [], []
    for own, _ in pairs:
        _, rows, h = own.shape
        tr = rows // steps
        in_specs += [pl.BlockSpec((1, tr, h), lambda i, s: (s[0], i, 0)),
                     pl.BlockSpec((3, tr, h), lambda i, s: (0, i, 0))]
        out_specs += [pl.BlockSpec((tr, h), lambda i, s: (i, 0))]
        out_shape += [jax.ShapeDtypeStruct((rows, h), F32)]
    return pl.pallas_call(
        body, name=name,
        grid_spec=pltpu.PrefetchScalarGridSpec(num_scalar_prefetch=1, grid=(steps,), in_specs=in_specs,
                                               out_specs=out_specs),
        out_shape=out_shape, compiler_params=_params(1),
    )(jnp.reshape(mine, (1,)).astype(jnp.int32), *[a for pair in pairs for a in pair])


def _sum_slots_adamw(slots, late_slots, vectors):
    late_rows = late_slots.shape[1]
    n = len(SMALL_PARTS)

    def body(s_ref, l_ref, *refs):
        ins, g_ref, outs = refs[:3 * n], refs[3 * n], refs[3 * n + 1:]
        g, late = s_ref[0], l_ref[0]
        for d in range(1, 8):
            g = g + s_ref[d]
            late = late + l_ref[d]
        g = jnp.concatenate([g[:late_rows] + late, g[late_rows:]], axis=0)
        g_ref[...] = g
        for i, (_, row, col, size) in enumerate(SMALL_PARTS):
            w_ref, m_ref, v_ref = ins[3 * i:3 * i + 3]
            go_ref, d_ref, m2_ref, v2_ref = outs[4 * i:4 * i + 4]
            piece = g[row:row + 1, col:col + size]
            go_ref[...] = piece
            d_ref[...], m2_ref[...], v2_ref[...] = _adam_update(piece, w_ref[...], m_ref[...], v_ref[...])

    vm = pl.BlockSpec(memory_space=pltpu.VMEM)
    out_shape = [jax.ShapeDtypeStruct(slots.shape[1:], F32)]
    for _, _, _, size in SMALL_PARTS:
        out_shape += [jax.ShapeDtypeStruct((1, size), F32)] * 4
    res = pl.pallas_call(body, name="small_sum_adamw", in_specs=[vm] * (2 + 3 * n), out_specs=[vm] * len(out_shape),
                         out_shape=out_shape)(slots, late_slots, *[a for wmv in vectors for a in wmv])
    return res[0], [res[1 + 4 * i:5 + 4 * i] for i in range(n)]


def _mesh_pos():
    return lax.axis_index("x"), lax.axis_index("y"), lax.axis_index("c")


def _other_chips(x, y):
    return [(1 - x, y), (x, 1 - y), (1 - x, 1 - y)]


def _half(ref, c, axis):
    n = ref.shape[axis] // 2
    return ref.at[(slice(None),) * axis + (pl.ds(c * n, n),)]


def _remote(src, dst, send_sem, recv_sem, device):
    return pltpu.make_async_remote_copy(src_ref=src, dst_ref=dst, send_sem=send_sem, recv_sem=recv_sem,
                                        device_id=device, device_id_type=MESH)


def _gather_plan(split, whole=(), axes=None):
    split, whole = list(split), list(whole)
    ns, n = len(split), len(split) + len(whole)

    def make(ins, outs, sems):
        ici_send, ici_recv, d2d_send, d2d_recv, own_send, own_recv = sems
        x, y, c = _mesh_pos()
        mine = 2 * x + y
        chips = _other_chips(x, y)
        blocks = [2 * px + py for px, py in chips]

        def own(a):
            return _remote(ins[a], outs[a].at[mine], own_send.at[a], own_recv.at[a], (x, y, 1 - c))

        def ici(a, k, block):
            px, py = chips[k]
            src, dst = ins[a], outs[a].at[block]
            if a < ns:
                src, dst = _half(src, c, axes[a]), _half(dst, c, axes[a])
            return _remote(src, dst, ici_send.at[3 * a + k], ici_recv.at[3 * a + k], (px, py, c))

        def d2d(a, k, half):
            part = _half(outs[a].at[blocks[k]], half, axes[a])
            return _remote(part, part, d2d_send.at[3 * a + k], d2d_recv.at[3 * a + k], (x, y, 1 - c))

        def start():
            for a in range(n):
                for k in range(3):
                    ici(a, k, mine).start()
                own(a).start()

        def relay():
            for a in range(n):
                for k in range(3):
                    ici(a, k, blocks[k]).wait_recv()
                    if a < ns:
                        d2d(a, k, c).start()

        def finish():
            for a in range(ns):
                for k in range(3):
                    d2d(a, k, 1 - c).wait_recv()
            for a in range(n):
                for k in range(3):
                    ici(a, k, mine).wait_send()
                    if a < ns:
                        d2d(a, k, c).wait_send()
                own(a).wait()

        return start, relay, finish

    arrays = split + whole
    axes = [0] * ns if axes is None else list(axes)
    return _Plan(arrays, [jax.ShapeDtypeStruct((N_CHIPS,) + s.shape, s.dtype) for s in arrays],
                 [pltpu.SemaphoreType.DMA((3 * n,)), pltpu.SemaphoreType.DMA((3 * n,)),
                  pltpu.SemaphoreType.DMA((3 * ns,)), pltpu.SemaphoreType.DMA((3 * ns,)),
                  pltpu.SemaphoreType.DMA((n,)), pltpu.SemaphoreType.DMA((n,))], make)


def _to_sibling_plan(gs):
    n = len(gs)

    def make(ins, outs, sems):
        send_sems, recv_sems = sems
        x, y, c = _mesh_pos()

        def copy(a):
            return _remote(_half(ins[a], 1 - c, len(ins[a].shape) - 1), outs[a], send_sems.at[a],
                           recv_sems.at[a], (x, y, 1 - c))

        def start():
            for a in range(n):
                copy(a).start()

        def finish():
            for a in range(n):
                copy(a).wait()

        return start, finish

    return _Plan(list(gs), [jax.ShapeDtypeStruct(g.shape[:-1] + (g.shape[-1] // 2,), g.dtype) for g in gs],
                 [pltpu.SemaphoreType.DMA((n,)), pltpu.SemaphoreType.DMA((n,))], make)


def _chip_exchange_plan(ps):
    n = len(ps)

    def make(ins, outs, sems):
        send_sems, recv_sems = sems
        x, y, c = _mesh_pos()
        chips = _other_chips(x, y)

        def ici(a, k):
            px, py = chips[k]
            return _remote(ins[a].at[2 * px + py], outs[a].at[k], send_sems.at[3 * a + k],
                           recv_sems.at[3 * a + k], (px, py, c))

        def start():
            for a in range(n):
                for k in range(3):
                    ici(a, k).start()

        def finish():
            for a in range(n):
                for k in range(3):
                    ici(a, k).wait()

        return start, finish

    return _Plan(list(ps), [jax.ShapeDtypeStruct((3,) + p.shape[1:], p.dtype) for p in ps],
                 [pltpu.SemaphoreType.DMA((3 * n,)), pltpu.SemaphoreType.DMA((3 * n,))], make)


def _share_plan(halves):
    n = len(halves)

    def make(ins, outs, sems):
        send_sems, recv_sems = sems
        x, y, c = _mesh_pos()

        def d2d(a):
            return _remote(ins[a], outs[a], send_sems.at[a], recv_sems.at[a], (x, y, 1 - c))

        def start():
            for a in range(n):
                d2d(a).start()

        def finish():
            for a in range(n):
                d2d(a).wait()

        return start, finish

    return _Plan(list(halves), [jax.ShapeDtypeStruct(p.shape, p.dtype) for p in halves],
                 [pltpu.SemaphoreType.DMA((n,)), pltpu.SemaphoreType.DMA((n,))], make)


def _all_to_all_plan(part):
    def make(ins, outs, sems):
        send_sems, recv_sems, local_sem = sems
        (p_ref,), (slots,) = ins, outs
        x, y, c = _mesh_pos()
        me = 4 * x + 2 * y + c
        peers = [(px, py, pc) for px in (x, 1 - x) for py in (y, 1 - y) for pc in (c, 1 - c)][1:]

        def remote(k, slot):
            return _remote(p_ref, slots.at[slot], send_sems.at[k], recv_sems.at[k], peers[k])

        def local():
            return pltpu.make_async_copy(p_ref, slots.at[me], local_sem)

        def start():
            for k in range(7):
                remote(k, me).start()
            local().start()

        def finish():
            for k, (px, py, pc) in enumerate(peers):
                remote(k, 4 * px + 2 * py + pc).wait_recv()
            for k in range(7):
                remote(k, me).wait_send()
            local().wait()

        return start, finish

    return _Plan([part], [jax.ShapeDtypeStruct((8,) + part.shape, part.dtype)],
                 [pltpu.SemaphoreType.DMA((7,)), pltpu.SemaphoreType.DMA((7,)), pltpu.SemaphoreType.DMA(())], make)


def _merge_plans(a, b):
    na_in, na_out, na_sems = len(a.arrays), len(a.out_shape), len(a.sems)

    def make(ins, outs, sems):
        phases_a = _phases(a.make(ins[:na_in], outs[:na_out], sems[:na_sems]))
        phases_b = _phases(b.make(ins[na_in:], outs[na_out:], sems[na_sems:]))

        def both(i):
            def run():
                phases_a[i]()
                phases_b[i]()
            return run

        return both(0), both(1), both(2)

    return _Plan(list(a.arrays) + list(b.arrays), list(a.out_shape) + list(b.out_shape),
                 list(a.sems) + list(b.sems), make)


def _exchange(plan, name):
    n_in, n_out = len(plan.arrays), len(plan.out_shape)

    def body(*refs):
        for phase in _phases(plan.make(refs[:n_in], refs[n_in:n_in + n_out], refs[n_in + n_out:])):
            phase()

    return pl.pallas_call(
        body, name=name, in_specs=[HBM_SPEC] * n_in, out_specs=[HBM_SPEC] * n_out, out_shape=list(plan.out_shape),
        scratch_shapes=list(plan.sems), compiler_params=pltpu.CompilerParams(has_side_effects=True),
    )(*plan.arrays)


def _pack_small(parts):
    rows = []
    for r in range(SMALL_ROWS):
        pieces, col = [], 0
        for name, row, start, size in SMALL_PARTS:
            if row == r:
                assert start == col
                pieces.append(parts[name].reshape(1, size).astype(F32))
                col += size
        rows.append(jnp.concatenate(pieces + [jnp.zeros((1, D - col), F32)], axis=1))
    return jnp.concatenate(rows, axis=0)


def _columns(gathered):
    return jnp.concatenate([gathered[j] for j in range(N_CHIPS)], axis=1)


def kernel(x, meta_tokens, norm_mix_g, w_in, conv_w, conv_b, conv_ln_g, conv_ln_b, gla_w_gate2, gla_gate_b, gla_norm_g, w_out, norm_ffn_g, w_ffn_gate, w_ffn_up, w_ffn_down, norm_final_g, loss_target, m_meta_tokens, m_norm_mix_g, m_w_in, m_conv_w, m_conv_b, m_conv_ln_g, m_conv_ln_b, m_gla_w_gate2, m_gla_gate_b, m_gla_norm_g, m_w_out, m_norm_ffn_g, m_w_ffn_gate, m_w_ffn_up, m_w_ffn_down, m_norm_final_g, v_meta_tokens, v_norm_mix_g, v_w_in, v_conv_w, v_conv_b, v_conv_ln_g, v_conv_ln_b, v_gla_w_gate2, v_gla_gate_b, v_gla_norm_g, v_w_out, v_norm_ffn_g, v_w_ffn_gate, v_w_ffn_up, v_w_ffn_down, v_norm_final_g):
    ws = dict(zip(WEIGHT_NAMES, (meta_tokens, norm_mix_g, w_in, conv_w, conv_b, conv_ln_g, conv_ln_b, gla_w_gate2,
                                 gla_gate_b, gla_norm_g, w_out, norm_ffn_g, w_ffn_gate, w_ffn_up, w_ffn_down,
                                 norm_final_g)))
    ms = dict(zip(WEIGHT_NAMES, (m_meta_tokens, m_norm_mix_g, m_w_in, m_conv_w, m_conv_b, m_conv_ln_g, m_conv_ln_b,
                                 m_gla_w_gate2, m_gla_gate_b, m_gla_norm_g, m_w_out, m_norm_ffn_g, m_w_ffn_gate,
                                 m_w_ffn_up, m_w_ffn_down, m_norm_final_g)))
    vs = dict(zip(WEIGHT_NAMES, (v_meta_tokens, v_norm_mix_g, v_w_in, v_conv_w, v_conv_b, v_conv_ln_g, v_conv_ln_b,
                                 v_gla_w_gate2, v_gla_gate_b, v_gla_norm_g, v_w_out, v_norm_ffn_g, v_w_ffn_gate,
                                 v_w_ffn_up, v_w_ffn_down, v_norm_final_g)))
    c = lax.axis_index("c")
    mine = 2 * lax.axis_index("x") + lax.axis_index("y")
    shard = lambda d, name: d[name].reshape(d[name].shape[-2:])
    vec = {name: ws[name].reshape(1, -1) for name, _, _, _ in SMALL_PARTS}
    n_ex, seq, _ = x.shape
    lp = HEAD_ROWS + seq
    t = n_ex * lp

    (tgt, h0, gate_s, up_s, down_s, out_s), (w_in_g, meta_g, conv_w_g, w2_g) = _pad_head_rows(
        [loss_target, x],
        [shard(ws, "w_ffn_gate").T, shard(ws, "w_ffn_up").T, shard(ws, "w_ffn_down")], [shard(ws, "w_out")],
        plan=_gather_plan([shard(ws, "w_in").T.astype(BF16)],
                          [shard(ws, "meta_tokens"), shard(ws, "conv_w"), shard(ws, "gla_w_gate2")], axes=[1]))
    w_in_t = jnp.concatenate([w_in_g.reshape(D_IN, D), jnp.zeros((D_IN_PAD - D_IN, D), BF16)], axis=0)
    conv_w_full = jnp.concatenate([_columns(conv_w_g), jnp.zeros((32 - CONV_W, C_CONV), F32)], axis=0)
    w2_full = jnp.concatenate([_columns(w2_g), jnp.zeros((128 - RANK, GLA_K), F32)], axis=0).astype(BF16)
    h0 = _set_meta_rows(h0, _columns(meta_g)).reshape(t, D)
    tgt = tgt.reshape(t, D)
    row_mask = jnp.concatenate([jnp.zeros((n_ex, HEAD_ROWS, 1), F32), jnp.ones((n_ex, seq, 1), F32)],
                               axis=1).reshape(t, 1)

    (u, hn), (gate_g,) = _in_proj(h0, vec["norm_mix_g"], w_in_t, plan=_gather_plan([gate_s]))
    (yc, y_conv), (up_g, w_out_g) = _conv_fwd(
        u, conv_w_full, vec["conv_b"], vec["conv_ln_g"], vec["conv_ln_b"], n_ex, lp,
        plan=_gather_plan([up_s, out_s]))
    (y_gla, states), _ = _gla_fwd(u, w2_full, vec["gla_gate_b"], vec["gla_norm_g"], n_ex, lp)
    w_gate_t, w_up_t = gate_g.reshape(D_FF, D), up_g.reshape(D_FF, D)
    (h1, hn2, gate, up, act), (down_g,) = _mix_out_ffn_up(
        h0, y_conv, y_gla, w_out_g, vec["norm_ffn_g"], w_gate_t, w_up_t,
        plan=_gather_plan([down_s]))
    w_down_full = down_g.reshape(D_FF, D)
    dh2, loss, d_final_g = _ffn_down_loss(act, w_down_full, h1, tgt, vec["norm_final_g"], row_mask)
    dgate, dup, dh1, dycat, d_ffn_g = _ffn_bwd(dh2, gate, up, h1, w_down_full.T, w_gate_t, w_up_t, w_out_g,
                                                vec["norm_ffn_g"])

    ffn_block = lambda g: g.reshape(N_CHIPS, D_FF // N_CHIPS, D)
    g_gate = ffn_block(_wgrad(dgate, hn2, "wgrad_gate")[0])
    g_up, (gate_sib,) = _wgrad(dup, hn2, "wgrad_up", _to_sibling_plan([g_gate]))
    g_up = ffn_block(g_up)
    g_down, (up_sib,) = _wgrad(act, dh2, "wgrad_down", _to_sibling_plan([g_up]))
    g_down = ffn_block(g_down)
    g_out = _wgrad_pair(y_conv, y_gla, dh1, "wgrad_out").reshape(N_CHIPS, D // N_CHIPS, D)
    cs_gate, cs_up = _rs_add_halves([(g_gate, gate_sib), (g_up, up_sib)], c, "rs_add_gate_up")
    (du_conv, d_conv_w, d_conv_b, d_ln_g, d_ln_b), (ex_gate, ex_up, down_sib, out_sib) = _conv_bwd(
        dycat, yc, u, conv_w_full, vec["conv_ln_g"], vec["conv_ln_b"], n_ex, lp,
        plan=_merge_plans(_chip_exchange_plan([cs_gate, cs_up]), _to_sibling_plan([g_down, g_out])))
    cs_down, cs_out = _rs_add_halves([(g_down, down_sib), (g_out, out_sib)], c, "rs_add_down_out")
    (du_gla, d_w2, d_gate_b, d_norm_g), (ex_down, ex_out) = _gla_bwd(
        dycat, u, states, w2_full, vec["gla_gate_b"], vec["gla_norm_g"], n_ex, lp,
        plan=_chip_exchange_plan([cs_down, cs_out]))
    halves = _rs_sum([(cs_gate, ex_gate), (cs_up, ex_up), (cs_down, ex_down), (cs_out, ex_out)], mine,
                     "rs_sum_early")

    small = {"norm_mix_g": jnp.zeros((1, D), F32), "norm_ffn_g": d_ffn_g, "norm_final_g": d_final_g,
             "conv_b": d_conv_b, "conv_ln_g": d_ln_g, "conv_ln_b": d_ln_b, "gla_gate_b": d_gate_b,
             "gla_norm_g": d_norm_g}
    part = lax.dynamic_update_slice(_pack_small(small), loss[:, :1], (LOSS_ROW, 0))
    part = jnp.concatenate([part, jnp.zeros((N_META, D), F32), d_conv_w.reshape(16, D), d_w2[:RANK].reshape(4, D),
                            jnp.zeros((4, D), F32)], axis=0)
    g_in_conv = _wgrad(du_conv, hn, "wgrad_in_conv")[0][None]
    g_in_gla, (slots, conv_sib) = _wgrad(du_gla, hn, "wgrad_in_gla",
                                         _merge_plans(_all_to_all_plan(part), _to_sibling_plan([g_in_conv])))
    pieces = [g_in_conv, g_in_gla[None]]
    (gla_sib,) = _exchange(_to_sibling_plan(pieces[1:]), "rs_late_to_sibling")
    sums = _rs_add_halves(list(zip(pieces, (conv_sib, gla_sib))), c, "rs_add_w_in")
    in_chip_sum = jnp.concatenate([sums[0][0], sums[1][0]], axis=0)[:D_IN].reshape(N_CHIPS, D_IN // N_CHIPS, D // 2)
    (dh0, d_mix_g), shared = _in_proj_bwd(
        du_conv, du_gla, w_in_t[:2 * C_CONV], w_in_t[2 * C_CONV:], h0, dh1, vec["norm_mix_g"],
        plan=_merge_plans(_share_plan(halves), _chip_exchange_plan([in_chip_sum])))
    dh0 = dh0.reshape(n_ex, lp, D)
    grad_x = dh0[:, HEAD_ROWS:]
    late_part = jnp.concatenate([d_mix_g, jnp.zeros((SMALL_ROWS - 1, D), F32),
                                 jnp.sum(dh0[:, PAD_ROWS:HEAD_ROWS], axis=0)], axis=0)
    (in_half,) = _rs_sum([(in_chip_sum, shared[4])], mine, "rs_sum_w_in")
    in_shared, late_slots = _exchange(_merge_plans(_share_plan([in_half]), _all_to_all_plan(late_part)),
                                      "late_exchange")

    out = {"grad": {}, "delta": {}, "new_m": {}, "new_v": {}}

    def record(name, res, transposed=False):
        for kind, a in zip(("grad", "delta", "new_m", "new_v"), res):
            out[kind][name] = (a.T if transposed else a).reshape(ws[name].shape)

    def operands(name, transposed):
        lay = (lambda a: a.T) if transposed else (lambda a: a)
        return lay(shard(ws, name)), lay(shard(ms, name)), lay(shard(vs, name))

    early_layout = (("w_ffn_gate", True), ("w_ffn_up", True), ("w_ffn_down", False), ("w_out", False))
    items = [(mine_half, their_half, *operands(name, transposed))
             for (name, transposed), mine_half, their_half in zip(early_layout, halves, shared)]
    for (name, transposed), res in zip(early_layout, _adamw_halves(items, c, "adamw_early")):
        record(name, res, transposed)

    tile_rows = lambda a: a.reshape(a.shape[0], 1, a.shape[1])
    by_output = lambda d: jnp.transpose(d["w_in"], (2, 0, 1))
    res = _adamw_halves([(tile_rows(in_half), tile_rows(in_shared), by_output(ws), by_output(ms), by_output(vs))],
                        c, "adamw_w_in")[0]
    for kind, a in zip(("grad", "delta", "new_m", "new_v"), res):
        out[kind]["w_in"] = jnp.transpose(a, (1, 2, 0))

    flat = lambda d, name: d[name].reshape(1, -1)
    g_s, updated = _sum_slots_adamw(slots, late_slots,
                                    [(flat(ws, name), flat(ms, name), flat(vs, name)) for name, _, _, _ in SMALL_PARTS])
    for (name, _, _, _), res in zip(SMALL_PARTS, updated):
        record(name, res)
    loss = g_s[LOSS_ROW, 0]
    block = lambda a, width: lax.dynamic_slice_in_dim(a, mine * width, width, axis=1)
    small_sharded = {"meta_tokens": block(g_s[8:24], D // N_CHIPS),
                     "conv_w": block(g_s[24:40].reshape(32, C_CONV), C_CONV // N_CHIPS)[:CONV_W],
                     "gla_w_gate2": block(g_s[40:44].reshape(RANK, GLA_K), GLA_K // N_CHIPS)}
    for name, g in small_sharded.items():
        record(name, [g, *_adamw(g, *operands(name, False), "adamw_" + name)])

    return (loss, grad_x, *[out[kind][name] for kind in ("grad", "delta", "new_m", "new_v") for name in WEIGHT_NAMES])
```

```python
import functools
from typing import Any, Callable, NamedTuple, Sequence

import jax
import jax.numpy as jnp
from jax import lax
from jax.experimental import pallas as pl
from jax.experimental.pallas import tpu as pltpu

F32 = jnp.float32
BF16 = jnp.bfloat16
MESH = pl.DeviceIdType.MESH

D = 1024
N_META = 16
C_CONV = 512
CONV_W = 31
GLA_K = 256
GLA_V = 512
N_HEADS = 4
DK = 64
DV = 128
RANK = 16
CHUNK = 64
PAD_ROWS = CHUNK - N_META
HEAD_ROWS = CHUNK
D_IN = 2576
D_IN_PAD = 2688
D_GLA_IN = D_IN_PAD - 2 * C_CONV
D_FF = 2816
RMS_EPS = 1e-6
LN_EPS = 1e-5
GATE_TAU = 16.0
N_CHIPS = 4

ADAM_LR = 0.001
ADAM_B1 = 0.9
ADAM_B2 = 0.999
ADAM_EPS = 1e-08
ADAM_WD = 0.01
ADAM_STEP = 10

V7X_VMEM_BYTES = 64 * 1024 * 1024
VMEM_LIMIT = V7X_VMEM_BYTES - 8 * 1024 * 1024
SUBLANES = 8
ROW_PART = 128
FFN_BWD_TILE = 192

WEIGHT_NAMES = ("meta_tokens", "norm_mix_g", "w_in", "conv_w", "conv_b", "conv_ln_g", "conv_ln_b", "gla_w_gate2",
                "gla_gate_b", "gla_norm_g", "w_out", "norm_ffn_g", "w_ffn_gate", "w_ffn_up", "w_ffn_down",
                "norm_final_g")

SMALL_ROWS = 8
SMALL_PARTS = (("norm_mix_g", 0, 0, D), ("norm_ffn_g", 1, 0, D), ("norm_final_g", 2, 0, D),
               ("conv_b", 3, 0, C_CONV), ("conv_ln_g", 3, C_CONV, C_CONV), ("conv_ln_b", 4, 0, C_CONV),
               ("gla_gate_b", 4, C_CONV, GLA_K), ("gla_norm_g", 4, C_CONV + GLA_K, DV))
LOSS_ROW = 5

HBM_SPEC = pl.BlockSpec(memory_space=pltpu.HBM)


def _dot(a, b):
    return jnp.dot(a, b, preferred_element_type=F32)


def _dot_nt(a, b):
    return lax.dot_general(a, b, (((1,), (1,)), ((), ())), preferred_element_type=F32)


def _dot_tn(a, b):
    return lax.dot_general(a, b, (((0,), (0,)), ((), ())), preferred_element_type=F32)


def _sigmoid(x):
    return 1.0 / (1.0 + jnp.exp(-x))


def _const_spec(shape):
    return pl.BlockSpec(shape, lambda *_: (0,) * len(shape), pipeline_mode=pl.Buffered(1))


def _acc_spec(shape):
    return pl.BlockSpec(shape, lambda *_: (0,) * len(shape))


def _params(n_axes):
    return pltpu.CompilerParams(dimension_semantics=("arbitrary",) * n_axes, vmem_limit_bytes=VMEM_LIMIT)


def _row_tile(t, want):
    for r in (want, 384, 192, 128, 64):
        if r <= want and t % r == 0:
            return r
    raise ValueError(f"no row tile for {t}")


def _row_parts(r):
    if r % ROW_PART:
        return [slice(None)]
    return [pl.ds(i * ROW_PART, ROW_PART) for i in range(r // ROW_PART)]


def _in_lockstep(bodies):
    live = list(bodies)
    while live:
        still = []
        for g in live:
            try:
                next(g)
                still.append(g)
            except StopIteration:
                pass
        live = still


class _Plan(NamedTuple):
    arrays: Sequence[Any]
    out_shape: Sequence[Any]
    sems: Sequence[Any]
    make: Callable


def _phases(made):
    return made if len(made) == 3 else (made[0], lambda: None, made[1])


def _call(body, *, name, grid, in_specs, out_specs, out_shape, scratch_shapes=(), plan=None):
    n_in, n_out, n_scr = len(in_specs), len(out_specs), len(scratch_shapes)
    if plan is None:
        plan = _Plan([], [], [], lambda ins, outs, sems: (lambda: None, lambda: None))
    nx_in, nx_out = len(plan.arrays), len(plan.out_shape)
    n_steps = functools.reduce(lambda a, b: a * b, grid)

    def hosted(*refs):
        ins, xins = refs[:n_in], refs[n_in:n_in + nx_in]
        o0 = n_in + nx_in
        outs, xouts = refs[o0:o0 + n_out], refs[o0 + n_out:o0 + n_out + nx_out]
        s0 = o0 + n_out + nx_out
        scr, sems = refs[s0:s0 + n_scr], refs[s0 + n_scr:]
        step = functools.reduce(lambda acc, a: acc * grid[a] + pl.program_id(a), range(len(grid)), 0)
        start, relay, finish = _phases(plan.make(xins, xouts, sems))
        pl.when(step == 0)(start)
        pl.when(step == n_steps - 1)(relay)
        body(*ins, *outs, *scr)
        pl.when(step == n_steps - 1)(finish)

    call = pl.pallas_call(
        hosted, name=name, grid=grid, in_specs=list(in_specs) + [HBM_SPEC] * nx_in,
        out_specs=list(out_specs) + [HBM_SPEC] * nx_out, out_shape=list(out_shape) + list(plan.out_shape),
        scratch_shapes=list(scratch_shapes) + list(plan.sems),
        compiler_params=pltpu.CompilerParams(dimension_semantics=("arbitrary",) * len(grid),
                                             vmem_limit_bytes=VMEM_LIMIT, has_side_effects=nx_in > 0))

    def run(*args):
        res = call(*args, *plan.arrays)
        return res[:n_out], res[n_out:]

    return run


def _pad_head_rows(arrays, casts, transposed_casts, plan=None):
    n_ex, seq, _ = arrays[0].shape
    nc = (HEAD_ROWS + seq) // CHUNK
    n, k = len(arrays), len(casts)
    kt = k + len(transposed_casts)

    def body(*refs):
        ins, outs = refs[:n + kt], refs[n + kt:]
        for a_ref, o_ref in zip(ins[:n], outs[:n]):
            o_ref[...] = jnp.where(pl.program_id(0) > 0, a_ref[...], 0.0)

        @pl.when(pl.program_id(0) == 0)
        def _():
            for a_ref, o_ref in zip(ins[n:n + k], outs[n:n + k]):
                o_ref[...] = a_ref[...].astype(BF16)
            for a_ref, o_ref in zip(ins[n + k:], outs[n + k:]):
                o_ref[...] = a_ref[...].T.astype(BF16)

    whole = lambda shape: pl.BlockSpec(shape, lambda i: (0, 0))
    cast_shapes = [a.shape for a in casts] + [a.shape[::-1] for a in transposed_casts]
    return _call(
        body, name="pad_head_rows", grid=(nc,),
        in_specs=([pl.BlockSpec((n_ex, CHUNK, D), lambda i: (0, jnp.maximum(i - 1, 0), 0))] * n
                  + [_const_spec(a.shape) for a in (*casts, *transposed_casts)]),
        out_specs=[pl.BlockSpec((n_ex, CHUNK, D), lambda i: (0, i, 0))] * n + [whole(s) for s in cast_shapes],
        out_shape=([jax.ShapeDtypeStruct((n_ex, HEAD_ROWS + seq, D), F32)] * n
                   + [jax.ShapeDtypeStruct(s, BF16) for s in cast_shapes]),
        plan=plan,
    )(*arrays, *casts, *transposed_casts)


def _set_meta_rows(h0, meta):
    n_ex = h0.shape[0]

    def body(h_ref, meta_ref, o_ref):
        o_ref[...] = jnp.concatenate(
            [h_ref[:, :PAD_ROWS, :], jnp.broadcast_to(meta_ref[...][None], (n_ex, N_META, D))], axis=1)

    head = pl.BlockSpec((n_ex, HEAD_ROWS, D), lambda i: (0, 0, 0))
    return pl.pallas_call(
        body, name="set_meta_rows", grid=(1,), in_specs=[head, pl.BlockSpec((N_META, D), lambda i: (0, 0))],
        out_specs=head, out_shape=jax.ShapeDtypeStruct(h0.shape, F32), input_output_aliases={0: 0},
        compiler_params=_params(1),
    )(h0, meta)


def _in_proj(h0, g_mix, w_in_blocks, plan=None):
    t = h0.shape[0]
    r = _row_tile(t, 384)
    n_blocks, rows_block, _ = w_in_blocks.shape

    def body(h_ref, g_ref, w_ref, u_ref, hn_ref, wt_ref):
        @pl.when(pl.program_id(0) == 0)
        def _():
            for j in range(n_blocks):
                wt_ref[j * rows_block:(j + 1) * rows_block, :] = w_ref[j]
            wt_ref[n_blocks * rows_block:, :] = jnp.zeros((D_IN_PAD - n_blocks * rows_block, D), BF16)

        h = h_ref[...]
        rstd = lax.rsqrt(jnp.mean(h * h, axis=-1, keepdims=True) + RMS_EPS)
        hn = (h * rstd * g_ref[...]).astype(BF16)
        hn_ref[...] = hn
        u_ref[...] = _dot_nt(hn, wt_ref[...])

    return _call(
        body, name="in_proj", grid=(t // r,),
        in_specs=[pl.BlockSpec((r, D), lambda i: (i, 0)), _const_spec((1, D)), _const_spec(w_in_blocks.shape)],
        out_specs=[pl.BlockSpec((r, D_IN_PAD), lambda i: (i, 0)), pl.BlockSpec((r, D), lambda i: (i, 0)),
                   _acc_spec((D_IN_PAD, D))],
        out_shape=[jax.ShapeDtypeStruct((t, D_IN_PAD), F32), jax.ShapeDtypeStruct((t, D), BF16),
                   jax.ShapeDtypeStruct((D_IN_PAD, D), BF16)],
        plan=plan,
    )(h0, g_mix, w_in_blocks)


CONV_TILE = 192
CONV_SUB = 32
CONV_LEAD = CONV_SUB - (CONV_W - 1)


def _shifted_copies(src, dst, r):
    for s in range(1, SUBLANES):
        dst[s - 1] = src[s:s + r + CONV_SUB - SUBLANES, :]


def _shifted_rows(src, shifted, start):
    base, s = SUBLANES * (start // SUBLANES), start % SUBLANES
    if s == 0:
        return src[base:base + CONV_SUB, :]
    return shifted[s - 1, base:base + CONV_SUB, :]


def _conv_fwd(u, conv_w, conv_b, ln_g, ln_b, n_ex, lp, plan=None):
    r = CONV_TILE
    nt = lp // r
    hb = r // CONV_SUB

    def body(cur_ref, prev_ref, w_ref, b_ref, lg_ref, lb_ref, yc_ref, y_ref, glu, glu_sh):
        i = pl.program_id(1)
        cur = cur_ref[...]
        glu[CONV_SUB:CONV_SUB + r, :] = cur[:, :C_CONV] * _sigmoid(cur[:, C_CONV:])
        pv = prev_ref[...]
        halo = pv[:, :C_CONV] * _sigmoid(pv[:, C_CONV:])
        glu[0:CONV_SUB, :] = jnp.where(i > 0, halo, 0.0)
        _shifted_copies(glu, glu_sh, r)
        w = w_ref[...]
        for j in range(r // CONV_SUB):
            r0 = j * CONV_SUB
            acc = jnp.zeros((CONV_SUB, C_CONV), F32) + b_ref[...]
            for k in range(CONV_W):
                acc = acc + w[k:k + 1, :] * _shifted_rows(glu, glu_sh, r0 + CONV_LEAD + k)
            mu = jnp.mean(acc, axis=-1, keepdims=True)
            cen = acc - mu
            var = jnp.mean(cen * cen, axis=-1, keepdims=True)
            out = cen * lax.rsqrt(var + LN_EPS) * lg_ref[...] + lb_ref[...]
            y = out * _sigmoid(out)
            row = i * r + r0 + lax.broadcasted_iota(jnp.int32, (CONV_SUB, 1), 0)
            y = jnp.where(row >= PAD_ROWS, y, 0.0)
            yc_ref[r0:r0 + CONV_SUB, :] = acc
            y_ref[r0:r0 + CONV_SUB, :] = y.astype(BF16)

    t = n_ex * lp
    return _call(
        body, name="conv_fwd", grid=(n_ex, nt),
        in_specs=[pl.BlockSpec((r, 2 * C_CONV), lambda b, i: (b * nt + i, 0)),
                  pl.BlockSpec((CONV_SUB, 2 * C_CONV), lambda b, i: (jnp.maximum((b * nt + i) * hb - 1, 0), 0)),
                  _const_spec((32, C_CONV)), _const_spec((1, C_CONV)), _const_spec((1, C_CONV)), _const_spec((1, C_CONV))],
        out_specs=[pl.BlockSpec((r, C_CONV), lambda b, i: (b * nt + i, 0)),
                   pl.BlockSpec((r, C_CONV), lambda b, i: (b * nt + i, 0))],
        out_shape=[jax.ShapeDtypeStruct((t, C_CONV), F32), jax.ShapeDtypeStruct((t, C_CONV), BF16)],
        scratch_shapes=[pltpu.VMEM((r + CONV_SUB, C_CONV), F32),
                        pltpu.VMEM((SUBLANES - 1, r + CONV_SUB - SUBLANES, C_CONV), F32)],
        plan=plan,
    )(u, u, conv_w, conv_b, ln_g, ln_b)


def _mix_out_ffn_up(h0, y_conv, y_gla, w_out_t, g_ffn, w_gate_t, w_up_t, plan=None):
    t = h0.shape[0]
    r = _row_tile(t, 384)
    wb = D // N_CHIPS

    def body(h0_ref, yc_ref, yg_ref, wo_ref, g_ref, wg_ref, wu_ref, h1_ref, hn_ref, gate_ref, up_ref, act_ref):
        h1 = h0_ref[...]
        for j in range(N_CHIPS):
            y_ref, col = (yc_ref, j * wb) if j * wb < C_CONV else (yg_ref, j * wb - C_CONV)
            h1 = h1 + _dot_nt(y_ref[:, col:col + wb], wo_ref[j])
        h1_ref[...] = h1
        rstd = lax.rsqrt(jnp.mean(h1 * h1, axis=-1, keepdims=True) + RMS_EPS)
        hn = (h1 * rstd * g_ref[...]).astype(BF16)
        hn_ref[...] = hn
        gate = _dot_nt(hn, wg_ref[...])
        up = _dot_nt(hn, wu_ref[...])
        gate_ref[...] = gate
        up_ref[...] = up
        act_ref[...] = (gate * _sigmoid(gate) * up).astype(BF16)

    rows = lambda w: pl.BlockSpec((r, w), lambda i: (i, 0))
    return _call(
        body, name="mix_out_ffn_up", grid=(t // r,),
        in_specs=[rows(D), rows(C_CONV), rows(GLA_V), _const_spec((N_CHIPS, D, wb)), _const_spec((1, D)),
                  _const_spec((D_FF, D)), _const_spec((D_FF, D))],
        out_specs=[rows(D), rows(D), rows(D_FF), rows(D_FF), rows(D_FF)],
        out_shape=[jax.ShapeDtypeStruct((t, D), F32), jax.ShapeDtypeStruct((t, D), BF16),
                   jax.ShapeDtypeStruct((t, D_FF), F32), jax.ShapeDtypeStruct((t, D_FF), F32),
                   jax.ShapeDtypeStruct((t, D_FF), BF16)],
        plan=plan,
    )(h0, y_conv, y_gla, w_out_t, g_ffn, w_gate_t, w_up_t)


def _ffn_down_loss(act, w_down, h1, target, g_final, row_mask):
    t = h1.shape[0]
    r = _row_tile(t, 384)

    def body(act_ref, wd_ref, h1_ref, tgt_ref, gf_ref, mask_ref, dh2_ref, loss_ref, dgf_ref):
        @pl.when(pl.program_id(0) == 0)
        def _():
            loss_ref[...] = jnp.zeros_like(loss_ref)
            dgf_ref[...] = jnp.zeros_like(dgf_ref)

        gf = gf_ref[...]

        def part(rows):
            h2 = h1_ref[rows, :] + _dot(act_ref[rows, :], wd_ref[...])
            yield
            rstd = lax.rsqrt(jnp.mean(h2 * h2, axis=-1, keepdims=True) + RMS_EPS)
            nrm = h2 * rstd
            err = (nrm * gf - tgt_ref[rows, :]) * mask_ref[rows, :]
            loss_ref[...] += jnp.sum(err * err) * (0.5 / D)
            dy = err * (1.0 / D)
            dgf_ref[...] += jnp.sum(dy * nrm, axis=0, keepdims=True)
            dn = dy * gf
            dh2_ref[rows, :] = rstd * (dn - nrm * jnp.mean(dn * nrm, axis=-1, keepdims=True))

        _in_lockstep(part(rows) for rows in _row_parts(r))

    rows = lambda w: pl.BlockSpec((r, w), lambda i: (i, 0))
    return pl.pallas_call(
        body, name="ffn_down_loss", grid=(t // r,),
        in_specs=[rows(D_FF), _const_spec((D_FF, D)), rows(D), rows(D), _const_spec((1, D)), rows(1)],
        out_specs=[rows(D), _acc_spec((1, 128)), _acc_spec((1, D))],
        out_shape=[jax.ShapeDtypeStruct((t, D), F32), jax.ShapeDtypeStruct((1, 128), F32),
                   jax.ShapeDtypeStruct((1, D), F32)],
        compiler_params=_params(1),
    )(act, w_down, h1, target, g_final, row_mask)


def _ffn_bwd(dh2, gate, up, h1, w_down_t, w_gate_t, w_up_t, w_out_t, g_ffn):
    t = h1.shape[0]
    r = _row_tile(t, FFN_BWD_TILE)
    wb = D // N_CHIPS

    def body(dh2_ref, gate_ref, up_ref, h1_ref, wd_ref, wg_ref, wu_ref, wo_ref, g_ref,
             dgate_ref, dup_ref, dh1_ref, dycat_ref, dg_ref):
        @pl.when(pl.program_id(0) == 0)
        def _():
            dg_ref[...] = jnp.zeros_like(dg_ref)

        dh2 = dh2_ref[...]
        dact = _dot(dh2.astype(BF16), wd_ref[...])
        gate = gate_ref[...]
        sg = _sigmoid(gate)
        dgate = (dact * up_ref[...] * (sg * (1.0 + gate * (1.0 - sg)))).astype(BF16)
        dup = (dact * (gate * sg)).astype(BF16)
        dgate_ref[...] = dgate
        dup_ref[...] = dup
        dhn = _dot(dgate, wg_ref[...]) + _dot(dup, wu_ref[...])
        h1 = h1_ref[...]
        rstd = lax.rsqrt(jnp.mean(h1 * h1, axis=-1, keepdims=True) + RMS_EPS)
        nrm = h1 * rstd
        dg_ref[...] += jnp.sum(dhn * nrm, axis=0, keepdims=True)
        dn = dhn * g_ref[...]
        dh1 = dh2 + rstd * (dn - nrm * jnp.mean(dn * nrm, axis=-1, keepdims=True))
        dh1_ref[...] = dh1
        dh1 = dh1.astype(BF16)
        for j in range(N_CHIPS):
            dycat_ref[:, j * wb:(j + 1) * wb] = _dot(dh1, wo_ref[j])

    rows = lambda w: pl.BlockSpec((r, w), lambda i: (i, 0))
    return pl.pallas_call(
        body, name="ffn_bwd", grid=(t // r,),
        in_specs=[rows(D), rows(D_FF), rows(D_FF), rows(D), _const_spec((D, D_FF)), _const_spec((D_FF, D)),
                  _const_spec((D_FF, D)), _const_spec((N_CHIPS, D, wb)), _const_spec((1, D))],
        out_specs=[rows(D_FF), rows(D_FF), rows(D), rows(D), _acc_spec((1, D))],
        out_shape=[jax.ShapeDtypeStruct((t, D_FF), BF16), jax.ShapeDtypeStruct((t, D_FF), BF16),
                   jax.ShapeDtypeStruct((t, D), F32), jax.ShapeDtypeStruct((t, D), F32),
                   jax.ShapeDtypeStruct((1, D), F32)],
        compiler_params=_params(1),
    )(dh2, gate, up, h1, w_down_t, w_gate_t, w_up_t, w_out_t, g_ffn)


def _conv_bwd(dycat, yc, u, conv_w, ln_g, ln_b, n_ex, lp, plan=None):
    r = CONV_TILE
    nt = lp // r
    hb = r // CONV_SUB
    nsub = r // CONV_SUB

    def ln_bwd(dy, yc_rows, live, lg, lb):
        mu = jnp.mean(yc_rows, axis=-1, keepdims=True)
        cen = yc_rows - mu
        rs = lax.rsqrt(jnp.mean(cen * cen, axis=-1, keepdims=True) + LN_EPS)
        yn = cen * rs
        out = yn * lg + lb
        so = _sigmoid(out)
        dout = jnp.where(live, dy * (so * (1.0 + out * (1.0 - so))), 0.0)
        dyn = dout * lg
        dyc = rs * (dyn - jnp.mean(dyn, axis=-1, keepdims=True) - yn * jnp.mean(dyn * yn, axis=-1, keepdims=True))
        return dyc, dout, yn

    def body(dy_ref, dyn_ref, yc_ref, ycn_ref, cur_ref, prev_ref, w_ref, lg_ref, lb_ref,
             du_ref, dw_ref, db_ref, dlg_ref, dlb_ref, glu, dycs, dwacc, glu_sh, dycs_sh):
        b = pl.program_id(0)
        i = pl.program_id(1)
        first = jnp.logical_and(b == 0, i == 0)

        @pl.when(first)
        def _():
            dwacc[...] = jnp.zeros_like(dwacc)
            db_ref[...] = jnp.zeros_like(db_ref)
            dlg_ref[...] = jnp.zeros_like(dlg_ref)
            dlb_ref[...] = jnp.zeros_like(dlb_ref)

        lg, lb = lg_ref[...], lb_ref[...]
        cur = cur_ref[...]
        sig = _sigmoid(cur[:, C_CONV:])
        glu[CONV_SUB:CONV_SUB + r, :] = cur[:, :C_CONV] * sig
        pv = prev_ref[...]
        glu[0:CONV_SUB, :] = jnp.where(i > 0, pv[:, :C_CONV] * _sigmoid(pv[:, C_CONV:]), 0.0)

        row = i * r + lax.broadcasted_iota(jnp.int32, (r, 1), 0)
        dyc, dout, yn = ln_bwd(dy_ref[...], yc_ref[...], row >= PAD_ROWS, lg, lb)
        dycs[0:r, :] = dyc
        dycn, _, _ = ln_bwd(dyn_ref[...], ycn_ref[...], i < nt - 1, lg, lb)
        dycs[r:r + CONV_SUB, :] = dycn
        db_ref[...] += jnp.sum(dyc, axis=0, keepdims=True)
        dlg_ref[...] += jnp.sum(dout * yn, axis=0, keepdims=True)
        dlb_ref[...] += jnp.sum(dout, axis=0, keepdims=True)

        _shifted_copies(glu, glu_sh, r)
        _shifted_copies(dycs, dycs_sh, r)
        w = w_ref[...]
        for j in range(nsub):
            r0 = j * CONV_SUB
            dblk = dycs[r0:r0 + CONV_SUB, :]
            dglu = jnp.zeros((CONV_SUB, C_CONV), F32)
            for k in range(CONV_W):
                dglu = dglu + w[k:k + 1, :] * _shifted_rows(dycs, dycs_sh, r0 + (CONV_W - 1) - k)
                prod = dblk * _shifted_rows(glu, glu_sh, r0 + CONV_LEAD + k)
                dwacc[k] += prod.reshape(CONV_SUB // SUBLANES, SUBLANES, C_CONV).sum(axis=0)
            sg = sig[r0:r0 + CONV_SUB, :]
            cv = cur[r0:r0 + CONV_SUB, :C_CONV]
            du_ref[r0:r0 + CONV_SUB, :C_CONV] = (dglu * sg).astype(BF16)
            du_ref[r0:r0 + CONV_SUB, C_CONV:] = (dglu * cv * sg * (1.0 - sg)).astype(BF16)

        @pl.when(jnp.logical_and(b == n_ex - 1, i == nt - 1))
        def _():
            dw_ref[...] = jnp.sum(dwacc[...], axis=1)

    t = n_ex * lp
    cur_rows = lambda w, col: pl.BlockSpec((r, w), lambda b, i: (b * nt + i, col))
    nxt_rows = lambda w, col: pl.BlockSpec(
        (CONV_SUB, w), lambda b, i: (jnp.minimum((b * nt + i + 1) * hb, n_ex * nt * hb - 1), col))
    return _call(
        body, name="conv_bwd", grid=(n_ex, nt),
        in_specs=[cur_rows(C_CONV, 0), nxt_rows(C_CONV, 0), cur_rows(C_CONV, 0), nxt_rows(C_CONV, 0),
                  cur_rows(2 * C_CONV, 0),
                  pl.BlockSpec((CONV_SUB, 2 * C_CONV), lambda b, i: (jnp.maximum((b * nt + i) * hb - 1, 0), 0)),
                  _const_spec((32, C_CONV)), _const_spec((1, C_CONV)), _const_spec((1, C_CONV))],
        out_specs=[cur_rows(2 * C_CONV, 0), _acc_spec((32, C_CONV)), _acc_spec((1, C_CONV)),
                   _acc_spec((1, C_CONV)), _acc_spec((1, C_CONV))],
        out_shape=[jax.ShapeDtypeStruct((t, 2 * C_CONV), BF16), jax.ShapeDtypeStruct((32, C_CONV), F32),
                   jax.ShapeDtypeStruct((1, C_CONV), F32), jax.ShapeDtypeStruct((1, C_CONV), F32),
                   jax.ShapeDtypeStruct((1, C_CONV), F32)],
        scratch_shapes=[pltpu.VMEM((r + CONV_SUB, C_CONV), F32), pltpu.VMEM((r + CONV_SUB, C_CONV), F32),
                        pltpu.VMEM((32, SUBLANES, C_CONV), F32),
                        pltpu.VMEM((SUBLANES - 1, r + CONV_SUB - SUBLANES, C_CONV), F32),
                        pltpu.VMEM((SUBLANES - 1, r + CONV_SUB - SUBLANES, C_CONV), F32)],
        plan=plan,
    )(dycat, dycat, yc, yc, u, u, conv_w, ln_g, ln_b)


HEAD_ROWS_ALL = N_HEADS * CHUNK


def _gla_gates(lr, w2, gb, first_chunk):
    z = _dot(lr.astype(BF16), w2) + gb
    a = (jnp.minimum(z, 0.0) - jnp.log(1.0 + jnp.exp(-jnp.abs(z)))) * (1.0 / GATE_TAU)
    row = lax.broadcasted_iota(jnp.int32, (CHUNK, 1), 0)
    live = jnp.logical_or(jnp.logical_not(first_chunk), row >= PAD_ROWS)
    return z, jnp.where(live, a, 0.0), live


def _tri(lower):
    i = lax.broadcasted_iota(jnp.int32, (CHUNK, CHUNK), 0)
    j = lax.broadcasted_iota(jnp.int32, (CHUNK, CHUNK), 1)
    return (i >= j) if lower else (i <= j)


def _head_of(shape, axis, per_head):
    return lax.broadcasted_iota(jnp.int32, shape, axis) // per_head


def _expand(x, lanes_per_head):
    rows, lanes = HEAD_ROWS_ALL, x.shape[1]
    keep = _head_of((rows, lanes), 0, CHUNK) == _head_of((rows, lanes), 1, lanes_per_head)
    return jnp.where(keep, jnp.tile(x, (N_HEADS, 1)), 0.0)


def _expand_lanes(x):
    rows, w = x.shape
    keep = _head_of((rows, N_HEADS * w), 0, CHUNK) == _head_of((rows, N_HEADS * w), 1, w)
    return jnp.where(keep, jnp.tile(x, (1, N_HEADS)), 0.0)


def _expand_state(st):
    rows, lanes = N_HEADS * DV, st.shape[1]
    keep = _head_of((rows, lanes), 0, DV) == _head_of((rows, lanes), 1, DK)
    return jnp.where(keep, jnp.tile(st, (N_HEADS, 1)), 0.0)


def _fold(t, rows_per_head):
    lane_head = _head_of((rows_per_head, t.shape[1]), 1, DK)
    out = jnp.where(lane_head == 0, t[0:rows_per_head], 0.0)
    for h in range(1, N_HEADS):
        out = out + jnp.where(lane_head == h, t[h * rows_per_head:(h + 1) * rows_per_head], 0.0)
    return out


def _rows_by_head(x):
    return jnp.concatenate([x[:, h * DV:(h + 1) * DV] for h in range(N_HEADS)], axis=0)


def _lanes_by_head(x):
    return jnp.concatenate([x[h * CHUNK:(h + 1) * CHUNK] for h in range(N_HEADS)], axis=1)


def _running_sum(a, lower):
    hi = a.astype(BF16)
    rest = a - hi.astype(F32)
    mid = rest.astype(BF16)
    lo = (rest - mid.astype(F32)).astype(BF16)
    w = a.shape[1]
    parts = _dot(_tri(lower).astype(F32).astype(BF16), jnp.concatenate([hi, mid, lo], axis=1))
    return parts[:, :w] + parts[:, w:2 * w] + parts[:, 2 * w:]


def _stacked_causal():
    i = lax.broadcasted_iota(jnp.int32, (HEAD_ROWS_ALL, CHUNK), 0) % CHUNK
    j = lax.broadcasted_iota(jnp.int32, (HEAD_ROWS_ALL, CHUNK), 1)
    return i >= j


GLA_GROUP = 3


def _gla_chunk(q, k, v, lr, w2, gb, first_chunk):
    z, a, live = _gla_gates(lr, w2, gb, first_chunk)
    yield
    b = _running_sum(a, True)
    yield
    bl = b[CHUNK - 1:CHUNK, :]
    e_pos, e_neg, e_dec = jnp.exp(b), jnp.exp(-b), jnp.exp(bl - b)
    q_f, k_f, kd_f = q * (DK ** -0.5) * e_pos, k * e_neg, k * e_dec
    qx = _expand(q_f, DK).astype(BF16)
    k_in, k_dec, v_b = k_f.astype(BF16), kd_f.astype(BF16), v.astype(BF16)
    s = jnp.where(_stacked_causal(), _dot_nt(qx, k_in), 0.0).astype(BF16)
    yield
    p = _dot(s, v_b)
    yield
    o_intra = jnp.concatenate([p[h * CHUNK:(h + 1) * CHUNK, h * DV:(h + 1) * DV] for h in range(N_HEADS)], axis=0)
    return dict(z=z, live=live, bl=bl, e_pos=e_pos, e_neg=e_neg, e_dec=e_dec, q_f=q_f, k_f=k_f, kd_f=kd_f,
                qx=qx, k_in=k_in, k_dec=k_dec, v_b=v_b, s=s, o_intra=o_intra, decay=jnp.exp(bl))


def _gla_fwd(u, w2, gb, ng, n_ex, lp, plan=None):
    nc = lp // CHUNK
    t = n_ex * lp
    rows_of = lambda j: pl.ds(j * CHUNK, CHUNK)

    def body(qk_ref, v_ref, g_ref, lr_ref, w2_ref, gb_ref, ng_ref, y_ref, st_ref, state):
        n = pl.program_id(0)

        @pl.when(n == 0)
        def _():
            state[...] = jnp.zeros_like(state)

        carried = [state[e] for e in range(n_ex)]

        def one_chunk(e, j):
            rows = rows_of(j)
            qk = qk_ref[e, rows, :]
            first = jnp.logical_and(n == 0, j == 0)
            c = yield from _gla_chunk(qk[:, :GLA_K], qk[:, GLA_K:], v_ref[e, rows, :], lr_ref[e, rows, :],
                                      w2_ref[...], gb_ref[...], first)
            kv = _fold(_dot_tn(c["v_b"], c["k_dec"]), DV)
            g = _rows_by_head(g_ref[e, rows, :])
            gate = ng_ref[...] * (g * _sigmoid(g))
            yield
            for _ in range(j):
                yield
            st = carried[e]
            st_ref[e, pl.ds(j * DV, DV), :] = st
            o = c["o_intra"] + _dot_nt(c["qx"], st.astype(BF16))
            rstd = lax.rsqrt(jnp.mean(o * o, axis=-1, keepdims=True) + RMS_EPS)
            y_ref[e, rows, :] = _lanes_by_head(o * rstd * gate).astype(BF16)
            carried[e] = c["decay"] * st + kv

        _in_lockstep(one_chunk(e, j) for j in range(GLA_GROUP) for e in range(n_ex))
        for e in range(n_ex):
            state[e] = carried[e]

    u3 = u.reshape(n_ex, lp, D_IN_PAD)
    blk = lambda w, col: pl.BlockSpec((n_ex, GLA_GROUP * CHUNK, w), lambda n: (0, n, col))
    (y, states), extra = _call(
        body, name="gla_fwd", grid=(nc // GLA_GROUP,),
        in_specs=[blk(2 * GLA_K, 2), blk(GLA_V, 3), blk(GLA_V, 4), blk(128, 20),
                  _const_spec((128, GLA_K)), _const_spec((1, GLA_K)), _const_spec((1, DV))],
        out_specs=[blk(GLA_V, 0), pl.BlockSpec((n_ex, GLA_GROUP * DV, GLA_K), lambda n: (0, n, 0))],
        out_shape=[jax.ShapeDtypeStruct((n_ex, lp, GLA_V), BF16),
                   jax.ShapeDtypeStruct((n_ex, nc * DV, GLA_K), F32)],
        scratch_shapes=[pltpu.VMEM((n_ex, DV, GLA_K), F32)],
        plan=plan,
    )(u3, u3, u3, u3, w2, gb, ng)
    return (y.reshape(t, GLA_V), states), extra


def _gla_bwd(dycat, u, states, w2, gb, ng, n_ex, lp, plan=None):
    nc = lp // CHUNK
    t = n_ex * lp

    def body(dy_ref, qk_ref, v_ref, g_ref, lr_ref, st_ref, w2_ref, gb_ref, ng_ref,
             du_ref, dw2_ref, dgb_ref, dng_ref, dstate):
        n = pl.program_id(0)
        group = nc // GLA_GROUP - 1 - n

        @pl.when(n == 0)
        def _():
            dw2_ref[...] = jnp.zeros_like(dw2_ref)
            dgb_ref[...] = jnp.zeros_like(dgb_ref)
            dng_ref[...] = jnp.zeros_like(dng_ref)
            dstate[...] = jnp.zeros_like(dstate)

        carried = [dstate[e] for e in range(n_ex)]

        def one_chunk(e, order):
            j = GLA_GROUP - 1 - order
            rows = pl.ds(j * CHUNK, CHUNK)
            qk = qk_ref[e, rows, :]
            lr = lr_ref[e, rows, :]
            st = st_ref[e, pl.ds(j * DV, DV), :]
            first = jnp.logical_and(group == 0, j == 0)
            c = yield from _gla_chunk(qk[:, :GLA_K], qk[:, GLA_K:], v_ref[e, rows, :], lr, w2_ref[...], gb_ref[...],
                                      first)
            qx, k_in, k_dec, v_b, s = c["qx"], c["k_in"], c["k_dec"], c["v_b"], c["s"]
            st_b = st.astype(BF16)
            o = c["o_intra"] + _dot_nt(qx, st_b)
            ngv = ng_ref[...]
            yield
            rstd = lax.rsqrt(jnp.mean(o * o, axis=-1, keepdims=True) + RMS_EPS)
            nrm = o * rstd
            g = _rows_by_head(g_ref[e, rows, :])
            dy = _rows_by_head(dy_ref[e, rows, :])
            sg = _sigmoid(g)
            dg = dy * nrm * ngv * (sg * (1.0 + g * (1.0 - sg)))
            dt = dy * (g * sg)
            dng_ref[...] += jnp.sum(dt * nrm, axis=0, keepdims=True)
            dn = dt * ngv
            do = rstd * (dn - nrm * jnp.mean(dn * nrm, axis=-1, keepdims=True))
            do_b = do.astype(BF16)
            dox = _expand_lanes(do).astype(BF16)
            yield
            da = jnp.where(_stacked_causal(), _dot_nt(dox, v_b), 0.0).astype(BF16)
            dv_intra = _dot_tn(s, dox)
            dst_own = _dot_tn(do_b, qx)
            yield
            dq_in = _fold(_dot(da, k_in) + _dot(do_b, st_b), CHUNK)
            dk_in = _dot_tn(da, qx)
            dq = dq_in * (DK ** -0.5) * c["e_pos"]
            yield
            for _ in range(order):
                yield
            dst = carried[e]
            dstx = _expand_state(dst).astype(BF16)
            dv = dv_intra + _dot_nt(k_dec, dstx)
            dk_dec = _dot(v_b, dstx)
            carried[e] = dst_own + c["decay"] * dst
            yield
            dbl = (jnp.sum(dk_dec * c["kd_f"], axis=0, keepdims=True)
                   + c["decay"] * jnp.sum(dst * st, axis=0, keepdims=True))
            dk = dk_in * c["e_neg"] + dk_dec * c["e_dec"]
            db = dq_in * c["q_f"] - dk_in * c["k_f"] - dk_dec * c["kd_f"]
            row = lax.broadcasted_iota(jnp.int32, (CHUNK, 1), 0)
            da_log = _running_sum(db + jnp.where(row == CHUNK - 1, dbl, 0.0), False)
            yield
            dz = jnp.where(c["live"], da_log * (1.0 - _sigmoid(c["z"])) * (1.0 / GATE_TAU), 0.0)
            dz_b = dz.astype(BF16)
            out = du_ref.at[e, rows, :]
            out[:, 0:GLA_K] = dq.astype(BF16)
            out[:, GLA_K:2 * GLA_K] = dk.astype(BF16)
            out[:, 2 * GLA_K:2 * GLA_K + GLA_V] = dv.astype(BF16)
            out[:, 2 * GLA_K + GLA_V:2 * GLA_K + 2 * GLA_V] = _lanes_by_head(dg).astype(BF16)
            out[:, 2 * GLA_K + 2 * GLA_V:] = _dot_nt(dz_b, w2_ref[...]).astype(BF16)
            dw2_ref[...] += _dot_tn(lr.astype(BF16), dz_b)
            dgb_ref[...] += jnp.sum(dz, axis=0, keepdims=True)

        _in_lockstep(one_chunk(e, order) for order in range(GLA_GROUP) for e in range(n_ex))
        for e in range(n_ex):
            dstate[e] = carried[e]

    u3 = u.reshape(n_ex, lp, D_IN_PAD)
    rev = lambda w, col: pl.BlockSpec((n_ex, GLA_GROUP * CHUNK, w), lambda n: (0, nc // GLA_GROUP - 1 - n, col))
    (du, d_w2, d_gb, d_ng), extra = _call(
        body, name="gla_bwd", grid=(nc // GLA_GROUP,),
        in_specs=[rev(GLA_V, 1), rev(2 * GLA_K, 2), rev(GLA_V, 3), rev(GLA_V, 4), rev(128, 20),
                  pl.BlockSpec((n_ex, GLA_GROUP * DV, GLA_K), lambda n: (0, nc // GLA_GROUP - 1 - n, 0)),
                  _const_spec((128, GLA_K)), _const_spec((1, GLA_K)), _const_spec((1, DV))],
        out_specs=[rev(D_GLA_IN, 0), _acc_spec((128, GLA_K)), _acc_spec((1, GLA_K)), _acc_spec((1, DV))],
        out_shape=[jax.ShapeDtypeStruct((n_ex, lp, D_GLA_IN), BF16), jax.ShapeDtypeStruct((128, GLA_K), F32),
                   jax.ShapeDtypeStruct((1, GLA_K), F32), jax.ShapeDtypeStruct((1, DV), F32)],
        scratch_shapes=[pltpu.VMEM((n_ex, DV, GLA_K), F32)],
        plan=plan,
    )(dycat.reshape(n_ex, lp, D), u3, u3, u3, u3, states, w2, gb, ng)
    return (du.reshape(t, D_GLA_IN), d_w2, d_gb, d_ng), extra


def _in_proj_bwd(du_conv, du_gla, w_in_t, h0, dh1, g_mix, plan=None):
    t = h0.shape[0]
    r = _row_tile(t, 384)

    def body(dc_ref, dg_ref, w_ref, h_ref, dh1_ref, g_ref, dh0_ref, dgm_ref):
        @pl.when(pl.program_id(0) == 0)
        def _():
            dgm_ref[...] = jnp.zeros_like(dgm_ref)

        dhn = _dot(dc_ref[...], w_ref[:2 * C_CONV, :]) + _dot(dg_ref[...], w_ref[2 * C_CONV:, :])
        h = h_ref[...]
        rstd = lax.rsqrt(jnp.mean(h * h, axis=-1, keepdims=True) + RMS_EPS)
        nrm = h * rstd
        dgm_ref[...] += jnp.sum(dhn * nrm, axis=0, keepdims=True)
        dn = dhn * g_ref[...]
        dh0_ref[...] = dh1_ref[...] + rstd * (dn - nrm * jnp.mean(dn * nrm, axis=-1, keepdims=True))

    rows = lambda w: pl.BlockSpec((r, w), lambda i: (i, 0))
    return _call(
        body, name="in_proj_bwd", grid=(t // r,),
        in_specs=[rows(2 * C_CONV), rows(D_GLA_IN), _const_spec((D_IN_PAD, D)),
                  rows(D), rows(D), _const_spec((1, D))],
        out_specs=[rows(D), _acc_spec((1, D))],
        out_shape=[jax.ShapeDtypeStruct((t, D), F32), jax.ShapeDtypeStruct((1, D), F32)],
        plan=plan,
    )(du_conv, du_gla, w_in_t, h0, dh1, g_mix)


def _wgrad(x, dy, name, plan=None):
    t, m = x.shape
    n = dy.shape[1]
    tk = t // 3 if t % (3 * 128) == 0 else _row_tile(t, 384)
    tm = m if m <= D_GLA_IN else m // 2

    def body(x_ref, dy_ref, o_ref):
        @pl.when(pl.program_id(1) == 0)
        def _():
            o_ref[...] = jnp.zeros_like(o_ref)

        o_ref[...] += _dot_tn(x_ref[...].astype(BF16), dy_ref[...].astype(BF16))

    (out,), extra = _call(
        body, name=name, grid=(m // tm, t // tk),
        in_specs=[pl.BlockSpec((tk, tm), lambda i, k: (k, i)), pl.BlockSpec((tk, n), lambda i, k: (k, 0))],
        out_specs=[pl.BlockSpec((tm, n), lambda i, k: (i, 0))],
        out_shape=[jax.ShapeDtypeStruct((m, n), F32)],
        plan=plan,
    )(x, dy)
    return out, extra


def _wgrad_pair(xa, xb, dy, name):
    t, m = xa.shape
    n = dy.shape[1]
    tk = t // 3 if t % (3 * 128) == 0 else _row_tile(t, 384)

    def body(xa_ref, xb_ref, dy_ref, o_ref):
        @pl.when(pl.program_id(1) == 0)
        def _():
            o_ref[...] = jnp.zeros_like(o_ref)

        x = jnp.where(pl.program_id(0) == 0, xa_ref[...], xb_ref[...])
        o_ref[...] += _dot_tn(x.astype(BF16), dy_ref[...].astype(BF16))

    rows = lambda w: pl.BlockSpec((tk, w), lambda i, k: (k, 0))
    return pl.pallas_call(
        body, name=name, grid=(2, t // tk), in_specs=[rows(m), rows(m), rows(n)],
        out_specs=pl.BlockSpec((m, n), lambda i, k: (i, 0)),
        out_shape=jax.ShapeDtypeStruct((2 * m, n), F32), compiler_params=_params(2),
    )(xa, xb, dy)


def _adam_update(g, w, m, v):
    m2 = ADAM_B1 * m + (1.0 - ADAM_B1) * g
    v2 = ADAM_B2 * v + (1.0 - ADAM_B2) * (g * g)
    m_hat = m2 / (1.0 - ADAM_B1 ** ADAM_STEP)
    v_hat = v2 / (1.0 - ADAM_B2 ** ADAM_STEP)
    delta = -ADAM_LR * (m_hat / (jnp.sqrt(v_hat) + ADAM_EPS) + ADAM_WD * w)
    return delta, m2, v2


ADAMW_STEPS = 4


def _adamw(g, w, m, v, name):
    rows, cols = g.shape
    steps = ADAMW_STEPS if rows % (ADAMW_STEPS * SUBLANES) == 0 else 1

    def body(g_ref, w_ref, m_ref, v_ref, d_ref, m2_ref, v2_ref):
        d_ref[...], m2_ref[...], v2_ref[...] = _adam_update(g_ref[...], w_ref[...], m_ref[...], v_ref[...])

    spec = pl.BlockSpec((rows // steps, cols), lambda i: (i, 0))
    return pl.pallas_call(
        body, name=name, grid=(steps,), in_specs=[spec] * 4, out_specs=[spec] * 3,
        out_shape=[jax.ShapeDtypeStruct(g.shape, F32)] * 3, compiler_params=_params(1),
    )(g, w, m, v)


def _adamw_halves(items, c, name):
    n = len(items)
    h = items[0][0].shape[-1]
    splits = lambda a: a.shape[0] % (ADAMW_STEPS * (SUBLANES if a.ndim == 2 else 1)) == 0
    steps = ADAMW_STEPS if all(splits(it[0]) for it in items) else 1

    def body(c_ref, *refs):
        ins, outs = refs[:5 * n], refs[5 * n:]
        own = pl.program_id(1) == c_ref[0]
        for i in range(n):
            a_ref, b_ref, w_ref, m_ref, v_ref = ins[5 * i:5 * i + 5]
            go_ref, d_ref, m2_ref, v2_ref = outs[4 * i:4 * i + 4]
            g = jnp.where(own, a_ref[...], b_ref[...])
            go_ref[...] = g
            d_ref[...], m2_ref[...], v2_ref[...] = _adam_update(g, w_ref[...], m_ref[...], v_ref[...])

    in_specs, out_specs, out_shape, args = [pl.BlockSpec(memory_space=pltpu.SMEM)], [], [], []
    for mine, theirs, w, m, v in items:
        tr = mine.shape[0] // steps
        mid = (0,) * (mine.ndim - 2)
        half = pl.BlockSpec((tr,) + mine.shape[1:-1] + (h,), lambda i, j, mid=mid: (i, *mid, 0))
        full = pl.BlockSpec((tr,) + mine.shape[1:-1] + (h,), lambda i, j, mid=mid: (i, *mid, j))
        in_specs += [half, half, full, full, full]
        out_specs += [full] * 4
        out_shape += [jax.ShapeDtypeStruct(w.shape, F32)] * 4
        args += [mine, theirs, w, m, v]
    res = pl.pallas_call(
        body, name=name, grid=(steps, 2), in_specs=in_specs, out_specs=out_specs, out_shape=out_shape,
        compiler_params=_params(2),
    )(jnp.reshape(c, (1,)).astype(jnp.int32), *args)
    return [res[4 * i:4 * i + 4] for i in range(n)]


def _rs_add_halves(pairs, c, name):
    blocks = pairs[0][0].shape[0]
    n = len(pairs)

    def body(c_ref, *refs):
        for i in range(n):
            refs[2 * n + i][...] = (refs[2 * i][...] + refs[2 * i + 1][...]).astype(BF16)

    in_specs, out_specs, out_shape = [], [], []
    for g, _ in pairs:
        _, rows, w = g.shape
        in_specs += [pl.BlockSpec((1, rows, w // 2), lambda j, s: (j, 0, s[0])),
                     pl.BlockSpec((1, rows, w // 2), lambda j, s: (j, 0, 0))]
        out_specs += [pl.BlockSpec((1, rows, w // 2), lambda j, s: (j, 0, 0))]
        out_shape += [jax.ShapeDtypeStruct((blocks, rows, w // 2), BF16)]
    return pl.pallas_call(
        body, name=name,
        grid_spec=pltpu.PrefetchScalarGridSpec(num_scalar_prefetch=1, grid=(blocks,), in_specs=in_specs,
                                               out_specs=out_specs),
        out_shape=out_shape, compiler_params=_params(1),
    )(jnp.reshape(c, (1,)).astype(jnp.int32), *[a for pair in pairs for a in pair])


def _rs_sum(pairs, mine, name):
    n = len(pairs)
    steps = 2 if all(own.shape[1] % (2 * 16) == 0 for own, _ in pairs) else 1

    def body(mine_ref, *refs):
        for i in range(n):
            p = refs[2 * i + 1][...].astype(F32)
            refs[2 * n + i][...] = ((refs[2 * i][0].astype(F32) + p[0]) + p[1]) + p[2]

    in_specs, out_specs, out_shape = [], [], []
    for own, _ in pairs:
        _, rows, h = own.shape
        tr = rows // steps
        in_specs += [pl.BlockSpec((1, tr, h), lambda i, s: (s[0], i, 0)),
                     pl.BlockSpec((3, tr, h), lambda i, s: (0, i, 0))]
        out_specs += [pl.BlockSpec((tr, h), lambda i, s: (i, 0))]
        out_shape += [jax.ShapeDtypeStruct((rows, h), F32)]
    return pl.pallas_call(
        body, name=name,
        grid_spec=pltpu.PrefetchScalarGridSpec(num_scalar_prefetch=1, grid=(steps,), in_specs=in_specs,
                                               out_specs=out_specs),
        out_shape=out_shape, compiler_params=_params(1),
    )(jnp.reshape(mine, (1,)).astype(jnp.int32), *[a for pair in pairs for a in pair])


def _sum_slots_adamw(slots, late_slots, vectors):
    late_rows = late_slots.shape[1]
    n = len(SMALL_PARTS)

    def body(s_ref, l_ref, *refs):
        ins, g_ref, outs = refs[:3 * n], refs[3 * n], refs[3 * n + 1:]
        g, late = s_ref[0], l_ref[0]
        for d in range(1, 8):
            g = g + s_ref[d]
            late = late + l_ref[d]
        g = jnp.concatenate([g[:late_rows] + late, g[late_rows:]], axis=0)
        g_ref[...] = g
        for i, (_, row, col, size) in enumerate(SMALL_PARTS):
            w_ref, m_ref, v_ref = ins[3 * i:3 * i + 3]
            go_ref, d_ref, m2_ref, v2_ref = outs[4 * i:4 * i + 4]
            piece = g[row:row + 1, col:col + size]
            go_ref[...] = piece
            d_ref[...], m2_ref[...], v2_ref[...] = _adam_update(piece, w_ref[...], m_ref[...], v_ref[...])

    vm = pl.BlockSpec(memory_space=pltpu.VMEM)
    out_shape = [jax.ShapeDtypeStruct(slots.shape[1:], F32)]
    for _, _, _, size in SMALL_PARTS:
        out_shape += [jax.ShapeDtypeStruct((1, size), F32)] * 4
    res = pl.pallas_call(body, name="small_sum_adamw", in_specs=[vm] * (2 + 3 * n), out_specs=[vm] * len(out_shape),
                         out_shape=out_shape)(slots, late_slots, *[a for wmv in vectors for a in wmv])
    return res[0], [res[1 + 4 * i:5 + 4 * i] for i in range(n)]


def _mesh_pos():
    return lax.axis_index("x"), lax.axis_index("y"), lax.axis_index("c")


def _other_chips(x, y):
    return [(1 - x, y), (x, 1 - y), (1 - x, 1 - y)]


def _half(ref, c, axis):
    n = ref.shape[axis] // 2
    return ref.at[(slice(None),) * axis + (pl.ds(c * n, n),)]


def _remote(src, dst, send_sem, recv_sem, device):
    return pltpu.make_async_remote_copy(src_ref=src, dst_ref=dst, send_sem=send_sem, recv_sem=recv_sem,
                                        device_id=device, device_id_type=MESH)


def _gather_plan(split, whole=(), axes=None):
    split, whole = list(split), list(whole)
    ns, n = len(split), len(split) + len(whole)

    def make(ins, outs, sems):
        ici_send, ici_recv, d2d_send, d2d_recv, own_send, own_recv = sems
        x, y, c = _mesh_pos()
        mine = 2 * x + y
        chips = _other_chips(x, y)
        blocks = [2 * px + py for px, py in chips]

        def own(a):
            return _remote(ins[a], outs[a].at[mine], own_send.at[a], own_recv.at[a], (x, y, 1 - c))

        def ici(a, k, block):
            px, py = chips[k]
            src, dst = ins[a], outs[a].at[block]
            if a < ns:
                src, dst = _half(src, c, axes[a]), _half(dst, c, axes[a])
            return _remote(src, dst, ici_send.at[3 * a + k], ici_recv.at[3 * a + k], (px, py, c))

        def d2d(a, k, half):
            part = _half(outs[a].at[blocks[k]], half, axes[a])
            return _remote(part, part, d2d_send.at[3 * a + k], d2d_recv.at[3 * a + k], (x, y, 1 - c))

        def start():
            for a in range(n):
                for k in range(3):
                    ici(a, k, mine).start()
                own(a).start()

        def relay():
            for a in range(n):
                for k in range(3):
                    ici(a, k, blocks[k]).wait_recv()
                    if a < ns:
                        d2d(a, k, c).start()

        def finish():
            for a in range(ns):
                for k in range(3):
                    d2d(a, k, 1 - c).wait_recv()
            for a in range(n):
                for k in range(3):
                    ici(a, k, mine).wait_send()
                    if a < ns:
                        d2d(a, k, c).wait_send()
                own(a).wait()

        return start, relay, finish

    arrays = split + whole
    axes = [0] * ns if axes is None else list(axes)
    return _Plan(arrays, [jax.ShapeDtypeStruct((N_CHIPS,) + s.shape, s.dtype) for s in arrays],
                 [pltpu.SemaphoreType.DMA((3 * n,)), pltpu.SemaphoreType.DMA((3 * n,)),
                  pltpu.SemaphoreType.DMA((3 * ns,)), pltpu.SemaphoreType.DMA((3 * ns,)),
                  pltpu.SemaphoreType.DMA((n,)), pltpu.SemaphoreType.DMA((n,))], make)


def _to_sibling_plan(gs):
    n = len(gs)

    def make(ins, outs, sems):
        send_sems, recv_sems = sems
        x, y, c = _mesh_pos()

        def copy(a):
            return _remote(_half(ins[a], 1 - c, len(ins[a].shape) - 1), outs[a], send_sems.at[a],
                           recv_sems.at[a], (x, y, 1 - c))

        def start():
            for a in range(n):
                copy(a).start()

        def finish():
            for a in range(n):
                copy(a).wait()

        return start, finish

    return _Plan(list(gs), [jax.ShapeDtypeStruct(g.shape[:-1] + (g.shape[-1] // 2,), g.dtype) for g in gs],
                 [pltpu.SemaphoreType.DMA((n,)), pltpu.SemaphoreType.DMA((n,))], make)


def _chip_exchange_plan(ps):
    n = len(ps)

    def make(ins, outs, sems):
        send_sems, recv_sems = sems
        x, y, c = _mesh_pos()
        chips = _other_chips(x, y)

        def ici(a, k):
            px, py = chips[k]
            return _remote(ins[a].at[2 * px + py], outs[a].at[k], send_sems.at[3 * a + k],
                           recv_sems.at[3 * a + k], (px, py, c))

        def start():
            for a in range(n):
                for k in range(3):
                    ici(a, k).start()

        def finish():
            for a in range(n):
                for k in range(3):
                    ici(a, k).wait()

        return start, finish

    return _Plan(list(ps), [jax.ShapeDtypeStruct((3,) + p.shape[1:], p.dtype) for p in ps],
                 [pltpu.SemaphoreType.DMA((3 * n,)), pltpu.SemaphoreType.DMA((3 * n,))], make)


def _share_plan(halves):
    n = len(halves)

    def make(ins, outs, sems):
        send_sems, recv_sems = sems
        x, y, c = _mesh_pos()

        def d2d(a):
            return _remote(ins[a], outs[a], send_sems.at[a], recv_sems.at[a], (x, y, 1 - c))

        def start():
            for a in range(n):
                d2d(a).start()

        def finish():
            for a in range(n):
                d2d(a).wait()

        return start, finish

    return _Plan(list(halves), [jax.ShapeDtypeStruct(p.shape, p.dtype) for p in halves],
                 [pltpu.SemaphoreType.DMA((n,)), pltpu.SemaphoreType.DMA((n,))], make)


def _all_to_all_plan(part):
    def make(ins, outs, sems):
        send_sems, recv_sems, local_sem = sems
        (p_ref,), (slots,) = ins, outs
        x, y, c = _mesh_pos()
        me = 4 * x + 2 * y + c
        peers = [(px, py, pc) for px in (x, 1 - x) for py in (y, 1 - y) for pc in (c, 1 - c)][1:]

        def remote(k, slot):
            return _remote(p_ref, slots.at[slot], send_sems.at[k], recv_sems.at[k], peers[k])

        def local():
            return pltpu.make_async_copy(p_ref, slots.at[me], local_sem)

        def start():
            for k in range(7):
                remote(k, me).start()
            local().start()

        def finish():
            for k, (px, py, pc) in enumerate(peers):
                remote(k, 4 * px + 2 * py + pc).wait_recv()
            for k in range(7):
                remote(k, me).wait_send()
            local().wait()

        return start, finish

    return _Plan([part], [jax.ShapeDtypeStruct((8,) + part.shape, part.dtype)],
                 [pltpu.SemaphoreType.DMA((7,)), pltpu.SemaphoreType.DMA((7,)), pltpu.SemaphoreType.DMA(())], make)


def _merge_plans(a, b):
    na_in, na_out, na_sems = len(a.arrays), len(a.out_shape), len(a.sems)

    def make(ins, outs, sems):
        phases_a = _phases(a.make(ins[:na_in], outs[:na_out], sems[:na_sems]))
        phases_b = _phases(b.make(ins[na_in:], outs[na_out:], sems[na_sems:]))

        def both(i):
            def run():
                phases_a[i]()
                phases_b[i]()
            return run

        return both(0), both(1), both(2)

    return _Plan(list(a.arrays) + list(b.arrays), list(a.out_shape) + list(b.out_shape),
                 list(a.sems) + list(b.sems), make)


def _exchange(plan, name):
    n_in, n_out = len(plan.arrays), len(plan.out_shape)

    def body(*refs):
        for phase in _phases(plan.make(refs[:n_in], refs[n_in:n_in + n_out], refs[n_in + n_out:])):
            phase()

    return pl.pallas_call(
        body, name=name, in_specs=[HBM_SPEC] * n_in, out_specs=[HBM_SPEC] * n_out, out_shape=list(plan.out_shape),
        scratch_shapes=list(plan.sems), compiler_params=pltpu.CompilerParams(has_side_effects=True),
    )(*plan.arrays)


def _pack_small(parts):
    rows = []
    for r in range(SMALL_ROWS):
        pieces, col = [], 0
        for name, row, start, size in SMALL_PARTS:
            if row == r:
                assert start == col
                pieces.append(parts[name].reshape(1, size).astype(F32))
                col += size
        rows.append(jnp.concatenate(pieces + [jnp.zeros((1, D - col), F32)], axis=1))
    return jnp.concatenate(rows, axis=0)


def _columns(gathered):
    return jnp.concatenate([gathered[j] for j in range(N_CHIPS)], axis=1)


def kernel(x, meta_tokens, norm_mix_g, w_in, conv_w, conv_b, conv_ln_g, conv_ln_b, gla_w_gate2, gla_gate_b, gla_norm_g, w_out, norm_ffn_g, w_ffn_gate, w_ffn_up, w_ffn_down, norm_final_g, loss_target, m_meta_tokens, m_norm_mix_g, m_w_in, m_conv_w, m_conv_b, m_conv_ln_g, m_conv_ln_b, m_gla_w_gate2, m_gla_gate_b, m_gla_norm_g, m_w_out, m_norm_ffn_g, m_w_ffn_gate, m_w_ffn_up, m_w_ffn_down, m_norm_final_g, v_meta_tokens, v_norm_mix_g, v_w_in, v_conv_w, v_conv_b, v_conv_ln_g, v_conv_ln_b, v_gla_w_gate2, v_gla_gate_b, v_gla_norm_g, v_w_out, v_norm_ffn_g, v_w_ffn_gate, v_w_ffn_up, v_w_ffn_down, v_norm_final_g):
    ws = dict(zip(WEIGHT_NAMES, (meta_tokens, norm_mix_g, w_in, conv_w, conv_b, conv_ln_g, conv_ln_b, gla_w_gate2,
                                 gla_gate_b, gla_norm_g, w_out, norm_ffn_g, w_ffn_gate, w_ffn_up, w_ffn_down,
                                 norm_final_g)))
    ms = dict(zip(WEIGHT_NAMES, (m_meta_tokens, m_norm_mix_g, m_w_in, m_conv_w, m_conv_b, m_conv_ln_g, m_conv_ln_b,
                                 m_gla_w_gate2, m_gla_gate_b, m_gla_norm_g, m_w_out, m_norm_ffn_g, m_w_ffn_gate,
                                 m_w_ffn_up, m_w_ffn_down, m_norm_final_g)))
    vs = dict(zip(WEIGHT_NAMES, (v_meta_tokens, v_norm_mix_g, v_w_in, v_conv_w, v_conv_b, v_conv_ln_g, v_conv_ln_b,
                                 v_gla_w_gate2, v_gla_gate_b, v_gla_norm_g, v_w_out, v_norm_ffn_g, v_w_ffn_gate,
                                 v_w_ffn_up, v_w_ffn_down, v_norm_final_g)))
    c = lax.axis_index("c")
    mine = 2 * lax.axis_index("x") + lax.axis_index("y")
    shard = lambda d, name: d[name].reshape(d[name].shape[-2:])
    vec = {name: ws[name].reshape(1, -1) for name, _, _, _ in SMALL_PARTS}
    n_ex, seq, _ = x.shape
    lp = HEAD_ROWS + seq
    t = n_ex * lp

    (tgt, h0, gate_s, up_s, down_s, out_s), (w_in_g, meta_g, conv_w_g, w2_g) = _pad_head_rows(
        [loss_target, x],
        [shard(ws, "w_ffn_gate").T, shard(ws, "w_ffn_up").T, shard(ws, "w_ffn_down")], [shard(ws, "w_out")],
        plan=_gather_plan([shard(ws, "w_in").T.astype(BF16)],
                          [shard(ws, "meta_tokens"), shard(ws, "conv_w"), shard(ws, "gla_w_gate2")], axes=[1]))
    conv_w_full = jnp.concatenate([_columns(conv_w_g), jnp.zeros((32 - CONV_W, C_CONV), F32)], axis=0)
    w2_full = jnp.concatenate([_columns(w2_g), jnp.zeros((128 - RANK, GLA_K), F32)], axis=0).astype(BF16)
    h0 = _set_meta_rows(h0, _columns(meta_g)).reshape(t, D)
    tgt = tgt.reshape(t, D)
    row_mask = jnp.concatenate([jnp.zeros((n_ex, HEAD_ROWS, 1), F32), jnp.ones((n_ex, seq, 1), F32)],
                               axis=1).reshape(t, 1)

    (u, hn, w_in_t), (gate_g,) = _in_proj(h0, vec["norm_mix_g"], w_in_g, plan=_gather_plan([gate_s]))
    (yc, y_conv), (up_g, w_out_g) = _conv_fwd(
        u, conv_w_full, vec["conv_b"], vec["conv_ln_g"], vec["conv_ln_b"], n_ex, lp,
        plan=_gather_plan([up_s, out_s]))
    (y_gla, states), _ = _gla_fwd(u, w2_full, vec["gla_gate_b"], vec["gla_norm_g"], n_ex, lp)
    w_gate_t, w_up_t = gate_g.reshape(D_FF, D), up_g.reshape(D_FF, D)
    (h1, hn2, gate, up, act), (down_g,) = _mix_out_ffn_up(
        h0, y_conv, y_gla, w_out_g, vec["norm_ffn_g"], w_gate_t, w_up_t,
        plan=_gather_plan([down_s]))
    w_down_full = down_g.reshape(D_FF, D)
    dh2, loss, d_final_g = _ffn_down_loss(act, w_down_full, h1, tgt, vec["norm_final_g"], row_mask)
    dgate, dup, dh1, dycat, d_ffn_g = _ffn_bwd(dh2, gate, up, h1, w_down_full.T, w_gate_t, w_up_t, w_out_g,
                                                vec["norm_ffn_g"])

    ffn_block = lambda g: g.reshape(N_CHIPS, D_FF // N_CHIPS, D)
    g_gate = ffn_block(_wgrad(dgate, hn2, "wgrad_gate")[0])
    g_up, (gate_sib,) = _wgrad(dup, hn2, "wgrad_up", _to_sibling_plan([g_gate]))
    g_up = ffn_block(g_up)
    g_down, (up_sib,) = _wgrad(act, dh2, "wgrad_down", _to_sibling_plan([g_up]))
    g_down = ffn_block(g_down)
    g_out = _wgrad_pair(y_conv, y_gla, dh1, "wgrad_out").reshape(N_CHIPS, D // N_CHIPS, D)
    cs_gate, cs_up = _rs_add_halves([(g_gate, gate_sib), (g_up, up_sib)], c, "rs_add_gate_up")
    (du_conv, d_conv_w, d_conv_b, d_ln_g, d_ln_b), (ex_gate, ex_up, down_sib, out_sib) = _conv_bwd(
        dycat, yc, u, conv_w_full, vec["conv_ln_g"], vec["conv_ln_b"], n_ex, lp,
        plan=_merge_plans(_chip_exchange_plan([cs_gate, cs_up]), _to_sibling_plan([g_down, g_out])))
    cs_down, cs_out = _rs_add_halves([(g_down, down_sib), (g_out, out_sib)], c, "rs_add_down_out")
    (du_gla, d_w2, d_gate_b, d_norm_g), (ex_down, ex_out) = _gla_bwd(
        dycat, u, states, w2_full, vec["gla_gate_b"], vec["gla_norm_g"], n_ex, lp,
        plan=_chip_exchange_plan([cs_down, cs_out]))
    halves = _rs_sum([(cs_gate, ex_gate), (cs_up, ex_up), (cs_down, ex_down), (cs_out, ex_out)], mine,
                     "rs_sum_early")

    small = {"norm_mix_g": jnp.zeros((1, D), F32), "norm_ffn_g": d_ffn_g, "norm_final_g": d_final_g,
             "conv_b": d_conv_b, "conv_ln_g": d_ln_g, "conv_ln_b": d_ln_b, "gla_gate_b": d_gate_b,
             "gla_norm_g": d_norm_g}
    part = lax.dynamic_update_slice(_pack_small(small), loss[:, :1], (LOSS_ROW, 0))
    part = jnp.concatenate([part, jnp.zeros((N_META, D), F32), d_conv_w.reshape(16, D), d_w2[:RANK].reshape(4, D),
                            jnp.zeros((4, D), F32)], axis=0)
    g_in_conv = _wgrad(du_conv, hn, "wgrad_in_conv")[0][None]
    g_in_gla, (slots, conv_sib) = _wgrad(du_gla, hn, "wgrad_in_gla",
                                         _merge_plans(_all_to_all_plan(part), _to_sibling_plan([g_in_conv])))
    pieces = [g_in_conv, g_in_gla[None]]
    (gla_sib,) = _exchange(_to_sibling_plan(pieces[1:]), "rs_late_to_sibling")
    sums = _rs_add_halves(list(zip(pieces, (conv_sib, gla_sib))), c, "rs_add_w_in")
    in_chip_sum = jnp.concatenate([sums[0][0], sums[1][0]], axis=0)[:D_IN].reshape(N_CHIPS, D_IN // N_CHIPS, D // 2)
    (dh0, d_mix_g), shared = _in_proj_bwd(
        du_conv, du_gla, w_in_t, h0, dh1, vec["norm_mix_g"],
        plan=_merge_plans(_share_plan(halves), _chip_exchange_plan([in_chip_sum])))
    dh0 = dh0.reshape(n_ex, lp, D)
    grad_x = dh0[:, HEAD_ROWS:]
    late_part = jnp.concatenate([d_mix_g, jnp.zeros((SMALL_ROWS - 1, D), F32),
                                 jnp.sum(dh0[:, PAD_ROWS:HEAD_ROWS], axis=0)], axis=0)
    (in_half,) = _rs_sum([(in_chip_sum, shared[4])], mine, "rs_sum_w_in")
    in_shared, late_slots = _exchange(_merge_plans(_share_plan([in_half]), _all_to_all_plan(late_part)),
                                      "late_exchange")

    out = {"grad": {}, "delta": {}, "new_m": {}, "new_v": {}}

    def record(name, res, transposed=False):
        for kind, a in zip(("grad", "delta", "new_m", "new_v"), res):
            out[kind][name] = (a.T if transposed else a).reshape(ws[name].shape)

    def operands(name, transposed):
        lay = (lambda a: a.T) if transposed else (lambda a: a)
        return lay(shard(ws, name)), lay(shard(ms, name)), lay(shard(vs, name))

    early_layout = (("w_ffn_gate", True), ("w_ffn_up", True), ("w_ffn_down", False), ("w_out", False))
    items = [(mine_half, their_half, *operands(name, transposed))
             for (name, transposed), mine_half, their_half in zip(early_layout, halves, shared)]
    for (name, transposed), res in zip(early_layout, _adamw_halves(items, c, "adamw_early")):
        record(name, res, transposed)

    tile_rows = lambda a: a.reshape(a.shape[0], 1, a.shape[1])
    by_output = lambda d: jnp.transpose(d["w_in"], (2, 0, 1))
    res = _adamw_halves([(tile_rows(in_half), tile_rows(in_shared), by_output(ws), by_output(ms), by_output(vs))],
                        c, "adamw_w_in")[0]
    for kind, a in zip(("grad", "delta", "new_m", "new_v"), res):
        out[kind]["w_in"] = jnp.transpose(a, (1, 2, 0))

    flat = lambda d, name: d[name].reshape(1, -1)
    g_s, updated = _sum_slots_adamw(slots, late_slots,
                                    [(flat(ws, name), flat(ms, name), flat(vs, name)) for name, _, _, _ in SMALL_PARTS])
    for (name, _, _, _), res in zip(SMALL_PARTS, updated):
        record(name, res)
    loss = g_s[LOSS_ROW, 0]
    block = lambda a, width: lax.dynamic_slice_in_dim(a, mine * width, width, axis=1)
    small_sharded = {"meta_tokens": block(g_s[8:24], D // N_CHIPS),
                     "conv_w": block(g_s[24:40].reshape(32, C_CONV), C_CONV // N_CHIPS)[:CONV_W],
                     "gla_w_gate2": block(g_s[40:44].reshape(RANK, GLA_K), GLA_K // N_CHIPS)}
    for name, g in small_sharded.items():
        record(name, [g, *_adamw(g, *operands(name, False), "adamw_" + name)])

    return (loss, grad_x, *[out[kind][name] for kind in ("grad", "delta", "new_m", "new_v") for name in WEIGHT_NAMES])
```

```python
import functools
from typing import Any, Callable, NamedTuple, Sequence

import jax
import jax.numpy as jnp
from jax import lax
from jax.experimental import pallas as pl
from jax.experimental.pallas import tpu as pltpu

F32 = jnp.float32
BF16 = jnp.bfloat16
MESH = pl.DeviceIdType.MESH

D = 1024
N_META = 16
C_CONV = 512
CONV_W = 31
GLA_K = 256
GLA_V = 512
N_HEADS = 4
DK = 64
DV = 128
RANK = 16
CHUNK = 64
PAD_ROWS = CHUNK - N_META
HEAD_ROWS = CHUNK
D_IN = 2576
D_IN_PAD = 2688
D_GLA_IN = D_IN_PAD - 2 * C_CONV
D_FF = 2816
RMS_EPS = 1e-6
LN_EPS = 1e-5
GATE_TAU = 16.0
N_CHIPS = 4

ADAM_LR = 0.001
ADAM_B1 = 0.9
ADAM_B2 = 0.999
ADAM_EPS = 1e-08
ADAM_WD = 0.01
ADAM_STEP = 10

V7X_VMEM_BYTES = 64 * 1024 * 1024
VMEM_LIMIT = V7X_VMEM_BYTES - 8 * 1024 * 1024
SUBLANES = 8
ROW_PART = 128
FFN_BWD_TILE = 192

WEIGHT_NAMES = ("meta_tokens", "norm_mix_g", "w_in", "conv_w", "conv_b", "conv_ln_g", "conv_ln_b", "gla_w_gate2",
                "gla_gate_b", "gla_norm_g", "w_out", "norm_ffn_g", "w_ffn_gate", "w_ffn_up", "w_ffn_down",
                "norm_final_g")

SMALL_ROWS = 8
SMALL_PARTS = (("norm_mix_g", 0, 0, D), ("norm_ffn_g", 1, 0, D), ("norm_final_g", 2, 0, D),
               ("conv_b", 3, 0, C_CONV), ("conv_ln_g", 3, C_CONV, C_CONV), ("conv_ln_b", 4, 0, C_CONV),
               ("gla_gate_b", 4, C_CONV, GLA_K), ("gla_norm_g", 4, C_CONV + GLA_K, DV))
LOSS_ROW = 5

HBM_SPEC = pl.BlockSpec(memory_space=pltpu.HBM)


def _dot(a, b):
    return jnp.dot(a, b, preferred_element_type=F32)


def _dot_nt(a, b):
    return lax.dot_general(a, b, (((1,), (1,)), ((), ())), preferred_element_type=F32)


def _dot_tn(a, b):
    return lax.dot_general(a, b, (((0,), (0,)), ((), ())), preferred_element_type=F32)


def _sigmoid(x):
    return 1.0 / (1.0 + jnp.exp(-x))


def _const_spec(shape):
    return pl.BlockSpec(shape, lambda *_: (0,) * len(shape), pipeline_mode=pl.Buffered(1))


def _acc_spec(shape):
    return pl.BlockSpec(shape, lambda *_: (0,) * len(shape))


def _params(n_axes):
    return pltpu.CompilerParams(dimension_semantics=("arbitrary",) * n_axes, vmem_limit_bytes=VMEM_LIMIT)


def _row_tile(t, want):
    for r in (want, 384, 192, 128, 64):
        if r <= want and t % r == 0:
            return r
    raise ValueError(f"no row tile for {t}")


def _row_parts(r):
    if r % ROW_PART:
        return [slice(None)]
    return [pl.ds(i * ROW_PART, ROW_PART) for i in range(r // ROW_PART)]


def _in_lockstep(bodies):
    live = list(bodies)
    while live:
        still = []
        for g in live:
            try:
                next(g)
                still.append(g)
            except StopIteration:
                pass
        live = still


class _Plan(NamedTuple):
    arrays: Sequence[Any]
    out_shape: Sequence[Any]
    sems: Sequence[Any]
    make: Callable


def _phases(made):
    return made if len(made) == 3 else (made[0], lambda: None, made[1])


def _call(body, *, name, grid, in_specs, out_specs, out_shape, scratch_shapes=(), plan=None):
    n_in, n_out, n_scr = len(in_specs), len(out_specs), len(scratch_shapes)
    if plan is None:
        plan = _Plan([], [], [], lambda ins, outs, sems: (lambda: None, lambda: None))
    nx_in, nx_out = len(plan.arrays), len(plan.out_shape)
    n_steps = functools.reduce(lambda a, b: a * b, grid)

    def hosted(*refs):
        ins, xins = refs[:n_in], refs[n_in:n_in + nx_in]
        o0 = n_in + nx_in
        outs, xouts = refs[o0:o0 + n_out], refs[o0 + n_out:o0 + n_out + nx_out]
        s0 = o0 + n_out + nx_out
        scr, sems = refs[s0:s0 + n_scr], refs[s0 + n_scr:]
        step = functools.reduce(lambda acc, a: acc * grid[a] + pl.program_id(a), range(len(grid)), 0)
        start, relay, finish = _phases(plan.make(xins, xouts, sems))
        pl.when(step == 0)(start)
        pl.when(step == n_steps - 1)(relay)
        body(*ins, *outs, *scr)
        pl.when(step == n_steps - 1)(finish)

    call = pl.pallas_call(
        hosted, name=name, grid=grid, in_specs=list(in_specs) + [HBM_SPEC] * nx_in,
        out_specs=list(out_specs) + [HBM_SPEC] * nx_out, out_shape=list(out_shape) + list(plan.out_shape),
        scratch_shapes=list(scratch_shapes) + list(plan.sems),
        compiler_params=pltpu.CompilerParams(dimension_semantics=("arbitrary",) * len(grid),
                                             vmem_limit_bytes=VMEM_LIMIT, has_side_effects=nx_in > 0))

    def run(*args):
        res = call(*args, *plan.arrays)
        return res[:n_out], res[n_out:]

    return run


def _pad_head_rows(arrays, casts, transposed_casts, plan=None):
    n_ex, seq, _ = arrays[0].shape
    nc = (HEAD_ROWS + seq) // CHUNK
    n, k = len(arrays), len(casts)
    kt = k + len(transposed_casts)

    def body(*refs):
        ins, outs = refs[:n + kt], refs[n + kt:]
        for a_ref, o_ref in zip(ins[:n], outs[:n]):
            o_ref[...] = jnp.where(pl.program_id(0) > 0, a_ref[...], 0.0)

        @pl.when(pl.program_id(0) == 0)
        def _():
            for a_ref, o_ref in zip(ins[n:n + k], outs[n:n + k]):
                o_ref[...] = a_ref[...].astype(BF16)
            for a_ref, o_ref in zip(ins[n + k:], outs[n + k:]):
                o_ref[...] = a_ref[...].T.astype(BF16)

    whole = lambda shape: pl.BlockSpec(shape, lambda i: (0, 0))
    cast_shapes = [a.shape for a in casts] + [a.shape[::-1] for a in transposed_casts]
    return _call(
        body, name="pad_head_rows", grid=(nc,),
        in_specs=([pl.BlockSpec((n_ex, CHUNK, D), lambda i: (0, jnp.maximum(i - 1, 0), 0))] * n
                  + [_const_spec(a.shape) for a in (*casts, *transposed_casts)]),
        out_specs=[pl.BlockSpec((n_ex, CHUNK, D), lambda i: (0, i, 0))] * n + [whole(s) for s in cast_shapes],
        out_shape=([jax.ShapeDtypeStruct((n_ex, HEAD_ROWS + seq, D), F32)] * n
                   + [jax.ShapeDtypeStruct(s, BF16) for s in cast_shapes]),
        plan=plan,
    )(*arrays, *casts, *transposed_casts)


def _set_meta_rows(h0, meta):
    n_ex = h0.shape[0]

    def body(h_ref, meta_ref, o_ref):
        o_ref[...] = jnp.concatenate(
            [h_ref[:, :PAD_ROWS, :], jnp.broadcast_to(meta_ref[...][None], (n_ex, N_META, D))], axis=1)

    head = pl.BlockSpec((n_ex, HEAD_ROWS, D), lambda i: (0, 0, 0))
    return pl.pallas_call(
        body, name="set_meta_rows", grid=(1,), in_specs=[head, pl.BlockSpec((N_META, D), lambda i: (0, 0))],
        out_specs=head, out_shape=jax.ShapeDtypeStruct(h0.shape, F32), input_output_aliases={0: 0},
        compiler_params=_params(1),
    )(h0, meta)


def _in_proj(h0, g_mix, w_in_blocks, plan=None):
    t = h0.shape[0]
    r = _row_tile(t, 384)
    n_blocks, rows_block, _ = w_in_blocks.shape

    def body(h_ref, g_ref, w_ref, u_ref, hn_ref, wt_ref):
        @pl.when(pl.program_id(0) == 0)
        def _():
            for j in range(n_blocks):
                wt_ref[j * rows_block:(j + 1) * rows_block, :] = w_ref[j]
            wt_ref[n_blocks * rows_block:, :] = jnp.zeros((D_IN_PAD - n_blocks * rows_block, D), BF16)

        h = h_ref[...]
        rstd = lax.rsqrt(jnp.mean(h * h, axis=-1, keepdims=True) + RMS_EPS)
        hn = (h * rstd * g_ref[...]).astype(BF16)
        hn_ref[...] = hn
        u_ref[...] = _dot_nt(hn, wt_ref[...])

    return _call(
        body, name="in_proj", grid=(t // r,),
        in_specs=[pl.BlockSpec((r, D), lambda i: (i, 0)), _const_spec((1, D)), _const_spec(w_in_blocks.shape)],
        out_specs=[pl.BlockSpec((r, D_IN_PAD), lambda i: (i, 0)), pl.BlockSpec((r, D), lambda i: (i, 0)),
                   _acc_spec((D_IN_PAD, D))],
        out_shape=[jax.ShapeDtypeStruct((t, D_IN_PAD), F32), jax.ShapeDtypeStruct((t, D), BF16),
                   jax.ShapeDtypeStruct((D_IN_PAD, D), BF16)],
        plan=plan,
    )(h0, g_mix, w_in_blocks)


CONV_TILE = 192
CONV_SUB = 32
CONV_LEAD = CONV_SUB - (CONV_W - 1)


def _shifted_copies(src, dst, r):
    for s in range(1, SUBLANES):
        dst[s - 1] = src[s:s + r + CONV_SUB - SUBLANES, :]


def _shifted_rows(src, shifted, start):
    base, s = SUBLANES * (start // SUBLANES), start % SUBLANES
    if s == 0:
        return src[base:base + CONV_SUB, :]
    return shifted[s - 1, base:base + CONV_SUB, :]


def _conv_fwd(u, conv_w, conv_b, ln_g, ln_b, n_ex, lp, plan=None):
    r = CONV_TILE
    nt = lp // r
    hb = r // CONV_SUB

    def body(cur_ref, prev_ref, w_ref, b_ref, lg_ref, lb_ref, yc_ref, y_ref, glu, glu_sh):
        i = pl.program_id(1)
        cur = cur_ref[...]
        glu[CONV_SUB:CONV_SUB + r, :] = cur[:, :C_CONV] * _sigmoid(cur[:, C_CONV:])
        pv = prev_ref[...]
        halo = pv[:, :C_CONV] * _sigmoid(pv[:, C_CONV:])
        glu[0:CONV_SUB, :] = jnp.where(i > 0, halo, 0.0)
        _shifted_copies(glu, glu_sh, r)
        w = w_ref[...]
        for j in range(r // CONV_SUB):
            r0 = j * CONV_SUB
            acc = jnp.zeros((CONV_SUB, C_CONV), F32) + b_ref[...]
            for k in range(CONV_W):
                acc = acc + w[k:k + 1, :] * _shifted_rows(glu, glu_sh, r0 + CONV_LEAD + k)
            mu = jnp.mean(acc, axis=-1, keepdims=True)
            cen = acc - mu
            var = jnp.mean(cen * cen, axis=-1, keepdims=True)
            out = cen * lax.rsqrt(var + LN_EPS) * lg_ref[...] + lb_ref[...]
            y = out * _sigmoid(out)
            row = i * r + r0 + lax.broadcasted_iota(jnp.int32, (CONV_SUB, 1), 0)
            y = jnp.where(row >= PAD_ROWS, y, 0.0)
            yc_ref[r0:r0 + CONV_SUB, :] = acc
            y_ref[r0:r0 + CONV_SUB, :] = y.astype(BF16)

    t = n_ex * lp
    return _call(
        body, name="conv_fwd", grid=(n_ex, nt),
        in_specs=[pl.BlockSpec((r, 2 * C_CONV), lambda b, i: (b * nt + i, 0)),
                  pl.BlockSpec((CONV_SUB, 2 * C_CONV), lambda b, i: (jnp.maximum((b * nt + i) * hb - 1, 0), 0)),
                  _const_spec((32, C_CONV)), _const_spec((1, C_CONV)), _const_spec((1, C_CONV)), _const_spec((1, C_CONV))],
        out_specs=[pl.BlockSpec((r, C_CONV), lambda b, i: (b * nt + i, 0)),
                   pl.BlockSpec((r, C_CONV), lambda b, i: (b * nt + i, 0))],
        out_shape=[jax.ShapeDtypeStruct((t, C_CONV), F32), jax.ShapeDtypeStruct((t, C_CONV), BF16)],
        scratch_shapes=[pltpu.VMEM((r + CONV_SUB, C_CONV), F32),
                        pltpu.VMEM((SUBLANES - 1, r + CONV_SUB - SUBLANES, C_CONV), F32)],
        plan=plan,
    )(u, u, conv_w, conv_b, ln_g, ln_b)


def _mix_out_ffn_up(h0, y_conv, y_gla, w_out_t, g_ffn, w_gate_t, w_up_t, plan=None):
    t = h0.shape[0]
    r = _row_tile(t, 384)
    wb = D // N_CHIPS

    def body(h0_ref, yc_ref, yg_ref, wo_ref, g_ref, wg_ref, wu_ref, h1_ref, hn_ref, gate_ref, up_ref, act_ref):
        h1 = h0_ref[...]
        for j in range(N_CHIPS):
            y_ref, col = (yc_ref, j * wb) if j * wb < C_CONV else (yg_ref, j * wb - C_CONV)
            h1 = h1 + _dot_nt(y_ref[:, col:col + wb], wo_ref[j])
        h1_ref[...] = h1
        rstd = lax.rsqrt(jnp.mean(h1 * h1, axis=-1, keepdims=True) + RMS_EPS)
        hn = (h1 * rstd * g_ref[...]).astype(BF16)
        hn_ref[...] = hn
        gate = _dot_nt(hn, wg_ref[...])
        up = _dot_nt(hn, wu_ref[...])
        gate_ref[...] = gate
        up_ref[...] = up
        act_ref[...] = (gate * _sigmoid(gate) * up).astype(BF16)

    rows = lambda w: pl.BlockSpec((r, w), lambda i: (i, 0))
    return _call(
        body, name="mix_out_ffn_up", grid=(t // r,),
        in_specs=[rows(D), rows(C_CONV), rows(GLA_V), _const_spec((N_CHIPS, D, wb)), _const_spec((1, D)),
                  _const_spec((D_FF, D)), _const_spec((D_FF, D))],
        out_specs=[rows(D), rows(D), rows(D_FF), rows(D_FF), rows(D_FF)],
        out_shape=[jax.ShapeDtypeStruct((t, D), F32), jax.ShapeDtypeStruct((t, D), BF16),
                   jax.ShapeDtypeStruct((t, D_FF), F32), jax.ShapeDtypeStruct((t, D_FF), F32),
                   jax.ShapeDtypeStruct((t, D_FF), BF16)],
        plan=plan,
    )(h0, y_conv, y_gla, w_out_t, g_ffn, w_gate_t, w_up_t)


def _ffn_down_loss(act, w_down, h1, target, g_final, n_ex, lp):
    t = h1.shape[0]
    r = _row_tile(t, 384)

    def body(act_ref, wd_ref, h1_ref, tgt_ref, gf_ref, dh2_ref, loss_ref, dgf_ref):
        @pl.when(pl.program_id(0) == 0)
        def _():
            loss_ref[...] = jnp.zeros_like(loss_ref)
            dgf_ref[...] = jnp.zeros_like(dgf_ref)

        gf = gf_ref[...]

        def part(rows):
            h2 = h1_ref[rows, :] + _dot(act_ref[rows, :], wd_ref[...])
            yield
            first = pl.program_id(0) * r + (0 if rows == slice(None) else rows.start)
            row = first + lax.broadcasted_iota(jnp.int32, (h2.shape[0], 1), 0)
            in_seq = row < 0
            for e in range(n_ex):
                in_seq = in_seq | ((row >= e * lp + HEAD_ROWS) & (row < (e + 1) * lp))
            rstd = lax.rsqrt(jnp.mean(h2 * h2, axis=-1, keepdims=True) + RMS_EPS)
            nrm = h2 * rstd
            err = jnp.where(in_seq, nrm * gf - tgt_ref[rows, :], 0.0)
            loss_ref[...] += jnp.sum(err * err) * (0.5 / D)
            dy = err * (1.0 / D)
            dgf_ref[...] += jnp.sum(dy * nrm, axis=0, keepdims=True)
            dn = dy * gf
            dh2_ref[rows, :] = rstd * (dn - nrm * jnp.mean(dn * nrm, axis=-1, keepdims=True))

        _in_lockstep(part(rows) for rows in _row_parts(r))

    rows = lambda w: pl.BlockSpec((r, w), lambda i: (i, 0))
    return pl.pallas_call(
        body, name="ffn_down_loss", grid=(t // r,),
        in_specs=[rows(D_FF), _const_spec((D_FF, D)), rows(D), rows(D), _const_spec((1, D))],
        out_specs=[rows(D), _acc_spec((1, 128)), _acc_spec((1, D))],
        out_shape=[jax.ShapeDtypeStruct((t, D), F32), jax.ShapeDtypeStruct((1, 128), F32),
                   jax.ShapeDtypeStruct((1, D), F32)],
        compiler_params=_params(1),
    )(act, w_down, h1, target, g_final)


def _ffn_bwd(dh2, gate, up, h1, w_down_t, w_gate_t, w_up_t, w_out_t, g_ffn):
    t = h1.shape[0]
    r = _row_tile(t, FFN_BWD_TILE)
    wb = D // N_CHIPS

    def body(dh2_ref, gate_ref, up_ref, h1_ref, wd_ref, wg_ref, wu_ref, wo_ref, g_ref,
             dgate_ref, dup_ref, dh1_ref, dycat_ref, dg_ref):
        @pl.when(pl.program_id(0) == 0)
        def _():
            dg_ref[...] = jnp.zeros_like(dg_ref)

        dh2 = dh2_ref[...]
        dact = _dot(dh2.astype(BF16), wd_ref[...])
        gate = gate_ref[...]
        sg = _sigmoid(gate)
        dgate = (dact * up_ref[...] * (sg * (1.0 + gate * (1.0 - sg)))).astype(BF16)
        dup = (dact * (gate * sg)).astype(BF16)
        dgate_ref[...] = dgate
        dup_ref[...] = dup
        dhn = _dot(dgate, wg_ref[...]) + _dot(dup, wu_ref[...])
        h1 = h1_ref[...]
        rstd = lax.rsqrt(jnp.mean(h1 * h1, axis=-1, keepdims=True) + RMS_EPS)
        nrm = h1 * rstd
        dg_ref[...] += jnp.sum(dhn * nrm, axis=0, keepdims=True)
        dn = dhn * g_ref[...]
        dh1 = dh2 + rstd * (dn - nrm * jnp.mean(dn * nrm, axis=-1, keepdims=True))
        dh1_ref[...] = dh1
        dh1 = dh1.astype(BF16)
        for j in range(N_CHIPS):
            dycat_ref[:, j * wb:(j + 1) * wb] = _dot(dh1, wo_ref[j])

    rows = lambda w: pl.BlockSpec((r, w), lambda i: (i, 0))
    return pl.pallas_call(
        body, name="ffn_bwd", grid=(t // r,),
        in_specs=[rows(D), rows(D_FF), rows(D_FF), rows(D), _const_spec((D, D_FF)), _const_spec((D_FF, D)),
                  _const_spec((D_FF, D)), _const_spec((N_CHIPS, D, wb)), _const_spec((1, D))],
        out_specs=[rows(D_FF), rows(D_FF), rows(D), rows(D), _acc_spec((1, D))],
        out_shape=[jax.ShapeDtypeStruct((t, D_FF), BF16), jax.ShapeDtypeStruct((t, D_FF), BF16),
                   jax.ShapeDtypeStruct((t, D), F32), jax.ShapeDtypeStruct((t, D), F32),
                   jax.ShapeDtypeStruct((1, D), F32)],
        compiler_params=_params(1),
    )(dh2, gate, up, h1, w_down_t, w_gate_t, w_up_t, w_out_t, g_ffn)


def _conv_bwd(dycat, yc, u, conv_w, ln_g, ln_b, n_ex, lp, plan=None):
    r = CONV_TILE
    nt = lp // r
    hb = r // CONV_SUB
    nsub = r // CONV_SUB

    def ln_bwd(dy, yc_rows, live, lg, lb):
        mu = jnp.mean(yc_rows, axis=-1, keepdims=True)
        cen = yc_rows - mu
        rs = lax.rsqrt(jnp.mean(cen * cen, axis=-1, keepdims=True) + LN_EPS)
        yn = cen * rs
        out = yn * lg + lb
        so = _sigmoid(out)
        dout = jnp.where(live, dy * (so * (1.0 + out * (1.0 - so))), 0.0)
        dyn = dout * lg
        dyc = rs * (dyn - jnp.mean(dyn, axis=-1, keepdims=True) - yn * jnp.mean(dyn * yn, axis=-1, keepdims=True))
        return dyc, dout, yn

    def body(dy_ref, dyn_ref, yc_ref, ycn_ref, cur_ref, prev_ref, w_ref, lg_ref, lb_ref,
             du_ref, dw_ref, db_ref, dlg_ref, dlb_ref, glu, dycs, dwacc, glu_sh, dycs_sh):
        b = pl.program_id(0)
        i = pl.program_id(1)
        first = jnp.logical_and(b == 0, i == 0)

        @pl.when(first)
        def _():
            dwacc[...] = jnp.zeros_like(dwacc)
            db_ref[...] = jnp.zeros_like(db_ref)
            dlg_ref[...] = jnp.zeros_like(dlg_ref)
            dlb_ref[...] = jnp.zeros_like(dlb_ref)

        lg, lb = lg_ref[...], lb_ref[...]
        cur = cur_ref[...]
        sig = _sigmoid(cur[:, C_CONV:])
        glu[CONV_SUB:CONV_SUB + r, :] = cur[:, :C_CONV] * sig
        pv = prev_ref[...]
        glu[0:CONV_SUB, :] = jnp.where(i > 0, pv[:, :C_CONV] * _sigmoid(pv[:, C_CONV:]), 0.0)

        row = i * r + lax.broadcasted_iota(jnp.int32, (r, 1), 0)
        dyc, dout, yn = ln_bwd(dy_ref[...], yc_ref[...], row >= PAD_ROWS, lg, lb)
        dycs[0:r, :] = dyc
        dycn, _, _ = ln_bwd(dyn_ref[...], ycn_ref[...], i < nt - 1, lg, lb)
        dycs[r:r + CONV_SUB, :] = dycn
        db_ref[...] += jnp.sum(dyc, axis=0, keepdims=True)
        dlg_ref[...] += jnp.sum(dout * yn, axis=0, keepdims=True)
        dlb_ref[...] += jnp.sum(dout, axis=0, keepdims=True)

        _shifted_copies(glu, glu_sh, r)
        _shifted_copies(dycs, dycs_sh, r)
        w = w_ref[...]
        for j in range(nsub):
            r0 = j * CONV_SUB
            dblk = dycs[r0:r0 + CONV_SUB, :]
            dglu = jnp.zeros((CONV_SUB, C_CONV), F32)
            for k in range(CONV_W):
                dglu = dglu + w[k:k + 1, :] * _shifted_rows(dycs, dycs_sh, r0 + (CONV_W - 1) - k)
                prod = dblk * _shifted_rows(glu, glu_sh, r0 + CONV_LEAD + k)
                dwacc[k] += prod.reshape(CONV_SUB // SUBLANES, SUBLANES, C_CONV).sum(axis=0)
            sg = sig[r0:r0 + CONV_SUB, :]
            cv = cur[r0:r0 + CONV_SUB, :C_CONV]
            du_ref[r0:r0 + CONV_SUB, :C_CONV] = (dglu * sg).astype(BF16)
            du_ref[r0:r0 + CONV_SUB, C_CONV:] = (dglu * cv * sg * (1.0 - sg)).astype(BF16)

        @pl.when(jnp.logical_and(b == n_ex - 1, i == nt - 1))
        def _():
            dw_ref[...] = jnp.sum(dwacc[...], axis=1)

    t = n_ex * lp
    cur_rows = lambda w, col: pl.BlockSpec((r, w), lambda b, i: (b * nt + i, col))
    nxt_rows = lambda w, col: pl.BlockSpec(
        (CONV_SUB, w), lambda b, i: (jnp.minimum((b * nt + i + 1) * hb, n_ex * nt * hb - 1), col))
    return _call(
        body, name="conv_bwd", grid=(n_ex, nt),
        in_specs=[cur_rows(C_CONV, 0), nxt_rows(C_CONV, 0), cur_rows(C_CONV, 0), nxt_rows(C_CONV, 0),
                  cur_rows(2 * C_CONV, 0),
                  pl.BlockSpec((CONV_SUB, 2 * C_CONV), lambda b, i: (jnp.maximum((b * nt + i) * hb - 1, 0), 0)),
                  _const_spec((32, C_CONV)), _const_spec((1, C_CONV)), _const_spec((1, C_CONV))],
        out_specs=[cur_rows(2 * C_CONV, 0), _acc_spec((32, C_CONV)), _acc_spec((1, C_CONV)),
                   _acc_spec((1, C_CONV)), _acc_spec((1, C_CONV))],
        out_shape=[jax.ShapeDtypeStruct((t, 2 * C_CONV), BF16), jax.ShapeDtypeStruct((32, C_CONV), F32),
                   jax.ShapeDtypeStruct((1, C_CONV), F32), jax.ShapeDtypeStruct((1, C_CONV), F32),
                   jax.ShapeDtypeStruct((1, C_CONV), F32)],
        scratch_shapes=[pltpu.VMEM((r + CONV_SUB, C_CONV), F32), pltpu.VMEM((r + CONV_SUB, C_CONV), F32),
                        pltpu.VMEM((32, SUBLANES, C_CONV), F32),
                        pltpu.VMEM((SUBLANES - 1, r + CONV_SUB - SUBLANES, C_CONV), F32),
                        pltpu.VMEM((SUBLANES - 1, r + CONV_SUB - SUBLANES, C_CONV), F32)],
        plan=plan,
    )(dycat, dycat, yc, yc, u, u, conv_w, ln_g, ln_b)


HEAD_ROWS_ALL = N_HEADS * CHUNK


def _gla_gates(lr, w2, gb, first_chunk):
    z = _dot(lr.astype(BF16), w2) + gb
    a = (jnp.minimum(z, 0.0) - jnp.log(1.0 + jnp.exp(-jnp.abs(z)))) * (1.0 / GATE_TAU)
    row = lax.broadcasted_iota(jnp.int32, (CHUNK, 1), 0)
    live = jnp.logical_or(jnp.logical_not(first_chunk), row >= PAD_ROWS)
    return z, jnp.where(live, a, 0.0), live


def _tri(lower):
    i = lax.broadcasted_iota(jnp.int32, (CHUNK, CHUNK), 0)
    j = lax.broadcasted_iota(jnp.int32, (CHUNK, CHUNK), 1)
    return (i >= j) if lower else (i <= j)


def _head_of(shape, axis, per_head):
    return lax.broadcasted_iota(jnp.int32, shape, axis) // per_head


def _expand(x, lanes_per_head):
    rows, lanes = HEAD_ROWS_ALL, x.shape[1]
    keep = _head_of((rows, lanes), 0, CHUNK) == _head_of((rows, lanes), 1, lanes_per_head)
    return jnp.where(keep, jnp.tile(x, (N_HEADS, 1)), 0.0)


def _expand_lanes(x):
    rows, w = x.shape
    keep = _head_of((rows, N_HEADS * w), 0, CHUNK) == _head_of((rows, N_HEADS * w), 1, w)
    return jnp.where(keep, jnp.tile(x, (1, N_HEADS)), 0.0)


def _expand_state(st):
    rows, lanes = N_HEADS * DV, st.shape[1]
    keep = _head_of((rows, lanes), 0, DV) == _head_of((rows, lanes), 1, DK)
    return jnp.where(keep, jnp.tile(st, (N_HEADS, 1)), 0.0)


def _fold(t, rows_per_head):
    lane_head = _head_of((rows_per_head, t.shape[1]), 1, DK)
    out = jnp.where(lane_head == 0, t[0:rows_per_head], 0.0)
    for h in range(1, N_HEADS):
        out = out + jnp.where(lane_head == h, t[h * rows_per_head:(h + 1) * rows_per_head], 0.0)
    return out


def _rows_by_head(x):
    return jnp.concatenate([x[:, h * DV:(h + 1) * DV] for h in range(N_HEADS)], axis=0)


def _lanes_by_head(x):
    return jnp.concatenate([x[h * CHUNK:(h + 1) * CHUNK] for h in range(N_HEADS)], axis=1)


def _running_sum(a, lower):
    hi = a.astype(BF16)
    rest = a - hi.astype(F32)
    mid = rest.astype(BF16)
    lo = (rest - mid.astype(F32)).astype(BF16)
    w = a.shape[1]
    parts = _dot(_tri(lower).astype(F32).astype(BF16), jnp.concatenate([hi, mid, lo], axis=1))
    return parts[:, :w] + parts[:, w:2 * w] + parts[:, 2 * w:]


def _stacked_causal():
    i = lax.broadcasted_iota(jnp.int32, (HEAD_ROWS_ALL, CHUNK), 0) % CHUNK
    j = lax.broadcasted_iota(jnp.int32, (HEAD_ROWS_ALL, CHUNK), 1)
    return i >= j


GLA_GROUP = 3


def _gla_chunk(q, k, v, lr, w2, gb, first_chunk):
    z, a, live = _gla_gates(lr, w2, gb, first_chunk)
    yield
    b = _running_sum(a, True)
    yield
    bl = b[CHUNK - 1:CHUNK, :]
    e_pos, e_neg, e_dec = jnp.exp(b), jnp.exp(-b), jnp.exp(bl - b)
    q_f, k_f, kd_f = q * (DK ** -0.5) * e_pos, k * e_neg, k * e_dec
    qx = _expand(q_f, DK).astype(BF16)
    k_in, k_dec, v_b = k_f.astype(BF16), kd_f.astype(BF16), v.astype(BF16)
    s = jnp.where(_stacked_causal(), _dot_nt(qx, k_in), 0.0).astype(BF16)
    yield
    p = _dot(s, v_b)
    yield
    o_intra = jnp.concatenate([p[h * CHUNK:(h + 1) * CHUNK, h * DV:(h + 1) * DV] for h in range(N_HEADS)], axis=0)
    return dict(z=z, live=live, bl=bl, e_pos=e_pos, e_neg=e_neg, e_dec=e_dec, q_f=q_f, k_f=k_f, kd_f=kd_f,
                qx=qx, k_in=k_in, k_dec=k_dec, v_b=v_b, s=s, o_intra=o_intra, decay=jnp.exp(bl))


def _gla_fwd(u, w2, gb, ng, n_ex, lp, plan=None):
    nc = lp // CHUNK
    t = n_ex * lp
    rows_of = lambda j: pl.ds(j * CHUNK, CHUNK)

    def body(qk_ref, v_ref, g_ref, lr_ref, w2_ref, gb_ref, ng_ref, y_ref, st_ref, state):
        n = pl.program_id(0)

        @pl.when(n == 0)
        def _():
            state[...] = jnp.zeros_like(state)

        carried = [state[e] for e in range(n_ex)]

        def one_chunk(e, j):
            rows = rows_of(j)
            qk = qk_ref[e, rows, :]
            first = jnp.logical_and(n == 0, j == 0)
            c = yield from _gla_chunk(qk[:, :GLA_K], qk[:, GLA_K:], v_ref[e, rows, :], lr_ref[e, rows, :],
                                      w2_ref[...], gb_ref[...], first)
            kv = _fold(_dot_tn(c["v_b"], c["k_dec"]), DV)
            g = _rows_by_head(g_ref[e, rows, :])
            gate = ng_ref[...] * (g * _sigmoid(g))
            yield
            for _ in range(j):
                yield
            st = carried[e]
            st_ref[e, pl.ds(j * DV, DV), :] = st
            o = c["o_intra"] + _dot_nt(c["qx"], st.astype(BF16))
            rstd = lax.rsqrt(jnp.mean(o * o, axis=-1, keepdims=True) + RMS_EPS)
            y_ref[e, rows, :] = _lanes_by_head(o * rstd * gate).astype(BF16)
            carried[e] = c["decay"] * st + kv

        _in_lockstep(one_chunk(e, j) for j in range(GLA_GROUP) for e in range(n_ex))
        for e in range(n_ex):
            state[e] = carried[e]

    u3 = u.reshape(n_ex, lp, D_IN_PAD)
    blk = lambda w, col: pl.BlockSpec((n_ex, GLA_GROUP * CHUNK, w), lambda n: (0, n, col))
    (y, states), extra = _call(
        body, name="gla_fwd", grid=(nc // GLA_GROUP,),
        in_specs=[blk(2 * GLA_K, 2), blk(GLA_V, 3), blk(GLA_V, 4), blk(128, 20),
                  _const_spec((128, GLA_K)), _const_spec((1, GLA_K)), _const_spec((1, DV))],
        out_specs=[blk(GLA_V, 0), pl.BlockSpec((n_ex, GLA_GROUP * DV, GLA_K), lambda n: (0, n, 0))],
        out_shape=[jax.ShapeDtypeStruct((n_ex, lp, GLA_V), BF16),
                   jax.ShapeDtypeStruct((n_ex, nc * DV, GLA_K), F32)],
        scratch_shapes=[pltpu.VMEM((n_ex, DV, GLA_K), F32)],
        plan=plan,
    )(u3, u3, u3, u3, w2, gb, ng)
    return (y.reshape(t, GLA_V), states), extra


def _gla_bwd(dycat, u, states, w2, gb, ng, n_ex, lp, plan=None):
    nc = lp // CHUNK
    t = n_ex * lp

    def body(dy_ref, qk_ref, v_ref, g_ref, lr_ref, st_ref, w2_ref, gb_ref, ng_ref,
             du_ref, dw2_ref, dgb_ref, dng_ref, dstate):
        n = pl.program_id(0)
        group = nc // GLA_GROUP - 1 - n

        @pl.when(n == 0)
        def _():
            dw2_ref[...] = jnp.zeros_like(dw2_ref)
            dgb_ref[...] = jnp.zeros_like(dgb_ref)
            dng_ref[...] = jnp.zeros_like(dng_ref)
            dstate[...] = jnp.zeros_like(dstate)

        carried = [dstate[e] for e in range(n_ex)]

        def one_chunk(e, order):
            j = GLA_GROUP - 1 - order
            rows = pl.ds(j * CHUNK, CHUNK)
            qk = qk_ref[e, rows, :]
            lr = lr_ref[e, rows, :]
            st = st_ref[e, pl.ds(j * DV, DV), :]
            first = jnp.logical_and(group == 0, j == 0)
            c = yield from _gla_chunk(qk[:, :GLA_K], qk[:, GLA_K:], v_ref[e, rows, :], lr, w2_ref[...], gb_ref[...],
                                      first)
            qx, k_in, k_dec, v_b, s = c["qx"], c["k_in"], c["k_dec"], c["v_b"], c["s"]
            st_b = st.astype(BF16)
            o = c["o_intra"] + _dot_nt(qx, st_b)
            ngv = ng_ref[...]
            yield
            rstd = lax.rsqrt(jnp.mean(o * o, axis=-1, keepdims=True) + RMS_EPS)
            nrm = o * rstd
            g = _rows_by_head(g_ref[e, rows, :])
            dy = _rows_by_head(dy_ref[e, rows, :])
            sg = _sigmoid(g)
            dg = dy * nrm * ngv * (sg * (1.0 + g * (1.0 - sg)))
            dt = dy * (g * sg)
            dng_ref[...] += jnp.sum(dt * nrm, axis=0, keepdims=True)
            dn = dt * ngv
            do = rstd * (dn - nrm * jnp.mean(dn * nrm, axis=-1, keepdims=True))
            do_b = do.astype(BF16)
            dox = _expand_lanes(do).astype(BF16)
            yield
            da = jnp.where(_stacked_causal(), _dot_nt(dox, v_b), 0.0).astype(BF16)
            dv_intra = _dot_tn(s, dox)
            dst_own = _dot_tn(do_b, qx)
            yield
            dq_in = _fold(_dot(da, k_in) + _dot(do_b, st_b), CHUNK)
            dk_in = _dot_tn(da, qx)
            dq = dq_in * (DK ** -0.5) * c["e_pos"]
            yield
            for _ in range(order):
                yield
            dst = carried[e]
            dstx = _expand_state(dst).astype(BF16)
            dv = dv_intra + _dot_nt(k_dec, dstx)
            dk_dec = _dot(v_b, dstx)
            carried[e] = dst_own + c["decay"] * dst
            yield
            dbl = (jnp.sum(dk_dec * c["kd_f"], axis=0, keepdims=True)
                   + c["decay"] * jnp.sum(dst * st, axis=0, keepdims=True))
            dk = dk_in * c["e_neg"] + dk_dec * c["e_dec"]
            db = dq_in * c["q_f"] - dk_in * c["k_f"] - dk_dec * c["kd_f"]
            row = lax.broadcasted_iota(jnp.int32, (CHUNK, 1), 0)
            da_log = _running_sum(db + jnp.where(row == CHUNK - 1, dbl, 0.0), False)
            yield
            dz = jnp.where(c["live"], da_log * (1.0 - _sigmoid(c["z"])) * (1.0 / GATE_TAU), 0.0)
            dz_b = dz.astype(BF16)
            out = du_ref.at[e, rows, :]
            out[:, 0:GLA_K] = dq.astype(BF16)
            out[:, GLA_K:2 * GLA_K] = dk.astype(BF16)
            out[:, 2 * GLA_K:2 * GLA_K + GLA_V] = dv.astype(BF16)
            out[:, 2 * GLA_K + GLA_V:2 * GLA_K + 2 * GLA_V] = _lanes_by_head(dg).astype(BF16)
            out[:, 2 * GLA_K + 2 * GLA_V:] = _dot_nt(dz_b, w2_ref[...]).astype(BF16)
            dw2_ref[...] += _dot_tn(lr.astype(BF16), dz_b)
            dgb_ref[...] += jnp.sum(dz, axis=0, keepdims=True)

        _in_lockstep(one_chunk(e, order) for order in range(GLA_GROUP) for e in range(n_ex))
        for e in range(n_ex):
            dstate[e] = carried[e]

    u3 = u.reshape(n_ex, lp, D_IN_PAD)
    rev = lambda w, col: pl.BlockSpec((n_ex, GLA_GROUP * CHUNK, w), lambda n: (0, nc // GLA_GROUP - 1 - n, col))
    (du, d_w2, d_gb, d_ng), extra = _call(
        body, name="gla_bwd", grid=(nc // GLA_GROUP,),
        in_specs=[rev(GLA_V, 1), rev(2 * GLA_K, 2), rev(GLA_V, 3), rev(GLA_V, 4), rev(128, 20),
                  pl.BlockSpec((n_ex, GLA_GROUP * DV, GLA_K), lambda n: (0, nc // GLA_GROUP - 1 - n, 0)),
                  _const_spec((128, GLA_K)), _const_spec((1, GLA_K)), _const_spec((1, DV))],
        out_specs=[rev(D_GLA_IN, 0), _acc_spec((128, GLA_K)), _acc_spec((1, GLA_K)), _acc_spec((1, DV))],
        out_shape=[jax.ShapeDtypeStruct((n_ex, lp, D_GLA_IN), BF16), jax.ShapeDtypeStruct((128, GLA_K), F32),
                   jax.ShapeDtypeStruct((1, GLA_K), F32), jax.ShapeDtypeStruct((1, DV), F32)],
        scratch_shapes=[pltpu.VMEM((n_ex, DV, GLA_K), F32)],
        plan=plan,
    )(dycat.reshape(n_ex, lp, D), u3, u3, u3, u3, states, w2, gb, ng)
    return (du.reshape(t, D_GLA_IN), d_w2, d_gb, d_ng), extra


def _in_proj_bwd(du_conv, du_gla, w_in_t, h0, dh1, g_mix, plan=None):
    t = h0.shape[0]
    r = _row_tile(t, 384)

    def body(dc_ref, dg_ref, w_ref, h_ref, dh1_ref, g_ref, dh0_ref, dgm_ref):
        @pl.when(pl.program_id(0) == 0)
        def _():
            dgm_ref[...] = jnp.zeros_like(dgm_ref)

        dhn = _dot(dc_ref[...], w_ref[:2 * C_CONV, :]) + _dot(dg_ref[...], w_ref[2 * C_CONV:, :])
        h = h_ref[...]
        rstd = lax.rsqrt(jnp.mean(h * h, axis=-1, keepdims=True) + RMS_EPS)
        nrm = h * rstd
        dgm_ref[...] += jnp.sum(dhn * nrm, axis=0, keepdims=True)
        dn = dhn * g_ref[...]
        dh0_ref[...] = dh1_ref[...] + rstd * (dn - nrm * jnp.mean(dn * nrm, axis=-1, keepdims=True))

    rows = lambda w: pl.BlockSpec((r, w), lambda i: (i, 0))
    return _call(
        body, name="in_proj_bwd", grid=(t // r,),
        in_specs=[rows(2 * C_CONV), rows(D_GLA_IN), _const_spec((D_IN_PAD, D)),
                  rows(D), rows(D), _const_spec((1, D))],
        out_specs=[rows(D), _acc_spec((1, D))],
        out_shape=[jax.ShapeDtypeStruct((t, D), F32), jax.ShapeDtypeStruct((1, D), F32)],
        plan=plan,
    )(du_conv, du_gla, w_in_t, h0, dh1, g_mix)


def _wgrad(x, dy, name, plan=None):
    t, m = x.shape
    n = dy.shape[1]
    tk = t // 3 if t % (3 * 128) == 0 else _row_tile(t, 384)
    tm = m if m <= D_GLA_IN else m // 2

    def body(x_ref, dy_ref, o_ref):
        @pl.when(pl.program_id(1) == 0)
        def _():
            o_ref[...] = jnp.zeros_like(o_ref)

        o_ref[...] += _dot_tn(x_ref[...].astype(BF16), dy_ref[...].astype(BF16))

    (out,), extra = _call(
        body, name=name, grid=(m // tm, t // tk),
        in_specs=[pl.BlockSpec((tk, tm), lambda i, k: (k, i)), pl.BlockSpec((tk, n), lambda i, k: (k, 0))],
        out_specs=[pl.BlockSpec((tm, n), lambda i, k: (i, 0))],
        out_shape=[jax.ShapeDtypeStruct((m, n), F32)],
        plan=plan,
    )(x, dy)
    return out, extra


def _wgrad_pair(xa, xb, dy, name):
    t, m = xa.shape
    n = dy.shape[1]
    tk = t // 3 if t % (3 * 128) == 0 else _row_tile(t, 384)

    def body(xa_ref, xb_ref, dy_ref, o_ref):
        @pl.when(pl.program_id(1) == 0)
        def _():
            o_ref[...] = jnp.zeros_like(o_ref)

        x = jnp.where(pl.program_id(0) == 0, xa_ref[...], xb_ref[...])
        o_ref[...] += _dot_tn(x.astype(BF16), dy_ref[...].astype(BF16))

    rows = lambda w: pl.BlockSpec((tk, w), lambda i, k: (k, 0))
    return pl.pallas_call(
        body, name=name, grid=(2, t // tk), in_specs=[rows(m), rows(m), rows(n)],
        out_specs=pl.BlockSpec((m, n), lambda i, k: (i, 0)),
        out_shape=jax.ShapeDtypeStruct((2 * m, n), F32), compiler_params=_params(2),
    )(xa, xb, dy)


def _adam_update(g, w, m, v):
    m2 = ADAM_B1 * m + (1.0 - ADAM_B1) * g
    v2 = ADAM_B2 * v + (1.0 - ADAM_B2) * (g * g)
    m_hat = m2 / (1.0 - ADAM_B1 ** ADAM_STEP)
    v_hat = v2 / (1.0 - ADAM_B2 ** ADAM_STEP)
    delta = -ADAM_LR * (m_hat / (jnp.sqrt(v_hat) + ADAM_EPS) + ADAM_WD * w)
    return delta, m2, v2


ADAMW_STEPS = 4


def _adamw(g, w, m, v, name):
    rows, cols = g.shape
    steps = ADAMW_STEPS if rows % (ADAMW_STEPS * SUBLANES) == 0 else 1

    def body(g_ref, w_ref, m_ref, v_ref, d_ref, m2_ref, v2_ref):
        d_ref[...], m2_ref[...], v2_ref[...] = _adam_update(g_ref[...], w_ref[...], m_ref[...], v_ref[...])

    spec = pl.BlockSpec((rows // steps, cols), lambda i: (i, 0))
    return pl.pallas_call(
        body, name=name, grid=(steps,), in_specs=[spec] * 4, out_specs=[spec] * 3,
        out_shape=[jax.ShapeDtypeStruct(g.shape, F32)] * 3, compiler_params=_params(1),
    )(g, w, m, v)


def _adamw_halves(items, c, name):
    n = len(items)
    h = items[0][0].shape[-1]
    splits = lambda a: a.shape[0] % (ADAMW_STEPS * (SUBLANES if a.ndim == 2 else 1)) == 0
    steps = ADAMW_STEPS if all(splits(it[0]) for it in items) else 1

    def body(c_ref, *refs):
        ins, outs = refs[:5 * n], refs[5 * n:]
        own = pl.program_id(1) == c_ref[0]
        for i in range(n):
            a_ref, b_ref, w_ref, m_ref, v_ref = ins[5 * i:5 * i + 5]
            go_ref, d_ref, m2_ref, v2_ref = outs[4 * i:4 * i + 4]
            g = jnp.where(own, a_ref[...], b_ref[...])
            go_ref[...] = g
            d_ref[...], m2_ref[...], v2_ref[...] = _adam_update(g, w_ref[...], m_ref[...], v_ref[...])

    in_specs, out_specs, out_shape, args = [pl.BlockSpec(memory_space=pltpu.SMEM)], [], [], []
    for mine, theirs, w, m, v in items:
        tr = mine.shape[0] // steps
        mid = (0,) * (mine.ndim - 2)
        half = pl.BlockSpec((tr,) + mine.shape[1:-1] + (h,), lambda i, j, mid=mid: (i, *mid, 0))
        full = pl.BlockSpec((tr,) + mine.shape[1:-1] + (h,), lambda i, j, mid=mid: (i, *mid, j))
        in_specs += [half, half, full, full, full]
        out_specs += [full] * 4
        out_shape += [jax.ShapeDtypeStruct(w.shape, F32)] * 4
        args += [mine, theirs, w, m, v]
    res = pl.pallas_call(
        body, name=name, grid=(steps, 2), in_specs=in_specs, out_specs=out_specs, out_shape=out_shape,
        compiler_params=_params(2),
    )(jnp.reshape(c, (1,)).astype(jnp.int32), *args)
    return [res[4 * i:4 * i + 4] for i in range(n)]


def _rs_add_halves(pairs, c, name):
    blocks = pairs[0][0].shape[0]
    n = len(pairs)

    def body(c_ref, *refs):
        for i in range(n):
            refs[2 * n + i][...] = (refs[2 * i][...] + refs[2 * i + 1][...]).astype(BF16)

    in_specs, out_specs, out_shape = [], [], []
    for g, _ in pairs:
        _, rows, w = g.shape
        in_specs += [pl.BlockSpec((1, rows, w // 2), lambda j, s: (j, 0, s[0])),
                     pl.BlockSpec((1, rows, w // 2), lambda j, s: (j, 0, 0))]
        out_specs += [pl.BlockSpec((1, rows, w // 2), lambda j, s: (j, 0, 0))]
        out_shape += [jax.ShapeDtypeStruct((blocks, rows, w // 2), BF16)]
    return pl.pallas_call(
        body, name=name,
        grid_spec=pltpu.PrefetchScalarGridSpec(num_scalar_prefetch=1, grid=(blocks,), in_specs=in_specs,
                                               out_specs=out_specs),
        out_shape=out_shape, compiler_params=_params(1),
    )(jnp.reshape(c, (1,)).astype(jnp.int32), *[a for pair in pairs for a in pair])


def _rs_sum(pairs, mine, name):
    n = len(pairs)
    steps = 2 if all(own.shape[1] % (2 * 16) == 0 for own, _ in pairs) else 1

    def body(mine_ref, *refs):
        for i in range(n):
            p = refs[2 * i + 1][...].astype(F32)
            refs[2 * n + i][...] = ((refs[2 * i][0].astype(F32) + p[0]) + p[1]) + p[2]

    in_specs, out_specs, out_shape = [], [], []
    for own, _ in pairs:
        _, rows, h = own.shape
        tr = rows // steps
        in_specs += [pl.BlockSpec((1, tr, h), lambda i, s: (s[0], i, 0)),
                     pl.BlockSpec((3, tr, h), lambda i, s: (0, i, 0))]
        out_specs += [pl.BlockSpec((tr, h), lambda i, s: (i, 0))]
        out_shape += [jax.ShapeDtypeStruct((rows, h), F32)]
    return pl.pallas_call(
        body, name=name,
        grid_spec=pltpu.PrefetchScalarGridSpec(num_scalar_prefetch=1, grid=(steps,), in_specs=in_specs,
                                               out_specs=out_specs),
        out_shape=out_shape, compiler_params=_params(1),
    )(jnp.reshape(mine, (1,)).astype(jnp.int32), *[a for pair in pairs for a in pair])


def _sum_slots_adamw(slots, late_slots, vectors):
    late_rows = late_slots.shape[1]
    n = len(SMALL_PARTS)

    def body(s_ref, l_ref, *refs):
        ins, g_ref, outs = refs[:3 * n], refs[3 * n], refs[3 * n + 1:]
        g, late = s_ref[0], l_ref[0]
        for d in range(1, 8):
            g = g + s_ref[d]
            late = late + l_ref[d]
        g = jnp.concatenate([g[:late_rows] + late, g[late_rows:]], axis=0)
        g_ref[...] = g
        for i, (_, row, col, size) in enumerate(SMALL_PARTS):
            w_ref, m_ref, v_ref = ins[3 * i:3 * i + 3]
            go_ref, d_ref, m2_ref, v2_ref = outs[4 * i:4 * i + 4]
            piece = g[row:row + 1, col:col + size]
            go_ref[...] = piece
            d_ref[...], m2_ref[...], v2_ref[...] = _adam_update(piece, w_ref[...], m_ref[...], v_ref[...])

    vm = pl.BlockSpec(memory_space=pltpu.VMEM)
    out_shape = [jax.ShapeDtypeStruct(slots.shape[1:], F32)]
    for _, _, _, size in SMALL_PARTS:
        out_shape += [jax.ShapeDtypeStruct((1, size), F32)] * 4
    res = pl.pallas_call(body, name="small_sum_adamw", in_specs=[vm] * (2 + 3 * n), out_specs=[vm] * len(out_shape),
                         out_shape=out_shape)(slots, late_slots, *[a for wmv in vectors for a in wmv])
    return res[0], [res[1 + 4 * i:5 + 4 * i] for i in range(n)]


def _mesh_pos():
    return lax.axis_index("x"), lax.axis_index("y"), lax.axis_index("c")


def _other_chips(x, y):
    return [(1 - x, y), (x, 1 - y), (1 - x, 1 - y)]


def _half(ref, c, axis):
    n = ref.shape[axis] // 2
    return ref.at[(slice(None),) * axis + (pl.ds(c * n, n),)]


def _remote(src, dst, send_sem, recv_sem, device):
    return pltpu.make_async_remote_copy(src_ref=src, dst_ref=dst, send_sem=send_sem, recv_sem=recv_sem,
                                        device_id=device, device_id_type=MESH)


def _gather_plan(split, whole=(), axes=None):
    split, whole = list(split), list(whole)
    ns, n = len(split), len(split) + len(whole)

    def make(ins, outs, sems):
        ici_send, ici_recv, d2d_send, d2d_recv, own_send, own_recv = sems
        x, y, c = _mesh_pos()
        mine = 2 * x + y
        chips = _other_chips(x, y)
        blocks = [2 * px + py for px, py in chips]

        def own(a):
            return _remote(ins[a], outs[a].at[mine], own_send.at[a], own_recv.at[a], (x, y, 1 - c))

        def ici(a, k, block):
            px, py = chips[k]
            src, dst = ins[a], outs[a].at[block]
            if a < ns:
                src, dst = _half(src, c, axes[a]), _half(dst, c, axes[a])
            return _remote(src, dst, ici_send.at[3 * a + k], ici_recv.at[3 * a + k], (px, py, c))

        def d2d(a, k, half):
            part = _half(outs[a].at[blocks[k]], half, axes[a])
            return _remote(part, part, d2d_send.at[3 * a + k], d2d_recv.at[3 * a + k], (x, y, 1 - c))

        def start():
            for a in range(n):
                for k in range(3):
                    ici(a, k, mine).start()
                own(a).start()

        def relay():
            for a in range(n):
                for k in range(3):
                    ici(a, k, blocks[k]).wait_recv()
                    if a < ns:
                        d2d(a, k, c).start()

        def finish():
            for a in range(ns):
                for k in range(3):
                    d2d(a, k, 1 - c).wait_recv()
            for a in range(n):
                for k in range(3):
                    ici(a, k, mine).wait_send()
                    if a < ns:
                        d2d(a, k, c).wait_send()
                own(a).wait()

        return start, relay, finish

    arrays = split + whole
    axes = [0] * ns if axes is None else list(axes)
    return _Plan(arrays, [jax.ShapeDtypeStruct((N_CHIPS,) + s.shape, s.dtype) for s in arrays],
                 [pltpu.SemaphoreType.DMA((3 * n,)), pltpu.SemaphoreType.DMA((3 * n,)),
                  pltpu.SemaphoreType.DMA((3 * ns,)), pltpu.SemaphoreType.DMA((3 * ns,)),
                  pltpu.SemaphoreType.DMA((n,)), pltpu.SemaphoreType.DMA((n,))], make)


def _to_sibling_plan(gs):
    n = len(gs)

    def make(ins, outs, sems):
        send_sems, recv_sems = sems
        x, y, c = _mesh_pos()

        def copy(a):
            return _remote(_half(ins[a], 1 - c, len(ins[a].shape) - 1), outs[a], send_sems.at[a],
                           recv_sems.at[a], (x, y, 1 - c))

        def start():
            for a in range(n):
                copy(a).start()

        def finish():
            for a in range(n):
                copy(a).wait()

        return start, finish

    return _Plan(list(gs), [jax.ShapeDtypeStruct(g.shape[:-1] + (g.shape[-1] // 2,), g.dtype) for g in gs],
                 [pltpu.SemaphoreType.DMA((n,)), pltpu.SemaphoreType.DMA((n,))], make)


def _chip_exchange_plan(ps):
    n = len(ps)

    def make(ins, outs, sems):
        send_sems, recv_sems = sems
        x, y, c = _mesh_pos()
        chips = _other_chips(x, y)

        def ici(a, k):
            px, py = chips[k]
            return _remote(ins[a].at[2 * px + py], outs[a].at[k], send_sems.at[3 * a + k],
                           recv_sems.at[3 * a + k], (px, py, c))

        def start():
            for a in range(n):
                for k in range(3):
                    ici(a, k).start()

        def finish():
            for a in range(n):
                for k in range(3):
                    ici(a, k).wait()

        return start, finish

    return _Plan(list(ps), [jax.ShapeDtypeStruct((3,) + p.shape[1:], p.dtype) for p in ps],
                 [pltpu.SemaphoreType.DMA((3 * n,)), pltpu.SemaphoreType.DMA((3 * n,))], make)


def _share_plan(halves):
    n = len(halves)

    def make(ins, outs, sems):
        send_sems, recv_sems = sems
        x, y, c = _mesh_pos()

        def d2d(a):
            return _remote(ins[a], outs[a], send_sems.at[a], recv_sems.at[a], (x, y, 1 - c))

        def start():
            for a in range(n):
                d2d(a).start()

        def finish():
            for a in range(n):
                d2d(a).wait()

        return start, finish

    return _Plan(list(halves), [jax.ShapeDtypeStruct(p.shape, p.dtype) for p in halves],
                 [pltpu.SemaphoreType.DMA((n,)), pltpu.SemaphoreType.DMA((n,))], make)


def _all_to_all_plan(part):
    def make(ins, outs, sems):
        send_sems, recv_sems, local_sem = sems
        (p_ref,), (slots,) = ins, outs
        x, y, c = _mesh_pos()
        me = 4 * x + 2 * y + c
        peers = [(px, py, pc) for px in (x, 1 - x) for py in (y, 1 - y) for pc in (c, 1 - c)][1:]

        def remote(k, slot):
            return _remote(p_ref, slots.at[slot], send_sems.at[k], recv_sems.at[k], peers[k])

        def local():
            return pltpu.make_async_copy(p_ref, slots.at[me], local_sem)

        def start():
            for k in range(7):
                remote(k, me).start()
            local().start()

        def finish():
            for k, (px, py, pc) in enumerate(peers):
                remote(k, 4 * px + 2 * py + pc).wait_recv()
            for k in range(7):
                remote(k, me).wait_send()
            local().wait()

        return start, finish

    return _Plan([part], [jax.ShapeDtypeStruct((8,) + part.shape, part.dtype)],
                 [pltpu.SemaphoreType.DMA((7,)), pltpu.SemaphoreType.DMA((7,)), pltpu.SemaphoreType.DMA(())], make)


def _merge_plans(a, b):
    na_in, na_out, na_sems = len(a.arrays), len(a.out_shape), len(a.sems)

    def make(ins, outs, sems):
        phases_a = _phases(a.make(ins[:na_in], outs[:na_out], sems[:na_sems]))
        phases_b = _phases(b.make(ins[na_in:], outs[na_out:], sems[na_sems:]))

        def both(i):
            def run():
                phases_a[i]()
                phases_b[i]()
            return run

        return both(0), both(1), both(2)

    return _Plan(list(a.arrays) + list(b.arrays), list(a.out_shape) + list(b.out_shape),
                 list(a.sems) + list(b.sems), make)


def _exchange(plan, name):
    n_in, n_out = len(plan.arrays), len(plan.out_shape)

    def body(*refs):
        for phase in _phases(plan.make(refs[:n_in], refs[n_in:n_in + n_out], refs[n_in + n_out:])):
            phase()

    return pl.pallas_call(
        body, name=name, in_specs=[HBM_SPEC] * n_in, out_specs=[HBM_SPEC] * n_out, out_shape=list(plan.out_shape),
        scratch_shapes=list(plan.sems), compiler_params=pltpu.CompilerParams(has_side_effects=True),
    )(*plan.arrays)


def _pack_small(parts):
    rows = []
    for r in range(SMALL_ROWS):
        pieces, col = [], 0
        for name, row, start, size in SMALL_PARTS:
            if row == r:
                assert start == col
                pieces.append(parts[name].reshape(1, size).astype(F32))
                col += size
        rows.append(jnp.concatenate(pieces + [jnp.zeros((1, D - col), F32)], axis=1))
    return jnp.concatenate(rows, axis=0)


def _columns(gathered):
    return jnp.concatenate([gathered[j] for j in range(N_CHIPS)], axis=1)


def kernel(x, meta_tokens, norm_mix_g, w_in, conv_w, conv_b, conv_ln_g, conv_ln_b, gla_w_gate2, gla_gate_b, gla_norm_g, w_out, norm_ffn_g, w_ffn_gate, w_ffn_up, w_ffn_down, norm_final_g, loss_target, m_meta_tokens, m_norm_mix_g, m_w_in, m_conv_w, m_conv_b, m_conv_ln_g, m_conv_ln_b, m_gla_w_gate2, m_gla_gate_b, m_gla_norm_g, m_w_out, m_norm_ffn_g, m_w_ffn_gate, m_w_ffn_up, m_w_ffn_down, m_norm_final_g, v_meta_tokens, v_norm_mix_g, v_w_in, v_conv_w, v_conv_b, v_conv_ln_g, v_conv_ln_b, v_gla_w_gate2, v_gla_gate_b, v_gla_norm_g, v_w_out, v_norm_ffn_g, v_w_ffn_gate, v_w_ffn_up, v_w_ffn_down, v_norm_final_g):
    ws = dict(zip(WEIGHT_NAMES, (meta_tokens, norm_mix_g, w_in, conv_w, conv_b, conv_ln_g, conv_ln_b, gla_w_gate2,
                                 gla_gate_b, gla_norm_g, w_out, norm_ffn_g, w_ffn_gate, w_ffn_up, w_ffn_down,
                                 norm_final_g)))
    ms = dict(zip(WEIGHT_NAMES, (m_meta_tokens, m_norm_mix_g, m_w_in, m_conv_w, m_conv_b, m_conv_ln_g, m_conv_ln_b,
                                 m_gla_w_gate2, m_gla_gate_b, m_gla_norm_g, m_w_out, m_norm_ffn_g, m_w_ffn_gate,
                                 m_w_ffn_up, m_w_ffn_down, m_norm_final_g)))
    vs = dict(zip(WEIGHT_NAMES, (v_meta_tokens, v_norm_mix_g, v_w_in, v_conv_w, v_conv_b, v_conv_ln_g, v_conv_ln_b,
                                 v_gla_w_gate2, v_gla_gate_b, v_gla_norm_g, v_w_out, v_norm_ffn_g, v_w_ffn_gate,
                                 v_w_ffn_up, v_w_ffn_down, v_norm_final_g)))
    c = lax.axis_index("c")
    mine = 2 * lax.axis_index("x") + lax.axis_index("y")
    shard = lambda d, name: d[name].reshape(d[name].shape[-2:])
    vec = {name: ws[name].reshape(1, -1) for name, _, _, _ in SMALL_PARTS}
    n_ex, seq, _ = x.shape
    lp = HEAD_ROWS + seq
    t = n_ex * lp

    (tgt, h0, gate_s, up_s, down_s, out_s), (w_in_g, meta_g, conv_w_g, w2_g) = _pad_head_rows(
        [loss_target, x],
        [shard(ws, "w_ffn_gate").T, shard(ws, "w_ffn_up").T, shard(ws, "w_ffn_down")], [shard(ws, "w_out")],
        plan=_gather_plan([shard(ws, "w_in").T.astype(BF16)],
                          [shard(ws, "meta_tokens"), shard(ws, "conv_w"), shard(ws, "gla_w_gate2")], axes=[1]))
    conv_w_full = jnp.concatenate([_columns(conv_w_g), jnp.zeros((32 - CONV_W, C_CONV), F32)], axis=0)
    w2_full = jnp.concatenate([_columns(w2_g), jnp.zeros((128 - RANK, GLA_K), F32)], axis=0).astype(BF16)
    h0 = _set_meta_rows(h0, _columns(meta_g)).reshape(t, D)
    tgt = tgt.reshape(t, D)

    (u, hn, w_in_t), (gate_g,) = _in_proj(h0, vec["norm_mix_g"], w_in_g, plan=_gather_plan([gate_s]))
    (yc, y_conv), (up_g, w_out_g) = _conv_fwd(
        u, conv_w_full, vec["conv_b"], vec["conv_ln_g"], vec["conv_ln_b"], n_ex, lp,
        plan=_gather_plan([up_s, out_s]))
    (y_gla, states), _ = _gla_fwd(u, w2_full, vec["gla_gate_b"], vec["gla_norm_g"], n_ex, lp)
    w_gate_t, w_up_t = gate_g.reshape(D_FF, D), up_g.reshape(D_FF, D)
    (h1, hn2, gate, up, act), (down_g,) = _mix_out_ffn_up(
        h0, y_conv, y_gla, w_out_g, vec["norm_ffn_g"], w_gate_t, w_up_t,
        plan=_gather_plan([down_s]))
    w_down_full = down_g.reshape(D_FF, D)
    dh2, loss, d_final_g = _ffn_down_loss(act, w_down_full, h1, tgt, vec["norm_final_g"], n_ex, lp)
    dgate, dup, dh1, dycat, d_ffn_g = _ffn_bwd(dh2, gate, up, h1, w_down_full.T, w_gate_t, w_up_t, w_out_g,
                                                vec["norm_ffn_g"])

    ffn_block = lambda g: g.reshape(N_CHIPS, D_FF // N_CHIPS, D)
    g_gate = ffn_block(_wgrad(dgate, hn2, "wgrad_gate")[0])
    g_up, (gate_sib,) = _wgrad(dup, hn2, "wgrad_up", _to_sibling_plan([g_gate]))
    g_up = ffn_block(g_up)
    g_down, (up_sib,) = _wgrad(act, dh2, "wgrad_down", _to_sibling_plan([g_up]))
    g_down = ffn_block(g_down)
    g_out = _wgrad_pair(y_conv, y_gla, dh1, "wgrad_out").reshape(N_CHIPS, D // N_CHIPS, D)
    cs_gate, cs_up = _rs_add_halves([(g_gate, gate_sib), (g_up, up_sib)], c, "rs_add_gate_up")
    (du_conv, d_conv_w, d_conv_b, d_ln_g, d_ln_b), (ex_gate, ex_up, down_sib, out_sib) = _conv_bwd(
        dycat, yc, u, conv_w_full, vec["conv_ln_g"], vec["conv_ln_b"], n_ex, lp,
        plan=_merge_plans(_chip_exchange_plan([cs_gate, cs_up]), _to_sibling_plan([g_down, g_out])))
    cs_down, cs_out = _rs_add_halves([(g_down, down_sib), (g_out, out_sib)], c, "rs_add_down_out")
    (du_gla, d_w2, d_gate_b, d_norm_g), (ex_down, ex_out) = _gla_bwd(
        dycat, u, states, w2_full, vec["gla_gate_b"], vec["gla_norm_g"], n_ex, lp,
        plan=_chip_exchange_plan([cs_down, cs_out]))
    halves = _rs_sum([(cs_gate, ex_gate), (cs_up, ex_up), (cs_down, ex_down), (cs_out, ex_out)], mine,
                     "rs_sum_early")

    small = {"norm_mix_g": jnp.zeros((1, D), F32), "norm_ffn_g": d_ffn_g, "norm_final_g": d_final_g,
             "conv_b": d_conv_b, "conv_ln_g": d_ln_g, "conv_ln_b": d_ln_b, "gla_gate_b": d_gate_b,
             "gla_norm_g": d_norm_g}
    part = lax.dynamic_update_slice(_pack_small(small), loss[:, :1], (LOSS_ROW, 0))
    part = jnp.concatenate([part, jnp.zeros((N_META, D), F32), d_conv_w.reshape(16, D), d_w2[:RANK].reshape(4, D),
                            jnp.zeros((4, D), F32)], axis=0)
    g_in_conv = _wgrad(du_conv, hn, "wgrad_in_conv")[0][None]
    g_in_gla, (slots, conv_sib) = _wgrad(du_gla, hn, "wgrad_in_gla",
                                         _merge_plans(_all_to_all_plan(part), _to_sibling_plan([g_in_conv])))
    pieces = [g_in_conv, g_in_gla[None]]
    (gla_sib,) = _exchange(_to_sibling_plan(pieces[1:]), "rs_late_to_sibling")
    sums = _rs_add_halves(list(zip(pieces, (conv_sib, gla_sib))), c, "rs_add_w_in")
    in_chip_sum = jnp.concatenate([sums[0][0], sums[1][0]], axis=0)[:D_IN].reshape(N_CHIPS, D_IN // N_CHIPS, D // 2)
    (dh0, d_mix_g), shared = _in_proj_bwd(
        du_conv, du_gla, w_in_t, h0, dh1, vec["norm_mix_g"],
        plan=_merge_plans(_share_plan(halves), _chip_exchange_plan([in_chip_sum])))
    dh0 = dh0.reshape(n_ex, lp, D)
    grad_x = dh0[:, HEAD_ROWS:]
    late_part = jnp.concatenate([d_mix_g, jnp.zeros((SMALL_ROWS - 1, D), F32),
                                 jnp.sum(dh0[:, PAD_ROWS:HEAD_ROWS], axis=0)], axis=0)
    (in_half,) = _rs_sum([(in_chip_sum, shared[4])], mine, "rs_sum_w_in")
    in_shared, late_slots = _exchange(_merge_plans(_share_plan([in_half]), _all_to_all_plan(late_part)),
                                      "late_exchange")

    out = {"grad": {}, "delta": {}, "new_m": {}, "new_v": {}}

    def record(name, res, transposed=False):
        for kind, a in zip(("grad", "delta", "new_m", "new_v"), res):
            out[kind][name] = (a.T if transposed else a).reshape(ws[name].shape)

    def operands(name, transposed):
        lay = (lambda a: a.T) if transposed else (lambda a: a)
        return lay(shard(ws, name)), lay(shard(ms, name)), lay(shard(vs, name))

    early_layout = (("w_ffn_gate", True), ("w_ffn_up", True), ("w_ffn_down", False), ("w_out", False))
    items = [(mine_half, their_half, *operands(name, transposed))
             for (name, transposed), mine_half, their_half in zip(early_layout, halves, shared)]
    for (name, transposed), res in zip(early_layout, _adamw_halves(items, c, "adamw_early")):
        record(name, res, transposed)

    tile_rows = lambda a: a.reshape(a.shape[0], 1, a.shape[1])
    by_output = lambda d: jnp.transpose(d["w_in"], (2, 0, 1))
    res = _adamw_halves([(tile_rows(in_half), tile_rows(in_shared), by_output(ws), by_output(ms), by_output(vs))],
                        c, "adamw_w_in")[0]
    for kind, a in zip(("grad", "delta", "new_m", "new_v"), res):
        out[kind]["w_in"] = jnp.transpose(a, (1, 2, 0))

    flat = lambda d, name: d[name].reshape(1, -1)
    g_s, updated = _sum_slots_adamw(slots, late_slots,
                                    [(flat(ws, name), flat(ms, name), flat(vs, name)) for name, _, _, _ in SMALL_PARTS])
    for (name, _, _, _), res in zip(SMALL_PARTS, updated):
        record(name, res)
    loss = g_s[LOSS_ROW, 0]
    block = lambda a, width: lax.dynamic_slice_in_dim(a, mine * width, width, axis=1)
    small_sharded = {"meta_tokens": block(g_s[8:24], D // N_CHIPS),
                     "conv_w": block(g_s[24:40].reshape(32, C_CONV), C_CONV // N_CHIPS)[:CONV_W],
                     "gla_w_gate2": block(g_s[40:44].reshape(RANK, GLA_K), GLA_K // N_CHIPS)}
    for name, g in small_sharded.items():
        record(name, [g, *_adamw(g, *operands(name, False), "adamw_" + name)])

    return (loss, grad_x, *[out[kind][name] for kind in ("grad", "delta", "new_m", "new_v") for name in WEIGHT_NAMES])
```
